```python
import math
import jax, jax.numpy as jnp
from jax import lax
import numpy as np

D_MODEL = 1024
BATCH = 8
SEQ = 8192
DEPTH = 1

CHUNK = 64
D_PLE = 256
D_FF = 2816
RET_HEADS = 8
RET_DK = 128
RET_DV = 256
RET_QK = RET_HEADS * RET_DK
RET_V = RET_HEADS * RET_DV
MLA_HEADS = 8
MLA_NOPE = 128
MLA_ROPE = 64
MLA_DV = 128
Q_LORA = 256
KV_LORA = 256
Q_BLOCK = 128
ROPE_BASE = 10000.0
EPS = 1e-5
N_LN = 4
DEEPNORM_ALPHA = (2.0 * DEPTH) ** 0.25
DEEPNORM_BETA = (8.0 * DEPTH) ** -0.25
SPLITS = (RET_QK, RET_QK, RET_V, RET_V, Q_LORA, KV_LORA, MLA_ROPE, D_MODEL, D_MODEL)
D_IN_TOTAL = sum(SPLITS)

kernel_name = "hybrid_retention_mla_macaron_deepnorm"


def layer_norm(x, g, b):
    xf = x.astype(jnp.float32)
    mu = jnp.mean(xf, axis=-1, keepdims=True)
    var = jnp.mean(jnp.square(xf - mu), axis=-1, keepdims=True)
    y = (xf - mu) * lax.rsqrt(var + EPS)
    return (y * g.astype(jnp.float32) + b.astype(jnp.float32)).astype(x.dtype)


def rms_norm(x, g):
    xf = x.astype(jnp.float32)
    y = xf * lax.rsqrt(jnp.mean(jnp.square(xf), axis=-1, keepdims=True) + EPS)
    return (y * g.astype(jnp.float32)).astype(x.dtype)


def rope(t, positions):
    half = t.shape[-1] // 2
    inv_freq = ROPE_BASE ** (-jnp.arange(half, dtype=jnp.float32) / half)
    ang = positions.astype(jnp.float32)[:, :, None] * inv_freq
    cos = jnp.cos(ang)[:, :, None, :]
    sin = jnp.sin(ang)[:, :, None, :]
    t1 = t[..., :half].astype(jnp.float32)
    t2 = t[..., half:].astype(jnp.float32)
    return jnp.concatenate([t1 * cos - t2 * sin, t2 * cos + t1 * sin], axis=-1).astype(t.dtype)


def swiglu_ffn(x, w_in, w_out):
    g, u = jnp.split(x @ w_in, 2, axis=-1)
    return (jax.nn.silu(g) * u) @ w_out


def chunk_retention(q, k, v):
    B, S, H, dk = q.shape
    dv = v.shape[-1]
    n_chunks = S // CHUNK
    log_gamma = jnp.log(1.0 - 2.0 ** (-5.0 - jnp.arange(H, dtype=jnp.float32)))
    idx = jnp.arange(CHUNK, dtype=jnp.float32)
    intra_decay = jnp.exp(log_gamma[:, None, None] * jnp.abs(idx[:, None] - idx[None, :]))
    xi = jnp.exp(log_gamma[:, None] * (idx + 1.0))
    zeta = jnp.exp(log_gamma[:, None] * (CHUNK - 1.0 - idx))
    chunk_decay = jnp.exp(log_gamma * CHUNK)

    qc = q.reshape(B, n_chunks, CHUNK, H, dk)
    kc = k.reshape(B, n_chunks, CHUNK, H, dk)
    vc = v.reshape(B, n_chunks, CHUNK, H, dv)

    scores = jnp.einsum('bnchd,bnshd->bnhcs', qc, kc) * intra_decay
    y_intra = jnp.einsum('bnhcs,bnshe->bnche', scores, vc)

    xi_ch = xi.T[None, :, :, None]

    def step(state, inp):
        q_n, k_n, v_n = inp
        cross = jnp.einsum('bchd,bhde->bche', q_n, state) * xi_ch
        state = state * chunk_decay[None, :, None, None] + jnp.einsum('bchd,bche,hc->bhde', k_n, v_n, zeta)
        return state, cross

    s0 = jnp.zeros((B, H, dk, dv), jnp.float32)
    xs = (jnp.moveaxis(qc, 1, 0), jnp.moveaxis(kc, 1, 0), jnp.moveaxis(vc, 1, 0))
    _, y_cross = lax.scan(step, s0, xs)
    y = y_intra + jnp.moveaxis(y_cross, 0, 1)
    return y.reshape(B, S, H, dv)


def mla_block_attention(q_nope, q_pe, k_nope, k_pe, v):
    B, S, H, _ = q_nope.shape
    n_blocks = S // Q_BLOCK
    scale = (MLA_NOPE + MLA_ROPE) ** -0.5
    key_chunk = jnp.arange(S) // CHUNK

    def to_blocks(t):
        return jnp.moveaxis(t.reshape(B, n_blocks, Q_BLOCK, *t.shape[2:]), 1, 0)

    def one_block(args):
        blk, qn, qp = args
        s = (jnp.einsum('bqhd,bkhd->bhqk', qn, k_nope)
             + jnp.einsum('bqhr,bkr->bhqk', qp, k_pe)).astype(jnp.float32) * scale
        q_chunk = (blk * Q_BLOCK + jnp.arange(Q_BLOCK)) // CHUNK
        mask = key_chunk[None, :] <= q_chunk[:, None]
        s = jnp.where(mask[None, None], s, -jnp.inf)
        probs = jax.nn.softmax(s, axis=-1).astype(v.dtype)
        return jnp.einsum('bhqk,bkhe->bqhe', probs, v)

    o = lax.map(one_block, (jnp.arange(n_blocks), to_blocks(q_nope), to_blocks(q_pe)))
    return jnp.moveaxis(o, 0, 1).reshape(B, S, H * v.shape[-1])


def hybrid_mixer(h, positions, w_in, ret_gn_g, w_ret_o, q_norm_g, kv_norm_g,
                 w_uq, w_ukv, w_mla_o, w_out):
    B, S, _ = h.shape
    proj = h @ w_in
    (r_q, r_k, r_v, r_g, c_q, c_kv, k_pe_raw, gate_ret, gate_mla) = jnp.split(
        proj, np.cumsum(SPLITS)[:-1], axis=-1)

    rq = rope(r_q.reshape(B, S, RET_HEADS, RET_DK), positions)
    rk = rope(r_k.reshape(B, S, RET_HEADS, RET_DK), positions) * (RET_DK ** -0.5)
    rv = r_v.reshape(B, S, RET_HEADS, RET_DV)
    y = chunk_retention(rq.astype(jnp.float32), rk.astype(jnp.float32), rv.astype(jnp.float32))
    mu = jnp.mean(y, axis=-1, keepdims=True)
    var = jnp.mean(jnp.square(y - mu), axis=-1, keepdims=True)
    y = ((y - mu) * lax.rsqrt(var + EPS)).reshape(B, S, RET_V) * ret_gn_g.astype(jnp.float32)
    y = (jax.nn.silu(r_g.astype(jnp.float32)) * y).astype(h.dtype)
    y_ret = y @ w_ret_o

    q = (rms_norm(c_q, q_norm_g) @ w_uq).reshape(B, S, MLA_HEADS, MLA_NOPE + MLA_ROPE)
    q_nope, q_pe = q[..., :MLA_NOPE], rope(q[..., MLA_NOPE:], positions)
    kv = (rms_norm(c_kv, kv_norm_g) @ w_ukv).reshape(B, S, MLA_HEADS, MLA_NOPE + MLA_DV)
    k_nope, v = kv[..., :MLA_NOPE], kv[..., MLA_NOPE:]
    k_pe = rope(k_pe_raw[:, :, None, :], positions)[:, :, 0, :]
    y_mla = mla_block_attention(q_nope, q_pe, k_nope, k_pe, v) @ w_mla_o

    mix = jax.nn.sigmoid(gate_ret) * y_ret + jax.nn.sigmoid(gate_mla) * y_mla
    return mix @ w_out


def _fwd_setup_inputs(seed: int = 0) -> dict:
    key = jax.random.key(seed)
    ks = jax.random.split(key, 24)
    f32 = jnp.float32

    def nrm(k, shape, scale):
        return jax.random.normal(k, shape, f32) * scale

    offset = jax.random.randint(ks[2], (BATCH, 1), 0, 4096, dtype=jnp.int32)
    positions = offset + jnp.arange(SEQ, dtype=jnp.int32)[None, :]
    return {
        "x": nrm(ks[0], (BATCH, SEQ, D_MODEL), 1.0),
        "p": nrm(ks[1], (DEPTH, BATCH, SEQ, D_PLE), 1.0),
        "positions": positions,
        "ln_g": 1.0 + nrm(ks[3], (DEPTH, N_LN, D_MODEL), 0.02),
        "ln_b": nrm(ks[4], (DEPTH, N_LN, D_MODEL), 0.02),
        "ffn1_w_in": nrm(ks[5], (DEPTH, D_MODEL, 2 * D_FF), D_MODEL ** -0.5),
        "ffn1_w_out": nrm(ks[6], (DEPTH, D_FF, D_MODEL), DEEPNORM_BETA * D_FF ** -0.5),
        "w_in": nrm(ks[7], (DEPTH, D_MODEL, D_IN_TOTAL), D_MODEL ** -0.5),
        "ret_gn_g": 1.0 + nrm(ks[8], (DEPTH, RET_V), 0.02),
        "w_ret_o": nrm(ks[9], (DEPTH, RET_V, D_MODEL), DEEPNORM_BETA * RET_V ** -0.5),
        "q_norm_g": 1.0 + nrm(ks[10], (DEPTH, Q_LORA), 0.02),
        "kv_norm_g": 1.0 + nrm(ks[11], (DEPTH, KV_LORA), 0.02),
        "w_uq": nrm(ks[12], (DEPTH, Q_LORA, MLA_HEADS * (MLA_NOPE + MLA_ROPE)), Q_LORA ** -0.5),
        "w_ukv": nrm(ks[13], (DEPTH, KV_LORA, MLA_HEADS * (MLA_NOPE + MLA_DV)), KV_LORA ** -0.5),
        "w_mla_o": nrm(ks[14], (DEPTH, MLA_HEADS * MLA_DV, D_MODEL), DEEPNORM_BETA * (MLA_HEADS * MLA_DV) ** -0.5),
        "w_out": nrm(ks[15], (DEPTH, D_MODEL, D_MODEL), DEEPNORM_BETA * D_MODEL ** -0.5),
        "ffn2_w_in": nrm(ks[16], (DEPTH, D_MODEL, 2 * D_FF), D_MODEL ** -0.5),
        "ffn2_w_out": nrm(ks[17], (DEPTH, D_FF, D_MODEL), DEEPNORM_BETA * D_FF ** -0.5),
        "ple_w_gate": nrm(ks[18], (DEPTH, D_MODEL, D_MODEL), D_MODEL ** -0.5),
        "ple_w_proj": nrm(ks[19], (DEPTH, D_PLE, D_MODEL), DEEPNORM_BETA * D_PLE ** -0.5),
    }


def _fwd_reference(x, p, positions, ln_g, ln_b, ffn1_w_in, ffn1_w_out, w_in, ret_gn_g,
              w_ret_o, q_norm_g, kv_norm_g, w_uq, w_ukv, w_mla_o, w_out,
              ffn2_w_in, ffn2_w_out, ple_w_gate, ple_w_proj):
    h = x
    for i in range(DEPTH):
        h = layer_norm(DEEPNORM_ALPHA * h + 0.5 * swiglu_ffn(h, ffn1_w_in[i], ffn1_w_out[i]),
                       ln_g[i, 0], ln_b[i, 0])
        mixed = hybrid_mixer(h, positions, w_in[i], ret_gn_g[i], w_ret_o[i], q_norm_g[i],
                             kv_norm_g[i], w_uq[i], w_ukv[i], w_mla_o[i], w_out[i])
        h = layer_norm(DEEPNORM_ALPHA * h + mixed, ln_g[i, 1], ln_b[i, 1])
        h = layer_norm(DEEPNORM_ALPHA * h + 0.5 * swiglu_ffn(h, ffn2_w_in[i], ffn2_w_out[i]),
                       ln_g[i, 2], ln_b[i, 2])
        ple = jax.nn.sigmoid(h @ ple_w_gate[i]) * (p[i] @ ple_w_proj[i])
        h = layer_norm(DEEPNORM_ALPHA * h + ple, ln_g[i, 3], ln_b[i, 3])
    return h


import jax as _jax
import jax.numpy as _jnp

TWIN_FORMAT = 'train_step'
FWD_PARAMS = ['x', 'p', 'positions', 'ln_g', 'ln_b', 'ffn1_w_in', 'ffn1_w_out', 'w_in', 'ret_gn_g', 'w_ret_o', 'q_norm_g', 'kv_norm_g', 'w_uq', 'w_ukv', 'w_mla_o', 'w_out', 'ffn2_w_in', 'ffn2_w_out', 'ple_w_gate', 'ple_w_proj']
TWIN_WEIGHTS = ['ln_g', 'ln_b', 'ffn1_w_in', 'ffn1_w_out', 'w_in', 'ret_gn_g', 'w_ret_o', 'q_norm_g', 'kv_norm_g', 'w_uq', 'w_ukv', 'w_mla_o', 'w_out', 'ffn2_w_in', 'ffn2_w_out', 'ple_w_gate', 'ple_w_proj']
TWIN_DIFF_INPUT = 'x'
TWIN_INPUTS = ['x', 'p', 'positions', 'ln_g', 'ln_b', 'ffn1_w_in', 'ffn1_w_out', 'w_in', 'ret_gn_g', 'w_ret_o', 'q_norm_g', 'kv_norm_g', 'w_uq', 'w_ukv', 'w_mla_o', 'w_out', 'ffn2_w_in', 'ffn2_w_out', 'ple_w_gate', 'ple_w_proj', 'loss_target', 'm_ln_g', 'm_ln_b', 'm_ffn1_w_in', 'm_ffn1_w_out', 'm_w_in', 'm_ret_gn_g', 'm_w_ret_o', 'm_q_norm_g', 'm_kv_norm_g', 'm_w_uq', 'm_w_ukv', 'm_w_mla_o', 'm_w_out', 'm_ffn2_w_in', 'm_ffn2_w_out', 'm_ple_w_gate', 'm_ple_w_proj', 'v_ln_g', 'v_ln_b', 'v_ffn1_w_in', 'v_ffn1_w_out', 'v_w_in', 'v_ret_gn_g', 'v_w_ret_o', 'v_q_norm_g', 'v_kv_norm_g', 'v_w_uq', 'v_w_ukv', 'v_w_mla_o', 'v_w_out', 'v_ffn2_w_in', 'v_ffn2_w_out', 'v_ple_w_gate', 'v_ple_w_proj']
TWIN_OUTPUTS = ['loss', 'grad_x', 'grad_ln_g', 'grad_ln_b', 'grad_ffn1_w_in', 'grad_ffn1_w_out', 'grad_w_in', 'grad_ret_gn_g', 'grad_w_ret_o', 'grad_q_norm_g', 'grad_kv_norm_g', 'grad_w_uq', 'grad_w_ukv', 'grad_w_mla_o', 'grad_w_out', 'grad_ffn2_w_in', 'grad_ffn2_w_out', 'grad_ple_w_gate', 'grad_ple_w_proj', 'delta_ln_g', 'delta_ln_b', 'delta_ffn1_w_in', 'delta_ffn1_w_out', 'delta_w_in', 'delta_ret_gn_g', 'delta_w_ret_o', 'delta_q_norm_g', 'delta_kv_norm_g', 'delta_w_uq', 'delta_w_ukv', 'delta_w_mla_o', 'delta_w_out', 'delta_ffn2_w_in', 'delta_ffn2_w_out', 'delta_ple_w_gate', 'delta_ple_w_proj', 'new_m_ln_g', 'new_m_ln_b', 'new_m_ffn1_w_in', 'new_m_ffn1_w_out', 'new_m_w_in', 'new_m_ret_gn_g', 'new_m_w_ret_o', 'new_m_q_norm_g', 'new_m_kv_norm_g', 'new_m_w_uq', 'new_m_w_ukv', 'new_m_w_mla_o', 'new_m_w_out', 'new_m_ffn2_w_in', 'new_m_ffn2_w_out', 'new_m_ple_w_gate', 'new_m_ple_w_proj', 'new_v_ln_g', 'new_v_ln_b', 'new_v_ffn1_w_in', 'new_v_ffn1_w_out', 'new_v_w_in', 'new_v_ret_gn_g', 'new_v_w_ret_o', 'new_v_q_norm_g', 'new_v_kv_norm_g', 'new_v_w_uq', 'new_v_w_ukv', 'new_v_w_mla_o', 'new_v_w_out', 'new_v_ffn2_w_in', 'new_v_ffn2_w_out', 'new_v_ple_w_gate', 'new_v_ple_w_proj']
TWIN_LEAF_KINDS = {'loss': 'loss', 'grad_x': 'grad_x', 'grad_ln_g': 'grad_w', 'grad_ln_b': 'grad_w', 'grad_ffn1_w_in': 'grad_w', 'grad_ffn1_w_out': 'grad_w', 'grad_w_in': 'grad_w', 'grad_ret_gn_g': 'grad_w', 'grad_w_ret_o': 'grad_w', 'grad_q_norm_g': 'grad_w', 'grad_kv_norm_g': 'grad_w', 'grad_w_uq': 'grad_w', 'grad_w_ukv': 'grad_w', 'grad_w_mla_o': 'grad_w', 'grad_w_out': 'grad_w', 'grad_ffn2_w_in': 'grad_w', 'grad_ffn2_w_out': 'grad_w', 'grad_ple_w_gate': 'grad_w', 'grad_ple_w_proj': 'grad_w', 'delta_ln_g': 'delta_w', 'delta_ln_b': 'delta_w', 'delta_ffn1_w_in': 'delta_w', 'delta_ffn1_w_out': 'delta_w', 'delta_w_in': 'delta_w', 'delta_ret_gn_g': 'delta_w', 'delta_w_ret_o': 'delta_w', 'delta_q_norm_g': 'delta_w', 'delta_kv_norm_g': 'delta_w', 'delta_w_uq': 'delta_w', 'delta_w_ukv': 'delta_w', 'delta_w_mla_o': 'delta_w', 'delta_w_out': 'delta_w', 'delta_ffn2_w_in': 'delta_w', 'delta_ffn2_w_out': 'delta_w', 'delta_ple_w_gate': 'delta_w', 'delta_ple_w_proj': 'delta_w', 'new_m_ln_g': 'new_m', 'new_m_ln_b': 'new_m', 'new_m_ffn1_w_in': 'new_m', 'new_m_ffn1_w_out': 'new_m', 'new_m_w_in': 'new_m', 'new_m_ret_gn_g': 'new_m', 'new_m_w_ret_o': 'new_m', 'new_m_q_norm_g': 'new_m', 'new_m_kv_norm_g': 'new_m', 'new_m_w_uq': 'new_m', 'new_m_w_ukv': 'new_m', 'new_m_w_mla_o': 'new_m', 'new_m_w_out': 'new_m', 'new_m_ffn2_w_in': 'new_m', 'new_m_ffn2_w_out': 'new_m', 'new_m_ple_w_gate': 'new_m', 'new_m_ple_w_proj': 'new_m', 'new_v_ln_g': 'new_v', 'new_v_ln_b': 'new_v', 'new_v_ffn1_w_in': 'new_v', 'new_v_ffn1_w_out': 'new_v', 'new_v_w_in': 'new_v', 'new_v_ret_gn_g': 'new_v', 'new_v_w_ret_o': 'new_v', 'new_v_q_norm_g': 'new_v', 'new_v_kv_norm_g': 'new_v', 'new_v_w_uq': 'new_v', 'new_v_w_ukv': 'new_v', 'new_v_w_mla_o': 'new_v', 'new_v_w_out': 'new_v', 'new_v_ffn2_w_in': 'new_v', 'new_v_ffn2_w_out': 'new_v', 'new_v_ple_w_gate': 'new_v', 'new_v_ple_w_proj': 'new_v'}


def _forward(args):
    return _fwd_reference(*[args[k] for k in FWD_PARAMS])


def _output_shape():
    def fwd():
        inp = _fwd_setup_inputs(0)
        return _fwd_reference(*[inp[k] for k in FWD_PARAMS])
    out = _jax.eval_shape(fwd)
    return out.shape, out.dtype

N_MICROBATCH = 1
ADAM_LR = 0.001
ADAM_B1 = 0.9
ADAM_B2 = 0.999
ADAM_EPS = 1e-08
ADAM_WD = 0.01
ADAM_STEP = 10
PER_EXAMPLE_BATCH_AXIS = {'x': 0, 'p': 1, 'positions': 0, 'loss_target': 0}
SHARED_INPUTS = []
_WEIGHT_DTYPES = {'ln_g': _jnp.float32, 'ln_b': _jnp.float32, 'ffn1_w_in': _jnp.float32, 'ffn1_w_out': _jnp.float32, 'w_in': _jnp.float32, 'ret_gn_g': _jnp.float32, 'w_ret_o': _jnp.float32, 'q_norm_g': _jnp.float32, 'kv_norm_g': _jnp.float32, 'w_uq': _jnp.float32, 'w_ukv': _jnp.float32, 'w_mla_o': _jnp.float32, 'w_out': _jnp.float32, 'ffn2_w_in': _jnp.float32, 'ffn2_w_out': _jnp.float32, 'ple_w_gate': _jnp.float32, 'ple_w_proj': _jnp.float32}
MOMENT_SCALE = {'ln_g': 3.212526e+01, 'ln_b': 1.317006e+00, 'ffn1_w_in': 2.252928e-02, 'ffn1_w_out': 6.190374e-02, 'w_in': 1.695441e-02, 'ret_gn_g': 1.802927e-02, 'w_ret_o': 3.951707e-02, 'q_norm_g': 1.135364e-02, 'kv_norm_g': 1.715326e-02, 'w_uq': 4.708899e-03, 'w_ukv': 5.573020e-03, 'w_mla_o': 1.047159e-02, 'w_out': 4.075860e-02, 'ffn2_w_in': 2.200138e-02, 'ffn2_w_out': 6.050591e-02, 'ple_w_gate': 2.554403e-02, 'ple_w_proj': 1.102624e-01}


def _to_microbatches(a, axis):
    t = _jnp.moveaxis(a, axis, 0)
    t = t.reshape((N_MICROBATCH, t.shape[0] // N_MICROBATCH) + t.shape[1:])
    return _jnp.moveaxis(t, 1, axis + 1)


def setup_inputs(seed: int = 0) -> dict:
    inp = _fwd_setup_inputs(seed)
    key = _jax.random.fold_in(_jax.random.key(seed), 7919)
    shape, _ = _output_shape()
    out = dict(inp)
    out["loss_target"] = _jax.random.normal(_jax.random.fold_in(key, 0), shape, _jnp.float32)
    for i, name in enumerate(TWIN_WEIGHTS):
        w = inp[name].astype(_jnp.float32)
        if MOMENT_SCALE is None:
            s = _jnp.sqrt(_jnp.mean(_jnp.square(w)) + 1e-30)
        else:
            s = MOMENT_SCALE[name]
        km, kv = _jax.random.split(_jax.random.fold_in(key, i + 1))
        out[name] = w
        out["m_" + name] = s * _jax.random.normal(km, w.shape, _jnp.float32)
        out["v_" + name] = (s * s) * _jax.random.uniform(kv, w.shape, _jnp.float32, 0.5, 1.5)
    if N_MICROBATCH > 1:
        for name, axis in PER_EXAMPLE_BATCH_AXIS.items():
            out[name] = _to_microbatches(out[name], axis)
    return {'x': out['x'], 'p': out['p'], 'positions': out['positions'], 'ln_g': out['ln_g'], 'ln_b': out['ln_b'], 'ffn1_w_in': out['ffn1_w_in'], 'ffn1_w_out': out['ffn1_w_out'], 'w_in': out['w_in'], 'ret_gn_g': out['ret_gn_g'], 'w_ret_o': out['w_ret_o'], 'q_norm_g': out['q_norm_g'], 'kv_norm_g': out['kv_norm_g'], 'w_uq': out['w_uq'], 'w_ukv': out['w_ukv'], 'w_mla_o': out['w_mla_o'], 'w_out': out['w_out'], 'ffn2_w_in': out['ffn2_w_in'], 'ffn2_w_out': out['ffn2_w_out'], 'ple_w_gate': out['ple_w_gate'], 'ple_w_proj': out['ple_w_proj'], 'loss_target': out['loss_target'], 'm_ln_g': out['m_ln_g'], 'm_ln_b': out['m_ln_b'], 'm_ffn1_w_in': out['m_ffn1_w_in'], 'm_ffn1_w_out': out['m_ffn1_w_out'], 'm_w_in': out['m_w_in'], 'm_ret_gn_g': out['m_ret_gn_g'], 'm_w_ret_o': out['m_w_ret_o'], 'm_q_norm_g': out['m_q_norm_g'], 'm_kv_norm_g': out['m_kv_norm_g'], 'm_w_uq': out['m_w_uq'], 'm_w_ukv': out['m_w_ukv'], 'm_w_mla_o': out['m_w_mla_o'], 'm_w_out': out['m_w_out'], 'm_ffn2_w_in': out['m_ffn2_w_in'], 'm_ffn2_w_out': out['m_ffn2_w_out'], 'm_ple_w_gate': out['m_ple_w_gate'], 'm_ple_w_proj': out['m_ple_w_proj'], 'v_ln_g': out['v_ln_g'], 'v_ln_b': out['v_ln_b'], 'v_ffn1_w_in': out['v_ffn1_w_in'], 'v_ffn1_w_out': out['v_ffn1_w_out'], 'v_w_in': out['v_w_in'], 'v_ret_gn_g': out['v_ret_gn_g'], 'v_w_ret_o': out['v_w_ret_o'], 'v_q_norm_g': out['v_q_norm_g'], 'v_kv_norm_g': out['v_kv_norm_g'], 'v_w_uq': out['v_w_uq'], 'v_w_ukv': out['v_w_ukv'], 'v_w_mla_o': out['v_w_mla_o'], 'v_w_out': out['v_w_out'], 'v_ffn2_w_in': out['v_ffn2_w_in'], 'v_ffn2_w_out': out['v_ffn2_w_out'], 'v_ple_w_gate': out['v_ple_w_gate'], 'v_ple_w_proj': out['v_ple_w_proj']}


def _loss(weights, diff, rest, loss_target):
    with _jax.named_scope("forward"):
        args = {**rest, TWIN_DIFF_INPUT: diff, **{k: w.astype(_WEIGHT_DTYPES[k]) for k, w in weights.items()}}
        y = _forward(args)
    with _jax.named_scope("loss_head"):
        err = _jnp.square(y.astype(_jnp.float32) - loss_target)
        return 0.5 * _jnp.sum(_jnp.mean(err, axis=-1)) if err.ndim else 0.5 * err


def _adamw(w, g, m, v):
    m = ADAM_B1 * m + (1.0 - ADAM_B1) * g
    v = ADAM_B2 * v + (1.0 - ADAM_B2) * _jnp.square(g)
    m_hat = m / (1.0 - ADAM_B1 ** ADAM_STEP)
    v_hat = v / (1.0 - ADAM_B2 ** ADAM_STEP)
    delta = -ADAM_LR * (m_hat / (_jnp.sqrt(v_hat) + ADAM_EPS) + ADAM_WD * w)
    return delta, m, v


def reference(x, p, positions, ln_g, ln_b, ffn1_w_in, ffn1_w_out, w_in, ret_gn_g, w_ret_o, q_norm_g, kv_norm_g, w_uq, w_ukv, w_mla_o, w_out, ffn2_w_in, ffn2_w_out, ple_w_gate, ple_w_proj, loss_target, m_ln_g, m_ln_b, m_ffn1_w_in, m_ffn1_w_out, m_w_in, m_ret_gn_g, m_w_ret_o, m_q_norm_g, m_kv_norm_g, m_w_uq, m_w_ukv, m_w_mla_o, m_w_out, m_ffn2_w_in, m_ffn2_w_out, m_ple_w_gate, m_ple_w_proj, v_ln_g, v_ln_b, v_ffn1_w_in, v_ffn1_w_out, v_w_in, v_ret_gn_g, v_w_ret_o, v_q_norm_g, v_kv_norm_g, v_w_uq, v_w_ukv, v_w_mla_o, v_w_out, v_ffn2_w_in, v_ffn2_w_out, v_ple_w_gate, v_ple_w_proj):
    given = dict(x=x, p=p, positions=positions, ln_g=ln_g, ln_b=ln_b, ffn1_w_in=ffn1_w_in, ffn1_w_out=ffn1_w_out, w_in=w_in, ret_gn_g=ret_gn_g, w_ret_o=w_ret_o, q_norm_g=q_norm_g, kv_norm_g=kv_norm_g, w_uq=w_uq, w_ukv=w_ukv, w_mla_o=w_mla_o, w_out=w_out, ffn2_w_in=ffn2_w_in, ffn2_w_out=ffn2_w_out, ple_w_gate=ple_w_gate, ple_w_proj=ple_w_proj, loss_target=loss_target, m_ln_g=m_ln_g, m_ln_b=m_ln_b, m_ffn1_w_in=m_ffn1_w_in, m_ffn1_w_out=m_ffn1_w_out, m_w_in=m_w_in, m_ret_gn_g=m_ret_gn_g, m_w_ret_o=m_w_ret_o, m_q_norm_g=m_q_norm_g, m_kv_norm_g=m_kv_norm_g, m_w_uq=m_w_uq, m_w_ukv=m_w_ukv, m_w_mla_o=m_w_mla_o, m_w_out=m_w_out, m_ffn2_w_in=m_ffn2_w_in, m_ffn2_w_out=m_ffn2_w_out, m_ple_w_gate=m_ple_w_gate, m_ple_w_proj=m_ple_w_proj, v_ln_g=v_ln_g, v_ln_b=v_ln_b, v_ffn1_w_in=v_ffn1_w_in, v_ffn1_w_out=v_ffn1_w_out, v_w_in=v_w_in, v_ret_gn_g=v_ret_gn_g, v_w_ret_o=v_w_ret_o, v_q_norm_g=v_q_norm_g, v_kv_norm_g=v_kv_norm_g, v_w_uq=v_w_uq, v_w_ukv=v_w_ukv, v_w_mla_o=v_w_mla_o, v_w_out=v_w_out, v_ffn2_w_in=v_ffn2_w_in, v_ffn2_w_out=v_ffn2_w_out, v_ple_w_gate=v_ple_w_gate, v_ple_w_proj=v_ple_w_proj)
    weights = {n: given[n] for n in TWIN_WEIGHTS}
    shared = {n: given[n] for n in SHARED_INPUTS}
    per_example = {n: given[n] for n in ['x', 'p', 'positions']}
    grad_fn = _jax.value_and_grad(_loss, argnums=(0, 1))

    def one_microbatch(ex, loss_target):
        ex = dict(ex)
        diff = ex.pop(TWIN_DIFF_INPUT)
        return grad_fn(weights, diff, {**shared, **ex}, loss_target)

    if N_MICROBATCH == 1:
        loss, (grad_w, grad_x) = one_microbatch(per_example, given["loss_target"])
    else:
        def body(carry, xs):
            loss_sum, grad_sum = carry
            l_k, (gw_k, gx_k) = one_microbatch(xs[0], xs[1])
            with _jax.named_scope("update"):
                return (loss_sum + l_k, _jax.tree.map(_jnp.add, grad_sum, gw_k)), gx_k

        init = (_jnp.zeros((), _jnp.float32), _jax.tree.map(_jnp.zeros_like, weights))
        (loss, grad_w), grad_x = _jax.lax.scan(body, init, (per_example, given["loss_target"]))
    with _jax.named_scope("update"):
        delta_w, new_m, new_v = {}, {}, {}
        for n in TWIN_WEIGHTS:
            delta_w[n], new_m[n], new_v[n] = _adamw(weights[n], grad_w[n], given["m_" + n], given["v_" + n])
    return (loss, grad_x, *[grad_w[n] for n in TWIN_WEIGHTS], *[delta_w[n] for n in TWIN_WEIGHTS],
            *[new_m[n] for n in TWIN_WEIGHTS], *[new_v[n] for n in TWIN_WEIGHTS])
```

```python
import functools
import math

import jax
import jax.numpy as jnp
from jax import lax
from jax.experimental import pallas as pl
from jax.experimental.pallas import tpu as pltpu

F32 = jnp.float32
BF16 = jnp.bfloat16

N_DEV = 8
D = 1024
D_FF = 2816
D_PLE = 256
CHUNK = 64
HEADS = 8
RET_DK = 128
RET_DV = 256
MLA_NOPE = 128
MLA_ROPE = 64
MLA_DV = 128
LORA = 256
ROPE_BASE = 10000.0
EPS = 1e-5
ALPHA = 2.0 ** 0.25
RET_SCALE = RET_DK ** -0.5
MLA_SCALE = (MLA_NOPE + MLA_ROPE) ** -0.5
NEG = -1e30

ADAM_LR = 0.001
ADAM_B1 = 0.9
ADAM_B2 = 0.999
ADAM_EPS = 1e-08
ADAM_WD = 0.01
ADAM_STEP = 10

P_RQ, P_RK, P_RV, P_RG, P_GR, P_GM, P_CQ, P_CKV, P_KPE, P_W = 0, 1024, 2048, 4096, 6144, 7168, 8192, 8448, 8704, 8960
W_IN_COLS = 8768
RET_L = 256
ATT_T = 512

PACK = (("ffn1_w_in", 704), ("ffn1_w_out", 352), ("w_in", 1096), ("w_ret_o", 256), ("w_uq", 48), ("w_ukv", 64),
        ("w_mla_o", 128), ("w_out", 128), ("ffn2_w_in", 704), ("ffn2_w_out", 352), ("ple_w_gate", 128), ("ple_w_proj", 32))


def _pad16(r):
    return -(-r // 16) * 16


PACK_ROWS = sum(_pad16(r) for _, r in PACK)
SMALL_ROWS = 16


def _pcall(body, **kw):
    return pl.pallas_call(body, **kw)


def _pick(dim, prefs):
    for p in prefs:
        if dim % p == 0:
            return p
    return dim


def _sigmoid(x):
    return 1.0 / (1.0 + jnp.exp(-x))


def _silu(x):
    return x * _sigmoid(x)


def _ln(r, g, b):
    mu = jnp.mean(r, axis=-1, keepdims=True)
    var = jnp.mean(jnp.square(r - mu), axis=-1, keepdims=True)
    return (r - mu) * lax.rsqrt(var + EPS) * g + b


def _rms(x, g):
    return x * lax.rsqrt(jnp.mean(jnp.square(x), axis=-1, keepdims=True) + EPS) * g


def _dot(a, b, ca, cb):
    return lax.dot_general(a, b, (((ca,), (cb,)), ((), ())), preferred_element_type=F32)


def _accum(ref, val):
    @pl.when(pl.program_id(0) == 0)
    def _():
        ref[...] = jnp.zeros_like(ref)

    ref[...] += val


def _mm(a, b, *, ta=False, tb=False, add=None, out_dtype=F32, name, tm=None, tn=None, tk=None):
    if ta:
        K, M = a.shape
    else:
        M, K = a.shape
    if tb:
        N, K2 = b.shape
    else:
        K2, N = b.shape
    assert K == K2, (a.shape, b.shape, ta, tb)
    tm = tm or _pick(M, (1024, 512, 768, 256, 128))
    tn = tn or _pick(N, (512, 256, 128))
    tk = tk or (K if K <= 2816 else _pick(K, (1024, 1280, 512, 256)))
    nk = K // tk
    grid = (M // tm, N // tn, nk)
    a_spec = pl.BlockSpec((tk, tm), lambda i, j, k: (k, i)) if ta else pl.BlockSpec((tm, tk), lambda i, j, k: (i, k))
    b_spec = pl.BlockSpec((tn, tk), lambda i, j, k: (j, k)) if tb else pl.BlockSpec((tk, tn), lambda i, j, k: (k, j))
    o_spec = pl.BlockSpec((tm, tn), lambda i, j, k: (i, j))
    ca, cb = (0 if ta else 1), (1 if tb else 0)
    has_add = add is not None

    def body(*refs):
        if has_add:
            a_ref, b_ref, add_ref, o_ref, acc_ref = refs
        else:
            a_ref, b_ref, o_ref, acc_ref = refs
        k = pl.program_id(2)

        @pl.when(k == 0)
        def _():
            acc_ref[...] = jnp.zeros_like(acc_ref)

        acc_ref[...] += _dot(a_ref[...], b_ref[...], ca, cb)

        @pl.when(k == nk - 1)
        def _():
            r = acc_ref[...]
            if has_add:
                r = r + add_ref[...].astype(F32)
            o_ref[...] = r.astype(out_dtype)

    in_specs = [a_spec, b_spec] + ([o_spec] if has_add else [])
    args = (a, b) + ((add,) if has_add else ())
    return _pcall(
        body, grid=grid, in_specs=in_specs, out_specs=o_spec, out_shape=jax.ShapeDtypeStruct((M, N), out_dtype),
        scratch_shapes=[pltpu.VMEM((tm, tn), F32)], name=name,
        compiler_params=pltpu.CompilerParams(dimension_semantics=("parallel", "parallel", "arbitrary")),
    )(*args)


def _rows(body, T, tm, ins, outs, name, accs=()):
    in_specs, args = [], []
    for arr, w, cb in ins:
        if w is None:
            in_specs.append(pl.BlockSpec(arr.shape, lambda i, _n=arr.ndim: (0,) * _n))
        else:
            in_specs.append(pl.BlockSpec((tm, w), lambda i, _cb=cb: (i, _cb)))
        args.append(arr)
    out_specs = [pl.BlockSpec((tm, w), lambda i: (i, 0)) for w, _ in outs]
    out_shape = [jax.ShapeDtypeStruct((T, w), dt) for w, dt in outs]
    for r, w in accs:
        out_specs.append(pl.BlockSpec((r, w), lambda i: (0, 0)))
        out_shape.append(jax.ShapeDtypeStruct((r, w), F32))
    return _pcall(
        body, grid=(T // tm,), in_specs=in_specs, out_specs=out_specs, out_shape=out_shape, name=name,
        compiler_params=pltpu.CompilerParams(dimension_semantics=("arbitrary",)),
    )(*args)


def _swiglu_fwd(a, T, name):
    def body(g_ref, u_ref, o_ref):
        o_ref[...] = (_silu(g_ref[...]) * u_ref[...]).astype(BF16)

    return _rows(body, T, 256, [(a, D_FF, 0), (a, D_FF, 1)], [(D_FF, BF16)], name)[0]


def _swiglu_bwd(a, dhs, T, name):
    def body(g_ref, u_ref, d_ref, o_ref):
        _, vjp = jax.vjp(lambda g, u: _silu(g) * u, g_ref[...], u_ref[...])
        dg, du = vjp(d_ref[...])
        o_ref[:, :D_FF] = dg.astype(BF16)
        o_ref[:, D_FF:] = du.astype(BF16)

    return _rows(body, T, 128, [(a, D_FF, 0), (a, D_FF, 1), (dhs, D_FF, 0)], [(2 * D_FF, BF16)], name)[0]


def _ln_fwd(res, f, c, g, b, T, name):
    def body(r_ref, f_ref, g_ref, b_ref, h_ref, hb_ref):
        h = _ln(ALPHA * r_ref[...] + c * f_ref[...], g_ref[...], b_ref[...])
        h_ref[...] = h
        hb_ref[...] = h.astype(BF16)

    return _rows(body, T, 256, [(res, D, 0), (f, D, 0), (g, None, None), (b, None, None)], [(D, F32), (D, BF16)], name)


def _ln_bwd(res, f, c, g, b, dh, T, name):
    def body(r_ref, f_ref, g_ref, b_ref, d_ref, dr_ref, df_ref, dg_ref, db_ref):
        _, vjp = jax.vjp(lambda r, ff, gg, bb: _ln(ALPHA * r + c * ff, gg, bb), r_ref[...], f_ref[...], g_ref[...], b_ref[...])
        dr, df, dg, db = vjp(d_ref[...])
        dr_ref[...] = dr
        df_ref[...] = df.astype(BF16)
        _accum(dg_ref, dg)
        _accum(db_ref, db)

    return _rows(body, T, 256, [(res, D, 0), (f, D, 0), (g, None, None), (b, None, None), (dh, D, 0)],
                 [(D, F32), (D, BF16)], name, accs=[(1, D), (1, D)])


def _rope128(t, cos, sin_s):
    return t * cos + pltpu.roll(t, 64, 1) * sin_s


def _rope128_t(g, cos, sin_s):
    return g * cos - pltpu.roll(g, 64, 1) * sin_s


def _partner32(t):
    lane = lax.broadcasted_iota(jnp.int32, t.shape, 1)
    return jnp.where((lane & 32) == 0, pltpu.roll(t, 96, 1), pltpu.roll(t, 32, 1))


def _rope64(t, cos, sin_s):
    return t * cos + _partner32(t) * sin_s


def _rope64_t(g, cos, sin_s):
    return g * cos - _partner32(g) * sin_s


def _mixer_prep_fwd(proj, tabs, qg, kvg, T):
    cos128, sin128, cos64, sin64 = tabs

    def body(rq_ref, rk_ref, rv_ref, cq_ref, ckv_ref, kpe_ref, c1_ref, s1_ref, c2_ref, s2_ref, qg_ref, kvg_ref,
             oq_ref, ok_ref, ov_ref, oqn_ref, okvn_ref, okpe_ref):
        c1, s1 = c1_ref[...], s1_ref[...]
        for h in range(HEADS):
            sl = slice(h * RET_DK, (h + 1) * RET_DK)
            oq_ref[:, sl] = _rope128(rq_ref[:, sl], c1, s1).astype(BF16)
            ok_ref[:, sl] = (_rope128(rk_ref[:, sl], c1, s1) * RET_SCALE).astype(BF16)
        ov_ref[...] = rv_ref[...].astype(BF16)
        oqn_ref[...] = _rms(cq_ref[...], qg_ref[...]).astype(BF16)
        okvn_ref[...] = _rms(ckv_ref[...], kvg_ref[...]).astype(BF16)
        okpe_ref[...] = _rope64(kpe_ref[...], c2_ref[...], s2_ref[...]).astype(BF16)

    ins = [(proj, 1024, 0), (proj, 1024, 1), (proj, 2048, 1), (proj, 256, P_CQ // 256), (proj, 256, P_CKV // 256),
           (proj, 128, P_KPE // 128), (cos128, 128, 0), (sin128, 128, 0), (cos64, 128, 0), (sin64, 128, 0),
           (qg, None, None), (kvg, None, None)]
    outs = [(1024, BF16), (1024, BF16), (2048, BF16), (LORA, BF16), (LORA, BF16), (128, BF16)]
    return _rows(body, T, 256, ins, outs, "mixer_prep_fwd")


def _rms_bwd(proj, dqn, dkvn, dkpe_all, tabs, qg, kvg, T):
    _, _, cos64, sin64 = tabs

    def body(cq_ref, ckv_ref, dq_ref, dkv_ref, dk_ref, c2_ref, s2_ref, qg_ref, kvg_ref, ocq_ref, ockv_ref, okpe_ref,
             dqg_ref, dkvg_ref):
        _, vjp = jax.vjp(_rms, cq_ref[...], qg_ref[...])
        dx, dg = vjp(dq_ref[...])
        ocq_ref[...] = dx.astype(BF16)
        _accum(dqg_ref, dg)
        _, vjp = jax.vjp(_rms, ckv_ref[...], kvg_ref[...])
        dx, dg = vjp(dkv_ref[...])
        ockv_ref[...] = dx.astype(BF16)
        _accum(dkvg_ref, dg)
        g = dk_ref[:, 0:128]
        for h in range(1, HEADS):
            g = g + dk_ref[:, h * 128:(h + 1) * 128]
        lane = lax.broadcasted_iota(jnp.int32, g.shape, 1)
        g = jnp.where(lane < MLA_ROPE, g, 0.0)
        okpe_ref[...] = _rope64_t(g, c2_ref[...], s2_ref[...]).astype(BF16)

    ins = [(proj, 256, P_CQ // 256), (proj, 256, P_CKV // 256), (dqn, LORA, 0), (dkvn, LORA, 0), (dkpe_all, 1024, 0),
           (cos64, 128, 0), (sin64, 128, 0), (qg, None, None), (kvg, None, None)]
    return _rows(body, T, 256, ins, [(LORA, BF16), (LORA, BF16), (128, BF16)], "rms_bwd", accs=[(1, LORA), (1, LORA)])


def _gn_gate(y, rg, g):
    mu = jnp.mean(y, axis=-1, keepdims=True)
    var = jnp.mean(jnp.square(y - mu), axis=-1, keepdims=True)
    return _silu(rg) * ((y - mu) * lax.rsqrt(var + EPS) * g)


def _ret_out_fwd(y, proj, gn_g, T):
    def body(y_ref, rg_ref, g_ref, o_ref):
        for h in range(HEADS):
            sl = slice(h * RET_DV, (h + 1) * RET_DV)
            o_ref[:, sl] = _gn_gate(y_ref[:, sl], rg_ref[:, sl], g_ref[:, sl]).astype(BF16)

    return _rows(body, T, 256, [(y, 2048, 0), (proj, 2048, P_RG // 2048), (gn_g, None, None)], [(2048, BF16)], "ret_out_fwd")[0]


def _ret_out_bwd(y, proj, gn_g, dyr, T):
    def body(y_ref, rg_ref, g_ref, d_ref, dy_ref, drg_ref, dg_ref):
        dgs = []
        for h in range(HEADS):
            sl = slice(h * RET_DV, (h + 1) * RET_DV)
            _, vjp = jax.vjp(_gn_gate, y_ref[:, sl], rg_ref[:, sl], g_ref[:, sl])
            dy, drg, dg = vjp(d_ref[:, sl])
            dy_ref[:, sl] = dy.astype(BF16)
            drg_ref[:, sl] = drg.astype(BF16)
            dgs.append(dg)
        _accum(dg_ref, jnp.concatenate(dgs, axis=1))

    return _rows(body, T, 128, [(y, 2048, 0), (proj, 2048, P_RG // 2048), (gn_g, None, None), (dyr, 2048, 0)],
                 [(2048, BF16), (2048, BF16)], "ret_out_bwd", accs=[(1, 2048)])


def _q_assemble_fwd(q, tabs, T):
    _, _, cos64, sin64 = tabs

    def body(q_ref, c_ref, s_ref, on_ref, op_ref):
        on_ref[...] = q_ref[:, :1024].astype(BF16)
        c, s = c_ref[...], s_ref[...]
        lane = lax.broadcasted_iota(jnp.int32, c.shape, 1)
        for j in range(HEADS // 2):
            r = _rope64(q_ref[:, 1024 + 128 * j:1024 + 128 * (j + 1)], c, s)
            op_ref[:, 256 * j:256 * j + 128] = jnp.where(lane < 64, r, 0.0).astype(BF16)
            op_ref[:, 256 * j + 128:256 * j + 256] = jnp.where(lane < 64, pltpu.roll(r, 64, 1), 0.0).astype(BF16)

    return _rows(body, T, 256, [(q, 1536, 0), (cos64, 128, 0), (sin64, 128, 0)], [(1024, BF16), (1024, BF16)], "q_assemble_fwd")


def _q_assemble_bwd(dqn, dqpe, tabs, T):
    _, _, cos64, sin64 = tabs

    def body(dn_ref, dp_ref, c_ref, s_ref, on_ref, op_ref):
        on_ref[...] = dn_ref[...].astype(BF16)
        c, s = c_ref[...], s_ref[...]
        lane = lax.broadcasted_iota(jnp.int32, c.shape, 1)
        for j in range(HEADS // 2):
            g = jnp.where(lane < 64, dp_ref[:, 256 * j:256 * j + 128], pltpu.roll(dp_ref[:, 256 * j + 128:256 * j + 256], 64, 1))
            op_ref[:, 128 * j:128 * (j + 1)] = _rope64_t(g, c, s).astype(BF16)

    return _rows(body, T, 256, [(dqn, 1024, 0), (dqpe, 1024, 0), (cos64, 128, 0), (sin64, 128, 0)],
                 [(1024, BF16), (512, BF16)], "q_assemble_bwd")


def _mix_fn(gr, gm, yr, ym):
    return _sigmoid(gr) * yr + _sigmoid(gm) * ym


def _mix_fwd(proj, y_ret, y_mla, T):
    def body(gr_ref, gm_ref, yr_ref, ym_ref, o_ref):
        o_ref[...] = _mix_fn(gr_ref[...], gm_ref[...], yr_ref[...], ym_ref[...]).astype(BF16)

    return _rows(body, T, 256, [(proj, 1024, P_GR // 1024), (proj, 1024, P_GM // 1024), (y_ret, D, 0), (y_mla, D, 0)],
                 [(D, BF16)], "mix_fwd")[0]


def _mix_bwd(proj, y_ret, y_mla, dmix, T):
    def body(gr_ref, gm_ref, yr_ref, ym_ref, d_ref, dgr_ref, dgm_ref, dyr_ref, dym_ref):
        _, vjp = jax.vjp(_mix_fn, gr_ref[...], gm_ref[...], yr_ref[...], ym_ref[...])
        dgr, dgm, dyr, dym = vjp(d_ref[...])
        dgr_ref[...] = dgr.astype(BF16)
        dgm_ref[...] = dgm.astype(BF16)
        dyr_ref[...] = dyr.astype(BF16)
        dym_ref[...] = dym.astype(BF16)

    return _rows(body, T, 256, [(proj, 1024, P_GR // 1024), (proj, 1024, P_GM // 1024), (y_ret, D, 0), (y_mla, D, 0), (dmix, D, 0)],
                 [(D, BF16)] * 4, "mix_bwd")


def _head(h3, gl, pp, tgt, g, b, T):
    def body(h_ref, gl_ref, pp_ref, t_ref, g_ref, b_ref, dh_ref, dgl_ref, dpp_ref, dg_ref, db_ref, loss_ref):
        def f(h, gg, p, lg, lb):
            return _ln(ALPHA * h + _sigmoid(gg) * p, lg, lb)

        h4, vjp = jax.vjp(f, h_ref[...], gl_ref[...], pp_ref[...], g_ref[...], b_ref[...])
        err = h4 - t_ref[...]
        dh, dgl, dpp, dg, db = vjp(err * (1.0 / D))
        dh_ref[...] = dh
        dgl_ref[...] = dgl.astype(BF16)
        dpp_ref[...] = dpp.astype(BF16)
        _accum(dg_ref, dg)
        _accum(db_ref, db)
        part = 0.5 * jnp.sum(jnp.mean(jnp.square(err), axis=-1, keepdims=True), axis=0, keepdims=True)
        _accum(loss_ref, jnp.broadcast_to(part, loss_ref.shape))

    return _rows(body, T, 256, [(h3, D, 0), (gl, D, 0), (pp, D, 0), (tgt, D, 0), (g, None, None), (b, None, None)],
                 [(D, F32), (D, BF16), (D, BF16)], "head", accs=[(1, D), (1, D), (8, 128)])


def _attn_delta(do, o, T):
    def body(d_ref, o_ref, db_ref, dl_ref):
        for h in range(HEADS):
            sl = slice(h * MLA_DV, (h + 1) * MLA_DV)
            d = d_ref[:, sl]
            dl = jnp.sum(d * o_ref[:, sl].astype(F32), axis=-1, keepdims=True)
            dl_ref[:, sl] = jnp.broadcast_to(dl, d.shape)
        db_ref[...] = d_ref[...].astype(BF16)

    return _rows(body, T, 256, [(do, D, 0), (o, D, 0)], [(D, BF16), (D, F32)], "attn_delta")


def _ret_consts():
    L = RET_L
    lg = jnp.log(1.0 - 2.0 ** (-5.0 - jnp.arange(HEADS, dtype=F32)))[:, None, None]
    idx = jnp.arange(L, dtype=F32)
    ch = jnp.arange(L) // CHUNK
    dist = idx[:, None] - idx[None, :]
    same = (ch[:, None] == ch[None, :])[None]
    earlier = (ch[None, :] < ch[:, None])[None]
    dm = jnp.where(same, jnp.exp(lg * jnp.abs(dist)[None]), jnp.where(earlier, jnp.exp(lg * dist[None]), 0.0))
    xi = jnp.broadcast_to(jnp.exp(lg * (idx + 1.0)[None, :, None]), (HEADS, L, 128))
    zeta = jnp.broadcast_to(jnp.exp(lg * (L - 1.0 - idx)[None, :, None]), (HEADS, L, 128))
    gl = jnp.broadcast_to(jnp.exp(lg * float(L)), (HEADS, 8, 128))
    return dm.astype(F32), xi.astype(F32), zeta.astype(F32), gl.astype(F32)


def _ret_fwd(q, k, v, consts, T):
    dm, xi, zeta, gl = consts
    L = RET_L
    n_sc = T // L

    def body(q_ref, k_ref, v_ref, dm_ref, xi_ref, ze_ref, gl_ref, y_ref, s_ref, st_ref):
        @pl.when(pl.program_id(1) == 0)
        def _():
            st_ref[...] = jnp.zeros_like(st_ref)

        qq, kk, vv = q_ref[...], k_ref[...], v_ref[...]
        st = st_ref[...]
        s_ref[0, 0] = st
        p = (_dot(qq, kk, 1, 1) * dm_ref[0]).astype(BF16)
        cross = _dot(qq, st.astype(BF16), 1, 0)
        xi_c = jnp.concatenate([xi_ref[0], xi_ref[0]], axis=1)
        y_ref[...] = _dot(p, vv, 1, 0) + cross * xi_c
        kz = (kk.astype(F32) * ze_ref[0]).astype(BF16)
        gl2 = jnp.concatenate([gl_ref[0, 0:1, :], gl_ref[0, 0:1, :]], axis=1)
        st_ref[...] = st * gl2 + _dot(kz, vv, 0, 0)

    return _pcall(
        body, grid=(HEADS, n_sc),
        in_specs=[pl.BlockSpec((L, 128), lambda h, n: (n, h)), pl.BlockSpec((L, 128), lambda h, n: (n, h)),
                  pl.BlockSpec((L, 256), lambda h, n: (n, h)), pl.BlockSpec((1, L, L), lambda h, n: (h, 0, 0)),
                  pl.BlockSpec((1, L, 128), lambda h, n: (h, 0, 0)), pl.BlockSpec((1, L, 128), lambda h, n: (h, 0, 0)),
                  pl.BlockSpec((1, 8, 128), lambda h, n: (h, 0, 0))],
        out_specs=[pl.BlockSpec((L, 256), lambda h, n: (n, h)), pl.BlockSpec((1, 1, 128, 256), lambda h, n: (h, n, 0, 0))],
        out_shape=[jax.ShapeDtypeStruct((T, HEADS * RET_DV), F32), jax.ShapeDtypeStruct((HEADS, n_sc, 128, 256), F32)],
        scratch_shapes=[pltpu.VMEM((128, 256), F32)], name="ret_fwd",
        compiler_params=pltpu.CompilerParams(dimension_semantics=("arbitrary", "arbitrary")),
    )(q, k, v, dm, xi, zeta, gl)


def _ret_bwd(q, k, v, states, dy, consts, tabs, T):
    dm, xi, zeta, gl = consts
    cos128, sin128, _, _ = tabs
    L = RET_L
    n_sc = T // L

    def body(q_ref, k_ref, v_ref, dy_ref, s_ref, dm_ref, xi_ref, ze_ref, gl_ref, c_ref, sn_ref, dq_ref, dk_ref, dv_ref, g_ref):
        @pl.when(pl.program_id(1) == 0)
        def _():
            g_ref[...] = jnp.zeros_like(g_ref)

        qq, kk, vv, dyy = q_ref[...], k_ref[...], v_ref[...], dy_ref[...]
        dmm = dm_ref[0]
        gb = g_ref[...].astype(BF16)
        sb = s_ref[0, 0].astype(BF16)
        xi_c = jnp.concatenate([xi_ref[0], xi_ref[0]], axis=1)
        pb = (_dot(qq, kk, 1, 1) * dmm).astype(BF16)
        kz = (kk.astype(F32) * ze_ref[0]).astype(BF16)
        dv_ref[...] = (_dot(pb, dyy, 0, 0) + _dot(kz, gb, 1, 0)).astype(BF16)
        da = (_dot(dyy, vv, 1, 1) * dmm).astype(BF16)
        dyx = (dyy.astype(F32) * xi_c).astype(BF16)
        dq = _dot(da, kk, 1, 0) + _dot(dyx, sb, 1, 1)
        dk = _dot(da, qq, 0, 0) + _dot(vv, gb, 1, 1) * ze_ref[0]
        c, sn = c_ref[...], sn_ref[...]
        dq_ref[...] = _rope128_t(dq, c, sn).astype(BF16)
        dk_ref[...] = (_rope128_t(dk, c, sn) * RET_SCALE).astype(BF16)
        gl2 = jnp.concatenate([gl_ref[0, 0:1, :], gl_ref[0, 0:1, :]], axis=1)
        g_ref[...] = g_ref[...] * gl2 + _dot(qq, dyx, 0, 0)

    rev = lambda n: n_sc - 1 - n
    return _pcall(
        body, grid=(HEADS, n_sc),
        in_specs=[pl.BlockSpec((L, 128), lambda h, n: (rev(n), h)), pl.BlockSpec((L, 128), lambda h, n: (rev(n), h)),
                  pl.BlockSpec((L, 256), lambda h, n: (rev(n), h)), pl.BlockSpec((L, 256), lambda h, n: (rev(n), h)),
                  pl.BlockSpec((1, 1, 128, 256), lambda h, n: (h, rev(n), 0, 0)),
                  pl.BlockSpec((1, L, L), lambda h, n: (h, 0, 0)), pl.BlockSpec((1, L, 128), lambda h, n: (h, 0, 0)),
                  pl.BlockSpec((1, L, 128), lambda h, n: (h, 0, 0)), pl.BlockSpec((1, 8, 128), lambda h, n: (h, 0, 0)),
                  pl.BlockSpec((L, 128), lambda h, n: (rev(n), 0)), pl.BlockSpec((L, 128), lambda h, n: (rev(n), 0))],
        out_specs=[pl.BlockSpec((L, 128), lambda h, n: (rev(n), h)), pl.BlockSpec((L, 128), lambda h, n: (rev(n), h)),
                   pl.BlockSpec((L, 256), lambda h, n: (rev(n), h))],
        out_shape=[jax.ShapeDtypeStruct((T, 1024), BF16), jax.ShapeDtypeStruct((T, 1024), BF16), jax.ShapeDtypeStruct((T, 2048), BF16)],
        scratch_shapes=[pltpu.VMEM((128, 256), F32)], name="ret_bwd",
        compiler_params=pltpu.CompilerParams(dimension_semantics=("arbitrary", "arbitrary")),
    )(q, k, v, dy, states, dm, xi, zeta, gl, cos128, sin128)


def _chunk_mask(i, j, t):
    qpos = i * t + lax.broadcasted_iota(jnp.int32, (t, t), 0)
    kpos = j * t + lax.broadcasted_iota(jnp.int32, (t, t), 1)
    return lax.shift_right_logical(kpos, 6) <= lax.shift_right_logical(qpos, 6)


def _attn_fwd(qn, qpe, kv, kpe, T):
    t = min(ATT_T, T)
    nb = T // t

    def body(qn_ref, qp_ref, kn_ref, kp_ref, v_ref, o_ref, lse_ref, m_sc, l_sc, acc_sc):
        i, j = pl.program_id(1), pl.program_id(2)

        @pl.when(j == 0)
        def _():
            m_sc[...] = jnp.full_like(m_sc, NEG)
            l_sc[...] = jnp.zeros_like(l_sc)
            acc_sc[...] = jnp.zeros_like(acc_sc)

        @pl.when(j <= i)
        def _():
            q = jnp.concatenate([qn_ref[...], qp_ref[...]], axis=1)
            k = jnp.concatenate([kn_ref[...], kp_ref[...]], axis=1)
            s = _dot(q, k, 1, 1) * MLA_SCALE
            s = jnp.where(_chunk_mask(i, j, t), s, NEG)
            m_prev = m_sc[...]
            m_new = jnp.maximum(m_prev, jnp.max(s, axis=1, keepdims=True))
            a = jnp.exp(m_prev - m_new)
            p = jnp.exp(s - m_new[:, 0:1])
            l_sc[...] = a * l_sc[...] + jnp.sum(p, axis=1, keepdims=True)
            acc_sc[...] = a * acc_sc[...] + _dot(p.astype(BF16), v_ref[...], 1, 0)
            m_sc[...] = m_new

        @pl.when(j == nb - 1)
        def _():
            o_ref[...] = (acc_sc[...] / l_sc[...]).astype(BF16)
            lse_ref[...] = m_sc[...] + jnp.log(l_sc[...])

    kj = lambda i, j: jnp.minimum(i, j)
    return _pcall(
        body, grid=(HEADS, nb, nb),
        in_specs=[pl.BlockSpec((t, 128), lambda h, i, j: (i, h)), pl.BlockSpec((t, 128), lambda h, i, j: (i, h)),
                  pl.BlockSpec((t, 128), lambda h, i, j: (kj(i, j), h)), pl.BlockSpec((t, 128), lambda h, i, j: (kj(i, j), 0)),
                  pl.BlockSpec((t, 128), lambda h, i, j: (kj(i, j), HEADS + h))],
        out_specs=[pl.BlockSpec((t, 128), lambda h, i, j: (i, h)), pl.BlockSpec((t, 128), lambda h, i, j: (i, h))],
        out_shape=[jax.ShapeDtypeStruct((T, D), BF16), jax.ShapeDtypeStruct((T, D), F32)],
        scratch_shapes=[pltpu.VMEM((t, 128), F32), pltpu.VMEM((t, 128), F32), pltpu.VMEM((t, 128), F32)], name="attn_fwd",
        compiler_params=pltpu.CompilerParams(dimension_semantics=("parallel", "parallel", "arbitrary")),
    )(qn, qpe, kv, kpe, kv)


def _attn_bwd_dq(qn, qpe, kv, kpe, do, lse, delta, T):
    t = min(ATT_T, T)
    nb = T // t

    def body(qn_ref, qp_ref, kn_ref, kp_ref, v_ref, do_ref, lse_ref, dl_ref, dqn_ref, dqp_ref, acc_sc):
        i, j = pl.program_id(1), pl.program_id(2)

        @pl.when(j == 0)
        def _():
            acc_sc[...] = jnp.zeros_like(acc_sc)

        @pl.when(j <= i)
        def _():
            q = jnp.concatenate([qn_ref[...], qp_ref[...]], axis=1)
            k = jnp.concatenate([kn_ref[...], kp_ref[...]], axis=1)
            s = _dot(q, k, 1, 1) * MLA_SCALE
            s = jnp.where(_chunk_mask(i, j, t), s, NEG)
            p = jnp.exp(s - lse_ref[:, 0:1])
            dp = _dot(do_ref[...], v_ref[...], 1, 1)
            ds = (p * (dp - dl_ref[:, 0:1]) * MLA_SCALE).astype(BF16)
            acc_sc[...] += _dot(ds, k, 1, 0)

        @pl.when(j == nb - 1)
        def _():
            dqn_ref[...] = acc_sc[:, :128]
            dqp_ref[...] = acc_sc[:, 128:]

    kj = lambda i, j: jnp.minimum(i, j)
    qs = pl.BlockSpec((t, 128), lambda h, i, j: (i, h))
    return _pcall(
        body, grid=(HEADS, nb, nb),
        in_specs=[qs, qs, pl.BlockSpec((t, 128), lambda h, i, j: (kj(i, j), h)), pl.BlockSpec((t, 128), lambda h, i, j: (kj(i, j), 0)),
                  pl.BlockSpec((t, 128), lambda h, i, j: (kj(i, j), HEADS + h)), qs, qs, qs],
        out_specs=[qs, qs],
        out_shape=[jax.ShapeDtypeStruct((T, D), F32), jax.ShapeDtypeStruct((T, D), F32)],
        scratch_shapes=[pltpu.VMEM((t, 256), F32)], name="attn_bwd_dq",
        compiler_params=pltpu.CompilerParams(dimension_semantics=("parallel", "parallel", "arbitrary")),
    )(qn, qpe, kv, kpe, kv, do, lse, delta)


def _attn_bwd_dkv(qn, qpe, kv, kpe, do, lse, delta, T):
    t = min(ATT_T, T)
    nb = T // t

    def body(qn_ref, qp_ref, kn_ref, kp_ref, v_ref, do_ref, lse_ref, dl_ref, dkn_ref, dkp_ref, dv_ref, dk_sc, dv_sc):
        j, i = pl.program_id(1), pl.program_id(2)

        @pl.when(i == 0)
        def _():
            dk_sc[...] = jnp.zeros_like(dk_sc)
            dv_sc[...] = jnp.zeros_like(dv_sc)

        @pl.when(i >= j)
        def _():
            q = jnp.concatenate([qn_ref[...], qp_ref[...]], axis=1)
            k = jnp.concatenate([kn_ref[...], kp_ref[...]], axis=1)
            s = _dot(q, k, 1, 1) * MLA_SCALE
            s = jnp.where(_chunk_mask(i, j, t), s, NEG)
            p = jnp.exp(s - lse_ref[:, 0:1])
            dv_sc[...] += _dot(p.astype(BF16), do_ref[...], 0, 0)
            dp = _dot(do_ref[...], v_ref[...], 1, 1)
            ds = (p * (dp - dl_ref[:, 0:1]) * MLA_SCALE).astype(BF16)
            dk_sc[...] += _dot(ds, q, 0, 0)

        @pl.when(i == nb - 1)
        def _():
            dkn_ref[...] = dk_sc[:, :128].astype(BF16)
            dkp_ref[...] = dk_sc[:, 128:]
            dv_ref[...] = dv_sc[...].astype(BF16)

    qi = lambda i, j: jnp.maximum(i, j)
    qs = pl.BlockSpec((t, 128), lambda h, j, i: (qi(i, j), h))
    ks = pl.BlockSpec((t, 128), lambda h, j, i: (j, h))
    return _pcall(
        body, grid=(HEADS, nb, nb),
        in_specs=[qs, qs, ks, pl.BlockSpec((t, 128), lambda h, j, i: (j, 0)), pl.BlockSpec((t, 128), lambda h, j, i: (j, HEADS + h)),
                  qs, qs, qs],
        out_specs=[ks, ks, ks],
        out_shape=[jax.ShapeDtypeStruct((T, D), BF16), jax.ShapeDtypeStruct((T, D), F32), jax.ShapeDtypeStruct((T, D), BF16)],
        scratch_shapes=[pltpu.VMEM((t, 256), F32), pltpu.VMEM((t, 128), F32)], name="attn_bwd_dkv",
        compiler_params=pltpu.CompilerParams(dimension_semantics=("parallel", "parallel", "arbitrary")),
    )(qn, qpe, kv, kpe, kv, do, lse, delta)


def _mesh_pos():
    x, y, c = lax.axis_index("x"), lax.axis_index("y"), lax.axis_index("c")
    return x, y, c, 4 * x + 2 * y + c


def _peer(x, y, c, k):
    px, py, pc = (x + ((k >> 2) & 1)) % 2, (y + ((k >> 1) & 1)) % 2, (c + (k & 1)) % 2
    return (px, py, pc), 4 * px + 2 * py + pc


_ANY = pl.BlockSpec(memory_space=pl.ANY)


def _all_gather(wsh, ssh):
    def body(w_ref, s_ref, wall_ref, sall_ref, send_sems, recv_sems, loc_sems):
        x, y, c, me = _mesh_pos()
        loc = [pltpu.make_async_copy(w_ref, wall_ref.at[me], loc_sems.at[0]),
               pltpu.make_async_copy(s_ref, sall_ref.at[me], loc_sems.at[1])]
        for cp in loc:
            cp.start()
        sends, recvs = [], []
        for k in range(1, N_DEV):
            to, pidx = _peer(x, y, c, k)
            for n, (src, dst) in enumerate(((w_ref, wall_ref), (s_ref, sall_ref))):
                sem = 2 * (k - 1) + n
                sends.append(pltpu.make_async_remote_copy(src_ref=src, dst_ref=dst.at[me], send_sem=send_sems.at[sem],
                                                          recv_sem=recv_sems.at[sem], device_id=to, device_id_type=pl.DeviceIdType.MESH))
                recvs.append(pltpu.make_async_remote_copy(src_ref=src, dst_ref=dst.at[pidx], send_sem=send_sems.at[sem],
                                                          recv_sem=recv_sems.at[sem], device_id=to, device_id_type=pl.DeviceIdType.MESH))
        for cp in sends:
            cp.start()
        for cp in recvs:
            cp.wait_recv()
        for cp in sends:
            cp.wait_send()
        for cp in loc:
            cp.wait()

    return _pcall(
        body, in_specs=[_ANY, _ANY], out_specs=[_ANY, _ANY],
        out_shape=[jax.ShapeDtypeStruct((N_DEV,) + wsh.shape, wsh.dtype), jax.ShapeDtypeStruct((N_DEV,) + ssh.shape, ssh.dtype)],
        scratch_shapes=[pltpu.SemaphoreType.DMA((2 * (N_DEV - 1),)), pltpu.SemaphoreType.DMA((2 * (N_DEV - 1),)), pltpu.SemaphoreType.DMA((2,))],
        name="all_gather_weights",
    )(wsh, ssh)


def _exchange_grads(gfull, gsmall):
    def body(g_ref, s_ref, recv_ref, srecv_ref, send_sems, recv_sems, loc_sems):
        x, y, c, me = _mesh_pos()
        loc = [pltpu.make_async_copy(g_ref.at[me], recv_ref.at[me], loc_sems.at[0]),
               pltpu.make_async_copy(s_ref, srecv_ref.at[me], loc_sems.at[1])]
        for cp in loc:
            cp.start()
        sends, recvs = [], []
        for k in range(1, N_DEV):
            to, pidx = _peer(x, y, c, k)
            for n, (src, dst) in enumerate(((g_ref.at[pidx], recv_ref), (s_ref, srecv_ref))):
                sem = 2 * (k - 1) + n
                sends.append(pltpu.make_async_remote_copy(src_ref=src, dst_ref=dst.at[me], send_sem=send_sems.at[sem],
                                                          recv_sem=recv_sems.at[sem], device_id=to, device_id_type=pl.DeviceIdType.MESH))
                recvs.append(pltpu.make_async_remote_copy(src_ref=src, dst_ref=dst.at[pidx], send_sem=send_sems.at[sem],
                                                          recv_sem=recv_sems.at[sem], device_id=to, device_id_type=pl.DeviceIdType.MESH))
        for cp in sends:
            cp.start()
        for cp in recvs:
            cp.wait_recv()
        for cp in sends:
            cp.wait_send()
        for cp in loc:
            cp.wait()

    return _pcall(
        body, in_specs=[_ANY, _ANY], out_specs=[_ANY, _ANY],
        out_shape=[jax.ShapeDtypeStruct(gfull.shape, gfull.dtype), jax.ShapeDtypeStruct((N_DEV,) + gsmall.shape, gsmall.dtype)],
        scratch_shapes=[pltpu.SemaphoreType.DMA((2 * (N_DEV - 1),)), pltpu.SemaphoreType.DMA((2 * (N_DEV - 1),)), pltpu.SemaphoreType.DMA((2,))],
        name="exchange_grads",
    )(gfull, gsmall)


def _sum_slots(recv, name):
    n, R, C = recv.shape
    tr = _pick(R, (400, 256, 128, 16, 8))

    def body(r_ref, o_ref):
        acc = r_ref[0].astype(F32)
        for s in range(1, n):
            acc = acc + r_ref[s].astype(F32)
        o_ref[...] = acc

    return _pcall(body, grid=(R // tr,), in_specs=[pl.BlockSpec((n, tr, C), lambda i: (0, i, 0))],
                  out_specs=pl.BlockSpec((tr, C), lambda i: (i, 0)), out_shape=jax.ShapeDtypeStruct((R, C), F32), name=name)(recv)


def _adamw(w, g, m, v, name):
    shape = w.shape
    w2, g2, m2, v2 = (a.reshape(-1, shape[-1]) for a in (w, g, m, v))
    R, C = w2.shape
    tr = _pick(R, (256, 128, 64, 32, 16, 8)) if R > 256 else R

    def body(w_ref, g_ref, m_ref, v_ref, d_ref, nm_ref, nv_ref):
        gg = g_ref[...]
        nm = ADAM_B1 * m_ref[...] + (1.0 - ADAM_B1) * gg
        nv = ADAM_B2 * v_ref[...] + (1.0 - ADAM_B2) * jnp.square(gg)
        m_hat = nm / (1.0 - ADAM_B1 ** ADAM_STEP)
        v_hat = nv / (1.0 - ADAM_B2 ** ADAM_STEP)
        d_ref[...] = -ADAM_LR * (m_hat / (jnp.sqrt(v_hat) + ADAM_EPS) + ADAM_WD * w_ref[...])
        nm_ref[...] = nm
        nv_ref[...] = nv

    spec = pl.BlockSpec((tr, C), lambda i: (i, 0))
    d, nm, nv = _pcall(body, grid=(R // tr,), in_specs=[spec] * 4, out_specs=[spec] * 3,
                       out_shape=[jax.ShapeDtypeStruct((R, C), F32)] * 3, name=name)(w2, g2, m2, v2)
    return d.reshape(shape), nm.reshape(shape), nv.reshape(shape)


def _to_rows(name, w):
    w = w[0]
    if name in ("ffn1_w_in", "ffn2_w_in", "w_in"):
        return w.T
    if name in ("w_uq", "w_ukv", "ple_w_proj"):
        return w.T.reshape(-1, 1024)
    return w


def _from_rows(name, g, shape):
    if name in ("ffn1_w_in", "ffn2_w_in", "w_in"):
        return g.T.reshape(shape)
    if name in ("w_uq", "w_ukv", "ple_w_proj"):
        return g.reshape(-1, shape[1]).T.reshape(shape)
    return g.reshape(shape)


def _unpack(wall):
    out, off = {}, 0
    for name, r in PACK:
        out[name] = wall[:, off:off + r, :].reshape(N_DEV * r, 1024)
        off += _pad16(r)
    return out


def _w_in_internal(wt):
    return jnp.concatenate([wt[0:6144], wt[6720:8768], wt[6144:6720], jnp.zeros((P_W - W_IN_COLS, 1024), wt.dtype)], axis=0)


def _w_in_external(d):
    return jnp.concatenate([d[0:6144], d[8192:8768], d[6144:8192]], axis=0)


def _rope_tables(positions):
    pos = positions[0].astype(F32)

    def cs(half):
        inv = ROPE_BASE ** (-jnp.arange(half, dtype=F32) / half)
        ang = pos[:, None] * inv
        return jnp.cos(ang), jnp.sin(ang)

    c, s = cs(64)
    c2, s2 = cs(32)
    return (jnp.concatenate([c, c], axis=1), jnp.concatenate([-s, s], axis=1),
            jnp.concatenate([c2, c2, c2, c2], axis=1), jnp.concatenate([-s2, s2, -s2, s2], axis=1))


def _local_step(x, p, positions, target, W, ln_g, ln_b, gn_g, qg, kvg):
    T = x.shape[0]
    tabs = _rope_tables(positions)
    rc = _ret_consts()
    lg = [ln_g[i:i + 1] for i in range(4)]
    lb = [ln_b[i:i + 1] for i in range(4)]
    xb = x.astype(BF16)
    pb = p.astype(BF16)
    w_in_t = _w_in_internal(W["w_in"])
    wuq = W["w_uq"].reshape(1536, LORA).reshape(HEADS, 192, LORA)
    wuq = jnp.concatenate([wuq[:, :128].reshape(1024, LORA), wuq[:, 128:].reshape(512, LORA)], axis=0)
    wukv = W["w_ukv"].reshape(2048, LORA).reshape(HEADS, 2, 128, LORA).transpose(1, 0, 2, 3).reshape(2048, LORA)
    wp_t = W["ple_w_proj"].reshape(1024, D_PLE)

    a1 = _mm(xb, W["ffn1_w_in"], tb=True, name="ffn1_in")
    hs1 = _swiglu_fwd(a1, T, "ffn1_act")
    f1 = _mm(hs1, W["ffn1_w_out"], name="ffn1_out")
    h1, h1b = _ln_fwd(x, f1, 0.5, lg[0], lb[0], T, "ln0")
    proj = _mm(h1b, w_in_t, tb=True, name="mixer_in")
    rq, rk, rv, qn, kvn, kpe = _mixer_prep_fwd(proj, tabs, qg, kvg, T)
    y, states = _ret_fwd(rq, rk, rv, rc, T)
    yr = _ret_out_fwd(y, proj, gn_g, T)
    y_ret = _mm(yr, W["w_ret_o"], name="ret_o")
    q = _mm(qn, wuq, tb=True, name="mla_uq")
    kv = _mm(kvn, wukv, tb=True, out_dtype=BF16, name="mla_ukv")
    qnope, qpe = _q_assemble_fwd(q, tabs, T)
    o, lse = _attn_fwd(qnope, qpe, kv, kpe, T)
    y_mla = _mm(o, W["w_mla_o"], name="mla_o")
    mix = _mix_fwd(proj, y_ret, y_mla, T)
    mixed = _mm(mix, W["w_out"], name="mixer_out")
    h2, h2b = _ln_fwd(h1, mixed, 1.0, lg[1], lb[1], T, "ln1")
    a2 = _mm(h2b, W["ffn2_w_in"], tb=True, name="ffn2_in")
    hs2 = _swiglu_fwd(a2, T, "ffn2_act")
    f2 = _mm(hs2, W["ffn2_w_out"], name="ffn2_out")
    h3, h3b = _ln_fwd(h2, f2, 0.5, lg[2], lb[2], T, "ln2")
    gl = _mm(h3b, W["ple_w_gate"], name="ple_gate")
    pp = _mm(pb, wp_t, tb=True, name="ple_proj")

    G = {}
    dh3_a, dgl, dpp, dg3, db3, loss = _head(h3, gl, pp, target, lg[3], lb[3], T)
    G["ple_w_gate"] = _mm(h3b, dgl, ta=True, name="d_ple_gate")
    G["ple_w_proj"] = _mm(dpp, pb, ta=True, name="d_ple_proj")
    dh3 = _mm(dgl, W["ple_w_gate"], tb=True, add=dh3_a, name="dh3")
    dh2_a, df2, dg2, db2 = _ln_bwd(h2, f2, 0.5, lg[2], lb[2], dh3, T, "ln2_bwd")
    G["ffn2_w_out"] = _mm(hs2, df2, ta=True, name="d_ffn2_out")
    dhs2 = _mm(df2, W["ffn2_w_out"], tb=True, name="dhs2")
    da2 = _swiglu_bwd(a2, dhs2, T, "ffn2_act_bwd")
    G["ffn2_w_in"] = _mm(da2, h2b, ta=True, name="d_ffn2_in")
    dh2 = _mm(da2, W["ffn2_w_in"], add=dh2_a, name="dh2")
    dh1_a, dmixed, dg1, db1 = _ln_bwd(h1, mixed, 1.0, lg[1], lb[1], dh2, T, "ln1_bwd")
    G["w_out"] = _mm(mix, dmixed, ta=True, name="d_mixer_out")
    dmix = _mm(dmixed, W["w_out"], tb=True, name="dmix")
    dgr, dgm, dy_ret, dy_mla = _mix_bwd(proj, y_ret, y_mla, dmix, T)
    G["w_mla_o"] = _mm(o, dy_mla, ta=True, name="d_mla_o")
    do = _mm(dy_mla, W["w_mla_o"], tb=True, name="do")
    dob, delta = _attn_delta(do, o, T)
    dqn_f, dqpe_f = _attn_bwd_dq(qnope, qpe, kv, kpe, dob, lse, delta, T)
    dkn, dkpe_all, dv = _attn_bwd_dkv(qnope, qpe, kv, kpe, dob, lse, delta, T)
    dq_n, dq_r = _q_assemble_bwd(dqn_f, dqpe_f, tabs, T)
    g_uq = jnp.concatenate([_mm(dq_n, qn, ta=True, name="d_uq_nope"), _mm(dq_r, qn, ta=True, name="d_uq_rope")], axis=0)
    g_uq = jnp.concatenate([g_uq[:1024].reshape(HEADS, 128, LORA), g_uq[1024:].reshape(HEADS, 64, LORA)], axis=1)
    G["w_uq"] = g_uq.reshape(1536 * LORA // 1024, 1024)
    dqn = _mm(dq_r, wuq[1024:], add=_mm(dq_n, wuq[:1024], name="dqn_a"), name="dqn_b")
    g_ukv = jnp.stack([_mm(dkn, kvn, ta=True, name="d_ukv_k"), _mm(dv, kvn, ta=True, name="d_ukv_v")], axis=0)
    G["w_ukv"] = g_ukv.reshape(2, HEADS, 128, LORA).transpose(1, 0, 2, 3).reshape(2048 * LORA // 1024, 1024)
    dkvn = _mm(dv, wukv[1024:], add=_mm(dkn, wukv[:1024], name="dkvn_a"), name="dkvn_b")
    dcq, dckv, dkpe, dqg, dkvg = _rms_bwd(proj, dqn, dkvn, dkpe_all, tabs, qg, kvg, T)
    G["w_ret_o"] = _mm(yr, dy_ret, ta=True, name="d_ret_o")
    dyr = _mm(dy_ret, W["w_ret_o"], tb=True, name="dyr")
    dy, drg, dgn = _ret_out_bwd(y, proj, gn_g, dyr, T)
    drq, drk, drv = _ret_bwd(rq, rk, rv, states, dy, rc, tabs, T)
    dproj = jnp.concatenate([drq, drk, drv, drg, dgr, dgm, dcq, dckv, dkpe, jnp.zeros((T, P_W - P_KPE - 128), BF16)], axis=1)
    G["w_in"] = _w_in_external(_mm(dproj, h1b, ta=True, name="d_mixer_in"))
    dh1 = _mm(dproj, w_in_t, add=dh1_a, name="dh1")
    dx_a, df1, dg0, db0 = _ln_bwd(x, f1, 0.5, lg[0], lb[0], dh1, T, "ln0_bwd")
    G["ffn1_w_out"] = _mm(hs1, df1, ta=True, name="d_ffn1_out")
    dhs1 = _mm(df1, W["ffn1_w_out"], tb=True, name="dhs1")
    da1 = _swiglu_bwd(a1, dhs1, T, "ffn1_act_bwd")
    G["ffn1_w_in"] = _mm(da1, xb, ta=True, name="d_ffn1_in")
    grad_x = _mm(da1, W["ffn1_w_in"], add=dx_a, name="grad_x")

    small = dict(ln_g=jnp.concatenate([dg0, dg1, dg2, dg3], axis=0), ln_b=jnp.concatenate([db0, db1, db2, db3], axis=0),
                 ret_gn_g=dgn, q_norm_g=dqg, kv_norm_g=dkvg)
    return loss, grad_x, G, small


def kernel(x, p, positions, ln_g, ln_b, ffn1_w_in, ffn1_w_out, w_in, ret_gn_g, w_ret_o, q_norm_g, kv_norm_g, w_uq, w_ukv, w_mla_o, w_out, ffn2_w_in, ffn2_w_out, ple_w_gate, ple_w_proj, loss_target, m_ln_g, m_ln_b, m_ffn1_w_in, m_ffn1_w_out, m_w_in, m_ret_gn_g, m_w_ret_o, m_q_norm_g, m_kv_norm_g, m_w_uq, m_w_ukv, m_w_mla_o, m_w_out, m_ffn2_w_in, m_ffn2_w_out, m_ple_w_gate, m_ple_w_proj, v_ln_g, v_ln_b, v_ffn1_w_in, v_ffn1_w_out, v_w_in, v_ret_gn_g, v_w_ret_o, v_q_norm_g, v_kv_norm_g, v_w_uq, v_w_ukv, v_w_mla_o, v_w_out, v_ffn2_w_in, v_ffn2_w_out, v_ple_w_gate, v_ple_w_proj):
    names = ("ln_g", "ln_b", "ffn1_w_in", "ffn1_w_out", "w_in", "ret_gn_g", "w_ret_o", "q_norm_g", "kv_norm_g", "w_uq", "w_ukv",
             "w_mla_o", "w_out", "ffn2_w_in", "ffn2_w_out", "ple_w_gate", "ple_w_proj")
    ws = dict(zip(names, (ln_g, ln_b, ffn1_w_in, ffn1_w_out, w_in, ret_gn_g, w_ret_o, q_norm_g, kv_norm_g, w_uq, w_ukv, w_mla_o,
                          w_out, ffn2_w_in, ffn2_w_out, ple_w_gate, ple_w_proj)))
    ms = dict(zip(names, (m_ln_g, m_ln_b, m_ffn1_w_in, m_ffn1_w_out, m_w_in, m_ret_gn_g, m_w_ret_o, m_q_norm_g, m_kv_norm_g, m_w_uq,
                          m_w_ukv, m_w_mla_o, m_w_out, m_ffn2_w_in, m_ffn2_w_out, m_ple_w_gate, m_ple_w_proj)))
    vs = dict(zip(names, (v_ln_g, v_ln_b, v_ffn1_w_in, v_ffn1_w_out, v_w_in, v_ret_gn_g, v_w_ret_o, v_q_norm_g, v_kv_norm_g, v_w_uq,
                          v_w_ukv, v_w_mla_o, v_w_out, v_ffn2_w_in, v_ffn2_w_out, v_ple_w_gate, v_ple_w_proj)))

    parts = []
    for name, r in PACK:
        rows = _to_rows(name, ws[name])
        if _pad16(r) != r:
            rows = jnp.concatenate([rows, jnp.zeros((_pad16(r) - r, 1024), F32)], axis=0)
        parts.append(rows)
    wsh = jnp.concatenate(parts, axis=0).astype(BF16)
    ssh = jnp.concatenate([ln_g[0], ln_b[0]], axis=0)
    wall, sall = _all_gather(wsh, ssh)
    W = _unpack(wall)
    ln_full = sall.reshape(N_DEV, 2, 4, 128).transpose(1, 2, 0, 3).reshape(2, 4, 1024)

    loss_p, grad_x, G, small = _local_step(x[0], p[0, 0], positions, loss_target[0], W, ln_full[0], ln_full[1],
                                           ret_gn_g, q_norm_g, kv_norm_g)

    gparts = []
    for name, r in PACK:
        g = G[name].reshape(N_DEV, r, 1024)
        if _pad16(r) != r:
            g = jnp.concatenate([g, jnp.zeros((N_DEV, _pad16(r) - r, 1024), F32)], axis=1)
        gparts.append(g)
    gfull = jnp.concatenate(gparts, axis=1).astype(BF16)
    pad256 = lambda a: jnp.concatenate([a, jnp.zeros((1, 1024 - a.shape[1]), F32)], axis=1)
    gsmall = jnp.concatenate([small["ln_g"], small["ln_b"], small["ret_gn_g"].reshape(2, 1024), pad256(small["q_norm_g"]),
                              pad256(small["kv_norm_g"]), jnp.zeros((SMALL_ROWS - 12, 1024), F32)], axis=0)
    recv, srecv = _exchange_grads(gfull, gsmall)
    gsh = _sum_slots(recv, "sum_grads")
    ssum = _sum_slots(srecv, "sum_small_grads")

    grads = {}
    off = 0
    for name, r in PACK:
        grads[name] = _from_rows(name, gsh[off:off + r], ws[name].shape)
        off += _pad16(r)
    me = 4 * lax.axis_index("x") + 2 * lax.axis_index("y") + lax.axis_index("c")
    grads["ln_g"] = lax.dynamic_slice(ssum[0:4], (0, me * 128), (4, 128)).reshape(1, 4, 128)
    grads["ln_b"] = lax.dynamic_slice(ssum[4:8], (0, me * 128), (4, 128)).reshape(1, 4, 128)
    grads["ret_gn_g"] = ssum[8:10].reshape(1, 2048)
    grads["q_norm_g"] = ssum[10:11, :256]
    grads["kv_norm_g"] = ssum[11:12, :256]

    delta, new_m, new_v = {}, {}, {}
    for name in names:
        delta[name], new_m[name], new_v[name] = _adamw(ws[name], grads[name], ms[name], vs[name], "adamw_" + name)

    loss = lax.psum(loss_p[0, 0], ("x", "y", "c"))
    return (loss, grad_x[None], *[grads[n] for n in names], *[delta[n] for n in names],
            *[new_m[n] for n in names], *[new_v[n] for n in names])
```

```python
import functools
import math

import jax
import jax.numpy as jnp
from jax import lax
from jax.experimental import pallas as pl
from jax.experimental.pallas import tpu as pltpu

F32 = jnp.float32
BF16 = jnp.bfloat16

N_DEV = 8
D = 1024
D_FF = 2816
D_PLE = 256
CHUNK = 64
HEADS = 8
RET_DK = 128
RET_DV = 256
MLA_NOPE = 128
MLA_ROPE = 64
MLA_DV = 128
LORA = 256
ROPE_BASE = 10000.0
EPS = 1e-5
ALPHA = 2.0 ** 0.25
RET_SCALE = RET_DK ** -0.5
MLA_SCALE = (MLA_NOPE + MLA_ROPE) ** -0.5
NEG = -1e30

ADAM_LR = 0.001
ADAM_B1 = 0.9
ADAM_B2 = 0.999
ADAM_EPS = 1e-08
ADAM_WD = 0.01
ADAM_STEP = 10

P_RQ, P_RK, P_RV, P_RG, P_GR, P_GM, P_CQ, P_CKV, P_KPE, P_W = 0, 1024, 2048, 4096, 6144, 7168, 8192, 8448, 8704, 8960
W_IN_COLS = 8768
RET_L = 256
ATT_TF = 1024
ATT_TB = 512
LOG2E = math.log2(math.e)
Q_PRESCALE = MLA_SCALE * LOG2E

PACK = (("ffn1_w_in", 704), ("ffn1_w_out", 352), ("w_in", 1096), ("w_ret_o", 256), ("w_uq", 48), ("w_ukv", 64),
        ("w_mla_o", 128), ("w_out", 128), ("ffn2_w_in", 704), ("ffn2_w_out", 352), ("ple_w_gate", 128), ("ple_w_proj", 32))


def _pad16(r):
    return -(-r // 16) * 16


PACK_ROWS = sum(_pad16(r) for _, r in PACK)
SMALL_ROWS = 16


def _pcall(body, **kw):
    return pl.pallas_call(body, **kw)


def _pick(dim, prefs):
    for p in prefs:
        if dim % p == 0:
            return p
    return dim


def _sigmoid(x):
    return 1.0 / (1.0 + jnp.exp(-x))


def _silu(x):
    return x * _sigmoid(x)


def _ln(r, g, b):
    mu = jnp.mean(r, axis=-1, keepdims=True)
    var = jnp.mean(jnp.square(r - mu), axis=-1, keepdims=True)
    return (r - mu) * lax.rsqrt(var + EPS) * g + b


def _rms(x, g):
    return x * lax.rsqrt(jnp.mean(jnp.square(x), axis=-1, keepdims=True) + EPS) * g


def _dot(a, b, ca, cb):
    return lax.dot_general(a, b, (((ca,), (cb,)), ((), ())), preferred_element_type=F32)


def _accum(ref, val):
    @pl.when(pl.program_id(0) == 0)
    def _():
        ref[...] = jnp.zeros_like(ref)

    ref[...] += val


def _mm(a, b, *, ta=False, tb=False, add=None, out_dtype=F32, name, tm=None, tn=None, tk=None):
    if ta:
        K, M = a.shape
    else:
        M, K = a.shape
    if tb:
        N, K2 = b.shape
    else:
        K2, N = b.shape
    assert K == K2, (a.shape, b.shape, ta, tb)
    tm = tm or _pick(M, (1024, 512, 768, 256, 128))
    tn = tn or _pick(N, (512, 256, 128))
    tk = tk or (K if K <= 2816 else _pick(K, (1024, 1280, 512, 256)))
    nk = K // tk
    grid = (M // tm, N // tn, nk)
    a_spec = pl.BlockSpec((tk, tm), lambda i, j, k: (k, i)) if ta else pl.BlockSpec((tm, tk), lambda i, j, k: (i, k))
    b_spec = pl.BlockSpec((tn, tk), lambda i, j, k: (j, k)) if tb else pl.BlockSpec((tk, tn), lambda i, j, k: (k, j))
    o_spec = pl.BlockSpec((tm, tn), lambda i, j, k: (i, j))
    ca, cb = (0 if ta else 1), (1 if tb else 0)
    has_add = add is not None

    def body(*refs):
        if has_add:
            a_ref, b_ref, add_ref, o_ref, acc_ref = refs
        else:
            a_ref, b_ref, o_ref, acc_ref = refs
        k = pl.program_id(2)

        @pl.when(k == 0)
        def _():
            acc_ref[...] = jnp.zeros_like(acc_ref)

        acc_ref[...] += _dot(a_ref[...], b_ref[...], ca, cb)

        @pl.when(k == nk - 1)
        def _():
            r = acc_ref[...]
            if has_add:
                r = r + add_ref[...].astype(F32)
            o_ref[...] = r.astype(out_dtype)

    in_specs = [a_spec, b_spec] + ([o_spec] if has_add else [])
    args = (a, b) + ((add,) if has_add else ())
    return _pcall(
        body, grid=grid, in_specs=in_specs, out_specs=o_spec, out_shape=jax.ShapeDtypeStruct((M, N), out_dtype),
        scratch_shapes=[pltpu.VMEM((tm, tn), F32)], name=name,
        compiler_params=pltpu.CompilerParams(dimension_semantics=("parallel", "parallel", "arbitrary")),
    )(*args)


def _rows(body, T, tm, ins, outs, name, accs=()):
    in_specs, args = [], []
    for arr, w, cb in ins:
        if w is None:
            in_specs.append(pl.BlockSpec(arr.shape, lambda i, _n=arr.ndim: (0,) * _n))
        else:
            in_specs.append(pl.BlockSpec((tm, w), lambda i, _cb=cb: (i, _cb)))
        args.append(arr)
    out_specs = [pl.BlockSpec((tm, w), lambda i: (i, 0)) for w, _ in outs]
    out_shape = [jax.ShapeDtypeStruct((T, w), dt) for w, dt in outs]
    for r, w in accs:
        out_specs.append(pl.BlockSpec((r, w), lambda i: (0, 0)))
        out_shape.append(jax.ShapeDtypeStruct((r, w), F32))
    return _pcall(
        body, grid=(T // tm,), in_specs=in_specs, out_specs=out_specs, out_shape=out_shape, name=name,
        compiler_params=pltpu.CompilerParams(dimension_semantics=("arbitrary",)),
    )(*args)


def _swiglu_fwd(a, T, name):
    def body(g_ref, u_ref, o_ref):
        o_ref[...] = (_silu(g_ref[...]) * u_ref[...]).astype(BF16)

    return _rows(body, T, 256, [(a, D_FF, 0), (a, D_FF, 1)], [(D_FF, BF16)], name)[0]


def _swiglu_bwd(a, dhs, T, name):
    def body(g_ref, u_ref, d_ref, o_ref):
        _, vjp = jax.vjp(lambda g, u: _silu(g) * u, g_ref[...], u_ref[...])
        dg, du = vjp(d_ref[...])
        o_ref[:, :D_FF] = dg.astype(BF16)
        o_ref[:, D_FF:] = du.astype(BF16)

    return _rows(body, T, 128, [(a, D_FF, 0), (a, D_FF, 1), (dhs, D_FF, 0)], [(2 * D_FF, BF16)], name)[0]


def _ln_fwd(res, f, c, g, b, T, name):
    def body(r_ref, f_ref, g_ref, b_ref, h_ref, hb_ref):
        h = _ln(ALPHA * r_ref[...] + c * f_ref[...], g_ref[...], b_ref[...])
        h_ref[...] = h
        hb_ref[...] = h.astype(BF16)

    return _rows(body, T, 256, [(res, D, 0), (f, D, 0), (g, None, None), (b, None, None)], [(D, F32), (D, BF16)], name)


def _ln_bwd(res, f, c, g, b, dh, T, name):
    def body(r_ref, f_ref, g_ref, b_ref, d_ref, dr_ref, df_ref, dg_ref, db_ref):
        _, vjp = jax.vjp(lambda r, ff, gg, bb: _ln(ALPHA * r + c * ff, gg, bb), r_ref[...], f_ref[...], g_ref[...], b_ref[...])
        dr, df, dg, db = vjp(d_ref[...])
        dr_ref[...] = dr
        df_ref[...] = df.astype(BF16)
        _accum(dg_ref, dg)
        _accum(db_ref, db)

    return _rows(body, T, 256, [(res, D, 0), (f, D, 0), (g, None, None), (b, None, None), (dh, D, 0)],
                 [(D, F32), (D, BF16)], name, accs=[(1, D), (1, D)])


def _rope128(t, cos, sin_s):
    return t * cos + pltpu.roll(t, 64, 1) * sin_s


def _rope128_t(g, cos, sin_s):
    return g * cos - pltpu.roll(g, 64, 1) * sin_s


def _partner32(t):
    lane = lax.broadcasted_iota(jnp.int32, t.shape, 1)
    return jnp.where((lane & 32) == 0, pltpu.roll(t, 96, 1), pltpu.roll(t, 32, 1))


def _rope64(t, cos, sin_s):
    return t * cos + _partner32(t) * sin_s


def _rope64_t(g, cos, sin_s):
    return g * cos - _partner32(g) * sin_s


def _mixer_prep_fwd(proj, tabs, qg, kvg, T):
    cos128, sin128, cos64, sin64 = tabs

    def body(rq_ref, rk_ref, rv_ref, cq_ref, ckv_ref, kpe_ref, c1_ref, s1_ref, c2_ref, s2_ref, qg_ref, kvg_ref,
             oq_ref, ok_ref, ov_ref, oqn_ref, okvn_ref, okpe_ref):
        c1, s1 = c1_ref[...], s1_ref[...]
        for h in range(HEADS):
            sl = slice(h * RET_DK, (h + 1) * RET_DK)
            oq_ref[:, sl] = _rope128(rq_ref[:, sl], c1, s1).astype(BF16)
            ok_ref[:, sl] = (_rope128(rk_ref[:, sl], c1, s1) * RET_SCALE).astype(BF16)
        ov_ref[...] = rv_ref[...].astype(BF16)
        oqn_ref[...] = _rms(cq_ref[...], qg_ref[...]).astype(BF16)
        okvn_ref[...] = _rms(ckv_ref[...], kvg_ref[...]).astype(BF16)
        okpe_ref[...] = _rope64(kpe_ref[...], c2_ref[...], s2_ref[...]).astype(BF16)

    ins = [(proj, 1024, 0), (proj, 1024, 1), (proj, 2048, 1), (proj, 256, P_CQ // 256), (proj, 256, P_CKV // 256),
           (proj, 128, P_KPE // 128), (cos128, 128, 0), (sin128, 128, 0), (cos64, 128, 0), (sin64, 128, 0),
           (qg, None, None), (kvg, None, None)]
    outs = [(1024, BF16), (1024, BF16), (2048, BF16), (LORA, BF16), (LORA, BF16), (128, BF16)]
    return _rows(body, T, 256, ins, outs, "mixer_prep_fwd")


def _rms_bwd(proj, dqn, dkvn, dkpe_all, tabs, qg, kvg, T):
    _, _, cos64, sin64 = tabs

    def body(cq_ref, ckv_ref, dq_ref, dkv_ref, dk_ref, c2_ref, s2_ref, qg_ref, kvg_ref, ocq_ref, ockv_ref, okpe_ref,
             dqg_ref, dkvg_ref):
        _, vjp = jax.vjp(_rms, cq_ref[...], qg_ref[...])
        dx, dg = vjp(dq_ref[...])
        ocq_ref[...] = dx.astype(BF16)
        _accum(dqg_ref, dg)
        _, vjp = jax.vjp(_rms, ckv_ref[...], kvg_ref[...])
        dx, dg = vjp(dkv_ref[...])
        ockv_ref[...] = dx.astype(BF16)
        _accum(dkvg_ref, dg)
        g = dk_ref[:, 0:128]
        for h in range(1, HEADS):
            g = g + dk_ref[:, h * 128:(h + 1) * 128]
        lane = lax.broadcasted_iota(jnp.int32, g.shape, 1)
        g = jnp.where(lane < MLA_ROPE, g, 0.0)
        okpe_ref[...] = _rope64_t(g, c2_ref[...], s2_ref[...]).astype(BF16)

    ins = [(proj, 256, P_CQ // 256), (proj, 256, P_CKV // 256), (dqn, LORA, 0), (dkvn, LORA, 0), (dkpe_all, 1024, 0),
           (cos64, 128, 0), (sin64, 128, 0), (qg, None, None), (kvg, None, None)]
    return _rows(body, T, 256, ins, [(LORA, BF16), (LORA, BF16), (128, BF16)], "rms_bwd", accs=[(1, LORA), (1, LORA)])


def _gn_gate(y, rg, g):
    mu = jnp.mean(y, axis=-1, keepdims=True)
    var = jnp.mean(jnp.square(y - mu), axis=-1, keepdims=True)
    return _silu(rg) * ((y - mu) * lax.rsqrt(var + EPS) * g)


def _ret_out_fwd(y, proj, gn_g, T):
    def body(y_ref, rg_ref, g_ref, o_ref):
        for h in range(HEADS):
            sl = slice(h * RET_DV, (h + 1) * RET_DV)
            o_ref[:, sl] = _gn_gate(y_ref[:, sl], rg_ref[:, sl], g_ref[:, sl]).astype(BF16)

    return _rows(body, T, 256, [(y, 2048, 0), (proj, 2048, P_RG // 2048), (gn_g, None, None)], [(2048, BF16)], "ret_out_fwd")[0]


def _ret_out_bwd(y, proj, gn_g, dyr, T):
    def body(y_ref, rg_ref, g_ref, d_ref, dy_ref, drg_ref, dg_ref):
        dgs = []
        for h in range(HEADS):
            sl = slice(h * RET_DV, (h + 1) * RET_DV)
            _, vjp = jax.vjp(_gn_gate, y_ref[:, sl], rg_ref[:, sl], g_ref[:, sl])
            dy, drg, dg = vjp(d_ref[:, sl])
            dy_ref[:, sl] = dy.astype(BF16)
            drg_ref[:, sl] = drg.astype(BF16)
            dgs.append(dg)
        _accum(dg_ref, jnp.concatenate(dgs, axis=1))

    return _rows(body, T, 128, [(y, 2048, 0), (proj, 2048, P_RG // 2048), (gn_g, None, None), (dyr, 2048, 0)],
                 [(2048, BF16), (2048, BF16)], "ret_out_bwd", accs=[(1, 2048)])


def _q_assemble_fwd(q, tabs, T):
    _, _, cos64, sin64 = tabs

    def body(q_ref, c_ref, s_ref, on_ref, op_ref):
        on_ref[...] = (q_ref[:, :1024] * Q_PRESCALE).astype(BF16)
        c, s = c_ref[...], s_ref[...]
        lane = lax.broadcasted_iota(jnp.int32, c.shape, 1)
        for j in range(HEADS // 2):
            r = _rope64(q_ref[:, 1024 + 128 * j:1024 + 128 * (j + 1)], c, s) * Q_PRESCALE
            op_ref[:, 256 * j:256 * j + 128] = jnp.where(lane < 64, r, 0.0).astype(BF16)
            op_ref[:, 256 * j + 128:256 * j + 256] = jnp.where(lane < 64, pltpu.roll(r, 64, 1), 0.0).astype(BF16)

    return _rows(body, T, 256, [(q, 1536, 0), (cos64, 128, 0), (sin64, 128, 0)], [(1024, BF16), (1024, BF16)], "q_assemble_fwd")


def _q_assemble_bwd(dqn, dqpe, tabs, T):
    _, _, cos64, sin64 = tabs

    def body(dn_ref, dp_ref, c_ref, s_ref, on_ref, op_ref):
        on_ref[...] = dn_ref[...].astype(BF16)
        c, s = c_ref[...], s_ref[...]
        lane = lax.broadcasted_iota(jnp.int32, c.shape, 1)
        for j in range(HEADS // 2):
            g = jnp.where(lane < 64, dp_ref[:, 256 * j:256 * j + 128], pltpu.roll(dp_ref[:, 256 * j + 128:256 * j + 256], 64, 1))
            op_ref[:, 128 * j:128 * (j + 1)] = _rope64_t(g, c, s).astype(BF16)

    return _rows(body, T, 256, [(dqn, 1024, 0), (dqpe, 1024, 0), (cos64, 128, 0), (sin64, 128, 0)],
                 [(1024, BF16), (512, BF16)], "q_assemble_bwd")


def _mix_fn(gr, gm, yr, ym):
    return _sigmoid(gr) * yr + _sigmoid(gm) * ym


def _mix_fwd(proj, y_ret, y_mla, T):
    def body(gr_ref, gm_ref, yr_ref, ym_ref, o_ref):
        o_ref[...] = _mix_fn(gr_ref[...], gm_ref[...], yr_ref[...], ym_ref[...]).astype(BF16)

    return _rows(body, T, 256, [(proj, 1024, P_GR // 1024), (proj, 1024, P_GM // 1024), (y_ret, D, 0), (y_mla, D, 0)],
                 [(D, BF16)], "mix_fwd")[0]


def _mix_bwd(proj, y_ret, y_mla, dmix, T):
    def body(gr_ref, gm_ref, yr_ref, ym_ref, d_ref, dgr_ref, dgm_ref, dyr_ref, dym_ref):
        _, vjp = jax.vjp(_mix_fn, gr_ref[...], gm_ref[...], yr_ref[...], ym_ref[...])
        dgr, dgm, dyr, dym = vjp(d_ref[...])
        dgr_ref[...] = dgr.astype(BF16)
        dgm_ref[...] = dgm.astype(BF16)
        dyr_ref[...] = dyr.astype(BF16)
        dym_ref[...] = dym.astype(BF16)

    return _rows(body, T, 256, [(proj, 1024, P_GR // 1024), (proj, 1024, P_GM // 1024), (y_ret, D, 0), (y_mla, D, 0), (dmix, D, 0)],
                 [(D, BF16)] * 4, "mix_bwd")


def _head(h3, gl, pp, tgt, g, b, T):
    def body(h_ref, gl_ref, pp_ref, t_ref, g_ref, b_ref, dh_ref, dgl_ref, dpp_ref, dg_ref, db_ref, loss_ref):
        def f(h, gg, p, lg, lb):
            return _ln(ALPHA * h + _sigmoid(gg) * p, lg, lb)

        h4, vjp = jax.vjp(f, h_ref[...], gl_ref[...], pp_ref[...], g_ref[...], b_ref[...])
        err = h4 - t_ref[...]
        dh, dgl, dpp, dg, db = vjp(err * (1.0 / D))
        dh_ref[...] = dh
        dgl_ref[...] = dgl.astype(BF16)
        dpp_ref[...] = dpp.astype(BF16)
        _accum(dg_ref, dg)
        _accum(db_ref, db)
        part = 0.5 * jnp.sum(jnp.mean(jnp.square(err), axis=-1, keepdims=True), axis=0, keepdims=True)
        _accum(loss_ref, jnp.broadcast_to(part, loss_ref.shape))

    return _rows(body, T, 256, [(h3, D, 0), (gl, D, 0), (pp, D, 0), (tgt, D, 0), (g, None, None), (b, None, None)],
                 [(D, F32), (D, BF16), (D, BF16)], "head", accs=[(1, D), (1, D), (8, 128)])


def _attn_delta(do, o, T):
    def body(d_ref, o_ref, db_ref, dl_ref):
        for h in range(HEADS):
            sl = slice(h * MLA_DV, (h + 1) * MLA_DV)
            d = d_ref[:, sl]
            dl = jnp.sum(d * o_ref[:, sl].astype(F32), axis=-1, keepdims=True)
            dl_ref[:, sl] = jnp.broadcast_to(dl, d.shape)
        db_ref[...] = d_ref[...].astype(BF16)

    return _rows(body, T, 256, [(do, D, 0), (o, D, 0)], [(D, BF16), (D, F32)], "attn_delta")


def _ret_consts():
    L = RET_L
    lg = jnp.log(1.0 - 2.0 ** (-5.0 - jnp.arange(HEADS, dtype=F32)))[:, None, None]
    idx = jnp.arange(L, dtype=F32)
    ch = jnp.arange(L) // CHUNK
    dist = idx[:, None] - idx[None, :]
    same = (ch[:, None] == ch[None, :])[None]
    earlier = (ch[None, :] < ch[:, None])[None]
    dm = jnp.where(same, jnp.exp(lg * jnp.abs(dist)[None]), jnp.where(earlier, jnp.exp(lg * dist[None]), 0.0))
    xi = jnp.broadcast_to(jnp.exp(lg * (idx + 1.0)[None, :, None]), (HEADS, L, 128))
    zeta = jnp.broadcast_to(jnp.exp(lg * (L - 1.0 - idx)[None, :, None]), (HEADS, L, 128))
    gl = jnp.broadcast_to(jnp.exp(lg * float(L)), (HEADS, 8, 128))
    return dm.astype(F32), xi.astype(F32), zeta.astype(F32), gl.astype(F32)


def _ret_fwd(q, k, v, consts, T):
    dm, xi, zeta, gl = consts
    L = RET_L
    n_sc = T // L

    def body(q_ref, k_ref, v_ref, dm_ref, xi_ref, ze_ref, gl_ref, y_ref, s_ref, st_ref):
        @pl.when(pl.program_id(1) == 0)
        def _():
            st_ref[...] = jnp.zeros_like(st_ref)

        qq, kk, vv = q_ref[...], k_ref[...], v_ref[...]
        st = st_ref[...]
        s_ref[0, 0] = st
        p = (_dot(qq, kk, 1, 1) * dm_ref[0]).astype(BF16)
        cross = _dot(qq, st.astype(BF16), 1, 0)
        xi_c = jnp.concatenate([xi_ref[0], xi_ref[0]], axis=1)
        y_ref[...] = _dot(p, vv, 1, 0) + cross * xi_c
        kz = (kk.astype(F32) * ze_ref[0]).astype(BF16)
        gl2 = jnp.concatenate([gl_ref[0, 0:1, :], gl_ref[0, 0:1, :]], axis=1)
        st_ref[...] = st * gl2 + _dot(kz, vv, 0, 0)

    return _pcall(
        body, grid=(HEADS, n_sc),
        in_specs=[pl.BlockSpec((L, 128), lambda h, n: (n, h)), pl.BlockSpec((L, 128), lambda h, n: (n, h)),
                  pl.BlockSpec((L, 256), lambda h, n: (n, h)), pl.BlockSpec((1, L, L), lambda h, n: (h, 0, 0)),
                  pl.BlockSpec((1, L, 128), lambda h, n: (h, 0, 0)), pl.BlockSpec((1, L, 128), lambda h, n: (h, 0, 0)),
                  pl.BlockSpec((1, 8, 128), lambda h, n: (h, 0, 0))],
        out_specs=[pl.BlockSpec((L, 256), lambda h, n: (n, h)), pl.BlockSpec((1, 1, 128, 256), lambda h, n: (h, n, 0, 0))],
        out_shape=[jax.ShapeDtypeStruct((T, HEADS * RET_DV), F32), jax.ShapeDtypeStruct((HEADS, n_sc, 128, 256), F32)],
        scratch_shapes=[pltpu.VMEM((128, 256), F32)], name="ret_fwd",
        compiler_params=pltpu.CompilerParams(dimension_semantics=("arbitrary", "arbitrary")),
    )(q, k, v, dm, xi, zeta, gl)


def _ret_bwd(q, k, v, states, dy, consts, tabs, T):
    dm, xi, zeta, gl = consts
    cos128, sin128, _, _ = tabs
    L = RET_L
    n_sc = T // L

    def body(q_ref, k_ref, v_ref, dy_ref, s_ref, dm_ref, xi_ref, ze_ref, gl_ref, c_ref, sn_ref, dq_ref, dk_ref, dv_ref, g_ref):
        @pl.when(pl.program_id(1) == 0)
        def _():
            g_ref[...] = jnp.zeros_like(g_ref)

        qq, kk, vv, dyy = q_ref[...], k_ref[...], v_ref[...], dy_ref[...]
        dmm = dm_ref[0]
        gb = g_ref[...].astype(BF16)
        sb = s_ref[0, 0].astype(BF16)
        xi_c = jnp.concatenate([xi_ref[0], xi_ref[0]], axis=1)
        pb = (_dot(qq, kk, 1, 1) * dmm).astype(BF16)
        kz = (kk.astype(F32) * ze_ref[0]).astype(BF16)
        dv_ref[...] = (_dot(pb, dyy, 0, 0) + _dot(kz, gb, 1, 0)).astype(BF16)
        da = (_dot(dyy, vv, 1, 1) * dmm).astype(BF16)
        dyx = (dyy.astype(F32) * xi_c).astype(BF16)
        dq = _dot(da, kk, 1, 0) + _dot(dyx, sb, 1, 1)
        dk = _dot(da, qq, 0, 0) + _dot(vv, gb, 1, 1) * ze_ref[0]
        c, sn = c_ref[...], sn_ref[...]
        dq_ref[...] = _rope128_t(dq, c, sn).astype(BF16)
        dk_ref[...] = (_rope128_t(dk, c, sn) * RET_SCALE).astype(BF16)
        gl2 = jnp.concatenate([gl_ref[0, 0:1, :], gl_ref[0, 0:1, :]], axis=1)
        g_ref[...] = g_ref[...] * gl2 + _dot(qq, dyx, 0, 0)

    rev = lambda n: n_sc - 1 - n
    return _pcall(
        body, grid=(HEADS, n_sc),
        in_specs=[pl.BlockSpec((L, 128), lambda h, n: (rev(n), h)), pl.BlockSpec((L, 128), lambda h, n: (rev(n), h)),
                  pl.BlockSpec((L, 256), lambda h, n: (rev(n), h)), pl.BlockSpec((L, 256), lambda h, n: (rev(n), h)),
                  pl.BlockSpec((1, 1, 128, 256), lambda h, n: (h, rev(n), 0, 0)),
                  pl.BlockSpec((1, L, L), lambda h, n: (h, 0, 0)), pl.BlockSpec((1, L, 128), lambda h, n: (h, 0, 0)),
                  pl.BlockSpec((1, L, 128), lambda h, n: (h, 0, 0)), pl.BlockSpec((1, 8, 128), lambda h, n: (h, 0, 0)),
                  pl.BlockSpec((L, 128), lambda h, n: (rev(n), 0)), pl.BlockSpec((L, 128), lambda h, n: (rev(n), 0))],
        out_specs=[pl.BlockSpec((L, 128), lambda h, n: (rev(n), h)), pl.BlockSpec((L, 128), lambda h, n: (rev(n), h)),
                   pl.BlockSpec((L, 256), lambda h, n: (rev(n), h))],
        out_shape=[jax.ShapeDtypeStruct((T, 1024), BF16), jax.ShapeDtypeStruct((T, 1024), BF16), jax.ShapeDtypeStruct((T, 2048), BF16)],
        scratch_shapes=[pltpu.VMEM((128, 256), F32)], name="ret_bwd",
        compiler_params=pltpu.CompilerParams(dimension_semantics=("arbitrary", "arbitrary")),
    )(q, k, v, dy, states, dm, xi, zeta, gl, cos128, sin128)


def _diag_mask(t):
    row = lax.broadcasted_iota(jnp.int32, (t, t), 0)
    col = lax.broadcasted_iota(jnp.int32, (t, t), 1)
    return lax.shift_right_logical(col, 6) <= lax.shift_right_logical(row, 6)


def _tri_steps(nb, by_key):
    if by_key:
        pairs = [(i, j) for j in range(nb) for i in range(j, nb)]
    else:
        pairs = [(i, j) for i in range(nb) for j in range(i + 1)]
    return jnp.array([a for a, _ in pairs], jnp.int32), jnp.array([b for _, b in pairs], jnp.int32)


def _attn_fwd(qn, qpe, kv, kpe, T):
    t = min(ATT_TF, T)
    nb = T // t
    ii, jj = _tri_steps(nb, by_key=False)

    def body(ii_ref, jj_ref, qn_ref, qp_ref, kn_ref, kp_ref, v_ref, o_ref, lse_ref, m_sc, l_sc, acc_sc):
        st = pl.program_id(1)
        i, j = ii_ref[st], jj_ref[st]

        @pl.when(j == 0)
        def _():
            m_sc[...] = jnp.full_like(m_sc, NEG)
            l_sc[...] = jnp.zeros_like(l_sc)
            acc_sc[...] = jnp.zeros_like(acc_sc)

        def update(diag):
            q = jnp.concatenate([qn_ref[...], qp_ref[...]], axis=1)
            k = jnp.concatenate([kn_ref[...], kp_ref[...]], axis=1)
            s = _dot(q, k, 1, 1)
            if diag:
                s = jnp.where(_diag_mask(t), s, NEG)
            m_prev = m_sc[...]
            m_new = jnp.maximum(m_prev, jnp.max(s, axis=1, keepdims=True))
            a = jnp.exp2(m_prev - m_new)
            p = jnp.exp2(s - m_new[:, 0:1])
            l_sc[...] = a * l_sc[...] + jnp.sum(p, axis=1, keepdims=True)
            acc_sc[...] = a * acc_sc[...] + _dot(p.astype(BF16), v_ref[...], 1, 0)
            m_sc[...] = m_new

        @pl.when(j < i)
        def _():
            update(False)

        @pl.when(j == i)
        def _():
            update(True)
            o_ref[...] = (acc_sc[...] / l_sc[...]).astype(BF16)
            lse_ref[...] = m_sc[...] + jnp.log2(l_sc[...])

    qs = pl.BlockSpec((t, 128), lambda h, s, ii, jj: (ii[s], h))
    grid_spec = pltpu.PrefetchScalarGridSpec(
        num_scalar_prefetch=2, grid=(HEADS, int(ii.shape[0])),
        in_specs=[qs, qs, pl.BlockSpec((t, 128), lambda h, s, ii, jj: (jj[s], h)), pl.BlockSpec((t, 128), lambda h, s, ii, jj: (jj[s], 0)),
                  pl.BlockSpec((t, 128), lambda h, s, ii, jj: (jj[s], HEADS + h))],
        out_specs=[qs, qs],
        scratch_shapes=[pltpu.VMEM((t, 128), F32), pltpu.VMEM((t, 128), F32), pltpu.VMEM((t, 128), F32)])
    return _pcall(
        body, grid_spec=grid_spec, out_shape=[jax.ShapeDtypeStruct((T, D), BF16), jax.ShapeDtypeStruct((T, D), F32)], name="attn_fwd",
        compiler_params=pltpu.CompilerParams(dimension_semantics=("arbitrary", "arbitrary")),
    )(ii, jj, qn, qpe, kv, kpe, kv)


def _attn_bwd(qn, qpe, kv, kpe, do, lse, delta, T):
    t = min(ATT_TB, T)
    nb = T // t
    ii, jj = _tri_steps(nb, by_key=True)

    def body(ii_ref, jj_ref, qn_ref, qp_ref, kn_ref, kp_ref, v_ref, do_ref, lse_ref, dl_ref,
             dqn_ref, dqp_ref, dkn_ref, dkp_ref, dv_ref, dk_sc, dv_sc):
        st = pl.program_id(1)
        i, j = ii_ref[st], jj_ref[st]

        @pl.when(st == 0)
        def _():
            dqn_ref[...] = jnp.zeros_like(dqn_ref)
            dqp_ref[...] = jnp.zeros_like(dqp_ref)

        @pl.when(i == j)
        def _():
            dk_sc[...] = jnp.zeros_like(dk_sc)
            dv_sc[...] = jnp.zeros_like(dv_sc)

        def update(diag):
            q = jnp.concatenate([qn_ref[...], qp_ref[...]], axis=1)
            k = jnp.concatenate([kn_ref[...], kp_ref[...]], axis=1)
            dob = do_ref[...]
            s = _dot(q, k, 1, 1)
            if diag:
                s = jnp.where(_diag_mask(t), s, NEG)
            p = jnp.exp2(s - lse_ref[:, 0:1])
            dv_sc[...] += _dot(p.astype(BF16), dob, 0, 0)
            dp = _dot(dob, v_ref[...], 1, 1)
            ds = (p * (dp - dl_ref[:, 0:1])).astype(BF16)
            dk_sc[...] += _dot(ds, q, 0, 0)
            dq = _dot(ds, k, 1, 0) * MLA_SCALE
            rows = pl.ds(pl.multiple_of(i * t, t), t)
            dqn_ref[rows, :] += dq[:, :128]
            dqp_ref[rows, :] += dq[:, 128:]

        @pl.when(i > j)
        def _():
            update(False)

        @pl.when(i == j)
        def _():
            update(True)

        @pl.when(i == nb - 1)
        def _():
            dkn_ref[...] = (dk_sc[:, :128] * (1.0 / LOG2E)).astype(BF16)
            dkp_ref[...] = dk_sc[:, 128:] * (1.0 / LOG2E)
            dv_ref[...] = dv_sc[...].astype(BF16)

    qs = pl.BlockSpec((t, 128), lambda h, s, ii, jj: (ii[s], h))
    ks = pl.BlockSpec((t, 128), lambda h, s, ii, jj: (jj[s], h))
    hs = pl.BlockSpec((T, 128), lambda h, s, ii, jj: (0, h))
    grid_spec = pltpu.PrefetchScalarGridSpec(
        num_scalar_prefetch=2, grid=(HEADS, int(ii.shape[0])),
        in_specs=[qs, qs, ks, pl.BlockSpec((t, 128), lambda h, s, ii, jj: (jj[s], 0)),
                  pl.BlockSpec((t, 128), lambda h, s, ii, jj: (jj[s], HEADS + h)), qs, qs, qs],
        out_specs=[hs, hs, ks, ks, ks],
        scratch_shapes=[pltpu.VMEM((t, 256), F32), pltpu.VMEM((t, 128), F32)])
    return _pcall(
        body, grid_spec=grid_spec,
        out_shape=[jax.ShapeDtypeStruct((T, D), F32), jax.ShapeDtypeStruct((T, D), F32), jax.ShapeDtypeStruct((T, D), BF16),
                   jax.ShapeDtypeStruct((T, D), F32), jax.ShapeDtypeStruct((T, D), BF16)], name="attn_bwd",
        compiler_params=pltpu.CompilerParams(dimension_semantics=("arbitrary", "arbitrary")),
    )(ii, jj, qn, qpe, kv, kpe, kv, do, lse, delta)


def _mesh_pos():
    x, y, c = lax.axis_index("x"), lax.axis_index("y"), lax.axis_index("c")
    return x, y, c, 4 * x + 2 * y + c


def _peer(x, y, c, k):
    px, py, pc = (x + ((k >> 2) & 1)) % 2, (y + ((k >> 1) & 1)) % 2, (c + (k & 1)) % 2
    return (px, py, pc), 4 * px + 2 * py + pc


_ANY = pl.BlockSpec(memory_space=pl.ANY)


def _all_gather(wsh, ssh):
    def body(w_ref, s_ref, wall_ref, sall_ref, send_sems, recv_sems, loc_sems):
        x, y, c, me = _mesh_pos()
        loc = [pltpu.make_async_copy(w_ref, wall_ref.at[me], loc_sems.at[0]),
               pltpu.make_async_copy(s_ref, sall_ref.at[me], loc_sems.at[1])]
        for cp in loc:
            cp.start()
        sends, recvs = [], []
        for k in range(1, N_DEV):
            to, pidx = _peer(x, y, c, k)
            for n, (src, dst) in enumerate(((w_ref, wall_ref), (s_ref, sall_ref))):
                sem = 2 * (k - 1) + n
                sends.append(pltpu.make_async_remote_copy(src_ref=src, dst_ref=dst.at[me], send_sem=send_sems.at[sem],
                                                          recv_sem=recv_sems.at[sem], device_id=to, device_id_type=pl.DeviceIdType.MESH))
                recvs.append(pltpu.make_async_remote_copy(src_ref=src, dst_ref=dst.at[pidx], send_sem=send_sems.at[sem],
                                                          recv_sem=recv_sems.at[sem], device_id=to, device_id_type=pl.DeviceIdType.MESH))
        for cp in sends:
            cp.start()
        for cp in recvs:
            cp.wait_recv()
        for cp in sends:
            cp.wait_send()
        for cp in loc:
            cp.wait()

    return _pcall(
        body, in_specs=[_ANY, _ANY], out_specs=[_ANY, _ANY],
        out_shape=[jax.ShapeDtypeStruct((N_DEV,) + wsh.shape, wsh.dtype), jax.ShapeDtypeStruct((N_DEV,) + ssh.shape, ssh.dtype)],
        scratch_shapes=[pltpu.SemaphoreType.DMA((2 * (N_DEV - 1),)), pltpu.SemaphoreType.DMA((2 * (N_DEV - 1),)), pltpu.SemaphoreType.DMA((2,))],
        name="all_gather_weights",
    )(wsh, ssh)


def _exchange_grads(gfull, gsmall):
    def body(g_ref, s_ref, recv_ref, srecv_ref, send_sems, recv_sems, loc_sems):
        x, y, c, me = _mesh_pos()
        loc = [pltpu.make_async_copy(g_ref.at[me], recv_ref.at[me], loc_sems.at[0]),
               pltpu.make_async_copy(s_ref, srecv_ref.at[me], loc_sems.at[1])]
        for cp in loc:
            cp.start()
        sends, recvs = [], []
        for k in range(1, N_DEV):
            to, pidx = _peer(x, y, c, k)
            for n, (src, dst) in enumerate(((g_ref.at[pidx], recv_ref), (s_ref, srecv_ref))):
                sem = 2 * (k - 1) + n
                sends.append(pltpu.make_async_remote_copy(src_ref=src, dst_ref=dst.at[me], send_sem=send_sems.at[sem],
                                                          recv_sem=recv_sems.at[sem], device_id=to, device_id_type=pl.DeviceIdType.MESH))
                recvs.append(pltpu.make_async_remote_copy(src_ref=src, dst_ref=dst.at[pidx], send_sem=send_sems.at[sem],
                                                          recv_sem=recv_sems.at[sem], device_id=to, device_id_type=pl.DeviceIdType.MESH))
        for cp in sends:
            cp.start()
        for cp in recvs:
            cp.wait_recv()
        for cp in sends:
            cp.wait_send()
        for cp in loc:
            cp.wait()

    return _pcall(
        body, in_specs=[_ANY, _ANY], out_specs=[_ANY, _ANY],
        out_shape=[jax.ShapeDtypeStruct(gfull.shape, gfull.dtype), jax.ShapeDtypeStruct((N_DEV,) + gsmall.shape, gsmall.dtype)],
        scratch_shapes=[pltpu.SemaphoreType.DMA((2 * (N_DEV - 1),)), pltpu.SemaphoreType.DMA((2 * (N_DEV - 1),)), pltpu.SemaphoreType.DMA((2,))],
        name="exchange_grads",
    )(gfull, gsmall)


def _sum_slots(recv, name):
    n, R, C = recv.shape
    tr = _pick(R, (400, 256, 128, 16, 8))

    def body(r_ref, o_ref):
        acc = r_ref[0].astype(F32)
        for s in range(1, n):
            acc = acc + r_ref[s].astype(F32)
        o_ref[...] = acc

    return _pcall(body, grid=(R // tr,), in_specs=[pl.BlockSpec((n, tr, C), lambda i: (0, i, 0))],
                  out_specs=pl.BlockSpec((tr, C), lambda i: (i, 0)), out_shape=jax.ShapeDtypeStruct((R, C), F32), name=name)(recv)


def _adamw(w, g, m, v, name):
    shape = w.shape
    w2, g2, m2, v2 = (a.reshape(-1, shape[-1]) for a in (w, g, m, v))
    R, C = w2.shape
    tr = _pick(R, (256, 128, 64, 32, 16, 8)) if R > 256 else R

    def body(w_ref, g_ref, m_ref, v_ref, d_ref, nm_ref, nv_ref):
        gg = g_ref[...]
        nm = ADAM_B1 * m_ref[...] + (1.0 - ADAM_B1) * gg
        nv = ADAM_B2 * v_ref[...] + (1.0 - ADAM_B2) * jnp.square(gg)
        m_hat = nm / (1.0 - ADAM_B1 ** ADAM_STEP)
        v_hat = nv / (1.0 - ADAM_B2 ** ADAM_STEP)
        d_ref[...] = -ADAM_LR * (m_hat / (jnp.sqrt(v_hat) + ADAM_EPS) + ADAM_WD * w_ref[...])
        nm_ref[...] = nm
        nv_ref[...] = nv

    spec = pl.BlockSpec((tr, C), lambda i: (i, 0))
    d, nm, nv = _pcall(body, grid=(R // tr,), in_specs=[spec] * 4, out_specs=[spec] * 3,
                       out_shape=[jax.ShapeDtypeStruct((R, C), F32)] * 3, name=name)(w2, g2, m2, v2)
    return d.reshape(shape), nm.reshape(shape), nv.reshape(shape)


def _to_rows(name, w):
    w = w[0]
    if name in ("ffn1_w_in", "ffn2_w_in", "w_in"):
        return w.T
    if name in ("w_uq", "w_ukv", "ple_w_proj"):
        return w.T.reshape(-1, 1024)
    return w


def _from_rows(name, g, shape):
    if name in ("ffn1_w_in", "ffn2_w_in", "w_in"):
        return g.T.reshape(shape)
    if name in ("w_uq", "w_ukv", "ple_w_proj"):
        return g.reshape(-1, shape[1]).T.reshape(shape)
    return g.reshape(shape)


def _unpack(wall):
    out, off = {}, 0
    for name, r in PACK:
        out[name] = wall[:, off:off + r, :].reshape(N_DEV * r, 1024)
        off += _pad16(r)
    return out


def _w_in_internal(wt):
    return jnp.concatenate([wt[0:6144], wt[6720:8768], wt[6144:6720], jnp.zeros((P_W - W_IN_COLS, 1024), wt.dtype)], axis=0)


def _w_in_external(d):
    return jnp.concatenate([d[0:6144], d[8192:8768], d[6144:8192]], axis=0)


def _rope_tables(positions):
    pos = positions[0].astype(F32)

    def cs(half):
        inv = ROPE_BASE ** (-jnp.arange(half, dtype=F32) / half)
        ang = pos[:, None] * inv
        return jnp.cos(ang), jnp.sin(ang)

    c, s = cs(64)
    c2, s2 = cs(32)
    return (jnp.concatenate([c, c], axis=1), jnp.concatenate([-s, s], axis=1),
            jnp.concatenate([c2, c2, c2, c2], axis=1), jnp.concatenate([-s2, s2, -s2, s2], axis=1))


def _local_step(x, p, positions, target, W, ln_g, ln_b, gn_g, qg, kvg):
    T = x.shape[0]
    tabs = _rope_tables(positions)
    rc = _ret_consts()
    lg = [ln_g[i:i + 1] for i in range(4)]
    lb = [ln_b[i:i + 1] for i in range(4)]
    xb = x.astype(BF16)
    pb = p.astype(BF16)
    w_in_t = _w_in_internal(W["w_in"])
    wuq = W["w_uq"].reshape(1536, LORA).reshape(HEADS, 192, LORA)
    wuq = jnp.concatenate([wuq[:, :128].reshape(1024, LORA), wuq[:, 128:].reshape(512, LORA)], axis=0)
    wukv = W["w_ukv"].reshape(2048, LORA).reshape(HEADS, 2, 128, LORA).transpose(1, 0, 2, 3).reshape(2048, LORA)
    wp_t = W["ple_w_proj"].reshape(1024, D_PLE)

    a1 = _mm(xb, W["ffn1_w_in"], tb=True, name="ffn1_in")
    hs1 = _swiglu_fwd(a1, T, "ffn1_act")
    f1 = _mm(hs1, W["ffn1_w_out"], name="ffn1_out")
    h1, h1b = _ln_fwd(x, f1, 0.5, lg[0], lb[0], T, "ln0")
    proj = _mm(h1b, w_in_t, tb=True, name="mixer_in")
    rq, rk, rv, qn, kvn, kpe = _mixer_prep_fwd(proj, tabs, qg, kvg, T)
    y, states = _ret_fwd(rq, rk, rv, rc, T)
    yr = _ret_out_fwd(y, proj, gn_g, T)
    y_ret = _mm(yr, W["w_ret_o"], name="ret_o")
    q = _mm(qn, wuq, tb=True, name="mla_uq")
    kv = _mm(kvn, wukv, tb=True, out_dtype=BF16, name="mla_ukv")
    qnope, qpe = _q_assemble_fwd(q, tabs, T)
    o, lse = _attn_fwd(qnope, qpe, kv, kpe, T)
    y_mla = _mm(o, W["w_mla_o"], name="mla_o")
    mix = _mix_fwd(proj, y_ret, y_mla, T)
    mixed = _mm(mix, W["w_out"], name="mixer_out")
    h2, h2b = _ln_fwd(h1, mixed, 1.0, lg[1], lb[1], T, "ln1")
    a2 = _mm(h2b, W["ffn2_w_in"], tb=True, name="ffn2_in")
    hs2 = _swiglu_fwd(a2, T, "ffn2_act")
    f2 = _mm(hs2, W["ffn2_w_out"], name="ffn2_out")
    h3, h3b = _ln_fwd(h2, f2, 0.5, lg[2], lb[2], T, "ln2")
    gl = _mm(h3b, W["ple_w_gate"], name="ple_gate")
    pp = _mm(pb, wp_t, tb=True, name="ple_proj")

    G = {}
    dh3_a, dgl, dpp, dg3, db3, loss = _head(h3, gl, pp, target, lg[3], lb[3], T)
    G["ple_w_gate"] = _mm(h3b, dgl, ta=True, name="d_ple_gate")
    G["ple_w_proj"] = _mm(dpp, pb, ta=True, name="d_ple_proj")
    dh3 = _mm(dgl, W["ple_w_gate"], tb=True, add=dh3_a, name="dh3")
    dh2_a, df2, dg2, db2 = _ln_bwd(h2, f2, 0.5, lg[2], lb[2], dh3, T, "ln2_bwd")
    G["ffn2_w_out"] = _mm(hs2, df2, ta=True, name="d_ffn2_out")
    dhs2 = _mm(df2, W["ffn2_w_out"], tb=True, name="dhs2")
    da2 = _swiglu_bwd(a2, dhs2, T, "ffn2_act_bwd")
    G["ffn2_w_in"] = _mm(da2, h2b, ta=True, name="d_ffn2_in")
    dh2 = _mm(da2, W["ffn2_w_in"], add=dh2_a, name="dh2")
    dh1_a, dmixed, dg1, db1 = _ln_bwd(h1, mixed, 1.0, lg[1], lb[1], dh2, T, "ln1_bwd")
    G["w_out"] = _mm(mix, dmixed, ta=True, name="d_mixer_out")
    dmix = _mm(dmixed, W["w_out"], tb=True, name="dmix")
    dgr, dgm, dy_ret, dy_mla = _mix_bwd(proj, y_ret, y_mla, dmix, T)
    G["w_mla_o"] = _mm(o, dy_mla, ta=True, name="d_mla_o")
    do = _mm(dy_mla, W["w_mla_o"], tb=True, name="do")
    dob, delta = _attn_delta(do, o, T)
    dqn_f, dqpe_f, dkn, dkpe_all, dv = _attn_bwd(qnope, qpe, kv, kpe, dob, lse, delta, T)
    dq_n, dq_r = _q_assemble_bwd(dqn_f, dqpe_f, tabs, T)
    g_uq = jnp.concatenate([_mm(dq_n, qn, ta=True, name="d_uq_nope"), _mm(dq_r, qn, ta=True, name="d_uq_rope")], axis=0)
    g_uq = jnp.concatenate([g_uq[:1024].reshape(HEADS, 128, LORA), g_uq[1024:].reshape(HEADS, 64, LORA)], axis=1)
    G["w_uq"] = g_uq.reshape(1536 * LORA // 1024, 1024)
    dqn = _mm(dq_r, wuq[1024:], add=_mm(dq_n, wuq[:1024], name="dqn_a"), name="dqn_b")
    g_ukv = jnp.stack([_mm(dkn, kvn, ta=True, name="d_ukv_k"), _mm(dv, kvn, ta=True, name="d_ukv_v")], axis=0)
    G["w_ukv"] = g_ukv.reshape(2, HEADS, 128, LORA).transpose(1, 0, 2, 3).reshape(2048 * LORA // 1024, 1024)
    dkvn = _mm(dv, wukv[1024:], add=_mm(dkn, wukv[:1024], name="dkvn_a"), name="dkvn_b")
    dcq, dckv, dkpe, dqg, dkvg = _rms_bwd(proj, dqn, dkvn, dkpe_all, tabs, qg, kvg, T)
    G["w_ret_o"] = _mm(yr, dy_ret, ta=True, name="d_ret_o")
    dyr = _mm(dy_ret, W["w_ret_o"], tb=True, name="dyr")
    dy, drg, dgn = _ret_out_bwd(y, proj, gn_g, dyr, T)
    drq, drk, drv = _ret_bwd(rq, rk, rv, states, dy, rc, tabs, T)
    dproj = jnp.concatenate([drq, drk, drv, drg, dgr, dgm, dcq, dckv, dkpe, jnp.zeros((T, P_W - P_KPE - 128), BF16)], axis=1)
    G["w_in"] = _w_in_external(_mm(dproj, h1b, ta=True, name="d_mixer_in"))
    dh1 = _mm(dproj, w_in_t, add=dh1_a, name="dh1")
    dx_a, df1, dg0, db0 = _ln_bwd(x, f1, 0.5, lg[0], lb[0], dh1, T, "ln0_bwd")
    G["ffn1_w_out"] = _mm(hs1, df1, ta=True, name="d_ffn1_out")
    dhs1 = _mm(df1, W["ffn1_w_out"], tb=True, name="dhs1")
    da1 = _swiglu_bwd(a1, dhs1, T, "ffn1_act_bwd")
    G["ffn1_w_in"] = _mm(da1, xb, ta=True, name="d_ffn1_in")
    grad_x = _mm(da1, W["ffn1_w_in"], add=dx_a, name="grad_x")

    small = dict(ln_g=jnp.concatenate([dg0, dg1, dg2, dg3], axis=0), ln_b=jnp.concatenate([db0, db1, db2, db3], axis=0),
                 ret_gn_g=dgn, q_norm_g=dqg, kv_norm_g=dkvg)
    return loss, grad_x, G, small


def kernel(x, p, positions, ln_g, ln_b, ffn1_w_in, ffn1_w_out, w_in, ret_gn_g, w_ret_o, q_norm_g, kv_norm_g, w_uq, w_ukv, w_mla_o, w_out, ffn2_w_in, ffn2_w_out, ple_w_gate, ple_w_proj, loss_target, m_ln_g, m_ln_b, m_ffn1_w_in, m_ffn1_w_out, m_w_in, m_ret_gn_g, m_w_ret_o, m_q_norm_g, m_kv_norm_g, m_w_uq, m_w_ukv, m_w_mla_o, m_w_out, m_ffn2_w_in, m_ffn2_w_out, m_ple_w_gate, m_ple_w_proj, v_ln_g, v_ln_b, v_ffn1_w_in, v_ffn1_w_out, v_w_in, v_ret_gn_g, v_w_ret_o, v_q_norm_g, v_kv_norm_g, v_w_uq, v_w_ukv, v_w_mla_o, v_w_out, v_ffn2_w_in, v_ffn2_w_out, v_ple_w_gate, v_ple_w_proj):
    names = ("ln_g", "ln_b", "ffn1_w_in", "ffn1_w_out", "w_in", "ret_gn_g", "w_ret_o", "q_norm_g", "kv_norm_g", "w_uq", "w_ukv",
             "w_mla_o", "w_out", "ffn2_w_in", "ffn2_w_out", "ple_w_gate", "ple_w_proj")
    ws = dict(zip(names, (ln_g, ln_b, ffn1_w_in, ffn1_w_out, w_in, ret_gn_g, w_ret_o, q_norm_g, kv_norm_g, w_uq, w_ukv, w_mla_o,
                          w_out, ffn2_w_in, ffn2_w_out, ple_w_gate, ple_w_proj)))
    ms = dict(zip(names, (m_ln_g, m_ln_b, m_ffn1_w_in, m_ffn1_w_out, m_w_in, m_ret_gn_g, m_w_ret_o, m_q_norm_g, m_kv_norm_g, m_w_uq,
                          m_w_ukv, m_w_mla_o, m_w_out, m_ffn2_w_in, m_ffn2_w_out, m_ple_w_gate, m_ple_w_proj)))
    vs = dict(zip(names, (v_ln_g, v_ln_b, v_ffn1_w_in, v_ffn1_w_out, v_w_in, v_ret_gn_g, v_w_ret_o, v_q_norm_g, v_kv_norm_g, v_w_uq,
                          v_w_ukv, v_w_mla_o, v_w_out, v_ffn2_w_in, v_ffn2_w_out, v_ple_w_gate, v_ple_w_proj)))

    parts = []
    for name, r in PACK:
        rows = _to_rows(name, ws[name])
        if _pad16(r) != r:
            rows = jnp.concatenate([rows, jnp.zeros((_pad16(r) - r, 1024), F32)], axis=0)
        parts.append(rows)
    wsh = jnp.concatenate(parts, axis=0).astype(BF16)
    ssh = jnp.concatenate([ln_g[0], ln_b[0]], axis=0)
    wall, sall = _all_gather(wsh, ssh)
    W = _unpack(wall)
    ln_full = sall.reshape(N_DEV, 2, 4, 128).transpose(1, 2, 0, 3).reshape(2, 4, 1024)

    loss_p, grad_x, G, small = _local_step(x[0], p[0, 0], positions, loss_target[0], W, ln_full[0], ln_full[1],
                                           ret_gn_g, q_norm_g, kv_norm_g)

    gparts = []
    for name, r in PACK:
        g = G[name].reshape(N_DEV, r, 1024)
        if _pad16(r) != r:
            g = jnp.concatenate([g, jnp.zeros((N_DEV, _pad16(r) - r, 1024), F32)], axis=1)
        gparts.append(g)
    gfull = jnp.concatenate(gparts, axis=1).astype(BF16)
    pad256 = lambda a: jnp.concatenate([a, jnp.zeros((1, 1024 - a.shape[1]), F32)], axis=1)
    gsmall = jnp.concatenate([small["ln_g"], small["ln_b"], small["ret_gn_g"].reshape(2, 1024), pad256(small["q_norm_g"]),
                              pad256(small["kv_norm_g"]), jnp.zeros((SMALL_ROWS - 12, 1024), F32)], axis=0)
    recv, srecv = _exchange_grads(gfull, gsmall)
    gsh = _sum_slots(recv, "sum_grads")
    ssum = _sum_slots(srecv, "sum_small_grads")

    grads = {}
    off = 0
    for name, r in PACK:
        grads[name] = _from_rows(name, gsh[off:off + r], ws[name].shape)
        off += _pad16(r)
    me = 4 * lax.axis_index("x") + 2 * lax.axis_index("y") + lax.axis_index("c")
    grads["ln_g"] = lax.dynamic_slice(ssum[0:4], (0, me * 128), (4, 128)).reshape(1, 4, 128)
    grads["ln_b"] = lax.dynamic_slice(ssum[4:8], (0, me * 128), (4, 128)).reshape(1, 4, 128)
    grads["ret_gn_g"] = ssum[8:10].reshape(1, 2048)
    grads["q_norm_g"] = ssum[10:11, :256]
    grads["kv_norm_g"] = ssum[11:12, :256]

    delta, new_m, new_v = {}, {}, {}
    for name in names:
        delta[name], new_m[name], new_v[name] = _adamw(ws[name], grads[name], ms[name], vs[name], "adamw_" + name)

    loss = lax.psum(loss_p[0, 0], ("x", "y", "c"))
    return (loss, grad_x[None], *[grads[n] for n in names], *[delta[n] for n in names],
            *[new_m[n] for n in names], *[new_v[n] for n in names])
```

```python
import functools
import math

import jax
import jax.numpy as jnp
from jax import lax
from jax.experimental import pallas as pl
from jax.experimental.pallas import tpu as pltpu

F32 = jnp.float32
BF16 = jnp.bfloat16

N_DEV = 8
D = 1024
D_FF = 2816
D_PLE = 256
CHUNK = 64
HEADS = 8
RET_DK = 128
RET_DV = 256
MLA_NOPE = 128
MLA_ROPE = 64
MLA_DV = 128
LORA = 256
ROPE_BASE = 10000.0
EPS = 1e-5
ALPHA = 2.0 ** 0.25
RET_SCALE = RET_DK ** -0.5
MLA_SCALE = (MLA_NOPE + MLA_ROPE) ** -0.5
NEG = -1e30

ADAM_LR = 0.001
ADAM_B1 = 0.9
ADAM_B2 = 0.999
ADAM_EPS = 1e-08
ADAM_WD = 0.01
ADAM_STEP = 10

P_RQ, P_RK, P_RV, P_RG, P_GR, P_GM, P_CQ, P_CKV, P_KPE, P_W = 0, 1024, 2048, 4096, 6144, 7168, 8192, 8448, 8704, 8960
W_IN_COLS = 8768
RET_L = 256
ATT_TF = 1024
ATT_TB = 512
LOG2E = math.log2(math.e)
Q_PRESCALE = MLA_SCALE * LOG2E

PACK = (("ffn1_w_in", 704), ("ffn1_w_out", 352), ("w_in", 1096), ("w_ret_o", 256), ("w_uq", 48), ("w_ukv", 64),
        ("w_mla_o", 128), ("w_out", 128), ("ffn2_w_in", 704), ("ffn2_w_out", 352), ("ple_w_gate", 128), ("ple_w_proj", 32))


def _pad16(r):
    return -(-r // 16) * 16


PACK_ROWS = sum(_pad16(r) for _, r in PACK)
SMALL_ROWS = 16


def _pcall(body, **kw):
    return pl.pallas_call(body, **kw)


def _pick(dim, prefs):
    for p in prefs:
        if dim % p == 0:
            return p
    return dim


def _sigmoid(x):
    return 1.0 / (1.0 + jnp.exp(-x))


def _silu(x):
    return x * _sigmoid(x)


def _ln(r, g, b):
    mu = jnp.mean(r, axis=-1, keepdims=True)
    var = jnp.mean(jnp.square(r - mu), axis=-1, keepdims=True)
    return (r - mu) * lax.rsqrt(var + EPS) * g + b


def _rms(x, g):
    return x * lax.rsqrt(jnp.mean(jnp.square(x), axis=-1, keepdims=True) + EPS) * g


def _dot(a, b, ca, cb):
    return lax.dot_general(a, b, (((ca,), (cb,)), ((), ())), preferred_element_type=F32)


def _accum(ref, val):
    @pl.when(pl.program_id(0) == 0)
    def _():
        ref[...] = jnp.zeros_like(ref)

    ref[...] += val


def _mm(a, b, *, ta=False, tb=False, add=None, out_dtype=F32, name, tm=None, tn=None, tk=None):
    if ta:
        K, M = a.shape
    else:
        M, K = a.shape
    if tb:
        N, K2 = b.shape
    else:
        K2, N = b.shape
    assert K == K2, (a.shape, b.shape, ta, tb)
    big = (1024, 1408, 1280, 768, 512, 256, 128)
    tm = tm or _pick(M, big)
    tn = tn or (N if N <= 1024 else _pick(N, big))
    tk = tk or (K if K <= 2816 else _pick(K, (2048, 1408, 1280, 1024, 512) if tm <= 1024 else (1024, 1408, 1280, 512)))
    nk = K // tk
    grid = (M // tm, N // tn, nk)
    a_spec = pl.BlockSpec((tk, tm), lambda i, j, k: (k, i)) if ta else pl.BlockSpec((tm, tk), lambda i, j, k: (i, k))
    b_spec = pl.BlockSpec((tn, tk), lambda i, j, k: (j, k)) if tb else pl.BlockSpec((tk, tn), lambda i, j, k: (k, j))
    o_spec = pl.BlockSpec((tm, tn), lambda i, j, k: (i, j))
    ca, cb = (0 if ta else 1), (1 if tb else 0)
    has_add = add is not None

    def body(*refs):
        a_ref, b_ref = refs[0], refs[1]
        add_ref = refs[2] if has_add else None
        o_ref = refs[3] if has_add else refs[2]

        def finish(r):
            if has_add:
                r = r + add_ref[...].astype(F32)
            o_ref[...] = r.astype(out_dtype)

        if nk == 1:
            finish(_dot(a_ref[...], b_ref[...], ca, cb))
            return
        acc_ref = refs[-1]
        k = pl.program_id(2)

        @pl.when(k == 0)
        def _():
            acc_ref[...] = jnp.zeros_like(acc_ref)

        acc_ref[...] += _dot(a_ref[...], b_ref[...], ca, cb)

        @pl.when(k == nk - 1)
        def _():
            finish(acc_ref[...])

    in_specs = [a_spec, b_spec] + ([o_spec] if has_add else [])
    args = (a, b) + ((add,) if has_add else ())
    return _pcall(
        body, grid=grid, in_specs=in_specs, out_specs=o_spec, out_shape=jax.ShapeDtypeStruct((M, N), out_dtype),
        scratch_shapes=[pltpu.VMEM((tm, tn), F32)] if nk > 1 else [], name=name,
        compiler_params=pltpu.CompilerParams(dimension_semantics=("parallel", "parallel", "arbitrary")),
    )(*args)


def _rows(body, T, tm, ins, outs, name, accs=()):
    in_specs, args = [], []
    for arr, w, cb in ins:
        if w is None:
            in_specs.append(pl.BlockSpec(arr.shape, lambda i, _n=arr.ndim: (0,) * _n))
        else:
            in_specs.append(pl.BlockSpec((tm, w), lambda i, _cb=cb: (i, _cb)))
        args.append(arr)
    out_specs = [pl.BlockSpec((tm, w), lambda i: (i, 0)) for w, _ in outs]
    out_shape = [jax.ShapeDtypeStruct((T, w), dt) for w, dt in outs]
    for r, w in accs:
        out_specs.append(pl.BlockSpec((r, w), lambda i: (0, 0)))
        out_shape.append(jax.ShapeDtypeStruct((r, w), F32))
    return _pcall(
        body, grid=(T // tm,), in_specs=in_specs, out_specs=out_specs, out_shape=out_shape, name=name,
        compiler_params=pltpu.CompilerParams(dimension_semantics=("arbitrary",)),
    )(*args)


def _swiglu_fwd(a, T, name):
    def body(g_ref, u_ref, o_ref):
        o_ref[...] = (_silu(g_ref[...]) * u_ref[...]).astype(BF16)

    return _rows(body, T, 256, [(a, D_FF, 0), (a, D_FF, 1)], [(D_FF, BF16)], name)[0]


def _swiglu_bwd(a, dhs, T, name):
    def body(g_ref, u_ref, d_ref, o_ref):
        _, vjp = jax.vjp(lambda g, u: _silu(g) * u, g_ref[...], u_ref[...])
        dg, du = vjp(d_ref[...])
        o_ref[:, :D_FF] = dg.astype(BF16)
        o_ref[:, D_FF:] = du.astype(BF16)

    return _rows(body, T, 128, [(a, D_FF, 0), (a, D_FF, 1), (dhs, D_FF, 0)], [(2 * D_FF, BF16)], name)[0]


def _ln_fwd(res, f, c, g, b, T, name):
    def body(r_ref, f_ref, g_ref, b_ref, h_ref, hb_ref):
        h = _ln(ALPHA * r_ref[...] + c * f_ref[...], g_ref[...], b_ref[...])
        h_ref[...] = h
        hb_ref[...] = h.astype(BF16)

    return _rows(body, T, 256, [(res, D, 0), (f, D, 0), (g, None, None), (b, None, None)], [(D, F32), (D, BF16)], name)


def _ln_bwd(res, f, c, g, b, dh, T, name):
    def body(r_ref, f_ref, g_ref, b_ref, d_ref, dr_ref, df_ref, dg_ref, db_ref):
        _, vjp = jax.vjp(lambda r, ff, gg, bb: _ln(ALPHA * r + c * ff, gg, bb), r_ref[...], f_ref[...], g_ref[...], b_ref[...])
        dr, df, dg, db = vjp(d_ref[...])
        dr_ref[...] = dr
        df_ref[...] = df.astype(BF16)
        _accum(dg_ref, dg)
        _accum(db_ref, db)

    return _rows(body, T, 256, [(res, D, 0), (f, D, 0), (g, None, None), (b, None, None), (dh, D, 0)],
                 [(D, F32), (D, BF16)], name, accs=[(1, D), (1, D)])


def _rope128(t, cos, sin_s):
    return t * cos + pltpu.roll(t, 64, 1) * sin_s


def _rope128_t(g, cos, sin_s):
    return g * cos - pltpu.roll(g, 64, 1) * sin_s


def _partner32(t):
    lane = lax.broadcasted_iota(jnp.int32, t.shape, 1)
    return jnp.where((lane & 32) == 0, pltpu.roll(t, 96, 1), pltpu.roll(t, 32, 1))


def _rope64(t, cos, sin_s):
    return t * cos + _partner32(t) * sin_s


def _rope64_t(g, cos, sin_s):
    return g * cos - _partner32(g) * sin_s


def _mixer_prep_fwd(proj, tabs, qg, kvg, T):
    cos128, sin128, cos64, sin64 = tabs

    def body(rq_ref, rk_ref, rv_ref, cq_ref, ckv_ref, kpe_ref, c1_ref, s1_ref, c2_ref, s2_ref, qg_ref, kvg_ref,
             oq_ref, ok_ref, ov_ref, oqn_ref, okvn_ref, okpe_ref):
        c1, s1 = c1_ref[...], s1_ref[...]
        for h in range(HEADS):
            sl = slice(h * RET_DK, (h + 1) * RET_DK)
            oq_ref[:, sl] = _rope128(rq_ref[:, sl], c1, s1).astype(BF16)
            ok_ref[:, sl] = (_rope128(rk_ref[:, sl], c1, s1) * RET_SCALE).astype(BF16)
        ov_ref[...] = rv_ref[...].astype(BF16)
        oqn_ref[...] = _rms(cq_ref[...], qg_ref[...]).astype(BF16)
        okvn_ref[...] = _rms(ckv_ref[...], kvg_ref[...]).astype(BF16)
        okpe_ref[...] = _rope64(kpe_ref[...], c2_ref[...], s2_ref[...]).astype(BF16)

    ins = [(proj, 1024, 0), (proj, 1024, 1), (proj, 2048, 1), (proj, 256, P_CQ // 256), (proj, 256, P_CKV // 256),
           (proj, 128, P_KPE // 128), (cos128, 128, 0), (sin128, 128, 0), (cos64, 128, 0), (sin64, 128, 0),
           (qg, None, None), (kvg, None, None)]
    outs = [(1024, BF16), (1024, BF16), (2048, BF16), (LORA, BF16), (LORA, BF16), (128, BF16)]
    return _rows(body, T, 256, ins, outs, "mixer_prep_fwd")


def _rms_bwd(proj, dqn, dkvn, dkpe_all, tabs, qg, kvg, T):
    _, _, cos64, sin64 = tabs

    def body(cq_ref, ckv_ref, dq_ref, dkv_ref, dk_ref, c2_ref, s2_ref, qg_ref, kvg_ref, ocq_ref, ockv_ref, okpe_ref,
             dqg_ref, dkvg_ref):
        _, vjp = jax.vjp(_rms, cq_ref[...], qg_ref[...])
        dx, dg = vjp(dq_ref[...])
        ocq_ref[...] = dx.astype(BF16)
        _accum(dqg_ref, dg)
        _, vjp = jax.vjp(_rms, ckv_ref[...], kvg_ref[...])
        dx, dg = vjp(dkv_ref[...])
        ockv_ref[...] = dx.astype(BF16)
        _accum(dkvg_ref, dg)
        g = dk_ref[:, 0:128]
        for h in range(1, HEADS):
            g = g + dk_ref[:, h * 128:(h + 1) * 128]
        lane = lax.broadcasted_iota(jnp.int32, g.shape, 1)
        g = jnp.where(lane < MLA_ROPE, g, 0.0)
        okpe_ref[...] = _rope64_t(g, c2_ref[...], s2_ref[...]).astype(BF16)

    ins = [(proj, 256, P_CQ // 256), (proj, 256, P_CKV // 256), (dqn, LORA, 0), (dkvn, LORA, 0), (dkpe_all, 1024, 0),
           (cos64, 128, 0), (sin64, 128, 0), (qg, None, None), (kvg, None, None)]
    return _rows(body, T, 256, ins, [(LORA, BF16), (LORA, BF16), (128, BF16)], "rms_bwd", accs=[(1, LORA), (1, LORA)])


def _gn_gate(y, rg, g):
    mu = jnp.mean(y, axis=-1, keepdims=True)
    var = jnp.mean(jnp.square(y - mu), axis=-1, keepdims=True)
    return _silu(rg) * ((y - mu) * lax.rsqrt(var + EPS) * g)


def _ret_out_fwd(y, proj, gn_g, T):
    def body(y_ref, rg_ref, g_ref, o_ref):
        for h in range(HEADS):
            sl = slice(h * RET_DV, (h + 1) * RET_DV)
            o_ref[:, sl] = _gn_gate(y_ref[:, sl], rg_ref[:, sl], g_ref[:, sl]).astype(BF16)

    return _rows(body, T, 256, [(y, 2048, 0), (proj, 2048, P_RG // 2048), (gn_g, None, None)], [(2048, BF16)], "ret_out_fwd")[0]


def _ret_out_bwd(y, proj, gn_g, dyr, T):
    def body(y_ref, rg_ref, g_ref, d_ref, dy_ref, drg_ref, dg_ref):
        dgs = []
        for h in range(HEADS):
            sl = slice(h * RET_DV, (h + 1) * RET_DV)
            _, vjp = jax.vjp(_gn_gate, y_ref[:, sl], rg_ref[:, sl], g_ref[:, sl])
            dy, drg, dg = vjp(d_ref[:, sl])
            dy_ref[:, sl] = dy.astype(BF16)
            drg_ref[:, sl] = drg.astype(BF16)
            dgs.append(dg)
        _accum(dg_ref, jnp.concatenate(dgs, axis=1))

    return _rows(body, T, 128, [(y, 2048, 0), (proj, 2048, P_RG // 2048), (gn_g, None, None), (dyr, 2048, 0)],
                 [(2048, BF16), (2048, BF16)], "ret_out_bwd", accs=[(1, 2048)])


def _q_assemble_fwd(q, tabs, T):
    _, _, cos64, sin64 = tabs

    def body(q_ref, c_ref, s_ref, on_ref, op_ref):
        on_ref[...] = (q_ref[:, :1024] * Q_PRESCALE).astype(BF16)
        c, s = c_ref[...], s_ref[...]
        lane = lax.broadcasted_iota(jnp.int32, c.shape, 1)
        for j in range(HEADS // 2):
            r = _rope64(q_ref[:, 1024 + 128 * j:1024 + 128 * (j + 1)], c, s) * Q_PRESCALE
            op_ref[:, 256 * j:256 * j + 128] = jnp.where(lane < 64, r, 0.0).astype(BF16)
            op_ref[:, 256 * j + 128:256 * j + 256] = jnp.where(lane < 64, pltpu.roll(r, 64, 1), 0.0).astype(BF16)

    return _rows(body, T, 256, [(q, 1536, 0), (cos64, 128, 0), (sin64, 128, 0)], [(1024, BF16), (1024, BF16)], "q_assemble_fwd")


def _q_assemble_bwd(dqn, dqpe, tabs, T):
    _, _, cos64, sin64 = tabs

    def body(dn_ref, dp_ref, c_ref, s_ref, on_ref, op_ref):
        on_ref[...] = dn_ref[...].astype(BF16)
        c, s = c_ref[...], s_ref[...]
        lane = lax.broadcasted_iota(jnp.int32, c.shape, 1)
        for j in range(HEADS // 2):
            g = jnp.where(lane < 64, dp_ref[:, 256 * j:256 * j + 128], pltpu.roll(dp_ref[:, 256 * j + 128:256 * j + 256], 64, 1))
            op_ref[:, 128 * j:128 * (j + 1)] = _rope64_t(g, c, s).astype(BF16)

    return _rows(body, T, 256, [(dqn, 1024, 0), (dqpe, 1024, 0), (cos64, 128, 0), (sin64, 128, 0)],
                 [(1024, BF16), (512, BF16)], "q_assemble_bwd")


def _mix_fn(gr, gm, yr, ym):
    return _sigmoid(gr) * yr + _sigmoid(gm) * ym


def _mix_fwd(proj, y_ret, y_mla, T):
    def body(gr_ref, gm_ref, yr_ref, ym_ref, o_ref):
        o_ref[...] = _mix_fn(gr_ref[...], gm_ref[...], yr_ref[...], ym_ref[...]).astype(BF16)

    return _rows(body, T, 256, [(proj, 1024, P_GR // 1024), (proj, 1024, P_GM // 1024), (y_ret, D, 0), (y_mla, D, 0)],
                 [(D, BF16)], "mix_fwd")[0]


def _mix_bwd(proj, y_ret, y_mla, dmix, T):
    def body(gr_ref, gm_ref, yr_ref, ym_ref, d_ref, dgr_ref, dgm_ref, dyr_ref, dym_ref):
        _, vjp = jax.vjp(_mix_fn, gr_ref[...], gm_ref[...], yr_ref[...], ym_ref[...])
        dgr, dgm, dyr, dym = vjp(d_ref[...])
        dgr_ref[...] = dgr.astype(BF16)
        dgm_ref[...] = dgm.astype(BF16)
        dyr_ref[...] = dyr.astype(BF16)
        dym_ref[...] = dym.astype(BF16)

    return _rows(body, T, 256, [(proj, 1024, P_GR // 1024), (proj, 1024, P_GM // 1024), (y_ret, D, 0), (y_mla, D, 0), (dmix, D, 0)],
                 [(D, BF16)] * 4, "mix_bwd")


def _head(h3, gl, pp, tgt, g, b, T):
    def body(h_ref, gl_ref, pp_ref, t_ref, g_ref, b_ref, dh_ref, dgl_ref, dpp_ref, dg_ref, db_ref, loss_ref):
        def f(h, gg, p, lg, lb):
            return _ln(ALPHA * h + _sigmoid(gg) * p, lg, lb)

        h4, vjp = jax.vjp(f, h_ref[...], gl_ref[...], pp_ref[...], g_ref[...], b_ref[...])
        err = h4 - t_ref[...]
        dh, dgl, dpp, dg, db = vjp(err * (1.0 / D))
        dh_ref[...] = dh
        dgl_ref[...] = dgl.astype(BF16)
        dpp_ref[...] = dpp.astype(BF16)
        _accum(dg_ref, dg)
        _accum(db_ref, db)
        part = 0.5 * jnp.sum(jnp.mean(jnp.square(err), axis=-1, keepdims=True), axis=0, keepdims=True)
        _accum(loss_ref, jnp.broadcast_to(part, loss_ref.shape))

    return _rows(body, T, 256, [(h3, D, 0), (gl, D, 0), (pp, D, 0), (tgt, D, 0), (g, None, None), (b, None, None)],
                 [(D, F32), (D, BF16), (D, BF16)], "head", accs=[(1, D), (1, D), (8, 128)])


def _attn_delta(do, o, T):
    def body(d_ref, o_ref, db_ref, dl_ref):
        for h in range(HEADS):
            sl = slice(h * MLA_DV, (h + 1) * MLA_DV)
            d = d_ref[:, sl]
            dl = jnp.sum(d * o_ref[:, sl].astype(F32), axis=-1, keepdims=True)
            dl_ref[:, sl] = jnp.broadcast_to(dl, d.shape)
        db_ref[...] = d_ref[...].astype(BF16)

    return _rows(body, T, 256, [(do, D, 0), (o, D, 0)], [(D, BF16), (D, F32)], "attn_delta")


def _ret_consts():
    L = RET_L
    lg = jnp.log(1.0 - 2.0 ** (-5.0 - jnp.arange(HEADS, dtype=F32)))[:, None, None]
    idx = jnp.arange(L, dtype=F32)
    ch = jnp.arange(L) // CHUNK
    dist = idx[:, None] - idx[None, :]
    same = (ch[:, None] == ch[None, :])[None]
    earlier = (ch[None, :] < ch[:, None])[None]
    dm = jnp.where(same, jnp.exp(lg * jnp.abs(dist)[None]), jnp.where(earlier, jnp.exp(lg * dist[None]), 0.0))
    xi = jnp.broadcast_to(jnp.exp(lg * (idx + 1.0)[None, :, None]), (HEADS, L, 128))
    zeta = jnp.broadcast_to(jnp.exp(lg * (L - 1.0 - idx)[None, :, None]), (HEADS, L, 128))
    gl = jnp.broadcast_to(jnp.exp(lg * float(L)), (HEADS, 8, 128))
    return dm.astype(F32), xi.astype(F32), zeta.astype(F32), gl.astype(F32)


def _ret_fwd(q, k, v, consts, T):
    dm, xi, zeta, gl = consts
    L = RET_L
    n_sc = T // L

    def body(q_ref, k_ref, v_ref, dm_ref, xi_ref, ze_ref, gl_ref, y_ref, s_ref, st_ref):
        @pl.when(pl.program_id(1) == 0)
        def _():
            st_ref[...] = jnp.zeros_like(st_ref)

        qq, kk, vv = q_ref[...], k_ref[...], v_ref[...]
        st = st_ref[...]
        s_ref[0, 0] = st
        p = (_dot(qq, kk, 1, 1) * dm_ref[0]).astype(BF16)
        cross = _dot(qq, st.astype(BF16), 1, 0)
        xi_c = jnp.concatenate([xi_ref[0], xi_ref[0]], axis=1)
        y_ref[...] = _dot(p, vv, 1, 0) + cross * xi_c
        kz = (kk.astype(F32) * ze_ref[0]).astype(BF16)
        gl2 = jnp.concatenate([gl_ref[0, 0:1, :], gl_ref[0, 0:1, :]], axis=1)
        st_ref[...] = st * gl2 + _dot(kz, vv, 0, 0)

    return _pcall(
        body, grid=(HEADS, n_sc),
        in_specs=[pl.BlockSpec((L, 128), lambda h, n: (n, h)), pl.BlockSpec((L, 128), lambda h, n: (n, h)),
                  pl.BlockSpec((L, 256), lambda h, n: (n, h)), pl.BlockSpec((1, L, L), lambda h, n: (h, 0, 0)),
                  pl.BlockSpec((1, L, 128), lambda h, n: (h, 0, 0)), pl.BlockSpec((1, L, 128), lambda h, n: (h, 0, 0)),
                  pl.BlockSpec((1, 8, 128), lambda h, n: (h, 0, 0))],
        out_specs=[pl.BlockSpec((L, 256), lambda h, n: (n, h)), pl.BlockSpec((1, 1, 128, 256), lambda h, n: (h, n, 0, 0))],
        out_shape=[jax.ShapeDtypeStruct((T, HEADS * RET_DV), F32), jax.ShapeDtypeStruct((HEADS, n_sc, 128, 256), F32)],
        scratch_shapes=[pltpu.VMEM((128, 256), F32)], name="ret_fwd",
        compiler_params=pltpu.CompilerParams(dimension_semantics=("arbitrary", "arbitrary")),
    )(q, k, v, dm, xi, zeta, gl)


def _ret_bwd(q, k, v, states, dy, consts, tabs, T):
    dm, xi, zeta, gl = consts
    cos128, sin128, _, _ = tabs
    L = RET_L
    n_sc = T // L

    def body(q_ref, k_ref, v_ref, dy_ref, s_ref, dm_ref, xi_ref, ze_ref, gl_ref, c_ref, sn_ref, dq_ref, dk_ref, dv_ref, g_ref):
        @pl.when(pl.program_id(1) == 0)
        def _():
            g_ref[...] = jnp.zeros_like(g_ref)

        qq, kk, vv, dyy = q_ref[...], k_ref[...], v_ref[...], dy_ref[...]
        dmm = dm_ref[0]
        gb = g_ref[...].astype(BF16)
        sb = s_ref[0, 0].astype(BF16)
        xi_c = jnp.concatenate([xi_ref[0], xi_ref[0]], axis=1)
        pb = (_dot(qq, kk, 1, 1) * dmm).astype(BF16)
        kz = (kk.astype(F32) * ze_ref[0]).astype(BF16)
        dv_ref[...] = (_dot(pb, dyy, 0, 0) + _dot(kz, gb, 1, 0)).astype(BF16)
        da = (_dot(dyy, vv, 1, 1) * dmm).astype(BF16)
        dyx = (dyy.astype(F32) * xi_c).astype(BF16)
        dq = _dot(da, kk, 1, 0) + _dot(dyx, sb, 1, 1)
        dk = _dot(da, qq, 0, 0) + _dot(vv, gb, 1, 1) * ze_ref[0]
        c, sn = c_ref[...], sn_ref[...]
        dq_ref[...] = _rope128_t(dq, c, sn).astype(BF16)
        dk_ref[...] = (_rope128_t(dk, c, sn) * RET_SCALE).astype(BF16)
        gl2 = jnp.concatenate([gl_ref[0, 0:1, :], gl_ref[0, 0:1, :]], axis=1)
        g_ref[...] = g_ref[...] * gl2 + _dot(qq, dyx, 0, 0)

    rev = lambda n: n_sc - 1 - n
    return _pcall(
        body, grid=(HEADS, n_sc),
        in_specs=[pl.BlockSpec((L, 128), lambda h, n: (rev(n), h)), pl.BlockSpec((L, 128), lambda h, n: (rev(n), h)),
                  pl.BlockSpec((L, 256), lambda h, n: (rev(n), h)), pl.BlockSpec((L, 256), lambda h, n: (rev(n), h)),
                  pl.BlockSpec((1, 1, 128, 256), lambda h, n: (h, rev(n), 0, 0)),
                  pl.BlockSpec((1, L, L), lambda h, n: (h, 0, 0)), pl.BlockSpec((1, L, 128), lambda h, n: (h, 0, 0)),
                  pl.BlockSpec((1, L, 128), lambda h, n: (h, 0, 0)), pl.BlockSpec((1, 8, 128), lambda h, n: (h, 0, 0)),
                  pl.BlockSpec((L, 128), lambda h, n: (rev(n), 0)), pl.BlockSpec((L, 128), lambda h, n: (rev(n), 0))],
        out_specs=[pl.BlockSpec((L, 128), lambda h, n: (rev(n), h)), pl.BlockSpec((L, 128), lambda h, n: (rev(n), h)),
                   pl.BlockSpec((L, 256), lambda h, n: (rev(n), h))],
        out_shape=[jax.ShapeDtypeStruct((T, 1024), BF16), jax.ShapeDtypeStruct((T, 1024), BF16), jax.ShapeDtypeStruct((T, 2048), BF16)],
        scratch_shapes=[pltpu.VMEM((128, 256), F32)], name="ret_bwd",
        compiler_params=pltpu.CompilerParams(dimension_semantics=("arbitrary", "arbitrary")),
    )(q, k, v, dy, states, dm, xi, zeta, gl, cos128, sin128)


def _diag_mask(t):
    row = lax.broadcasted_iota(jnp.int32, (t, t), 0)
    col = lax.broadcasted_iota(jnp.int32, (t, t), 1)
    return lax.shift_right_logical(col, 6) <= lax.shift_right_logical(row, 6)


def _tri_steps(nb, by_key):
    if by_key:
        pairs = [(i, j) for j in range(nb) for i in range(j, nb)]
    else:
        pairs = [(i, j) for i in range(nb) for j in range(i + 1)]
    return jnp.array([a for a, _ in pairs], jnp.int32), jnp.array([b for _, b in pairs], jnp.int32)


def _attn_fwd(qn, qpe, kv, kpe, T):
    t = min(ATT_TF, T)
    nb = T // t
    ii, jj = _tri_steps(nb, by_key=False)

    def body(ii_ref, jj_ref, qn_ref, qp_ref, kn_ref, kp_ref, v_ref, o_ref, lse_ref, m_sc, l_sc, acc_sc):
        st = pl.program_id(1)
        i, j = ii_ref[st], jj_ref[st]

        @pl.when(j == 0)
        def _():
            m_sc[...] = jnp.full_like(m_sc, NEG)
            l_sc[...] = jnp.zeros_like(l_sc)
            acc_sc[...] = jnp.zeros_like(acc_sc)

        def update(diag):
            q = jnp.concatenate([qn_ref[...], qp_ref[...]], axis=1)
            k = jnp.concatenate([kn_ref[...], kp_ref[...]], axis=1)
            s = _dot(q, k, 1, 1)
            if diag:
                s = jnp.where(_diag_mask(t), s, NEG)
            m_prev = m_sc[...]
            m_new = jnp.maximum(m_prev, jnp.max(s, axis=1, keepdims=True))
            a = jnp.exp2(m_prev - m_new)
            p = jnp.exp2(s - m_new[:, 0:1])
            l_sc[...] = a * l_sc[...] + jnp.sum(p, axis=1, keepdims=True)
            acc_sc[...] = a * acc_sc[...] + _dot(p.astype(BF16), v_ref[...], 1, 0)
            m_sc[...] = m_new

        @pl.when(j < i)
        def _():
            update(False)

        @pl.when(j == i)
        def _():
            update(True)
            o_ref[...] = (acc_sc[...] / l_sc[...]).astype(BF16)
            lse_ref[...] = m_sc[...] + jnp.log2(l_sc[...])

    qs = pl.BlockSpec((t, 128), lambda h, s, ii, jj: (ii[s], h))
    grid_spec = pltpu.PrefetchScalarGridSpec(
        num_scalar_prefetch=2, grid=(HEADS, int(ii.shape[0])),
        in_specs=[qs, qs, pl.BlockSpec((t, 128), lambda h, s, ii, jj: (jj[s], h)), pl.BlockSpec((t, 128), lambda h, s, ii, jj: (jj[s], 0)),
                  pl.BlockSpec((t, 128), lambda h, s, ii, jj: (jj[s], HEADS + h))],
        out_specs=[qs, qs],
        scratch_shapes=[pltpu.VMEM((t, 128), F32), pltpu.VMEM((t, 128), F32), pltpu.VMEM((t, 128), F32)])
    return _pcall(
        body, grid_spec=grid_spec, out_shape=[jax.ShapeDtypeStruct((T, D), BF16), jax.ShapeDtypeStruct((T, D), F32)], name="attn_fwd",
        compiler_params=pltpu.CompilerParams(dimension_semantics=("arbitrary", "arbitrary")),
    )(ii, jj, qn, qpe, kv, kpe, kv)


def _attn_bwd(qn, qpe, kv, kpe, do, lse, delta, T):
    t = min(ATT_TB, T)
    nb = T // t
    ii, jj = _tri_steps(nb, by_key=True)

    def body(ii_ref, jj_ref, qn_ref, qp_ref, kn_ref, kp_ref, v_ref, do_ref, lse_ref, dl_ref,
             dqn_ref, dqp_ref, dkn_ref, dkp_ref, dv_ref, dk_sc, dv_sc):
        st = pl.program_id(1)
        i, j = ii_ref[st], jj_ref[st]

        @pl.when(st == 0)
        def _():
            dqn_ref[...] = jnp.zeros_like(dqn_ref)
            dqp_ref[...] = jnp.zeros_like(dqp_ref)

        @pl.when(i == j)
        def _():
            dk_sc[...] = jnp.zeros_like(dk_sc)
            dv_sc[...] = jnp.zeros_like(dv_sc)

        def update(diag):
            q = jnp.concatenate([qn_ref[...], qp_ref[...]], axis=1)
            k = jnp.concatenate([kn_ref[...], kp_ref[...]], axis=1)
            dob = do_ref[...]
            s = _dot(q, k, 1, 1)
            if diag:
                s = jnp.where(_diag_mask(t), s, NEG)
            p = jnp.exp2(s - lse_ref[:, 0:1])
            dv_sc[...] += _dot(p.astype(BF16), dob, 0, 0)
            dp = _dot(dob, v_ref[...], 1, 1)
            ds = (p * (dp - dl_ref[:, 0:1])).astype(BF16)
            dk_sc[...] += _dot(ds, q, 0, 0)
            dq = _dot(ds, k, 1, 0) * MLA_SCALE
            rows = pl.ds(pl.multiple_of(i * t, t), t)
            dqn_ref[rows, :] += dq[:, :128]
            dqp_ref[rows, :] += dq[:, 128:]

        @pl.when(i > j)
        def _():
            update(False)

        @pl.when(i == j)
        def _():
            update(True)

        @pl.when(i == nb - 1)
        def _():
            dkn_ref[...] = (dk_sc[:, :128] * (1.0 / LOG2E)).astype(BF16)
            dkp_ref[...] = dk_sc[:, 128:] * (1.0 / LOG2E)
            dv_ref[...] = dv_sc[...].astype(BF16)

    qs = pl.BlockSpec((t, 128), lambda h, s, ii, jj: (ii[s], h))
    ks = pl.BlockSpec((t, 128), lambda h, s, ii, jj: (jj[s], h))
    hs = pl.BlockSpec((T, 128), lambda h, s, ii, jj: (0, h))
    grid_spec = pltpu.PrefetchScalarGridSpec(
        num_scalar_prefetch=2, grid=(HEADS, int(ii.shape[0])),
        in_specs=[qs, qs, ks, pl.BlockSpec((t, 128), lambda h, s, ii, jj: (jj[s], 0)),
                  pl.BlockSpec((t, 128), lambda h, s, ii, jj: (jj[s], HEADS + h)), qs, qs, qs],
        out_specs=[hs, hs, ks, ks, ks],
        scratch_shapes=[pltpu.VMEM((t, 256), F32), pltpu.VMEM((t, 128), F32)])
    return _pcall(
        body, grid_spec=grid_spec,
        out_shape=[jax.ShapeDtypeStruct((T, D), F32), jax.ShapeDtypeStruct((T, D), F32), jax.ShapeDtypeStruct((T, D), BF16),
                   jax.ShapeDtypeStruct((T, D), F32), jax.ShapeDtypeStruct((T, D), BF16)], name="attn_bwd",
        compiler_params=pltpu.CompilerParams(dimension_semantics=("arbitrary", "arbitrary")),
    )(ii, jj, qn, qpe, kv, kpe, kv, do, lse, delta)


def _mesh_pos():
    x, y, c = lax.axis_index("x"), lax.axis_index("y"), lax.axis_index("c")
    return x, y, c, 4 * x + 2 * y + c


def _peer(x, y, c, k):
    px, py, pc = (x + ((k >> 2) & 1)) % 2, (y + ((k >> 1) & 1)) % 2, (c + (k & 1)) % 2
    return (px, py, pc), 4 * px + 2 * py + pc


_ANY = pl.BlockSpec(memory_space=pl.ANY)


def _rcopy(src, dst, send_sems, recv_sems, k, to):
    return pltpu.make_async_remote_copy(src_ref=src, dst_ref=dst, send_sem=send_sems.at[k], recv_sem=recv_sems.at[k],
                                        device_id=to, device_id_type=pl.DeviceIdType.MESH)


def _all_gather(wsh, ssh):
    def body(w_ref, s_ref, wall_ref, sall_ref, send_sems, recv_sems, loc_sems):
        x, y, c, me = _mesh_pos()
        sib = (x, y, 1 - c)
        chips = [(1 - x, y), (x, 1 - y), (1 - x, 1 - y)]
        slot = lambda px, py, pc: 4 * px + 2 * py + pc
        loc = [pltpu.make_async_copy(w_ref, wall_ref.at[me], loc_sems.at[0]),
               pltpu.make_async_copy(s_ref, sall_ref.at[me], loc_sems.at[1])]
        for cp in loc:
            cp.start()
        sends, fwd_waits = [], []
        for n, (src, dst) in enumerate(((w_ref, wall_ref), (s_ref, sall_ref))):
            o = 7 * n
            sends.append(_rcopy(src, dst.at[me], send_sems, recv_sems, o, sib))
            for j, chip in enumerate(chips):
                sends.append(_rcopy(src, dst.at[me], send_sems, recv_sems, o + 1 + j, (*chip, c)))
        for cp in sends:
            cp.start()
        for n, (src, dst) in enumerate(((w_ref, wall_ref), (s_ref, sall_ref))):
            o = 7 * n
            for j, chip in enumerate(chips):
                got = dst.at[slot(*chip, c)]
                _rcopy(src, got, send_sems, recv_sems, o + 1 + j, sib).wait_recv()
                fw = _rcopy(got, got, send_sems, recv_sems, o + 4 + j, sib)
                fw.start()
                sends.append(fw)
            fwd_waits.append(_rcopy(src, dst.at[slot(x, y, 1 - c)], send_sems, recv_sems, o, sib))
            for j, chip in enumerate(chips):
                fwd_waits.append(_rcopy(src, dst.at[slot(*chip, 1 - c)], send_sems, recv_sems, o + 4 + j, sib))
        for cp in fwd_waits:
            cp.wait_recv()
        for cp in sends:
            cp.wait_send()
        for cp in loc:
            cp.wait()

    return _pcall(
        body, in_specs=[_ANY, _ANY], out_specs=[_ANY, _ANY],
        out_shape=[jax.ShapeDtypeStruct((N_DEV,) + wsh.shape, wsh.dtype), jax.ShapeDtypeStruct((N_DEV,) + ssh.shape, ssh.dtype)],
        scratch_shapes=[pltpu.SemaphoreType.DMA((14,)), pltpu.SemaphoreType.DMA((14,)), pltpu.SemaphoreType.DMA((2,))],
        name="all_gather_weights",
    )(wsh, ssh)


def _exchange_sibling(g4):
    def body(g_ref, r_ref, send_sems, recv_sems):
        x, y, c, _ = _mesh_pos()
        sib = (x, y, 1 - c)
        cps = [_rcopy(g_ref.at[q, 1 - c], r_ref.at[q], send_sems, recv_sems, q, sib) for q in range(4)]
        for cp in cps:
            cp.start()
        for cp in cps:
            cp.wait()

    n, _, R, C = g4.shape
    return _pcall(
        body, in_specs=[_ANY], out_specs=_ANY, out_shape=jax.ShapeDtypeStruct((n, R, C), g4.dtype),
        scratch_shapes=[pltpu.SemaphoreType.DMA((4,)), pltpu.SemaphoreType.DMA((4,))], name="exchange_sibling",
    )(g4)


def _sum_sibling(g4, rsib, cvec):
    n, _, R, C = g4.shape
    tr = _pick(R, (400, 256, 128, 16, 8))

    def body(c_ref, g_ref, r_ref, o_ref):
        o_ref[...] = (g_ref[...].astype(F32) + r_ref[...].astype(F32)).astype(o_ref.dtype)

    grid_spec = pltpu.PrefetchScalarGridSpec(
        num_scalar_prefetch=1, grid=(n, R // tr),
        in_specs=[pl.BlockSpec((None, None, tr, C), lambda q, i, cr: (q, cr[0], i, 0)), pl.BlockSpec((None, tr, C), lambda q, i, cr: (q, i, 0))],
        out_specs=pl.BlockSpec((None, tr, C), lambda q, i, cr: (q, i, 0)))
    return _pcall(body, grid_spec=grid_spec, out_shape=jax.ShapeDtypeStruct((n, R, C), g4.dtype), name="sum_sibling")(cvec, g4, rsib)


def _exchange_chips(part, gsmall):
    def body(p_ref, s_ref, recv_ref, srecv_ref, send_sems, recv_sems, loc_sems):
        x, y, c, me = _mesh_pos()
        myq = 2 * x + y
        loc = [pltpu.make_async_copy(p_ref.at[myq], recv_ref.at[myq], loc_sems.at[0]),
               pltpu.make_async_copy(s_ref, srecv_ref.at[me], loc_sems.at[1])]
        for cp in loc:
            cp.start()
        sends, recvs = [], []
        for j, (px, py) in enumerate([(1 - x, y), (x, 1 - y), (1 - x, 1 - y)]):
            q = 2 * px + py
            sends.append(_rcopy(p_ref.at[q], recv_ref.at[myq], send_sems, recv_sems, j, (px, py, c)))
            recvs.append(_rcopy(p_ref.at[q], recv_ref.at[q], send_sems, recv_sems, j, (px, py, c)))
        for k in range(1, N_DEV):
            to, pidx = _peer(x, y, c, k)
            sends.append(_rcopy(s_ref, srecv_ref.at[me], send_sems, recv_sems, 2 + k, to))
            recvs.append(_rcopy(s_ref, srecv_ref.at[pidx], send_sems, recv_sems, 2 + k, to))
        for cp in sends:
            cp.start()
        for cp in recvs:
            cp.wait_recv()
        for cp in sends:
            cp.wait_send()
        for cp in loc:
            cp.wait()

    return _pcall(
        body, in_specs=[_ANY, _ANY], out_specs=[_ANY, _ANY],
        out_shape=[jax.ShapeDtypeStruct(part.shape, part.dtype), jax.ShapeDtypeStruct((N_DEV,) + gsmall.shape, gsmall.dtype)],
        scratch_shapes=[pltpu.SemaphoreType.DMA((10,)), pltpu.SemaphoreType.DMA((10,)), pltpu.SemaphoreType.DMA((2,))],
        name="exchange_chips",
    )(part, gsmall)


def _sum_slots(recv, name):
    n, R, C = recv.shape
    tr = _pick(R, (400, 256, 128, 16, 8))

    def body(r_ref, o_ref):
        acc = r_ref[0].astype(F32)
        for s in range(1, n):
            acc = acc + r_ref[s].astype(F32)
        o_ref[...] = acc

    return _pcall(body, grid=(R // tr,), in_specs=[pl.BlockSpec((n, tr, C), lambda i: (0, i, 0))],
                  out_specs=pl.BlockSpec((tr, C), lambda i: (i, 0)), out_shape=jax.ShapeDtypeStruct((R, C), F32), name=name)(recv)


def _adamw(w, g, m, v, name):
    shape = w.shape
    w2, g2, m2, v2 = (a.reshape(-1, shape[-1]) for a in (w, g, m, v))
    R, C = w2.shape
    tr = _pick(R, (256, 128, 64, 32, 16, 8)) if R > 256 else R

    def body(w_ref, g_ref, m_ref, v_ref, d_ref, nm_ref, nv_ref):
        gg = g_ref[...]
        nm = ADAM_B1 * m_ref[...] + (1.0 - ADAM_B1) * gg
        nv = ADAM_B2 * v_ref[...] + (1.0 - ADAM_B2) * jnp.square(gg)
        m_hat = nm / (1.0 - ADAM_B1 ** ADAM_STEP)
        v_hat = nv / (1.0 - ADAM_B2 ** ADAM_STEP)
        d_ref[...] = -ADAM_LR * (m_hat / (jnp.sqrt(v_hat) + ADAM_EPS) + ADAM_WD * w_ref[...])
        nm_ref[...] = nm
        nv_ref[...] = nv

    spec = pl.BlockSpec((tr, C), lambda i: (i, 0))
    d, nm, nv = _pcall(body, grid=(R // tr,), in_specs=[spec] * 4, out_specs=[spec] * 3,
                       out_shape=[jax.ShapeDtypeStruct((R, C), F32)] * 3, name=name)(w2, g2, m2, v2)
    return d.reshape(shape), nm.reshape(shape), nv.reshape(shape)


def _to_rows(name, w):
    w = w[0]
    if name in ("ffn1_w_in", "ffn2_w_in", "w_in"):
        return w.T
    if name in ("w_uq", "w_ukv", "ple_w_proj"):
        return w.T.reshape(-1, 1024)
    return w


def _from_rows(name, g, shape):
    if name in ("ffn1_w_in", "ffn2_w_in", "w_in"):
        return g.T.reshape(shape)
    if name in ("w_uq", "w_ukv", "ple_w_proj"):
        return g.reshape(-1, shape[1]).T.reshape(shape)
    return g.reshape(shape)


def _unpack(wall):
    out, off = {}, 0
    for name, r in PACK:
        out[name] = wall[:, off:off + r, :].reshape(N_DEV * r, 1024)
        off += _pad16(r)
    return out


def _w_in_internal(wt):
    return jnp.concatenate([wt[0:6144], wt[6720:8768], wt[6144:6720], jnp.zeros((P_W - W_IN_COLS, 1024), wt.dtype)], axis=0)


def _w_in_external(d):
    return jnp.concatenate([d[0:6144], d[8192:8768], d[6144:8192]], axis=0)


def _rope_tables(positions):
    pos = positions[0].astype(F32)

    def cs(half):
        inv = ROPE_BASE ** (-jnp.arange(half, dtype=F32) / half)
        ang = pos[:, None] * inv
        return jnp.cos(ang), jnp.sin(ang)

    c, s = cs(64)
    c2, s2 = cs(32)
    return (jnp.concatenate([c, c], axis=1), jnp.concatenate([-s, s], axis=1),
            jnp.concatenate([c2, c2, c2, c2], axis=1), jnp.concatenate([-s2, s2, -s2, s2], axis=1))


def _local_step(x, p, positions, target, W, ln_g, ln_b, gn_g, qg, kvg):
    T = x.shape[0]
    tabs = _rope_tables(positions)
    rc = _ret_consts()
    lg = [ln_g[i:i + 1] for i in range(4)]
    lb = [ln_b[i:i + 1] for i in range(4)]
    xb = x.astype(BF16)
    pb = p.astype(BF16)
    w_in_t = _w_in_internal(W["w_in"])
    wuq = W["w_uq"].reshape(1536, LORA).reshape(HEADS, 192, LORA)
    wuq = jnp.concatenate([wuq[:, :128].reshape(1024, LORA), wuq[:, 128:].reshape(512, LORA)], axis=0)
    wukv = W["w_ukv"].reshape(2048, LORA).reshape(HEADS, 2, 128, LORA).transpose(1, 0, 2, 3).reshape(2048, LORA)
    wp_t = W["ple_w_proj"].reshape(1024, D_PLE)

    a1 = _mm(xb, W["ffn1_w_in"], tb=True, name="ffn1_in")
    hs1 = _swiglu_fwd(a1, T, "ffn1_act")
    f1 = _mm(hs1, W["ffn1_w_out"], name="ffn1_out")
    h1, h1b = _ln_fwd(x, f1, 0.5, lg[0], lb[0], T, "ln0")
    proj = _mm(h1b, w_in_t, tb=True, name="mixer_in")
    rq, rk, rv, qn, kvn, kpe = _mixer_prep_fwd(proj, tabs, qg, kvg, T)
    y, states = _ret_fwd(rq, rk, rv, rc, T)
    yr = _ret_out_fwd(y, proj, gn_g, T)
    y_ret = _mm(yr, W["w_ret_o"], name="ret_o")
    q = _mm(qn, wuq, tb=True, name="mla_uq")
    kv = _mm(kvn, wukv, tb=True, out_dtype=BF16, name="mla_ukv")
    qnope, qpe = _q_assemble_fwd(q, tabs, T)
    o, lse = _attn_fwd(qnope, qpe, kv, kpe, T)
    y_mla = _mm(o, W["w_mla_o"], name="mla_o")
    mix = _mix_fwd(proj, y_ret, y_mla, T)
    mixed = _mm(mix, W["w_out"], name="mixer_out")
    h2, h2b = _ln_fwd(h1, mixed, 1.0, lg[1], lb[1], T, "ln1")
    a2 = _mm(h2b, W["ffn2_w_in"], tb=True, name="ffn2_in")
    hs2 = _swiglu_fwd(a2, T, "ffn2_act")
    f2 = _mm(hs2, W["ffn2_w_out"], name="ffn2_out")
    h3, h3b = _ln_fwd(h2, f2, 0.5, lg[2], lb[2], T, "ln2")
    gl = _mm(h3b, W["ple_w_gate"], name="ple_gate")
    pp = _mm(pb, wp_t, tb=True, name="ple_proj")

    G = {}
    dh3_a, dgl, dpp, dg3, db3, loss = _head(h3, gl, pp, target, lg[3], lb[3], T)
    G["ple_w_gate"] = _mm(h3b, dgl, ta=True, name="d_ple_gate")
    G["ple_w_proj"] = _mm(dpp, pb, ta=True, name="d_ple_proj")
    dh3 = _mm(dgl, W["ple_w_gate"], tb=True, add=dh3_a, name="dh3")
    dh2_a, df2, dg2, db2 = _ln_bwd(h2, f2, 0.5, lg[2], lb[2], dh3, T, "ln2_bwd")
    G["ffn2_w_out"] = _mm(hs2, df2, ta=True, name="d_ffn2_out")
    dhs2 = _mm(df2, W["ffn2_w_out"], tb=True, name="dhs2")
    da2 = _swiglu_bwd(a2, dhs2, T, "ffn2_act_bwd")
    G["ffn2_w_in"] = _mm(da2, h2b, ta=True, name="d_ffn2_in")
    dh2 = _mm(da2, W["ffn2_w_in"], add=dh2_a, name="dh2")
    dh1_a, dmixed, dg1, db1 = _ln_bwd(h1, mixed, 1.0, lg[1], lb[1], dh2, T, "ln1_bwd")
    G["w_out"] = _mm(mix, dmixed, ta=True, name="d_mixer_out")
    dmix = _mm(dmixed, W["w_out"], tb=True, name="dmix")
    dgr, dgm, dy_ret, dy_mla = _mix_bwd(proj, y_ret, y_mla, dmix, T)
    G["w_mla_o"] = _mm(o, dy_mla, ta=True, name="d_mla_o")
    do = _mm(dy_mla, W["w_mla_o"], tb=True, name="do")
    dob, delta = _attn_delta(do, o, T)
    dqn_f, dqpe_f, dkn, dkpe_all, dv = _attn_bwd(qnope, qpe, kv, kpe, dob, lse, delta, T)
    dq_n, dq_r = _q_assemble_bwd(dqn_f, dqpe_f, tabs, T)
    g_uq = jnp.concatenate([_mm(dq_n, qn, ta=True, name="d_uq_nope"), _mm(dq_r, qn, ta=True, name="d_uq_rope")], axis=0)
    g_uq = jnp.concatenate([g_uq[:1024].reshape(HEADS, 128, LORA), g_uq[1024:].reshape(HEADS, 64, LORA)], axis=1)
    G["w_uq"] = g_uq.reshape(1536 * LORA // 1024, 1024)
    dqn = _mm(dq_r, wuq[1024:], add=_mm(dq_n, wuq[:1024], name="dqn_a"), name="dqn_b")
    g_ukv = jnp.stack([_mm(dkn, kvn, ta=True, name="d_ukv_k"), _mm(dv, kvn, ta=True, name="d_ukv_v")], axis=0)
    G["w_ukv"] = g_ukv.reshape(2, HEADS, 128, LORA).transpose(1, 0, 2, 3).reshape(2048 * LORA // 1024, 1024)
    dkvn = _mm(dv, wukv[1024:], add=_mm(dkn, wukv[:1024], name="dkvn_a"), name="dkvn_b")
    dcq, dckv, dkpe, dqg, dkvg = _rms_bwd(proj, dqn, dkvn, dkpe_all, tabs, qg, kvg, T)
    G["w_ret_o"] = _mm(yr, dy_ret, ta=True, name="d_ret_o")
    dyr = _mm(dy_ret, W["w_ret_o"], tb=True, name="dyr")
    dy, drg, dgn = _ret_out_bwd(y, proj, gn_g, dyr, T)
    drq, drk, drv = _ret_bwd(rq, rk, rv, states, dy, rc, tabs, T)
    dproj = jnp.concatenate([drq, drk, drv, drg, dgr, dgm, dcq, dckv, dkpe, jnp.zeros((T, P_W - P_KPE - 128), BF16)], axis=1)
    G["w_in"] = _w_in_external(_mm(dproj, h1b, ta=True, name="d_mixer_in"))
    dh1 = _mm(dproj, w_in_t, add=dh1_a, name="dh1")
    dx_a, df1, dg0, db0 = _ln_bwd(x, f1, 0.5, lg[0], lb[0], dh1, T, "ln0_bwd")
    G["ffn1_w_out"] = _mm(hs1, df1, ta=True, name="d_ffn1_out")
    dhs1 = _mm(df1, W["ffn1_w_out"], tb=True, name="dhs1")
    da1 = _swiglu_bwd(a1, dhs1, T, "ffn1_act_bwd")
    G["ffn1_w_in"] = _mm(da1, xb, ta=True, name="d_ffn1_in")
    grad_x = _mm(da1, W["ffn1_w_in"], add=dx_a, name="grad_x")

    small = dict(ln_g=jnp.concatenate([dg0, dg1, dg2, dg3], axis=0), ln_b=jnp.concatenate([db0, db1, db2, db3], axis=0),
                 ret_gn_g=dgn, q_norm_g=dqg, kv_norm_g=dkvg)
    return loss, grad_x, G, small


def kernel(x, p, positions, ln_g, ln_b, ffn1_w_in, ffn1_w_out, w_in, ret_gn_g, w_ret_o, q_norm_g, kv_norm_g, w_uq, w_ukv, w_mla_o, w_out, ffn2_w_in, ffn2_w_out, ple_w_gate, ple_w_proj, loss_target, m_ln_g, m_ln_b, m_ffn1_w_in, m_ffn1_w_out, m_w_in, m_ret_gn_g, m_w_ret_o, m_q_norm_g, m_kv_norm_g, m_w_uq, m_w_ukv, m_w_mla_o, m_w_out, m_ffn2_w_in, m_ffn2_w_out, m_ple_w_gate, m_ple_w_proj, v_ln_g, v_ln_b, v_ffn1_w_in, v_ffn1_w_out, v_w_in, v_ret_gn_g, v_w_ret_o, v_q_norm_g, v_kv_norm_g, v_w_uq, v_w_ukv, v_w_mla_o, v_w_out, v_ffn2_w_in, v_ffn2_w_out, v_ple_w_gate, v_ple_w_proj):
    names = ("ln_g", "ln_b", "ffn1_w_in", "ffn1_w_out", "w_in", "ret_gn_g", "w_ret_o", "q_norm_g", "kv_norm_g", "w_uq", "w_ukv",
             "w_mla_o", "w_out", "ffn2_w_in", "ffn2_w_out", "ple_w_gate", "ple_w_proj")
    ws = dict(zip(names, (ln_g, ln_b, ffn1_w_in, ffn1_w_out, w_in, ret_gn_g, w_ret_o, q_norm_g, kv_norm_g, w_uq, w_ukv, w_mla_o,
                          w_out, ffn2_w_in, ffn2_w_out, ple_w_gate, ple_w_proj)))
    ms = dict(zip(names, (m_ln_g, m_ln_b, m_ffn1_w_in, m_ffn1_w_out, m_w_in, m_ret_gn_g, m_w_ret_o, m_q_norm_g, m_kv_norm_g, m_w_uq,
                          m_w_ukv, m_w_mla_o, m_w_out, m_ffn2_w_in, m_ffn2_w_out, m_ple_w_gate, m_ple_w_proj)))
    vs = dict(zip(names, (v_ln_g, v_ln_b, v_ffn1_w_in, v_ffn1_w_out, v_w_in, v_ret_gn_g, v_w_ret_o, v_q_norm_g, v_kv_norm_g, v_w_uq,
                          v_w_ukv, v_w_mla_o, v_w_out, v_ffn2_w_in, v_ffn2_w_out, v_ple_w_gate, v_ple_w_proj)))

    parts = []
    for name, r in PACK:
        rows = _to_rows(name, ws[name])
        if _pad16(r) != r:
            rows = jnp.concatenate([rows, jnp.zeros((_pad16(r) - r, 1024), F32)], axis=0)
        parts.append(rows)
    wsh = jnp.concatenate(parts, axis=0).astype(BF16)
    ssh = jnp.concatenate([ln_g[0], ln_b[0]], axis=0)
    wall, sall = _all_gather(wsh, ssh)
    W = _unpack(wall)
    ln_full = sall.reshape(N_DEV, 2, 4, 128).transpose(1, 2, 0, 3).reshape(2, 4, 1024)

    loss_p, grad_x, G, small = _local_step(x[0], p[0, 0], positions, loss_target[0], W, ln_full[0], ln_full[1],
                                           ret_gn_g, q_norm_g, kv_norm_g)

    gparts = []
    for name, r in PACK:
        g = G[name].reshape(N_DEV, r, 1024)
        if _pad16(r) != r:
            g = jnp.concatenate([g, jnp.zeros((N_DEV, _pad16(r) - r, 1024), F32)], axis=1)
        gparts.append(g)
    gfull = jnp.concatenate(gparts, axis=1).astype(BF16)
    pad256 = lambda a: jnp.concatenate([a, jnp.zeros((1, 1024 - a.shape[1]), F32)], axis=1)
    gsmall = jnp.concatenate([small["ln_g"], small["ln_b"], small["ret_gn_g"].reshape(2, 1024), pad256(small["q_norm_g"]),
                              pad256(small["kv_norm_g"]), jnp.zeros((SMALL_ROWS - 12, 1024), F32)], axis=0)
    g4 = gfull.reshape(4, 2, PACK_ROWS, 1024)
    cvec = lax.axis_index("c").astype(jnp.int32).reshape(1)
    part = _sum_sibling(g4, _exchange_sibling(g4), cvec)
    recv, srecv = _exchange_chips(part, gsmall)
    gsh = _sum_slots(recv, "sum_grads")
    ssum = _sum_slots(srecv, "sum_small_grads")

    grads = {}
    off = 0
    for name, r in PACK:
        grads[name] = _from_rows(name, gsh[off:off + r], ws[name].shape)
        off += _pad16(r)
    me = 4 * lax.axis_index("x") + 2 * lax.axis_index("y") + lax.axis_index("c")
    grads["ln_g"] = lax.dynamic_slice(ssum[0:4], (0, me * 128), (4, 128)).reshape(1, 4, 128)
    grads["ln_b"] = lax.dynamic_slice(ssum[4:8], (0, me * 128), (4, 128)).reshape(1, 4, 128)
    grads["ret_gn_g"] = ssum[8:10].reshape(1, 2048)
    grads["q_norm_g"] = ssum[10:11, :256]
    grads["kv_norm_g"] = ssum[11:12, :256]

    delta, new_m, new_v = {}, {}, {}
    for name in names:
        delta[name], new_m[name], new_v[name] = _adamw(ws[name], grads[name], ms[name], vs[name], "adamw_" + name)

    loss = lax.psum(loss_p[0, 0], ("x", "y", "c"))
    return (loss, grad_x[None], *[grads[n] for n in names], *[delta[n] for n in names],
            *[new_m[n] for n in names], *[new_v[n] for n in names])
```

```python
import functools
import math

import jax
import jax.numpy as jnp
from jax import lax
from jax.experimental import pallas as pl
from jax.experimental.pallas import tpu as pltpu

F32 = jnp.float32
BF16 = jnp.bfloat16

N_DEV = 8
D = 1024
D_FF = 2816
D_PLE = 256
CHUNK = 64
HEADS = 8
RET_DK = 128
RET_DV = 256
MLA_NOPE = 128
MLA_ROPE = 64
MLA_DV = 128
LORA = 256
ROPE_BASE = 10000.0
EPS = 1e-5
ALPHA = 2.0 ** 0.25
RET_SCALE = RET_DK ** -0.5
MLA_SCALE = (MLA_NOPE + MLA_ROPE) ** -0.5
NEG = -1e30

ADAM_LR = 0.001
ADAM_B1 = 0.9
ADAM_B2 = 0.999
ADAM_EPS = 1e-08
ADAM_WD = 0.01
ADAM_STEP = 10

P_RQ, P_RK, P_RV, P_RG, P_GR, P_GM, P_CQ, P_CKV, P_KPE, P_W = 0, 1024, 2048, 4096, 6144, 7168, 8192, 8448, 8704, 8960
W_IN_COLS = 8768
RET_L = 256
ATT_TF = 1024
ATT_TB = 1024
LOG2E = math.log2(math.e)
Q_PRESCALE = MLA_SCALE * LOG2E

PACK = (("ffn1_w_in", 704), ("ffn1_w_out", 352), ("w_in", 1096), ("w_ret_o", 256), ("w_uq", 48), ("w_ukv", 64),
        ("w_mla_o", 128), ("w_out", 128), ("ffn2_w_in", 704), ("ffn2_w_out", 352), ("ple_w_gate", 128), ("ple_w_proj", 32))


def _pad16(r):
    return -(-r // 16) * 16


PACK_ROWS = sum(_pad16(r) for _, r in PACK)
SMALL_ROWS = 16


def _pcall(body, **kw):
    return pl.pallas_call(body, **kw)


def _pick(dim, prefs):
    for p in prefs:
        if dim % p == 0:
            return p
    return dim


def _sigmoid(x):
    return 1.0 / (1.0 + jnp.exp(-x))


def _silu(x):
    return x * _sigmoid(x)


def _ln(r, g, b):
    mu = jnp.mean(r, axis=-1, keepdims=True)
    var = jnp.mean(jnp.square(r - mu), axis=-1, keepdims=True)
    return (r - mu) * lax.rsqrt(var + EPS) * g + b


def _rms(x, g):
    return x * lax.rsqrt(jnp.mean(jnp.square(x), axis=-1, keepdims=True) + EPS) * g


def _dot(a, b, ca, cb):
    return lax.dot_general(a, b, (((ca,), (cb,)), ((), ())), preferred_element_type=F32)


def _accum(ref, val):
    @pl.when(pl.program_id(0) == 0)
    def _():
        ref[...] = jnp.zeros_like(ref)

    ref[...] += val


def _mm(a, b, *, ta=False, tb=False, add=None, out_dtype=F32, name, tm=None, tn=None, tk=None):
    if ta:
        K, M = a.shape
    else:
        M, K = a.shape
    if tb:
        N, K2 = b.shape
    else:
        K2, N = b.shape
    assert K == K2, (a.shape, b.shape, ta, tb)
    big = (1024, 1408, 1280, 768, 512, 256, 128)
    tm = tm or _pick(M, big)
    tn = tn or (N if N <= 1024 else _pick(N, big))
    tk = tk or (K if K <= 2816 else _pick(K, (2048, 1408, 1280, 1024, 512) if tm <= 1024 else (1024, 1408, 1280, 512)))
    nk = K // tk
    grid = (M // tm, N // tn, nk)
    a_spec = pl.BlockSpec((tk, tm), lambda i, j, k: (k, i)) if ta else pl.BlockSpec((tm, tk), lambda i, j, k: (i, k))
    b_spec = pl.BlockSpec((tn, tk), lambda i, j, k: (j, k)) if tb else pl.BlockSpec((tk, tn), lambda i, j, k: (k, j))
    o_spec = pl.BlockSpec((tm, tn), lambda i, j, k: (i, j))
    ca, cb = (0 if ta else 1), (1 if tb else 0)
    has_add = add is not None

    def body(*refs):
        a_ref, b_ref = refs[0], refs[1]
        add_ref = refs[2] if has_add else None
        o_ref = refs[3] if has_add else refs[2]

        def finish(r):
            if has_add:
                r = r + add_ref[...].astype(F32)
            o_ref[...] = r.astype(out_dtype)

        if nk == 1:
            finish(_dot(a_ref[...], b_ref[...], ca, cb))
            return
        acc_ref = refs[-1]
        k = pl.program_id(2)

        @pl.when(k == 0)
        def _():
            acc_ref[...] = jnp.zeros_like(acc_ref)

        acc_ref[...] += _dot(a_ref[...], b_ref[...], ca, cb)

        @pl.when(k == nk - 1)
        def _():
            finish(acc_ref[...])

    in_specs = [a_spec, b_spec] + ([o_spec] if has_add else [])
    args = (a, b) + ((add,) if has_add else ())
    return _pcall(
        body, grid=grid, in_specs=in_specs, out_specs=o_spec, out_shape=jax.ShapeDtypeStruct((M, N), out_dtype),
        scratch_shapes=[pltpu.VMEM((tm, tn), F32)] if nk > 1 else [], name=name,
        compiler_params=pltpu.CompilerParams(dimension_semantics=("parallel", "parallel", "arbitrary")),
    )(*args)


def _rows(body, T, tm, ins, outs, name, accs=()):
    in_specs, args = [], []
    for arr, w, cb in ins:
        if w is None:
            in_specs.append(pl.BlockSpec(arr.shape, lambda i, _n=arr.ndim: (0,) * _n))
        else:
            in_specs.append(pl.BlockSpec((tm, w), lambda i, _cb=cb: (i, _cb)))
        args.append(arr)
    out_specs = [pl.BlockSpec((tm, w), lambda i: (i, 0)) for w, _ in outs]
    out_shape = [jax.ShapeDtypeStruct((T, w), dt) for w, dt in outs]
    for r, w in accs:
        out_specs.append(pl.BlockSpec((r, w), lambda i: (0, 0)))
        out_shape.append(jax.ShapeDtypeStruct((r, w), F32))
    return _pcall(
        body, grid=(T // tm,), in_specs=in_specs, out_specs=out_specs, out_shape=out_shape, name=name,
        compiler_params=pltpu.CompilerParams(dimension_semantics=("arbitrary",)),
    )(*args)


def _swiglu_fwd(a, T, name):
    def body(g_ref, u_ref, o_ref):
        o_ref[...] = (_silu(g_ref[...]) * u_ref[...]).astype(BF16)

    return _rows(body, T, 256, [(a, D_FF, 0), (a, D_FF, 1)], [(D_FF, BF16)], name)[0]


def _swiglu_bwd(a, dhs, T, name):
    def body(g_ref, u_ref, d_ref, o_ref):
        _, vjp = jax.vjp(lambda g, u: _silu(g) * u, g_ref[...], u_ref[...])
        dg, du = vjp(d_ref[...])
        o_ref[:, :D_FF] = dg.astype(BF16)
        o_ref[:, D_FF:] = du.astype(BF16)

    return _rows(body, T, 128, [(a, D_FF, 0), (a, D_FF, 1), (dhs, D_FF, 0)], [(2 * D_FF, BF16)], name)[0]


def _ln_fwd(res, f, c, g, b, T, name):
    def body(r_ref, f_ref, g_ref, b_ref, h_ref, hb_ref):
        h = _ln(ALPHA * r_ref[...] + c * f_ref[...], g_ref[...], b_ref[...])
        h_ref[...] = h
        hb_ref[...] = h.astype(BF16)

    return _rows(body, T, 256, [(res, D, 0), (f, D, 0), (g, None, None), (b, None, None)], [(D, F32), (D, BF16)], name)


def _ln_bwd(res, f, c, g, b, dh, T, name):
    def body(r_ref, f_ref, g_ref, b_ref, d_ref, dr_ref, df_ref, dg_ref, db_ref):
        _, vjp = jax.vjp(lambda r, ff, gg, bb: _ln(ALPHA * r + c * ff, gg, bb), r_ref[...], f_ref[...], g_ref[...], b_ref[...])
        dr, df, dg, db = vjp(d_ref[...])
        dr_ref[...] = dr
        df_ref[...] = df.astype(BF16)
        _accum(dg_ref, dg)
        _accum(db_ref, db)

    return _rows(body, T, 256, [(res, D, 0), (f, D, 0), (g, None, None), (b, None, None), (dh, D, 0)],
                 [(D, F32), (D, BF16)], name, accs=[(1, D), (1, D)])


def _rope128(t, cos, sin_s):
    return t * cos + pltpu.roll(t, 64, 1) * sin_s


def _rope128_t(g, cos, sin_s):
    return g * cos - pltpu.roll(g, 64, 1) * sin_s


def _partner32(t):
    lane = lax.broadcasted_iota(jnp.int32, t.shape, 1)
    return jnp.where((lane & 32) == 0, pltpu.roll(t, 96, 1), pltpu.roll(t, 32, 1))


def _rope64(t, cos, sin_s):
    return t * cos + _partner32(t) * sin_s


def _rope64_t(g, cos, sin_s):
    return g * cos - _partner32(g) * sin_s


def _mixer_prep_fwd(proj, tabs, qg, kvg, T):
    cos128, sin128, cos64, sin64 = tabs

    def body(rq_ref, rk_ref, rv_ref, cq_ref, ckv_ref, kpe_ref, c1_ref, s1_ref, c2_ref, s2_ref, qg_ref, kvg_ref,
             oq_ref, ok_ref, ov_ref, oqn_ref, okvn_ref, okpe_ref):
        c1, s1 = c1_ref[...], s1_ref[...]
        for h in range(HEADS):
            sl = slice(h * RET_DK, (h + 1) * RET_DK)
            oq_ref[:, sl] = _rope128(rq_ref[:, sl], c1, s1).astype(BF16)
            ok_ref[:, sl] = (_rope128(rk_ref[:, sl], c1, s1) * RET_SCALE).astype(BF16)
        ov_ref[...] = rv_ref[...].astype(BF16)
        oqn_ref[...] = _rms(cq_ref[...], qg_ref[...]).astype(BF16)
        okvn_ref[...] = _rms(ckv_ref[...], kvg_ref[...]).astype(BF16)
        okpe_ref[...] = _rope64(kpe_ref[...], c2_ref[...], s2_ref[...]).astype(BF16)

    ins = [(proj, 1024, 0), (proj, 1024, 1), (proj, 2048, 1), (proj, 256, P_CQ // 256), (proj, 256, P_CKV // 256),
           (proj, 128, P_KPE // 128), (cos128, 128, 0), (sin128, 128, 0), (cos64, 128, 0), (sin64, 128, 0),
           (qg, None, None), (kvg, None, None)]
    outs = [(1024, BF16), (1024, BF16), (2048, BF16), (LORA, BF16), (LORA, BF16), (128, BF16)]
    return _rows(body, T, 256, ins, outs, "mixer_prep_fwd")


def _rms_bwd(proj, dqn, dkvn, dkpe_all, tabs, qg, kvg, T):
    _, _, cos64, sin64 = tabs

    def body(cq_ref, ckv_ref, dq_ref, dkv_ref, dk_ref, c2_ref, s2_ref, qg_ref, kvg_ref, ocq_ref, ockv_ref, okpe_ref,
             dqg_ref, dkvg_ref):
        _, vjp = jax.vjp(_rms, cq_ref[...], qg_ref[...])
        dx, dg = vjp(dq_ref[...])
        ocq_ref[...] = dx.astype(BF16)
        _accum(dqg_ref, dg)
        _, vjp = jax.vjp(_rms, ckv_ref[...], kvg_ref[...])
        dx, dg = vjp(dkv_ref[...])
        ockv_ref[...] = dx.astype(BF16)
        _accum(dkvg_ref, dg)
        g = dk_ref[:, 0:128]
        for h in range(1, HEADS):
            g = g + dk_ref[:, h * 128:(h + 1) * 128]
        lane = lax.broadcasted_iota(jnp.int32, g.shape, 1)
        g = jnp.where(lane < MLA_ROPE, g, 0.0)
        okpe_ref[...] = _rope64_t(g, c2_ref[...], s2_ref[...]).astype(BF16)

    ins = [(proj, 256, P_CQ // 256), (proj, 256, P_CKV // 256), (dqn, LORA, 0), (dkvn, LORA, 0), (dkpe_all, 1024, 0),
           (cos64, 128, 0), (sin64, 128, 0), (qg, None, None), (kvg, None, None)]
    return _rows(body, T, 256, ins, [(LORA, BF16), (LORA, BF16), (128, BF16)], "rms_bwd", accs=[(1, LORA), (1, LORA)])


def _gn_gate(y, rg, g):
    mu = jnp.mean(y, axis=-1, keepdims=True)
    var = jnp.mean(jnp.square(y - mu), axis=-1, keepdims=True)
    return _silu(rg) * ((y - mu) * lax.rsqrt(var + EPS) * g)


def _q_assemble_fwd(q, tabs, T):
    _, _, cos64, sin64 = tabs

    def body(q_ref, c_ref, s_ref, on_ref, op_ref):
        on_ref[...] = (q_ref[:, :1024] * Q_PRESCALE).astype(BF16)
        c, s = c_ref[...], s_ref[...]
        lane = lax.broadcasted_iota(jnp.int32, c.shape, 1)
        for j in range(HEADS // 2):
            r = _rope64(q_ref[:, 1024 + 128 * j:1024 + 128 * (j + 1)], c, s) * Q_PRESCALE
            op_ref[:, 256 * j:256 * j + 128] = jnp.where(lane < 64, r, 0.0).astype(BF16)
            op_ref[:, 256 * j + 128:256 * j + 256] = jnp.where(lane < 64, pltpu.roll(r, 64, 1), 0.0).astype(BF16)

    return _rows(body, T, 256, [(q, 1536, 0), (cos64, 128, 0), (sin64, 128, 0)], [(1024, BF16), (1024, BF16)], "q_assemble_fwd")


def _q_assemble_bwd(dqn, dqpe, tabs, T):
    _, _, cos64, sin64 = tabs

    def body(dn_ref, dp_ref, c_ref, s_ref, on_ref, op_ref):
        on_ref[...] = dn_ref[...].astype(BF16)
        c, s = c_ref[...], s_ref[...]
        lane = lax.broadcasted_iota(jnp.int32, c.shape, 1)
        for j in range(HEADS // 2):
            g = jnp.where(lane < 64, dp_ref[:, 256 * j:256 * j + 128], pltpu.roll(dp_ref[:, 256 * j + 128:256 * j + 256], 64, 1))
            op_ref[:, 128 * j:128 * (j + 1)] = _rope64_t(g, c, s).astype(BF16)

    return _rows(body, T, 256, [(dqn, 1024, 0), (dqpe, 1024, 0), (cos64, 128, 0), (sin64, 128, 0)],
                 [(1024, BF16), (512, BF16)], "q_assemble_bwd")


def _mix_fn(gr, gm, yr, ym):
    return _sigmoid(gr) * yr + _sigmoid(gm) * ym


def _mix_fwd(proj, y_ret, y_mla, T):
    def body(gr_ref, gm_ref, yr_ref, ym_ref, o_ref):
        o_ref[...] = _mix_fn(gr_ref[...], gm_ref[...], yr_ref[...], ym_ref[...]).astype(BF16)

    return _rows(body, T, 256, [(proj, 1024, P_GR // 1024), (proj, 1024, P_GM // 1024), (y_ret, D, 0), (y_mla, D, 0)],
                 [(D, BF16)], "mix_fwd")[0]


def _mix_bwd(proj, y_ret, y_mla, dmix, T):
    def body(gr_ref, gm_ref, yr_ref, ym_ref, d_ref, dgr_ref, dgm_ref, dyr_ref, dym_ref):
        _, vjp = jax.vjp(_mix_fn, gr_ref[...], gm_ref[...], yr_ref[...], ym_ref[...])
        dgr, dgm, dyr, dym = vjp(d_ref[...])
        dgr_ref[...] = dgr.astype(BF16)
        dgm_ref[...] = dgm.astype(BF16)
        dyr_ref[...] = dyr.astype(BF16)
        dym_ref[...] = dym.astype(BF16)

    return _rows(body, T, 256, [(proj, 1024, P_GR // 1024), (proj, 1024, P_GM // 1024), (y_ret, D, 0), (y_mla, D, 0), (dmix, D, 0)],
                 [(D, BF16)] * 4, "mix_bwd")


def _head(h3, gl, pp, tgt, g, b, T):
    def body(h_ref, gl_ref, pp_ref, t_ref, g_ref, b_ref, dh_ref, dgl_ref, dpp_ref, dg_ref, db_ref, loss_ref):
        def f(h, gg, p, lg, lb):
            return _ln(ALPHA * h + _sigmoid(gg) * p, lg, lb)

        h4, vjp = jax.vjp(f, h_ref[...], gl_ref[...], pp_ref[...], g_ref[...], b_ref[...])
        err = h4 - t_ref[...]
        dh, dgl, dpp, dg, db = vjp(err * (1.0 / D))
        dh_ref[...] = dh
        dgl_ref[...] = dgl.astype(BF16)
        dpp_ref[...] = dpp.astype(BF16)
        _accum(dg_ref, dg)
        _accum(db_ref, db)
        part = 0.5 * jnp.sum(jnp.mean(jnp.square(err), axis=-1, keepdims=True), axis=0, keepdims=True)
        _accum(loss_ref, jnp.broadcast_to(part, loss_ref.shape))

    return _rows(body, T, 256, [(h3, D, 0), (gl, D, 0), (pp, D, 0), (tgt, D, 0), (g, None, None), (b, None, None)],
                 [(D, F32), (D, BF16), (D, BF16)], "head", accs=[(1, D), (1, D), (8, 128)])


def _attn_delta(do, o, T):
    def body(d_ref, o_ref, db_ref, dl_ref):
        for h in range(HEADS):
            sl = slice(h * MLA_DV, (h + 1) * MLA_DV)
            d = d_ref[:, sl]
            dl = jnp.sum(d * o_ref[:, sl].astype(F32), axis=-1, keepdims=True)
            dl_ref[:, sl] = jnp.broadcast_to(dl, d.shape)
        db_ref[...] = d_ref[...].astype(BF16)

    return _rows(body, T, 256, [(do, D, 0), (o, D, 0)], [(D, BF16), (D, F32)], "attn_delta")


def _ret_consts():
    L = RET_L
    lg = jnp.log(1.0 - 2.0 ** (-5.0 - jnp.arange(HEADS, dtype=F32)))[:, None, None]
    idx = jnp.arange(L, dtype=F32)
    ch = jnp.arange(L) // CHUNK
    dist = idx[:, None] - idx[None, :]
    same = (ch[:, None] == ch[None, :])[None]
    earlier = (ch[None, :] < ch[:, None])[None]
    dm = jnp.where(same, jnp.exp(lg * jnp.abs(dist)[None]), jnp.where(earlier, jnp.exp(lg * dist[None]), 0.0))
    xi = jnp.broadcast_to(jnp.exp(lg * (idx + 1.0)[None, :, None]), (HEADS, L, 128))
    zeta = jnp.broadcast_to(jnp.exp(lg * (L - 1.0 - idx)[None, :, None]), (HEADS, L, 128))
    gl = jnp.broadcast_to(jnp.exp(lg * float(L)), (HEADS, 8, 128))
    return dm.astype(F32), xi.astype(F32), zeta.astype(F32), gl.astype(F32)


def _whole(arr):
    return pl.BlockSpec(arr.shape, lambda n, _nd=arr.ndim: (0,) * _nd)


def _ret_fwd(q, k, v, proj, gn_g, consts, T):
    dm, xi, zeta, gl = consts
    L = RET_L
    n_sc = T // L

    def body(q_ref, k_ref, v_ref, rg_ref, g_ref, dm_ref, xi_ref, ze_ref, gl_ref, y_ref, yr_ref, s_ref, st_ref):
        @pl.when(pl.program_id(0) == 0)
        def _():
            st_ref[...] = jnp.zeros_like(st_ref)

        for h in range(HEADS):
            ks, vs = slice(h * RET_DK, (h + 1) * RET_DK), slice(h * RET_DV, (h + 1) * RET_DV)
            qq, kk, vv = q_ref[:, ks], k_ref[:, ks], v_ref[:, vs]
            st = st_ref[h]
            s_ref[h, 0] = st
            p = (_dot(qq, kk, 1, 1) * dm_ref[h]).astype(BF16)
            cross = _dot(qq, st.astype(BF16), 1, 0)
            xi_c = jnp.concatenate([xi_ref[h], xi_ref[h]], axis=1)
            y = _dot(p, vv, 1, 0) + cross * xi_c
            y_ref[:, vs] = y
            yr_ref[:, vs] = _gn_gate(y, rg_ref[:, vs], g_ref[:, vs]).astype(BF16)
            kz = (kk.astype(F32) * ze_ref[h]).astype(BF16)
            gl2 = jnp.concatenate([gl_ref[h, 0:1, :], gl_ref[h, 0:1, :]], axis=1)
            st_ref[h] = st * gl2 + _dot(kz, vv, 0, 0)

    return _pcall(
        body, grid=(n_sc,),
        in_specs=[pl.BlockSpec((L, 1024), lambda n: (n, 0)), pl.BlockSpec((L, 1024), lambda n: (n, 0)),
                  pl.BlockSpec((L, 2048), lambda n: (n, 0)), pl.BlockSpec((L, 2048), lambda n: (n, P_RG // 2048)),
                  _whole(gn_g), _whole(dm), _whole(xi), _whole(zeta), _whole(gl)],
        out_specs=[pl.BlockSpec((L, 2048), lambda n: (n, 0)), pl.BlockSpec((L, 2048), lambda n: (n, 0)),
                   pl.BlockSpec((HEADS, 1, 128, 256), lambda n: (0, n, 0, 0))],
        out_shape=[jax.ShapeDtypeStruct((T, HEADS * RET_DV), F32), jax.ShapeDtypeStruct((T, HEADS * RET_DV), BF16),
                   jax.ShapeDtypeStruct((HEADS, n_sc, 128, 256), F32)],
        scratch_shapes=[pltpu.VMEM((HEADS, 128, 256), F32)], name="ret_fwd",
        compiler_params=pltpu.CompilerParams(dimension_semantics=("arbitrary",)),
    )(q, k, v, proj, gn_g, dm, xi, zeta, gl)


def _ret_bwd(q, k, v, y, proj, gn_g, states, dyr, consts, tabs, T):
    dm, xi, zeta, gl = consts
    cos128, sin128, _, _ = tabs
    L = RET_L
    n_sc = T // L

    def body(q_ref, k_ref, v_ref, y_ref, rg_ref, g_ref, d_ref, s_ref, dm_ref, xi_ref, ze_ref, gl_ref, c_ref, sn_ref,
             dq_ref, dk_ref, dv_ref, drg_ref, dg_ref, gs_ref):
        @pl.when(pl.program_id(0) == 0)
        def _():
            gs_ref[...] = jnp.zeros_like(gs_ref)

        c, sn = c_ref[...], sn_ref[...]
        dgs = []
        for h in range(HEADS):
            ks, vs = slice(h * RET_DK, (h + 1) * RET_DK), slice(h * RET_DV, (h + 1) * RET_DV)
            _, vjp = jax.vjp(_gn_gate, y_ref[:, vs], rg_ref[:, vs], g_ref[:, vs])
            dy, drg, dg = vjp(d_ref[:, vs])
            drg_ref[:, vs] = drg.astype(BF16)
            dgs.append(dg)
            qq, kk, vv, dyy = q_ref[:, ks], k_ref[:, ks], v_ref[:, vs], dy.astype(BF16)
            dmm = dm_ref[h]
            gb = gs_ref[h].astype(BF16)
            sb = s_ref[h, 0].astype(BF16)
            xi_c = jnp.concatenate([xi_ref[h], xi_ref[h]], axis=1)
            pb = (_dot(qq, kk, 1, 1) * dmm).astype(BF16)
            kz = (kk.astype(F32) * ze_ref[h]).astype(BF16)
            dv_ref[:, vs] = (_dot(pb, dyy, 0, 0) + _dot(kz, gb, 1, 0)).astype(BF16)
            da = (_dot(dyy, vv, 1, 1) * dmm).astype(BF16)
            dyx = (dyy.astype(F32) * xi_c).astype(BF16)
            dq = _dot(da, kk, 1, 0) + _dot(dyx, sb, 1, 1)
            dk = _dot(da, qq, 0, 0) + _dot(vv, gb, 1, 1) * ze_ref[h]
            dq_ref[:, ks] = _rope128_t(dq, c, sn).astype(BF16)
            dk_ref[:, ks] = (_rope128_t(dk, c, sn) * RET_SCALE).astype(BF16)
            gl2 = jnp.concatenate([gl_ref[h, 0:1, :], gl_ref[h, 0:1, :]], axis=1)
            gs_ref[h] = gs_ref[h] * gl2 + _dot(qq, dyx, 0, 0)
        _accum(dg_ref, jnp.concatenate(dgs, axis=1))

    rev = lambda n: n_sc - 1 - n
    return _pcall(
        body, grid=(n_sc,),
        in_specs=[pl.BlockSpec((L, 1024), lambda n: (rev(n), 0)), pl.BlockSpec((L, 1024), lambda n: (rev(n), 0)),
                  pl.BlockSpec((L, 2048), lambda n: (rev(n), 0)), pl.BlockSpec((L, 2048), lambda n: (rev(n), 0)),
                  pl.BlockSpec((L, 2048), lambda n: (rev(n), P_RG // 2048)), _whole(gn_g),
                  pl.BlockSpec((L, 2048), lambda n: (rev(n), 0)),
                  pl.BlockSpec((HEADS, 1, 128, 256), lambda n: (0, rev(n), 0, 0)),
                  _whole(dm), _whole(xi), _whole(zeta), _whole(gl),
                  pl.BlockSpec((L, 128), lambda n: (rev(n), 0)), pl.BlockSpec((L, 128), lambda n: (rev(n), 0))],
        out_specs=[pl.BlockSpec((L, 1024), lambda n: (rev(n), 0)), pl.BlockSpec((L, 1024), lambda n: (rev(n), 0)),
                   pl.BlockSpec((L, 2048), lambda n: (rev(n), 0)), pl.BlockSpec((L, 2048), lambda n: (rev(n), 0)),
                   pl.BlockSpec((1, 2048), lambda n: (0, 0))],
        out_shape=[jax.ShapeDtypeStruct((T, 1024), BF16), jax.ShapeDtypeStruct((T, 1024), BF16), jax.ShapeDtypeStruct((T, 2048), BF16),
                   jax.ShapeDtypeStruct((T, 2048), BF16), jax.ShapeDtypeStruct((1, 2048), F32)],
        scratch_shapes=[pltpu.VMEM((HEADS, 128, 256), F32)], name="ret_bwd",
        compiler_params=pltpu.CompilerParams(dimension_semantics=("arbitrary",)),
    )(q, k, v, y, proj, gn_g, dyr, states, dm, xi, zeta, gl, cos128, sin128)


def _diag_mask(t):
    row = lax.broadcasted_iota(jnp.int32, (t, t), 0)
    col = lax.broadcasted_iota(jnp.int32, (t, t), 1)
    return lax.shift_right_logical(col, 6) <= lax.shift_right_logical(row, 6)


def _tri_steps(nb, by_key):
    if by_key:
        pairs = [(i, j) for j in range(nb) for i in range(j, nb)]
    else:
        pairs = [(i, j) for i in range(nb) for j in range(i + 1)]
    return jnp.array([a for a, _ in pairs], jnp.int32), jnp.array([b for _, b in pairs], jnp.int32)


def _attn_fwd(qn, qpe, kv, kpe, T):
    t = min(ATT_TF, T)
    nb = T // t
    ii, jj = _tri_steps(nb, by_key=False)

    def body(ii_ref, jj_ref, qn_ref, qp_ref, kn_ref, kp_ref, v_ref, o_ref, lse_ref, m_sc, l_sc, acc_sc):
        st = pl.program_id(1)
        i, j = ii_ref[st], jj_ref[st]

        @pl.when(j == 0)
        def _():
            m_sc[...] = jnp.full_like(m_sc, NEG)
            l_sc[...] = jnp.zeros_like(l_sc)
            acc_sc[...] = jnp.zeros_like(acc_sc)

        def update(diag):
            q = jnp.concatenate([qn_ref[...], qp_ref[...]], axis=1)
            k = jnp.concatenate([kn_ref[...], kp_ref[...]], axis=1)
            s = _dot(q, k, 1, 1)
            if diag:
                s = jnp.where(_diag_mask(t), s, NEG)
            m_prev = m_sc[...]
            m_new = jnp.maximum(m_prev, jnp.max(s, axis=1, keepdims=True))
            a = jnp.exp2(m_prev - m_new)
            p = jnp.exp2(s - m_new[:, 0:1])
            l_sc[...] = a * l_sc[...] + jnp.sum(p, axis=1, keepdims=True)
            acc_sc[...] = a * acc_sc[...] + _dot(p.astype(BF16), v_ref[...], 1, 0)
            m_sc[...] = m_new

        @pl.when(j < i)
        def _():
            update(False)

        @pl.when(j == i)
        def _():
            update(True)
            o_ref[...] = (acc_sc[...] / l_sc[...]).astype(BF16)
            lse_ref[...] = m_sc[...] + jnp.log2(l_sc[...])

    qs = pl.BlockSpec((t, 128), lambda h, s, ii, jj: (ii[s], h))
    grid_spec = pltpu.PrefetchScalarGridSpec(
        num_scalar_prefetch=2, grid=(HEADS, int(ii.shape[0])),
        in_specs=[qs, qs, pl.BlockSpec((t, 128), lambda h, s, ii, jj: (jj[s], h)), pl.BlockSpec((t, 128), lambda h, s, ii, jj: (jj[s], 0)),
                  pl.BlockSpec((t, 128), lambda h, s, ii, jj: (jj[s], HEADS + h))],
        out_specs=[qs, qs],
        scratch_shapes=[pltpu.VMEM((t, 128), F32), pltpu.VMEM((t, 128), F32), pltpu.VMEM((t, 128), F32)])
    return _pcall(
        body, grid_spec=grid_spec, out_shape=[jax.ShapeDtypeStruct((T, D), BF16), jax.ShapeDtypeStruct((T, D), F32)], name="attn_fwd",
        compiler_params=pltpu.CompilerParams(dimension_semantics=("arbitrary", "arbitrary")),
    )(ii, jj, qn, qpe, kv, kpe, kv)


def _attn_bwd(qn, qpe, kv, kpe, do, lse, delta, T):
    t = min(ATT_TB, T)
    nb = T // t
    ii, jj = _tri_steps(nb, by_key=True)

    def body(ii_ref, jj_ref, qn_ref, qp_ref, kn_ref, kp_ref, v_ref, do_ref, lse_ref, dl_ref,
             dqn_ref, dqp_ref, dkn_ref, dkp_ref, dv_ref, dk_sc, dv_sc):
        st = pl.program_id(1)
        i, j = ii_ref[st], jj_ref[st]

        @pl.when(st == 0)
        def _():
            dqn_ref[...] = jnp.zeros_like(dqn_ref)
            dqp_ref[...] = jnp.zeros_like(dqp_ref)

        @pl.when(i == j)
        def _():
            dk_sc[...] = jnp.zeros_like(dk_sc)
            dv_sc[...] = jnp.zeros_like(dv_sc)

        def update(diag):
            q = jnp.concatenate([qn_ref[...], qp_ref[...]], axis=1)
            k = jnp.concatenate([kn_ref[...], kp_ref[...]], axis=1)
            dob = do_ref[...]
            s = _dot(q, k, 1, 1)
            if diag:
                s = jnp.where(_diag_mask(t), s, NEG)
            p = jnp.exp2(s - lse_ref[:, 0:1])
            dv_sc[...] += _dot(p.astype(BF16), dob, 0, 0)
            dp = _dot(dob, v_ref[...], 1, 1)
            ds = (p * (dp - dl_ref[:, 0:1])).astype(BF16)
            dk_sc[...] += _dot(ds, q, 0, 0)
            dq = _dot(ds, k, 1, 0) * MLA_SCALE
            rows = pl.ds(pl.multiple_of(i * t, t), t)
            dqn_ref[rows, :] += dq[:, :128]
            dqp_ref[rows, :] += dq[:, 128:]

        @pl.when(i > j)
        def _():
            update(False)

        @pl.when(i == j)
        def _():
            update(True)

        @pl.when(i == nb - 1)
        def _():
            dkn_ref[...] = (dk_sc[:, :128] * (1.0 / LOG2E)).astype(BF16)
            dkp_ref[...] = dk_sc[:, 128:] * (1.0 / LOG2E)
            dv_ref[...] = dv_sc[...].astype(BF16)

    qs = pl.BlockSpec((t, 128), lambda h, s, ii, jj: (ii[s], h))
    ks = pl.BlockSpec((t, 128), lambda h, s, ii, jj: (jj[s], h))
    hs = pl.BlockSpec((T, 128), lambda h, s, ii, jj: (0, h))
    grid_spec = pltpu.PrefetchScalarGridSpec(
        num_scalar_prefetch=2, grid=(HEADS, int(ii.shape[0])),
        in_specs=[qs, qs, ks, pl.BlockSpec((t, 128), lambda h, s, ii, jj: (jj[s], 0)),
                  pl.BlockSpec((t, 128), lambda h, s, ii, jj: (jj[s], HEADS + h)), qs, qs, qs],
        out_specs=[hs, hs, ks, ks, ks],
        scratch_shapes=[pltpu.VMEM((t, 256), F32), pltpu.VMEM((t, 128), F32)])
    return _pcall(
        body, grid_spec=grid_spec,
        out_shape=[jax.ShapeDtypeStruct((T, D), F32), jax.ShapeDtypeStruct((T, D), F32), jax.ShapeDtypeStruct((T, D), BF16),
                   jax.ShapeDtypeStruct((T, D), F32), jax.ShapeDtypeStruct((T, D), BF16)], name="attn_bwd",
        compiler_params=pltpu.CompilerParams(dimension_semantics=("arbitrary", "arbitrary")),
    )(ii, jj, qn, qpe, kv, kpe, kv, do, lse, delta)


def _mesh_pos():
    x, y, c = lax.axis_index("x"), lax.axis_index("y"), lax.axis_index("c")
    return x, y, c, 4 * x + 2 * y + c


def _peer(x, y, c, k):
    px, py, pc = (x + ((k >> 2) & 1)) % 2, (y + ((k >> 1) & 1)) % 2, (c + (k & 1)) % 2
    return (px, py, pc), 4 * px + 2 * py + pc


_ANY = pl.BlockSpec(memory_space=pl.ANY)


def _rcopy(src, dst, send_sems, recv_sems, k, to):
    return pltpu.make_async_remote_copy(src_ref=src, dst_ref=dst, send_sem=send_sems.at[k], recv_sem=recv_sems.at[k],
                                        device_id=to, device_id_type=pl.DeviceIdType.MESH)


def _all_gather(wsh, ssh):
    def body(w_ref, s_ref, wall_ref, sall_ref, send_sems, recv_sems, loc_sems):
        x, y, c, me = _mesh_pos()
        sib = (x, y, 1 - c)
        chips = [(1 - x, y), (x, 1 - y), (1 - x, 1 - y)]
        slot = lambda px, py, pc: 4 * px + 2 * py + pc
        loc = [pltpu.make_async_copy(w_ref, wall_ref.at[me], loc_sems.at[0]),
               pltpu.make_async_copy(s_ref, sall_ref.at[me], loc_sems.at[1])]
        for cp in loc:
            cp.start()
        sends, fwd_waits = [], []
        for n, (src, dst) in enumerate(((w_ref, wall_ref), (s_ref, sall_ref))):
            o = 7 * n
            sends.append(_rcopy(src, dst.at[me], send_sems, recv_sems, o, sib))
            for j, chip in enumerate(chips):
                sends.append(_rcopy(src, dst.at[me], send_sems, recv_sems, o + 1 + j, (*chip, c)))
        for cp in sends:
            cp.start()
        for n, (src, dst) in enumerate(((w_ref, wall_ref), (s_ref, sall_ref))):
            o = 7 * n
            for j, chip in enumerate(chips):
                got = dst.at[slot(*chip, c)]
                _rcopy(src, got, send_sems, recv_sems, o + 1 + j, sib).wait_recv()
                fw = _rcopy(got, got, send_sems, recv_sems, o + 4 + j, sib)
                fw.start()
                sends.append(fw)
            fwd_waits.append(_rcopy(src, dst.at[slot(x, y, 1 - c)], send_sems, recv_sems, o, sib))
            for j, chip in enumerate(chips):
                fwd_waits.append(_rcopy(src, dst.at[slot(*chip, 1 - c)], send_sems, recv_sems, o + 4 + j, sib))
        for cp in fwd_waits:
            cp.wait_recv()
        for cp in sends:
            cp.wait_send()
        for cp in loc:
            cp.wait()

    return _pcall(
        body, in_specs=[_ANY, _ANY], out_specs=[_ANY, _ANY],
        out_shape=[jax.ShapeDtypeStruct((N_DEV,) + wsh.shape, wsh.dtype), jax.ShapeDtypeStruct((N_DEV,) + ssh.shape, ssh.dtype)],
        scratch_shapes=[pltpu.SemaphoreType.DMA((14,)), pltpu.SemaphoreType.DMA((14,)), pltpu.SemaphoreType.DMA((2,))],
        name="all_gather_weights",
    )(wsh, ssh)


def _exchange_sibling(g4):
    def body(g_ref, r_ref, send_sems, recv_sems):
        x, y, c, _ = _mesh_pos()
        sib = (x, y, 1 - c)
        cps = [_rcopy(g_ref.at[q, 1 - c], r_ref.at[q], send_sems, recv_sems, q, sib) for q in range(4)]
        for cp in cps:
            cp.start()
        for cp in cps:
            cp.wait()

    n, _, R, C = g4.shape
    return _pcall(
        body, in_specs=[_ANY], out_specs=_ANY, out_shape=jax.ShapeDtypeStruct((n, R, C), g4.dtype),
        scratch_shapes=[pltpu.SemaphoreType.DMA((4,)), pltpu.SemaphoreType.DMA((4,))], name="exchange_sibling",
    )(g4)


def _sum_sibling(g4, rsib, cvec):
    n, _, R, C = g4.shape
    tr = _pick(R, (400, 256, 128, 16, 8))

    def body(c_ref, g_ref, r_ref, o_ref):
        o_ref[...] = (g_ref[...].astype(F32) + r_ref[...].astype(F32)).astype(o_ref.dtype)

    grid_spec = pltpu.PrefetchScalarGridSpec(
        num_scalar_prefetch=1, grid=(n, R // tr),
        in_specs=[pl.BlockSpec((None, None, tr, C), lambda q, i, cr: (q, cr[0], i, 0)), pl.BlockSpec((None, tr, C), lambda q, i, cr: (q, i, 0))],
        out_specs=pl.BlockSpec((None, tr, C), lambda q, i, cr: (q, i, 0)))
    return _pcall(body, grid_spec=grid_spec, out_shape=jax.ShapeDtypeStruct((n, R, C), g4.dtype), name="sum_sibling")(cvec, g4, rsib)


def _exchange_chips(part, gsmall):
    def body(p_ref, s_ref, recv_ref, srecv_ref, send_sems, recv_sems, loc_sems):
        x, y, c, me = _mesh_pos()
        myq = 2 * x + y
        loc = [pltpu.make_async_copy(p_ref.at[myq], recv_ref.at[myq], loc_sems.at[0]),
               pltpu.make_async_copy(s_ref, srecv_ref.at[me], loc_sems.at[1])]
        for cp in loc:
            cp.start()
        sends, recvs = [], []
        for j, (px, py) in enumerate([(1 - x, y), (x, 1 - y), (1 - x, 1 - y)]):
            q = 2 * px + py
            sends.append(_rcopy(p_ref.at[q], recv_ref.at[myq], send_sems, recv_sems, j, (px, py, c)))
            recvs.append(_rcopy(p_ref.at[q], recv_ref.at[q], send_sems, recv_sems, j, (px, py, c)))
        for k in range(1, N_DEV):
            to, pidx = _peer(x, y, c, k)
            sends.append(_rcopy(s_ref, srecv_ref.at[me], send_sems, recv_sems, 2 + k, to))
            recvs.append(_rcopy(s_ref, srecv_ref.at[pidx], send_sems, recv_sems, 2 + k, to))
        for cp in sends:
            cp.start()
        for cp in recvs:
            cp.wait_recv()
        for cp in sends:
            cp.wait_send()
        for cp in loc:
            cp.wait()

    return _pcall(
        body, in_specs=[_ANY, _ANY], out_specs=[_ANY, _ANY],
        out_shape=[jax.ShapeDtypeStruct(part.shape, part.dtype), jax.ShapeDtypeStruct((N_DEV,) + gsmall.shape, gsmall.dtype)],
        scratch_shapes=[pltpu.SemaphoreType.DMA((10,)), pltpu.SemaphoreType.DMA((10,)), pltpu.SemaphoreType.DMA((2,))],
        name="exchange_chips",
    )(part, gsmall)


def _sum_slots(recv, name):
    n, R, C = recv.shape
    tr = _pick(R, (400, 256, 128, 16, 8))

    def body(r_ref, o_ref):
        acc = r_ref[0].astype(F32)
        for s in range(1, n):
            acc = acc + r_ref[s].astype(F32)
        o_ref[...] = acc

    return _pcall(body, grid=(R // tr,), in_specs=[pl.BlockSpec((n, tr, C), lambda i: (0, i, 0))],
                  out_specs=pl.BlockSpec((tr, C), lambda i: (i, 0)), out_shape=jax.ShapeDtypeStruct((R, C), F32), name=name)(recv)


def _adamw(w, g, m, v, name):
    shape = w.shape
    w2, g2, m2, v2 = (a.reshape(-1, shape[-1]) for a in (w, g, m, v))
    R, C = w2.shape
    tr = _pick(R, (256, 128, 64, 32, 16, 8)) if R > 256 else R

    def body(w_ref, g_ref, m_ref, v_ref, d_ref, nm_ref, nv_ref):
        gg = g_ref[...]
        nm = ADAM_B1 * m_ref[...] + (1.0 - ADAM_B1) * gg
        nv = ADAM_B2 * v_ref[...] + (1.0 - ADAM_B2) * jnp.square(gg)
        m_hat = nm / (1.0 - ADAM_B1 ** ADAM_STEP)
        v_hat = nv / (1.0 - ADAM_B2 ** ADAM_STEP)
        d_ref[...] = -ADAM_LR * (m_hat / (jnp.sqrt(v_hat) + ADAM_EPS) + ADAM_WD * w_ref[...])
        nm_ref[...] = nm
        nv_ref[...] = nv

    spec = pl.BlockSpec((tr, C), lambda i: (i, 0))
    d, nm, nv = _pcall(body, grid=(R // tr,), in_specs=[spec] * 4, out_specs=[spec] * 3,
                       out_shape=[jax.ShapeDtypeStruct((R, C), F32)] * 3, name=name)(w2, g2, m2, v2)
    return d.reshape(shape), nm.reshape(shape), nv.reshape(shape)


def _to_rows(name, w):
    w = w[0]
    if name in ("ffn1_w_in", "ffn2_w_in", "w_in"):
        return w.T
    if name in ("w_uq", "w_ukv", "ple_w_proj"):
        return w.T.reshape(-1, 1024)
    return w


def _from_rows(name, g, shape):
    if name in ("ffn1_w_in", "ffn2_w_in", "w_in"):
        return g.T.reshape(shape)
    if name in ("w_uq", "w_ukv", "ple_w_proj"):
        return g.reshape(-1, shape[1]).T.reshape(shape)
    return g.reshape(shape)


def _unpack(wall):
    out, off = {}, 0
    for name, r in PACK:
        out[name] = wall[:, off:off + r, :].reshape(N_DEV * r, 1024)
        off += _pad16(r)
    return out


def _w_in_internal(wt):
    return jnp.concatenate([wt[0:6144], wt[6720:8768], wt[6144:6720], jnp.zeros((P_W - W_IN_COLS, 1024), wt.dtype)], axis=0)


def _w_in_external(d):
    return jnp.concatenate([d[0:6144], d[8192:8768], d[6144:8192]], axis=0)


def _rope_tables(positions):
    pos = positions[0].astype(F32)

    def cs(half):
        inv = ROPE_BASE ** (-jnp.arange(half, dtype=F32) / half)
        ang = pos[:, None] * inv
        return jnp.cos(ang), jnp.sin(ang)

    c, s = cs(64)
    c2, s2 = cs(32)
    return (jnp.concatenate([c, c], axis=1), jnp.concatenate([-s, s], axis=1),
            jnp.concatenate([c2, c2, c2, c2], axis=1), jnp.concatenate([-s2, s2, -s2, s2], axis=1))


def _local_step(x, p, positions, target, W, ln_g, ln_b, gn_g, qg, kvg):
    T = x.shape[0]
    tabs = _rope_tables(positions)
    rc = _ret_consts()
    lg = [ln_g[i:i + 1] for i in range(4)]
    lb = [ln_b[i:i + 1] for i in range(4)]
    xb = x.astype(BF16)
    pb = p.astype(BF16)
    w_in_t = _w_in_internal(W["w_in"])
    wuq = W["w_uq"].reshape(1536, LORA).reshape(HEADS, 192, LORA)
    wuq = jnp.concatenate([wuq[:, :128].reshape(1024, LORA), wuq[:, 128:].reshape(512, LORA)], axis=0)
    wukv = W["w_ukv"].reshape(2048, LORA).reshape(HEADS, 2, 128, LORA).transpose(1, 0, 2, 3).reshape(2048, LORA)
    wp_t = W["ple_w_proj"].reshape(1024, D_PLE)

    a1 = _mm(xb, W["ffn1_w_in"], tb=True, name="ffn1_in")
    hs1 = _swiglu_fwd(a1, T, "ffn1_act")
    f1 = _mm(hs1, W["ffn1_w_out"], name="ffn1_out")
    h1, h1b = _ln_fwd(x, f1, 0.5, lg[0], lb[0], T, "ln0")
    proj = _mm(h1b, w_in_t, tb=True, name="mixer_in")
    rq, rk, rv, qn, kvn, kpe = _mixer_prep_fwd(proj, tabs, qg, kvg, T)
    y, yr, states = _ret_fwd(rq, rk, rv, proj, gn_g, rc, T)
    y_ret = _mm(yr, W["w_ret_o"], name="ret_o")
    q = _mm(qn, wuq, tb=True, name="mla_uq")
    kv = _mm(kvn, wukv, tb=True, out_dtype=BF16, name="mla_ukv")
    qnope, qpe = _q_assemble_fwd(q, tabs, T)
    o, lse = _attn_fwd(qnope, qpe, kv, kpe, T)
    y_mla = _mm(o, W["w_mla_o"], name="mla_o")
    mix = _mix_fwd(proj, y_ret, y_mla, T)
    mixed = _mm(mix, W["w_out"], name="mixer_out")
    h2, h2b = _ln_fwd(h1, mixed, 1.0, lg[1], lb[1], T, "ln1")
    a2 = _mm(h2b, W["ffn2_w_in"], tb=True, name="ffn2_in")
    hs2 = _swiglu_fwd(a2, T, "ffn2_act")
    f2 = _mm(hs2, W["ffn2_w_out"], name="ffn2_out")
    h3, h3b = _ln_fwd(h2, f2, 0.5, lg[2], lb[2], T, "ln2")
    gl = _mm(h3b, W["ple_w_gate"], name="ple_gate")
    pp = _mm(pb, wp_t, tb=True, name="ple_proj")

    G = {}
    dh3_a, dgl, dpp, dg3, db3, loss = _head(h3, gl, pp, target, lg[3], lb[3], T)
    G["ple_w_gate"] = _mm(h3b, dgl, ta=True, name="d_ple_gate")
    G["ple_w_proj"] = _mm(dpp, pb, ta=True, name="d_ple_proj")
    dh3 = _mm(dgl, W["ple_w_gate"], tb=True, add=dh3_a, name="dh3")
    dh2_a, df2, dg2, db2 = _ln_bwd(h2, f2, 0.5, lg[2], lb[2], dh3, T, "ln2_bwd")
    G["ffn2_w_out"] = _mm(hs2, df2, ta=True, name="d_ffn2_out")
    dhs2 = _mm(df2, W["ffn2_w_out"], tb=True, name="dhs2")
    da2 = _swiglu_bwd(a2, dhs2, T, "ffn2_act_bwd")
    G["ffn2_w_in"] = _mm(da2, h2b, ta=True, name="d_ffn2_in")
    dh2 = _mm(da2, W["ffn2_w_in"], add=dh2_a, name="dh2")
    dh1_a, dmixed, dg1, db1 = _ln_bwd(h1, mixed, 1.0, lg[1], lb[1], dh2, T, "ln1_bwd")
    G["w_out"] = _mm(mix, dmixed, ta=True, name="d_mixer_out")
    dmix = _mm(dmixed, W["w_out"], tb=True, name="dmix")
    dgr, dgm, dy_ret, dy_mla = _mix_bwd(proj, y_ret, y_mla, dmix, T)
    G["w_mla_o"] = _mm(o, dy_mla, ta=True, name="d_mla_o")
    do = _mm(dy_mla, W["w_mla_o"], tb=True, name="do")
    dob, delta = _attn_delta(do, o, T)
    dqn_f, dqpe_f, dkn, dkpe_all, dv = _attn_bwd(qnope, qpe, kv, kpe, dob, lse, delta, T)
    dq_n, dq_r = _q_assemble_bwd(dqn_f, dqpe_f, tabs, T)
    g_uq = jnp.concatenate([_mm(dq_n, qn, ta=True, name="d_uq_nope"), _mm(dq_r, qn, ta=True, name="d_uq_rope")], axis=0)
    g_uq = jnp.concatenate([g_uq[:1024].reshape(HEADS, 128, LORA), g_uq[1024:].reshape(HEADS, 64, LORA)], axis=1)
    G["w_uq"] = g_uq.reshape(1536 * LORA // 1024, 1024)
    dqn = _mm(dq_r, wuq[1024:], add=_mm(dq_n, wuq[:1024], name="dqn_a"), name="dqn_b")
    g_ukv = jnp.stack([_mm(dkn, kvn, ta=True, name="d_ukv_k"), _mm(dv, kvn, ta=True, name="d_ukv_v")], axis=0)
    G["w_ukv"] = g_ukv.reshape(2, HEADS, 128, LORA).transpose(1, 0, 2, 3).reshape(2048 * LORA // 1024, 1024)
    dkvn = _mm(dv, wukv[1024:], add=_mm(dkn, wukv[:1024], name="dkvn_a"), name="dkvn_b")
    dcq, dckv, dkpe, dqg, dkvg = _rms_bwd(proj, dqn, dkvn, dkpe_all, tabs, qg, kvg, T)
    G["w_ret_o"] = _mm(yr, dy_ret, ta=True, name="d_ret_o")
    dyr = _mm(dy_ret, W["w_ret_o"], tb=True, name="dyr")
    drq, drk, drv, drg, dgn = _ret_bwd(rq, rk, rv, y, proj, gn_g, states, dyr, rc, tabs, T)
    dproj = jnp.concatenate([drq, drk, drv, drg, dgr, dgm, dcq, dckv, dkpe, jnp.zeros((T, P_W - P_KPE - 128), BF16)], axis=1)
    G["w_in"] = _w_in_external(_mm(dproj, h1b, ta=True, name="d_mixer_in"))
    dh1 = _mm(dproj, w_in_t, add=dh1_a, name="dh1")
    dx_a, df1, dg0, db0 = _ln_bwd(x, f1, 0.5, lg[0], lb[0], dh1, T, "ln0_bwd")
    G["ffn1_w_out"] = _mm(hs1, df1, ta=True, name="d_ffn1_out")
    dhs1 = _mm(df1, W["ffn1_w_out"], tb=True, name="dhs1")
    da1 = _swiglu_bwd(a1, dhs1, T, "ffn1_act_bwd")
    G["ffn1_w_in"] = _mm(da1, xb, ta=True, name="d_ffn1_in")
    grad_x = _mm(da1, W["ffn1_w_in"], add=dx_a, name="grad_x")

    small = dict(ln_g=jnp.concatenate([dg0, dg1, dg2, dg3], axis=0), ln_b=jnp.concatenate([db0, db1, db2, db3], axis=0),
                 ret_gn_g=dgn, q_norm_g=dqg, kv_norm_g=dkvg)
    return loss, grad_x, G, small


def kernel(x, p, positions, ln_g, ln_b, ffn1_w_in, ffn1_w_out, w_in, ret_gn_g, w_ret_o, q_norm_g, kv_norm_g, w_uq, w_ukv, w_mla_o, w_out, ffn2_w_in, ffn2_w_out, ple_w_gate, ple_w_proj, loss_target, m_ln_g, m_ln_b, m_ffn1_w_in, m_ffn1_w_out, m_w_in, m_ret_gn_g, m_w_ret_o, m_q_norm_g, m_kv_norm_g, m_w_uq, m_w_ukv, m_w_mla_o, m_w_out, m_ffn2_w_in, m_ffn2_w_out, m_ple_w_gate, m_ple_w_proj, v_ln_g, v_ln_b, v_ffn1_w_in, v_ffn1_w_out, v_w_in, v_ret_gn_g, v_w_ret_o, v_q_norm_g, v_kv_norm_g, v_w_uq, v_w_ukv, v_w_mla_o, v_w_out, v_ffn2_w_in, v_ffn2_w_out, v_ple_w_gate, v_ple_w_proj):
    names = ("ln_g", "ln_b", "ffn1_w_in", "ffn1_w_out", "w_in", "ret_gn_g", "w_ret_o", "q_norm_g", "kv_norm_g", "w_uq", "w_ukv",
             "w_mla_o", "w_out", "ffn2_w_in", "ffn2_w_out", "ple_w_gate", "ple_w_proj")
    ws = dict(zip(names, (ln_g, ln_b, ffn1_w_in, ffn1_w_out, w_in, ret_gn_g, w_ret_o, q_norm_g, kv_norm_g, w_uq, w_ukv, w_mla_o,
                          w_out, ffn2_w_in, ffn2_w_out, ple_w_gate, ple_w_proj)))
    ms = dict(zip(names, (m_ln_g, m_ln_b, m_ffn1_w_in, m_ffn1_w_out, m_w_in, m_ret_gn_g, m_w_ret_o, m_q_norm_g, m_kv_norm_g, m_w_uq,
                          m_w_ukv, m_w_mla_o, m_w_out, m_ffn2_w_in, m_ffn2_w_out, m_ple_w_gate, m_ple_w_proj)))
    vs = dict(zip(names, (v_ln_g, v_ln_b, v_ffn1_w_in, v_ffn1_w_out, v_w_in, v_ret_gn_g, v_w_ret_o, v_q_norm_g, v_kv_norm_g, v_w_uq,
                          v_w_ukv, v_w_mla_o, v_w_out, v_ffn2_w_in, v_ffn2_w_out, v_ple_w_gate, v_ple_w_proj)))

    parts = []
    for name, r in PACK:
        rows = _to_rows(name, ws[name])
        if _pad16(r) != r:
            rows = jnp.concatenate([rows, jnp.zeros((_pad16(r) - r, 1024), F32)], axis=0)
        parts.append(rows)
    wsh = jnp.concatenate(parts, axis=0).astype(BF16)
    ssh = jnp.concatenate([ln_g[0], ln_b[0]], axis=0)
    wall, sall = _all_gather(wsh, ssh)
    W = _unpack(wall)
    ln_full = sall.reshape(N_DEV, 2, 4, 128).transpose(1, 2, 0, 3).reshape(2, 4, 1024)

    loss_p, grad_x, G, small = _local_step(x[0], p[0, 0], positions, loss_target[0], W, ln_full[0], ln_full[1],
                                           ret_gn_g, q_norm_g, kv_norm_g)

    gparts = []
    for name, r in PACK:
        g = G[name].reshape(N_DEV, r, 1024)
        if _pad16(r) != r:
            g = jnp.concatenate([g, jnp.zeros((N_DEV, _pad16(r) - r, 1024), F32)], axis=1)
        gparts.append(g)
    gfull = jnp.concatenate(gparts, axis=1).astype(BF16)
    pad256 = lambda a: jnp.concatenate([a, jnp.zeros((1, 1024 - a.shape[1]), F32)], axis=1)
    gsmall = jnp.concatenate([small["ln_g"], small["ln_b"], small["ret_gn_g"].reshape(2, 1024), pad256(small["q_norm_g"]),
                              pad256(small["kv_norm_g"]), jnp.zeros((SMALL_ROWS - 12, 1024), F32)], axis=0)
    g4 = gfull.reshape(4, 2, PACK_ROWS, 1024)
    cvec = lax.axis_index("c").astype(jnp.int32).reshape(1)
    part = _sum_sibling(g4, _exchange_sibling(g4), cvec)
    recv, srecv = _exchange_chips(part, gsmall)
    gsh = _sum_slots(recv, "sum_grads")
    ssum = _sum_slots(srecv, "sum_small_grads")

    grads = {}
    off = 0
    for name, r in PACK:
        grads[name] = _from_rows(name, gsh[off:off + r], ws[name].shape)
        off += _pad16(r)
    me = 4 * lax.axis_index("x") + 2 * lax.axis_index("y") + lax.axis_index("c")
    grads["ln_g"] = lax.dynamic_slice(ssum[0:4], (0, me * 128), (4, 128)).reshape(1, 4, 128)
    grads["ln_b"] = lax.dynamic_slice(ssum[4:8], (0, me * 128), (4, 128)).reshape(1, 4, 128)
    grads["ret_gn_g"] = ssum[8:10].reshape(1, 2048)
    grads["q_norm_g"] = ssum[10:11, :256]
    grads["kv_norm_g"] = ssum[11:12, :256]

    delta, new_m, new_v = {}, {}, {}
    for name in names:
        delta[name], new_m[name], new_v[name] = _adamw(ws[name], grads[name], ms[name], vs[name], "adamw_" + name)

    loss = lax.psum(loss_p[0, 0], ("x", "y", "c"))
    return (loss, grad_x[None], *[grads[n] for n in names], *[delta[n] for n in names],
            *[new_m[n] for n in names], *[new_v[n] for n in names])
```

```python
import functools
import math

import jax
import jax.numpy as jnp
from jax import lax
from jax.experimental import pallas as pl
from jax.experimental.pallas import tpu as pltpu

F32 = jnp.float32
BF16 = jnp.bfloat16

N_DEV = 8
D = 1024
D_FF = 2816
D_PLE = 256
CHUNK = 64
HEADS = 8
RET_DK = 128
RET_DV = 256
MLA_NOPE = 128
MLA_ROPE = 64
MLA_DV = 128
LORA = 256
ROPE_BASE = 10000.0
EPS = 1e-5
ALPHA = 2.0 ** 0.25
RET_SCALE = RET_DK ** -0.5
MLA_SCALE = (MLA_NOPE + MLA_ROPE) ** -0.5
NEG = -1e30

ADAM_LR = 0.001
ADAM_B1 = 0.9
ADAM_B2 = 0.999
ADAM_EPS = 1e-08
ADAM_WD = 0.01
ADAM_STEP = 10

P_RQ, P_RK, P_RV, P_RG, P_GR, P_GM, P_CQ, P_CKV, P_KPE, P_W = 0, 1024, 2048, 4096, 6144, 7168, 8192, 8448, 8704, 8960
W_IN_COLS = 8768
RET_L = 256
ATT_TF = 1024
ATT_TB = 1024
LOG2E = math.log2(math.e)
Q_PRESCALE = MLA_SCALE * LOG2E

PACK = (("ffn1_w_in", 704), ("ffn1_w_out", 352), ("w_in", 1096), ("w_ret_o", 256), ("w_uq", 48), ("w_ukv", 64),
        ("w_mla_o", 128), ("w_out", 128), ("ffn2_w_in", 704), ("ffn2_w_out", 352), ("ple_w_gate", 128), ("ple_w_proj", 32))


def _pad16(r):
    return -(-r // 16) * 16


PACK_ROWS = sum(_pad16(r) for _, r in PACK)
SMALL_ROWS = 16


def _pcall(body, **kw):
    return pl.pallas_call(body, **kw)


def _pick(dim, prefs):
    for p in prefs:
        if dim % p == 0:
            return p
    return dim


def _sigmoid(x):
    return 1.0 / (1.0 + jnp.exp(-x))


def _silu(x):
    return x * _sigmoid(x)


def _ln(r, g, b):
    mu = jnp.mean(r, axis=-1, keepdims=True)
    var = jnp.mean(jnp.square(r - mu), axis=-1, keepdims=True)
    return (r - mu) * lax.rsqrt(var + EPS) * g + b


def _rms(x, g):
    return x * lax.rsqrt(jnp.mean(jnp.square(x), axis=-1, keepdims=True) + EPS) * g


def _dot(a, b, ca, cb):
    return lax.dot_general(a, b, (((ca,), (cb,)), ((), ())), preferred_element_type=F32)


def _accum(ref, val, first=None):
    @pl.when(pl.program_id(0) == 0 if first is None else first)
    def _():
        ref[...] = jnp.zeros_like(ref)

    ref[...] += val


def _mm(a, b, *, ta=False, tb=False, add=None, out_dtype=F32, name, tm=None, tn=None, tk=None, epilogue=None):
    parts = a.shape[0] if a.ndim == 3 else 1
    ar, ac = a.shape[-2], a.shape[-1]
    if ta:
        K, M = ar, ac * parts
    else:
        M, K = ar, ac * parts
    if tb:
        N, K2 = b.shape
    else:
        K2, N = b.shape
    assert K == K2, (a.shape, b.shape, ta, tb)
    big = (1024, 1408, 1280, 768, 512, 256, 128)
    tm = tm or _pick(ac if (ta and parts > 1) else M, big)
    tn = tn or (N if N <= 1024 else _pick(N, big))
    kdim = ac if (not ta and parts > 1) else K
    tk = tk or (kdim if kdim <= 2816 and parts == 1 else
                _pick(kdim, (2048, 1408, 1280, 1024, 512) if tm <= 1024 else (1024, 1408, 1280, 512)))
    nk = K // tk
    grid = (M // tm, N // tn, nk)
    if parts > 1 and ta:
        per = ac // tm
        a_spec = pl.BlockSpec((None, tk, tm), lambda i, j, k: (i // per, k, i % per))
    elif parts > 1:
        per = ac // tk
        a_spec = pl.BlockSpec((None, tm, tk), lambda i, j, k: (k // per, i, k % per))
    else:
        a_spec = pl.BlockSpec((tk, tm), lambda i, j, k: (k, i)) if ta else pl.BlockSpec((tm, tk), lambda i, j, k: (i, k))
    b_spec = pl.BlockSpec((tn, tk), lambda i, j, k: (j, k)) if tb else pl.BlockSpec((tk, tn), lambda i, j, k: (k, j))
    o_spec = pl.BlockSpec((tm, tn), lambda i, j, k: (i, j))
    ca, cb = (0 if ta else 1), (1 if tb else 0)
    has_add = add is not None
    n_in = 2 + int(has_add)
    if epilogue is not None:
        assert tn == N and not ta
        ep_fn, ep_rows, ep_whole, ep_outs, ep_accs = epilogue
        n_ep_in = len(ep_rows) + len(ep_whole)
        n_out = len(ep_outs) + len(ep_accs)
    else:
        n_ep_in, n_out = 0, 1

    def body(*refs):
        a_ref, b_ref = refs[0], refs[1]
        add_ref = refs[2] if has_add else None
        o_ref = refs[n_in + n_ep_in]
        first_row_tile = pl.program_id(0) == 0

        def finish(r):
            if has_add:
                r = r + add_ref[...].astype(F32)
            if epilogue is not None:
                ep_fn(r, refs[n_in:n_in + n_ep_in], refs[n_in + n_ep_in:n_in + n_ep_in + n_out], first_row_tile)
            else:
                o_ref[...] = r.astype(out_dtype)

        if nk == 1:
            finish(_dot(a_ref[...], b_ref[...], ca, cb))
            return
        acc_ref = refs[-1]
        k = pl.program_id(2)

        @pl.when(k == 0)
        def _():
            acc_ref[...] = jnp.zeros_like(acc_ref)

        acc_ref[...] += _dot(a_ref[...], b_ref[...], ca, cb)

        @pl.when(k == nk - 1)
        def _():
            finish(acc_ref[...])

    in_specs = [a_spec, b_spec] + ([o_spec] if has_add else [])
    args = (a, b) + ((add,) if has_add else ())
    out_specs, out_shape = o_spec, jax.ShapeDtypeStruct((M, N), out_dtype)
    if epilogue is not None:
        in_specs += [o_spec] * len(ep_rows) + [pl.BlockSpec(w.shape, lambda i, j, k, _n=w.ndim: (0,) * _n) for w in ep_whole]
        args += tuple(ep_rows) + tuple(ep_whole)
        out_specs = [o_spec] * len(ep_outs) + [pl.BlockSpec((r, N), lambda i, j, k: (0, 0)) for r in ep_accs]
        out_shape = [jax.ShapeDtypeStruct((M, N), dt) for dt in ep_outs] + [jax.ShapeDtypeStruct((r, N), F32) for r in ep_accs]
    return _pcall(
        body, grid=grid, in_specs=in_specs, out_specs=out_specs, out_shape=out_shape,
        scratch_shapes=[pltpu.VMEM((tm, tn), F32)] if nk > 1 else [], name=name,
        compiler_params=pltpu.CompilerParams(dimension_semantics=("arbitrary" if epilogue is not None else "parallel", "parallel", "arbitrary")),
    )(*args)


def _ln_epilogue(res, c, g, b):
    def fn(r, ins, outs, first):
        res_ref, g_ref, b_ref = ins
        f_ref, h_ref, hb_ref = outs
        h = _ln(ALPHA * res_ref[...] + c * r, g_ref[...], b_ref[...])
        f_ref[...] = r
        h_ref[...] = h
        hb_ref[...] = h.astype(BF16)

    return (fn, [res], [g, b], (F32, F32, BF16), ())


def _ln_bwd_epilogue(res, f, c, g, b):
    def fn(r, ins, outs, first):
        res_ref, f_ref, g_ref, b_ref = ins
        dr_ref, df_ref, dg_ref, db_ref = outs
        _, vjp = jax.vjp(lambda rr, ff, gg, bb: _ln(ALPHA * rr + c * ff, gg, bb), res_ref[...], f_ref[...], g_ref[...], b_ref[...])
        dr, df, dg, db = vjp(r)
        dr_ref[...] = dr
        df_ref[...] = df.astype(BF16)
        _accum(dg_ref, dg, first)
        _accum(db_ref, db, first)

    return (fn, [res, f], [g, b], (F32, BF16), (1, 1))


def _rows(body, T, tm, ins, outs, name, accs=()):
    in_specs, args = [], []
    for arr, w, cb in ins:
        if w is None:
            in_specs.append(pl.BlockSpec(arr.shape, lambda i, _n=arr.ndim: (0,) * _n))
        else:
            in_specs.append(pl.BlockSpec((tm, w), lambda i, _cb=cb: (i, _cb)))
        args.append(arr)
    out_specs = [pl.BlockSpec((tm, w), lambda i: (i, 0)) for w, _ in outs]
    out_shape = [jax.ShapeDtypeStruct((T, w), dt) for w, dt in outs]
    for r, w in accs:
        out_specs.append(pl.BlockSpec((r, w), lambda i: (0, 0)))
        out_shape.append(jax.ShapeDtypeStruct((r, w), F32))
    return _pcall(
        body, grid=(T // tm,), in_specs=in_specs, out_specs=out_specs, out_shape=out_shape, name=name,
        compiler_params=pltpu.CompilerParams(dimension_semantics=("arbitrary",)),
    )(*args)


FFN_TN = 1408
FFN_TM = 512
LN_TM = 512


def _ffn_in(xb, wt, name):
    T = xb.shape[0]
    tm, tn = min(FFN_TM, T), FFN_TN
    nj = D_FF // tn

    def body(x_ref, wg_ref, wu_ref, hs_ref, gu_ref):
        x = x_ref[...]
        g = _dot(x, wg_ref[...], 1, 1)
        u = _dot(x, wu_ref[...], 1, 1)
        hs_ref[...] = (_silu(g) * u).astype(BF16)
        gu_ref[0] = g.astype(BF16)
        gu_ref[1] = u.astype(BF16)

    return _pcall(
        body, grid=(T // tm, nj),
        in_specs=[pl.BlockSpec((tm, D), lambda i, j: (i, 0)), pl.BlockSpec((tn, D), lambda i, j: (j, 0)),
                  pl.BlockSpec((tn, D), lambda i, j: (j + nj, 0))],
        out_specs=[pl.BlockSpec((tm, tn), lambda i, j: (i, j)), pl.BlockSpec((2, tm, tn), lambda i, j: (0, i, j))],
        out_shape=[jax.ShapeDtypeStruct((T, D_FF), BF16), jax.ShapeDtypeStruct((2, T, D_FF), BF16)], name=name,
        compiler_params=pltpu.CompilerParams(dimension_semantics=("parallel", "parallel")),
    )(xb, wt, wt)


def _ffn_act_bwd(df, wo, gu, name):
    T = df.shape[0]
    tm, tn = min(FFN_TM, T), FFN_TN

    def body(d_ref, w_ref, gu_ref, o_ref):
        dhs = _dot(d_ref[...], w_ref[...], 1, 1)
        _, vjp = jax.vjp(lambda g, u: _silu(g) * u, gu_ref[0].astype(F32), gu_ref[1].astype(F32))
        dg, du = vjp(dhs)
        o_ref[0] = dg.astype(BF16)
        o_ref[1] = du.astype(BF16)

    return _pcall(
        body, grid=(T // tm, D_FF // tn),
        in_specs=[pl.BlockSpec((tm, D), lambda i, j: (i, 0)), pl.BlockSpec((tn, D), lambda i, j: (j, 0)),
                  pl.BlockSpec((2, tm, tn), lambda i, j: (0, i, j))],
        out_specs=pl.BlockSpec((2, tm, tn), lambda i, j: (0, i, j)),
        out_shape=jax.ShapeDtypeStruct((2, T, D_FF), BF16), name=name,
        compiler_params=pltpu.CompilerParams(dimension_semantics=("parallel", "parallel")),
    )(df, wo, gu)


def _rope128(t, cos, sin_s):
    return t * cos + pltpu.roll(t, 64, 1) * sin_s


def _rope128_t(g, cos, sin_s):
    return g * cos - pltpu.roll(g, 64, 1) * sin_s


def _partner32(t):
    lane = lax.broadcasted_iota(jnp.int32, t.shape, 1)
    return jnp.where((lane & 32) == 0, pltpu.roll(t, 96, 1), pltpu.roll(t, 32, 1))


def _rope64(t, cos, sin_s):
    return t * cos + _partner32(t) * sin_s


def _rope64_t(g, cos, sin_s):
    return g * cos - _partner32(g) * sin_s


def _mixer_prep_fwd(proj, tabs, qg, kvg, T):
    cos128, sin128, cos64, sin64 = tabs

    def body(rq_ref, rk_ref, rv_ref, cq_ref, ckv_ref, kpe_ref, c1_ref, s1_ref, c2_ref, s2_ref, qg_ref, kvg_ref,
             oq_ref, ok_ref, ov_ref, oqn_ref, okvn_ref, okpe_ref):
        c1, s1 = c1_ref[...], s1_ref[...]
        for h in range(HEADS):
            sl = slice(h * RET_DK, (h + 1) * RET_DK)
            oq_ref[:, sl] = _rope128(rq_ref[:, sl], c1, s1).astype(BF16)
            ok_ref[:, sl] = (_rope128(rk_ref[:, sl], c1, s1) * RET_SCALE).astype(BF16)
        ov_ref[...] = rv_ref[...].astype(BF16)
        oqn_ref[...] = _rms(cq_ref[...], qg_ref[...]).astype(BF16)
        okvn_ref[...] = _rms(ckv_ref[...], kvg_ref[...]).astype(BF16)
        okpe_ref[...] = _rope64(kpe_ref[...], c2_ref[...], s2_ref[...]).astype(BF16)

    ins = [(proj, 1024, 0), (proj, 1024, 1), (proj, 2048, 1), (proj, 256, P_CQ // 256), (proj, 256, P_CKV // 256),
           (proj, 128, P_KPE // 128), (cos128, 128, 0), (sin128, 128, 0), (cos64, 128, 0), (sin64, 128, 0),
           (qg, None, None), (kvg, None, None)]
    outs = [(1024, BF16), (1024, BF16), (2048, BF16), (LORA, BF16), (LORA, BF16), (128, BF16)]
    return _rows(body, T, 256, ins, outs, "mixer_prep_fwd")


def _rms_bwd(proj, dqn, dkvn, dkpe_all, tabs, qg, kvg, T):
    _, _, cos64, sin64 = tabs

    def body(cq_ref, ckv_ref, dq_ref, dkv_ref, dk_ref, c2_ref, s2_ref, qg_ref, kvg_ref, ocq_ref, ockv_ref, okpe_ref,
             dqg_ref, dkvg_ref):
        _, vjp = jax.vjp(_rms, cq_ref[...], qg_ref[...])
        dx, dg = vjp(dq_ref[...])
        ocq_ref[...] = dx.astype(BF16)
        _accum(dqg_ref, dg)
        _, vjp = jax.vjp(_rms, ckv_ref[...], kvg_ref[...])
        dx, dg = vjp(dkv_ref[...])
        ockv_ref[...] = dx.astype(BF16)
        _accum(dkvg_ref, dg)
        g = dk_ref[:, 0:128]
        for h in range(1, HEADS):
            g = g + dk_ref[:, h * 128:(h + 1) * 128]
        lane = lax.broadcasted_iota(jnp.int32, g.shape, 1)
        g = jnp.where(lane < MLA_ROPE, g, 0.0)
        okpe_ref[...] = _rope64_t(g, c2_ref[...], s2_ref[...]).astype(BF16)

    ins = [(proj, 256, P_CQ // 256), (proj, 256, P_CKV // 256), (dqn, LORA, 0), (dkvn, LORA, 0), (dkpe_all, 1024, 0),
           (cos64, 128, 0), (sin64, 128, 0), (qg, None, None), (kvg, None, None)]
    return _rows(body, T, 256, ins, [(LORA, BF16), (LORA, BF16), (128, BF16)], "rms_bwd", accs=[(1, LORA), (1, LORA)])


def _gn_gate(y, rg, g):
    mu = jnp.mean(y, axis=-1, keepdims=True)
    var = jnp.mean(jnp.square(y - mu), axis=-1, keepdims=True)
    return _silu(rg) * ((y - mu) * lax.rsqrt(var + EPS) * g)


def _q_assemble_fwd(q, tabs, T):
    _, _, cos64, sin64 = tabs

    def body(q_ref, c_ref, s_ref, on_ref, op_ref):
        on_ref[...] = (q_ref[:, :1024] * Q_PRESCALE).astype(BF16)
        c, s = c_ref[...], s_ref[...]
        lane = lax.broadcasted_iota(jnp.int32, c.shape, 1)
        for j in range(HEADS // 2):
            r = _rope64(q_ref[:, 1024 + 128 * j:1024 + 128 * (j + 1)], c, s) * Q_PRESCALE
            op_ref[:, 256 * j:256 * j + 128] = jnp.where(lane < 64, r, 0.0).astype(BF16)
            op_ref[:, 256 * j + 128:256 * j + 256] = jnp.where(lane < 64, pltpu.roll(r, 64, 1), 0.0).astype(BF16)

    return _rows(body, T, 256, [(q, 1536, 0), (cos64, 128, 0), (sin64, 128, 0)], [(1024, BF16), (1024, BF16)], "q_assemble_fwd")


def _q_assemble_bwd(dqn, dqpe, tabs, T):
    _, _, cos64, sin64 = tabs

    def body(dn_ref, dp_ref, c_ref, s_ref, on_ref, op_ref):
        on_ref[...] = dn_ref[...].astype(BF16)
        c, s = c_ref[...], s_ref[...]
        lane = lax.broadcasted_iota(jnp.int32, c.shape, 1)
        for j in range(HEADS // 2):
            g = jnp.where(lane < 64, dp_ref[:, 256 * j:256 * j + 128], pltpu.roll(dp_ref[:, 256 * j + 128:256 * j + 256], 64, 1))
            op_ref[:, 128 * j:128 * (j + 1)] = _rope64_t(g, c, s).astype(BF16)

    return _rows(body, T, 256, [(dqn, 1024, 0), (dqpe, 1024, 0), (cos64, 128, 0), (sin64, 128, 0)],
                 [(1024, BF16), (512, BF16)], "q_assemble_bwd")


def _mix_fn(gr, gm, yr, ym):
    return _sigmoid(gr) * yr + _sigmoid(gm) * ym


def _mix_fwd(proj, y_ret, y_mla, T):
    def body(gr_ref, gm_ref, yr_ref, ym_ref, o_ref):
        o_ref[...] = _mix_fn(gr_ref[...], gm_ref[...], yr_ref[...], ym_ref[...]).astype(BF16)

    return _rows(body, T, 256, [(proj, 1024, P_GR // 1024), (proj, 1024, P_GM // 1024), (y_ret, D, 0), (y_mla, D, 0)],
                 [(D, BF16)], "mix_fwd")[0]


def _mix_bwd(proj, y_ret, y_mla, dmix, T):
    def body(gr_ref, gm_ref, yr_ref, ym_ref, d_ref, dgr_ref, dgm_ref, dyr_ref, dym_ref):
        _, vjp = jax.vjp(_mix_fn, gr_ref[...], gm_ref[...], yr_ref[...], ym_ref[...])
        dgr, dgm, dyr, dym = vjp(d_ref[...])
        dgr_ref[...] = dgr.astype(BF16)
        dgm_ref[...] = dgm.astype(BF16)
        dyr_ref[...] = dyr.astype(BF16)
        dym_ref[...] = dym.astype(BF16)

    return _rows(body, T, 256, [(proj, 1024, P_GR // 1024), (proj, 1024, P_GM // 1024), (y_ret, D, 0), (y_mla, D, 0), (dmix, D, 0)],
                 [(D, BF16)] * 4, "mix_bwd")


def _head(h3, gl, pp, tgt, g, b, T):
    def body(h_ref, gl_ref, pp_ref, t_ref, g_ref, b_ref, dh_ref, dgl_ref, dpp_ref, dg_ref, db_ref, loss_ref):
        def f(h, gg, p, lg, lb):
            return _ln(ALPHA * h + _sigmoid(gg) * p, lg, lb)

        h4, vjp = jax.vjp(f, h_ref[...], gl_ref[...], pp_ref[...], g_ref[...], b_ref[...])
        err = h4 - t_ref[...]
        dh, dgl, dpp, dg, db = vjp(err * (1.0 / D))
        dh_ref[...] = dh
        dgl_ref[...] = dgl.astype(BF16)
        dpp_ref[...] = dpp.astype(BF16)
        _accum(dg_ref, dg)
        _accum(db_ref, db)
        part = 0.5 * jnp.sum(jnp.mean(jnp.square(err), axis=-1, keepdims=True), axis=0, keepdims=True)
        _accum(loss_ref, jnp.broadcast_to(part, loss_ref.shape))

    return _rows(body, T, 256, [(h3, D, 0), (gl, D, 0), (pp, D, 0), (tgt, D, 0), (g, None, None), (b, None, None)],
                 [(D, F32), (D, BF16), (D, BF16)], "head", accs=[(1, D), (1, D), (8, 128)])


def _attn_delta(do, o, T):
    def body(d_ref, o_ref, db_ref, dl_ref):
        for h in range(HEADS):
            sl = slice(h * MLA_DV, (h + 1) * MLA_DV)
            d = d_ref[:, sl]
            dl = jnp.sum(d * o_ref[:, sl].astype(F32), axis=-1, keepdims=True)
            dl_ref[:, sl] = jnp.broadcast_to(dl, d.shape)
        db_ref[...] = d_ref[...].astype(BF16)

    return _rows(body, T, 256, [(do, D, 0), (o, D, 0)], [(D, BF16), (D, F32)], "attn_delta")


def _ret_consts():
    L = RET_L
    lg = jnp.log(1.0 - 2.0 ** (-5.0 - jnp.arange(HEADS, dtype=F32)))[:, None, None]
    idx = jnp.arange(L, dtype=F32)
    ch = jnp.arange(L) // CHUNK
    dist = idx[:, None] - idx[None, :]
    same = (ch[:, None] == ch[None, :])[None]
    earlier = (ch[None, :] < ch[:, None])[None]
    dm = jnp.where(same, jnp.exp(lg * jnp.abs(dist)[None]), jnp.where(earlier, jnp.exp(lg * dist[None]), 0.0))
    xi = jnp.broadcast_to(jnp.exp(lg * (idx + 1.0)[None, :, None]), (HEADS, L, 128))
    zeta = jnp.broadcast_to(jnp.exp(lg * (L - 1.0 - idx)[None, :, None]), (HEADS, L, 128))
    gl = jnp.broadcast_to(jnp.exp(lg * float(L)), (HEADS, 8, 128))
    return dm.astype(F32), xi.astype(F32), zeta.astype(F32), gl.astype(F32)


def _whole(arr):
    return pl.BlockSpec(arr.shape, lambda n, _nd=arr.ndim: (0,) * _nd)


def _ret_fwd(q, k, v, proj, gn_g, consts, T):
    dm, xi, zeta, gl = consts
    L = RET_L
    n_sc = T // L

    def body(q_ref, k_ref, v_ref, rg_ref, g_ref, dm_ref, xi_ref, ze_ref, gl_ref, y_ref, yr_ref, s_ref, st_ref):
        @pl.when(pl.program_id(0) == 0)
        def _():
            st_ref[...] = jnp.zeros_like(st_ref)

        for h in range(HEADS):
            ks, vs = slice(h * RET_DK, (h + 1) * RET_DK), slice(h * RET_DV, (h + 1) * RET_DV)
            qq, kk, vv = q_ref[:, ks], k_ref[:, ks], v_ref[:, vs]
            st = st_ref[h]
            s_ref[h, 0] = st
            p = (_dot(qq, kk, 1, 1) * dm_ref[h]).astype(BF16)
            cross = _dot(qq, st.astype(BF16), 1, 0)
            xi_c = jnp.concatenate([xi_ref[h], xi_ref[h]], axis=1)
            y = _dot(p, vv, 1, 0) + cross * xi_c
            y_ref[:, vs] = y
            yr_ref[:, vs] = _gn_gate(y, rg_ref[:, vs], g_ref[:, vs]).astype(BF16)
            kz = (kk.astype(F32) * ze_ref[h]).astype(BF16)
            gl2 = jnp.concatenate([gl_ref[h, 0:1, :], gl_ref[h, 0:1, :]], axis=1)
            st_ref[h] = st * gl2 + _dot(kz, vv, 0, 0)

    return _pcall(
        body, grid=(n_sc,),
        in_specs=[pl.BlockSpec((L, 1024), lambda n: (n, 0)), pl.BlockSpec((L, 1024), lambda n: (n, 0)),
                  pl.BlockSpec((L, 2048), lambda n: (n, 0)), pl.BlockSpec((L, 2048), lambda n: (n, P_RG // 2048)),
                  _whole(gn_g), _whole(dm), _whole(xi), _whole(zeta), _whole(gl)],
        out_specs=[pl.BlockSpec((L, 2048), lambda n: (n, 0)), pl.BlockSpec((L, 2048), lambda n: (n, 0)),
                   pl.BlockSpec((HEADS, 1, 128, 256), lambda n: (0, n, 0, 0))],
        out_shape=[jax.ShapeDtypeStruct((T, HEADS * RET_DV), F32), jax.ShapeDtypeStruct((T, HEADS * RET_DV), BF16),
                   jax.ShapeDtypeStruct((HEADS, n_sc, 128, 256), F32)],
        scratch_shapes=[pltpu.VMEM((HEADS, 128, 256), F32)], name="ret_fwd",
        compiler_params=pltpu.CompilerParams(dimension_semantics=("arbitrary",)),
    )(q, k, v, proj, gn_g, dm, xi, zeta, gl)


def _ret_bwd(q, k, v, y, proj, gn_g, states, dyr, consts, tabs, T):
    dm, xi, zeta, gl = consts
    cos128, sin128, _, _ = tabs
    L = RET_L
    n_sc = T // L

    def body(q_ref, k_ref, v_ref, y_ref, rg_ref, g_ref, d_ref, s_ref, dm_ref, xi_ref, ze_ref, gl_ref, c_ref, sn_ref,
             dq_ref, dk_ref, dv_ref, drg_ref, dg_ref, gs_ref):
        @pl.when(pl.program_id(0) == 0)
        def _():
            gs_ref[...] = jnp.zeros_like(gs_ref)

        c, sn = c_ref[...], sn_ref[...]
        dgs = []
        for h in range(HEADS):
            ks, vs = slice(h * RET_DK, (h + 1) * RET_DK), slice(h * RET_DV, (h + 1) * RET_DV)
            _, vjp = jax.vjp(_gn_gate, y_ref[:, vs], rg_ref[:, vs], g_ref[:, vs])
            dy, drg, dg = vjp(d_ref[:, vs])
            drg_ref[:, vs] = drg.astype(BF16)
            dgs.append(dg)
            qq, kk, vv, dyy = q_ref[:, ks], k_ref[:, ks], v_ref[:, vs], dy.astype(BF16)
            dmm = dm_ref[h]
            gb = gs_ref[h].astype(BF16)
            sb = s_ref[h, 0].astype(BF16)
            xi_c = jnp.concatenate([xi_ref[h], xi_ref[h]], axis=1)
            pb = (_dot(qq, kk, 1, 1) * dmm).astype(BF16)
            kz = (kk.astype(F32) * ze_ref[h]).astype(BF16)
            dv_ref[:, vs] = (_dot(pb, dyy, 0, 0) + _dot(kz, gb, 1, 0)).astype(BF16)
            da = (_dot(dyy, vv, 1, 1) * dmm).astype(BF16)
            dyx = (dyy.astype(F32) * xi_c).astype(BF16)
            dq = _dot(da, kk, 1, 0) + _dot(dyx, sb, 1, 1)
            dk = _dot(da, qq, 0, 0) + _dot(vv, gb, 1, 1) * ze_ref[h]
            dq_ref[:, ks] = _rope128_t(dq, c, sn).astype(BF16)
            dk_ref[:, ks] = (_rope128_t(dk, c, sn) * RET_SCALE).astype(BF16)
            gl2 = jnp.concatenate([gl_ref[h, 0:1, :], gl_ref[h, 0:1, :]], axis=1)
            gs_ref[h] = gs_ref[h] * gl2 + _dot(qq, dyx, 0, 0)
        _accum(dg_ref, jnp.concatenate(dgs, axis=1))

    rev = lambda n: n_sc - 1 - n
    return _pcall(
        body, grid=(n_sc,),
        in_specs=[pl.BlockSpec((L, 1024), lambda n: (rev(n), 0)), pl.BlockSpec((L, 1024), lambda n: (rev(n), 0)),
                  pl.BlockSpec((L, 2048), lambda n: (rev(n), 0)), pl.BlockSpec((L, 2048), lambda n: (rev(n), 0)),
                  pl.BlockSpec((L, 2048), lambda n: (rev(n), P_RG // 2048)), _whole(gn_g),
                  pl.BlockSpec((L, 2048), lambda n: (rev(n), 0)),
                  pl.BlockSpec((HEADS, 1, 128, 256), lambda n: (0, rev(n), 0, 0)),
                  _whole(dm), _whole(xi), _whole(zeta), _whole(gl),
                  pl.BlockSpec((L, 128), lambda n: (rev(n), 0)), pl.BlockSpec((L, 128), lambda n: (rev(n), 0))],
        out_specs=[pl.BlockSpec((L, 1024), lambda n: (rev(n), 0)), pl.BlockSpec((L, 1024), lambda n: (rev(n), 0)),
                   pl.BlockSpec((L, 2048), lambda n: (rev(n), 0)), pl.BlockSpec((L, 2048), lambda n: (rev(n), 0)),
                   pl.BlockSpec((1, 2048), lambda n: (0, 0))],
        out_shape=[jax.ShapeDtypeStruct((T, 1024), BF16), jax.ShapeDtypeStruct((T, 1024), BF16), jax.ShapeDtypeStruct((T, 2048), BF16),
                   jax.ShapeDtypeStruct((T, 2048), BF16), jax.ShapeDtypeStruct((1, 2048), F32)],
        scratch_shapes=[pltpu.VMEM((HEADS, 128, 256), F32)], name="ret_bwd",
        compiler_params=pltpu.CompilerParams(dimension_semantics=("arbitrary",)),
    )(q, k, v, y, proj, gn_g, dyr, states, dm, xi, zeta, gl, cos128, sin128)


def _diag_mask(nrows, ncols, row0):
    row = row0 + lax.broadcasted_iota(jnp.int32, (nrows, ncols), 0)
    col = lax.broadcasted_iota(jnp.int32, (nrows, ncols), 1)
    return lax.shift_right_logical(col, 6) <= lax.shift_right_logical(row, 6)


def _tri_steps(nb, by_key):
    if by_key:
        pairs = [(i, j) for j in range(nb) for i in range(j, nb)]
    else:
        pairs = [(i, j) for i in range(nb) for j in range(i + 1)]
    return jnp.array([a for a, _ in pairs], jnp.int32), jnp.array([b for _, b in pairs], jnp.int32)


def _attn_fwd(qn, qpe, kv, kpe, T):
    t = min(ATT_TF, T)
    nb = T // t
    ii, jj = _tri_steps(nb, by_key=False)

    def body(ii_ref, jj_ref, qn_ref, qp_ref, kn_ref, kp_ref, v_ref, o_ref, lse_ref, m_sc, l_sc, acc_sc):
        st = pl.program_id(1)
        i, j = ii_ref[st], jj_ref[st]

        @pl.when(j == 0)
        def _():
            m_sc[...] = jnp.full_like(m_sc, NEG)
            l_sc[...] = jnp.zeros_like(l_sc)
            acc_sc[...] = jnp.zeros_like(acc_sc)

        def update(diag):
            q = jnp.concatenate([qn_ref[...], qp_ref[...]], axis=1)
            k = jnp.concatenate([kn_ref[...], kp_ref[...]], axis=1)
            s = _dot(q, k, 1, 1)
            if diag:
                s = jnp.where(_diag_mask(t, t, 0), s, NEG)
            m_prev = m_sc[...]
            m_new = jnp.maximum(m_prev, jnp.max(s, axis=1, keepdims=True))
            a = jnp.exp2(m_prev - m_new)
            p = jnp.exp2(s - m_new[:, 0:1])
            l_sc[...] = a * l_sc[...] + jnp.sum(p, axis=1, keepdims=True)
            acc_sc[...] = a * acc_sc[...] + _dot(p.astype(BF16), v_ref[...], 1, 0)
            m_sc[...] = m_new

        @pl.when(j < i)
        def _():
            update(False)

        @pl.when(j == i)
        def _():
            update(True)
            o_ref[...] = (acc_sc[...] / l_sc[...]).astype(BF16)
            lse_ref[...] = m_sc[...] + jnp.log2(l_sc[...])

    qs = pl.BlockSpec((t, 128), lambda h, s, ii, jj: (ii[s], h))
    grid_spec = pltpu.PrefetchScalarGridSpec(
        num_scalar_prefetch=2, grid=(HEADS, int(ii.shape[0])),
        in_specs=[qs, qs, pl.BlockSpec((t, 128), lambda h, s, ii, jj: (jj[s], h)), pl.BlockSpec((t, 128), lambda h, s, ii, jj: (jj[s], 0)),
                  pl.BlockSpec((t, 128), lambda h, s, ii, jj: (jj[s], HEADS + h))],
        out_specs=[qs, qs],
        scratch_shapes=[pltpu.VMEM((t, 128), F32), pltpu.VMEM((t, 128), F32), pltpu.VMEM((t, 128), F32)])
    return _pcall(
        body, grid_spec=grid_spec, out_shape=[jax.ShapeDtypeStruct((T, D), BF16), jax.ShapeDtypeStruct((T, D), F32)], name="attn_fwd",
        compiler_params=pltpu.CompilerParams(dimension_semantics=("arbitrary", "arbitrary")),
    )(ii, jj, qn, qpe, kv, kpe, kv)


def _attn_bwd(qn, qpe, kv, kpe, do, lse, delta, T):
    t = min(ATT_TB, T)
    nb = T // t
    ii, jj = _tri_steps(nb, by_key=True)

    def body(ii_ref, jj_ref, qn_ref, qp_ref, kn_ref, kp_ref, v_ref, do_ref, lse_ref, dl_ref,
             dqn_ref, dqp_ref, dkn_ref, dkp_ref, dv_ref, dk_sc, dv_sc):
        st = pl.program_id(1)
        i, j = ii_ref[st], jj_ref[st]

        @pl.when(st == 0)
        def _():
            dqn_ref[...] = jnp.zeros_like(dqn_ref)
            dqp_ref[...] = jnp.zeros_like(dqp_ref)

        @pl.when(i == j)
        def _():
            dk_sc[...] = jnp.zeros_like(dk_sc)
            dv_sc[...] = jnp.zeros_like(dv_sc)

        def update(diag):
            q = jnp.concatenate([qn_ref[...], qp_ref[...]], axis=1)
            k = jnp.concatenate([kn_ref[...], kp_ref[...]], axis=1)
            dob = do_ref[...]
            s = _dot(q, k, 1, 1)
            if diag:
                s = jnp.where(_diag_mask(t, t, 0), s, NEG)
            p = jnp.exp2(s - lse_ref[:, 0:1])
            dv_sc[...] += _dot(p.astype(BF16), dob, 0, 0)
            dp = _dot(dob, v_ref[...], 1, 1)
            ds = (p * (dp - dl_ref[:, 0:1])).astype(BF16)
            dk_sc[...] += _dot(ds, q, 0, 0)
            dq = _dot(ds, k, 1, 0) * MLA_SCALE
            rows = pl.ds(pl.multiple_of(i * t, t), t)
            dqn_ref[rows, :] += dq[:, :128]
            dqp_ref[rows, :] += dq[:, 128:]

        @pl.when(i > j)
        def _():
            update(False)

        @pl.when(i == j)
        def _():
            update(True)

        @pl.when(i == nb - 1)
        def _():
            dkn_ref[...] = (dk_sc[:, :128] * (1.0 / LOG2E)).astype(BF16)
            dkp_ref[...] = dk_sc[:, 128:] * (1.0 / LOG2E)
            dv_ref[...] = dv_sc[...].astype(BF16)

    qs = pl.BlockSpec((t, 128), lambda h, s, ii, jj: (ii[s], h))
    ks = pl.BlockSpec((t, 128), lambda h, s, ii, jj: (jj[s], h))
    hs = pl.BlockSpec((T, 128), lambda h, s, ii, jj: (0, h))
    grid_spec = pltpu.PrefetchScalarGridSpec(
        num_scalar_prefetch=2, grid=(HEADS, int(ii.shape[0])),
        in_specs=[qs, qs, ks, pl.BlockSpec((t, 128), lambda h, s, ii, jj: (jj[s], 0)),
                  pl.BlockSpec((t, 128), lambda h, s, ii, jj: (jj[s], HEADS + h)), qs, qs, qs],
        out_specs=[hs, hs, ks, ks, ks],
        scratch_shapes=[pltpu.VMEM((t, 256), F32), pltpu.VMEM((t, 128), F32)])
    return _pcall(
        body, grid_spec=grid_spec,
        out_shape=[jax.ShapeDtypeStruct((T, D), F32), jax.ShapeDtypeStruct((T, D), F32), jax.ShapeDtypeStruct((T, D), BF16),
                   jax.ShapeDtypeStruct((T, D), F32), jax.ShapeDtypeStruct((T, D), BF16)], name="attn_bwd",
        compiler_params=pltpu.CompilerParams(dimension_semantics=("arbitrary", "arbitrary")),
    )(ii, jj, qn, qpe, kv, kpe, kv, do, lse, delta)


def _mesh_pos():
    x, y, c = lax.axis_index("x"), lax.axis_index("y"), lax.axis_index("c")
    return x, y, c, 4 * x + 2 * y + c


def _peer(x, y, c, k):
    px, py, pc = (x + ((k >> 2) & 1)) % 2, (y + ((k >> 1) & 1)) % 2, (c + (k & 1)) % 2
    return (px, py, pc), 4 * px + 2 * py + pc


_ANY = pl.BlockSpec(memory_space=pl.ANY)


def _rcopy(src, dst, send_sems, recv_sems, k, to):
    return pltpu.make_async_remote_copy(src_ref=src, dst_ref=dst, send_sem=send_sems.at[k], recv_sem=recv_sems.at[k],
                                        device_id=to, device_id_type=pl.DeviceIdType.MESH)


def _all_gather(wsh, ssh):
    def body(w_ref, s_ref, wall_ref, sall_ref, send_sems, recv_sems, loc_sems):
        x, y, c, me = _mesh_pos()
        sib = (x, y, 1 - c)
        chips = [(1 - x, y), (x, 1 - y), (1 - x, 1 - y)]
        slot = lambda px, py, pc: 4 * px + 2 * py + pc
        loc = [pltpu.make_async_copy(w_ref, wall_ref.at[me], loc_sems.at[0]),
               pltpu.make_async_copy(s_ref, sall_ref.at[me], loc_sems.at[1])]
        for cp in loc:
            cp.start()
        sends, fwd_waits = [], []
        for n, (src, dst) in enumerate(((w_ref, wall_ref), (s_ref, sall_ref))):
            o = 7 * n
            sends.append(_rcopy(src, dst.at[me], send_sems, recv_sems, o, sib))
            for j, chip in enumerate(chips):
                sends.append(_rcopy(src, dst.at[me], send_sems, recv_sems, o + 1 + j, (*chip, c)))
        for cp in sends:
            cp.start()
        for n, (src, dst) in enumerate(((w_ref, wall_ref), (s_ref, sall_ref))):
            o = 7 * n
            for j, chip in enumerate(chips):
                got = dst.at[slot(*chip, c)]
                _rcopy(src, got, send_sems, recv_sems, o + 1 + j, sib).wait_recv()
                fw = _rcopy(got, got, send_sems, recv_sems, o + 4 + j, sib)
                fw.start()
                sends.append(fw)
            fwd_waits.append(_rcopy(src, dst.at[slot(x, y, 1 - c)], send_sems, recv_sems, o, sib))
            for j, chip in enumerate(chips):
                fwd_waits.append(_rcopy(src, dst.at[slot(*chip, 1 - c)], send_sems, recv_sems, o + 4 + j, sib))
        for cp in fwd_waits:
            cp.wait_recv()
        for cp in sends:
            cp.wait_send()
        for cp in loc:
            cp.wait()

    return _pcall(
        body, in_specs=[_ANY, _ANY], out_specs=[_ANY, _ANY],
        out_shape=[jax.ShapeDtypeStruct((N_DEV,) + wsh.shape, wsh.dtype), jax.ShapeDtypeStruct((N_DEV,) + ssh.shape, ssh.dtype)],
        scratch_shapes=[pltpu.SemaphoreType.DMA((14,)), pltpu.SemaphoreType.DMA((14,)), pltpu.SemaphoreType.DMA((2,))],
        name="all_gather_weights",
    )(wsh, ssh)


def _exchange_sibling(g4):
    def body(g_ref, r_ref, send_sems, recv_sems):
        x, y, c, _ = _mesh_pos()
        sib = (x, y, 1 - c)
        cps = [_rcopy(g_ref.at[q, 1 - c], r_ref.at[q], send_sems, recv_sems, q, sib) for q in range(4)]
        for cp in cps:
            cp.start()
        for cp in cps:
            cp.wait()

    n, _, R, C = g4.shape
    return _pcall(
        body, in_specs=[_ANY], out_specs=_ANY, out_shape=jax.ShapeDtypeStruct((n, R, C), g4.dtype),
        scratch_shapes=[pltpu.SemaphoreType.DMA((4,)), pltpu.SemaphoreType.DMA((4,))], name="exchange_sibling",
    )(g4)


def _sum_sibling(g4, rsib, cvec):
    n, _, R, C = g4.shape
    tr = _pick(R, (400, 256, 128, 16, 8))

    def body(c_ref, g_ref, r_ref, o_ref):
        o_ref[...] = (g_ref[...].astype(F32) + r_ref[...].astype(F32)).astype(o_ref.dtype)

    grid_spec = pltpu.PrefetchScalarGridSpec(
        num_scalar_prefetch=1, grid=(n, R // tr),
        in_specs=[pl.BlockSpec((None, None, tr, C), lambda q, i, cr: (q, cr[0], i, 0)), pl.BlockSpec((None, tr, C), lambda q, i, cr: (q, i, 0))],
        out_specs=pl.BlockSpec((None, tr, C), lambda q, i, cr: (q, i, 0)))
    return _pcall(body, grid_spec=grid_spec, out_shape=jax.ShapeDtypeStruct((n, R, C), g4.dtype), name="sum_sibling")(cvec, g4, rsib)


def _exchange_chips(part, gsmall):
    def body(p_ref, s_ref, recv_ref, srecv_ref, send_sems, recv_sems, loc_sems):
        x, y, c, me = _mesh_pos()
        myq = 2 * x + y
        loc = [pltpu.make_async_copy(p_ref.at[myq], recv_ref.at[myq], loc_sems.at[0]),
               pltpu.make_async_copy(s_ref, srecv_ref.at[me], loc_sems.at[1])]
        for cp in loc:
            cp.start()
        sends, recvs = [], []
        for j, (px, py) in enumerate([(1 - x, y), (x, 1 - y), (1 - x, 1 - y)]):
            q = 2 * px + py
            sends.append(_rcopy(p_ref.at[q], recv_ref.at[myq], send_sems, recv_sems, j, (px, py, c)))
            recvs.append(_rcopy(p_ref.at[q], recv_ref.at[q], send_sems, recv_sems, j, (px, py, c)))
        for k in range(1, N_DEV):
            to, pidx = _peer(x, y, c, k)
            sends.append(_rcopy(s_ref, srecv_ref.at[me], send_sems, recv_sems, 2 + k, to))
            recvs.append(_rcopy(s_ref, srecv_ref.at[pidx], send_sems, recv_sems, 2 + k, to))
        for cp in sends:
            cp.start()
        for cp in recvs:
            cp.wait_recv()
        for cp in sends:
            cp.wait_send()
        for cp in loc:
            cp.wait()

    return _pcall(
        body, in_specs=[_ANY, _ANY], out_specs=[_ANY, _ANY],
        out_shape=[jax.ShapeDtypeStruct(part.shape, part.dtype), jax.ShapeDtypeStruct((N_DEV,) + gsmall.shape, gsmall.dtype)],
        scratch_shapes=[pltpu.SemaphoreType.DMA((10,)), pltpu.SemaphoreType.DMA((10,)), pltpu.SemaphoreType.DMA((2,))],
        name="exchange_chips",
    )(part, gsmall)


def _sum_slots(recv, name):
    n, R, C = recv.shape
    tr = _pick(R, (400, 256, 128, 16, 8))

    def body(r_ref, o_ref):
        acc = r_ref[0].astype(F32)
        for s in range(1, n):
            acc = acc + r_ref[s].astype(F32)
        o_ref[...] = acc

    return _pcall(body, grid=(R // tr,), in_specs=[pl.BlockSpec((n, tr, C), lambda i: (0, i, 0))],
                  out_specs=pl.BlockSpec((tr, C), lambda i: (i, 0)), out_shape=jax.ShapeDtypeStruct((R, C), F32), name=name)(recv)


def _adamw(w, g, m, v, name):
    shape = w.shape
    w2, g2, m2, v2 = (a.reshape(-1, shape[-1]) for a in (w, g, m, v))
    R, C = w2.shape
    tr = _pick(R, (256, 128, 64, 32, 16, 8)) if R > 256 else R

    def body(w_ref, g_ref, m_ref, v_ref, d_ref, nm_ref, nv_ref):
        gg = g_ref[...]
        nm = ADAM_B1 * m_ref[...] + (1.0 - ADAM_B1) * gg
        nv = ADAM_B2 * v_ref[...] + (1.0 - ADAM_B2) * jnp.square(gg)
        m_hat = nm / (1.0 - ADAM_B1 ** ADAM_STEP)
        v_hat = nv / (1.0 - ADAM_B2 ** ADAM_STEP)
        d_ref[...] = -ADAM_LR * (m_hat / (jnp.sqrt(v_hat) + ADAM_EPS) + ADAM_WD * w_ref[...])
        nm_ref[...] = nm
        nv_ref[...] = nv

    spec = pl.BlockSpec((tr, C), lambda i: (i, 0))
    d, nm, nv = _pcall(body, grid=(R // tr,), in_specs=[spec] * 4, out_specs=[spec] * 3,
                       out_shape=[jax.ShapeDtypeStruct((R, C), F32)] * 3, name=name)(w2, g2, m2, v2)
    return d.reshape(shape), nm.reshape(shape), nv.reshape(shape)


def _to_rows(name, w):
    w = w[0]
    if name in ("ffn1_w_in", "ffn2_w_in", "w_in"):
        return w.T
    if name in ("w_uq", "w_ukv", "ple_w_proj"):
        return w.T.reshape(-1, 1024)
    return w


def _from_rows(name, g, shape):
    if name in ("ffn1_w_in", "ffn2_w_in", "w_in"):
        return g.T.reshape(shape)
    if name in ("w_uq", "w_ukv", "ple_w_proj"):
        return g.reshape(-1, shape[1]).T.reshape(shape)
    return g.reshape(shape)


def _unpack(wall):
    out, off = {}, 0
    for name, r in PACK:
        out[name] = wall[:, off:off + r, :].reshape(N_DEV * r, 1024)
        off += _pad16(r)
    return out


def _w_in_internal(wt):
    return jnp.concatenate([wt[0:6144], wt[6720:8768], wt[6144:6720], jnp.zeros((P_W - W_IN_COLS, 1024), wt.dtype)], axis=0)


def _w_in_external(d):
    return jnp.concatenate([d[0:6144], d[8192:8768], d[6144:8192]], axis=0)


def _rope_tables(positions):
    pos = positions[0].astype(F32)

    def cs(half):
        inv = ROPE_BASE ** (-jnp.arange(half, dtype=F32) / half)
        ang = pos[:, None] * inv
        return jnp.cos(ang), jnp.sin(ang)

    c, s = cs(64)
    c2, s2 = cs(32)
    return (jnp.concatenate([c, c], axis=1), jnp.concatenate([-s, s], axis=1),
            jnp.concatenate([c2, c2, c2, c2], axis=1), jnp.concatenate([-s2, s2, -s2, s2], axis=1))


def _local_step(x, p, positions, target, W, ln_g, ln_b, gn_g, qg, kvg):
    T = x.shape[0]
    tabs = _rope_tables(positions)
    rc = _ret_consts()
    lg = [ln_g[i:i + 1] for i in range(4)]
    lb = [ln_b[i:i + 1] for i in range(4)]
    xb = x.astype(BF16)
    pb = p.astype(BF16)
    w_in_t = _w_in_internal(W["w_in"])
    wuq = W["w_uq"].reshape(1536, LORA).reshape(HEADS, 192, LORA)
    wuq = jnp.concatenate([wuq[:, :128].reshape(1024, LORA), wuq[:, 128:].reshape(512, LORA)], axis=0)
    wukv = W["w_ukv"].reshape(2048, LORA).reshape(HEADS, 2, 128, LORA).transpose(1, 0, 2, 3).reshape(2048, LORA)
    wp_t = W["ple_w_proj"].reshape(1024, D_PLE)

    hs1, gu1 = _ffn_in(xb, W["ffn1_w_in"], "ffn1_in")
    f1, h1, h1b = _mm(hs1, W["ffn1_w_out"], name="ffn1_out", tm=LN_TM, epilogue=_ln_epilogue(x, 0.5, lg[0], lb[0]))
    proj = _mm(h1b, w_in_t, tb=True, name="mixer_in")
    rq, rk, rv, qn, kvn, kpe = _mixer_prep_fwd(proj, tabs, qg, kvg, T)
    y, yr, states = _ret_fwd(rq, rk, rv, proj, gn_g, rc, T)
    y_ret = _mm(yr, W["w_ret_o"], name="ret_o")
    q = _mm(qn, wuq, tb=True, name="mla_uq")
    kv = _mm(kvn, wukv, tb=True, out_dtype=BF16, name="mla_ukv")
    qnope, qpe = _q_assemble_fwd(q, tabs, T)
    o, lse = _attn_fwd(qnope, qpe, kv, kpe, T)
    y_mla = _mm(o, W["w_mla_o"], name="mla_o")
    mix = _mix_fwd(proj, y_ret, y_mla, T)
    mixed, h2, h2b = _mm(mix, W["w_out"], name="mixer_out", tm=LN_TM, epilogue=_ln_epilogue(h1, 1.0, lg[1], lb[1]))
    hs2, gu2 = _ffn_in(h2b, W["ffn2_w_in"], "ffn2_in")
    f2, h3, h3b = _mm(hs2, W["ffn2_w_out"], name="ffn2_out", tm=LN_TM, epilogue=_ln_epilogue(h2, 0.5, lg[2], lb[2]))
    gl = _mm(h3b, W["ple_w_gate"], name="ple_gate")
    pp = _mm(pb, wp_t, tb=True, name="ple_proj")

    G = {}
    dh3_a, dgl, dpp, dg3, db3, loss = _head(h3, gl, pp, target, lg[3], lb[3], T)
    G["ple_w_gate"] = _mm(h3b, dgl, ta=True, name="d_ple_gate")
    G["ple_w_proj"] = _mm(dpp, pb, ta=True, name="d_ple_proj")
    dh2_a, df2, dg2, db2 = _mm(dgl, W["ple_w_gate"], tb=True, add=dh3_a, name="dh3", tm=LN_TM,
                               epilogue=_ln_bwd_epilogue(h2, f2, 0.5, lg[2], lb[2]))
    G["ffn2_w_out"] = _mm(hs2, df2, ta=True, name="d_ffn2_out")
    da2 = _ffn_act_bwd(df2, W["ffn2_w_out"], gu2, "ffn2_act_bwd")
    G["ffn2_w_in"] = _mm(da2, h2b, ta=True, name="d_ffn2_in")
    dh1_a, dmixed, dg1, db1 = _mm(da2, W["ffn2_w_in"], add=dh2_a, name="dh2", tm=LN_TM,
                                  epilogue=_ln_bwd_epilogue(h1, mixed, 1.0, lg[1], lb[1]))
    G["w_out"] = _mm(mix, dmixed, ta=True, name="d_mixer_out")
    dmix = _mm(dmixed, W["w_out"], tb=True, name="dmix")
    dgr, dgm, dy_ret, dy_mla = _mix_bwd(proj, y_ret, y_mla, dmix, T)
    G["w_mla_o"] = _mm(o, dy_mla, ta=True, name="d_mla_o")
    do = _mm(dy_mla, W["w_mla_o"], tb=True, name="do")
    dob, delta = _attn_delta(do, o, T)
    dqn_f, dqpe_f, dkn, dkpe_all, dv = _attn_bwd(qnope, qpe, kv, kpe, dob, lse, delta, T)
    dq_n, dq_r = _q_assemble_bwd(dqn_f, dqpe_f, tabs, T)
    g_uq = jnp.concatenate([_mm(dq_n, qn, ta=True, name="d_uq_nope"), _mm(dq_r, qn, ta=True, name="d_uq_rope")], axis=0)
    g_uq = jnp.concatenate([g_uq[:1024].reshape(HEADS, 128, LORA), g_uq[1024:].reshape(HEADS, 64, LORA)], axis=1)
    G["w_uq"] = g_uq.reshape(1536 * LORA // 1024, 1024)
    dqn = _mm(dq_r, wuq[1024:], add=_mm(dq_n, wuq[:1024], name="dqn_a"), name="dqn_b")
    g_ukv = jnp.stack([_mm(dkn, kvn, ta=True, name="d_ukv_k"), _mm(dv, kvn, ta=True, name="d_ukv_v")], axis=0)
    G["w_ukv"] = g_ukv.reshape(2, HEADS, 128, LORA).transpose(1, 0, 2, 3).reshape(2048 * LORA // 1024, 1024)
    dkvn = _mm(dv, wukv[1024:], add=_mm(dkn, wukv[:1024], name="dkvn_a"), name="dkvn_b")
    dcq, dckv, dkpe, dqg, dkvg = _rms_bwd(proj, dqn, dkvn, dkpe_all, tabs, qg, kvg, T)
    G["w_ret_o"] = _mm(yr, dy_ret, ta=True, name="d_ret_o")
    dyr = _mm(dy_ret, W["w_ret_o"], tb=True, name="dyr")
    drq, drk, drv, drg, dgn = _ret_bwd(rq, rk, rv, y, proj, gn_g, states, dyr, rc, tabs, T)
    dproj = jnp.concatenate([drq, drk, drv, drg, dgr, dgm, dcq, dckv, dkpe, jnp.zeros((T, P_W - P_KPE - 128), BF16)], axis=1)
    G["w_in"] = _w_in_external(_mm(dproj, h1b, ta=True, name="d_mixer_in"))
    dx_a, df1, dg0, db0 = _mm(dproj, w_in_t, add=dh1_a, name="dh1", tm=LN_TM,
                              epilogue=_ln_bwd_epilogue(x, f1, 0.5, lg[0], lb[0]))
    G["ffn1_w_out"] = _mm(hs1, df1, ta=True, name="d_ffn1_out")
    da1 = _ffn_act_bwd(df1, W["ffn1_w_out"], gu1, "ffn1_act_bwd")
    G["ffn1_w_in"] = _mm(da1, xb, ta=True, name="d_ffn1_in")
    grad_x = _mm(da1, W["ffn1_w_in"], add=dx_a, name="grad_x")

    small = dict(ln_g=jnp.concatenate([dg0, dg1, dg2, dg3], axis=0), ln_b=jnp.concatenate([db0, db1, db2, db3], axis=0),
                 ret_gn_g=dgn, q_norm_g=dqg, kv_norm_g=dkvg)
    return loss, grad_x, G, small


def kernel(x, p, positions, ln_g, ln_b, ffn1_w_in, ffn1_w_out, w_in, ret_gn_g, w_ret_o, q_norm_g, kv_norm_g, w_uq, w_ukv, w_mla_o, w_out, ffn2_w_in, ffn2_w_out, ple_w_gate, ple_w_proj, loss_target, m_ln_g, m_ln_b, m_ffn1_w_in, m_ffn1_w_out, m_w_in, m_ret_gn_g, m_w_ret_o, m_q_norm_g, m_kv_norm_g, m_w_uq, m_w_ukv, m_w_mla_o, m_w_out, m_ffn2_w_in, m_ffn2_w_out, m_ple_w_gate, m_ple_w_proj, v_ln_g, v_ln_b, v_ffn1_w_in, v_ffn1_w_out, v_w_in, v_ret_gn_g, v_w_ret_o, v_q_norm_g, v_kv_norm_g, v_w_uq, v_w_ukv, v_w_mla_o, v_w_out, v_ffn2_w_in, v_ffn2_w_out, v_ple_w_gate, v_ple_w_proj):
    names = ("ln_g", "ln_b", "ffn1_w_in", "ffn1_w_out", "w_in", "ret_gn_g", "w_ret_o", "q_norm_g", "kv_norm_g", "w_uq", "w_ukv",
             "w_mla_o", "w_out", "ffn2_w_in", "ffn2_w_out", "ple_w_gate", "ple_w_proj")
    ws = dict(zip(names, (ln_g, ln_b, ffn1_w_in, ffn1_w_out, w_in, ret_gn_g, w_ret_o, q_norm_g, kv_norm_g, w_uq, w_ukv, w_mla_o,
                          w_out, ffn2_w_in, ffn2_w_out, ple_w_gate, ple_w_proj)))
    ms = dict(zip(names, (m_ln_g, m_ln_b, m_ffn1_w_in, m_ffn1_w_out, m_w_in, m_ret_gn_g, m_w_ret_o, m_q_norm_g, m_kv_norm_g, m_w_uq,
                          m_w_ukv, m_w_mla_o, m_w_out, m_ffn2_w_in, m_ffn2_w_out, m_ple_w_gate, m_ple_w_proj)))
    vs = dict(zip(names, (v_ln_g, v_ln_b, v_ffn1_w_in, v_ffn1_w_out, v_w_in, v_ret_gn_g, v_w_ret_o, v_q_norm_g, v_kv_norm_g, v_w_uq,
                          v_w_ukv, v_w_mla_o, v_w_out, v_ffn2_w_in, v_ffn2_w_out, v_ple_w_gate, v_ple_w_proj)))

    parts = []
    for name, r in PACK:
        rows = _to_rows(name, ws[name])
        if _pad16(r) != r:
            rows = jnp.concatenate([rows, jnp.zeros((_pad16(r) - r, 1024), F32)], axis=0)
        parts.append(rows)
    wsh = jnp.concatenate(parts, axis=0).astype(BF16)
    ssh = jnp.concatenate([ln_g[0], ln_b[0]], axis=0)
    wall, sall = _all_gather(wsh, ssh)
    W = _unpack(wall)
    ln_full = sall.reshape(N_DEV, 2, 4, 128).transpose(1, 2, 0, 3).reshape(2, 4, 1024)

    loss_p, grad_x, G, small = _local_step(x[0], p[0, 0], positions, loss_target[0], W, ln_full[0], ln_full[1],
                                           ret_gn_g, q_norm_g, kv_norm_g)

    gparts = []
    for name, r in PACK:
        g = G[name].reshape(N_DEV, r, 1024)
        if _pad16(r) != r:
            g = jnp.concatenate([g, jnp.zeros((N_DEV, _pad16(r) - r, 1024), F32)], axis=1)
        gparts.append(g)
    gfull = jnp.concatenate(gparts, axis=1).astype(BF16)
    pad256 = lambda a: jnp.concatenate([a, jnp.zeros((1, 1024 - a.shape[1]), F32)], axis=1)
    gsmall = jnp.concatenate([small["ln_g"], small["ln_b"], small["ret_gn_g"].reshape(2, 1024), pad256(small["q_norm_g"]),
                              pad256(small["kv_norm_g"]), jnp.zeros((SMALL_ROWS - 12, 1024), F32)], axis=0)
    g4 = gfull.reshape(4, 2, PACK_ROWS, 1024)
    cvec = lax.axis_index("c").astype(jnp.int32).reshape(1)
    part = _sum_sibling(g4, _exchange_sibling(g4), cvec)
    recv, srecv = _exchange_chips(part, gsmall)
    gsh = _sum_slots(recv, "sum_grads")
    ssum = _sum_slots(srecv, "sum_small_grads")

    grads = {}
    off = 0
    for name, r in PACK:
        grads[name] = _from_rows(name, gsh[off:off + r], ws[name].shape)
        off += _pad16(r)
    me = 4 * lax.axis_index("x") + 2 * lax.axis_index("y") + lax.axis_index("c")
    grads["ln_g"] = lax.dynamic_slice(ssum[0:4], (0, me * 128), (4, 128)).reshape(1, 4, 128)
    grads["ln_b"] = lax.dynamic_slice(ssum[4:8], (0, me * 128), (4, 128)).reshape(1, 4, 128)
    grads["ret_gn_g"] = ssum[8:10].reshape(1, 2048)
    grads["q_norm_g"] = ssum[10:11, :256]
    grads["kv_norm_g"] = ssum[11:12, :256]

    delta, new_m, new_v = {}, {}, {}
    for name in names:
        delta[name], new_m[name], new_v[name] = _adamw(ws[name], grads[name], ms[name], vs[name], "adamw_" + name)

    loss = lax.psum(loss_p[0, 0], ("x", "y", "c"))
    return (loss, grad_x[None], *[grads[n] for n in names], *[delta[n] for n in names],
            *[new_m[n] for n in names], *[new_v[n] for n in names])
```

```python
import functools
import math

import jax
import jax.numpy as jnp
from jax import lax
from jax.experimental import pallas as pl
from jax.experimental.pallas import tpu as pltpu

F32 = jnp.float32
BF16 = jnp.bfloat16

N_DEV = 8
D = 1024
D_FF = 2816
D_PLE = 256
CHUNK = 64
HEADS = 8
RET_DK = 128
RET_DV = 256
MLA_NOPE = 128
MLA_ROPE = 64
MLA_DV = 128
LORA = 256
ROPE_BASE = 10000.0
EPS = 1e-5
ALPHA = 2.0 ** 0.25
RET_SCALE = RET_DK ** -0.5
MLA_SCALE = (MLA_NOPE + MLA_ROPE) ** -0.5
NEG = -1e30

ADAM_LR = 0.001
ADAM_B1 = 0.9
ADAM_B2 = 0.999
ADAM_EPS = 1e-08
ADAM_WD = 0.01
ADAM_STEP = 10

P_RQ, P_RK, P_RV, P_RG, P_GR, P_GM, P_CQ, P_CKV, P_KPE, P_W = 0, 1024, 2048, 4096, 6144, 7168, 8192, 8448, 8704, 8960
W_IN_COLS = 8768
RET_L = 256
ATT_TF = 1024
ATT_TB = 1024
ATT_HP = 2
LOG2E = math.log2(math.e)
Q_PRESCALE = MLA_SCALE * LOG2E

PACK = (("ffn1_w_in", 704), ("ffn1_w_out", 352), ("w_in", 1096), ("w_ret_o", 256), ("w_uq", 48), ("w_ukv", 64),
        ("w_mla_o", 128), ("w_out", 128), ("ffn2_w_in", 704), ("ffn2_w_out", 352), ("ple_w_gate", 128), ("ple_w_proj", 32))


def _pad16(r):
    return -(-r // 16) * 16


PACK_ROWS = sum(_pad16(r) for _, r in PACK)
PACK_LATE = PACK[:2]
PACK_EARLY = PACK[2:]
SMALL_ROWS = 16


def _pcall(body, **kw):
    return pl.pallas_call(body, **kw)


def _pick(dim, prefs):
    for p in prefs:
        if dim % p == 0:
            return p
    return dim


def _sigmoid(x):
    return 1.0 / (1.0 + jnp.exp(-x))


def _silu(x):
    return x * _sigmoid(x)


def _ln(r, g, b):
    mu = jnp.mean(r, axis=-1, keepdims=True)
    var = jnp.mean(jnp.square(r - mu), axis=-1, keepdims=True)
    return (r - mu) * lax.rsqrt(var + EPS) * g + b


def _rms(x, g):
    return x * lax.rsqrt(jnp.mean(jnp.square(x), axis=-1, keepdims=True) + EPS) * g


def _dot(a, b, ca, cb):
    return lax.dot_general(a, b, (((ca,), (cb,)), ((), ())), preferred_element_type=F32)


def _accum(ref, val, first=None):
    @pl.when(pl.program_id(0) == 0 if first is None else first)
    def _():
        ref[...] = jnp.zeros_like(ref)

    ref[...] += val


def _mm(a, b, *, ta=False, tb=False, add=None, out_dtype=F32, name, tm=None, tn=None, tk=None, epilogue=None):
    parts = a.shape[0] if a.ndim == 3 else 1
    ar, ac = a.shape[-2], a.shape[-1]
    if ta:
        K, M = ar, ac * parts
    else:
        M, K = ar, ac * parts
    if tb:
        N, K2 = b.shape
    else:
        K2, N = b.shape
    assert K == K2, (a.shape, b.shape, ta, tb)
    big = (1024, 1408, 1280, 768, 512, 256, 128)
    tm = tm or _pick(ac if (ta and parts > 1) else M, big)
    tn = tn or (N if N <= 1024 else _pick(N, big))
    kdim = ac if (not ta and parts > 1) else K
    tk = tk or (kdim if kdim <= 2816 and parts == 1 else
                _pick(kdim, (2048, 1408, 1280, 1024, 512) if tm <= 1024 else (1024, 1408, 1280, 512)))
    nk = K // tk
    grid = (M // tm, N // tn, nk)
    if parts > 1 and ta:
        per = ac // tm
        a_spec = pl.BlockSpec((None, tk, tm), lambda i, j, k: (i // per, k, i % per))
    elif parts > 1:
        per = ac // tk
        a_spec = pl.BlockSpec((None, tm, tk), lambda i, j, k: (k // per, i, k % per))
    else:
        a_spec = pl.BlockSpec((tk, tm), lambda i, j, k: (k, i)) if ta else pl.BlockSpec((tm, tk), lambda i, j, k: (i, k))
    b_spec = pl.BlockSpec((tn, tk), lambda i, j, k: (j, k)) if tb else pl.BlockSpec((tk, tn), lambda i, j, k: (k, j))
    o_spec = pl.BlockSpec((tm, tn), lambda i, j, k: (i, j))
    ca, cb = (0 if ta else 1), (1 if tb else 0)
    has_add = add is not None
    n_in = 2 + int(has_add)
    if epilogue is not None:
        assert tn == N and not ta
        ep_fn, ep_rows, ep_whole, ep_outs, ep_accs = epilogue
        n_ep_in = len(ep_rows) + len(ep_whole)
        n_out = len(ep_outs) + len(ep_accs)
    else:
        n_ep_in, n_out = 0, 1

    def body(*refs):
        a_ref, b_ref = refs[0], refs[1]
        add_ref = refs[2] if has_add else None
        o_ref = refs[n_in + n_ep_in]
        first_row_tile = pl.program_id(0) == 0

        def finish(r):
            if has_add:
                r = r + add_ref[...].astype(F32)
            if epilogue is not None:
                ep_fn(r, refs[n_in:n_in + n_ep_in], refs[n_in + n_ep_in:n_in + n_ep_in + n_out], first_row_tile)
            else:
                o_ref[...] = r.astype(out_dtype)

        if nk == 1:
            finish(_dot(a_ref[...], b_ref[...], ca, cb))
            return
        acc_ref = refs[-1]
        k = pl.program_id(2)

        @pl.when(k == 0)
        def _():
            acc_ref[...] = jnp.zeros_like(acc_ref)

        acc_ref[...] += _dot(a_ref[...], b_ref[...], ca, cb)

        @pl.when(k == nk - 1)
        def _():
            finish(acc_ref[...])

    in_specs = [a_spec, b_spec] + ([o_spec] if has_add else [])
    args = (a, b) + ((add,) if has_add else ())
    out_specs, out_shape = o_spec, jax.ShapeDtypeStruct((M, N), out_dtype)
    if epilogue is not None:
        in_specs += [o_spec] * len(ep_rows) + [pl.BlockSpec(w.shape, lambda i, j, k, _n=w.ndim: (0,) * _n) for w in ep_whole]
        args += tuple(ep_rows) + tuple(ep_whole)
        out_specs = [o_spec] * len(ep_outs) + [pl.BlockSpec((r, N), lambda i, j, k: (0, 0)) for r in ep_accs]
        out_shape = [jax.ShapeDtypeStruct((M, N), dt) for dt in ep_outs] + [jax.ShapeDtypeStruct((r, N), F32) for r in ep_accs]
    return _pcall(
        body, grid=grid, in_specs=in_specs, out_specs=out_specs, out_shape=out_shape,
        scratch_shapes=[pltpu.VMEM((tm, tn), F32)] if nk > 1 else [], name=name,
        compiler_params=pltpu.CompilerParams(dimension_semantics=("arbitrary" if epilogue is not None else "parallel", "parallel", "arbitrary")),
    )(*args)


def _ln_epilogue(res, c, g, b):
    def fn(r, ins, outs, first):
        res_ref, g_ref, b_ref = ins
        f_ref, h_ref, hb_ref = outs
        h = _ln(ALPHA * res_ref[...] + c * r, g_ref[...], b_ref[...])
        f_ref[...] = r
        h_ref[...] = h
        hb_ref[...] = h.astype(BF16)

    return (fn, [res], [g, b], (F32, F32, BF16), ())


def _ln_bwd_epilogue(res, f, c, g, b):
    def fn(r, ins, outs, first):
        res_ref, f_ref, g_ref, b_ref = ins
        dr_ref, df_ref, dg_ref, db_ref = outs
        _, vjp = jax.vjp(lambda rr, ff, gg, bb: _ln(ALPHA * rr + c * ff, gg, bb), res_ref[...], f_ref[...], g_ref[...], b_ref[...])
        dr, df, dg, db = vjp(r)
        dr_ref[...] = dr
        df_ref[...] = df.astype(BF16)
        _accum(dg_ref, dg, first)
        _accum(db_ref, db, first)

    return (fn, [res, f], [g, b], (F32, BF16), (1, 1))


def _rows(body, T, tm, ins, outs, name, accs=()):
    in_specs, args = [], []
    for arr, w, cb in ins:
        if w is None:
            in_specs.append(pl.BlockSpec(arr.shape, lambda i, _n=arr.ndim: (0,) * _n))
        else:
            in_specs.append(pl.BlockSpec((tm, w), lambda i, _cb=cb: (i, _cb)))
        args.append(arr)
    out_specs = [pl.BlockSpec((tm, w), lambda i: (i, 0)) for w, _ in outs]
    out_shape = [jax.ShapeDtypeStruct((T, w), dt) for w, dt in outs]
    for r, w in accs:
        out_specs.append(pl.BlockSpec((r, w), lambda i: (0, 0)))
        out_shape.append(jax.ShapeDtypeStruct((r, w), F32))
    return _pcall(
        body, grid=(T // tm,), in_specs=in_specs, out_specs=out_specs, out_shape=out_shape, name=name,
        compiler_params=pltpu.CompilerParams(dimension_semantics=("arbitrary",)),
    )(*args)


FFN_TN = 1408
FFN_TM = 512
LN_TM = 512


def _ffn_in(xb, wt, name):
    T = xb.shape[0]
    tm, tn = min(FFN_TM, T), FFN_TN
    nj = D_FF // tn

    def body(x_ref, wg_ref, wu_ref, hs_ref, gu_ref):
        x = x_ref[...]
        g = _dot(x, wg_ref[...], 1, 1)
        u = _dot(x, wu_ref[...], 1, 1)
        hs_ref[...] = (_silu(g) * u).astype(BF16)
        gu_ref[0] = g.astype(BF16)
        gu_ref[1] = u.astype(BF16)

    return _pcall(
        body, grid=(T // tm, nj),
        in_specs=[pl.BlockSpec((tm, D), lambda i, j: (i, 0)), pl.BlockSpec((tn, D), lambda i, j: (j, 0)),
                  pl.BlockSpec((tn, D), lambda i, j: (j + nj, 0))],
        out_specs=[pl.BlockSpec((tm, tn), lambda i, j: (i, j)), pl.BlockSpec((2, tm, tn), lambda i, j: (0, i, j))],
        out_shape=[jax.ShapeDtypeStruct((T, D_FF), BF16), jax.ShapeDtypeStruct((2, T, D_FF), BF16)], name=name,
        compiler_params=pltpu.CompilerParams(dimension_semantics=("parallel", "parallel")),
    )(xb, wt, wt)


def _ffn_act_bwd(df, wo, gu, name):
    T = df.shape[0]
    tm, tn = min(FFN_TM, T), FFN_TN

    def body(d_ref, w_ref, gu_ref, o_ref):
        dhs = _dot(d_ref[...], w_ref[...], 1, 1)
        _, vjp = jax.vjp(lambda g, u: _silu(g) * u, gu_ref[0].astype(F32), gu_ref[1].astype(F32))
        dg, du = vjp(dhs)
        o_ref[0] = dg.astype(BF16)
        o_ref[1] = du.astype(BF16)

    return _pcall(
        body, grid=(T // tm, D_FF // tn),
        in_specs=[pl.BlockSpec((tm, D), lambda i, j: (i, 0)), pl.BlockSpec((tn, D), lambda i, j: (j, 0)),
                  pl.BlockSpec((2, tm, tn), lambda i, j: (0, i, j))],
        out_specs=pl.BlockSpec((2, tm, tn), lambda i, j: (0, i, j)),
        out_shape=jax.ShapeDtypeStruct((2, T, D_FF), BF16), name=name,
        compiler_params=pltpu.CompilerParams(dimension_semantics=("parallel", "parallel")),
    )(df, wo, gu)


def _rope128(t, cos, sin_s):
    return t * cos + pltpu.roll(t, 64, 1) * sin_s


def _rope128_t(g, cos, sin_s):
    return g * cos - pltpu.roll(g, 64, 1) * sin_s


def _partner32(t):
    lane = lax.broadcasted_iota(jnp.int32, t.shape, 1)
    return jnp.where((lane & 32) == 0, pltpu.roll(t, 96, 1), pltpu.roll(t, 32, 1))


def _rope64(t, cos, sin_s):
    return t * cos + _partner32(t) * sin_s


def _rope64_t(g, cos, sin_s):
    return g * cos - _partner32(g) * sin_s


def _mixer_prep_fwd(proj, tabs, qg, kvg, T):
    cos128, sin128, cos64, sin64 = tabs

    def body(rq_ref, rk_ref, rv_ref, cq_ref, ckv_ref, kpe_ref, c1_ref, s1_ref, c2_ref, s2_ref, qg_ref, kvg_ref,
             oq_ref, ok_ref, ov_ref, oqn_ref, okvn_ref, okpe_ref):
        c1, s1 = c1_ref[...], s1_ref[...]
        for h in range(HEADS):
            sl = slice(h * RET_DK, (h + 1) * RET_DK)
            oq_ref[:, sl] = _rope128(rq_ref[:, sl], c1, s1).astype(BF16)
            ok_ref[:, sl] = (_rope128(rk_ref[:, sl], c1, s1) * RET_SCALE).astype(BF16)
        ov_ref[...] = rv_ref[...].astype(BF16)
        oqn_ref[...] = _rms(cq_ref[...], qg_ref[...]).astype(BF16)
        okvn_ref[...] = _rms(ckv_ref[...], kvg_ref[...]).astype(BF16)
        okpe_ref[...] = _rope64(kpe_ref[...], c2_ref[...], s2_ref[...]).astype(BF16)

    ins = [(proj, 1024, 0), (proj, 1024, 1), (proj, 2048, 1), (proj, 256, P_CQ // 256), (proj, 256, P_CKV // 256),
           (proj, 128, P_KPE // 128), (cos128, 128, 0), (sin128, 128, 0), (cos64, 128, 0), (sin64, 128, 0),
           (qg, None, None), (kvg, None, None)]
    outs = [(1024, BF16), (1024, BF16), (2048, BF16), (LORA, BF16), (LORA, BF16), (128, BF16)]
    return _rows(body, T, 256, ins, outs, "mixer_prep_fwd")


def _rms_bwd(proj, dqn, dkvn, dkpe_all, tabs, qg, kvg, T):
    _, _, cos64, sin64 = tabs

    def body(cq_ref, ckv_ref, dq_ref, dkv_ref, dk_ref, c2_ref, s2_ref, qg_ref, kvg_ref, ocq_ref, ockv_ref, okpe_ref,
             dqg_ref, dkvg_ref):
        _, vjp = jax.vjp(_rms, cq_ref[...], qg_ref[...])
        dx, dg = vjp(dq_ref[...])
        ocq_ref[...] = dx.astype(BF16)
        _accum(dqg_ref, dg)
        _, vjp = jax.vjp(_rms, ckv_ref[...], kvg_ref[...])
        dx, dg = vjp(dkv_ref[...])
        ockv_ref[...] = dx.astype(BF16)
        _accum(dkvg_ref, dg)
        g = dk_ref[:, 0:128]
        for h in range(1, HEADS):
            g = g + dk_ref[:, h * 128:(h + 1) * 128]
        lane = lax.broadcasted_iota(jnp.int32, g.shape, 1)
        g = jnp.where(lane < MLA_ROPE, g, 0.0)
        okpe_ref[...] = _rope64_t(g, c2_ref[...], s2_ref[...]).astype(BF16)

    ins = [(proj, 256, P_CQ // 256), (proj, 256, P_CKV // 256), (dqn, LORA, 0), (dkvn, LORA, 0), (dkpe_all, 1024, 0),
           (cos64, 128, 0), (sin64, 128, 0), (qg, None, None), (kvg, None, None)]
    return _rows(body, T, 256, ins, [(LORA, BF16), (LORA, BF16), (128, BF16)], "rms_bwd", accs=[(1, LORA), (1, LORA)])


def _gn_gate(y, rg, g):
    mu = jnp.mean(y, axis=-1, keepdims=True)
    var = jnp.mean(jnp.square(y - mu), axis=-1, keepdims=True)
    return _silu(rg) * ((y - mu) * lax.rsqrt(var + EPS) * g)


def _q_assemble_fwd(q, tabs, T):
    _, _, cos64, sin64 = tabs

    def body(q_ref, c_ref, s_ref, on_ref, op_ref):
        on_ref[...] = (q_ref[:, :1024] * Q_PRESCALE).astype(BF16)
        c, s = c_ref[...], s_ref[...]
        lane = lax.broadcasted_iota(jnp.int32, c.shape, 1)
        for j in range(HEADS // 2):
            r = _rope64(q_ref[:, 1024 + 128 * j:1024 + 128 * (j + 1)], c, s) * Q_PRESCALE
            op_ref[:, 256 * j:256 * j + 128] = jnp.where(lane < 64, r, 0.0).astype(BF16)
            op_ref[:, 256 * j + 128:256 * j + 256] = jnp.where(lane < 64, pltpu.roll(r, 64, 1), 0.0).astype(BF16)

    return _rows(body, T, 256, [(q, 1536, 0), (cos64, 128, 0), (sin64, 128, 0)], [(1024, BF16), (1024, BF16)], "q_assemble_fwd")


def _q_assemble_bwd(dqn, dqpe, tabs, T):
    _, _, cos64, sin64 = tabs

    def body(dn_ref, dp_ref, c_ref, s_ref, on_ref, op_ref):
        on_ref[...] = dn_ref[...].astype(BF16)
        c, s = c_ref[...], s_ref[...]
        lane = lax.broadcasted_iota(jnp.int32, c.shape, 1)
        for j in range(HEADS // 2):
            g = jnp.where(lane < 64, dp_ref[:, 256 * j:256 * j + 128], pltpu.roll(dp_ref[:, 256 * j + 128:256 * j + 256], 64, 1))
            op_ref[:, 128 * j:128 * (j + 1)] = _rope64_t(g, c, s).astype(BF16)

    return _rows(body, T, 256, [(dqn, 1024, 0), (dqpe, 1024, 0), (cos64, 128, 0), (sin64, 128, 0)],
                 [(1024, BF16), (512, BF16)], "q_assemble_bwd")


def _mix_fn(gr, gm, yr, ym):
    return _sigmoid(gr) * yr + _sigmoid(gm) * ym


def _mix_fwd(proj, y_ret, y_mla, T):
    def body(gr_ref, gm_ref, yr_ref, ym_ref, o_ref):
        o_ref[...] = _mix_fn(gr_ref[...], gm_ref[...], yr_ref[...], ym_ref[...]).astype(BF16)

    return _rows(body, T, 256, [(proj, 1024, P_GR // 1024), (proj, 1024, P_GM // 1024), (y_ret, D, 0), (y_mla, D, 0)],
                 [(D, BF16)], "mix_fwd")[0]


def _mix_bwd(proj, y_ret, y_mla, dmix, T):
    def body(gr_ref, gm_ref, yr_ref, ym_ref, d_ref, dgr_ref, dgm_ref, dyr_ref, dym_ref):
        _, vjp = jax.vjp(_mix_fn, gr_ref[...], gm_ref[...], yr_ref[...], ym_ref[...])
        dgr, dgm, dyr, dym = vjp(d_ref[...])
        dgr_ref[...] = dgr.astype(BF16)
        dgm_ref[...] = dgm.astype(BF16)
        dyr_ref[...] = dyr.astype(BF16)
        dym_ref[...] = dym.astype(BF16)

    return _rows(body, T, 256, [(proj, 1024, P_GR // 1024), (proj, 1024, P_GM // 1024), (y_ret, D, 0), (y_mla, D, 0), (dmix, D, 0)],
                 [(D, BF16)] * 4, "mix_bwd")


def _head(h3, gl, pp, tgt, g, b, T):
    def body(h_ref, gl_ref, pp_ref, t_ref, g_ref, b_ref, dh_ref, dgl_ref, dpp_ref, dg_ref, db_ref, loss_ref):
        def f(h, gg, p, lg, lb):
            return _ln(ALPHA * h + _sigmoid(gg) * p, lg, lb)

        h4, vjp = jax.vjp(f, h_ref[...], gl_ref[...], pp_ref[...], g_ref[...], b_ref[...])
        err = h4 - t_ref[...]
        dh, dgl, dpp, dg, db = vjp(err * (1.0 / D))
        dh_ref[...] = dh
        dgl_ref[...] = dgl.astype(BF16)
        dpp_ref[...] = dpp.astype(BF16)
        _accum(dg_ref, dg)
        _accum(db_ref, db)
        part = 0.5 * jnp.sum(jnp.mean(jnp.square(err), axis=-1, keepdims=True), axis=0, keepdims=True)
        _accum(loss_ref, jnp.broadcast_to(part, loss_ref.shape))

    return _rows(body, T, 256, [(h3, D, 0), (gl, D, 0), (pp, D, 0), (tgt, D, 0), (g, None, None), (b, None, None)],
                 [(D, F32), (D, BF16), (D, BF16)], "head", accs=[(1, D), (1, D), (8, 128)])


def _attn_delta(do, o, T):
    def body(d_ref, o_ref, db_ref, dl_ref):
        for h in range(HEADS):
            sl = slice(h * MLA_DV, (h + 1) * MLA_DV)
            d = d_ref[:, sl]
            dl = jnp.sum(d * o_ref[:, sl].astype(F32), axis=-1, keepdims=True)
            dl_ref[:, sl] = jnp.broadcast_to(dl, d.shape)
        db_ref[...] = d_ref[...].astype(BF16)

    return _rows(body, T, 256, [(do, D, 0), (o, D, 0)], [(D, BF16), (D, F32)], "attn_delta")


def _ret_consts():
    L = RET_L
    lg = jnp.log(1.0 - 2.0 ** (-5.0 - jnp.arange(HEADS, dtype=F32)))[:, None, None]
    idx = jnp.arange(L, dtype=F32)
    ch = jnp.arange(L) // CHUNK
    dist = idx[:, None] - idx[None, :]
    same = (ch[:, None] == ch[None, :])[None]
    earlier = (ch[None, :] < ch[:, None])[None]
    dm = jnp.where(same, jnp.exp(lg * jnp.abs(dist)[None]), jnp.where(earlier, jnp.exp(lg * dist[None]), 0.0))
    xi = jnp.broadcast_to(jnp.exp(lg * (idx + 1.0)[None, :, None]), (HEADS, L, 128))
    zeta = jnp.broadcast_to(jnp.exp(lg * (L - 1.0 - idx)[None, :, None]), (HEADS, L, 128))
    gl = jnp.broadcast_to(jnp.exp(lg * float(L)), (HEADS, 8, 128))
    return dm.astype(F32), xi.astype(F32), zeta.astype(F32), gl.astype(F32)


def _whole(arr):
    return pl.BlockSpec(arr.shape, lambda n, _nd=arr.ndim: (0,) * _nd)


def _ret_fwd(q, k, v, proj, gn_g, consts, T):
    dm, xi, zeta, gl = consts
    L = RET_L
    n_sc = T // L

    def body(q_ref, k_ref, v_ref, rg_ref, g_ref, dm_ref, xi_ref, ze_ref, gl_ref, y_ref, yr_ref, s_ref, st_ref):
        @pl.when(pl.program_id(0) == 0)
        def _():
            st_ref[...] = jnp.zeros_like(st_ref)

        for h in range(HEADS):
            ks, vs = slice(h * RET_DK, (h + 1) * RET_DK), slice(h * RET_DV, (h + 1) * RET_DV)
            qq, kk, vv = q_ref[:, ks], k_ref[:, ks], v_ref[:, vs]
            st = st_ref[h]
            s_ref[h, 0] = st
            p = (_dot(qq, kk, 1, 1) * dm_ref[h]).astype(BF16)
            cross = _dot(qq, st.astype(BF16), 1, 0)
            xi_c = jnp.concatenate([xi_ref[h], xi_ref[h]], axis=1)
            y = _dot(p, vv, 1, 0) + cross * xi_c
            y_ref[:, vs] = y
            yr_ref[:, vs] = _gn_gate(y, rg_ref[:, vs], g_ref[:, vs]).astype(BF16)
            kz = (kk.astype(F32) * ze_ref[h]).astype(BF16)
            gl2 = jnp.concatenate([gl_ref[h, 0:1, :], gl_ref[h, 0:1, :]], axis=1)
            st_ref[h] = st * gl2 + _dot(kz, vv, 0, 0)

    return _pcall(
        body, grid=(n_sc,),
        in_specs=[pl.BlockSpec((L, 1024), lambda n: (n, 0)), pl.BlockSpec((L, 1024), lambda n: (n, 0)),
                  pl.BlockSpec((L, 2048), lambda n: (n, 0)), pl.BlockSpec((L, 2048), lambda n: (n, P_RG // 2048)),
                  _whole(gn_g), _whole(dm), _whole(xi), _whole(zeta), _whole(gl)],
        out_specs=[pl.BlockSpec((L, 2048), lambda n: (n, 0)), pl.BlockSpec((L, 2048), lambda n: (n, 0)),
                   pl.BlockSpec((HEADS, 1, 128, 256), lambda n: (0, n, 0, 0))],
        out_shape=[jax.ShapeDtypeStruct((T, HEADS * RET_DV), F32), jax.ShapeDtypeStruct((T, HEADS * RET_DV), BF16),
                   jax.ShapeDtypeStruct((HEADS, n_sc, 128, 256), F32)],
        scratch_shapes=[pltpu.VMEM((HEADS, 128, 256), F32)], name="ret_fwd",
        compiler_params=pltpu.CompilerParams(dimension_semantics=("arbitrary",)),
    )(q, k, v, proj, gn_g, dm, xi, zeta, gl)


def _ret_bwd(q, k, v, y, proj, gn_g, states, dyr, consts, tabs, T):
    dm, xi, zeta, gl = consts
    cos128, sin128, _, _ = tabs
    L = RET_L
    n_sc = T // L

    def body(q_ref, k_ref, v_ref, y_ref, rg_ref, g_ref, d_ref, s_ref, dm_ref, xi_ref, ze_ref, gl_ref, c_ref, sn_ref,
             dq_ref, dk_ref, dv_ref, drg_ref, dg_ref, gs_ref):
        @pl.when(pl.program_id(0) == 0)
        def _():
            gs_ref[...] = jnp.zeros_like(gs_ref)

        c, sn = c_ref[...], sn_ref[...]
        dgs = []
        for h in range(HEADS):
            ks, vs = slice(h * RET_DK, (h + 1) * RET_DK), slice(h * RET_DV, (h + 1) * RET_DV)
            _, vjp = jax.vjp(_gn_gate, y_ref[:, vs], rg_ref[:, vs], g_ref[:, vs])
            dy, drg, dg = vjp(d_ref[:, vs])
            drg_ref[:, vs] = drg.astype(BF16)
            dgs.append(dg)
            qq, kk, vv, dyy = q_ref[:, ks], k_ref[:, ks], v_ref[:, vs], dy.astype(BF16)
            dmm = dm_ref[h]
            gb = gs_ref[h].astype(BF16)
            sb = s_ref[h, 0].astype(BF16)
            xi_c = jnp.concatenate([xi_ref[h], xi_ref[h]], axis=1)
            pb = (_dot(qq, kk, 1, 1) * dmm).astype(BF16)
            kz = (kk.astype(F32) * ze_ref[h]).astype(BF16)
            dv_ref[:, vs] = (_dot(pb, dyy, 0, 0) + _dot(kz, gb, 1, 0)).astype(BF16)
            da = (_dot(dyy, vv, 1, 1) * dmm).astype(BF16)
            dyx = (dyy.astype(F32) * xi_c).astype(BF16)
            dq = _dot(da, kk, 1, 0) + _dot(dyx, sb, 1, 1)
            dk = _dot(da, qq, 0, 0) + _dot(vv, gb, 1, 1) * ze_ref[h]
            dq_ref[:, ks] = _rope128_t(dq, c, sn).astype(BF16)
            dk_ref[:, ks] = (_rope128_t(dk, c, sn) * RET_SCALE).astype(BF16)
            gl2 = jnp.concatenate([gl_ref[h, 0:1, :], gl_ref[h, 0:1, :]], axis=1)
            gs_ref[h] = gs_ref[h] * gl2 + _dot(qq, dyx, 0, 0)
        _accum(dg_ref, jnp.concatenate(dgs, axis=1))

    rev = lambda n: n_sc - 1 - n
    return _pcall(
        body, grid=(n_sc,),
        in_specs=[pl.BlockSpec((L, 1024), lambda n: (rev(n), 0)), pl.BlockSpec((L, 1024), lambda n: (rev(n), 0)),
                  pl.BlockSpec((L, 2048), lambda n: (rev(n), 0)), pl.BlockSpec((L, 2048), lambda n: (rev(n), 0)),
                  pl.BlockSpec((L, 2048), lambda n: (rev(n), P_RG // 2048)), _whole(gn_g),
                  pl.BlockSpec((L, 2048), lambda n: (rev(n), 0)),
                  pl.BlockSpec((HEADS, 1, 128, 256), lambda n: (0, rev(n), 0, 0)),
                  _whole(dm), _whole(xi), _whole(zeta), _whole(gl),
                  pl.BlockSpec((L, 128), lambda n: (rev(n), 0)), pl.BlockSpec((L, 128), lambda n: (rev(n), 0))],
        out_specs=[pl.BlockSpec((L, 1024), lambda n: (rev(n), 0)), pl.BlockSpec((L, 1024), lambda n: (rev(n), 0)),
                   pl.BlockSpec((L, 2048), lambda n: (rev(n), 0)), pl.BlockSpec((L, 2048), lambda n: (rev(n), 0)),
                   pl.BlockSpec((1, 2048), lambda n: (0, 0))],
        out_shape=[jax.ShapeDtypeStruct((T, 1024), BF16), jax.ShapeDtypeStruct((T, 1024), BF16), jax.ShapeDtypeStruct((T, 2048), BF16),
                   jax.ShapeDtypeStruct((T, 2048), BF16), jax.ShapeDtypeStruct((1, 2048), F32)],
        scratch_shapes=[pltpu.VMEM((HEADS, 128, 256), F32)], name="ret_bwd",
        compiler_params=pltpu.CompilerParams(dimension_semantics=("arbitrary",)),
    )(q, k, v, y, proj, gn_g, dyr, states, dm, xi, zeta, gl, cos128, sin128)


def _diag_mask(nrows, ncols, row0):
    row = row0 + lax.broadcasted_iota(jnp.int32, (nrows, ncols), 0)
    col = lax.broadcasted_iota(jnp.int32, (nrows, ncols), 1)
    return lax.shift_right_logical(col, 6) <= lax.shift_right_logical(row, 6)


def _tri_steps(nb, by_key):
    if by_key:
        pairs = [(i, j) for j in range(nb) for i in range(j, nb)]
    else:
        pairs = [(i, j) for i in range(nb) for j in range(i + 1)]
    return jnp.array([a for a, _ in pairs], jnp.int32), jnp.array([b for _, b in pairs], jnp.int32)


def _attn_fwd(qn, qpe, kv, kpe, T):
    t = min(ATT_TF, T)
    nb = T // t
    ii, jj = _tri_steps(nb, by_key=False)

    hp = ATT_HP
    w = 128 * hp

    def body(ii_ref, jj_ref, qn_ref, qp_ref, kn_ref, kp_ref, v_ref, o_ref, lse_ref, m_sc, l_sc, acc_sc):
        st = pl.program_id(1)
        i, j = ii_ref[st], jj_ref[st]

        @pl.when(j == 0)
        def _():
            m_sc[...] = jnp.full_like(m_sc, NEG)
            l_sc[...] = jnp.zeros_like(l_sc)
            acc_sc[...] = jnp.zeros_like(acc_sc)

        def update(diag):
            kp = kp_ref[...]
            for hh in range(hp):
                sl = slice(128 * hh, 128 * (hh + 1))
                q = jnp.concatenate([qn_ref[:, sl], qp_ref[:, sl]], axis=1)
                k = jnp.concatenate([kn_ref[:, sl], kp], axis=1)
                s = _dot(q, k, 1, 1)
                if diag:
                    s = jnp.where(_diag_mask(t, t, 0), s, NEG)
                m_prev = m_sc[:, sl]
                m_new = jnp.maximum(m_prev, jnp.max(s, axis=1, keepdims=True))
                a = jnp.exp2(m_prev - m_new)
                p = jnp.exp2(s - m_new[:, 0:1])
                l_sc[:, sl] = a * l_sc[:, sl] + jnp.sum(p, axis=1, keepdims=True)
                acc_sc[:, sl] = a * acc_sc[:, sl] + _dot(p.astype(BF16), v_ref[:, sl], 1, 0)
                m_sc[:, sl] = m_new

        @pl.when(j < i)
        def _():
            update(False)

        @pl.when(j == i)
        def _():
            update(True)
            o_ref[...] = (acc_sc[...] / l_sc[...]).astype(BF16)
            lse_ref[...] = m_sc[...] + jnp.log2(l_sc[...])

    qs = pl.BlockSpec((t, w), lambda h, s, ii, jj: (ii[s], h))
    grid_spec = pltpu.PrefetchScalarGridSpec(
        num_scalar_prefetch=2, grid=(HEADS // hp, int(ii.shape[0])),
        in_specs=[qs, qs, pl.BlockSpec((t, w), lambda h, s, ii, jj: (jj[s], h)), pl.BlockSpec((t, 128), lambda h, s, ii, jj: (jj[s], 0)),
                  pl.BlockSpec((t, w), lambda h, s, ii, jj: (jj[s], HEADS // hp + h))],
        out_specs=[qs, qs],
        scratch_shapes=[pltpu.VMEM((t, w), F32), pltpu.VMEM((t, w), F32), pltpu.VMEM((t, w), F32)])
    return _pcall(
        body, grid_spec=grid_spec, out_shape=[jax.ShapeDtypeStruct((T, D), BF16), jax.ShapeDtypeStruct((T, D), F32)], name="attn_fwd",
        compiler_params=pltpu.CompilerParams(dimension_semantics=("arbitrary", "arbitrary")),
    )(ii, jj, qn, qpe, kv, kpe, kv)


def _attn_bwd(qn, qpe, kv, kpe, do, lse, delta, T):
    t = min(ATT_TB, T)
    nb = T // t
    ii, jj = _tri_steps(nb, by_key=True)

    def body(ii_ref, jj_ref, qn_ref, qp_ref, kn_ref, kp_ref, v_ref, do_ref, lse_ref, dl_ref,
             dqn_ref, dqp_ref, dkn_ref, dkp_ref, dv_ref, dk_sc, dv_sc):
        st = pl.program_id(1)
        i, j = ii_ref[st], jj_ref[st]

        @pl.when(st == 0)
        def _():
            dqn_ref[...] = jnp.zeros_like(dqn_ref)
            dqp_ref[...] = jnp.zeros_like(dqp_ref)

        @pl.when(i == j)
        def _():
            dk_sc[...] = jnp.zeros_like(dk_sc)
            dv_sc[...] = jnp.zeros_like(dv_sc)

        def update(diag):
            q = jnp.concatenate([qn_ref[...], qp_ref[...]], axis=1)
            k = jnp.concatenate([kn_ref[...], kp_ref[...]], axis=1)
            dob = do_ref[...]
            s = _dot(q, k, 1, 1)
            if diag:
                s = jnp.where(_diag_mask(t, t, 0), s, NEG)
            p = jnp.exp2(s - lse_ref[:, 0:1])
            dv_sc[...] += _dot(p.astype(BF16), dob, 0, 0)
            dp = _dot(dob, v_ref[...], 1, 1)
            ds = (p * (dp - dl_ref[:, 0:1])).astype(BF16)
            dk_sc[...] += _dot(ds, q, 0, 0)
            dq = _dot(ds, k, 1, 0) * MLA_SCALE
            rows = pl.ds(pl.multiple_of(i * t, t), t)
            dqn_ref[rows, :] += dq[:, :128]
            dqp_ref[rows, :] += dq[:, 128:]

        @pl.when(i > j)
        def _():
            update(False)

        @pl.when(i == j)
        def _():
            update(True)

        @pl.when(i == nb - 1)
        def _():
            dkn_ref[...] = (dk_sc[:, :128] * (1.0 / LOG2E)).astype(BF16)
            dkp_ref[...] = dk_sc[:, 128:] * (1.0 / LOG2E)
            dv_ref[...] = dv_sc[...].astype(BF16)

    qs = pl.BlockSpec((t, 128), lambda h, s, ii, jj: (ii[s], h))
    ks = pl.BlockSpec((t, 128), lambda h, s, ii, jj: (jj[s], h))
    hs = pl.BlockSpec((T, 128), lambda h, s, ii, jj: (0, h))
    grid_spec = pltpu.PrefetchScalarGridSpec(
        num_scalar_prefetch=2, grid=(HEADS, int(ii.shape[0])),
        in_specs=[qs, qs, ks, pl.BlockSpec((t, 128), lambda h, s, ii, jj: (jj[s], 0)),
                  pl.BlockSpec((t, 128), lambda h, s, ii, jj: (jj[s], HEADS + h)), qs, qs, qs],
        out_specs=[hs, hs, ks, ks, ks],
        scratch_shapes=[pltpu.VMEM((t, 256), F32), pltpu.VMEM((t, 128), F32)])
    return _pcall(
        body, grid_spec=grid_spec,
        out_shape=[jax.ShapeDtypeStruct((T, D), F32), jax.ShapeDtypeStruct((T, D), F32), jax.ShapeDtypeStruct((T, D), BF16),
                   jax.ShapeDtypeStruct((T, D), F32), jax.ShapeDtypeStruct((T, D), BF16)], name="attn_bwd",
        compiler_params=pltpu.CompilerParams(dimension_semantics=("arbitrary", "arbitrary")),
    )(ii, jj, qn, qpe, kv, kpe, kv, do, lse, delta)


def _mesh_pos():
    x, y, c = lax.axis_index("x"), lax.axis_index("y"), lax.axis_index("c")
    return x, y, c, 4 * x + 2 * y + c


def _peer(x, y, c, k):
    px, py, pc = (x + ((k >> 2) & 1)) % 2, (y + ((k >> 1) & 1)) % 2, (c + (k & 1)) % 2
    return (px, py, pc), 4 * px + 2 * py + pc


_ANY = pl.BlockSpec(memory_space=pl.ANY)


def _rcopy(src, dst, send_sems, recv_sems, k, to):
    return pltpu.make_async_remote_copy(src_ref=src, dst_ref=dst, send_sem=send_sems.at[k], recv_sem=recv_sems.at[k],
                                        device_id=to, device_id_type=pl.DeviceIdType.MESH)


def _all_gather(wsh, ssh):
    def body(w_ref, s_ref, wall_ref, sall_ref, send_sems, recv_sems, loc_sems):
        x, y, c, me = _mesh_pos()
        sib = (x, y, 1 - c)
        chips = [(1 - x, y), (x, 1 - y), (1 - x, 1 - y)]
        slot = lambda px, py, pc: 4 * px + 2 * py + pc
        loc = [pltpu.make_async_copy(w_ref, wall_ref.at[me], loc_sems.at[0]),
               pltpu.make_async_copy(s_ref, sall_ref.at[me], loc_sems.at[1])]
        for cp in loc:
            cp.start()
        sends, fwd_waits = [], []
        for n, (src, dst) in enumerate(((w_ref, wall_ref), (s_ref, sall_ref))):
            o = 7 * n
            sends.append(_rcopy(src, dst.at[me], send_sems, recv_sems, o, sib))
            for j, chip in enumerate(chips):
                sends.append(_rcopy(src, dst.at[me], send_sems, recv_sems, o + 1 + j, (*chip, c)))
        for cp in sends:
            cp.start()
        for n, (src, dst) in enumerate(((w_ref, wall_ref), (s_ref, sall_ref))):
            o = 7 * n
            for j, chip in enumerate(chips):
                got = dst.at[slot(*chip, c)]
                _rcopy(src, got, send_sems, recv_sems, o + 1 + j, sib).wait_recv()
                fw = _rcopy(got, got, send_sems, recv_sems, o + 4 + j, sib)
                fw.start()
                sends.append(fw)
            fwd_waits.append(_rcopy(src, dst.at[slot(x, y, 1 - c)], send_sems, recv_sems, o, sib))
            for j, chip in enumerate(chips):
                fwd_waits.append(_rcopy(src, dst.at[slot(*chip, 1 - c)], send_sems, recv_sems, o + 4 + j, sib))
        for cp in fwd_waits:
            cp.wait_recv()
        for cp in sends:
            cp.wait_send()
        for cp in loc:
            cp.wait()

    return _pcall(
        body, in_specs=[_ANY, _ANY], out_specs=[_ANY, _ANY],
        out_shape=[jax.ShapeDtypeStruct((N_DEV,) + wsh.shape, wsh.dtype), jax.ShapeDtypeStruct((N_DEV,) + ssh.shape, ssh.dtype)],
        scratch_shapes=[pltpu.SemaphoreType.DMA((14,)), pltpu.SemaphoreType.DMA((14,)), pltpu.SemaphoreType.DMA((2,))],
        name="all_gather_weights",
    )(wsh, ssh)


_ROW_TILES = (400, 368, 352, 256, 128, 16, 8)


def _exchange_sibling(g4, name):
    def body(g_ref, r_ref, send_sems, recv_sems):
        x, y, c, _ = _mesh_pos()
        sib = (x, y, 1 - c)
        cps = [_rcopy(g_ref.at[q, 1 - c], r_ref.at[q], send_sems, recv_sems, q, sib) for q in range(4)]
        for cp in cps:
            cp.start()
        for cp in cps:
            cp.wait()

    n, _, R, C = g4.shape
    return _pcall(
        body, in_specs=[_ANY], out_specs=_ANY, out_shape=jax.ShapeDtypeStruct((n, R, C), g4.dtype),
        scratch_shapes=[pltpu.SemaphoreType.DMA((4,)), pltpu.SemaphoreType.DMA((4,))], name=name,
    )(g4)


def _sum_sibling(g4, rsib, cvec, name):
    n, _, R, C = g4.shape
    tr = _pick(R, _ROW_TILES)

    def body(c_ref, g_ref, r_ref, o_ref):
        o_ref[...] = (g_ref[...].astype(F32) + r_ref[...].astype(F32)).astype(o_ref.dtype)

    grid_spec = pltpu.PrefetchScalarGridSpec(
        num_scalar_prefetch=1, grid=(n, R // tr),
        in_specs=[pl.BlockSpec((None, None, tr, C), lambda q, i, cr: (q, cr[0], i, 0)), pl.BlockSpec((None, tr, C), lambda q, i, cr: (q, i, 0))],
        out_specs=pl.BlockSpec((None, tr, C), lambda q, i, cr: (q, i, 0)))
    return _pcall(body, grid_spec=grid_spec, out_shape=jax.ShapeDtypeStruct((n, R, C), g4.dtype), name=name)(cvec, g4, rsib)


_HBM = pl.BlockSpec(memory_space=pltpu.HBM)
_SEM = pl.BlockSpec(memory_space=pltpu.SEMAPHORE)
_EFFECT = pltpu.SideEffectType.DATAFLOW_SIDE_EFFECTING


def _other_chips(x, y):
    return [(1 - x, y), (x, 1 - y), (1 - x, 1 - y)]


def _chips_start(part):
    def body(p_ref, land_ref, send_sems, recv_sems, p_thru, land_thru, token):
        x, y, c, _ = _mesh_pos()
        myq = 2 * x + y
        for j, (px, py) in enumerate(_other_chips(x, y)):
            _rcopy(p_ref.at[2 * px + py], land_ref.at[myq], send_sems, recv_sems, j, (px, py, c)).start()
        token[...] = jnp.zeros_like(token)

    land = pltpu.with_memory_space_constraint(lax.empty(part.shape, part.dtype), pltpu.HBM)
    return _pcall(
        body, name="chips_start",
        out_shape=(pltpu.SemaphoreType.DMA((3,)), pltpu.SemaphoreType.DMA((3,)), pltpu.HBM(part.shape, part.dtype),
                   pltpu.HBM(part.shape, part.dtype), jax.ShapeDtypeStruct((8, 128), F32)),
        in_specs=(_HBM, _HBM), out_specs=(_SEM, _SEM, _HBM, _HBM, pl.BlockSpec(memory_space=pltpu.VMEM)),
        input_output_aliases={0: 2, 1: 3}, compiler_params=pltpu.CompilerParams(has_side_effects=_EFFECT),
    )(pltpu.with_memory_space_constraint(part, pltpu.HBM), land)


def _chips_wait(send_sems, recv_sems, p_thru, land_thru, after):
    def body(p_ref, land_ref, send_sems, recv_sems, after_ref, p_out, land_out):
        x, y, c, _ = _mesh_pos()
        for j, (px, py) in enumerate(_other_chips(x, y)):
            q = 2 * px + py
            cp = _rcopy(p_ref.at[q], land_ref.at[q], send_sems, recv_sems, j, (px, py, c))
            cp.wait_send()
            cp.wait_recv()

    return _pcall(
        body, name="chips_wait", out_shape=(pltpu.HBM(p_thru.shape, p_thru.dtype), pltpu.HBM(p_thru.shape, p_thru.dtype)),
        in_specs=(_HBM, _HBM, _SEM, _SEM, _ANY), out_specs=(_HBM, _HBM), input_output_aliases={0: 0, 1: 1},
        compiler_params=pltpu.CompilerParams(has_side_effects=_EFFECT),
    )(p_thru, land_thru, send_sems, recv_sems, after)


def _sum_chips(part, land, qvec, name):
    n, R, C = part.shape
    tr = _pick(R, _ROW_TILES)

    def body(q_ref, p_ref, *refs):
        o_ref = refs[n]
        acc = None
        for q in range(n):
            term = jnp.where(q_ref[0] == q, p_ref[...], refs[q][...]).astype(F32)
            acc = term if acc is None else acc + term
        o_ref[...] = acc

    def land_spec(q):
        return pl.BlockSpec((None, tr, C), lambda i, m: (jnp.where(m[0] == q, (q + 1) % n, q), i, 0))

    grid_spec = pltpu.PrefetchScalarGridSpec(
        num_scalar_prefetch=1, grid=(R // tr,),
        in_specs=[pl.BlockSpec((None, tr, C), lambda i, m: (m[0], i, 0))] + [land_spec(q) for q in range(n)],
        out_specs=pl.BlockSpec((tr, C), lambda i, m: (i, 0)))
    return _pcall(body, grid_spec=grid_spec, out_shape=jax.ShapeDtypeStruct((R, C), F32), name=name)(qvec, part, *([land] * n))


def _exchange_chips(part, gsmall):
    def body(p_ref, s_ref, recv_ref, srecv_ref, send_sems, recv_sems, loc_sems):
        x, y, c, me = _mesh_pos()
        myq = 2 * x + y
        loc = [pltpu.make_async_copy(p_ref.at[myq], recv_ref.at[myq], loc_sems.at[0]),
               pltpu.make_async_copy(s_ref, srecv_ref.at[me], loc_sems.at[1])]
        for cp in loc:
            cp.start()
        sends, recvs = [], []
        for j, (px, py) in enumerate([(1 - x, y), (x, 1 - y), (1 - x, 1 - y)]):
            q = 2 * px + py
            sends.append(_rcopy(p_ref.at[q], recv_ref.at[myq], send_sems, recv_sems, j, (px, py, c)))
            recvs.append(_rcopy(p_ref.at[q], recv_ref.at[q], send_sems, recv_sems, j, (px, py, c)))
        for k in range(1, N_DEV):
            to, pidx = _peer(x, y, c, k)
            sends.append(_rcopy(s_ref, srecv_ref.at[me], send_sems, recv_sems, 2 + k, to))
            recvs.append(_rcopy(s_ref, srecv_ref.at[pidx], send_sems, recv_sems, 2 + k, to))
        for cp in sends:
            cp.start()
        for cp in recvs:
            cp.wait_recv()
        for cp in sends:
            cp.wait_send()
        for cp in loc:
            cp.wait()

    return _pcall(
        body, in_specs=[_ANY, _ANY], out_specs=[_ANY, _ANY],
        out_shape=[jax.ShapeDtypeStruct(part.shape, part.dtype), jax.ShapeDtypeStruct((N_DEV,) + gsmall.shape, gsmall.dtype)],
        scratch_shapes=[pltpu.SemaphoreType.DMA((10,)), pltpu.SemaphoreType.DMA((10,)), pltpu.SemaphoreType.DMA((2,))],
        name="exchange_chips",
    )(part, gsmall)


def _sum_slots(recv, name):
    n, R, C = recv.shape
    tr = _pick(R, _ROW_TILES)

    def body(r_ref, o_ref):
        acc = r_ref[0].astype(F32)
        for s in range(1, n):
            acc = acc + r_ref[s].astype(F32)
        o_ref[...] = acc

    return _pcall(body, grid=(R // tr,), in_specs=[pl.BlockSpec((n, tr, C), lambda i: (0, i, 0))],
                  out_specs=pl.BlockSpec((tr, C), lambda i: (i, 0)), out_shape=jax.ShapeDtypeStruct((R, C), F32), name=name)(recv)


def _adamw(w, g, m, v, name):
    shape = w.shape
    w2, g2, m2, v2 = (a.reshape(-1, shape[-1]) for a in (w, g, m, v))
    R, C = w2.shape
    tr = _pick(R, (256, 128, 64, 32, 16, 8)) if R > 256 else R

    def body(w_ref, g_ref, m_ref, v_ref, d_ref, nm_ref, nv_ref):
        gg = g_ref[...]
        nm = ADAM_B1 * m_ref[...] + (1.0 - ADAM_B1) * gg
        nv = ADAM_B2 * v_ref[...] + (1.0 - ADAM_B2) * jnp.square(gg)
        m_hat = nm / (1.0 - ADAM_B1 ** ADAM_STEP)
        v_hat = nv / (1.0 - ADAM_B2 ** ADAM_STEP)
        d_ref[...] = -ADAM_LR * (m_hat / (jnp.sqrt(v_hat) + ADAM_EPS) + ADAM_WD * w_ref[...])
        nm_ref[...] = nm
        nv_ref[...] = nv

    spec = pl.BlockSpec((tr, C), lambda i: (i, 0))
    d, nm, nv = _pcall(body, grid=(R // tr,), in_specs=[spec] * 4, out_specs=[spec] * 3,
                       out_shape=[jax.ShapeDtypeStruct((R, C), F32)] * 3, name=name)(w2, g2, m2, v2)
    return d.reshape(shape), nm.reshape(shape), nv.reshape(shape)


def _to_rows(name, w):
    w = w[0]
    if name in ("ffn1_w_in", "ffn2_w_in", "w_in"):
        return w.T
    if name in ("w_uq", "w_ukv", "ple_w_proj"):
        return w.T.reshape(-1, 1024)
    return w


def _from_rows(name, g, shape):
    if name in ("ffn1_w_in", "ffn2_w_in", "w_in"):
        return g.T.reshape(shape)
    if name in ("w_uq", "w_ukv", "ple_w_proj"):
        return g.reshape(-1, shape[1]).T.reshape(shape)
    return g.reshape(shape)


def _unpack(wall):
    out, off = {}, 0
    for name, r in PACK:
        out[name] = wall[:, off:off + r, :].reshape(N_DEV * r, 1024)
        off += _pad16(r)
    return out


def _w_in_internal(wt):
    return jnp.concatenate([wt[0:6144], wt[6720:8768], wt[6144:6720], jnp.zeros((P_W - W_IN_COLS, 1024), wt.dtype)], axis=0)


def _w_in_external(d):
    return jnp.concatenate([d[0:6144], d[8192:8768], d[6144:8192]], axis=0)


def _rope_tables(positions):
    pos = positions[0].astype(F32)

    def cs(half):
        inv = ROPE_BASE ** (-jnp.arange(half, dtype=F32) / half)
        ang = pos[:, None] * inv
        return jnp.cos(ang), jnp.sin(ang)

    c, s = cs(64)
    c2, s2 = cs(32)
    return (jnp.concatenate([c, c], axis=1), jnp.concatenate([-s, s], axis=1),
            jnp.concatenate([c2, c2, c2, c2], axis=1), jnp.concatenate([-s2, s2, -s2, s2], axis=1))


def _local_step(x, p, positions, target, W, ln_g, ln_b, gn_g, qg, kvg, early=None):
    T = x.shape[0]
    tabs = _rope_tables(positions)
    rc = _ret_consts()
    lg = [ln_g[i:i + 1] for i in range(4)]
    lb = [ln_b[i:i + 1] for i in range(4)]
    xb = x.astype(BF16)
    pb = p.astype(BF16)
    w_in_t = _w_in_internal(W["w_in"])
    wuq = W["w_uq"].reshape(1536, LORA).reshape(HEADS, 192, LORA)
    wuq = jnp.concatenate([wuq[:, :128].reshape(1024, LORA), wuq[:, 128:].reshape(512, LORA)], axis=0)
    wukv = W["w_ukv"].reshape(2048, LORA).reshape(HEADS, 2, 128, LORA).transpose(1, 0, 2, 3).reshape(2048, LORA)
    wp_t = W["ple_w_proj"].reshape(1024, D_PLE)

    hs1, gu1 = _ffn_in(xb, W["ffn1_w_in"], "ffn1_in")
    f1, h1, h1b = _mm(hs1, W["ffn1_w_out"], name="ffn1_out", tm=LN_TM, epilogue=_ln_epilogue(x, 0.5, lg[0], lb[0]))
    proj = _mm(h1b, w_in_t, tb=True, name="mixer_in")
    rq, rk, rv, qn, kvn, kpe = _mixer_prep_fwd(proj, tabs, qg, kvg, T)
    y, yr, states = _ret_fwd(rq, rk, rv, proj, gn_g, rc, T)
    y_ret = _mm(yr, W["w_ret_o"], name="ret_o")
    q = _mm(qn, wuq, tb=True, name="mla_uq")
    kv = _mm(kvn, wukv, tb=True, out_dtype=BF16, name="mla_ukv")
    qnope, qpe = _q_assemble_fwd(q, tabs, T)
    o, lse = _attn_fwd(qnope, qpe, kv, kpe, T)
    y_mla = _mm(o, W["w_mla_o"], name="mla_o")
    mix = _mix_fwd(proj, y_ret, y_mla, T)
    mixed, h2, h2b = _mm(mix, W["w_out"], name="mixer_out", tm=LN_TM, epilogue=_ln_epilogue(h1, 1.0, lg[1], lb[1]))
    hs2, gu2 = _ffn_in(h2b, W["ffn2_w_in"], "ffn2_in")
    f2, h3, h3b = _mm(hs2, W["ffn2_w_out"], name="ffn2_out", tm=LN_TM, epilogue=_ln_epilogue(h2, 0.5, lg[2], lb[2]))
    gl = _mm(h3b, W["ple_w_gate"], name="ple_gate")
    pp = _mm(pb, wp_t, tb=True, name="ple_proj")

    G = {}
    dh3_a, dgl, dpp, dg3, db3, loss = _head(h3, gl, pp, target, lg[3], lb[3], T)
    G["ple_w_gate"] = _mm(h3b, dgl, ta=True, name="d_ple_gate")
    G["ple_w_proj"] = _mm(dpp, pb, ta=True, name="d_ple_proj")
    dh2_a, df2, dg2, db2 = _mm(dgl, W["ple_w_gate"], tb=True, add=dh3_a, name="dh3", tm=LN_TM,
                               epilogue=_ln_bwd_epilogue(h2, f2, 0.5, lg[2], lb[2]))
    G["ffn2_w_out"] = _mm(hs2, df2, ta=True, name="d_ffn2_out")
    da2 = _ffn_act_bwd(df2, W["ffn2_w_out"], gu2, "ffn2_act_bwd")
    G["ffn2_w_in"] = _mm(da2, h2b, ta=True, name="d_ffn2_in")
    dh1_a, dmixed, dg1, db1 = _mm(da2, W["ffn2_w_in"], add=dh2_a, name="dh2", tm=LN_TM,
                                  epilogue=_ln_bwd_epilogue(h1, mixed, 1.0, lg[1], lb[1]))
    G["w_out"] = _mm(mix, dmixed, ta=True, name="d_mixer_out")
    dmix = _mm(dmixed, W["w_out"], tb=True, name="dmix")
    dgr, dgm, dy_ret, dy_mla = _mix_bwd(proj, y_ret, y_mla, dmix, T)
    G["w_mla_o"] = _mm(o, dy_mla, ta=True, name="d_mla_o")
    do = _mm(dy_mla, W["w_mla_o"], tb=True, name="do")
    dob, delta = _attn_delta(do, o, T)
    dqn_f, dqpe_f, dkn, dkpe_all, dv = _attn_bwd(qnope, qpe, kv, kpe, dob, lse, delta, T)
    dq_n, dq_r = _q_assemble_bwd(dqn_f, dqpe_f, tabs, T)
    g_uq = jnp.concatenate([_mm(dq_n, qn, ta=True, name="d_uq_nope"), _mm(dq_r, qn, ta=True, name="d_uq_rope")], axis=0)
    g_uq = jnp.concatenate([g_uq[:1024].reshape(HEADS, 128, LORA), g_uq[1024:].reshape(HEADS, 64, LORA)], axis=1)
    G["w_uq"] = g_uq.reshape(1536 * LORA // 1024, 1024)
    dqn = _mm(dq_r, wuq[1024:], add=_mm(dq_n, wuq[:1024], name="dqn_a"), name="dqn_b")
    g_ukv = jnp.stack([_mm(dkn, kvn, ta=True, name="d_ukv_k"), _mm(dv, kvn, ta=True, name="d_ukv_v")], axis=0)
    G["w_ukv"] = g_ukv.reshape(2, HEADS, 128, LORA).transpose(1, 0, 2, 3).reshape(2048 * LORA // 1024, 1024)
    dkvn = _mm(dv, wukv[1024:], add=_mm(dkn, wukv[:1024], name="dkvn_a"), name="dkvn_b")
    dcq, dckv, dkpe, dqg, dkvg = _rms_bwd(proj, dqn, dkvn, dkpe_all, tabs, qg, kvg, T)
    G["w_ret_o"] = _mm(yr, dy_ret, ta=True, name="d_ret_o")
    dyr = _mm(dy_ret, W["w_ret_o"], tb=True, name="dyr")
    drq, drk, drv, drg, dgn = _ret_bwd(rq, rk, rv, y, proj, gn_g, states, dyr, rc, tabs, T)
    dproj = jnp.concatenate([drq, drk, drv, drg, dgr, dgm, dcq, dckv, dkpe, jnp.zeros((T, P_W - P_KPE - 128), BF16)], axis=1)
    G["w_in"] = _w_in_external(_mm(dproj, h1b, ta=True, name="d_mixer_in"))
    lg0 = lg[0] if early is None else lg[0] + early(G)[0:1, 0:1]
    dx_a, df1, dg0, db0 = _mm(dproj, w_in_t, add=dh1_a, name="dh1", tm=LN_TM,
                              epilogue=_ln_bwd_epilogue(x, f1, 0.5, lg0, lb[0]))
    G["ffn1_w_out"] = _mm(hs1, df1, ta=True, name="d_ffn1_out")
    da1 = _ffn_act_bwd(df1, W["ffn1_w_out"], gu1, "ffn1_act_bwd")
    G["ffn1_w_in"] = _mm(da1, xb, ta=True, name="d_ffn1_in")
    grad_x = _mm(da1, W["ffn1_w_in"], add=dx_a, name="grad_x")

    small = dict(ln_g=jnp.concatenate([dg0, dg1, dg2, dg3], axis=0), ln_b=jnp.concatenate([db0, db1, db2, db3], axis=0),
                 ret_gn_g=dgn, q_norm_g=dqg, kv_norm_g=dkvg)
    return loss, grad_x, G, small


def kernel(x, p, positions, ln_g, ln_b, ffn1_w_in, ffn1_w_out, w_in, ret_gn_g, w_ret_o, q_norm_g, kv_norm_g, w_uq, w_ukv, w_mla_o, w_out, ffn2_w_in, ffn2_w_out, ple_w_gate, ple_w_proj, loss_target, m_ln_g, m_ln_b, m_ffn1_w_in, m_ffn1_w_out, m_w_in, m_ret_gn_g, m_w_ret_o, m_q_norm_g, m_kv_norm_g, m_w_uq, m_w_ukv, m_w_mla_o, m_w_out, m_ffn2_w_in, m_ffn2_w_out, m_ple_w_gate, m_ple_w_proj, v_ln_g, v_ln_b, v_ffn1_w_in, v_ffn1_w_out, v_w_in, v_ret_gn_g, v_w_ret_o, v_q_norm_g, v_kv_norm_g, v_w_uq, v_w_ukv, v_w_mla_o, v_w_out, v_ffn2_w_in, v_ffn2_w_out, v_ple_w_gate, v_ple_w_proj):
    names = ("ln_g", "ln_b", "ffn1_w_in", "ffn1_w_out", "w_in", "ret_gn_g", "w_ret_o", "q_norm_g", "kv_norm_g", "w_uq", "w_ukv",
             "w_mla_o", "w_out", "ffn2_w_in", "ffn2_w_out", "ple_w_gate", "ple_w_proj")
    ws = dict(zip(names, (ln_g, ln_b, ffn1_w_in, ffn1_w_out, w_in, ret_gn_g, w_ret_o, q_norm_g, kv_norm_g, w_uq, w_ukv, w_mla_o,
                          w_out, ffn2_w_in, ffn2_w_out, ple_w_gate, ple_w_proj)))
    ms = dict(zip(names, (m_ln_g, m_ln_b, m_ffn1_w_in, m_ffn1_w_out, m_w_in, m_ret_gn_g, m_w_ret_o, m_q_norm_g, m_kv_norm_g, m_w_uq,
                          m_w_ukv, m_w_mla_o, m_w_out, m_ffn2_w_in, m_ffn2_w_out, m_ple_w_gate, m_ple_w_proj)))
    vs = dict(zip(names, (v_ln_g, v_ln_b, v_ffn1_w_in, v_ffn1_w_out, v_w_in, v_ret_gn_g, v_w_ret_o, v_q_norm_g, v_kv_norm_g, v_w_uq,
                          v_w_ukv, v_w_mla_o, v_w_out, v_ffn2_w_in, v_ffn2_w_out, v_ple_w_gate, v_ple_w_proj)))

    parts = []
    for name, r in PACK:
        rows = _to_rows(name, ws[name])
        if _pad16(r) != r:
            rows = jnp.concatenate([rows, jnp.zeros((_pad16(r) - r, 1024), F32)], axis=0)
        parts.append(rows)
    wsh = jnp.concatenate(parts, axis=0).astype(BF16)
    ssh = jnp.concatenate([ln_g[0], ln_b[0]], axis=0)
    wall, sall = _all_gather(wsh, ssh)
    W = _unpack(wall)
    ln_full = sall.reshape(N_DEV, 2, 4, 128).transpose(1, 2, 0, 3).reshape(2, 4, 1024)

    cvec = lax.axis_index("c").astype(jnp.int32).reshape(1)
    qvec = (2 * lax.axis_index("x") + lax.axis_index("y")).astype(jnp.int32).reshape(1)

    def chip_partials(G, group, tag):
        gparts = []
        for name, r in group:
            g = G[name].reshape(N_DEV, r, 1024)
            if _pad16(r) != r:
                g = jnp.concatenate([g, jnp.zeros((N_DEV, _pad16(r) - r, 1024), F32)], axis=1)
            gparts.append(g)
        gfull = jnp.concatenate(gparts, axis=1).astype(BF16)
        g4 = gfull.reshape(4, 2, gfull.shape[1], 1024)
        return _sum_sibling(g4, _exchange_sibling(g4, "exchange_sibling_" + tag), cvec, "sum_sibling_" + tag)

    in_flight = []

    def early(G):
        *handles, token = _chips_start(chip_partials(G, PACK_EARLY, "early"))
        in_flight.append(handles)
        return token

    loss_p, grad_x, G, small = _local_step(x[0], p[0, 0], positions, loss_target[0], W, ln_full[0], ln_full[1],
                                           ret_gn_g, q_norm_g, kv_norm_g, early=early)

    part_e, land_e = _chips_wait(*in_flight[0], grad_x)
    gsh_early = _sum_chips(part_e, land_e, qvec, "sum_grads_early")
    pad256 = lambda a: jnp.concatenate([a, jnp.zeros((1, 1024 - a.shape[1]), F32)], axis=1)
    gsmall = jnp.concatenate([small["ln_g"], small["ln_b"], small["ret_gn_g"].reshape(2, 1024), pad256(small["q_norm_g"]),
                              pad256(small["kv_norm_g"]), jnp.zeros((SMALL_ROWS - 12, 1024), F32)], axis=0)
    recv, srecv = _exchange_chips(chip_partials(G, PACK_LATE, "late"), gsmall)
    gsh_late = _sum_slots(recv, "sum_grads_late")
    ssum = _sum_slots(srecv, "sum_small_grads")

    grads = {}
    for group, gsh in ((PACK_LATE, gsh_late), (PACK_EARLY, gsh_early)):
        off = 0
        for name, r in group:
            grads[name] = _from_rows(name, gsh[off:off + r], ws[name].shape)
            off += _pad16(r)
    me = 4 * lax.axis_index("x") + 2 * lax.axis_index("y") + lax.axis_index("c")
    grads["ln_g"] = lax.dynamic_slice(ssum[0:4], (0, me * 128), (4, 128)).reshape(1, 4, 128)
    grads["ln_b"] = lax.dynamic_slice(ssum[4:8], (0, me * 128), (4, 128)).reshape(1, 4, 128)
    grads["ret_gn_g"] = ssum[8:10].reshape(1, 2048)
    grads["q_norm_g"] = ssum[10:11, :256]
    grads["kv_norm_g"] = ssum[11:12, :256]

    delta, new_m, new_v = {}, {}, {}
    for name in names:
        delta[name], new_m[name], new_v[name] = _adamw(ws[name], grads[name], ms[name], vs[name], "adamw_" + name)

    loss = lax.psum(loss_p[0, 0], ("x", "y", "c"))
    return (loss, grad_x[None], *[grads[n] for n in names], *[delta[n] for n in names],
            *[new_m[n] for n in names], *[new_v[n] for n in names])
```

```python
import functools
import math

import jax
import jax.numpy as jnp
from jax import lax
from jax.experimental import pallas as pl
from jax.experimental.pallas import tpu as pltpu

F32 = jnp.float32
BF16 = jnp.bfloat16

N_DEV = 8
D = 1024
D_FF = 2816
D_PLE = 256
CHUNK = 64
HEADS = 8
RET_DK = 128
RET_DV = 256
MLA_NOPE = 128
MLA_ROPE = 64
MLA_DV = 128
LORA = 256
ROPE_BASE = 10000.0
EPS = 1e-5
ALPHA = 2.0 ** 0.25
RET_SCALE = RET_DK ** -0.5
MLA_SCALE = (MLA_NOPE + MLA_ROPE) ** -0.5
NEG = -1e30

ADAM_LR = 0.001
ADAM_B1 = 0.9
ADAM_B2 = 0.999
ADAM_EPS = 1e-08
ADAM_WD = 0.01
ADAM_STEP = 10

P_RQ, P_RK, P_RV, P_RG, P_GR, P_GM, P_CQ, P_CKV, P_KPE, P_W = 0, 1024, 2048, 4096, 6144, 7168, 8192, 8448, 8704, 8960
W_IN_COLS = 8768
RET_L = 256
ATT_TF = 1024
ATT_TB = 1024
ATT_HP = 2
LOG2E = math.log2(math.e)
Q_PRESCALE = MLA_SCALE * LOG2E

PACK = (("ffn1_w_in", 704), ("ffn1_w_out", 352), ("w_in", 1096), ("w_ret_o", 256), ("w_uq", 48), ("w_ukv", 64),
        ("w_mla_o", 128), ("w_out", 128), ("ffn2_w_in", 704), ("ffn2_w_out", 352), ("ple_w_gate", 128), ("ple_w_proj", 32))


def _pad16(r):
    return -(-r // 16) * 16


PACK_ROWS = sum(_pad16(r) for _, r in PACK)
PACK_LATE = PACK[:2]
PACK_EARLY = PACK[2:]
SMALL_ROWS = 16


def _pcall(body, **kw):
    return pl.pallas_call(body, **kw)


def _pick(dim, prefs):
    for p in prefs:
        if dim % p == 0:
            return p
    return dim


def _sigmoid(x):
    return 1.0 / (1.0 + jnp.exp(-x))


def _silu(x):
    return x * _sigmoid(x)


def _ln(r, g, b):
    mu = jnp.mean(r, axis=-1, keepdims=True)
    var = jnp.mean(jnp.square(r - mu), axis=-1, keepdims=True)
    return (r - mu) * lax.rsqrt(var + EPS) * g + b


def _rms(x, g):
    return x * lax.rsqrt(jnp.mean(jnp.square(x), axis=-1, keepdims=True) + EPS) * g


def _dot(a, b, ca, cb):
    return lax.dot_general(a, b, (((ca,), (cb,)), ((), ())), preferred_element_type=F32)


def _accum(ref, val, first=None):
    @pl.when(pl.program_id(0) == 0 if first is None else first)
    def _():
        ref[...] = jnp.zeros_like(ref)

    ref[...] += val


def _mm(a, b, *, ta=False, tb=False, add=None, out_dtype=F32, name, tm=None, tn=None, tk=None, epilogue=None):
    parts = a.shape[0] if a.ndim == 3 else 1
    ar, ac = a.shape[-2], a.shape[-1]
    if ta:
        K, M = ar, ac * parts
    else:
        M, K = ar, ac * parts
    if tb:
        N, K2 = b.shape
    else:
        K2, N = b.shape
    assert K == K2, (a.shape, b.shape, ta, tb)
    big = (1024, 1408, 1280, 768, 512, 256, 128)
    tm = tm or _pick(ac if (ta and parts > 1) else M, big)
    tn = tn or (N if N <= 1024 else _pick(N, big))
    kdim = ac if (not ta and parts > 1) else K
    tk = tk or (kdim if kdim <= 2816 and parts == 1 else
                _pick(kdim, (2048, 1408, 1280, 1024, 512) if tm <= 1024 else (1024, 1408, 1280, 512)))
    nk = K // tk
    grid = (M // tm, N // tn, nk)
    if parts > 1 and ta:
        per = ac // tm
        a_spec = pl.BlockSpec((None, tk, tm), lambda i, j, k: (i // per, k, i % per))
    elif parts > 1:
        per = ac // tk
        a_spec = pl.BlockSpec((None, tm, tk), lambda i, j, k: (k // per, i, k % per))
    else:
        a_spec = pl.BlockSpec((tk, tm), lambda i, j, k: (k, i)) if ta else pl.BlockSpec((tm, tk), lambda i, j, k: (i, k))
    b_spec = pl.BlockSpec((tn, tk), lambda i, j, k: (j, k)) if tb else pl.BlockSpec((tk, tn), lambda i, j, k: (k, j))
    o_spec = pl.BlockSpec((tm, tn), lambda i, j, k: (i, j))
    ca, cb = (0 if ta else 1), (1 if tb else 0)
    has_add = add is not None
    n_in = 2 + int(has_add)
    if epilogue is not None:
        assert tn == N and not ta
        ep_fn, ep_rows, ep_whole, ep_outs, ep_accs = epilogue
        n_ep_in = len(ep_rows) + len(ep_whole)
        n_out = len(ep_outs) + len(ep_accs)
    else:
        n_ep_in, n_out = 0, 1

    def body(*refs):
        a_ref, b_ref = refs[0], refs[1]
        add_ref = refs[2] if has_add else None
        o_ref = refs[n_in + n_ep_in]
        first_row_tile = pl.program_id(0) == 0

        def finish(r):
            if has_add:
                r = r + add_ref[...].astype(F32)
            if epilogue is not None:
                ep_fn(r, refs[n_in:n_in + n_ep_in], refs[n_in + n_ep_in:n_in + n_ep_in + n_out], first_row_tile)
            else:
                o_ref[...] = r.astype(out_dtype)

        if nk == 1:
            finish(_dot(a_ref[...], b_ref[...], ca, cb))
            return
        acc_ref = refs[-1]
        k = pl.program_id(2)

        @pl.when(k == 0)
        def _():
            acc_ref[...] = jnp.zeros_like(acc_ref)

        acc_ref[...] += _dot(a_ref[...], b_ref[...], ca, cb)

        @pl.when(k == nk - 1)
        def _():
            finish(acc_ref[...])

    in_specs = [a_spec, b_spec] + ([o_spec] if has_add else [])
    args = (a, b) + ((add,) if has_add else ())
    out_specs, out_shape = o_spec, jax.ShapeDtypeStruct((M, N), out_dtype)
    if epilogue is not None:
        in_specs += [o_spec] * len(ep_rows) + [pl.BlockSpec(w.shape, lambda i, j, k, _n=w.ndim: (0,) * _n) for w in ep_whole]
        args += tuple(ep_rows) + tuple(ep_whole)
        out_specs = [o_spec] * len(ep_outs) + [pl.BlockSpec((r, N), lambda i, j, k: (0, 0)) for r in ep_accs]
        out_shape = [jax.ShapeDtypeStruct((M, N), dt) for dt in ep_outs] + [jax.ShapeDtypeStruct((r, N), F32) for r in ep_accs]
    return _pcall(
        body, grid=grid, in_specs=in_specs, out_specs=out_specs, out_shape=out_shape,
        scratch_shapes=[pltpu.VMEM((tm, tn), F32)] if nk > 1 else [], name=name,
        compiler_params=pltpu.CompilerParams(dimension_semantics=("arbitrary" if epilogue is not None else "parallel", "parallel", "arbitrary")),
    )(*args)


def _ln_epilogue(res, c, g, b):
    def fn(r, ins, outs, first):
        res_ref, g_ref, b_ref = ins
        f_ref, h_ref, hb_ref = outs
        h = _ln(ALPHA * res_ref[...] + c * r, g_ref[...], b_ref[...])
        f_ref[...] = r
        h_ref[...] = h
        hb_ref[...] = h.astype(BF16)

    return (fn, [res], [g, b], (F32, F32, BF16), ())


def _ln_bwd_epilogue(res, f, c, g, b):
    def fn(r, ins, outs, first):
        res_ref, f_ref, g_ref, b_ref = ins
        dr_ref, df_ref, dg_ref, db_ref = outs
        _, vjp = jax.vjp(lambda rr, ff, gg, bb: _ln(ALPHA * rr + c * ff, gg, bb), res_ref[...], f_ref[...], g_ref[...], b_ref[...])
        dr, df, dg, db = vjp(r)
        dr_ref[...] = dr
        df_ref[...] = df.astype(BF16)
        _accum(dg_ref, dg, first)
        _accum(db_ref, db, first)

    return (fn, [res, f], [g, b], (F32, BF16), (1, 1))


def _rows(body, T, tm, ins, outs, name, accs=()):
    in_specs, args = [], []
    for arr, w, cb in ins:
        if w is None:
            in_specs.append(pl.BlockSpec(arr.shape, lambda i, _n=arr.ndim: (0,) * _n))
        else:
            in_specs.append(pl.BlockSpec((tm, w), lambda i, _cb=cb: (i, _cb)))
        args.append(arr)
    out_specs = [pl.BlockSpec((tm, w), lambda i: (i, 0)) for w, _ in outs]
    out_shape = [jax.ShapeDtypeStruct((T, w), dt) for w, dt in outs]
    for r, w in accs:
        out_specs.append(pl.BlockSpec((r, w), lambda i: (0, 0)))
        out_shape.append(jax.ShapeDtypeStruct((r, w), F32))
    return _pcall(
        body, grid=(T // tm,), in_specs=in_specs, out_specs=out_specs, out_shape=out_shape, name=name,
        compiler_params=pltpu.CompilerParams(dimension_semantics=("arbitrary",)),
    )(*args)


FFN_TN = 1408
FFN_TM = 512
LN_TM = 512


def _ffn_in(xb, wt, name):
    T = xb.shape[0]
    tm, tn = min(FFN_TM, T), FFN_TN
    nj = D_FF // tn

    def body(x_ref, wg_ref, wu_ref, hs_ref, gu_ref):
        x = x_ref[...]
        g = _dot(x, wg_ref[...], 1, 1)
        u = _dot(x, wu_ref[...], 1, 1)
        hs_ref[...] = (_silu(g) * u).astype(BF16)
        gu_ref[0] = g.astype(BF16)
        gu_ref[1] = u.astype(BF16)

    return _pcall(
        body, grid=(T // tm, nj),
        in_specs=[pl.BlockSpec((tm, D), lambda i, j: (i, 0)), pl.BlockSpec((tn, D), lambda i, j: (j, 0)),
                  pl.BlockSpec((tn, D), lambda i, j: (j + nj, 0))],
        out_specs=[pl.BlockSpec((tm, tn), lambda i, j: (i, j)), pl.BlockSpec((2, tm, tn), lambda i, j: (0, i, j))],
        out_shape=[jax.ShapeDtypeStruct((T, D_FF), BF16), jax.ShapeDtypeStruct((2, T, D_FF), BF16)], name=name,
        compiler_params=pltpu.CompilerParams(dimension_semantics=("parallel", "parallel")),
    )(xb, wt, wt)


def _ffn_act_bwd(df, wo, gu, name):
    T = df.shape[0]
    tm, tn = min(FFN_TM, T), FFN_TN

    def body(d_ref, w_ref, gu_ref, o_ref):
        dhs = _dot(d_ref[...], w_ref[...], 1, 1)
        _, vjp = jax.vjp(lambda g, u: _silu(g) * u, gu_ref[0].astype(F32), gu_ref[1].astype(F32))
        dg, du = vjp(dhs)
        o_ref[0] = dg.astype(BF16)
        o_ref[1] = du.astype(BF16)

    return _pcall(
        body, grid=(T // tm, D_FF // tn),
        in_specs=[pl.BlockSpec((tm, D), lambda i, j: (i, 0)), pl.BlockSpec((tn, D), lambda i, j: (j, 0)),
                  pl.BlockSpec((2, tm, tn), lambda i, j: (0, i, j))],
        out_specs=pl.BlockSpec((2, tm, tn), lambda i, j: (0, i, j)),
        out_shape=jax.ShapeDtypeStruct((2, T, D_FF), BF16), name=name,
        compiler_params=pltpu.CompilerParams(dimension_semantics=("parallel", "parallel")),
    )(df, wo, gu)


def _rope128(t, cos, sin_s):
    return t * cos + pltpu.roll(t, 64, 1) * sin_s


def _rope128_t(g, cos, sin_s):
    return g * cos - pltpu.roll(g, 64, 1) * sin_s


def _partner32(t):
    lane = lax.broadcasted_iota(jnp.int32, t.shape, 1)
    return jnp.where((lane & 32) == 0, pltpu.roll(t, 96, 1), pltpu.roll(t, 32, 1))


def _rope64(t, cos, sin_s):
    return t * cos + _partner32(t) * sin_s


def _rope64_t(g, cos, sin_s):
    return g * cos - _partner32(g) * sin_s


def _mixer_prep_fwd(proj, tabs, qg, kvg, T):
    cos128, sin128, cos64, sin64 = tabs

    def body(rq_ref, rk_ref, rv_ref, cq_ref, ckv_ref, kpe_ref, c1_ref, s1_ref, c2_ref, s2_ref, qg_ref, kvg_ref,
             oq_ref, ok_ref, ov_ref, oqn_ref, okvn_ref, okpe_ref):
        c1, s1 = c1_ref[...], s1_ref[...]
        for h in range(HEADS):
            sl = slice(h * RET_DK, (h + 1) * RET_DK)
            oq_ref[:, sl] = _rope128(rq_ref[:, sl], c1, s1).astype(BF16)
            ok_ref[:, sl] = (_rope128(rk_ref[:, sl], c1, s1) * RET_SCALE).astype(BF16)
        ov_ref[...] = rv_ref[...].astype(BF16)
        oqn_ref[...] = _rms(cq_ref[...], qg_ref[...]).astype(BF16)
        okvn_ref[...] = _rms(ckv_ref[...], kvg_ref[...]).astype(BF16)
        okpe_ref[...] = _rope64(kpe_ref[...], c2_ref[...], s2_ref[...]).astype(BF16)

    ins = [(proj, 1024, 0), (proj, 1024, 1), (proj, 2048, 1), (proj, 256, P_CQ // 256), (proj, 256, P_CKV // 256),
           (proj, 128, P_KPE // 128), (cos128, 128, 0), (sin128, 128, 0), (cos64, 128, 0), (sin64, 128, 0),
           (qg, None, None), (kvg, None, None)]
    outs = [(1024, BF16), (1024, BF16), (2048, BF16), (LORA, BF16), (LORA, BF16), (128, BF16)]
    return _rows(body, T, 256, ins, outs, "mixer_prep_fwd")


def _rms_bwd(proj, dqn, dkvn, dkpe_all, tabs, qg, kvg, T):
    _, _, cos64, sin64 = tabs

    def body(cq_ref, ckv_ref, dq_ref, dkv_ref, dk_ref, c2_ref, s2_ref, qg_ref, kvg_ref, ocq_ref, ockv_ref, okpe_ref,
             dqg_ref, dkvg_ref):
        _, vjp = jax.vjp(_rms, cq_ref[...], qg_ref[...])
        dx, dg = vjp(dq_ref[...])
        ocq_ref[...] = dx.astype(BF16)
        _accum(dqg_ref, dg)
        _, vjp = jax.vjp(_rms, ckv_ref[...], kvg_ref[...])
        dx, dg = vjp(dkv_ref[...])
        ockv_ref[...] = dx.astype(BF16)
        _accum(dkvg_ref, dg)
        g = dk_ref[:, 0:128]
        for h in range(1, HEADS):
            g = g + dk_ref[:, h * 128:(h + 1) * 128]
        lane = lax.broadcasted_iota(jnp.int32, g.shape, 1)
        g = jnp.where(lane < MLA_ROPE, g, 0.0)
        okpe_ref[...] = _rope64_t(g, c2_ref[...], s2_ref[...]).astype(BF16)

    ins = [(proj, 256, P_CQ // 256), (proj, 256, P_CKV // 256), (dqn, LORA, 0), (dkvn, LORA, 0), (dkpe_all, 1024, 0),
           (cos64, 128, 0), (sin64, 128, 0), (qg, None, None), (kvg, None, None)]
    return _rows(body, T, 256, ins, [(LORA, BF16), (LORA, BF16), (128, BF16)], "rms_bwd", accs=[(1, LORA), (1, LORA)])


def _gn_gate(y, rg, g):
    mu = jnp.mean(y, axis=-1, keepdims=True)
    var = jnp.mean(jnp.square(y - mu), axis=-1, keepdims=True)
    return _silu(rg) * ((y - mu) * lax.rsqrt(var + EPS) * g)


def _q_assemble_fwd(q, tabs, T):
    _, _, cos64, sin64 = tabs

    def body(q_ref, c_ref, s_ref, on_ref, op_ref):
        on_ref[...] = (q_ref[:, :1024] * Q_PRESCALE).astype(BF16)
        c, s = c_ref[...], s_ref[...]
        lane = lax.broadcasted_iota(jnp.int32, c.shape, 1)
        for j in range(HEADS // 2):
            r = _rope64(q_ref[:, 1024 + 128 * j:1024 + 128 * (j + 1)], c, s) * Q_PRESCALE
            op_ref[:, 256 * j:256 * j + 128] = jnp.where(lane < 64, r, 0.0).astype(BF16)
            op_ref[:, 256 * j + 128:256 * j + 256] = jnp.where(lane < 64, pltpu.roll(r, 64, 1), 0.0).astype(BF16)

    return _rows(body, T, 256, [(q, 1536, 0), (cos64, 128, 0), (sin64, 128, 0)], [(1024, BF16), (1024, BF16)], "q_assemble_fwd")


def _q_assemble_bwd(dqn, dqpe, tabs, T):
    _, _, cos64, sin64 = tabs

    def body(dn_ref, dp_ref, c_ref, s_ref, on_ref, op_ref):
        on_ref[...] = dn_ref[...].astype(BF16)
        c, s = c_ref[...], s_ref[...]
        lane = lax.broadcasted_iota(jnp.int32, c.shape, 1)
        for j in range(HEADS // 2):
            g = jnp.where(lane < 64, dp_ref[:, 256 * j:256 * j + 128], pltpu.roll(dp_ref[:, 256 * j + 128:256 * j + 256], 64, 1))
            op_ref[:, 128 * j:128 * (j + 1)] = _rope64_t(g, c, s).astype(BF16)

    return _rows(body, T, 256, [(dqn, 1024, 0), (dqpe, 1024, 0), (cos64, 128, 0), (sin64, 128, 0)],
                 [(1024, BF16), (512, BF16)], "q_assemble_bwd")


def _mix_fn(gr, gm, yr, ym):
    return _sigmoid(gr) * yr + _sigmoid(gm) * ym


def _mix_fwd(proj, y_ret, y_mla, T):
    def body(gr_ref, gm_ref, yr_ref, ym_ref, o_ref):
        o_ref[...] = _mix_fn(gr_ref[...], gm_ref[...], yr_ref[...], ym_ref[...]).astype(BF16)

    return _rows(body, T, 256, [(proj, 1024, P_GR // 1024), (proj, 1024, P_GM // 1024), (y_ret, D, 0), (y_mla, D, 0)],
                 [(D, BF16)], "mix_fwd")[0]


def _mix_bwd(proj, y_ret, y_mla, dmix, T):
    def body(gr_ref, gm_ref, yr_ref, ym_ref, d_ref, dgr_ref, dgm_ref, dyr_ref, dym_ref):
        _, vjp = jax.vjp(_mix_fn, gr_ref[...], gm_ref[...], yr_ref[...], ym_ref[...])
        dgr, dgm, dyr, dym = vjp(d_ref[...])
        dgr_ref[...] = dgr.astype(BF16)
        dgm_ref[...] = dgm.astype(BF16)
        dyr_ref[...] = dyr.astype(BF16)
        dym_ref[...] = dym.astype(BF16)

    return _rows(body, T, 256, [(proj, 1024, P_GR // 1024), (proj, 1024, P_GM // 1024), (y_ret, D, 0), (y_mla, D, 0), (dmix, D, 0)],
                 [(D, BF16)] * 4, "mix_bwd")


def _head(h3, gl, pp, tgt, g, b, T):
    def body(h_ref, gl_ref, pp_ref, t_ref, g_ref, b_ref, dh_ref, dgl_ref, dpp_ref, dg_ref, db_ref, loss_ref):
        def f(h, gg, p, lg, lb):
            return _ln(ALPHA * h + _sigmoid(gg) * p, lg, lb)

        h4, vjp = jax.vjp(f, h_ref[...], gl_ref[...], pp_ref[...], g_ref[...], b_ref[...])
        err = h4 - t_ref[...]
        dh, dgl, dpp, dg, db = vjp(err * (1.0 / D))
        dh_ref[...] = dh
        dgl_ref[...] = dgl.astype(BF16)
        dpp_ref[...] = dpp.astype(BF16)
        _accum(dg_ref, dg)
        _accum(db_ref, db)
        part = 0.5 * jnp.sum(jnp.mean(jnp.square(err), axis=-1, keepdims=True), axis=0, keepdims=True)
        _accum(loss_ref, jnp.broadcast_to(part, loss_ref.shape))

    return _rows(body, T, 256, [(h3, D, 0), (gl, D, 0), (pp, D, 0), (tgt, D, 0), (g, None, None), (b, None, None)],
                 [(D, F32), (D, BF16), (D, BF16)], "head", accs=[(1, D), (1, D), (8, 128)])


def _attn_delta(do, o, T):
    def body(d_ref, o_ref, db_ref, dl_ref):
        for h in range(HEADS):
            sl = slice(h * MLA_DV, (h + 1) * MLA_DV)
            d = d_ref[:, sl]
            dl = jnp.sum(d * o_ref[:, sl].astype(F32), axis=-1, keepdims=True)
            dl_ref[:, sl] = jnp.broadcast_to(dl, d.shape)
        db_ref[...] = d_ref[...].astype(BF16)

    return _rows(body, T, 256, [(do, D, 0), (o, D, 0)], [(D, BF16), (D, F32)], "attn_delta")


def _ret_consts():
    L = RET_L
    lg = jnp.log(1.0 - 2.0 ** (-5.0 - jnp.arange(HEADS, dtype=F32)))[:, None, None]
    idx = jnp.arange(L, dtype=F32)
    ch = jnp.arange(L) // CHUNK
    dist = idx[:, None] - idx[None, :]
    same = (ch[:, None] == ch[None, :])[None]
    earlier = (ch[None, :] < ch[:, None])[None]
    dm = jnp.where(same, jnp.exp(lg * jnp.abs(dist)[None]), jnp.where(earlier, jnp.exp(lg * dist[None]), 0.0))
    xi = jnp.broadcast_to(jnp.exp(lg * (idx + 1.0)[None, :, None]), (HEADS, L, 128))
    zeta = jnp.broadcast_to(jnp.exp(lg * (L - 1.0 - idx)[None, :, None]), (HEADS, L, 128))
    gl = jnp.broadcast_to(jnp.exp(lg * float(L)), (HEADS, 8, 128))
    return dm.astype(F32), xi.astype(F32), zeta.astype(F32), gl.astype(F32)


def _whole(arr):
    return pl.BlockSpec(arr.shape, lambda n, _nd=arr.ndim: (0,) * _nd)


def _ret_fwd(q, k, v, proj, gn_g, consts, T):
    dm, xi, zeta, gl = consts
    L = RET_L
    n_sc = T // L

    def body(q_ref, k_ref, v_ref, rg_ref, g_ref, dm_ref, xi_ref, ze_ref, gl_ref, y_ref, yr_ref, s_ref, st_ref):
        @pl.when(pl.program_id(0) == 0)
        def _():
            st_ref[...] = jnp.zeros_like(st_ref)

        for h in range(HEADS):
            ks, vs = slice(h * RET_DK, (h + 1) * RET_DK), slice(h * RET_DV, (h + 1) * RET_DV)
            qq, kk, vv = q_ref[:, ks], k_ref[:, ks], v_ref[:, vs]
            st = st_ref[h]
            s_ref[h, 0] = st
            p = (_dot(qq, kk, 1, 1) * dm_ref[h]).astype(BF16)
            cross = _dot(qq, st.astype(BF16), 1, 0)
            xi_c = jnp.concatenate([xi_ref[h], xi_ref[h]], axis=1)
            y = _dot(p, vv, 1, 0) + cross * xi_c
            y_ref[:, vs] = y
            yr_ref[:, vs] = _gn_gate(y, rg_ref[:, vs], g_ref[:, vs]).astype(BF16)
            kz = (kk.astype(F32) * ze_ref[h]).astype(BF16)
            gl2 = jnp.concatenate([gl_ref[h, 0:1, :], gl_ref[h, 0:1, :]], axis=1)
            st_ref[h] = st * gl2 + _dot(kz, vv, 0, 0)

    return _pcall(
        body, grid=(n_sc,),
        in_specs=[pl.BlockSpec((L, 1024), lambda n: (n, 0)), pl.BlockSpec((L, 1024), lambda n: (n, 0)),
                  pl.BlockSpec((L, 2048), lambda n: (n, 0)), pl.BlockSpec((L, 2048), lambda n: (n, P_RG // 2048)),
                  _whole(gn_g), _whole(dm), _whole(xi), _whole(zeta), _whole(gl)],
        out_specs=[pl.BlockSpec((L, 2048), lambda n: (n, 0)), pl.BlockSpec((L, 2048), lambda n: (n, 0)),
                   pl.BlockSpec((HEADS, 1, 128, 256), lambda n: (0, n, 0, 0))],
        out_shape=[jax.ShapeDtypeStruct((T, HEADS * RET_DV), F32), jax.ShapeDtypeStruct((T, HEADS * RET_DV), BF16),
                   jax.ShapeDtypeStruct((HEADS, n_sc, 128, 256), F32)],
        scratch_shapes=[pltpu.VMEM((HEADS, 128, 256), F32)], name="ret_fwd",
        compiler_params=pltpu.CompilerParams(dimension_semantics=("arbitrary",)),
    )(q, k, v, proj, gn_g, dm, xi, zeta, gl)


def _ret_bwd(q, k, v, y, proj, gn_g, states, dyr, consts, tabs, T):
    dm, xi, zeta, gl = consts
    cos128, sin128, _, _ = tabs
    L = RET_L
    n_sc = T // L

    def body(q_ref, k_ref, v_ref, y_ref, rg_ref, g_ref, d_ref, s_ref, dm_ref, xi_ref, ze_ref, gl_ref, c_ref, sn_ref,
             dq_ref, dk_ref, dv_ref, drg_ref, dg_ref, gs_ref):
        @pl.when(pl.program_id(0) == 0)
        def _():
            gs_ref[...] = jnp.zeros_like(gs_ref)

        c, sn = c_ref[...], sn_ref[...]
        dgs = []
        for h in range(HEADS):
            ks, vs = slice(h * RET_DK, (h + 1) * RET_DK), slice(h * RET_DV, (h + 1) * RET_DV)
            _, vjp = jax.vjp(_gn_gate, y_ref[:, vs], rg_ref[:, vs], g_ref[:, vs])
            dy, drg, dg = vjp(d_ref[:, vs])
            drg_ref[:, vs] = drg.astype(BF16)
            dgs.append(dg)
            qq, kk, vv, dyy = q_ref[:, ks], k_ref[:, ks], v_ref[:, vs], dy.astype(BF16)
            dmm = dm_ref[h]
            gb = gs_ref[h].astype(BF16)
            sb = s_ref[h, 0].astype(BF16)
            xi_c = jnp.concatenate([xi_ref[h], xi_ref[h]], axis=1)
            pb = (_dot(qq, kk, 1, 1) * dmm).astype(BF16)
            kz = (kk.astype(F32) * ze_ref[h]).astype(BF16)
            dv_ref[:, vs] = (_dot(pb, dyy, 0, 0) + _dot(kz, gb, 1, 0)).astype(BF16)
            da = (_dot(dyy, vv, 1, 1) * dmm).astype(BF16)
            dyx = (dyy.astype(F32) * xi_c).astype(BF16)
            dq = _dot(da, kk, 1, 0) + _dot(dyx, sb, 1, 1)
            dk = _dot(da, qq, 0, 0) + _dot(vv, gb, 1, 1) * ze_ref[h]
            dq_ref[:, ks] = _rope128_t(dq, c, sn).astype(BF16)
            dk_ref[:, ks] = (_rope128_t(dk, c, sn) * RET_SCALE).astype(BF16)
            gl2 = jnp.concatenate([gl_ref[h, 0:1, :], gl_ref[h, 0:1, :]], axis=1)
            gs_ref[h] = gs_ref[h] * gl2 + _dot(qq, dyx, 0, 0)
        _accum(dg_ref, jnp.concatenate(dgs, axis=1))

    rev = lambda n: n_sc - 1 - n
    return _pcall(
        body, grid=(n_sc,),
        in_specs=[pl.BlockSpec((L, 1024), lambda n: (rev(n), 0)), pl.BlockSpec((L, 1024), lambda n: (rev(n), 0)),
                  pl.BlockSpec((L, 2048), lambda n: (rev(n), 0)), pl.BlockSpec((L, 2048), lambda n: (rev(n), 0)),
                  pl.BlockSpec((L, 2048), lambda n: (rev(n), P_RG // 2048)), _whole(gn_g),
                  pl.BlockSpec((L, 2048), lambda n: (rev(n), 0)),
                  pl.BlockSpec((HEADS, 1, 128, 256), lambda n: (0, rev(n), 0, 0)),
                  _whole(dm), _whole(xi), _whole(zeta), _whole(gl),
                  pl.BlockSpec((L, 128), lambda n: (rev(n), 0)), pl.BlockSpec((L, 128), lambda n: (rev(n), 0))],
        out_specs=[pl.BlockSpec((L, 1024), lambda n: (rev(n), 0)), pl.BlockSpec((L, 1024), lambda n: (rev(n), 0)),
                   pl.BlockSpec((L, 2048), lambda n: (rev(n), 0)), pl.BlockSpec((L, 2048), lambda n: (rev(n), 0)),
                   pl.BlockSpec((1, 2048), lambda n: (0, 0))],
        out_shape=[jax.ShapeDtypeStruct((T, 1024), BF16), jax.ShapeDtypeStruct((T, 1024), BF16), jax.ShapeDtypeStruct((T, 2048), BF16),
                   jax.ShapeDtypeStruct((T, 2048), BF16), jax.ShapeDtypeStruct((1, 2048), F32)],
        scratch_shapes=[pltpu.VMEM((HEADS, 128, 256), F32)], name="ret_bwd",
        compiler_params=pltpu.CompilerParams(dimension_semantics=("arbitrary",)),
    )(q, k, v, y, proj, gn_g, dyr, states, dm, xi, zeta, gl, cos128, sin128)


def _diag_mask(nrows, ncols, row0):
    row = row0 + lax.broadcasted_iota(jnp.int32, (nrows, ncols), 0)
    col = lax.broadcasted_iota(jnp.int32, (nrows, ncols), 1)
    return lax.shift_right_logical(col, 6) <= lax.shift_right_logical(row, 6)


def _tri_steps(nb, by_key):
    if by_key:
        pairs = [(i, j) for j in range(nb) for i in range(j, nb)]
    else:
        pairs = [(i, j) for i in range(nb) for j in range(i + 1)]
    return jnp.array([a for a, _ in pairs], jnp.int32), jnp.array([b for _, b in pairs], jnp.int32)


def _attn_fwd(qn, qpe, kv, kpe, T):
    t = min(ATT_TF, T)
    nb = T // t
    ii, jj = _tri_steps(nb, by_key=False)

    hp = ATT_HP
    w = 128 * hp

    def body(ii_ref, jj_ref, qn_ref, qp_ref, kn_ref, kp_ref, v_ref, o_ref, lse_ref, m_sc, l_sc, acc_sc):
        st = pl.program_id(1)
        i, j = ii_ref[st], jj_ref[st]

        @pl.when(j == 0)
        def _():
            m_sc[...] = jnp.full_like(m_sc, NEG)
            l_sc[...] = jnp.zeros_like(l_sc)
            acc_sc[...] = jnp.zeros_like(acc_sc)

        def update(diag):
            kp = kp_ref[...]
            for hh in range(hp):
                sl = slice(128 * hh, 128 * (hh + 1))
                q = jnp.concatenate([qn_ref[:, sl], qp_ref[:, sl]], axis=1)
                k = jnp.concatenate([kn_ref[:, sl], kp], axis=1)
                s = _dot(q, k, 1, 1)
                if diag:
                    s = jnp.where(_diag_mask(t, t, 0), s, NEG)
                m_prev = m_sc[:, sl]
                m_new = jnp.maximum(m_prev, jnp.max(s, axis=1, keepdims=True))
                a = jnp.exp2(m_prev - m_new)
                p = jnp.exp2(s - m_new[:, 0:1])
                l_sc[:, sl] = a * l_sc[:, sl] + jnp.sum(p, axis=1, keepdims=True)
                acc_sc[:, sl] = a * acc_sc[:, sl] + _dot(p.astype(BF16), v_ref[:, sl], 1, 0)
                m_sc[:, sl] = m_new

        @pl.when(j < i)
        def _():
            update(False)

        @pl.when(j == i)
        def _():
            update(True)
            o_ref[...] = (acc_sc[...] / l_sc[...]).astype(BF16)
            lse_ref[...] = m_sc[...] + jnp.log2(l_sc[...])

    qs = pl.BlockSpec((t, w), lambda h, s, ii, jj: (ii[s], h))
    grid_spec = pltpu.PrefetchScalarGridSpec(
        num_scalar_prefetch=2, grid=(HEADS // hp, int(ii.shape[0])),
        in_specs=[qs, qs, pl.BlockSpec((t, w), lambda h, s, ii, jj: (jj[s], h)), pl.BlockSpec((t, 128), lambda h, s, ii, jj: (jj[s], 0)),
                  pl.BlockSpec((t, w), lambda h, s, ii, jj: (jj[s], HEADS // hp + h))],
        out_specs=[qs, qs],
        scratch_shapes=[pltpu.VMEM((t, w), F32), pltpu.VMEM((t, w), F32), pltpu.VMEM((t, w), F32)])
    return _pcall(
        body, grid_spec=grid_spec, out_shape=[jax.ShapeDtypeStruct((T, D), BF16), jax.ShapeDtypeStruct((T, D), F32)], name="attn_fwd",
        compiler_params=pltpu.CompilerParams(dimension_semantics=("arbitrary", "arbitrary")),
    )(ii, jj, qn, qpe, kv, kpe, kv)


def _attn_bwd(qn, qpe, kv, kpe, do, lse, delta, T):
    t = min(ATT_TB, T)
    nb = T // t
    ii, jj = _tri_steps(nb, by_key=True)

    def body(ii_ref, jj_ref, qn_ref, qp_ref, kn_ref, kp_ref, v_ref, do_ref, lse_ref, dl_ref,
             dqn_ref, dqp_ref, dkn_ref, dkp_ref, dv_ref, dk_sc, dv_sc):
        st = pl.program_id(1)
        i, j = ii_ref[st], jj_ref[st]

        @pl.when(st == 0)
        def _():
            dqn_ref[...] = jnp.zeros_like(dqn_ref)
            dqp_ref[...] = jnp.zeros_like(dqp_ref)

        @pl.when(i == j)
        def _():
            dk_sc[...] = jnp.zeros_like(dk_sc)
            dv_sc[...] = jnp.zeros_like(dv_sc)

        def update(diag):
            q = jnp.concatenate([qn_ref[...], qp_ref[...]], axis=1)
            k = jnp.concatenate([kn_ref[...], kp_ref[...]], axis=1)
            dob = do_ref[...]
            s = _dot(q, k, 1, 1)
            if diag:
                s = jnp.where(_diag_mask(t, t, 0), s, NEG)
            p = jnp.exp2(s - lse_ref[:, 0:1])
            dv_sc[...] += _dot(p.astype(BF16), dob, 0, 0)
            dp = _dot(dob, v_ref[...], 1, 1)
            ds = (p * (dp - dl_ref[:, 0:1])).astype(BF16)
            dk_sc[...] += _dot(ds, q, 0, 0)
            dq = _dot(ds, k, 1, 0) * MLA_SCALE
            rows = pl.ds(pl.multiple_of(i * t, t), t)
            dqn_ref[rows, :] += dq[:, :128]
            dqp_ref[rows, :] += dq[:, 128:]

        @pl.when(i > j)
        def _():
            update(False)

        @pl.when(i == j)
        def _():
            update(True)

        @pl.when(i == nb - 1)
        def _():
            dkn_ref[...] = (dk_sc[:, :128] * (1.0 / LOG2E)).astype(BF16)
            dkp_ref[...] = dk_sc[:, 128:] * (1.0 / LOG2E)
            dv_ref[...] = dv_sc[...].astype(BF16)

    qs = pl.BlockSpec((t, 128), lambda h, s, ii, jj: (ii[s], h))
    ks = pl.BlockSpec((t, 128), lambda h, s, ii, jj: (jj[s], h))
    hs = pl.BlockSpec((T, 128), lambda h, s, ii, jj: (0, h))
    grid_spec = pltpu.PrefetchScalarGridSpec(
        num_scalar_prefetch=2, grid=(HEADS, int(ii.shape[0])),
        in_specs=[qs, qs, ks, pl.BlockSpec((t, 128), lambda h, s, ii, jj: (jj[s], 0)),
                  pl.BlockSpec((t, 128), lambda h, s, ii, jj: (jj[s], HEADS + h)), qs, qs, qs],
        out_specs=[hs, hs, ks, ks, ks],
        scratch_shapes=[pltpu.VMEM((t, 256), F32), pltpu.VMEM((t, 128), F32)])
    return _pcall(
        body, grid_spec=grid_spec,
        out_shape=[jax.ShapeDtypeStruct((T, D), F32), jax.ShapeDtypeStruct((T, D), F32), jax.ShapeDtypeStruct((T, D), BF16),
                   jax.ShapeDtypeStruct((T, D), F32), jax.ShapeDtypeStruct((T, D), BF16)], name="attn_bwd",
        compiler_params=pltpu.CompilerParams(dimension_semantics=("arbitrary", "arbitrary")),
    )(ii, jj, qn, qpe, kv, kpe, kv, do, lse, delta)


def _mesh_pos():
    x, y, c = lax.axis_index("x"), lax.axis_index("y"), lax.axis_index("c")
    return x, y, c, 4 * x + 2 * y + c


def _peer(x, y, c, k):
    px, py, pc = (x + ((k >> 2) & 1)) % 2, (y + ((k >> 1) & 1)) % 2, (c + (k & 1)) % 2
    return (px, py, pc), 4 * px + 2 * py + pc


_ANY = pl.BlockSpec(memory_space=pl.ANY)


def _rcopy(src, dst, send_sems, recv_sems, k, to):
    return pltpu.make_async_remote_copy(src_ref=src, dst_ref=dst, send_sem=send_sems.at[k], recv_sem=recv_sems.at[k],
                                        device_id=to, device_id_type=pl.DeviceIdType.MESH)


def _all_gather(wsh, ssh):
    def body(w_ref, s_ref, wall_ref, sall_ref, send_sems, recv_sems, loc_sems):
        x, y, c, me = _mesh_pos()
        sib = (x, y, 1 - c)
        chips = [(1 - x, y), (x, 1 - y), (1 - x, 1 - y)]
        slot = lambda px, py, pc: 4 * px + 2 * py + pc
        loc = [pltpu.make_async_copy(w_ref, wall_ref.at[me], loc_sems.at[0]),
               pltpu.make_async_copy(s_ref, sall_ref.at[me], loc_sems.at[1])]
        for cp in loc:
            cp.start()
        sends, fwd_waits = [], []
        for n, (src, dst) in enumerate(((w_ref, wall_ref), (s_ref, sall_ref))):
            o = 7 * n
            sends.append(_rcopy(src, dst.at[me], send_sems, recv_sems, o, sib))
            for j, chip in enumerate(chips):
                sends.append(_rcopy(src, dst.at[me], send_sems, recv_sems, o + 1 + j, (*chip, c)))
        for cp in sends:
            cp.start()
        for n, (src, dst) in enumerate(((w_ref, wall_ref), (s_ref, sall_ref))):
            o = 7 * n
            for j, chip in enumerate(chips):
                got = dst.at[slot(*chip, c)]
                _rcopy(src, got, send_sems, recv_sems, o + 1 + j, sib).wait_recv()
                fw = _rcopy(got, got, send_sems, recv_sems, o + 4 + j, sib)
                fw.start()
                sends.append(fw)
            fwd_waits.append(_rcopy(src, dst.at[slot(x, y, 1 - c)], send_sems, recv_sems, o, sib))
            for j, chip in enumerate(chips):
                fwd_waits.append(_rcopy(src, dst.at[slot(*chip, 1 - c)], send_sems, recv_sems, o + 4 + j, sib))
        for cp in fwd_waits:
            cp.wait_recv()
        for cp in sends:
            cp.wait_send()
        for cp in loc:
            cp.wait()

    return _pcall(
        body, in_specs=[_ANY, _ANY], out_specs=[_ANY, _ANY],
        out_shape=[jax.ShapeDtypeStruct((N_DEV,) + wsh.shape, wsh.dtype), jax.ShapeDtypeStruct((N_DEV,) + ssh.shape, ssh.dtype)],
        scratch_shapes=[pltpu.SemaphoreType.DMA((14,)), pltpu.SemaphoreType.DMA((14,)), pltpu.SemaphoreType.DMA((2,))],
        name="all_gather_weights",
    )(wsh, ssh)


_HBM = pl.BlockSpec(memory_space=pltpu.HBM)
_SEM = pl.BlockSpec(memory_space=pltpu.SEMAPHORE)
_EFFECT = pltpu.SideEffectType.DATAFLOW_SIDE_EFFECTING


def _other_chips(x, y):
    return [(1 - x, y), (x, 1 - y), (1 - x, 1 - y)]


def _gather_start(wsh, order_after):
    def body(w_ref, land_ref, dep_ref, send_sems, recv_sems, w_thru, land_thru, token):
        x, y, c, me = _mesh_pos()
        _rcopy(w_ref, land_ref.at[me], send_sems, recv_sems, 0, (x, y, 1 - c)).start()
        for j, chip in enumerate(_other_chips(x, y)):
            _rcopy(w_ref, land_ref.at[me], send_sems, recv_sems, 1 + j, (*chip, c)).start()
        token[...] = jnp.zeros_like(token)

    shape = (N_DEV,) + wsh.shape
    land = pltpu.with_memory_space_constraint(lax.empty(shape, wsh.dtype), pltpu.HBM)
    return _pcall(
        body, name="gather_start",
        out_shape=(pltpu.SemaphoreType.DMA((4,)), pltpu.SemaphoreType.DMA((4,)), pltpu.HBM(wsh.shape, wsh.dtype),
                   pltpu.HBM(shape, wsh.dtype), jax.ShapeDtypeStruct((8, 128), F32)),
        in_specs=(_HBM, _HBM, _ANY), out_specs=(_SEM, _SEM, _HBM, _HBM, pl.BlockSpec(memory_space=pltpu.VMEM)),
        input_output_aliases={0: 2, 1: 3}, compiler_params=pltpu.CompilerParams(has_side_effects=_EFFECT),
    )(pltpu.with_memory_space_constraint(wsh, pltpu.HBM), land, order_after)


def _gather_wait(send_sems, recv_sems, w_thru, land_thru, after):
    def body(w_ref, land_ref, send_sems, recv_sems, after_ref, w_out, land_out):
        x, y, c, _ = _mesh_pos()
        senders = [(x, y, 1 - c)] + [(*chip, c) for chip in _other_chips(x, y)]
        for k, (px, py, pc) in enumerate(senders):
            cp = _rcopy(w_ref, land_ref.at[4 * px + 2 * py + pc], send_sems, recv_sems, k, (px, py, pc))
            cp.wait_send()
            cp.wait_recv()

    return _pcall(
        body, name="gather_wait", out_shape=(pltpu.HBM(w_thru.shape, w_thru.dtype), pltpu.HBM(land_thru.shape, land_thru.dtype)),
        in_specs=(_HBM, _HBM, _SEM, _SEM, _ANY), out_specs=(_HBM, _HBM), input_output_aliases={0: 0, 1: 1},
        compiler_params=pltpu.CompilerParams(has_side_effects=_EFFECT),
    )(w_thru, land_thru, send_sems, recv_sems, after)


def _gather_finish(wsh, land):
    def body(w_ref, land_ref, out_ref, send_sems, recv_sems, loc_sem):
        x, y, c, me = _mesh_pos()
        sib = (x, y, 1 - c)
        own = pltpu.make_async_copy(w_ref, out_ref.at[me], loc_sem.at[0])
        own.start()
        sends, recvs = [], []
        for j, (px, py) in enumerate(_other_chips(x, y)):
            mine, theirs = 4 * px + 2 * py + c, 4 * px + 2 * py + (1 - c)
            sends.append(_rcopy(out_ref.at[mine], out_ref.at[mine], send_sems, recv_sems, j, sib))
            recvs.append(_rcopy(out_ref.at[theirs], out_ref.at[theirs], send_sems, recv_sems, j, sib))
        for cp in sends:
            cp.start()
        for cp in recvs:
            cp.wait_recv()
        for cp in sends:
            cp.wait_send()
        own.wait()

    return _pcall(
        body, in_specs=[_ANY, _ANY], out_specs=_ANY, out_shape=jax.ShapeDtypeStruct(land.shape, land.dtype),
        input_output_aliases={1: 0},
        scratch_shapes=[pltpu.SemaphoreType.DMA((3,)), pltpu.SemaphoreType.DMA((3,)), pltpu.SemaphoreType.DMA((1,))],
        name="gather_finish",
    )(wsh, land)


_ROW_TILES = (400, 368, 352, 256, 128, 16, 8)


def _exchange_sibling(g4, name):
    def body(g_ref, r_ref, send_sems, recv_sems):
        x, y, c, _ = _mesh_pos()
        sib = (x, y, 1 - c)
        cps = [_rcopy(g_ref.at[q, 1 - c], r_ref.at[q], send_sems, recv_sems, q, sib) for q in range(4)]
        for cp in cps:
            cp.start()
        for cp in cps:
            cp.wait()

    n, _, R, C = g4.shape
    return _pcall(
        body, in_specs=[_ANY], out_specs=_ANY, out_shape=jax.ShapeDtypeStruct((n, R, C), g4.dtype),
        scratch_shapes=[pltpu.SemaphoreType.DMA((4,)), pltpu.SemaphoreType.DMA((4,))], name=name,
    )(g4)


def _sum_sibling(g4, rsib, cvec, name):
    n, _, R, C = g4.shape
    tr = _pick(R, _ROW_TILES)

    def body(c_ref, g_ref, r_ref, o_ref):
        o_ref[...] = (g_ref[...].astype(F32) + r_ref[...].astype(F32)).astype(o_ref.dtype)

    grid_spec = pltpu.PrefetchScalarGridSpec(
        num_scalar_prefetch=1, grid=(n, R // tr),
        in_specs=[pl.BlockSpec((None, None, tr, C), lambda q, i, cr: (q, cr[0], i, 0)), pl.BlockSpec((None, tr, C), lambda q, i, cr: (q, i, 0))],
        out_specs=pl.BlockSpec((None, tr, C), lambda q, i, cr: (q, i, 0)))
    return _pcall(body, grid_spec=grid_spec, out_shape=jax.ShapeDtypeStruct((n, R, C), g4.dtype), name=name)(cvec, g4, rsib)


def _chips_start(part):
    def body(p_ref, land_ref, send_sems, recv_sems, p_thru, land_thru, token):
        x, y, c, _ = _mesh_pos()
        myq = 2 * x + y
        for j, (px, py) in enumerate(_other_chips(x, y)):
            _rcopy(p_ref.at[2 * px + py], land_ref.at[myq], send_sems, recv_sems, j, (px, py, c)).start()
        token[...] = jnp.zeros_like(token)

    land = pltpu.with_memory_space_constraint(lax.empty(part.shape, part.dtype), pltpu.HBM)
    return _pcall(
        body, name="chips_start",
        out_shape=(pltpu.SemaphoreType.DMA((3,)), pltpu.SemaphoreType.DMA((3,)), pltpu.HBM(part.shape, part.dtype),
                   pltpu.HBM(part.shape, part.dtype), jax.ShapeDtypeStruct((8, 128), F32)),
        in_specs=(_HBM, _HBM), out_specs=(_SEM, _SEM, _HBM, _HBM, pl.BlockSpec(memory_space=pltpu.VMEM)),
        input_output_aliases={0: 2, 1: 3}, compiler_params=pltpu.CompilerParams(has_side_effects=_EFFECT),
    )(pltpu.with_memory_space_constraint(part, pltpu.HBM), land)


def _chips_wait(send_sems, recv_sems, p_thru, land_thru, after):
    def body(p_ref, land_ref, send_sems, recv_sems, after_ref, p_out, land_out):
        x, y, c, _ = _mesh_pos()
        for j, (px, py) in enumerate(_other_chips(x, y)):
            q = 2 * px + py
            cp = _rcopy(p_ref.at[q], land_ref.at[q], send_sems, recv_sems, j, (px, py, c))
            cp.wait_send()
            cp.wait_recv()

    return _pcall(
        body, name="chips_wait", out_shape=(pltpu.HBM(p_thru.shape, p_thru.dtype), pltpu.HBM(p_thru.shape, p_thru.dtype)),
        in_specs=(_HBM, _HBM, _SEM, _SEM, _ANY), out_specs=(_HBM, _HBM), input_output_aliases={0: 0, 1: 1},
        compiler_params=pltpu.CompilerParams(has_side_effects=_EFFECT),
    )(p_thru, land_thru, send_sems, recv_sems, after)


def _sum_chips(part, land, qvec, name):
    n, R, C = part.shape
    tr = _pick(R, _ROW_TILES)

    def body(q_ref, p_ref, *refs):
        o_ref = refs[n]
        acc = None
        for q in range(n):
            term = jnp.where(q_ref[0] == q, p_ref[...], refs[q][...]).astype(F32)
            acc = term if acc is None else acc + term
        o_ref[...] = acc

    def land_spec(q):
        return pl.BlockSpec((None, tr, C), lambda i, m: (jnp.where(m[0] == q, (q + 1) % n, q), i, 0))

    grid_spec = pltpu.PrefetchScalarGridSpec(
        num_scalar_prefetch=1, grid=(R // tr,),
        in_specs=[pl.BlockSpec((None, tr, C), lambda i, m: (m[0], i, 0))] + [land_spec(q) for q in range(n)],
        out_specs=pl.BlockSpec((tr, C), lambda i, m: (i, 0)))
    return _pcall(body, grid_spec=grid_spec, out_shape=jax.ShapeDtypeStruct((R, C), F32), name=name)(qvec, part, *([land] * n))


def _exchange_chips(part, gsmall):
    def body(p_ref, s_ref, recv_ref, srecv_ref, send_sems, recv_sems, loc_sems):
        x, y, c, me = _mesh_pos()
        myq = 2 * x + y
        loc = [pltpu.make_async_copy(p_ref.at[myq], recv_ref.at[myq], loc_sems.at[0]),
               pltpu.make_async_copy(s_ref, srecv_ref.at[me], loc_sems.at[1])]
        for cp in loc:
            cp.start()
        sends, recvs = [], []
        for j, (px, py) in enumerate([(1 - x, y), (x, 1 - y), (1 - x, 1 - y)]):
            q = 2 * px + py
            sends.append(_rcopy(p_ref.at[q], recv_ref.at[myq], send_sems, recv_sems, j, (px, py, c)))
            recvs.append(_rcopy(p_ref.at[q], recv_ref.at[q], send_sems, recv_sems, j, (px, py, c)))
        for k in range(1, N_DEV):
            to, pidx = _peer(x, y, c, k)
            sends.append(_rcopy(s_ref, srecv_ref.at[me], send_sems, recv_sems, 2 + k, to))
            recvs.append(_rcopy(s_ref, srecv_ref.at[pidx], send_sems, recv_sems, 2 + k, to))
        for cp in sends:
            cp.start()
        for cp in recvs:
            cp.wait_recv()
        for cp in sends:
            cp.wait_send()
        for cp in loc:
            cp.wait()

    return _pcall(
        body, in_specs=[_ANY, _ANY], out_specs=[_ANY, _ANY],
        out_shape=[jax.ShapeDtypeStruct(part.shape, part.dtype), jax.ShapeDtypeStruct((N_DEV,) + gsmall.shape, gsmall.dtype)],
        scratch_shapes=[pltpu.SemaphoreType.DMA((10,)), pltpu.SemaphoreType.DMA((10,)), pltpu.SemaphoreType.DMA((2,))],
        name="exchange_chips",
    )(part, gsmall)


def _sum_slots(recv, name):
    n, R, C = recv.shape
    tr = _pick(R, _ROW_TILES)

    def body(r_ref, o_ref):
        acc = r_ref[0].astype(F32)
        for s in range(1, n):
            acc = acc + r_ref[s].astype(F32)
        o_ref[...] = acc

    return _pcall(body, grid=(R // tr,), in_specs=[pl.BlockSpec((n, tr, C), lambda i: (0, i, 0))],
                  out_specs=pl.BlockSpec((tr, C), lambda i: (i, 0)), out_shape=jax.ShapeDtypeStruct((R, C), F32), name=name)(recv)


def _adamw(w, g, m, v, name):
    shape = w.shape
    w2, g2, m2, v2 = (a.reshape(-1, shape[-1]) for a in (w, g, m, v))
    R, C = w2.shape
    tr = _pick(R, (256, 128, 64, 32, 16, 8)) if R > 256 else R

    def body(w_ref, g_ref, m_ref, v_ref, d_ref, nm_ref, nv_ref):
        gg = g_ref[...]
        nm = ADAM_B1 * m_ref[...] + (1.0 - ADAM_B1) * gg
        nv = ADAM_B2 * v_ref[...] + (1.0 - ADAM_B2) * jnp.square(gg)
        m_hat = nm / (1.0 - ADAM_B1 ** ADAM_STEP)
        v_hat = nv / (1.0 - ADAM_B2 ** ADAM_STEP)
        d_ref[...] = -ADAM_LR * (m_hat / (jnp.sqrt(v_hat) + ADAM_EPS) + ADAM_WD * w_ref[...])
        nm_ref[...] = nm
        nv_ref[...] = nv

    spec = pl.BlockSpec((tr, C), lambda i: (i, 0))
    d, nm, nv = _pcall(body, grid=(R // tr,), in_specs=[spec] * 4, out_specs=[spec] * 3,
                       out_shape=[jax.ShapeDtypeStruct((R, C), F32)] * 3, name=name)(w2, g2, m2, v2)
    return d.reshape(shape), nm.reshape(shape), nv.reshape(shape)


def _to_rows(name, w):
    w = w[0]
    if name in ("ffn1_w_in", "ffn2_w_in", "w_in"):
        return w.T
    if name in ("w_uq", "w_ukv", "ple_w_proj"):
        return w.T.reshape(-1, 1024)
    return w


def _from_rows(name, g, shape):
    if name in ("ffn1_w_in", "ffn2_w_in", "w_in"):
        return g.T.reshape(shape)
    if name in ("w_uq", "w_ukv", "ple_w_proj"):
        return g.reshape(-1, shape[1]).T.reshape(shape)
    return g.reshape(shape)


def _unpack(wall, group):
    out, off = {}, 0
    for name, r in group:
        out[name] = wall[:, off:off + r, :].reshape(N_DEV * r, 1024)
        off += _pad16(r)
    return out


def _w_in_internal(wt):
    return jnp.concatenate([wt[0:6144], wt[6720:8768], wt[6144:6720], jnp.zeros((P_W - W_IN_COLS, 1024), wt.dtype)], axis=0)


def _w_in_external(d):
    return jnp.concatenate([d[0:6144], d[8192:8768], d[6144:8192]], axis=0)


def _rope_tables(positions):
    pos = positions[0].astype(F32)

    def cs(half):
        inv = ROPE_BASE ** (-jnp.arange(half, dtype=F32) / half)
        ang = pos[:, None] * inv
        return jnp.cos(ang), jnp.sin(ang)

    c, s = cs(64)
    c2, s2 = cs(32)
    return (jnp.concatenate([c, c], axis=1), jnp.concatenate([-s, s], axis=1),
            jnp.concatenate([c2, c2, c2, c2], axis=1), jnp.concatenate([-s2, s2, -s2, s2], axis=1))


def _local_step(x, p, positions, target, W, ln_g, ln_b, gn_g, qg, kvg, early=None, rest_weights=None, start_token=None):
    T = x.shape[0]
    tabs = _rope_tables(positions)
    rc = _ret_consts()
    lg = [ln_g[i:i + 1] for i in range(4)]
    lb = [ln_b[i:i + 1] for i in range(4)]
    xb = (x if start_token is None else x + start_token[0:1, 0:1]).astype(BF16)
    pb = p.astype(BF16)

    hs1, gu1 = _ffn_in(xb, W["ffn1_w_in"], "ffn1_in")
    f1, h1, h1b = _mm(hs1, W["ffn1_w_out"], name="ffn1_out", tm=LN_TM, epilogue=_ln_epilogue(x, 0.5, lg[0], lb[0]))
    if rest_weights is not None:
        W = {**W, **rest_weights(h1b)}
    w_in_t = _w_in_internal(W["w_in"])
    wuq = W["w_uq"].reshape(1536, LORA).reshape(HEADS, 192, LORA)
    wuq = jnp.concatenate([wuq[:, :128].reshape(1024, LORA), wuq[:, 128:].reshape(512, LORA)], axis=0)
    wukv = W["w_ukv"].reshape(2048, LORA).reshape(HEADS, 2, 128, LORA).transpose(1, 0, 2, 3).reshape(2048, LORA)
    wp_t = W["ple_w_proj"].reshape(1024, D_PLE)
    proj = _mm(h1b, w_in_t, tb=True, name="mixer_in")
    rq, rk, rv, qn, kvn, kpe = _mixer_prep_fwd(proj, tabs, qg, kvg, T)
    y, yr, states = _ret_fwd(rq, rk, rv, proj, gn_g, rc, T)
    y_ret = _mm(yr, W["w_ret_o"], name="ret_o")
    q = _mm(qn, wuq, tb=True, name="mla_uq")
    kv = _mm(kvn, wukv, tb=True, out_dtype=BF16, name="mla_ukv")
    qnope, qpe = _q_assemble_fwd(q, tabs, T)
    o, lse = _attn_fwd(qnope, qpe, kv, kpe, T)
    y_mla = _mm(o, W["w_mla_o"], name="mla_o")
    mix = _mix_fwd(proj, y_ret, y_mla, T)
    mixed, h2, h2b = _mm(mix, W["w_out"], name="mixer_out", tm=LN_TM, epilogue=_ln_epilogue(h1, 1.0, lg[1], lb[1]))
    hs2, gu2 = _ffn_in(h2b, W["ffn2_w_in"], "ffn2_in")
    f2, h3, h3b = _mm(hs2, W["ffn2_w_out"], name="ffn2_out", tm=LN_TM, epilogue=_ln_epilogue(h2, 0.5, lg[2], lb[2]))
    gl = _mm(h3b, W["ple_w_gate"], name="ple_gate")
    pp = _mm(pb, wp_t, tb=True, name="ple_proj")

    G = {}
    dh3_a, dgl, dpp, dg3, db3, loss = _head(h3, gl, pp, target, lg[3], lb[3], T)
    G["ple_w_gate"] = _mm(h3b, dgl, ta=True, name="d_ple_gate")
    G["ple_w_proj"] = _mm(dpp, pb, ta=True, name="d_ple_proj")
    dh2_a, df2, dg2, db2 = _mm(dgl, W["ple_w_gate"], tb=True, add=dh3_a, name="dh3", tm=LN_TM,
                               epilogue=_ln_bwd_epilogue(h2, f2, 0.5, lg[2], lb[2]))
    G["ffn2_w_out"] = _mm(hs2, df2, ta=True, name="d_ffn2_out")
    da2 = _ffn_act_bwd(df2, W["ffn2_w_out"], gu2, "ffn2_act_bwd")
    G["ffn2_w_in"] = _mm(da2, h2b, ta=True, name="d_ffn2_in")
    dh1_a, dmixed, dg1, db1 = _mm(da2, W["ffn2_w_in"], add=dh2_a, name="dh2", tm=LN_TM,
                                  epilogue=_ln_bwd_epilogue(h1, mixed, 1.0, lg[1], lb[1]))
    G["w_out"] = _mm(mix, dmixed, ta=True, name="d_mixer_out")
    dmix = _mm(dmixed, W["w_out"], tb=True, name="dmix")
    dgr, dgm, dy_ret, dy_mla = _mix_bwd(proj, y_ret, y_mla, dmix, T)
    G["w_mla_o"] = _mm(o, dy_mla, ta=True, name="d_mla_o")
    do = _mm(dy_mla, W["w_mla_o"], tb=True, name="do")
    dob, delta = _attn_delta(do, o, T)
    dqn_f, dqpe_f, dkn, dkpe_all, dv = _attn_bwd(qnope, qpe, kv, kpe, dob, lse, delta, T)
    dq_n, dq_r = _q_assemble_bwd(dqn_f, dqpe_f, tabs, T)
    g_uq = jnp.concatenate([_mm(dq_n, qn, ta=True, name="d_uq_nope"), _mm(dq_r, qn, ta=True, name="d_uq_rope")], axis=0)
    g_uq = jnp.concatenate([g_uq[:1024].reshape(HEADS, 128, LORA), g_uq[1024:].reshape(HEADS, 64, LORA)], axis=1)
    G["w_uq"] = g_uq.reshape(1536 * LORA // 1024, 1024)
    dqn = _mm(dq_r, wuq[1024:], add=_mm(dq_n, wuq[:1024], name="dqn_a"), name="dqn_b")
    g_ukv = jnp.stack([_mm(dkn, kvn, ta=True, name="d_ukv_k"), _mm(dv, kvn, ta=True, name="d_ukv_v")], axis=0)
    G["w_ukv"] = g_ukv.reshape(2, HEADS, 128, LORA).transpose(1, 0, 2, 3).reshape(2048 * LORA // 1024, 1024)
    dkvn = _mm(dv, wukv[1024:], add=_mm(dkn, wukv[:1024], name="dkvn_a"), name="dkvn_b")
    dcq, dckv, dkpe, dqg, dkvg = _rms_bwd(proj, dqn, dkvn, dkpe_all, tabs, qg, kvg, T)
    G["w_ret_o"] = _mm(yr, dy_ret, ta=True, name="d_ret_o")
    dyr = _mm(dy_ret, W["w_ret_o"], tb=True, name="dyr")
    drq, drk, drv, drg, dgn = _ret_bwd(rq, rk, rv, y, proj, gn_g, states, dyr, rc, tabs, T)
    dproj = jnp.concatenate([drq, drk, drv, drg, dgr, dgm, dcq, dckv, dkpe, jnp.zeros((T, P_W - P_KPE - 128), BF16)], axis=1)
    G["w_in"] = _w_in_external(_mm(dproj, h1b, ta=True, name="d_mixer_in"))
    lg0 = lg[0] if early is None else lg[0] + early(G)[0:1, 0:1]
    dx_a, df1, dg0, db0 = _mm(dproj, w_in_t, add=dh1_a, name="dh1", tm=LN_TM,
                              epilogue=_ln_bwd_epilogue(x, f1, 0.5, lg0, lb[0]))
    G["ffn1_w_out"] = _mm(hs1, df1, ta=True, name="d_ffn1_out")
    da1 = _ffn_act_bwd(df1, W["ffn1_w_out"], gu1, "ffn1_act_bwd")
    G["ffn1_w_in"] = _mm(da1, xb, ta=True, name="d_ffn1_in")
    grad_x = _mm(da1, W["ffn1_w_in"], add=dx_a, name="grad_x")

    small = dict(ln_g=jnp.concatenate([dg0, dg1, dg2, dg3], axis=0), ln_b=jnp.concatenate([db0, db1, db2, db3], axis=0),
                 ret_gn_g=dgn, q_norm_g=dqg, kv_norm_g=dkvg)
    return loss, grad_x, G, small


def kernel(x, p, positions, ln_g, ln_b, ffn1_w_in, ffn1_w_out, w_in, ret_gn_g, w_ret_o, q_norm_g, kv_norm_g, w_uq, w_ukv, w_mla_o, w_out, ffn2_w_in, ffn2_w_out, ple_w_gate, ple_w_proj, loss_target, m_ln_g, m_ln_b, m_ffn1_w_in, m_ffn1_w_out, m_w_in, m_ret_gn_g, m_w_ret_o, m_q_norm_g, m_kv_norm_g, m_w_uq, m_w_ukv, m_w_mla_o, m_w_out, m_ffn2_w_in, m_ffn2_w_out, m_ple_w_gate, m_ple_w_proj, v_ln_g, v_ln_b, v_ffn1_w_in, v_ffn1_w_out, v_w_in, v_ret_gn_g, v_w_ret_o, v_q_norm_g, v_kv_norm_g, v_w_uq, v_w_ukv, v_w_mla_o, v_w_out, v_ffn2_w_in, v_ffn2_w_out, v_ple_w_gate, v_ple_w_proj):
    names = ("ln_g", "ln_b", "ffn1_w_in", "ffn1_w_out", "w_in", "ret_gn_g", "w_ret_o", "q_norm_g", "kv_norm_g", "w_uq", "w_ukv",
             "w_mla_o", "w_out", "ffn2_w_in", "ffn2_w_out", "ple_w_gate", "ple_w_proj")
    ws = dict(zip(names, (ln_g, ln_b, ffn1_w_in, ffn1_w_out, w_in, ret_gn_g, w_ret_o, q_norm_g, kv_norm_g, w_uq, w_ukv, w_mla_o,
                          w_out, ffn2_w_in, ffn2_w_out, ple_w_gate, ple_w_proj)))
    ms = dict(zip(names, (m_ln_g, m_ln_b, m_ffn1_w_in, m_ffn1_w_out, m_w_in, m_ret_gn_g, m_w_ret_o, m_q_norm_g, m_kv_norm_g, m_w_uq,
                          m_w_ukv, m_w_mla_o, m_w_out, m_ffn2_w_in, m_ffn2_w_out, m_ple_w_gate, m_ple_w_proj)))
    vs = dict(zip(names, (v_ln_g, v_ln_b, v_ffn1_w_in, v_ffn1_w_out, v_w_in, v_ret_gn_g, v_w_ret_o, v_q_norm_g, v_kv_norm_g, v_w_uq,
                          v_w_ukv, v_w_mla_o, v_w_out, v_ffn2_w_in, v_ffn2_w_out, v_ple_w_gate, v_ple_w_proj)))

    parts = []
    for name, r in PACK:
        rows = _to_rows(name, ws[name])
        if _pad16(r) != r:
            rows = jnp.concatenate([rows, jnp.zeros((_pad16(r) - r, 1024), F32)], axis=0)
        parts.append(rows)
    wsh_first = jnp.concatenate(parts[:len(PACK_LATE)], axis=0).astype(BF16)
    wsh_rest = jnp.concatenate(parts[len(PACK_LATE):], axis=0).astype(BF16)
    ssh = jnp.concatenate([ln_g[0], ln_b[0]], axis=0)
    wall_first, sall = _all_gather(wsh_first, ssh)
    *gather_handles, start_token = _gather_start(wsh_rest, wall_first)
    W = _unpack(wall_first, PACK_LATE)
    ln_full = sall.reshape(N_DEV, 2, 4, 128).transpose(1, 2, 0, 3).reshape(2, 4, 1024)

    def rest_weights(after):
        w_thru, land = _gather_wait(*gather_handles, after)
        return _unpack(_gather_finish(w_thru, land), PACK_EARLY)

    cvec = lax.axis_index("c").astype(jnp.int32).reshape(1)
    qvec = (2 * lax.axis_index("x") + lax.axis_index("y")).astype(jnp.int32).reshape(1)

    def chip_partials(G, group, tag):
        gparts = []
        for name, r in group:
            g = G[name].reshape(N_DEV, r, 1024)
            if _pad16(r) != r:
                g = jnp.concatenate([g, jnp.zeros((N_DEV, _pad16(r) - r, 1024), F32)], axis=1)
            gparts.append(g)
        gfull = jnp.concatenate(gparts, axis=1).astype(BF16)
        g4 = gfull.reshape(4, 2, gfull.shape[1], 1024)
        return _sum_sibling(g4, _exchange_sibling(g4, "exchange_sibling_" + tag), cvec, "sum_sibling_" + tag)

    in_flight = []

    def early(G):
        *handles, token = _chips_start(chip_partials(G, PACK_EARLY, "early"))
        in_flight.append(handles)
        return token

    loss_p, grad_x, G, small = _local_step(x[0], p[0, 0], positions, loss_target[0], W, ln_full[0], ln_full[1],
                                           ret_gn_g, q_norm_g, kv_norm_g, early=early, rest_weights=rest_weights,
                                           start_token=start_token)

    part_e, land_e = _chips_wait(*in_flight[0], grad_x)
    gsh_early = _sum_chips(part_e, land_e, qvec, "sum_grads_early")
    pad256 = lambda a: jnp.concatenate([a, jnp.zeros((1, 1024 - a.shape[1]), F32)], axis=1)
    gsmall = jnp.concatenate([small["ln_g"], small["ln_b"], small["ret_gn_g"].reshape(2, 1024), pad256(small["q_norm_g"]),
                              pad256(small["kv_norm_g"]), jnp.zeros((SMALL_ROWS - 12, 1024), F32)], axis=0)
    recv, srecv = _exchange_chips(chip_partials(G, PACK_LATE, "late"), gsmall)
    gsh_late = _sum_slots(recv, "sum_grads_late")
    ssum = _sum_slots(srecv, "sum_small_grads")

    grads = {}
    for group, gsh in ((PACK_LATE, gsh_late), (PACK_EARLY, gsh_early)):
        off = 0
        for name, r in group:
            grads[name] = _from_rows(name, gsh[off:off + r], ws[name].shape)
            off += _pad16(r)
    me = 4 * lax.axis_index("x") + 2 * lax.axis_index("y") + lax.axis_index("c")
    grads["ln_g"] = lax.dynamic_slice(ssum[0:4], (0, me * 128), (4, 128)).reshape(1, 4, 128)
    grads["ln_b"] = lax.dynamic_slice(ssum[4:8], (0, me * 128), (4, 128)).reshape(1, 4, 128)
    grads["ret_gn_g"] = ssum[8:10].reshape(1, 2048)
    grads["q_norm_g"] = ssum[10:11, :256]
    grads["kv_norm_g"] = ssum[11:12, :256]

    delta, new_m, new_v = {}, {}, {}
    for name in names:
        delta[name], new_m[name], new_v[name] = _adamw(ws[name], grads[name], ms[name], vs[name], "adamw_" + name)

    loss = lax.psum(loss_p[0, 0], ("x", "y", "c"))
    return (loss, grad_x[None], *[grads[n] for n in names], *[delta[n] for n in names],
            *[new_m[n] for n in names], *[new_v[n] for n in names])
```

```python
import functools
import math

import jax
import jax.numpy as jnp
from jax import lax
from jax.experimental import pallas as pl
from jax.experimental.pallas import tpu as pltpu

F32 = jnp.float32
BF16 = jnp.bfloat16

N_DEV = 8
D = 1024
D_FF = 2816
D_PLE = 256
CHUNK = 64
HEADS = 8
RET_DK = 128
RET_DV = 256
MLA_NOPE = 128
MLA_ROPE = 64
MLA_DV = 128
LORA = 256
ROPE_BASE = 10000.0
EPS = 1e-5
ALPHA = 2.0 ** 0.25
RET_SCALE = RET_DK ** -0.5
MLA_SCALE = (MLA_NOPE + MLA_ROPE) ** -0.5
NEG = -1e30

ADAM_LR = 0.001
ADAM_B1 = 0.9
ADAM_B2 = 0.999
ADAM_EPS = 1e-08
ADAM_WD = 0.01
ADAM_STEP = 10

P_RQ, P_RK, P_RV, P_RG, P_GR, P_GM, P_CQ, P_CKV, P_KPE, P_W = 0, 1024, 2048, 4096, 6144, 7168, 8192, 8448, 8704, 8960
W_IN_COLS = 8768
RET_L = 256
ATT_TF = 1024
ATT_TB = 1024
ATT_HP = 2
LOG2E = math.log2(math.e)
Q_PRESCALE = MLA_SCALE * LOG2E

PACK = (("ffn1_w_in", 704), ("ffn1_w_out", 352), ("w_in", 1096), ("w_ret_o", 256), ("w_uq", 48), ("w_ukv", 64),
        ("w_mla_o", 128), ("w_out", 128), ("ffn2_w_in", 704), ("ffn2_w_out", 352), ("ple_w_gate", 128), ("ple_w_proj", 32))


def _pad16(r):
    return -(-r // 16) * 16


PACK_ROWS = sum(_pad16(r) for _, r in PACK)
PACK_LATE = PACK[:2]
PACK_EARLY = PACK[2:]
SMALL_ROWS = 16


def _pcall(body, **kw):
    return pl.pallas_call(body, **kw)


def _pick(dim, prefs):
    for p in prefs:
        if dim % p == 0:
            return p
    return dim


def _sigmoid(x):
    return 1.0 / (1.0 + jnp.exp(-x))


def _silu(x):
    return x * _sigmoid(x)


def _ln(r, g, b):
    mu = jnp.mean(r, axis=-1, keepdims=True)
    var = jnp.mean(jnp.square(r - mu), axis=-1, keepdims=True)
    return (r - mu) * lax.rsqrt(var + EPS) * g + b


def _rms(x, g):
    return x * lax.rsqrt(jnp.mean(jnp.square(x), axis=-1, keepdims=True) + EPS) * g


def _dot(a, b, ca, cb):
    return lax.dot_general(a, b, (((ca,), (cb,)), ((), ())), preferred_element_type=F32)


def _accum(ref, val, first=None):
    @pl.when(pl.program_id(0) == 0 if first is None else first)
    def _():
        ref[...] = jnp.zeros_like(ref)

    ref[...] += val


def _mm(a, b, *, ta=False, tb=False, add=None, out_dtype=F32, name, tm=None, tn=None, tk=None, epilogue=None):
    parts = a.shape[0] if a.ndim == 3 else 1
    ar, ac = a.shape[-2], a.shape[-1]
    if ta:
        K, M = ar, ac * parts
    else:
        M, K = ar, ac * parts
    if tb:
        N, K2 = b.shape
    else:
        K2, N = b.shape
    assert K == K2, (a.shape, b.shape, ta, tb)
    big = (1024, 1408, 1280, 768, 512, 256, 128)
    tm = tm or _pick(ac if (ta and parts > 1) else M, big)
    tn = tn or (N if N <= 1024 else _pick(N, big))
    kdim = ac if (not ta and parts > 1) else K
    tk = tk or (kdim if kdim <= 2816 and parts == 1 else
                _pick(kdim, (2048, 1408, 1280, 1024, 512) if tm <= 1024 else (1024, 1408, 1280, 512)))
    nk = K // tk
    grid = (M // tm, N // tn, nk)
    if parts > 1 and ta:
        per = ac // tm
        a_spec = pl.BlockSpec((None, tk, tm), lambda i, j, k: (i // per, k, i % per))
    elif parts > 1:
        per = ac // tk
        a_spec = pl.BlockSpec((None, tm, tk), lambda i, j, k: (k // per, i, k % per))
    else:
        a_spec = pl.BlockSpec((tk, tm), lambda i, j, k: (k, i)) if ta else pl.BlockSpec((tm, tk), lambda i, j, k: (i, k))
    b_spec = pl.BlockSpec((tn, tk), lambda i, j, k: (j, k)) if tb else pl.BlockSpec((tk, tn), lambda i, j, k: (k, j))
    o_spec = pl.BlockSpec((tm, tn), lambda i, j, k: (i, j))
    ca, cb = (0 if ta else 1), (1 if tb else 0)
    has_add = add is not None
    n_in = 2 + int(has_add)
    if epilogue is not None:
        assert tn == N and not ta
        ep_fn, ep_rows, ep_whole, ep_outs, ep_accs = epilogue
        n_ep_in = len(ep_rows) + len(ep_whole)
        n_out = len(ep_outs) + len(ep_accs)
    else:
        n_ep_in, n_out = 0, 1

    def body(*refs):
        a_ref, b_ref = refs[0], refs[1]
        add_ref = refs[2] if has_add else None
        o_ref = refs[n_in + n_ep_in]
        first_row_tile = pl.program_id(0) == 0

        def finish(r):
            if has_add:
                r = r + add_ref[...].astype(F32)
            if epilogue is not None:
                ep_fn(r, refs[n_in:n_in + n_ep_in], refs[n_in + n_ep_in:n_in + n_ep_in + n_out], first_row_tile)
            else:
                o_ref[...] = r.astype(out_dtype)

        if nk == 1:
            finish(_dot(a_ref[...], b_ref[...], ca, cb))
            return
        acc_ref = refs[-1]
        k = pl.program_id(2)

        @pl.when(k == 0)
        def _():
            acc_ref[...] = jnp.zeros_like(acc_ref)

        acc_ref[...] += _dot(a_ref[...], b_ref[...], ca, cb)

        @pl.when(k == nk - 1)
        def _():
            finish(acc_ref[...])

    in_specs = [a_spec, b_spec] + ([o_spec] if has_add else [])
    args = (a, b) + ((add,) if has_add else ())
    out_specs, out_shape = o_spec, jax.ShapeDtypeStruct((M, N), out_dtype)
    if epilogue is not None:
        in_specs += [o_spec] * len(ep_rows) + [pl.BlockSpec(w.shape, lambda i, j, k, _n=w.ndim: (0,) * _n) for w in ep_whole]
        args += tuple(ep_rows) + tuple(ep_whole)
        out_specs = [o_spec] * len(ep_outs) + [pl.BlockSpec((r, N), lambda i, j, k: (0, 0)) for r in ep_accs]
        out_shape = [jax.ShapeDtypeStruct((M, N), dt) for dt in ep_outs] + [jax.ShapeDtypeStruct((r, N), F32) for r in ep_accs]
    return _pcall(
        body, grid=grid, in_specs=in_specs, out_specs=out_specs, out_shape=out_shape,
        scratch_shapes=[pltpu.VMEM((tm, tn), F32)] if nk > 1 else [], name=name,
        compiler_params=pltpu.CompilerParams(dimension_semantics=("arbitrary" if epilogue is not None else "parallel", "parallel", "arbitrary")),
    )(*args)


def _ln_epilogue(res, c, g, b):
    def fn(r, ins, outs, first):
        res_ref, g_ref, b_ref = ins
        f_ref, h_ref, hb_ref = outs
        h = _ln(ALPHA * res_ref[...] + c * r, g_ref[...], b_ref[...])
        f_ref[...] = r
        h_ref[...] = h
        hb_ref[...] = h.astype(BF16)

    return (fn, [res], [g, b], (F32, F32, BF16), ())


def _ln_bwd_epilogue(res, f, c, g, b):
    def fn(r, ins, outs, first):
        res_ref, f_ref, g_ref, b_ref = ins
        dr_ref, df_ref, dg_ref, db_ref = outs
        _, vjp = jax.vjp(lambda rr, ff, gg, bb: _ln(ALPHA * rr + c * ff, gg, bb), res_ref[...], f_ref[...], g_ref[...], b_ref[...])
        dr, df, dg, db = vjp(r)
        dr_ref[...] = dr
        df_ref[...] = df.astype(BF16)
        _accum(dg_ref, dg, first)
        _accum(db_ref, db, first)

    return (fn, [res, f], [g, b], (F32, BF16), (1, 1))


def _rows(body, T, tm, ins, outs, name, accs=()):
    in_specs, args = [], []
    for arr, w, cb in ins:
        if w is None:
            in_specs.append(pl.BlockSpec(arr.shape, lambda i, _n=arr.ndim: (0,) * _n))
        else:
            in_specs.append(pl.BlockSpec((tm, w), lambda i, _cb=cb: (i, _cb)))
        args.append(arr)
    out_specs = [pl.BlockSpec((tm, w), lambda i: (i, 0)) for w, _ in outs]
    out_shape = [jax.ShapeDtypeStruct((T, w), dt) for w, dt in outs]
    for r, w in accs:
        out_specs.append(pl.BlockSpec((r, w), lambda i: (0, 0)))
        out_shape.append(jax.ShapeDtypeStruct((r, w), F32))
    return _pcall(
        body, grid=(T // tm,), in_specs=in_specs, out_specs=out_specs, out_shape=out_shape, name=name,
        compiler_params=pltpu.CompilerParams(dimension_semantics=("arbitrary",)),
    )(*args)


FFN_TN = 1408
FFN_TM = 512
LN_TM = 512


def _ffn_in(xb, wt, name):
    T = xb.shape[0]
    tm, tn = min(FFN_TM, T), FFN_TN
    nj = D_FF // tn

    def body(x_ref, wg_ref, wu_ref, hs_ref, gu_ref):
        x = x_ref[...]
        g = _dot(x, wg_ref[...], 1, 1)
        u = _dot(x, wu_ref[...], 1, 1)
        hs_ref[...] = (_silu(g) * u).astype(BF16)
        gu_ref[0] = g.astype(BF16)
        gu_ref[1] = u.astype(BF16)

    return _pcall(
        body, grid=(T // tm, nj),
        in_specs=[pl.BlockSpec((tm, D), lambda i, j: (i, 0)), pl.BlockSpec((tn, D), lambda i, j: (j, 0)),
                  pl.BlockSpec((tn, D), lambda i, j: (j + nj, 0))],
        out_specs=[pl.BlockSpec((tm, tn), lambda i, j: (i, j)), pl.BlockSpec((2, tm, tn), lambda i, j: (0, i, j))],
        out_shape=[jax.ShapeDtypeStruct((T, D_FF), BF16), jax.ShapeDtypeStruct((2, T, D_FF), BF16)], name=name,
        compiler_params=pltpu.CompilerParams(dimension_semantics=("parallel", "parallel")),
    )(xb, wt, wt)


def _ffn_act_bwd(df, wo, gu, name):
    T = df.shape[0]
    tm, tn = min(FFN_TM, T), FFN_TN

    def body(d_ref, w_ref, gu_ref, o_ref):
        dhs = _dot(d_ref[...], w_ref[...], 1, 1)
        _, vjp = jax.vjp(lambda g, u: _silu(g) * u, gu_ref[0].astype(F32), gu_ref[1].astype(F32))
        dg, du = vjp(dhs)
        o_ref[0] = dg.astype(BF16)
        o_ref[1] = du.astype(BF16)

    return _pcall(
        body, grid=(T // tm, D_FF // tn),
        in_specs=[pl.BlockSpec((tm, D), lambda i, j: (i, 0)), pl.BlockSpec((tn, D), lambda i, j: (j, 0)),
                  pl.BlockSpec((2, tm, tn), lambda i, j: (0, i, j))],
        out_specs=pl.BlockSpec((2, tm, tn), lambda i, j: (0, i, j)),
        out_shape=jax.ShapeDtypeStruct((2, T, D_FF), BF16), name=name,
        compiler_params=pltpu.CompilerParams(dimension_semantics=("parallel", "parallel")),
    )(df, wo, gu)


def _rope128(t, cos, sin_s):
    return t * cos + pltpu.roll(t, 64, 1) * sin_s


def _rope128_t(g, cos, sin_s):
    return g * cos - pltpu.roll(g, 64, 1) * sin_s


def _partner32(t):
    lane = lax.broadcasted_iota(jnp.int32, t.shape, 1)
    return jnp.where((lane & 32) == 0, pltpu.roll(t, 96, 1), pltpu.roll(t, 32, 1))


def _rope64(t, cos, sin_s):
    return t * cos + _partner32(t) * sin_s


def _rope64_t(g, cos, sin_s):
    return g * cos - _partner32(g) * sin_s


def _mixer_prep_fwd(proj, tabs, qg, kvg, T):
    cos128, sin128, cos64, sin64 = tabs

    def body(rq_ref, rk_ref, rv_ref, cq_ref, ckv_ref, kpe_ref, c1_ref, s1_ref, c2_ref, s2_ref, qg_ref, kvg_ref,
             oq_ref, ok_ref, ov_ref, oqn_ref, okvn_ref, okpe_ref):
        c1, s1 = c1_ref[...], s1_ref[...]
        for h in range(HEADS):
            sl = slice(h * RET_DK, (h + 1) * RET_DK)
            oq_ref[:, sl] = _rope128(rq_ref[:, sl], c1, s1).astype(BF16)
            ok_ref[:, sl] = (_rope128(rk_ref[:, sl], c1, s1) * RET_SCALE).astype(BF16)
        ov_ref[...] = rv_ref[...].astype(BF16)
        oqn_ref[...] = _rms(cq_ref[...], qg_ref[...]).astype(BF16)
        okvn_ref[...] = _rms(ckv_ref[...], kvg_ref[...]).astype(BF16)
        okpe_ref[...] = _rope64(kpe_ref[...], c2_ref[...], s2_ref[...]).astype(BF16)

    ins = [(proj, 1024, 0), (proj, 1024, 1), (proj, 2048, 1), (proj, 256, P_CQ // 256), (proj, 256, P_CKV // 256),
           (proj, 128, P_KPE // 128), (cos128, 128, 0), (sin128, 128, 0), (cos64, 128, 0), (sin64, 128, 0),
           (qg, None, None), (kvg, None, None)]
    outs = [(1024, BF16), (1024, BF16), (2048, BF16), (LORA, BF16), (LORA, BF16), (128, BF16)]
    return _rows(body, T, 256, ins, outs, "mixer_prep_fwd")


def _rms_bwd(proj, dqn, dkvn, dkpe_all, tabs, qg, kvg, T):
    _, _, cos64, sin64 = tabs

    def body(cq_ref, ckv_ref, dq_ref, dkv_ref, dk_ref, c2_ref, s2_ref, qg_ref, kvg_ref, ocq_ref, ockv_ref, okpe_ref,
             dqg_ref, dkvg_ref):
        _, vjp = jax.vjp(_rms, cq_ref[...], qg_ref[...])
        dx, dg = vjp(dq_ref[...])
        ocq_ref[...] = dx.astype(BF16)
        _accum(dqg_ref, dg)
        _, vjp = jax.vjp(_rms, ckv_ref[...], kvg_ref[...])
        dx, dg = vjp(dkv_ref[...])
        ockv_ref[...] = dx.astype(BF16)
        _accum(dkvg_ref, dg)
        g = dk_ref[:, 0:128]
        for h in range(1, HEADS):
            g = g + dk_ref[:, h * 128:(h + 1) * 128]
        lane = lax.broadcasted_iota(jnp.int32, g.shape, 1)
        g = jnp.where(lane < MLA_ROPE, g, 0.0)
        okpe_ref[...] = _rope64_t(g, c2_ref[...], s2_ref[...]).astype(BF16)

    ins = [(proj, 256, P_CQ // 256), (proj, 256, P_CKV // 256), (dqn, LORA, 0), (dkvn, LORA, 0), (dkpe_all, 1024, 0),
           (cos64, 128, 0), (sin64, 128, 0), (qg, None, None), (kvg, None, None)]
    return _rows(body, T, 256, ins, [(LORA, BF16), (LORA, BF16), (128, BF16)], "rms_bwd", accs=[(1, LORA), (1, LORA)])


def _gn_gate(y, rg, g):
    mu = jnp.mean(y, axis=-1, keepdims=True)
    var = jnp.mean(jnp.square(y - mu), axis=-1, keepdims=True)
    return _silu(rg) * ((y - mu) * lax.rsqrt(var + EPS) * g)


def _q_assemble_fwd(q, tabs, T):
    _, _, cos64, sin64 = tabs

    def body(q_ref, c_ref, s_ref, on_ref, op_ref):
        on_ref[...] = (q_ref[:, :1024] * Q_PRESCALE).astype(BF16)
        c, s = c_ref[...], s_ref[...]
        lane = lax.broadcasted_iota(jnp.int32, c.shape, 1)
        for j in range(HEADS // 2):
            r = _rope64(q_ref[:, 1024 + 128 * j:1024 + 128 * (j + 1)], c, s) * Q_PRESCALE
            op_ref[:, 256 * j:256 * j + 128] = jnp.where(lane < 64, r, 0.0).astype(BF16)
            op_ref[:, 256 * j + 128:256 * j + 256] = jnp.where(lane < 64, pltpu.roll(r, 64, 1), 0.0).astype(BF16)

    return _rows(body, T, 256, [(q, 1536, 0), (cos64, 128, 0), (sin64, 128, 0)], [(1024, BF16), (1024, BF16)], "q_assemble_fwd")


def _q_assemble_bwd(dqn, dqpe, tabs, T):
    _, _, cos64, sin64 = tabs

    def body(dn_ref, dp_ref, c_ref, s_ref, on_ref, op_ref):
        on_ref[...] = dn_ref[...].astype(BF16)
        c, s = c_ref[...], s_ref[...]
        lane = lax.broadcasted_iota(jnp.int32, c.shape, 1)
        for j in range(HEADS // 2):
            g = jnp.where(lane < 64, dp_ref[:, 256 * j:256 * j + 128], pltpu.roll(dp_ref[:, 256 * j + 128:256 * j + 256], 64, 1))
            op_ref[:, 128 * j:128 * (j + 1)] = _rope64_t(g, c, s).astype(BF16)

    return _rows(body, T, 256, [(dqn, 1024, 0), (dqpe, 1024, 0), (cos64, 128, 0), (sin64, 128, 0)],
                 [(1024, BF16), (512, BF16)], "q_assemble_bwd")


def _mix_fn(gr, gm, yr, ym):
    return _sigmoid(gr) * yr + _sigmoid(gm) * ym


def _mix_fwd(proj, y_ret, y_mla, T):
    def body(gr_ref, gm_ref, yr_ref, ym_ref, o_ref):
        o_ref[...] = _mix_fn(gr_ref[...], gm_ref[...], yr_ref[...], ym_ref[...]).astype(BF16)

    return _rows(body, T, 256, [(proj, 1024, P_GR // 1024), (proj, 1024, P_GM // 1024), (y_ret, D, 0), (y_mla, D, 0)],
                 [(D, BF16)], "mix_fwd")[0]


def _mix_bwd(proj, y_ret, y_mla, dmix, T):
    def body(gr_ref, gm_ref, yr_ref, ym_ref, d_ref, dgr_ref, dgm_ref, dyr_ref, dym_ref):
        _, vjp = jax.vjp(_mix_fn, gr_ref[...], gm_ref[...], yr_ref[...], ym_ref[...])
        dgr, dgm, dyr, dym = vjp(d_ref[...])
        dgr_ref[...] = dgr.astype(BF16)
        dgm_ref[...] = dgm.astype(BF16)
        dyr_ref[...] = dyr.astype(BF16)
        dym_ref[...] = dym.astype(BF16)

    return _rows(body, T, 256, [(proj, 1024, P_GR // 1024), (proj, 1024, P_GM // 1024), (y_ret, D, 0), (y_mla, D, 0), (dmix, D, 0)],
                 [(D, BF16)] * 4, "mix_bwd")


def _head(h3, gl, pp, tgt, g, b, T):
    def body(h_ref, gl_ref, pp_ref, t_ref, g_ref, b_ref, dh_ref, dgl_ref, dpp_ref, dg_ref, db_ref, loss_ref):
        def f(h, gg, p, lg, lb):
            return _ln(ALPHA * h + _sigmoid(gg) * p, lg, lb)

        h4, vjp = jax.vjp(f, h_ref[...], gl_ref[...], pp_ref[...], g_ref[...], b_ref[...])
        err = h4 - t_ref[...]
        dh, dgl, dpp, dg, db = vjp(err * (1.0 / D))
        dh_ref[...] = dh
        dgl_ref[...] = dgl.astype(BF16)
        dpp_ref[...] = dpp.astype(BF16)
        _accum(dg_ref, dg)
        _accum(db_ref, db)
        part = 0.5 * jnp.sum(jnp.mean(jnp.square(err), axis=-1, keepdims=True), axis=0, keepdims=True)
        _accum(loss_ref, jnp.broadcast_to(part, loss_ref.shape))

    return _rows(body, T, 256, [(h3, D, 0), (gl, D, 0), (pp, D, 0), (tgt, D, 0), (g, None, None), (b, None, None)],
                 [(D, F32), (D, BF16), (D, BF16)], "head", accs=[(1, D), (1, D), (8, 128)])


def _attn_delta(do, o, T):
    def body(d_ref, o_ref, db_ref, dl_ref):
        for h in range(HEADS):
            sl = slice(h * MLA_DV, (h + 1) * MLA_DV)
            d = d_ref[:, sl]
            dl = jnp.sum(d * o_ref[:, sl].astype(F32), axis=-1, keepdims=True)
            dl_ref[:, sl] = jnp.broadcast_to(dl, d.shape)
        db_ref[...] = d_ref[...].astype(BF16)

    return _rows(body, T, 256, [(do, D, 0), (o, D, 0)], [(D, BF16), (D, F32)], "attn_delta")


def _ret_consts():
    L = RET_L
    lg = jnp.log(1.0 - 2.0 ** (-5.0 - jnp.arange(HEADS, dtype=F32)))[:, None, None]
    idx = jnp.arange(L, dtype=F32)
    ch = jnp.arange(L) // CHUNK
    dist = idx[:, None] - idx[None, :]
    same = (ch[:, None] == ch[None, :])[None]
    earlier = (ch[None, :] < ch[:, None])[None]
    dm = jnp.where(same, jnp.exp(lg * jnp.abs(dist)[None]), jnp.where(earlier, jnp.exp(lg * dist[None]), 0.0))
    xi = jnp.broadcast_to(jnp.exp(lg * (idx + 1.0)[None, :, None]), (HEADS, L, 128))
    zeta = jnp.broadcast_to(jnp.exp(lg * (L - 1.0 - idx)[None, :, None]), (HEADS, L, 128))
    gl = jnp.broadcast_to(jnp.exp(lg * float(L)), (HEADS, 8, 128))
    return dm.astype(F32), xi.astype(F32), zeta.astype(F32), gl.astype(F32)


def _whole(arr):
    return pl.BlockSpec(arr.shape, lambda n, _nd=arr.ndim: (0,) * _nd)


def _ret_fwd(q, k, v, proj, gn_g, consts, T):
    dm, xi, zeta, gl = consts
    L = RET_L
    n_sc = T // L

    def body(q_ref, k_ref, v_ref, rg_ref, g_ref, dm_ref, xi_ref, ze_ref, gl_ref, y_ref, yr_ref, s_ref, st_ref):
        @pl.when(pl.program_id(0) == 0)
        def _():
            st_ref[...] = jnp.zeros_like(st_ref)

        for h in range(HEADS):
            ks, vs = slice(h * RET_DK, (h + 1) * RET_DK), slice(h * RET_DV, (h + 1) * RET_DV)
            qq, kk, vv = q_ref[:, ks], k_ref[:, ks], v_ref[:, vs]
            st = st_ref[h]
            s_ref[h, 0] = st
            p = (_dot(qq, kk, 1, 1) * dm_ref[h]).astype(BF16)
            cross = _dot(qq, st.astype(BF16), 1, 0)
            xi_c = jnp.concatenate([xi_ref[h], xi_ref[h]], axis=1)
            y = _dot(p, vv, 1, 0) + cross * xi_c
            y_ref[:, vs] = y
            yr_ref[:, vs] = _gn_gate(y, rg_ref[:, vs], g_ref[:, vs]).astype(BF16)
            kz = (kk.astype(F32) * ze_ref[h]).astype(BF16)
            gl2 = jnp.concatenate([gl_ref[h, 0:1, :], gl_ref[h, 0:1, :]], axis=1)
            st_ref[h] = st * gl2 + _dot(kz, vv, 0, 0)

    return _pcall(
        body, grid=(n_sc,),
        in_specs=[pl.BlockSpec((L, 1024), lambda n: (n, 0)), pl.BlockSpec((L, 1024), lambda n: (n, 0)),
                  pl.BlockSpec((L, 2048), lambda n: (n, 0)), pl.BlockSpec((L, 2048), lambda n: (n, P_RG // 2048)),
                  _whole(gn_g), _whole(dm), _whole(xi), _whole(zeta), _whole(gl)],
        out_specs=[pl.BlockSpec((L, 2048), lambda n: (n, 0)), pl.BlockSpec((L, 2048), lambda n: (n, 0)),
                   pl.BlockSpec((HEADS, 1, 128, 256), lambda n: (0, n, 0, 0))],
        out_shape=[jax.ShapeDtypeStruct((T, HEADS * RET_DV), F32), jax.ShapeDtypeStruct((T, HEADS * RET_DV), BF16),
                   jax.ShapeDtypeStruct((HEADS, n_sc, 128, 256), F32)],
        scratch_shapes=[pltpu.VMEM((HEADS, 128, 256), F32)], name="ret_fwd",
        compiler_params=pltpu.CompilerParams(dimension_semantics=("arbitrary",)),
    )(q, k, v, proj, gn_g, dm, xi, zeta, gl)


def _ret_bwd(q, k, v, y, proj, gn_g, states, dyr, consts, tabs, T):
    dm, xi, zeta, gl = consts
    cos128, sin128, _, _ = tabs
    L = RET_L
    n_sc = T // L

    def body(q_ref, k_ref, v_ref, y_ref, rg_ref, g_ref, d_ref, s_ref, dm_ref, xi_ref, ze_ref, gl_ref, c_ref, sn_ref,
             dq_ref, dk_ref, dv_ref, drg_ref, dg_ref, gs_ref):
        @pl.when(pl.program_id(0) == 0)
        def _():
            gs_ref[...] = jnp.zeros_like(gs_ref)

        c, sn = c_ref[...], sn_ref[...]
        dgs = []
        for h in range(HEADS):
            ks, vs = slice(h * RET_DK, (h + 1) * RET_DK), slice(h * RET_DV, (h + 1) * RET_DV)
            _, vjp = jax.vjp(_gn_gate, y_ref[:, vs], rg_ref[:, vs], g_ref[:, vs])
            dy, drg, dg = vjp(d_ref[:, vs])
            drg_ref[:, vs] = drg.astype(BF16)
            dgs.append(dg)
            qq, kk, vv, dyy = q_ref[:, ks], k_ref[:, ks], v_ref[:, vs], dy.astype(BF16)
            dmm = dm_ref[h]
            gb = gs_ref[h].astype(BF16)
            sb = s_ref[h, 0].astype(BF16)
            xi_c = jnp.concatenate([xi_ref[h], xi_ref[h]], axis=1)
            pb = (_dot(qq, kk, 1, 1) * dmm).astype(BF16)
            kz = (kk.astype(F32) * ze_ref[h]).astype(BF16)
            dv_ref[:, vs] = (_dot(pb, dyy, 0, 0) + _dot(kz, gb, 1, 0)).astype(BF16)
            da = (_dot(dyy, vv, 1, 1) * dmm).astype(BF16)
            dyx = (dyy.astype(F32) * xi_c).astype(BF16)
            dq = _dot(da, kk, 1, 0) + _dot(dyx, sb, 1, 1)
            dk = _dot(da, qq, 0, 0) + _dot(vv, gb, 1, 1) * ze_ref[h]
            dq_ref[:, ks] = _rope128_t(dq, c, sn).astype(BF16)
            dk_ref[:, ks] = (_rope128_t(dk, c, sn) * RET_SCALE).astype(BF16)
            gl2 = jnp.concatenate([gl_ref[h, 0:1, :], gl_ref[h, 0:1, :]], axis=1)
            gs_ref[h] = gs_ref[h] * gl2 + _dot(qq, dyx, 0, 0)
        _accum(dg_ref, jnp.concatenate(dgs, axis=1))

    rev = lambda n: n_sc - 1 - n
    return _pcall(
        body, grid=(n_sc,),
        in_specs=[pl.BlockSpec((L, 1024), lambda n: (rev(n), 0)), pl.BlockSpec((L, 1024), lambda n: (rev(n), 0)),
                  pl.BlockSpec((L, 2048), lambda n: (rev(n), 0)), pl.BlockSpec((L, 2048), lambda n: (rev(n), 0)),
                  pl.BlockSpec((L, 2048), lambda n: (rev(n), P_RG // 2048)), _whole(gn_g),
                  pl.BlockSpec((L, 2048), lambda n: (rev(n), 0)),
                  pl.BlockSpec((HEADS, 1, 128, 256), lambda n: (0, rev(n), 0, 0)),
                  _whole(dm), _whole(xi), _whole(zeta), _whole(gl),
                  pl.BlockSpec((L, 128), lambda n: (rev(n), 0)), pl.BlockSpec((L, 128), lambda n: (rev(n), 0))],
        out_specs=[pl.BlockSpec((L, 1024), lambda n: (rev(n), 0)), pl.BlockSpec((L, 1024), lambda n: (rev(n), 0)),
                   pl.BlockSpec((L, 2048), lambda n: (rev(n), 0)), pl.BlockSpec((L, 2048), lambda n: (rev(n), 0)),
                   pl.BlockSpec((1, 2048), lambda n: (0, 0))],
        out_shape=[jax.ShapeDtypeStruct((T, 1024), BF16), jax.ShapeDtypeStruct((T, 1024), BF16), jax.ShapeDtypeStruct((T, 2048), BF16),
                   jax.ShapeDtypeStruct((T, 2048), BF16), jax.ShapeDtypeStruct((1, 2048), F32)],
        scratch_shapes=[pltpu.VMEM((HEADS, 128, 256), F32)], name="ret_bwd",
        compiler_params=pltpu.CompilerParams(dimension_semantics=("arbitrary",)),
    )(q, k, v, y, proj, gn_g, dyr, states, dm, xi, zeta, gl, cos128, sin128)


def _diag_mask(nrows, ncols, row0):
    row = row0 + lax.broadcasted_iota(jnp.int32, (nrows, ncols), 0)
    col = lax.broadcasted_iota(jnp.int32, (nrows, ncols), 1)
    return lax.shift_right_logical(col, 6) <= lax.shift_right_logical(row, 6)


def _tri_steps(nb, by_key):
    if by_key:
        pairs = [(i, j) for j in range(nb) for i in range(j, nb)]
    else:
        pairs = [(i, j) for i in range(nb) for j in range(i + 1)]
    return jnp.array([a for a, _ in pairs], jnp.int32), jnp.array([b for _, b in pairs], jnp.int32)


def _attn_fwd(qn, qpe, kv, kpe, T):
    t = min(ATT_TF, T)
    nb = T // t
    ii, jj = _tri_steps(nb, by_key=False)

    hp = ATT_HP
    w = 128 * hp

    def body(ii_ref, jj_ref, qn_ref, qp_ref, kn_ref, kp_ref, v_ref, o_ref, lse_ref, m_sc, l_sc, acc_sc):
        st = pl.program_id(1)
        i, j = ii_ref[st], jj_ref[st]

        @pl.when(j == 0)
        def _():
            m_sc[...] = jnp.full_like(m_sc, NEG)
            l_sc[...] = jnp.zeros_like(l_sc)
            acc_sc[...] = jnp.zeros_like(acc_sc)

        def update(diag):
            kp = kp_ref[...]
            for hh in range(hp):
                sl = slice(128 * hh, 128 * (hh + 1))
                q = jnp.concatenate([qn_ref[:, sl], qp_ref[:, sl]], axis=1)
                k = jnp.concatenate([kn_ref[:, sl], kp], axis=1)
                s = _dot(q, k, 1, 1)
                if diag:
                    s = jnp.where(_diag_mask(t, t, 0), s, NEG)
                m_prev = m_sc[:, sl]
                m_new = jnp.maximum(m_prev, jnp.max(s, axis=1, keepdims=True))
                a = jnp.exp2(m_prev - m_new)
                p = jnp.exp2(s - m_new[:, 0:1])
                l_sc[:, sl] = a * l_sc[:, sl] + jnp.sum(p, axis=1, keepdims=True)
                acc_sc[:, sl] = a * acc_sc[:, sl] + _dot(p.astype(BF16), v_ref[:, sl], 1, 0)
                m_sc[:, sl] = m_new

        @pl.when(j < i)
        def _():
            update(False)

        @pl.when(j == i)
        def _():
            update(True)
            o_ref[...] = (acc_sc[...] / l_sc[...]).astype(BF16)
            lse_ref[...] = m_sc[...] + jnp.log2(l_sc[...])

    qs = pl.BlockSpec((t, w), lambda h, s, ii, jj: (ii[s], h))
    grid_spec = pltpu.PrefetchScalarGridSpec(
        num_scalar_prefetch=2, grid=(HEADS // hp, int(ii.shape[0])),
        in_specs=[qs, qs, pl.BlockSpec((t, w), lambda h, s, ii, jj: (jj[s], h)), pl.BlockSpec((t, 128), lambda h, s, ii, jj: (jj[s], 0)),
                  pl.BlockSpec((t, w), lambda h, s, ii, jj: (jj[s], HEADS // hp + h))],
        out_specs=[qs, qs],
        scratch_shapes=[pltpu.VMEM((t, w), F32), pltpu.VMEM((t, w), F32), pltpu.VMEM((t, w), F32)])
    return _pcall(
        body, grid_spec=grid_spec, out_shape=[jax.ShapeDtypeStruct((T, D), BF16), jax.ShapeDtypeStruct((T, D), F32)], name="attn_fwd",
        compiler_params=pltpu.CompilerParams(dimension_semantics=("arbitrary", "arbitrary")),
    )(ii, jj, qn, qpe, kv, kpe, kv)


def _attn_bwd(qn, qpe, kv, kpe, do, lse, delta, T):
    t = min(ATT_TB, T)
    nb = T // t
    ii, jj = _tri_steps(nb, by_key=True)

    def body(ii_ref, jj_ref, qn_ref, qp_ref, kn_ref, kp_ref, v_ref, do_ref, lse_ref, dl_ref,
             dqn_ref, dqp_ref, dkn_ref, dkp_ref, dv_ref, dk_sc, dv_sc):
        st = pl.program_id(1)
        i, j = ii_ref[st], jj_ref[st]

        @pl.when(st == 0)
        def _():
            dqn_ref[...] = jnp.zeros_like(dqn_ref)
            dqp_ref[...] = jnp.zeros_like(dqp_ref)

        @pl.when(i == j)
        def _():
            dk_sc[...] = jnp.zeros_like(dk_sc)
            dv_sc[...] = jnp.zeros_like(dv_sc)

        def update(diag):
            q = jnp.concatenate([qn_ref[...], qp_ref[...]], axis=1)
            k = jnp.concatenate([kn_ref[...], kp_ref[...]], axis=1)
            dob = do_ref[...]
            s = _dot(q, k, 1, 1)
            if diag:
                s = jnp.where(_diag_mask(t, t, 0), s, NEG)
            p = jnp.exp2(s - lse_ref[:, 0:1])
            dv_sc[...] += _dot(p.astype(BF16), dob, 0, 0)
            dp = _dot(dob, v_ref[...], 1, 1)
            ds = (p * (dp - dl_ref[:, 0:1])).astype(BF16)
            dk_sc[...] += _dot(ds, q, 0, 0)
            dq = _dot(ds, k, 1, 0) * MLA_SCALE
            rows = pl.ds(pl.multiple_of(i * t, t), t)
            dqn_ref[rows, :] += dq[:, :128]
            dqp_ref[rows, :] += dq[:, 128:]

        @pl.when(i > j)
        def _():
            update(False)

        @pl.when(i == j)
        def _():
            update(True)

        @pl.when(i == nb - 1)
        def _():
            dkn_ref[...] = (dk_sc[:, :128] * (1.0 / LOG2E)).astype(BF16)
            dkp_ref[...] = dk_sc[:, 128:] * (1.0 / LOG2E)
            dv_ref[...] = dv_sc[...].astype(BF16)

    qs = pl.BlockSpec((t, 128), lambda h, s, ii, jj: (ii[s], h))
    ks = pl.BlockSpec((t, 128), lambda h, s, ii, jj: (jj[s], h))
    hs = pl.BlockSpec((T, 128), lambda h, s, ii, jj: (0, h))
    grid_spec = pltpu.PrefetchScalarGridSpec(
        num_scalar_prefetch=2, grid=(HEADS, int(ii.shape[0])),
        in_specs=[qs, qs, ks, pl.BlockSpec((t, 128), lambda h, s, ii, jj: (jj[s], 0)),
                  pl.BlockSpec((t, 128), lambda h, s, ii, jj: (jj[s], HEADS + h)), qs, qs, qs],
        out_specs=[hs, hs, ks, ks, ks],
        scratch_shapes=[pltpu.VMEM((t, 256), F32), pltpu.VMEM((t, 128), F32)])
    return _pcall(
        body, grid_spec=grid_spec,
        out_shape=[jax.ShapeDtypeStruct((T, D), F32), jax.ShapeDtypeStruct((T, D), F32), jax.ShapeDtypeStruct((T, D), BF16),
                   jax.ShapeDtypeStruct((T, D), F32), jax.ShapeDtypeStruct((T, D), BF16)], name="attn_bwd",
        compiler_params=pltpu.CompilerParams(dimension_semantics=("arbitrary", "arbitrary")),
    )(ii, jj, qn, qpe, kv, kpe, kv, do, lse, delta)


def _mesh_pos():
    x, y, c = lax.axis_index("x"), lax.axis_index("y"), lax.axis_index("c")
    return x, y, c, 4 * x + 2 * y + c


def _peer(x, y, c, k):
    px, py, pc = (x + ((k >> 2) & 1)) % 2, (y + ((k >> 1) & 1)) % 2, (c + (k & 1)) % 2
    return (px, py, pc), 4 * px + 2 * py + pc


_ANY = pl.BlockSpec(memory_space=pl.ANY)


def _rcopy(src, dst, send_sems, recv_sems, k, to):
    return pltpu.make_async_remote_copy(src_ref=src, dst_ref=dst, send_sem=send_sems.at[k], recv_sem=recv_sems.at[k],
                                        device_id=to, device_id_type=pl.DeviceIdType.MESH)


def _all_gather(wsh, ssh):
    def body(w_ref, s_ref, wall_ref, sall_ref, send_sems, recv_sems, loc_sems):
        x, y, c, me = _mesh_pos()
        sib = (x, y, 1 - c)
        chips = [(1 - x, y), (x, 1 - y), (1 - x, 1 - y)]
        slot = lambda px, py, pc: 4 * px + 2 * py + pc
        loc = [pltpu.make_async_copy(s_ref, sall_ref.at[me], loc_sems.at[0])]
        for cp in loc:
            cp.start()
        sends, fwd_waits = [], []
        for n, (src, dst) in enumerate(((w_ref, wall_ref), (s_ref, sall_ref))):
            o = 7 * n
            sends.append(_rcopy(src, dst.at[me], send_sems, recv_sems, o, sib))
            for j, chip in enumerate(chips):
                sends.append(_rcopy(src, dst.at[me], send_sems, recv_sems, o + 1 + j, (*chip, c)))
        for cp in sends:
            cp.start()
        for n, (src, dst) in enumerate(((w_ref, wall_ref), (s_ref, sall_ref))):
            o = 7 * n
            for j, chip in enumerate(chips):
                got = dst.at[slot(*chip, c)]
                _rcopy(src, got, send_sems, recv_sems, o + 1 + j, sib).wait_recv()
                fw = _rcopy(got, got, send_sems, recv_sems, o + 4 + j, sib)
                fw.start()
                sends.append(fw)
            fwd_waits.append(_rcopy(src, dst.at[slot(x, y, 1 - c)], send_sems, recv_sems, o, sib))
            for j, chip in enumerate(chips):
                fwd_waits.append(_rcopy(src, dst.at[slot(*chip, 1 - c)], send_sems, recv_sems, o + 4 + j, sib))
        for cp in fwd_waits:
            cp.wait_recv()
        for cp in sends:
            cp.wait_send()
        for cp in loc:
            cp.wait()

    wall, sall = _pcall(
        body, in_specs=[_ANY, _ANY], out_specs=[_ANY, _ANY],
        out_shape=[jax.ShapeDtypeStruct((N_DEV,) + wsh.shape, wsh.dtype), jax.ShapeDtypeStruct((N_DEV,) + ssh.shape, ssh.dtype)],
        scratch_shapes=[pltpu.SemaphoreType.DMA((14,)), pltpu.SemaphoreType.DMA((14,)), pltpu.SemaphoreType.DMA((1,))],
        name="all_gather_weights",
    )(wsh, ssh)
    me = 4 * lax.axis_index("x") + 2 * lax.axis_index("y") + lax.axis_index("c")
    return lax.dynamic_update_index_in_dim(wall, wsh, me, axis=0), sall


_HBM = pl.BlockSpec(memory_space=pltpu.HBM)
_SEM = pl.BlockSpec(memory_space=pltpu.SEMAPHORE)
_EFFECT = pltpu.SideEffectType.DATAFLOW_SIDE_EFFECTING


def _other_chips(x, y):
    return [(1 - x, y), (x, 1 - y), (1 - x, 1 - y)]


def _gather_start(wsh, order_after):
    def body(w_ref, land_ref, dep_ref, send_sems, recv_sems, w_thru, land_thru, token):
        x, y, c, me = _mesh_pos()
        _rcopy(w_ref, land_ref.at[me], send_sems, recv_sems, 0, (x, y, 1 - c)).start()
        for j, chip in enumerate(_other_chips(x, y)):
            _rcopy(w_ref, land_ref.at[me], send_sems, recv_sems, 1 + j, (*chip, c)).start()
        token[...] = jnp.zeros_like(token)

    shape = (N_DEV,) + wsh.shape
    land = pltpu.with_memory_space_constraint(lax.empty(shape, wsh.dtype), pltpu.HBM)
    return _pcall(
        body, name="gather_start",
        out_shape=(pltpu.SemaphoreType.DMA((4,)), pltpu.SemaphoreType.DMA((4,)), pltpu.HBM(wsh.shape, wsh.dtype),
                   pltpu.HBM(shape, wsh.dtype), jax.ShapeDtypeStruct((8, 128), F32)),
        in_specs=(_HBM, _HBM, _ANY), out_specs=(_SEM, _SEM, _HBM, _HBM, pl.BlockSpec(memory_space=pltpu.VMEM)),
        input_output_aliases={0: 2, 1: 3}, compiler_params=pltpu.CompilerParams(has_side_effects=_EFFECT),
    )(pltpu.with_memory_space_constraint(wsh, pltpu.HBM), land, order_after)


def _gather_wait(send_sems, recv_sems, w_thru, land_thru, after):
    def body(w_ref, land_ref, send_sems, recv_sems, after_ref, w_out, land_out):
        x, y, c, _ = _mesh_pos()
        senders = [(x, y, 1 - c)] + [(*chip, c) for chip in _other_chips(x, y)]
        for k, (px, py, pc) in enumerate(senders):
            cp = _rcopy(w_ref, land_ref.at[4 * px + 2 * py + pc], send_sems, recv_sems, k, (px, py, pc))
            cp.wait_send()
            cp.wait_recv()

    return _pcall(
        body, name="gather_wait", out_shape=(pltpu.HBM(w_thru.shape, w_thru.dtype), pltpu.HBM(land_thru.shape, land_thru.dtype)),
        in_specs=(_HBM, _HBM, _SEM, _SEM, _ANY), out_specs=(_HBM, _HBM), input_output_aliases={0: 0, 1: 1},
        compiler_params=pltpu.CompilerParams(has_side_effects=_EFFECT),
    )(w_thru, land_thru, send_sems, recv_sems, after)


def _gather_finish(wsh, land):
    def body(land_ref, out_ref, send_sems, recv_sems):
        x, y, c, _ = _mesh_pos()
        sib = (x, y, 1 - c)
        sends, recvs = [], []
        for j, (px, py) in enumerate(_other_chips(x, y)):
            mine, theirs = 4 * px + 2 * py + c, 4 * px + 2 * py + (1 - c)
            sends.append(_rcopy(land_ref.at[mine], out_ref.at[mine], send_sems, recv_sems, j, sib))
            recvs.append(_rcopy(land_ref.at[theirs], out_ref.at[theirs], send_sems, recv_sems, j, sib))
        for cp in sends:
            cp.start()
        for cp in recvs:
            cp.wait_recv()
        for cp in sends:
            cp.wait_send()

    done = _pcall(
        body, in_specs=[_ANY], out_specs=_ANY, out_shape=jax.ShapeDtypeStruct(land.shape, land.dtype),
        input_output_aliases={0: 0},
        scratch_shapes=[pltpu.SemaphoreType.DMA((3,)), pltpu.SemaphoreType.DMA((3,))], name="gather_finish",
    )(land)
    me = 4 * lax.axis_index("x") + 2 * lax.axis_index("y") + lax.axis_index("c")
    return lax.dynamic_update_index_in_dim(done, wsh, me, axis=0)


_ROW_TILES = (400, 368, 352, 256, 128, 16, 8)


def _exchange_sibling(g4, name):
    def body(g_ref, r_ref, send_sems, recv_sems):
        x, y, c, _ = _mesh_pos()
        sib = (x, y, 1 - c)
        cps = [_rcopy(g_ref.at[q, 1 - c], r_ref.at[q], send_sems, recv_sems, q, sib) for q in range(4)]
        for cp in cps:
            cp.start()
        for cp in cps:
            cp.wait()

    n, _, R, C = g4.shape
    return _pcall(
        body, in_specs=[_ANY], out_specs=_ANY, out_shape=jax.ShapeDtypeStruct((n, R, C), g4.dtype),
        scratch_shapes=[pltpu.SemaphoreType.DMA((4,)), pltpu.SemaphoreType.DMA((4,))], name=name,
    )(g4)


def _sum_sibling(g4, rsib, cvec, name):
    n, _, R, C = g4.shape
    tr = _pick(R, _ROW_TILES)

    def body(c_ref, g_ref, r_ref, o_ref):
        o_ref[...] = (g_ref[...].astype(F32) + r_ref[...].astype(F32)).astype(o_ref.dtype)

    grid_spec = pltpu.PrefetchScalarGridSpec(
        num_scalar_prefetch=1, grid=(n, R // tr),
        in_specs=[pl.BlockSpec((None, None, tr, C), lambda q, i, cr: (q, cr[0], i, 0)), pl.BlockSpec((None, tr, C), lambda q, i, cr: (q, i, 0))],
        out_specs=pl.BlockSpec((None, tr, C), lambda q, i, cr: (q, i, 0)))
    return _pcall(body, grid_spec=grid_spec, out_shape=jax.ShapeDtypeStruct((n, R, C), g4.dtype), name=name)(cvec, g4, rsib)


def _chips_start(part):
    def body(p_ref, land_ref, send_sems, recv_sems, p_thru, land_thru, token):
        x, y, c, _ = _mesh_pos()
        myq = 2 * x + y
        for j, (px, py) in enumerate(_other_chips(x, y)):
            _rcopy(p_ref.at[2 * px + py], land_ref.at[myq], send_sems, recv_sems, j, (px, py, c)).start()
        token[...] = jnp.zeros_like(token)

    land = pltpu.with_memory_space_constraint(lax.empty(part.shape, part.dtype), pltpu.HBM)
    return _pcall(
        body, name="chips_start",
        out_shape=(pltpu.SemaphoreType.DMA((3,)), pltpu.SemaphoreType.DMA((3,)), pltpu.HBM(part.shape, part.dtype),
                   pltpu.HBM(part.shape, part.dtype), jax.ShapeDtypeStruct((8, 128), F32)),
        in_specs=(_HBM, _HBM), out_specs=(_SEM, _SEM, _HBM, _HBM, pl.BlockSpec(memory_space=pltpu.VMEM)),
        input_output_aliases={0: 2, 1: 3}, compiler_params=pltpu.CompilerParams(has_side_effects=_EFFECT),
    )(pltpu.with_memory_space_constraint(part, pltpu.HBM), land)


def _chips_wait(send_sems, recv_sems, p_thru, land_thru, after):
    def body(p_ref, land_ref, send_sems, recv_sems, after_ref, p_out, land_out):
        x, y, c, _ = _mesh_pos()
        for j, (px, py) in enumerate(_other_chips(x, y)):
            q = 2 * px + py
            cp = _rcopy(p_ref.at[q], land_ref.at[q], send_sems, recv_sems, j, (px, py, c))
            cp.wait_send()
            cp.wait_recv()

    return _pcall(
        body, name="chips_wait", out_shape=(pltpu.HBM(p_thru.shape, p_thru.dtype), pltpu.HBM(p_thru.shape, p_thru.dtype)),
        in_specs=(_HBM, _HBM, _SEM, _SEM, _ANY), out_specs=(_HBM, _HBM), input_output_aliases={0: 0, 1: 1},
        compiler_params=pltpu.CompilerParams(has_side_effects=_EFFECT),
    )(p_thru, land_thru, send_sems, recv_sems, after)


def _sum_chips(part, land, qvec, name):
    n, R, C = part.shape
    tr = _pick(R, _ROW_TILES)

    def body(q_ref, p_ref, *refs):
        o_ref = refs[n]
        acc = None
        for q in range(n):
            term = jnp.where(q_ref[0] == q, p_ref[...], refs[q][...]).astype(F32)
            acc = term if acc is None else acc + term
        o_ref[...] = acc

    def land_spec(q):
        return pl.BlockSpec((None, tr, C), lambda i, m: (jnp.where(m[0] == q, (q + 1) % n, q), i, 0))

    grid_spec = pltpu.PrefetchScalarGridSpec(
        num_scalar_prefetch=1, grid=(R // tr,),
        in_specs=[pl.BlockSpec((None, tr, C), lambda i, m: (m[0], i, 0))] + [land_spec(q) for q in range(n)],
        out_specs=pl.BlockSpec((tr, C), lambda i, m: (i, 0)))
    return _pcall(body, grid_spec=grid_spec, out_shape=jax.ShapeDtypeStruct((R, C), F32), name=name)(qvec, part, *([land] * n))


def _exchange_chips(part, gsmall):
    def body(p_ref, s_ref, recv_ref, srecv_ref, send_sems, recv_sems, loc_sems):
        x, y, c, me = _mesh_pos()
        myq = 2 * x + y
        loc = [pltpu.make_async_copy(s_ref, srecv_ref.at[me], loc_sems.at[0])]
        for cp in loc:
            cp.start()
        sends, recvs = [], []
        for j, (px, py) in enumerate([(1 - x, y), (x, 1 - y), (1 - x, 1 - y)]):
            q = 2 * px + py
            sends.append(_rcopy(p_ref.at[q], recv_ref.at[myq], send_sems, recv_sems, j, (px, py, c)))
            recvs.append(_rcopy(p_ref.at[q], recv_ref.at[q], send_sems, recv_sems, j, (px, py, c)))
        for k in range(1, N_DEV):
            to, pidx = _peer(x, y, c, k)
            sends.append(_rcopy(s_ref, srecv_ref.at[me], send_sems, recv_sems, 2 + k, to))
            recvs.append(_rcopy(s_ref, srecv_ref.at[pidx], send_sems, recv_sems, 2 + k, to))
        for cp in sends:
            cp.start()
        for cp in recvs:
            cp.wait_recv()
        for cp in sends:
            cp.wait_send()
        for cp in loc:
            cp.wait()

    return _pcall(
        body, in_specs=[_ANY, _ANY], out_specs=[_ANY, _ANY],
        out_shape=[jax.ShapeDtypeStruct(part.shape, part.dtype), jax.ShapeDtypeStruct((N_DEV,) + gsmall.shape, gsmall.dtype)],
        scratch_shapes=[pltpu.SemaphoreType.DMA((10,)), pltpu.SemaphoreType.DMA((10,)), pltpu.SemaphoreType.DMA((1,))],
        name="exchange_chips",
    )(part, gsmall)


def _sum_slots(recv, name):
    n, R, C = recv.shape
    tr = _pick(R, _ROW_TILES)

    def body(r_ref, o_ref):
        acc = r_ref[0].astype(F32)
        for s in range(1, n):
            acc = acc + r_ref[s].astype(F32)
        o_ref[...] = acc

    return _pcall(body, grid=(R // tr,), in_specs=[pl.BlockSpec((n, tr, C), lambda i: (0, i, 0))],
                  out_specs=pl.BlockSpec((tr, C), lambda i: (i, 0)), out_shape=jax.ShapeDtypeStruct((R, C), F32), name=name)(recv)


def _adamw(w, g, m, v, name):
    shape = w.shape
    w2, g2, m2, v2 = (a.reshape(-1, shape[-1]) for a in (w, g, m, v))
    R, C = w2.shape
    tr = _pick(R, (256, 128, 64, 32, 16, 8)) if R > 256 else R

    def body(w_ref, g_ref, m_ref, v_ref, d_ref, nm_ref, nv_ref):
        gg = g_ref[...]
        nm = ADAM_B1 * m_ref[...] + (1.0 - ADAM_B1) * gg
        nv = ADAM_B2 * v_ref[...] + (1.0 - ADAM_B2) * jnp.square(gg)
        m_hat = nm / (1.0 - ADAM_B1 ** ADAM_STEP)
        v_hat = nv / (1.0 - ADAM_B2 ** ADAM_STEP)
        d_ref[...] = -ADAM_LR * (m_hat / (jnp.sqrt(v_hat) + ADAM_EPS) + ADAM_WD * w_ref[...])
        nm_ref[...] = nm
        nv_ref[...] = nv

    spec = pl.BlockSpec((tr, C), lambda i: (i, 0))
    d, nm, nv = _pcall(body, grid=(R // tr,), in_specs=[spec] * 4, out_specs=[spec] * 3,
                       out_shape=[jax.ShapeDtypeStruct((R, C), F32)] * 3, name=name)(w2, g2, m2, v2)
    return d.reshape(shape), nm.reshape(shape), nv.reshape(shape)


def _to_rows(name, w):
    w = w[0]
    if name in ("ffn1_w_in", "ffn2_w_in", "w_in"):
        return w.T
    if name in ("w_uq", "w_ukv", "ple_w_proj"):
        return w.T.reshape(-1, 1024)
    return w


def _from_rows(name, g, shape):
    if name in ("ffn1_w_in", "ffn2_w_in", "w_in"):
        return g.T.reshape(shape)
    if name in ("w_uq", "w_ukv", "ple_w_proj"):
        return g.reshape(-1, shape[1]).T.reshape(shape)
    return g.reshape(shape)


def _unpack(wall, group):
    out, off = {}, 0
    for name, r in group:
        out[name] = wall[:, off:off + r, :].reshape(N_DEV * r, 1024)
        off += _pad16(r)
    return out


def _w_in_internal(wt):
    return jnp.concatenate([wt[0:6144], wt[6720:8768], wt[6144:6720], jnp.zeros((P_W - W_IN_COLS, 1024), wt.dtype)], axis=0)


def _w_in_external(d):
    return jnp.concatenate([d[0:6144], d[8192:8768], d[6144:8192]], axis=0)


def _rope_tables(positions):
    pos = positions[0].astype(F32)

    def cs(half):
        inv = ROPE_BASE ** (-jnp.arange(half, dtype=F32) / half)
        ang = pos[:, None] * inv
        return jnp.cos(ang), jnp.sin(ang)

    c, s = cs(64)
    c2, s2 = cs(32)
    return (jnp.concatenate([c, c], axis=1), jnp.concatenate([-s, s], axis=1),
            jnp.concatenate([c2, c2, c2, c2], axis=1), jnp.concatenate([-s2, s2, -s2, s2], axis=1))


def _local_step(x, p, positions, target, W, ln_g, ln_b, gn_g, qg, kvg, early=None, rest_weights=None, start_token=None):
    T = x.shape[0]
    tabs = _rope_tables(positions)
    rc = _ret_consts()
    lg = [ln_g[i:i + 1] for i in range(4)]
    lb = [ln_b[i:i + 1] for i in range(4)]
    xb = (x if start_token is None else x + start_token[0:1, 0:1]).astype(BF16)
    pb = p.astype(BF16)

    hs1, gu1 = _ffn_in(xb, W["ffn1_w_in"], "ffn1_in")
    f1, h1, h1b = _mm(hs1, W["ffn1_w_out"], name="ffn1_out", tm=LN_TM, epilogue=_ln_epilogue(x, 0.5, lg[0], lb[0]))
    if rest_weights is not None:
        W = {**W, **rest_weights(h1b)}
    w_in_t = _w_in_internal(W["w_in"])
    wuq = W["w_uq"].reshape(1536, LORA).reshape(HEADS, 192, LORA)
    wuq = jnp.concatenate([wuq[:, :128].reshape(1024, LORA), wuq[:, 128:].reshape(512, LORA)], axis=0)
    wukv = W["w_ukv"].reshape(2048, LORA).reshape(HEADS, 2, 128, LORA).transpose(1, 0, 2, 3).reshape(2048, LORA)
    wp_t = W["ple_w_proj"].reshape(1024, D_PLE)
    proj = _mm(h1b, w_in_t, tb=True, name="mixer_in")
    rq, rk, rv, qn, kvn, kpe = _mixer_prep_fwd(proj, tabs, qg, kvg, T)
    y, yr, states = _ret_fwd(rq, rk, rv, proj, gn_g, rc, T)
    y_ret = _mm(yr, W["w_ret_o"], name="ret_o")
    q = _mm(qn, wuq, tb=True, name="mla_uq")
    kv = _mm(kvn, wukv, tb=True, out_dtype=BF16, name="mla_ukv")
    qnope, qpe = _q_assemble_fwd(q, tabs, T)
    o, lse = _attn_fwd(qnope, qpe, kv, kpe, T)
    y_mla = _mm(o, W["w_mla_o"], name="mla_o")
    mix = _mix_fwd(proj, y_ret, y_mla, T)
    mixed, h2, h2b = _mm(mix, W["w_out"], name="mixer_out", tm=LN_TM, epilogue=_ln_epilogue(h1, 1.0, lg[1], lb[1]))
    hs2, gu2 = _ffn_in(h2b, W["ffn2_w_in"], "ffn2_in")
    f2, h3, h3b = _mm(hs2, W["ffn2_w_out"], name="ffn2_out", tm=LN_TM, epilogue=_ln_epilogue(h2, 0.5, lg[2], lb[2]))
    gl = _mm(h3b, W["ple_w_gate"], name="ple_gate")
    pp = _mm(pb, wp_t, tb=True, name="ple_proj")

    G = {}
    dh3_a, dgl, dpp, dg3, db3, loss = _head(h3, gl, pp, target, lg[3], lb[3], T)
    G["ple_w_gate"] = _mm(h3b, dgl, ta=True, name="d_ple_gate")
    G["ple_w_proj"] = _mm(dpp, pb, ta=True, name="d_ple_proj")
    dh2_a, df2, dg2, db2 = _mm(dgl, W["ple_w_gate"], tb=True, add=dh3_a, name="dh3", tm=LN_TM,
                               epilogue=_ln_bwd_epilogue(h2, f2, 0.5, lg[2], lb[2]))
    G["ffn2_w_out"] = _mm(hs2, df2, ta=True, name="d_ffn2_out")
    da2 = _ffn_act_bwd(df2, W["ffn2_w_out"], gu2, "ffn2_act_bwd")
    G["ffn2_w_in"] = _mm(da2, h2b, ta=True, name="d_ffn2_in")
    dh1_a, dmixed, dg1, db1 = _mm(da2, W["ffn2_w_in"], add=dh2_a, name="dh2", tm=LN_TM,
                                  epilogue=_ln_bwd_epilogue(h1, mixed, 1.0, lg[1], lb[1]))
    G["w_out"] = _mm(mix, dmixed, ta=True, name="d_mixer_out")
    dmix = _mm(dmixed, W["w_out"], tb=True, name="dmix")
    dgr, dgm, dy_ret, dy_mla = _mix_bwd(proj, y_ret, y_mla, dmix, T)
    G["w_mla_o"] = _mm(o, dy_mla, ta=True, name="d_mla_o")
    do = _mm(dy_mla, W["w_mla_o"], tb=True, name="do")
    dob, delta = _attn_delta(do, o, T)
    dqn_f, dqpe_f, dkn, dkpe_all, dv = _attn_bwd(qnope, qpe, kv, kpe, dob, lse, delta, T)
    dq_n, dq_r = _q_assemble_bwd(dqn_f, dqpe_f, tabs, T)
    g_uq = jnp.concatenate([_mm(dq_n, qn, ta=True, name="d_uq_nope"), _mm(dq_r, qn, ta=True, name="d_uq_rope")], axis=0)
    g_uq = jnp.concatenate([g_uq[:1024].reshape(HEADS, 128, LORA), g_uq[1024:].reshape(HEADS, 64, LORA)], axis=1)
    G["w_uq"] = g_uq.reshape(1536 * LORA // 1024, 1024)
    dqn = _mm(dq_r, wuq[1024:], add=_mm(dq_n, wuq[:1024], name="dqn_a"), name="dqn_b")
    g_ukv = jnp.stack([_mm(dkn, kvn, ta=True, name="d_ukv_k"), _mm(dv, kvn, ta=True, name="d_ukv_v")], axis=0)
    G["w_ukv"] = g_ukv.reshape(2, HEADS, 128, LORA).transpose(1, 0, 2, 3).reshape(2048 * LORA // 1024, 1024)
    dkvn = _mm(dv, wukv[1024:], add=_mm(dkn, wukv[:1024], name="dkvn_a"), name="dkvn_b")
    dcq, dckv, dkpe, dqg, dkvg = _rms_bwd(proj, dqn, dkvn, dkpe_all, tabs, qg, kvg, T)
    G["w_ret_o"] = _mm(yr, dy_ret, ta=True, name="d_ret_o")
    dyr = _mm(dy_ret, W["w_ret_o"], tb=True, name="dyr")
    drq, drk, drv, drg, dgn = _ret_bwd(rq, rk, rv, y, proj, gn_g, states, dyr, rc, tabs, T)
    dproj = jnp.concatenate([drq, drk, drv, drg, dgr, dgm, dcq, dckv, dkpe, jnp.zeros((T, P_W - P_KPE - 128), BF16)], axis=1)
    G["w_in"] = _w_in_external(_mm(dproj, h1b, ta=True, name="d_mixer_in"))
    lg0 = lg[0] if early is None else lg[0] + early(G)[0:1, 0:1]
    dx_a, df1, dg0, db0 = _mm(dproj, w_in_t, add=dh1_a, name="dh1", tm=LN_TM,
                              epilogue=_ln_bwd_epilogue(x, f1, 0.5, lg0, lb[0]))
    G["ffn1_w_out"] = _mm(hs1, df1, ta=True, name="d_ffn1_out")
    da1 = _ffn_act_bwd(df1, W["ffn1_w_out"], gu1, "ffn1_act_bwd")
    G["ffn1_w_in"] = _mm(da1, xb, ta=True, name="d_ffn1_in")
    grad_x = _mm(da1, W["ffn1_w_in"], add=dx_a, name="grad_x")

    small = dict(ln_g=jnp.concatenate([dg0, dg1, dg2, dg3], axis=0), ln_b=jnp.concatenate([db0, db1, db2, db3], axis=0),
                 ret_gn_g=dgn, q_norm_g=dqg, kv_norm_g=dkvg)
    return loss, grad_x, G, small


def kernel(x, p, positions, ln_g, ln_b, ffn1_w_in, ffn1_w_out, w_in, ret_gn_g, w_ret_o, q_norm_g, kv_norm_g, w_uq, w_ukv, w_mla_o, w_out, ffn2_w_in, ffn2_w_out, ple_w_gate, ple_w_proj, loss_target, m_ln_g, m_ln_b, m_ffn1_w_in, m_ffn1_w_out, m_w_in, m_ret_gn_g, m_w_ret_o, m_q_norm_g, m_kv_norm_g, m_w_uq, m_w_ukv, m_w_mla_o, m_w_out, m_ffn2_w_in, m_ffn2_w_out, m_ple_w_gate, m_ple_w_proj, v_ln_g, v_ln_b, v_ffn1_w_in, v_ffn1_w_out, v_w_in, v_ret_gn_g, v_w_ret_o, v_q_norm_g, v_kv_norm_g, v_w_uq, v_w_ukv, v_w_mla_o, v_w_out, v_ffn2_w_in, v_ffn2_w_out, v_ple_w_gate, v_ple_w_proj):
    names = ("ln_g", "ln_b", "ffn1_w_in", "ffn1_w_out", "w_in", "ret_gn_g", "w_ret_o", "q_norm_g", "kv_norm_g", "w_uq", "w_ukv",
             "w_mla_o", "w_out", "ffn2_w_in", "ffn2_w_out", "ple_w_gate", "ple_w_proj")
    ws = dict(zip(names, (ln_g, ln_b, ffn1_w_in, ffn1_w_out, w_in, ret_gn_g, w_ret_o, q_norm_g, kv_norm_g, w_uq, w_ukv, w_mla_o,
                          w_out, ffn2_w_in, ffn2_w_out, ple_w_gate, ple_w_proj)))
    ms = dict(zip(names, (m_ln_g, m_ln_b, m_ffn1_w_in, m_ffn1_w_out, m_w_in, m_ret_gn_g, m_w_ret_o, m_q_norm_g, m_kv_norm_g, m_w_uq,
                          m_w_ukv, m_w_mla_o, m_w_out, m_ffn2_w_in, m_ffn2_w_out, m_ple_w_gate, m_ple_w_proj)))
    vs = dict(zip(names, (v_ln_g, v_ln_b, v_ffn1_w_in, v_ffn1_w_out, v_w_in, v_ret_gn_g, v_w_ret_o, v_q_norm_g, v_kv_norm_g, v_w_uq,
                          v_w_ukv, v_w_mla_o, v_w_out, v_ffn2_w_in, v_ffn2_w_out, v_ple_w_gate, v_ple_w_proj)))

    parts = []
    for name, r in PACK:
        rows = _to_rows(name, ws[name])
        if _pad16(r) != r:
            rows = jnp.concatenate([rows, jnp.zeros((_pad16(r) - r, 1024), F32)], axis=0)
        parts.append(rows)
    wsh_first = jnp.concatenate(parts[:len(PACK_LATE)], axis=0).astype(BF16)
    wsh_rest = jnp.concatenate(parts[len(PACK_LATE):], axis=0).astype(BF16)
    ssh = jnp.concatenate([ln_g[0], ln_b[0]], axis=0)
    wall_first, sall = _all_gather(wsh_first, ssh)
    *gather_handles, start_token = _gather_start(wsh_rest, wall_first)
    W = _unpack(wall_first, PACK_LATE)
    ln_full = sall.reshape(N_DEV, 2, 4, 128).transpose(1, 2, 0, 3).reshape(2, 4, 1024)

    def rest_weights(after):
        w_thru, land = _gather_wait(*gather_handles, after)
        return _unpack(_gather_finish(w_thru, land), PACK_EARLY)

    cvec = lax.axis_index("c").astype(jnp.int32).reshape(1)
    qvec = (2 * lax.axis_index("x") + lax.axis_index("y")).astype(jnp.int32).reshape(1)

    def chip_partials(G, group, tag):
        gparts = []
        for name, r in group:
            g = G[name].reshape(N_DEV, r, 1024)
            if _pad16(r) != r:
                g = jnp.concatenate([g, jnp.zeros((N_DEV, _pad16(r) - r, 1024), F32)], axis=1)
            gparts.append(g)
        gfull = jnp.concatenate(gparts, axis=1).astype(BF16)
        g4 = gfull.reshape(4, 2, gfull.shape[1], 1024)
        return _sum_sibling(g4, _exchange_sibling(g4, "exchange_sibling_" + tag), cvec, "sum_sibling_" + tag)

    in_flight = []

    def early(G):
        *handles, token = _chips_start(chip_partials(G, PACK_EARLY, "early"))
        in_flight.append(handles)
        return token

    loss_p, grad_x, G, small = _local_step(x[0], p[0, 0], positions, loss_target[0], W, ln_full[0], ln_full[1],
                                           ret_gn_g, q_norm_g, kv_norm_g, early=early, rest_weights=rest_weights,
                                           start_token=start_token)

    part_e, land_e = _chips_wait(*in_flight[0], grad_x)
    gsh_early = _sum_chips(part_e, land_e, qvec, "sum_grads_early")
    pad256 = lambda a: jnp.concatenate([a, jnp.zeros((1, 1024 - a.shape[1]), F32)], axis=1)
    gsmall = jnp.concatenate([small["ln_g"], small["ln_b"], small["ret_gn_g"].reshape(2, 1024), pad256(small["q_norm_g"]),
                              pad256(small["kv_norm_g"]), jnp.zeros((SMALL_ROWS - 12, 1024), F32)], axis=0)
    part_l = chip_partials(G, PACK_LATE, "late")
    recv, srecv = _exchange_chips(part_l, gsmall)
    gsh_late = _sum_chips(part_l, recv, qvec, "sum_grads_late")
    ssum = _sum_slots(srecv, "sum_small_grads")

    grads = {}
    for group, gsh in ((PACK_LATE, gsh_late), (PACK_EARLY, gsh_early)):
        off = 0
        for name, r in group:
            grads[name] = _from_rows(name, gsh[off:off + r], ws[name].shape)
            off += _pad16(r)
    me = 4 * lax.axis_index("x") + 2 * lax.axis_index("y") + lax.axis_index("c")
    grads["ln_g"] = lax.dynamic_slice(ssum[0:4], (0, me * 128), (4, 128)).reshape(1, 4, 128)
    grads["ln_b"] = lax.dynamic_slice(ssum[4:8], (0, me * 128), (4, 128)).reshape(1, 4, 128)
    grads["ret_gn_g"] = ssum[8:10].reshape(1, 2048)
    grads["q_norm_g"] = ssum[10:11, :256]
    grads["kv_norm_g"] = ssum[11:12, :256]

    delta, new_m, new_v = {}, {}, {}
    for name in names:
        delta[name], new_m[name], new_v[name] = _adamw(ws[name], grads[name], ms[name], vs[name], "adamw_" + name)

    loss = lax.psum(loss_p[0, 0], ("x", "y", "c"))
    return (loss, grad_x[None], *[grads[n] for n in names], *[delta[n] for n in names],
            *[new_m[n] for n in names], *[new_v[n] for n in names])
```

```python
import functools
import math

import jax
import jax.numpy as jnp
from jax import lax
from jax.experimental import pallas as pl
from jax.experimental.pallas import tpu as pltpu

F32 = jnp.float32
BF16 = jnp.bfloat16

N_DEV = 8
D = 1024
D_FF = 2816
D_PLE = 256
CHUNK = 64
HEADS = 8
RET_DK = 128
RET_DV = 256
MLA_NOPE = 128
MLA_ROPE = 64
MLA_DV = 128
LORA = 256
ROPE_BASE = 10000.0
EPS = 1e-5
ALPHA = 2.0 ** 0.25
RET_SCALE = RET_DK ** -0.5
MLA_SCALE = (MLA_NOPE + MLA_ROPE) ** -0.5
NEG = -1e30

ADAM_LR = 0.001
ADAM_B1 = 0.9
ADAM_B2 = 0.999
ADAM_EPS = 1e-08
ADAM_WD = 0.01
ADAM_STEP = 10

P_RQ, P_RK, P_RV, P_RG, P_GR, P_GM, P_CQ, P_CKV, P_KPE, P_W = 0, 1024, 2048, 4096, 6144, 7168, 8192, 8448, 8704, 8960
W_IN_COLS = 8768
RET_L = 256
ATT_TF = 1024
ATT_TB = 1024
ATT_HP = 2
LOG2E = math.log2(math.e)
Q_PRESCALE = MLA_SCALE * LOG2E

PACK = (("ffn1_w_in", 704), ("ffn1_w_out", 352), ("w_in", 1096), ("w_ret_o", 256), ("w_uq", 48), ("w_ukv", 64),
        ("w_mla_o", 128), ("w_out", 128), ("ffn2_w_in", 704), ("ffn2_w_out", 352), ("ple_w_gate", 128), ("ple_w_proj", 32))


def _pad16(r):
    return -(-r // 16) * 16


PACK_ROWS = sum(_pad16(r) for _, r in PACK)
PACK_LATE = PACK[:2]
PACK_EARLY = PACK[2:]
SMALL_ROWS = 16


def _pcall(body, **kw):
    return pl.pallas_call(body, **kw)


def _pick(dim, prefs):
    for p in prefs:
        if dim % p == 0:
            return p
    return dim


def _sigmoid(x):
    return 1.0 / (1.0 + jnp.exp(-x))


def _silu(x):
    return x * _sigmoid(x)


def _ln(r, g, b):
    mu = jnp.mean(r, axis=-1, keepdims=True)
    var = jnp.mean(jnp.square(r - mu), axis=-1, keepdims=True)
    return (r - mu) * lax.rsqrt(var + EPS) * g + b


def _rms(x, g):
    return x * lax.rsqrt(jnp.mean(jnp.square(x), axis=-1, keepdims=True) + EPS) * g


def _dot(a, b, ca, cb):
    return lax.dot_general(a, b, (((ca,), (cb,)), ((), ())), preferred_element_type=F32)


def _accum(ref, val, first=None):
    @pl.when(pl.program_id(0) == 0 if first is None else first)
    def _():
        ref[...] = jnp.zeros_like(ref)

    ref[...] += val


def _mm(a, b, *, ta=False, tb=False, add=None, out_dtype=None, name, tm=None, tn=None, tk=None, epilogue=None):
    parts = a.shape[0] if a.ndim == 3 else 1
    ar, ac = a.shape[-2], a.shape[-1]
    out_dtype = out_dtype or (BF16 if ta else F32)
    if ta:
        K, M = ar, ac * parts
    else:
        M, K = ar, ac * parts
    if tb:
        N, K2 = b.shape
    else:
        K2, N = b.shape
    assert K == K2, (a.shape, b.shape, ta, tb)
    big = (1024, 1408, 1280, 768, 512, 256, 128)
    tm = tm or _pick(ac if (ta and parts > 1) else M, big)
    tn = tn or (N if N <= 1024 else _pick(N, big))
    kdim = ac if (not ta and parts > 1) else K
    tk = tk or (kdim if kdim <= 2816 and parts == 1 else
                _pick(kdim, (2048, 1408, 1280, 1024, 512) if tm <= 1024 else (1024, 1408, 1280, 512)))
    nk = K // tk
    grid = (M // tm, N // tn, nk)
    if parts > 1 and ta:
        per = ac // tm
        a_spec = pl.BlockSpec((None, tk, tm), lambda i, j, k: (i // per, k, i % per))
    elif parts > 1:
        per = ac // tk
        a_spec = pl.BlockSpec((None, tm, tk), lambda i, j, k: (k // per, i, k % per))
    else:
        a_spec = pl.BlockSpec((tk, tm), lambda i, j, k: (k, i)) if ta else pl.BlockSpec((tm, tk), lambda i, j, k: (i, k))
    b_spec = pl.BlockSpec((tn, tk), lambda i, j, k: (j, k)) if tb else pl.BlockSpec((tk, tn), lambda i, j, k: (k, j))
    o_spec = pl.BlockSpec((tm, tn), lambda i, j, k: (i, j))
    ca, cb = (0 if ta else 1), (1 if tb else 0)
    has_add = add is not None
    n_in = 2 + int(has_add)
    if epilogue is not None:
        assert tn == N and not ta
        ep_fn, ep_rows, ep_whole, ep_outs, ep_accs = epilogue
        n_ep_in = len(ep_rows) + len(ep_whole)
        n_out = len(ep_outs) + len(ep_accs)
    else:
        n_ep_in, n_out = 0, 1

    def body(*refs):
        a_ref, b_ref = refs[0], refs[1]
        add_ref = refs[2] if has_add else None
        o_ref = refs[n_in + n_ep_in]
        first_row_tile = pl.program_id(0) == 0

        def finish(r):
            if has_add:
                r = r + add_ref[...].astype(F32)
            if epilogue is not None:
                ep_fn(r, refs[n_in:n_in + n_ep_in], refs[n_in + n_ep_in:n_in + n_ep_in + n_out], first_row_tile)
            else:
                o_ref[...] = r.astype(out_dtype)

        if nk == 1:
            finish(_dot(a_ref[...], b_ref[...], ca, cb))
            return
        acc_ref = refs[-1]
        k = pl.program_id(2)

        @pl.when(k == 0)
        def _():
            acc_ref[...] = jnp.zeros_like(acc_ref)

        acc_ref[...] += _dot(a_ref[...], b_ref[...], ca, cb)

        @pl.when(k == nk - 1)
        def _():
            finish(acc_ref[...])

    in_specs = [a_spec, b_spec] + ([o_spec] if has_add else [])
    args = (a, b) + ((add,) if has_add else ())
    out_specs, out_shape = o_spec, jax.ShapeDtypeStruct((M, N), out_dtype)
    if epilogue is not None:
        in_specs += [o_spec] * len(ep_rows) + [pl.BlockSpec(w.shape, lambda i, j, k, _n=w.ndim: (0,) * _n) for w in ep_whole]
        args += tuple(ep_rows) + tuple(ep_whole)
        out_specs = [o_spec] * len(ep_outs) + [pl.BlockSpec((r, N), lambda i, j, k: (0, 0)) for r in ep_accs]
        out_shape = [jax.ShapeDtypeStruct((M, N), dt) for dt in ep_outs] + [jax.ShapeDtypeStruct((r, N), F32) for r in ep_accs]
    return _pcall(
        body, grid=grid, in_specs=in_specs, out_specs=out_specs, out_shape=out_shape,
        scratch_shapes=[pltpu.VMEM((tm, tn), F32)] if nk > 1 else [], name=name,
        compiler_params=pltpu.CompilerParams(dimension_semantics=("arbitrary" if epilogue is not None else "parallel", "parallel", "arbitrary")),
    )(*args)


def _ln_epilogue(res, c, g, b):
    def fn(r, ins, outs, first):
        res_ref, g_ref, b_ref = ins
        f_ref, h_ref, hb_ref = outs
        h = _ln(ALPHA * res_ref[...] + c * r, g_ref[...], b_ref[...])
        f_ref[...] = r
        h_ref[...] = h
        hb_ref[...] = h.astype(BF16)

    return (fn, [res], [g, b], (F32, F32, BF16), ())


def _ln_bwd_epilogue(res, f, c, g, b):
    def fn(r, ins, outs, first):
        res_ref, f_ref, g_ref, b_ref = ins
        dr_ref, df_ref, dg_ref, db_ref = outs
        _, vjp = jax.vjp(lambda rr, ff, gg, bb: _ln(ALPHA * rr + c * ff, gg, bb), res_ref[...], f_ref[...], g_ref[...], b_ref[...])
        dr, df, dg, db = vjp(r)
        dr_ref[...] = dr
        df_ref[...] = df.astype(BF16)
        _accum(dg_ref, dg, first)
        _accum(db_ref, db, first)

    return (fn, [res, f], [g, b], (F32, BF16), (1, 1))


def _rows(body, T, tm, ins, outs, name, accs=()):
    in_specs, args = [], []
    for arr, w, cb in ins:
        if w is None:
            in_specs.append(pl.BlockSpec(arr.shape, lambda i, _n=arr.ndim: (0,) * _n))
        else:
            in_specs.append(pl.BlockSpec((tm, w), lambda i, _cb=cb: (i, _cb)))
        args.append(arr)
    out_specs = [pl.BlockSpec((tm, w), lambda i: (i, 0)) for w, _ in outs]
    out_shape = [jax.ShapeDtypeStruct((T, w), dt) for w, dt in outs]
    for r, w in accs:
        out_specs.append(pl.BlockSpec((r, w), lambda i: (0, 0)))
        out_shape.append(jax.ShapeDtypeStruct((r, w), F32))
    return _pcall(
        body, grid=(T // tm,), in_specs=in_specs, out_specs=out_specs, out_shape=out_shape, name=name,
        compiler_params=pltpu.CompilerParams(dimension_semantics=("arbitrary",)),
    )(*args)


FFN_TN = 1408
FFN_TM = 512
LN_TM = 512


def _ffn_in(x, wt, name, order_after=None):
    T = x.shape[0]
    tm, tn = min(FFN_TM, T), FFN_TN
    nj = D_FF // tn
    emit_xb = x.dtype != BF16
    n_in = 3 + int(order_after is not None)

    def body(*refs):
        x_ref, wg_ref, wu_ref = refs[:3]
        hs_ref, gu_ref = refs[n_in], refs[n_in + 1]
        xv = x_ref[...].astype(BF16)
        g = _dot(xv, wg_ref[...], 1, 1)
        u = _dot(xv, wu_ref[...], 1, 1)
        hs_ref[...] = (_silu(g) * u).astype(BF16)
        gu_ref[0] = g.astype(BF16)
        gu_ref[1] = u.astype(BF16)
        if emit_xb:
            refs[n_in + 2][...] = xv

    in_specs = [pl.BlockSpec((tm, D), lambda i, j: (i, 0)), pl.BlockSpec((tn, D), lambda i, j: (j, 0)),
                pl.BlockSpec((tn, D), lambda i, j: (j + nj, 0))]
    args = [x, wt, wt]
    if order_after is not None:
        in_specs.append(pl.BlockSpec(order_after.shape, lambda i, j: (0, 0)))
        args.append(order_after)
    out_specs = [pl.BlockSpec((tm, tn), lambda i, j: (i, j)), pl.BlockSpec((2, tm, tn), lambda i, j: (0, i, j))]
    out_shape = [jax.ShapeDtypeStruct((T, D_FF), BF16), jax.ShapeDtypeStruct((2, T, D_FF), BF16)]
    if emit_xb:
        out_specs.append(pl.BlockSpec((tm, D), lambda i, j: (i, 0)))
        out_shape.append(jax.ShapeDtypeStruct((T, D), BF16))
    return _pcall(
        body, grid=(T // tm, nj), in_specs=in_specs, out_specs=out_specs, out_shape=out_shape, name=name,
        compiler_params=pltpu.CompilerParams(dimension_semantics=("parallel", "arbitrary")),
    )(*args)


def _ffn_act_bwd(df, wo, gu, name):
    T = df.shape[0]
    tm, tn = min(FFN_TM, T), FFN_TN

    def body(d_ref, w_ref, gu_ref, o_ref):
        dhs = _dot(d_ref[...], w_ref[...], 1, 1)
        g, u = gu_ref[0].astype(F32), gu_ref[1].astype(F32)
        sig = _sigmoid(g)
        act = g * sig
        o_ref[0] = (dhs * u * (sig + act * (1.0 - sig))).astype(BF16)
        o_ref[1] = (dhs * act).astype(BF16)

    return _pcall(
        body, grid=(T // tm, D_FF // tn),
        in_specs=[pl.BlockSpec((tm, D), lambda i, j: (i, 0)), pl.BlockSpec((tn, D), lambda i, j: (j, 0)),
                  pl.BlockSpec((2, tm, tn), lambda i, j: (0, i, j))],
        out_specs=pl.BlockSpec((2, tm, tn), lambda i, j: (0, i, j)),
        out_shape=jax.ShapeDtypeStruct((2, T, D_FF), BF16), name=name,
        compiler_params=pltpu.CompilerParams(dimension_semantics=("parallel", "parallel")),
    )(df, wo, gu)


def _rope128(t, cos, sin_s):
    return t * cos + pltpu.roll(t, 64, 1) * sin_s


def _rope128_t(g, cos, sin_s):
    return g * cos - pltpu.roll(g, 64, 1) * sin_s


def _partner32(t):
    lane = lax.broadcasted_iota(jnp.int32, t.shape, 1)
    return jnp.where((lane & 32) == 0, pltpu.roll(t, 96, 1), pltpu.roll(t, 32, 1))


def _rope64(t, cos, sin_s):
    return t * cos + _partner32(t) * sin_s


def _rope64_t(g, cos, sin_s):
    return g * cos - _partner32(g) * sin_s


def _mixer_prep_fwd(proj, tabs, qg, kvg, T):
    cos128, sin128, cos64, sin64 = tabs

    def body(rq_ref, rk_ref, rv_ref, cq_ref, ckv_ref, kpe_ref, c1_ref, s1_ref, c2_ref, s2_ref, qg_ref, kvg_ref,
             oq_ref, ok_ref, ov_ref, oqn_ref, okvn_ref, okpe_ref):
        c1, s1 = c1_ref[...], s1_ref[...]
        for h in range(HEADS):
            sl = slice(h * RET_DK, (h + 1) * RET_DK)
            oq_ref[:, sl] = _rope128(rq_ref[:, sl], c1, s1).astype(BF16)
            ok_ref[:, sl] = (_rope128(rk_ref[:, sl], c1, s1) * RET_SCALE).astype(BF16)
        ov_ref[...] = rv_ref[...].astype(BF16)
        oqn_ref[...] = _rms(cq_ref[...], qg_ref[...]).astype(BF16)
        okvn_ref[...] = _rms(ckv_ref[...], kvg_ref[...]).astype(BF16)
        okpe_ref[...] = _rope64(kpe_ref[...], c2_ref[...], s2_ref[...]).astype(BF16)

    ins = [(proj, 1024, 0), (proj, 1024, 1), (proj, 2048, 1), (proj, 256, P_CQ // 256), (proj, 256, P_CKV // 256),
           (proj, 128, P_KPE // 128), (cos128, 128, 0), (sin128, 128, 0), (cos64, 128, 0), (sin64, 128, 0),
           (qg, None, None), (kvg, None, None)]
    outs = [(1024, BF16), (1024, BF16), (2048, BF16), (LORA, BF16), (LORA, BF16), (128, BF16)]
    return _rows(body, T, 256, ins, outs, "mixer_prep_fwd")


def _rms_bwd(proj, dqn, dkvn, dkpe_all, tabs, qg, kvg, T):
    _, _, cos64, sin64 = tabs

    def body(cq_ref, ckv_ref, dq_ref, dkv_ref, dk_ref, c2_ref, s2_ref, qg_ref, kvg_ref, ocq_ref, ockv_ref, okpe_ref,
             dqg_ref, dkvg_ref):
        _, vjp = jax.vjp(_rms, cq_ref[...], qg_ref[...])
        dx, dg = vjp(dq_ref[...])
        ocq_ref[...] = dx.astype(BF16)
        _accum(dqg_ref, dg)
        _, vjp = jax.vjp(_rms, ckv_ref[...], kvg_ref[...])
        dx, dg = vjp(dkv_ref[...])
        ockv_ref[...] = dx.astype(BF16)
        _accum(dkvg_ref, dg)
        g = dk_ref[:, 0:128]
        for h in range(1, HEADS):
            g = g + dk_ref[:, h * 128:(h + 1) * 128]
        lane = lax.broadcasted_iota(jnp.int32, g.shape, 1)
        g = jnp.where(lane < MLA_ROPE, g, 0.0)
        okpe_ref[...] = _rope64_t(g, c2_ref[...], s2_ref[...]).astype(BF16)

    ins = [(proj, 256, P_CQ // 256), (proj, 256, P_CKV // 256), (dqn, LORA, 0), (dkvn, LORA, 0), (dkpe_all, 1024, 0),
           (cos64, 128, 0), (sin64, 128, 0), (qg, None, None), (kvg, None, None)]
    return _rows(body, T, 256, ins, [(LORA, BF16), (LORA, BF16), (128, BF16)], "rms_bwd", accs=[(1, LORA), (1, LORA)])


def _gn_gate(y, rg, g):
    mu = jnp.mean(y, axis=-1, keepdims=True)
    var = jnp.mean(jnp.square(y - mu), axis=-1, keepdims=True)
    return _silu(rg) * ((y - mu) * lax.rsqrt(var + EPS) * g)


def _q_assemble_fwd(q, tabs, T):
    _, _, cos64, sin64 = tabs

    def body(q_ref, c_ref, s_ref, on_ref, op_ref):
        on_ref[...] = (q_ref[:, :1024] * Q_PRESCALE).astype(BF16)
        c, s = c_ref[...], s_ref[...]
        lane = lax.broadcasted_iota(jnp.int32, c.shape, 1)
        for j in range(HEADS // 2):
            r = _rope64(q_ref[:, 1024 + 128 * j:1024 + 128 * (j + 1)], c, s) * Q_PRESCALE
            op_ref[:, 256 * j:256 * j + 128] = jnp.where(lane < 64, r, 0.0).astype(BF16)
            op_ref[:, 256 * j + 128:256 * j + 256] = jnp.where(lane < 64, pltpu.roll(r, 64, 1), 0.0).astype(BF16)

    return _rows(body, T, 256, [(q, 1536, 0), (cos64, 128, 0), (sin64, 128, 0)], [(1024, BF16), (1024, BF16)], "q_assemble_fwd")


def _q_assemble_bwd(dqn, dqpe, tabs, T):
    _, _, cos64, sin64 = tabs

    def body(dn_ref, dp_ref, c_ref, s_ref, on_ref, op_ref):
        on_ref[...] = dn_ref[...].astype(BF16)
        c, s = c_ref[...], s_ref[...]
        lane = lax.broadcasted_iota(jnp.int32, c.shape, 1)
        for j in range(HEADS // 2):
            g = jnp.where(lane < 64, dp_ref[:, 256 * j:256 * j + 128], pltpu.roll(dp_ref[:, 256 * j + 128:256 * j + 256], 64, 1))
            op_ref[:, 128 * j:128 * (j + 1)] = _rope64_t(g, c, s).astype(BF16)

    return _rows(body, T, 256, [(dqn, 1024, 0), (dqpe, 1024, 0), (cos64, 128, 0), (sin64, 128, 0)],
                 [(1024, BF16), (512, BF16)], "q_assemble_bwd")


def _mix_fn(gr, gm, yr, ym):
    return _sigmoid(gr) * yr + _sigmoid(gm) * ym


def _mix_fwd(proj, y_ret, y_mla, T):
    def body(gr_ref, gm_ref, yr_ref, ym_ref, o_ref):
        o_ref[...] = _mix_fn(gr_ref[...], gm_ref[...], yr_ref[...], ym_ref[...]).astype(BF16)

    return _rows(body, T, 256, [(proj, 1024, P_GR // 1024), (proj, 1024, P_GM // 1024), (y_ret, D, 0), (y_mla, D, 0)],
                 [(D, BF16)], "mix_fwd")[0]


def _mix_bwd(proj, y_ret, y_mla, dmix, T):
    def body(gr_ref, gm_ref, yr_ref, ym_ref, d_ref, dgr_ref, dgm_ref, dyr_ref, dym_ref):
        _, vjp = jax.vjp(_mix_fn, gr_ref[...], gm_ref[...], yr_ref[...], ym_ref[...])
        dgr, dgm, dyr, dym = vjp(d_ref[...])
        dgr_ref[...] = dgr.astype(BF16)
        dgm_ref[...] = dgm.astype(BF16)
        dyr_ref[...] = dyr.astype(BF16)
        dym_ref[...] = dym.astype(BF16)

    return _rows(body, T, 256, [(proj, 1024, P_GR // 1024), (proj, 1024, P_GM // 1024), (y_ret, D, 0), (y_mla, D, 0), (dmix, D, 0)],
                 [(D, BF16)] * 4, "mix_bwd")


def _head(h3, gl, pp, tgt, g, b, T):
    def body(h_ref, gl_ref, pp_ref, t_ref, g_ref, b_ref, dh_ref, dgl_ref, dpp_ref, dg_ref, db_ref, loss_ref):
        def f(h, gg, p, lg, lb):
            return _ln(ALPHA * h + _sigmoid(gg) * p, lg, lb)

        h4, vjp = jax.vjp(f, h_ref[...], gl_ref[...], pp_ref[...], g_ref[...], b_ref[...])
        err = h4 - t_ref[...]
        dh, dgl, dpp, dg, db = vjp(err * (1.0 / D))
        dh_ref[...] = dh
        dgl_ref[...] = dgl.astype(BF16)
        dpp_ref[...] = dpp.astype(BF16)
        _accum(dg_ref, dg)
        _accum(db_ref, db)
        part = 0.5 * jnp.sum(jnp.mean(jnp.square(err), axis=-1, keepdims=True), axis=0, keepdims=True)
        _accum(loss_ref, jnp.broadcast_to(part, loss_ref.shape))

    return _rows(body, T, 256, [(h3, D, 0), (gl, D, 0), (pp, D, 0), (tgt, D, 0), (g, None, None), (b, None, None)],
                 [(D, F32), (D, BF16), (D, BF16)], "head", accs=[(1, D), (1, D), (8, 128)])


def _attn_delta(do, o, T):
    def body(d_ref, o_ref, db_ref, dl_ref):
        for h in range(HEADS):
            sl = slice(h * MLA_DV, (h + 1) * MLA_DV)
            d = d_ref[:, sl]
            dl = jnp.sum(d * o_ref[:, sl].astype(F32), axis=-1, keepdims=True)
            dl_ref[:, sl] = jnp.broadcast_to(dl, d.shape)
        db_ref[...] = d_ref[...].astype(BF16)

    return _rows(body, T, 256, [(do, D, 0), (o, D, 0)], [(D, BF16), (D, F32)], "attn_delta")


def _ret_consts():
    L = RET_L
    lg = jnp.log(1.0 - 2.0 ** (-5.0 - jnp.arange(HEADS, dtype=F32)))[:, None, None]
    idx = jnp.arange(L, dtype=F32)
    ch = jnp.arange(L) // CHUNK
    dist = idx[:, None] - idx[None, :]
    same = (ch[:, None] == ch[None, :])[None]
    earlier = (ch[None, :] < ch[:, None])[None]
    dm = jnp.where(same, jnp.exp(lg * jnp.abs(dist)[None]), jnp.where(earlier, jnp.exp(lg * dist[None]), 0.0))
    xi = jnp.broadcast_to(jnp.exp(lg * (idx + 1.0)[None, :, None]), (HEADS, L, 128))
    zeta = jnp.broadcast_to(jnp.exp(lg * (L - 1.0 - idx)[None, :, None]), (HEADS, L, 128))
    gl = jnp.broadcast_to(jnp.exp(lg * float(L)), (HEADS, 8, 128))
    return dm.astype(F32), xi.astype(F32), zeta.astype(F32), gl.astype(F32)


def _whole(arr):
    return pl.BlockSpec(arr.shape, lambda n, _nd=arr.ndim: (0,) * _nd)


def _ret_fwd(q, k, v, proj, gn_g, consts, T):
    dm, xi, zeta, gl = consts
    L = RET_L
    n_sc = T // L

    def body(q_ref, k_ref, v_ref, rg_ref, g_ref, dm_ref, xi_ref, ze_ref, gl_ref, y_ref, yr_ref, s_ref, st_ref):
        @pl.when(pl.program_id(0) == 0)
        def _():
            st_ref[...] = jnp.zeros_like(st_ref)

        for h in range(HEADS):
            ks, vs = slice(h * RET_DK, (h + 1) * RET_DK), slice(h * RET_DV, (h + 1) * RET_DV)
            qq, kk, vv = q_ref[:, ks], k_ref[:, ks], v_ref[:, vs]
            st = st_ref[h]
            s_ref[h, 0] = st
            p = (_dot(qq, kk, 1, 1) * dm_ref[h]).astype(BF16)
            cross = _dot(qq, st.astype(BF16), 1, 0)
            xi_c = jnp.concatenate([xi_ref[h], xi_ref[h]], axis=1)
            y = _dot(p, vv, 1, 0) + cross * xi_c
            y_ref[:, vs] = y
            yr_ref[:, vs] = _gn_gate(y, rg_ref[:, vs], g_ref[:, vs]).astype(BF16)
            kz = (kk.astype(F32) * ze_ref[h]).astype(BF16)
            gl2 = jnp.concatenate([gl_ref[h, 0:1, :], gl_ref[h, 0:1, :]], axis=1)
            st_ref[h] = st * gl2 + _dot(kz, vv, 0, 0)

    return _pcall(
        body, grid=(n_sc,),
        in_specs=[pl.BlockSpec((L, 1024), lambda n: (n, 0)), pl.BlockSpec((L, 1024), lambda n: (n, 0)),
                  pl.BlockSpec((L, 2048), lambda n: (n, 0)), pl.BlockSpec((L, 2048), lambda n: (n, P_RG // 2048)),
                  _whole(gn_g), _whole(dm), _whole(xi), _whole(zeta), _whole(gl)],
        out_specs=[pl.BlockSpec((L, 2048), lambda n: (n, 0)), pl.BlockSpec((L, 2048), lambda n: (n, 0)),
                   pl.BlockSpec((HEADS, 1, 128, 256), lambda n: (0, n, 0, 0))],
        out_shape=[jax.ShapeDtypeStruct((T, HEADS * RET_DV), F32), jax.ShapeDtypeStruct((T, HEADS * RET_DV), BF16),
                   jax.ShapeDtypeStruct((HEADS, n_sc, 128, 256), F32)],
        scratch_shapes=[pltpu.VMEM((HEADS, 128, 256), F32)], name="ret_fwd",
        compiler_params=pltpu.CompilerParams(dimension_semantics=("arbitrary",)),
    )(q, k, v, proj, gn_g, dm, xi, zeta, gl)


def _ret_bwd(q, k, v, y, proj, gn_g, states, dyr, consts, tabs, T):
    dm, xi, zeta, gl = consts
    cos128, sin128, _, _ = tabs
    L = RET_L
    n_sc = T // L

    def body(q_ref, k_ref, v_ref, y_ref, rg_ref, g_ref, d_ref, s_ref, dm_ref, xi_ref, ze_ref, gl_ref, c_ref, sn_ref,
             dq_ref, dk_ref, dv_ref, drg_ref, dg_ref, gs_ref):
        @pl.when(pl.program_id(0) == 0)
        def _():
            gs_ref[...] = jnp.zeros_like(gs_ref)

        c, sn = c_ref[...], sn_ref[...]
        dgs = []
        for h in range(HEADS):
            ks, vs = slice(h * RET_DK, (h + 1) * RET_DK), slice(h * RET_DV, (h + 1) * RET_DV)
            _, vjp = jax.vjp(_gn_gate, y_ref[:, vs], rg_ref[:, vs], g_ref[:, vs])
            dy, drg, dg = vjp(d_ref[:, vs])
            drg_ref[:, vs] = drg.astype(BF16)
            dgs.append(dg)
            qq, kk, vv, dyy = q_ref[:, ks], k_ref[:, ks], v_ref[:, vs], dy.astype(BF16)
            dmm = dm_ref[h]
            gb = gs_ref[h].astype(BF16)
            sb = s_ref[h, 0].astype(BF16)
            xi_c = jnp.concatenate([xi_ref[h], xi_ref[h]], axis=1)
            pb = (_dot(qq, kk, 1, 1) * dmm).astype(BF16)
            kz = (kk.astype(F32) * ze_ref[h]).astype(BF16)
            dv_ref[:, vs] = (_dot(pb, dyy, 0, 0) + _dot(kz, gb, 1, 0)).astype(BF16)
            da = (_dot(dyy, vv, 1, 1) * dmm).astype(BF16)
            dyx = (dyy.astype(F32) * xi_c).astype(BF16)
            dq = _dot(da, kk, 1, 0) + _dot(dyx, sb, 1, 1)
            dk = _dot(da, qq, 0, 0) + _dot(vv, gb, 1, 1) * ze_ref[h]
            dq_ref[:, ks] = _rope128_t(dq, c, sn).astype(BF16)
            dk_ref[:, ks] = (_rope128_t(dk, c, sn) * RET_SCALE).astype(BF16)
            gl2 = jnp.concatenate([gl_ref[h, 0:1, :], gl_ref[h, 0:1, :]], axis=1)
            gs_ref[h] = gs_ref[h] * gl2 + _dot(qq, dyx, 0, 0)
        _accum(dg_ref, jnp.concatenate(dgs, axis=1))

    rev = lambda n: n_sc - 1 - n
    return _pcall(
        body, grid=(n_sc,),
        in_specs=[pl.BlockSpec((L, 1024), lambda n: (rev(n), 0)), pl.BlockSpec((L, 1024), lambda n: (rev(n), 0)),
                  pl.BlockSpec((L, 2048), lambda n: (rev(n), 0)), pl.BlockSpec((L, 2048), lambda n: (rev(n), 0)),
                  pl.BlockSpec((L, 2048), lambda n: (rev(n), P_RG // 2048)), _whole(gn_g),
                  pl.BlockSpec((L, 2048), lambda n: (rev(n), 0)),
                  pl.BlockSpec((HEADS, 1, 128, 256), lambda n: (0, rev(n), 0, 0)),
                  _whole(dm), _whole(xi), _whole(zeta), _whole(gl),
                  pl.BlockSpec((L, 128), lambda n: (rev(n), 0)), pl.BlockSpec((L, 128), lambda n: (rev(n), 0))],
        out_specs=[pl.BlockSpec((L, 1024), lambda n: (rev(n), 0)), pl.BlockSpec((L, 1024), lambda n: (rev(n), 0)),
                   pl.BlockSpec((L, 2048), lambda n: (rev(n), 0)), pl.BlockSpec((L, 2048), lambda n: (rev(n), 0)),
                   pl.BlockSpec((1, 2048), lambda n: (0, 0))],
        out_shape=[jax.ShapeDtypeStruct((T, 1024), BF16), jax.ShapeDtypeStruct((T, 1024), BF16), jax.ShapeDtypeStruct((T, 2048), BF16),
                   jax.ShapeDtypeStruct((T, 2048), BF16), jax.ShapeDtypeStruct((1, 2048), F32)],
        scratch_shapes=[pltpu.VMEM((HEADS, 128, 256), F32)], name="ret_bwd",
        compiler_params=pltpu.CompilerParams(dimension_semantics=("arbitrary",)),
    )(q, k, v, y, proj, gn_g, dyr, states, dm, xi, zeta, gl, cos128, sin128)


def _diag_mask(nrows, ncols, row0):
    row = row0 + lax.broadcasted_iota(jnp.int32, (nrows, ncols), 0)
    col = lax.broadcasted_iota(jnp.int32, (nrows, ncols), 1)
    return lax.shift_right_logical(col, 6) <= lax.shift_right_logical(row, 6)


def _tri_steps(nb, by_key):
    if by_key:
        pairs = [(i, j) for j in range(nb) for i in range(j, nb)]
    else:
        pairs = [(i, j) for i in range(nb) for j in range(i + 1)]
    return jnp.array([a for a, _ in pairs], jnp.int32), jnp.array([b for _, b in pairs], jnp.int32)


def _attn_fwd(qn, qpe, kv, kpe, T):
    t = min(ATT_TF, T)
    nb = T // t
    ii, jj = _tri_steps(nb, by_key=False)

    hp = ATT_HP
    w = 128 * hp

    def body(ii_ref, jj_ref, qn_ref, qp_ref, kn_ref, kp_ref, v_ref, o_ref, lse_ref, m_sc, l_sc, acc_sc):
        st = pl.program_id(1)
        i, j = ii_ref[st], jj_ref[st]

        @pl.when(j == 0)
        def _():
            m_sc[...] = jnp.full_like(m_sc, NEG)
            l_sc[...] = jnp.zeros_like(l_sc)
            acc_sc[...] = jnp.zeros_like(acc_sc)

        def update(diag):
            kp = kp_ref[...]
            for hh in range(hp):
                sl = slice(128 * hh, 128 * (hh + 1))
                q = jnp.concatenate([qn_ref[:, sl], qp_ref[:, sl]], axis=1)
                k = jnp.concatenate([kn_ref[:, sl], kp], axis=1)
                s = _dot(q, k, 1, 1)
                if diag:
                    s = jnp.where(_diag_mask(t, t, 0), s, NEG)
                m_prev = m_sc[:, sl]
                m_new = jnp.maximum(m_prev, jnp.max(s, axis=1, keepdims=True))
                a = jnp.exp2(m_prev - m_new)
                p = jnp.exp2(s - m_new[:, 0:1])
                l_sc[:, sl] = a * l_sc[:, sl] + jnp.sum(p, axis=1, keepdims=True)
                acc_sc[:, sl] = a * acc_sc[:, sl] + _dot(p.astype(BF16), v_ref[:, sl], 1, 0)
                m_sc[:, sl] = m_new

        @pl.when(j < i)
        def _():
            update(False)

        @pl.when(j == i)
        def _():
            update(True)
            o_ref[...] = (acc_sc[...] / l_sc[...]).astype(BF16)
            lse_ref[...] = m_sc[...] + jnp.log2(l_sc[...])

    qs = pl.BlockSpec((t, w), lambda h, s, ii, jj: (ii[s], h))
    grid_spec = pltpu.PrefetchScalarGridSpec(
        num_scalar_prefetch=2, grid=(HEADS // hp, int(ii.shape[0])),
        in_specs=[qs, qs, pl.BlockSpec((t, w), lambda h, s, ii, jj: (jj[s], h)), pl.BlockSpec((t, 128), lambda h, s, ii, jj: (jj[s], 0)),
                  pl.BlockSpec((t, w), lambda h, s, ii, jj: (jj[s], HEADS // hp + h))],
        out_specs=[qs, qs],
        scratch_shapes=[pltpu.VMEM((t, w), F32), pltpu.VMEM((t, w), F32), pltpu.VMEM((t, w), F32)])
    return _pcall(
        body, grid_spec=grid_spec, out_shape=[jax.ShapeDtypeStruct((T, D), BF16), jax.ShapeDtypeStruct((T, D), F32)], name="attn_fwd",
        compiler_params=pltpu.CompilerParams(dimension_semantics=("arbitrary", "arbitrary")),
    )(ii, jj, qn, qpe, kv, kpe, kv)


def _attn_bwd(qn, qpe, kv, kpe, do, lse, delta, T):
    t = min(ATT_TB, T)
    nb = T // t
    ii, jj = _tri_steps(nb, by_key=True)

    def body(ii_ref, jj_ref, qn_ref, qp_ref, kn_ref, kp_ref, v_ref, do_ref, lse_ref, dl_ref,
             dqn_ref, dqp_ref, dkn_ref, dkp_ref, dv_ref, dk_sc, dv_sc):
        st = pl.program_id(1)
        i, j = ii_ref[st], jj_ref[st]

        @pl.when(st == 0)
        def _():
            dqn_ref[...] = jnp.zeros_like(dqn_ref)
            dqp_ref[...] = jnp.zeros_like(dqp_ref)

        @pl.when(i == j)
        def _():
            dk_sc[...] = jnp.zeros_like(dk_sc)
            dv_sc[...] = jnp.zeros_like(dv_sc)

        def update(diag):
            q = jnp.concatenate([qn_ref[...], qp_ref[...]], axis=1)
            k = jnp.concatenate([kn_ref[...], kp_ref[...]], axis=1)
            dob = do_ref[...]
            s = _dot(q, k, 1, 1)
            if diag:
                s = jnp.where(_diag_mask(t, t, 0), s, NEG)
            p = jnp.exp2(s - lse_ref[:, 0:1])
            dv_sc[...] += _dot(p.astype(BF16), dob, 0, 0)
            dp = _dot(dob, v_ref[...], 1, 1)
            ds = (p * (dp - dl_ref[:, 0:1])).astype(BF16)
            dk_sc[...] += _dot(ds, q, 0, 0)
            dq = _dot(ds, k, 1, 0) * MLA_SCALE
            rows = pl.ds(pl.multiple_of(i * t, t), t)
            dqn_ref[rows, :] += dq[:, :128]
            dqp_ref[rows, :] += dq[:, 128:]

        @pl.when(i > j)
        def _():
            update(False)

        @pl.when(i == j)
        def _():
            update(True)

        @pl.when(i == nb - 1)
        def _():
            dkn_ref[...] = (dk_sc[:, :128] * (1.0 / LOG2E)).astype(BF16)
            dkp_ref[...] = dk_sc[:, 128:] * (1.0 / LOG2E)
            dv_ref[...] = dv_sc[...].astype(BF16)

    qs = pl.BlockSpec((t, 128), lambda h, s, ii, jj: (ii[s], h))
    ks = pl.BlockSpec((t, 128), lambda h, s, ii, jj: (jj[s], h))
    hs = pl.BlockSpec((T, 128), lambda h, s, ii, jj: (0, h))
    grid_spec = pltpu.PrefetchScalarGridSpec(
        num_scalar_prefetch=2, grid=(HEADS, int(ii.shape[0])),
        in_specs=[qs, qs, ks, pl.BlockSpec((t, 128), lambda h, s, ii, jj: (jj[s], 0)),
                  pl.BlockSpec((t, 128), lambda h, s, ii, jj: (jj[s], HEADS + h)), qs, qs, qs],
        out_specs=[hs, hs, ks, ks, ks],
        scratch_shapes=[pltpu.VMEM((t, 256), F32), pltpu.VMEM((t, 128), F32)])
    return _pcall(
        body, grid_spec=grid_spec,
        out_shape=[jax.ShapeDtypeStruct((T, D), F32), jax.ShapeDtypeStruct((T, D), F32), jax.ShapeDtypeStruct((T, D), BF16),
                   jax.ShapeDtypeStruct((T, D), F32), jax.ShapeDtypeStruct((T, D), BF16)], name="attn_bwd",
        compiler_params=pltpu.CompilerParams(dimension_semantics=("arbitrary", "arbitrary")),
    )(ii, jj, qn, qpe, kv, kpe, kv, do, lse, delta)


def _mesh_pos():
    x, y, c = lax.axis_index("x"), lax.axis_index("y"), lax.axis_index("c")
    return x, y, c, 4 * x + 2 * y + c


def _peer(x, y, c, k):
    px, py, pc = (x + ((k >> 2) & 1)) % 2, (y + ((k >> 1) & 1)) % 2, (c + (k & 1)) % 2
    return (px, py, pc), 4 * px + 2 * py + pc


_ANY = pl.BlockSpec(memory_space=pl.ANY)


def _rcopy(src, dst, send_sems, recv_sems, k, to):
    return pltpu.make_async_remote_copy(src_ref=src, dst_ref=dst, send_sem=send_sems.at[k], recv_sem=recv_sems.at[k],
                                        device_id=to, device_id_type=pl.DeviceIdType.MESH)


def _all_gather(wsh, ssh):
    def body(w_ref, s_ref, wall_ref, sall_ref, send_sems, recv_sems, loc_sems):
        x, y, c, me = _mesh_pos()
        sib = (x, y, 1 - c)
        chips = [(1 - x, y), (x, 1 - y), (1 - x, 1 - y)]
        slot = lambda px, py, pc: 4 * px + 2 * py + pc
        loc = [pltpu.make_async_copy(s_ref, sall_ref.at[me], loc_sems.at[0])]
        for cp in loc:
            cp.start()
        sends, fwd_waits = [], []
        for n, (src, dst) in enumerate(((w_ref, wall_ref), (s_ref, sall_ref))):
            o = 7 * n
            sends.append(_rcopy(src, dst.at[me], send_sems, recv_sems, o, sib))
            for j, chip in enumerate(chips):
                sends.append(_rcopy(src, dst.at[me], send_sems, recv_sems, o + 1 + j, (*chip, c)))
        for cp in sends:
            cp.start()
        for n, (src, dst) in enumerate(((w_ref, wall_ref), (s_ref, sall_ref))):
            o = 7 * n
            for j, chip in enumerate(chips):
                got = dst.at[slot(*chip, c)]
                _rcopy(src, got, send_sems, recv_sems, o + 1 + j, sib).wait_recv()
                fw = _rcopy(got, got, send_sems, recv_sems, o + 4 + j, sib)
                fw.start()
                sends.append(fw)
            fwd_waits.append(_rcopy(src, dst.at[slot(x, y, 1 - c)], send_sems, recv_sems, o, sib))
            for j, chip in enumerate(chips):
                fwd_waits.append(_rcopy(src, dst.at[slot(*chip, 1 - c)], send_sems, recv_sems, o + 4 + j, sib))
        for cp in fwd_waits:
            cp.wait_recv()
        for cp in sends:
            cp.wait_send()
        for cp in loc:
            cp.wait()

    wall, sall = _pcall(
        body, in_specs=[_ANY, _ANY], out_specs=[_ANY, _ANY],
        out_shape=[jax.ShapeDtypeStruct((N_DEV,) + wsh.shape, wsh.dtype), jax.ShapeDtypeStruct((N_DEV,) + ssh.shape, ssh.dtype)],
        scratch_shapes=[pltpu.SemaphoreType.DMA((14,)), pltpu.SemaphoreType.DMA((14,)), pltpu.SemaphoreType.DMA((1,))],
        name="all_gather_weights",
    )(wsh, ssh)
    me = 4 * lax.axis_index("x") + 2 * lax.axis_index("y") + lax.axis_index("c")
    return lax.dynamic_update_index_in_dim(wall, wsh, me, axis=0), sall


_HBM = pl.BlockSpec(memory_space=pltpu.HBM)
_SEM = pl.BlockSpec(memory_space=pltpu.SEMAPHORE)
_EFFECT = pltpu.SideEffectType.DATAFLOW_SIDE_EFFECTING


def _other_chips(x, y):
    return [(1 - x, y), (x, 1 - y), (1 - x, 1 - y)]


def _gather_start(wsh, order_after):
    def body(w_ref, land_ref, dep_ref, send_sems, recv_sems, w_thru, land_thru, token):
        x, y, c, me = _mesh_pos()
        _rcopy(w_ref, land_ref.at[me], send_sems, recv_sems, 0, (x, y, 1 - c)).start()
        for j, chip in enumerate(_other_chips(x, y)):
            _rcopy(w_ref, land_ref.at[me], send_sems, recv_sems, 1 + j, (*chip, c)).start()
        token[...] = jnp.zeros_like(token)

    shape = (N_DEV,) + wsh.shape
    land = pltpu.with_memory_space_constraint(lax.empty(shape, wsh.dtype), pltpu.HBM)
    return _pcall(
        body, name="gather_start",
        out_shape=(pltpu.SemaphoreType.DMA((4,)), pltpu.SemaphoreType.DMA((4,)), pltpu.HBM(wsh.shape, wsh.dtype),
                   pltpu.HBM(shape, wsh.dtype), jax.ShapeDtypeStruct((8, 128), F32)),
        in_specs=(_HBM, _HBM, _ANY), out_specs=(_SEM, _SEM, _HBM, _HBM, pl.BlockSpec(memory_space=pltpu.VMEM)),
        input_output_aliases={0: 2, 1: 3}, compiler_params=pltpu.CompilerParams(has_side_effects=_EFFECT),
    )(pltpu.with_memory_space_constraint(wsh, pltpu.HBM), land, order_after)


def _gather_wait(send_sems, recv_sems, w_thru, land_thru, after):
    def body(w_ref, land_ref, send_sems, recv_sems, after_ref, w_out, land_out):
        x, y, c, _ = _mesh_pos()
        senders = [(x, y, 1 - c)] + [(*chip, c) for chip in _other_chips(x, y)]
        for k, (px, py, pc) in enumerate(senders):
            cp = _rcopy(w_ref, land_ref.at[4 * px + 2 * py + pc], send_sems, recv_sems, k, (px, py, pc))
            cp.wait_send()
            cp.wait_recv()

    return _pcall(
        body, name="gather_wait", out_shape=(pltpu.HBM(w_thru.shape, w_thru.dtype), pltpu.HBM(land_thru.shape, land_thru.dtype)),
        in_specs=(_HBM, _HBM, _SEM, _SEM, _ANY), out_specs=(_HBM, _HBM), input_output_aliases={0: 0, 1: 1},
        compiler_params=pltpu.CompilerParams(has_side_effects=_EFFECT),
    )(w_thru, land_thru, send_sems, recv_sems, after)


def _gather_finish(wsh, land):
    def body(land_ref, out_ref, send_sems, recv_sems):
        x, y, c, _ = _mesh_pos()
        sib = (x, y, 1 - c)
        sends, recvs = [], []
        for j, (px, py) in enumerate(_other_chips(x, y)):
            mine, theirs = 4 * px + 2 * py + c, 4 * px + 2 * py + (1 - c)
            sends.append(_rcopy(land_ref.at[mine], out_ref.at[mine], send_sems, recv_sems, j, sib))
            recvs.append(_rcopy(land_ref.at[theirs], out_ref.at[theirs], send_sems, recv_sems, j, sib))
        for cp in sends:
            cp.start()
        for cp in recvs:
            cp.wait_recv()
        for cp in sends:
            cp.wait_send()

    done = _pcall(
        body, in_specs=[_ANY], out_specs=_ANY, out_shape=jax.ShapeDtypeStruct(land.shape, land.dtype),
        input_output_aliases={0: 0},
        scratch_shapes=[pltpu.SemaphoreType.DMA((3,)), pltpu.SemaphoreType.DMA((3,))], name="gather_finish",
    )(land)
    me = 4 * lax.axis_index("x") + 2 * lax.axis_index("y") + lax.axis_index("c")
    return lax.dynamic_update_index_in_dim(done, wsh, me, axis=0)


_ROW_TILES = (400, 368, 352, 256, 128, 16, 8)


def _exchange_sibling(g4, name):
    def body(g_ref, r_ref, send_sems, recv_sems):
        x, y, c, _ = _mesh_pos()
        sib = (x, y, 1 - c)
        cps = [_rcopy(g_ref.at[q, 1 - c], r_ref.at[q], send_sems, recv_sems, q, sib) for q in range(4)]
        for cp in cps:
            cp.start()
        for cp in cps:
            cp.wait()

    n, _, R, C = g4.shape
    return _pcall(
        body, in_specs=[_ANY], out_specs=_ANY, out_shape=jax.ShapeDtypeStruct((n, R, C), g4.dtype),
        scratch_shapes=[pltpu.SemaphoreType.DMA((4,)), pltpu.SemaphoreType.DMA((4,))], name=name,
    )(g4)


def _sum_sibling(g4, rsib, cvec, name):
    n, _, R, C = g4.shape
    tr = _pick(R, _ROW_TILES)

    def body(c_ref, g_ref, r_ref, o_ref):
        o_ref[...] = (g_ref[...].astype(F32) + r_ref[...].astype(F32)).astype(o_ref.dtype)

    grid_spec = pltpu.PrefetchScalarGridSpec(
        num_scalar_prefetch=1, grid=(n, R // tr),
        in_specs=[pl.BlockSpec((None, None, tr, C), lambda q, i, cr: (q, cr[0], i, 0)), pl.BlockSpec((None, tr, C), lambda q, i, cr: (q, i, 0))],
        out_specs=pl.BlockSpec((None, tr, C), lambda q, i, cr: (q, i, 0)))
    return _pcall(body, grid_spec=grid_spec, out_shape=jax.ShapeDtypeStruct((n, R, C), g4.dtype), name=name)(cvec, g4, rsib)


def _chips_start(part):
    def body(p_ref, land_ref, send_sems, recv_sems, p_thru, land_thru, token):
        x, y, c, _ = _mesh_pos()
        myq = 2 * x + y
        for j, (px, py) in enumerate(_other_chips(x, y)):
            _rcopy(p_ref.at[2 * px + py], land_ref.at[myq], send_sems, recv_sems, j, (px, py, c)).start()
        token[...] = jnp.zeros_like(token)

    land = pltpu.with_memory_space_constraint(lax.empty(part.shape, part.dtype), pltpu.HBM)
    return _pcall(
        body, name="chips_start",
        out_shape=(pltpu.SemaphoreType.DMA((3,)), pltpu.SemaphoreType.DMA((3,)), pltpu.HBM(part.shape, part.dtype),
                   pltpu.HBM(part.shape, part.dtype), jax.ShapeDtypeStruct((8, 128), F32)),
        in_specs=(_HBM, _HBM), out_specs=(_SEM, _SEM, _HBM, _HBM, pl.BlockSpec(memory_space=pltpu.VMEM)),
        input_output_aliases={0: 2, 1: 3}, compiler_params=pltpu.CompilerParams(has_side_effects=_EFFECT),
    )(pltpu.with_memory_space_constraint(part, pltpu.HBM), land)


def _chips_wait(send_sems, recv_sems, p_thru, land_thru, after):
    def body(p_ref, land_ref, send_sems, recv_sems, after_ref, p_out, land_out):
        x, y, c, _ = _mesh_pos()
        for j, (px, py) in enumerate(_other_chips(x, y)):
            q = 2 * px + py
            cp = _rcopy(p_ref.at[q], land_ref.at[q], send_sems, recv_sems, j, (px, py, c))
            cp.wait_send()
            cp.wait_recv()

    return _pcall(
        body, name="chips_wait", out_shape=(pltpu.HBM(p_thru.shape, p_thru.dtype), pltpu.HBM(p_thru.shape, p_thru.dtype)),
        in_specs=(_HBM, _HBM, _SEM, _SEM, _ANY), out_specs=(_HBM, _HBM), input_output_aliases={0: 0, 1: 1},
        compiler_params=pltpu.CompilerParams(has_side_effects=_EFFECT),
    )(p_thru, land_thru, send_sems, recv_sems, after)


def _sum_chips(part, land, qvec, name):
    n, R, C = part.shape
    tr = _pick(R, _ROW_TILES)

    def body(q_ref, p_ref, *refs):
        o_ref = refs[n]
        acc = None
        for q in range(n):
            term = jnp.where(q_ref[0] == q, p_ref[...], refs[q][...]).astype(F32)
            acc = term if acc is None else acc + term
        o_ref[...] = acc

    def land_spec(q):
        return pl.BlockSpec((None, tr, C), lambda i, m: (jnp.where(m[0] == q, (q + 1) % n, q), i, 0))

    grid_spec = pltpu.PrefetchScalarGridSpec(
        num_scalar_prefetch=1, grid=(R // tr,),
        in_specs=[pl.BlockSpec((None, tr, C), lambda i, m: (m[0], i, 0))] + [land_spec(q) for q in range(n)],
        out_specs=pl.BlockSpec((tr, C), lambda i, m: (i, 0)))
    return _pcall(body, grid_spec=grid_spec, out_shape=jax.ShapeDtypeStruct((R, C), F32), name=name)(qvec, part, *([land] * n))


def _exchange_chips(part, gsmall):
    def body(p_ref, s_ref, recv_ref, srecv_ref, send_sems, recv_sems, loc_sems):
        x, y, c, me = _mesh_pos()
        myq = 2 * x + y
        loc = [pltpu.make_async_copy(s_ref, srecv_ref.at[me], loc_sems.at[0])]
        for cp in loc:
            cp.start()
        sends, recvs = [], []
        for j, (px, py) in enumerate([(1 - x, y), (x, 1 - y), (1 - x, 1 - y)]):
            q = 2 * px + py
            sends.append(_rcopy(p_ref.at[q], recv_ref.at[myq], send_sems, recv_sems, j, (px, py, c)))
            recvs.append(_rcopy(p_ref.at[q], recv_ref.at[q], send_sems, recv_sems, j, (px, py, c)))
        for k in range(1, N_DEV):
            to, pidx = _peer(x, y, c, k)
            sends.append(_rcopy(s_ref, srecv_ref.at[me], send_sems, recv_sems, 2 + k, to))
            recvs.append(_rcopy(s_ref, srecv_ref.at[pidx], send_sems, recv_sems, 2 + k, to))
        for cp in sends:
            cp.start()
        for cp in recvs:
            cp.wait_recv()
        for cp in sends:
            cp.wait_send()
        for cp in loc:
            cp.wait()

    return _pcall(
        body, in_specs=[_ANY, _ANY], out_specs=[_ANY, _ANY],
        out_shape=[jax.ShapeDtypeStruct(part.shape, part.dtype), jax.ShapeDtypeStruct((N_DEV,) + gsmall.shape, gsmall.dtype)],
        scratch_shapes=[pltpu.SemaphoreType.DMA((10,)), pltpu.SemaphoreType.DMA((10,)), pltpu.SemaphoreType.DMA((1,))],
        name="exchange_chips",
    )(part, gsmall)


def _sum_slots(recv, name):
    n, R, C = recv.shape
    tr = _pick(R, _ROW_TILES)

    def body(r_ref, o_ref):
        acc = r_ref[0].astype(F32)
        for s in range(1, n):
            acc = acc + r_ref[s].astype(F32)
        o_ref[...] = acc

    return _pcall(body, grid=(R // tr,), in_specs=[pl.BlockSpec((n, tr, C), lambda i: (0, i, 0))],
                  out_specs=pl.BlockSpec((tr, C), lambda i: (i, 0)), out_shape=jax.ShapeDtypeStruct((R, C), F32), name=name)(recv)


def _adamw(w, g, m, v, name):
    shape = w.shape
    w2, g2, m2, v2 = (a.reshape(-1, shape[-1]) for a in (w, g, m, v))
    R, C = w2.shape
    tr = _pick(R, (256, 128, 64, 32, 16, 8)) if R > 256 else R

    def body(w_ref, g_ref, m_ref, v_ref, d_ref, nm_ref, nv_ref):
        gg = g_ref[...]
        nm = ADAM_B1 * m_ref[...] + (1.0 - ADAM_B1) * gg
        nv = ADAM_B2 * v_ref[...] + (1.0 - ADAM_B2) * jnp.square(gg)
        m_hat = nm / (1.0 - ADAM_B1 ** ADAM_STEP)
        v_hat = nv / (1.0 - ADAM_B2 ** ADAM_STEP)
        d_ref[...] = -ADAM_LR * (m_hat / (jnp.sqrt(v_hat) + ADAM_EPS) + ADAM_WD * w_ref[...])
        nm_ref[...] = nm
        nv_ref[...] = nv

    spec = pl.BlockSpec((tr, C), lambda i: (i, 0))
    d, nm, nv = _pcall(body, grid=(R // tr,), in_specs=[spec] * 4, out_specs=[spec] * 3,
                       out_shape=[jax.ShapeDtypeStruct((R, C), F32)] * 3, name=name)(w2, g2, m2, v2)
    return d.reshape(shape), nm.reshape(shape), nv.reshape(shape)


def _to_rows(name, w):
    w = w[0]
    if name in ("ffn1_w_in", "ffn2_w_in", "w_in"):
        return w.T
    if name in ("w_uq", "w_ukv", "ple_w_proj"):
        return w.T.reshape(-1, 1024)
    return w


def _from_rows(name, g, shape):
    if name in ("ffn1_w_in", "ffn2_w_in", "w_in"):
        return g.T.reshape(shape)
    if name in ("w_uq", "w_ukv", "ple_w_proj"):
        return g.reshape(-1, shape[1]).T.reshape(shape)
    return g.reshape(shape)


def _unpack(wall, group):
    out, off = {}, 0
    for name, r in group:
        out[name] = wall[:, off:off + r, :].reshape(N_DEV * r, 1024)
        off += _pad16(r)
    return out


def _w_in_internal(wt):
    return jnp.concatenate([wt[0:6144], wt[6720:8768], wt[6144:6720], jnp.zeros((P_W - W_IN_COLS, 1024), wt.dtype)], axis=0)


def _w_in_external(d):
    return jnp.concatenate([d[0:6144], d[8192:8768], d[6144:8192]], axis=0)


def _rope_tables(positions):
    pos = positions[0].astype(F32)

    def cs(half):
        inv = ROPE_BASE ** (-jnp.arange(half, dtype=F32) / half)
        ang = pos[:, None] * inv
        return jnp.cos(ang), jnp.sin(ang)

    c, s = cs(64)
    c2, s2 = cs(32)
    return (jnp.concatenate([c, c], axis=1), jnp.concatenate([-s, s], axis=1),
            jnp.concatenate([c2, c2, c2, c2], axis=1), jnp.concatenate([-s2, s2, -s2, s2], axis=1))


def _local_step(x, p, positions, target, W, ln_g, ln_b, gn_g, qg, kvg, early=None, rest_weights=None, start_token=None):
    T = x.shape[0]
    tabs = _rope_tables(positions)
    rc = _ret_consts()
    lg = [ln_g[i:i + 1] for i in range(4)]
    lb = [ln_b[i:i + 1] for i in range(4)]
    pb = p.astype(BF16)

    hs1, gu1, *xb = _ffn_in(x, W["ffn1_w_in"], "ffn1_in", order_after=start_token)
    xb = xb[0] if xb else x
    f1, h1, h1b = _mm(hs1, W["ffn1_w_out"], name="ffn1_out", tm=LN_TM, epilogue=_ln_epilogue(x, 0.5, lg[0], lb[0]))
    if rest_weights is not None:
        W = {**W, **rest_weights(h1b)}
    w_in_t = _w_in_internal(W["w_in"])
    wuq = W["w_uq"].reshape(1536, LORA).reshape(HEADS, 192, LORA)
    wuq = jnp.concatenate([wuq[:, :128].reshape(1024, LORA), wuq[:, 128:].reshape(512, LORA)], axis=0)
    wukv = W["w_ukv"].reshape(2048, LORA).reshape(HEADS, 2, 128, LORA).transpose(1, 0, 2, 3).reshape(2048, LORA)
    wp_t = W["ple_w_proj"].reshape(1024, D_PLE)
    proj = _mm(h1b, w_in_t, tb=True, name="mixer_in")
    rq, rk, rv, qn, kvn, kpe = _mixer_prep_fwd(proj, tabs, qg, kvg, T)
    y, yr, states = _ret_fwd(rq, rk, rv, proj, gn_g, rc, T)
    y_ret = _mm(yr, W["w_ret_o"], name="ret_o")
    q = _mm(qn, wuq, tb=True, name="mla_uq")
    kv = _mm(kvn, wukv, tb=True, out_dtype=BF16, name="mla_ukv")
    qnope, qpe = _q_assemble_fwd(q, tabs, T)
    o, lse = _attn_fwd(qnope, qpe, kv, kpe, T)
    y_mla = _mm(o, W["w_mla_o"], name="mla_o")
    mix = _mix_fwd(proj, y_ret, y_mla, T)
    mixed, h2, h2b = _mm(mix, W["w_out"], name="mixer_out", tm=LN_TM, epilogue=_ln_epilogue(h1, 1.0, lg[1], lb[1]))
    hs2, gu2 = _ffn_in(h2b, W["ffn2_w_in"], "ffn2_in")
    f2, h3, h3b = _mm(hs2, W["ffn2_w_out"], name="ffn2_out", tm=LN_TM, epilogue=_ln_epilogue(h2, 0.5, lg[2], lb[2]))
    gl = _mm(h3b, W["ple_w_gate"], name="ple_gate")
    pp = _mm(pb, wp_t, tb=True, name="ple_proj")

    G = {}
    dh3_a, dgl, dpp, dg3, db3, loss = _head(h3, gl, pp, target, lg[3], lb[3], T)
    G["ple_w_gate"] = _mm(h3b, dgl, ta=True, name="d_ple_gate")
    G["ple_w_proj"] = _mm(dpp, pb, ta=True, name="d_ple_proj")
    dh2_a, df2, dg2, db2 = _mm(dgl, W["ple_w_gate"], tb=True, add=dh3_a, name="dh3", tm=LN_TM,
                               epilogue=_ln_bwd_epilogue(h2, f2, 0.5, lg[2], lb[2]))
    G["ffn2_w_out"] = _mm(hs2, df2, ta=True, name="d_ffn2_out")
    da2 = _ffn_act_bwd(df2, W["ffn2_w_out"], gu2, "ffn2_act_bwd")
    G["ffn2_w_in"] = _mm(da2, h2b, ta=True, name="d_ffn2_in")
    dh1_a, dmixed, dg1, db1 = _mm(da2, W["ffn2_w_in"], add=dh2_a, name="dh2", tm=LN_TM,
                                  epilogue=_ln_bwd_epilogue(h1, mixed, 1.0, lg[1], lb[1]))
    G["w_out"] = _mm(mix, dmixed, ta=True, name="d_mixer_out")
    dmix = _mm(dmixed, W["w_out"], tb=True, name="dmix")
    dgr, dgm, dy_ret, dy_mla = _mix_bwd(proj, y_ret, y_mla, dmix, T)
    G["w_mla_o"] = _mm(o, dy_mla, ta=True, name="d_mla_o")
    do = _mm(dy_mla, W["w_mla_o"], tb=True, name="do")
    dob, delta = _attn_delta(do, o, T)
    dqn_f, dqpe_f, dkn, dkpe_all, dv = _attn_bwd(qnope, qpe, kv, kpe, dob, lse, delta, T)
    dq_n, dq_r = _q_assemble_bwd(dqn_f, dqpe_f, tabs, T)
    g_uq = jnp.concatenate([_mm(dq_n, qn, ta=True, name="d_uq_nope"), _mm(dq_r, qn, ta=True, name="d_uq_rope")], axis=0)
    g_uq = jnp.concatenate([g_uq[:1024].reshape(HEADS, 128, LORA), g_uq[1024:].reshape(HEADS, 64, LORA)], axis=1)
    G["w_uq"] = g_uq.reshape(1536 * LORA // 1024, 1024)
    dqn = _mm(dq_r, wuq[1024:], add=_mm(dq_n, wuq[:1024], name="dqn_a"), name="dqn_b")
    g_ukv = jnp.stack([_mm(dkn, kvn, ta=True, name="d_ukv_k"), _mm(dv, kvn, ta=True, name="d_ukv_v")], axis=0)
    G["w_ukv"] = g_ukv.reshape(2, HEADS, 128, LORA).transpose(1, 0, 2, 3).reshape(2048 * LORA // 1024, 1024)
    dkvn = _mm(dv, wukv[1024:], add=_mm(dkn, wukv[:1024], name="dkvn_a"), name="dkvn_b")
    dcq, dckv, dkpe, dqg, dkvg = _rms_bwd(proj, dqn, dkvn, dkpe_all, tabs, qg, kvg, T)
    G["w_ret_o"] = _mm(yr, dy_ret, ta=True, name="d_ret_o")
    dyr = _mm(dy_ret, W["w_ret_o"], tb=True, name="dyr")
    drq, drk, drv, drg, dgn = _ret_bwd(rq, rk, rv, y, proj, gn_g, states, dyr, rc, tabs, T)
    dproj = jnp.concatenate([drq, drk, drv, drg, dgr, dgm, dcq, dckv, dkpe, jnp.zeros((T, P_W - P_KPE - 128), BF16)], axis=1)
    G["w_in"] = _w_in_external(_mm(dproj, h1b, ta=True, name="d_mixer_in"))
    lg0 = lg[0] if early is None else lg[0] + early(G)[0:1, 0:1]
    dx_a, df1, dg0, db0 = _mm(dproj, w_in_t, add=dh1_a, name="dh1", tm=LN_TM,
                              epilogue=_ln_bwd_epilogue(x, f1, 0.5, lg0, lb[0]))
    G["ffn1_w_out"] = _mm(hs1, df1, ta=True, name="d_ffn1_out")
    da1 = _ffn_act_bwd(df1, W["ffn1_w_out"], gu1, "ffn1_act_bwd")
    G["ffn1_w_in"] = _mm(da1, xb, ta=True, name="d_ffn1_in")
    grad_x = _mm(da1, W["ffn1_w_in"], add=dx_a, name="grad_x")

    small = dict(ln_g=jnp.concatenate([dg0, dg1, dg2, dg3], axis=0), ln_b=jnp.concatenate([db0, db1, db2, db3], axis=0),
                 ret_gn_g=dgn, q_norm_g=dqg, kv_norm_g=dkvg)
    return loss, grad_x, G, small


def kernel(x, p, positions, ln_g, ln_b, ffn1_w_in, ffn1_w_out, w_in, ret_gn_g, w_ret_o, q_norm_g, kv_norm_g, w_uq, w_ukv, w_mla_o, w_out, ffn2_w_in, ffn2_w_out, ple_w_gate, ple_w_proj, loss_target, m_ln_g, m_ln_b, m_ffn1_w_in, m_ffn1_w_out, m_w_in, m_ret_gn_g, m_w_ret_o, m_q_norm_g, m_kv_norm_g, m_w_uq, m_w_ukv, m_w_mla_o, m_w_out, m_ffn2_w_in, m_ffn2_w_out, m_ple_w_gate, m_ple_w_proj, v_ln_g, v_ln_b, v_ffn1_w_in, v_ffn1_w_out, v_w_in, v_ret_gn_g, v_w_ret_o, v_q_norm_g, v_kv_norm_g, v_w_uq, v_w_ukv, v_w_mla_o, v_w_out, v_ffn2_w_in, v_ffn2_w_out, v_ple_w_gate, v_ple_w_proj):
    names = ("ln_g", "ln_b", "ffn1_w_in", "ffn1_w_out", "w_in", "ret_gn_g", "w_ret_o", "q_norm_g", "kv_norm_g", "w_uq", "w_ukv",
             "w_mla_o", "w_out", "ffn2_w_in", "ffn2_w_out", "ple_w_gate", "ple_w_proj")
    ws = dict(zip(names, (ln_g, ln_b, ffn1_w_in, ffn1_w_out, w_in, ret_gn_g, w_ret_o, q_norm_g, kv_norm_g, w_uq, w_ukv, w_mla_o,
                          w_out, ffn2_w_in, ffn2_w_out, ple_w_gate, ple_w_proj)))
    ms = dict(zip(names, (m_ln_g, m_ln_b, m_ffn1_w_in, m_ffn1_w_out, m_w_in, m_ret_gn_g, m_w_ret_o, m_q_norm_g, m_kv_norm_g, m_w_uq,
                          m_w_ukv, m_w_mla_o, m_w_out, m_ffn2_w_in, m_ffn2_w_out, m_ple_w_gate, m_ple_w_proj)))
    vs = dict(zip(names, (v_ln_g, v_ln_b, v_ffn1_w_in, v_ffn1_w_out, v_w_in, v_ret_gn_g, v_w_ret_o, v_q_norm_g, v_kv_norm_g, v_w_uq,
                          v_w_ukv, v_w_mla_o, v_w_out, v_ffn2_w_in, v_ffn2_w_out, v_ple_w_gate, v_ple_w_proj)))

    parts = []
    for name, r in PACK:
        rows = _to_rows(name, ws[name])
        if _pad16(r) != r:
            rows = jnp.concatenate([rows, jnp.zeros((_pad16(r) - r, 1024), F32)], axis=0)
        parts.append(rows)
    wsh_first = jnp.concatenate(parts[:len(PACK_LATE)], axis=0).astype(BF16)
    wsh_rest = jnp.concatenate(parts[len(PACK_LATE):], axis=0).astype(BF16)
    ssh = jnp.concatenate([ln_g[0], ln_b[0]], axis=0)
    wall_first, sall = _all_gather(wsh_first, ssh)
    *gather_handles, start_token = _gather_start(wsh_rest, wall_first)
    W = _unpack(wall_first, PACK_LATE)
    ln_full = sall.reshape(N_DEV, 2, 4, 128).transpose(1, 2, 0, 3).reshape(2, 4, 1024)

    def rest_weights(after):
        w_thru, land = _gather_wait(*gather_handles, after)
        return _unpack(_gather_finish(w_thru, land), PACK_EARLY)

    cvec = lax.axis_index("c").astype(jnp.int32).reshape(1)
    qvec = (2 * lax.axis_index("x") + lax.axis_index("y")).astype(jnp.int32).reshape(1)

    def chip_partials(G, group, tag):
        gparts = []
        for name, r in group:
            g = G[name].reshape(N_DEV, r, 1024)
            if _pad16(r) != r:
                g = jnp.concatenate([g, jnp.zeros((N_DEV, _pad16(r) - r, 1024), g.dtype)], axis=1)
            gparts.append(g)
        gfull = jnp.concatenate(gparts, axis=1).astype(BF16)
        g4 = gfull.reshape(4, 2, gfull.shape[1], 1024)
        return _sum_sibling(g4, _exchange_sibling(g4, "exchange_sibling_" + tag), cvec, "sum_sibling_" + tag)

    in_flight = []

    def early(G):
        *handles, token = _chips_start(chip_partials(G, PACK_EARLY, "early"))
        in_flight.append(handles)
        return token

    loss_p, grad_x, G, small = _local_step(x[0], p[0, 0], positions, loss_target[0], W, ln_full[0], ln_full[1],
                                           ret_gn_g, q_norm_g, kv_norm_g, early=early, rest_weights=rest_weights,
                                           start_token=start_token)

    part_e, land_e = _chips_wait(*in_flight[0], grad_x)
    gsh_early = _sum_chips(part_e, land_e, qvec, "sum_grads_early")
    pad256 = lambda a: jnp.concatenate([a, jnp.zeros((1, 1024 - a.shape[1]), F32)], axis=1)
    gsmall = jnp.concatenate([small["ln_g"], small["ln_b"], small["ret_gn_g"].reshape(2, 1024), pad256(small["q_norm_g"]),
                              pad256(small["kv_norm_g"]), jnp.zeros((SMALL_ROWS - 12, 1024), F32)], axis=0)
    part_l = chip_partials(G, PACK_LATE, "late")
    recv, srecv = _exchange_chips(part_l, gsmall)
    gsh_late = _sum_chips(part_l, recv, qvec, "sum_grads_late")
    ssum = _sum_slots(srecv, "sum_small_grads")

    grads = {}
    for group, gsh in ((PACK_LATE, gsh_late), (PACK_EARLY, gsh_early)):
        off = 0
        for name, r in group:
            grads[name] = _from_rows(name, gsh[off:off + r], ws[name].shape)
            off += _pad16(r)
    me = 4 * lax.axis_index("x") + 2 * lax.axis_index("y") + lax.axis_index("c")
    grads["ln_g"] = lax.dynamic_slice(ssum[0:4], (0, me * 128), (4, 128)).reshape(1, 4, 128)
    grads["ln_b"] = lax.dynamic_slice(ssum[4:8], (0, me * 128), (4, 128)).reshape(1, 4, 128)
    grads["ret_gn_g"] = ssum[8:10].reshape(1, 2048)
    grads["q_norm_g"] = ssum[10:11, :256]
    grads["kv_norm_g"] = ssum[11:12, :256]

    delta, new_m, new_v = {}, {}, {}
    for name in names:
        delta[name], new_m[name], new_v[name] = _adamw(ws[name], grads[name], ms[name], vs[name], "adamw_" + name)

    loss = lax.psum(loss_p[0, 0], ("x", "y", "c"))
    return (loss, grad_x[None], *[grads[n] for n in names], *[delta[n] for n in names],
            *[new_m[n] for n in names], *[new_v[n] for n in names])
```

```python
import functools
import math

import jax
import jax.numpy as jnp
from jax import lax
from jax.experimental import pallas as pl
from jax.experimental.pallas import tpu as pltpu

F32 = jnp.float32
BF16 = jnp.bfloat16

N_DEV = 8
D = 1024
D_FF = 2816
D_PLE = 256
CHUNK = 64
HEADS = 8
RET_DK = 128
RET_DV = 256
MLA_NOPE = 128
MLA_ROPE = 64
MLA_DV = 128
LORA = 256
ROPE_BASE = 10000.0
EPS = 1e-5
ALPHA = 2.0 ** 0.25
RET_SCALE = RET_DK ** -0.5
MLA_SCALE = (MLA_NOPE + MLA_ROPE) ** -0.5
NEG = -1e30

ADAM_LR = 0.001
ADAM_B1 = 0.9
ADAM_B2 = 0.999
ADAM_EPS = 1e-08
ADAM_WD = 0.01
ADAM_STEP = 10

P_RQ, P_RK, P_RV, P_RG, P_GR, P_GM, P_CQ, P_CKV, P_KPE, P_W = 0, 1024, 2048, 4096, 6144, 7168, 8192, 8448, 8704, 8960
W_IN_COLS = 8768
RET_L = 256
ATT_TF = 1024
ATT_TB = 1024
ATT_HP = 2
LOG2E = math.log2(math.e)
Q_PRESCALE = MLA_SCALE * LOG2E

PACK = (("ffn1_w_in", 704), ("ffn1_w_out", 352), ("w_in", 1096), ("w_ret_o", 256), ("w_uq", 48), ("w_ukv", 64),
        ("w_mla_o", 128), ("w_out", 128), ("ffn2_w_in", 704), ("ffn2_w_out", 352), ("ple_w_gate", 128), ("ple_w_proj", 32))


def _pad16(r):
    return -(-r // 16) * 16


PACK_ROWS = sum(_pad16(r) for _, r in PACK)
PACK_LATE = PACK[:2]
PACK_EARLY = PACK[2:]
SMALL_ROWS = 16


def _pcall(body, **kw):
    return pl.pallas_call(body, **kw)


def _pick(dim, prefs):
    for p in prefs:
        if dim % p == 0:
            return p
    return dim


def _sigmoid(x):
    return 1.0 / (1.0 + jnp.exp(-x))


def _silu(x):
    return x * _sigmoid(x)


def _ln(r, g, b):
    mu = jnp.mean(r, axis=-1, keepdims=True)
    var = jnp.mean(jnp.square(r - mu), axis=-1, keepdims=True)
    return (r - mu) * lax.rsqrt(var + EPS) * g + b


def _rms(x, g):
    return x * lax.rsqrt(jnp.mean(jnp.square(x), axis=-1, keepdims=True) + EPS) * g


def _dot(a, b, ca, cb):
    return lax.dot_general(a, b, (((ca,), (cb,)), ((), ())), preferred_element_type=F32)


def _accum(ref, val, first=None):
    @pl.when(pl.program_id(0) == 0 if first is None else first)
    def _():
        ref[...] = jnp.zeros_like(ref)

    ref[...] += val


def _mm(a, b, *, ta=False, tb=False, add=None, out_dtype=None, name, tm=None, tn=None, tk=None, epilogue=None):
    parts = a.shape[0] if a.ndim == 3 else 1
    ar, ac = a.shape[-2], a.shape[-1]
    out_dtype = out_dtype or (BF16 if ta else F32)
    if ta:
        K, M = ar, ac * parts
    else:
        M, K = ar, ac * parts
    if tb:
        N, K2 = b.shape
    else:
        K2, N = b.shape
    assert K == K2, (a.shape, b.shape, ta, tb)
    big = (1024, 1408, 1280, 768, 512, 256, 128)
    tm = tm or _pick(ac if (ta and parts > 1) else M, big)
    tn = tn or (N if N <= 1024 else _pick(N, big))
    kdim = ac if (not ta and parts > 1) else K
    tk = tk or (kdim if kdim <= 2816 and parts == 1 else
                _pick(kdim, (2048, 1408, 1280, 1024, 512) if tm <= 1024 else (1024, 1408, 1280, 512)))
    nk = K // tk
    grid = (M // tm, N // tn, nk)
    if parts > 1 and ta:
        per = ac // tm
        a_spec = pl.BlockSpec((None, tk, tm), lambda i, j, k: (i // per, k, i % per))
    elif parts > 1:
        per = ac // tk
        a_spec = pl.BlockSpec((None, tm, tk), lambda i, j, k: (k // per, i, k % per))
    else:
        a_spec = pl.BlockSpec((tk, tm), lambda i, j, k: (k, i)) if ta else pl.BlockSpec((tm, tk), lambda i, j, k: (i, k))
    b_spec = pl.BlockSpec((tn, tk), lambda i, j, k: (j, k)) if tb else pl.BlockSpec((tk, tn), lambda i, j, k: (k, j))
    o_spec = pl.BlockSpec((tm, tn), lambda i, j, k: (i, j))
    ca, cb = (0 if ta else 1), (1 if tb else 0)
    has_add = add is not None
    n_in = 2 + int(has_add)
    if epilogue is not None:
        assert tn == N and not ta
        ep_fn, ep_rows, ep_whole, ep_outs, ep_accs = epilogue
        n_ep_in = len(ep_rows) + len(ep_whole)
        n_out = len(ep_outs) + len(ep_accs)
    else:
        n_ep_in, n_out = 0, 1

    def body(*refs):
        a_ref, b_ref = refs[0], refs[1]
        add_ref = refs[2] if has_add else None
        o_ref = refs[n_in + n_ep_in]
        first_row_tile = pl.program_id(0) == 0

        def finish(r):
            if has_add:
                r = r + add_ref[...].astype(F32)
            if epilogue is not None:
                ep_fn(r, refs[n_in:n_in + n_ep_in], refs[n_in + n_ep_in:n_in + n_ep_in + n_out], first_row_tile)
            else:
                o_ref[...] = r.astype(out_dtype)

        if nk == 1:
            finish(_dot(a_ref[...], b_ref[...], ca, cb))
            return
        acc_ref = refs[-1]
        k = pl.program_id(2)

        @pl.when(k == 0)
        def _():
            acc_ref[...] = jnp.zeros_like(acc_ref)

        acc_ref[...] += _dot(a_ref[...], b_ref[...], ca, cb)

        @pl.when(k == nk - 1)
        def _():
            finish(acc_ref[...])

    in_specs = [a_spec, b_spec] + ([o_spec] if has_add else [])
    args = (a, b) + ((add,) if has_add else ())
    out_specs, out_shape = o_spec, jax.ShapeDtypeStruct((M, N), out_dtype)
    if epilogue is not None:
        for row_in in ep_rows:
            arr, col_block = row_in if isinstance(row_in, tuple) else (row_in, 0)
            in_specs.append(pl.BlockSpec((tm, N), lambda i, j, k, _c=col_block: (i, _c)))
            args += (arr,)
        in_specs += [pl.BlockSpec(w.shape, lambda i, j, k, _n=w.ndim: (0,) * _n) for w in ep_whole]
        args += tuple(ep_whole)
        out_specs = [o_spec] * len(ep_outs) + [pl.BlockSpec((r, N), lambda i, j, k: (0, 0)) for r in ep_accs]
        out_shape = [jax.ShapeDtypeStruct((M, N), dt) for dt in ep_outs] + [jax.ShapeDtypeStruct((r, N), F32) for r in ep_accs]
    return _pcall(
        body, grid=grid, in_specs=in_specs, out_specs=out_specs, out_shape=out_shape,
        scratch_shapes=[pltpu.VMEM((tm, tn), F32)] if nk > 1 else [], name=name,
        compiler_params=pltpu.CompilerParams(dimension_semantics=("arbitrary" if epilogue is not None else "parallel", "parallel", "arbitrary")),
    )(*args)


def _ln_epilogue(res, c, g, b):
    def fn(r, ins, outs, first):
        res_ref, g_ref, b_ref = ins
        f_ref, h_ref, hb_ref = outs
        h = _ln(ALPHA * res_ref[...] + c * r, g_ref[...], b_ref[...])
        f_ref[...] = r
        h_ref[...] = h
        hb_ref[...] = h.astype(BF16)

    return (fn, [res], [g, b], (F32, F32, BF16), ())


def _ln_bwd_epilogue(res, f, c, g, b):
    def fn(r, ins, outs, first):
        res_ref, f_ref, g_ref, b_ref = ins
        dr_ref, df_ref, dg_ref, db_ref = outs
        _, vjp = jax.vjp(lambda rr, ff, gg, bb: _ln(ALPHA * rr + c * ff, gg, bb), res_ref[...], f_ref[...], g_ref[...], b_ref[...])
        dr, df, dg, db = vjp(r)
        dr_ref[...] = dr
        df_ref[...] = df.astype(BF16)
        _accum(dg_ref, dg, first)
        _accum(db_ref, db, first)

    return (fn, [res, f], [g, b], (F32, BF16), (1, 1))


def _rows(body, T, tm, ins, outs, name, accs=()):
    in_specs, args = [], []
    for arr, w, cb in ins:
        if w is None:
            in_specs.append(pl.BlockSpec(arr.shape, lambda i, _n=arr.ndim: (0,) * _n))
        else:
            in_specs.append(pl.BlockSpec((tm, w), lambda i, _cb=cb: (i, _cb)))
        args.append(arr)
    out_specs = [pl.BlockSpec((tm, w), lambda i: (i, 0)) for w, _ in outs]
    out_shape = [jax.ShapeDtypeStruct((T, w), dt) for w, dt in outs]
    for r, w in accs:
        out_specs.append(pl.BlockSpec((r, w), lambda i: (0, 0)))
        out_shape.append(jax.ShapeDtypeStruct((r, w), F32))
    return _pcall(
        body, grid=(T // tm,), in_specs=in_specs, out_specs=out_specs, out_shape=out_shape, name=name,
        compiler_params=pltpu.CompilerParams(dimension_semantics=("arbitrary",)),
    )(*args)


FFN_TN = 1408
FFN_TM = 512
LN_TM = 512


def _ffn_in(x, wt, name, order_after=None):
    T = x.shape[0]
    tm, tn = min(FFN_TM, T), FFN_TN
    nj = D_FF // tn
    emit_xb = x.dtype != BF16
    n_in = 3 + int(order_after is not None)

    def body(*refs):
        x_ref, wg_ref, wu_ref = refs[:3]
        hs_ref, gu_ref = refs[n_in], refs[n_in + 1]
        xv = x_ref[...].astype(BF16)
        g = _dot(xv, wg_ref[...], 1, 1)
        u = _dot(xv, wu_ref[...], 1, 1)
        hs_ref[...] = (_silu(g) * u).astype(BF16)
        gu_ref[0] = g.astype(BF16)
        gu_ref[1] = u.astype(BF16)
        if emit_xb:
            refs[n_in + 2][...] = xv

    in_specs = [pl.BlockSpec((tm, D), lambda i, j: (i, 0)), pl.BlockSpec((tn, D), lambda i, j: (j, 0)),
                pl.BlockSpec((tn, D), lambda i, j: (j + nj, 0))]
    args = [x, wt, wt]
    if order_after is not None:
        in_specs.append(pl.BlockSpec(order_after.shape, lambda i, j: (0, 0)))
        args.append(order_after)
    out_specs = [pl.BlockSpec((tm, tn), lambda i, j: (i, j)), pl.BlockSpec((2, tm, tn), lambda i, j: (0, i, j))]
    out_shape = [jax.ShapeDtypeStruct((T, D_FF), BF16), jax.ShapeDtypeStruct((2, T, D_FF), BF16)]
    if emit_xb:
        out_specs.append(pl.BlockSpec((tm, D), lambda i, j: (i, 0)))
        out_shape.append(jax.ShapeDtypeStruct((T, D), BF16))
    return _pcall(
        body, grid=(T // tm, nj), in_specs=in_specs, out_specs=out_specs, out_shape=out_shape, name=name,
        compiler_params=pltpu.CompilerParams(dimension_semantics=("parallel", "arbitrary")),
    )(*args)


def _ffn_act_bwd(df, wo, gu, name):
    T = df.shape[0]
    tm, tn = min(FFN_TM, T), FFN_TN

    def body(d_ref, w_ref, gu_ref, o_ref):
        dhs = _dot(d_ref[...], w_ref[...], 1, 1)
        g, u = gu_ref[0].astype(F32), gu_ref[1].astype(F32)
        sig = _sigmoid(g)
        act = g * sig
        o_ref[0] = (dhs * u * (sig + act * (1.0 - sig))).astype(BF16)
        o_ref[1] = (dhs * act).astype(BF16)

    return _pcall(
        body, grid=(T // tm, D_FF // tn),
        in_specs=[pl.BlockSpec((tm, D), lambda i, j: (i, 0)), pl.BlockSpec((tn, D), lambda i, j: (j, 0)),
                  pl.BlockSpec((2, tm, tn), lambda i, j: (0, i, j))],
        out_specs=pl.BlockSpec((2, tm, tn), lambda i, j: (0, i, j)),
        out_shape=jax.ShapeDtypeStruct((2, T, D_FF), BF16), name=name,
        compiler_params=pltpu.CompilerParams(dimension_semantics=("parallel", "parallel")),
    )(df, wo, gu)


def _rope128(t, cos, sin_s):
    return t * cos + pltpu.roll(t, 64, 1) * sin_s


def _rope128_t(g, cos, sin_s):
    return g * cos - pltpu.roll(g, 64, 1) * sin_s


def _partner32(t):
    lane = lax.broadcasted_iota(jnp.int32, t.shape, 1)
    return jnp.where((lane & 32) == 0, pltpu.roll(t, 96, 1), pltpu.roll(t, 32, 1))


def _rope64(t, cos, sin_s):
    return t * cos + _partner32(t) * sin_s


def _rope64_t(g, cos, sin_s):
    return g * cos - _partner32(g) * sin_s


def _mixer_prep_fwd(proj, tabs, qg, kvg, T):
    cos128, sin128, cos64, sin64 = tabs

    def body(rq_ref, rk_ref, cq_ref, ckv_ref, kpe_ref, c1_ref, s1_ref, c2_ref, s2_ref, qg_ref, kvg_ref,
             oq_ref, ok_ref, oqn_ref, okvn_ref, okpe_ref):
        c1, s1 = c1_ref[...], s1_ref[...]
        for h in range(HEADS):
            sl = slice(h * RET_DK, (h + 1) * RET_DK)
            oq_ref[:, sl] = _rope128(rq_ref[:, sl].astype(F32), c1, s1).astype(BF16)
            ok_ref[:, sl] = (_rope128(rk_ref[:, sl].astype(F32), c1, s1) * RET_SCALE).astype(BF16)
        oqn_ref[...] = _rms(cq_ref[...].astype(F32), qg_ref[...]).astype(BF16)
        okvn_ref[...] = _rms(ckv_ref[...].astype(F32), kvg_ref[...]).astype(BF16)
        okpe_ref[...] = _rope64(kpe_ref[...].astype(F32), c2_ref[...], s2_ref[...]).astype(BF16)

    ins = [(proj, 1024, 0), (proj, 1024, 1), (proj, 256, P_CQ // 256), (proj, 256, P_CKV // 256),
           (proj, 128, P_KPE // 128), (cos128, 128, 0), (sin128, 128, 0), (cos64, 128, 0), (sin64, 128, 0),
           (qg, None, None), (kvg, None, None)]
    outs = [(1024, BF16), (1024, BF16), (LORA, BF16), (LORA, BF16), (128, BF16)]
    return _rows(body, T, 256, ins, outs, "mixer_prep_fwd")


def _rms_bwd(proj, dqn, dkvn, dkpe_all, tabs, qg, kvg, T):
    _, _, cos64, sin64 = tabs

    def body(cq_ref, ckv_ref, dq_ref, dkv_ref, dk_ref, c2_ref, s2_ref, qg_ref, kvg_ref, ocq_ref, ockv_ref, okpe_ref,
             dqg_ref, dkvg_ref):
        _, vjp = jax.vjp(_rms, cq_ref[...].astype(F32), qg_ref[...])
        dx, dg = vjp(dq_ref[...])
        ocq_ref[...] = dx.astype(BF16)
        _accum(dqg_ref, dg)
        _, vjp = jax.vjp(_rms, ckv_ref[...].astype(F32), kvg_ref[...])
        dx, dg = vjp(dkv_ref[...])
        ockv_ref[...] = dx.astype(BF16)
        _accum(dkvg_ref, dg)
        g = dk_ref[:, 0:128]
        for h in range(1, HEADS):
            g = g + dk_ref[:, h * 128:(h + 1) * 128]
        lane = lax.broadcasted_iota(jnp.int32, g.shape, 1)
        g = jnp.where(lane < MLA_ROPE, g, 0.0)
        okpe_ref[...] = _rope64_t(g, c2_ref[...], s2_ref[...]).astype(BF16)

    ins = [(proj, 256, P_CQ // 256), (proj, 256, P_CKV // 256), (dqn, LORA, 0), (dkvn, LORA, 0), (dkpe_all, 1024, 0),
           (cos64, 128, 0), (sin64, 128, 0), (qg, None, None), (kvg, None, None)]
    return _rows(body, T, 256, ins, [(LORA, BF16), (LORA, BF16), (128, BF16)], "rms_bwd", accs=[(1, LORA), (1, LORA)])


def _gn_gate(y, rg, g):
    mu = jnp.mean(y, axis=-1, keepdims=True)
    var = jnp.mean(jnp.square(y - mu), axis=-1, keepdims=True)
    return _silu(rg) * ((y - mu) * lax.rsqrt(var + EPS) * g)


def _q_assemble_fwd(q, tabs, T):
    _, _, cos64, sin64 = tabs

    def body(q_ref, c_ref, s_ref, on_ref, op_ref):
        on_ref[...] = (q_ref[:, :1024] * Q_PRESCALE).astype(BF16)
        c, s = c_ref[...], s_ref[...]
        lane = lax.broadcasted_iota(jnp.int32, c.shape, 1)
        for j in range(HEADS // 2):
            r = _rope64(q_ref[:, 1024 + 128 * j:1024 + 128 * (j + 1)], c, s) * Q_PRESCALE
            op_ref[:, 256 * j:256 * j + 128] = jnp.where(lane < 64, r, 0.0).astype(BF16)
            op_ref[:, 256 * j + 128:256 * j + 256] = jnp.where(lane < 64, pltpu.roll(r, 64, 1), 0.0).astype(BF16)

    return _rows(body, T, 256, [(q, 1536, 0), (cos64, 128, 0), (sin64, 128, 0)], [(1024, BF16), (1024, BF16)], "q_assemble_fwd")


def _q_assemble_bwd(dqn, dqpe, tabs, T):
    _, _, cos64, sin64 = tabs

    def body(dn_ref, dp_ref, c_ref, s_ref, on_ref, op_ref):
        on_ref[...] = dn_ref[...].astype(BF16)
        c, s = c_ref[...], s_ref[...]
        lane = lax.broadcasted_iota(jnp.int32, c.shape, 1)
        for j in range(HEADS // 2):
            g = jnp.where(lane < 64, dp_ref[:, 256 * j:256 * j + 128], pltpu.roll(dp_ref[:, 256 * j + 128:256 * j + 256], 64, 1))
            op_ref[:, 128 * j:128 * (j + 1)] = _rope64_t(g, c, s).astype(BF16)

    return _rows(body, T, 256, [(dqn, 1024, 0), (dqpe, 1024, 0), (cos64, 128, 0), (sin64, 128, 0)],
                 [(1024, BF16), (512, BF16)], "q_assemble_bwd")


def _mix_fn(gr, gm, yr, ym):
    return _sigmoid(gr) * yr + _sigmoid(gm) * ym


def _mix_epilogue(proj, y_ret):
    def fn(r, ins, outs, first):
        gr_ref, gm_ref, yr_ref = ins
        ym_ref, mix_ref = outs
        ym_ref[...] = r
        mix_ref[...] = _mix_fn(gr_ref[...].astype(F32), gm_ref[...].astype(F32), yr_ref[...], r).astype(BF16)

    return (fn, [(proj, P_GR // D), (proj, P_GM // D), y_ret], [], (F32, BF16), ())


def _mix_bwd_epilogue(proj, y_ret, y_mla):
    def fn(r, ins, outs, first):
        gr_ref, gm_ref, yr_ref, ym_ref = ins
        _, vjp = jax.vjp(_mix_fn, gr_ref[...].astype(F32), gm_ref[...].astype(F32), yr_ref[...], ym_ref[...])
        for o_ref, d in zip(outs, vjp(r)):
            o_ref[...] = d.astype(BF16)

    return (fn, [(proj, P_GR // D), (proj, P_GM // D), y_ret, y_mla], [], (BF16,) * 4, ())


def _head_epilogue(h3, pp, tgt, g, b):
    def fn(r, ins, outs, first):
        h_ref, pp_ref, t_ref, g_ref, b_ref = ins
        dh_ref, dgl_ref, dpp_ref, dg_ref, db_ref, loss_ref = outs

        def f(h, gg, p, lg, lb):
            return _ln(ALPHA * h + _sigmoid(gg) * p, lg, lb)

        h4, vjp = jax.vjp(f, h_ref[...], r, pp_ref[...], g_ref[...], b_ref[...])
        err = h4 - t_ref[...]
        dh, dgl, dpp, dg, db = vjp(err * (1.0 / D))
        dh_ref[...] = dh
        dgl_ref[...] = dgl.astype(BF16)
        dpp_ref[...] = dpp.astype(BF16)
        _accum(dg_ref, dg, first)
        _accum(db_ref, db, first)
        part = 0.5 * jnp.sum(jnp.mean(jnp.square(err), axis=-1, keepdims=True), axis=0, keepdims=True)
        _accum(loss_ref, jnp.broadcast_to(part, loss_ref.shape), first)

    return (fn, [h3, pp, tgt], [g, b], (F32, BF16, BF16), (1, 1, 8))


def _delta_epilogue(o):
    def fn(r, ins, outs, first):
        (o_ref,) = ins
        db_ref, dl_ref = outs
        db_ref[...] = r.astype(BF16)
        for h in range(HEADS):
            sl = slice(h * MLA_DV, (h + 1) * MLA_DV)
            dl = jnp.sum(r[:, sl] * o_ref[:, sl].astype(F32), axis=-1, keepdims=True)
            dl_ref[:, sl] = jnp.broadcast_to(dl, (r.shape[0], MLA_DV))

    return (fn, [o], [], (BF16, F32), ())


def _ret_consts():
    L = RET_L
    lg = jnp.log(1.0 - 2.0 ** (-5.0 - jnp.arange(HEADS, dtype=F32)))[:, None, None]
    idx = jnp.arange(L, dtype=F32)
    ch = jnp.arange(L) // CHUNK
    dist = idx[:, None] - idx[None, :]
    same = (ch[:, None] == ch[None, :])[None]
    earlier = (ch[None, :] < ch[:, None])[None]
    dm = jnp.where(same, jnp.exp(lg * jnp.abs(dist)[None]), jnp.where(earlier, jnp.exp(lg * dist[None]), 0.0))
    xi = jnp.broadcast_to(jnp.exp(lg * (idx + 1.0)[None, :, None]), (HEADS, L, 128))
    zeta = jnp.broadcast_to(jnp.exp(lg * (L - 1.0 - idx)[None, :, None]), (HEADS, L, 128))
    gl = jnp.broadcast_to(jnp.exp(lg * float(L)), (HEADS, 8, 128))
    return dm.astype(F32), xi.astype(F32), zeta.astype(F32), gl.astype(F32)


def _whole(arr):
    return pl.BlockSpec(arr.shape, lambda n, _nd=arr.ndim: (0,) * _nd)


def _ret_fwd(q, k, v, proj, gn_g, consts, T):
    dm, xi, zeta, gl = consts
    L = RET_L
    n_sc = T // L

    def body(q_ref, k_ref, v_ref, rg_ref, g_ref, dm_ref, xi_ref, ze_ref, gl_ref, y_ref, yr_ref, s_ref, st_ref):
        @pl.when(pl.program_id(0) == 0)
        def _():
            st_ref[...] = jnp.zeros_like(st_ref)

        for h in range(HEADS):
            ks, vs = slice(h * RET_DK, (h + 1) * RET_DK), slice(h * RET_DV, (h + 1) * RET_DV)
            qq, kk, vv = q_ref[:, ks], k_ref[:, ks], v_ref[:, vs]
            st = st_ref[h]
            s_ref[h, 0] = st
            p = (_dot(qq, kk, 1, 1) * dm_ref[h]).astype(BF16)
            cross = _dot(qq, st.astype(BF16), 1, 0)
            xi_c = jnp.concatenate([xi_ref[h], xi_ref[h]], axis=1)
            y = _dot(p, vv, 1, 0) + cross * xi_c
            y_ref[:, vs] = y
            yr_ref[:, vs] = _gn_gate(y, rg_ref[:, vs].astype(F32), g_ref[:, vs]).astype(BF16)
            kz = (kk.astype(F32) * ze_ref[h]).astype(BF16)
            gl2 = jnp.concatenate([gl_ref[h, 0:1, :], gl_ref[h, 0:1, :]], axis=1)
            st_ref[h] = st * gl2 + _dot(kz, vv, 0, 0)

    return _pcall(
        body, grid=(n_sc,),
        in_specs=[pl.BlockSpec((L, 1024), lambda n: (n, 0)), pl.BlockSpec((L, 1024), lambda n: (n, 0)),
                  pl.BlockSpec((L, 2048), lambda n: (n, P_RV // 2048)), pl.BlockSpec((L, 2048), lambda n: (n, P_RG // 2048)),
                  _whole(gn_g), _whole(dm), _whole(xi), _whole(zeta), _whole(gl)],
        out_specs=[pl.BlockSpec((L, 2048), lambda n: (n, 0)), pl.BlockSpec((L, 2048), lambda n: (n, 0)),
                   pl.BlockSpec((HEADS, 1, 128, 256), lambda n: (0, n, 0, 0))],
        out_shape=[jax.ShapeDtypeStruct((T, HEADS * RET_DV), F32), jax.ShapeDtypeStruct((T, HEADS * RET_DV), BF16),
                   jax.ShapeDtypeStruct((HEADS, n_sc, 128, 256), F32)],
        scratch_shapes=[pltpu.VMEM((HEADS, 128, 256), F32)], name="ret_fwd",
        compiler_params=pltpu.CompilerParams(dimension_semantics=("arbitrary",)),
    )(q, k, v, proj, gn_g, dm, xi, zeta, gl)


def _ret_bwd(q, k, v, y, proj, gn_g, states, dyr, consts, tabs, T):
    dm, xi, zeta, gl = consts
    cos128, sin128, _, _ = tabs
    L = RET_L
    n_sc = T // L

    def body(q_ref, k_ref, v_ref, y_ref, rg_ref, g_ref, d_ref, s_ref, dm_ref, xi_ref, ze_ref, gl_ref, c_ref, sn_ref,
             dq_ref, dk_ref, dv_ref, drg_ref, dg_ref, gs_ref):
        @pl.when(pl.program_id(0) == 0)
        def _():
            gs_ref[...] = jnp.zeros_like(gs_ref)

        c, sn = c_ref[...], sn_ref[...]
        dgs = []
        for h in range(HEADS):
            ks, vs = slice(h * RET_DK, (h + 1) * RET_DK), slice(h * RET_DV, (h + 1) * RET_DV)
            _, vjp = jax.vjp(_gn_gate, y_ref[:, vs], rg_ref[:, vs].astype(F32), g_ref[:, vs])
            dy, drg, dg = vjp(d_ref[:, vs])
            drg_ref[:, vs] = drg.astype(BF16)
            dgs.append(dg)
            qq, kk, vv, dyy = q_ref[:, ks], k_ref[:, ks], v_ref[:, vs], dy.astype(BF16)
            dmm = dm_ref[h]
            gb = gs_ref[h].astype(BF16)
            sb = s_ref[h, 0].astype(BF16)
            xi_c = jnp.concatenate([xi_ref[h], xi_ref[h]], axis=1)
            pb = (_dot(qq, kk, 1, 1) * dmm).astype(BF16)
            kz = (kk.astype(F32) * ze_ref[h]).astype(BF16)
            dv_ref[:, vs] = (_dot(pb, dyy, 0, 0) + _dot(kz, gb, 1, 0)).astype(BF16)
            da = (_dot(dyy, vv, 1, 1) * dmm).astype(BF16)
            dyx = (dyy.astype(F32) * xi_c).astype(BF16)
            dq = _dot(da, kk, 1, 0) + _dot(dyx, sb, 1, 1)
            dk = _dot(da, qq, 0, 0) + _dot(vv, gb, 1, 1) * ze_ref[h]
            dq_ref[:, ks] = _rope128_t(dq, c, sn).astype(BF16)
            dk_ref[:, ks] = (_rope128_t(dk, c, sn) * RET_SCALE).astype(BF16)
            gl2 = jnp.concatenate([gl_ref[h, 0:1, :], gl_ref[h, 0:1, :]], axis=1)
            gs_ref[h] = gs_ref[h] * gl2 + _dot(qq, dyx, 0, 0)
        _accum(dg_ref, jnp.concatenate(dgs, axis=1))

    rev = lambda n: n_sc - 1 - n
    return _pcall(
        body, grid=(n_sc,),
        in_specs=[pl.BlockSpec((L, 1024), lambda n: (rev(n), 0)), pl.BlockSpec((L, 1024), lambda n: (rev(n), 0)),
                  pl.BlockSpec((L, 2048), lambda n: (rev(n), P_RV // 2048)), pl.BlockSpec((L, 2048), lambda n: (rev(n), 0)),
                  pl.BlockSpec((L, 2048), lambda n: (rev(n), P_RG // 2048)), _whole(gn_g),
                  pl.BlockSpec((L, 2048), lambda n: (rev(n), 0)),
                  pl.BlockSpec((HEADS, 1, 128, 256), lambda n: (0, rev(n), 0, 0)),
                  _whole(dm), _whole(xi), _whole(zeta), _whole(gl),
                  pl.BlockSpec((L, 128), lambda n: (rev(n), 0)), pl.BlockSpec((L, 128), lambda n: (rev(n), 0))],
        out_specs=[pl.BlockSpec((L, 1024), lambda n: (rev(n), 0)), pl.BlockSpec((L, 1024), lambda n: (rev(n), 0)),
                   pl.BlockSpec((L, 2048), lambda n: (rev(n), 0)), pl.BlockSpec((L, 2048), lambda n: (rev(n), 0)),
                   pl.BlockSpec((1, 2048), lambda n: (0, 0))],
        out_shape=[jax.ShapeDtypeStruct((T, 1024), BF16), jax.ShapeDtypeStruct((T, 1024), BF16), jax.ShapeDtypeStruct((T, 2048), BF16),
                   jax.ShapeDtypeStruct((T, 2048), BF16), jax.ShapeDtypeStruct((1, 2048), F32)],
        scratch_shapes=[pltpu.VMEM((HEADS, 128, 256), F32)], name="ret_bwd",
        compiler_params=pltpu.CompilerParams(dimension_semantics=("arbitrary",)),
    )(q, k, v, y, proj, gn_g, dyr, states, dm, xi, zeta, gl, cos128, sin128)


def _diag_mask(nrows, ncols, row0):
    row = row0 + lax.broadcasted_iota(jnp.int32, (nrows, ncols), 0)
    col = lax.broadcasted_iota(jnp.int32, (nrows, ncols), 1)
    return lax.shift_right_logical(col, 6) <= lax.shift_right_logical(row, 6)


def _tri_steps(nb, by_key):
    if by_key:
        pairs = [(i, j) for j in range(nb) for i in range(j, nb)]
    else:
        pairs = [(i, j) for i in range(nb) for j in range(i + 1)]
    return jnp.array([a for a, _ in pairs], jnp.int32), jnp.array([b for _, b in pairs], jnp.int32)


def _attn_fwd(qn, qpe, kv, kpe, T):
    t = min(ATT_TF, T)
    nb = T // t
    ii, jj = _tri_steps(nb, by_key=False)

    hp = ATT_HP
    w = 128 * hp

    def body(ii_ref, jj_ref, qn_ref, qp_ref, kn_ref, kp_ref, v_ref, o_ref, lse_ref, m_sc, l_sc, acc_sc):
        st = pl.program_id(1)
        i, j = ii_ref[st], jj_ref[st]

        @pl.when(j == 0)
        def _():
            m_sc[...] = jnp.full_like(m_sc, NEG)
            l_sc[...] = jnp.zeros_like(l_sc)
            acc_sc[...] = jnp.zeros_like(acc_sc)

        def update(diag):
            kp = kp_ref[...]
            for hh in range(hp):
                sl = slice(128 * hh, 128 * (hh + 1))
                q = jnp.concatenate([qn_ref[:, sl], qp_ref[:, sl]], axis=1)
                k = jnp.concatenate([kn_ref[:, sl], kp], axis=1)
                s = _dot(q, k, 1, 1)
                if diag:
                    s = jnp.where(_diag_mask(t, t, 0), s, NEG)
                m_prev = m_sc[:, sl]
                m_new = jnp.maximum(m_prev, jnp.max(s, axis=1, keepdims=True))
                a = jnp.exp2(m_prev - m_new)
                p = jnp.exp2(s - m_new[:, 0:1])
                l_sc[:, sl] = a * l_sc[:, sl] + jnp.sum(p, axis=1, keepdims=True)
                acc_sc[:, sl] = a * acc_sc[:, sl] + _dot(p.astype(BF16), v_ref[:, sl], 1, 0)
                m_sc[:, sl] = m_new

        @pl.when(j < i)
        def _():
            update(False)

        @pl.when(j == i)
        def _():
            update(True)
            o_ref[...] = (acc_sc[...] / l_sc[...]).astype(BF16)
            lse_ref[...] = m_sc[...] + jnp.log2(l_sc[...])

    qs = pl.BlockSpec((t, w), lambda h, s, ii, jj: (ii[s], h))
    grid_spec = pltpu.PrefetchScalarGridSpec(
        num_scalar_prefetch=2, grid=(HEADS // hp, int(ii.shape[0])),
        in_specs=[qs, qs, pl.BlockSpec((t, w), lambda h, s, ii, jj: (jj[s], h)), pl.BlockSpec((t, 128), lambda h, s, ii, jj: (jj[s], 0)),
                  pl.BlockSpec((t, w), lambda h, s, ii, jj: (jj[s], HEADS // hp + h))],
        out_specs=[qs, qs],
        scratch_shapes=[pltpu.VMEM((t, w), F32), pltpu.VMEM((t, w), F32), pltpu.VMEM((t, w), F32)])
    return _pcall(
        body, grid_spec=grid_spec, out_shape=[jax.ShapeDtypeStruct((T, D), BF16), jax.ShapeDtypeStruct((T, D), F32)], name="attn_fwd",
        compiler_params=pltpu.CompilerParams(dimension_semantics=("arbitrary", "arbitrary")),
    )(ii, jj, qn, qpe, kv, kpe, kv)


def _attn_bwd(qn, qpe, kv, kpe, do, lse, delta, T):
    t = min(ATT_TB, T)
    nb = T // t
    ii, jj = _tri_steps(nb, by_key=True)

    def body(ii_ref, jj_ref, qn_ref, qp_ref, kn_ref, kp_ref, v_ref, do_ref, lse_ref, dl_ref,
             dqn_ref, dqp_ref, dkn_ref, dkp_ref, dv_ref, dk_sc, dv_sc):
        st = pl.program_id(1)
        i, j = ii_ref[st], jj_ref[st]

        @pl.when(st == 0)
        def _():
            dqn_ref[...] = jnp.zeros_like(dqn_ref)
            dqp_ref[...] = jnp.zeros_like(dqp_ref)

        @pl.when(i == j)
        def _():
            dk_sc[...] = jnp.zeros_like(dk_sc)
            dv_sc[...] = jnp.zeros_like(dv_sc)

        def update(diag):
            q = jnp.concatenate([qn_ref[...], qp_ref[...]], axis=1)
            k = jnp.concatenate([kn_ref[...], kp_ref[...]], axis=1)
            dob = do_ref[...]
            s = _dot(q, k, 1, 1)
            if diag:
                s = jnp.where(_diag_mask(t, t, 0), s, NEG)
            p = jnp.exp2(s - lse_ref[:, 0:1])
            dv_sc[...] += _dot(p.astype(BF16), dob, 0, 0)
            dp = _dot(dob, v_ref[...], 1, 1)
            ds = (p * (dp - dl_ref[:, 0:1])).astype(BF16)
            dk_sc[...] += _dot(ds, q, 0, 0)
            dq = _dot(ds, k, 1, 0) * MLA_SCALE
            rows = pl.ds(pl.multiple_of(i * t, t), t)
            dqn_ref[rows, :] += dq[:, :128]
            dqp_ref[rows, :] += dq[:, 128:]

        @pl.when(i > j)
        def _():
            update(False)

        @pl.when(i == j)
        def _():
            update(True)

        @pl.when(i == nb - 1)
        def _():
            dkn_ref[...] = (dk_sc[:, :128] * (1.0 / LOG2E)).astype(BF16)
            dkp_ref[...] = dk_sc[:, 128:] * (1.0 / LOG2E)
            dv_ref[...] = dv_sc[...].astype(BF16)

    qs = pl.BlockSpec((t, 128), lambda h, s, ii, jj: (ii[s], h))
    ks = pl.BlockSpec((t, 128), lambda h, s, ii, jj: (jj[s], h))
    hs = pl.BlockSpec((T, 128), lambda h, s, ii, jj: (0, h))
    grid_spec = pltpu.PrefetchScalarGridSpec(
        num_scalar_prefetch=2, grid=(HEADS, int(ii.shape[0])),
        in_specs=[qs, qs, ks, pl.BlockSpec((t, 128), lambda h, s, ii, jj: (jj[s], 0)),
                  pl.BlockSpec((t, 128), lambda h, s, ii, jj: (jj[s], HEADS + h)), qs, qs, qs],
        out_specs=[hs, hs, ks, ks, ks],
        scratch_shapes=[pltpu.VMEM((t, 256), F32), pltpu.VMEM((t, 128), F32)])
    return _pcall(
        body, grid_spec=grid_spec,
        out_shape=[jax.ShapeDtypeStruct((T, D), F32), jax.ShapeDtypeStruct((T, D), F32), jax.ShapeDtypeStruct((T, D), BF16),
                   jax.ShapeDtypeStruct((T, D), F32), jax.ShapeDtypeStruct((T, D), BF16)], name="attn_bwd",
        compiler_params=pltpu.CompilerParams(dimension_semantics=("arbitrary", "arbitrary")),
    )(ii, jj, qn, qpe, kv, kpe, kv, do, lse, delta)


def _mesh_pos():
    x, y, c = lax.axis_index("x"), lax.axis_index("y"), lax.axis_index("c")
    return x, y, c, 4 * x + 2 * y + c


def _peer(x, y, c, k):
    px, py, pc = (x + ((k >> 2) & 1)) % 2, (y + ((k >> 1) & 1)) % 2, (c + (k & 1)) % 2
    return (px, py, pc), 4 * px + 2 * py + pc


_ANY = pl.BlockSpec(memory_space=pl.ANY)


def _rcopy(src, dst, send_sems, recv_sems, k, to):
    return pltpu.make_async_remote_copy(src_ref=src, dst_ref=dst, send_sem=send_sems.at[k], recv_sem=recv_sems.at[k],
                                        device_id=to, device_id_type=pl.DeviceIdType.MESH)


def _all_gather(wsh, ssh):
    def body(w_ref, s_ref, wall_ref, sall_ref, send_sems, recv_sems, loc_sems):
        x, y, c, me = _mesh_pos()
        sib = (x, y, 1 - c)
        chips = [(1 - x, y), (x, 1 - y), (1 - x, 1 - y)]
        slot = lambda px, py, pc: 4 * px + 2 * py + pc
        loc = [pltpu.make_async_copy(s_ref, sall_ref.at[me], loc_sems.at[0])]
        for cp in loc:
            cp.start()
        sends, fwd_waits = [], []
        for n, (src, dst) in enumerate(((w_ref, wall_ref), (s_ref, sall_ref))):
            o = 7 * n
            sends.append(_rcopy(src, dst.at[me], send_sems, recv_sems, o, sib))
            for j, chip in enumerate(chips):
                sends.append(_rcopy(src, dst.at[me], send_sems, recv_sems, o + 1 + j, (*chip, c)))
        for cp in sends:
            cp.start()
        for n, (src, dst) in enumerate(((w_ref, wall_ref), (s_ref, sall_ref))):
            o = 7 * n
            for j, chip in enumerate(chips):
                got = dst.at[slot(*chip, c)]
                _rcopy(src, got, send_sems, recv_sems, o + 1 + j, sib).wait_recv()
                fw = _rcopy(got, got, send_sems, recv_sems, o + 4 + j, sib)
                fw.start()
                sends.append(fw)
            fwd_waits.append(_rcopy(src, dst.at[slot(x, y, 1 - c)], send_sems, recv_sems, o, sib))
            for j, chip in enumerate(chips):
                fwd_waits.append(_rcopy(src, dst.at[slot(*chip, 1 - c)], send_sems, recv_sems, o + 4 + j, sib))
        for cp in fwd_waits:
            cp.wait_recv()
        for cp in sends:
            cp.wait_send()
        for cp in loc:
            cp.wait()

    wall, sall = _pcall(
        body, in_specs=[_ANY, _ANY], out_specs=[_ANY, _ANY],
        out_shape=[jax.ShapeDtypeStruct((N_DEV,) + wsh.shape, wsh.dtype), jax.ShapeDtypeStruct((N_DEV,) + ssh.shape, ssh.dtype)],
        scratch_shapes=[pltpu.SemaphoreType.DMA((14,)), pltpu.SemaphoreType.DMA((14,)), pltpu.SemaphoreType.DMA((1,))],
        name="all_gather_weights",
    )(wsh, ssh)
    me = 4 * lax.axis_index("x") + 2 * lax.axis_index("y") + lax.axis_index("c")
    return lax.dynamic_update_index_in_dim(wall, wsh, me, axis=0), sall


_HBM = pl.BlockSpec(memory_space=pltpu.HBM)
_SEM = pl.BlockSpec(memory_space=pltpu.SEMAPHORE)
_EFFECT = pltpu.SideEffectType.DATAFLOW_SIDE_EFFECTING


def _other_chips(x, y):
    return [(1 - x, y), (x, 1 - y), (1 - x, 1 - y)]


def _gather_start(wsh, order_after):
    def body(w_ref, land_ref, dep_ref, send_sems, recv_sems, w_thru, land_thru, token):
        x, y, c, me = _mesh_pos()
        _rcopy(w_ref, land_ref.at[me], send_sems, recv_sems, 0, (x, y, 1 - c)).start()
        for j, chip in enumerate(_other_chips(x, y)):
            _rcopy(w_ref, land_ref.at[me], send_sems, recv_sems, 1 + j, (*chip, c)).start()
        token[...] = jnp.zeros_like(token)

    shape = (N_DEV,) + wsh.shape
    land = pltpu.with_memory_space_constraint(lax.empty(shape, wsh.dtype), pltpu.HBM)
    return _pcall(
        body, name="gather_start",
        out_shape=(pltpu.SemaphoreType.DMA((4,)), pltpu.SemaphoreType.DMA((4,)), pltpu.HBM(wsh.shape, wsh.dtype),
                   pltpu.HBM(shape, wsh.dtype), jax.ShapeDtypeStruct((8, 128), F32)),
        in_specs=(_HBM, _HBM, _ANY), out_specs=(_SEM, _SEM, _HBM, _HBM, pl.BlockSpec(memory_space=pltpu.VMEM)),
        input_output_aliases={0: 2, 1: 3}, compiler_params=pltpu.CompilerParams(has_side_effects=_EFFECT),
    )(pltpu.with_memory_space_constraint(wsh, pltpu.HBM), land, order_after)


def _gather_wait(send_sems, recv_sems, w_thru, land_thru, after):
    def body(w_ref, land_ref, send_sems, recv_sems, after_ref, w_out, land_out):
        x, y, c, _ = _mesh_pos()
        senders = [(x, y, 1 - c)] + [(*chip, c) for chip in _other_chips(x, y)]
        for k, (px, py, pc) in enumerate(senders):
            cp = _rcopy(w_ref, land_ref.at[4 * px + 2 * py + pc], send_sems, recv_sems, k, (px, py, pc))
            cp.wait_send()
            cp.wait_recv()

    return _pcall(
        body, name="gather_wait", out_shape=(pltpu.HBM(w_thru.shape, w_thru.dtype), pltpu.HBM(land_thru.shape, land_thru.dtype)),
        in_specs=(_HBM, _HBM, _SEM, _SEM, _ANY), out_specs=(_HBM, _HBM), input_output_aliases={0: 0, 1: 1},
        compiler_params=pltpu.CompilerParams(has_side_effects=_EFFECT),
    )(w_thru, land_thru, send_sems, recv_sems, after)


def _gather_finish(wsh, land):
    def body(land_ref, out_ref, send_sems, recv_sems):
        x, y, c, _ = _mesh_pos()
        sib = (x, y, 1 - c)
        sends, recvs = [], []
        for j, (px, py) in enumerate(_other_chips(x, y)):
            mine, theirs = 4 * px + 2 * py + c, 4 * px + 2 * py + (1 - c)
            sends.append(_rcopy(land_ref.at[mine], out_ref.at[mine], send_sems, recv_sems, j, sib))
            recvs.append(_rcopy(land_ref.at[theirs], out_ref.at[theirs], send_sems, recv_sems, j, sib))
        for cp in sends:
            cp.start()
        for cp in recvs:
            cp.wait_recv()
        for cp in sends:
            cp.wait_send()

    done = _pcall(
        body, in_specs=[_ANY], out_specs=_ANY, out_shape=jax.ShapeDtypeStruct(land.shape, land.dtype),
        input_output_aliases={0: 0},
        scratch_shapes=[pltpu.SemaphoreType.DMA((3,)), pltpu.SemaphoreType.DMA((3,))], name="gather_finish",
    )(land)
    me = 4 * lax.axis_index("x") + 2 * lax.axis_index("y") + lax.axis_index("c")
    return lax.dynamic_update_index_in_dim(done, wsh, me, axis=0)


_ROW_TILES = (400, 368, 352, 256, 128, 16, 8)


def _exchange_sibling(g4, name):
    def body(g_ref, r_ref, send_sems, recv_sems):
        x, y, c, _ = _mesh_pos()
        sib = (x, y, 1 - c)
        cps = [_rcopy(g_ref.at[q, 1 - c], r_ref.at[q], send_sems, recv_sems, q, sib) for q in range(4)]
        for cp in cps:
            cp.start()
        for cp in cps:
            cp.wait()

    n, _, R, C = g4.shape
    return _pcall(
        body, in_specs=[_ANY], out_specs=_ANY, out_shape=jax.ShapeDtypeStruct((n, R, C), g4.dtype),
        scratch_shapes=[pltpu.SemaphoreType.DMA((4,)), pltpu.SemaphoreType.DMA((4,))], name=name,
    )(g4)


def _sum_sibling(g4, rsib, cvec, name):
    n, _, R, C = g4.shape
    tr = _pick(R, _ROW_TILES)

    def body(c_ref, g_ref, r_ref, o_ref):
        o_ref[...] = (g_ref[...].astype(F32) + r_ref[...].astype(F32)).astype(o_ref.dtype)

    grid_spec = pltpu.PrefetchScalarGridSpec(
        num_scalar_prefetch=1, grid=(n, R // tr),
        in_specs=[pl.BlockSpec((None, None, tr, C), lambda q, i, cr: (q, cr[0], i, 0)), pl.BlockSpec((None, tr, C), lambda q, i, cr: (q, i, 0))],
        out_specs=pl.BlockSpec((None, tr, C), lambda q, i, cr: (q, i, 0)))
    return _pcall(body, grid_spec=grid_spec, out_shape=jax.ShapeDtypeStruct((n, R, C), g4.dtype), name=name)(cvec, g4, rsib)


def _chips_start(part):
    def body(p_ref, land_ref, send_sems, recv_sems, p_thru, land_thru, token):
        x, y, c, _ = _mesh_pos()
        myq = 2 * x + y
        for j, (px, py) in enumerate(_other_chips(x, y)):
            _rcopy(p_ref.at[2 * px + py], land_ref.at[myq], send_sems, recv_sems, j, (px, py, c)).start()
        token[...] = jnp.zeros_like(token)

    land = pltpu.with_memory_space_constraint(lax.empty(part.shape, part.dtype), pltpu.HBM)
    return _pcall(
        body, name="chips_start",
        out_shape=(pltpu.SemaphoreType.DMA((3,)), pltpu.SemaphoreType.DMA((3,)), pltpu.HBM(part.shape, part.dtype),
                   pltpu.HBM(part.shape, part.dtype), jax.ShapeDtypeStruct((8, 128), F32)),
        in_specs=(_HBM, _HBM), out_specs=(_SEM, _SEM, _HBM, _HBM, pl.BlockSpec(memory_space=pltpu.VMEM)),
        input_output_aliases={0: 2, 1: 3}, compiler_params=pltpu.CompilerParams(has_side_effects=_EFFECT),
    )(pltpu.with_memory_space_constraint(part, pltpu.HBM), land)


def _chips_wait(send_sems, recv_sems, p_thru, land_thru, after):
    def body(p_ref, land_ref, send_sems, recv_sems, after_ref, p_out, land_out):
        x, y, c, _ = _mesh_pos()
        for j, (px, py) in enumerate(_other_chips(x, y)):
            q = 2 * px + py
            cp = _rcopy(p_ref.at[q], land_ref.at[q], send_sems, recv_sems, j, (px, py, c))
            cp.wait_send()
            cp.wait_recv()

    return _pcall(
        body, name="chips_wait", out_shape=(pltpu.HBM(p_thru.shape, p_thru.dtype), pltpu.HBM(p_thru.shape, p_thru.dtype)),
        in_specs=(_HBM, _HBM, _SEM, _SEM, _ANY), out_specs=(_HBM, _HBM), input_output_aliases={0: 0, 1: 1},
        compiler_params=pltpu.CompilerParams(has_side_effects=_EFFECT),
    )(p_thru, land_thru, send_sems, recv_sems, after)


def _sum_chips(part, land, qvec, name):
    n, R, C = part.shape
    tr = _pick(R, _ROW_TILES)

    def body(q_ref, p_ref, *refs):
        o_ref = refs[n]
        acc = None
        for q in range(n):
            term = jnp.where(q_ref[0] == q, p_ref[...], refs[q][...]).astype(F32)
            acc = term if acc is None else acc + term
        o_ref[...] = acc

    def land_spec(q):
        return pl.BlockSpec((None, tr, C), lambda i, m: (jnp.where(m[0] == q, (q + 1) % n, q), i, 0))

    grid_spec = pltpu.PrefetchScalarGridSpec(
        num_scalar_prefetch=1, grid=(R // tr,),
        in_specs=[pl.BlockSpec((None, tr, C), lambda i, m: (m[0], i, 0))] + [land_spec(q) for q in range(n)],
        out_specs=pl.BlockSpec((tr, C), lambda i, m: (i, 0)))
    return _pcall(body, grid_spec=grid_spec, out_shape=jax.ShapeDtypeStruct((R, C), F32), name=name)(qvec, part, *([land] * n))


def _exchange_chips(part, gsmall):
    def body(p_ref, s_ref, recv_ref, srecv_ref, send_sems, recv_sems, loc_sems):
        x, y, c, me = _mesh_pos()
        myq = 2 * x + y
        loc = [pltpu.make_async_copy(s_ref, srecv_ref.at[me], loc_sems.at[0])]
        for cp in loc:
            cp.start()
        sends, recvs = [], []
        for j, (px, py) in enumerate([(1 - x, y), (x, 1 - y), (1 - x, 1 - y)]):
            q = 2 * px + py
            sends.append(_rcopy(p_ref.at[q], recv_ref.at[myq], send_sems, recv_sems, j, (px, py, c)))
            recvs.append(_rcopy(p_ref.at[q], recv_ref.at[q], send_sems, recv_sems, j, (px, py, c)))
        for k in range(1, N_DEV):
            to, pidx = _peer(x, y, c, k)
            sends.append(_rcopy(s_ref, srecv_ref.at[me], send_sems, recv_sems, 2 + k, to))
            recvs.append(_rcopy(s_ref, srecv_ref.at[pidx], send_sems, recv_sems, 2 + k, to))
        for cp in sends:
            cp.start()
        for cp in recvs:
            cp.wait_recv()
        for cp in sends:
            cp.wait_send()
        for cp in loc:
            cp.wait()

    return _pcall(
        body, in_specs=[_ANY, _ANY], out_specs=[_ANY, _ANY],
        out_shape=[jax.ShapeDtypeStruct(part.shape, part.dtype), jax.ShapeDtypeStruct((N_DEV,) + gsmall.shape, gsmall.dtype)],
        scratch_shapes=[pltpu.SemaphoreType.DMA((10,)), pltpu.SemaphoreType.DMA((10,)), pltpu.SemaphoreType.DMA((1,))],
        name="exchange_chips",
    )(part, gsmall)


def _sum_slots(recv, name):
    n, R, C = recv.shape
    tr = _pick(R, _ROW_TILES)

    def body(r_ref, o_ref):
        acc = r_ref[0].astype(F32)
        for s in range(1, n):
            acc = acc + r_ref[s].astype(F32)
        o_ref[...] = acc

    return _pcall(body, grid=(R // tr,), in_specs=[pl.BlockSpec((n, tr, C), lambda i: (0, i, 0))],
                  out_specs=pl.BlockSpec((tr, C), lambda i: (i, 0)), out_shape=jax.ShapeDtypeStruct((R, C), F32), name=name)(recv)


def _adamw(w, g, m, v, name):
    shape = w.shape
    w2, g2, m2, v2 = (a.reshape(-1, shape[-1]) for a in (w, g, m, v))
    R, C = w2.shape
    tr = _pick(R, (256, 128, 64, 32, 16, 8)) if R > 256 else R

    def body(w_ref, g_ref, m_ref, v_ref, d_ref, nm_ref, nv_ref):
        gg = g_ref[...]
        nm = ADAM_B1 * m_ref[...] + (1.0 - ADAM_B1) * gg
        nv = ADAM_B2 * v_ref[...] + (1.0 - ADAM_B2) * jnp.square(gg)
        m_hat = nm / (1.0 - ADAM_B1 ** ADAM_STEP)
        v_hat = nv / (1.0 - ADAM_B2 ** ADAM_STEP)
        d_ref[...] = -ADAM_LR * (m_hat / (jnp.sqrt(v_hat) + ADAM_EPS) + ADAM_WD * w_ref[...])
        nm_ref[...] = nm
        nv_ref[...] = nv

    spec = pl.BlockSpec((tr, C), lambda i: (i, 0))
    d, nm, nv = _pcall(body, grid=(R // tr,), in_specs=[spec] * 4, out_specs=[spec] * 3,
                       out_shape=[jax.ShapeDtypeStruct((R, C), F32)] * 3, name=name)(w2, g2, m2, v2)
    return d.reshape(shape), nm.reshape(shape), nv.reshape(shape)


def _to_rows(name, w):
    w = w[0]
    if name in ("ffn1_w_in", "ffn2_w_in", "w_in"):
        return w.T
    if name in ("w_uq", "w_ukv", "ple_w_proj"):
        return w.T.reshape(-1, 1024)
    return w


def _from_rows(name, g, shape):
    if name in ("ffn1_w_in", "ffn2_w_in", "w_in"):
        return g.T.reshape(shape)
    if name in ("w_uq", "w_ukv", "ple_w_proj"):
        return g.reshape(-1, shape[1]).T.reshape(shape)
    return g.reshape(shape)


def _unpack(wall, group):
    out, off = {}, 0
    for name, r in group:
        out[name] = wall[:, off:off + r, :].reshape(N_DEV * r, 1024)
        off += _pad16(r)
    return out


def _w_in_internal(wt):
    return jnp.concatenate([wt[0:6144], wt[6720:8768], wt[6144:6720], jnp.zeros((P_W - W_IN_COLS, 1024), wt.dtype)], axis=0)


def _w_in_external(d):
    return jnp.concatenate([d[0:6144], d[8192:8768], d[6144:8192]], axis=0)


def _rope_tables(positions):
    pos = positions[0].astype(F32)

    def cs(half):
        inv = ROPE_BASE ** (-jnp.arange(half, dtype=F32) / half)
        ang = pos[:, None] * inv
        return jnp.cos(ang), jnp.sin(ang)

    c, s = cs(64)
    c2, s2 = cs(32)
    return (jnp.concatenate([c, c], axis=1), jnp.concatenate([-s, s], axis=1),
            jnp.concatenate([c2, c2, c2, c2], axis=1), jnp.concatenate([-s2, s2, -s2, s2], axis=1))


def _local_step(x, p, positions, target, W, ln_g, ln_b, gn_g, qg, kvg, early=None, rest_weights=None, start_token=None):
    T = x.shape[0]
    tabs = _rope_tables(positions)
    rc = _ret_consts()
    lg = [ln_g[i:i + 1] for i in range(4)]
    lb = [ln_b[i:i + 1] for i in range(4)]
    pb = p.astype(BF16)

    hs1, gu1, *xb = _ffn_in(x, W["ffn1_w_in"], "ffn1_in", order_after=start_token)
    xb = xb[0] if xb else x
    f1, h1, h1b = _mm(hs1, W["ffn1_w_out"], name="ffn1_out", tm=LN_TM, epilogue=_ln_epilogue(x, 0.5, lg[0], lb[0]))
    if rest_weights is not None:
        W = {**W, **rest_weights(h1b)}
    w_in_t = _w_in_internal(W["w_in"])
    wuq = W["w_uq"].reshape(1536, LORA).reshape(HEADS, 192, LORA)
    wuq = jnp.concatenate([wuq[:, :128].reshape(1024, LORA), wuq[:, 128:].reshape(512, LORA)], axis=0)
    wukv = W["w_ukv"].reshape(2048, LORA).reshape(HEADS, 2, 128, LORA).transpose(1, 0, 2, 3).reshape(2048, LORA)
    wp_t = W["ple_w_proj"].reshape(1024, D_PLE)
    proj = _mm(h1b, w_in_t, tb=True, out_dtype=BF16, name="mixer_in")
    rq, rk, qn, kvn, kpe = _mixer_prep_fwd(proj, tabs, qg, kvg, T)
    y, yr, states = _ret_fwd(rq, rk, proj, proj, gn_g, rc, T)
    y_ret = _mm(yr, W["w_ret_o"], name="ret_o")
    q = _mm(qn, wuq, tb=True, name="mla_uq")
    kv = _mm(kvn, wukv, tb=True, out_dtype=BF16, name="mla_ukv")
    qnope, qpe = _q_assemble_fwd(q, tabs, T)
    o, lse = _attn_fwd(qnope, qpe, kv, kpe, T)
    y_mla, mix = _mm(o, W["w_mla_o"], name="mla_o", tm=LN_TM, epilogue=_mix_epilogue(proj, y_ret))
    mixed, h2, h2b = _mm(mix, W["w_out"], name="mixer_out", tm=LN_TM, epilogue=_ln_epilogue(h1, 1.0, lg[1], lb[1]))
    hs2, gu2 = _ffn_in(h2b, W["ffn2_w_in"], "ffn2_in")
    f2, h3, h3b = _mm(hs2, W["ffn2_w_out"], name="ffn2_out", tm=LN_TM, epilogue=_ln_epilogue(h2, 0.5, lg[2], lb[2]))
    pp = _mm(pb, wp_t, tb=True, name="ple_proj")

    G = {}
    dh3_a, dgl, dpp, dg3, db3, loss = _mm(h3b, W["ple_w_gate"], name="ple_gate", tm=LN_TM,
                                          epilogue=_head_epilogue(h3, pp, target, lg[3], lb[3]))
    G["ple_w_gate"] = _mm(h3b, dgl, ta=True, name="d_ple_gate")
    G["ple_w_proj"] = _mm(dpp, pb, ta=True, name="d_ple_proj")
    dh2_a, df2, dg2, db2 = _mm(dgl, W["ple_w_gate"], tb=True, add=dh3_a, name="dh3", tm=LN_TM,
                               epilogue=_ln_bwd_epilogue(h2, f2, 0.5, lg[2], lb[2]))
    G["ffn2_w_out"] = _mm(hs2, df2, ta=True, name="d_ffn2_out")
    da2 = _ffn_act_bwd(df2, W["ffn2_w_out"], gu2, "ffn2_act_bwd")
    G["ffn2_w_in"] = _mm(da2, h2b, ta=True, name="d_ffn2_in")
    dh1_a, dmixed, dg1, db1 = _mm(da2, W["ffn2_w_in"], add=dh2_a, name="dh2", tm=LN_TM,
                                  epilogue=_ln_bwd_epilogue(h1, mixed, 1.0, lg[1], lb[1]))
    G["w_out"] = _mm(mix, dmixed, ta=True, name="d_mixer_out")
    dgr, dgm, dy_ret, dy_mla = _mm(dmixed, W["w_out"], tb=True, name="dmix", tm=LN_TM,
                                   epilogue=_mix_bwd_epilogue(proj, y_ret, y_mla))
    G["w_mla_o"] = _mm(o, dy_mla, ta=True, name="d_mla_o")
    dob, delta = _mm(dy_mla, W["w_mla_o"], tb=True, name="do", tm=LN_TM, epilogue=_delta_epilogue(o))
    dqn_f, dqpe_f, dkn, dkpe_all, dv = _attn_bwd(qnope, qpe, kv, kpe, dob, lse, delta, T)
    dq_n, dq_r = _q_assemble_bwd(dqn_f, dqpe_f, tabs, T)
    g_uq = jnp.concatenate([_mm(dq_n, qn, ta=True, name="d_uq_nope"), _mm(dq_r, qn, ta=True, name="d_uq_rope")], axis=0)
    g_uq = jnp.concatenate([g_uq[:1024].reshape(HEADS, 128, LORA), g_uq[1024:].reshape(HEADS, 64, LORA)], axis=1)
    G["w_uq"] = g_uq.reshape(1536 * LORA // 1024, 1024)
    dqn = _mm(dq_r, wuq[1024:], add=_mm(dq_n, wuq[:1024], name="dqn_a"), name="dqn_b")
    g_ukv = jnp.stack([_mm(dkn, kvn, ta=True, name="d_ukv_k"), _mm(dv, kvn, ta=True, name="d_ukv_v")], axis=0)
    G["w_ukv"] = g_ukv.reshape(2, HEADS, 128, LORA).transpose(1, 0, 2, 3).reshape(2048 * LORA // 1024, 1024)
    dkvn = _mm(dv, wukv[1024:], add=_mm(dkn, wukv[:1024], name="dkvn_a"), name="dkvn_b")
    dcq, dckv, dkpe, dqg, dkvg = _rms_bwd(proj, dqn, dkvn, dkpe_all, tabs, qg, kvg, T)
    G["w_ret_o"] = _mm(yr, dy_ret, ta=True, name="d_ret_o")
    dyr = _mm(dy_ret, W["w_ret_o"], tb=True, name="dyr")
    drq, drk, drv, drg, dgn = _ret_bwd(rq, rk, proj, y, proj, gn_g, states, dyr, rc, tabs, T)
    dproj = jnp.concatenate([drq, drk, drv, drg, dgr, dgm, dcq, dckv, dkpe, jnp.zeros((T, P_W - P_KPE - 128), BF16)], axis=1)
    G["w_in"] = _w_in_external(_mm(dproj, h1b, ta=True, name="d_mixer_in"))
    lg0 = lg[0] if early is None else lg[0] + early(G)[0:1, 0:1]
    dx_a, df1, dg0, db0 = _mm(dproj, w_in_t, add=dh1_a, name="dh1", tm=LN_TM,
                              epilogue=_ln_bwd_epilogue(x, f1, 0.5, lg0, lb[0]))
    G["ffn1_w_out"] = _mm(hs1, df1, ta=True, name="d_ffn1_out")
    da1 = _ffn_act_bwd(df1, W["ffn1_w_out"], gu1, "ffn1_act_bwd")
    G["ffn1_w_in"] = _mm(da1, xb, ta=True, name="d_ffn1_in")
    grad_x = _mm(da1, W["ffn1_w_in"], add=dx_a, name="grad_x")

    small = dict(ln_g=jnp.concatenate([dg0, dg1, dg2, dg3], axis=0), ln_b=jnp.concatenate([db0, db1, db2, db3], axis=0),
                 ret_gn_g=dgn, q_norm_g=dqg, kv_norm_g=dkvg)
    return loss, grad_x, G, small


def kernel(x, p, positions, ln_g, ln_b, ffn1_w_in, ffn1_w_out, w_in, ret_gn_g, w_ret_o, q_norm_g, kv_norm_g, w_uq, w_ukv, w_mla_o, w_out, ffn2_w_in, ffn2_w_out, ple_w_gate, ple_w_proj, loss_target, m_ln_g, m_ln_b, m_ffn1_w_in, m_ffn1_w_out, m_w_in, m_ret_gn_g, m_w_ret_o, m_q_norm_g, m_kv_norm_g, m_w_uq, m_w_ukv, m_w_mla_o, m_w_out, m_ffn2_w_in, m_ffn2_w_out, m_ple_w_gate, m_ple_w_proj, v_ln_g, v_ln_b, v_ffn1_w_in, v_ffn1_w_out, v_w_in, v_ret_gn_g, v_w_ret_o, v_q_norm_g, v_kv_norm_g, v_w_uq, v_w_ukv, v_w_mla_o, v_w_out, v_ffn2_w_in, v_ffn2_w_out, v_ple_w_gate, v_ple_w_proj):
    names = ("ln_g", "ln_b", "ffn1_w_in", "ffn1_w_out", "w_in", "ret_gn_g", "w_ret_o", "q_norm_g", "kv_norm_g", "w_uq", "w_ukv",
             "w_mla_o", "w_out", "ffn2_w_in", "ffn2_w_out", "ple_w_gate", "ple_w_proj")
    ws = dict(zip(names, (ln_g, ln_b, ffn1_w_in, ffn1_w_out, w_in, ret_gn_g, w_ret_o, q_norm_g, kv_norm_g, w_uq, w_ukv, w_mla_o,
                          w_out, ffn2_w_in, ffn2_w_out, ple_w_gate, ple_w_proj)))
    ms = dict(zip(names, (m_ln_g, m_ln_b, m_ffn1_w_in, m_ffn1_w_out, m_w_in, m_ret_gn_g, m_w_ret_o, m_q_norm_g, m_kv_norm_g, m_w_uq,
                          m_w_ukv, m_w_mla_o, m_w_out, m_ffn2_w_in, m_ffn2_w_out, m_ple_w_gate, m_ple_w_proj)))
    vs = dict(zip(names, (v_ln_g, v_ln_b, v_ffn1_w_in, v_ffn1_w_out, v_w_in, v_ret_gn_g, v_w_ret_o, v_q_norm_g, v_kv_norm_g, v_w_uq,
                          v_w_ukv, v_w_mla_o, v_w_out, v_ffn2_w_in, v_ffn2_w_out, v_ple_w_gate, v_ple_w_proj)))

    parts = []
    for name, r in PACK:
        rows = _to_rows(name, ws[name])
        if _pad16(r) != r:
            rows = jnp.concatenate([rows, jnp.zeros((_pad16(r) - r, 1024), F32)], axis=0)
        parts.append(rows)
    wsh_first = jnp.concatenate(parts[:len(PACK_LATE)], axis=0).astype(BF16)
    wsh_rest = jnp.concatenate(parts[len(PACK_LATE):], axis=0).astype(BF16)
    ssh = jnp.concatenate([ln_g[0], ln_b[0]], axis=0)
    wall_first, sall = _all_gather(wsh_first, ssh)
    *gather_handles, start_token = _gather_start(wsh_rest, wall_first)
    W = _unpack(wall_first, PACK_LATE)
    ln_full = sall.reshape(N_DEV, 2, 4, 128).transpose(1, 2, 0, 3).reshape(2, 4, 1024)

    def rest_weights(after):
        w_thru, land = _gather_wait(*gather_handles, after)
        return _unpack(_gather_finish(w_thru, land), PACK_EARLY)

    cvec = lax.axis_index("c").astype(jnp.int32).reshape(1)
    qvec = (2 * lax.axis_index("x") + lax.axis_index("y")).astype(jnp.int32).reshape(1)

    def chip_partials(G, group, tag):
        gparts = []
        for name, r in group:
            g = G[name].reshape(N_DEV, r, 1024)
            if _pad16(r) != r:
                g = jnp.concatenate([g, jnp.zeros((N_DEV, _pad16(r) - r, 1024), g.dtype)], axis=1)
            gparts.append(g)
        gfull = jnp.concatenate(gparts, axis=1).astype(BF16)
        g4 = gfull.reshape(4, 2, gfull.shape[1], 1024)
        return _sum_sibling(g4, _exchange_sibling(g4, "exchange_sibling_" + tag), cvec, "sum_sibling_" + tag)

    in_flight = []

    def early(G):
        *handles, token = _chips_start(chip_partials(G, PACK_EARLY, "early"))
        in_flight.append(handles)
        return token

    loss_p, grad_x, G, small = _local_step(x[0], p[0, 0], positions, loss_target[0], W, ln_full[0], ln_full[1],
                                           ret_gn_g, q_norm_g, kv_norm_g, early=early, rest_weights=rest_weights,
                                           start_token=start_token)

    part_e, land_e = _chips_wait(*in_flight[0], grad_x)
    gsh_early = _sum_chips(part_e, land_e, qvec, "sum_grads_early")
    pad256 = lambda a: jnp.concatenate([a, jnp.zeros((1, 1024 - a.shape[1]), F32)], axis=1)
    gsmall = jnp.concatenate([small["ln_g"], small["ln_b"], small["ret_gn_g"].reshape(2, 1024), pad256(small["q_norm_g"]),
                              pad256(small["kv_norm_g"]), jnp.zeros((SMALL_ROWS - 12, 1024), F32)], axis=0)
    part_l = chip_partials(G, PACK_LATE, "late")
    recv, srecv = _exchange_chips(part_l, gsmall)
    gsh_late = _sum_chips(part_l, recv, qvec, "sum_grads_late")
    ssum = _sum_slots(srecv, "sum_small_grads")

    grads = {}
    for group, gsh in ((PACK_LATE, gsh_late), (PACK_EARLY, gsh_early)):
        off = 0
        for name, r in group:
            grads[name] = _from_rows(name, gsh[off:off + r], ws[name].shape)
            off += _pad16(r)
    me = 4 * lax.axis_index("x") + 2 * lax.axis_index("y") + lax.axis_index("c")
    grads["ln_g"] = lax.dynamic_slice(ssum[0:4], (0, me * 128), (4, 128)).reshape(1, 4, 128)
    grads["ln_b"] = lax.dynamic_slice(ssum[4:8], (0, me * 128), (4, 128)).reshape(1, 4, 128)
    grads["ret_gn_g"] = ssum[8:10].reshape(1, 2048)
    grads["q_norm_g"] = ssum[10:11, :256]
    grads["kv_norm_g"] = ssum[11:12, :256]

    delta, new_m, new_v = {}, {}, {}
    for name in names:
        delta[name], new_m[name], new_v[name] = _adamw(ws[name], grads[name], ms[name], vs[name], "adamw_" + name)

    loss = lax.psum(loss_p[0, 0], ("x", "y", "c"))
    return (loss, grad_x[None], *[grads[n] for n in names], *[delta[n] for n in names],
            *[new_m[n] for n in names], *[new_v[n] for n in names])
```

```python
import functools
import math

import jax
import jax.numpy as jnp
from jax import lax
from jax.experimental import pallas as pl
from jax.experimental.pallas import tpu as pltpu

F32 = jnp.float32
BF16 = jnp.bfloat16

N_DEV = 8
D = 1024
D_FF = 2816
D_PLE = 256
CHUNK = 64
HEADS = 8
RET_DK = 128
RET_DV = 256
MLA_NOPE = 128
MLA_ROPE = 64
MLA_DV = 128
LORA = 256
ROPE_BASE = 10000.0
EPS = 1e-5
ALPHA = 2.0 ** 0.25
RET_SCALE = RET_DK ** -0.5
MLA_SCALE = (MLA_NOPE + MLA_ROPE) ** -0.5
NEG = -1e30

ADAM_LR = 0.001
ADAM_B1 = 0.9
ADAM_B2 = 0.999
ADAM_EPS = 1e-08
ADAM_WD = 0.01
ADAM_STEP = 10

P_RQ, P_RK, P_RV, P_RG, P_GR, P_GM, P_CQ, P_CKV, P_KPE, P_W = 0, 1024, 2048, 4096, 6144, 7168, 8192, 8448, 8704, 8960
W_IN_COLS = 8768
RET_L = 256
ATT_TF = 1024
ATT_TB = 1024
ATT_HP = 4
LOG2E = math.log2(math.e)
Q_PRESCALE = MLA_SCALE * LOG2E

PACK = (("ffn1_w_in", 704), ("ffn1_w_out", 352), ("w_in", 1096), ("w_ret_o", 256), ("w_uq", 48), ("w_ukv", 64),
        ("w_mla_o", 128), ("w_out", 128), ("ffn2_w_in", 704), ("ffn2_w_out", 352), ("ple_w_gate", 128), ("ple_w_proj", 32))


def _pad16(r):
    return -(-r // 16) * 16


PACK_ROWS = sum(_pad16(r) for _, r in PACK)
PACK_LATE = PACK[:2]
PACK_EARLY = PACK[2:]
SMALL_ROWS = 16


def _pcall(body, **kw):
    return pl.pallas_call(body, **kw)


def _pick(dim, prefs):
    for p in prefs:
        if dim % p == 0:
            return p
    return dim


def _sigmoid(x):
    return 1.0 / (1.0 + jnp.exp(-x))


def _silu(x):
    return x * _sigmoid(x)


def _ln(r, g, b):
    mu = jnp.mean(r, axis=-1, keepdims=True)
    var = jnp.mean(jnp.square(r - mu), axis=-1, keepdims=True)
    return (r - mu) * lax.rsqrt(var + EPS) * g + b


def _rms(x, g):
    return x * lax.rsqrt(jnp.mean(jnp.square(x), axis=-1, keepdims=True) + EPS) * g


def _dot(a, b, ca, cb):
    return lax.dot_general(a, b, (((ca,), (cb,)), ((), ())), preferred_element_type=F32)


def _accum(ref, val, first=None):
    @pl.when(pl.program_id(0) == 0 if first is None else first)
    def _():
        ref[...] = jnp.zeros_like(ref)

    ref[...] += val


def _mm(a, b, *, ta=False, tb=False, add=None, out_dtype=None, name, tm=None, tn=None, tk=None, epilogue=None):
    parts = a.shape[0] if a.ndim == 3 else 1
    ar, ac = a.shape[-2], a.shape[-1]
    out_dtype = out_dtype or (BF16 if ta else F32)
    if ta:
        K, M = ar, ac * parts
    else:
        M, K = ar, ac * parts
    if tb:
        N, K2 = b.shape
    else:
        K2, N = b.shape
    assert K == K2, (a.shape, b.shape, ta, tb)
    big = (1024, 1408, 1280, 768, 512, 256, 128)
    tm = tm or _pick(ac if (ta and parts > 1) else M, big)
    tn = tn or (N if N <= 1024 else _pick(N, big))
    kdim = ac if (not ta and parts > 1) else K
    tk = tk or (kdim if kdim <= 2816 and parts == 1 else
                _pick(kdim, (2048, 1408, 1280, 1024, 512) if tm <= 1024 else (1024, 1408, 1280, 512)))
    nk = K // tk
    grid = (M // tm, N // tn, nk)
    if parts > 1 and ta:
        per = ac // tm
        a_spec = pl.BlockSpec((None, tk, tm), lambda i, j, k: (i // per, k, i % per))
    elif parts > 1:
        per = ac // tk
        a_spec = pl.BlockSpec((None, tm, tk), lambda i, j, k: (k // per, i, k % per))
    else:
        a_spec = pl.BlockSpec((tk, tm), lambda i, j, k: (k, i)) if ta else pl.BlockSpec((tm, tk), lambda i, j, k: (i, k))
    b_spec = pl.BlockSpec((tn, tk), lambda i, j, k: (j, k)) if tb else pl.BlockSpec((tk, tn), lambda i, j, k: (k, j))
    o_spec = pl.BlockSpec((tm, tn), lambda i, j, k: (i, j))
    ca, cb = (0 if ta else 1), (1 if tb else 0)
    has_add = add is not None
    n_in = 2 + int(has_add)
    if epilogue is not None:
        assert tn == N and not ta
        ep_fn, ep_rows, ep_whole, ep_outs, ep_accs = epilogue
        n_ep_in = len(ep_rows) + len(ep_whole)
        n_out = len(ep_outs) + len(ep_accs)
    else:
        n_ep_in, n_out = 0, 1

    def body(*refs):
        a_ref, b_ref = refs[0], refs[1]
        add_ref = refs[2] if has_add else None
        o_ref = refs[n_in + n_ep_in]
        first_row_tile = pl.program_id(0) == 0

        def finish(r):
            if has_add:
                r = r + add_ref[...].astype(F32)
            if epilogue is not None:
                ep_fn(r, refs[n_in:n_in + n_ep_in], refs[n_in + n_ep_in:n_in + n_ep_in + n_out], first_row_tile)
            else:
                o_ref[...] = r.astype(out_dtype)

        if nk == 1:
            finish(_dot(a_ref[...], b_ref[...], ca, cb))
            return
        acc_ref = refs[-1]
        k = pl.program_id(2)

        @pl.when(k == 0)
        def _():
            acc_ref[...] = jnp.zeros_like(acc_ref)

        acc_ref[...] += _dot(a_ref[...], b_ref[...], ca, cb)

        @pl.when(k == nk - 1)
        def _():
            finish(acc_ref[...])

    in_specs = [a_spec, b_spec] + ([o_spec] if has_add else [])
    args = (a, b) + ((add,) if has_add else ())
    out_specs, out_shape = o_spec, jax.ShapeDtypeStruct((M, N), out_dtype)
    if epilogue is not None:
        for row_in in ep_rows:
            arr, col_block = row_in if isinstance(row_in, tuple) else (row_in, 0)
            in_specs.append(pl.BlockSpec((tm, N), lambda i, j, k, _c=col_block: (i, _c)))
            args += (arr,)
        in_specs += [pl.BlockSpec(w.shape, lambda i, j, k, _n=w.ndim: (0,) * _n) for w in ep_whole]
        args += tuple(ep_whole)
        out_specs = [o_spec] * len(ep_outs) + [pl.BlockSpec((r, N), lambda i, j, k: (0, 0)) for r in ep_accs]
        out_shape = [jax.ShapeDtypeStruct((M, N), dt) for dt in ep_outs] + [jax.ShapeDtypeStruct((r, N), F32) for r in ep_accs]
    return _pcall(
        body, grid=grid, in_specs=in_specs, out_specs=out_specs, out_shape=out_shape,
        scratch_shapes=[pltpu.VMEM((tm, tn), F32)] if nk > 1 else [], name=name,
        compiler_params=pltpu.CompilerParams(dimension_semantics=("arbitrary" if epilogue is not None else "parallel", "parallel", "arbitrary")),
    )(*args)


def _ln_epilogue(res, c, g, b):
    def fn(r, ins, outs, first):
        res_ref, g_ref, b_ref = ins
        f_ref, h_ref, hb_ref = outs
        h = _ln(ALPHA * res_ref[...] + c * r, g_ref[...], b_ref[...])
        f_ref[...] = r
        h_ref[...] = h
        hb_ref[...] = h.astype(BF16)

    return (fn, [res], [g, b], (F32, F32, BF16), ())


def _ln_bwd_epilogue(res, f, c, g, b):
    def fn(r, ins, outs, first):
        res_ref, f_ref, g_ref, b_ref = ins
        dr_ref, df_ref, dg_ref, db_ref = outs
        _, vjp = jax.vjp(lambda rr, ff, gg, bb: _ln(ALPHA * rr + c * ff, gg, bb), res_ref[...], f_ref[...], g_ref[...], b_ref[...])
        dr, df, dg, db = vjp(r)
        dr_ref[...] = dr
        df_ref[...] = df.astype(BF16)
        _accum(dg_ref, dg, first)
        _accum(db_ref, db, first)

    return (fn, [res, f], [g, b], (F32, BF16), (1, 1))


def _rows(body, T, tm, ins, outs, name, accs=()):
    in_specs, args = [], []
    for arr, w, cb in ins:
        if w is None:
            in_specs.append(pl.BlockSpec(arr.shape, lambda i, _n=arr.ndim: (0,) * _n))
        else:
            in_specs.append(pl.BlockSpec((tm, w), lambda i, _cb=cb: (i, _cb)))
        args.append(arr)
    out_specs = [pl.BlockSpec((tm, w), lambda i: (i, 0)) for w, _ in outs]
    out_shape = [jax.ShapeDtypeStruct((T, w), dt) for w, dt in outs]
    for r, w in accs:
        out_specs.append(pl.BlockSpec((r, w), lambda i: (0, 0)))
        out_shape.append(jax.ShapeDtypeStruct((r, w), F32))
    return _pcall(
        body, grid=(T // tm,), in_specs=in_specs, out_specs=out_specs, out_shape=out_shape, name=name,
        compiler_params=pltpu.CompilerParams(dimension_semantics=("arbitrary",)),
    )(*args)


FFN_TN = 1408
FFN_TM = 512
LN_TM = 512


def _ffn_in(x, wt, name, order_after=None):
    T = x.shape[0]
    tm, tn = min(FFN_TM, T), FFN_TN
    nj = D_FF // tn
    emit_xb = x.dtype != BF16
    n_in = 3 + int(order_after is not None)

    def body(*refs):
        x_ref, wg_ref, wu_ref = refs[:3]
        hs_ref, gu_ref = refs[n_in], refs[n_in + 1]
        xv = x_ref[...].astype(BF16)
        g = _dot(xv, wg_ref[...], 1, 1)
        u = _dot(xv, wu_ref[...], 1, 1)
        hs_ref[...] = (_silu(g) * u).astype(BF16)
        gu_ref[0] = g.astype(BF16)
        gu_ref[1] = u.astype(BF16)
        if emit_xb:
            refs[n_in + 2][...] = xv

    in_specs = [pl.BlockSpec((tm, D), lambda i, j: (i, 0)), pl.BlockSpec((tn, D), lambda i, j: (j, 0)),
                pl.BlockSpec((tn, D), lambda i, j: (j + nj, 0))]
    args = [x, wt, wt]
    if order_after is not None:
        in_specs.append(pl.BlockSpec(order_after.shape, lambda i, j: (0, 0)))
        args.append(order_after)
    out_specs = [pl.BlockSpec((tm, tn), lambda i, j: (i, j)), pl.BlockSpec((2, tm, tn), lambda i, j: (0, i, j))]
    out_shape = [jax.ShapeDtypeStruct((T, D_FF), BF16), jax.ShapeDtypeStruct((2, T, D_FF), BF16)]
    if emit_xb:
        out_specs.append(pl.BlockSpec((tm, D), lambda i, j: (i, 0)))
        out_shape.append(jax.ShapeDtypeStruct((T, D), BF16))
    return _pcall(
        body, grid=(T // tm, nj), in_specs=in_specs, out_specs=out_specs, out_shape=out_shape, name=name,
        compiler_params=pltpu.CompilerParams(dimension_semantics=("parallel", "arbitrary")),
    )(*args)


def _ffn_act_bwd(df, wo, gu, name):
    T = df.shape[0]
    tm, tn = min(FFN_TM, T), FFN_TN

    def body(d_ref, w_ref, gu_ref, o_ref):
        dhs = _dot(d_ref[...], w_ref[...], 1, 1)
        g, u = gu_ref[0].astype(F32), gu_ref[1].astype(F32)
        sig = _sigmoid(g)
        act = g * sig
        o_ref[0] = (dhs * u * (sig + act * (1.0 - sig))).astype(BF16)
        o_ref[1] = (dhs * act).astype(BF16)

    return _pcall(
        body, grid=(T // tm, D_FF // tn),
        in_specs=[pl.BlockSpec((tm, D), lambda i, j: (i, 0)), pl.BlockSpec((tn, D), lambda i, j: (j, 0)),
                  pl.BlockSpec((2, tm, tn), lambda i, j: (0, i, j))],
        out_specs=pl.BlockSpec((2, tm, tn), lambda i, j: (0, i, j)),
        out_shape=jax.ShapeDtypeStruct((2, T, D_FF), BF16), name=name,
        compiler_params=pltpu.CompilerParams(dimension_semantics=("parallel", "parallel")),
    )(df, wo, gu)


def _rope128(t, cos, sin_s):
    return t * cos + pltpu.roll(t, 64, 1) * sin_s


def _rope128_t(g, cos, sin_s):
    return g * cos - pltpu.roll(g, 64, 1) * sin_s


def _partner32(t):
    lane = lax.broadcasted_iota(jnp.int32, t.shape, 1)
    return jnp.where((lane & 32) == 0, pltpu.roll(t, 96, 1), pltpu.roll(t, 32, 1))


def _rope64(t, cos, sin_s):
    return t * cos + _partner32(t) * sin_s


def _rope64_t(g, cos, sin_s):
    return g * cos - _partner32(g) * sin_s


def _mixer_prep_fwd(proj, tabs, qg, kvg, T):
    cos128, sin128, cos64, sin64 = tabs

    def body(rq_ref, rk_ref, cq_ref, ckv_ref, kpe_ref, c1_ref, s1_ref, c2_ref, s2_ref, qg_ref, kvg_ref,
             oq_ref, ok_ref, oqn_ref, okvn_ref, okpe_ref):
        c1, s1 = c1_ref[...], s1_ref[...]
        for h in range(HEADS):
            sl = slice(h * RET_DK, (h + 1) * RET_DK)
            oq_ref[:, sl] = _rope128(rq_ref[:, sl].astype(F32), c1, s1).astype(BF16)
            ok_ref[:, sl] = (_rope128(rk_ref[:, sl].astype(F32), c1, s1) * RET_SCALE).astype(BF16)
        oqn_ref[...] = _rms(cq_ref[...].astype(F32), qg_ref[...]).astype(BF16)
        okvn_ref[...] = _rms(ckv_ref[...].astype(F32), kvg_ref[...]).astype(BF16)
        okpe_ref[...] = _rope64(kpe_ref[...].astype(F32), c2_ref[...], s2_ref[...]).astype(BF16)

    ins = [(proj, 1024, 0), (proj, 1024, 1), (proj, 256, P_CQ // 256), (proj, 256, P_CKV // 256),
           (proj, 128, P_KPE // 128), (cos128, 128, 0), (sin128, 128, 0), (cos64, 128, 0), (sin64, 128, 0),
           (qg, None, None), (kvg, None, None)]
    outs = [(1024, BF16), (1024, BF16), (LORA, BF16), (LORA, BF16), (128, BF16)]
    return _rows(body, T, 256, ins, outs, "mixer_prep_fwd")


def _rms_bwd(proj, dqn, dkvn, dkpe_all, tabs, qg, kvg, T):
    _, _, cos64, sin64 = tabs

    def body(cq_ref, ckv_ref, dq_ref, dkv_ref, dk_ref, c2_ref, s2_ref, qg_ref, kvg_ref, ocq_ref, ockv_ref, okpe_ref,
             dqg_ref, dkvg_ref):
        _, vjp = jax.vjp(_rms, cq_ref[...].astype(F32), qg_ref[...])
        dx, dg = vjp(dq_ref[...])
        ocq_ref[...] = dx.astype(BF16)
        _accum(dqg_ref, dg)
        _, vjp = jax.vjp(_rms, ckv_ref[...].astype(F32), kvg_ref[...])
        dx, dg = vjp(dkv_ref[...])
        ockv_ref[...] = dx.astype(BF16)
        _accum(dkvg_ref, dg)
        g = dk_ref[:, 0:128]
        for h in range(1, HEADS):
            g = g + dk_ref[:, h * 128:(h + 1) * 128]
        lane = lax.broadcasted_iota(jnp.int32, g.shape, 1)
        g = jnp.where(lane < MLA_ROPE, g, 0.0)
        okpe_ref[...] = _rope64_t(g, c2_ref[...], s2_ref[...]).astype(BF16)

    ins = [(proj, 256, P_CQ // 256), (proj, 256, P_CKV // 256), (dqn, LORA, 0), (dkvn, LORA, 0), (dkpe_all, 1024, 0),
           (cos64, 128, 0), (sin64, 128, 0), (qg, None, None), (kvg, None, None)]
    return _rows(body, T, 256, ins, [(LORA, BF16), (LORA, BF16), (128, BF16)], "rms_bwd", accs=[(1, LORA), (1, LORA)])


def _gn_gate(y, rg, g):
    mu = jnp.mean(y, axis=-1, keepdims=True)
    var = jnp.mean(jnp.square(y - mu), axis=-1, keepdims=True)
    return _silu(rg) * ((y - mu) * lax.rsqrt(var + EPS) * g)


def _q_assemble_fwd(q, tabs, T):
    _, _, cos64, sin64 = tabs

    def body(q_ref, c_ref, s_ref, on_ref, op_ref):
        on_ref[...] = (q_ref[:, :1024] * Q_PRESCALE).astype(BF16)
        c, s = c_ref[...], s_ref[...]
        lane = lax.broadcasted_iota(jnp.int32, c.shape, 1)
        for j in range(HEADS // 2):
            r = _rope64(q_ref[:, 1024 + 128 * j:1024 + 128 * (j + 1)], c, s) * Q_PRESCALE
            op_ref[:, 256 * j:256 * j + 128] = jnp.where(lane < 64, r, 0.0).astype(BF16)
            op_ref[:, 256 * j + 128:256 * j + 256] = jnp.where(lane < 64, pltpu.roll(r, 64, 1), 0.0).astype(BF16)

    return _rows(body, T, 256, [(q, 1536, 0), (cos64, 128, 0), (sin64, 128, 0)], [(1024, BF16), (1024, BF16)], "q_assemble_fwd")


def _q_assemble_bwd(dqn, dqpe, tabs, T):
    _, _, cos64, sin64 = tabs

    def body(dn_ref, dp_ref, c_ref, s_ref, on_ref, op_ref):
        on_ref[...] = dn_ref[...].astype(BF16)
        c, s = c_ref[...], s_ref[...]
        lane = lax.broadcasted_iota(jnp.int32, c.shape, 1)
        for j in range(HEADS // 2):
            g = jnp.where(lane < 64, dp_ref[:, 256 * j:256 * j + 128], pltpu.roll(dp_ref[:, 256 * j + 128:256 * j + 256], 64, 1))
            op_ref[:, 128 * j:128 * (j + 1)] = _rope64_t(g, c, s).astype(BF16)

    return _rows(body, T, 256, [(dqn, 1024, 0), (dqpe, 1024, 0), (cos64, 128, 0), (sin64, 128, 0)],
                 [(1024, BF16), (512, BF16)], "q_assemble_bwd")


def _mix_fn(gr, gm, yr, ym):
    return _sigmoid(gr) * yr + _sigmoid(gm) * ym


def _mix_epilogue(proj, y_ret):
    def fn(r, ins, outs, first):
        gr_ref, gm_ref, yr_ref = ins
        ym_ref, mix_ref = outs
        ym_ref[...] = r
        mix_ref[...] = _mix_fn(gr_ref[...].astype(F32), gm_ref[...].astype(F32), yr_ref[...], r).astype(BF16)

    return (fn, [(proj, P_GR // D), (proj, P_GM // D), y_ret], [], (F32, BF16), ())


def _mix_bwd_epilogue(proj, y_ret, y_mla):
    def fn(r, ins, outs, first):
        gr_ref, gm_ref, yr_ref, ym_ref = ins
        _, vjp = jax.vjp(_mix_fn, gr_ref[...].astype(F32), gm_ref[...].astype(F32), yr_ref[...], ym_ref[...])
        for o_ref, d in zip(outs, vjp(r)):
            o_ref[...] = d.astype(BF16)

    return (fn, [(proj, P_GR // D), (proj, P_GM // D), y_ret, y_mla], [], (BF16,) * 4, ())


def _head_epilogue(h3, pp, tgt, g, b):
    def fn(r, ins, outs, first):
        h_ref, pp_ref, t_ref, g_ref, b_ref = ins
        dh_ref, dgl_ref, dpp_ref, dg_ref, db_ref, loss_ref = outs

        def f(h, gg, p, lg, lb):
            return _ln(ALPHA * h + _sigmoid(gg) * p, lg, lb)

        h4, vjp = jax.vjp(f, h_ref[...], r, pp_ref[...], g_ref[...], b_ref[...])
        err = h4 - t_ref[...]
        dh, dgl, dpp, dg, db = vjp(err * (1.0 / D))
        dh_ref[...] = dh
        dgl_ref[...] = dgl.astype(BF16)
        dpp_ref[...] = dpp.astype(BF16)
        _accum(dg_ref, dg, first)
        _accum(db_ref, db, first)
        part = 0.5 * jnp.sum(jnp.mean(jnp.square(err), axis=-1, keepdims=True), axis=0, keepdims=True)
        _accum(loss_ref, jnp.broadcast_to(part, loss_ref.shape), first)

    return (fn, [h3, pp, tgt], [g, b], (F32, BF16, BF16), (1, 1, 8))


def _delta_epilogue(o):
    def fn(r, ins, outs, first):
        (o_ref,) = ins
        db_ref, dl_ref = outs
        db_ref[...] = r.astype(BF16)
        for h in range(HEADS):
            sl = slice(h * MLA_DV, (h + 1) * MLA_DV)
            dl = jnp.sum(r[:, sl] * o_ref[:, sl].astype(F32), axis=-1, keepdims=True)
            dl_ref[:, sl] = jnp.broadcast_to(dl, (r.shape[0], MLA_DV))

    return (fn, [o], [], (BF16, F32), ())


def _ret_consts():
    L = RET_L
    lg = jnp.log(1.0 - 2.0 ** (-5.0 - jnp.arange(HEADS, dtype=F32)))[:, None, None]
    idx = jnp.arange(L, dtype=F32)
    ch = jnp.arange(L) // CHUNK
    dist = idx[:, None] - idx[None, :]
    same = (ch[:, None] == ch[None, :])[None]
    earlier = (ch[None, :] < ch[:, None])[None]
    dm = jnp.where(same, jnp.exp(lg * jnp.abs(dist)[None]), jnp.where(earlier, jnp.exp(lg * dist[None]), 0.0))
    xi = jnp.broadcast_to(jnp.exp(lg * (idx + 1.0)[None, :, None]), (HEADS, L, 128))
    zeta = jnp.broadcast_to(jnp.exp(lg * (L - 1.0 - idx)[None, :, None]), (HEADS, L, 128))
    gl = jnp.broadcast_to(jnp.exp(lg * float(L)), (HEADS, 8, 128))
    return dm.astype(F32), xi.astype(F32), zeta.astype(F32), gl.astype(F32)


def _whole(arr):
    return pl.BlockSpec(arr.shape, lambda n, _nd=arr.ndim: (0,) * _nd)


def _ret_fwd(q, k, v, proj, gn_g, consts, T):
    dm, xi, zeta, gl = consts
    L = RET_L
    n_sc = T // L

    def body(q_ref, k_ref, v_ref, rg_ref, g_ref, dm_ref, xi_ref, ze_ref, gl_ref, y_ref, yr_ref, s_ref, st_ref):
        @pl.when(pl.program_id(0) == 0)
        def _():
            st_ref[...] = jnp.zeros_like(st_ref)

        for h in range(HEADS):
            ks, vs = slice(h * RET_DK, (h + 1) * RET_DK), slice(h * RET_DV, (h + 1) * RET_DV)
            qq, kk, vv = q_ref[:, ks], k_ref[:, ks], v_ref[:, vs]
            st = st_ref[h]
            s_ref[h, 0] = st
            p = (_dot(qq, kk, 1, 1) * dm_ref[h]).astype(BF16)
            cross = _dot(qq, st.astype(BF16), 1, 0)
            xi_c = jnp.concatenate([xi_ref[h], xi_ref[h]], axis=1)
            y = _dot(p, vv, 1, 0) + cross * xi_c
            y_ref[:, vs] = y
            yr_ref[:, vs] = _gn_gate(y, rg_ref[:, vs].astype(F32), g_ref[:, vs]).astype(BF16)
            kz = (kk.astype(F32) * ze_ref[h]).astype(BF16)
            gl2 = jnp.concatenate([gl_ref[h, 0:1, :], gl_ref[h, 0:1, :]], axis=1)
            st_ref[h] = st * gl2 + _dot(kz, vv, 0, 0)

    return _pcall(
        body, grid=(n_sc,),
        in_specs=[pl.BlockSpec((L, 1024), lambda n: (n, 0)), pl.BlockSpec((L, 1024), lambda n: (n, 0)),
                  pl.BlockSpec((L, 2048), lambda n: (n, P_RV // 2048)), pl.BlockSpec((L, 2048), lambda n: (n, P_RG // 2048)),
                  _whole(gn_g), _whole(dm), _whole(xi), _whole(zeta), _whole(gl)],
        out_specs=[pl.BlockSpec((L, 2048), lambda n: (n, 0)), pl.BlockSpec((L, 2048), lambda n: (n, 0)),
                   pl.BlockSpec((HEADS, 1, 128, 256), lambda n: (0, n, 0, 0))],
        out_shape=[jax.ShapeDtypeStruct((T, HEADS * RET_DV), F32), jax.ShapeDtypeStruct((T, HEADS * RET_DV), BF16),
                   jax.ShapeDtypeStruct((HEADS, n_sc, 128, 256), F32)],
        scratch_shapes=[pltpu.VMEM((HEADS, 128, 256), F32)], name="ret_fwd",
        compiler_params=pltpu.CompilerParams(dimension_semantics=("arbitrary",)),
    )(q, k, v, proj, gn_g, dm, xi, zeta, gl)


def _ret_bwd(q, k, v, y, proj, gn_g, states, dyr, consts, tabs, T):
    dm, xi, zeta, gl = consts
    cos128, sin128, _, _ = tabs
    L = RET_L
    n_sc = T // L

    def body(q_ref, k_ref, v_ref, y_ref, rg_ref, g_ref, d_ref, s_ref, dm_ref, xi_ref, ze_ref, gl_ref, c_ref, sn_ref,
             dq_ref, dk_ref, dv_ref, drg_ref, dg_ref, gs_ref):
        @pl.when(pl.program_id(0) == 0)
        def _():
            gs_ref[...] = jnp.zeros_like(gs_ref)

        c, sn = c_ref[...], sn_ref[...]
        dgs = []
        for h in range(HEADS):
            ks, vs = slice(h * RET_DK, (h + 1) * RET_DK), slice(h * RET_DV, (h + 1) * RET_DV)
            _, vjp = jax.vjp(_gn_gate, y_ref[:, vs], rg_ref[:, vs].astype(F32), g_ref[:, vs])
            dy, drg, dg = vjp(d_ref[:, vs])
            drg_ref[:, vs] = drg.astype(BF16)
            dgs.append(dg)
            qq, kk, vv, dyy = q_ref[:, ks], k_ref[:, ks], v_ref[:, vs], dy.astype(BF16)
            dmm = dm_ref[h]
            gb = gs_ref[h].astype(BF16)
            sb = s_ref[h, 0].astype(BF16)
            xi_c = jnp.concatenate([xi_ref[h], xi_ref[h]], axis=1)
            pb = (_dot(qq, kk, 1, 1) * dmm).astype(BF16)
            kz = (kk.astype(F32) * ze_ref[h]).astype(BF16)
            dv_ref[:, vs] = (_dot(pb, dyy, 0, 0) + _dot(kz, gb, 1, 0)).astype(BF16)
            da = (_dot(dyy, vv, 1, 1) * dmm).astype(BF16)
            dyx = (dyy.astype(F32) * xi_c).astype(BF16)
            dq = _dot(da, kk, 1, 0) + _dot(dyx, sb, 1, 1)
            dk = _dot(da, qq, 0, 0) + _dot(vv, gb, 1, 1) * ze_ref[h]
            dq_ref[:, ks] = _rope128_t(dq, c, sn).astype(BF16)
            dk_ref[:, ks] = (_rope128_t(dk, c, sn) * RET_SCALE).astype(BF16)
            gl2 = jnp.concatenate([gl_ref[h, 0:1, :], gl_ref[h, 0:1, :]], axis=1)
            gs_ref[h] = gs_ref[h] * gl2 + _dot(qq, dyx, 0, 0)
        _accum(dg_ref, jnp.concatenate(dgs, axis=1))

    rev = lambda n: n_sc - 1 - n
    return _pcall(
        body, grid=(n_sc,),
        in_specs=[pl.BlockSpec((L, 1024), lambda n: (rev(n), 0)), pl.BlockSpec((L, 1024), lambda n: (rev(n), 0)),
                  pl.BlockSpec((L, 2048), lambda n: (rev(n), P_RV // 2048)), pl.BlockSpec((L, 2048), lambda n: (rev(n), 0)),
                  pl.BlockSpec((L, 2048), lambda n: (rev(n), P_RG // 2048)), _whole(gn_g),
                  pl.BlockSpec((L, 2048), lambda n: (rev(n), 0)),
                  pl.BlockSpec((HEADS, 1, 128, 256), lambda n: (0, rev(n), 0, 0)),
                  _whole(dm), _whole(xi), _whole(zeta), _whole(gl),
                  pl.BlockSpec((L, 128), lambda n: (rev(n), 0)), pl.BlockSpec((L, 128), lambda n: (rev(n), 0))],
        out_specs=[pl.BlockSpec((L, 1024), lambda n: (rev(n), 0)), pl.BlockSpec((L, 1024), lambda n: (rev(n), 0)),
                   pl.BlockSpec((L, 2048), lambda n: (rev(n), 0)), pl.BlockSpec((L, 2048), lambda n: (rev(n), 0)),
                   pl.BlockSpec((1, 2048), lambda n: (0, 0))],
        out_shape=[jax.ShapeDtypeStruct((T, 1024), BF16), jax.ShapeDtypeStruct((T, 1024), BF16), jax.ShapeDtypeStruct((T, 2048), BF16),
                   jax.ShapeDtypeStruct((T, 2048), BF16), jax.ShapeDtypeStruct((1, 2048), F32)],
        scratch_shapes=[pltpu.VMEM((HEADS, 128, 256), F32)], name="ret_bwd",
        compiler_params=pltpu.CompilerParams(dimension_semantics=("arbitrary",)),
    )(q, k, v, y, proj, gn_g, dyr, states, dm, xi, zeta, gl, cos128, sin128)


def _diag_mask(nrows, ncols, row0):
    row = row0 + lax.broadcasted_iota(jnp.int32, (nrows, ncols), 0)
    col = lax.broadcasted_iota(jnp.int32, (nrows, ncols), 1)
    return lax.shift_right_logical(col, 6) <= lax.shift_right_logical(row, 6)


def _tri_steps(nb, by_key):
    if by_key:
        pairs = [(i, j) for j in range(nb) for i in range(j, nb)]
    else:
        pairs = [(i, j) for i in range(nb) for j in range(i + 1)]
    return jnp.array([a for a, _ in pairs], jnp.int32), jnp.array([b for _, b in pairs], jnp.int32)


def _attn_fwd(qn, qpe, kv, kpe, T):
    t = min(ATT_TF, T)
    nb = T // t
    ii, jj = _tri_steps(nb, by_key=False)

    hp = ATT_HP
    w = 128 * hp

    def body(ii_ref, jj_ref, qn_ref, qp_ref, kn_ref, kp_ref, v_ref, o_ref, lse_ref, m_sc, l_sc, acc_sc):
        st = pl.program_id(1)
        i, j = ii_ref[st], jj_ref[st]

        @pl.when(j == 0)
        def _():
            m_sc[...] = jnp.full_like(m_sc, NEG)
            l_sc[...] = jnp.zeros_like(l_sc)
            acc_sc[...] = jnp.zeros_like(acc_sc)

        def update(diag):
            kp = kp_ref[...]
            for hh in range(hp):
                sl = slice(128 * hh, 128 * (hh + 1))
                q = jnp.concatenate([qn_ref[:, sl], qp_ref[:, sl]], axis=1)
                k = jnp.concatenate([kn_ref[:, sl], kp], axis=1)
                s = _dot(q, k, 1, 1)
                if diag:
                    s = jnp.where(_diag_mask(t, t, 0), s, NEG)
                m_prev = m_sc[:, sl]
                m_new = jnp.maximum(m_prev, jnp.max(s, axis=1, keepdims=True))
                a = jnp.exp2(m_prev - m_new)
                p = jnp.exp2(s - m_new[:, 0:1])
                l_sc[:, sl] = a * l_sc[:, sl] + jnp.sum(p, axis=1, keepdims=True)
                acc_sc[:, sl] = a * acc_sc[:, sl] + _dot(p.astype(BF16), v_ref[:, sl], 1, 0)
                m_sc[:, sl] = m_new

        @pl.when(j < i)
        def _():
            update(False)

        @pl.when(j == i)
        def _():
            update(True)
            o_ref[...] = (acc_sc[...] / l_sc[...]).astype(BF16)
            lse_ref[...] = m_sc[...] + jnp.log2(l_sc[...])

    qs = pl.BlockSpec((t, w), lambda h, s, ii, jj: (ii[s], h))
    grid_spec = pltpu.PrefetchScalarGridSpec(
        num_scalar_prefetch=2, grid=(HEADS // hp, int(ii.shape[0])),
        in_specs=[qs, qs, pl.BlockSpec((t, w), lambda h, s, ii, jj: (jj[s], h)), pl.BlockSpec((t, 128), lambda h, s, ii, jj: (jj[s], 0)),
                  pl.BlockSpec((t, w), lambda h, s, ii, jj: (jj[s], HEADS // hp + h))],
        out_specs=[qs, qs],
        scratch_shapes=[pltpu.VMEM((t, w), F32), pltpu.VMEM((t, w), F32), pltpu.VMEM((t, w), F32)])
    return _pcall(
        body, grid_spec=grid_spec, out_shape=[jax.ShapeDtypeStruct((T, D), BF16), jax.ShapeDtypeStruct((T, D), F32)], name="attn_fwd",
        compiler_params=pltpu.CompilerParams(dimension_semantics=("arbitrary", "arbitrary")),
    )(ii, jj, qn, qpe, kv, kpe, kv)


def _attn_bwd(qn, qpe, kv, kpe, do, lse, delta, T):
    t = min(ATT_TB, T)
    nb = T // t
    ii, jj = _tri_steps(nb, by_key=True)

    def body(ii_ref, jj_ref, qn_ref, qp_ref, kn_ref, kp_ref, v_ref, do_ref, lse_ref, dl_ref,
             dqn_ref, dqp_ref, dkn_ref, dkp_ref, dv_ref, dk_sc, dv_sc):
        st = pl.program_id(1)
        i, j = ii_ref[st], jj_ref[st]

        @pl.when(st == 0)
        def _():
            dqn_ref[...] = jnp.zeros_like(dqn_ref)
            dqp_ref[...] = jnp.zeros_like(dqp_ref)

        @pl.when(i == j)
        def _():
            dk_sc[...] = jnp.zeros_like(dk_sc)
            dv_sc[...] = jnp.zeros_like(dv_sc)

        def update(diag):
            q = jnp.concatenate([qn_ref[...], qp_ref[...]], axis=1)
            k = jnp.concatenate([kn_ref[...], kp_ref[...]], axis=1)
            dob = do_ref[...]
            s = _dot(q, k, 1, 1)
            if diag:
                s = jnp.where(_diag_mask(t, t, 0), s, NEG)
            p = jnp.exp2(s - lse_ref[:, 0:1])
            dv_sc[...] += _dot(p.astype(BF16), dob, 0, 0)
            dp = _dot(dob, v_ref[...], 1, 1)
            ds = (p * (dp - dl_ref[:, 0:1])).astype(BF16)
            dk_sc[...] += _dot(ds, q, 0, 0)
            dq = _dot(ds, k, 1, 0) * MLA_SCALE
            rows = pl.ds(pl.multiple_of(i * t, t), t)
            dqn_ref[rows, :] += dq[:, :128]
            dqp_ref[rows, :] += dq[:, 128:]

        @pl.when(i > j)
        def _():
            update(False)

        @pl.when(i == j)
        def _():
            update(True)

        @pl.when(i == nb - 1)
        def _():
            dkn_ref[...] = (dk_sc[:, :128] * (1.0 / LOG2E)).astype(BF16)
            dkp_ref[...] = dk_sc[:, 128:] * (1.0 / LOG2E)
            dv_ref[...] = dv_sc[...].astype(BF16)

    qs = pl.BlockSpec((t, 128), lambda h, s, ii, jj: (ii[s], h))
    ks = pl.BlockSpec((t, 128), lambda h, s, ii, jj: (jj[s], h))
    hs = pl.BlockSpec((T, 128), lambda h, s, ii, jj: (0, h))
    grid_spec = pltpu.PrefetchScalarGridSpec(
        num_scalar_prefetch=2, grid=(HEADS, int(ii.shape[0])),
        in_specs=[qs, qs, ks, pl.BlockSpec((t, 128), lambda h, s, ii, jj: (jj[s], 0)),
                  pl.BlockSpec((t, 128), lambda h, s, ii, jj: (jj[s], HEADS + h)), qs, qs, qs],
        out_specs=[hs, hs, ks, ks, ks],
        scratch_shapes=[pltpu.VMEM((t, 256), F32), pltpu.VMEM((t, 128), F32)])
    return _pcall(
        body, grid_spec=grid_spec,
        out_shape=[jax.ShapeDtypeStruct((T, D), F32), jax.ShapeDtypeStruct((T, D), F32), jax.ShapeDtypeStruct((T, D), BF16),
                   jax.ShapeDtypeStruct((T, D), F32), jax.ShapeDtypeStruct((T, D), BF16)], name="attn_bwd",
        compiler_params=pltpu.CompilerParams(dimension_semantics=("arbitrary", "arbitrary")),
    )(ii, jj, qn, qpe, kv, kpe, kv, do, lse, delta)


def _mesh_pos():
    x, y, c = lax.axis_index("x"), lax.axis_index("y"), lax.axis_index("c")
    return x, y, c, 4 * x + 2 * y + c


def _peer(x, y, c, k):
    px, py, pc = (x + ((k >> 2) & 1)) % 2, (y + ((k >> 1) & 1)) % 2, (c + (k & 1)) % 2
    return (px, py, pc), 4 * px + 2 * py + pc


_ANY = pl.BlockSpec(memory_space=pl.ANY)


def _rcopy(src, dst, send_sems, recv_sems, k, to):
    return pltpu.make_async_remote_copy(src_ref=src, dst_ref=dst, send_sem=send_sems.at[k], recv_sem=recv_sems.at[k],
                                        device_id=to, device_id_type=pl.DeviceIdType.MESH)


def _all_gather(wsh, ssh):
    def body(w_ref, s_ref, wall_ref, sall_ref, send_sems, recv_sems, loc_sems):
        x, y, c, me = _mesh_pos()
        sib = (x, y, 1 - c)
        chips = [(1 - x, y), (x, 1 - y), (1 - x, 1 - y)]
        slot = lambda px, py, pc: 4 * px + 2 * py + pc
        loc = [pltpu.make_async_copy(s_ref, sall_ref.at[me], loc_sems.at[0])]
        for cp in loc:
            cp.start()
        sends, fwd_waits = [], []
        for n, (src, dst) in enumerate(((w_ref, wall_ref), (s_ref, sall_ref))):
            o = 7 * n
            sends.append(_rcopy(src, dst.at[me], send_sems, recv_sems, o, sib))
            for j, chip in enumerate(chips):
                sends.append(_rcopy(src, dst.at[me], send_sems, recv_sems, o + 1 + j, (*chip, c)))
        for cp in sends:
            cp.start()
        for n, (src, dst) in enumerate(((w_ref, wall_ref), (s_ref, sall_ref))):
            o = 7 * n
            for j, chip in enumerate(chips):
                got = dst.at[slot(*chip, c)]
                _rcopy(src, got, send_sems, recv_sems, o + 1 + j, sib).wait_recv()
                fw = _rcopy(got, got, send_sems, recv_sems, o + 4 + j, sib)
                fw.start()
                sends.append(fw)
            fwd_waits.append(_rcopy(src, dst.at[slot(x, y, 1 - c)], send_sems, recv_sems, o, sib))
            for j, chip in enumerate(chips):
                fwd_waits.append(_rcopy(src, dst.at[slot(*chip, 1 - c)], send_sems, recv_sems, o + 4 + j, sib))
        for cp in fwd_waits:
            cp.wait_recv()
        for cp in sends:
            cp.wait_send()
        for cp in loc:
            cp.wait()

    wall, sall = _pcall(
        body, in_specs=[_ANY, _ANY], out_specs=[_ANY, _ANY],
        out_shape=[jax.ShapeDtypeStruct((N_DEV,) + wsh.shape, wsh.dtype), jax.ShapeDtypeStruct((N_DEV,) + ssh.shape, ssh.dtype)],
        scratch_shapes=[pltpu.SemaphoreType.DMA((14,)), pltpu.SemaphoreType.DMA((14,)), pltpu.SemaphoreType.DMA((1,))],
        name="all_gather_weights",
    )(wsh, ssh)
    me = 4 * lax.axis_index("x") + 2 * lax.axis_index("y") + lax.axis_index("c")
    return lax.dynamic_update_index_in_dim(wall, wsh, me, axis=0), sall


_HBM = pl.BlockSpec(memory_space=pltpu.HBM)
_SEM = pl.BlockSpec(memory_space=pltpu.SEMAPHORE)
_EFFECT = pltpu.SideEffectType.DATAFLOW_SIDE_EFFECTING


def _other_chips(x, y):
    return [(1 - x, y), (x, 1 - y), (1 - x, 1 - y)]


def _gather_start(wsh, order_after):
    def body(w_ref, land_ref, dep_ref, send_sems, recv_sems, w_thru, land_thru, token):
        x, y, c, me = _mesh_pos()
        _rcopy(w_ref, land_ref.at[me], send_sems, recv_sems, 0, (x, y, 1 - c)).start()
        for j, chip in enumerate(_other_chips(x, y)):
            _rcopy(w_ref, land_ref.at[me], send_sems, recv_sems, 1 + j, (*chip, c)).start()
        token[...] = jnp.zeros_like(token)

    shape = (N_DEV,) + wsh.shape
    land = pltpu.with_memory_space_constraint(lax.empty(shape, wsh.dtype), pltpu.HBM)
    return _pcall(
        body, name="gather_start",
        out_shape=(pltpu.SemaphoreType.DMA((4,)), pltpu.SemaphoreType.DMA((4,)), pltpu.HBM(wsh.shape, wsh.dtype),
                   pltpu.HBM(shape, wsh.dtype), jax.ShapeDtypeStruct((8, 128), F32)),
        in_specs=(_HBM, _HBM, _ANY), out_specs=(_SEM, _SEM, _HBM, _HBM, pl.BlockSpec(memory_space=pltpu.VMEM)),
        input_output_aliases={0: 2, 1: 3}, compiler_params=pltpu.CompilerParams(has_side_effects=_EFFECT),
    )(pltpu.with_memory_space_constraint(wsh, pltpu.HBM), land, order_after)


def _gather_wait(send_sems, recv_sems, w_thru, land_thru, after):
    def body(w_ref, land_ref, send_sems, recv_sems, after_ref, w_out, land_out):
        x, y, c, _ = _mesh_pos()
        senders = [(x, y, 1 - c)] + [(*chip, c) for chip in _other_chips(x, y)]
        for k, (px, py, pc) in enumerate(senders):
            cp = _rcopy(w_ref, land_ref.at[4 * px + 2 * py + pc], send_sems, recv_sems, k, (px, py, pc))
            cp.wait_send()
            cp.wait_recv()

    return _pcall(
        body, name="gather_wait", out_shape=(pltpu.HBM(w_thru.shape, w_thru.dtype), pltpu.HBM(land_thru.shape, land_thru.dtype)),
        in_specs=(_HBM, _HBM, _SEM, _SEM, _ANY), out_specs=(_HBM, _HBM), input_output_aliases={0: 0, 1: 1},
        compiler_params=pltpu.CompilerParams(has_side_effects=_EFFECT),
    )(w_thru, land_thru, send_sems, recv_sems, after)


def _gather_finish(wsh, land):
    def body(land_ref, out_ref, send_sems, recv_sems):
        x, y, c, _ = _mesh_pos()
        sib = (x, y, 1 - c)
        sends, recvs = [], []
        for j, (px, py) in enumerate(_other_chips(x, y)):
            mine, theirs = 4 * px + 2 * py + c, 4 * px + 2 * py + (1 - c)
            sends.append(_rcopy(land_ref.at[mine], out_ref.at[mine], send_sems, recv_sems, j, sib))
            recvs.append(_rcopy(land_ref.at[theirs], out_ref.at[theirs], send_sems, recv_sems, j, sib))
        for cp in sends:
            cp.start()
        for cp in recvs:
            cp.wait_recv()
        for cp in sends:
            cp.wait_send()

    done = _pcall(
        body, in_specs=[_ANY], out_specs=_ANY, out_shape=jax.ShapeDtypeStruct(land.shape, land.dtype),
        input_output_aliases={0: 0},
        scratch_shapes=[pltpu.SemaphoreType.DMA((3,)), pltpu.SemaphoreType.DMA((3,))], name="gather_finish",
    )(land)
    me = 4 * lax.axis_index("x") + 2 * lax.axis_index("y") + lax.axis_index("c")
    return lax.dynamic_update_index_in_dim(done, wsh, me, axis=0)


_ROW_TILES = (400, 368, 352, 256, 128, 16, 8)


def _exchange_sibling(g4, name):
    def body(g_ref, r_ref, send_sems, recv_sems):
        x, y, c, _ = _mesh_pos()
        sib = (x, y, 1 - c)
        cps = [_rcopy(g_ref.at[q, 1 - c], r_ref.at[q], send_sems, recv_sems, q, sib) for q in range(4)]
        for cp in cps:
            cp.start()
        for cp in cps:
            cp.wait()

    n, _, R, C = g4.shape
    return _pcall(
        body, in_specs=[_ANY], out_specs=_ANY, out_shape=jax.ShapeDtypeStruct((n, R, C), g4.dtype),
        scratch_shapes=[pltpu.SemaphoreType.DMA((4,)), pltpu.SemaphoreType.DMA((4,))], name=name,
    )(g4)


def _sum_sibling(g4, rsib, cvec, name):
    n, _, R, C = g4.shape
    tr = _pick(R, _ROW_TILES)

    def body(c_ref, g_ref, r_ref, o_ref):
        o_ref[...] = (g_ref[...].astype(F32) + r_ref[...].astype(F32)).astype(o_ref.dtype)

    grid_spec = pltpu.PrefetchScalarGridSpec(
        num_scalar_prefetch=1, grid=(n, R // tr),
        in_specs=[pl.BlockSpec((None, None, tr, C), lambda q, i, cr: (q, cr[0], i, 0)), pl.BlockSpec((None, tr, C), lambda q, i, cr: (q, i, 0))],
        out_specs=pl.BlockSpec((None, tr, C), lambda q, i, cr: (q, i, 0)))
    return _pcall(body, grid_spec=grid_spec, out_shape=jax.ShapeDtypeStruct((n, R, C), g4.dtype), name=name)(cvec, g4, rsib)


def _chips_start(part, name):
    def body(p_ref, land_ref, send_sems, recv_sems, p_thru, land_thru, token):
        x, y, c, _ = _mesh_pos()
        myq = 2 * x + y
        for j, (px, py) in enumerate(_other_chips(x, y)):
            _rcopy(p_ref.at[2 * px + py], land_ref.at[myq], send_sems, recv_sems, j, (px, py, c)).start()
        token[...] = jnp.zeros_like(token)

    land = pltpu.with_memory_space_constraint(lax.empty(part.shape, part.dtype), pltpu.HBM)
    return _pcall(
        body, name=name,
        out_shape=(pltpu.SemaphoreType.DMA((3,)), pltpu.SemaphoreType.DMA((3,)), pltpu.HBM(part.shape, part.dtype),
                   pltpu.HBM(part.shape, part.dtype), jax.ShapeDtypeStruct((8, 128), F32)),
        in_specs=(_HBM, _HBM), out_specs=(_SEM, _SEM, _HBM, _HBM, pl.BlockSpec(memory_space=pltpu.VMEM)),
        input_output_aliases={0: 2, 1: 3}, compiler_params=pltpu.CompilerParams(has_side_effects=_EFFECT),
    )(pltpu.with_memory_space_constraint(part, pltpu.HBM), land)


def _chips_wait(send_sems, recv_sems, p_thru, land_thru, after, name):
    def body(p_ref, land_ref, send_sems, recv_sems, after_ref, p_out, land_out):
        x, y, c, _ = _mesh_pos()
        for j, (px, py) in enumerate(_other_chips(x, y)):
            q = 2 * px + py
            cp = _rcopy(p_ref.at[q], land_ref.at[q], send_sems, recv_sems, j, (px, py, c))
            cp.wait_send()
            cp.wait_recv()

    return _pcall(
        body, name=name, out_shape=(pltpu.HBM(p_thru.shape, p_thru.dtype), pltpu.HBM(p_thru.shape, p_thru.dtype)),
        in_specs=(_HBM, _HBM, _SEM, _SEM, _ANY), out_specs=(_HBM, _HBM), input_output_aliases={0: 0, 1: 1},
        compiler_params=pltpu.CompilerParams(has_side_effects=_EFFECT),
    )(p_thru, land_thru, send_sems, recv_sems, after)


def _sum_chips(part, land, qvec, name):
    n, R, C = part.shape
    tr = _pick(R, _ROW_TILES)

    def body(q_ref, p_ref, *refs):
        o_ref = refs[n]
        acc = None
        for q in range(n):
            term = jnp.where(q_ref[0] == q, p_ref[...], refs[q][...]).astype(F32)
            acc = term if acc is None else acc + term
        o_ref[...] = acc

    def land_spec(q):
        return pl.BlockSpec((None, tr, C), lambda i, m: (jnp.where(m[0] == q, (q + 1) % n, q), i, 0))

    grid_spec = pltpu.PrefetchScalarGridSpec(
        num_scalar_prefetch=1, grid=(R // tr,),
        in_specs=[pl.BlockSpec((None, tr, C), lambda i, m: (m[0], i, 0))] + [land_spec(q) for q in range(n)],
        out_specs=pl.BlockSpec((tr, C), lambda i, m: (i, 0)))
    return _pcall(body, grid_spec=grid_spec, out_shape=jax.ShapeDtypeStruct((R, C), F32), name=name)(qvec, part, *([land] * n))


def _exchange_small(gsmall):
    def body(s_ref, srecv_ref, send_sems, recv_sems, loc_sem):
        x, y, c, me = _mesh_pos()
        loc = pltpu.make_async_copy(s_ref, srecv_ref.at[me], loc_sem.at[0])
        loc.start()
        sends, recvs = [], []
        for k in range(1, N_DEV):
            to, pidx = _peer(x, y, c, k)
            sends.append(_rcopy(s_ref, srecv_ref.at[me], send_sems, recv_sems, k - 1, to))
            recvs.append(_rcopy(s_ref, srecv_ref.at[pidx], send_sems, recv_sems, k - 1, to))
        for cp in sends:
            cp.start()
        for cp in recvs:
            cp.wait_recv()
        for cp in sends:
            cp.wait_send()
        loc.wait()

    return _pcall(
        body, in_specs=[_ANY], out_specs=_ANY, out_shape=jax.ShapeDtypeStruct((N_DEV,) + gsmall.shape, gsmall.dtype),
        scratch_shapes=[pltpu.SemaphoreType.DMA((7,)), pltpu.SemaphoreType.DMA((7,)), pltpu.SemaphoreType.DMA((1,))],
        name="exchange_small",
    )(gsmall)


def _sum_slots(recv, name):
    n, R, C = recv.shape
    tr = _pick(R, _ROW_TILES)

    def body(r_ref, o_ref):
        acc = r_ref[0].astype(F32)
        for s in range(1, n):
            acc = acc + r_ref[s].astype(F32)
        o_ref[...] = acc

    return _pcall(body, grid=(R // tr,), in_specs=[pl.BlockSpec((n, tr, C), lambda i: (0, i, 0))],
                  out_specs=pl.BlockSpec((tr, C), lambda i: (i, 0)), out_shape=jax.ShapeDtypeStruct((R, C), F32), name=name)(recv)


def _adamw(w, g, m, v, name, order_after=None):
    shape = w.shape
    w2, g2, m2, v2 = (a.reshape(-1, shape[-1]) for a in (w, g, m, v))
    R, C = w2.shape
    tr = _pick(R, (256, 128, 64, 32, 16, 8)) if R > 256 else R
    n_in = 4 + int(order_after is not None)

    def body(*refs):
        w_ref, g_ref, m_ref, v_ref = refs[:4]
        d_ref, nm_ref, nv_ref = refs[n_in:]
        gg = g_ref[...]
        nm = ADAM_B1 * m_ref[...] + (1.0 - ADAM_B1) * gg
        nv = ADAM_B2 * v_ref[...] + (1.0 - ADAM_B2) * jnp.square(gg)
        m_hat = nm / (1.0 - ADAM_B1 ** ADAM_STEP)
        v_hat = nv / (1.0 - ADAM_B2 ** ADAM_STEP)
        d_ref[...] = -ADAM_LR * (m_hat / (jnp.sqrt(v_hat) + ADAM_EPS) + ADAM_WD * w_ref[...])
        nm_ref[...] = nm
        nv_ref[...] = nv

    spec = pl.BlockSpec((tr, C), lambda i: (i, 0))
    in_specs, args = [spec] * 4, [w2, g2, m2, v2]
    if order_after is not None:
        in_specs.append(pl.BlockSpec(order_after.shape, lambda i: (0, 0)))
        args.append(order_after)
    d, nm, nv = _pcall(body, grid=(R // tr,), in_specs=in_specs, out_specs=[spec] * 3,
                       out_shape=[jax.ShapeDtypeStruct((R, C), F32)] * 3, name=name)(*args)
    return d.reshape(shape), nm.reshape(shape), nv.reshape(shape)


def _to_rows(name, w):
    w = w[0]
    if name in ("ffn1_w_in", "ffn2_w_in", "w_in"):
        return w.T
    if name in ("w_uq", "w_ukv", "ple_w_proj"):
        return w.T.reshape(-1, 1024)
    return w


def _from_rows(name, g, shape):
    if name in ("ffn1_w_in", "ffn2_w_in", "w_in"):
        return g.T.reshape(shape)
    if name in ("w_uq", "w_ukv", "ple_w_proj"):
        return g.reshape(-1, shape[1]).T.reshape(shape)
    return g.reshape(shape)


def _unpack(wall, group):
    out, off = {}, 0
    for name, r in group:
        out[name] = wall[:, off:off + r, :].reshape(N_DEV * r, 1024)
        off += _pad16(r)
    return out


def _w_in_internal(wt):
    return jnp.concatenate([wt[0:6144], wt[6720:8768], wt[6144:6720], jnp.zeros((P_W - W_IN_COLS, 1024), wt.dtype)], axis=0)


def _w_in_external(d):
    return jnp.concatenate([d[0:6144], d[8192:8768], d[6144:8192]], axis=0)


def _rope_tables(positions):
    pos = positions[0].astype(F32)

    def cs(half):
        inv = ROPE_BASE ** (-jnp.arange(half, dtype=F32) / half)
        ang = pos[:, None] * inv
        return jnp.cos(ang), jnp.sin(ang)

    c, s = cs(64)
    c2, s2 = cs(32)
    return (jnp.concatenate([c, c], axis=1), jnp.concatenate([-s, s], axis=1),
            jnp.concatenate([c2, c2, c2, c2], axis=1), jnp.concatenate([-s2, s2, -s2, s2], axis=1))


def _local_step(x, p, positions, target, W, ln_g, ln_b, gn_g, qg, kvg, early=None, rest_weights=None, start_token=None):
    T = x.shape[0]
    tabs = _rope_tables(positions)
    rc = _ret_consts()
    lg = [ln_g[i:i + 1] for i in range(4)]
    lb = [ln_b[i:i + 1] for i in range(4)]
    pb = p.astype(BF16)

    hs1, gu1, *xb = _ffn_in(x, W["ffn1_w_in"], "ffn1_in", order_after=start_token)
    xb = xb[0] if xb else x
    f1, h1, h1b = _mm(hs1, W["ffn1_w_out"], name="ffn1_out", tm=LN_TM, epilogue=_ln_epilogue(x, 0.5, lg[0], lb[0]))
    if rest_weights is not None:
        W = {**W, **rest_weights(h1b)}
    w_in_t = _w_in_internal(W["w_in"])
    wuq = W["w_uq"].reshape(1536, LORA).reshape(HEADS, 192, LORA)
    wuq = jnp.concatenate([wuq[:, :128].reshape(1024, LORA), wuq[:, 128:].reshape(512, LORA)], axis=0)
    wukv = W["w_ukv"].reshape(2048, LORA).reshape(HEADS, 2, 128, LORA).transpose(1, 0, 2, 3).reshape(2048, LORA)
    wp_t = W["ple_w_proj"].reshape(1024, D_PLE)
    proj = _mm(h1b, w_in_t, tb=True, out_dtype=BF16, name="mixer_in")
    rq, rk, qn, kvn, kpe = _mixer_prep_fwd(proj, tabs, qg, kvg, T)
    y, yr, states = _ret_fwd(rq, rk, proj, proj, gn_g, rc, T)
    y_ret = _mm(yr, W["w_ret_o"], name="ret_o")
    q = _mm(qn, wuq, tb=True, name="mla_uq")
    kv = _mm(kvn, wukv, tb=True, out_dtype=BF16, name="mla_ukv")
    qnope, qpe = _q_assemble_fwd(q, tabs, T)
    o, lse = _attn_fwd(qnope, qpe, kv, kpe, T)
    y_mla, mix = _mm(o, W["w_mla_o"], name="mla_o", tm=LN_TM, epilogue=_mix_epilogue(proj, y_ret))
    mixed, h2, h2b = _mm(mix, W["w_out"], name="mixer_out", tm=LN_TM, epilogue=_ln_epilogue(h1, 1.0, lg[1], lb[1]))
    hs2, gu2 = _ffn_in(h2b, W["ffn2_w_in"], "ffn2_in")
    f2, h3, h3b = _mm(hs2, W["ffn2_w_out"], name="ffn2_out", tm=LN_TM, epilogue=_ln_epilogue(h2, 0.5, lg[2], lb[2]))
    pp = _mm(pb, wp_t, tb=True, name="ple_proj")

    G = {}
    dh3_a, dgl, dpp, dg3, db3, loss = _mm(h3b, W["ple_w_gate"], name="ple_gate", tm=LN_TM,
                                          epilogue=_head_epilogue(h3, pp, target, lg[3], lb[3]))
    G["ple_w_gate"] = _mm(h3b, dgl, ta=True, name="d_ple_gate")
    G["ple_w_proj"] = _mm(dpp, pb, ta=True, name="d_ple_proj")
    dh2_a, df2, dg2, db2 = _mm(dgl, W["ple_w_gate"], tb=True, add=dh3_a, name="dh3", tm=LN_TM,
                               epilogue=_ln_bwd_epilogue(h2, f2, 0.5, lg[2], lb[2]))
    G["ffn2_w_out"] = _mm(hs2, df2, ta=True, name="d_ffn2_out")
    da2 = _ffn_act_bwd(df2, W["ffn2_w_out"], gu2, "ffn2_act_bwd")
    G["ffn2_w_in"] = _mm(da2, h2b, ta=True, name="d_ffn2_in")
    dh1_a, dmixed, dg1, db1 = _mm(da2, W["ffn2_w_in"], add=dh2_a, name="dh2", tm=LN_TM,
                                  epilogue=_ln_bwd_epilogue(h1, mixed, 1.0, lg[1], lb[1]))
    G["w_out"] = _mm(mix, dmixed, ta=True, name="d_mixer_out")
    dgr, dgm, dy_ret, dy_mla = _mm(dmixed, W["w_out"], tb=True, name="dmix", tm=LN_TM,
                                   epilogue=_mix_bwd_epilogue(proj, y_ret, y_mla))
    G["w_mla_o"] = _mm(o, dy_mla, ta=True, name="d_mla_o")
    dob, delta = _mm(dy_mla, W["w_mla_o"], tb=True, name="do", tm=LN_TM, epilogue=_delta_epilogue(o))
    dqn_f, dqpe_f, dkn, dkpe_all, dv = _attn_bwd(qnope, qpe, kv, kpe, dob, lse, delta, T)
    dq_n, dq_r = _q_assemble_bwd(dqn_f, dqpe_f, tabs, T)
    g_uq = jnp.concatenate([_mm(dq_n, qn, ta=True, name="d_uq_nope"), _mm(dq_r, qn, ta=True, name="d_uq_rope")], axis=0)
    g_uq = jnp.concatenate([g_uq[:1024].reshape(HEADS, 128, LORA), g_uq[1024:].reshape(HEADS, 64, LORA)], axis=1)
    G["w_uq"] = g_uq.reshape(1536 * LORA // 1024, 1024)
    dqn = _mm(dq_r, wuq[1024:], add=_mm(dq_n, wuq[:1024], name="dqn_a"), name="dqn_b")
    g_ukv = jnp.stack([_mm(dkn, kvn, ta=True, name="d_ukv_k"), _mm(dv, kvn, ta=True, name="d_ukv_v")], axis=0)
    G["w_ukv"] = g_ukv.reshape(2, HEADS, 128, LORA).transpose(1, 0, 2, 3).reshape(2048 * LORA // 1024, 1024)
    dkvn = _mm(dv, wukv[1024:], add=_mm(dkn, wukv[:1024], name="dkvn_a"), name="dkvn_b")
    dcq, dckv, dkpe, dqg, dkvg = _rms_bwd(proj, dqn, dkvn, dkpe_all, tabs, qg, kvg, T)
    G["w_ret_o"] = _mm(yr, dy_ret, ta=True, name="d_ret_o")
    dyr = _mm(dy_ret, W["w_ret_o"], tb=True, name="dyr")
    drq, drk, drv, drg, dgn = _ret_bwd(rq, rk, proj, y, proj, gn_g, states, dyr, rc, tabs, T)
    dproj = jnp.concatenate([drq, drk, drv, drg, dgr, dgm, dcq, dckv, dkpe, jnp.zeros((T, P_W - P_KPE - 128), BF16)], axis=1)
    G["w_in"] = _w_in_external(_mm(dproj, h1b, ta=True, name="d_mixer_in"))
    lg0 = lg[0] if early is None else lg[0] + early(G)[0:1, 0:1]
    dx_a, df1, dg0, db0 = _mm(dproj, w_in_t, add=dh1_a, name="dh1", tm=LN_TM,
                              epilogue=_ln_bwd_epilogue(x, f1, 0.5, lg0, lb[0]))
    G["ffn1_w_out"] = _mm(hs1, df1, ta=True, name="d_ffn1_out")
    da1 = _ffn_act_bwd(df1, W["ffn1_w_out"], gu1, "ffn1_act_bwd")
    G["ffn1_w_in"] = _mm(da1, xb, ta=True, name="d_ffn1_in")
    grad_x = _mm(da1, W["ffn1_w_in"], add=dx_a, name="grad_x")

    small = dict(ln_g=jnp.concatenate([dg0, dg1, dg2, dg3], axis=0), ln_b=jnp.concatenate([db0, db1, db2, db3], axis=0),
                 ret_gn_g=dgn, q_norm_g=dqg, kv_norm_g=dkvg)
    return loss, grad_x, G, small


def kernel(x, p, positions, ln_g, ln_b, ffn1_w_in, ffn1_w_out, w_in, ret_gn_g, w_ret_o, q_norm_g, kv_norm_g, w_uq, w_ukv, w_mla_o, w_out, ffn2_w_in, ffn2_w_out, ple_w_gate, ple_w_proj, loss_target, m_ln_g, m_ln_b, m_ffn1_w_in, m_ffn1_w_out, m_w_in, m_ret_gn_g, m_w_ret_o, m_q_norm_g, m_kv_norm_g, m_w_uq, m_w_ukv, m_w_mla_o, m_w_out, m_ffn2_w_in, m_ffn2_w_out, m_ple_w_gate, m_ple_w_proj, v_ln_g, v_ln_b, v_ffn1_w_in, v_ffn1_w_out, v_w_in, v_ret_gn_g, v_w_ret_o, v_q_norm_g, v_kv_norm_g, v_w_uq, v_w_ukv, v_w_mla_o, v_w_out, v_ffn2_w_in, v_ffn2_w_out, v_ple_w_gate, v_ple_w_proj):
    names = ("ln_g", "ln_b", "ffn1_w_in", "ffn1_w_out", "w_in", "ret_gn_g", "w_ret_o", "q_norm_g", "kv_norm_g", "w_uq", "w_ukv",
             "w_mla_o", "w_out", "ffn2_w_in", "ffn2_w_out", "ple_w_gate", "ple_w_proj")
    ws = dict(zip(names, (ln_g, ln_b, ffn1_w_in, ffn1_w_out, w_in, ret_gn_g, w_ret_o, q_norm_g, kv_norm_g, w_uq, w_ukv, w_mla_o,
                          w_out, ffn2_w_in, ffn2_w_out, ple_w_gate, ple_w_proj)))
    ms = dict(zip(names, (m_ln_g, m_ln_b, m_ffn1_w_in, m_ffn1_w_out, m_w_in, m_ret_gn_g, m_w_ret_o, m_q_norm_g, m_kv_norm_g, m_w_uq,
                          m_w_ukv, m_w_mla_o, m_w_out, m_ffn2_w_in, m_ffn2_w_out, m_ple_w_gate, m_ple_w_proj)))
    vs = dict(zip(names, (v_ln_g, v_ln_b, v_ffn1_w_in, v_ffn1_w_out, v_w_in, v_ret_gn_g, v_w_ret_o, v_q_norm_g, v_kv_norm_g, v_w_uq,
                          v_w_ukv, v_w_mla_o, v_w_out, v_ffn2_w_in, v_ffn2_w_out, v_ple_w_gate, v_ple_w_proj)))

    parts = []
    for name, r in PACK:
        rows = _to_rows(name, ws[name])
        if _pad16(r) != r:
            rows = jnp.concatenate([rows, jnp.zeros((_pad16(r) - r, 1024), F32)], axis=0)
        parts.append(rows)
    wsh_first = jnp.concatenate(parts[:len(PACK_LATE)], axis=0).astype(BF16)
    wsh_rest = jnp.concatenate(parts[len(PACK_LATE):], axis=0).astype(BF16)
    ssh = jnp.concatenate([ln_g[0], ln_b[0]], axis=0)
    wall_first, sall = _all_gather(wsh_first, ssh)
    *gather_handles, start_token = _gather_start(wsh_rest, wall_first)
    W = _unpack(wall_first, PACK_LATE)
    ln_full = sall.reshape(N_DEV, 2, 4, 128).transpose(1, 2, 0, 3).reshape(2, 4, 1024)

    def rest_weights(after):
        w_thru, land = _gather_wait(*gather_handles, after)
        return _unpack(_gather_finish(w_thru, land), PACK_EARLY)

    cvec = lax.axis_index("c").astype(jnp.int32).reshape(1)
    qvec = (2 * lax.axis_index("x") + lax.axis_index("y")).astype(jnp.int32).reshape(1)

    def chip_partials(G, group, tag):
        gparts = []
        for name, r in group:
            g = G[name].reshape(N_DEV, r, 1024)
            if _pad16(r) != r:
                g = jnp.concatenate([g, jnp.zeros((N_DEV, _pad16(r) - r, 1024), g.dtype)], axis=1)
            gparts.append(g)
        gfull = jnp.concatenate(gparts, axis=1).astype(BF16)
        g4 = gfull.reshape(4, 2, gfull.shape[1], 1024)
        return _sum_sibling(g4, _exchange_sibling(g4, "exchange_sibling_" + tag), cvec, "sum_sibling_" + tag)

    in_flight = []

    def early(G):
        *handles, token = _chips_start(chip_partials(G, PACK_EARLY, "early"), "chips_start_early")
        in_flight.append(handles)
        return token

    loss_p, grad_x, G, small = _local_step(x[0], p[0, 0], positions, loss_target[0], W, ln_full[0], ln_full[1],
                                           ret_gn_g, q_norm_g, kv_norm_g, early=early, rest_weights=rest_weights,
                                           start_token=start_token)

    part_e, land_e = _chips_wait(*in_flight[0], grad_x, "chips_wait_early")
    gsh_early = _sum_chips(part_e, land_e, qvec, "sum_grads_early")
    *late_handles, late_token = _chips_start(chip_partials(G, PACK_LATE, "late"), "chips_start_late")
    pad256 = lambda a: jnp.concatenate([a, jnp.zeros((1, 1024 - a.shape[1]), F32)], axis=1)
    gsmall = jnp.concatenate([small["ln_g"], small["ln_b"], small["ret_gn_g"].reshape(2, 1024), pad256(small["q_norm_g"]),
                              pad256(small["kv_norm_g"]), jnp.zeros((SMALL_ROWS - 12, 1024), F32)], axis=0)
    ssum = _sum_slots(_exchange_small(gsmall), "sum_small_grads")

    def unpack_grads(group, gsh):
        out, off = {}, 0
        for name, r in group:
            out[name] = _from_rows(name, gsh[off:off + r], ws[name].shape)
            off += _pad16(r)
        return out

    grads = unpack_grads(PACK_EARLY, gsh_early)
    me = 4 * lax.axis_index("x") + 2 * lax.axis_index("y") + lax.axis_index("c")
    grads["ln_g"] = lax.dynamic_slice(ssum[0:4], (0, me * 128), (4, 128)).reshape(1, 4, 128)
    grads["ln_b"] = lax.dynamic_slice(ssum[4:8], (0, me * 128), (4, 128)).reshape(1, 4, 128)
    grads["ret_gn_g"] = ssum[8:10].reshape(1, 2048)
    grads["q_norm_g"] = ssum[10:11, :256]
    grads["kv_norm_g"] = ssum[11:12, :256]

    delta, new_m, new_v = {}, {}, {}
    late_names = [n for n, _ in PACK_LATE]
    for name in names:
        if name not in late_names:
            delta[name], new_m[name], new_v[name] = _adamw(ws[name], grads[name], ms[name], vs[name], "adamw_" + name,
                                                           order_after=late_token)
    part_l, land_l = _chips_wait(*late_handles, new_v["w_in"], "chips_wait_late")
    grads.update(unpack_grads(PACK_LATE, _sum_chips(part_l, land_l, qvec, "sum_grads_late")))
    for name in late_names:
        delta[name], new_m[name], new_v[name] = _adamw(ws[name], grads[name], ms[name], vs[name], "adamw_" + name)

    loss = lax.psum(loss_p[0, 0], ("x", "y", "c"))
    return (loss, grad_x[None], *[grads[n] for n in names], *[delta[n] for n in names],
            *[new_m[n] for n in names], *[new_v[n] for n in names])
```

```python
import functools
import math

import jax
import jax.numpy as jnp
from jax import lax
from jax.experimental import pallas as pl
from jax.experimental.pallas import tpu as pltpu

F32 = jnp.float32
BF16 = jnp.bfloat16

N_DEV = 8
D = 1024
D_FF = 2816
D_PLE = 256
CHUNK = 64
HEADS = 8
RET_DK = 128
RET_DV = 256
MLA_NOPE = 128
MLA_ROPE = 64
MLA_DV = 128
LORA = 256
ROPE_BASE = 10000.0
EPS = 1e-5
ALPHA = 2.0 ** 0.25
RET_SCALE = RET_DK ** -0.5
MLA_SCALE = (MLA_NOPE + MLA_ROPE) ** -0.5
NEG = -1e30

ADAM_LR = 0.001
ADAM_B1 = 0.9
ADAM_B2 = 0.999
ADAM_EPS = 1e-08
ADAM_WD = 0.01
ADAM_STEP = 10

P_RQ, P_RK, P_RV, P_RG, P_GR, P_GM, P_CQ, P_CKV, P_KPE, P_W = 0, 1024, 2048, 4096, 6144, 7168, 8192, 8448, 8704, 8960
W_IN_COLS = 8768
RET_L = 256
ATT_TF = 1024
ATT_TB = 1024
ATT_HP = 4
LOG2E = math.log2(math.e)
Q_PRESCALE = MLA_SCALE * LOG2E

PACK = (("ffn1_w_in", 704), ("ffn1_w_out", 352), ("w_in", 1096), ("w_ret_o", 256), ("w_uq", 48), ("w_ukv", 64),
        ("w_mla_o", 128), ("w_out", 128), ("ffn2_w_in", 704), ("ffn2_w_out", 352), ("ple_w_gate", 128), ("ple_w_proj", 32))


def _pad16(r):
    return -(-r // 16) * 16


PACK_ROWS = sum(_pad16(r) for _, r in PACK)
PACK_LATE = PACK[:2]
PACK_EARLY = PACK[2:]
SMALL_ROWS = 16


def _pcall(body, **kw):
    return pl.pallas_call(body, **kw)


def _pick(dim, prefs):
    for p in prefs:
        if dim % p == 0:
            return p
    return dim


def _sigmoid(x):
    return 1.0 / (1.0 + jnp.exp(-x))


def _silu(x):
    return x * _sigmoid(x)


def _ln(r, g, b):
    mu = jnp.mean(r, axis=-1, keepdims=True)
    var = jnp.mean(jnp.square(r - mu), axis=-1, keepdims=True)
    return (r - mu) * lax.rsqrt(var + EPS) * g + b


def _rms(x, g):
    return x * lax.rsqrt(jnp.mean(jnp.square(x), axis=-1, keepdims=True) + EPS) * g


def _dot(a, b, ca, cb):
    return lax.dot_general(a, b, (((ca,), (cb,)), ((), ())), preferred_element_type=F32)


def _accum(ref, val, first=None):
    @pl.when(pl.program_id(0) == 0 if first is None else first)
    def _():
        ref[...] = jnp.zeros_like(ref)

    ref[...] += val


def _mm(a, b, *, ta=False, tb=False, add=None, out_dtype=None, name, tm=None, tn=None, tk=None, epilogue=None):
    parts = a.shape[0] if a.ndim == 3 else 1
    ar, ac = a.shape[-2], a.shape[-1]
    out_dtype = out_dtype or (BF16 if ta else F32)
    if ta:
        K, M = ar, ac * parts
    else:
        M, K = ar, ac * parts
    if tb:
        N, K2 = b.shape
    else:
        K2, N = b.shape
    assert K == K2, (a.shape, b.shape, ta, tb)
    big = (1024, 1408, 1280, 768, 512, 256, 128)
    tm = tm or _pick(ac if (ta and parts > 1) else M, big)
    tn = tn or (N if N <= 1024 else _pick(N, big))
    kdim = ac if (not ta and parts > 1) else K
    tk = tk or (kdim if kdim <= 2816 and parts == 1 else
                _pick(kdim, (2048, 1408, 1280, 1024, 512) if tm <= 1024 else (1024, 1408, 1280, 512)))
    nk = K // tk
    grid = (M // tm, N // tn, nk)
    if parts > 1 and ta:
        per = ac // tm
        a_spec = pl.BlockSpec((None, tk, tm), lambda i, j, k: (i // per, k, i % per))
    elif parts > 1:
        per = ac // tk
        a_spec = pl.BlockSpec((None, tm, tk), lambda i, j, k: (k // per, i, k % per))
    else:
        a_spec = pl.BlockSpec((tk, tm), lambda i, j, k: (k, i)) if ta else pl.BlockSpec((tm, tk), lambda i, j, k: (i, k))
    b_spec = pl.BlockSpec((tn, tk), lambda i, j, k: (j, k)) if tb else pl.BlockSpec((tk, tn), lambda i, j, k: (k, j))
    o_spec = pl.BlockSpec((tm, tn), lambda i, j, k: (i, j))
    ca, cb = (0 if ta else 1), (1 if tb else 0)
    has_add = add is not None
    n_in = 2 + int(has_add)
    if epilogue is not None:
        assert tn == N and not ta
        ep_fn, ep_rows, ep_whole, ep_outs, ep_accs = epilogue
        n_ep_in = len(ep_rows) + len(ep_whole)
        n_out = len(ep_outs) + len(ep_accs)
    else:
        n_ep_in, n_out = 0, 1

    def body(*refs):
        a_ref, b_ref = refs[0], refs[1]
        add_ref = refs[2] if has_add else None
        o_ref = refs[n_in + n_ep_in]
        first_row_tile = pl.program_id(0) == 0

        def finish(r):
            if has_add:
                r = r + add_ref[...].astype(F32)
            if epilogue is not None:
                ep_fn(r, refs[n_in:n_in + n_ep_in], refs[n_in + n_ep_in:n_in + n_ep_in + n_out], first_row_tile)
            else:
                o_ref[...] = r.astype(out_dtype)

        if nk == 1:
            finish(_dot(a_ref[...], b_ref[...], ca, cb))
            return
        acc_ref = refs[-1]
        k = pl.program_id(2)

        @pl.when(k == 0)
        def _():
            acc_ref[...] = jnp.zeros_like(acc_ref)

        acc_ref[...] += _dot(a_ref[...], b_ref[...], ca, cb)

        @pl.when(k == nk - 1)
        def _():
            finish(acc_ref[...])

    in_specs = [a_spec, b_spec] + ([o_spec] if has_add else [])
    args = (a, b) + ((add,) if has_add else ())
    out_specs, out_shape = o_spec, jax.ShapeDtypeStruct((M, N), out_dtype)
    if epilogue is not None:
        for row_in in ep_rows:
            arr, col_block, width = (tuple(row_in) + (N,))[:3] if isinstance(row_in, tuple) else (row_in, 0, N)
            in_specs.append(pl.BlockSpec((tm, width), lambda i, j, k, _c=col_block: (i, _c)))
            args += (arr,)
        in_specs += [pl.BlockSpec(w.shape, lambda i, j, k, _n=w.ndim: (0,) * _n) for w in ep_whole]
        args += tuple(ep_whole)
        outs = [o if isinstance(o, tuple) else (o, N) for o in ep_outs]
        out_specs = ([pl.BlockSpec((tm, w), lambda i, j, k: (i, 0)) for _, w in outs]
                     + [pl.BlockSpec((r, N), lambda i, j, k: (0, 0)) for r in ep_accs])
        out_shape = [jax.ShapeDtypeStruct((M, w), dt) for dt, w in outs] + [jax.ShapeDtypeStruct((r, N), F32) for r in ep_accs]
    return _pcall(
        body, grid=grid, in_specs=in_specs, out_specs=out_specs, out_shape=out_shape,
        scratch_shapes=[pltpu.VMEM((tm, tn), F32)] if nk > 1 else [], name=name,
        compiler_params=pltpu.CompilerParams(dimension_semantics=("arbitrary" if epilogue is not None else "parallel", "parallel", "arbitrary")),
    )(*args)


def _ln_epilogue(res, c, g, b):
    def fn(r, ins, outs, first):
        res_ref, g_ref, b_ref = ins
        f_ref, h_ref, hb_ref = outs
        h = _ln(ALPHA * res_ref[...] + c * r, g_ref[...], b_ref[...])
        f_ref[...] = r
        h_ref[...] = h
        hb_ref[...] = h.astype(BF16)

    return (fn, [res], [g, b], (F32, F32, BF16), ())


def _ln_bwd_epilogue(res, f, c, g, b):
    def fn(r, ins, outs, first):
        res_ref, f_ref, g_ref, b_ref = ins
        dr_ref, df_ref, dg_ref, db_ref = outs
        pre = ALPHA * res_ref[...] + c * f_ref[...]
        xc = pre - jnp.mean(pre, axis=-1, keepdims=True)
        rstd = lax.rsqrt(jnp.mean(jnp.square(xc), axis=-1, keepdims=True) + EPS)
        xhat = xc * rstd
        dyg = r * g_ref[...]
        dpre = rstd * (dyg - jnp.mean(dyg, axis=-1, keepdims=True) - xhat * jnp.mean(dyg * xhat, axis=-1, keepdims=True))
        dr_ref[...] = ALPHA * dpre
        df_ref[...] = (c * dpre).astype(BF16)
        _accum(dg_ref, jnp.sum(r * xhat, axis=0, keepdims=True), first)
        _accum(db_ref, jnp.sum(r, axis=0, keepdims=True), first)

    return (fn, [res, f], [g, b], (F32, BF16), (1, 1))


def _rows(body, T, tm, ins, outs, name, accs=()):
    in_specs, args = [], []
    for arr, w, cb in ins:
        if w is None:
            in_specs.append(pl.BlockSpec(arr.shape, lambda i, _n=arr.ndim: (0,) * _n))
        else:
            in_specs.append(pl.BlockSpec((tm, w), lambda i, _cb=cb: (i, _cb)))
        args.append(arr)
    out_specs = [pl.BlockSpec((tm, w), lambda i: (i, 0)) for w, _ in outs]
    out_shape = [jax.ShapeDtypeStruct((T, w), dt) for w, dt in outs]
    for r, w in accs:
        out_specs.append(pl.BlockSpec((r, w), lambda i: (0, 0)))
        out_shape.append(jax.ShapeDtypeStruct((r, w), F32))
    return _pcall(
        body, grid=(T // tm,), in_specs=in_specs, out_specs=out_specs, out_shape=out_shape, name=name,
        compiler_params=pltpu.CompilerParams(dimension_semantics=("arbitrary",)),
    )(*args)


FFN_TN = 1408
FFN_TM = 512
LN_TM = 512


def _ffn_in(x, wt, name, order_after=None):
    T = x.shape[0]
    tm, tn = min(FFN_TM, T), FFN_TN
    nj = D_FF // tn
    emit_xb = x.dtype != BF16
    n_in = 3 + int(order_after is not None)

    def body(*refs):
        x_ref, wg_ref, wu_ref = refs[:3]
        hs_ref, gu_ref = refs[n_in], refs[n_in + 1]
        xv = x_ref[...].astype(BF16)
        g = _dot(xv, wg_ref[...], 1, 1)
        u = _dot(xv, wu_ref[...], 1, 1)
        hs_ref[...] = (_silu(g) * u).astype(BF16)
        gu_ref[0] = g.astype(BF16)
        gu_ref[1] = u.astype(BF16)
        if emit_xb:
            refs[n_in + 2][...] = xv

    in_specs = [pl.BlockSpec((tm, D), lambda i, j: (i, 0)), pl.BlockSpec((tn, D), lambda i, j: (j, 0)),
                pl.BlockSpec((tn, D), lambda i, j: (j + nj, 0))]
    args = [x, wt, wt]
    if order_after is not None:
        in_specs.append(pl.BlockSpec(order_after.shape, lambda i, j: (0, 0)))
        args.append(order_after)
    out_specs = [pl.BlockSpec((tm, tn), lambda i, j: (i, j)), pl.BlockSpec((2, tm, tn), lambda i, j: (0, i, j))]
    out_shape = [jax.ShapeDtypeStruct((T, D_FF), BF16), jax.ShapeDtypeStruct((2, T, D_FF), BF16)]
    if emit_xb:
        out_specs.append(pl.BlockSpec((tm, D), lambda i, j: (i, 0)))
        out_shape.append(jax.ShapeDtypeStruct((T, D), BF16))
    return _pcall(
        body, grid=(T // tm, nj), in_specs=in_specs, out_specs=out_specs, out_shape=out_shape, name=name,
        compiler_params=pltpu.CompilerParams(dimension_semantics=("parallel", "arbitrary")),
    )(*args)


def _ffn_act_bwd(df, wo, gu, name):
    T = df.shape[0]
    tm, tn = min(FFN_TM, T), FFN_TN

    def body(d_ref, w_ref, gu_ref, o_ref):
        dhs = _dot(d_ref[...], w_ref[...], 1, 1)
        g, u = gu_ref[0].astype(F32), gu_ref[1].astype(F32)
        sig = _sigmoid(g)
        act = g * sig
        o_ref[0] = (dhs * u * (sig + act * (1.0 - sig))).astype(BF16)
        o_ref[1] = (dhs * act).astype(BF16)

    return _pcall(
        body, grid=(T // tm, D_FF // tn),
        in_specs=[pl.BlockSpec((tm, D), lambda i, j: (i, 0)), pl.BlockSpec((tn, D), lambda i, j: (j, 0)),
                  pl.BlockSpec((2, tm, tn), lambda i, j: (0, i, j))],
        out_specs=pl.BlockSpec((2, tm, tn), lambda i, j: (0, i, j)),
        out_shape=jax.ShapeDtypeStruct((2, T, D_FF), BF16), name=name,
        compiler_params=pltpu.CompilerParams(dimension_semantics=("parallel", "parallel")),
    )(df, wo, gu)


def _rope128(t, cos, sin_s):
    return t * cos + pltpu.roll(t, 64, 1) * sin_s


def _rope128_t(g, cos, sin_s):
    return g * cos - pltpu.roll(g, 64, 1) * sin_s


def _partner32(t):
    lane = lax.broadcasted_iota(jnp.int32, t.shape, 1)
    return jnp.where((lane & 32) == 0, pltpu.roll(t, 96, 1), pltpu.roll(t, 32, 1))


def _rope64(t, cos, sin_s):
    return t * cos + _partner32(t) * sin_s


def _rope64_t(g, cos, sin_s):
    return g * cos - _partner32(g) * sin_s


def _mixer_prep_fwd(proj, tabs, qg, kvg, T):
    cos128, sin128, cos64, sin64 = tabs

    def body(rq_ref, rk_ref, cq_ref, ckv_ref, kpe_ref, c1_ref, s1_ref, c2_ref, s2_ref, qg_ref, kvg_ref,
             oq_ref, ok_ref, oqn_ref, okvn_ref, okpe_ref):
        c1, s1 = c1_ref[...], s1_ref[...]
        for h in range(HEADS):
            sl = slice(h * RET_DK, (h + 1) * RET_DK)
            oq_ref[:, sl] = _rope128(rq_ref[:, sl].astype(F32), c1, s1).astype(BF16)
            ok_ref[:, sl] = (_rope128(rk_ref[:, sl].astype(F32), c1, s1) * RET_SCALE).astype(BF16)
        oqn_ref[...] = _rms(cq_ref[...].astype(F32), qg_ref[...]).astype(BF16)
        okvn_ref[...] = _rms(ckv_ref[...].astype(F32), kvg_ref[...]).astype(BF16)
        okpe_ref[...] = _rope64(kpe_ref[...].astype(F32), c2_ref[...], s2_ref[...]).astype(BF16)

    ins = [(proj, 1024, 0), (proj, 1024, 1), (proj, 256, P_CQ // 256), (proj, 256, P_CKV // 256),
           (proj, 128, P_KPE // 128), (cos128, 128, 0), (sin128, 128, 0), (cos64, 128, 0), (sin64, 128, 0),
           (qg, None, None), (kvg, None, None)]
    outs = [(1024, BF16), (1024, BF16), (LORA, BF16), (LORA, BF16), (128, BF16)]
    return _rows(body, T, 256, ins, outs, "mixer_prep_fwd")


def _rms_bwd(proj, dqn, dkvn, dkpe_all, tabs, qg, kvg, T):
    _, _, cos64, sin64 = tabs

    def body(cq_ref, ckv_ref, dq_ref, dkv_ref, dk_ref, c2_ref, s2_ref, qg_ref, kvg_ref, ocq_ref, ockv_ref, okpe_ref,
             dqg_ref, dkvg_ref):
        _, vjp = jax.vjp(_rms, cq_ref[...].astype(F32), qg_ref[...])
        dx, dg = vjp(dq_ref[...])
        ocq_ref[...] = dx.astype(BF16)
        _accum(dqg_ref, dg)
        _, vjp = jax.vjp(_rms, ckv_ref[...].astype(F32), kvg_ref[...])
        dx, dg = vjp(dkv_ref[...])
        ockv_ref[...] = dx.astype(BF16)
        _accum(dkvg_ref, dg)
        g = dk_ref[:, 0:128]
        for h in range(1, HEADS):
            g = g + dk_ref[:, h * 128:(h + 1) * 128]
        lane = lax.broadcasted_iota(jnp.int32, g.shape, 1)
        g = jnp.where(lane < MLA_ROPE, g, 0.0)
        okpe_ref[...] = _rope64_t(g, c2_ref[...], s2_ref[...]).astype(BF16)

    ins = [(proj, 256, P_CQ // 256), (proj, 256, P_CKV // 256), (dqn, LORA, 0), (dkvn, LORA, 0), (dkpe_all, 1024, 0),
           (cos64, 128, 0), (sin64, 128, 0), (qg, None, None), (kvg, None, None)]
    return _rows(body, T, 256, ins, [(LORA, BF16), (LORA, BF16), (128, BF16)], "rms_bwd", accs=[(1, LORA), (1, LORA)])


def _gn_gate(y, rg, g):
    mu = jnp.mean(y, axis=-1, keepdims=True)
    var = jnp.mean(jnp.square(y - mu), axis=-1, keepdims=True)
    return _silu(rg) * ((y - mu) * lax.rsqrt(var + EPS) * g)


def _q_assemble_epilogue(tabs):
    _, _, cos64, sin64 = tabs

    def fn(q, ins, outs, first):
        c_ref, s_ref = ins
        on_ref, op_ref = outs
        on_ref[...] = (q[:, :1024] * Q_PRESCALE).astype(BF16)
        c, s = c_ref[...], s_ref[...]
        lane = lax.broadcasted_iota(jnp.int32, c.shape, 1)
        for j in range(HEADS // 2):
            r = _rope64(q[:, 1024 + 128 * j:1024 + 128 * (j + 1)], c, s) * Q_PRESCALE
            op_ref[:, 256 * j:256 * j + 128] = jnp.where(lane < 64, r, 0.0).astype(BF16)
            op_ref[:, 256 * j + 128:256 * j + 256] = jnp.where(lane < 64, pltpu.roll(r, 64, 1), 0.0).astype(BF16)

    return (fn, [(cos64, 0, 128), (sin64, 0, 128)], [], ((BF16, 1024), (BF16, 1024)), ())


def _q_assemble_bwd(dqn, dqpe, tabs, T):
    _, _, cos64, sin64 = tabs

    def body(dn_ref, dp_ref, c_ref, s_ref, on_ref, op_ref):
        on_ref[...] = dn_ref[...].astype(BF16)
        c, s = c_ref[...], s_ref[...]
        lane = lax.broadcasted_iota(jnp.int32, c.shape, 1)
        for j in range(HEADS // 2):
            g = jnp.where(lane < 64, dp_ref[:, 256 * j:256 * j + 128], pltpu.roll(dp_ref[:, 256 * j + 128:256 * j + 256], 64, 1))
            op_ref[:, 128 * j:128 * (j + 1)] = _rope64_t(g, c, s).astype(BF16)

    return _rows(body, T, 256, [(dqn, 1024, 0), (dqpe, 1024, 0), (cos64, 128, 0), (sin64, 128, 0)],
                 [(1024, BF16), (512, BF16)], "q_assemble_bwd")


def _mix_fn(gr, gm, yr, ym):
    return _sigmoid(gr) * yr + _sigmoid(gm) * ym


def _mix_epilogue(proj, y_ret):
    def fn(r, ins, outs, first):
        gr_ref, gm_ref, yr_ref = ins
        ym_ref, mix_ref = outs
        ym_ref[...] = r
        mix_ref[...] = _mix_fn(gr_ref[...].astype(F32), gm_ref[...].astype(F32), yr_ref[...], r).astype(BF16)

    return (fn, [(proj, P_GR // D), (proj, P_GM // D), y_ret], [], (F32, BF16), ())


def _mix_bwd_epilogue(proj, y_ret, y_mla):
    def fn(r, ins, outs, first):
        gr_ref, gm_ref, yr_ref, ym_ref = ins
        _, vjp = jax.vjp(_mix_fn, gr_ref[...].astype(F32), gm_ref[...].astype(F32), yr_ref[...], ym_ref[...])
        for o_ref, d in zip(outs, vjp(r)):
            o_ref[...] = d.astype(BF16)

    return (fn, [(proj, P_GR // D), (proj, P_GM // D), y_ret, y_mla], [], (BF16,) * 4, ())


def _head_epilogue(h3, pp, tgt, g, b):
    def fn(r, ins, outs, first):
        h_ref, pp_ref, t_ref, g_ref, b_ref = ins
        dh_ref, dgl_ref, dpp_ref, dg_ref, db_ref, loss_ref = outs

        def f(h, gg, p, lg, lb):
            return _ln(ALPHA * h + _sigmoid(gg) * p, lg, lb)

        h4, vjp = jax.vjp(f, h_ref[...], r, pp_ref[...], g_ref[...], b_ref[...])
        err = h4 - t_ref[...]
        dh, dgl, dpp, dg, db = vjp(err * (1.0 / D))
        dh_ref[...] = dh
        dgl_ref[...] = dgl.astype(BF16)
        dpp_ref[...] = dpp.astype(BF16)
        _accum(dg_ref, dg, first)
        _accum(db_ref, db, first)
        part = 0.5 * jnp.sum(jnp.mean(jnp.square(err), axis=-1, keepdims=True), axis=0, keepdims=True)
        _accum(loss_ref, jnp.broadcast_to(part, loss_ref.shape), first)

    return (fn, [h3, pp, tgt], [g, b], (F32, BF16, BF16), (1, 1, 8))


def _delta_epilogue(o):
    def fn(r, ins, outs, first):
        (o_ref,) = ins
        db_ref, dl_ref = outs
        db_ref[...] = r.astype(BF16)
        for h in range(HEADS):
            sl = slice(h * MLA_DV, (h + 1) * MLA_DV)
            dl = jnp.sum(r[:, sl] * o_ref[:, sl].astype(F32), axis=-1, keepdims=True)
            dl_ref[:, sl] = jnp.broadcast_to(dl, (r.shape[0], MLA_DV))

    return (fn, [o], [], (BF16, F32), ())


def _ret_consts():
    L = RET_L
    lg = jnp.log(1.0 - 2.0 ** (-5.0 - jnp.arange(HEADS, dtype=F32)))[:, None, None]
    idx = jnp.arange(L, dtype=F32)
    ch = jnp.arange(L) // CHUNK
    dist = idx[:, None] - idx[None, :]
    same = (ch[:, None] == ch[None, :])[None]
    earlier = (ch[None, :] < ch[:, None])[None]
    dm = jnp.where(same, jnp.exp(lg * jnp.abs(dist)[None]), jnp.where(earlier, jnp.exp(lg * dist[None]), 0.0))
    xi = jnp.broadcast_to(jnp.exp(lg * (idx + 1.0)[None, :, None]), (HEADS, L, 128))
    zeta = jnp.broadcast_to(jnp.exp(lg * (L - 1.0 - idx)[None, :, None]), (HEADS, L, 128))
    gl = jnp.broadcast_to(jnp.exp(lg * float(L)), (HEADS, 8, 128))
    return dm.astype(F32), xi.astype(F32), zeta.astype(F32), gl.astype(F32)


def _whole(arr):
    return pl.BlockSpec(arr.shape, lambda n, _nd=arr.ndim: (0,) * _nd)


def _ret_fwd(q, k, v, proj, gn_g, consts, T):
    dm, xi, zeta, gl = consts
    L = RET_L
    n_sc = T // L

    def body(q_ref, k_ref, v_ref, rg_ref, g_ref, dm_ref, xi_ref, ze_ref, gl_ref, y_ref, yr_ref, s_ref, st_ref):
        @pl.when(pl.program_id(0) == 0)
        def _():
            st_ref[...] = jnp.zeros_like(st_ref)

        for h in range(HEADS):
            ks, vs = slice(h * RET_DK, (h + 1) * RET_DK), slice(h * RET_DV, (h + 1) * RET_DV)
            qq, kk, vv = q_ref[:, ks], k_ref[:, ks], v_ref[:, vs]
            st = st_ref[h]
            s_ref[h, 0] = st
            p = (_dot(qq, kk, 1, 1) * dm_ref[h]).astype(BF16)
            cross = _dot(qq, st.astype(BF16), 1, 0)
            xi_c = jnp.concatenate([xi_ref[h], xi_ref[h]], axis=1)
            y = _dot(p, vv, 1, 0) + cross * xi_c
            y_ref[:, vs] = y
            yr_ref[:, vs] = _gn_gate(y, rg_ref[:, vs].astype(F32), g_ref[:, vs]).astype(BF16)
            kz = (kk.astype(F32) * ze_ref[h]).astype(BF16)
            gl2 = jnp.concatenate([gl_ref[h, 0:1, :], gl_ref[h, 0:1, :]], axis=1)
            st_ref[h] = st * gl2 + _dot(kz, vv, 0, 0)

    return _pcall(
        body, grid=(n_sc,),
        in_specs=[pl.BlockSpec((L, 1024), lambda n: (n, 0)), pl.BlockSpec((L, 1024), lambda n: (n, 0)),
                  pl.BlockSpec((L, 2048), lambda n: (n, P_RV // 2048)), pl.BlockSpec((L, 2048), lambda n: (n, P_RG // 2048)),
                  _whole(gn_g), _whole(dm), _whole(xi), _whole(zeta), _whole(gl)],
        out_specs=[pl.BlockSpec((L, 2048), lambda n: (n, 0)), pl.BlockSpec((L, 2048), lambda n: (n, 0)),
                   pl.BlockSpec((HEADS, 1, 128, 256), lambda n: (0, n, 0, 0))],
        out_shape=[jax.ShapeDtypeStruct((T, HEADS * RET_DV), F32), jax.ShapeDtypeStruct((T, HEADS * RET_DV), BF16),
                   jax.ShapeDtypeStruct((HEADS, n_sc, 128, 256), F32)],
        scratch_shapes=[pltpu.VMEM((HEADS, 128, 256), F32)], name="ret_fwd",
        compiler_params=pltpu.CompilerParams(dimension_semantics=("arbitrary",)),
    )(q, k, v, proj, gn_g, dm, xi, zeta, gl)


def _ret_bwd(q, k, v, y, proj, gn_g, states, dyr, consts, tabs, T):
    dm, xi, zeta, gl = consts
    cos128, sin128, _, _ = tabs
    L = RET_L
    n_sc = T // L

    def body(q_ref, k_ref, v_ref, y_ref, rg_ref, g_ref, d_ref, s_ref, dm_ref, xi_ref, ze_ref, gl_ref, c_ref, sn_ref,
             dq_ref, dk_ref, dv_ref, drg_ref, dg_ref, gs_ref):
        @pl.when(pl.program_id(0) == 0)
        def _():
            gs_ref[...] = jnp.zeros_like(gs_ref)

        c, sn = c_ref[...], sn_ref[...]
        dgs = []
        for h in range(HEADS):
            ks, vs = slice(h * RET_DK, (h + 1) * RET_DK), slice(h * RET_DV, (h + 1) * RET_DV)
            _, vjp = jax.vjp(_gn_gate, y_ref[:, vs], rg_ref[:, vs].astype(F32), g_ref[:, vs])
            dy, drg, dg = vjp(d_ref[:, vs])
            drg_ref[:, vs] = drg.astype(BF16)
            dgs.append(dg)
            qq, kk, vv, dyy = q_ref[:, ks], k_ref[:, ks], v_ref[:, vs], dy.astype(BF16)
            dmm = dm_ref[h]
            gb = gs_ref[h].astype(BF16)
            sb = s_ref[h, 0].astype(BF16)
            xi_c = jnp.concatenate([xi_ref[h], xi_ref[h]], axis=1)
            pb = (_dot(qq, kk, 1, 1) * dmm).astype(BF16)
            kz = (kk.astype(F32) * ze_ref[h]).astype(BF16)
            dv_ref[:, vs] = (_dot(pb, dyy, 0, 0) + _dot(kz, gb, 1, 0)).astype(BF16)
            da = (_dot(dyy, vv, 1, 1) * dmm).astype(BF16)
            dyx = (dyy.astype(F32) * xi_c).astype(BF16)
            dq = _dot(da, kk, 1, 0) + _dot(dyx, sb, 1, 1)
            dk = _dot(da, qq, 0, 0) + _dot(vv, gb, 1, 1) * ze_ref[h]
            dq_ref[:, ks] = _rope128_t(dq, c, sn).astype(BF16)
            dk_ref[:, ks] = (_rope128_t(dk, c, sn) * RET_SCALE).astype(BF16)
            gl2 = jnp.concatenate([gl_ref[h, 0:1, :], gl_ref[h, 0:1, :]], axis=1)
            gs_ref[h] = gs_ref[h] * gl2 + _dot(qq, dyx, 0, 0)
        _accum(dg_ref, jnp.concatenate(dgs, axis=1))

    rev = lambda n: n_sc - 1 - n
    return _pcall(
        body, grid=(n_sc,),
        in_specs=[pl.BlockSpec((L, 1024), lambda n: (rev(n), 0)), pl.BlockSpec((L, 1024), lambda n: (rev(n), 0)),
                  pl.BlockSpec((L, 2048), lambda n: (rev(n), P_RV // 2048)), pl.BlockSpec((L, 2048), lambda n: (rev(n), 0)),
                  pl.BlockSpec((L, 2048), lambda n: (rev(n), P_RG // 2048)), _whole(gn_g),
                  pl.BlockSpec((L, 2048), lambda n: (rev(n), 0)),
                  pl.BlockSpec((HEADS, 1, 128, 256), lambda n: (0, rev(n), 0, 0)),
                  _whole(dm), _whole(xi), _whole(zeta), _whole(gl),
                  pl.BlockSpec((L, 128), lambda n: (rev(n), 0)), pl.BlockSpec((L, 128), lambda n: (rev(n), 0))],
        out_specs=[pl.BlockSpec((L, 1024), lambda n: (rev(n), 0)), pl.BlockSpec((L, 1024), lambda n: (rev(n), 0)),
                   pl.BlockSpec((L, 2048), lambda n: (rev(n), 0)), pl.BlockSpec((L, 2048), lambda n: (rev(n), 0)),
                   pl.BlockSpec((1, 2048), lambda n: (0, 0))],
        out_shape=[jax.ShapeDtypeStruct((T, 1024), BF16), jax.ShapeDtypeStruct((T, 1024), BF16), jax.ShapeDtypeStruct((T, 2048), BF16),
                   jax.ShapeDtypeStruct((T, 2048), BF16), jax.ShapeDtypeStruct((1, 2048), F32)],
        scratch_shapes=[pltpu.VMEM((HEADS, 128, 256), F32)], name="ret_bwd",
        compiler_params=pltpu.CompilerParams(dimension_semantics=("arbitrary",)),
    )(q, k, v, y, proj, gn_g, dyr, states, dm, xi, zeta, gl, cos128, sin128)


def _diag_mask(nrows, ncols, row0):
    row = row0 + lax.broadcasted_iota(jnp.int32, (nrows, ncols), 0)
    col = lax.broadcasted_iota(jnp.int32, (nrows, ncols), 1)
    return lax.shift_right_logical(col, 6) <= lax.shift_right_logical(row, 6)


def _tri_steps(nb, by_key):
    if by_key:
        pairs = [(i, j) for j in range(nb) for i in range(j, nb)]
    else:
        pairs = [(i, j) for i in range(nb) for j in range(i + 1)]
    return jnp.array([a for a, _ in pairs], jnp.int32), jnp.array([b for _, b in pairs], jnp.int32)


def _attn_fwd(qn, qpe, kv, kpe, T):
    t = min(ATT_TF, T)
    nb = T // t
    ii, jj = _tri_steps(nb, by_key=False)

    hp = ATT_HP
    w = 128 * hp

    def body(ii_ref, jj_ref, qn_ref, qp_ref, kn_ref, kp_ref, v_ref, o_ref, lse_ref, m_sc, l_sc, acc_sc):
        st = pl.program_id(1)
        i, j = ii_ref[st], jj_ref[st]

        @pl.when(j == 0)
        def _():
            m_sc[...] = jnp.full_like(m_sc, NEG)
            l_sc[...] = jnp.zeros_like(l_sc)
            acc_sc[...] = jnp.zeros_like(acc_sc)

        def update(diag):
            kp = kp_ref[...]
            for hh in range(hp):
                sl = slice(128 * hh, 128 * (hh + 1))
                q = jnp.concatenate([qn_ref[:, sl], qp_ref[:, sl]], axis=1)
                k = jnp.concatenate([kn_ref[:, sl], kp], axis=1)
                s = _dot(q, k, 1, 1)
                if diag:
                    s = jnp.where(_diag_mask(t, t, 0), s, NEG)
                m_prev = m_sc[:, sl]
                m_new = jnp.maximum(m_prev, jnp.max(s, axis=1, keepdims=True))
                a = jnp.exp2(m_prev - m_new)
                p = jnp.exp2(s - m_new[:, 0:1])
                l_sc[:, sl] = a * l_sc[:, sl] + jnp.sum(p, axis=1, keepdims=True)
                acc_sc[:, sl] = a * acc_sc[:, sl] + _dot(p.astype(BF16), v_ref[:, sl], 1, 0)
                m_sc[:, sl] = m_new

        @pl.when(j < i)
        def _():
            update(False)

        @pl.when(j == i)
        def _():
            update(True)
            o_ref[...] = (acc_sc[...] / l_sc[...]).astype(BF16)
            lse_ref[...] = m_sc[...] + jnp.log2(l_sc[...])

    qs = pl.BlockSpec((t, w), lambda h, s, ii, jj: (ii[s], h))
    grid_spec = pltpu.PrefetchScalarGridSpec(
        num_scalar_prefetch=2, grid=(HEADS // hp, int(ii.shape[0])),
        in_specs=[qs, qs, pl.BlockSpec((t, w), lambda h, s, ii, jj: (jj[s], h)), pl.BlockSpec((t, 128), lambda h, s, ii, jj: (jj[s], 0)),
                  pl.BlockSpec((t, w), lambda h, s, ii, jj: (jj[s], HEADS // hp + h))],
        out_specs=[qs, qs],
        scratch_shapes=[pltpu.VMEM((t, w), F32), pltpu.VMEM((t, w), F32), pltpu.VMEM((t, w), F32)])
    return _pcall(
        body, grid_spec=grid_spec, out_shape=[jax.ShapeDtypeStruct((T, D), BF16), jax.ShapeDtypeStruct((T, D), F32)], name="attn_fwd",
        compiler_params=pltpu.CompilerParams(dimension_semantics=("arbitrary", "arbitrary")),
    )(ii, jj, qn, qpe, kv, kpe, kv)


def _attn_bwd(qn, qpe, kv, kpe, do, lse, delta, T):
    t = min(ATT_TB, T)
    nb = T // t
    ii, jj = _tri_steps(nb, by_key=True)

    def body(ii_ref, jj_ref, qn_ref, qp_ref, kn_ref, kp_ref, v_ref, do_ref, lse_ref, dl_ref,
             dqn_ref, dqp_ref, dkn_ref, dkp_ref, dv_ref, dk_sc, dv_sc):
        st = pl.program_id(1)
        i, j = ii_ref[st], jj_ref[st]

        @pl.when(st == 0)
        def _():
            dqn_ref[...] = jnp.zeros_like(dqn_ref)
            dqp_ref[...] = jnp.zeros_like(dqp_ref)

        @pl.when(i == j)
        def _():
            dk_sc[...] = jnp.zeros_like(dk_sc)
            dv_sc[...] = jnp.zeros_like(dv_sc)

        def update(diag):
            q = jnp.concatenate([qn_ref[...], qp_ref[...]], axis=1)
            k = jnp.concatenate([kn_ref[...], kp_ref[...]], axis=1)
            dob = do_ref[...]
            s = _dot(q, k, 1, 1)
            if diag:
                s = jnp.where(_diag_mask(t, t, 0), s, NEG)
            p = jnp.exp2(s - lse_ref[:, 0:1])
            dv_sc[...] += _dot(p.astype(BF16), dob, 0, 0)
            dp = _dot(dob, v_ref[...], 1, 1)
            ds = (p * (dp - dl_ref[:, 0:1])).astype(BF16)
            dk_sc[...] += _dot(ds, q, 0, 0)
            dq = _dot(ds, k, 1, 0) * MLA_SCALE
            rows = pl.ds(pl.multiple_of(i * t, t), t)
            dqn_ref[rows, :] += dq[:, :128]
            dqp_ref[rows, :] += dq[:, 128:]

        @pl.when(i > j)
        def _():
            update(False)

        @pl.when(i == j)
        def _():
            update(True)

        @pl.when(i == nb - 1)
        def _():
            dkn_ref[...] = (dk_sc[:, :128] * (1.0 / LOG2E)).astype(BF16)
            dkp_ref[...] = dk_sc[:, 128:] * (1.0 / LOG2E)
            dv_ref[...] = dv_sc[...].astype(BF16)

    qs = pl.BlockSpec((t, 128), lambda h, s, ii, jj: (ii[s], h))
    ks = pl.BlockSpec((t, 128), lambda h, s, ii, jj: (jj[s], h))
    hs = pl.BlockSpec((T, 128), lambda h, s, ii, jj: (0, h))
    grid_spec = pltpu.PrefetchScalarGridSpec(
        num_scalar_prefetch=2, grid=(HEADS, int(ii.shape[0])),
        in_specs=[qs, qs, ks, pl.BlockSpec((t, 128), lambda h, s, ii, jj: (jj[s], 0)),
                  pl.BlockSpec((t, 128), lambda h, s, ii, jj: (jj[s], HEADS + h)), qs, qs, qs],
        out_specs=[hs, hs, ks, ks, ks],
        scratch_shapes=[pltpu.VMEM((t, 256), F32), pltpu.VMEM((t, 128), F32)])
    return _pcall(
        body, grid_spec=grid_spec,
        out_shape=[jax.ShapeDtypeStruct((T, D), F32), jax.ShapeDtypeStruct((T, D), F32), jax.ShapeDtypeStruct((T, D), BF16),
                   jax.ShapeDtypeStruct((T, D), F32), jax.ShapeDtypeStruct((T, D), BF16)], name="attn_bwd",
        compiler_params=pltpu.CompilerParams(dimension_semantics=("arbitrary", "arbitrary")),
    )(ii, jj, qn, qpe, kv, kpe, kv, do, lse, delta)


def _mesh_pos():
    x, y, c = lax.axis_index("x"), lax.axis_index("y"), lax.axis_index("c")
    return x, y, c, 4 * x + 2 * y + c


def _peer(x, y, c, k):
    px, py, pc = (x + ((k >> 2) & 1)) % 2, (y + ((k >> 1) & 1)) % 2, (c + (k & 1)) % 2
    return (px, py, pc), 4 * px + 2 * py + pc


_ANY = pl.BlockSpec(memory_space=pl.ANY)


def _rcopy(src, dst, send_sems, recv_sems, k, to):
    return pltpu.make_async_remote_copy(src_ref=src, dst_ref=dst, send_sem=send_sems.at[k], recv_sem=recv_sems.at[k],
                                        device_id=to, device_id_type=pl.DeviceIdType.MESH)


def _all_gather(wsh, ssh):
    def body(w_ref, s_ref, wall_ref, sall_ref, send_sems, recv_sems, loc_sems):
        x, y, c, me = _mesh_pos()
        sib = (x, y, 1 - c)
        chips = [(1 - x, y), (x, 1 - y), (1 - x, 1 - y)]
        slot = lambda px, py, pc: 4 * px + 2 * py + pc
        loc = [pltpu.make_async_copy(s_ref, sall_ref.at[me], loc_sems.at[0])]
        for cp in loc:
            cp.start()
        sends, fwd_waits = [], []
        for n, (src, dst) in enumerate(((w_ref, wall_ref), (s_ref, sall_ref))):
            o = 7 * n
            sends.append(_rcopy(src, dst.at[me], send_sems, recv_sems, o, sib))
            for j, chip in enumerate(chips):
                sends.append(_rcopy(src, dst.at[me], send_sems, recv_sems, o + 1 + j, (*chip, c)))
        for cp in sends:
            cp.start()
        for n, (src, dst) in enumerate(((w_ref, wall_ref), (s_ref, sall_ref))):
            o = 7 * n
            for j, chip in enumerate(chips):
                got = dst.at[slot(*chip, c)]
                _rcopy(src, got, send_sems, recv_sems, o + 1 + j, sib).wait_recv()
                fw = _rcopy(got, got, send_sems, recv_sems, o + 4 + j, sib)
                fw.start()
                sends.append(fw)
            fwd_waits.append(_rcopy(src, dst.at[slot(x, y, 1 - c)], send_sems, recv_sems, o, sib))
            for j, chip in enumerate(chips):
                fwd_waits.append(_rcopy(src, dst.at[slot(*chip, 1 - c)], send_sems, recv_sems, o + 4 + j, sib))
        for cp in fwd_waits:
            cp.wait_recv()
        for cp in sends:
            cp.wait_send()
        for cp in loc:
            cp.wait()

    wall, sall = _pcall(
        body, in_specs=[_ANY, _ANY], out_specs=[_ANY, _ANY],
        out_shape=[jax.ShapeDtypeStruct((N_DEV,) + wsh.shape, wsh.dtype), jax.ShapeDtypeStruct((N_DEV,) + ssh.shape, ssh.dtype)],
        scratch_shapes=[pltpu.SemaphoreType.DMA((14,)), pltpu.SemaphoreType.DMA((14,)), pltpu.SemaphoreType.DMA((1,))],
        name="all_gather_weights",
    )(wsh, ssh)
    me = 4 * lax.axis_index("x") + 2 * lax.axis_index("y") + lax.axis_index("c")
    return lax.dynamic_update_index_in_dim(wall, wsh, me, axis=0), sall


_HBM = pl.BlockSpec(memory_space=pltpu.HBM)
_SEM = pl.BlockSpec(memory_space=pltpu.SEMAPHORE)
_EFFECT = pltpu.SideEffectType.DATAFLOW_SIDE_EFFECTING


def _other_chips(x, y):
    return [(1 - x, y), (x, 1 - y), (1 - x, 1 - y)]


def _gather_start(wsh, order_after):
    def body(w_ref, land_ref, dep_ref, send_sems, recv_sems, w_thru, land_thru, token):
        x, y, c, me = _mesh_pos()
        _rcopy(w_ref, land_ref.at[me], send_sems, recv_sems, 0, (x, y, 1 - c)).start()
        for j, chip in enumerate(_other_chips(x, y)):
            _rcopy(w_ref, land_ref.at[me], send_sems, recv_sems, 1 + j, (*chip, c)).start()
        token[...] = jnp.zeros_like(token)

    shape = (N_DEV,) + wsh.shape
    land = pltpu.with_memory_space_constraint(lax.empty(shape, wsh.dtype), pltpu.HBM)
    return _pcall(
        body, name="gather_start",
        out_shape=(pltpu.SemaphoreType.DMA((4,)), pltpu.SemaphoreType.DMA((4,)), pltpu.HBM(wsh.shape, wsh.dtype),
                   pltpu.HBM(shape, wsh.dtype), jax.ShapeDtypeStruct((8, 128), F32)),
        in_specs=(_HBM, _HBM, _ANY), out_specs=(_SEM, _SEM, _HBM, _HBM, pl.BlockSpec(memory_space=pltpu.VMEM)),
        input_output_aliases={0: 2, 1: 3}, compiler_params=pltpu.CompilerParams(has_side_effects=_EFFECT),
    )(pltpu.with_memory_space_constraint(wsh, pltpu.HBM), land, order_after)


def _gather_wait(send_sems, recv_sems, w_thru, land_thru, after):
    def body(w_ref, land_ref, send_sems, recv_sems, after_ref, w_out, land_out):
        x, y, c, _ = _mesh_pos()
        senders = [(x, y, 1 - c)] + [(*chip, c) for chip in _other_chips(x, y)]
        for k, (px, py, pc) in enumerate(senders):
            cp = _rcopy(w_ref, land_ref.at[4 * px + 2 * py + pc], send_sems, recv_sems, k, (px, py, pc))
            cp.wait_send()
            cp.wait_recv()

    return _pcall(
        body, name="gather_wait", out_shape=(pltpu.HBM(w_thru.shape, w_thru.dtype), pltpu.HBM(land_thru.shape, land_thru.dtype)),
        in_specs=(_HBM, _HBM, _SEM, _SEM, _ANY), out_specs=(_HBM, _HBM), input_output_aliases={0: 0, 1: 1},
        compiler_params=pltpu.CompilerParams(has_side_effects=_EFFECT),
    )(w_thru, land_thru, send_sems, recv_sems, after)


def _gather_finish(wsh, land):
    def body(land_ref, out_ref, send_sems, recv_sems):
        x, y, c, _ = _mesh_pos()
        sib = (x, y, 1 - c)
        sends, recvs = [], []
        for j, (px, py) in enumerate(_other_chips(x, y)):
            mine, theirs = 4 * px + 2 * py + c, 4 * px + 2 * py + (1 - c)
            sends.append(_rcopy(land_ref.at[mine], out_ref.at[mine], send_sems, recv_sems, j, sib))
            recvs.append(_rcopy(land_ref.at[theirs], out_ref.at[theirs], send_sems, recv_sems, j, sib))
        for cp in sends:
            cp.start()
        for cp in recvs:
            cp.wait_recv()
        for cp in sends:
            cp.wait_send()

    done = _pcall(
        body, in_specs=[_ANY], out_specs=_ANY, out_shape=jax.ShapeDtypeStruct(land.shape, land.dtype),
        input_output_aliases={0: 0},
        scratch_shapes=[pltpu.SemaphoreType.DMA((3,)), pltpu.SemaphoreType.DMA((3,))], name="gather_finish",
    )(land)
    me = 4 * lax.axis_index("x") + 2 * lax.axis_index("y") + lax.axis_index("c")
    return lax.dynamic_update_index_in_dim(done, wsh, me, axis=0)


_ROW_TILES = (400, 368, 352, 256, 128, 16, 8)


def _exchange_sibling(g4, name):
    def body(g_ref, r_ref, send_sems, recv_sems):
        x, y, c, _ = _mesh_pos()
        sib = (x, y, 1 - c)
        cps = [_rcopy(g_ref.at[q, 1 - c], r_ref.at[q], send_sems, recv_sems, q, sib) for q in range(4)]
        for cp in cps:
            cp.start()
        for cp in cps:
            cp.wait()

    n, _, R, C = g4.shape
    return _pcall(
        body, in_specs=[_ANY], out_specs=_ANY, out_shape=jax.ShapeDtypeStruct((n, R, C), g4.dtype),
        scratch_shapes=[pltpu.SemaphoreType.DMA((4,)), pltpu.SemaphoreType.DMA((4,))], name=name,
    )(g4)


def _sum_sibling(g4, rsib, cvec, name):
    n, _, R, C = g4.shape
    tr = _pick(R, _ROW_TILES)

    def body(c_ref, g_ref, r_ref, o_ref):
        o_ref[...] = (g_ref[...].astype(F32) + r_ref[...].astype(F32)).astype(o_ref.dtype)

    grid_spec = pltpu.PrefetchScalarGridSpec(
        num_scalar_prefetch=1, grid=(n, R // tr),
        in_specs=[pl.BlockSpec((None, None, tr, C), lambda q, i, cr: (q, cr[0], i, 0)), pl.BlockSpec((None, tr, C), lambda q, i, cr: (q, i, 0))],
        out_specs=pl.BlockSpec((None, tr, C), lambda q, i, cr: (q, i, 0)))
    return _pcall(body, grid_spec=grid_spec, out_shape=jax.ShapeDtypeStruct((n, R, C), g4.dtype), name=name)(cvec, g4, rsib)


def _chips_start(part, name, order_after=None):
    n_in = 2 + int(order_after is not None)

    def body(*refs):
        p_ref, land_ref = refs[:2]
        send_sems, recv_sems, token = refs[n_in], refs[n_in + 1], refs[n_in + 4]
        x, y, c, _ = _mesh_pos()
        myq = 2 * x + y
        for j, (px, py) in enumerate(_other_chips(x, y)):
            _rcopy(p_ref.at[2 * px + py], land_ref.at[myq], send_sems, recv_sems, j, (px, py, c)).start()
        token[...] = jnp.zeros_like(token)

    land = pltpu.with_memory_space_constraint(lax.empty(part.shape, part.dtype), pltpu.HBM)
    extra = () if order_after is None else (order_after,)
    return _pcall(
        body, name=name,
        out_shape=(pltpu.SemaphoreType.DMA((3,)), pltpu.SemaphoreType.DMA((3,)), pltpu.HBM(part.shape, part.dtype),
                   pltpu.HBM(part.shape, part.dtype), jax.ShapeDtypeStruct((8, 128), F32)),
        in_specs=(_HBM, _HBM) + (_ANY,) * len(extra),
        out_specs=(_SEM, _SEM, _HBM, _HBM, pl.BlockSpec(memory_space=pltpu.VMEM)),
        input_output_aliases={0: 2, 1: 3}, compiler_params=pltpu.CompilerParams(has_side_effects=_EFFECT),
    )(pltpu.with_memory_space_constraint(part, pltpu.HBM), land, *extra)


def _chips_wait(send_sems, recv_sems, p_thru, land_thru, after, name):
    def body(p_ref, land_ref, send_sems, recv_sems, after_ref, p_out, land_out):
        x, y, c, _ = _mesh_pos()
        for j, (px, py) in enumerate(_other_chips(x, y)):
            q = 2 * px + py
            cp = _rcopy(p_ref.at[q], land_ref.at[q], send_sems, recv_sems, j, (px, py, c))
            cp.wait_send()
            cp.wait_recv()

    return _pcall(
        body, name=name, out_shape=(pltpu.HBM(p_thru.shape, p_thru.dtype), pltpu.HBM(p_thru.shape, p_thru.dtype)),
        in_specs=(_HBM, _HBM, _SEM, _SEM, _ANY), out_specs=(_HBM, _HBM), input_output_aliases={0: 0, 1: 1},
        compiler_params=pltpu.CompilerParams(has_side_effects=_EFFECT),
    )(p_thru, land_thru, send_sems, recv_sems, after)


def _sum_chips(part, land, qvec, name):
    n, R, C = part.shape
    tr = _pick(R, _ROW_TILES)

    def body(q_ref, p_ref, *refs):
        o_ref = refs[n]
        acc = None
        for q in range(n):
            term = jnp.where(q_ref[0] == q, p_ref[...], refs[q][...]).astype(F32)
            acc = term if acc is None else acc + term
        o_ref[...] = acc

    def land_spec(q):
        return pl.BlockSpec((None, tr, C), lambda i, m: (jnp.where(m[0] == q, (q + 1) % n, q), i, 0))

    grid_spec = pltpu.PrefetchScalarGridSpec(
        num_scalar_prefetch=1, grid=(R // tr,),
        in_specs=[pl.BlockSpec((None, tr, C), lambda i, m: (m[0], i, 0))] + [land_spec(q) for q in range(n)],
        out_specs=pl.BlockSpec((tr, C), lambda i, m: (i, 0)))
    return _pcall(body, grid_spec=grid_spec, out_shape=jax.ShapeDtypeStruct((R, C), F32), name=name)(qvec, part, *([land] * n))


def _exchange_small(gsmall):
    def body(s_ref, srecv_ref, send_sems, recv_sems, loc_sem):
        x, y, c, me = _mesh_pos()
        loc = pltpu.make_async_copy(s_ref, srecv_ref.at[me], loc_sem.at[0])
        loc.start()
        sends, recvs = [], []
        for k in range(1, N_DEV):
            to, pidx = _peer(x, y, c, k)
            sends.append(_rcopy(s_ref, srecv_ref.at[me], send_sems, recv_sems, k - 1, to))
            recvs.append(_rcopy(s_ref, srecv_ref.at[pidx], send_sems, recv_sems, k - 1, to))
        for cp in sends:
            cp.start()
        for cp in recvs:
            cp.wait_recv()
        for cp in sends:
            cp.wait_send()
        loc.wait()

    return _pcall(
        body, in_specs=[_ANY], out_specs=_ANY, out_shape=jax.ShapeDtypeStruct((N_DEV,) + gsmall.shape, gsmall.dtype),
        scratch_shapes=[pltpu.SemaphoreType.DMA((7,)), pltpu.SemaphoreType.DMA((7,)), pltpu.SemaphoreType.DMA((1,))],
        name="exchange_small",
    )(gsmall)


def _sum_slots(recv, name):
    n, R, C = recv.shape
    tr = _pick(R, _ROW_TILES)

    def body(r_ref, o_ref):
        acc = r_ref[0].astype(F32)
        for s in range(1, n):
            acc = acc + r_ref[s].astype(F32)
        o_ref[...] = acc

    return _pcall(body, grid=(R // tr,), in_specs=[pl.BlockSpec((n, tr, C), lambda i: (0, i, 0))],
                  out_specs=pl.BlockSpec((tr, C), lambda i: (i, 0)), out_shape=jax.ShapeDtypeStruct((R, C), F32), name=name)(recv)


def _adamw(w, g, m, v, name, order_after=None):
    shape = w.shape
    w2, g2, m2, v2 = (a.reshape(-1, shape[-1]) for a in (w, g, m, v))
    R, C = w2.shape
    tr = _pick(R, (256, 128, 64, 32, 16, 8)) if R > 256 else R
    n_in = 4 + int(order_after is not None)

    def body(*refs):
        w_ref, g_ref, m_ref, v_ref = refs[:4]
        d_ref, nm_ref, nv_ref = refs[n_in:]
        gg = g_ref[...]
        nm = ADAM_B1 * m_ref[...] + (1.0 - ADAM_B1) * gg
        nv = ADAM_B2 * v_ref[...] + (1.0 - ADAM_B2) * jnp.square(gg)
        m_hat = nm / (1.0 - ADAM_B1 ** ADAM_STEP)
        v_hat = nv / (1.0 - ADAM_B2 ** ADAM_STEP)
        d_ref[...] = -ADAM_LR * (m_hat / (jnp.sqrt(v_hat) + ADAM_EPS) + ADAM_WD * w_ref[...])
        nm_ref[...] = nm
        nv_ref[...] = nv

    spec = pl.BlockSpec((tr, C), lambda i: (i, 0))
    in_specs, args = [spec] * 4, [w2, g2, m2, v2]
    if order_after is not None:
        in_specs.append(pl.BlockSpec(order_after.shape, lambda i: (0, 0)))
        args.append(order_after)
    d, nm, nv = _pcall(body, grid=(R // tr,), in_specs=in_specs, out_specs=[spec] * 3,
                       out_shape=[jax.ShapeDtypeStruct((R, C), F32)] * 3, name=name)(*args)
    return d.reshape(shape), nm.reshape(shape), nv.reshape(shape)


def _to_rows(name, w):
    w = w[0]
    if name in ("ffn1_w_in", "ffn2_w_in", "w_in"):
        return w.T
    if name in ("w_uq", "w_ukv", "ple_w_proj"):
        return w.T.reshape(-1, 1024)
    return w


def _from_rows(name, g, shape):
    if name in ("ffn1_w_in", "ffn2_w_in", "w_in"):
        return g.T.reshape(shape)
    if name in ("w_uq", "w_ukv", "ple_w_proj"):
        return g.reshape(-1, shape[1]).T.reshape(shape)
    return g.reshape(shape)


def _unpack(wall, group):
    out, off = {}, 0
    for name, r in group:
        out[name] = wall[:, off:off + r, :].reshape(N_DEV * r, 1024)
        off += _pad16(r)
    return out


def _w_in_internal(wt):
    return jnp.concatenate([wt[0:6144], wt[6720:8768], wt[6144:6720], jnp.zeros((P_W - W_IN_COLS, 1024), wt.dtype)], axis=0)


def _w_in_external(d):
    return jnp.concatenate([d[0:6144], d[8192:8768], d[6144:8192]], axis=0)


def _rope_tables(positions):
    pos = positions[0].astype(F32)

    def cs(half):
        inv = ROPE_BASE ** (-jnp.arange(half, dtype=F32) / half)
        ang = pos[:, None] * inv
        return jnp.cos(ang), jnp.sin(ang)

    c, s = cs(64)
    c2, s2 = cs(32)
    return (jnp.concatenate([c, c], axis=1), jnp.concatenate([-s, s], axis=1),
            jnp.concatenate([c2, c2, c2, c2], axis=1), jnp.concatenate([-s2, s2, -s2, s2], axis=1))


def _local_step(x, p, positions, target, W, ln_g, ln_b, gn_g, qg, kvg, early=None, rest_weights=None, start_token=None):
    T = x.shape[0]
    tabs = _rope_tables(positions)
    rc = _ret_consts()
    lg = [ln_g[i:i + 1] for i in range(4)]
    lb = [ln_b[i:i + 1] for i in range(4)]
    pb = p.astype(BF16)

    hs1, gu1, *xb = _ffn_in(x, W["ffn1_w_in"], "ffn1_in", order_after=start_token)
    xb = xb[0] if xb else x
    f1, h1, h1b = _mm(hs1, W["ffn1_w_out"], name="ffn1_out", tm=LN_TM, epilogue=_ln_epilogue(x, 0.5, lg[0], lb[0]))
    if rest_weights is not None:
        W = {**W, **rest_weights(h1b)}
    w_in_t = _w_in_internal(W["w_in"])
    wuq = W["w_uq"].reshape(1536, LORA).reshape(HEADS, 192, LORA)
    wuq = jnp.concatenate([wuq[:, :128].reshape(1024, LORA), wuq[:, 128:].reshape(512, LORA)], axis=0)
    wukv = W["w_ukv"].reshape(2048, LORA).reshape(HEADS, 2, 128, LORA).transpose(1, 0, 2, 3).reshape(2048, LORA)
    wp_t = W["ple_w_proj"].reshape(1024, D_PLE)
    proj = _mm(h1b, w_in_t, tb=True, out_dtype=BF16, name="mixer_in")
    rq, rk, qn, kvn, kpe = _mixer_prep_fwd(proj, tabs, qg, kvg, T)
    y, yr, states = _ret_fwd(rq, rk, proj, proj, gn_g, rc, T)
    y_ret = _mm(yr, W["w_ret_o"], name="ret_o")
    qnope, qpe = _mm(qn, wuq, tb=True, name="mla_uq", tm=LN_TM, tn=1536, epilogue=_q_assemble_epilogue(tabs))
    kv = _mm(kvn, wukv, tb=True, out_dtype=BF16, name="mla_ukv")
    o, lse = _attn_fwd(qnope, qpe, kv, kpe, T)
    y_mla, mix = _mm(o, W["w_mla_o"], name="mla_o", tm=LN_TM, epilogue=_mix_epilogue(proj, y_ret))
    mixed, h2, h2b = _mm(mix, W["w_out"], name="mixer_out", tm=LN_TM, epilogue=_ln_epilogue(h1, 1.0, lg[1], lb[1]))
    hs2, gu2 = _ffn_in(h2b, W["ffn2_w_in"], "ffn2_in")
    f2, h3, h3b = _mm(hs2, W["ffn2_w_out"], name="ffn2_out", tm=LN_TM, epilogue=_ln_epilogue(h2, 0.5, lg[2], lb[2]))
    pp = _mm(pb, wp_t, tb=True, name="ple_proj")

    G = {}
    dh3_a, dgl, dpp, dg3, db3, loss = _mm(h3b, W["ple_w_gate"], name="ple_gate", tm=LN_TM,
                                          epilogue=_head_epilogue(h3, pp, target, lg[3], lb[3]))
    G["ple_w_gate"] = _mm(h3b, dgl, ta=True, name="d_ple_gate")
    G["ple_w_proj"] = _mm(dpp, pb, ta=True, name="d_ple_proj")
    dh2_a, df2, dg2, db2 = _mm(dgl, W["ple_w_gate"], tb=True, add=dh3_a, name="dh3", tm=LN_TM,
                               epilogue=_ln_bwd_epilogue(h2, f2, 0.5, lg[2], lb[2]))
    G["ffn2_w_out"] = _mm(hs2, df2, ta=True, name="d_ffn2_out")
    da2 = _ffn_act_bwd(df2, W["ffn2_w_out"], gu2, "ffn2_act_bwd")
    G["ffn2_w_in"] = _mm(da2, h2b, ta=True, name="d_ffn2_in")
    dh1_a, dmixed, dg1, db1 = _mm(da2, W["ffn2_w_in"], add=dh2_a, name="dh2", tm=LN_TM,
                                  epilogue=_ln_bwd_epilogue(h1, mixed, 1.0, lg[1], lb[1]))
    G["w_out"] = _mm(mix, dmixed, ta=True, name="d_mixer_out")
    dgr, dgm, dy_ret, dy_mla = _mm(dmixed, W["w_out"], tb=True, name="dmix", tm=LN_TM,
                                   epilogue=_mix_bwd_epilogue(proj, y_ret, y_mla))
    G["w_mla_o"] = _mm(o, dy_mla, ta=True, name="d_mla_o")
    dob, delta = _mm(dy_mla, W["w_mla_o"], tb=True, name="do", tm=LN_TM, epilogue=_delta_epilogue(o))
    dqn_f, dqpe_f, dkn, dkpe_all, dv = _attn_bwd(qnope, qpe, kv, kpe, dob, lse, delta, T)
    dq_n, dq_r = _q_assemble_bwd(dqn_f, dqpe_f, tabs, T)
    g_uq = jnp.concatenate([_mm(dq_n, qn, ta=True, name="d_uq_nope"), _mm(dq_r, qn, ta=True, name="d_uq_rope")], axis=0)
    g_uq = jnp.concatenate([g_uq[:1024].reshape(HEADS, 128, LORA), g_uq[1024:].reshape(HEADS, 64, LORA)], axis=1)
    G["w_uq"] = g_uq.reshape(1536 * LORA // 1024, 1024)
    dqn = _mm(dq_r, wuq[1024:], add=_mm(dq_n, wuq[:1024], name="dqn_a"), name="dqn_b")
    g_ukv = jnp.stack([_mm(dkn, kvn, ta=True, name="d_ukv_k"), _mm(dv, kvn, ta=True, name="d_ukv_v")], axis=0)
    G["w_ukv"] = g_ukv.reshape(2, HEADS, 128, LORA).transpose(1, 0, 2, 3).reshape(2048 * LORA // 1024, 1024)
    dkvn = _mm(dv, wukv[1024:], add=_mm(dkn, wukv[:1024], name="dkvn_a"), name="dkvn_b")
    dcq, dckv, dkpe, dqg, dkvg = _rms_bwd(proj, dqn, dkvn, dkpe_all, tabs, qg, kvg, T)
    G["w_ret_o"] = _mm(yr, dy_ret, ta=True, name="d_ret_o")
    dyr = _mm(dy_ret, W["w_ret_o"], tb=True, name="dyr")
    drq, drk, drv, drg, dgn = _ret_bwd(rq, rk, proj, y, proj, gn_g, states, dyr, rc, tabs, T)
    dproj = jnp.concatenate([drq, drk, drv, drg, dgr, dgm, dcq, dckv, dkpe, jnp.zeros((T, P_W - P_KPE - 128), BF16)], axis=1)
    G["w_in"] = _w_in_external(_mm(dproj, h1b, ta=True, name="d_mixer_in"))
    lg0 = lg[0] if early is None else lg[0] + early(G)[0:1, 0:1]
    dx_a, df1, dg0, db0 = _mm(dproj, w_in_t, add=dh1_a, name="dh1", tm=LN_TM,
                              epilogue=_ln_bwd_epilogue(x, f1, 0.5, lg0, lb[0]))
    G["ffn1_w_out"] = _mm(hs1, df1, ta=True, name="d_ffn1_out")
    da1 = _ffn_act_bwd(df1, W["ffn1_w_out"], gu1, "ffn1_act_bwd")
    G["ffn1_w_in"] = _mm(da1, xb, ta=True, name="d_ffn1_in")
    grad_x = _mm(da1, W["ffn1_w_in"], add=dx_a, name="grad_x")

    small = dict(ln_g=jnp.concatenate([dg0, dg1, dg2, dg3], axis=0), ln_b=jnp.concatenate([db0, db1, db2, db3], axis=0),
                 ret_gn_g=dgn, q_norm_g=dqg, kv_norm_g=dkvg)
    return loss, grad_x, G, small


def kernel(x, p, positions, ln_g, ln_b, ffn1_w_in, ffn1_w_out, w_in, ret_gn_g, w_ret_o, q_norm_g, kv_norm_g, w_uq, w_ukv, w_mla_o, w_out, ffn2_w_in, ffn2_w_out, ple_w_gate, ple_w_proj, loss_target, m_ln_g, m_ln_b, m_ffn1_w_in, m_ffn1_w_out, m_w_in, m_ret_gn_g, m_w_ret_o, m_q_norm_g, m_kv_norm_g, m_w_uq, m_w_ukv, m_w_mla_o, m_w_out, m_ffn2_w_in, m_ffn2_w_out, m_ple_w_gate, m_ple_w_proj, v_ln_g, v_ln_b, v_ffn1_w_in, v_ffn1_w_out, v_w_in, v_ret_gn_g, v_w_ret_o, v_q_norm_g, v_kv_norm_g, v_w_uq, v_w_ukv, v_w_mla_o, v_w_out, v_ffn2_w_in, v_ffn2_w_out, v_ple_w_gate, v_ple_w_proj):
    names = ("ln_g", "ln_b", "ffn1_w_in", "ffn1_w_out", "w_in", "ret_gn_g", "w_ret_o", "q_norm_g", "kv_norm_g", "w_uq", "w_ukv",
             "w_mla_o", "w_out", "ffn2_w_in", "ffn2_w_out", "ple_w_gate", "ple_w_proj")
    ws = dict(zip(names, (ln_g, ln_b, ffn1_w_in, ffn1_w_out, w_in, ret_gn_g, w_ret_o, q_norm_g, kv_norm_g, w_uq, w_ukv, w_mla_o,
                          w_out, ffn2_w_in, ffn2_w_out, ple_w_gate, ple_w_proj)))
    ms = dict(zip(names, (m_ln_g, m_ln_b, m_ffn1_w_in, m_ffn1_w_out, m_w_in, m_ret_gn_g, m_w_ret_o, m_q_norm_g, m_kv_norm_g, m_w_uq,
                          m_w_ukv, m_w_mla_o, m_w_out, m_ffn2_w_in, m_ffn2_w_out, m_ple_w_gate, m_ple_w_proj)))
    vs = dict(zip(names, (v_ln_g, v_ln_b, v_ffn1_w_in, v_ffn1_w_out, v_w_in, v_ret_gn_g, v_w_ret_o, v_q_norm_g, v_kv_norm_g, v_w_uq,
                          v_w_ukv, v_w_mla_o, v_w_out, v_ffn2_w_in, v_ffn2_w_out, v_ple_w_gate, v_ple_w_proj)))

    parts = []
    for name, r in PACK:
        rows = _to_rows(name, ws[name])
        if _pad16(r) != r:
            rows = jnp.concatenate([rows, jnp.zeros((_pad16(r) - r, 1024), F32)], axis=0)
        parts.append(rows)
    wsh_first = jnp.concatenate(parts[:len(PACK_LATE)], axis=0).astype(BF16)
    wsh_rest = jnp.concatenate(parts[len(PACK_LATE):], axis=0).astype(BF16)
    ssh = jnp.concatenate([ln_g[0], ln_b[0]], axis=0)
    wall_first, sall = _all_gather(wsh_first, ssh)
    *gather_handles, start_token = _gather_start(wsh_rest, wall_first)
    W = _unpack(wall_first, PACK_LATE)
    ln_full = sall.reshape(N_DEV, 2, 4, 128).transpose(1, 2, 0, 3).reshape(2, 4, 1024)

    def rest_weights(after):
        w_thru, land = _gather_wait(*gather_handles, after)
        return _unpack(_gather_finish(w_thru, land), PACK_EARLY)

    cvec = lax.axis_index("c").astype(jnp.int32).reshape(1)
    qvec = (2 * lax.axis_index("x") + lax.axis_index("y")).astype(jnp.int32).reshape(1)

    def chip_partials(G, group, tag):
        gparts = []
        for name, r in group:
            g = G[name].reshape(N_DEV, r, 1024)
            if _pad16(r) != r:
                g = jnp.concatenate([g, jnp.zeros((N_DEV, _pad16(r) - r, 1024), g.dtype)], axis=1)
            gparts.append(g)
        gfull = jnp.concatenate(gparts, axis=1).astype(BF16)
        g4 = gfull.reshape(4, 2, gfull.shape[1], 1024)
        return _sum_sibling(g4, _exchange_sibling(g4, "exchange_sibling_" + tag), cvec, "sum_sibling_" + tag)

    in_flight = []

    def early(G):
        *handles, token = _chips_start(chip_partials(G, PACK_EARLY, "early"), "chips_start_early")
        in_flight.append(handles)
        return token

    loss_p, grad_x, G, small = _local_step(x[0], p[0, 0], positions, loss_target[0], W, ln_full[0], ln_full[1],
                                           ret_gn_g, q_norm_g, kv_norm_g, early=early, rest_weights=rest_weights,
                                           start_token=start_token)

    part_e, land_e = _chips_wait(*in_flight[0], grad_x, "chips_wait_early")
    gsh_early = _sum_chips(part_e, land_e, qvec, "sum_grads_early")
    pad256 = lambda a: jnp.concatenate([a, jnp.zeros((1, 1024 - a.shape[1]), F32)], axis=1)
    gsmall = jnp.concatenate([small["ln_g"], small["ln_b"], small["ret_gn_g"].reshape(2, 1024), pad256(small["q_norm_g"]),
                              pad256(small["kv_norm_g"]), jnp.zeros((SMALL_ROWS - 12, 1024), F32)], axis=0)
    srecv = _exchange_small(gsmall)
    part_l = chip_partials(G, PACK_LATE, "late")
    *late_handles, late_token = _chips_start(part_l, "chips_start_late", order_after=srecv)
    ssum = _sum_slots(srecv, "sum_small_grads")

    def unpack_grads(group, gsh):
        out, off = {}, 0
        for name, r in group:
            out[name] = _from_rows(name, gsh[off:off + r], ws[name].shape)
            off += _pad16(r)
        return out

    grads = unpack_grads(PACK_EARLY, gsh_early)
    me = 4 * lax.axis_index("x") + 2 * lax.axis_index("y") + lax.axis_index("c")
    grads["ln_g"] = lax.dynamic_slice(ssum[0:4], (0, me * 128), (4, 128)).reshape(1, 4, 128)
    grads["ln_b"] = lax.dynamic_slice(ssum[4:8], (0, me * 128), (4, 128)).reshape(1, 4, 128)
    grads["ret_gn_g"] = ssum[8:10].reshape(1, 2048)
    grads["q_norm_g"] = ssum[10:11, :256]
    grads["kv_norm_g"] = ssum[11:12, :256]

    delta, new_m, new_v = {}, {}, {}
    late_names = [n for n, _ in PACK_LATE]
    for name in names:
        if name not in late_names:
            delta[name], new_m[name], new_v[name] = _adamw(ws[name], grads[name], ms[name], vs[name], "adamw_" + name,
                                                           order_after=late_token)
    part_l, land_l = _chips_wait(*late_handles, new_v["w_in"], "chips_wait_late")
    grads.update(unpack_grads(PACK_LATE, _sum_chips(part_l, land_l, qvec, "sum_grads_late")))
    for name in late_names:
        delta[name], new_m[name], new_v[name] = _adamw(ws[name], grads[name], ms[name], vs[name], "adamw_" + name)

    loss = lax.psum(loss_p[0, 0], ("x", "y", "c"))
    return (loss, grad_x[None], *[grads[n] for n in names], *[delta[n] for n in names],
            *[new_m[n] for n in names], *[new_v[n] for n in names])
```

```python
import functools
import math

import jax
import jax.numpy as jnp
from jax import lax
from jax.experimental import pallas as pl
from jax.experimental.pallas import tpu as pltpu

F32 = jnp.float32
BF16 = jnp.bfloat16

N_DEV = 8
D = 1024
D_FF = 2816
D_PLE = 256
CHUNK = 64
HEADS = 8
RET_DK = 128
RET_DV = 256
MLA_NOPE = 128
MLA_ROPE = 64
MLA_DV = 128
LORA = 256
ROPE_BASE = 10000.0
EPS = 1e-5
ALPHA = 2.0 ** 0.25
RET_SCALE = RET_DK ** -0.5
MLA_SCALE = (MLA_NOPE + MLA_ROPE) ** -0.5
NEG = -1e30

ADAM_LR = 0.001
ADAM_B1 = 0.9
ADAM_B2 = 0.999
ADAM_EPS = 1e-08
ADAM_WD = 0.01
ADAM_STEP = 10

P_RQ, P_RK, P_RV, P_RG, P_GR, P_GM, P_CQ, P_CKV, P_KPE, P_W = 0, 1024, 2048, 4096, 6144, 7168, 8192, 8448, 8704, 8960
W_IN_COLS = 8768
RET_L = 256
ATT_TF = 1024
ATT_TB = 1024
ATT_HP = 4
LOG2E = math.log2(math.e)
Q_PRESCALE = MLA_SCALE * LOG2E

PACK = (("ffn1_w_in", 704), ("ffn1_w_out", 352), ("w_in", 1096), ("w_ret_o", 256), ("w_uq", 48), ("w_ukv", 64),
        ("w_mla_o", 128), ("w_out", 128), ("ffn2_w_in", 704), ("ffn2_w_out", 352), ("ple_w_gate", 128), ("ple_w_proj", 32))


def _pad16(r):
    return -(-r // 16) * 16


PACK_ROWS = sum(_pad16(r) for _, r in PACK)
PACK_LATE = PACK[:2]
PACK_EARLY = PACK[2:]
SMALL_ROWS = 16


def _pcall(body, **kw):
    return pl.pallas_call(body, **kw)


def _pick(dim, prefs):
    for p in prefs:
        if dim % p == 0:
            return p
    return dim


def _sigmoid(x):
    return 1.0 / (1.0 + jnp.exp(-x))


def _silu(x):
    return x * _sigmoid(x)


def _ln(r, g, b):
    mu = jnp.mean(r, axis=-1, keepdims=True)
    var = jnp.mean(jnp.square(r - mu), axis=-1, keepdims=True)
    return (r - mu) * lax.rsqrt(var + EPS) * g + b


def _rms(x, g):
    return x * lax.rsqrt(jnp.mean(jnp.square(x), axis=-1, keepdims=True) + EPS) * g


def _dot(a, b, ca, cb):
    return lax.dot_general(a, b, (((ca,), (cb,)), ((), ())), preferred_element_type=F32)


def _accum(ref, val, first=None):
    @pl.when(pl.program_id(0) == 0 if first is None else first)
    def _():
        ref[...] = jnp.zeros_like(ref)

    ref[...] += val


def _mm(a, b, *, ta=False, tb=False, add=None, out_dtype=None, name, tm=None, tn=None, tk=None, epilogue=None, vmem_limit=None):
    parts = a.shape[0] if a.ndim == 3 else 1
    ar, ac = a.shape[-2], a.shape[-1]
    out_dtype = out_dtype or (BF16 if ta else F32)
    if ta:
        K, M = ar, ac * parts
    else:
        M, K = ar, ac * parts
    if tb:
        N, K2 = b.shape
    else:
        K2, N = b.shape
    assert K == K2, (a.shape, b.shape, ta, tb)
    big = (1024, 1408, 1280, 768, 512, 256, 128)
    tm = tm or _pick(ac if (ta and parts > 1) else M, big)
    tn = tn or (N if N <= 1024 else _pick(N, big))
    kdim = ac if (not ta and parts > 1) else K
    tk = tk or (kdim if kdim <= 2816 and parts == 1 else
                _pick(kdim, (2048, 1408, 1280, 1024, 512) if tm <= 1024 else (1024, 1408, 1280, 512)))
    nk = K // tk
    grid = (M // tm, N // tn, nk)
    if parts > 1 and ta:
        per = ac // tm
        a_spec = pl.BlockSpec((None, tk, tm), lambda i, j, k: (i // per, k, i % per))
    elif parts > 1:
        per = ac // tk
        a_spec = pl.BlockSpec((None, tm, tk), lambda i, j, k: (k // per, i, k % per))
    else:
        a_spec = pl.BlockSpec((tk, tm), lambda i, j, k: (k, i)) if ta else pl.BlockSpec((tm, tk), lambda i, j, k: (i, k))
    b_spec = pl.BlockSpec((tn, tk), lambda i, j, k: (j, k)) if tb else pl.BlockSpec((tk, tn), lambda i, j, k: (k, j))
    o_spec = pl.BlockSpec((tm, tn), lambda i, j, k: (i, j))
    ca, cb = (0 if ta else 1), (1 if tb else 0)
    has_add = add is not None
    n_in = 2 + int(has_add)
    if epilogue is not None:
        assert tn == N and not ta
        ep_fn, ep_rows, ep_whole, ep_outs, ep_accs = epilogue
        n_ep_in = len(ep_rows) + len(ep_whole)
        n_out = len(ep_outs) + len(ep_accs)
    else:
        n_ep_in, n_out = 0, 1

    def body(*refs):
        a_ref, b_ref = refs[0], refs[1]
        add_ref = refs[2] if has_add else None
        o_ref = refs[n_in + n_ep_in]
        first_row_tile = pl.program_id(0) == 0

        def finish(r):
            if has_add:
                r = r + add_ref[...].astype(F32)
            if epilogue is not None:
                ep_fn(r, refs[n_in:n_in + n_ep_in], refs[n_in + n_ep_in:n_in + n_ep_in + n_out], first_row_tile)
            else:
                o_ref[...] = r.astype(out_dtype)

        if nk == 1:
            finish(_dot(a_ref[...], b_ref[...], ca, cb))
            return
        acc_ref = refs[-1]
        k = pl.program_id(2)

        @pl.when(k == 0)
        def _():
            acc_ref[...] = jnp.zeros_like(acc_ref)

        acc_ref[...] += _dot(a_ref[...], b_ref[...], ca, cb)

        @pl.when(k == nk - 1)
        def _():
            finish(acc_ref[...])

    in_specs = [a_spec, b_spec] + ([o_spec] if has_add else [])
    args = (a, b) + ((add,) if has_add else ())
    out_specs, out_shape = o_spec, jax.ShapeDtypeStruct((M, N), out_dtype)
    if epilogue is not None:
        for row_in in ep_rows:
            arr, col_block, width = (tuple(row_in) + (N,))[:3] if isinstance(row_in, tuple) else (row_in, 0, N)
            in_specs.append(pl.BlockSpec((tm, width), lambda i, j, k, _c=col_block: (i, _c)))
            args += (arr,)
        in_specs += [pl.BlockSpec(w.shape, lambda i, j, k, _n=w.ndim: (0,) * _n) for w in ep_whole]
        args += tuple(ep_whole)
        outs = [o if isinstance(o, tuple) else (o, N) for o in ep_outs]
        out_specs = ([pl.BlockSpec((tm, w), lambda i, j, k: (i, 0)) for _, w in outs]
                     + [pl.BlockSpec((r, N), lambda i, j, k: (0, 0)) for r in ep_accs])
        out_shape = [jax.ShapeDtypeStruct((M, w), dt) for dt, w in outs] + [jax.ShapeDtypeStruct((r, N), F32) for r in ep_accs]
    return _pcall(
        body, grid=grid, in_specs=in_specs, out_specs=out_specs, out_shape=out_shape,
        scratch_shapes=[pltpu.VMEM((tm, tn), F32)] if nk > 1 else [], name=name,
        compiler_params=pltpu.CompilerParams(
            dimension_semantics=("arbitrary" if epilogue is not None else "parallel", "parallel", "arbitrary"), vmem_limit_bytes=vmem_limit),
    )(*args)


def _ln_epilogue(res, c, g, b):
    def fn(r, ins, outs, first):
        res_ref, g_ref, b_ref = ins
        f_ref, h_ref, hb_ref = outs
        h = _ln(ALPHA * res_ref[...] + c * r, g_ref[...], b_ref[...])
        f_ref[...] = r
        h_ref[...] = h
        hb_ref[...] = h.astype(BF16)

    return (fn, [res], [g, b], (F32, F32, BF16), ())


def _ln_bwd_epilogue(res, f, c, g, b):
    def fn(r, ins, outs, first):
        res_ref, f_ref, g_ref, b_ref = ins
        dr_ref, df_ref, dg_ref, db_ref = outs
        pre = ALPHA * res_ref[...] + c * f_ref[...]
        xc = pre - jnp.mean(pre, axis=-1, keepdims=True)
        rstd = lax.rsqrt(jnp.mean(jnp.square(xc), axis=-1, keepdims=True) + EPS)
        xhat = xc * rstd
        dyg = r * g_ref[...]
        dpre = rstd * (dyg - jnp.mean(dyg, axis=-1, keepdims=True) - xhat * jnp.mean(dyg * xhat, axis=-1, keepdims=True))
        dr_ref[...] = ALPHA * dpre
        df_ref[...] = (c * dpre).astype(BF16)
        _accum(dg_ref, jnp.sum(r * xhat, axis=0, keepdims=True), first)
        _accum(db_ref, jnp.sum(r, axis=0, keepdims=True), first)

    return (fn, [res, f], [g, b], (F32, BF16), (1, 1))


def _rows(body, T, tm, ins, outs, name, accs=()):
    in_specs, args = [], []
    for arr, w, cb in ins:
        if w is None:
            in_specs.append(pl.BlockSpec(arr.shape, lambda i, _n=arr.ndim: (0,) * _n))
        else:
            in_specs.append(pl.BlockSpec((tm, w), lambda i, _cb=cb: (i, _cb)))
        args.append(arr)
    out_specs = [pl.BlockSpec((tm, w), lambda i: (i, 0)) for w, _ in outs]
    out_shape = [jax.ShapeDtypeStruct((T, w), dt) for w, dt in outs]
    for r, w in accs:
        out_specs.append(pl.BlockSpec((r, w), lambda i: (0, 0)))
        out_shape.append(jax.ShapeDtypeStruct((r, w), F32))
    return _pcall(
        body, grid=(T // tm,), in_specs=in_specs, out_specs=out_specs, out_shape=out_shape, name=name,
        compiler_params=pltpu.CompilerParams(dimension_semantics=("arbitrary",)),
    )(*args)


FFN_TN = 1408
FFN_TM = 512
LN_TM = 512
VMEM_NEAR_FULL = 62 * 1024 * 1024


def _ffn_in(x, wt, name, order_after=None):
    T = x.shape[0]
    tm, tn = min(FFN_TM, T), FFN_TN
    nj = D_FF // tn
    emit_xb = x.dtype != BF16
    n_in = 3 + int(order_after is not None)

    def body(*refs):
        x_ref, wg_ref, wu_ref = refs[:3]
        hs_ref, gu_ref = refs[n_in], refs[n_in + 1]
        xv = x_ref[...].astype(BF16)
        g = _dot(xv, wg_ref[...], 1, 1)
        u = _dot(xv, wu_ref[...], 1, 1)
        hs_ref[...] = (_silu(g) * u).astype(BF16)
        gu_ref[0] = g.astype(BF16)
        gu_ref[1] = u.astype(BF16)
        if emit_xb:
            refs[n_in + 2][...] = xv

    in_specs = [pl.BlockSpec((tm, D), lambda i, j: (i, 0)), pl.BlockSpec((tn, D), lambda i, j: (j, 0)),
                pl.BlockSpec((tn, D), lambda i, j: (j + nj, 0))]
    args = [x, wt, wt]
    if order_after is not None:
        in_specs.append(pl.BlockSpec(order_after.shape, lambda i, j: (0, 0)))
        args.append(order_after)
    out_specs = [pl.BlockSpec((tm, tn), lambda i, j: (i, j)), pl.BlockSpec((2, tm, tn), lambda i, j: (0, i, j))]
    out_shape = [jax.ShapeDtypeStruct((T, D_FF), BF16), jax.ShapeDtypeStruct((2, T, D_FF), BF16)]
    if emit_xb:
        out_specs.append(pl.BlockSpec((tm, D), lambda i, j: (i, 0)))
        out_shape.append(jax.ShapeDtypeStruct((T, D), BF16))
    return _pcall(
        body, grid=(T // tm, nj), in_specs=in_specs, out_specs=out_specs, out_shape=out_shape, name=name,
        compiler_params=pltpu.CompilerParams(dimension_semantics=("parallel", "arbitrary")),
    )(*args)


def _ffn_act_bwd(df, wo, gu, name):
    T = df.shape[0]
    tm, tn = min(FFN_TM, T), FFN_TN

    def body(d_ref, w_ref, gu_ref, o_ref):
        dhs = _dot(d_ref[...], w_ref[...], 1, 1)
        g, u = gu_ref[0].astype(F32), gu_ref[1].astype(F32)
        sig = _sigmoid(g)
        act = g * sig
        o_ref[0] = (dhs * u * (sig + act * (1.0 - sig))).astype(BF16)
        o_ref[1] = (dhs * act).astype(BF16)

    return _pcall(
        body, grid=(T // tm, D_FF // tn),
        in_specs=[pl.BlockSpec((tm, D), lambda i, j: (i, 0)), pl.BlockSpec((tn, D), lambda i, j: (j, 0)),
                  pl.BlockSpec((2, tm, tn), lambda i, j: (0, i, j))],
        out_specs=pl.BlockSpec((2, tm, tn), lambda i, j: (0, i, j)),
        out_shape=jax.ShapeDtypeStruct((2, T, D_FF), BF16), name=name,
        compiler_params=pltpu.CompilerParams(dimension_semantics=("parallel", "parallel")),
    )(df, wo, gu)


def _rope128(t, cos, sin_s):
    return t * cos + pltpu.roll(t, 64, 1) * sin_s


def _rope128_t(g, cos, sin_s):
    return g * cos - pltpu.roll(g, 64, 1) * sin_s


def _partner32(t):
    lane = lax.broadcasted_iota(jnp.int32, t.shape, 1)
    return jnp.where((lane & 32) == 0, pltpu.roll(t, 96, 1), pltpu.roll(t, 32, 1))


def _rope64(t, cos, sin_s):
    return t * cos + _partner32(t) * sin_s


def _rope64_t(g, cos, sin_s):
    return g * cos - _partner32(g) * sin_s


def _mixer_prep_fwd(proj, tabs, qg, kvg, T):
    cos128, sin128, cos64, sin64 = tabs

    def body(rq_ref, rk_ref, cq_ref, ckv_ref, kpe_ref, c1_ref, s1_ref, c2_ref, s2_ref, qg_ref, kvg_ref,
             oq_ref, ok_ref, oqn_ref, okvn_ref, okpe_ref):
        c1, s1 = c1_ref[...], s1_ref[...]
        for h in range(HEADS):
            sl = slice(h * RET_DK, (h + 1) * RET_DK)
            oq_ref[:, sl] = _rope128(rq_ref[:, sl].astype(F32), c1, s1).astype(BF16)
            ok_ref[:, sl] = (_rope128(rk_ref[:, sl].astype(F32), c1, s1) * RET_SCALE).astype(BF16)
        oqn_ref[...] = _rms(cq_ref[...].astype(F32), qg_ref[...]).astype(BF16)
        okvn_ref[...] = _rms(ckv_ref[...].astype(F32), kvg_ref[...]).astype(BF16)
        okpe_ref[...] = _rope64(kpe_ref[...].astype(F32), c2_ref[...], s2_ref[...]).astype(BF16)

    ins = [(proj, 1024, 0), (proj, 1024, 1), (proj, 256, P_CQ // 256), (proj, 256, P_CKV // 256),
           (proj, 128, P_KPE // 128), (cos128, 128, 0), (sin128, 128, 0), (cos64, 128, 0), (sin64, 128, 0),
           (qg, None, None), (kvg, None, None)]
    outs = [(1024, BF16), (1024, BF16), (LORA, BF16), (LORA, BF16), (128, BF16)]
    return _rows(body, T, 256, ins, outs, "mixer_prep_fwd")


def _rms_bwd(proj, dqn, dkvn, dkpe_all, tabs, qg, kvg, T):
    _, _, cos64, sin64 = tabs

    def body(cq_ref, ckv_ref, dq_ref, dkv_ref, dk_ref, c2_ref, s2_ref, qg_ref, kvg_ref, ocq_ref, ockv_ref, okpe_ref,
             dqg_ref, dkvg_ref):
        _, vjp = jax.vjp(_rms, cq_ref[...].astype(F32), qg_ref[...])
        dx, dg = vjp(dq_ref[...])
        ocq_ref[...] = dx.astype(BF16)
        _accum(dqg_ref, dg)
        _, vjp = jax.vjp(_rms, ckv_ref[...].astype(F32), kvg_ref[...])
        dx, dg = vjp(dkv_ref[...])
        ockv_ref[...] = dx.astype(BF16)
        _accum(dkvg_ref, dg)
        g = dk_ref[:, 0:128]
        for h in range(1, HEADS):
            g = g + dk_ref[:, h * 128:(h + 1) * 128]
        lane = lax.broadcasted_iota(jnp.int32, g.shape, 1)
        g = jnp.where(lane < MLA_ROPE, g, 0.0)
        okpe_ref[...] = _rope64_t(g, c2_ref[...], s2_ref[...]).astype(BF16)

    ins = [(proj, 256, P_CQ // 256), (proj, 256, P_CKV // 256), (dqn, LORA, 0), (dkvn, LORA, 0), (dkpe_all, 1024, 0),
           (cos64, 128, 0), (sin64, 128, 0), (qg, None, None), (kvg, None, None)]
    return _rows(body, T, 256, ins, [(LORA, BF16), (LORA, BF16), (128, BF16)], "rms_bwd", accs=[(1, LORA), (1, LORA)])


def _gn_gate(y, rg, g):
    mu = jnp.mean(y, axis=-1, keepdims=True)
    var = jnp.mean(jnp.square(y - mu), axis=-1, keepdims=True)
    return _silu(rg) * ((y - mu) * lax.rsqrt(var + EPS) * g)


def _q_assemble_epilogue(tabs):
    _, _, cos64, sin64 = tabs

    def fn(q, ins, outs, first):
        c_ref, s_ref = ins
        on_ref, op_ref = outs
        on_ref[...] = (q[:, :1024] * Q_PRESCALE).astype(BF16)
        c, s = c_ref[...], s_ref[...]
        lane = lax.broadcasted_iota(jnp.int32, c.shape, 1)
        for j in range(HEADS // 2):
            r = _rope64(q[:, 1024 + 128 * j:1024 + 128 * (j + 1)], c, s) * Q_PRESCALE
            op_ref[:, 256 * j:256 * j + 128] = jnp.where(lane < 64, r, 0.0).astype(BF16)
            op_ref[:, 256 * j + 128:256 * j + 256] = jnp.where(lane < 64, pltpu.roll(r, 64, 1), 0.0).astype(BF16)

    return (fn, [(cos64, 0, 128), (sin64, 0, 128)], [], ((BF16, 1024), (BF16, 1024)), ())


def _q_assemble_bwd(dqn, dqpe, tabs, T):
    _, _, cos64, sin64 = tabs

    def body(dn_ref, dp_ref, c_ref, s_ref, on_ref, op_ref):
        on_ref[...] = dn_ref[...].astype(BF16)
        c, s = c_ref[...], s_ref[...]
        lane = lax.broadcasted_iota(jnp.int32, c.shape, 1)
        for j in range(HEADS // 2):
            g = jnp.where(lane < 64, dp_ref[:, 256 * j:256 * j + 128], pltpu.roll(dp_ref[:, 256 * j + 128:256 * j + 256], 64, 1))
            op_ref[:, 128 * j:128 * (j + 1)] = _rope64_t(g, c, s).astype(BF16)

    return _rows(body, T, 256, [(dqn, 1024, 0), (dqpe, 1024, 0), (cos64, 128, 0), (sin64, 128, 0)],
                 [(1024, BF16), (512, BF16)], "q_assemble_bwd")


def _mix_fn(gr, gm, yr, ym):
    return _sigmoid(gr) * yr + _sigmoid(gm) * ym


def _mix_epilogue(proj, y_ret):
    def fn(r, ins, outs, first):
        gr_ref, gm_ref, yr_ref = ins
        ym_ref, mix_ref = outs
        ym_ref[...] = r
        mix_ref[...] = _mix_fn(gr_ref[...].astype(F32), gm_ref[...].astype(F32), yr_ref[...], r).astype(BF16)

    return (fn, [(proj, P_GR // D), (proj, P_GM // D), y_ret], [], (F32, BF16), ())


def _mix_bwd_epilogue(proj, y_ret, y_mla):
    def fn(r, ins, outs, first):
        gr_ref, gm_ref, yr_ref, ym_ref = ins
        _, vjp = jax.vjp(_mix_fn, gr_ref[...].astype(F32), gm_ref[...].astype(F32), yr_ref[...], ym_ref[...])
        for o_ref, d in zip(outs, vjp(r)):
            o_ref[...] = d.astype(BF16)

    return (fn, [(proj, P_GR // D), (proj, P_GM // D), y_ret, y_mla], [], (BF16,) * 4, ())


def _head_epilogue(h3, pp, tgt, g, b):
    def fn(r, ins, outs, first):
        h_ref, pp_ref, t_ref, g_ref, b_ref = ins
        dh_ref, dgl_ref, dpp_ref, dg_ref, db_ref, loss_ref = outs

        def f(h, gg, p, lg, lb):
            return _ln(ALPHA * h + _sigmoid(gg) * p, lg, lb)

        h4, vjp = jax.vjp(f, h_ref[...], r, pp_ref[...], g_ref[...], b_ref[...])
        err = h4 - t_ref[...]
        dh, dgl, dpp, dg, db = vjp(err * (1.0 / D))
        dh_ref[...] = dh
        dgl_ref[...] = dgl.astype(BF16)
        dpp_ref[...] = dpp.astype(BF16)
        _accum(dg_ref, dg, first)
        _accum(db_ref, db, first)
        part = 0.5 * jnp.sum(jnp.mean(jnp.square(err), axis=-1, keepdims=True), axis=0, keepdims=True)
        _accum(loss_ref, jnp.broadcast_to(part, loss_ref.shape), first)

    return (fn, [h3, pp, tgt], [g, b], (F32, BF16, BF16), (1, 1, 8))


def _delta_epilogue(o):
    def fn(r, ins, outs, first):
        (o_ref,) = ins
        db_ref, dl_ref = outs
        db_ref[...] = r.astype(BF16)
        for h in range(HEADS):
            sl = slice(h * MLA_DV, (h + 1) * MLA_DV)
            dl = jnp.sum(r[:, sl] * o_ref[:, sl].astype(F32), axis=-1, keepdims=True)
            dl_ref[:, sl] = jnp.broadcast_to(dl, (r.shape[0], MLA_DV))

    return (fn, [o], [], (BF16, F32), ())


def _ret_consts():
    L = RET_L
    lg = jnp.log(1.0 - 2.0 ** (-5.0 - jnp.arange(HEADS, dtype=F32)))[:, None, None]
    idx = jnp.arange(L, dtype=F32)
    ch = jnp.arange(L) // CHUNK
    dist = idx[:, None] - idx[None, :]
    same = (ch[:, None] == ch[None, :])[None]
    earlier = (ch[None, :] < ch[:, None])[None]
    dm = jnp.where(same, jnp.exp(lg * jnp.abs(dist)[None]), jnp.where(earlier, jnp.exp(lg * dist[None]), 0.0))
    xi = jnp.broadcast_to(jnp.exp(lg * (idx + 1.0)[None, :, None]), (HEADS, L, 128))
    zeta = jnp.broadcast_to(jnp.exp(lg * (L - 1.0 - idx)[None, :, None]), (HEADS, L, 128))
    gl = jnp.broadcast_to(jnp.exp(lg * float(L)), (HEADS, 8, 128))
    return dm.astype(F32), xi.astype(F32), zeta.astype(F32), gl.astype(F32)


def _whole(arr):
    return pl.BlockSpec(arr.shape, lambda n, _nd=arr.ndim: (0,) * _nd)


def _ret_fwd(q, k, v, proj, gn_g, consts, T):
    dm, xi, zeta, gl = consts
    L = RET_L
    n_sc = T // L

    def body(q_ref, k_ref, v_ref, rg_ref, g_ref, dm_ref, xi_ref, ze_ref, gl_ref, y_ref, yr_ref, s_ref, st_ref):
        @pl.when(pl.program_id(0) == 0)
        def _():
            st_ref[...] = jnp.zeros_like(st_ref)

        for h in range(HEADS):
            ks, vs = slice(h * RET_DK, (h + 1) * RET_DK), slice(h * RET_DV, (h + 1) * RET_DV)
            qq, kk, vv = q_ref[:, ks], k_ref[:, ks], v_ref[:, vs]
            st = st_ref[h]
            s_ref[h, 0] = st
            p = (_dot(qq, kk, 1, 1) * dm_ref[h]).astype(BF16)
            cross = _dot(qq, st.astype(BF16), 1, 0)
            xi_c = jnp.concatenate([xi_ref[h], xi_ref[h]], axis=1)
            y = _dot(p, vv, 1, 0) + cross * xi_c
            y_ref[:, vs] = y
            yr_ref[:, vs] = _gn_gate(y, rg_ref[:, vs].astype(F32), g_ref[:, vs]).astype(BF16)
            kz = (kk.astype(F32) * ze_ref[h]).astype(BF16)
            gl2 = jnp.concatenate([gl_ref[h, 0:1, :], gl_ref[h, 0:1, :]], axis=1)
            st_ref[h] = st * gl2 + _dot(kz, vv, 0, 0)

    return _pcall(
        body, grid=(n_sc,),
        in_specs=[pl.BlockSpec((L, 1024), lambda n: (n, 0)), pl.BlockSpec((L, 1024), lambda n: (n, 0)),
                  pl.BlockSpec((L, 2048), lambda n: (n, P_RV // 2048)), pl.BlockSpec((L, 2048), lambda n: (n, P_RG // 2048)),
                  _whole(gn_g), _whole(dm), _whole(xi), _whole(zeta), _whole(gl)],
        out_specs=[pl.BlockSpec((L, 2048), lambda n: (n, 0)), pl.BlockSpec((L, 2048), lambda n: (n, 0)),
                   pl.BlockSpec((HEADS, 1, 128, 256), lambda n: (0, n, 0, 0))],
        out_shape=[jax.ShapeDtypeStruct((T, HEADS * RET_DV), F32), jax.ShapeDtypeStruct((T, HEADS * RET_DV), BF16),
                   jax.ShapeDtypeStruct((HEADS, n_sc, 128, 256), F32)],
        scratch_shapes=[pltpu.VMEM((HEADS, 128, 256), F32)], name="ret_fwd",
        compiler_params=pltpu.CompilerParams(dimension_semantics=("arbitrary",)),
    )(q, k, v, proj, gn_g, dm, xi, zeta, gl)


def _ret_bwd(q, k, v, y, proj, gn_g, states, dyr, consts, tabs, T):
    dm, xi, zeta, gl = consts
    cos128, sin128, _, _ = tabs
    L = RET_L
    n_sc = T // L

    def body(q_ref, k_ref, v_ref, y_ref, rg_ref, g_ref, d_ref, s_ref, dm_ref, xi_ref, ze_ref, gl_ref, c_ref, sn_ref,
             dq_ref, dk_ref, dv_ref, drg_ref, dg_ref, gs_ref):
        @pl.when(pl.program_id(0) == 0)
        def _():
            gs_ref[...] = jnp.zeros_like(gs_ref)

        c, sn = c_ref[...], sn_ref[...]
        dgs = []
        for h in range(HEADS):
            ks, vs = slice(h * RET_DK, (h + 1) * RET_DK), slice(h * RET_DV, (h + 1) * RET_DV)
            _, vjp = jax.vjp(_gn_gate, y_ref[:, vs], rg_ref[:, vs].astype(F32), g_ref[:, vs])
            dy, drg, dg = vjp(d_ref[:, vs])
            drg_ref[:, vs] = drg.astype(BF16)
            dgs.append(dg)
            qq, kk, vv, dyy = q_ref[:, ks], k_ref[:, ks], v_ref[:, vs], dy.astype(BF16)
            dmm = dm_ref[h]
            gb = gs_ref[h].astype(BF16)
            sb = s_ref[h, 0].astype(BF16)
            xi_c = jnp.concatenate([xi_ref[h], xi_ref[h]], axis=1)
            pb = (_dot(qq, kk, 1, 1) * dmm).astype(BF16)
            kz = (kk.astype(F32) * ze_ref[h]).astype(BF16)
            dv_ref[:, vs] = (_dot(pb, dyy, 0, 0) + _dot(kz, gb, 1, 0)).astype(BF16)
            da = (_dot(dyy, vv, 1, 1) * dmm).astype(BF16)
            dyx = (dyy.astype(F32) * xi_c).astype(BF16)
            dq = _dot(da, kk, 1, 0) + _dot(dyx, sb, 1, 1)
            dk = _dot(da, qq, 0, 0) + _dot(vv, gb, 1, 1) * ze_ref[h]
            dq_ref[:, ks] = _rope128_t(dq, c, sn).astype(BF16)
            dk_ref[:, ks] = (_rope128_t(dk, c, sn) * RET_SCALE).astype(BF16)
            gl2 = jnp.concatenate([gl_ref[h, 0:1, :], gl_ref[h, 0:1, :]], axis=1)
            gs_ref[h] = gs_ref[h] * gl2 + _dot(qq, dyx, 0, 0)
        _accum(dg_ref, jnp.concatenate(dgs, axis=1))

    rev = lambda n: n_sc - 1 - n
    return _pcall(
        body, grid=(n_sc,),
        in_specs=[pl.BlockSpec((L, 1024), lambda n: (rev(n), 0)), pl.BlockSpec((L, 1024), lambda n: (rev(n), 0)),
                  pl.BlockSpec((L, 2048), lambda n: (rev(n), P_RV // 2048)), pl.BlockSpec((L, 2048), lambda n: (rev(n), 0)),
                  pl.BlockSpec((L, 2048), lambda n: (rev(n), P_RG // 2048)), _whole(gn_g),
                  pl.BlockSpec((L, 2048), lambda n: (rev(n), 0)),
                  pl.BlockSpec((HEADS, 1, 128, 256), lambda n: (0, rev(n), 0, 0)),
                  _whole(dm), _whole(xi), _whole(zeta), _whole(gl),
                  pl.BlockSpec((L, 128), lambda n: (rev(n), 0)), pl.BlockSpec((L, 128), lambda n: (rev(n), 0))],
        out_specs=[pl.BlockSpec((L, 1024), lambda n: (rev(n), 0)), pl.BlockSpec((L, 1024), lambda n: (rev(n), 0)),
                   pl.BlockSpec((L, 2048), lambda n: (rev(n), 0)), pl.BlockSpec((L, 2048), lambda n: (rev(n), 0)),
                   pl.BlockSpec((1, 2048), lambda n: (0, 0))],
        out_shape=[jax.ShapeDtypeStruct((T, 1024), BF16), jax.ShapeDtypeStruct((T, 1024), BF16), jax.ShapeDtypeStruct((T, 2048), BF16),
                   jax.ShapeDtypeStruct((T, 2048), BF16), jax.ShapeDtypeStruct((1, 2048), F32)],
        scratch_shapes=[pltpu.VMEM((HEADS, 128, 256), F32)], name="ret_bwd",
        compiler_params=pltpu.CompilerParams(dimension_semantics=("arbitrary",)),
    )(q, k, v, y, proj, gn_g, dyr, states, dm, xi, zeta, gl, cos128, sin128)


def _diag_mask(nrows, ncols, row0):
    row = row0 + lax.broadcasted_iota(jnp.int32, (nrows, ncols), 0)
    col = lax.broadcasted_iota(jnp.int32, (nrows, ncols), 1)
    return lax.shift_right_logical(col, 6) <= lax.shift_right_logical(row, 6)


def _diag_spans(t):
    h = t // 2
    return ((0, h, h), (h, h, t)) if h % 128 == 0 else ((0, t, t),)


def _tri_steps(nb, by_key):
    if by_key:
        pairs = [(i, j) for j in range(nb) for i in range(j, nb)]
    else:
        pairs = [(i, j) for i in range(nb) for j in range(i + 1)]
    return jnp.array([a for a, _ in pairs], jnp.int32), jnp.array([b for _, b in pairs], jnp.int32)


def _attn_fwd(qn, qpe, kv, kpe, T):
    t = min(ATT_TF, T)
    nb = T // t
    ii, jj = _tri_steps(nb, by_key=False)

    hp = ATT_HP
    w = 128 * hp

    def body(ii_ref, jj_ref, qn_ref, qp_ref, kn_ref, kp_ref, v_ref, o_ref, lse_ref, m_sc, l_sc, acc_sc):
        st = pl.program_id(1)
        i, j = ii_ref[st], jj_ref[st]

        @pl.when(j == 0)
        def _():
            m_sc[...] = jnp.full_like(m_sc, NEG)
            l_sc[...] = jnp.zeros_like(l_sc)
            acc_sc[...] = jnp.zeros_like(acc_sc)

        def update(diag):
            kp = kp_ref[...]
            spans = _diag_spans(t) if diag else ((0, t, t),)
            for hh in range(hp):
                sl = slice(128 * hh, 128 * (hh + 1))
                for row0, nr, nkeys in spans:
                    rows = slice(row0, row0 + nr)
                    q = jnp.concatenate([qn_ref[rows, sl], qp_ref[rows, sl]], axis=1)
                    k = jnp.concatenate([kn_ref[:nkeys, sl], kp[:nkeys]], axis=1)
                    s = _dot(q, k, 1, 1)
                    if diag:
                        s = jnp.where(_diag_mask(nr, nkeys, row0), s, NEG)
                    m_prev = m_sc[rows, sl]
                    m_new = jnp.maximum(m_prev, jnp.max(s, axis=1, keepdims=True))
                    a = jnp.exp2(m_prev - m_new)
                    p = jnp.exp2(s - m_new[:, 0:1])
                    l_sc[rows, sl] = a * l_sc[rows, sl] + jnp.sum(p, axis=1, keepdims=True)
                    acc_sc[rows, sl] = a * acc_sc[rows, sl] + _dot(p.astype(BF16), v_ref[:nkeys, sl], 1, 0)
                    m_sc[rows, sl] = m_new

        @pl.when(j < i)
        def _():
            update(False)

        @pl.when(j == i)
        def _():
            update(True)
            o_ref[...] = (acc_sc[...] / l_sc[...]).astype(BF16)
            lse_ref[...] = m_sc[...] + jnp.log2(l_sc[...])

    qs = pl.BlockSpec((t, w), lambda h, s, ii, jj: (ii[s], h))
    grid_spec = pltpu.PrefetchScalarGridSpec(
        num_scalar_prefetch=2, grid=(HEADS // hp, int(ii.shape[0])),
        in_specs=[qs, qs, pl.BlockSpec((t, w), lambda h, s, ii, jj: (jj[s], h)), pl.BlockSpec((t, 128), lambda h, s, ii, jj: (jj[s], 0)),
                  pl.BlockSpec((t, w), lambda h, s, ii, jj: (jj[s], HEADS // hp + h))],
        out_specs=[qs, qs],
        scratch_shapes=[pltpu.VMEM((t, w), F32), pltpu.VMEM((t, w), F32), pltpu.VMEM((t, w), F32)])
    return _pcall(
        body, grid_spec=grid_spec, out_shape=[jax.ShapeDtypeStruct((T, D), BF16), jax.ShapeDtypeStruct((T, D), F32)], name="attn_fwd",
        compiler_params=pltpu.CompilerParams(dimension_semantics=("arbitrary", "arbitrary")),
    )(ii, jj, qn, qpe, kv, kpe, kv)


def _attn_bwd(qn, qpe, kv, kpe, do, lse, delta, T):
    t = min(ATT_TB, T)
    nb = T // t
    ii, jj = _tri_steps(nb, by_key=True)

    def body(ii_ref, jj_ref, qn_ref, qp_ref, kn_ref, kp_ref, v_ref, do_ref, lse_ref, dl_ref,
             dqn_ref, dqp_ref, dkn_ref, dkp_ref, dv_ref, dk_sc, dv_sc):
        st = pl.program_id(1)
        i, j = ii_ref[st], jj_ref[st]

        @pl.when(st == 0)
        def _():
            dqn_ref[...] = jnp.zeros_like(dqn_ref)
            dqp_ref[...] = jnp.zeros_like(dqp_ref)

        @pl.when(i == j)
        def _():
            dk_sc[...] = jnp.zeros_like(dk_sc)
            dv_sc[...] = jnp.zeros_like(dv_sc)

        def update(diag):
            for row0, nr, nkeys in (_diag_spans(t) if diag else ((0, t, t),)):
                rows = slice(row0, row0 + nr)
                q = jnp.concatenate([qn_ref[rows, :], qp_ref[rows, :]], axis=1)
                k = jnp.concatenate([kn_ref[:nkeys, :], kp_ref[:nkeys, :]], axis=1)
                dob = do_ref[rows, :]
                s = _dot(q, k, 1, 1)
                if diag:
                    s = jnp.where(_diag_mask(nr, nkeys, row0), s, NEG)
                p = jnp.exp2(s - lse_ref[rows, 0:1])
                dv_sc[:nkeys, :] += _dot(p.astype(BF16), dob, 0, 0)
                dp = _dot(dob, v_ref[:nkeys, :], 1, 1)
                ds = (p * (dp - dl_ref[rows, 0:1])).astype(BF16)
                dk_sc[:nkeys, :] += _dot(ds, q, 0, 0)
                dq = _dot(ds, k, 1, 0) * MLA_SCALE
                grows = pl.ds(pl.multiple_of(i * t + row0, nr), nr)
                dqn_ref[grows, :] += dq[:, :128]
                dqp_ref[grows, :] += dq[:, 128:]

        @pl.when(i > j)
        def _():
            update(False)

        @pl.when(i == j)
        def _():
            update(True)

        @pl.when(i == nb - 1)
        def _():
            dkn_ref[...] = (dk_sc[:, :128] * (1.0 / LOG2E)).astype(BF16)
            dkp_ref[...] = dk_sc[:, 128:] * (1.0 / LOG2E)
            dv_ref[...] = dv_sc[...].astype(BF16)

    qs = pl.BlockSpec((t, 128), lambda h, s, ii, jj: (ii[s], h))
    ks = pl.BlockSpec((t, 128), lambda h, s, ii, jj: (jj[s], h))
    hs = pl.BlockSpec((T, 128), lambda h, s, ii, jj: (0, h))
    grid_spec = pltpu.PrefetchScalarGridSpec(
        num_scalar_prefetch=2, grid=(HEADS, int(ii.shape[0])),
        in_specs=[qs, qs, ks, pl.BlockSpec((t, 128), lambda h, s, ii, jj: (jj[s], 0)),
                  pl.BlockSpec((t, 128), lambda h, s, ii, jj: (jj[s], HEADS + h)), qs, qs, qs],
        out_specs=[hs, hs, ks, ks, ks],
        scratch_shapes=[pltpu.VMEM((t, 256), F32), pltpu.VMEM((t, 128), F32)])
    return _pcall(
        body, grid_spec=grid_spec,
        out_shape=[jax.ShapeDtypeStruct((T, D), F32), jax.ShapeDtypeStruct((T, D), F32), jax.ShapeDtypeStruct((T, D), BF16),
                   jax.ShapeDtypeStruct((T, D), F32), jax.ShapeDtypeStruct((T, D), BF16)], name="attn_bwd",
        compiler_params=pltpu.CompilerParams(dimension_semantics=("arbitrary", "arbitrary")),
    )(ii, jj, qn, qpe, kv, kpe, kv, do, lse, delta)


def _mesh_pos():
    x, y, c = lax.axis_index("x"), lax.axis_index("y"), lax.axis_index("c")
    return x, y, c, 4 * x + 2 * y + c


def _peer(x, y, c, k):
    px, py, pc = (x + ((k >> 2) & 1)) % 2, (y + ((k >> 1) & 1)) % 2, (c + (k & 1)) % 2
    return (px, py, pc), 4 * px + 2 * py + pc


_ANY = pl.BlockSpec(memory_space=pl.ANY)


def _rcopy(src, dst, send_sems, recv_sems, k, to):
    return pltpu.make_async_remote_copy(src_ref=src, dst_ref=dst, send_sem=send_sems.at[k], recv_sem=recv_sems.at[k],
                                        device_id=to, device_id_type=pl.DeviceIdType.MESH)


def _all_gather(wsh, ssh):
    def body(w_ref, s_ref, wall_ref, sall_ref, send_sems, recv_sems, loc_sems):
        x, y, c, me = _mesh_pos()
        sib = (x, y, 1 - c)
        chips = [(1 - x, y), (x, 1 - y), (1 - x, 1 - y)]
        slot = lambda px, py, pc: 4 * px + 2 * py + pc
        loc = [pltpu.make_async_copy(s_ref, sall_ref.at[me], loc_sems.at[0])]
        for cp in loc:
            cp.start()
        sends, fwd_waits = [], []
        for n, (src, dst) in enumerate(((w_ref, wall_ref), (s_ref, sall_ref))):
            o = 7 * n
            sends.append(_rcopy(src, dst.at[me], send_sems, recv_sems, o, sib))
            for j, chip in enumerate(chips):
                sends.append(_rcopy(src, dst.at[me], send_sems, recv_sems, o + 1 + j, (*chip, c)))
        for cp in sends:
            cp.start()
        for n, (src, dst) in enumerate(((w_ref, wall_ref), (s_ref, sall_ref))):
            o = 7 * n
            for j, chip in enumerate(chips):
                got = dst.at[slot(*chip, c)]
                _rcopy(src, got, send_sems, recv_sems, o + 1 + j, sib).wait_recv()
                fw = _rcopy(got, got, send_sems, recv_sems, o + 4 + j, sib)
                fw.start()
                sends.append(fw)
            fwd_waits.append(_rcopy(src, dst.at[slot(x, y, 1 - c)], send_sems, recv_sems, o, sib))
            for j, chip in enumerate(chips):
                fwd_waits.append(_rcopy(src, dst.at[slot(*chip, 1 - c)], send_sems, recv_sems, o + 4 + j, sib))
        for cp in fwd_waits:
            cp.wait_recv()
        for cp in sends:
            cp.wait_send()
        for cp in loc:
            cp.wait()

    wall, sall = _pcall(
        body, in_specs=[_ANY, _ANY], out_specs=[_ANY, _ANY],
        out_shape=[jax.ShapeDtypeStruct((N_DEV,) + wsh.shape, wsh.dtype), jax.ShapeDtypeStruct((N_DEV,) + ssh.shape, ssh.dtype)],
        scratch_shapes=[pltpu.SemaphoreType.DMA((14,)), pltpu.SemaphoreType.DMA((14,)), pltpu.SemaphoreType.DMA((1,))],
        name="all_gather_weights",
    )(wsh, ssh)
    me = 4 * lax.axis_index("x") + 2 * lax.axis_index("y") + lax.axis_index("c")
    return lax.dynamic_update_index_in_dim(wall, wsh, me, axis=0), sall


_HBM = pl.BlockSpec(memory_space=pltpu.HBM)
_SEM = pl.BlockSpec(memory_space=pltpu.SEMAPHORE)
_EFFECT = pltpu.SideEffectType.DATAFLOW_SIDE_EFFECTING


def _other_chips(x, y):
    return [(1 - x, y), (x, 1 - y), (1 - x, 1 - y)]


def _gather_start(wsh, order_after):
    def body(w_ref, land_ref, dep_ref, send_sems, recv_sems, w_thru, land_thru, token):
        x, y, c, me = _mesh_pos()
        _rcopy(w_ref, land_ref.at[me], send_sems, recv_sems, 0, (x, y, 1 - c)).start()
        for j, chip in enumerate(_other_chips(x, y)):
            _rcopy(w_ref, land_ref.at[me], send_sems, recv_sems, 1 + j, (*chip, c)).start()
        token[...] = jnp.zeros_like(token)

    shape = (N_DEV,) + wsh.shape
    land = pltpu.with_memory_space_constraint(lax.empty(shape, wsh.dtype), pltpu.HBM)
    return _pcall(
        body, name="gather_start",
        out_shape=(pltpu.SemaphoreType.DMA((4,)), pltpu.SemaphoreType.DMA((4,)), pltpu.HBM(wsh.shape, wsh.dtype),
                   pltpu.HBM(shape, wsh.dtype), jax.ShapeDtypeStruct((8, 128), F32)),
        in_specs=(_HBM, _HBM, _ANY), out_specs=(_SEM, _SEM, _HBM, _HBM, pl.BlockSpec(memory_space=pltpu.VMEM)),
        input_output_aliases={0: 2, 1: 3}, compiler_params=pltpu.CompilerParams(has_side_effects=_EFFECT),
    )(pltpu.with_memory_space_constraint(wsh, pltpu.HBM), land, order_after)


def _gather_wait(send_sems, recv_sems, w_thru, land_thru, after):
    def body(w_ref, land_ref, send_sems, recv_sems, after_ref, w_out, land_out):
        x, y, c, _ = _mesh_pos()
        senders = [(x, y, 1 - c)] + [(*chip, c) for chip in _other_chips(x, y)]
        for k, (px, py, pc) in enumerate(senders):
            cp = _rcopy(w_ref, land_ref.at[4 * px + 2 * py + pc], send_sems, recv_sems, k, (px, py, pc))
            cp.wait_send()
            cp.wait_recv()

    return _pcall(
        body, name="gather_wait", out_shape=(pltpu.HBM(w_thru.shape, w_thru.dtype), pltpu.HBM(land_thru.shape, land_thru.dtype)),
        in_specs=(_HBM, _HBM, _SEM, _SEM, _ANY), out_specs=(_HBM, _HBM), input_output_aliases={0: 0, 1: 1},
        compiler_params=pltpu.CompilerParams(has_side_effects=_EFFECT),
    )(w_thru, land_thru, send_sems, recv_sems, after)


def _gather_finish(wsh, land):
    def body(land_ref, out_ref, send_sems, recv_sems):
        x, y, c, _ = _mesh_pos()
        sib = (x, y, 1 - c)
        sends, recvs = [], []
        for j, (px, py) in enumerate(_other_chips(x, y)):
            mine, theirs = 4 * px + 2 * py + c, 4 * px + 2 * py + (1 - c)
            sends.append(_rcopy(land_ref.at[mine], out_ref.at[mine], send_sems, recv_sems, j, sib))
            recvs.append(_rcopy(land_ref.at[theirs], out_ref.at[theirs], send_sems, recv_sems, j, sib))
        for cp in sends:
            cp.start()
        for cp in recvs:
            cp.wait_recv()
        for cp in sends:
            cp.wait_send()

    done = _pcall(
        body, in_specs=[_ANY], out_specs=_ANY, out_shape=jax.ShapeDtypeStruct(land.shape, land.dtype),
        input_output_aliases={0: 0},
        scratch_shapes=[pltpu.SemaphoreType.DMA((3,)), pltpu.SemaphoreType.DMA((3,))], name="gather_finish",
    )(land)
    me = 4 * lax.axis_index("x") + 2 * lax.axis_index("y") + lax.axis_index("c")
    return lax.dynamic_update_index_in_dim(done, wsh, me, axis=0)


_ROW_TILES = (400, 368, 352, 256, 128, 16, 8)


def _exchange_sibling(g4, name):
    def body(g_ref, r_ref, send_sems, recv_sems):
        x, y, c, _ = _mesh_pos()
        sib = (x, y, 1 - c)
        cps = [_rcopy(g_ref.at[q, 1 - c], r_ref.at[q], send_sems, recv_sems, q, sib) for q in range(4)]
        for cp in cps:
            cp.start()
        for cp in cps:
            cp.wait()

    n, _, R, C = g4.shape
    return _pcall(
        body, in_specs=[_ANY], out_specs=_ANY, out_shape=jax.ShapeDtypeStruct((n, R, C), g4.dtype),
        scratch_shapes=[pltpu.SemaphoreType.DMA((4,)), pltpu.SemaphoreType.DMA((4,))], name=name,
    )(g4)


def _sum_sibling(g4, rsib, cvec, name):
    n, _, R, C = g4.shape
    tr = _pick(R, _ROW_TILES)

    def body(c_ref, g_ref, r_ref, o_ref):
        o_ref[...] = (g_ref[...].astype(F32) + r_ref[...].astype(F32)).astype(o_ref.dtype)

    grid_spec = pltpu.PrefetchScalarGridSpec(
        num_scalar_prefetch=1, grid=(n, R // tr),
        in_specs=[pl.BlockSpec((None, None, tr, C), lambda q, i, cr: (q, cr[0], i, 0)), pl.BlockSpec((None, tr, C), lambda q, i, cr: (q, i, 0))],
        out_specs=pl.BlockSpec((None, tr, C), lambda q, i, cr: (q, i, 0)))
    return _pcall(body, grid_spec=grid_spec, out_shape=jax.ShapeDtypeStruct((n, R, C), g4.dtype), name=name)(cvec, g4, rsib)


def _chips_start(part, name, order_after=None):
    n_in = 2 + int(order_after is not None)

    def body(*refs):
        p_ref, land_ref = refs[:2]
        send_sems, recv_sems, token = refs[n_in], refs[n_in + 1], refs[n_in + 4]
        x, y, c, _ = _mesh_pos()
        myq = 2 * x + y
        for j, (px, py) in enumerate(_other_chips(x, y)):
            _rcopy(p_ref.at[2 * px + py], land_ref.at[myq], send_sems, recv_sems, j, (px, py, c)).start()
        token[...] = jnp.zeros_like(token)

    land = pltpu.with_memory_space_constraint(lax.empty(part.shape, part.dtype), pltpu.HBM)
    extra = () if order_after is None else (order_after,)
    return _pcall(
        body, name=name,
        out_shape=(pltpu.SemaphoreType.DMA((3,)), pltpu.SemaphoreType.DMA((3,)), pltpu.HBM(part.shape, part.dtype),
                   pltpu.HBM(part.shape, part.dtype), jax.ShapeDtypeStruct((8, 128), F32)),
        in_specs=(_HBM, _HBM) + (_ANY,) * len(extra),
        out_specs=(_SEM, _SEM, _HBM, _HBM, pl.BlockSpec(memory_space=pltpu.VMEM)),
        input_output_aliases={0: 2, 1: 3}, compiler_params=pltpu.CompilerParams(has_side_effects=_EFFECT),
    )(pltpu.with_memory_space_constraint(part, pltpu.HBM), land, *extra)


def _chips_wait(send_sems, recv_sems, p_thru, land_thru, after, name):
    def body(p_ref, land_ref, send_sems, recv_sems, after_ref, p_out, land_out):
        x, y, c, _ = _mesh_pos()
        for j, (px, py) in enumerate(_other_chips(x, y)):
            q = 2 * px + py
            cp = _rcopy(p_ref.at[q], land_ref.at[q], send_sems, recv_sems, j, (px, py, c))
            cp.wait_send()
            cp.wait_recv()

    return _pcall(
        body, name=name, out_shape=(pltpu.HBM(p_thru.shape, p_thru.dtype), pltpu.HBM(p_thru.shape, p_thru.dtype)),
        in_specs=(_HBM, _HBM, _SEM, _SEM, _ANY), out_specs=(_HBM, _HBM), input_output_aliases={0: 0, 1: 1},
        compiler_params=pltpu.CompilerParams(has_side_effects=_EFFECT),
    )(p_thru, land_thru, send_sems, recv_sems, after)


def _sum_chips(part, land, qvec, name):
    n, R, C = part.shape
    tr = _pick(R, _ROW_TILES)

    def body(q_ref, p_ref, *refs):
        o_ref = refs[n]
        acc = None
        for q in range(n):
            term = jnp.where(q_ref[0] == q, p_ref[...], refs[q][...]).astype(F32)
            acc = term if acc is None else acc + term
        o_ref[...] = acc

    def land_spec(q):
        return pl.BlockSpec((None, tr, C), lambda i, m: (jnp.where(m[0] == q, (q + 1) % n, q), i, 0))

    grid_spec = pltpu.PrefetchScalarGridSpec(
        num_scalar_prefetch=1, grid=(R // tr,),
        in_specs=[pl.BlockSpec((None, tr, C), lambda i, m: (m[0], i, 0))] + [land_spec(q) for q in range(n)],
        out_specs=pl.BlockSpec((tr, C), lambda i, m: (i, 0)))
    return _pcall(body, grid_spec=grid_spec, out_shape=jax.ShapeDtypeStruct((R, C), F32), name=name)(qvec, part, *([land] * n))


def _exchange_small(gsmall):
    def body(s_ref, srecv_ref, send_sems, recv_sems, loc_sem):
        x, y, c, me = _mesh_pos()
        loc = pltpu.make_async_copy(s_ref, srecv_ref.at[me], loc_sem.at[0])
        loc.start()
        sends, recvs = [], []
        for k in range(1, N_DEV):
            to, pidx = _peer(x, y, c, k)
            sends.append(_rcopy(s_ref, srecv_ref.at[me], send_sems, recv_sems, k - 1, to))
            recvs.append(_rcopy(s_ref, srecv_ref.at[pidx], send_sems, recv_sems, k - 1, to))
        for cp in sends:
            cp.start()
        for cp in recvs:
            cp.wait_recv()
        for cp in sends:
            cp.wait_send()
        loc.wait()

    return _pcall(
        body, in_specs=[_ANY], out_specs=_ANY, out_shape=jax.ShapeDtypeStruct((N_DEV,) + gsmall.shape, gsmall.dtype),
        scratch_shapes=[pltpu.SemaphoreType.DMA((7,)), pltpu.SemaphoreType.DMA((7,)), pltpu.SemaphoreType.DMA((1,))],
        name="exchange_small",
    )(gsmall)


def _sum_slots(recv, name):
    n, R, C = recv.shape
    tr = _pick(R, _ROW_TILES)

    def body(r_ref, o_ref):
        acc = r_ref[0].astype(F32)
        for s in range(1, n):
            acc = acc + r_ref[s].astype(F32)
        o_ref[...] = acc

    return _pcall(body, grid=(R // tr,), in_specs=[pl.BlockSpec((n, tr, C), lambda i: (0, i, 0))],
                  out_specs=pl.BlockSpec((tr, C), lambda i: (i, 0)), out_shape=jax.ShapeDtypeStruct((R, C), F32), name=name)(recv)


def _adamw(w, g, m, v, name, order_after=None):
    shape = w.shape
    w2, g2, m2, v2 = (a.reshape(-1, shape[-1]) for a in (w, g, m, v))
    R, C = w2.shape
    tr = _pick(R, (256, 128, 64, 32, 16, 8)) if R > 256 else R
    n_in = 4 + int(order_after is not None)

    def body(*refs):
        w_ref, g_ref, m_ref, v_ref = refs[:4]
        d_ref, nm_ref, nv_ref = refs[n_in:]
        gg = g_ref[...]
        nm = ADAM_B1 * m_ref[...] + (1.0 - ADAM_B1) * gg
        nv = ADAM_B2 * v_ref[...] + (1.0 - ADAM_B2) * jnp.square(gg)
        m_hat = nm / (1.0 - ADAM_B1 ** ADAM_STEP)
        v_hat = nv / (1.0 - ADAM_B2 ** ADAM_STEP)
        d_ref[...] = -ADAM_LR * (m_hat / (jnp.sqrt(v_hat) + ADAM_EPS) + ADAM_WD * w_ref[...])
        nm_ref[...] = nm
        nv_ref[...] = nv

    spec = pl.BlockSpec((tr, C), lambda i: (i, 0))
    in_specs, args = [spec] * 4, [w2, g2, m2, v2]
    if order_after is not None:
        in_specs.append(pl.BlockSpec(order_after.shape, lambda i: (0, 0)))
        args.append(order_after)
    d, nm, nv = _pcall(body, grid=(R // tr,), in_specs=in_specs, out_specs=[spec] * 3,
                       out_shape=[jax.ShapeDtypeStruct((R, C), F32)] * 3, name=name)(*args)
    return d.reshape(shape), nm.reshape(shape), nv.reshape(shape)


def _to_rows(name, w):
    w = w[0]
    if name in ("ffn1_w_in", "ffn2_w_in", "w_in"):
        return w.T
    if name in ("w_uq", "w_ukv", "ple_w_proj"):
        return w.T.reshape(-1, 1024)
    return w


def _from_rows(name, g, shape):
    if name in ("ffn1_w_in", "ffn2_w_in", "w_in"):
        return g.T.reshape(shape)
    if name in ("w_uq", "w_ukv", "ple_w_proj"):
        return g.reshape(-1, shape[1]).T.reshape(shape)
    return g.reshape(shape)


def _unpack(wall, group):
    out, off = {}, 0
    for name, r in group:
        out[name] = wall[:, off:off + r, :].reshape(N_DEV * r, 1024)
        off += _pad16(r)
    return out


def _w_in_internal(wt):
    return jnp.concatenate([wt[0:6144], wt[6720:8768], wt[6144:6720], jnp.zeros((P_W - W_IN_COLS, 1024), wt.dtype)], axis=0)


def _w_in_external(d):
    return jnp.concatenate([d[0:6144], d[8192:8768], d[6144:8192]], axis=0)


def _rope_tables(positions):
    pos = positions[0].astype(F32)

    def cs(half):
        inv = ROPE_BASE ** (-jnp.arange(half, dtype=F32) / half)
        ang = pos[:, None] * inv
        return jnp.cos(ang), jnp.sin(ang)

    c, s = cs(64)
    c2, s2 = cs(32)
    return (jnp.concatenate([c, c], axis=1), jnp.concatenate([-s, s], axis=1),
            jnp.concatenate([c2, c2, c2, c2], axis=1), jnp.concatenate([-s2, s2, -s2, s2], axis=1))


def _local_step(x, p, positions, target, W, ln_g, ln_b, gn_g, qg, kvg, early=None, rest_weights=None, start_token=None):
    T = x.shape[0]
    tabs = _rope_tables(positions)
    rc = _ret_consts()
    lg = [ln_g[i:i + 1] for i in range(4)]
    lb = [ln_b[i:i + 1] for i in range(4)]
    pb = p.astype(BF16)

    hs1, gu1, *xb = _ffn_in(x, W["ffn1_w_in"], "ffn1_in", order_after=start_token)
    xb = xb[0] if xb else x
    f1, h1, h1b = _mm(hs1, W["ffn1_w_out"], name="ffn1_out", tm=LN_TM, epilogue=_ln_epilogue(x, 0.5, lg[0], lb[0]))
    if rest_weights is not None:
        W = {**W, **rest_weights(h1b)}
    w_in_t = _w_in_internal(W["w_in"])
    wuq = W["w_uq"].reshape(1536, LORA).reshape(HEADS, 192, LORA)
    wuq = jnp.concatenate([wuq[:, :128].reshape(1024, LORA), wuq[:, 128:].reshape(512, LORA)], axis=0)
    wukv = W["w_ukv"].reshape(2048, LORA).reshape(HEADS, 2, 128, LORA).transpose(1, 0, 2, 3).reshape(2048, LORA)
    wp_t = W["ple_w_proj"].reshape(1024, D_PLE)
    proj = _mm(h1b, w_in_t, tb=True, out_dtype=BF16, name="mixer_in")
    rq, rk, qn, kvn, kpe = _mixer_prep_fwd(proj, tabs, qg, kvg, T)
    y, yr, states = _ret_fwd(rq, rk, proj, proj, gn_g, rc, T)
    y_ret = _mm(yr, W["w_ret_o"], name="ret_o")
    qnope, qpe = _mm(qn, wuq, tb=True, name="mla_uq", tm=LN_TM, tn=1536, epilogue=_q_assemble_epilogue(tabs))
    kv = _mm(kvn, wukv, tb=True, out_dtype=BF16, name="mla_ukv")
    o, lse = _attn_fwd(qnope, qpe, kv, kpe, T)
    y_mla, mix = _mm(o, W["w_mla_o"], name="mla_o", tm=LN_TM, epilogue=_mix_epilogue(proj, y_ret))
    mixed, h2, h2b = _mm(mix, W["w_out"], name="mixer_out", tm=LN_TM, epilogue=_ln_epilogue(h1, 1.0, lg[1], lb[1]))
    hs2, gu2 = _ffn_in(h2b, W["ffn2_w_in"], "ffn2_in")
    f2, h3, h3b = _mm(hs2, W["ffn2_w_out"], name="ffn2_out", tm=LN_TM, epilogue=_ln_epilogue(h2, 0.5, lg[2], lb[2]))
    pp = _mm(pb, wp_t, tb=True, name="ple_proj")

    G = {}
    dh3_a, dgl, dpp, dg3, db3, loss = _mm(h3b, W["ple_w_gate"], name="ple_gate", tm=LN_TM,
                                          epilogue=_head_epilogue(h3, pp, target, lg[3], lb[3]))
    G["ple_w_gate"] = _mm(h3b, dgl, ta=True, name="d_ple_gate")
    G["ple_w_proj"] = _mm(dpp, pb, ta=True, name="d_ple_proj")
    dh2_a, df2, dg2, db2 = _mm(dgl, W["ple_w_gate"], tb=True, add=dh3_a, name="dh3", tm=LN_TM,
                               epilogue=_ln_bwd_epilogue(h2, f2, 0.5, lg[2], lb[2]))
    G["ffn2_w_out"] = _mm(hs2, df2, ta=True, name="d_ffn2_out")
    da2 = _ffn_act_bwd(df2, W["ffn2_w_out"], gu2, "ffn2_act_bwd")
    G["ffn2_w_in"] = _mm(da2, h2b, ta=True, name="d_ffn2_in")
    dh1_a, dmixed, dg1, db1 = _mm(da2, W["ffn2_w_in"], add=dh2_a, name="dh2", tm=LN_TM,
                                  epilogue=_ln_bwd_epilogue(h1, mixed, 1.0, lg[1], lb[1]))
    G["w_out"] = _mm(mix, dmixed, ta=True, name="d_mixer_out")
    dgr, dgm, dy_ret, dy_mla = _mm(dmixed, W["w_out"], tb=True, name="dmix", tm=LN_TM,
                                   epilogue=_mix_bwd_epilogue(proj, y_ret, y_mla))
    G["w_mla_o"] = _mm(o, dy_mla, ta=True, name="d_mla_o")
    dob, delta = _mm(dy_mla, W["w_mla_o"], tb=True, name="do", tm=LN_TM, epilogue=_delta_epilogue(o))
    dqn_f, dqpe_f, dkn, dkpe_all, dv = _attn_bwd(qnope, qpe, kv, kpe, dob, lse, delta, T)
    dq_n, dq_r = _q_assemble_bwd(dqn_f, dqpe_f, tabs, T)
    g_uq = jnp.concatenate([_mm(dq_n, qn, ta=True, name="d_uq_nope"), _mm(dq_r, qn, ta=True, name="d_uq_rope")], axis=0)
    g_uq = jnp.concatenate([g_uq[:1024].reshape(HEADS, 128, LORA), g_uq[1024:].reshape(HEADS, 64, LORA)], axis=1)
    G["w_uq"] = g_uq.reshape(1536 * LORA // 1024, 1024)
    dqn = _mm(dq_r, wuq[1024:], add=_mm(dq_n, wuq[:1024], name="dqn_a"), name="dqn_b")
    g_ukv = jnp.stack([_mm(dkn, kvn, ta=True, name="d_ukv_k"), _mm(dv, kvn, ta=True, name="d_ukv_v")], axis=0)
    G["w_ukv"] = g_ukv.reshape(2, HEADS, 128, LORA).transpose(1, 0, 2, 3).reshape(2048 * LORA // 1024, 1024)
    dkvn = _mm(dv, wukv[1024:], add=_mm(dkn, wukv[:1024], name="dkvn_a"), name="dkvn_b")
    dcq, dckv, dkpe, dqg, dkvg = _rms_bwd(proj, dqn, dkvn, dkpe_all, tabs, qg, kvg, T)
    G["w_ret_o"] = _mm(yr, dy_ret, ta=True, name="d_ret_o")
    dyr = _mm(dy_ret, W["w_ret_o"], tb=True, name="dyr")
    drq, drk, drv, drg, dgn = _ret_bwd(rq, rk, proj, y, proj, gn_g, states, dyr, rc, tabs, T)
    dproj = jnp.concatenate([drq, drk, drv, drg, dgr, dgm, dcq, dckv, dkpe, jnp.zeros((T, P_W - P_KPE - 128), BF16)], axis=1)
    G["w_in"] = _w_in_external(_mm(dproj, h1b, ta=True, name="d_mixer_in"))
    lg0 = lg[0] if early is None else lg[0] + early(G)[0:1, 0:1]
    dx_a, df1, dg0, db0 = _mm(dproj, w_in_t, add=dh1_a, name="dh1", tm=min(1024, T), tk=640, vmem_limit=VMEM_NEAR_FULL,
                              epilogue=_ln_bwd_epilogue(x, f1, 0.5, lg0, lb[0]))
    G["ffn1_w_out"] = _mm(hs1, df1, ta=True, name="d_ffn1_out")
    da1 = _ffn_act_bwd(df1, W["ffn1_w_out"], gu1, "ffn1_act_bwd")
    G["ffn1_w_in"] = _mm(da1, xb, ta=True, name="d_ffn1_in")
    grad_x = _mm(da1, W["ffn1_w_in"], add=dx_a, name="grad_x")

    small = dict(ln_g=jnp.concatenate([dg0, dg1, dg2, dg3], axis=0), ln_b=jnp.concatenate([db0, db1, db2, db3], axis=0),
                 ret_gn_g=dgn, q_norm_g=dqg, kv_norm_g=dkvg)
    return loss, grad_x, G, small


def kernel(x, p, positions, ln_g, ln_b, ffn1_w_in, ffn1_w_out, w_in, ret_gn_g, w_ret_o, q_norm_g, kv_norm_g, w_uq, w_ukv, w_mla_o, w_out, ffn2_w_in, ffn2_w_out, ple_w_gate, ple_w_proj, loss_target, m_ln_g, m_ln_b, m_ffn1_w_in, m_ffn1_w_out, m_w_in, m_ret_gn_g, m_w_ret_o, m_q_norm_g, m_kv_norm_g, m_w_uq, m_w_ukv, m_w_mla_o, m_w_out, m_ffn2_w_in, m_ffn2_w_out, m_ple_w_gate, m_ple_w_proj, v_ln_g, v_ln_b, v_ffn1_w_in, v_ffn1_w_out, v_w_in, v_ret_gn_g, v_w_ret_o, v_q_norm_g, v_kv_norm_g, v_w_uq, v_w_ukv, v_w_mla_o, v_w_out, v_ffn2_w_in, v_ffn2_w_out, v_ple_w_gate, v_ple_w_proj):
    names = ("ln_g", "ln_b", "ffn1_w_in", "ffn1_w_out", "w_in", "ret_gn_g", "w_ret_o", "q_norm_g", "kv_norm_g", "w_uq", "w_ukv",
             "w_mla_o", "w_out", "ffn2_w_in", "ffn2_w_out", "ple_w_gate", "ple_w_proj")
    ws = dict(zip(names, (ln_g, ln_b, ffn1_w_in, ffn1_w_out, w_in, ret_gn_g, w_ret_o, q_norm_g, kv_norm_g, w_uq, w_ukv, w_mla_o,
                          w_out, ffn2_w_in, ffn2_w_out, ple_w_gate, ple_w_proj)))
    ms = dict(zip(names, (m_ln_g, m_ln_b, m_ffn1_w_in, m_ffn1_w_out, m_w_in, m_ret_gn_g, m_w_ret_o, m_q_norm_g, m_kv_norm_g, m_w_uq,
                          m_w_ukv, m_w_mla_o, m_w_out, m_ffn2_w_in, m_ffn2_w_out, m_ple_w_gate, m_ple_w_proj)))
    vs = dict(zip(names, (v_ln_g, v_ln_b, v_ffn1_w_in, v_ffn1_w_out, v_w_in, v_ret_gn_g, v_w_ret_o, v_q_norm_g, v_kv_norm_g, v_w_uq,
                          v_w_ukv, v_w_mla_o, v_w_out, v_ffn2_w_in, v_ffn2_w_out, v_ple_w_gate, v_ple_w_proj)))

    parts = []
    for name, r in PACK:
        rows = _to_rows(name, ws[name])
        if _pad16(r) != r:
            rows = jnp.concatenate([rows, jnp.zeros((_pad16(r) - r, 1024), F32)], axis=0)
        parts.append(rows)
    wsh_first = jnp.concatenate(parts[:len(PACK_LATE)], axis=0).astype(BF16)
    wsh_rest = jnp.concatenate(parts[len(PACK_LATE):], axis=0).astype(BF16)
    ssh = jnp.concatenate([ln_g[0], ln_b[0]], axis=0)
    wall_first, sall = _all_gather(wsh_first, ssh)
    *gather_handles, start_token = _gather_start(wsh_rest, wall_first)
    W = _unpack(wall_first, PACK_LATE)
    ln_full = sall.reshape(N_DEV, 2, 4, 128).transpose(1, 2, 0, 3).reshape(2, 4, 1024)

    def rest_weights(after):
        w_thru, land = _gather_wait(*gather_handles, after)
        return _unpack(_gather_finish(w_thru, land), PACK_EARLY)

    cvec = lax.axis_index("c").astype(jnp.int32).reshape(1)
    qvec = (2 * lax.axis_index("x") + lax.axis_index("y")).astype(jnp.int32).reshape(1)

    def chip_partials(G, group, tag):
        gparts = []
        for name, r in group:
            g = G[name].reshape(N_DEV, r, 1024)
            if _pad16(r) != r:
                g = jnp.concatenate([g, jnp.zeros((N_DEV, _pad16(r) - r, 1024), g.dtype)], axis=1)
            gparts.append(g)
        gfull = jnp.concatenate(gparts, axis=1).astype(BF16)
        g4 = gfull.reshape(4, 2, gfull.shape[1], 1024)
        return _sum_sibling(g4, _exchange_sibling(g4, "exchange_sibling_" + tag), cvec, "sum_sibling_" + tag)

    in_flight = []

    def early(G):
        *handles, token = _chips_start(chip_partials(G, PACK_EARLY, "early"), "chips_start_early")
        in_flight.append(handles)
        return token

    loss_p, grad_x, G, small = _local_step(x[0], p[0, 0], positions, loss_target[0], W, ln_full[0], ln_full[1],
                                           ret_gn_g, q_norm_g, kv_norm_g, early=early, rest_weights=rest_weights,
                                           start_token=start_token)

    part_e, land_e = _chips_wait(*in_flight[0], grad_x, "chips_wait_early")
    gsh_early = _sum_chips(part_e, land_e, qvec, "sum_grads_early")
    pad256 = lambda a: jnp.concatenate([a, jnp.zeros((1, 1024 - a.shape[1]), F32)], axis=1)
    gsmall = jnp.concatenate([small["ln_g"], small["ln_b"], small["ret_gn_g"].reshape(2, 1024), pad256(small["q_norm_g"]),
                              pad256(small["kv_norm_g"]), jnp.zeros((SMALL_ROWS - 12, 1024), F32)], axis=0)
    srecv = _exchange_small(gsmall)
    part_l = chip_partials(G, PACK_LATE, "late")
    *late_handles, late_token = _chips_start(part_l, "chips_start_late", order_after=srecv)
    ssum = _sum_slots(srecv, "sum_small_grads")

    def unpack_grads(group, gsh):
        out, off = {}, 0
        for name, r in group:
            out[name] = _from_rows(name, gsh[off:off + r], ws[name].shape)
            off += _pad16(r)
        return out

    grads = unpack_grads(PACK_EARLY, gsh_early)
    me = 4 * lax.axis_index("x") + 2 * lax.axis_index("y") + lax.axis_index("c")
    grads["ln_g"] = lax.dynamic_slice(ssum[0:4], (0, me * 128), (4, 128)).reshape(1, 4, 128)
    grads["ln_b"] = lax.dynamic_slice(ssum[4:8], (0, me * 128), (4, 128)).reshape(1, 4, 128)
    grads["ret_gn_g"] = ssum[8:10].reshape(1, 2048)
    grads["q_norm_g"] = ssum[10:11, :256]
    grads["kv_norm_g"] = ssum[11:12, :256]

    delta, new_m, new_v = {}, {}, {}
    late_names = [n for n, _ in PACK_LATE]
    for name in names:
        if name not in late_names:
            delta[name], new_m[name], new_v[name] = _adamw(ws[name], grads[name], ms[name], vs[name], "adamw_" + name,
                                                           order_after=late_token)
    part_l, land_l = _chips_wait(*late_handles, new_v["w_in"], "chips_wait_late")
    grads.update(unpack_grads(PACK_LATE, _sum_chips(part_l, land_l, qvec, "sum_grads_late")))
    for name in late_names:
        delta[name], new_m[name], new_v[name] = _adamw(ws[name], grads[name], ms[name], vs[name], "adamw_" + name)

    loss = lax.psum(loss_p[0, 0], ("x", "y", "c"))
    return (loss, grad_x[None], *[grads[n] for n in names], *[delta[n] for n in names],
            *[new_m[n] for n in names], *[new_v[n] for n in names])
```

```python
import functools
import math

import jax
import jax.numpy as jnp
from jax import lax
from jax.experimental import pallas as pl
from jax.experimental.pallas import tpu as pltpu

F32 = jnp.float32
BF16 = jnp.bfloat16

N_DEV = 8
D = 1024
D_FF = 2816
D_PLE = 256
CHUNK = 64
HEADS = 8
RET_DK = 128
RET_DV = 256
MLA_NOPE = 128
MLA_ROPE = 64
MLA_DV = 128
LORA = 256
ROPE_BASE = 10000.0
EPS = 1e-5
ALPHA = 2.0 ** 0.25
RET_SCALE = RET_DK ** -0.5
MLA_SCALE = (MLA_NOPE + MLA_ROPE) ** -0.5
NEG = -1e30

ADAM_LR = 0.001
ADAM_B1 = 0.9
ADAM_B2 = 0.999
ADAM_EPS = 1e-08
ADAM_WD = 0.01
ADAM_STEP = 10

P_RQ, P_RK, P_RV, P_RG, P_GR, P_GM, P_CQ, P_CKV, P_KPE, P_W = 0, 1024, 2048, 4096, 6144, 7168, 8192, 8448, 8704, 8960
W_IN_COLS = 8768
RET_L = 256
ATT_TF = 1024
ATT_TB = 1024
ATT_HP = 4
LOG2E = math.log2(math.e)
Q_PRESCALE = MLA_SCALE * LOG2E

PACK = (("ffn1_w_in", 704), ("ffn1_w_out", 352), ("w_in", 1096), ("w_ret_o", 256), ("w_uq", 48), ("w_ukv", 64),
        ("w_mla_o", 128), ("w_out", 128), ("ffn2_w_in", 704), ("ffn2_w_out", 352), ("ple_w_gate", 128), ("ple_w_proj", 32))


def _pad16(r):
    return -(-r // 16) * 16


PACK_ROWS = sum(_pad16(r) for _, r in PACK)
PACK_LATE = PACK[:2]
PACK_EARLY = PACK[2:]
SMALL_ROWS = 16


def _pcall(body, **kw):
    return pl.pallas_call(body, **kw)


def _pick(dim, prefs):
    for p in prefs:
        if dim % p == 0:
            return p
    return dim


def _sigmoid(x):
    return 1.0 / (1.0 + jnp.exp(-x))


def _silu(x):
    return x * _sigmoid(x)


def _ln(r, g, b):
    mu = jnp.mean(r, axis=-1, keepdims=True)
    var = jnp.mean(jnp.square(r - mu), axis=-1, keepdims=True)
    return (r - mu) * lax.rsqrt(var + EPS) * g + b


def _rms(x, g):
    return x * lax.rsqrt(jnp.mean(jnp.square(x), axis=-1, keepdims=True) + EPS) * g


def _dot(a, b, ca, cb):
    return lax.dot_general(a, b, (((ca,), (cb,)), ((), ())), preferred_element_type=F32)


def _accum(ref, val, first=None):
    @pl.when(pl.program_id(0) == 0 if first is None else first)
    def _():
        ref[...] = jnp.zeros_like(ref)

    ref[...] += val


def _mm(a, b, *, ta=False, tb=False, add=None, out_dtype=None, name, tm=None, tn=None, tk=None, epilogue=None):
    parts = a.shape[0] if a.ndim == 3 else 1
    ar, ac = a.shape[-2], a.shape[-1]
    out_dtype = out_dtype or (BF16 if ta else F32)
    if ta:
        K, M = ar, ac * parts
    else:
        M, K = ar, ac * parts
    if tb:
        N, K2 = b.shape
    else:
        K2, N = b.shape
    assert K == K2, (a.shape, b.shape, ta, tb)
    big = (1024, 1408, 1280, 768, 512, 256, 128)
    tm = tm or _pick(ac if (ta and parts > 1) else M, big)
    tn = tn or (N if N <= 1024 else _pick(N, big))
    kdim = ac if (not ta and parts > 1) else K
    tk = tk or (kdim if kdim <= 2816 and parts == 1 else
                _pick(kdim, (2048, 1408, 1280, 1024, 512) if tm <= 1024 else (1024, 1408, 1280, 512)))
    nk = K // tk
    grid = (M // tm, N // tn, nk)
    if parts > 1 and ta:
        per = ac // tm
        a_spec = pl.BlockSpec((None, tk, tm), lambda i, j, k: (i // per, k, i % per))
    elif parts > 1:
        per = ac // tk
        a_spec = pl.BlockSpec((None, tm, tk), lambda i, j, k: (k // per, i, k % per))
    else:
        a_spec = pl.BlockSpec((tk, tm), lambda i, j, k: (k, i)) if ta else pl.BlockSpec((tm, tk), lambda i, j, k: (i, k))
    b_spec = pl.BlockSpec((tn, tk), lambda i, j, k: (j, k)) if tb else pl.BlockSpec((tk, tn), lambda i, j, k: (k, j))
    o_spec = pl.BlockSpec((tm, tn), lambda i, j, k: (i, j))
    ca, cb = (0 if ta else 1), (1 if tb else 0)
    has_add = add is not None
    n_in = 2 + int(has_add)
    if epilogue is not None:
        assert tn == N and not ta
        ep_fn, ep_rows, ep_whole, ep_outs, ep_accs = epilogue
        n_ep_in = len(ep_rows) + len(ep_whole)
        n_out = len(ep_outs) + len(ep_accs)
    else:
        n_ep_in, n_out = 0, 1

    def body(*refs):
        a_ref, b_ref = refs[0], refs[1]
        add_ref = refs[2] if has_add else None
        o_ref = refs[n_in + n_ep_in]
        first_row_tile = pl.program_id(0) == 0

        def finish(r):
            if has_add:
                r = r + add_ref[...].astype(F32)
            if epilogue is not None:
                ep_fn(r, refs[n_in:n_in + n_ep_in], refs[n_in + n_ep_in:n_in + n_ep_in + n_out], first_row_tile)
            else:
                o_ref[...] = r.astype(out_dtype)

        if nk == 1:
            finish(_dot(a_ref[...], b_ref[...], ca, cb))
            return
        acc_ref = refs[-1]
        k = pl.program_id(2)

        @pl.when(k == 0)
        def _():
            acc_ref[...] = jnp.zeros_like(acc_ref)

        acc_ref[...] += _dot(a_ref[...], b_ref[...], ca, cb)

        @pl.when(k == nk - 1)
        def _():
            finish(acc_ref[...])

    in_specs = [a_spec, b_spec] + ([o_spec] if has_add else [])
    args = (a, b) + ((add,) if has_add else ())
    out_specs, out_shape = o_spec, jax.ShapeDtypeStruct((M, N), out_dtype)
    if epilogue is not None:
        for row_in in ep_rows:
            arr, col_block, width = (tuple(row_in) + (N,))[:3] if isinstance(row_in, tuple) else (row_in, 0, N)
            in_specs.append(pl.BlockSpec((tm, width), lambda i, j, k, _c=col_block: (i, _c)))
            args += (arr,)
        in_specs += [pl.BlockSpec(w.shape, lambda i, j, k, _n=w.ndim: (0,) * _n) for w in ep_whole]
        args += tuple(ep_whole)
        outs = [o if isinstance(o, tuple) else (o, N) for o in ep_outs]
        out_specs = ([pl.BlockSpec((tm, w), lambda i, j, k: (i, 0)) for _, w in outs]
                     + [pl.BlockSpec((r, N), lambda i, j, k: (0, 0)) for r in ep_accs])
        out_shape = [jax.ShapeDtypeStruct((M, w), dt) for dt, w in outs] + [jax.ShapeDtypeStruct((r, N), F32) for r in ep_accs]
    return _pcall(
        body, grid=grid, in_specs=in_specs, out_specs=out_specs, out_shape=out_shape,
        scratch_shapes=[pltpu.VMEM((tm, tn), F32)] if nk > 1 else [], name=name,
        compiler_params=pltpu.CompilerParams(dimension_semantics=("arbitrary" if epilogue is not None else "parallel", "parallel", "arbitrary")),
    )(*args)


def _ln_epilogue(res, c, g, b):
    def fn(r, ins, outs, first):
        res_ref, g_ref, b_ref = ins
        f_ref, h_ref, hb_ref = outs
        h = _ln(ALPHA * res_ref[...] + c * r, g_ref[...], b_ref[...])
        f_ref[...] = r
        h_ref[...] = h
        hb_ref[...] = h.astype(BF16)

    return (fn, [res], [g, b], (F32, F32, BF16), ())


def _ln_bwd_epilogue(res, f, c, g, b):
    def fn(r, ins, outs, first):
        res_ref, f_ref, g_ref, b_ref = ins
        dr_ref, df_ref, dg_ref, db_ref = outs
        pre = ALPHA * res_ref[...] + c * f_ref[...]
        xc = pre - jnp.mean(pre, axis=-1, keepdims=True)
        rstd = lax.rsqrt(jnp.mean(jnp.square(xc), axis=-1, keepdims=True) + EPS)
        xhat = xc * rstd
        dyg = r * g_ref[...]
        dpre = rstd * (dyg - jnp.mean(dyg, axis=-1, keepdims=True) - xhat * jnp.mean(dyg * xhat, axis=-1, keepdims=True))
        dr_ref[...] = ALPHA * dpre
        df_ref[...] = (c * dpre).astype(BF16)
        _accum(dg_ref, jnp.sum(r * xhat, axis=0, keepdims=True), first)
        _accum(db_ref, jnp.sum(r, axis=0, keepdims=True), first)

    return (fn, [res, f], [g, b], (F32, BF16), (1, 1))


def _rows(body, T, tm, ins, outs, name, accs=()):
    in_specs, args = [], []
    for arr, w, cb in ins:
        if w is None:
            in_specs.append(pl.BlockSpec(arr.shape, lambda i, _n=arr.ndim: (0,) * _n))
        else:
            in_specs.append(pl.BlockSpec((tm, w), lambda i, _cb=cb: (i, _cb)))
        args.append(arr)
    out_specs = [pl.BlockSpec((tm, w), lambda i: (i, 0)) for w, _ in outs]
    out_shape = [jax.ShapeDtypeStruct((T, w), dt) for w, dt in outs]
    for r, w in accs:
        out_specs.append(pl.BlockSpec((r, w), lambda i: (0, 0)))
        out_shape.append(jax.ShapeDtypeStruct((r, w), F32))
    return _pcall(
        body, grid=(T // tm,), in_specs=in_specs, out_specs=out_specs, out_shape=out_shape, name=name,
        compiler_params=pltpu.CompilerParams(dimension_semantics=("arbitrary",)),
    )(*args)


FFN_TN = 1408
FFN_TM = 512
LN_TM = 512


def _ffn_in(x, wt, name, order_after=None):
    T = x.shape[0]
    tm, tn = min(FFN_TM, T), FFN_TN
    nj = D_FF // tn
    emit_xb = x.dtype != BF16
    n_in = 3 + int(order_after is not None)

    def body(*refs):
        x_ref, wg_ref, wu_ref = refs[:3]
        hs_ref, gu_ref = refs[n_in], refs[n_in + 1]
        xv = x_ref[...].astype(BF16)
        g = _dot(xv, wg_ref[...], 1, 1)
        u = _dot(xv, wu_ref[...], 1, 1)
        hs_ref[...] = (_silu(g) * u).astype(BF16)
        gu_ref[0] = g.astype(BF16)
        gu_ref[1] = u.astype(BF16)
        if emit_xb:
            refs[n_in + 2][...] = xv

    in_specs = [pl.BlockSpec((tm, D), lambda i, j: (i, 0)), pl.BlockSpec((tn, D), lambda i, j: (j, 0)),
                pl.BlockSpec((tn, D), lambda i, j: (j + nj, 0))]
    args = [x, wt, wt]
    if order_after is not None:
        in_specs.append(pl.BlockSpec(order_after.shape, lambda i, j: (0, 0)))
        args.append(order_after)
    out_specs = [pl.BlockSpec((tm, tn), lambda i, j: (i, j)), pl.BlockSpec((2, tm, tn), lambda i, j: (0, i, j))]
    out_shape = [jax.ShapeDtypeStruct((T, D_FF), BF16), jax.ShapeDtypeStruct((2, T, D_FF), BF16)]
    if emit_xb:
        out_specs.append(pl.BlockSpec((tm, D), lambda i, j: (i, 0)))
        out_shape.append(jax.ShapeDtypeStruct((T, D), BF16))
    return _pcall(
        body, grid=(T // tm, nj), in_specs=in_specs, out_specs=out_specs, out_shape=out_shape, name=name,
        compiler_params=pltpu.CompilerParams(dimension_semantics=("parallel", "arbitrary")),
    )(*args)


def _ffn_act_bwd(df, wo, gu, name):
    T = df.shape[0]
    tm, tn = min(FFN_TM, T), FFN_TN

    def body(d_ref, w_ref, gu_ref, o_ref):
        dhs = _dot(d_ref[...], w_ref[...], 1, 1)
        g, u = gu_ref[0].astype(F32), gu_ref[1].astype(F32)
        sig = _sigmoid(g)
        act = g * sig
        o_ref[0] = (dhs * u * (sig + act * (1.0 - sig))).astype(BF16)
        o_ref[1] = (dhs * act).astype(BF16)

    return _pcall(
        body, grid=(T // tm, D_FF // tn),
        in_specs=[pl.BlockSpec((tm, D), lambda i, j: (i, 0)), pl.BlockSpec((tn, D), lambda i, j: (j, 0)),
                  pl.BlockSpec((2, tm, tn), lambda i, j: (0, i, j))],
        out_specs=pl.BlockSpec((2, tm, tn), lambda i, j: (0, i, j)),
        out_shape=jax.ShapeDtypeStruct((2, T, D_FF), BF16), name=name,
        compiler_params=pltpu.CompilerParams(dimension_semantics=("parallel", "parallel")),
    )(df, wo, gu)


def _rope128(t, cos, sin_s):
    return t * cos + pltpu.roll(t, 64, 1) * sin_s


def _rope128_t(g, cos, sin_s):
    return g * cos - pltpu.roll(g, 64, 1) * sin_s


def _partner32(t):
    lane = lax.broadcasted_iota(jnp.int32, t.shape, 1)
    return jnp.where((lane & 32) == 0, pltpu.roll(t, 96, 1), pltpu.roll(t, 32, 1))


def _rope64(t, cos, sin_s):
    return t * cos + _partner32(t) * sin_s


def _rope64_t(g, cos, sin_s):
    return g * cos - _partner32(g) * sin_s


def _mixer_prep_fwd(proj, tabs, qg, kvg, T):
    cos128, sin128, cos64, sin64 = tabs

    def body(rq_ref, rk_ref, cq_ref, ckv_ref, kpe_ref, c1_ref, s1_ref, c2_ref, s2_ref, qg_ref, kvg_ref,
             oq_ref, ok_ref, oqn_ref, okvn_ref, okpe_ref):
        c1, s1 = c1_ref[...], s1_ref[...]
        for h in range(HEADS):
            sl = slice(h * RET_DK, (h + 1) * RET_DK)
            oq_ref[:, sl] = _rope128(rq_ref[:, sl].astype(F32), c1, s1).astype(BF16)
            ok_ref[:, sl] = (_rope128(rk_ref[:, sl].astype(F32), c1, s1) * RET_SCALE).astype(BF16)
        oqn_ref[...] = _rms(cq_ref[...].astype(F32), qg_ref[...]).astype(BF16)
        okvn_ref[...] = _rms(ckv_ref[...].astype(F32), kvg_ref[...]).astype(BF16)
        okpe_ref[...] = _rope64(kpe_ref[...].astype(F32), c2_ref[...], s2_ref[...]).astype(BF16)

    ins = [(proj, 1024, 0), (proj, 1024, 1), (proj, 256, P_CQ // 256), (proj, 256, P_CKV // 256),
           (proj, 128, P_KPE // 128), (cos128, 128, 0), (sin128, 128, 0), (cos64, 128, 0), (sin64, 128, 0),
           (qg, None, None), (kvg, None, None)]
    outs = [(1024, BF16), (1024, BF16), (LORA, BF16), (LORA, BF16), (128, BF16)]
    return _rows(body, T, 256, ins, outs, "mixer_prep_fwd")


def _rms_bwd(proj, dqn, dkvn, dkpe_all, tabs, qg, kvg, T):
    _, _, cos64, sin64 = tabs

    def body(cq_ref, ckv_ref, dq_ref, dkv_ref, dk_ref, c2_ref, s2_ref, qg_ref, kvg_ref, ocq_ref, ockv_ref, okpe_ref,
             dqg_ref, dkvg_ref):
        _, vjp = jax.vjp(_rms, cq_ref[...].astype(F32), qg_ref[...])
        dx, dg = vjp(dq_ref[...])
        ocq_ref[...] = dx.astype(BF16)
        _accum(dqg_ref, dg)
        _, vjp = jax.vjp(_rms, ckv_ref[...].astype(F32), kvg_ref[...])
        dx, dg = vjp(dkv_ref[...])
        ockv_ref[...] = dx.astype(BF16)
        _accum(dkvg_ref, dg)
        g = dk_ref[:, 0:128]
        for h in range(1, HEADS):
            g = g + dk_ref[:, h * 128:(h + 1) * 128]
        lane = lax.broadcasted_iota(jnp.int32, g.shape, 1)
        g = jnp.where(lane < MLA_ROPE, g, 0.0)
        okpe_ref[...] = _rope64_t(g, c2_ref[...], s2_ref[...]).astype(BF16)

    ins = [(proj, 256, P_CQ // 256), (proj, 256, P_CKV // 256), (dqn, LORA, 0), (dkvn, LORA, 0), (dkpe_all, 1024, 0),
           (cos64, 128, 0), (sin64, 128, 0), (qg, None, None), (kvg, None, None)]
    return _rows(body, T, 256, ins, [(LORA, BF16), (LORA, BF16), (128, BF16)], "rms_bwd", accs=[(1, LORA), (1, LORA)])


def _gn_gate(y, rg, g):
    mu = jnp.mean(y, axis=-1, keepdims=True)
    var = jnp.mean(jnp.square(y - mu), axis=-1, keepdims=True)
    return _silu(rg) * ((y - mu) * lax.rsqrt(var + EPS) * g)


def _q_assemble_epilogue(tabs):
    _, _, cos64, sin64 = tabs

    def fn(q, ins, outs, first):
        c_ref, s_ref = ins
        on_ref, op_ref = outs
        on_ref[...] = (q[:, :1024] * Q_PRESCALE).astype(BF16)
        c, s = c_ref[...], s_ref[...]
        lane = lax.broadcasted_iota(jnp.int32, c.shape, 1)
        for j in range(HEADS // 2):
            r = _rope64(q[:, 1024 + 128 * j:1024 + 128 * (j + 1)], c, s) * Q_PRESCALE
            op_ref[:, 256 * j:256 * j + 128] = jnp.where(lane < 64, r, 0.0).astype(BF16)
            op_ref[:, 256 * j + 128:256 * j + 256] = jnp.where(lane < 64, pltpu.roll(r, 64, 1), 0.0).astype(BF16)

    return (fn, [(cos64, 0, 128), (sin64, 0, 128)], [], ((BF16, 1024), (BF16, 1024)), ())


def _q_assemble_bwd(dqn, dqpe, tabs, T):
    _, _, cos64, sin64 = tabs

    def body(dn_ref, dp_ref, c_ref, s_ref, on_ref, op_ref):
        on_ref[...] = dn_ref[...].astype(BF16)
        c, s = c_ref[...], s_ref[...]
        lane = lax.broadcasted_iota(jnp.int32, c.shape, 1)
        for j in range(HEADS // 2):
            g = jnp.where(lane < 64, dp_ref[:, 256 * j:256 * j + 128], pltpu.roll(dp_ref[:, 256 * j + 128:256 * j + 256], 64, 1))
            op_ref[:, 128 * j:128 * (j + 1)] = _rope64_t(g, c, s).astype(BF16)

    return _rows(body, T, 256, [(dqn, 1024, 0), (dqpe, 1024, 0), (cos64, 128, 0), (sin64, 128, 0)],
                 [(1024, BF16), (512, BF16)], "q_assemble_bwd")


def _mix_fn(gr, gm, yr, ym):
    return _sigmoid(gr) * yr + _sigmoid(gm) * ym


def _mix_epilogue(proj, y_ret):
    def fn(r, ins, outs, first):
        gr_ref, gm_ref, yr_ref = ins
        ym_ref, mix_ref = outs
        ym_ref[...] = r
        mix_ref[...] = _mix_fn(gr_ref[...].astype(F32), gm_ref[...].astype(F32), yr_ref[...], r).astype(BF16)

    return (fn, [(proj, P_GR // D), (proj, P_GM // D), y_ret], [], (F32, BF16), ())


def _mix_bwd_epilogue(proj, y_ret, y_mla):
    def fn(r, ins, outs, first):
        gr_ref, gm_ref, yr_ref, ym_ref = ins
        _, vjp = jax.vjp(_mix_fn, gr_ref[...].astype(F32), gm_ref[...].astype(F32), yr_ref[...], ym_ref[...])
        for o_ref, d in zip(outs, vjp(r)):
            o_ref[...] = d.astype(BF16)

    return (fn, [(proj, P_GR // D), (proj, P_GM // D), y_ret, y_mla], [], (BF16,) * 4, ())


def _head_epilogue(h3, pp, tgt, g, b):
    def fn(r, ins, outs, first):
        h_ref, pp_ref, t_ref, g_ref, b_ref = ins
        dh_ref, dgl_ref, dpp_ref, dg_ref, db_ref, loss_ref = outs

        def f(h, gg, p, lg, lb):
            return _ln(ALPHA * h + _sigmoid(gg) * p, lg, lb)

        h4, vjp = jax.vjp(f, h_ref[...], r, pp_ref[...], g_ref[...], b_ref[...])
        err = h4 - t_ref[...]
        dh, dgl, dpp, dg, db = vjp(err * (1.0 / D))
        dh_ref[...] = dh
        dgl_ref[...] = dgl.astype(BF16)
        dpp_ref[...] = dpp.astype(BF16)
        _accum(dg_ref, dg, first)
        _accum(db_ref, db, first)
        part = 0.5 * jnp.sum(jnp.mean(jnp.square(err), axis=-1, keepdims=True), axis=0, keepdims=True)
        _accum(loss_ref, jnp.broadcast_to(part, loss_ref.shape), first)

    return (fn, [h3, pp, tgt], [g, b], (F32, BF16, BF16), (1, 1, 8))


def _delta_epilogue(o):
    def fn(r, ins, outs, first):
        (o_ref,) = ins
        db_ref, dl_ref = outs
        db_ref[...] = r.astype(BF16)
        for h in range(HEADS):
            sl = slice(h * MLA_DV, (h + 1) * MLA_DV)
            dl = jnp.sum(r[:, sl] * o_ref[:, sl].astype(F32), axis=-1, keepdims=True)
            dl_ref[:, sl] = jnp.broadcast_to(dl, (r.shape[0], MLA_DV))

    return (fn, [o], [], (BF16, F32), ())


def _ret_consts():
    L = RET_L
    lg = jnp.log(1.0 - 2.0 ** (-5.0 - jnp.arange(HEADS, dtype=F32)))[:, None, None]
    idx = jnp.arange(L, dtype=F32)
    ch = jnp.arange(L) // CHUNK
    dist = idx[:, None] - idx[None, :]
    same = (ch[:, None] == ch[None, :])[None]
    earlier = (ch[None, :] < ch[:, None])[None]
    dm = jnp.where(same, jnp.exp(lg * jnp.abs(dist)[None]), jnp.where(earlier, jnp.exp(lg * dist[None]), 0.0))
    xi = jnp.broadcast_to(jnp.exp(lg * (idx + 1.0)[None, :, None]), (HEADS, L, 128))
    zeta = jnp.broadcast_to(jnp.exp(lg * (L - 1.0 - idx)[None, :, None]), (HEADS, L, 128))
    gl = jnp.broadcast_to(jnp.exp(lg * float(L)), (HEADS, 8, 128))
    return dm.astype(F32), xi.astype(F32), zeta.astype(F32), gl.astype(F32)


def _whole(arr):
    return pl.BlockSpec(arr.shape, lambda n, _nd=arr.ndim: (0,) * _nd)


def _ret_fwd(q, k, v, proj, gn_g, consts, T):
    dm, xi, zeta, gl = consts
    L = RET_L
    n_sc = T // L

    def body(q_ref, k_ref, v_ref, rg_ref, g_ref, dm_ref, xi_ref, ze_ref, gl_ref, y_ref, yr_ref, s_ref, st_ref):
        @pl.when(pl.program_id(0) == 0)
        def _():
            st_ref[...] = jnp.zeros_like(st_ref)

        for h in range(HEADS):
            ks, vs = slice(h * RET_DK, (h + 1) * RET_DK), slice(h * RET_DV, (h + 1) * RET_DV)
            qq, kk, vv = q_ref[:, ks], k_ref[:, ks], v_ref[:, vs]
            st = st_ref[h]
            s_ref[h, 0] = st
            p = (_dot(qq, kk, 1, 1) * dm_ref[h]).astype(BF16)
            cross = _dot(qq, st.astype(BF16), 1, 0)
            xi_c = jnp.concatenate([xi_ref[h], xi_ref[h]], axis=1)
            y = _dot(p, vv, 1, 0) + cross * xi_c
            y_ref[:, vs] = y
            yr_ref[:, vs] = _gn_gate(y, rg_ref[:, vs].astype(F32), g_ref[:, vs]).astype(BF16)
            kz = (kk.astype(F32) * ze_ref[h]).astype(BF16)
            gl2 = jnp.concatenate([gl_ref[h, 0:1, :], gl_ref[h, 0:1, :]], axis=1)
            st_ref[h] = st * gl2 + _dot(kz, vv, 0, 0)

    return _pcall(
        body, grid=(n_sc,),
        in_specs=[pl.BlockSpec((L, 1024), lambda n: (n, 0)), pl.BlockSpec((L, 1024), lambda n: (n, 0)),
                  pl.BlockSpec((L, 2048), lambda n: (n, P_RV // 2048)), pl.BlockSpec((L, 2048), lambda n: (n, P_RG // 2048)),
                  _whole(gn_g), _whole(dm), _whole(xi), _whole(zeta), _whole(gl)],
        out_specs=[pl.BlockSpec((L, 2048), lambda n: (n, 0)), pl.BlockSpec((L, 2048), lambda n: (n, 0)),
                   pl.BlockSpec((HEADS, 1, 128, 256), lambda n: (0, n, 0, 0))],
        out_shape=[jax.ShapeDtypeStruct((T, HEADS * RET_DV), F32), jax.ShapeDtypeStruct((T, HEADS * RET_DV), BF16),
                   jax.ShapeDtypeStruct((HEADS, n_sc, 128, 256), F32)],
        scratch_shapes=[pltpu.VMEM((HEADS, 128, 256), F32)], name="ret_fwd",
        compiler_params=pltpu.CompilerParams(dimension_semantics=("arbitrary",)),
    )(q, k, v, proj, gn_g, dm, xi, zeta, gl)


def _ret_bwd(q, k, v, y, proj, gn_g, states, dyr, consts, tabs, T):
    dm, xi, zeta, gl = consts
    cos128, sin128, _, _ = tabs
    L = RET_L
    n_sc = T // L

    def body(q_ref, k_ref, v_ref, y_ref, rg_ref, g_ref, d_ref, s_ref, dm_ref, xi_ref, ze_ref, gl_ref, c_ref, sn_ref,
             dq_ref, dk_ref, dv_ref, drg_ref, dg_ref, gs_ref):
        @pl.when(pl.program_id(0) == 0)
        def _():
            gs_ref[...] = jnp.zeros_like(gs_ref)

        c, sn = c_ref[...], sn_ref[...]
        dgs = []
        for h in range(HEADS):
            ks, vs = slice(h * RET_DK, (h + 1) * RET_DK), slice(h * RET_DV, (h + 1) * RET_DV)
            _, vjp = jax.vjp(_gn_gate, y_ref[:, vs], rg_ref[:, vs].astype(F32), g_ref[:, vs])
            dy, drg, dg = vjp(d_ref[:, vs])
            drg_ref[:, vs] = drg.astype(BF16)
            dgs.append(dg)
            qq, kk, vv, dyy = q_ref[:, ks], k_ref[:, ks], v_ref[:, vs], dy.astype(BF16)
            dmm = dm_ref[h]
            gb = gs_ref[h].astype(BF16)
            sb = s_ref[h, 0].astype(BF16)
            xi_c = jnp.concatenate([xi_ref[h], xi_ref[h]], axis=1)
            pb = (_dot(qq, kk, 1, 1) * dmm).astype(BF16)
            kz = (kk.astype(F32) * ze_ref[h]).astype(BF16)
            dv_ref[:, vs] = (_dot(pb, dyy, 0, 0) + _dot(kz, gb, 1, 0)).astype(BF16)
            da = (_dot(dyy, vv, 1, 1) * dmm).astype(BF16)
            dyx = (dyy.astype(F32) * xi_c).astype(BF16)
            dq = _dot(da, kk, 1, 0) + _dot(dyx, sb, 1, 1)
            dk = _dot(da, qq, 0, 0) + _dot(vv, gb, 1, 1) * ze_ref[h]
            dq_ref[:, ks] = _rope128_t(dq, c, sn).astype(BF16)
            dk_ref[:, ks] = (_rope128_t(dk, c, sn) * RET_SCALE).astype(BF16)
            gl2 = jnp.concatenate([gl_ref[h, 0:1, :], gl_ref[h, 0:1, :]], axis=1)
            gs_ref[h] = gs_ref[h] * gl2 + _dot(qq, dyx, 0, 0)
        _accum(dg_ref, jnp.concatenate(dgs, axis=1))

    rev = lambda n: n_sc - 1 - n
    return _pcall(
        body, grid=(n_sc,),
        in_specs=[pl.BlockSpec((L, 1024), lambda n: (rev(n), 0)), pl.BlockSpec((L, 1024), lambda n: (rev(n), 0)),
                  pl.BlockSpec((L, 2048), lambda n: (rev(n), P_RV // 2048)), pl.BlockSpec((L, 2048), lambda n: (rev(n), 0)),
                  pl.BlockSpec((L, 2048), lambda n: (rev(n), P_RG // 2048)), _whole(gn_g),
                  pl.BlockSpec((L, 2048), lambda n: (rev(n), 0)),
                  pl.BlockSpec((HEADS, 1, 128, 256), lambda n: (0, rev(n), 0, 0)),
                  _whole(dm), _whole(xi), _whole(zeta), _whole(gl),
                  pl.BlockSpec((L, 128), lambda n: (rev(n), 0)), pl.BlockSpec((L, 128), lambda n: (rev(n), 0))],
        out_specs=[pl.BlockSpec((L, 1024), lambda n: (rev(n), 0)), pl.BlockSpec((L, 1024), lambda n: (rev(n), 0)),
                   pl.BlockSpec((L, 2048), lambda n: (rev(n), 0)), pl.BlockSpec((L, 2048), lambda n: (rev(n), 0)),
                   pl.BlockSpec((1, 2048), lambda n: (0, 0))],
        out_shape=[jax.ShapeDtypeStruct((T, 1024), BF16), jax.ShapeDtypeStruct((T, 1024), BF16), jax.ShapeDtypeStruct((T, 2048), BF16),
                   jax.ShapeDtypeStruct((T, 2048), BF16), jax.ShapeDtypeStruct((1, 2048), F32)],
        scratch_shapes=[pltpu.VMEM((HEADS, 128, 256), F32)], name="ret_bwd",
        compiler_params=pltpu.CompilerParams(dimension_semantics=("arbitrary",)),
    )(q, k, v, y, proj, gn_g, dyr, states, dm, xi, zeta, gl, cos128, sin128)


def _diag_mask(nrows, ncols, row0):
    row = row0 + lax.broadcasted_iota(jnp.int32, (nrows, ncols), 0)
    col = lax.broadcasted_iota(jnp.int32, (nrows, ncols), 1)
    return lax.shift_right_logical(col, 6) <= lax.shift_right_logical(row, 6)


def _diag_spans(t):
    h = t // 2
    return ((0, h, h), (h, h, t)) if h % 128 == 0 else ((0, t, t),)


def _tri_steps(nb, by_key):
    if by_key:
        pairs = [(i, j) for j in range(nb) for i in range(j, nb)]
    else:
        pairs = [(i, j) for i in range(nb) for j in range(i + 1)]
    return jnp.array([a for a, _ in pairs], jnp.int32), jnp.array([b for _, b in pairs], jnp.int32)


def _attn_fwd(qn, qpe, kv, kpe, T):
    t = min(ATT_TF, T)
    nb = T // t
    ii, jj = _tri_steps(nb, by_key=False)

    hp = ATT_HP
    w = 128 * hp

    def body(ii_ref, jj_ref, qn_ref, qp_ref, kn_ref, kp_ref, v_ref, o_ref, lse_ref, m_sc, l_sc, acc_sc):
        st = pl.program_id(1)
        i, j = ii_ref[st], jj_ref[st]

        @pl.when(j == 0)
        def _():
            m_sc[...] = jnp.full_like(m_sc, NEG)
            l_sc[...] = jnp.zeros_like(l_sc)
            acc_sc[...] = jnp.zeros_like(acc_sc)

        def update(diag):
            kp = kp_ref[...]
            for hh in range(hp):
                sl = slice(128 * hh, 128 * (hh + 1))
                q = jnp.concatenate([qn_ref[:, sl], qp_ref[:, sl]], axis=1)
                k = jnp.concatenate([kn_ref[:, sl], kp], axis=1)
                s = _dot(q, k, 1, 1)
                if diag:
                    s = jnp.where(_diag_mask(t, t, 0), s, NEG)
                m_prev = m_sc[:, sl]
                m_new = jnp.maximum(m_prev, jnp.max(s, axis=1, keepdims=True))
                a = jnp.exp2(m_prev - m_new)
                p = jnp.exp2(s - m_new[:, 0:1])
                l_sc[:, sl] = a * l_sc[:, sl] + jnp.sum(p, axis=1, keepdims=True)
                acc_sc[:, sl] = a * acc_sc[:, sl] + _dot(p.astype(BF16), v_ref[:, sl], 1, 0)
                m_sc[:, sl] = m_new

        @pl.when(j < i)
        def _():
            update(False)

        @pl.when(j == i)
        def _():
            update(True)
            o_ref[...] = (acc_sc[...] / l_sc[...]).astype(BF16)
            lse_ref[...] = m_sc[...] + jnp.log2(l_sc[...])

    qs = pl.BlockSpec((t, w), lambda h, s, ii, jj: (ii[s], h))
    grid_spec = pltpu.PrefetchScalarGridSpec(
        num_scalar_prefetch=2, grid=(HEADS // hp, int(ii.shape[0])),
        in_specs=[qs, qs, pl.BlockSpec((t, w), lambda h, s, ii, jj: (jj[s], h)), pl.BlockSpec((t, 128), lambda h, s, ii, jj: (jj[s], 0)),
                  pl.BlockSpec((t, w), lambda h, s, ii, jj: (jj[s], HEADS // hp + h))],
        out_specs=[qs, qs],
        scratch_shapes=[pltpu.VMEM((t, w), F32), pltpu.VMEM((t, w), F32), pltpu.VMEM((t, w), F32)])
    return _pcall(
        body, grid_spec=grid_spec, out_shape=[jax.ShapeDtypeStruct((T, D), BF16), jax.ShapeDtypeStruct((T, D), F32)], name="attn_fwd",
        compiler_params=pltpu.CompilerParams(dimension_semantics=("arbitrary", "arbitrary")),
    )(ii, jj, qn, qpe, kv, kpe, kv)


def _attn_bwd(qn, qpe, kv, kpe, do, lse, delta, T):
    t = min(ATT_TB, T)
    nb = T // t
    ii, jj = _tri_steps(nb, by_key=True)

    def body(ii_ref, jj_ref, qn_ref, qp_ref, kn_ref, kp_ref, v_ref, do_ref, lse_ref, dl_ref,
             dqn_ref, dqp_ref, dkn_ref, dkp_ref, dv_ref, dk_sc, dv_sc):
        st = pl.program_id(1)
        i, j = ii_ref[st], jj_ref[st]

        @pl.when(st == 0)
        def _():
            dqn_ref[...] = jnp.zeros_like(dqn_ref)
            dqp_ref[...] = jnp.zeros_like(dqp_ref)

        @pl.when(i == j)
        def _():
            dk_sc[...] = jnp.zeros_like(dk_sc)
            dv_sc[...] = jnp.zeros_like(dv_sc)

        def update(diag):
            for row0, nr, nkeys in (_diag_spans(t) if diag else ((0, t, t),)):
                rows = slice(row0, row0 + nr)
                q = jnp.concatenate([qn_ref[rows, :], qp_ref[rows, :]], axis=1)
                k = jnp.concatenate([kn_ref[:nkeys, :], kp_ref[:nkeys, :]], axis=1)
                dob = do_ref[rows, :]
                s = _dot(q, k, 1, 1)
                if diag:
                    s = jnp.where(_diag_mask(nr, nkeys, row0), s, NEG)
                p = jnp.exp2(s - lse_ref[rows, 0:1])
                dv_sc[:nkeys, :] += _dot(p.astype(BF16), dob, 0, 0)
                dp = _dot(dob, v_ref[:nkeys, :], 1, 1)
                ds = (p * (dp - dl_ref[rows, 0:1])).astype(BF16)
                dk_sc[:nkeys, :] += _dot(ds, q, 0, 0)
                dq = _dot(ds, k, 1, 0) * MLA_SCALE
                grows = pl.ds(pl.multiple_of(i * t + row0, nr), nr)
                dqn_ref[grows, :] += dq[:, :128]
                dqp_ref[grows, :] += dq[:, 128:]

        @pl.when(i > j)
        def _():
            update(False)

        @pl.when(i == j)
        def _():
            update(True)

        @pl.when(i == nb - 1)
        def _():
            dkn_ref[...] = (dk_sc[:, :128] * (1.0 / LOG2E)).astype(BF16)
            dkp_ref[...] = dk_sc[:, 128:] * (1.0 / LOG2E)
            dv_ref[...] = dv_sc[...].astype(BF16)

    qs = pl.BlockSpec((t, 128), lambda h, s, ii, jj: (ii[s], h))
    ks = pl.BlockSpec((t, 128), lambda h, s, ii, jj: (jj[s], h))
    hs = pl.BlockSpec((T, 128), lambda h, s, ii, jj: (0, h))
    grid_spec = pltpu.PrefetchScalarGridSpec(
        num_scalar_prefetch=2, grid=(HEADS, int(ii.shape[0])),
        in_specs=[qs, qs, ks, pl.BlockSpec((t, 128), lambda h, s, ii, jj: (jj[s], 0)),
                  pl.BlockSpec((t, 128), lambda h, s, ii, jj: (jj[s], HEADS + h)), qs, qs, qs],
        out_specs=[hs, hs, ks, ks, ks],
        scratch_shapes=[pltpu.VMEM((t, 256), F32), pltpu.VMEM((t, 128), F32)])
    return _pcall(
        body, grid_spec=grid_spec,
        out_shape=[jax.ShapeDtypeStruct((T, D), F32), jax.ShapeDtypeStruct((T, D), F32), jax.ShapeDtypeStruct((T, D), BF16),
                   jax.ShapeDtypeStruct((T, D), F32), jax.ShapeDtypeStruct((T, D), BF16)], name="attn_bwd",
        compiler_params=pltpu.CompilerParams(dimension_semantics=("arbitrary", "arbitrary")),
    )(ii, jj, qn, qpe, kv, kpe, kv, do, lse, delta)


def _mesh_pos():
    x, y, c = lax.axis_index("x"), lax.axis_index("y"), lax.axis_index("c")
    return x, y, c, 4 * x + 2 * y + c


def _peer(x, y, c, k):
    px, py, pc = (x + ((k >> 2) & 1)) % 2, (y + ((k >> 1) & 1)) % 2, (c + (k & 1)) % 2
    return (px, py, pc), 4 * px + 2 * py + pc


_ANY = pl.BlockSpec(memory_space=pl.ANY)


def _rcopy(src, dst, send_sems, recv_sems, k, to):
    return pltpu.make_async_remote_copy(src_ref=src, dst_ref=dst, send_sem=send_sems.at[k], recv_sem=recv_sems.at[k],
                                        device_id=to, device_id_type=pl.DeviceIdType.MESH)


def _all_gather(wsh, ssh):
    def body(w_ref, s_ref, wall_ref, sall_ref, send_sems, recv_sems, loc_sems):
        x, y, c, me = _mesh_pos()
        sib = (x, y, 1 - c)
        chips = [(1 - x, y), (x, 1 - y), (1 - x, 1 - y)]
        slot = lambda px, py, pc: 4 * px + 2 * py + pc
        loc = [pltpu.make_async_copy(s_ref, sall_ref.at[me], loc_sems.at[0])]
        for cp in loc:
            cp.start()
        sends, fwd_waits = [], []
        for n, (src, dst) in enumerate(((w_ref, wall_ref), (s_ref, sall_ref))):
            o = 7 * n
            sends.append(_rcopy(src, dst.at[me], send_sems, recv_sems, o, sib))
            for j, chip in enumerate(chips):
                sends.append(_rcopy(src, dst.at[me], send_sems, recv_sems, o + 1 + j, (*chip, c)))
        for cp in sends:
            cp.start()
        for n, (src, dst) in enumerate(((w_ref, wall_ref), (s_ref, sall_ref))):
            o = 7 * n
            for j, chip in enumerate(chips):
                got = dst.at[slot(*chip, c)]
                _rcopy(src, got, send_sems, recv_sems, o + 1 + j, sib).wait_recv()
                fw = _rcopy(got, got, send_sems, recv_sems, o + 4 + j, sib)
                fw.start()
                sends.append(fw)
            fwd_waits.append(_rcopy(src, dst.at[slot(x, y, 1 - c)], send_sems, recv_sems, o, sib))
            for j, chip in enumerate(chips):
                fwd_waits.append(_rcopy(src, dst.at[slot(*chip, 1 - c)], send_sems, recv_sems, o + 4 + j, sib))
        for cp in fwd_waits:
            cp.wait_recv()
        for cp in sends:
            cp.wait_send()
        for cp in loc:
            cp.wait()

    wall, sall = _pcall(
        body, in_specs=[_ANY, _ANY], out_specs=[_ANY, _ANY],
        out_shape=[jax.ShapeDtypeStruct((N_DEV,) + wsh.shape, wsh.dtype), jax.ShapeDtypeStruct((N_DEV,) + ssh.shape, ssh.dtype)],
        scratch_shapes=[pltpu.SemaphoreType.DMA((14,)), pltpu.SemaphoreType.DMA((14,)), pltpu.SemaphoreType.DMA((1,))],
        name="all_gather_weights",
    )(wsh, ssh)
    me = 4 * lax.axis_index("x") + 2 * lax.axis_index("y") + lax.axis_index("c")
    return lax.dynamic_update_index_in_dim(wall, wsh, me, axis=0), sall


_HBM = pl.BlockSpec(memory_space=pltpu.HBM)
_SEM = pl.BlockSpec(memory_space=pltpu.SEMAPHORE)
_EFFECT = pltpu.SideEffectType.DATAFLOW_SIDE_EFFECTING


def _other_chips(x, y):
    return [(1 - x, y), (x, 1 - y), (1 - x, 1 - y)]


def _gather_start(wsh, order_after):
    def body(w_ref, land_ref, dep_ref, send_sems, recv_sems, w_thru, land_thru, token):
        x, y, c, me = _mesh_pos()
        _rcopy(w_ref, land_ref.at[me], send_sems, recv_sems, 0, (x, y, 1 - c)).start()
        for j, chip in enumerate(_other_chips(x, y)):
            _rcopy(w_ref, land_ref.at[me], send_sems, recv_sems, 1 + j, (*chip, c)).start()
        token[...] = jnp.zeros_like(token)

    shape = (N_DEV,) + wsh.shape
    land = pltpu.with_memory_space_constraint(lax.empty(shape, wsh.dtype), pltpu.HBM)
    return _pcall(
        body, name="gather_start",
        out_shape=(pltpu.SemaphoreType.DMA((4,)), pltpu.SemaphoreType.DMA((4,)), pltpu.HBM(wsh.shape, wsh.dtype),
                   pltpu.HBM(shape, wsh.dtype), jax.ShapeDtypeStruct((8, 128), F32)),
        in_specs=(_HBM, _HBM, _ANY), out_specs=(_SEM, _SEM, _HBM, _HBM, pl.BlockSpec(memory_space=pltpu.VMEM)),
        input_output_aliases={0: 2, 1: 3}, compiler_params=pltpu.CompilerParams(has_side_effects=_EFFECT),
    )(pltpu.with_memory_space_constraint(wsh, pltpu.HBM), land, order_after)


def _gather_wait(send_sems, recv_sems, w_thru, land_thru, after):
    def body(w_ref, land_ref, send_sems, recv_sems, after_ref, w_out, land_out):
        x, y, c, _ = _mesh_pos()
        senders = [(x, y, 1 - c)] + [(*chip, c) for chip in _other_chips(x, y)]
        for k, (px, py, pc) in enumerate(senders):
            cp = _rcopy(w_ref, land_ref.at[4 * px + 2 * py + pc], send_sems, recv_sems, k, (px, py, pc))
            cp.wait_send()
            cp.wait_recv()

    return _pcall(
        body, name="gather_wait", out_shape=(pltpu.HBM(w_thru.shape, w_thru.dtype), pltpu.HBM(land_thru.shape, land_thru.dtype)),
        in_specs=(_HBM, _HBM, _SEM, _SEM, _ANY), out_specs=(_HBM, _HBM), input_output_aliases={0: 0, 1: 1},
        compiler_params=pltpu.CompilerParams(has_side_effects=_EFFECT),
    )(w_thru, land_thru, send_sems, recv_sems, after)


def _gather_finish(wsh, land):
    def body(land_ref, out_ref, send_sems, recv_sems):
        x, y, c, _ = _mesh_pos()
        sib = (x, y, 1 - c)
        sends, recvs = [], []
        for j, (px, py) in enumerate(_other_chips(x, y)):
            mine, theirs = 4 * px + 2 * py + c, 4 * px + 2 * py + (1 - c)
            sends.append(_rcopy(land_ref.at[mine], out_ref.at[mine], send_sems, recv_sems, j, sib))
            recvs.append(_rcopy(land_ref.at[theirs], out_ref.at[theirs], send_sems, recv_sems, j, sib))
        for cp in sends:
            cp.start()
        for cp in recvs:
            cp.wait_recv()
        for cp in sends:
            cp.wait_send()

    done = _pcall(
        body, in_specs=[_ANY], out_specs=_ANY, out_shape=jax.ShapeDtypeStruct(land.shape, land.dtype),
        input_output_aliases={0: 0},
        scratch_shapes=[pltpu.SemaphoreType.DMA((3,)), pltpu.SemaphoreType.DMA((3,))], name="gather_finish",
    )(land)
    me = 4 * lax.axis_index("x") + 2 * lax.axis_index("y") + lax.axis_index("c")
    return lax.dynamic_update_index_in_dim(done, wsh, me, axis=0)


_ROW_TILES = (400, 368, 352, 256, 128, 16, 8)


def _exchange_sibling(g4, name):
    def body(g_ref, r_ref, send_sems, recv_sems):
        x, y, c, _ = _mesh_pos()
        sib = (x, y, 1 - c)
        cps = [_rcopy(g_ref.at[q, 1 - c], r_ref.at[q], send_sems, recv_sems, q, sib) for q in range(4)]
        for cp in cps:
            cp.start()
        for cp in cps:
            cp.wait()

    n, _, R, C = g4.shape
    return _pcall(
        body, in_specs=[_ANY], out_specs=_ANY, out_shape=jax.ShapeDtypeStruct((n, R, C), g4.dtype),
        scratch_shapes=[pltpu.SemaphoreType.DMA((4,)), pltpu.SemaphoreType.DMA((4,))], name=name,
    )(g4)


def _sum_sibling(g4, rsib, cvec, name):
    n, _, R, C = g4.shape
    tr = _pick(R, _ROW_TILES)

    def body(c_ref, g_ref, r_ref, o_ref):
        o_ref[...] = (g_ref[...].astype(F32) + r_ref[...].astype(F32)).astype(o_ref.dtype)

    grid_spec = pltpu.PrefetchScalarGridSpec(
        num_scalar_prefetch=1, grid=(n, R // tr),
        in_specs=[pl.BlockSpec((None, None, tr, C), lambda q, i, cr: (q, cr[0], i, 0)), pl.BlockSpec((None, tr, C), lambda q, i, cr: (q, i, 0))],
        out_specs=pl.BlockSpec((None, tr, C), lambda q, i, cr: (q, i, 0)))
    return _pcall(body, grid_spec=grid_spec, out_shape=jax.ShapeDtypeStruct((n, R, C), g4.dtype), name=name)(cvec, g4, rsib)


def _chips_start(part, name, order_after=None):
    n_in = 2 + int(order_after is not None)

    def body(*refs):
        p_ref, land_ref = refs[:2]
        send_sems, recv_sems, token = refs[n_in], refs[n_in + 1], refs[n_in + 4]
        x, y, c, _ = _mesh_pos()
        myq = 2 * x + y
        for j, (px, py) in enumerate(_other_chips(x, y)):
            _rcopy(p_ref.at[2 * px + py], land_ref.at[myq], send_sems, recv_sems, j, (px, py, c)).start()
        token[...] = jnp.zeros_like(token)

    land = pltpu.with_memory_space_constraint(lax.empty(part.shape, part.dtype), pltpu.HBM)
    extra = () if order_after is None else (order_after,)
    return _pcall(
        body, name=name,
        out_shape=(pltpu.SemaphoreType.DMA((3,)), pltpu.SemaphoreType.DMA((3,)), pltpu.HBM(part.shape, part.dtype),
                   pltpu.HBM(part.shape, part.dtype), jax.ShapeDtypeStruct((8, 128), F32)),
        in_specs=(_HBM, _HBM) + (_ANY,) * len(extra),
        out_specs=(_SEM, _SEM, _HBM, _HBM, pl.BlockSpec(memory_space=pltpu.VMEM)),
        input_output_aliases={0: 2, 1: 3}, compiler_params=pltpu.CompilerParams(has_side_effects=_EFFECT),
    )(pltpu.with_memory_space_constraint(part, pltpu.HBM), land, *extra)


def _chips_wait(send_sems, recv_sems, p_thru, land_thru, after, name):
    def body(p_ref, land_ref, send_sems, recv_sems, after_ref, p_out, land_out):
        x, y, c, _ = _mesh_pos()
        for j, (px, py) in enumerate(_other_chips(x, y)):
            q = 2 * px + py
            cp = _rcopy(p_ref.at[q], land_ref.at[q], send_sems, recv_sems, j, (px, py, c))
            cp.wait_send()
            cp.wait_recv()

    return _pcall(
        body, name=name, out_shape=(pltpu.HBM(p_thru.shape, p_thru.dtype), pltpu.HBM(p_thru.shape, p_thru.dtype)),
        in_specs=(_HBM, _HBM, _SEM, _SEM, _ANY), out_specs=(_HBM, _HBM), input_output_aliases={0: 0, 1: 1},
        compiler_params=pltpu.CompilerParams(has_side_effects=_EFFECT),
    )(p_thru, land_thru, send_sems, recv_sems, after)


def _sum_chips(part, land, qvec, name):
    n, R, C = part.shape
    tr = _pick(R, _ROW_TILES)

    def body(q_ref, p_ref, *refs):
        o_ref = refs[n]
        acc = None
        for q in range(n):
            term = jnp.where(q_ref[0] == q, p_ref[...], refs[q][...]).astype(F32)
            acc = term if acc is None else acc + term
        o_ref[...] = acc

    def land_spec(q):
        return pl.BlockSpec((None, tr, C), lambda i, m: (jnp.where(m[0] == q, (q + 1) % n, q), i, 0))

    grid_spec = pltpu.PrefetchScalarGridSpec(
        num_scalar_prefetch=1, grid=(R // tr,),
        in_specs=[pl.BlockSpec((None, tr, C), lambda i, m: (m[0], i, 0))] + [land_spec(q) for q in range(n)],
        out_specs=pl.BlockSpec((tr, C), lambda i, m: (i, 0)))
    return _pcall(body, grid_spec=grid_spec, out_shape=jax.ShapeDtypeStruct((R, C), F32), name=name)(qvec, part, *([land] * n))


def _exchange_small(gsmall):
    def body(s_ref, srecv_ref, send_sems, recv_sems, loc_sem):
        x, y, c, me = _mesh_pos()
        loc = pltpu.make_async_copy(s_ref, srecv_ref.at[me], loc_sem.at[0])
        loc.start()
        sends, recvs = [], []
        for k in range(1, N_DEV):
            to, pidx = _peer(x, y, c, k)
            sends.append(_rcopy(s_ref, srecv_ref.at[me], send_sems, recv_sems, k - 1, to))
            recvs.append(_rcopy(s_ref, srecv_ref.at[pidx], send_sems, recv_sems, k - 1, to))
        for cp in sends:
            cp.start()
        for cp in recvs:
            cp.wait_recv()
        for cp in sends:
            cp.wait_send()
        loc.wait()

    return _pcall(
        body, in_specs=[_ANY], out_specs=_ANY, out_shape=jax.ShapeDtypeStruct((N_DEV,) + gsmall.shape, gsmall.dtype),
        scratch_shapes=[pltpu.SemaphoreType.DMA((7,)), pltpu.SemaphoreType.DMA((7,)), pltpu.SemaphoreType.DMA((1,))],
        name="exchange_small",
    )(gsmall)


def _sum_slots(recv, name):
    n, R, C = recv.shape
    tr = _pick(R, _ROW_TILES)

    def body(r_ref, o_ref):
        acc = r_ref[0].astype(F32)
        for s in range(1, n):
            acc = acc + r_ref[s].astype(F32)
        o_ref[...] = acc

    return _pcall(body, grid=(R // tr,), in_specs=[pl.BlockSpec((n, tr, C), lambda i: (0, i, 0))],
                  out_specs=pl.BlockSpec((tr, C), lambda i: (i, 0)), out_shape=jax.ShapeDtypeStruct((R, C), F32), name=name)(recv)


def _adamw(w, g, m, v, name, order_after=None):
    shape = w.shape
    w2, g2, m2, v2 = (a.reshape(-1, shape[-1]) for a in (w, g, m, v))
    R, C = w2.shape
    tr = _pick(R, (256, 128, 64, 32, 16, 8)) if R > 256 else R
    n_in = 4 + int(order_after is not None)

    def body(*refs):
        w_ref, g_ref, m_ref, v_ref = refs[:4]
        d_ref, nm_ref, nv_ref = refs[n_in:]
        gg = g_ref[...]
        nm = ADAM_B1 * m_ref[...] + (1.0 - ADAM_B1) * gg
        nv = ADAM_B2 * v_ref[...] + (1.0 - ADAM_B2) * jnp.square(gg)
        m_hat = nm / (1.0 - ADAM_B1 ** ADAM_STEP)
        v_hat = nv / (1.0 - ADAM_B2 ** ADAM_STEP)
        d_ref[...] = -ADAM_LR * (m_hat / (jnp.sqrt(v_hat) + ADAM_EPS) + ADAM_WD * w_ref[...])
        nm_ref[...] = nm
        nv_ref[...] = nv

    spec = pl.BlockSpec((tr, C), lambda i: (i, 0))
    in_specs, args = [spec] * 4, [w2, g2, m2, v2]
    if order_after is not None:
        in_specs.append(pl.BlockSpec(order_after.shape, lambda i: (0, 0)))
        args.append(order_after)
    d, nm, nv = _pcall(body, grid=(R // tr,), in_specs=in_specs, out_specs=[spec] * 3,
                       out_shape=[jax.ShapeDtypeStruct((R, C), F32)] * 3, name=name)(*args)
    return d.reshape(shape), nm.reshape(shape), nv.reshape(shape)


def _to_rows(name, w):
    w = w[0]
    if name in ("ffn1_w_in", "ffn2_w_in", "w_in"):
        return w.T
    if name in ("w_uq", "w_ukv", "ple_w_proj"):
        return w.T.reshape(-1, 1024)
    return w


def _from_rows(name, g, shape):
    if name in ("ffn1_w_in", "ffn2_w_in", "w_in"):
        return g.T.reshape(shape)
    if name in ("w_uq", "w_ukv", "ple_w_proj"):
        return g.reshape(-1, shape[1]).T.reshape(shape)
    return g.reshape(shape)


def _unpack(wall, group):
    out, off = {}, 0
    for name, r in group:
        out[name] = wall[:, off:off + r, :].reshape(N_DEV * r, 1024)
        off += _pad16(r)
    return out


def _w_in_internal(wt):
    return jnp.concatenate([wt[0:6144], wt[6720:8768], wt[6144:6720], jnp.zeros((P_W - W_IN_COLS, 1024), wt.dtype)], axis=0)


def _w_in_external(d):
    return jnp.concatenate([d[0:6144], d[8192:8768], d[6144:8192]], axis=0)


def _rope_tables(positions):
    pos = positions[0].astype(F32)

    def cs(half):
        inv = ROPE_BASE ** (-jnp.arange(half, dtype=F32) / half)
        ang = pos[:, None] * inv
        return jnp.cos(ang), jnp.sin(ang)

    c, s = cs(64)
    c2, s2 = cs(32)
    return (jnp.concatenate([c, c], axis=1), jnp.concatenate([-s, s], axis=1),
            jnp.concatenate([c2, c2, c2, c2], axis=1), jnp.concatenate([-s2, s2, -s2, s2], axis=1))


def _local_step(x, p, positions, target, W, ln_g, ln_b, gn_g, qg, kvg, early=None, rest_weights=None, start_token=None):
    T = x.shape[0]
    tabs = _rope_tables(positions)
    rc = _ret_consts()
    lg = [ln_g[i:i + 1] for i in range(4)]
    lb = [ln_b[i:i + 1] for i in range(4)]
    pb = p.astype(BF16)

    hs1, gu1, *xb = _ffn_in(x, W["ffn1_w_in"], "ffn1_in", order_after=start_token)
    xb = xb[0] if xb else x
    f1, h1, h1b = _mm(hs1, W["ffn1_w_out"], name="ffn1_out", tm=LN_TM, epilogue=_ln_epilogue(x, 0.5, lg[0], lb[0]))
    if rest_weights is not None:
        W = {**W, **rest_weights(h1b)}
    w_in_t = _w_in_internal(W["w_in"])
    wuq = W["w_uq"].reshape(1536, LORA).reshape(HEADS, 192, LORA)
    wuq = jnp.concatenate([wuq[:, :128].reshape(1024, LORA), wuq[:, 128:].reshape(512, LORA)], axis=0)
    wukv = W["w_ukv"].reshape(2048, LORA).reshape(HEADS, 2, 128, LORA).transpose(1, 0, 2, 3).reshape(2048, LORA)
    wp_t = W["ple_w_proj"].reshape(1024, D_PLE)
    proj = _mm(h1b, w_in_t, tb=True, out_dtype=BF16, name="mixer_in")
    rq, rk, qn, kvn, kpe = _mixer_prep_fwd(proj, tabs, qg, kvg, T)
    y, yr, states = _ret_fwd(rq, rk, proj, proj, gn_g, rc, T)
    y_ret = _mm(yr, W["w_ret_o"], name="ret_o")
    qnope, qpe = _mm(qn, wuq, tb=True, name="mla_uq", tm=LN_TM, tn=1536, epilogue=_q_assemble_epilogue(tabs))
    kv = _mm(kvn, wukv, tb=True, out_dtype=BF16, name="mla_ukv")
    o, lse = _attn_fwd(qnope, qpe, kv, kpe, T)
    y_mla, mix = _mm(o, W["w_mla_o"], name="mla_o", tm=LN_TM, epilogue=_mix_epilogue(proj, y_ret))
    mixed, h2, h2b = _mm(mix, W["w_out"], name="mixer_out", tm=LN_TM, epilogue=_ln_epilogue(h1, 1.0, lg[1], lb[1]))
    hs2, gu2 = _ffn_in(h2b, W["ffn2_w_in"], "ffn2_in")
    f2, h3, h3b = _mm(hs2, W["ffn2_w_out"], name="ffn2_out", tm=LN_TM, epilogue=_ln_epilogue(h2, 0.5, lg[2], lb[2]))
    pp = _mm(pb, wp_t, tb=True, name="ple_proj")

    G = {}
    dh3_a, dgl, dpp, dg3, db3, loss = _mm(h3b, W["ple_w_gate"], name="ple_gate", tm=LN_TM,
                                          epilogue=_head_epilogue(h3, pp, target, lg[3], lb[3]))
    G["ple_w_gate"] = _mm(h3b, dgl, ta=True, name="d_ple_gate")
    G["ple_w_proj"] = _mm(dpp, pb, ta=True, name="d_ple_proj")
    dh2_a, df2, dg2, db2 = _mm(dgl, W["ple_w_gate"], tb=True, add=dh3_a, name="dh3", tm=LN_TM,
                               epilogue=_ln_bwd_epilogue(h2, f2, 0.5, lg[2], lb[2]))
    G["ffn2_w_out"] = _mm(hs2, df2, ta=True, name="d_ffn2_out")
    da2 = _ffn_act_bwd(df2, W["ffn2_w_out"], gu2, "ffn2_act_bwd")
    G["ffn2_w_in"] = _mm(da2, h2b, ta=True, name="d_ffn2_in")
    dh1_a, dmixed, dg1, db1 = _mm(da2, W["ffn2_w_in"], add=dh2_a, name="dh2", tm=LN_TM,
                                  epilogue=_ln_bwd_epilogue(h1, mixed, 1.0, lg[1], lb[1]))
    G["w_out"] = _mm(mix, dmixed, ta=True, name="d_mixer_out")
    dgr, dgm, dy_ret, dy_mla = _mm(dmixed, W["w_out"], tb=True, name="dmix", tm=LN_TM,
                                   epilogue=_mix_bwd_epilogue(proj, y_ret, y_mla))
    G["w_mla_o"] = _mm(o, dy_mla, ta=True, name="d_mla_o")
    dob, delta = _mm(dy_mla, W["w_mla_o"], tb=True, name="do", tm=LN_TM, epilogue=_delta_epilogue(o))
    dqn_f, dqpe_f, dkn, dkpe_all, dv = _attn_bwd(qnope, qpe, kv, kpe, dob, lse, delta, T)
    dq_n, dq_r = _q_assemble_bwd(dqn_f, dqpe_f, tabs, T)
    g_uq = jnp.concatenate([_mm(dq_n, qn, ta=True, name="d_uq_nope"), _mm(dq_r, qn, ta=True, name="d_uq_rope")], axis=0)
    g_uq = jnp.concatenate([g_uq[:1024].reshape(HEADS, 128, LORA), g_uq[1024:].reshape(HEADS, 64, LORA)], axis=1)
    G["w_uq"] = g_uq.reshape(1536 * LORA // 1024, 1024)
    dqn = _mm(dq_r, wuq[1024:], add=_mm(dq_n, wuq[:1024], name="dqn_a"), name="dqn_b")
    g_ukv = jnp.stack([_mm(dkn, kvn, ta=True, name="d_ukv_k"), _mm(dv, kvn, ta=True, name="d_ukv_v")], axis=0)
    G["w_ukv"] = g_ukv.reshape(2, HEADS, 128, LORA).transpose(1, 0, 2, 3).reshape(2048 * LORA // 1024, 1024)
    dkvn = _mm(dv, wukv[1024:], add=_mm(dkn, wukv[:1024], name="dkvn_a"), name="dkvn_b")
    dcq, dckv, dkpe, dqg, dkvg = _rms_bwd(proj, dqn, dkvn, dkpe_all, tabs, qg, kvg, T)
    G["w_ret_o"] = _mm(yr, dy_ret, ta=True, name="d_ret_o")
    dyr = _mm(dy_ret, W["w_ret_o"], tb=True, name="dyr")
    drq, drk, drv, drg, dgn = _ret_bwd(rq, rk, proj, y, proj, gn_g, states, dyr, rc, tabs, T)
    dproj = jnp.concatenate([drq, drk, drv, drg, dgr, dgm, dcq, dckv, dkpe, jnp.zeros((T, P_W - P_KPE - 128), BF16)], axis=1)
    G["w_in"] = _w_in_external(_mm(dproj, h1b, ta=True, name="d_mixer_in"))
    lg0 = lg[0] if early is None else lg[0] + early(G)[0:1, 0:1]
    dx_a, df1, dg0, db0 = _mm(dproj, w_in_t, add=dh1_a, name="dh1", tm=LN_TM,
                              epilogue=_ln_bwd_epilogue(x, f1, 0.5, lg0, lb[0]))
    G["ffn1_w_out"] = _mm(hs1, df1, ta=True, name="d_ffn1_out")
    da1 = _ffn_act_bwd(df1, W["ffn1_w_out"], gu1, "ffn1_act_bwd")
    G["ffn1_w_in"] = _mm(da1, xb, ta=True, name="d_ffn1_in")
    grad_x = _mm(da1, W["ffn1_w_in"], add=dx_a, name="grad_x")

    small = dict(ln_g=jnp.concatenate([dg0, dg1, dg2, dg3], axis=0), ln_b=jnp.concatenate([db0, db1, db2, db3], axis=0),
                 ret_gn_g=dgn, q_norm_g=dqg, kv_norm_g=dkvg)
    return loss, grad_x, G, small


def kernel(x, p, positions, ln_g, ln_b, ffn1_w_in, ffn1_w_out, w_in, ret_gn_g, w_ret_o, q_norm_g, kv_norm_g, w_uq, w_ukv, w_mla_o, w_out, ffn2_w_in, ffn2_w_out, ple_w_gate, ple_w_proj, loss_target, m_ln_g, m_ln_b, m_ffn1_w_in, m_ffn1_w_out, m_w_in, m_ret_gn_g, m_w_ret_o, m_q_norm_g, m_kv_norm_g, m_w_uq, m_w_ukv, m_w_mla_o, m_w_out, m_ffn2_w_in, m_ffn2_w_out, m_ple_w_gate, m_ple_w_proj, v_ln_g, v_ln_b, v_ffn1_w_in, v_ffn1_w_out, v_w_in, v_ret_gn_g, v_w_ret_o, v_q_norm_g, v_kv_norm_g, v_w_uq, v_w_ukv, v_w_mla_o, v_w_out, v_ffn2_w_in, v_ffn2_w_out, v_ple_w_gate, v_ple_w_proj):
    names = ("ln_g", "ln_b", "ffn1_w_in", "ffn1_w_out", "w_in", "ret_gn_g", "w_ret_o", "q_norm_g", "kv_norm_g", "w_uq", "w_ukv",
             "w_mla_o", "w_out", "ffn2_w_in", "ffn2_w_out", "ple_w_gate", "ple_w_proj")
    ws = dict(zip(names, (ln_g, ln_b, ffn1_w_in, ffn1_w_out, w_in, ret_gn_g, w_ret_o, q_norm_g, kv_norm_g, w_uq, w_ukv, w_mla_o,
                          w_out, ffn2_w_in, ffn2_w_out, ple_w_gate, ple_w_proj)))
    ms = dict(zip(names, (m_ln_g, m_ln_b, m_ffn1_w_in, m_ffn1_w_out, m_w_in, m_ret_gn_g, m_w_ret_o, m_q_norm_g, m_kv_norm_g, m_w_uq,
                          m_w_ukv, m_w_mla_o, m_w_out, m_ffn2_w_in, m_ffn2_w_out, m_ple_w_gate, m_ple_w_proj)))
    vs = dict(zip(names, (v_ln_g, v_ln_b, v_ffn1_w_in, v_ffn1_w_out, v_w_in, v_ret_gn_g, v_w_ret_o, v_q_norm_g, v_kv_norm_g, v_w_uq,
                          v_w_ukv, v_w_mla_o, v_w_out, v_ffn2_w_in, v_ffn2_w_out, v_ple_w_gate, v_ple_w_proj)))

    parts = []
    for name, r in PACK:
        rows = _to_rows(name, ws[name])
        if _pad16(r) != r:
            rows = jnp.concatenate([rows, jnp.zeros((_pad16(r) - r, 1024), F32)], axis=0)
        parts.append(rows)
    wsh_first = jnp.concatenate(parts[:len(PACK_LATE)], axis=0).astype(BF16)
    wsh_rest = jnp.concatenate(parts[len(PACK_LATE):], axis=0).astype(BF16)
    ssh = jnp.concatenate([ln_g[0], ln_b[0]], axis=0)
    wall_first, sall = _all_gather(wsh_first, ssh)
    *gather_handles, start_token = _gather_start(wsh_rest, wall_first)
    W = _unpack(wall_first, PACK_LATE)
    ln_full = sall.reshape(N_DEV, 2, 4, 128).transpose(1, 2, 0, 3).reshape(2, 4, 1024)

    def rest_weights(after):
        w_thru, land = _gather_wait(*gather_handles, after)
        return _unpack(_gather_finish(w_thru, land), PACK_EARLY)

    cvec = lax.axis_index("c").astype(jnp.int32).reshape(1)
    qvec = (2 * lax.axis_index("x") + lax.axis_index("y")).astype(jnp.int32).reshape(1)

    def chip_partials(G, group, tag):
        gparts = []
        for name, r in group:
            g = G[name].reshape(N_DEV, r, 1024)
            if _pad16(r) != r:
                g = jnp.concatenate([g, jnp.zeros((N_DEV, _pad16(r) - r, 1024), g.dtype)], axis=1)
            gparts.append(g)
        gfull = jnp.concatenate(gparts, axis=1).astype(BF16)
        g4 = gfull.reshape(4, 2, gfull.shape[1], 1024)
        return _sum_sibling(g4, _exchange_sibling(g4, "exchange_sibling_" + tag), cvec, "sum_sibling_" + tag)

    in_flight = []

    def early(G):
        *handles, token = _chips_start(chip_partials(G, PACK_EARLY, "early"), "chips_start_early")
        in_flight.append(handles)
        return token

    loss_p, grad_x, G, small = _local_step(x[0], p[0, 0], positions, loss_target[0], W, ln_full[0], ln_full[1],
                                           ret_gn_g, q_norm_g, kv_norm_g, early=early, rest_weights=rest_weights,
                                           start_token=start_token)

    part_e, land_e = _chips_wait(*in_flight[0], grad_x, "chips_wait_early")
    gsh_early = _sum_chips(part_e, land_e, qvec, "sum_grads_early")
    pad256 = lambda a: jnp.concatenate([a, jnp.zeros((1, 1024 - a.shape[1]), F32)], axis=1)
    gsmall = jnp.concatenate([small["ln_g"], small["ln_b"], small["ret_gn_g"].reshape(2, 1024), pad256(small["q_norm_g"]),
                              pad256(small["kv_norm_g"]), jnp.zeros((SMALL_ROWS - 12, 1024), F32)], axis=0)
    srecv = _exchange_small(gsmall)
    part_l = chip_partials(G, PACK_LATE, "late")
    *late_handles, late_token = _chips_start(part_l, "chips_start_late", order_after=srecv)
    ssum = _sum_slots(srecv, "sum_small_grads")

    def unpack_grads(group, gsh):
        out, off = {}, 0
        for name, r in group:
            out[name] = _from_rows(name, gsh[off:off + r], ws[name].shape)
            off += _pad16(r)
        return out

    grads = unpack_grads(PACK_EARLY, gsh_early)
    me = 4 * lax.axis_index("x") + 2 * lax.axis_index("y") + lax.axis_index("c")
    grads["ln_g"] = lax.dynamic_slice(ssum[0:4], (0, me * 128), (4, 128)).reshape(1, 4, 128)
    grads["ln_b"] = lax.dynamic_slice(ssum[4:8], (0, me * 128), (4, 128)).reshape(1, 4, 128)
    grads["ret_gn_g"] = ssum[8:10].reshape(1, 2048)
    grads["q_norm_g"] = ssum[10:11, :256]
    grads["kv_norm_g"] = ssum[11:12, :256]

    delta, new_m, new_v = {}, {}, {}
    late_names = [n for n, _ in PACK_LATE]
    for name in names:
        if name not in late_names:
            delta[name], new_m[name], new_v[name] = _adamw(ws[name], grads[name], ms[name], vs[name], "adamw_" + name,
                                                           order_after=late_token)
    part_l, land_l = _chips_wait(*late_handles, new_v["w_in"], "chips_wait_late")
    grads.update(unpack_grads(PACK_LATE, _sum_chips(part_l, land_l, qvec, "sum_grads_late")))
    for name in late_names:
        delta[name], new_m[name], new_v[name] = _adamw(ws[name], grads[name], ms[name], vs[name], "adamw_" + name)

    loss = lax.psum(loss_p[0, 0], ("x", "y", "c"))
    return (loss, grad_x[None], *[grads[n] for n in names], *[delta[n] for n in names],
            *[new_m[n] for n in names], *[new_v[n] for n in names])
```

```python
import functools
import math

import jax
import jax.numpy as jnp
from jax import lax
from jax.experimental import pallas as pl
from jax.experimental.pallas import tpu as pltpu

F32 = jnp.float32
BF16 = jnp.bfloat16

N_DEV = 8
D = 1024
D_FF = 2816
D_PLE = 256
CHUNK = 64
HEADS = 8
RET_DK = 128
RET_DV = 256
MLA_NOPE = 128
MLA_ROPE = 64
MLA_DV = 128
LORA = 256
ROPE_BASE = 10000.0
EPS = 1e-5
ALPHA = 2.0 ** 0.25
RET_SCALE = RET_DK ** -0.5
MLA_SCALE = (MLA_NOPE + MLA_ROPE) ** -0.5
NEG = -1e30

ADAM_LR = 0.001
ADAM_B1 = 0.9
ADAM_B2 = 0.999
ADAM_EPS = 1e-08
ADAM_WD = 0.01
ADAM_STEP = 10

P_RQ, P_RK, P_RV, P_RG, P_GR, P_GM, P_CQ, P_CKV, P_KPE, P_W = 0, 1024, 2048, 4096, 6144, 7168, 8192, 8448, 8704, 8960
W_IN_COLS = 8768
RET_L = 256
ATT_TF = 1024
ATT_TB = 1024
ATT_HP = 4
LOG2E = math.log2(math.e)
Q_PRESCALE = MLA_SCALE * LOG2E

PACK = (("ffn1_w_in", 704), ("ffn1_w_out", 352), ("w_in", 1096), ("w_ret_o", 256), ("w_uq", 48), ("w_ukv", 64),
        ("w_mla_o", 128), ("w_out", 128), ("ffn2_w_in", 704), ("ffn2_w_out", 352), ("ple_w_gate", 128), ("ple_w_proj", 32))


def _pad16(r):
    return -(-r // 16) * 16


PACK_ROWS = sum(_pad16(r) for _, r in PACK)
PACK_LATE = PACK[:2]
PACK_EARLY = PACK[2:]
SMALL_ROWS = 16


def _pcall(body, **kw):
    return pl.pallas_call(body, **kw)


def _pick(dim, prefs):
    for p in prefs:
        if dim % p == 0:
            return p
    return dim


def _sigmoid(x):
    return 1.0 / (1.0 + jnp.exp(-x))


def _silu(x):
    return x * _sigmoid(x)


def _ln(r, g, b):
    mu = jnp.mean(r, axis=-1, keepdims=True)
    var = jnp.mean(jnp.square(r - mu), axis=-1, keepdims=True)
    return (r - mu) * lax.rsqrt(var + EPS) * g + b


def _rms(x, g):
    return x * lax.rsqrt(jnp.mean(jnp.square(x), axis=-1, keepdims=True) + EPS) * g


def _dot(a, b, ca, cb):
    return lax.dot_general(a, b, (((ca,), (cb,)), ((), ())), preferred_element_type=F32)


def _accum(ref, val, first=None):
    @pl.when(pl.program_id(0) == 0 if first is None else first)
    def _():
        ref[...] = jnp.zeros_like(ref)

    ref[...] += val


def _mm(a, b, *, ta=False, tb=False, add=None, out_dtype=None, name, tm=None, tn=None, tk=None, epilogue=None):
    parts = a.shape[0] if a.ndim == 3 else 1
    ar, ac = a.shape[-2], a.shape[-1]
    out_dtype = out_dtype or (BF16 if ta else F32)
    if ta:
        K, M = ar, ac * parts
    else:
        M, K = ar, ac * parts
    if tb:
        N, K2 = b.shape
    else:
        K2, N = b.shape
    assert K == K2, (a.shape, b.shape, ta, tb)
    big = (1024, 1408, 1280, 768, 512, 256, 128)
    tm = tm or _pick(ac if (ta and parts > 1) else M, big)
    tn = tn or (N if N <= 1024 else _pick(N, big))
    kdim = ac if (not ta and parts > 1) else K
    tk = tk or (kdim if kdim <= 2816 and parts == 1 else
                _pick(kdim, (2048, 1408, 1280, 1024, 512) if tm <= 1024 else (1024, 1408, 1280, 512)))
    nk = K // tk
    grid = (M // tm, N // tn, nk)
    if parts > 1 and ta:
        per = ac // tm
        a_spec = pl.BlockSpec((None, tk, tm), lambda i, j, k: (i // per, k, i % per))
    elif parts > 1:
        per = ac // tk
        a_spec = pl.BlockSpec((None, tm, tk), lambda i, j, k: (k // per, i, k % per))
    else:
        a_spec = pl.BlockSpec((tk, tm), lambda i, j, k: (k, i)) if ta else pl.BlockSpec((tm, tk), lambda i, j, k: (i, k))
    b_spec = pl.BlockSpec((tn, tk), lambda i, j, k: (j, k)) if tb else pl.BlockSpec((tk, tn), lambda i, j, k: (k, j))
    o_spec = pl.BlockSpec((tm, tn), lambda i, j, k: (i, j))
    ca, cb = (0 if ta else 1), (1 if tb else 0)
    has_add = add is not None
    n_in = 2 + int(has_add)
    if epilogue is not None:
        assert tn == N and not ta
        ep_fn, ep_rows, ep_whole, ep_outs, ep_accs = epilogue
        n_ep_in = len(ep_rows) + len(ep_whole)
        n_out = len(ep_outs) + len(ep_accs)
    else:
        n_ep_in, n_out = 0, 1

    def body(*refs):
        a_ref, b_ref = refs[0], refs[1]
        add_ref = refs[2] if has_add else None
        o_ref = refs[n_in + n_ep_in]
        first_row_tile = pl.program_id(0) == 0

        def finish(r):
            if has_add:
                r = r + add_ref[...].astype(F32)
            if epilogue is not None:
                ep_fn(r, refs[n_in:n_in + n_ep_in], refs[n_in + n_ep_in:n_in + n_ep_in + n_out], first_row_tile)
            else:
                o_ref[...] = r.astype(out_dtype)

        if nk == 1:
            finish(_dot(a_ref[...], b_ref[...], ca, cb))
            return
        acc_ref = refs[-1]
        k = pl.program_id(2)

        @pl.when(k == 0)
        def _():
            acc_ref[...] = jnp.zeros_like(acc_ref)

        acc_ref[...] += _dot(a_ref[...], b_ref[...], ca, cb)

        @pl.when(k == nk - 1)
        def _():
            finish(acc_ref[...])

    in_specs = [a_spec, b_spec] + ([o_spec] if has_add else [])
    args = (a, b) + ((add,) if has_add else ())
    out_specs, out_shape = o_spec, jax.ShapeDtypeStruct((M, N), out_dtype)
    if epilogue is not None:
        for row_in in ep_rows:
            arr, col_block, width = (tuple(row_in) + (N,))[:3] if isinstance(row_in, tuple) else (row_in, 0, N)
            in_specs.append(pl.BlockSpec((tm, width), lambda i, j, k, _c=col_block: (i, _c)))
            args += (arr,)
        in_specs += [pl.BlockSpec(w.shape, lambda i, j, k, _n=w.ndim: (0,) * _n) for w in ep_whole]
        args += tuple(ep_whole)
        outs = [o if isinstance(o, tuple) else (o, N) for o in ep_outs]
        out_specs = ([pl.BlockSpec((tm, w), lambda i, j, k: (i, 0)) for _, w in outs]
                     + [pl.BlockSpec((r, N), lambda i, j, k: (0, 0)) for r in ep_accs])
        out_shape = [jax.ShapeDtypeStruct((M, w), dt) for dt, w in outs] + [jax.ShapeDtypeStruct((r, N), F32) for r in ep_accs]
    return _pcall(
        body, grid=grid, in_specs=in_specs, out_specs=out_specs, out_shape=out_shape,
        scratch_shapes=[pltpu.VMEM((tm, tn), F32)] if nk > 1 else [], name=name,
        compiler_params=pltpu.CompilerParams(dimension_semantics=("arbitrary" if epilogue is not None else "parallel", "parallel", "arbitrary")),
    )(*args)


def _ln_epilogue(res, c, g, b):
    def fn(r, ins, outs, first):
        res_ref, g_ref, b_ref = ins
        f_ref, h_ref, hb_ref = outs
        h = _ln(ALPHA * res_ref[...] + c * r, g_ref[...], b_ref[...])
        f_ref[...] = r
        h_ref[...] = h
        hb_ref[...] = h.astype(BF16)

    return (fn, [res], [g, b], (F32, F32, BF16), ())


def _ln_bwd_epilogue(res, f, c, g, b):
    def fn(r, ins, outs, first):
        res_ref, f_ref, g_ref, b_ref = ins
        dr_ref, df_ref, dg_ref, db_ref = outs
        pre = ALPHA * res_ref[...] + c * f_ref[...]
        xc = pre - jnp.mean(pre, axis=-1, keepdims=True)
        rstd = lax.rsqrt(jnp.mean(jnp.square(xc), axis=-1, keepdims=True) + EPS)
        xhat = xc * rstd
        dyg = r * g_ref[...]
        dpre = rstd * (dyg - jnp.mean(dyg, axis=-1, keepdims=True) - xhat * jnp.mean(dyg * xhat, axis=-1, keepdims=True))
        dr_ref[...] = ALPHA * dpre
        df_ref[...] = (c * dpre).astype(BF16)
        _accum(dg_ref, jnp.sum(r * xhat, axis=0, keepdims=True), first)
        _accum(db_ref, jnp.sum(r, axis=0, keepdims=True), first)

    return (fn, [res, f], [g, b], (F32, BF16), (1, 1))


def _rows(body, T, tm, ins, outs, name, accs=()):
    in_specs, args = [], []
    for arr, w, cb in ins:
        if w is None:
            in_specs.append(pl.BlockSpec(arr.shape, lambda i, _n=arr.ndim: (0,) * _n))
        else:
            in_specs.append(pl.BlockSpec((tm, w), lambda i, _cb=cb: (i, _cb)))
        args.append(arr)
    out_specs = [pl.BlockSpec((tm, w), lambda i: (i, 0)) for w, _ in outs]
    out_shape = [jax.ShapeDtypeStruct((T, w), dt) for w, dt in outs]
    for r, w in accs:
        out_specs.append(pl.BlockSpec((r, w), lambda i: (0, 0)))
        out_shape.append(jax.ShapeDtypeStruct((r, w), F32))
    return _pcall(
        body, grid=(T // tm,), in_specs=in_specs, out_specs=out_specs, out_shape=out_shape, name=name,
        compiler_params=pltpu.CompilerParams(dimension_semantics=("arbitrary",)),
    )(*args)


FFN_TN = 1408
FFN_TM = 512
LN_TM = 512


def _ffn_in(x, wt, name, order_after=None):
    T = x.shape[0]
    tm, tn = min(FFN_TM, T), FFN_TN
    nj = D_FF // tn
    emit_xb = x.dtype != BF16
    n_in = 3 + int(order_after is not None)

    def body(*refs):
        x_ref, wg_ref, wu_ref = refs[:3]
        hs_ref, gu_ref = refs[n_in], refs[n_in + 1]
        xv = x_ref[...].astype(BF16)
        g = _dot(xv, wg_ref[...], 1, 1)
        u = _dot(xv, wu_ref[...], 1, 1)
        hs_ref[...] = (_silu(g) * u).astype(BF16)
        gu_ref[0] = g.astype(BF16)
        gu_ref[1] = u.astype(BF16)
        if emit_xb:
            refs[n_in + 2][...] = xv

    in_specs = [pl.BlockSpec((tm, D), lambda i, j: (i, 0)), pl.BlockSpec((tn, D), lambda i, j: (j, 0)),
                pl.BlockSpec((tn, D), lambda i, j: (j + nj, 0))]
    args = [x, wt, wt]
    if order_after is not None:
        in_specs.append(pl.BlockSpec(order_after.shape, lambda i, j: (0, 0)))
        args.append(order_after)
    out_specs = [pl.BlockSpec((tm, tn), lambda i, j: (i, j)), pl.BlockSpec((2, tm, tn), lambda i, j: (0, i, j))]
    out_shape = [jax.ShapeDtypeStruct((T, D_FF), BF16), jax.ShapeDtypeStruct((2, T, D_FF), BF16)]
    if emit_xb:
        out_specs.append(pl.BlockSpec((tm, D), lambda i, j: (i, 0)))
        out_shape.append(jax.ShapeDtypeStruct((T, D), BF16))
    return _pcall(
        body, grid=(T // tm, nj), in_specs=in_specs, out_specs=out_specs, out_shape=out_shape, name=name,
        compiler_params=pltpu.CompilerParams(dimension_semantics=("parallel", "arbitrary")),
    )(*args)


def _ffn_act_bwd(df, wo, gu, name):
    T = df.shape[0]
    tm, tn = min(FFN_TM, T), FFN_TN

    def body(d_ref, w_ref, gu_ref, o_ref):
        dhs = _dot(d_ref[...], w_ref[...], 1, 1)
        g, u = gu_ref[0].astype(F32), gu_ref[1].astype(F32)
        sig = _sigmoid(g)
        act = g * sig
        o_ref[0] = (dhs * u * (sig + act * (1.0 - sig))).astype(BF16)
        o_ref[1] = (dhs * act).astype(BF16)

    return _pcall(
        body, grid=(T // tm, D_FF // tn),
        in_specs=[pl.BlockSpec((tm, D), lambda i, j: (i, 0)), pl.BlockSpec((tn, D), lambda i, j: (j, 0)),
                  pl.BlockSpec((2, tm, tn), lambda i, j: (0, i, j))],
        out_specs=pl.BlockSpec((2, tm, tn), lambda i, j: (0, i, j)),
        out_shape=jax.ShapeDtypeStruct((2, T, D_FF), BF16), name=name,
        compiler_params=pltpu.CompilerParams(dimension_semantics=("parallel", "parallel")),
    )(df, wo, gu)


def _rope128(t, cos, sin_s):
    return t * cos + pltpu.roll(t, 64, 1) * sin_s


def _rope128_t(g, cos, sin_s):
    return g * cos - pltpu.roll(g, 64, 1) * sin_s


def _partner32(t):
    lane = lax.broadcasted_iota(jnp.int32, t.shape, 1)
    return jnp.where((lane & 32) == 0, pltpu.roll(t, 96, 1), pltpu.roll(t, 32, 1))


def _rope64(t, cos, sin_s):
    return t * cos + _partner32(t) * sin_s


def _rope64_t(g, cos, sin_s):
    return g * cos - _partner32(g) * sin_s


def _mixer_prep_fwd(proj, tabs, qg, kvg, T):
    cos128, sin128, cos64, sin64 = tabs

    def body(rq_ref, rk_ref, cq_ref, ckv_ref, kpe_ref, c1_ref, s1_ref, c2_ref, s2_ref, qg_ref, kvg_ref,
             oq_ref, ok_ref, oqn_ref, okvn_ref, okpe_ref):
        c1, s1 = c1_ref[...], s1_ref[...]
        for h in range(HEADS):
            sl = slice(h * RET_DK, (h + 1) * RET_DK)
            oq_ref[:, sl] = _rope128(rq_ref[:, sl].astype(F32), c1, s1).astype(BF16)
            ok_ref[:, sl] = (_rope128(rk_ref[:, sl].astype(F32), c1, s1) * RET_SCALE).astype(BF16)
        oqn_ref[...] = _rms(cq_ref[...].astype(F32), qg_ref[...]).astype(BF16)
        okvn_ref[...] = _rms(ckv_ref[...].astype(F32), kvg_ref[...]).astype(BF16)
        okpe_ref[...] = _rope64(kpe_ref[...].astype(F32), c2_ref[...], s2_ref[...]).astype(BF16)

    ins = [(proj, 1024, 0), (proj, 1024, 1), (proj, 256, P_CQ // 256), (proj, 256, P_CKV // 256),
           (proj, 128, P_KPE // 128), (cos128, 128, 0), (sin128, 128, 0), (cos64, 128, 0), (sin64, 128, 0),
           (qg, None, None), (kvg, None, None)]
    outs = [(1024, BF16), (1024, BF16), (LORA, BF16), (LORA, BF16), (128, BF16)]
    return _rows(body, T, 256, ins, outs, "mixer_prep_fwd")


def _rms_bwd(proj, dqn, dkvn, dkpe_all, tabs, qg, kvg, T):
    _, _, cos64, sin64 = tabs

    def body(cq_ref, ckv_ref, dq_ref, dkv_ref, dk_ref, c2_ref, s2_ref, qg_ref, kvg_ref, ocq_ref, ockv_ref, okpe_ref,
             dqg_ref, dkvg_ref):
        _, vjp = jax.vjp(_rms, cq_ref[...].astype(F32), qg_ref[...])
        dx, dg = vjp(dq_ref[...])
        ocq_ref[...] = dx.astype(BF16)
        _accum(dqg_ref, dg)
        _, vjp = jax.vjp(_rms, ckv_ref[...].astype(F32), kvg_ref[...])
        dx, dg = vjp(dkv_ref[...])
        ockv_ref[...] = dx.astype(BF16)
        _accum(dkvg_ref, dg)
        g = dk_ref[:, 0:128]
        for h in range(1, HEADS):
            g = g + dk_ref[:, h * 128:(h + 1) * 128]
        lane = lax.broadcasted_iota(jnp.int32, g.shape, 1)
        g = jnp.where(lane < MLA_ROPE, g, 0.0)
        okpe_ref[...] = _rope64_t(g, c2_ref[...], s2_ref[...]).astype(BF16)

    ins = [(proj, 256, P_CQ // 256), (proj, 256, P_CKV // 256), (dqn, LORA, 0), (dkvn, LORA, 0), (dkpe_all, 1024, 0),
           (cos64, 128, 0), (sin64, 128, 0), (qg, None, None), (kvg, None, None)]
    return _rows(body, T, 256, ins, [(LORA, BF16), (LORA, BF16), (128, BF16)], "rms_bwd", accs=[(1, LORA), (1, LORA)])


def _gn_gate_bwd(y, rg, g, d):
    xc = y - jnp.mean(y, axis=-1, keepdims=True)
    rstd = lax.rsqrt(jnp.mean(jnp.square(xc), axis=-1, keepdims=True) + EPS)
    xhat = xc * rstd
    sig = _sigmoid(rg)
    act = rg * sig
    dn = d * act
    drg = d * (xhat * g) * (sig + act * (1.0 - sig))
    dxh = dn * g
    dy = rstd * (dxh - jnp.mean(dxh, axis=-1, keepdims=True) - xhat * jnp.mean(dxh * xhat, axis=-1, keepdims=True))
    return dy, drg, jnp.sum(dn * xhat, axis=0, keepdims=True)


def _gn_gate(y, rg, g):
    mu = jnp.mean(y, axis=-1, keepdims=True)
    var = jnp.mean(jnp.square(y - mu), axis=-1, keepdims=True)
    return _silu(rg) * ((y - mu) * lax.rsqrt(var + EPS) * g)


def _q_assemble_epilogue(tabs):
    _, _, cos64, sin64 = tabs

    def fn(q, ins, outs, first):
        c_ref, s_ref = ins
        on_ref, op_ref = outs
        on_ref[...] = (q[:, :1024] * Q_PRESCALE).astype(BF16)
        c, s = c_ref[...], s_ref[...]
        lane = lax.broadcasted_iota(jnp.int32, c.shape, 1)
        for j in range(HEADS // 2):
            r = _rope64(q[:, 1024 + 128 * j:1024 + 128 * (j + 1)], c, s) * Q_PRESCALE
            op_ref[:, 256 * j:256 * j + 128] = jnp.where(lane < 64, r, 0.0).astype(BF16)
            op_ref[:, 256 * j + 128:256 * j + 256] = jnp.where(lane < 64, pltpu.roll(r, 64, 1), 0.0).astype(BF16)

    return (fn, [(cos64, 0, 128), (sin64, 0, 128)], [], ((BF16, 1024), (BF16, 1024)), ())


def _q_assemble_bwd(dqn, dqpe, tabs, T):
    _, _, cos64, sin64 = tabs

    def body(dn_ref, dp_ref, c_ref, s_ref, on_ref, op_ref):
        on_ref[...] = dn_ref[...].astype(BF16)
        c, s = c_ref[...], s_ref[...]
        lane = lax.broadcasted_iota(jnp.int32, c.shape, 1)
        for j in range(HEADS // 2):
            g = jnp.where(lane < 64, dp_ref[:, 256 * j:256 * j + 128], pltpu.roll(dp_ref[:, 256 * j + 128:256 * j + 256], 64, 1))
            op_ref[:, 128 * j:128 * (j + 1)] = _rope64_t(g, c, s).astype(BF16)

    return _rows(body, T, 256, [(dqn, 1024, 0), (dqpe, 1024, 0), (cos64, 128, 0), (sin64, 128, 0)],
                 [(1024, BF16), (512, BF16)], "q_assemble_bwd")


def _mix_fn(gr, gm, yr, ym):
    return _sigmoid(gr) * yr + _sigmoid(gm) * ym


def _mix_epilogue(proj, y_ret):
    def fn(r, ins, outs, first):
        gr_ref, gm_ref, yr_ref = ins
        ym_ref, mix_ref = outs
        ym_ref[...] = r
        mix_ref[...] = _mix_fn(gr_ref[...].astype(F32), gm_ref[...].astype(F32), yr_ref[...], r).astype(BF16)

    return (fn, [(proj, P_GR // D), (proj, P_GM // D), y_ret], [], (F32, BF16), ())


def _mix_bwd_epilogue(proj, y_ret, y_mla):
    def fn(r, ins, outs, first):
        gr_ref, gm_ref, yr_ref, ym_ref = ins
        dgr_ref, dgm_ref, dyr_ref, dym_ref = outs
        sr, sm = _sigmoid(gr_ref[...].astype(F32)), _sigmoid(gm_ref[...].astype(F32))
        dgr_ref[...] = (r * yr_ref[...] * (sr * (1.0 - sr))).astype(BF16)
        dgm_ref[...] = (r * ym_ref[...] * (sm * (1.0 - sm))).astype(BF16)
        dyr_ref[...] = (r * sr).astype(BF16)
        dym_ref[...] = (r * sm).astype(BF16)

    return (fn, [(proj, P_GR // D), (proj, P_GM // D), y_ret, y_mla], [], (BF16,) * 4, ())


def _head_epilogue(h3, pp, tgt, g, b):
    def fn(r, ins, outs, first):
        h_ref, pp_ref, t_ref, g_ref, b_ref = ins
        dh_ref, dgl_ref, dpp_ref, dg_ref, db_ref, loss_ref = outs

        sg, pp, gain = _sigmoid(r), pp_ref[...], g_ref[...]
        pre = ALPHA * h_ref[...] + sg * pp
        xc = pre - jnp.mean(pre, axis=-1, keepdims=True)
        rstd = lax.rsqrt(jnp.mean(jnp.square(xc), axis=-1, keepdims=True) + EPS)
        xhat = xc * rstd
        err = xhat * gain + b_ref[...] - t_ref[...]
        dy = err * (1.0 / D)
        dyg = dy * gain
        dpre = rstd * (dyg - jnp.mean(dyg, axis=-1, keepdims=True) - xhat * jnp.mean(dyg * xhat, axis=-1, keepdims=True))
        dh_ref[...] = ALPHA * dpre
        dgl_ref[...] = (dpre * pp * (sg * (1.0 - sg))).astype(BF16)
        dpp_ref[...] = (dpre * sg).astype(BF16)
        _accum(dg_ref, jnp.sum(dy * xhat, axis=0, keepdims=True), first)
        _accum(db_ref, jnp.sum(dy, axis=0, keepdims=True), first)
        part = 0.5 * jnp.sum(jnp.mean(jnp.square(err), axis=-1, keepdims=True), axis=0, keepdims=True)
        _accum(loss_ref, jnp.broadcast_to(part, loss_ref.shape), first)

    return (fn, [h3, pp, tgt], [g, b], (F32, BF16, BF16), (1, 1, 8))


def _delta_epilogue(o):
    def fn(r, ins, outs, first):
        (o_ref,) = ins
        db_ref, dl_ref = outs
        db_ref[...] = r.astype(BF16)
        for h in range(HEADS):
            sl = slice(h * MLA_DV, (h + 1) * MLA_DV)
            dl = jnp.sum(r[:, sl] * o_ref[:, sl].astype(F32), axis=-1, keepdims=True)
            dl_ref[:, sl] = jnp.broadcast_to(dl, (r.shape[0], MLA_DV))

    return (fn, [o], [], (BF16, F32), ())


def _ret_consts():
    L = RET_L
    lg = jnp.log(1.0 - 2.0 ** (-5.0 - jnp.arange(HEADS, dtype=F32)))[:, None, None]
    idx = jnp.arange(L, dtype=F32)
    ch = jnp.arange(L) // CHUNK
    dist = idx[:, None] - idx[None, :]
    same = (ch[:, None] == ch[None, :])[None]
    earlier = (ch[None, :] < ch[:, None])[None]
    dm = jnp.where(same, jnp.exp(lg * jnp.abs(dist)[None]), jnp.where(earlier, jnp.exp(lg * dist[None]), 0.0))
    xi = jnp.broadcast_to(jnp.exp(lg * (idx + 1.0)[None, :, None]), (HEADS, L, 128))
    zeta = jnp.broadcast_to(jnp.exp(lg * (L - 1.0 - idx)[None, :, None]), (HEADS, L, 128))
    gl = jnp.broadcast_to(jnp.exp(lg * float(L)), (HEADS, 8, 128))
    return dm.astype(F32), xi.astype(F32), zeta.astype(F32), gl.astype(F32)


def _whole(arr):
    return pl.BlockSpec(arr.shape, lambda n, _nd=arr.ndim: (0,) * _nd)


def _ret_fwd(q, k, v, proj, gn_g, consts, T):
    dm, xi, zeta, gl = consts
    L = RET_L
    n_sc = T // L

    def body(q_ref, k_ref, v_ref, rg_ref, g_ref, dm_ref, xi_ref, ze_ref, gl_ref, y_ref, yr_ref, s_ref, st_ref):
        @pl.when(pl.program_id(0) == 0)
        def _():
            st_ref[...] = jnp.zeros_like(st_ref)

        for h in range(HEADS):
            ks, vs = slice(h * RET_DK, (h + 1) * RET_DK), slice(h * RET_DV, (h + 1) * RET_DV)
            qq, kk, vv = q_ref[:, ks], k_ref[:, ks], v_ref[:, vs]
            st = st_ref[h]
            s_ref[h, 0] = st
            p = (_dot(qq, kk, 1, 1) * dm_ref[h]).astype(BF16)
            cross = _dot(qq, st.astype(BF16), 1, 0)
            xi_c = jnp.concatenate([xi_ref[h], xi_ref[h]], axis=1)
            y = _dot(p, vv, 1, 0) + cross * xi_c
            y_ref[:, vs] = y
            yr_ref[:, vs] = _gn_gate(y, rg_ref[:, vs].astype(F32), g_ref[:, vs]).astype(BF16)
            kz = (kk.astype(F32) * ze_ref[h]).astype(BF16)
            gl2 = jnp.concatenate([gl_ref[h, 0:1, :], gl_ref[h, 0:1, :]], axis=1)
            st_ref[h] = st * gl2 + _dot(kz, vv, 0, 0)

    return _pcall(
        body, grid=(n_sc,),
        in_specs=[pl.BlockSpec((L, 1024), lambda n: (n, 0)), pl.BlockSpec((L, 1024), lambda n: (n, 0)),
                  pl.BlockSpec((L, 2048), lambda n: (n, P_RV // 2048)), pl.BlockSpec((L, 2048), lambda n: (n, P_RG // 2048)),
                  _whole(gn_g), _whole(dm), _whole(xi), _whole(zeta), _whole(gl)],
        out_specs=[pl.BlockSpec((L, 2048), lambda n: (n, 0)), pl.BlockSpec((L, 2048), lambda n: (n, 0)),
                   pl.BlockSpec((HEADS, 1, 128, 256), lambda n: (0, n, 0, 0))],
        out_shape=[jax.ShapeDtypeStruct((T, HEADS * RET_DV), F32), jax.ShapeDtypeStruct((T, HEADS * RET_DV), BF16),
                   jax.ShapeDtypeStruct((HEADS, n_sc, 128, 256), F32)],
        scratch_shapes=[pltpu.VMEM((HEADS, 128, 256), F32)], name="ret_fwd",
        compiler_params=pltpu.CompilerParams(dimension_semantics=("arbitrary",)),
    )(q, k, v, proj, gn_g, dm, xi, zeta, gl)


def _ret_bwd(q, k, v, y, proj, gn_g, states, dyr, consts, tabs, T):
    dm, xi, zeta, gl = consts
    cos128, sin128, _, _ = tabs
    L = RET_L
    n_sc = T // L

    def body(q_ref, k_ref, v_ref, y_ref, rg_ref, g_ref, d_ref, s_ref, dm_ref, xi_ref, ze_ref, gl_ref, c_ref, sn_ref,
             dq_ref, dk_ref, dv_ref, drg_ref, dg_ref, gs_ref):
        @pl.when(pl.program_id(0) == 0)
        def _():
            gs_ref[...] = jnp.zeros_like(gs_ref)

        c, sn = c_ref[...], sn_ref[...]
        dgs = []
        for h in range(HEADS):
            ks, vs = slice(h * RET_DK, (h + 1) * RET_DK), slice(h * RET_DV, (h + 1) * RET_DV)
            dy, drg, dg = _gn_gate_bwd(y_ref[:, vs], rg_ref[:, vs].astype(F32), g_ref[:, vs], d_ref[:, vs])
            drg_ref[:, vs] = drg.astype(BF16)
            dgs.append(dg)
            qq, kk, vv, dyy = q_ref[:, ks], k_ref[:, ks], v_ref[:, vs], dy.astype(BF16)
            dmm = dm_ref[h]
            gb = gs_ref[h].astype(BF16)
            sb = s_ref[h, 0].astype(BF16)
            xi_c = jnp.concatenate([xi_ref[h], xi_ref[h]], axis=1)
            pb = (_dot(qq, kk, 1, 1) * dmm).astype(BF16)
            kz = (kk.astype(F32) * ze_ref[h]).astype(BF16)
            dv_ref[:, vs] = (_dot(pb, dyy, 0, 0) + _dot(kz, gb, 1, 0)).astype(BF16)
            da = (_dot(dyy, vv, 1, 1) * dmm).astype(BF16)
            dyx = (dyy.astype(F32) * xi_c).astype(BF16)
            dq = _dot(da, kk, 1, 0) + _dot(dyx, sb, 1, 1)
            dk = _dot(da, qq, 0, 0) + _dot(vv, gb, 1, 1) * ze_ref[h]
            dq_ref[:, ks] = _rope128_t(dq, c, sn).astype(BF16)
            dk_ref[:, ks] = (_rope128_t(dk, c, sn) * RET_SCALE).astype(BF16)
            gl2 = jnp.concatenate([gl_ref[h, 0:1, :], gl_ref[h, 0:1, :]], axis=1)
            gs_ref[h] = gs_ref[h] * gl2 + _dot(qq, dyx, 0, 0)
        _accum(dg_ref, jnp.concatenate(dgs, axis=1))

    rev = lambda n: n_sc - 1 - n
    return _pcall(
        body, grid=(n_sc,),
        in_specs=[pl.BlockSpec((L, 1024), lambda n: (rev(n), 0)), pl.BlockSpec((L, 1024), lambda n: (rev(n), 0)),
                  pl.BlockSpec((L, 2048), lambda n: (rev(n), P_RV // 2048)), pl.BlockSpec((L, 2048), lambda n: (rev(n), 0)),
                  pl.BlockSpec((L, 2048), lambda n: (rev(n), P_RG // 2048)), _whole(gn_g),
                  pl.BlockSpec((L, 2048), lambda n: (rev(n), 0)),
                  pl.BlockSpec((HEADS, 1, 128, 256), lambda n: (0, rev(n), 0, 0)),
                  _whole(dm), _whole(xi), _whole(zeta), _whole(gl),
                  pl.BlockSpec((L, 128), lambda n: (rev(n), 0)), pl.BlockSpec((L, 128), lambda n: (rev(n), 0))],
        out_specs=[pl.BlockSpec((L, 1024), lambda n: (rev(n), 0)), pl.BlockSpec((L, 1024), lambda n: (rev(n), 0)),
                   pl.BlockSpec((L, 2048), lambda n: (rev(n), 0)), pl.BlockSpec((L, 2048), lambda n: (rev(n), 0)),
                   pl.BlockSpec((1, 2048), lambda n: (0, 0))],
        out_shape=[jax.ShapeDtypeStruct((T, 1024), BF16), jax.ShapeDtypeStruct((T, 1024), BF16), jax.ShapeDtypeStruct((T, 2048), BF16),
                   jax.ShapeDtypeStruct((T, 2048), BF16), jax.ShapeDtypeStruct((1, 2048), F32)],
        scratch_shapes=[pltpu.VMEM((HEADS, 128, 256), F32)], name="ret_bwd",
        compiler_params=pltpu.CompilerParams(dimension_semantics=("arbitrary",)),
    )(q, k, v, y, proj, gn_g, dyr, states, dm, xi, zeta, gl, cos128, sin128)


def _diag_mask(nrows, ncols, row0):
    row = row0 + lax.broadcasted_iota(jnp.int32, (nrows, ncols), 0)
    col = lax.broadcasted_iota(jnp.int32, (nrows, ncols), 1)
    return lax.shift_right_logical(col, 6) <= lax.shift_right_logical(row, 6)


def _diag_spans(t):
    h = t // 2
    return ((0, h, h), (h, h, t)) if h % 128 == 0 else ((0, t, t),)


def _tri_steps(nb, by_key):
    if by_key:
        pairs = [(i, j) for j in range(nb) for i in range(j, nb)]
    else:
        pairs = [(i, j) for i in range(nb) for j in range(i + 1)]
    return jnp.array([a for a, _ in pairs], jnp.int32), jnp.array([b for _, b in pairs], jnp.int32)


def _attn_fwd(qn, qpe, kv, kpe, T):
    t = min(ATT_TF, T)
    nb = T // t
    ii, jj = _tri_steps(nb, by_key=False)

    hp = ATT_HP
    w = 128 * hp

    def body(ii_ref, jj_ref, qn_ref, qp_ref, kn_ref, kp_ref, v_ref, o_ref, lse_ref, m_sc, l_sc, acc_sc):
        st = pl.program_id(1)
        i, j = ii_ref[st], jj_ref[st]

        @pl.when(j == 0)
        def _():
            m_sc[...] = jnp.full_like(m_sc, NEG)
            l_sc[...] = jnp.zeros_like(l_sc)
            acc_sc[...] = jnp.zeros_like(acc_sc)

        def update(diag):
            kp = kp_ref[...]
            for hh in range(hp):
                sl = slice(128 * hh, 128 * (hh + 1))
                q = jnp.concatenate([qn_ref[:, sl], qp_ref[:, sl]], axis=1)
                k = jnp.concatenate([kn_ref[:, sl], kp], axis=1)
                s = _dot(q, k, 1, 1)
                if diag:
                    s = jnp.where(_diag_mask(t, t, 0), s, NEG)
                m_prev = m_sc[:, sl]
                m_new = jnp.maximum(m_prev, jnp.max(s, axis=1, keepdims=True))
                a = jnp.exp2(m_prev - m_new)
                p = jnp.exp2(s - m_new[:, 0:1])
                l_sc[:, sl] = a * l_sc[:, sl] + jnp.sum(p, axis=1, keepdims=True)
                acc_sc[:, sl] = a * acc_sc[:, sl] + _dot(p.astype(BF16), v_ref[:, sl], 1, 0)
                m_sc[:, sl] = m_new

        @pl.when(j < i)
        def _():
            update(False)

        @pl.when(j == i)
        def _():
            update(True)
            o_ref[...] = (acc_sc[...] / l_sc[...]).astype(BF16)
            lse_ref[...] = m_sc[...] + jnp.log2(l_sc[...])

    qs = pl.BlockSpec((t, w), lambda h, s, ii, jj: (ii[s], h))
    grid_spec = pltpu.PrefetchScalarGridSpec(
        num_scalar_prefetch=2, grid=(HEADS // hp, int(ii.shape[0])),
        in_specs=[qs, qs, pl.BlockSpec((t, w), lambda h, s, ii, jj: (jj[s], h)), pl.BlockSpec((t, 128), lambda h, s, ii, jj: (jj[s], 0)),
                  pl.BlockSpec((t, w), lambda h, s, ii, jj: (jj[s], HEADS // hp + h))],
        out_specs=[qs, qs],
        scratch_shapes=[pltpu.VMEM((t, w), F32), pltpu.VMEM((t, w), F32), pltpu.VMEM((t, w), F32)])
    return _pcall(
        body, grid_spec=grid_spec, out_shape=[jax.ShapeDtypeStruct((T, D), BF16), jax.ShapeDtypeStruct((T, D), F32)], name="attn_fwd",
        compiler_params=pltpu.CompilerParams(dimension_semantics=("arbitrary", "arbitrary")),
    )(ii, jj, qn, qpe, kv, kpe, kv)


def _attn_bwd(qn, qpe, kv, kpe, do, lse, delta, T):
    t = min(ATT_TB, T)
    nb = T // t
    ii, jj = _tri_steps(nb, by_key=True)

    def body(ii_ref, jj_ref, qn_ref, qp_ref, kn_ref, kp_ref, v_ref, do_ref, lse_ref, dl_ref,
             dqn_ref, dqp_ref, dkn_ref, dkp_ref, dv_ref, dk_sc, dv_sc):
        st = pl.program_id(1)
        i, j = ii_ref[st], jj_ref[st]

        @pl.when(st == 0)
        def _():
            dqn_ref[...] = jnp.zeros_like(dqn_ref)
            dqp_ref[...] = jnp.zeros_like(dqp_ref)

        @pl.when(i == j)
        def _():
            dk_sc[...] = jnp.zeros_like(dk_sc)
            dv_sc[...] = jnp.zeros_like(dv_sc)

        def update(diag):
            for row0, nr, nkeys in (_diag_spans(t) if diag else ((0, t, t),)):
                rows = slice(row0, row0 + nr)
                q = jnp.concatenate([qn_ref[rows, :], qp_ref[rows, :]], axis=1)
                k = jnp.concatenate([kn_ref[:nkeys, :], kp_ref[:nkeys, :]], axis=1)
                dob = do_ref[rows, :]
                s = _dot(q, k, 1, 1)
                if diag:
                    s = jnp.where(_diag_mask(nr, nkeys, row0), s, NEG)
                p = jnp.exp2(s - lse_ref[rows, 0:1])
                dv_sc[:nkeys, :] += _dot(p.astype(BF16), dob, 0, 0)
                dp = _dot(dob, v_ref[:nkeys, :], 1, 1)
                ds = (p * (dp - dl_ref[rows, 0:1])).astype(BF16)
                dk_sc[:nkeys, :] += _dot(ds, q, 0, 0)
                dq = _dot(ds, k, 1, 0) * MLA_SCALE
                grows = pl.ds(pl.multiple_of(i * t + row0, nr), nr)
                dqn_ref[grows, :] += dq[:, :128]
                dqp_ref[grows, :] += dq[:, 128:]

        @pl.when(i > j)
        def _():
            update(False)

        @pl.when(i == j)
        def _():
            update(True)

        @pl.when(i == nb - 1)
        def _():
            dkn_ref[...] = (dk_sc[:, :128] * (1.0 / LOG2E)).astype(BF16)
            dkp_ref[...] = dk_sc[:, 128:] * (1.0 / LOG2E)
            dv_ref[...] = dv_sc[...].astype(BF16)

    qs = pl.BlockSpec((t, 128), lambda h, s, ii, jj: (ii[s], h))
    ks = pl.BlockSpec((t, 128), lambda h, s, ii, jj: (jj[s], h))
    hs = pl.BlockSpec((T, 128), lambda h, s, ii, jj: (0, h))
    grid_spec = pltpu.PrefetchScalarGridSpec(
        num_scalar_prefetch=2, grid=(HEADS, int(ii.shape[0])),
        in_specs=[qs, qs, ks, pl.BlockSpec((t, 128), lambda h, s, ii, jj: (jj[s], 0)),
                  pl.BlockSpec((t, 128), lambda h, s, ii, jj: (jj[s], HEADS + h)), qs, qs, qs],
        out_specs=[hs, hs, ks, ks, ks],
        scratch_shapes=[pltpu.VMEM((t, 256), F32), pltpu.VMEM((t, 128), F32)])
    return _pcall(
        body, grid_spec=grid_spec,
        out_shape=[jax.ShapeDtypeStruct((T, D), F32), jax.ShapeDtypeStruct((T, D), F32), jax.ShapeDtypeStruct((T, D), BF16),
                   jax.ShapeDtypeStruct((T, D), F32), jax.ShapeDtypeStruct((T, D), BF16)], name="attn_bwd",
        compiler_params=pltpu.CompilerParams(dimension_semantics=("arbitrary", "arbitrary")),
    )(ii, jj, qn, qpe, kv, kpe, kv, do, lse, delta)


def _mesh_pos():
    x, y, c = lax.axis_index("x"), lax.axis_index("y"), lax.axis_index("c")
    return x, y, c, 4 * x + 2 * y + c


def _peer(x, y, c, k):
    px, py, pc = (x + ((k >> 2) & 1)) % 2, (y + ((k >> 1) & 1)) % 2, (c + (k & 1)) % 2
    return (px, py, pc), 4 * px + 2 * py + pc


_ANY = pl.BlockSpec(memory_space=pl.ANY)


def _rcopy(src, dst, send_sems, recv_sems, k, to):
    return pltpu.make_async_remote_copy(src_ref=src, dst_ref=dst, send_sem=send_sems.at[k], recv_sem=recv_sems.at[k],
                                        device_id=to, device_id_type=pl.DeviceIdType.MESH)


def _all_gather(wsh, ssh):
    def body(w_ref, s_ref, wall_ref, sall_ref, send_sems, recv_sems, loc_sems):
        x, y, c, me = _mesh_pos()
        sib = (x, y, 1 - c)
        chips = [(1 - x, y), (x, 1 - y), (1 - x, 1 - y)]
        slot = lambda px, py, pc: 4 * px + 2 * py + pc
        loc = [pltpu.make_async_copy(s_ref, sall_ref.at[me], loc_sems.at[0])]
        for cp in loc:
            cp.start()
        sends, fwd_waits = [], []
        for n, (src, dst) in enumerate(((w_ref, wall_ref), (s_ref, sall_ref))):
            o = 7 * n
            sends.append(_rcopy(src, dst.at[me], send_sems, recv_sems, o, sib))
            for j, chip in enumerate(chips):
                sends.append(_rcopy(src, dst.at[me], send_sems, recv_sems, o + 1 + j, (*chip, c)))
        for cp in sends:
            cp.start()
        for n, (src, dst) in enumerate(((w_ref, wall_ref), (s_ref, sall_ref))):
            o = 7 * n
            for j, chip in enumerate(chips):
                got = dst.at[slot(*chip, c)]
                _rcopy(src, got, send_sems, recv_sems, o + 1 + j, sib).wait_recv()
                fw = _rcopy(got, got, send_sems, recv_sems, o + 4 + j, sib)
                fw.start()
                sends.append(fw)
            fwd_waits.append(_rcopy(src, dst.at[slot(x, y, 1 - c)], send_sems, recv_sems, o, sib))
            for j, chip in enumerate(chips):
                fwd_waits.append(_rcopy(src, dst.at[slot(*chip, 1 - c)], send_sems, recv_sems, o + 4 + j, sib))
        for cp in fwd_waits:
            cp.wait_recv()
        for cp in sends:
            cp.wait_send()
        for cp in loc:
            cp.wait()

    wall, sall = _pcall(
        body, in_specs=[_ANY, _ANY], out_specs=[_ANY, _ANY],
        out_shape=[jax.ShapeDtypeStruct((N_DEV,) + wsh.shape, wsh.dtype), jax.ShapeDtypeStruct((N_DEV,) + ssh.shape, ssh.dtype)],
        scratch_shapes=[pltpu.SemaphoreType.DMA((14,)), pltpu.SemaphoreType.DMA((14,)), pltpu.SemaphoreType.DMA((1,))],
        name="all_gather_weights",
    )(wsh, ssh)
    me = 4 * lax.axis_index("x") + 2 * lax.axis_index("y") + lax.axis_index("c")
    return lax.dynamic_update_index_in_dim(wall, wsh, me, axis=0), sall


_HBM = pl.BlockSpec(memory_space=pltpu.HBM)
_SEM = pl.BlockSpec(memory_space=pltpu.SEMAPHORE)
_EFFECT = pltpu.SideEffectType.DATAFLOW_SIDE_EFFECTING


def _other_chips(x, y):
    return [(1 - x, y), (x, 1 - y), (1 - x, 1 - y)]


def _gather_start(wsh, order_after):
    def body(w_ref, land_ref, dep_ref, send_sems, recv_sems, w_thru, land_thru, token):
        x, y, c, me = _mesh_pos()
        _rcopy(w_ref, land_ref.at[me], send_sems, recv_sems, 0, (x, y, 1 - c)).start()
        for j, chip in enumerate(_other_chips(x, y)):
            _rcopy(w_ref, land_ref.at[me], send_sems, recv_sems, 1 + j, (*chip, c)).start()
        token[...] = jnp.zeros_like(token)

    shape = (N_DEV,) + wsh.shape
    land = pltpu.with_memory_space_constraint(lax.empty(shape, wsh.dtype), pltpu.HBM)
    return _pcall(
        body, name="gather_start",
        out_shape=(pltpu.SemaphoreType.DMA((4,)), pltpu.SemaphoreType.DMA((4,)), pltpu.HBM(wsh.shape, wsh.dtype),
                   pltpu.HBM(shape, wsh.dtype), jax.ShapeDtypeStruct((8, 128), F32)),
        in_specs=(_HBM, _HBM, _ANY), out_specs=(_SEM, _SEM, _HBM, _HBM, pl.BlockSpec(memory_space=pltpu.VMEM)),
        input_output_aliases={0: 2, 1: 3}, compiler_params=pltpu.CompilerParams(has_side_effects=_EFFECT),
    )(pltpu.with_memory_space_constraint(wsh, pltpu.HBM), land, order_after)


def _gather_wait(send_sems, recv_sems, w_thru, land_thru, after):
    def body(w_ref, land_ref, send_sems, recv_sems, after_ref, w_out, land_out):
        x, y, c, _ = _mesh_pos()
        senders = [(x, y, 1 - c)] + [(*chip, c) for chip in _other_chips(x, y)]
        for k, (px, py, pc) in enumerate(senders):
            cp = _rcopy(w_ref, land_ref.at[4 * px + 2 * py + pc], send_sems, recv_sems, k, (px, py, pc))
            cp.wait_send()
            cp.wait_recv()

    return _pcall(
        body, name="gather_wait", out_shape=(pltpu.HBM(w_thru.shape, w_thru.dtype), pltpu.HBM(land_thru.shape, land_thru.dtype)),
        in_specs=(_HBM, _HBM, _SEM, _SEM, _ANY), out_specs=(_HBM, _HBM), input_output_aliases={0: 0, 1: 1},
        compiler_params=pltpu.CompilerParams(has_side_effects=_EFFECT),
    )(w_thru, land_thru, send_sems, recv_sems, after)


def _gather_finish(wsh, land):
    def body(land_ref, out_ref, send_sems, recv_sems):
        x, y, c, _ = _mesh_pos()
        sib = (x, y, 1 - c)
        sends, recvs = [], []
        for j, (px, py) in enumerate(_other_chips(x, y)):
            mine, theirs = 4 * px + 2 * py + c, 4 * px + 2 * py + (1 - c)
            sends.append(_rcopy(land_ref.at[mine], out_ref.at[mine], send_sems, recv_sems, j, sib))
            recvs.append(_rcopy(land_ref.at[theirs], out_ref.at[theirs], send_sems, recv_sems, j, sib))
        for cp in sends:
            cp.start()
        for cp in recvs:
            cp.wait_recv()
        for cp in sends:
            cp.wait_send()

    done = _pcall(
        body, in_specs=[_ANY], out_specs=_ANY, out_shape=jax.ShapeDtypeStruct(land.shape, land.dtype),
        input_output_aliases={0: 0},
        scratch_shapes=[pltpu.SemaphoreType.DMA((3,)), pltpu.SemaphoreType.DMA((3,))], name="gather_finish",
    )(land)
    me = 4 * lax.axis_index("x") + 2 * lax.axis_index("y") + lax.axis_index("c")
    return lax.dynamic_update_index_in_dim(done, wsh, me, axis=0)


_ROW_TILES = (400, 368, 352, 256, 128, 16, 8)


def _exchange_sibling(g4, name):
    def body(g_ref, r_ref, send_sems, recv_sems):
        x, y, c, _ = _mesh_pos()
        sib = (x, y, 1 - c)
        cps = [_rcopy(g_ref.at[q, 1 - c], r_ref.at[q], send_sems, recv_sems, q, sib) for q in range(4)]
        for cp in cps:
            cp.start()
        for cp in cps:
            cp.wait()

    n, _, R, C = g4.shape
    return _pcall(
        body, in_specs=[_ANY], out_specs=_ANY, out_shape=jax.ShapeDtypeStruct((n, R, C), g4.dtype),
        scratch_shapes=[pltpu.SemaphoreType.DMA((4,)), pltpu.SemaphoreType.DMA((4,))], name=name,
    )(g4)


def _sum_sibling(g4, rsib, cvec, name):
    n, _, R, C = g4.shape
    tr = _pick(R, _ROW_TILES)

    def body(c_ref, g_ref, r_ref, o_ref):
        o_ref[...] = (g_ref[...].astype(F32) + r_ref[...].astype(F32)).astype(o_ref.dtype)

    grid_spec = pltpu.PrefetchScalarGridSpec(
        num_scalar_prefetch=1, grid=(n, R // tr),
        in_specs=[pl.BlockSpec((None, None, tr, C), lambda q, i, cr: (q, cr[0], i, 0)), pl.BlockSpec((None, tr, C), lambda q, i, cr: (q, i, 0))],
        out_specs=pl.BlockSpec((None, tr, C), lambda q, i, cr: (q, i, 0)))
    return _pcall(body, grid_spec=grid_spec, out_shape=jax.ShapeDtypeStruct((n, R, C), g4.dtype), name=name)(cvec, g4, rsib)


def _chips_start(part, name, order_after=None):
    n_in = 2 + int(order_after is not None)

    def body(*refs):
        p_ref, land_ref = refs[:2]
        send_sems, recv_sems, token = refs[n_in], refs[n_in + 1], refs[n_in + 4]
        x, y, c, _ = _mesh_pos()
        myq = 2 * x + y
        for j, (px, py) in enumerate(_other_chips(x, y)):
            _rcopy(p_ref.at[2 * px + py], land_ref.at[myq], send_sems, recv_sems, j, (px, py, c)).start()
        token[...] = jnp.zeros_like(token)

    land = pltpu.with_memory_space_constraint(lax.empty(part.shape, part.dtype), pltpu.HBM)
    extra = () if order_after is None else (order_after,)
    return _pcall(
        body, name=name,
        out_shape=(pltpu.SemaphoreType.DMA((3,)), pltpu.SemaphoreType.DMA((3,)), pltpu.HBM(part.shape, part.dtype),
                   pltpu.HBM(part.shape, part.dtype), jax.ShapeDtypeStruct((8, 128), F32)),
        in_specs=(_HBM, _HBM) + (_ANY,) * len(extra),
        out_specs=(_SEM, _SEM, _HBM, _HBM, pl.BlockSpec(memory_space=pltpu.VMEM)),
        input_output_aliases={0: 2, 1: 3}, compiler_params=pltpu.CompilerParams(has_side_effects=_EFFECT),
    )(pltpu.with_memory_space_constraint(part, pltpu.HBM), land, *extra)


def _chips_wait(send_sems, recv_sems, p_thru, land_thru, after, name):
    def body(p_ref, land_ref, send_sems, recv_sems, after_ref, p_out, land_out):
        x, y, c, _ = _mesh_pos()
        for j, (px, py) in enumerate(_other_chips(x, y)):
            q = 2 * px + py
            cp = _rcopy(p_ref.at[q], land_ref.at[q], send_sems, recv_sems, j, (px, py, c))
            cp.wait_send()
            cp.wait_recv()

    return _pcall(
        body, name=name, out_shape=(pltpu.HBM(p_thru.shape, p_thru.dtype), pltpu.HBM(p_thru.shape, p_thru.dtype)),
        in_specs=(_HBM, _HBM, _SEM, _SEM, _ANY), out_specs=(_HBM, _HBM), input_output_aliases={0: 0, 1: 1},
        compiler_params=pltpu.CompilerParams(has_side_effects=_EFFECT),
    )(p_thru, land_thru, send_sems, recv_sems, after)


def _sum_chips(part, land, qvec, name):
    n, R, C = part.shape
    tr = _pick(R, _ROW_TILES)

    def body(q_ref, p_ref, *refs):
        o_ref = refs[n]
        acc = None
        for q in range(n):
            term = jnp.where(q_ref[0] == q, p_ref[...], refs[q][...]).astype(F32)
            acc = term if acc is None else acc + term
        o_ref[...] = acc

    def land_spec(q):
        return pl.BlockSpec((None, tr, C), lambda i, m: (jnp.where(m[0] == q, (q + 1) % n, q), i, 0))

    grid_spec = pltpu.PrefetchScalarGridSpec(
        num_scalar_prefetch=1, grid=(R // tr,),
        in_specs=[pl.BlockSpec((None, tr, C), lambda i, m: (m[0], i, 0))] + [land_spec(q) for q in range(n)],
        out_specs=pl.BlockSpec((tr, C), lambda i, m: (i, 0)))
    return _pcall(body, grid_spec=grid_spec, out_shape=jax.ShapeDtypeStruct((R, C), F32), name=name)(qvec, part, *([land] * n))


def _exchange_small(gsmall):
    def body(s_ref, srecv_ref, send_sems, recv_sems, loc_sem):
        x, y, c, me = _mesh_pos()
        loc = pltpu.make_async_copy(s_ref, srecv_ref.at[me], loc_sem.at[0])
        loc.start()
        sends, recvs = [], []
        for k in range(1, N_DEV):
            to, pidx = _peer(x, y, c, k)
            sends.append(_rcopy(s_ref, srecv_ref.at[me], send_sems, recv_sems, k - 1, to))
            recvs.append(_rcopy(s_ref, srecv_ref.at[pidx], send_sems, recv_sems, k - 1, to))
        for cp in sends:
            cp.start()
        for cp in recvs:
            cp.wait_recv()
        for cp in sends:
            cp.wait_send()
        loc.wait()

    return _pcall(
        body, in_specs=[_ANY], out_specs=_ANY, out_shape=jax.ShapeDtypeStruct((N_DEV,) + gsmall.shape, gsmall.dtype),
        scratch_shapes=[pltpu.SemaphoreType.DMA((7,)), pltpu.SemaphoreType.DMA((7,)), pltpu.SemaphoreType.DMA((1,))],
        name="exchange_small",
    )(gsmall)


def _sum_slots(recv, name):
    n, R, C = recv.shape
    tr = _pick(R, _ROW_TILES)

    def body(r_ref, o_ref):
        acc = r_ref[0].astype(F32)
        for s in range(1, n):
            acc = acc + r_ref[s].astype(F32)
        o_ref[...] = acc

    return _pcall(body, grid=(R // tr,), in_specs=[pl.BlockSpec((n, tr, C), lambda i: (0, i, 0))],
                  out_specs=pl.BlockSpec((tr, C), lambda i: (i, 0)), out_shape=jax.ShapeDtypeStruct((R, C), F32), name=name)(recv)


def _adamw(w, g, m, v, name, order_after=None):
    shape = w.shape
    w2, g2, m2, v2 = (a.reshape(-1, shape[-1]) for a in (w, g, m, v))
    R, C = w2.shape
    tr = _pick(R, (256, 128, 64, 32, 16, 8)) if R > 256 else R
    n_in = 4 + int(order_after is not None)

    def body(*refs):
        w_ref, g_ref, m_ref, v_ref = refs[:4]
        d_ref, nm_ref, nv_ref = refs[n_in:]
        gg = g_ref[...]
        nm = ADAM_B1 * m_ref[...] + (1.0 - ADAM_B1) * gg
        nv = ADAM_B2 * v_ref[...] + (1.0 - ADAM_B2) * jnp.square(gg)
        m_hat = nm / (1.0 - ADAM_B1 ** ADAM_STEP)
        v_hat = nv / (1.0 - ADAM_B2 ** ADAM_STEP)
        d_ref[...] = -ADAM_LR * (m_hat / (jnp.sqrt(v_hat) + ADAM_EPS) + ADAM_WD * w_ref[...])
        nm_ref[...] = nm
        nv_ref[...] = nv

    spec = pl.BlockSpec((tr, C), lambda i: (i, 0))
    in_specs, args = [spec] * 4, [w2, g2, m2, v2]
    if order_after is not None:
        in_specs.append(pl.BlockSpec(order_after.shape, lambda i: (0, 0)))
        args.append(order_after)
    d, nm, nv = _pcall(body, grid=(R // tr,), in_specs=in_specs, out_specs=[spec] * 3,
                       out_shape=[jax.ShapeDtypeStruct((R, C), F32)] * 3, name=name)(*args)
    return d.reshape(shape), nm.reshape(shape), nv.reshape(shape)


def _to_rows(name, w):
    w = w[0]
    if name in ("ffn1_w_in", "ffn2_w_in", "w_in"):
        return w.T
    if name in ("w_uq", "w_ukv", "ple_w_proj"):
        return w.T.reshape(-1, 1024)
    return w


def _from_rows(name, g, shape):
    if name in ("ffn1_w_in", "ffn2_w_in", "w_in"):
        return g.T.reshape(shape)
    if name in ("w_uq", "w_ukv", "ple_w_proj"):
        return g.reshape(-1, shape[1]).T.reshape(shape)
    return g.reshape(shape)


def _unpack(wall, group):
    out, off = {}, 0
    for name, r in group:
        out[name] = wall[:, off:off + r, :].reshape(N_DEV * r, 1024)
        off += _pad16(r)
    return out


def _w_in_internal(wt):
    return jnp.concatenate([wt[0:6144], wt[6720:8768], wt[6144:6720], jnp.zeros((P_W - W_IN_COLS, 1024), wt.dtype)], axis=0)


def _w_in_external(d):
    return jnp.concatenate([d[0:6144], d[8192:8768], d[6144:8192]], axis=0)


def _rope_tables(positions):
    pos = positions[0].astype(F32)

    def cs(half):
        inv = ROPE_BASE ** (-jnp.arange(half, dtype=F32) / half)
        ang = pos[:, None] * inv
        return jnp.cos(ang), jnp.sin(ang)

    c, s = cs(64)
    c2, s2 = cs(32)
    return (jnp.concatenate([c, c], axis=1), jnp.concatenate([-s, s], axis=1),
            jnp.concatenate([c2, c2, c2, c2], axis=1), jnp.concatenate([-s2, s2, -s2, s2], axis=1))


def _local_step(x, p, positions, target, W, ln_g, ln_b, gn_g, qg, kvg, early=None, rest_weights=None, start_token=None):
    T = x.shape[0]
    tabs = _rope_tables(positions)
    rc = _ret_consts()
    lg = [ln_g[i:i + 1] for i in range(4)]
    lb = [ln_b[i:i + 1] for i in range(4)]
    pb = p.astype(BF16)

    hs1, gu1, *xb = _ffn_in(x, W["ffn1_w_in"], "ffn1_in", order_after=start_token)
    xb = xb[0] if xb else x
    f1, h1, h1b = _mm(hs1, W["ffn1_w_out"], name="ffn1_out", tm=LN_TM, epilogue=_ln_epilogue(x, 0.5, lg[0], lb[0]))
    if rest_weights is not None:
        W = {**W, **rest_weights(h1b)}
    w_in_t = _w_in_internal(W["w_in"])
    wuq = W["w_uq"].reshape(1536, LORA).reshape(HEADS, 192, LORA)
    wuq = jnp.concatenate([wuq[:, :128].reshape(1024, LORA), wuq[:, 128:].reshape(512, LORA)], axis=0)
    wukv = W["w_ukv"].reshape(2048, LORA).reshape(HEADS, 2, 128, LORA).transpose(1, 0, 2, 3).reshape(2048, LORA)
    wp_t = W["ple_w_proj"].reshape(1024, D_PLE)
    proj = _mm(h1b, w_in_t, tb=True, out_dtype=BF16, name="mixer_in")
    rq, rk, qn, kvn, kpe = _mixer_prep_fwd(proj, tabs, qg, kvg, T)
    y, yr, states = _ret_fwd(rq, rk, proj, proj, gn_g, rc, T)
    y_ret = _mm(yr, W["w_ret_o"], name="ret_o")
    qnope, qpe = _mm(qn, wuq, tb=True, name="mla_uq", tm=LN_TM, tn=1536, epilogue=_q_assemble_epilogue(tabs))
    kv = _mm(kvn, wukv, tb=True, out_dtype=BF16, name="mla_ukv")
    o, lse = _attn_fwd(qnope, qpe, kv, kpe, T)
    y_mla, mix = _mm(o, W["w_mla_o"], name="mla_o", tm=LN_TM, epilogue=_mix_epilogue(proj, y_ret))
    mixed, h2, h2b = _mm(mix, W["w_out"], name="mixer_out", tm=LN_TM, epilogue=_ln_epilogue(h1, 1.0, lg[1], lb[1]))
    hs2, gu2 = _ffn_in(h2b, W["ffn2_w_in"], "ffn2_in")
    f2, h3, h3b = _mm(hs2, W["ffn2_w_out"], name="ffn2_out", tm=LN_TM, epilogue=_ln_epilogue(h2, 0.5, lg[2], lb[2]))
    pp = _mm(pb, wp_t, tb=True, name="ple_proj")

    G = {}
    dh3_a, dgl, dpp, dg3, db3, loss = _mm(h3b, W["ple_w_gate"], name="ple_gate", tm=LN_TM,
                                          epilogue=_head_epilogue(h3, pp, target, lg[3], lb[3]))
    G["ple_w_gate"] = _mm(h3b, dgl, ta=True, name="d_ple_gate")
    G["ple_w_proj"] = _mm(dpp, pb, ta=True, name="d_ple_proj")
    dh2_a, df2, dg2, db2 = _mm(dgl, W["ple_w_gate"], tb=True, add=dh3_a, name="dh3", tm=LN_TM,
                               epilogue=_ln_bwd_epilogue(h2, f2, 0.5, lg[2], lb[2]))
    G["ffn2_w_out"] = _mm(hs2, df2, ta=True, name="d_ffn2_out")
    da2 = _ffn_act_bwd(df2, W["ffn2_w_out"], gu2, "ffn2_act_bwd")
    G["ffn2_w_in"] = _mm(da2, h2b, ta=True, name="d_ffn2_in")
    dh1_a, dmixed, dg1, db1 = _mm(da2, W["ffn2_w_in"], add=dh2_a, name="dh2", tm=LN_TM,
                                  epilogue=_ln_bwd_epilogue(h1, mixed, 1.0, lg[1], lb[1]))
    G["w_out"] = _mm(mix, dmixed, ta=True, name="d_mixer_out")
    dgr, dgm, dy_ret, dy_mla = _mm(dmixed, W["w_out"], tb=True, name="dmix", tm=LN_TM,
                                   epilogue=_mix_bwd_epilogue(proj, y_ret, y_mla))
    G["w_mla_o"] = _mm(o, dy_mla, ta=True, name="d_mla_o")
    dob, delta = _mm(dy_mla, W["w_mla_o"], tb=True, name="do", tm=LN_TM, epilogue=_delta_epilogue(o))
    dqn_f, dqpe_f, dkn, dkpe_all, dv = _attn_bwd(qnope, qpe, kv, kpe, dob, lse, delta, T)
    dq_n, dq_r = _q_assemble_bwd(dqn_f, dqpe_f, tabs, T)
    g_uq = jnp.concatenate([_mm(dq_n, qn, ta=True, name="d_uq_nope"), _mm(dq_r, qn, ta=True, name="d_uq_rope")], axis=0)
    g_uq = jnp.concatenate([g_uq[:1024].reshape(HEADS, 128, LORA), g_uq[1024:].reshape(HEADS, 64, LORA)], axis=1)
    G["w_uq"] = g_uq.reshape(1536 * LORA // 1024, 1024)
    dqn = _mm(dq_r, wuq[1024:], add=_mm(dq_n, wuq[:1024], name="dqn_a"), name="dqn_b")
    g_ukv = jnp.stack([_mm(dkn, kvn, ta=True, name="d_ukv_k"), _mm(dv, kvn, ta=True, name="d_ukv_v")], axis=0)
    G["w_ukv"] = g_ukv.reshape(2, HEADS, 128, LORA).transpose(1, 0, 2, 3).reshape(2048 * LORA // 1024, 1024)
    dkvn = _mm(dv, wukv[1024:], add=_mm(dkn, wukv[:1024], name="dkvn_a"), name="dkvn_b")
    dcq, dckv, dkpe, dqg, dkvg = _rms_bwd(proj, dqn, dkvn, dkpe_all, tabs, qg, kvg, T)
    G["w_ret_o"] = _mm(yr, dy_ret, ta=True, name="d_ret_o")
    dyr = _mm(dy_ret, W["w_ret_o"], tb=True, name="dyr")
    drq, drk, drv, drg, dgn = _ret_bwd(rq, rk, proj, y, proj, gn_g, states, dyr, rc, tabs, T)
    dproj = jnp.concatenate([drq, drk, drv, drg, dgr, dgm, dcq, dckv, dkpe, jnp.zeros((T, P_W - P_KPE - 128), BF16)], axis=1)
    G["w_in"] = _w_in_external(_mm(dproj, h1b, ta=True, name="d_mixer_in"))
    lg0 = lg[0] if early is None else lg[0] + early(G)[0:1, 0:1]
    dx_a, df1, dg0, db0 = _mm(dproj, w_in_t, add=dh1_a, name="dh1", tm=LN_TM,
                              epilogue=_ln_bwd_epilogue(x, f1, 0.5, lg0, lb[0]))
    G["ffn1_w_out"] = _mm(hs1, df1, ta=True, name="d_ffn1_out")
    da1 = _ffn_act_bwd(df1, W["ffn1_w_out"], gu1, "ffn1_act_bwd")
    G["ffn1_w_in"] = _mm(da1, xb, ta=True, name="d_ffn1_in")
    grad_x = _mm(da1, W["ffn1_w_in"], add=dx_a, name="grad_x")

    small = dict(ln_g=jnp.concatenate([dg0, dg1, dg2, dg3], axis=0), ln_b=jnp.concatenate([db0, db1, db2, db3], axis=0),
                 ret_gn_g=dgn, q_norm_g=dqg, kv_norm_g=dkvg)
    return loss, grad_x, G, small


def kernel(x, p, positions, ln_g, ln_b, ffn1_w_in, ffn1_w_out, w_in, ret_gn_g, w_ret_o, q_norm_g, kv_norm_g, w_uq, w_ukv, w_mla_o, w_out, ffn2_w_in, ffn2_w_out, ple_w_gate, ple_w_proj, loss_target, m_ln_g, m_ln_b, m_ffn1_w_in, m_ffn1_w_out, m_w_in, m_ret_gn_g, m_w_ret_o, m_q_norm_g, m_kv_norm_g, m_w_uq, m_w_ukv, m_w_mla_o, m_w_out, m_ffn2_w_in, m_ffn2_w_out, m_ple_w_gate, m_ple_w_proj, v_ln_g, v_ln_b, v_ffn1_w_in, v_ffn1_w_out, v_w_in, v_ret_gn_g, v_w_ret_o, v_q_norm_g, v_kv_norm_g, v_w_uq, v_w_ukv, v_w_mla_o, v_w_out, v_ffn2_w_in, v_ffn2_w_out, v_ple_w_gate, v_ple_w_proj):
    names = ("ln_g", "ln_b", "ffn1_w_in", "ffn1_w_out", "w_in", "ret_gn_g", "w_ret_o", "q_norm_g", "kv_norm_g", "w_uq", "w_ukv",
             "w_mla_o", "w_out", "ffn2_w_in", "ffn2_w_out", "ple_w_gate", "ple_w_proj")
    ws = dict(zip(names, (ln_g, ln_b, ffn1_w_in, ffn1_w_out, w_in, ret_gn_g, w_ret_o, q_norm_g, kv_norm_g, w_uq, w_ukv, w_mla_o,
                          w_out, ffn2_w_in, ffn2_w_out, ple_w_gate, ple_w_proj)))
    ms = dict(zip(names, (m_ln_g, m_ln_b, m_ffn1_w_in, m_ffn1_w_out, m_w_in, m_ret_gn_g, m_w_ret_o, m_q_norm_g, m_kv_norm_g, m_w_uq,
                          m_w_ukv, m_w_mla_o, m_w_out, m_ffn2_w_in, m_ffn2_w_out, m_ple_w_gate, m_ple_w_proj)))
    vs = dict(zip(names, (v_ln_g, v_ln_b, v_ffn1_w_in, v_ffn1_w_out, v_w_in, v_ret_gn_g, v_w_ret_o, v_q_norm_g, v_kv_norm_g, v_w_uq,
                          v_w_ukv, v_w_mla_o, v_w_out, v_ffn2_w_in, v_ffn2_w_out, v_ple_w_gate, v_ple_w_proj)))

    parts = []
    for name, r in PACK:
        rows = _to_rows(name, ws[name])
        if _pad16(r) != r:
            rows = jnp.concatenate([rows, jnp.zeros((_pad16(r) - r, 1024), F32)], axis=0)
        parts.append(rows)
    wsh_first = jnp.concatenate(parts[:len(PACK_LATE)], axis=0).astype(BF16)
    wsh_rest = jnp.concatenate(parts[len(PACK_LATE):], axis=0).astype(BF16)
    ssh = jnp.concatenate([ln_g[0], ln_b[0]], axis=0)
    wall_first, sall = _all_gather(wsh_first, ssh)
    *gather_handles, start_token = _gather_start(wsh_rest, wall_first)
    W = _unpack(wall_first, PACK_LATE)
    ln_full = sall.reshape(N_DEV, 2, 4, 128).transpose(1, 2, 0, 3).reshape(2, 4, 1024)

    def rest_weights(after):
        w_thru, land = _gather_wait(*gather_handles, after)
        return _unpack(_gather_finish(w_thru, land), PACK_EARLY)

    cvec = lax.axis_index("c").astype(jnp.int32).reshape(1)
    qvec = (2 * lax.axis_index("x") + lax.axis_index("y")).astype(jnp.int32).reshape(1)

    def chip_partials(G, group, tag):
        gparts = []
        for name, r in group:
            g = G[name].reshape(N_DEV, r, 1024)
            if _pad16(r) != r:
                g = jnp.concatenate([g, jnp.zeros((N_DEV, _pad16(r) - r, 1024), g.dtype)], axis=1)
            gparts.append(g)
        gfull = jnp.concatenate(gparts, axis=1).astype(BF16)
        g4 = gfull.reshape(4, 2, gfull.shape[1], 1024)
        return _sum_sibling(g4, _exchange_sibling(g4, "exchange_sibling_" + tag), cvec, "sum_sibling_" + tag)

    in_flight = []

    def early(G):
        *handles, token = _chips_start(chip_partials(G, PACK_EARLY, "early"), "chips_start_early")
        in_flight.append(handles)
        return token

    loss_p, grad_x, G, small = _local_step(x[0], p[0, 0], positions, loss_target[0], W, ln_full[0], ln_full[1],
                                           ret_gn_g, q_norm_g, kv_norm_g, early=early, rest_weights=rest_weights,
                                           start_token=start_token)

    part_e, land_e = _chips_wait(*in_flight[0], grad_x, "chips_wait_early")
    gsh_early = _sum_chips(part_e, land_e, qvec, "sum_grads_early")
    pad256 = lambda a: jnp.concatenate([a, jnp.zeros((1, 1024 - a.shape[1]), F32)], axis=1)
    gsmall = jnp.concatenate([small["ln_g"], small["ln_b"], small["ret_gn_g"].reshape(2, 1024), pad256(small["q_norm_g"]),
                              pad256(small["kv_norm_g"]), jnp.zeros((SMALL_ROWS - 12, 1024), F32)], axis=0)
    srecv = _exchange_small(gsmall)
    part_l = chip_partials(G, PACK_LATE, "late")
    *late_handles, late_token = _chips_start(part_l, "chips_start_late", order_after=srecv)
    ssum = _sum_slots(srecv, "sum_small_grads")

    def unpack_grads(group, gsh):
        out, off = {}, 0
        for name, r in group:
            out[name] = _from_rows(name, gsh[off:off + r], ws[name].shape)
            off += _pad16(r)
        return out

    grads = unpack_grads(PACK_EARLY, gsh_early)
    me = 4 * lax.axis_index("x") + 2 * lax.axis_index("y") + lax.axis_index("c")
    grads["ln_g"] = lax.dynamic_slice(ssum[0:4], (0, me * 128), (4, 128)).reshape(1, 4, 128)
    grads["ln_b"] = lax.dynamic_slice(ssum[4:8], (0, me * 128), (4, 128)).reshape(1, 4, 128)
    grads["ret_gn_g"] = ssum[8:10].reshape(1, 2048)
    grads["q_norm_g"] = ssum[10:11, :256]
    grads["kv_norm_g"] = ssum[11:12, :256]

    delta, new_m, new_v = {}, {}, {}
    late_names = [n for n, _ in PACK_LATE]
    for name in names:
        if name not in late_names:
            delta[name], new_m[name], new_v[name] = _adamw(ws[name], grads[name], ms[name], vs[name], "adamw_" + name,
                                                           order_after=late_token)
    part_l, land_l = _chips_wait(*late_handles, new_v["w_in"], "chips_wait_late")
    grads.update(unpack_grads(PACK_LATE, _sum_chips(part_l, land_l, qvec, "sum_grads_late")))
    for name in late_names:
        delta[name], new_m[name], new_v[name] = _adamw(ws[name], grads[name], ms[name], vs[name], "adamw_" + name)

    loss = lax.psum(loss_p[0, 0], ("x", "y", "c"))
    return (loss, grad_x[None], *[grads[n] for n in names], *[delta[n] for n in names],
            *[new_m[n] for n in names], *[new_v[n] for n in names])
```

```python
import functools
import math

import jax
import jax.numpy as jnp
from jax import lax
from jax.experimental import pallas as pl
from jax.experimental.pallas import tpu as pltpu

F32 = jnp.float32
BF16 = jnp.bfloat16

N_DEV = 8
D = 1024
D_FF = 2816
D_PLE = 256
CHUNK = 64
HEADS = 8
RET_DK = 128
RET_DV = 256
MLA_NOPE = 128
MLA_ROPE = 64
MLA_DV = 128
LORA = 256
ROPE_BASE = 10000.0
EPS = 1e-5
ALPHA = 2.0 ** 0.25
RET_SCALE = RET_DK ** -0.5
MLA_SCALE = (MLA_NOPE + MLA_ROPE) ** -0.5
NEG = -1e30

ADAM_LR = 0.001
ADAM_B1 = 0.9
ADAM_B2 = 0.999
ADAM_EPS = 1e-08
ADAM_WD = 0.01
ADAM_STEP = 10

P_RQ, P_RK, P_RV, P_RG, P_GR, P_GM, P_CQ, P_CKV, P_KPE, P_W = 0, 1024, 2048, 4096, 6144, 7168, 8192, 8448, 8704, 8960
W_IN_COLS = 8768
RET_L = 256
ATT_TF = 1024
ATT_TB = 1024
ATT_HP = 4
LOG2E = math.log2(math.e)
Q_PRESCALE = MLA_SCALE * LOG2E

PACK = (("ffn1_w_in", 704), ("ffn1_w_out", 352), ("w_in", 1096), ("w_ret_o", 256), ("w_uq", 48), ("w_ukv", 64),
        ("w_mla_o", 128), ("w_out", 128), ("ffn2_w_in", 704), ("ffn2_w_out", 352), ("ple_w_gate", 128), ("ple_w_proj", 32))


def _pad16(r):
    return -(-r // 16) * 16


PACK_ROWS = sum(_pad16(r) for _, r in PACK)
PACK_LATE = PACK[:2]
PACK_EARLY = PACK[2:]
SMALL_ROWS = 16


def _pcall(body, **kw):
    return pl.pallas_call(body, **kw)


def _pick(dim, prefs):
    for p in prefs:
        if dim % p == 0:
            return p
    return dim


def _sigmoid(x):
    return 1.0 / (1.0 + jnp.exp(-x))


def _silu(x):
    return x * _sigmoid(x)


def _ln(r, g, b):
    mu = jnp.mean(r, axis=-1, keepdims=True)
    var = jnp.mean(jnp.square(r - mu), axis=-1, keepdims=True)
    return (r - mu) * lax.rsqrt(var + EPS) * g + b


def _rms(x, g):
    return x * lax.rsqrt(jnp.mean(jnp.square(x), axis=-1, keepdims=True) + EPS) * g


def _dot(a, b, ca, cb):
    return lax.dot_general(a, b, (((ca,), (cb,)), ((), ())), preferred_element_type=F32)


def _accum(ref, val, first=None):
    @pl.when(pl.program_id(0) == 0 if first is None else first)
    def _():
        ref[...] = jnp.zeros_like(ref)

    ref[...] += val


def _mm(a, b, *, ta=False, tb=False, add=None, out_dtype=None, name, tm=None, tn=None, tk=None, epilogue=None):
    parts = a.shape[0] if a.ndim == 3 else 1
    ar, ac = a.shape[-2], a.shape[-1]
    out_dtype = out_dtype or (BF16 if ta else F32)
    if ta:
        K, M = ar, ac * parts
    else:
        M, K = ar, ac * parts
    if tb:
        N, K2 = b.shape
    else:
        K2, N = b.shape
    assert K == K2, (a.shape, b.shape, ta, tb)
    big = (1024, 1408, 1280, 768, 512, 256, 128)
    tm = tm or _pick(ac if (ta and parts > 1) else M, big)
    tn = tn or (N if N <= 1024 else _pick(N, big))
    kdim = ac if (not ta and parts > 1) else K
    tk = tk or (kdim if kdim <= 2816 and parts == 1 else
                _pick(kdim, (2048, 1408, 1280, 1024, 512) if tm <= 1024 else (1024, 1408, 1280, 512)))
    nk = K // tk
    grid = (M // tm, N // tn, nk)
    if parts > 1 and ta:
        per = ac // tm
        a_spec = pl.BlockSpec((None, tk, tm), lambda i, j, k: (i // per, k, i % per))
    elif parts > 1:
        per = ac // tk
        a_spec = pl.BlockSpec((None, tm, tk), lambda i, j, k: (k // per, i, k % per))
    else:
        a_spec = pl.BlockSpec((tk, tm), lambda i, j, k: (k, i)) if ta else pl.BlockSpec((tm, tk), lambda i, j, k: (i, k))
    b_spec = pl.BlockSpec((tn, tk), lambda i, j, k: (j, k)) if tb else pl.BlockSpec((tk, tn), lambda i, j, k: (k, j))
    o_spec = pl.BlockSpec((tm, tn), lambda i, j, k: (i, j))
    ca, cb = (0 if ta else 1), (1 if tb else 0)
    has_add = add is not None
    n_in = 2 + int(has_add)
    if epilogue is not None:
        assert tn == N and not ta
        ep_fn, ep_rows, ep_whole, ep_outs, ep_accs = epilogue
        n_ep_in = len(ep_rows) + len(ep_whole)
        n_out = len(ep_outs) + len(ep_accs)
    else:
        n_ep_in, n_out = 0, 1

    def body(*refs):
        a_ref, b_ref = refs[0], refs[1]
        add_ref = refs[2] if has_add else None
        o_ref = refs[n_in + n_ep_in]
        first_row_tile = pl.program_id(0) == 0

        def finish(r):
            if has_add:
                r = r + add_ref[...].astype(F32)
            if epilogue is not None:
                ep_fn(r, refs[n_in:n_in + n_ep_in], refs[n_in + n_ep_in:n_in + n_ep_in + n_out], first_row_tile)
            else:
                o_ref[...] = r.astype(out_dtype)

        if nk == 1:
            finish(_dot(a_ref[...], b_ref[...], ca, cb))
            return
        acc_ref = refs[-1]
        k = pl.program_id(2)

        @pl.when(k == 0)
        def _():
            acc_ref[...] = jnp.zeros_like(acc_ref)

        acc_ref[...] += _dot(a_ref[...], b_ref[...], ca, cb)

        @pl.when(k == nk - 1)
        def _():
            finish(acc_ref[...])

    in_specs = [a_spec, b_spec] + ([o_spec] if has_add else [])
    args = (a, b) + ((add,) if has_add else ())
    out_specs, out_shape = o_spec, jax.ShapeDtypeStruct((M, N), out_dtype)
    if epilogue is not None:
        for row_in in ep_rows:
            arr, col_block, width = (tuple(row_in) + (N,))[:3] if isinstance(row_in, tuple) else (row_in, 0, N)
            in_specs.append(pl.BlockSpec((tm, width), lambda i, j, k, _c=col_block: (i, _c)))
            args += (arr,)
        in_specs += [pl.BlockSpec(w.shape, lambda i, j, k, _n=w.ndim: (0,) * _n) for w in ep_whole]
        args += tuple(ep_whole)
        outs = [o if isinstance(o, tuple) else (o, N) for o in ep_outs]
        out_specs = ([pl.BlockSpec((tm, w), lambda i, j, k: (i, 0)) for _, w in outs]
                     + [pl.BlockSpec((r, N), lambda i, j, k: (0, 0)) for r in ep_accs])
        out_shape = [jax.ShapeDtypeStruct((M, w), dt) for dt, w in outs] + [jax.ShapeDtypeStruct((r, N), F32) for r in ep_accs]
    return _pcall(
        body, grid=grid, in_specs=in_specs, out_specs=out_specs, out_shape=out_shape,
        scratch_shapes=[pltpu.VMEM((tm, tn), F32)] if nk > 1 else [], name=name,
        compiler_params=pltpu.CompilerParams(dimension_semantics=("arbitrary" if epilogue is not None else "parallel", "parallel", "arbitrary")),
    )(*args)


def _ln_epilogue(res, c, g, b):
    def fn(r, ins, outs, first):
        res_ref, g_ref, b_ref = ins
        f_ref, h_ref, hb_ref = outs
        h = _ln(ALPHA * res_ref[...] + c * r, g_ref[...], b_ref[...])
        f_ref[...] = r
        h_ref[...] = h
        hb_ref[...] = h.astype(BF16)

    return (fn, [res], [g, b], (F32, F32, BF16), ())


def _ln_bwd_epilogue(res, f, c, g, b):
    def fn(r, ins, outs, first):
        res_ref, f_ref, g_ref, b_ref = ins
        dr_ref, df_ref, dg_ref, db_ref = outs
        pre = ALPHA * res_ref[...] + c * f_ref[...]
        xc = pre - jnp.mean(pre, axis=-1, keepdims=True)
        rstd = lax.rsqrt(jnp.mean(jnp.square(xc), axis=-1, keepdims=True) + EPS)
        xhat = xc * rstd
        dyg = r * g_ref[...]
        dpre = rstd * (dyg - jnp.mean(dyg, axis=-1, keepdims=True) - xhat * jnp.mean(dyg * xhat, axis=-1, keepdims=True))
        dr_ref[...] = ALPHA * dpre
        df_ref[...] = (c * dpre).astype(BF16)
        _accum(dg_ref, jnp.sum(r * xhat, axis=0, keepdims=True), first)
        _accum(db_ref, jnp.sum(r, axis=0, keepdims=True), first)

    return (fn, [res, f], [g, b], (F32, BF16), (1, 1))


def _rows(body, T, tm, ins, outs, name, accs=()):
    in_specs, args = [], []
    for arr, w, cb in ins:
        if w is None:
            in_specs.append(pl.BlockSpec(arr.shape, lambda i, _n=arr.ndim: (0,) * _n))
        else:
            in_specs.append(pl.BlockSpec((tm, w), lambda i, _cb=cb: (i, _cb)))
        args.append(arr)
    out_specs = [pl.BlockSpec((tm, w), lambda i: (i, 0)) for w, _ in outs]
    out_shape = [jax.ShapeDtypeStruct((T, w), dt) for w, dt in outs]
    for r, w in accs:
        out_specs.append(pl.BlockSpec((r, w), lambda i: (0, 0)))
        out_shape.append(jax.ShapeDtypeStruct((r, w), F32))
    return _pcall(
        body, grid=(T // tm,), in_specs=in_specs, out_specs=out_specs, out_shape=out_shape, name=name,
        compiler_params=pltpu.CompilerParams(dimension_semantics=("arbitrary",)),
    )(*args)


FFN_TN = 1408
FFN_TM = 512
LN_TM = 512


def _ffn_in(x, wt, name, order_after=None):
    T = x.shape[0]
    tm, tn = min(FFN_TM, T), FFN_TN
    nj = D_FF // tn
    emit_xb = x.dtype != BF16
    n_in = 3 + int(order_after is not None)

    def body(*refs):
        x_ref, wg_ref, wu_ref = refs[:3]
        hs_ref, gu_ref = refs[n_in], refs[n_in + 1]
        xv = x_ref[...].astype(BF16)
        g = _dot(xv, wg_ref[...], 1, 1)
        u = _dot(xv, wu_ref[...], 1, 1)
        hs_ref[...] = (_silu(g) * u).astype(BF16)
        gu_ref[0] = g.astype(BF16)
        gu_ref[1] = u.astype(BF16)
        if emit_xb:
            refs[n_in + 2][...] = xv

    in_specs = [pl.BlockSpec((tm, D), lambda i, j: (i, 0)), pl.BlockSpec((tn, D), lambda i, j: (j, 0)),
                pl.BlockSpec((tn, D), lambda i, j: (j + nj, 0))]
    args = [x, wt, wt]
    if order_after is not None:
        in_specs.append(pl.BlockSpec(order_after.shape, lambda i, j: (0, 0)))
        args.append(order_after)
    out_specs = [pl.BlockSpec((tm, tn), lambda i, j: (i, j)), pl.BlockSpec((2, tm, tn), lambda i, j: (0, i, j))]
    out_shape = [jax.ShapeDtypeStruct((T, D_FF), BF16), jax.ShapeDtypeStruct((2, T, D_FF), BF16)]
    if emit_xb:
        out_specs.append(pl.BlockSpec((tm, D), lambda i, j: (i, 0)))
        out_shape.append(jax.ShapeDtypeStruct((T, D), BF16))
    return _pcall(
        body, grid=(T // tm, nj), in_specs=in_specs, out_specs=out_specs, out_shape=out_shape, name=name,
        compiler_params=pltpu.CompilerParams(dimension_semantics=("parallel", "arbitrary")),
    )(*args)


def _ffn_act_bwd(df, wo, gu, name):
    T = df.shape[0]
    tm, tn = min(FFN_TM, T), FFN_TN

    def body(d_ref, w_ref, gu_ref, o_ref):
        dhs = _dot(d_ref[...], w_ref[...], 1, 1)
        g, u = gu_ref[0].astype(F32), gu_ref[1].astype(F32)
        sig = _sigmoid(g)
        act = g * sig
        o_ref[0] = (dhs * u * (sig + act * (1.0 - sig))).astype(BF16)
        o_ref[1] = (dhs * act).astype(BF16)

    return _pcall(
        body, grid=(T // tm, D_FF // tn),
        in_specs=[pl.BlockSpec((tm, D), lambda i, j: (i, 0)), pl.BlockSpec((tn, D), lambda i, j: (j, 0)),
                  pl.BlockSpec((2, tm, tn), lambda i, j: (0, i, j))],
        out_specs=pl.BlockSpec((2, tm, tn), lambda i, j: (0, i, j)),
        out_shape=jax.ShapeDtypeStruct((2, T, D_FF), BF16), name=name,
        compiler_params=pltpu.CompilerParams(dimension_semantics=("parallel", "parallel")),
    )(df, wo, gu)


def _rope128(t, cos, sin_s):
    return t * cos + pltpu.roll(t, 64, 1) * sin_s


def _rope128_t(g, cos, sin_s):
    return g * cos - pltpu.roll(g, 64, 1) * sin_s


def _partner32(t):
    lane = lax.broadcasted_iota(jnp.int32, t.shape, 1)
    return jnp.where((lane & 32) == 0, pltpu.roll(t, 96, 1), pltpu.roll(t, 32, 1))


def _rope64(t, cos, sin_s):
    return t * cos + _partner32(t) * sin_s


def _rope64_t(g, cos, sin_s):
    return g * cos - _partner32(g) * sin_s


def _mixer_prep_fwd(proj, tabs, qg, kvg, T):
    cos128, sin128, cos64, sin64 = tabs

    def body(rq_ref, rk_ref, cq_ref, ckv_ref, kpe_ref, c1_ref, s1_ref, c2_ref, s2_ref, qg_ref, kvg_ref,
             oq_ref, ok_ref, oqn_ref, okvn_ref, okpe_ref):
        c1, s1 = c1_ref[...], s1_ref[...]
        for h in range(HEADS):
            sl = slice(h * RET_DK, (h + 1) * RET_DK)
            oq_ref[:, sl] = _rope128(rq_ref[:, sl].astype(F32), c1, s1).astype(BF16)
            ok_ref[:, sl] = (_rope128(rk_ref[:, sl].astype(F32), c1, s1) * RET_SCALE).astype(BF16)
        oqn_ref[...] = _rms(cq_ref[...].astype(F32), qg_ref[...]).astype(BF16)
        okvn_ref[...] = _rms(ckv_ref[...].astype(F32), kvg_ref[...]).astype(BF16)
        okpe_ref[...] = _rope64(kpe_ref[...].astype(F32), c2_ref[...], s2_ref[...]).astype(BF16)

    ins = [(proj, 1024, 0), (proj, 1024, 1), (proj, 256, P_CQ // 256), (proj, 256, P_CKV // 256),
           (proj, 128, P_KPE // 128), (cos128, 128, 0), (sin128, 128, 0), (cos64, 128, 0), (sin64, 128, 0),
           (qg, None, None), (kvg, None, None)]
    outs = [(1024, BF16), (1024, BF16), (LORA, BF16), (LORA, BF16), (128, BF16)]
    return _rows(body, T, 256, ins, outs, "mixer_prep_fwd")


def _rms_bwd(proj, dqn, dkvn, dkpe_all, tabs, qg, kvg, T):
    _, _, cos64, sin64 = tabs

    def body(cq_ref, ckv_ref, dq_ref, dkv_ref, dk_ref, c2_ref, s2_ref, qg_ref, kvg_ref, ocq_ref, ockv_ref, okpe_ref,
             dqg_ref, dkvg_ref):
        _, vjp = jax.vjp(_rms, cq_ref[...].astype(F32), qg_ref[...])
        dx, dg = vjp(dq_ref[...])
        ocq_ref[...] = dx.astype(BF16)
        _accum(dqg_ref, dg)
        _, vjp = jax.vjp(_rms, ckv_ref[...].astype(F32), kvg_ref[...])
        dx, dg = vjp(dkv_ref[...])
        ockv_ref[...] = dx.astype(BF16)
        _accum(dkvg_ref, dg)
        g = dk_ref[:, 0:128]
        for h in range(1, HEADS):
            g = g + dk_ref[:, h * 128:(h + 1) * 128]
        lane = lax.broadcasted_iota(jnp.int32, g.shape, 1)
        g = jnp.where(lane < MLA_ROPE, g, 0.0)
        okpe_ref[...] = _rope64_t(g, c2_ref[...], s2_ref[...]).astype(BF16)

    ins = [(proj, 256, P_CQ // 256), (proj, 256, P_CKV // 256), (dqn, LORA, 0), (dkvn, LORA, 0), (dkpe_all, 1024, 0),
           (cos64, 128, 0), (sin64, 128, 0), (qg, None, None), (kvg, None, None)]
    return _rows(body, T, 256, ins, [(LORA, BF16), (LORA, BF16), (128, BF16)], "rms_bwd", accs=[(1, LORA), (1, LORA)])


def _gn_gate_bwd(y, rg, g, d):
    xc = y - jnp.mean(y, axis=-1, keepdims=True)
    rstd = lax.rsqrt(jnp.mean(jnp.square(xc), axis=-1, keepdims=True) + EPS)
    xhat = xc * rstd
    sig = _sigmoid(rg)
    act = rg * sig
    dn = d * act
    drg = d * (xhat * g) * (sig + act * (1.0 - sig))
    dxh = dn * g
    dy = rstd * (dxh - jnp.mean(dxh, axis=-1, keepdims=True) - xhat * jnp.mean(dxh * xhat, axis=-1, keepdims=True))
    return dy, drg, jnp.sum(dn * xhat, axis=0, keepdims=True)


def _gn_gate(y, rg, g):
    mu = jnp.mean(y, axis=-1, keepdims=True)
    var = jnp.mean(jnp.square(y - mu), axis=-1, keepdims=True)
    return _silu(rg) * ((y - mu) * lax.rsqrt(var + EPS) * g)


def _q_assemble_epilogue(tabs):
    _, _, cos64, sin64 = tabs

    def fn(q, ins, outs, first):
        c_ref, s_ref = ins
        on_ref, op_ref = outs
        on_ref[...] = (q[:, :1024] * Q_PRESCALE).astype(BF16)
        c, s = c_ref[...], s_ref[...]
        lane = lax.broadcasted_iota(jnp.int32, c.shape, 1)
        for j in range(HEADS // 2):
            r = _rope64(q[:, 1024 + 128 * j:1024 + 128 * (j + 1)], c, s) * Q_PRESCALE
            op_ref[:, 256 * j:256 * j + 128] = jnp.where(lane < 64, r, 0.0).astype(BF16)
            op_ref[:, 256 * j + 128:256 * j + 256] = jnp.where(lane < 64, pltpu.roll(r, 64, 1), 0.0).astype(BF16)

    return (fn, [(cos64, 0, 128), (sin64, 0, 128)], [], ((BF16, 1024), (BF16, 1024)), ())


def _q_assemble_bwd(dqn, dqpe, tabs, T):
    _, _, cos64, sin64 = tabs

    def body(dn_ref, dp_ref, c_ref, s_ref, on_ref, op_ref):
        on_ref[...] = dn_ref[...].astype(BF16)
        c, s = c_ref[...], s_ref[...]
        lane = lax.broadcasted_iota(jnp.int32, c.shape, 1)
        for j in range(HEADS // 2):
            g = jnp.where(lane < 64, dp_ref[:, 256 * j:256 * j + 128], pltpu.roll(dp_ref[:, 256 * j + 128:256 * j + 256], 64, 1))
            op_ref[:, 128 * j:128 * (j + 1)] = _rope64_t(g, c, s).astype(BF16)

    return _rows(body, T, 256, [(dqn, 1024, 0), (dqpe, 1024, 0), (cos64, 128, 0), (sin64, 128, 0)],
                 [(1024, BF16), (512, BF16)], "q_assemble_bwd")


def _mix_fn(gr, gm, yr, ym):
    return _sigmoid(gr) * yr + _sigmoid(gm) * ym


def _mix_epilogue(proj, y_ret):
    def fn(r, ins, outs, first):
        gr_ref, gm_ref, yr_ref = ins
        ym_ref, mix_ref = outs
        ym_ref[...] = r.astype(BF16)
        mix_ref[...] = _mix_fn(gr_ref[...].astype(F32), gm_ref[...].astype(F32), yr_ref[...].astype(F32), r).astype(BF16)

    return (fn, [(proj, P_GR // D), (proj, P_GM // D), y_ret], [], (BF16, BF16), ())


def _mix_bwd_epilogue(proj, y_ret, y_mla):
    def fn(r, ins, outs, first):
        gr_ref, gm_ref, yr_ref, ym_ref = ins
        dgr_ref, dgm_ref, dyr_ref, dym_ref = outs
        sr, sm = _sigmoid(gr_ref[...].astype(F32)), _sigmoid(gm_ref[...].astype(F32))
        dgr_ref[...] = (r * yr_ref[...].astype(F32) * (sr * (1.0 - sr))).astype(BF16)
        dgm_ref[...] = (r * ym_ref[...].astype(F32) * (sm * (1.0 - sm))).astype(BF16)
        dyr_ref[...] = (r * sr).astype(BF16)
        dym_ref[...] = (r * sm).astype(BF16)

    return (fn, [(proj, P_GR // D), (proj, P_GM // D), y_ret, y_mla], [], (BF16,) * 4, ())


def _head_epilogue(h3, pp, tgt, g, b):
    def fn(r, ins, outs, first):
        h_ref, pp_ref, t_ref, g_ref, b_ref = ins
        dh_ref, dgl_ref, dpp_ref, dg_ref, db_ref, loss_ref = outs

        sg, pp, gain = _sigmoid(r), pp_ref[...], g_ref[...]
        pre = ALPHA * h_ref[...] + sg * pp
        xc = pre - jnp.mean(pre, axis=-1, keepdims=True)
        rstd = lax.rsqrt(jnp.mean(jnp.square(xc), axis=-1, keepdims=True) + EPS)
        xhat = xc * rstd
        err = xhat * gain + b_ref[...] - t_ref[...]
        dy = err * (1.0 / D)
        dyg = dy * gain
        dpre = rstd * (dyg - jnp.mean(dyg, axis=-1, keepdims=True) - xhat * jnp.mean(dyg * xhat, axis=-1, keepdims=True))
        dh_ref[...] = ALPHA * dpre
        dgl_ref[...] = (dpre * pp * (sg * (1.0 - sg))).astype(BF16)
        dpp_ref[...] = (dpre * sg).astype(BF16)
        _accum(dg_ref, jnp.sum(dy * xhat, axis=0, keepdims=True), first)
        _accum(db_ref, jnp.sum(dy, axis=0, keepdims=True), first)
        part = 0.5 * jnp.sum(jnp.mean(jnp.square(err), axis=-1, keepdims=True), axis=0, keepdims=True)
        _accum(loss_ref, jnp.broadcast_to(part, loss_ref.shape), first)

    return (fn, [h3, pp, tgt], [g, b], (F32, BF16, BF16), (1, 1, 8))


def _delta_epilogue(o):
    def fn(r, ins, outs, first):
        (o_ref,) = ins
        db_ref, dl_ref = outs
        db_ref[...] = r.astype(BF16)
        for h in range(HEADS):
            sl = slice(h * MLA_DV, (h + 1) * MLA_DV)
            dl = jnp.sum(r[:, sl] * o_ref[:, sl].astype(F32), axis=-1, keepdims=True)
            dl_ref[:, sl] = jnp.broadcast_to(dl, (r.shape[0], MLA_DV))

    return (fn, [o], [], (BF16, F32), ())


def _ret_consts():
    L = RET_L
    lg = jnp.log(1.0 - 2.0 ** (-5.0 - jnp.arange(HEADS, dtype=F32)))[:, None, None]
    idx = jnp.arange(L, dtype=F32)
    ch = jnp.arange(L) // CHUNK
    dist = idx[:, None] - idx[None, :]
    same = (ch[:, None] == ch[None, :])[None]
    earlier = (ch[None, :] < ch[:, None])[None]
    dm = jnp.where(same, jnp.exp(lg * jnp.abs(dist)[None]), jnp.where(earlier, jnp.exp(lg * dist[None]), 0.0))
    xi = jnp.broadcast_to(jnp.exp(lg * (idx + 1.0)[None, :, None]), (HEADS, L, 128))
    zeta = jnp.broadcast_to(jnp.exp(lg * (L - 1.0 - idx)[None, :, None]), (HEADS, L, 128))
    gl = jnp.broadcast_to(jnp.exp(lg * float(L)), (HEADS, 8, 128))
    return dm.astype(F32), xi.astype(F32), zeta.astype(F32), gl.astype(F32)


def _whole(arr):
    return pl.BlockSpec(arr.shape, lambda n, _nd=arr.ndim: (0,) * _nd)


def _ret_fwd(q, k, v, proj, gn_g, consts, T):
    dm, xi, zeta, gl = consts
    L = RET_L
    n_sc = T // L

    def body(q_ref, k_ref, v_ref, rg_ref, g_ref, dm_ref, xi_ref, ze_ref, gl_ref, y_ref, yr_ref, s_ref, st_ref):
        @pl.when(pl.program_id(0) == 0)
        def _():
            st_ref[...] = jnp.zeros_like(st_ref)

        for h in range(HEADS):
            ks, vs = slice(h * RET_DK, (h + 1) * RET_DK), slice(h * RET_DV, (h + 1) * RET_DV)
            qq, kk, vv = q_ref[:, ks], k_ref[:, ks], v_ref[:, vs]
            st = st_ref[h]
            s_ref[h, 0] = st
            p = (_dot(qq, kk, 1, 1) * dm_ref[h]).astype(BF16)
            cross = _dot(qq, st.astype(BF16), 1, 0)
            xi_c = jnp.concatenate([xi_ref[h], xi_ref[h]], axis=1)
            y = _dot(p, vv, 1, 0) + cross * xi_c
            y_ref[:, vs] = y
            yr_ref[:, vs] = _gn_gate(y, rg_ref[:, vs].astype(F32), g_ref[:, vs]).astype(BF16)
            kz = (kk.astype(F32) * ze_ref[h]).astype(BF16)
            gl2 = jnp.concatenate([gl_ref[h, 0:1, :], gl_ref[h, 0:1, :]], axis=1)
            st_ref[h] = st * gl2 + _dot(kz, vv, 0, 0)

    return _pcall(
        body, grid=(n_sc,),
        in_specs=[pl.BlockSpec((L, 1024), lambda n: (n, 0)), pl.BlockSpec((L, 1024), lambda n: (n, 0)),
                  pl.BlockSpec((L, 2048), lambda n: (n, P_RV // 2048)), pl.BlockSpec((L, 2048), lambda n: (n, P_RG // 2048)),
                  _whole(gn_g), _whole(dm), _whole(xi), _whole(zeta), _whole(gl)],
        out_specs=[pl.BlockSpec((L, 2048), lambda n: (n, 0)), pl.BlockSpec((L, 2048), lambda n: (n, 0)),
                   pl.BlockSpec((HEADS, 1, 128, 256), lambda n: (0, n, 0, 0))],
        out_shape=[jax.ShapeDtypeStruct((T, HEADS * RET_DV), F32), jax.ShapeDtypeStruct((T, HEADS * RET_DV), BF16),
                   jax.ShapeDtypeStruct((HEADS, n_sc, 128, 256), F32)],
        scratch_shapes=[pltpu.VMEM((HEADS, 128, 256), F32)], name="ret_fwd",
        compiler_params=pltpu.CompilerParams(dimension_semantics=("arbitrary",)),
    )(q, k, v, proj, gn_g, dm, xi, zeta, gl)


def _ret_bwd(q, k, v, y, proj, gn_g, states, dyr, consts, tabs, T):
    dm, xi, zeta, gl = consts
    cos128, sin128, _, _ = tabs
    L = RET_L
    n_sc = T // L

    def body(q_ref, k_ref, v_ref, y_ref, rg_ref, g_ref, d_ref, s_ref, dm_ref, xi_ref, ze_ref, gl_ref, c_ref, sn_ref,
             dq_ref, dk_ref, dv_ref, drg_ref, dg_ref, gs_ref):
        @pl.when(pl.program_id(0) == 0)
        def _():
            gs_ref[...] = jnp.zeros_like(gs_ref)

        c, sn = c_ref[...], sn_ref[...]
        dgs = []
        for h in range(HEADS):
            ks, vs = slice(h * RET_DK, (h + 1) * RET_DK), slice(h * RET_DV, (h + 1) * RET_DV)
            dy, drg, dg = _gn_gate_bwd(y_ref[:, vs], rg_ref[:, vs].astype(F32), g_ref[:, vs], d_ref[:, vs].astype(F32))
            drg_ref[:, vs] = drg.astype(BF16)
            dgs.append(dg)
            qq, kk, vv, dyy = q_ref[:, ks], k_ref[:, ks], v_ref[:, vs], dy.astype(BF16)
            dmm = dm_ref[h]
            gb = gs_ref[h].astype(BF16)
            sb = s_ref[h, 0].astype(BF16)
            xi_c = jnp.concatenate([xi_ref[h], xi_ref[h]], axis=1)
            pb = (_dot(qq, kk, 1, 1) * dmm).astype(BF16)
            kz = (kk.astype(F32) * ze_ref[h]).astype(BF16)
            dv_ref[:, vs] = (_dot(pb, dyy, 0, 0) + _dot(kz, gb, 1, 0)).astype(BF16)
            da = (_dot(dyy, vv, 1, 1) * dmm).astype(BF16)
            dyx = (dyy.astype(F32) * xi_c).astype(BF16)
            dq = _dot(da, kk, 1, 0) + _dot(dyx, sb, 1, 1)
            dk = _dot(da, qq, 0, 0) + _dot(vv, gb, 1, 1) * ze_ref[h]
            dq_ref[:, ks] = _rope128_t(dq, c, sn).astype(BF16)
            dk_ref[:, ks] = (_rope128_t(dk, c, sn) * RET_SCALE).astype(BF16)
            gl2 = jnp.concatenate([gl_ref[h, 0:1, :], gl_ref[h, 0:1, :]], axis=1)
            gs_ref[h] = gs_ref[h] * gl2 + _dot(qq, dyx, 0, 0)
        _accum(dg_ref, jnp.concatenate(dgs, axis=1))

    rev = lambda n: n_sc - 1 - n
    return _pcall(
        body, grid=(n_sc,),
        in_specs=[pl.BlockSpec((L, 1024), lambda n: (rev(n), 0)), pl.BlockSpec((L, 1024), lambda n: (rev(n), 0)),
                  pl.BlockSpec((L, 2048), lambda n: (rev(n), P_RV // 2048)), pl.BlockSpec((L, 2048), lambda n: (rev(n), 0)),
                  pl.BlockSpec((L, 2048), lambda n: (rev(n), P_RG // 2048)), _whole(gn_g),
                  pl.BlockSpec((L, 2048), lambda n: (rev(n), 0)),
                  pl.BlockSpec((HEADS, 1, 128, 256), lambda n: (0, rev(n), 0, 0)),
                  _whole(dm), _whole(xi), _whole(zeta), _whole(gl),
                  pl.BlockSpec((L, 128), lambda n: (rev(n), 0)), pl.BlockSpec((L, 128), lambda n: (rev(n), 0))],
        out_specs=[pl.BlockSpec((L, 1024), lambda n: (rev(n), 0)), pl.BlockSpec((L, 1024), lambda n: (rev(n), 0)),
                   pl.BlockSpec((L, 2048), lambda n: (rev(n), 0)), pl.BlockSpec((L, 2048), lambda n: (rev(n), 0)),
                   pl.BlockSpec((1, 2048), lambda n: (0, 0))],
        out_shape=[jax.ShapeDtypeStruct((T, 1024), BF16), jax.ShapeDtypeStruct((T, 1024), BF16), jax.ShapeDtypeStruct((T, 2048), BF16),
                   jax.ShapeDtypeStruct((T, 2048), BF16), jax.ShapeDtypeStruct((1, 2048), F32)],
        scratch_shapes=[pltpu.VMEM((HEADS, 128, 256), F32)], name="ret_bwd",
        compiler_params=pltpu.CompilerParams(dimension_semantics=("arbitrary",)),
    )(q, k, v, y, proj, gn_g, dyr, states, dm, xi, zeta, gl, cos128, sin128)


def _diag_mask(nrows, ncols, row0):
    row = row0 + lax.broadcasted_iota(jnp.int32, (nrows, ncols), 0)
    col = lax.broadcasted_iota(jnp.int32, (nrows, ncols), 1)
    return lax.shift_right_logical(col, 6) <= lax.shift_right_logical(row, 6)


def _diag_spans(t):
    h = t // 2
    return ((0, h, h), (h, h, t)) if h % 128 == 0 else ((0, t, t),)


def _tri_steps(nb, by_key):
    if by_key:
        pairs = [(i, j) for j in range(nb) for i in range(j, nb)]
    else:
        pairs = [(i, j) for i in range(nb) for j in range(i + 1)]
    return jnp.array([a for a, _ in pairs], jnp.int32), jnp.array([b for _, b in pairs], jnp.int32)


def _attn_fwd(qn, qpe, kv, kpe, T):
    t = min(ATT_TF, T)
    nb = T // t
    ii, jj = _tri_steps(nb, by_key=False)

    hp = ATT_HP
    w = 128 * hp

    def body(ii_ref, jj_ref, qn_ref, qp_ref, kn_ref, kp_ref, v_ref, o_ref, lse_ref, m_sc, l_sc, acc_sc):
        st = pl.program_id(1)
        i, j = ii_ref[st], jj_ref[st]

        @pl.when(j == 0)
        def _():
            m_sc[...] = jnp.full_like(m_sc, NEG)
            l_sc[...] = jnp.zeros_like(l_sc)
            acc_sc[...] = jnp.zeros_like(acc_sc)

        def update(diag):
            kp = kp_ref[...]
            for hh in range(hp):
                sl = slice(128 * hh, 128 * (hh + 1))
                q = jnp.concatenate([qn_ref[:, sl], qp_ref[:, sl]], axis=1)
                k = jnp.concatenate([kn_ref[:, sl], kp], axis=1)
                s = _dot(q, k, 1, 1)
                if diag:
                    s = jnp.where(_diag_mask(t, t, 0), s, NEG)
                m_prev = m_sc[:, sl]
                m_new = jnp.maximum(m_prev, jnp.max(s, axis=1, keepdims=True))
                a = jnp.exp2(m_prev - m_new)
                p = jnp.exp2(s - m_new[:, 0:1])
                l_sc[:, sl] = a * l_sc[:, sl] + jnp.sum(p, axis=1, keepdims=True)
                acc_sc[:, sl] = a * acc_sc[:, sl] + _dot(p.astype(BF16), v_ref[:, sl], 1, 0)
                m_sc[:, sl] = m_new

        @pl.when(j < i)
        def _():
            update(False)

        @pl.when(j == i)
        def _():
            update(True)
            o_ref[...] = (acc_sc[...] / l_sc[...]).astype(BF16)
            lse_ref[...] = m_sc[...] + jnp.log2(l_sc[...])

    qs = pl.BlockSpec((t, w), lambda h, s, ii, jj: (ii[s], h))
    grid_spec = pltpu.PrefetchScalarGridSpec(
        num_scalar_prefetch=2, grid=(HEADS // hp, int(ii.shape[0])),
        in_specs=[qs, qs, pl.BlockSpec((t, w), lambda h, s, ii, jj: (jj[s], h)), pl.BlockSpec((t, 128), lambda h, s, ii, jj: (jj[s], 0)),
                  pl.BlockSpec((t, w), lambda h, s, ii, jj: (jj[s], HEADS // hp + h))],
        out_specs=[qs, qs],
        scratch_shapes=[pltpu.VMEM((t, w), F32), pltpu.VMEM((t, w), F32), pltpu.VMEM((t, w), F32)])
    return _pcall(
        body, grid_spec=grid_spec, out_shape=[jax.ShapeDtypeStruct((T, D), BF16), jax.ShapeDtypeStruct((T, D), F32)], name="attn_fwd",
        compiler_params=pltpu.CompilerParams(dimension_semantics=("arbitrary", "arbitrary")),
    )(ii, jj, qn, qpe, kv, kpe, kv)


def _attn_bwd(qn, qpe, kv, kpe, do, lse, delta, T):
    t = min(ATT_TB, T)
    nb = T // t
    ii, jj = _tri_steps(nb, by_key=True)

    def body(ii_ref, jj_ref, qn_ref, qp_ref, kn_ref, kp_ref, v_ref, do_ref, lse_ref, dl_ref,
             dqn_ref, dqp_ref, dkn_ref, dkp_ref, dv_ref, dk_sc, dv_sc):
        st = pl.program_id(1)
        i, j = ii_ref[st], jj_ref[st]

        @pl.when(st == 0)
        def _():
            dqn_ref[...] = jnp.zeros_like(dqn_ref)
            dqp_ref[...] = jnp.zeros_like(dqp_ref)

        @pl.when(i == j)
        def _():
            dk_sc[...] = jnp.zeros_like(dk_sc)
            dv_sc[...] = jnp.zeros_like(dv_sc)

        def update(diag):
            for row0, nr, nkeys in (_diag_spans(t) if diag else ((0, t, t),)):
                rows = slice(row0, row0 + nr)
                q = jnp.concatenate([qn_ref[rows, :], qp_ref[rows, :]], axis=1)
                k = jnp.concatenate([kn_ref[:nkeys, :], kp_ref[:nkeys, :]], axis=1)
                dob = do_ref[rows, :]
                s = _dot(q, k, 1, 1)
                if diag:
                    s = jnp.where(_diag_mask(nr, nkeys, row0), s, NEG)
                p = jnp.exp2(s - lse_ref[rows, 0:1])
                dv_sc[:nkeys, :] += _dot(p.astype(BF16), dob, 0, 0)
                dp = _dot(dob, v_ref[:nkeys, :], 1, 1)
                ds = (p * (dp - dl_ref[rows, 0:1])).astype(BF16)
                dk_sc[:nkeys, :] += _dot(ds, q, 0, 0)
                dq = _dot(ds, k, 1, 0) * MLA_SCALE
                grows = pl.ds(pl.multiple_of(i * t + row0, nr), nr)
                dqn_ref[grows, :] += dq[:, :128]
                dqp_ref[grows, :] += dq[:, 128:]

        @pl.when(i > j)
        def _():
            update(False)

        @pl.when(i == j)
        def _():
            update(True)

        @pl.when(i == nb - 1)
        def _():
            dkn_ref[...] = (dk_sc[:, :128] * (1.0 / LOG2E)).astype(BF16)
            dkp_ref[...] = dk_sc[:, 128:] * (1.0 / LOG2E)
            dv_ref[...] = dv_sc[...].astype(BF16)

    qs = pl.BlockSpec((t, 128), lambda h, s, ii, jj: (ii[s], h))
    ks = pl.BlockSpec((t, 128), lambda h, s, ii, jj: (jj[s], h))
    hs = pl.BlockSpec((T, 128), lambda h, s, ii, jj: (0, h))
    grid_spec = pltpu.PrefetchScalarGridSpec(
        num_scalar_prefetch=2, grid=(HEADS, int(ii.shape[0])),
        in_specs=[qs, qs, ks, pl.BlockSpec((t, 128), lambda h, s, ii, jj: (jj[s], 0)),
                  pl.BlockSpec((t, 128), lambda h, s, ii, jj: (jj[s], HEADS + h)), qs, qs, qs],
        out_specs=[hs, hs, ks, ks, ks],
        scratch_shapes=[pltpu.VMEM((t, 256), F32), pltpu.VMEM((t, 128), F32)])
    return _pcall(
        body, grid_spec=grid_spec,
        out_shape=[jax.ShapeDtypeStruct((T, D), F32), jax.ShapeDtypeStruct((T, D), F32), jax.ShapeDtypeStruct((T, D), BF16),
                   jax.ShapeDtypeStruct((T, D), F32), jax.ShapeDtypeStruct((T, D), BF16)], name="attn_bwd",
        compiler_params=pltpu.CompilerParams(dimension_semantics=("arbitrary", "arbitrary")),
    )(ii, jj, qn, qpe, kv, kpe, kv, do, lse, delta)


def _mesh_pos():
    x, y, c = lax.axis_index("x"), lax.axis_index("y"), lax.axis_index("c")
    return x, y, c, 4 * x + 2 * y + c


def _peer(x, y, c, k):
    px, py, pc = (x + ((k >> 2) & 1)) % 2, (y + ((k >> 1) & 1)) % 2, (c + (k & 1)) % 2
    return (px, py, pc), 4 * px + 2 * py + pc


_ANY = pl.BlockSpec(memory_space=pl.ANY)


def _rcopy(src, dst, send_sems, recv_sems, k, to):
    return pltpu.make_async_remote_copy(src_ref=src, dst_ref=dst, send_sem=send_sems.at[k], recv_sem=recv_sems.at[k],
                                        device_id=to, device_id_type=pl.DeviceIdType.MESH)


def _all_gather(wsh, ssh):
    def body(w_ref, s_ref, wall_ref, sall_ref, send_sems, recv_sems, loc_sems):
        x, y, c, me = _mesh_pos()
        sib = (x, y, 1 - c)
        chips = [(1 - x, y), (x, 1 - y), (1 - x, 1 - y)]
        slot = lambda px, py, pc: 4 * px + 2 * py + pc
        loc = [pltpu.make_async_copy(s_ref, sall_ref.at[me], loc_sems.at[0])]
        for cp in loc:
            cp.start()
        sends, fwd_waits = [], []
        for n, (src, dst) in enumerate(((w_ref, wall_ref), (s_ref, sall_ref))):
            o = 7 * n
            sends.append(_rcopy(src, dst.at[me], send_sems, recv_sems, o, sib))
            for j, chip in enumerate(chips):
                sends.append(_rcopy(src, dst.at[me], send_sems, recv_sems, o + 1 + j, (*chip, c)))
        for cp in sends:
            cp.start()
        for n, (src, dst) in enumerate(((w_ref, wall_ref), (s_ref, sall_ref))):
            o = 7 * n
            for j, chip in enumerate(chips):
                got = dst.at[slot(*chip, c)]
                _rcopy(src, got, send_sems, recv_sems, o + 1 + j, sib).wait_recv()
                fw = _rcopy(got, got, send_sems, recv_sems, o + 4 + j, sib)
                fw.start()
                sends.append(fw)
            fwd_waits.append(_rcopy(src, dst.at[slot(x, y, 1 - c)], send_sems, recv_sems, o, sib))
            for j, chip in enumerate(chips):
                fwd_waits.append(_rcopy(src, dst.at[slot(*chip, 1 - c)], send_sems, recv_sems, o + 4 + j, sib))
        for cp in fwd_waits:
            cp.wait_recv()
        for cp in sends:
            cp.wait_send()
        for cp in loc:
            cp.wait()

    wall, sall = _pcall(
        body, in_specs=[_ANY, _ANY], out_specs=[_ANY, _ANY],
        out_shape=[jax.ShapeDtypeStruct((N_DEV,) + wsh.shape, wsh.dtype), jax.ShapeDtypeStruct((N_DEV,) + ssh.shape, ssh.dtype)],
        scratch_shapes=[pltpu.SemaphoreType.DMA((14,)), pltpu.SemaphoreType.DMA((14,)), pltpu.SemaphoreType.DMA((1,))],
        name="all_gather_weights",
    )(wsh, ssh)
    me = 4 * lax.axis_index("x") + 2 * lax.axis_index("y") + lax.axis_index("c")
    return lax.dynamic_update_index_in_dim(wall, wsh, me, axis=0), sall


_HBM = pl.BlockSpec(memory_space=pltpu.HBM)
_SEM = pl.BlockSpec(memory_space=pltpu.SEMAPHORE)
_EFFECT = pltpu.SideEffectType.DATAFLOW_SIDE_EFFECTING


def _other_chips(x, y):
    return [(1 - x, y), (x, 1 - y), (1 - x, 1 - y)]


def _gather_start(wsh, order_after):
    def body(w_ref, land_ref, dep_ref, send_sems, recv_sems, w_thru, land_thru, token):
        x, y, c, me = _mesh_pos()
        _rcopy(w_ref, land_ref.at[me], send_sems, recv_sems, 0, (x, y, 1 - c)).start()
        for j, chip in enumerate(_other_chips(x, y)):
            _rcopy(w_ref, land_ref.at[me], send_sems, recv_sems, 1 + j, (*chip, c)).start()
        token[...] = jnp.zeros_like(token)

    shape = (N_DEV,) + wsh.shape
    land = pltpu.with_memory_space_constraint(lax.empty(shape, wsh.dtype), pltpu.HBM)
    return _pcall(
        body, name="gather_start",
        out_shape=(pltpu.SemaphoreType.DMA((4,)), pltpu.SemaphoreType.DMA((4,)), pltpu.HBM(wsh.shape, wsh.dtype),
                   pltpu.HBM(shape, wsh.dtype), jax.ShapeDtypeStruct((8, 128), F32)),
        in_specs=(_HBM, _HBM, _ANY), out_specs=(_SEM, _SEM, _HBM, _HBM, pl.BlockSpec(memory_space=pltpu.VMEM)),
        input_output_aliases={0: 2, 1: 3}, compiler_params=pltpu.CompilerParams(has_side_effects=_EFFECT),
    )(pltpu.with_memory_space_constraint(wsh, pltpu.HBM), land, order_after)


def _gather_wait(send_sems, recv_sems, w_thru, land_thru, after):
    def body(w_ref, land_ref, send_sems, recv_sems, after_ref, w_out, land_out):
        x, y, c, _ = _mesh_pos()
        senders = [(x, y, 1 - c)] + [(*chip, c) for chip in _other_chips(x, y)]
        for k, (px, py, pc) in enumerate(senders):
            cp = _rcopy(w_ref, land_ref.at[4 * px + 2 * py + pc], send_sems, recv_sems, k, (px, py, pc))
            cp.wait_send()
            cp.wait_recv()

    return _pcall(
        body, name="gather_wait", out_shape=(pltpu.HBM(w_thru.shape, w_thru.dtype), pltpu.HBM(land_thru.shape, land_thru.dtype)),
        in_specs=(_HBM, _HBM, _SEM, _SEM, _ANY), out_specs=(_HBM, _HBM), input_output_aliases={0: 0, 1: 1},
        compiler_params=pltpu.CompilerParams(has_side_effects=_EFFECT),
    )(w_thru, land_thru, send_sems, recv_sems, after)


def _gather_finish(wsh, land):
    def body(land_ref, out_ref, send_sems, recv_sems):
        x, y, c, _ = _mesh_pos()
        sib = (x, y, 1 - c)
        sends, recvs = [], []
        for j, (px, py) in enumerate(_other_chips(x, y)):
            mine, theirs = 4 * px + 2 * py + c, 4 * px + 2 * py + (1 - c)
            sends.append(_rcopy(land_ref.at[mine], out_ref.at[mine], send_sems, recv_sems, j, sib))
            recvs.append(_rcopy(land_ref.at[theirs], out_ref.at[theirs], send_sems, recv_sems, j, sib))
        for cp in sends:
            cp.start()
        for cp in recvs:
            cp.wait_recv()
        for cp in sends:
            cp.wait_send()

    done = _pcall(
        body, in_specs=[_ANY], out_specs=_ANY, out_shape=jax.ShapeDtypeStruct(land.shape, land.dtype),
        input_output_aliases={0: 0},
        scratch_shapes=[pltpu.SemaphoreType.DMA((3,)), pltpu.SemaphoreType.DMA((3,))], name="gather_finish",
    )(land)
    me = 4 * lax.axis_index("x") + 2 * lax.axis_index("y") + lax.axis_index("c")
    return lax.dynamic_update_index_in_dim(done, wsh, me, axis=0)


_ROW_TILES = (400, 368, 352, 256, 128, 16, 8)


def _exchange_sibling(g4, name):
    def body(g_ref, r_ref, send_sems, recv_sems):
        x, y, c, _ = _mesh_pos()
        sib = (x, y, 1 - c)
        cps = [_rcopy(g_ref.at[q, 1 - c], r_ref.at[q], send_sems, recv_sems, q, sib) for q in range(4)]
        for cp in cps:
            cp.start()
        for cp in cps:
            cp.wait()

    n, _, R, C = g4.shape
    return _pcall(
        body, in_specs=[_ANY], out_specs=_ANY, out_shape=jax.ShapeDtypeStruct((n, R, C), g4.dtype),
        scratch_shapes=[pltpu.SemaphoreType.DMA((4,)), pltpu.SemaphoreType.DMA((4,))], name=name,
    )(g4)


def _sum_sibling(g4, rsib, cvec, name):
    n, _, R, C = g4.shape
    tr = _pick(R, _ROW_TILES)

    def body(c_ref, g_ref, r_ref, o_ref):
        o_ref[...] = (g_ref[...].astype(F32) + r_ref[...].astype(F32)).astype(o_ref.dtype)

    grid_spec = pltpu.PrefetchScalarGridSpec(
        num_scalar_prefetch=1, grid=(n, R // tr),
        in_specs=[pl.BlockSpec((None, None, tr, C), lambda q, i, cr: (q, cr[0], i, 0)), pl.BlockSpec((None, tr, C), lambda q, i, cr: (q, i, 0))],
        out_specs=pl.BlockSpec((None, tr, C), lambda q, i, cr: (q, i, 0)))
    return _pcall(body, grid_spec=grid_spec, out_shape=jax.ShapeDtypeStruct((n, R, C), g4.dtype), name=name)(cvec, g4, rsib)


def _chips_start(part, name, order_after=None):
    n_in = 2 + int(order_after is not None)

    def body(*refs):
        p_ref, land_ref = refs[:2]
        send_sems, recv_sems, token = refs[n_in], refs[n_in + 1], refs[n_in + 4]
        x, y, c, _ = _mesh_pos()
        myq = 2 * x + y
        for j, (px, py) in enumerate(_other_chips(x, y)):
            _rcopy(p_ref.at[2 * px + py], land_ref.at[myq], send_sems, recv_sems, j, (px, py, c)).start()
        token[...] = jnp.zeros_like(token)

    land = pltpu.with_memory_space_constraint(lax.empty(part.shape, part.dtype), pltpu.HBM)
    extra = () if order_after is None else (order_after,)
    return _pcall(
        body, name=name,
        out_shape=(pltpu.SemaphoreType.DMA((3,)), pltpu.SemaphoreType.DMA((3,)), pltpu.HBM(part.shape, part.dtype),
                   pltpu.HBM(part.shape, part.dtype), jax.ShapeDtypeStruct((8, 128), F32)),
        in_specs=(_HBM, _HBM) + (_ANY,) * len(extra),
        out_specs=(_SEM, _SEM, _HBM, _HBM, pl.BlockSpec(memory_space=pltpu.VMEM)),
        input_output_aliases={0: 2, 1: 3}, compiler_params=pltpu.CompilerParams(has_side_effects=_EFFECT),
    )(pltpu.with_memory_space_constraint(part, pltpu.HBM), land, *extra)


def _chips_wait(send_sems, recv_sems, p_thru, land_thru, after, name):
    def body(p_ref, land_ref, send_sems, recv_sems, after_ref, p_out, land_out):
        x, y, c, _ = _mesh_pos()
        for j, (px, py) in enumerate(_other_chips(x, y)):
            q = 2 * px + py
            cp = _rcopy(p_ref.at[q], land_ref.at[q], send_sems, recv_sems, j, (px, py, c))
            cp.wait_send()
            cp.wait_recv()

    return _pcall(
        body, name=name, out_shape=(pltpu.HBM(p_thru.shape, p_thru.dtype), pltpu.HBM(p_thru.shape, p_thru.dtype)),
        in_specs=(_HBM, _HBM, _SEM, _SEM, _ANY), out_specs=(_HBM, _HBM), input_output_aliases={0: 0, 1: 1},
        compiler_params=pltpu.CompilerParams(has_side_effects=_EFFECT),
    )(p_thru, land_thru, send_sems, recv_sems, after)


def _sum_chips(part, land, qvec, name):
    n, R, C = part.shape
    tr = _pick(R, _ROW_TILES)

    def body(q_ref, p_ref, *refs):
        o_ref = refs[n]
        acc = None
        for q in range(n):
            term = jnp.where(q_ref[0] == q, p_ref[...], refs[q][...]).astype(F32)
            acc = term if acc is None else acc + term
        o_ref[...] = acc

    def land_spec(q):
        return pl.BlockSpec((None, tr, C), lambda i, m: (jnp.where(m[0] == q, (q + 1) % n, q), i, 0))

    grid_spec = pltpu.PrefetchScalarGridSpec(
        num_scalar_prefetch=1, grid=(R // tr,),
        in_specs=[pl.BlockSpec((None, tr, C), lambda i, m: (m[0], i, 0))] + [land_spec(q) for q in range(n)],
        out_specs=pl.BlockSpec((tr, C), lambda i, m: (i, 0)))
    return _pcall(body, grid_spec=grid_spec, out_shape=jax.ShapeDtypeStruct((R, C), F32), name=name)(qvec, part, *([land] * n))


def _exchange_small(gsmall):
    def body(s_ref, srecv_ref, send_sems, recv_sems, loc_sem):
        x, y, c, me = _mesh_pos()
        loc = pltpu.make_async_copy(s_ref, srecv_ref.at[me], loc_sem.at[0])
        loc.start()
        sends, recvs = [], []
        for k in range(1, N_DEV):
            to, pidx = _peer(x, y, c, k)
            sends.append(_rcopy(s_ref, srecv_ref.at[me], send_sems, recv_sems, k - 1, to))
            recvs.append(_rcopy(s_ref, srecv_ref.at[pidx], send_sems, recv_sems, k - 1, to))
        for cp in sends:
            cp.start()
        for cp in recvs:
            cp.wait_recv()
        for cp in sends:
            cp.wait_send()
        loc.wait()

    return _pcall(
        body, in_specs=[_ANY], out_specs=_ANY, out_shape=jax.ShapeDtypeStruct((N_DEV,) + gsmall.shape, gsmall.dtype),
        scratch_shapes=[pltpu.SemaphoreType.DMA((7,)), pltpu.SemaphoreType.DMA((7,)), pltpu.SemaphoreType.DMA((1,))],
        name="exchange_small",
    )(gsmall)


def _sum_slots(recv, name):
    n, R, C = recv.shape
    tr = _pick(R, _ROW_TILES)

    def body(r_ref, o_ref):
        acc = r_ref[0].astype(F32)
        for s in range(1, n):
            acc = acc + r_ref[s].astype(F32)
        o_ref[...] = acc

    return _pcall(body, grid=(R // tr,), in_specs=[pl.BlockSpec((n, tr, C), lambda i: (0, i, 0))],
                  out_specs=pl.BlockSpec((tr, C), lambda i: (i, 0)), out_shape=jax.ShapeDtypeStruct((R, C), F32), name=name)(recv)


def _adamw(w, g, m, v, name, order_after=None):
    shape = w.shape
    w2, g2, m2, v2 = (a.reshape(-1, shape[-1]) for a in (w, g, m, v))
    R, C = w2.shape
    tr = _pick(R, (256, 128, 64, 32, 16, 8)) if R > 256 else R
    n_in = 4 + int(order_after is not None)

    def body(*refs):
        w_ref, g_ref, m_ref, v_ref = refs[:4]
        d_ref, nm_ref, nv_ref = refs[n_in:]
        gg = g_ref[...]
        nm = ADAM_B1 * m_ref[...] + (1.0 - ADAM_B1) * gg
        nv = ADAM_B2 * v_ref[...] + (1.0 - ADAM_B2) * jnp.square(gg)
        m_hat = nm / (1.0 - ADAM_B1 ** ADAM_STEP)
        v_hat = nv / (1.0 - ADAM_B2 ** ADAM_STEP)
        d_ref[...] = -ADAM_LR * (m_hat / (jnp.sqrt(v_hat) + ADAM_EPS) + ADAM_WD * w_ref[...])
        nm_ref[...] = nm
        nv_ref[...] = nv

    spec = pl.BlockSpec((tr, C), lambda i: (i, 0))
    in_specs, args = [spec] * 4, [w2, g2, m2, v2]
    if order_after is not None:
        in_specs.append(pl.BlockSpec(order_after.shape, lambda i: (0, 0)))
        args.append(order_after)
    d, nm, nv = _pcall(body, grid=(R // tr,), in_specs=in_specs, out_specs=[spec] * 3,
                       out_shape=[jax.ShapeDtypeStruct((R, C), F32)] * 3, name=name)(*args)
    return d.reshape(shape), nm.reshape(shape), nv.reshape(shape)


def _to_rows(name, w):
    w = w[0]
    if name in ("ffn1_w_in", "ffn2_w_in", "w_in"):
        return w.T
    if name in ("w_uq", "w_ukv", "ple_w_proj"):
        return w.T.reshape(-1, 1024)
    return w


def _from_rows(name, g, shape):
    if name in ("ffn1_w_in", "ffn2_w_in", "w_in"):
        return g.T.reshape(shape)
    if name in ("w_uq", "w_ukv", "ple_w_proj"):
        return g.reshape(-1, shape[1]).T.reshape(shape)
    return g.reshape(shape)


def _unpack(wall, group):
    out, off = {}, 0
    for name, r in group:
        out[name] = wall[:, off:off + r, :].reshape(N_DEV * r, 1024)
        off += _pad16(r)
    return out


def _w_in_internal(wt):
    return jnp.concatenate([wt[0:6144], wt[6720:8768], wt[6144:6720], jnp.zeros((P_W - W_IN_COLS, 1024), wt.dtype)], axis=0)


def _w_in_external(d):
    return jnp.concatenate([d[0:6144], d[8192:8768], d[6144:8192]], axis=0)


def _rope_tables(positions):
    pos = positions[0].astype(F32)

    def cs(half):
        inv = ROPE_BASE ** (-jnp.arange(half, dtype=F32) / half)
        ang = pos[:, None] * inv
        return jnp.cos(ang), jnp.sin(ang)

    c, s = cs(64)
    c2, s2 = cs(32)
    return (jnp.concatenate([c, c], axis=1), jnp.concatenate([-s, s], axis=1),
            jnp.concatenate([c2, c2, c2, c2], axis=1), jnp.concatenate([-s2, s2, -s2, s2], axis=1))


def _local_step(x, p, positions, target, W, ln_g, ln_b, gn_g, qg, kvg, early=None, rest_weights=None, start_token=None):
    T = x.shape[0]
    tabs = _rope_tables(positions)
    rc = _ret_consts()
    lg = [ln_g[i:i + 1] for i in range(4)]
    lb = [ln_b[i:i + 1] for i in range(4)]
    pb = p.astype(BF16)

    hs1, gu1, *xb = _ffn_in(x, W["ffn1_w_in"], "ffn1_in", order_after=start_token)
    xb = xb[0] if xb else x
    f1, h1, h1b = _mm(hs1, W["ffn1_w_out"], name="ffn1_out", tm=LN_TM, epilogue=_ln_epilogue(x, 0.5, lg[0], lb[0]))
    if rest_weights is not None:
        W = {**W, **rest_weights(h1b)}
    w_in_t = _w_in_internal(W["w_in"])
    wuq = W["w_uq"].reshape(1536, LORA).reshape(HEADS, 192, LORA)
    wuq = jnp.concatenate([wuq[:, :128].reshape(1024, LORA), wuq[:, 128:].reshape(512, LORA)], axis=0)
    wukv = W["w_ukv"].reshape(2048, LORA).reshape(HEADS, 2, 128, LORA).transpose(1, 0, 2, 3).reshape(2048, LORA)
    wp_t = W["ple_w_proj"].reshape(1024, D_PLE)
    proj = _mm(h1b, w_in_t, tb=True, out_dtype=BF16, name="mixer_in")
    rq, rk, qn, kvn, kpe = _mixer_prep_fwd(proj, tabs, qg, kvg, T)
    y, yr, states = _ret_fwd(rq, rk, proj, proj, gn_g, rc, T)
    y_ret = _mm(yr, W["w_ret_o"], out_dtype=BF16, name="ret_o")
    qnope, qpe = _mm(qn, wuq, tb=True, name="mla_uq", tm=LN_TM, tn=1536, epilogue=_q_assemble_epilogue(tabs))
    kv = _mm(kvn, wukv, tb=True, out_dtype=BF16, name="mla_ukv")
    o, lse = _attn_fwd(qnope, qpe, kv, kpe, T)
    y_mla, mix = _mm(o, W["w_mla_o"], name="mla_o", tm=LN_TM, epilogue=_mix_epilogue(proj, y_ret))
    mixed, h2, h2b = _mm(mix, W["w_out"], name="mixer_out", tm=LN_TM, epilogue=_ln_epilogue(h1, 1.0, lg[1], lb[1]))
    hs2, gu2 = _ffn_in(h2b, W["ffn2_w_in"], "ffn2_in")
    f2, h3, h3b = _mm(hs2, W["ffn2_w_out"], name="ffn2_out", tm=LN_TM, epilogue=_ln_epilogue(h2, 0.5, lg[2], lb[2]))
    pp = _mm(pb, wp_t, tb=True, name="ple_proj")

    G = {}
    dh3_a, dgl, dpp, dg3, db3, loss = _mm(h3b, W["ple_w_gate"], name="ple_gate", tm=LN_TM,
                                          epilogue=_head_epilogue(h3, pp, target, lg[3], lb[3]))
    G["ple_w_gate"] = _mm(h3b, dgl, ta=True, name="d_ple_gate")
    G["ple_w_proj"] = _mm(dpp, pb, ta=True, name="d_ple_proj")
    dh2_a, df2, dg2, db2 = _mm(dgl, W["ple_w_gate"], tb=True, add=dh3_a, name="dh3", tm=LN_TM,
                               epilogue=_ln_bwd_epilogue(h2, f2, 0.5, lg[2], lb[2]))
    G["ffn2_w_out"] = _mm(hs2, df2, ta=True, name="d_ffn2_out")
    da2 = _ffn_act_bwd(df2, W["ffn2_w_out"], gu2, "ffn2_act_bwd")
    G["ffn2_w_in"] = _mm(da2, h2b, ta=True, name="d_ffn2_in")
    dh1_a, dmixed, dg1, db1 = _mm(da2, W["ffn2_w_in"], add=dh2_a, name="dh2", tm=LN_TM,
                                  epilogue=_ln_bwd_epilogue(h1, mixed, 1.0, lg[1], lb[1]))
    G["w_out"] = _mm(mix, dmixed, ta=True, name="d_mixer_out")
    dgr, dgm, dy_ret, dy_mla = _mm(dmixed, W["w_out"], tb=True, name="dmix", tm=LN_TM,
                                   epilogue=_mix_bwd_epilogue(proj, y_ret, y_mla))
    G["w_mla_o"] = _mm(o, dy_mla, ta=True, name="d_mla_o")
    dob, delta = _mm(dy_mla, W["w_mla_o"], tb=True, name="do", tm=LN_TM, epilogue=_delta_epilogue(o))
    dqn_f, dqpe_f, dkn, dkpe_all, dv = _attn_bwd(qnope, qpe, kv, kpe, dob, lse, delta, T)
    dq_n, dq_r = _q_assemble_bwd(dqn_f, dqpe_f, tabs, T)
    g_uq = jnp.concatenate([_mm(dq_n, qn, ta=True, name="d_uq_nope"), _mm(dq_r, qn, ta=True, name="d_uq_rope")], axis=0)
    g_uq = jnp.concatenate([g_uq[:1024].reshape(HEADS, 128, LORA), g_uq[1024:].reshape(HEADS, 64, LORA)], axis=1)
    G["w_uq"] = g_uq.reshape(1536 * LORA // 1024, 1024)
    dqn = _mm(dq_r, wuq[1024:], add=_mm(dq_n, wuq[:1024], name="dqn_a"), name="dqn_b")
    g_ukv = jnp.stack([_mm(dkn, kvn, ta=True, name="d_ukv_k"), _mm(dv, kvn, ta=True, name="d_ukv_v")], axis=0)
    G["w_ukv"] = g_ukv.reshape(2, HEADS, 128, LORA).transpose(1, 0, 2, 3).reshape(2048 * LORA // 1024, 1024)
    dkvn = _mm(dv, wukv[1024:], add=_mm(dkn, wukv[:1024], name="dkvn_a"), name="dkvn_b")
    dcq, dckv, dkpe, dqg, dkvg = _rms_bwd(proj, dqn, dkvn, dkpe_all, tabs, qg, kvg, T)
    G["w_ret_o"] = _mm(yr, dy_ret, ta=True, name="d_ret_o")
    dyr = _mm(dy_ret, W["w_ret_o"], tb=True, out_dtype=BF16, name="dyr")
    drq, drk, drv, drg, dgn = _ret_bwd(rq, rk, proj, y, proj, gn_g, states, dyr, rc, tabs, T)
    dproj = jnp.concatenate([drq, drk, drv, drg, dgr, dgm, dcq, dckv, dkpe, jnp.zeros((T, P_W - P_KPE - 128), BF16)], axis=1)
    G["w_in"] = _w_in_external(_mm(dproj, h1b, ta=True, name="d_mixer_in"))
    lg0 = lg[0] if early is None else lg[0] + early(G)[0:1, 0:1]
    dx_a, df1, dg0, db0 = _mm(dproj, w_in_t, add=dh1_a, name="dh1", tm=LN_TM,
                              epilogue=_ln_bwd_epilogue(x, f1, 0.5, lg0, lb[0]))
    G["ffn1_w_out"] = _mm(hs1, df1, ta=True, name="d_ffn1_out")
    da1 = _ffn_act_bwd(df1, W["ffn1_w_out"], gu1, "ffn1_act_bwd")
    G["ffn1_w_in"] = _mm(da1, xb, ta=True, name="d_ffn1_in")
    grad_x = _mm(da1, W["ffn1_w_in"], add=dx_a, name="grad_x")

    small = dict(ln_g=jnp.concatenate([dg0, dg1, dg2, dg3], axis=0), ln_b=jnp.concatenate([db0, db1, db2, db3], axis=0),
                 ret_gn_g=dgn, q_norm_g=dqg, kv_norm_g=dkvg)
    return loss, grad_x, G, small


def kernel(x, p, positions, ln_g, ln_b, ffn1_w_in, ffn1_w_out, w_in, ret_gn_g, w_ret_o, q_norm_g, kv_norm_g, w_uq, w_ukv, w_mla_o, w_out, ffn2_w_in, ffn2_w_out, ple_w_gate, ple_w_proj, loss_target, m_ln_g, m_ln_b, m_ffn1_w_in, m_ffn1_w_out, m_w_in, m_ret_gn_g, m_w_ret_o, m_q_norm_g, m_kv_norm_g, m_w_uq, m_w_ukv, m_w_mla_o, m_w_out, m_ffn2_w_in, m_ffn2_w_out, m_ple_w_gate, m_ple_w_proj, v_ln_g, v_ln_b, v_ffn1_w_in, v_ffn1_w_out, v_w_in, v_ret_gn_g, v_w_ret_o, v_q_norm_g, v_kv_norm_g, v_w_uq, v_w_ukv, v_w_mla_o, v_w_out, v_ffn2_w_in, v_ffn2_w_out, v_ple_w_gate, v_ple_w_proj):
    names = ("ln_g", "ln_b", "ffn1_w_in", "ffn1_w_out", "w_in", "ret_gn_g", "w_ret_o", "q_norm_g", "kv_norm_g", "w_uq", "w_ukv",
             "w_mla_o", "w_out", "ffn2_w_in", "ffn2_w_out", "ple_w_gate", "ple_w_proj")
    ws = dict(zip(names, (ln_g, ln_b, ffn1_w_in, ffn1_w_out, w_in, ret_gn_g, w_ret_o, q_norm_g, kv_norm_g, w_uq, w_ukv, w_mla_o,
                          w_out, ffn2_w_in, ffn2_w_out, ple_w_gate, ple_w_proj)))
    ms = dict(zip(names, (m_ln_g, m_ln_b, m_ffn1_w_in, m_ffn1_w_out, m_w_in, m_ret_gn_g, m_w_ret_o, m_q_norm_g, m_kv_norm_g, m_w_uq,
                          m_w_ukv, m_w_mla_o, m_w_out, m_ffn2_w_in, m_ffn2_w_out, m_ple_w_gate, m_ple_w_proj)))
    vs = dict(zip(names, (v_ln_g, v_ln_b, v_ffn1_w_in, v_ffn1_w_out, v_w_in, v_ret_gn_g, v_w_ret_o, v_q_norm_g, v_kv_norm_g, v_w_uq,
                          v_w_ukv, v_w_mla_o, v_w_out, v_ffn2_w_in, v_ffn2_w_out, v_ple_w_gate, v_ple_w_proj)))

    parts = []
    for name, r in PACK:
        rows = _to_rows(name, ws[name])
        if _pad16(r) != r:
            rows = jnp.concatenate([rows, jnp.zeros((_pad16(r) - r, 1024), F32)], axis=0)
        parts.append(rows)
    wsh_first = jnp.concatenate(parts[:len(PACK_LATE)], axis=0).astype(BF16)
    wsh_rest = jnp.concatenate(parts[len(PACK_LATE):], axis=0).astype(BF16)
    ssh = jnp.concatenate([ln_g[0], ln_b[0]], axis=0)
    wall_first, sall = _all_gather(wsh_first, ssh)
    *gather_handles, start_token = _gather_start(wsh_rest, wall_first)
    W = _unpack(wall_first, PACK_LATE)
    ln_full = sall.reshape(N_DEV, 2, 4, 128).transpose(1, 2, 0, 3).reshape(2, 4, 1024)

    def rest_weights(after):
        w_thru, land = _gather_wait(*gather_handles, after)
        return _unpack(_gather_finish(w_thru, land), PACK_EARLY)

    cvec = lax.axis_index("c").astype(jnp.int32).reshape(1)
    qvec = (2 * lax.axis_index("x") + lax.axis_index("y")).astype(jnp.int32).reshape(1)

    def chip_partials(G, group, tag):
        gparts = []
        for name, r in group:
            g = G[name].reshape(N_DEV, r, 1024)
            if _pad16(r) != r:
                g = jnp.concatenate([g, jnp.zeros((N_DEV, _pad16(r) - r, 1024), g.dtype)], axis=1)
            gparts.append(g)
        gfull = jnp.concatenate(gparts, axis=1).astype(BF16)
        g4 = gfull.reshape(4, 2, gfull.shape[1], 1024)
        return _sum_sibling(g4, _exchange_sibling(g4, "exchange_sibling_" + tag), cvec, "sum_sibling_" + tag)

    in_flight = []

    def early(G):
        *handles, token = _chips_start(chip_partials(G, PACK_EARLY, "early"), "chips_start_early")
        in_flight.append(handles)
        return token

    loss_p, grad_x, G, small = _local_step(x[0], p[0, 0], positions, loss_target[0], W, ln_full[0], ln_full[1],
                                           ret_gn_g, q_norm_g, kv_norm_g, early=early, rest_weights=rest_weights,
                                           start_token=start_token)

    part_e, land_e = _chips_wait(*in_flight[0], grad_x, "chips_wait_early")
    gsh_early = _sum_chips(part_e, land_e, qvec, "sum_grads_early")
    pad256 = lambda a: jnp.concatenate([a, jnp.zeros((1, 1024 - a.shape[1]), F32)], axis=1)
    gsmall = jnp.concatenate([small["ln_g"], small["ln_b"], small["ret_gn_g"].reshape(2, 1024), pad256(small["q_norm_g"]),
                              pad256(small["kv_norm_g"]), jnp.zeros((SMALL_ROWS - 12, 1024), F32)], axis=0)
    srecv = _exchange_small(gsmall)
    part_l = chip_partials(G, PACK_LATE, "late")
    *late_handles, late_token = _chips_start(part_l, "chips_start_late", order_after=srecv)
    ssum = _sum_slots(srecv, "sum_small_grads")

    def unpack_grads(group, gsh):
        out, off = {}, 0
        for name, r in group:
            out[name] = _from_rows(name, gsh[off:off + r], ws[name].shape)
            off += _pad16(r)
        return out

    grads = unpack_grads(PACK_EARLY, gsh_early)
    me = 4 * lax.axis_index("x") + 2 * lax.axis_index("y") + lax.axis_index("c")
    grads["ln_g"] = lax.dynamic_slice(ssum[0:4], (0, me * 128), (4, 128)).reshape(1, 4, 128)
    grads["ln_b"] = lax.dynamic_slice(ssum[4:8], (0, me * 128), (4, 128)).reshape(1, 4, 128)
    grads["ret_gn_g"] = ssum[8:10].reshape(1, 2048)
    grads["q_norm_g"] = ssum[10:11, :256]
    grads["kv_norm_g"] = ssum[11:12, :256]

    delta, new_m, new_v = {}, {}, {}
    late_names = [n for n, _ in PACK_LATE]
    for name in names:
        if name not in late_names:
            delta[name], new_m[name], new_v[name] = _adamw(ws[name], grads[name], ms[name], vs[name], "adamw_" + name,
                                                           order_after=late_token)
    part_l, land_l = _chips_wait(*late_handles, new_v["w_in"], "chips_wait_late")
    grads.update(unpack_grads(PACK_LATE, _sum_chips(part_l, land_l, qvec, "sum_grads_late")))
    for name in late_names:
        delta[name], new_m[name], new_v[name] = _adamw(ws[name], grads[name], ms[name], vs[name], "adamw_" + name)

    loss = lax.psum(loss_p[0, 0], ("x", "y", "c"))
    return (loss, grad_x[None], *[grads[n] for n in names], *[delta[n] for n in names],
            *[new_m[n] for n in names], *[new_v[n] for n in names])
```

```python
import math

import jax
import jax.numpy as jnp
from jax import lax
from jax.experimental import pallas as pl
from jax.experimental.pallas import tpu as pltpu

F32 = jnp.float32
BF16 = jnp.bfloat16

N_DEV = 8
D = 1024
D_FF = 2816
D_PLE = 256
CHUNK = 64
HEADS = 8
RET_DK = 128
RET_DV = 256
MLA_NOPE = 128
MLA_ROPE = 64
MLA_DV = 128
LORA = 256
ROPE_BASE = 10000.0
EPS = 1e-5
ALPHA = 2.0 ** 0.25
RET_SCALE = RET_DK ** -0.5
MLA_SCALE = (MLA_NOPE + MLA_ROPE) ** -0.5
NEG = -1e30

ADAM_LR = 0.001
ADAM_B1 = 0.9
ADAM_B2 = 0.999
ADAM_EPS = 1e-08
ADAM_WD = 0.01
ADAM_STEP = 10

P_RQ, P_RK, P_RV, P_RG, P_GR, P_GM, P_CQ, P_CKV, P_KPE, P_W = 0, 1024, 2048, 4096, 6144, 7168, 8192, 8448, 8704, 8960
W_IN_COLS = 8768
RET_L = 256
ATT_TF = 1024
ATT_TB = 1024
ATT_HP = 4
LOG2E = math.log2(math.e)
Q_PRESCALE = MLA_SCALE * LOG2E

PACK = (("ffn1_w_in", 704), ("ffn1_w_out", 352), ("w_in", 1096), ("w_ret_o", 256), ("w_uq", 48), ("w_ukv", 64),
        ("w_mla_o", 128), ("w_out", 128), ("ffn2_w_in", 704), ("ffn2_w_out", 352), ("ple_w_gate", 128), ("ple_w_proj", 32))


def _pad16(r):
    return -(-r // 16) * 16


PACK_LATE = PACK[:2]
PACK_EARLY = PACK[2:]
SMALL_ROWS = 16


def _pcall(body, **kw):
    return pl.pallas_call(body, **kw)


def _pick(dim, prefs):
    for p in prefs:
        if dim % p == 0:
            return p
    return dim


def _sigmoid(x):
    return 1.0 / (1.0 + jnp.exp(-x))


def _silu(x):
    return x * _sigmoid(x)


def _ln(r, g, b):
    mu = jnp.mean(r, axis=-1, keepdims=True)
    var = jnp.mean(jnp.square(r - mu), axis=-1, keepdims=True)
    return (r - mu) * lax.rsqrt(var + EPS) * g + b


def _rms(x, g):
    return x * lax.rsqrt(jnp.mean(jnp.square(x), axis=-1, keepdims=True) + EPS) * g


def _dot(a, b, ca, cb):
    return lax.dot_general(a, b, (((ca,), (cb,)), ((), ())), preferred_element_type=F32)


def _accum(ref, val, first=None):
    @pl.when(pl.program_id(0) == 0 if first is None else first)
    def _():
        ref[...] = jnp.zeros_like(ref)

    ref[...] += val


def _mm(a, b, *, ta=False, tb=False, add=None, out_dtype=None, name, tm=None, tn=None, tk=None, epilogue=None):
    parts = a.shape[0] if a.ndim == 3 else 1
    ar, ac = a.shape[-2], a.shape[-1]
    out_dtype = out_dtype or (BF16 if ta else F32)
    if ta:
        K, M = ar, ac * parts
    else:
        M, K = ar, ac * parts
    if tb:
        N, K2 = b.shape
    else:
        K2, N = b.shape
    assert K == K2, (a.shape, b.shape, ta, tb)
    big = (1024, 1408, 1280, 768, 512, 256, 128)
    tm = tm or _pick(ac if (ta and parts > 1) else M, big)
    tn = tn or (N if N <= 1024 else _pick(N, big))
    kdim = ac if (not ta and parts > 1) else K
    tk = tk or (kdim if kdim <= 2816 and parts == 1 else
                _pick(kdim, (2048, 1408, 1280, 1024, 512) if tm <= 1024 else (1024, 1408, 1280, 512)))
    nk = K // tk
    grid = (M // tm, N // tn, nk)
    if parts > 1 and ta:
        per = ac // tm
        a_spec = pl.BlockSpec((None, tk, tm), lambda i, j, k: (i // per, k, i % per))
    elif parts > 1:
        per = ac // tk
        a_spec = pl.BlockSpec((None, tm, tk), lambda i, j, k: (k // per, i, k % per))
    else:
        a_spec = pl.BlockSpec((tk, tm), lambda i, j, k: (k, i)) if ta else pl.BlockSpec((tm, tk), lambda i, j, k: (i, k))
    b_spec = pl.BlockSpec((tn, tk), lambda i, j, k: (j, k)) if tb else pl.BlockSpec((tk, tn), lambda i, j, k: (k, j))
    o_spec = pl.BlockSpec((tm, tn), lambda i, j, k: (i, j))
    ca, cb = (0 if ta else 1), (1 if tb else 0)
    has_add = add is not None
    n_in = 2 + int(has_add)
    if epilogue is not None:
        assert tn == N and not ta
        ep_fn, ep_rows, ep_whole, ep_outs, ep_accs = epilogue
        n_ep_in = len(ep_rows) + len(ep_whole)
        n_out = len(ep_outs) + len(ep_accs)
    else:
        n_ep_in, n_out = 0, 1

    def body(*refs):
        a_ref, b_ref = refs[0], refs[1]
        add_ref = refs[2] if has_add else None
        o_ref = refs[n_in + n_ep_in]
        first_row_tile = pl.program_id(0) == 0

        def finish(r):
            if has_add:
                r = r + add_ref[...].astype(F32)
            if epilogue is not None:
                ep_fn(r, refs[n_in:n_in + n_ep_in], refs[n_in + n_ep_in:n_in + n_ep_in + n_out], first_row_tile)
            else:
                o_ref[...] = r.astype(out_dtype)

        if nk == 1:
            finish(_dot(a_ref[...], b_ref[...], ca, cb))
            return
        acc_ref = refs[-1]
        k = pl.program_id(2)

        @pl.when(k == 0)
        def _():
            acc_ref[...] = jnp.zeros_like(acc_ref)

        acc_ref[...] += _dot(a_ref[...], b_ref[...], ca, cb)

        @pl.when(k == nk - 1)
        def _():
            finish(acc_ref[...])

    in_specs = [a_spec, b_spec] + ([o_spec] if has_add else [])
    args = (a, b) + ((add,) if has_add else ())
    out_specs, out_shape = o_spec, jax.ShapeDtypeStruct((M, N), out_dtype)
    if epilogue is not None:
        for row_in in ep_rows:
            arr, col_block, width = (tuple(row_in) + (N,))[:3] if isinstance(row_in, tuple) else (row_in, 0, N)
            in_specs.append(pl.BlockSpec((tm, width), lambda i, j, k, _c=col_block: (i, _c)))
            args += (arr,)
        in_specs += [pl.BlockSpec(w.shape, lambda i, j, k, _n=w.ndim: (0,) * _n) for w in ep_whole]
        args += tuple(ep_whole)
        outs = [o if isinstance(o, tuple) else (o, N) for o in ep_outs]
        out_specs = ([pl.BlockSpec((tm, w), lambda i, j, k: (i, 0)) for _, w in outs]
                     + [pl.BlockSpec((r, N), lambda i, j, k: (0, 0)) for r in ep_accs])
        out_shape = [jax.ShapeDtypeStruct((M, w), dt) for dt, w in outs] + [jax.ShapeDtypeStruct((r, N), F32) for r in ep_accs]
    return _pcall(
        body, grid=grid, in_specs=in_specs, out_specs=out_specs, out_shape=out_shape,
        scratch_shapes=[pltpu.VMEM((tm, tn), F32)] if nk > 1 else [], name=name,
        compiler_params=pltpu.CompilerParams(dimension_semantics=("arbitrary" if epilogue is not None else "parallel", "parallel", "arbitrary")),
    )(*args)


def _ln_epilogue(res, c, g, b):
    def fn(r, ins, outs, first):
        res_ref, g_ref, b_ref = ins
        f_ref, h_ref, hb_ref = outs
        fb = r.astype(BF16)
        h = _ln(ALPHA * res_ref[...] + c * fb.astype(F32), g_ref[...], b_ref[...])
        f_ref[...] = fb
        h_ref[...] = h
        hb_ref[...] = h.astype(BF16)

    return (fn, [res], [g, b], (BF16, F32, BF16), ())


def _ln_bwd_epilogue(res, f, c, g, b):
    def fn(r, ins, outs, first):
        res_ref, f_ref, g_ref, b_ref = ins
        dr_ref, df_ref, dg_ref, db_ref = outs
        pre = ALPHA * res_ref[...] + c * f_ref[...].astype(F32)
        xc = pre - jnp.mean(pre, axis=-1, keepdims=True)
        rstd = lax.rsqrt(jnp.mean(jnp.square(xc), axis=-1, keepdims=True) + EPS)
        xhat = xc * rstd
        dyg = r * g_ref[...]
        dpre = rstd * (dyg - jnp.mean(dyg, axis=-1, keepdims=True) - xhat * jnp.mean(dyg * xhat, axis=-1, keepdims=True))
        dr_ref[...] = ALPHA * dpre
        df_ref[...] = (c * dpre).astype(BF16)
        _accum(dg_ref, jnp.sum(r * xhat, axis=0, keepdims=True), first)
        _accum(db_ref, jnp.sum(r, axis=0, keepdims=True), first)

    return (fn, [res, f], [g, b], (F32, BF16), (1, 1))


def _rows(body, T, tm, ins, outs, name, accs=()):
    in_specs, args = [], []
    for arr, w, cb in ins:
        if w is None:
            in_specs.append(pl.BlockSpec(arr.shape, lambda i, _n=arr.ndim: (0,) * _n))
        else:
            in_specs.append(pl.BlockSpec((tm, w), lambda i, _cb=cb: (i, _cb)))
        args.append(arr)
    out_specs = [pl.BlockSpec((tm, w), lambda i: (i, 0)) for w, _ in outs]
    out_shape = [jax.ShapeDtypeStruct((T, w), dt) for w, dt in outs]
    for r, w in accs:
        out_specs.append(pl.BlockSpec((r, w), lambda i: (0, 0)))
        out_shape.append(jax.ShapeDtypeStruct((r, w), F32))
    return _pcall(
        body, grid=(T // tm,), in_specs=in_specs, out_specs=out_specs, out_shape=out_shape, name=name,
        compiler_params=pltpu.CompilerParams(dimension_semantics=("arbitrary",)),
    )(*args)


FFN_TN = 1408
FFN_TM = 512
LN_TM = 512


def _ffn_in(x, wt, name, order_after=None):
    T = x.shape[0]
    tm, tn = min(FFN_TM, T), FFN_TN
    nj = D_FF // tn
    emit_xb = x.dtype != BF16
    n_in = 3 + int(order_after is not None)

    def body(*refs):
        x_ref, wg_ref, wu_ref = refs[:3]
        hs_ref, gu_ref = refs[n_in], refs[n_in + 1]
        xv = x_ref[...].astype(BF16)
        g = _dot(xv, wg_ref[...], 1, 1)
        u = _dot(xv, wu_ref[...], 1, 1)
        hs_ref[...] = (_silu(g) * u).astype(BF16)
        gu_ref[0] = g.astype(BF16)
        gu_ref[1] = u.astype(BF16)
        if emit_xb:
            refs[n_in + 2][...] = xv

    in_specs = [pl.BlockSpec((tm, D), lambda i, j: (i, 0)), pl.BlockSpec((tn, D), lambda i, j: (j, 0)),
                pl.BlockSpec((tn, D), lambda i, j: (j + nj, 0))]
    args = [x, wt, wt]
    if order_after is not None:
        in_specs.append(pl.BlockSpec(order_after.shape, lambda i, j: (0, 0)))
        args.append(order_after)
    out_specs = [pl.BlockSpec((tm, tn), lambda i, j: (i, j)), pl.BlockSpec((2, tm, tn), lambda i, j: (0, i, j))]
    out_shape = [jax.ShapeDtypeStruct((T, D_FF), BF16), jax.ShapeDtypeStruct((2, T, D_FF), BF16)]
    if emit_xb:
        out_specs.append(pl.BlockSpec((tm, D), lambda i, j: (i, 0)))
        out_shape.append(jax.ShapeDtypeStruct((T, D), BF16))
    return _pcall(
        body, grid=(T // tm, nj), in_specs=in_specs, out_specs=out_specs, out_shape=out_shape, name=name,
        compiler_params=pltpu.CompilerParams(dimension_semantics=("parallel", "arbitrary")),
    )(*args)


def _ffn_act_bwd(df, wo, gu, name):
    T = df.shape[0]
    tm, tn = min(FFN_TM, T), FFN_TN

    def body(d_ref, w_ref, gu_ref, o_ref):
        dhs = _dot(d_ref[...], w_ref[...], 1, 1)
        g, u = gu_ref[0].astype(F32), gu_ref[1].astype(F32)
        sig = _sigmoid(g)
        act = g * sig
        o_ref[0] = (dhs * u * (sig + act * (1.0 - sig))).astype(BF16)
        o_ref[1] = (dhs * act).astype(BF16)

    return _pcall(
        body, grid=(T // tm, D_FF // tn),
        in_specs=[pl.BlockSpec((tm, D), lambda i, j: (i, 0)), pl.BlockSpec((tn, D), lambda i, j: (j, 0)),
                  pl.BlockSpec((2, tm, tn), lambda i, j: (0, i, j))],
        out_specs=pl.BlockSpec((2, tm, tn), lambda i, j: (0, i, j)),
        out_shape=jax.ShapeDtypeStruct((2, T, D_FF), BF16), name=name,
        compiler_params=pltpu.CompilerParams(dimension_semantics=("parallel", "parallel")),
    )(df, wo, gu)


def _rope128(t, cos, sin_s):
    return t * cos + pltpu.roll(t, 64, 1) * sin_s


def _rope128_t(g, cos, sin_s):
    return g * cos - pltpu.roll(g, 64, 1) * sin_s


def _partner32(t):
    lane = lax.broadcasted_iota(jnp.int32, t.shape, 1)
    return jnp.where((lane & 32) == 0, pltpu.roll(t, 96, 1), pltpu.roll(t, 32, 1))


def _rope64(t, cos, sin_s):
    return t * cos + _partner32(t) * sin_s


def _rope64_t(g, cos, sin_s):
    return g * cos - _partner32(g) * sin_s


def _mixer_prep_fwd(proj, tabs, qg, kvg, T):
    cos128, sin128, cos64, sin64 = tabs

    def body(rq_ref, rk_ref, cq_ref, ckv_ref, kpe_ref, c1_ref, s1_ref, c2_ref, s2_ref, qg_ref, kvg_ref,
             oq_ref, ok_ref, oqn_ref, okvn_ref, okpe_ref):
        c1, s1 = c1_ref[...], s1_ref[...]
        for h in range(HEADS):
            sl = slice(h * RET_DK, (h + 1) * RET_DK)
            oq_ref[:, sl] = _rope128(rq_ref[:, sl].astype(F32), c1, s1).astype(BF16)
            ok_ref[:, sl] = (_rope128(rk_ref[:, sl].astype(F32), c1, s1) * RET_SCALE).astype(BF16)
        oqn_ref[...] = _rms(cq_ref[...].astype(F32), qg_ref[...]).astype(BF16)
        okvn_ref[...] = _rms(ckv_ref[...].astype(F32), kvg_ref[...]).astype(BF16)
        okpe_ref[...] = _rope64(kpe_ref[...].astype(F32), c2_ref[...], s2_ref[...]).astype(BF16)

    ins = [(proj, 1024, 0), (proj, 1024, 1), (proj, 256, P_CQ // 256), (proj, 256, P_CKV // 256),
           (proj, 128, P_KPE // 128), (cos128, 128, 0), (sin128, 128, 0), (cos64, 128, 0), (sin64, 128, 0),
           (qg, None, None), (kvg, None, None)]
    outs = [(1024, BF16), (1024, BF16), (LORA, BF16), (LORA, BF16), (128, BF16)]
    return _rows(body, T, 256, ins, outs, "mixer_prep_fwd")


def _rms_bwd(proj, dqn, dkvn, dkpe_all, tabs, qg, kvg, T):
    _, _, cos64, sin64 = tabs

    def body(cq_ref, ckv_ref, dq_ref, dkv_ref, dk_ref, c2_ref, s2_ref, qg_ref, kvg_ref, ocq_ref, ockv_ref, okpe_ref,
             dqg_ref, dkvg_ref):
        _, vjp = jax.vjp(_rms, cq_ref[...].astype(F32), qg_ref[...])
        dx, dg = vjp(dq_ref[...])
        ocq_ref[...] = dx.astype(BF16)
        _accum(dqg_ref, dg)
        _, vjp = jax.vjp(_rms, ckv_ref[...].astype(F32), kvg_ref[...])
        dx, dg = vjp(dkv_ref[...])
        ockv_ref[...] = dx.astype(BF16)
        _accum(dkvg_ref, dg)
        g = dk_ref[:, 0:128]
        for h in range(1, HEADS):
            g = g + dk_ref[:, h * 128:(h + 1) * 128]
        lane = lax.broadcasted_iota(jnp.int32, g.shape, 1)
        g = jnp.where(lane < MLA_ROPE, g, 0.0)
        okpe_ref[...] = _rope64_t(g, c2_ref[...], s2_ref[...]).astype(BF16)

    ins = [(proj, 256, P_CQ // 256), (proj, 256, P_CKV // 256), (dqn, LORA, 0), (dkvn, LORA, 0), (dkpe_all, 1024, 0),
           (cos64, 128, 0), (sin64, 128, 0), (qg, None, None), (kvg, None, None)]
    return _rows(body, T, 256, ins, [(LORA, BF16), (LORA, BF16), (128, BF16)], "rms_bwd", accs=[(1, LORA), (1, LORA)])


def _gn_gate_bwd(y, rg, g, d):
    xc = y - jnp.mean(y, axis=-1, keepdims=True)
    rstd = lax.rsqrt(jnp.mean(jnp.square(xc), axis=-1, keepdims=True) + EPS)
    xhat = xc * rstd
    sig = _sigmoid(rg)
    act = rg * sig
    dn = d * act
    drg = d * (xhat * g) * (sig + act * (1.0 - sig))
    dxh = dn * g
    dy = rstd * (dxh - jnp.mean(dxh, axis=-1, keepdims=True) - xhat * jnp.mean(dxh * xhat, axis=-1, keepdims=True))
    return dy, drg, jnp.sum(dn * xhat, axis=0, keepdims=True)


def _gn_gate(y, rg, g):
    mu = jnp.mean(y, axis=-1, keepdims=True)
    var = jnp.mean(jnp.square(y - mu), axis=-1, keepdims=True)
    return _silu(rg) * ((y - mu) * lax.rsqrt(var + EPS) * g)


def _q_assemble_epilogue(tabs):
    _, _, cos64, sin64 = tabs

    def fn(q, ins, outs, first):
        c_ref, s_ref = ins
        on_ref, op_ref = outs
        on_ref[...] = (q[:, :1024] * Q_PRESCALE).astype(BF16)
        c, s = c_ref[...], s_ref[...]
        lane = lax.broadcasted_iota(jnp.int32, c.shape, 1)
        for j in range(HEADS // 2):
            r = _rope64(q[:, 1024 + 128 * j:1024 + 128 * (j + 1)], c, s) * Q_PRESCALE
            op_ref[:, 256 * j:256 * j + 128] = jnp.where(lane < 64, r, 0.0).astype(BF16)
            op_ref[:, 256 * j + 128:256 * j + 256] = jnp.where(lane < 64, pltpu.roll(r, 64, 1), 0.0).astype(BF16)

    return (fn, [(cos64, 0, 128), (sin64, 0, 128)], [], ((BF16, 1024), (BF16, 1024)), ())


def _q_assemble_bwd(dqn, dqpe, tabs, T):
    _, _, cos64, sin64 = tabs

    def body(dn_ref, dp_ref, c_ref, s_ref, on_ref, op_ref):
        on_ref[...] = dn_ref[...].astype(BF16)
        c, s = c_ref[...], s_ref[...]
        lane = lax.broadcasted_iota(jnp.int32, c.shape, 1)
        for j in range(HEADS // 2):
            g = jnp.where(lane < 64, dp_ref[:, 256 * j:256 * j + 128], pltpu.roll(dp_ref[:, 256 * j + 128:256 * j + 256], 64, 1))
            op_ref[:, 128 * j:128 * (j + 1)] = _rope64_t(g, c, s).astype(BF16)

    return _rows(body, T, 256, [(dqn, 1024, 0), (dqpe, 1024, 0), (cos64, 128, 0), (sin64, 128, 0)],
                 [(1024, BF16), (512, BF16)], "q_assemble_bwd")


def _mix_fn(gr, gm, yr, ym):
    return _sigmoid(gr) * yr + _sigmoid(gm) * ym


def _mix_epilogue(proj, y_ret):
    def fn(r, ins, outs, first):
        gr_ref, gm_ref, yr_ref = ins
        ym_ref, mix_ref = outs
        ym_ref[...] = r.astype(BF16)
        mix_ref[...] = _mix_fn(gr_ref[...].astype(F32), gm_ref[...].astype(F32), yr_ref[...].astype(F32), r).astype(BF16)

    return (fn, [(proj, P_GR // D), (proj, P_GM // D), y_ret], [], (BF16, BF16), ())


def _mix_bwd_epilogue(proj, y_ret, y_mla):
    def fn(r, ins, outs, first):
        gr_ref, gm_ref, yr_ref, ym_ref = ins
        dgr_ref, dgm_ref, dyr_ref, dym_ref = outs
        sr, sm = _sigmoid(gr_ref[...].astype(F32)), _sigmoid(gm_ref[...].astype(F32))
        dgr_ref[...] = (r * yr_ref[...].astype(F32) * (sr * (1.0 - sr))).astype(BF16)
        dgm_ref[...] = (r * ym_ref[...].astype(F32) * (sm * (1.0 - sm))).astype(BF16)
        dyr_ref[...] = (r * sr).astype(BF16)
        dym_ref[...] = (r * sm).astype(BF16)

    return (fn, [(proj, P_GR // D), (proj, P_GM // D), y_ret, y_mla], [], (BF16,) * 4, ())


def _head_epilogue(h3, pp, tgt, g, b):
    def fn(r, ins, outs, first):
        h_ref, pp_ref, t_ref, g_ref, b_ref = ins
        dh_ref, dgl_ref, dpp_ref, dg_ref, db_ref, loss_ref = outs

        sg, pp, gain = _sigmoid(r), pp_ref[...], g_ref[...]
        pre = ALPHA * h_ref[...] + sg * pp
        xc = pre - jnp.mean(pre, axis=-1, keepdims=True)
        rstd = lax.rsqrt(jnp.mean(jnp.square(xc), axis=-1, keepdims=True) + EPS)
        xhat = xc * rstd
        err = xhat * gain + b_ref[...] - t_ref[...]
        dy = err * (1.0 / D)
        dyg = dy * gain
        dpre = rstd * (dyg - jnp.mean(dyg, axis=-1, keepdims=True) - xhat * jnp.mean(dyg * xhat, axis=-1, keepdims=True))
        dh_ref[...] = ALPHA * dpre
        dgl_ref[...] = (dpre * pp * (sg * (1.0 - sg))).astype(BF16)
        dpp_ref[...] = (dpre * sg).astype(BF16)
        _accum(dg_ref, jnp.sum(dy * xhat, axis=0, keepdims=True), first)
        _accum(db_ref, jnp.sum(dy, axis=0, keepdims=True), first)
        part = 0.5 * jnp.sum(jnp.mean(jnp.square(err), axis=-1, keepdims=True), axis=0, keepdims=True)
        _accum(loss_ref, jnp.broadcast_to(part, loss_ref.shape), first)

    return (fn, [h3, pp, tgt], [g, b], (F32, BF16, BF16), (1, 1, 8))


def _delta_epilogue(o):
    def fn(r, ins, outs, first):
        (o_ref,) = ins
        db_ref, dl_ref = outs
        db_ref[...] = r.astype(BF16)
        for h in range(HEADS):
            sl = slice(h * MLA_DV, (h + 1) * MLA_DV)
            dl = jnp.sum(r[:, sl] * o_ref[:, sl].astype(F32), axis=-1, keepdims=True)
            dl_ref[:, sl] = jnp.broadcast_to(dl, (r.shape[0], MLA_DV))

    return (fn, [o], [], (BF16, F32), ())


def _ret_consts():
    L = RET_L
    lg = jnp.log(1.0 - 2.0 ** (-5.0 - jnp.arange(HEADS, dtype=F32)))[:, None, None]
    idx = jnp.arange(L, dtype=F32)
    ch = jnp.arange(L) // CHUNK
    dist = idx[:, None] - idx[None, :]
    same = (ch[:, None] == ch[None, :])[None]
    earlier = (ch[None, :] < ch[:, None])[None]
    dm = jnp.where(same, jnp.exp(lg * jnp.abs(dist)[None]), jnp.where(earlier, jnp.exp(lg * dist[None]), 0.0))
    xi = jnp.broadcast_to(jnp.exp(lg * (idx + 1.0)[None, :, None]), (HEADS, L, 128))
    zeta = jnp.broadcast_to(jnp.exp(lg * (L - 1.0 - idx)[None, :, None]), (HEADS, L, 128))
    gl = jnp.broadcast_to(jnp.exp(lg * float(L)), (HEADS, 8, 128))
    return dm.astype(F32), xi.astype(F32), zeta.astype(F32), gl.astype(F32)


def _whole(arr):
    return pl.BlockSpec(arr.shape, lambda n, _nd=arr.ndim: (0,) * _nd)


def _ret_fwd(q, k, v, proj, gn_g, consts, T):
    dm, xi, zeta, gl = consts
    L = RET_L
    n_sc = T // L

    def body(q_ref, k_ref, v_ref, rg_ref, g_ref, dm_ref, xi_ref, ze_ref, gl_ref, y_ref, yr_ref, s_ref, st_ref):
        @pl.when(pl.program_id(0) == 0)
        def _():
            st_ref[...] = jnp.zeros_like(st_ref)

        for h in range(HEADS):
            ks, vs = slice(h * RET_DK, (h + 1) * RET_DK), slice(h * RET_DV, (h + 1) * RET_DV)
            qq, kk, vv = q_ref[:, ks], k_ref[:, ks], v_ref[:, vs]
            st = st_ref[h]
            s_ref[h, 0] = st
            p = (_dot(qq, kk, 1, 1) * dm_ref[h]).astype(BF16)
            cross = _dot(qq, st.astype(BF16), 1, 0)
            xi_c = jnp.concatenate([xi_ref[h], xi_ref[h]], axis=1)
            y = _dot(p, vv, 1, 0) + cross * xi_c
            yb = y.astype(BF16)
            y_ref[:, vs] = yb
            yr_ref[:, vs] = _gn_gate(yb.astype(F32), rg_ref[:, vs].astype(F32), g_ref[:, vs]).astype(BF16)
            kz = (kk.astype(F32) * ze_ref[h]).astype(BF16)
            gl2 = jnp.concatenate([gl_ref[h, 0:1, :], gl_ref[h, 0:1, :]], axis=1)
            st_ref[h] = st * gl2 + _dot(kz, vv, 0, 0)

    return _pcall(
        body, grid=(n_sc,),
        in_specs=[pl.BlockSpec((L, 1024), lambda n: (n, 0)), pl.BlockSpec((L, 1024), lambda n: (n, 0)),
                  pl.BlockSpec((L, 2048), lambda n: (n, P_RV // 2048)), pl.BlockSpec((L, 2048), lambda n: (n, P_RG // 2048)),
                  _whole(gn_g), _whole(dm), _whole(xi), _whole(zeta), _whole(gl)],
        out_specs=[pl.BlockSpec((L, 2048), lambda n: (n, 0)), pl.BlockSpec((L, 2048), lambda n: (n, 0)),
                   pl.BlockSpec((HEADS, 1, 128, 256), lambda n: (0, n, 0, 0))],
        out_shape=[jax.ShapeDtypeStruct((T, HEADS * RET_DV), BF16), jax.ShapeDtypeStruct((T, HEADS * RET_DV), BF16),
                   jax.ShapeDtypeStruct((HEADS, n_sc, 128, 256), F32)],
        scratch_shapes=[pltpu.VMEM((HEADS, 128, 256), F32)], name="ret_fwd",
        compiler_params=pltpu.CompilerParams(dimension_semantics=("arbitrary",)),
    )(q, k, v, proj, gn_g, dm, xi, zeta, gl)


def _ret_bwd(q, k, v, y, proj, gn_g, states, dyr, consts, tabs, T):
    dm, xi, zeta, gl = consts
    cos128, sin128, _, _ = tabs
    L = RET_L
    n_sc = T // L

    def body(q_ref, k_ref, v_ref, y_ref, rg_ref, g_ref, d_ref, s_ref, dm_ref, xi_ref, ze_ref, gl_ref, c_ref, sn_ref,
             dq_ref, dk_ref, dv_ref, drg_ref, dg_ref, gs_ref):
        @pl.when(pl.program_id(0) == 0)
        def _():
            gs_ref[...] = jnp.zeros_like(gs_ref)

        c, sn = c_ref[...], sn_ref[...]
        dgs = []
        for h in range(HEADS):
            ks, vs = slice(h * RET_DK, (h + 1) * RET_DK), slice(h * RET_DV, (h + 1) * RET_DV)
            dy, drg, dg = _gn_gate_bwd(y_ref[:, vs].astype(F32), rg_ref[:, vs].astype(F32), g_ref[:, vs], d_ref[:, vs].astype(F32))
            drg_ref[:, vs] = drg.astype(BF16)
            dgs.append(dg)
            qq, kk, vv, dyy = q_ref[:, ks], k_ref[:, ks], v_ref[:, vs], dy.astype(BF16)
            dmm = dm_ref[h]
            gb = gs_ref[h].astype(BF16)
            sb = s_ref[h, 0].astype(BF16)
            xi_c = jnp.concatenate([xi_ref[h], xi_ref[h]], axis=1)
            pb = (_dot(qq, kk, 1, 1) * dmm).astype(BF16)
            kz = (kk.astype(F32) * ze_ref[h]).astype(BF16)
            dv_ref[:, vs] = (_dot(pb, dyy, 0, 0) + _dot(kz, gb, 1, 0)).astype(BF16)
            da = (_dot(dyy, vv, 1, 1) * dmm).astype(BF16)
            dyx = (dyy.astype(F32) * xi_c).astype(BF16)
            dq = _dot(da, kk, 1, 0) + _dot(dyx, sb, 1, 1)
            dk = _dot(da, qq, 0, 0) + _dot(vv, gb, 1, 1) * ze_ref[h]
            dq_ref[:, ks] = _rope128_t(dq, c, sn).astype(BF16)
            dk_ref[:, ks] = (_rope128_t(dk, c, sn) * RET_SCALE).astype(BF16)
            gl2 = jnp.concatenate([gl_ref[h, 0:1, :], gl_ref[h, 0:1, :]], axis=1)
            gs_ref[h] = gs_ref[h] * gl2 + _dot(qq, dyx, 0, 0)
        _accum(dg_ref, jnp.concatenate(dgs, axis=1))

    rev = lambda n: n_sc - 1 - n
    return _pcall(
        body, grid=(n_sc,),
        in_specs=[pl.BlockSpec((L, 1024), lambda n: (rev(n), 0)), pl.BlockSpec((L, 1024), lambda n: (rev(n), 0)),
                  pl.BlockSpec((L, 2048), lambda n: (rev(n), P_RV // 2048)), pl.BlockSpec((L, 2048), lambda n: (rev(n), 0)),
                  pl.BlockSpec((L, 2048), lambda n: (rev(n), P_RG // 2048)), _whole(gn_g),
                  pl.BlockSpec((L, 2048), lambda n: (rev(n), 0)),
                  pl.BlockSpec((HEADS, 1, 128, 256), lambda n: (0, rev(n), 0, 0)),
                  _whole(dm), _whole(xi), _whole(zeta), _whole(gl),
                  pl.BlockSpec((L, 128), lambda n: (rev(n), 0)), pl.BlockSpec((L, 128), lambda n: (rev(n), 0))],
        out_specs=[pl.BlockSpec((L, 1024), lambda n: (rev(n), 0)), pl.BlockSpec((L, 1024), lambda n: (rev(n), 0)),
                   pl.BlockSpec((L, 2048), lambda n: (rev(n), 0)), pl.BlockSpec((L, 2048), lambda n: (rev(n), 0)),
                   pl.BlockSpec((1, 2048), lambda n: (0, 0))],
        out_shape=[jax.ShapeDtypeStruct((T, 1024), BF16), jax.ShapeDtypeStruct((T, 1024), BF16), jax.ShapeDtypeStruct((T, 2048), BF16),
                   jax.ShapeDtypeStruct((T, 2048), BF16), jax.ShapeDtypeStruct((1, 2048), F32)],
        scratch_shapes=[pltpu.VMEM((HEADS, 128, 256), F32)], name="ret_bwd",
        compiler_params=pltpu.CompilerParams(dimension_semantics=("arbitrary",)),
    )(q, k, v, y, proj, gn_g, dyr, states, dm, xi, zeta, gl, cos128, sin128)


def _diag_mask(nrows, ncols, row0):
    row = row0 + lax.broadcasted_iota(jnp.int32, (nrows, ncols), 0)
    col = lax.broadcasted_iota(jnp.int32, (nrows, ncols), 1)
    return lax.shift_right_logical(col, 6) <= lax.shift_right_logical(row, 6)


def _diag_spans(t):
    h = t // 2
    return ((0, h, h), (h, h, t)) if h % 128 == 0 else ((0, t, t),)


def _tri_steps(nb, by_key):
    if by_key:
        pairs = [(i, j) for j in range(nb) for i in range(j, nb)]
    else:
        pairs = [(i, j) for i in range(nb) for j in range(i + 1)]
    return jnp.array([a for a, _ in pairs], jnp.int32), jnp.array([b for _, b in pairs], jnp.int32)


def _attn_fwd(qn, qpe, kv, kpe, T):
    t = min(ATT_TF, T)
    nb = T // t
    ii, jj = _tri_steps(nb, by_key=False)

    hp = ATT_HP
    w = 128 * hp

    def body(ii_ref, jj_ref, qn_ref, qp_ref, kn_ref, kp_ref, v_ref, o_ref, lse_ref, m_sc, l_sc, acc_sc):
        st = pl.program_id(1)
        i, j = ii_ref[st], jj_ref[st]

        @pl.when(j == 0)
        def _():
            m_sc[...] = jnp.full_like(m_sc, NEG)
            l_sc[...] = jnp.zeros_like(l_sc)
            acc_sc[...] = jnp.zeros_like(acc_sc)

        def update(diag):
            kp = kp_ref[...]
            for hh in range(hp):
                sl = slice(128 * hh, 128 * (hh + 1))
                q = jnp.concatenate([qn_ref[:, sl], qp_ref[:, sl]], axis=1)
                k = jnp.concatenate([kn_ref[:, sl], kp], axis=1)
                s = _dot(q, k, 1, 1)
                if diag:
                    s = jnp.where(_diag_mask(t, t, 0), s, NEG)
                m_prev = m_sc[:, sl]
                m_new = jnp.maximum(m_prev, jnp.max(s, axis=1, keepdims=True))
                a = jnp.exp2(m_prev - m_new)
                p = jnp.exp2(s - m_new[:, 0:1])
                l_sc[:, sl] = a * l_sc[:, sl] + jnp.sum(p, axis=1, keepdims=True)
                acc_sc[:, sl] = a * acc_sc[:, sl] + _dot(p.astype(BF16), v_ref[:, sl], 1, 0)
                m_sc[:, sl] = m_new

        @pl.when(j < i)
        def _():
            update(False)

        @pl.when(j == i)
        def _():
            update(True)
            o_ref[...] = (acc_sc[...] / l_sc[...]).astype(BF16)
            lse_ref[...] = m_sc[...] + jnp.log2(l_sc[...])

    qs = pl.BlockSpec((t, w), lambda h, s, ii, jj: (ii[s], h))
    grid_spec = pltpu.PrefetchScalarGridSpec(
        num_scalar_prefetch=2, grid=(HEADS // hp, int(ii.shape[0])),
        in_specs=[qs, qs, pl.BlockSpec((t, w), lambda h, s, ii, jj: (jj[s], h)), pl.BlockSpec((t, 128), lambda h, s, ii, jj: (jj[s], 0)),
                  pl.BlockSpec((t, w), lambda h, s, ii, jj: (jj[s], HEADS // hp + h))],
        out_specs=[qs, qs],
        scratch_shapes=[pltpu.VMEM((t, w), F32), pltpu.VMEM((t, w), F32), pltpu.VMEM((t, w), F32)])
    return _pcall(
        body, grid_spec=grid_spec, out_shape=[jax.ShapeDtypeStruct((T, D), BF16), jax.ShapeDtypeStruct((T, D), F32)], name="attn_fwd",
        compiler_params=pltpu.CompilerParams(dimension_semantics=("arbitrary", "arbitrary")),
    )(ii, jj, qn, qpe, kv, kpe, kv)


def _attn_bwd(qn, qpe, kv, kpe, do, lse, delta, T):
    t = min(ATT_TB, T)
    nb = T // t
    ii, jj = _tri_steps(nb, by_key=True)

    def body(ii_ref, jj_ref, qn_ref, qp_ref, kn_ref, kp_ref, v_ref, do_ref, lse_ref, dl_ref,
             dqn_ref, dqp_ref, dkn_ref, dkp_ref, dv_ref, dk_sc, dv_sc):
        st = pl.program_id(1)
        i, j = ii_ref[st], jj_ref[st]

        @pl.when(st == 0)
        def _():
            dqn_ref[...] = jnp.zeros_like(dqn_ref)
            dqp_ref[...] = jnp.zeros_like(dqp_ref)

        @pl.when(i == j)
        def _():
            dk_sc[...] = jnp.zeros_like(dk_sc)
            dv_sc[...] = jnp.zeros_like(dv_sc)

        def update(diag):
            for row0, nr, nkeys in (_diag_spans(t) if diag else ((0, t, t),)):
                rows = slice(row0, row0 + nr)
                q = jnp.concatenate([qn_ref[rows, :], qp_ref[rows, :]], axis=1)
                k = jnp.concatenate([kn_ref[:nkeys, :], kp_ref[:nkeys, :]], axis=1)
                dob = do_ref[rows, :]
                s = _dot(q, k, 1, 1)
                if diag:
                    s = jnp.where(_diag_mask(nr, nkeys, row0), s, NEG)
                p = jnp.exp2(s - lse_ref[rows, 0:1])
                dv_sc[:nkeys, :] += _dot(p.astype(BF16), dob, 0, 0)
                dp = _dot(dob, v_ref[:nkeys, :], 1, 1)
                ds = (p * (dp - dl_ref[rows, 0:1])).astype(BF16)
                dk_sc[:nkeys, :] += _dot(ds, q, 0, 0)
                dq = _dot(ds, k, 1, 0) * MLA_SCALE
                grows = pl.ds(pl.multiple_of(i * t + row0, nr), nr)
                dqn_ref[grows, :] += dq[:, :128]
                dqp_ref[grows, :] += dq[:, 128:]

        @pl.when(i > j)
        def _():
            update(False)

        @pl.when(i == j)
        def _():
            update(True)

        @pl.when(i == nb - 1)
        def _():
            dkn_ref[...] = (dk_sc[:, :128] * (1.0 / LOG2E)).astype(BF16)
            dkp_ref[...] = dk_sc[:, 128:] * (1.0 / LOG2E)
            dv_ref[...] = dv_sc[...].astype(BF16)

    qs = pl.BlockSpec((t, 128), lambda h, s, ii, jj: (ii[s], h))
    ks = pl.BlockSpec((t, 128), lambda h, s, ii, jj: (jj[s], h))
    hs = pl.BlockSpec((T, 128), lambda h, s, ii, jj: (0, h))
    grid_spec = pltpu.PrefetchScalarGridSpec(
        num_scalar_prefetch=2, grid=(HEADS, int(ii.shape[0])),
        in_specs=[qs, qs, ks, pl.BlockSpec((t, 128), lambda h, s, ii, jj: (jj[s], 0)),
                  pl.BlockSpec((t, 128), lambda h, s, ii, jj: (jj[s], HEADS + h)), qs, qs, qs],
        out_specs=[hs, hs, ks, ks, ks],
        scratch_shapes=[pltpu.VMEM((t, 256), F32), pltpu.VMEM((t, 128), F32)])
    return _pcall(
        body, grid_spec=grid_spec,
        out_shape=[jax.ShapeDtypeStruct((T, D), F32), jax.ShapeDtypeStruct((T, D), F32), jax.ShapeDtypeStruct((T, D), BF16),
                   jax.ShapeDtypeStruct((T, D), F32), jax.ShapeDtypeStruct((T, D), BF16)], name="attn_bwd",
        compiler_params=pltpu.CompilerParams(dimension_semantics=("arbitrary", "arbitrary")),
    )(ii, jj, qn, qpe, kv, kpe, kv, do, lse, delta)


def _mesh_pos():
    x, y, c = lax.axis_index("x"), lax.axis_index("y"), lax.axis_index("c")
    return x, y, c, 4 * x + 2 * y + c


def _peer(x, y, c, k):
    px, py, pc = (x + ((k >> 2) & 1)) % 2, (y + ((k >> 1) & 1)) % 2, (c + (k & 1)) % 2
    return (px, py, pc), 4 * px + 2 * py + pc


_ANY = pl.BlockSpec(memory_space=pl.ANY)


def _rcopy(src, dst, send_sems, recv_sems, k, to):
    return pltpu.make_async_remote_copy(src_ref=src, dst_ref=dst, send_sem=send_sems.at[k], recv_sem=recv_sems.at[k],
                                        device_id=to, device_id_type=pl.DeviceIdType.MESH)


def _all_gather(wsh, ssh):
    def body(w_ref, s_ref, wall_ref, sall_ref, send_sems, recv_sems, loc_sems):
        x, y, c, me = _mesh_pos()
        sib = (x, y, 1 - c)
        chips = [(1 - x, y), (x, 1 - y), (1 - x, 1 - y)]
        slot = lambda px, py, pc: 4 * px + 2 * py + pc
        loc = [pltpu.make_async_copy(s_ref, sall_ref.at[me], loc_sems.at[0])]
        for cp in loc:
            cp.start()
        sends, fwd_waits = [], []
        for n, (src, dst) in enumerate(((w_ref, wall_ref), (s_ref, sall_ref))):
            o = 7 * n
            sends.append(_rcopy(src, dst.at[me], send_sems, recv_sems, o, sib))
            for j, chip in enumerate(chips):
                sends.append(_rcopy(src, dst.at[me], send_sems, recv_sems, o + 1 + j, (*chip, c)))
        for cp in sends:
            cp.start()
        for n, (src, dst) in enumerate(((w_ref, wall_ref), (s_ref, sall_ref))):
            o = 7 * n
            for j, chip in enumerate(chips):
                got = dst.at[slot(*chip, c)]
                _rcopy(src, got, send_sems, recv_sems, o + 1 + j, sib).wait_recv()
                fw = _rcopy(got, got, send_sems, recv_sems, o + 4 + j, sib)
                fw.start()
                sends.append(fw)
            fwd_waits.append(_rcopy(src, dst.at[slot(x, y, 1 - c)], send_sems, recv_sems, o, sib))
            for j, chip in enumerate(chips):
                fwd_waits.append(_rcopy(src, dst.at[slot(*chip, 1 - c)], send_sems, recv_sems, o + 4 + j, sib))
        for cp in fwd_waits:
            cp.wait_recv()
        for cp in sends:
            cp.wait_send()
        for cp in loc:
            cp.wait()

    wall, sall = _pcall(
        body, in_specs=[_ANY, _ANY], out_specs=[_ANY, _ANY],
        out_shape=[jax.ShapeDtypeStruct((N_DEV,) + wsh.shape, wsh.dtype), jax.ShapeDtypeStruct((N_DEV,) + ssh.shape, ssh.dtype)],
        scratch_shapes=[pltpu.SemaphoreType.DMA((14,)), pltpu.SemaphoreType.DMA((14,)), pltpu.SemaphoreType.DMA((1,))],
        name="all_gather_weights",
    )(wsh, ssh)
    me = 4 * lax.axis_index("x") + 2 * lax.axis_index("y") + lax.axis_index("c")
    return lax.dynamic_update_index_in_dim(wall, wsh, me, axis=0), sall


_HBM = pl.BlockSpec(memory_space=pltpu.HBM)
_SEM = pl.BlockSpec(memory_space=pltpu.SEMAPHORE)
_EFFECT = pltpu.SideEffectType.DATAFLOW_SIDE_EFFECTING


def _other_chips(x, y):
    return [(1 - x, y), (x, 1 - y), (1 - x, 1 - y)]


def _gather_start(wsh, order_after):
    def body(w_ref, land_ref, dep_ref, send_sems, recv_sems, w_thru, land_thru, token):
        x, y, c, me = _mesh_pos()
        _rcopy(w_ref, land_ref.at[me], send_sems, recv_sems, 0, (x, y, 1 - c)).start()
        for j, chip in enumerate(_other_chips(x, y)):
            _rcopy(w_ref, land_ref.at[me], send_sems, recv_sems, 1 + j, (*chip, c)).start()
        token[...] = jnp.zeros_like(token)

    shape = (N_DEV,) + wsh.shape
    land = pltpu.with_memory_space_constraint(lax.empty(shape, wsh.dtype), pltpu.HBM)
    return _pcall(
        body, name="gather_start",
        out_shape=(pltpu.SemaphoreType.DMA((4,)), pltpu.SemaphoreType.DMA((4,)), pltpu.HBM(wsh.shape, wsh.dtype),
                   pltpu.HBM(shape, wsh.dtype), jax.ShapeDtypeStruct((8, 128), F32)),
        in_specs=(_HBM, _HBM, _ANY), out_specs=(_SEM, _SEM, _HBM, _HBM, pl.BlockSpec(memory_space=pltpu.VMEM)),
        input_output_aliases={0: 2, 1: 3}, compiler_params=pltpu.CompilerParams(has_side_effects=_EFFECT),
    )(pltpu.with_memory_space_constraint(wsh, pltpu.HBM), land, order_after)


def _gather_wait(send_sems, recv_sems, w_thru, land_thru, after):
    def body(w_ref, land_ref, send_sems, recv_sems, after_ref, w_out, land_out):
        x, y, c, _ = _mesh_pos()
        senders = [(x, y, 1 - c)] + [(*chip, c) for chip in _other_chips(x, y)]
        for k, (px, py, pc) in enumerate(senders):
            cp = _rcopy(w_ref, land_ref.at[4 * px + 2 * py + pc], send_sems, recv_sems, k, (px, py, pc))
            cp.wait_send()
            cp.wait_recv()

    return _pcall(
        body, name="gather_wait", out_shape=(pltpu.HBM(w_thru.shape, w_thru.dtype), pltpu.HBM(land_thru.shape, land_thru.dtype)),
        in_specs=(_HBM, _HBM, _SEM, _SEM, _ANY), out_specs=(_HBM, _HBM), input_output_aliases={0: 0, 1: 1},
        compiler_params=pltpu.CompilerParams(has_side_effects=_EFFECT),
    )(w_thru, land_thru, send_sems, recv_sems, after)


def _gather_finish(wsh, land):
    def body(land_ref, out_ref, send_sems, recv_sems):
        x, y, c, _ = _mesh_pos()
        sib = (x, y, 1 - c)
        sends, recvs = [], []
        for j, (px, py) in enumerate(_other_chips(x, y)):
            mine, theirs = 4 * px + 2 * py + c, 4 * px + 2 * py + (1 - c)
            sends.append(_rcopy(land_ref.at[mine], out_ref.at[mine], send_sems, recv_sems, j, sib))
            recvs.append(_rcopy(land_ref.at[theirs], out_ref.at[theirs], send_sems, recv_sems, j, sib))
        for cp in sends:
            cp.start()
        for cp in recvs:
            cp.wait_recv()
        for cp in sends:
            cp.wait_send()

    done = _pcall(
        body, in_specs=[_ANY], out_specs=_ANY, out_shape=jax.ShapeDtypeStruct(land.shape, land.dtype),
        input_output_aliases={0: 0},
        scratch_shapes=[pltpu.SemaphoreType.DMA((3,)), pltpu.SemaphoreType.DMA((3,))], name="gather_finish",
    )(land)
    me = 4 * lax.axis_index("x") + 2 * lax.axis_index("y") + lax.axis_index("c")
    return lax.dynamic_update_index_in_dim(done, wsh, me, axis=0)


_ROW_TILES = (400, 368, 352, 256, 128, 16, 8)


def _exchange_sibling(g4, name):
    def body(g_ref, r_ref, send_sems, recv_sems):
        x, y, c, _ = _mesh_pos()
        sib = (x, y, 1 - c)
        cps = [_rcopy(g_ref.at[q, 1 - c], r_ref.at[q], send_sems, recv_sems, q, sib) for q in range(4)]
        for cp in cps:
            cp.start()
        for cp in cps:
            cp.wait()

    n, _, R, C = g4.shape
    return _pcall(
        body, in_specs=[_ANY], out_specs=_ANY, out_shape=jax.ShapeDtypeStruct((n, R, C), g4.dtype),
        scratch_shapes=[pltpu.SemaphoreType.DMA((4,)), pltpu.SemaphoreType.DMA((4,))], name=name,
    )(g4)


def _sum_sibling(g4, rsib, cvec, name):
    n, _, R, C = g4.shape
    tr = _pick(R, _ROW_TILES)

    def body(c_ref, g_ref, r_ref, o_ref):
        o_ref[...] = (g_ref[...].astype(F32) + r_ref[...].astype(F32)).astype(o_ref.dtype)

    grid_spec = pltpu.PrefetchScalarGridSpec(
        num_scalar_prefetch=1, grid=(n, R // tr),
        in_specs=[pl.BlockSpec((None, None, tr, C), lambda q, i, cr: (q, cr[0], i, 0)), pl.BlockSpec((None, tr, C), lambda q, i, cr: (q, i, 0))],
        out_specs=pl.BlockSpec((None, tr, C), lambda q, i, cr: (q, i, 0)))
    return _pcall(body, grid_spec=grid_spec, out_shape=jax.ShapeDtypeStruct((n, R, C), g4.dtype), name=name)(cvec, g4, rsib)


def _chips_start(part, name, order_after=None):
    n_in = 2 + int(order_after is not None)

    def body(*refs):
        p_ref, land_ref = refs[:2]
        send_sems, recv_sems, token = refs[n_in], refs[n_in + 1], refs[n_in + 4]
        x, y, c, _ = _mesh_pos()
        myq = 2 * x + y
        for j, (px, py) in enumerate(_other_chips(x, y)):
            _rcopy(p_ref.at[2 * px + py], land_ref.at[myq], send_sems, recv_sems, j, (px, py, c)).start()
        token[...] = jnp.zeros_like(token)

    land = pltpu.with_memory_space_constraint(lax.empty(part.shape, part.dtype), pltpu.HBM)
    extra = () if order_after is None else (order_after,)
    return _pcall(
        body, name=name,
        out_shape=(pltpu.SemaphoreType.DMA((3,)), pltpu.SemaphoreType.DMA((3,)), pltpu.HBM(part.shape, part.dtype),
                   pltpu.HBM(part.shape, part.dtype), jax.ShapeDtypeStruct((8, 128), F32)),
        in_specs=(_HBM, _HBM) + (_ANY,) * len(extra),
        out_specs=(_SEM, _SEM, _HBM, _HBM, pl.BlockSpec(memory_space=pltpu.VMEM)),
        input_output_aliases={0: 2, 1: 3}, compiler_params=pltpu.CompilerParams(has_side_effects=_EFFECT),
    )(pltpu.with_memory_space_constraint(part, pltpu.HBM), land, *extra)


def _chips_wait(send_sems, recv_sems, p_thru, land_thru, after, name):
    def body(p_ref, land_ref, send_sems, recv_sems, after_ref, p_out, land_out):
        x, y, c, _ = _mesh_pos()
        for j, (px, py) in enumerate(_other_chips(x, y)):
            q = 2 * px + py
            cp = _rcopy(p_ref.at[q], land_ref.at[q], send_sems, recv_sems, j, (px, py, c))
            cp.wait_send()
            cp.wait_recv()

    return _pcall(
        body, name=name, out_shape=(pltpu.HBM(p_thru.shape, p_thru.dtype), pltpu.HBM(p_thru.shape, p_thru.dtype)),
        in_specs=(_HBM, _HBM, _SEM, _SEM, _ANY), out_specs=(_HBM, _HBM), input_output_aliases={0: 0, 1: 1},
        compiler_params=pltpu.CompilerParams(has_side_effects=_EFFECT),
    )(p_thru, land_thru, send_sems, recv_sems, after)


def _sum_chips(part, land, qvec, name):
    n, R, C = part.shape
    tr = _pick(R, _ROW_TILES)

    def body(q_ref, p_ref, *refs):
        o_ref = refs[n]
        acc = None
        for q in range(n):
            term = jnp.where(q_ref[0] == q, p_ref[...], refs[q][...]).astype(F32)
            acc = term if acc is None else acc + term
        o_ref[...] = acc

    def land_spec(q):
        return pl.BlockSpec((None, tr, C), lambda i, m: (jnp.where(m[0] == q, (q + 1) % n, q), i, 0))

    grid_spec = pltpu.PrefetchScalarGridSpec(
        num_scalar_prefetch=1, grid=(R // tr,),
        in_specs=[pl.BlockSpec((None, tr, C), lambda i, m: (m[0], i, 0))] + [land_spec(q) for q in range(n)],
        out_specs=pl.BlockSpec((tr, C), lambda i, m: (i, 0)))
    return _pcall(body, grid_spec=grid_spec, out_shape=jax.ShapeDtypeStruct((R, C), F32), name=name)(qvec, part, *([land] * n))


def _exchange_small(gsmall):
    def body(s_ref, srecv_ref, send_sems, recv_sems, loc_sem):
        x, y, c, me = _mesh_pos()
        loc = pltpu.make_async_copy(s_ref, srecv_ref.at[me], loc_sem.at[0])
        loc.start()
        sends, recvs = [], []
        for k in range(1, N_DEV):
            to, pidx = _peer(x, y, c, k)
            sends.append(_rcopy(s_ref, srecv_ref.at[me], send_sems, recv_sems, k - 1, to))
            recvs.append(_rcopy(s_ref, srecv_ref.at[pidx], send_sems, recv_sems, k - 1, to))
        for cp in sends:
            cp.start()
        for cp in recvs:
            cp.wait_recv()
        for cp in sends:
            cp.wait_send()
        loc.wait()

    return _pcall(
        body, in_specs=[_ANY], out_specs=_ANY, out_shape=jax.ShapeDtypeStruct((N_DEV,) + gsmall.shape, gsmall.dtype),
        scratch_shapes=[pltpu.SemaphoreType.DMA((7,)), pltpu.SemaphoreType.DMA((7,)), pltpu.SemaphoreType.DMA((1,))],
        name="exchange_small",
    )(gsmall)


def _sum_slots(recv, name):
    n, R, C = recv.shape
    tr = _pick(R, _ROW_TILES)

    def body(r_ref, o_ref):
        acc = r_ref[0].astype(F32)
        for s in range(1, n):
            acc = acc + r_ref[s].astype(F32)
        o_ref[...] = acc

    return _pcall(body, grid=(R // tr,), in_specs=[pl.BlockSpec((n, tr, C), lambda i: (0, i, 0))],
                  out_specs=pl.BlockSpec((tr, C), lambda i: (i, 0)), out_shape=jax.ShapeDtypeStruct((R, C), F32), name=name)(recv)


def _adamw(w, g, m, v, name, order_after=None):
    shape = w.shape
    w2, g2, m2, v2 = (a.reshape(-1, shape[-1]) for a in (w, g, m, v))
    R, C = w2.shape
    tr = _pick(R, (256, 128, 64, 32, 16, 8)) if R > 256 else R
    n_in = 4 + int(order_after is not None)

    def body(*refs):
        w_ref, g_ref, m_ref, v_ref = refs[:4]
        d_ref, nm_ref, nv_ref = refs[n_in:]
        gg = g_ref[...]
        nm = ADAM_B1 * m_ref[...] + (1.0 - ADAM_B1) * gg
        nv = ADAM_B2 * v_ref[...] + (1.0 - ADAM_B2) * jnp.square(gg)
        m_hat = nm / (1.0 - ADAM_B1 ** ADAM_STEP)
        v_hat = nv / (1.0 - ADAM_B2 ** ADAM_STEP)
        d_ref[...] = -ADAM_LR * (m_hat / (jnp.sqrt(v_hat) + ADAM_EPS) + ADAM_WD * w_ref[...])
        nm_ref[...] = nm
        nv_ref[...] = nv

    spec = pl.BlockSpec((tr, C), lambda i: (i, 0))
    in_specs, args = [spec] * 4, [w2, g2, m2, v2]
    if order_after is not None:
        in_specs.append(pl.BlockSpec(order_after.shape, lambda i: (0, 0)))
        args.append(order_after)
    d, nm, nv = _pcall(body, grid=(R // tr,), in_specs=in_specs, out_specs=[spec] * 3,
                       out_shape=[jax.ShapeDtypeStruct((R, C), F32)] * 3, name=name)(*args)
    return d.reshape(shape), nm.reshape(shape), nv.reshape(shape)


def _to_rows(name, w):
    w = w[0]
    if name in ("ffn1_w_in", "ffn2_w_in", "w_in"):
        return w.T
    if name in ("w_uq", "w_ukv", "ple_w_proj"):
        return w.T.reshape(-1, 1024)
    return w


def _from_rows(name, g, shape):
    if name in ("ffn1_w_in", "ffn2_w_in", "w_in"):
        return g.T.reshape(shape)
    if name in ("w_uq", "w_ukv", "ple_w_proj"):
        return g.reshape(-1, shape[1]).T.reshape(shape)
    return g.reshape(shape)


def _unpack(wall, group):
    out, off = {}, 0
    for name, r in group:
        out[name] = wall[:, off:off + r, :].reshape(N_DEV * r, 1024)
        off += _pad16(r)
    return out


def _w_in_internal(wt):
    return jnp.concatenate([wt[0:6144], wt[6720:8768], wt[6144:6720], jnp.zeros((P_W - W_IN_COLS, 1024), wt.dtype)], axis=0)


def _w_in_external(d):
    return jnp.concatenate([d[0:6144], d[8192:8768], d[6144:8192]], axis=0)


def _rope_tables(positions):
    pos = positions[0].astype(F32)

    def cs(half):
        inv = ROPE_BASE ** (-jnp.arange(half, dtype=F32) / half)
        ang = pos[:, None] * inv
        return jnp.cos(ang), jnp.sin(ang)

    c, s = cs(64)
    c2, s2 = cs(32)
    return (jnp.concatenate([c, c], axis=1), jnp.concatenate([-s, s], axis=1),
            jnp.concatenate([c2, c2, c2, c2], axis=1), jnp.concatenate([-s2, s2, -s2, s2], axis=1))


def _local_step(x, p, positions, target, W, ln_g, ln_b, gn_g, qg, kvg, early=None, rest_weights=None, start_token=None):
    T = x.shape[0]
    tabs = _rope_tables(positions)
    rc = _ret_consts()
    lg = [ln_g[i:i + 1] for i in range(4)]
    lb = [ln_b[i:i + 1] for i in range(4)]
    pb = p.astype(BF16)

    hs1, gu1, *xb = _ffn_in(x, W["ffn1_w_in"], "ffn1_in", order_after=start_token)
    xb = xb[0] if xb else x
    f1, h1, h1b = _mm(hs1, W["ffn1_w_out"], name="ffn1_out", tm=LN_TM, epilogue=_ln_epilogue(x, 0.5, lg[0], lb[0]))
    if rest_weights is not None:
        W = {**W, **rest_weights(h1b)}
    w_in_t = _w_in_internal(W["w_in"])
    wuq = W["w_uq"].reshape(1536, LORA).reshape(HEADS, 192, LORA)
    wuq = jnp.concatenate([wuq[:, :128].reshape(1024, LORA), wuq[:, 128:].reshape(512, LORA)], axis=0)
    wukv = W["w_ukv"].reshape(2048, LORA).reshape(HEADS, 2, 128, LORA).transpose(1, 0, 2, 3).reshape(2048, LORA)
    wp_t = W["ple_w_proj"].reshape(1024, D_PLE)
    proj = _mm(h1b, w_in_t, tb=True, out_dtype=BF16, name="mixer_in")
    rq, rk, qn, kvn, kpe = _mixer_prep_fwd(proj, tabs, qg, kvg, T)
    y, yr, states = _ret_fwd(rq, rk, proj, proj, gn_g, rc, T)
    y_ret = _mm(yr, W["w_ret_o"], out_dtype=BF16, name="ret_o")
    qnope, qpe = _mm(qn, wuq, tb=True, name="mla_uq", tm=LN_TM, tn=1536, epilogue=_q_assemble_epilogue(tabs))
    kv = _mm(kvn, wukv, tb=True, out_dtype=BF16, name="mla_ukv")
    o, lse = _attn_fwd(qnope, qpe, kv, kpe, T)
    y_mla, mix = _mm(o, W["w_mla_o"], name="mla_o", tm=LN_TM, epilogue=_mix_epilogue(proj, y_ret))
    mixed, h2, h2b = _mm(mix, W["w_out"], name="mixer_out", tm=LN_TM, epilogue=_ln_epilogue(h1, 1.0, lg[1], lb[1]))
    hs2, gu2 = _ffn_in(h2b, W["ffn2_w_in"], "ffn2_in")
    f2, h3, h3b = _mm(hs2, W["ffn2_w_out"], name="ffn2_out", tm=LN_TM, epilogue=_ln_epilogue(h2, 0.5, lg[2], lb[2]))
    pp = _mm(pb, wp_t, tb=True, name="ple_proj")

    G = {}
    dh3_a, dgl, dpp, dg3, db3, loss = _mm(h3b, W["ple_w_gate"], name="ple_gate", tm=LN_TM,
                                          epilogue=_head_epilogue(h3, pp, target, lg[3], lb[3]))
    G["ple_w_gate"] = _mm(h3b, dgl, ta=True, name="d_ple_gate")
    G["ple_w_proj"] = _mm(dpp, pb, ta=True, name="d_ple_proj")
    dh2_a, df2, dg2, db2 = _mm(dgl, W["ple_w_gate"], tb=True, add=dh3_a, name="dh3", tm=LN_TM,
                               epilogue=_ln_bwd_epilogue(h2, f2, 0.5, lg[2], lb[2]))
    G["ffn2_w_out"] = _mm(hs2, df2, ta=True, name="d_ffn2_out")
    da2 = _ffn_act_bwd(df2, W["ffn2_w_out"], gu2, "ffn2_act_bwd")
    G["ffn2_w_in"] = _mm(da2, h2b, ta=True, name="d_ffn2_in")
    dh1_a, dmixed, dg1, db1 = _mm(da2, W["ffn2_w_in"], add=dh2_a, name="dh2", tm=LN_TM,
                                  epilogue=_ln_bwd_epilogue(h1, mixed, 1.0, lg[1], lb[1]))
    G["w_out"] = _mm(mix, dmixed, ta=True, name="d_mixer_out")
    dgr, dgm, dy_ret, dy_mla = _mm(dmixed, W["w_out"], tb=True, name="dmix", tm=LN_TM,
                                   epilogue=_mix_bwd_epilogue(proj, y_ret, y_mla))
    G["w_mla_o"] = _mm(o, dy_mla, ta=True, name="d_mla_o")
    dob, delta = _mm(dy_mla, W["w_mla_o"], tb=True, name="do", tm=LN_TM, epilogue=_delta_epilogue(o))
    dqn_f, dqpe_f, dkn, dkpe_all, dv = _attn_bwd(qnope, qpe, kv, kpe, dob, lse, delta, T)
    dq_n, dq_r = _q_assemble_bwd(dqn_f, dqpe_f, tabs, T)
    g_uq = jnp.concatenate([_mm(dq_n, qn, ta=True, name="d_uq_nope"), _mm(dq_r, qn, ta=True, name="d_uq_rope")], axis=0)
    g_uq = jnp.concatenate([g_uq[:1024].reshape(HEADS, 128, LORA), g_uq[1024:].reshape(HEADS, 64, LORA)], axis=1)
    G["w_uq"] = g_uq.reshape(1536 * LORA // 1024, 1024)
    dqn = _mm(dq_r, wuq[1024:], add=_mm(dq_n, wuq[:1024], name="dqn_a"), name="dqn_b")
    g_ukv = jnp.stack([_mm(dkn, kvn, ta=True, name="d_ukv_k"), _mm(dv, kvn, ta=True, name="d_ukv_v")], axis=0)
    G["w_ukv"] = g_ukv.reshape(2, HEADS, 128, LORA).transpose(1, 0, 2, 3).reshape(2048 * LORA // 1024, 1024)
    dkvn = _mm(dv, wukv[1024:], add=_mm(dkn, wukv[:1024], name="dkvn_a"), name="dkvn_b")
    dcq, dckv, dkpe, dqg, dkvg = _rms_bwd(proj, dqn, dkvn, dkpe_all, tabs, qg, kvg, T)
    G["w_ret_o"] = _mm(yr, dy_ret, ta=True, name="d_ret_o")
    dyr = _mm(dy_ret, W["w_ret_o"], tb=True, out_dtype=BF16, name="dyr")
    drq, drk, drv, drg, dgn = _ret_bwd(rq, rk, proj, y, proj, gn_g, states, dyr, rc, tabs, T)
    dproj = jnp.concatenate([drq, drk, drv, drg, dgr, dgm, dcq, dckv, dkpe, jnp.zeros((T, P_W - P_KPE - 128), BF16)], axis=1)
    G["w_in"] = _w_in_external(_mm(dproj, h1b, ta=True, name="d_mixer_in"))
    lg0 = lg[0] if early is None else lg[0] + early(G)[0:1, 0:1]
    dx_a, df1, dg0, db0 = _mm(dproj, w_in_t, add=dh1_a, name="dh1", tm=LN_TM,
                              epilogue=_ln_bwd_epilogue(x, f1, 0.5, lg0, lb[0]))
    G["ffn1_w_out"] = _mm(hs1, df1, ta=True, name="d_ffn1_out")
    da1 = _ffn_act_bwd(df1, W["ffn1_w_out"], gu1, "ffn1_act_bwd")
    G["ffn1_w_in"] = _mm(da1, xb, ta=True, name="d_ffn1_in")
    grad_x = _mm(da1, W["ffn1_w_in"], add=dx_a, name="grad_x")

    small = dict(ln_g=jnp.concatenate([dg0, dg1, dg2, dg3], axis=0), ln_b=jnp.concatenate([db0, db1, db2, db3], axis=0),
                 ret_gn_g=dgn, q_norm_g=dqg, kv_norm_g=dkvg)
    return loss, grad_x, G, small


def kernel(x, p, positions, ln_g, ln_b, ffn1_w_in, ffn1_w_out, w_in, ret_gn_g, w_ret_o, q_norm_g, kv_norm_g, w_uq, w_ukv, w_mla_o, w_out, ffn2_w_in, ffn2_w_out, ple_w_gate, ple_w_proj, loss_target, m_ln_g, m_ln_b, m_ffn1_w_in, m_ffn1_w_out, m_w_in, m_ret_gn_g, m_w_ret_o, m_q_norm_g, m_kv_norm_g, m_w_uq, m_w_ukv, m_w_mla_o, m_w_out, m_ffn2_w_in, m_ffn2_w_out, m_ple_w_gate, m_ple_w_proj, v_ln_g, v_ln_b, v_ffn1_w_in, v_ffn1_w_out, v_w_in, v_ret_gn_g, v_w_ret_o, v_q_norm_g, v_kv_norm_g, v_w_uq, v_w_ukv, v_w_mla_o, v_w_out, v_ffn2_w_in, v_ffn2_w_out, v_ple_w_gate, v_ple_w_proj):
    names = ("ln_g", "ln_b", "ffn1_w_in", "ffn1_w_out", "w_in", "ret_gn_g", "w_ret_o", "q_norm_g", "kv_norm_g", "w_uq", "w_ukv",
             "w_mla_o", "w_out", "ffn2_w_in", "ffn2_w_out", "ple_w_gate", "ple_w_proj")
    ws = dict(zip(names, (ln_g, ln_b, ffn1_w_in, ffn1_w_out, w_in, ret_gn_g, w_ret_o, q_norm_g, kv_norm_g, w_uq, w_ukv, w_mla_o,
                          w_out, ffn2_w_in, ffn2_w_out, ple_w_gate, ple_w_proj)))
    ms = dict(zip(names, (m_ln_g, m_ln_b, m_ffn1_w_in, m_ffn1_w_out, m_w_in, m_ret_gn_g, m_w_ret_o, m_q_norm_g, m_kv_norm_g, m_w_uq,
                          m_w_ukv, m_w_mla_o, m_w_out, m_ffn2_w_in, m_ffn2_w_out, m_ple_w_gate, m_ple_w_proj)))
    vs = dict(zip(names, (v_ln_g, v_ln_b, v_ffn1_w_in, v_ffn1_w_out, v_w_in, v_ret_gn_g, v_w_ret_o, v_q_norm_g, v_kv_norm_g, v_w_uq,
                          v_w_ukv, v_w_mla_o, v_w_out, v_ffn2_w_in, v_ffn2_w_out, v_ple_w_gate, v_ple_w_proj)))

    parts = []
    for name, r in PACK:
        rows = _to_rows(name, ws[name])
        if _pad16(r) != r:
            rows = jnp.concatenate([rows, jnp.zeros((_pad16(r) - r, 1024), F32)], axis=0)
        parts.append(rows)
    wsh_first = jnp.concatenate(parts[:len(PACK_LATE)], axis=0).astype(BF16)
    wsh_rest = jnp.concatenate(parts[len(PACK_LATE):], axis=0).astype(BF16)
    ssh = jnp.concatenate([ln_g[0], ln_b[0]], axis=0)
    wall_first, sall = _all_gather(wsh_first, ssh)
    *gather_handles, start_token = _gather_start(wsh_rest, wall_first)
    W = _unpack(wall_first, PACK_LATE)
    ln_full = sall.reshape(N_DEV, 2, 4, 128).transpose(1, 2, 0, 3).reshape(2, 4, 1024)

    def rest_weights(after):
        w_thru, land = _gather_wait(*gather_handles, after)
        return _unpack(_gather_finish(w_thru, land), PACK_EARLY)

    cvec = lax.axis_index("c").astype(jnp.int32).reshape(1)
    qvec = (2 * lax.axis_index("x") + lax.axis_index("y")).astype(jnp.int32).reshape(1)

    def chip_partials(G, group, tag):
        gparts = []
        for name, r in group:
            g = G[name].reshape(N_DEV, r, 1024)
            if _pad16(r) != r:
                g = jnp.concatenate([g, jnp.zeros((N_DEV, _pad16(r) - r, 1024), g.dtype)], axis=1)
            gparts.append(g)
        gfull = jnp.concatenate(gparts, axis=1).astype(BF16)
        g4 = gfull.reshape(4, 2, gfull.shape[1], 1024)
        return _sum_sibling(g4, _exchange_sibling(g4, "exchange_sibling_" + tag), cvec, "sum_sibling_" + tag)

    in_flight = []

    def early(G):
        *handles, token = _chips_start(chip_partials(G, PACK_EARLY, "early"), "chips_start_early")
        in_flight.append(handles)
        return token

    loss_p, grad_x, G, small = _local_step(x[0], p[0, 0], positions, loss_target[0], W, ln_full[0], ln_full[1],
                                           ret_gn_g, q_norm_g, kv_norm_g, early=early, rest_weights=rest_weights,
                                           start_token=start_token)

    part_e, land_e = _chips_wait(*in_flight[0], grad_x, "chips_wait_early")
    gsh_early = _sum_chips(part_e, land_e, qvec, "sum_grads_early")
    pad256 = lambda a: jnp.concatenate([a, jnp.zeros((1, 1024 - a.shape[1]), F32)], axis=1)
    gsmall = jnp.concatenate([small["ln_g"], small["ln_b"], small["ret_gn_g"].reshape(2, 1024), pad256(small["q_norm_g"]),
                              pad256(small["kv_norm_g"]), jnp.zeros((SMALL_ROWS - 12, 1024), F32)], axis=0)
    srecv = _exchange_small(gsmall)
    part_l = chip_partials(G, PACK_LATE, "late")
    *late_handles, late_token = _chips_start(part_l, "chips_start_late", order_after=srecv)
    ssum = _sum_slots(srecv, "sum_small_grads")

    def unpack_grads(group, gsh):
        out, off = {}, 0
        for name, r in group:
            out[name] = _from_rows(name, gsh[off:off + r], ws[name].shape)
            off += _pad16(r)
        return out

    grads = unpack_grads(PACK_EARLY, gsh_early)
    me = 4 * lax.axis_index("x") + 2 * lax.axis_index("y") + lax.axis_index("c")
    grads["ln_g"] = lax.dynamic_slice(ssum[0:4], (0, me * 128), (4, 128)).reshape(1, 4, 128)
    grads["ln_b"] = lax.dynamic_slice(ssum[4:8], (0, me * 128), (4, 128)).reshape(1, 4, 128)
    grads["ret_gn_g"] = ssum[8:10].reshape(1, 2048)
    grads["q_norm_g"] = ssum[10:11, :256]
    grads["kv_norm_g"] = ssum[11:12, :256]

    delta, new_m, new_v = {}, {}, {}
    late_names = [n for n, _ in PACK_LATE]
    for name in names:
        if name not in late_names:
            delta[name], new_m[name], new_v[name] = _adamw(ws[name], grads[name], ms[name], vs[name], "adamw_" + name,
                                                           order_after=late_token)
    part_l, land_l = _chips_wait(*late_handles, new_v["w_in"], "chips_wait_late")
    grads.update(unpack_grads(PACK_LATE, _sum_chips(part_l, land_l, qvec, "sum_grads_late")))
    for name in late_names:
        delta[name], new_m[name], new_v[name] = _adamw(ws[name], grads[name], ms[name], vs[name], "adamw_" + name)

    loss = lax.psum(loss_p[0, 0], ("x", "y", "c"))
    return (loss, grad_x[None], *[grads[n] for n in names], *[delta[n] for n in names],
            *[new_m[n] for n in names], *[new_v[n] for n in names])
```

```python
import math

import jax
import jax.numpy as jnp
from jax import lax
from jax.experimental import pallas as pl
from jax.experimental.pallas import tpu as pltpu

F32 = jnp.float32
BF16 = jnp.bfloat16

N_DEV = 8
D = 1024
D_FF = 2816
D_PLE = 256
CHUNK = 64
HEADS = 8
RET_DK = 128
RET_DV = 256
MLA_NOPE = 128
MLA_ROPE = 64
MLA_DV = 128
LORA = 256
ROPE_BASE = 10000.0
EPS = 1e-5
ALPHA = 2.0 ** 0.25
RET_SCALE = RET_DK ** -0.5
MLA_SCALE = (MLA_NOPE + MLA_ROPE) ** -0.5
NEG = -1e30

ADAM_LR = 0.001
ADAM_B1 = 0.9
ADAM_B2 = 0.999
ADAM_EPS = 1e-08
ADAM_WD = 0.01
ADAM_STEP = 10

P_RQ, P_RK, P_RV, P_RG, P_GR, P_GM, P_CQ, P_CKV, P_KPE, P_W = 0, 1024, 2048, 4096, 6144, 7168, 8192, 8448, 8704, 8960
W_IN_COLS = 8768
RET_L = 256
ATT_TF = 1024
ATT_TB = 1024
ATT_HP = 4
LOG2E = math.log2(math.e)
Q_PRESCALE = MLA_SCALE * LOG2E

PACK = (("ffn1_w_in", 704), ("ffn1_w_out", 352), ("w_in", 1096), ("w_ret_o", 256), ("w_uq", 48), ("w_ukv", 64),
        ("w_mla_o", 128), ("w_out", 128), ("ffn2_w_in", 704), ("ffn2_w_out", 352), ("ple_w_gate", 128), ("ple_w_proj", 32))


def _pad16(r):
    return -(-r // 16) * 16


PACK_LATE = PACK[:2]
PACK_EARLY = PACK[2:]
SMALL_ROWS = 16


def _pcall(body, **kw):
    return pl.pallas_call(body, **kw)


def _pick(dim, prefs):
    for p in prefs:
        if dim % p == 0:
            return p
    return dim


def _sigmoid(x):
    return 1.0 / (1.0 + jnp.exp(-x))


def _silu(x):
    return x * _sigmoid(x)


def _ln(r, g, b):
    mu = jnp.mean(r, axis=-1, keepdims=True)
    var = jnp.mean(jnp.square(r - mu), axis=-1, keepdims=True)
    return (r - mu) * lax.rsqrt(var + EPS) * g + b


def _rms(x, g):
    return x * lax.rsqrt(jnp.mean(jnp.square(x), axis=-1, keepdims=True) + EPS) * g


def _dot(a, b, ca, cb):
    return lax.dot_general(a, b, (((ca,), (cb,)), ((), ())), preferred_element_type=F32)


def _accum(ref, val, first=None):
    @pl.when(pl.program_id(0) == 0 if first is None else first)
    def _():
        ref[...] = jnp.zeros_like(ref)

    ref[...] += val


def _mm(a, b, *, ta=False, tb=False, add=None, out_dtype=None, name, tm=None, tn=None, tk=None, epilogue=None):
    parts = a.shape[0] if a.ndim == 3 else 1
    ar, ac = a.shape[-2], a.shape[-1]
    out_dtype = out_dtype or (BF16 if ta else F32)
    if ta:
        K, M = ar, ac * parts
    else:
        M, K = ar, ac * parts
    if tb:
        N, K2 = b.shape
    else:
        K2, N = b.shape
    assert K == K2, (a.shape, b.shape, ta, tb)
    big = (1024, 1408, 1280, 768, 512, 256, 128)
    tm = tm or _pick(ac if (ta and parts > 1) else M, big)
    tn = tn or (N if N <= 1024 else _pick(N, big))
    kdim = ac if (not ta and parts > 1) else K
    tk = tk or (kdim if kdim <= 2816 and parts == 1 else
                _pick(kdim, (2048, 1408, 1280, 1024, 512) if tm <= 1024 else (1024, 1408, 1280, 512)))
    nk = K // tk
    grid = (M // tm, N // tn, nk)
    if parts > 1 and ta:
        per = ac // tm
        a_spec = pl.BlockSpec((None, tk, tm), lambda i, j, k: (i // per, k, i % per))
    elif parts > 1:
        per = ac // tk
        a_spec = pl.BlockSpec((None, tm, tk), lambda i, j, k: (k // per, i, k % per))
    else:
        a_spec = pl.BlockSpec((tk, tm), lambda i, j, k: (k, i)) if ta else pl.BlockSpec((tm, tk), lambda i, j, k: (i, k))
    b_spec = pl.BlockSpec((tn, tk), lambda i, j, k: (j, k)) if tb else pl.BlockSpec((tk, tn), lambda i, j, k: (k, j))
    o_spec = pl.BlockSpec((tm, tn), lambda i, j, k: (i, j))
    ca, cb = (0 if ta else 1), (1 if tb else 0)
    has_add = add is not None
    n_in = 2 + int(has_add)
    if epilogue is not None:
        assert tn == N and not ta
        ep_fn, ep_rows, ep_whole, ep_outs, ep_accs = epilogue
        n_ep_in = len(ep_rows) + len(ep_whole)
        n_out = len(ep_outs) + len(ep_accs)
    else:
        n_ep_in, n_out = 0, 1

    def body(*refs):
        a_ref, b_ref = refs[0], refs[1]
        add_ref = refs[2] if has_add else None
        o_ref = refs[n_in + n_ep_in]
        first_row_tile = pl.program_id(0) == 0

        def finish(r):
            if has_add:
                r = r + add_ref[...].astype(F32)
            if epilogue is not None:
                ep_fn(r, refs[n_in:n_in + n_ep_in], refs[n_in + n_ep_in:n_in + n_ep_in + n_out], first_row_tile)
            else:
                o_ref[...] = r.astype(out_dtype)

        if nk == 1:
            finish(_dot(a_ref[...], b_ref[...], ca, cb))
            return
        acc_ref = refs[-1]
        k = pl.program_id(2)

        @pl.when(k == 0)
        def _():
            acc_ref[...] = jnp.zeros_like(acc_ref)

        acc_ref[...] += _dot(a_ref[...], b_ref[...], ca, cb)

        @pl.when(k == nk - 1)
        def _():
            finish(acc_ref[...])

    in_specs = [a_spec, b_spec] + ([o_spec] if has_add else [])
    args = (a, b) + ((add,) if has_add else ())
    out_specs, out_shape = o_spec, jax.ShapeDtypeStruct((M, N), out_dtype)
    if epilogue is not None:
        for row_in in ep_rows:
            arr, col_block, width = (tuple(row_in) + (N,))[:3] if isinstance(row_in, tuple) else (row_in, 0, N)
            in_specs.append(pl.BlockSpec((tm, width), lambda i, j, k, _c=col_block: (i, _c)))
            args += (arr,)
        in_specs += [pl.BlockSpec(w.shape, lambda i, j, k, _n=w.ndim: (0,) * _n) for w in ep_whole]
        args += tuple(ep_whole)
        outs = [o if isinstance(o, tuple) else (o, N) for o in ep_outs]
        out_specs = ([pl.BlockSpec((tm, w), lambda i, j, k: (i, 0)) for _, w in outs]
                     + [pl.BlockSpec((r, N), lambda i, j, k: (0, 0)) for r in ep_accs])
        out_shape = [jax.ShapeDtypeStruct((M, w), dt) for dt, w in outs] + [jax.ShapeDtypeStruct((r, N), F32) for r in ep_accs]
    return _pcall(
        body, grid=grid, in_specs=in_specs, out_specs=out_specs, out_shape=out_shape,
        scratch_shapes=[pltpu.VMEM((tm, tn), F32)] if nk > 1 else [], name=name,
        compiler_params=pltpu.CompilerParams(dimension_semantics=("arbitrary" if epilogue is not None else "parallel", "parallel", "arbitrary")),
    )(*args)


def _ln_epilogue(res, c, g, b):
    def fn(r, ins, outs, first):
        res_ref, g_ref, b_ref = ins
        f_ref, h_ref, hb_ref = outs
        fb = r.astype(BF16)
        h = _ln(ALPHA * res_ref[...] + c * fb.astype(F32), g_ref[...], b_ref[...])
        f_ref[...] = fb
        h_ref[...] = h
        hb_ref[...] = h.astype(BF16)

    return (fn, [res], [g, b], (BF16, F32, BF16), ())


def _ln_bwd_epilogue(res, f, c, g, b):
    def fn(r, ins, outs, first):
        res_ref, f_ref, g_ref, b_ref = ins
        dr_ref, df_ref, dg_ref, db_ref = outs
        pre = ALPHA * res_ref[...] + c * f_ref[...].astype(F32)
        xc = pre - jnp.mean(pre, axis=-1, keepdims=True)
        rstd = lax.rsqrt(jnp.mean(jnp.square(xc), axis=-1, keepdims=True) + EPS)
        xhat = xc * rstd
        dyg = r * g_ref[...]
        dpre = rstd * (dyg - jnp.mean(dyg, axis=-1, keepdims=True) - xhat * jnp.mean(dyg * xhat, axis=-1, keepdims=True))
        dr_ref[...] = ALPHA * dpre
        df_ref[...] = (c * dpre).astype(BF16)
        _accum(dg_ref, jnp.sum(r * xhat, axis=0, keepdims=True), first)
        _accum(db_ref, jnp.sum(r, axis=0, keepdims=True), first)

    return (fn, [res, f], [g, b], (F32, BF16), (1, 1))


def _rows(body, T, tm, ins, outs, name, accs=()):
    in_specs, args = [], []
    for arr, w, cb in ins:
        if w is None:
            in_specs.append(pl.BlockSpec(arr.shape, lambda i, _n=arr.ndim: (0,) * _n))
        else:
            in_specs.append(pl.BlockSpec((tm, w), lambda i, _cb=cb: (i, _cb)))
        args.append(arr)
    out_specs = [pl.BlockSpec((tm, w), lambda i: (i, 0)) for w, _ in outs]
    out_shape = [jax.ShapeDtypeStruct((T, w), dt) for w, dt in outs]
    for r, w in accs:
        out_specs.append(pl.BlockSpec((r, w), lambda i: (0, 0)))
        out_shape.append(jax.ShapeDtypeStruct((r, w), F32))
    return _pcall(
        body, grid=(T // tm,), in_specs=in_specs, out_specs=out_specs, out_shape=out_shape, name=name,
        compiler_params=pltpu.CompilerParams(dimension_semantics=("arbitrary",)),
    )(*args)


FFN_TN = 1408
FFN_TM = 512
LN_TM = 512


def _ffn_in(x, wt, name, order_after=None):
    T = x.shape[0]
    tm, tn = min(FFN_TM, T), FFN_TN
    nj = D_FF // tn
    emit_xb = x.dtype != BF16
    n_in = 3 + int(order_after is not None)

    def body(*refs):
        x_ref, wg_ref, wu_ref = refs[:3]
        hs_ref, gu_ref = refs[n_in], refs[n_in + 1]
        xv = x_ref[...].astype(BF16)
        g = _dot(xv, wg_ref[...], 1, 1)
        u = _dot(xv, wu_ref[...], 1, 1)
        hs_ref[...] = (_silu(g) * u).astype(BF16)
        gu_ref[0] = g.astype(BF16)
        gu_ref[1] = u.astype(BF16)
        if emit_xb:
            refs[n_in + 2][...] = xv

    in_specs = [pl.BlockSpec((tm, D), lambda i, j: (i, 0)), pl.BlockSpec((tn, D), lambda i, j: (j, 0)),
                pl.BlockSpec((tn, D), lambda i, j: (j + nj, 0))]
    args = [x, wt, wt]
    if order_after is not None:
        in_specs.append(pl.BlockSpec(order_after.shape, lambda i, j: (0, 0)))
        args.append(order_after)
    out_specs = [pl.BlockSpec((tm, tn), lambda i, j: (i, j)), pl.BlockSpec((2, tm, tn), lambda i, j: (0, i, j))]
    out_shape = [jax.ShapeDtypeStruct((T, D_FF), BF16), jax.ShapeDtypeStruct((2, T, D_FF), BF16)]
    if emit_xb:
        out_specs.append(pl.BlockSpec((tm, D), lambda i, j: (i, 0)))
        out_shape.append(jax.ShapeDtypeStruct((T, D), BF16))
    return _pcall(
        body, grid=(T // tm, nj), in_specs=in_specs, out_specs=out_specs, out_shape=out_shape, name=name,
        compiler_params=pltpu.CompilerParams(dimension_semantics=("parallel", "arbitrary")),
    )(*args)


def _ffn_act_bwd(df, wo, gu, name):
    T = df.shape[0]
    tm, tn = min(FFN_TM, T), FFN_TN

    def body(d_ref, w_ref, gu_ref, o_ref):
        dhs = _dot(d_ref[...], w_ref[...], 1, 1)
        g, u = gu_ref[0].astype(F32), gu_ref[1].astype(F32)
        sig = _sigmoid(g)
        act = g * sig
        o_ref[0] = (dhs * u * (sig + act * (1.0 - sig))).astype(BF16)
        o_ref[1] = (dhs * act).astype(BF16)

    return _pcall(
        body, grid=(T // tm, D_FF // tn),
        in_specs=[pl.BlockSpec((tm, D), lambda i, j: (i, 0)), pl.BlockSpec((tn, D), lambda i, j: (j, 0)),
                  pl.BlockSpec((2, tm, tn), lambda i, j: (0, i, j))],
        out_specs=pl.BlockSpec((2, tm, tn), lambda i, j: (0, i, j)),
        out_shape=jax.ShapeDtypeStruct((2, T, D_FF), BF16), name=name,
        compiler_params=pltpu.CompilerParams(dimension_semantics=("parallel", "parallel")),
    )(df, wo, gu)


def _rope128(t, cos, sin_s):
    return t * cos + pltpu.roll(t, 64, 1) * sin_s


def _rope128_t(g, cos, sin_s):
    return g * cos - pltpu.roll(g, 64, 1) * sin_s


def _partner32(t):
    lane = lax.broadcasted_iota(jnp.int32, t.shape, 1)
    return jnp.where((lane & 32) == 0, pltpu.roll(t, 96, 1), pltpu.roll(t, 32, 1))


def _rope64(t, cos, sin_s):
    return t * cos + _partner32(t) * sin_s


def _rope64_t(g, cos, sin_s):
    return g * cos - _partner32(g) * sin_s


def _mixer_prep_fwd(proj, tabs, qg, kvg, T):
    cos128, sin128, cos64, sin64 = tabs

    def body(rq_ref, rk_ref, cq_ref, ckv_ref, kpe_ref, c1_ref, s1_ref, c2_ref, s2_ref, qg_ref, kvg_ref,
             oq_ref, ok_ref, oqn_ref, okvn_ref, okpe_ref):
        c1, s1 = c1_ref[...], s1_ref[...]
        for h in range(HEADS):
            sl = slice(h * RET_DK, (h + 1) * RET_DK)
            oq_ref[:, sl] = _rope128(rq_ref[:, sl].astype(F32), c1, s1).astype(BF16)
            ok_ref[:, sl] = (_rope128(rk_ref[:, sl].astype(F32), c1, s1) * RET_SCALE).astype(BF16)
        oqn_ref[...] = _rms(cq_ref[...].astype(F32), qg_ref[...]).astype(BF16)
        okvn_ref[...] = _rms(ckv_ref[...].astype(F32), kvg_ref[...]).astype(BF16)
        okpe_ref[...] = _rope64(kpe_ref[...].astype(F32), c2_ref[...], s2_ref[...]).astype(BF16)

    ins = [(proj, 1024, 0), (proj, 1024, 1), (proj, 256, P_CQ // 256), (proj, 256, P_CKV // 256),
           (proj, 128, P_KPE // 128), (cos128, 128, 0), (sin128, 128, 0), (cos64, 128, 0), (sin64, 128, 0),
           (qg, None, None), (kvg, None, None)]
    outs = [(1024, BF16), (1024, BF16), (LORA, BF16), (LORA, BF16), (128, BF16)]
    return _rows(body, T, 256, ins, outs, "mixer_prep_fwd")


def _rms_bwd(proj, dqn, dkvn, dkpe_all, tabs, qg, kvg, T):
    _, _, cos64, sin64 = tabs

    def body(cq_ref, ckv_ref, dq_ref, dkv_ref, dk_ref, c2_ref, s2_ref, qg_ref, kvg_ref, ocq_ref, ockv_ref, okpe_ref,
             dqg_ref, dkvg_ref):
        _, vjp = jax.vjp(_rms, cq_ref[...].astype(F32), qg_ref[...])
        dx, dg = vjp(dq_ref[...])
        ocq_ref[...] = dx.astype(BF16)
        _accum(dqg_ref, dg)
        _, vjp = jax.vjp(_rms, ckv_ref[...].astype(F32), kvg_ref[...])
        dx, dg = vjp(dkv_ref[...])
        ockv_ref[...] = dx.astype(BF16)
        _accum(dkvg_ref, dg)
        g = dk_ref[:, 0:128]
        for h in range(1, HEADS):
            g = g + dk_ref[:, h * 128:(h + 1) * 128]
        lane = lax.broadcasted_iota(jnp.int32, g.shape, 1)
        g = jnp.where(lane < MLA_ROPE, g, 0.0)
        okpe_ref[...] = _rope64_t(g, c2_ref[...], s2_ref[...]).astype(BF16)

    ins = [(proj, 256, P_CQ // 256), (proj, 256, P_CKV // 256), (dqn, LORA, 0), (dkvn, LORA, 0), (dkpe_all, 1024, 0),
           (cos64, 128, 0), (sin64, 128, 0), (qg, None, None), (kvg, None, None)]
    return _rows(body, T, 256, ins, [(LORA, BF16), (LORA, BF16), (128, BF16)], "rms_bwd", accs=[(1, LORA), (1, LORA)])


def _gn_gate_bwd(y, rg, g, d):
    xc = y - jnp.mean(y, axis=-1, keepdims=True)
    rstd = lax.rsqrt(jnp.mean(jnp.square(xc), axis=-1, keepdims=True) + EPS)
    xhat = xc * rstd
    sig = _sigmoid(rg)
    act = rg * sig
    dn = d * act
    drg = d * (xhat * g) * (sig + act * (1.0 - sig))
    dxh = dn * g
    dy = rstd * (dxh - jnp.mean(dxh, axis=-1, keepdims=True) - xhat * jnp.mean(dxh * xhat, axis=-1, keepdims=True))
    return dy, drg, jnp.sum(dn * xhat, axis=0, keepdims=True)


def _gn_gate(y, rg, g):
    mu = jnp.mean(y, axis=-1, keepdims=True)
    var = jnp.mean(jnp.square(y - mu), axis=-1, keepdims=True)
    return _silu(rg) * ((y - mu) * lax.rsqrt(var + EPS) * g)


def _q_assemble_epilogue(tabs):
    _, _, cos64, sin64 = tabs

    def fn(q, ins, outs, first):
        c_ref, s_ref = ins
        on_ref, op_ref = outs
        on_ref[...] = (q[:, :1024] * Q_PRESCALE).astype(BF16)
        c, s = c_ref[...], s_ref[...]
        lane = lax.broadcasted_iota(jnp.int32, c.shape, 1)
        for j in range(HEADS // 2):
            r = _rope64(q[:, 1024 + 128 * j:1024 + 128 * (j + 1)], c, s) * Q_PRESCALE
            op_ref[:, 256 * j:256 * j + 128] = jnp.where(lane < 64, r, 0.0).astype(BF16)
            op_ref[:, 256 * j + 128:256 * j + 256] = jnp.where(lane < 64, pltpu.roll(r, 64, 1), 0.0).astype(BF16)

    return (fn, [(cos64, 0, 128), (sin64, 0, 128)], [], ((BF16, 1024), (BF16, 1024)), ())


def _q_assemble_bwd(dqn, dqpe, tabs, T):
    _, _, cos64, sin64 = tabs

    def body(dn_ref, dp_ref, c_ref, s_ref, on_ref, op_ref):
        on_ref[...] = dn_ref[...].astype(BF16)
        c, s = c_ref[...], s_ref[...]
        lane = lax.broadcasted_iota(jnp.int32, c.shape, 1)
        for j in range(HEADS // 2):
            g = jnp.where(lane < 64, dp_ref[:, 256 * j:256 * j + 128], pltpu.roll(dp_ref[:, 256 * j + 128:256 * j + 256], 64, 1))
            op_ref[:, 128 * j:128 * (j + 1)] = _rope64_t(g, c, s).astype(BF16)

    return _rows(body, T, 256, [(dqn, 1024, 0), (dqpe, 1024, 0), (cos64, 128, 0), (sin64, 128, 0)],
                 [(1024, BF16), (512, BF16)], "q_assemble_bwd")


def _mix_fn(gr, gm, yr, ym):
    return _sigmoid(gr) * yr + _sigmoid(gm) * ym


def _mix_epilogue(proj, y_ret):
    def fn(r, ins, outs, first):
        gr_ref, gm_ref, yr_ref = ins
        ym_ref, mix_ref = outs
        ym_ref[...] = r.astype(BF16)
        mix_ref[...] = _mix_fn(gr_ref[...].astype(F32), gm_ref[...].astype(F32), yr_ref[...].astype(F32), r).astype(BF16)

    return (fn, [(proj, P_GR // D), (proj, P_GM // D), y_ret], [], (BF16, BF16), ())


def _mix_bwd_epilogue(proj, y_ret, y_mla):
    def fn(r, ins, outs, first):
        gr_ref, gm_ref, yr_ref, ym_ref = ins
        dgr_ref, dgm_ref, dyr_ref, dym_ref = outs
        sr, sm = _sigmoid(gr_ref[...].astype(F32)), _sigmoid(gm_ref[...].astype(F32))
        dgr_ref[...] = (r * yr_ref[...].astype(F32) * (sr * (1.0 - sr))).astype(BF16)
        dgm_ref[...] = (r * ym_ref[...].astype(F32) * (sm * (1.0 - sm))).astype(BF16)
        dyr_ref[...] = (r * sr).astype(BF16)
        dym_ref[...] = (r * sm).astype(BF16)

    return (fn, [(proj, P_GR // D), (proj, P_GM // D), y_ret, y_mla], [], (BF16,) * 4, ())


def _head_epilogue(h3, pp, tgt, g, b):
    def fn(r, ins, outs, first):
        h_ref, pp_ref, t_ref, g_ref, b_ref = ins
        dh_ref, dgl_ref, dpp_ref, dg_ref, db_ref, loss_ref = outs

        sg, pp, gain = _sigmoid(r), pp_ref[...], g_ref[...]
        pre = ALPHA * h_ref[...] + sg * pp
        xc = pre - jnp.mean(pre, axis=-1, keepdims=True)
        rstd = lax.rsqrt(jnp.mean(jnp.square(xc), axis=-1, keepdims=True) + EPS)
        xhat = xc * rstd
        err = xhat * gain + b_ref[...] - t_ref[...]
        dy = err * (1.0 / D)
        dyg = dy * gain
        dpre = rstd * (dyg - jnp.mean(dyg, axis=-1, keepdims=True) - xhat * jnp.mean(dyg * xhat, axis=-1, keepdims=True))
        dh_ref[...] = ALPHA * dpre
        dgl_ref[...] = (dpre * pp * (sg * (1.0 - sg))).astype(BF16)
        dpp_ref[...] = (dpre * sg).astype(BF16)
        _accum(dg_ref, jnp.sum(dy * xhat, axis=0, keepdims=True), first)
        _accum(db_ref, jnp.sum(dy, axis=0, keepdims=True), first)
        part = 0.5 * jnp.sum(jnp.mean(jnp.square(err), axis=-1, keepdims=True), axis=0, keepdims=True)
        _accum(loss_ref, jnp.broadcast_to(part, loss_ref.shape), first)

    return (fn, [h3, pp, tgt], [g, b], (F32, BF16, BF16), (1, 1, 8))


def _delta_epilogue(o):
    def fn(r, ins, outs, first):
        (o_ref,) = ins
        db_ref, dl_ref = outs
        db_ref[...] = r.astype(BF16)
        for h in range(HEADS):
            sl = slice(h * MLA_DV, (h + 1) * MLA_DV)
            dl = jnp.sum(r[:, sl] * o_ref[:, sl].astype(F32), axis=-1, keepdims=True)
            dl_ref[:, sl] = jnp.broadcast_to(dl, (r.shape[0], MLA_DV))

    return (fn, [o], [], (BF16, F32), ())


def _ret_consts():
    L = RET_L
    lg = jnp.log(1.0 - 2.0 ** (-5.0 - jnp.arange(HEADS, dtype=F32)))[:, None, None]
    idx = jnp.arange(L, dtype=F32)
    ch = jnp.arange(L) // CHUNK
    dist = idx[:, None] - idx[None, :]
    same = (ch[:, None] == ch[None, :])[None]
    earlier = (ch[None, :] < ch[:, None])[None]
    dm = jnp.where(same, jnp.exp(lg * jnp.abs(dist)[None]), jnp.where(earlier, jnp.exp(lg * dist[None]), 0.0))
    xi = jnp.broadcast_to(jnp.exp(lg * (idx + 1.0)[None, :, None]), (HEADS, L, 128))
    zeta = jnp.broadcast_to(jnp.exp(lg * (L - 1.0 - idx)[None, :, None]), (HEADS, L, 128))
    gl = jnp.broadcast_to(jnp.exp(lg * float(L)), (HEADS, 8, 128))
    return dm.astype(F32), xi.astype(F32), zeta.astype(F32), gl.astype(F32)


def _whole(arr):
    return pl.BlockSpec(arr.shape, lambda n, _nd=arr.ndim: (0,) * _nd)


def _ret_fwd(q, k, v, proj, gn_g, consts, T):
    dm, xi, zeta, gl = consts
    L = RET_L
    n_sc = T // L

    def body(q_ref, k_ref, v_ref, rg_ref, g_ref, dm_ref, xi_ref, ze_ref, gl_ref, y_ref, yr_ref, s_ref, st_ref):
        @pl.when(pl.program_id(0) == 0)
        def _():
            st_ref[...] = jnp.zeros_like(st_ref)

        for h in range(HEADS):
            ks, vs = slice(h * RET_DK, (h + 1) * RET_DK), slice(h * RET_DV, (h + 1) * RET_DV)
            qq, kk, vv = q_ref[:, ks], k_ref[:, ks], v_ref[:, vs]
            st = st_ref[h]
            s_ref[h, 0] = st
            p = (_dot(qq, kk, 1, 1) * dm_ref[h]).astype(BF16)
            cross = _dot(qq, st.astype(BF16), 1, 0)
            xi_c = jnp.concatenate([xi_ref[h], xi_ref[h]], axis=1)
            y = _dot(p, vv, 1, 0) + cross * xi_c
            yb = y.astype(BF16)
            y_ref[:, vs] = yb
            yr_ref[:, vs] = _gn_gate(yb.astype(F32), rg_ref[:, vs].astype(F32), g_ref[:, vs]).astype(BF16)
            kz = (kk.astype(F32) * ze_ref[h]).astype(BF16)
            gl2 = jnp.concatenate([gl_ref[h, 0:1, :], gl_ref[h, 0:1, :]], axis=1)
            st_ref[h] = st * gl2 + _dot(kz, vv, 0, 0)

    return _pcall(
        body, grid=(n_sc,),
        in_specs=[pl.BlockSpec((L, 1024), lambda n: (n, 0)), pl.BlockSpec((L, 1024), lambda n: (n, 0)),
                  pl.BlockSpec((L, 2048), lambda n: (n, P_RV // 2048)), pl.BlockSpec((L, 2048), lambda n: (n, P_RG // 2048)),
                  _whole(gn_g), _whole(dm), _whole(xi), _whole(zeta), _whole(gl)],
        out_specs=[pl.BlockSpec((L, 2048), lambda n: (n, 0)), pl.BlockSpec((L, 2048), lambda n: (n, 0)),
                   pl.BlockSpec((HEADS, 1, 128, 256), lambda n: (0, n, 0, 0))],
        out_shape=[jax.ShapeDtypeStruct((T, HEADS * RET_DV), BF16), jax.ShapeDtypeStruct((T, HEADS * RET_DV), BF16),
                   jax.ShapeDtypeStruct((HEADS, n_sc, 128, 256), F32)],
        scratch_shapes=[pltpu.VMEM((HEADS, 128, 256), F32)], name="ret_fwd",
        compiler_params=pltpu.CompilerParams(dimension_semantics=("arbitrary",)),
    )(q, k, v, proj, gn_g, dm, xi, zeta, gl)


def _ret_bwd(q, k, v, y, proj, gn_g, states, dyr, consts, tabs, T):
    dm, xi, zeta, gl = consts
    cos128, sin128, _, _ = tabs
    L = RET_L
    n_sc = T // L

    def body(q_ref, k_ref, v_ref, y_ref, rg_ref, g_ref, d_ref, s_ref, dm_ref, xi_ref, ze_ref, gl_ref, c_ref, sn_ref,
             dq_ref, dk_ref, dv_ref, drg_ref, dg_ref, gs_ref):
        @pl.when(pl.program_id(0) == 0)
        def _():
            gs_ref[...] = jnp.zeros_like(gs_ref)

        c, sn = c_ref[...], sn_ref[...]
        dgs = []
        for h in range(HEADS):
            ks, vs = slice(h * RET_DK, (h + 1) * RET_DK), slice(h * RET_DV, (h + 1) * RET_DV)
            dy, drg, dg = _gn_gate_bwd(y_ref[:, vs].astype(F32), rg_ref[:, vs].astype(F32), g_ref[:, vs], d_ref[:, vs].astype(F32))
            drg_ref[:, vs] = drg.astype(BF16)
            dgs.append(dg)
            qq, kk, vv, dyy = q_ref[:, ks], k_ref[:, ks], v_ref[:, vs], dy.astype(BF16)
            dmm = dm_ref[h]
            gb = gs_ref[h].astype(BF16)
            sb = s_ref[h, 0].astype(BF16)
            xi_c = jnp.concatenate([xi_ref[h], xi_ref[h]], axis=1)
            pb = (_dot(qq, kk, 1, 1) * dmm).astype(BF16)
            kz = (kk.astype(F32) * ze_ref[h]).astype(BF16)
            dv_ref[:, vs] = (_dot(pb, dyy, 0, 0) + _dot(kz, gb, 1, 0)).astype(BF16)
            da = (_dot(dyy, vv, 1, 1) * dmm).astype(BF16)
            dyx = (dyy.astype(F32) * xi_c).astype(BF16)
            dq = _dot(da, kk, 1, 0) + _dot(dyx, sb, 1, 1)
            dk = _dot(da, qq, 0, 0) + _dot(vv, gb, 1, 1) * ze_ref[h]
            dq_ref[:, ks] = _rope128_t(dq, c, sn).astype(BF16)
            dk_ref[:, ks] = (_rope128_t(dk, c, sn) * RET_SCALE).astype(BF16)
            gl2 = jnp.concatenate([gl_ref[h, 0:1, :], gl_ref[h, 0:1, :]], axis=1)
            gs_ref[h] = gs_ref[h] * gl2 + _dot(qq, dyx, 0, 0)
        _accum(dg_ref, jnp.concatenate(dgs, axis=1))

    rev = lambda n: n_sc - 1 - n
    return _pcall(
        body, grid=(n_sc,),
        in_specs=[pl.BlockSpec((L, 1024), lambda n: (rev(n), 0)), pl.BlockSpec((L, 1024), lambda n: (rev(n), 0)),
                  pl.BlockSpec((L, 2048), lambda n: (rev(n), P_RV // 2048)), pl.BlockSpec((L, 2048), lambda n: (rev(n), 0)),
                  pl.BlockSpec((L, 2048), lambda n: (rev(n), P_RG // 2048)), _whole(gn_g),
                  pl.BlockSpec((L, 2048), lambda n: (rev(n), 0)),
                  pl.BlockSpec((HEADS, 1, 128, 256), lambda n: (0, rev(n), 0, 0)),
                  _whole(dm), _whole(xi), _whole(zeta), _whole(gl),
                  pl.BlockSpec((L, 128), lambda n: (rev(n), 0)), pl.BlockSpec((L, 128), lambda n: (rev(n), 0))],
        out_specs=[pl.BlockSpec((L, 1024), lambda n: (rev(n), 0)), pl.BlockSpec((L, 1024), lambda n: (rev(n), 0)),
                   pl.BlockSpec((L, 2048), lambda n: (rev(n), 0)), pl.BlockSpec((L, 2048), lambda n: (rev(n), 0)),
                   pl.BlockSpec((1, 2048), lambda n: (0, 0))],
        out_shape=[jax.ShapeDtypeStruct((T, 1024), BF16), jax.ShapeDtypeStruct((T, 1024), BF16), jax.ShapeDtypeStruct((T, 2048), BF16),
                   jax.ShapeDtypeStruct((T, 2048), BF16), jax.ShapeDtypeStruct((1, 2048), F32)],
        scratch_shapes=[pltpu.VMEM((HEADS, 128, 256), F32)], name="ret_bwd",
        compiler_params=pltpu.CompilerParams(dimension_semantics=("arbitrary",)),
    )(q, k, v, y, proj, gn_g, dyr, states, dm, xi, zeta, gl, cos128, sin128)


def _diag_mask(nrows, ncols, row0):
    row = row0 + lax.broadcasted_iota(jnp.int32, (nrows, ncols), 0)
    col = lax.broadcasted_iota(jnp.int32, (nrows, ncols), 1)
    return lax.shift_right_logical(col, 6) <= lax.shift_right_logical(row, 6)


def _diag_spans(t):
    h = t // 2
    return ((0, h, h), (h, h, t)) if h % 128 == 0 else ((0, t, t),)


def _tri_steps(nb, by_key):
    if by_key:
        pairs = [(i, j) for j in range(nb) for i in range(j, nb)]
    else:
        pairs = [(i, j) for i in range(nb) for j in range(i + 1)]
    return jnp.array([a for a, _ in pairs], jnp.int32), jnp.array([b for _, b in pairs], jnp.int32)


def _attn_fwd(qn, qpe, kv, kpe, T):
    t = min(ATT_TF, T)
    nb = T // t
    ii, jj = _tri_steps(nb, by_key=False)

    hp = ATT_HP
    w = 128 * hp

    def body(ii_ref, jj_ref, qn_ref, qp_ref, kn_ref, kp_ref, v_ref, o_ref, lse_ref, m_sc, l_sc, acc_sc):
        st = pl.program_id(1)
        i, j = ii_ref[st], jj_ref[st]

        @pl.when(j == 0)
        def _():
            m_sc[...] = jnp.full_like(m_sc, NEG)
            l_sc[...] = jnp.zeros_like(l_sc)
            acc_sc[...] = jnp.zeros_like(acc_sc)

        def update(diag):
            kp = kp_ref[...]
            for hh in range(hp):
                sl = slice(128 * hh, 128 * (hh + 1))
                q = jnp.concatenate([qn_ref[:, sl], qp_ref[:, sl]], axis=1)
                k = jnp.concatenate([kn_ref[:, sl], kp], axis=1)
                s = _dot(q, k, 1, 1)
                if diag:
                    s = jnp.where(_diag_mask(t, t, 0), s, NEG)
                m_prev = m_sc[:, sl]
                m_new = jnp.maximum(m_prev, jnp.max(s, axis=1, keepdims=True))
                a = jnp.exp2(m_prev - m_new)
                p = jnp.exp2(s - m_new[:, 0:1])
                l_sc[:, sl] = a * l_sc[:, sl] + jnp.sum(p, axis=1, keepdims=True)
                acc_sc[:, sl] = a * acc_sc[:, sl] + _dot(p.astype(BF16), v_ref[:, sl], 1, 0)
                m_sc[:, sl] = m_new

        @pl.when(j < i)
        def _():
            update(False)

        @pl.when(j == i)
        def _():
            update(True)
            o_ref[...] = (acc_sc[...] / l_sc[...]).astype(BF16)
            lse_ref[...] = m_sc[...] + jnp.log2(l_sc[...])

    qs = pl.BlockSpec((t, w), lambda h, s, ii, jj: (ii[s], h))
    grid_spec = pltpu.PrefetchScalarGridSpec(
        num_scalar_prefetch=2, grid=(HEADS // hp, int(ii.shape[0])),
        in_specs=[qs, qs, pl.BlockSpec((t, w), lambda h, s, ii, jj: (jj[s], h)), pl.BlockSpec((t, 128), lambda h, s, ii, jj: (jj[s], 0)),
                  pl.BlockSpec((t, w), lambda h, s, ii, jj: (jj[s], HEADS // hp + h))],
        out_specs=[qs, qs],
        scratch_shapes=[pltpu.VMEM((t, w), F32), pltpu.VMEM((t, w), F32), pltpu.VMEM((t, w), F32)])
    return _pcall(
        body, grid_spec=grid_spec, out_shape=[jax.ShapeDtypeStruct((T, D), BF16), jax.ShapeDtypeStruct((T, D), F32)], name="attn_fwd",
        compiler_params=pltpu.CompilerParams(dimension_semantics=("arbitrary", "arbitrary")),
    )(ii, jj, qn, qpe, kv, kpe, kv)


def _attn_bwd(qn, qpe, kv, kpe, do, lse, delta, T):
    t = min(ATT_TB, T)
    nb = T // t
    ii, jj = _tri_steps(nb, by_key=True)

    def body(ii_ref, jj_ref, qn_ref, qp_ref, kn_ref, kp_ref, v_ref, do_ref, lse_ref, dl_ref,
             dqn_ref, dqp_ref, dkn_ref, dkp_ref, dv_ref, dk_sc, dv_sc):
        st = pl.program_id(1)
        i, j = ii_ref[st], jj_ref[st]

        @pl.when(st == 0)
        def _():
            dqn_ref[...] = jnp.zeros_like(dqn_ref)
            dqp_ref[...] = jnp.zeros_like(dqp_ref)

        @pl.when(i == j)
        def _():
            dk_sc[...] = jnp.zeros_like(dk_sc)
            dv_sc[...] = jnp.zeros_like(dv_sc)

        def update(diag):
            for row0, nr, nkeys in (_diag_spans(t) if diag else ((0, t, t),)):
                rows = slice(row0, row0 + nr)
                q = jnp.concatenate([qn_ref[rows, :], qp_ref[rows, :]], axis=1)
                k = jnp.concatenate([kn_ref[:nkeys, :], kp_ref[:nkeys, :]], axis=1)
                dob = do_ref[rows, :]
                s = _dot(q, k, 1, 1)
                if diag:
                    s = jnp.where(_diag_mask(nr, nkeys, row0), s, NEG)
                p = jnp.exp2(s - lse_ref[rows, 0:1])
                dv_sc[:nkeys, :] += _dot(p.astype(BF16), dob, 0, 0)
                dp = _dot(dob, v_ref[:nkeys, :], 1, 1)
                ds = (p * (dp - dl_ref[rows, 0:1])).astype(BF16)
                dk_sc[:nkeys, :] += _dot(ds, q, 0, 0)
                dq = _dot(ds, k, 1, 0) * MLA_SCALE
                grows = pl.ds(pl.multiple_of(i * t + row0, nr), nr)
                dqn_ref[grows, :] += dq[:, :128]
                dqp_ref[grows, :] += dq[:, 128:]

        @pl.when(i > j)
        def _():
            update(False)

        @pl.when(i == j)
        def _():
            update(True)

        @pl.when(i == nb - 1)
        def _():
            dkn_ref[...] = (dk_sc[:, :128] * (1.0 / LOG2E)).astype(BF16)
            dkp_ref[...] = dk_sc[:, 128:] * (1.0 / LOG2E)
            dv_ref[...] = dv_sc[...].astype(BF16)

    qs = pl.BlockSpec((t, 128), lambda h, s, ii, jj: (ii[s], h))
    ks = pl.BlockSpec((t, 128), lambda h, s, ii, jj: (jj[s], h))
    hs = pl.BlockSpec((T, 128), lambda h, s, ii, jj: (0, h))
    grid_spec = pltpu.PrefetchScalarGridSpec(
        num_scalar_prefetch=2, grid=(HEADS, int(ii.shape[0])),
        in_specs=[qs, qs, ks, pl.BlockSpec((t, 128), lambda h, s, ii, jj: (jj[s], 0)),
                  pl.BlockSpec((t, 128), lambda h, s, ii, jj: (jj[s], HEADS + h)), qs, qs, qs],
        out_specs=[hs, hs, ks, ks, ks],
        scratch_shapes=[pltpu.VMEM((t, 256), F32), pltpu.VMEM((t, 128), F32)])
    return _pcall(
        body, grid_spec=grid_spec,
        out_shape=[jax.ShapeDtypeStruct((T, D), F32), jax.ShapeDtypeStruct((T, D), F32), jax.ShapeDtypeStruct((T, D), BF16),
                   jax.ShapeDtypeStruct((T, D), F32), jax.ShapeDtypeStruct((T, D), BF16)], name="attn_bwd",
        compiler_params=pltpu.CompilerParams(dimension_semantics=("arbitrary", "arbitrary")),
    )(ii, jj, qn, qpe, kv, kpe, kv, do, lse, delta)


def _mesh_pos():
    x, y, c = lax.axis_index("x"), lax.axis_index("y"), lax.axis_index("c")
    return x, y, c, 4 * x + 2 * y + c


def _peer(x, y, c, k):
    px, py, pc = (x + ((k >> 2) & 1)) % 2, (y + ((k >> 1) & 1)) % 2, (c + (k & 1)) % 2
    return (px, py, pc), 4 * px + 2 * py + pc


_ANY = pl.BlockSpec(memory_space=pl.ANY)


def _rcopy(src, dst, send_sems, recv_sems, k, to):
    return pltpu.make_async_remote_copy(src_ref=src, dst_ref=dst, send_sem=send_sems.at[k], recv_sem=recv_sems.at[k],
                                        device_id=to, device_id_type=pl.DeviceIdType.MESH)


def _all_gather(wsh, ssh):
    def body(w_ref, s_ref, wall_ref, sall_ref, send_sems, recv_sems, loc_sems):
        x, y, c, me = _mesh_pos()
        sib = (x, y, 1 - c)
        chips = [(1 - x, y), (x, 1 - y), (1 - x, 1 - y)]
        slot = lambda px, py, pc: 4 * px + 2 * py + pc
        loc = [pltpu.make_async_copy(s_ref, sall_ref.at[me], loc_sems.at[0])]
        for cp in loc:
            cp.start()
        sends, fwd_waits = [], []
        for n, (src, dst) in enumerate(((w_ref, wall_ref), (s_ref, sall_ref))):
            o = 7 * n
            sends.append(_rcopy(src, dst.at[me], send_sems, recv_sems, o, sib))
            for j, chip in enumerate(chips):
                sends.append(_rcopy(src, dst.at[me], send_sems, recv_sems, o + 1 + j, (*chip, c)))
        for cp in sends:
            cp.start()
        for n, (src, dst) in enumerate(((w_ref, wall_ref), (s_ref, sall_ref))):
            o = 7 * n
            for j, chip in enumerate(chips):
                got = dst.at[slot(*chip, c)]
                _rcopy(src, got, send_sems, recv_sems, o + 1 + j, sib).wait_recv()
                fw = _rcopy(got, got, send_sems, recv_sems, o + 4 + j, sib)
                fw.start()
                sends.append(fw)
            fwd_waits.append(_rcopy(src, dst.at[slot(x, y, 1 - c)], send_sems, recv_sems, o, sib))
            for j, chip in enumerate(chips):
                fwd_waits.append(_rcopy(src, dst.at[slot(*chip, 1 - c)], send_sems, recv_sems, o + 4 + j, sib))
        for cp in fwd_waits:
            cp.wait_recv()
        for cp in sends:
            cp.wait_send()
        for cp in loc:
            cp.wait()

    wall, sall = _pcall(
        body, in_specs=[_ANY, _ANY], out_specs=[_ANY, _ANY],
        out_shape=[jax.ShapeDtypeStruct((N_DEV,) + wsh.shape, wsh.dtype), jax.ShapeDtypeStruct((N_DEV,) + ssh.shape, ssh.dtype)],
        scratch_shapes=[pltpu.SemaphoreType.DMA((14,)), pltpu.SemaphoreType.DMA((14,)), pltpu.SemaphoreType.DMA((1,))],
        name="all_gather_weights",
    )(wsh, ssh)
    me = 4 * lax.axis_index("x") + 2 * lax.axis_index("y") + lax.axis_index("c")
    return lax.dynamic_update_index_in_dim(wall, wsh, me, axis=0), sall


_HBM = pl.BlockSpec(memory_space=pltpu.HBM)
_SEM = pl.BlockSpec(memory_space=pltpu.SEMAPHORE)
_EFFECT = pltpu.SideEffectType.DATAFLOW_SIDE_EFFECTING


def _other_chips(x, y):
    return [(1 - x, y), (x, 1 - y), (1 - x, 1 - y)]


def _gather_start(wsh, order_after):
    def body(w_ref, land_ref, dep_ref, send_sems, recv_sems, w_thru, land_thru, token):
        x, y, c, me = _mesh_pos()
        _rcopy(w_ref, land_ref.at[me], send_sems, recv_sems, 0, (x, y, 1 - c)).start()
        for j, chip in enumerate(_other_chips(x, y)):
            _rcopy(w_ref, land_ref.at[me], send_sems, recv_sems, 1 + j, (*chip, c)).start()
        token[...] = jnp.zeros_like(token)

    shape = (N_DEV,) + wsh.shape
    land = pltpu.with_memory_space_constraint(lax.empty(shape, wsh.dtype), pltpu.HBM)
    return _pcall(
        body, name="gather_start",
        out_shape=(pltpu.SemaphoreType.DMA((4,)), pltpu.SemaphoreType.DMA((4,)), pltpu.HBM(wsh.shape, wsh.dtype),
                   pltpu.HBM(shape, wsh.dtype), jax.ShapeDtypeStruct((8, 128), F32)),
        in_specs=(_HBM, _HBM, _ANY), out_specs=(_SEM, _SEM, _HBM, _HBM, pl.BlockSpec(memory_space=pltpu.VMEM)),
        input_output_aliases={0: 2, 1: 3}, compiler_params=pltpu.CompilerParams(has_side_effects=_EFFECT),
    )(pltpu.with_memory_space_constraint(wsh, pltpu.HBM), land, order_after)


def _gather_wait(send_sems, recv_sems, w_thru, land_thru, after):
    def body(w_ref, land_ref, send_sems, recv_sems, after_ref, w_out, land_out):
        x, y, c, _ = _mesh_pos()
        senders = [(x, y, 1 - c)] + [(*chip, c) for chip in _other_chips(x, y)]
        for k, (px, py, pc) in enumerate(senders):
            cp = _rcopy(w_ref, land_ref.at[4 * px + 2 * py + pc], send_sems, recv_sems, k, (px, py, pc))
            cp.wait_send()
            cp.wait_recv()

    return _pcall(
        body, name="gather_wait", out_shape=(pltpu.HBM(w_thru.shape, w_thru.dtype), pltpu.HBM(land_thru.shape, land_thru.dtype)),
        in_specs=(_HBM, _HBM, _SEM, _SEM, _ANY), out_specs=(_HBM, _HBM), input_output_aliases={0: 0, 1: 1},
        compiler_params=pltpu.CompilerParams(has_side_effects=_EFFECT),
    )(w_thru, land_thru, send_sems, recv_sems, after)


def _gather_finish(wsh, land):
    def body(land_ref, out_ref, send_sems, recv_sems):
        x, y, c, _ = _mesh_pos()
        sib = (x, y, 1 - c)
        sends, recvs = [], []
        for j, (px, py) in enumerate(_other_chips(x, y)):
            mine, theirs = 4 * px + 2 * py + c, 4 * px + 2 * py + (1 - c)
            sends.append(_rcopy(land_ref.at[mine], out_ref.at[mine], send_sems, recv_sems, j, sib))
            recvs.append(_rcopy(land_ref.at[theirs], out_ref.at[theirs], send_sems, recv_sems, j, sib))
        for cp in sends:
            cp.start()
        for cp in recvs:
            cp.wait_recv()
        for cp in sends:
            cp.wait_send()

    done = _pcall(
        body, in_specs=[_ANY], out_specs=_ANY, out_shape=jax.ShapeDtypeStruct(land.shape, land.dtype),
        input_output_aliases={0: 0},
        scratch_shapes=[pltpu.SemaphoreType.DMA((3,)), pltpu.SemaphoreType.DMA((3,))], name="gather_finish",
    )(land)
    me = 4 * lax.axis_index("x") + 2 * lax.axis_index("y") + lax.axis_index("c")
    return lax.dynamic_update_index_in_dim(done, wsh, me, axis=0)


_ROW_TILES = (400, 368, 352, 256, 128, 16, 8)


def _exchange_sibling(g4, name):
    def body(g_ref, r_ref, send_sems, recv_sems):
        x, y, c, _ = _mesh_pos()
        sib = (x, y, 1 - c)
        cps = [_rcopy(g_ref.at[q, 1 - c], r_ref.at[q], send_sems, recv_sems, q, sib) for q in range(4)]
        for cp in cps:
            cp.start()
        for cp in cps:
            cp.wait()

    n, _, R, C = g4.shape
    return _pcall(
        body, in_specs=[_ANY], out_specs=_ANY, out_shape=jax.ShapeDtypeStruct((n, R, C), g4.dtype),
        scratch_shapes=[pltpu.SemaphoreType.DMA((4,)), pltpu.SemaphoreType.DMA((4,))], name=name,
    )(g4)


def _sum_sibling(g4, rsib, cvec, name):
    n, _, R, C = g4.shape
    tr = _pick(R, _ROW_TILES)

    def body(c_ref, g_ref, r_ref, o_ref):
        o_ref[...] = (g_ref[...].astype(F32) + r_ref[...].astype(F32)).astype(o_ref.dtype)

    grid_spec = pltpu.PrefetchScalarGridSpec(
        num_scalar_prefetch=1, grid=(n, R // tr),
        in_specs=[pl.BlockSpec((None, None, tr, C), lambda q, i, cr: (q, cr[0], i, 0)), pl.BlockSpec((None, tr, C), lambda q, i, cr: (q, i, 0))],
        out_specs=pl.BlockSpec((None, tr, C), lambda q, i, cr: (q, i, 0)))
    return _pcall(body, grid_spec=grid_spec, out_shape=jax.ShapeDtypeStruct((n, R, C), g4.dtype), name=name)(cvec, g4, rsib)


def _chips_start(part, name, order_after=None):
    n_in = 2 + int(order_after is not None)

    def body(*refs):
        p_ref, land_ref = refs[:2]
        send_sems, recv_sems, token = refs[n_in], refs[n_in + 1], refs[n_in + 4]
        x, y, c, _ = _mesh_pos()
        myq = 2 * x + y
        for j, (px, py) in enumerate(_other_chips(x, y)):
            _rcopy(p_ref.at[2 * px + py], land_ref.at[myq], send_sems, recv_sems, j, (px, py, c)).start()
        token[...] = jnp.zeros_like(token)

    land = pltpu.with_memory_space_constraint(lax.empty(part.shape, part.dtype), pltpu.HBM)
    extra = () if order_after is None else (order_after,)
    return _pcall(
        body, name=name,
        out_shape=(pltpu.SemaphoreType.DMA((3,)), pltpu.SemaphoreType.DMA((3,)), pltpu.HBM(part.shape, part.dtype),
                   pltpu.HBM(part.shape, part.dtype), jax.ShapeDtypeStruct((8, 128), F32)),
        in_specs=(_HBM, _HBM) + (_ANY,) * len(extra),
        out_specs=(_SEM, _SEM, _HBM, _HBM, pl.BlockSpec(memory_space=pltpu.VMEM)),
        input_output_aliases={0: 2, 1: 3}, compiler_params=pltpu.CompilerParams(has_side_effects=_EFFECT),
    )(pltpu.with_memory_space_constraint(part, pltpu.HBM), land, *extra)


def _chips_wait(send_sems, recv_sems, p_thru, land_thru, after, name):
    def body(p_ref, land_ref, send_sems, recv_sems, after_ref, p_out, land_out):
        x, y, c, _ = _mesh_pos()
        for j, (px, py) in enumerate(_other_chips(x, y)):
            q = 2 * px + py
            cp = _rcopy(p_ref.at[q], land_ref.at[q], send_sems, recv_sems, j, (px, py, c))
            cp.wait_send()
            cp.wait_recv()

    return _pcall(
        body, name=name, out_shape=(pltpu.HBM(p_thru.shape, p_thru.dtype), pltpu.HBM(p_thru.shape, p_thru.dtype)),
        in_specs=(_HBM, _HBM, _SEM, _SEM, _ANY), out_specs=(_HBM, _HBM), input_output_aliases={0: 0, 1: 1},
        compiler_params=pltpu.CompilerParams(has_side_effects=_EFFECT),
    )(p_thru, land_thru, send_sems, recv_sems, after)


def _sum_chips(part, land, qvec, name):
    n, R, C = part.shape
    tr = _pick(R, _ROW_TILES)

    def body(q_ref, p_ref, *refs):
        o_ref = refs[n]
        acc = None
        for q in range(n):
            term = jnp.where(q_ref[0] == q, p_ref[...], refs[q][...]).astype(F32)
            acc = term if acc is None else acc + term
        o_ref[...] = acc

    def land_spec(q):
        return pl.BlockSpec((None, tr, C), lambda i, m: (jnp.where(m[0] == q, (q + 1) % n, q), i, 0))

    grid_spec = pltpu.PrefetchScalarGridSpec(
        num_scalar_prefetch=1, grid=(R // tr,),
        in_specs=[pl.BlockSpec((None, tr, C), lambda i, m: (m[0], i, 0))] + [land_spec(q) for q in range(n)],
        out_specs=pl.BlockSpec((tr, C), lambda i, m: (i, 0)))
    return _pcall(body, grid_spec=grid_spec, out_shape=jax.ShapeDtypeStruct((R, C), F32), name=name)(qvec, part, *([land] * n))


def _exchange_small(gsmall):
    def body(s_ref, srecv_ref, send_sems, recv_sems, loc_sem):
        x, y, c, me = _mesh_pos()
        loc = pltpu.make_async_copy(s_ref, srecv_ref.at[me], loc_sem.at[0])
        loc.start()
        sends, recvs = [], []
        for k in range(1, N_DEV):
            to, pidx = _peer(x, y, c, k)
            sends.append(_rcopy(s_ref, srecv_ref.at[me], send_sems, recv_sems, k - 1, to))
            recvs.append(_rcopy(s_ref, srecv_ref.at[pidx], send_sems, recv_sems, k - 1, to))
        for cp in sends:
            cp.start()
        for cp in recvs:
            cp.wait_recv()
        for cp in sends:
            cp.wait_send()
        loc.wait()

    return _pcall(
        body, in_specs=[_ANY], out_specs=_ANY, out_shape=jax.ShapeDtypeStruct((N_DEV,) + gsmall.shape, gsmall.dtype),
        scratch_shapes=[pltpu.SemaphoreType.DMA((7,)), pltpu.SemaphoreType.DMA((7,)), pltpu.SemaphoreType.DMA((1,))],
        name="exchange_small",
    )(gsmall)


def _sum_slots(recv, name):
    n, R, C = recv.shape
    tr = _pick(R, _ROW_TILES)

    def body(r_ref, o_ref):
        acc = r_ref[0].astype(F32)
        for s in range(1, n):
            acc = acc + r_ref[s].astype(F32)
        o_ref[...] = acc

    return _pcall(body, grid=(R // tr,), in_specs=[pl.BlockSpec((n, tr, C), lambda i: (0, i, 0))],
                  out_specs=pl.BlockSpec((tr, C), lambda i: (i, 0)), out_shape=jax.ShapeDtypeStruct((R, C), F32), name=name)(recv)


def _adamw(w, g, m, v, name, order_after=None):
    shape = w.shape
    w2, g2, m2, v2 = (a.reshape(-1, shape[-1]) for a in (w, g, m, v))
    R, C = w2.shape
    tr = _pick(R, (256, 128, 64, 32, 16, 8)) if R > 256 else R
    n_in = 4 + int(order_after is not None)

    def body(*refs):
        w_ref, g_ref, m_ref, v_ref = refs[:4]
        d_ref, nm_ref, nv_ref = refs[n_in:]
        gg = g_ref[...]
        nm = ADAM_B1 * m_ref[...] + (1.0 - ADAM_B1) * gg
        nv = ADAM_B2 * v_ref[...] + (1.0 - ADAM_B2) * jnp.square(gg)
        m_hat = nm / (1.0 - ADAM_B1 ** ADAM_STEP)
        v_hat = nv / (1.0 - ADAM_B2 ** ADAM_STEP)
        d_ref[...] = -ADAM_LR * (m_hat / (jnp.sqrt(v_hat) + ADAM_EPS) + ADAM_WD * w_ref[...])
        nm_ref[...] = nm
        nv_ref[...] = nv

    spec = pl.BlockSpec((tr, C), lambda i: (i, 0))
    in_specs, args = [spec] * 4, [w2, g2, m2, v2]
    if order_after is not None:
        in_specs.append(pl.BlockSpec(memory_space=pl.ANY))
        args.append(order_after)
    d, nm, nv = _pcall(body, grid=(R // tr,), in_specs=in_specs, out_specs=[spec] * 3,
                       out_shape=[jax.ShapeDtypeStruct((R, C), F32)] * 3, name=name)(*args)
    return d.reshape(shape), nm.reshape(shape), nv.reshape(shape)


def _to_rows(name, w):
    w = w[0]
    if name in ("ffn1_w_in", "ffn2_w_in", "w_in"):
        return w.T
    if name in ("w_uq", "w_ukv", "ple_w_proj"):
        return w.T.reshape(-1, 1024)
    return w


def _from_rows(name, g, shape):
    if name in ("ffn1_w_in", "ffn2_w_in", "w_in"):
        return g.T.reshape(shape)
    if name in ("w_uq", "w_ukv", "ple_w_proj"):
        return g.reshape(-1, shape[1]).T.reshape(shape)
    return g.reshape(shape)


def _unpack(wall, group):
    out, off = {}, 0
    for name, r in group:
        out[name] = wall[:, off:off + r, :].reshape(N_DEV * r, 1024)
        off += _pad16(r)
    return out


def _w_in_internal(wt):
    return jnp.concatenate([wt[0:6144], wt[6720:8768], wt[6144:6720], jnp.zeros((P_W - W_IN_COLS, 1024), wt.dtype)], axis=0)


def _w_in_external(d):
    return jnp.concatenate([d[0:6144], d[8192:8768], d[6144:8192]], axis=0)


def _rope_tables(positions):
    pos = positions[0].astype(F32)

    def cs(half):
        inv = ROPE_BASE ** (-jnp.arange(half, dtype=F32) / half)
        ang = pos[:, None] * inv
        return jnp.cos(ang), jnp.sin(ang)

    c, s = cs(64)
    c2, s2 = cs(32)
    return (jnp.concatenate([c, c], axis=1), jnp.concatenate([-s, s], axis=1),
            jnp.concatenate([c2, c2, c2, c2], axis=1), jnp.concatenate([-s2, s2, -s2, s2], axis=1))


def _local_step(x, p, positions, target, W, ln_g, ln_b, gn_g, qg, kvg, early=None, rest_weights=None, start_token=None):
    T = x.shape[0]
    tabs = _rope_tables(positions)
    rc = _ret_consts()
    lg = [ln_g[i:i + 1] for i in range(4)]
    lb = [ln_b[i:i + 1] for i in range(4)]
    pb = p.astype(BF16)

    hs1, gu1, *xb = _ffn_in(x, W["ffn1_w_in"], "ffn1_in", order_after=start_token)
    xb = xb[0] if xb else x
    f1, h1, h1b = _mm(hs1, W["ffn1_w_out"], name="ffn1_out", tm=LN_TM, epilogue=_ln_epilogue(x, 0.5, lg[0], lb[0]))
    if rest_weights is not None:
        W = {**W, **rest_weights(h1b)}
    w_in_t = _w_in_internal(W["w_in"])
    wuq = W["w_uq"].reshape(1536, LORA).reshape(HEADS, 192, LORA)
    wuq = jnp.concatenate([wuq[:, :128].reshape(1024, LORA), wuq[:, 128:].reshape(512, LORA)], axis=0)
    wukv = W["w_ukv"].reshape(2048, LORA).reshape(HEADS, 2, 128, LORA).transpose(1, 0, 2, 3).reshape(2048, LORA)
    wp_t = W["ple_w_proj"].reshape(1024, D_PLE)
    proj = _mm(h1b, w_in_t, tb=True, out_dtype=BF16, name="mixer_in")
    rq, rk, qn, kvn, kpe = _mixer_prep_fwd(proj, tabs, qg, kvg, T)
    y, yr, states = _ret_fwd(rq, rk, proj, proj, gn_g, rc, T)
    y_ret = _mm(yr, W["w_ret_o"], out_dtype=BF16, name="ret_o")
    qnope, qpe = _mm(qn, wuq, tb=True, name="mla_uq", tm=LN_TM, tn=1536, epilogue=_q_assemble_epilogue(tabs))
    kv = _mm(kvn, wukv, tb=True, out_dtype=BF16, name="mla_ukv")
    o, lse = _attn_fwd(qnope, qpe, kv, kpe, T)
    y_mla, mix = _mm(o, W["w_mla_o"], name="mla_o", tm=LN_TM, epilogue=_mix_epilogue(proj, y_ret))
    mixed, h2, h2b = _mm(mix, W["w_out"], name="mixer_out", tm=LN_TM, epilogue=_ln_epilogue(h1, 1.0, lg[1], lb[1]))
    hs2, gu2 = _ffn_in(h2b, W["ffn2_w_in"], "ffn2_in")
    f2, h3, h3b = _mm(hs2, W["ffn2_w_out"], name="ffn2_out", tm=LN_TM, epilogue=_ln_epilogue(h2, 0.5, lg[2], lb[2]))
    pp = _mm(pb, wp_t, tb=True, name="ple_proj")

    G = {}
    dh3_a, dgl, dpp, dg3, db3, loss = _mm(h3b, W["ple_w_gate"], name="ple_gate", tm=LN_TM,
                                          epilogue=_head_epilogue(h3, pp, target, lg[3], lb[3]))
    G["ple_w_gate"] = _mm(h3b, dgl, ta=True, name="d_ple_gate")
    G["ple_w_proj"] = _mm(dpp, pb, ta=True, name="d_ple_proj")
    dh2_a, df2, dg2, db2 = _mm(dgl, W["ple_w_gate"], tb=True, add=dh3_a, name="dh3", tm=LN_TM,
                               epilogue=_ln_bwd_epilogue(h2, f2, 0.5, lg[2], lb[2]))
    G["ffn2_w_out"] = _mm(hs2, df2, ta=True, name="d_ffn2_out")
    da2 = _ffn_act_bwd(df2, W["ffn2_w_out"], gu2, "ffn2_act_bwd")
    G["ffn2_w_in"] = _mm(da2, h2b, ta=True, name="d_ffn2_in")
    dh1_a, dmixed, dg1, db1 = _mm(da2, W["ffn2_w_in"], add=dh2_a, name="dh2", tm=LN_TM,
                                  epilogue=_ln_bwd_epilogue(h1, mixed, 1.0, lg[1], lb[1]))
    G["w_out"] = _mm(mix, dmixed, ta=True, name="d_mixer_out")
    dgr, dgm, dy_ret, dy_mla = _mm(dmixed, W["w_out"], tb=True, name="dmix", tm=LN_TM,
                                   epilogue=_mix_bwd_epilogue(proj, y_ret, y_mla))
    G["w_mla_o"] = _mm(o, dy_mla, ta=True, name="d_mla_o")
    dob, delta = _mm(dy_mla, W["w_mla_o"], tb=True, name="do", tm=LN_TM, epilogue=_delta_epilogue(o))
    dqn_f, dqpe_f, dkn, dkpe_all, dv = _attn_bwd(qnope, qpe, kv, kpe, dob, lse, delta, T)
    dq_n, dq_r = _q_assemble_bwd(dqn_f, dqpe_f, tabs, T)
    g_uq = jnp.concatenate([_mm(dq_n, qn, ta=True, name="d_uq_nope"), _mm(dq_r, qn, ta=True, name="d_uq_rope")], axis=0)
    g_uq = jnp.concatenate([g_uq[:1024].reshape(HEADS, 128, LORA), g_uq[1024:].reshape(HEADS, 64, LORA)], axis=1)
    G["w_uq"] = g_uq.reshape(1536 * LORA // 1024, 1024)
    dqn = _mm(dq_r, wuq[1024:], add=_mm(dq_n, wuq[:1024], name="dqn_a"), name="dqn_b")
    g_ukv = jnp.stack([_mm(dkn, kvn, ta=True, name="d_ukv_k"), _mm(dv, kvn, ta=True, name="d_ukv_v")], axis=0)
    G["w_ukv"] = g_ukv.reshape(2, HEADS, 128, LORA).transpose(1, 0, 2, 3).reshape(2048 * LORA // 1024, 1024)
    dkvn = _mm(dv, wukv[1024:], add=_mm(dkn, wukv[:1024], name="dkvn_a"), name="dkvn_b")
    dcq, dckv, dkpe, dqg, dkvg = _rms_bwd(proj, dqn, dkvn, dkpe_all, tabs, qg, kvg, T)
    G["w_ret_o"] = _mm(yr, dy_ret, ta=True, name="d_ret_o")
    dyr = _mm(dy_ret, W["w_ret_o"], tb=True, out_dtype=BF16, name="dyr")
    drq, drk, drv, drg, dgn = _ret_bwd(rq, rk, proj, y, proj, gn_g, states, dyr, rc, tabs, T)
    dproj = jnp.concatenate([drq, drk, drv, drg, dgr, dgm, dcq, dckv, dkpe, jnp.zeros((T, P_W - P_KPE - 128), BF16)], axis=1)
    G["w_in"] = _w_in_external(_mm(dproj, h1b, ta=True, name="d_mixer_in"))
    lg0 = lg[0] if early is None else lg[0] + early(G)[0:1, 0:1]
    dx_a, df1, dg0, db0 = _mm(dproj, w_in_t, add=dh1_a, name="dh1", tm=LN_TM,
                              epilogue=_ln_bwd_epilogue(x, f1, 0.5, lg0, lb[0]))
    G["ffn1_w_out"] = _mm(hs1, df1, ta=True, name="d_ffn1_out")
    da1 = _ffn_act_bwd(df1, W["ffn1_w_out"], gu1, "ffn1_act_bwd")
    G["ffn1_w_in"] = _mm(da1, xb, ta=True, name="d_ffn1_in")
    grad_x = _mm(da1, W["ffn1_w_in"], add=dx_a, name="grad_x")

    small = dict(ln_g=jnp.concatenate([dg0, dg1, dg2, dg3], axis=0), ln_b=jnp.concatenate([db0, db1, db2, db3], axis=0),
                 ret_gn_g=dgn, q_norm_g=dqg, kv_norm_g=dkvg)
    return loss, grad_x, G, small


def kernel(x, p, positions, ln_g, ln_b, ffn1_w_in, ffn1_w_out, w_in, ret_gn_g, w_ret_o, q_norm_g, kv_norm_g, w_uq, w_ukv, w_mla_o, w_out, ffn2_w_in, ffn2_w_out, ple_w_gate, ple_w_proj, loss_target, m_ln_g, m_ln_b, m_ffn1_w_in, m_ffn1_w_out, m_w_in, m_ret_gn_g, m_w_ret_o, m_q_norm_g, m_kv_norm_g, m_w_uq, m_w_ukv, m_w_mla_o, m_w_out, m_ffn2_w_in, m_ffn2_w_out, m_ple_w_gate, m_ple_w_proj, v_ln_g, v_ln_b, v_ffn1_w_in, v_ffn1_w_out, v_w_in, v_ret_gn_g, v_w_ret_o, v_q_norm_g, v_kv_norm_g, v_w_uq, v_w_ukv, v_w_mla_o, v_w_out, v_ffn2_w_in, v_ffn2_w_out, v_ple_w_gate, v_ple_w_proj):
    names = ("ln_g", "ln_b", "ffn1_w_in", "ffn1_w_out", "w_in", "ret_gn_g", "w_ret_o", "q_norm_g", "kv_norm_g", "w_uq", "w_ukv",
             "w_mla_o", "w_out", "ffn2_w_in", "ffn2_w_out", "ple_w_gate", "ple_w_proj")
    ws = dict(zip(names, (ln_g, ln_b, ffn1_w_in, ffn1_w_out, w_in, ret_gn_g, w_ret_o, q_norm_g, kv_norm_g, w_uq, w_ukv, w_mla_o,
                          w_out, ffn2_w_in, ffn2_w_out, ple_w_gate, ple_w_proj)))
    ms = dict(zip(names, (m_ln_g, m_ln_b, m_ffn1_w_in, m_ffn1_w_out, m_w_in, m_ret_gn_g, m_w_ret_o, m_q_norm_g, m_kv_norm_g, m_w_uq,
                          m_w_ukv, m_w_mla_o, m_w_out, m_ffn2_w_in, m_ffn2_w_out, m_ple_w_gate, m_ple_w_proj)))
    vs = dict(zip(names, (v_ln_g, v_ln_b, v_ffn1_w_in, v_ffn1_w_out, v_w_in, v_ret_gn_g, v_w_ret_o, v_q_norm_g, v_kv_norm_g, v_w_uq,
                          v_w_ukv, v_w_mla_o, v_w_out, v_ffn2_w_in, v_ffn2_w_out, v_ple_w_gate, v_ple_w_proj)))

    parts = []
    for name, r in PACK:
        rows = _to_rows(name, ws[name])
        if _pad16(r) != r:
            rows = jnp.concatenate([rows, jnp.zeros((_pad16(r) - r, 1024), F32)], axis=0)
        parts.append(rows)
    wsh_first = jnp.concatenate(parts[:len(PACK_LATE)], axis=0).astype(BF16)
    wsh_rest = jnp.concatenate(parts[len(PACK_LATE):], axis=0).astype(BF16)
    ssh = jnp.concatenate([ln_g[0], ln_b[0]], axis=0)
    wall_first, sall = _all_gather(wsh_first, ssh)
    *gather_handles, start_token = _gather_start(wsh_rest, wall_first)
    W = _unpack(wall_first, PACK_LATE)
    ln_full = sall.reshape(N_DEV, 2, 4, 128).transpose(1, 2, 0, 3).reshape(2, 4, 1024)

    def rest_weights(after):
        w_thru, land = _gather_wait(*gather_handles, after)
        return _unpack(_gather_finish(w_thru, land), PACK_EARLY)

    cvec = lax.axis_index("c").astype(jnp.int32).reshape(1)
    qvec = (2 * lax.axis_index("x") + lax.axis_index("y")).astype(jnp.int32).reshape(1)

    def chip_partials(G, group, tag):
        gparts = []
        for name, r in group:
            g = G[name].reshape(N_DEV, r, 1024)
            if _pad16(r) != r:
                g = jnp.concatenate([g, jnp.zeros((N_DEV, _pad16(r) - r, 1024), g.dtype)], axis=1)
            gparts.append(g)
        gfull = jnp.concatenate(gparts, axis=1).astype(BF16)
        g4 = gfull.reshape(4, 2, gfull.shape[1], 1024)
        return _sum_sibling(g4, _exchange_sibling(g4, "exchange_sibling_" + tag), cvec, "sum_sibling_" + tag)

    in_flight = []

    def early(G):
        *handles, token = _chips_start(chip_partials(G, PACK_EARLY, "early"), "chips_start_early")
        in_flight.append(handles)
        return token

    loss_p, grad_x, G, small = _local_step(x[0], p[0, 0], positions, loss_target[0], W, ln_full[0], ln_full[1],
                                           ret_gn_g, q_norm_g, kv_norm_g, early=early, rest_weights=rest_weights,
                                           start_token=start_token)

    part_e, land_e = _chips_wait(*in_flight[0], grad_x, "chips_wait_early")
    gsh_early = _sum_chips(part_e, land_e, qvec, "sum_grads_early")
    pad256 = lambda a: jnp.concatenate([a, jnp.zeros((1, 1024 - a.shape[1]), F32)], axis=1)
    gsmall = jnp.concatenate([small["ln_g"], small["ln_b"], small["ret_gn_g"].reshape(2, 1024), pad256(small["q_norm_g"]),
                              pad256(small["kv_norm_g"]), jnp.zeros((SMALL_ROWS - 12, 1024), F32)], axis=0)
    srecv = _exchange_small(gsmall)
    part_l = chip_partials(G, PACK_LATE, "late")
    *late_handles, late_token = _chips_start(part_l, "chips_start_late", order_after=srecv)
    ssum = _sum_slots(srecv, "sum_small_grads")

    def unpack_grads(group, gsh):
        out, off = {}, 0
        for name, r in group:
            out[name] = _from_rows(name, gsh[off:off + r], ws[name].shape)
            off += _pad16(r)
        return out

    grads = unpack_grads(PACK_EARLY, gsh_early)
    me = 4 * lax.axis_index("x") + 2 * lax.axis_index("y") + lax.axis_index("c")
    grads["ln_g"] = lax.dynamic_slice(ssum[0:4], (0, me * 128), (4, 128)).reshape(1, 4, 128)
    grads["ln_b"] = lax.dynamic_slice(ssum[4:8], (0, me * 128), (4, 128)).reshape(1, 4, 128)
    grads["ret_gn_g"] = ssum[8:10].reshape(1, 2048)
    grads["q_norm_g"] = ssum[10:11, :256]
    grads["kv_norm_g"] = ssum[11:12, :256]

    delta, new_m, new_v = {}, {}, {}
    late_names = [n for n, _ in PACK_LATE]
    last = late_token
    for name in names:
        if name not in late_names:
            delta[name], new_m[name], new_v[name] = _adamw(ws[name], grads[name], ms[name], vs[name], "adamw_" + name,
                                                           order_after=last)
            last = new_v[name]
    part_l, land_l = _chips_wait(*late_handles, last, "chips_wait_late")
    grads.update(unpack_grads(PACK_LATE, _sum_chips(part_l, land_l, qvec, "sum_grads_late")))
    for name in late_names:
        delta[name], new_m[name], new_v[name] = _adamw(ws[name], grads[name], ms[name], vs[name], "adamw_" + name)

    loss = lax.psum(loss_p[0, 0], ("x", "y", "c"))
    return (loss, grad_x[None], *[grads[n] for n in names], *[delta[n] for n in names],
            *[new_m[n] for n in names], *[new_v[n] for n in names])
```

```python
import math

import jax
import jax.numpy as jnp
from jax import lax
from jax.experimental import pallas as pl
from jax.experimental.pallas import tpu as pltpu

F32 = jnp.float32
BF16 = jnp.bfloat16

N_DEV = 8
D = 1024
D_FF = 2816
D_PLE = 256
CHUNK = 64
HEADS = 8
RET_DK = 128
RET_DV = 256
MLA_NOPE = 128
MLA_ROPE = 64
MLA_DV = 128
LORA = 256
ROPE_BASE = 10000.0
EPS = 1e-5
ALPHA = 2.0 ** 0.25
RET_SCALE = RET_DK ** -0.5
MLA_SCALE = (MLA_NOPE + MLA_ROPE) ** -0.5
NEG = -1e30

ADAM_LR = 0.001
ADAM_B1 = 0.9
ADAM_B2 = 0.999
ADAM_EPS = 1e-08
ADAM_WD = 0.01
ADAM_STEP = 10

P_RQ, P_RK, P_RV, P_RG, P_GR, P_GM, P_CQ, P_CKV, P_KPE, P_W = 0, 1024, 2048, 4096, 6144, 7168, 8192, 8448, 8704, 8960
W_IN_COLS = 8768
RET_L = 256
ATT_TF = 1024
ATT_TB = 1024
ATT_HP = 4
LOG2E = math.log2(math.e)
Q_PRESCALE = MLA_SCALE * LOG2E

PACK = (("ffn1_w_in", 704), ("ffn1_w_out", 352), ("w_in", 1096), ("w_ret_o", 256), ("w_uq", 48), ("w_ukv", 64),
        ("w_mla_o", 128), ("w_out", 128), ("ffn2_w_in", 704), ("ffn2_w_out", 352), ("ple_w_gate", 128), ("ple_w_proj", 32))


def _pad16(r):
    return -(-r // 16) * 16


PACK_LATE = PACK[:2]
PACK_EARLY = PACK[2:]
SMALL_ROWS = 16


def _pcall(body, **kw):
    return pl.pallas_call(body, **kw)


def _pick(dim, prefs):
    for p in prefs:
        if dim % p == 0:
            return p
    return dim


def _sigmoid(x):
    return 1.0 / (1.0 + jnp.exp(-x))


def _silu(x):
    return x * _sigmoid(x)


def _ln(r, g, b):
    mu = jnp.mean(r, axis=-1, keepdims=True)
    var = jnp.mean(jnp.square(r - mu), axis=-1, keepdims=True)
    return (r - mu) * lax.rsqrt(var + EPS) * g + b


def _rms(x, g):
    return x * lax.rsqrt(jnp.mean(jnp.square(x), axis=-1, keepdims=True) + EPS) * g


def _dot(a, b, ca, cb):
    return lax.dot_general(a, b, (((ca,), (cb,)), ((), ())), preferred_element_type=F32)


def _accum(ref, val, first=None):
    @pl.when(pl.program_id(0) == 0 if first is None else first)
    def _():
        ref[...] = jnp.zeros_like(ref)

    ref[...] += val


def _mm(a, b, *, ta=False, tb=False, add=None, out_dtype=None, name, tm=None, tn=None, tk=None, epilogue=None):
    parts = a.shape[0] if a.ndim == 3 else 1
    ar, ac = a.shape[-2], a.shape[-1]
    out_dtype = out_dtype or (BF16 if ta else F32)
    if ta:
        K, M = ar, ac * parts
    else:
        M, K = ar, ac * parts
    if tb:
        N, K2 = b.shape
    else:
        K2, N = b.shape
    assert K == K2, (a.shape, b.shape, ta, tb)
    big = (1024, 1408, 1280, 768, 512, 256, 128)
    tm = tm or _pick(ac if (ta and parts > 1) else M, big)
    tn = tn or (N if N <= 1024 else _pick(N, big))
    kdim = ac if (not ta and parts > 1) else K
    tk = tk or (kdim if kdim <= 2816 and parts == 1 else
                _pick(kdim, (2048, 1408, 1280, 1024, 512) if tm <= 1024 else (1024, 1408, 1280, 512)))
    nk = K // tk
    grid = (M // tm, N // tn, nk)
    if parts > 1 and ta:
        per = ac // tm
        a_spec = pl.BlockSpec((None, tk, tm), lambda i, j, k: (i // per, k, i % per))
    elif parts > 1:
        per = ac // tk
        a_spec = pl.BlockSpec((None, tm, tk), lambda i, j, k: (k // per, i, k % per))
    else:
        a_spec = pl.BlockSpec((tk, tm), lambda i, j, k: (k, i)) if ta else pl.BlockSpec((tm, tk), lambda i, j, k: (i, k))
    b_spec = pl.BlockSpec((tn, tk), lambda i, j, k: (j, k)) if tb else pl.BlockSpec((tk, tn), lambda i, j, k: (k, j))
    o_spec = pl.BlockSpec((tm, tn), lambda i, j, k: (i, j))
    ca, cb = (0 if ta else 1), (1 if tb else 0)
    has_add = add is not None
    n_in = 2 + int(has_add)
    if epilogue is not None:
        assert tn == N and not ta
        ep_fn, ep_rows, ep_whole, ep_outs, ep_accs = epilogue
        n_ep_in = len(ep_rows) + len(ep_whole)
        n_out = len(ep_outs) + len(ep_accs)
    else:
        n_ep_in, n_out = 0, 1

    def body(*refs):
        a_ref, b_ref = refs[0], refs[1]
        add_ref = refs[2] if has_add else None
        o_ref = refs[n_in + n_ep_in]
        first_row_tile = pl.program_id(0) == 0

        def finish(r):
            if has_add:
                r = r + add_ref[...].astype(F32)
            if epilogue is not None:
                ep_fn(r, refs[n_in:n_in + n_ep_in], refs[n_in + n_ep_in:n_in + n_ep_in + n_out], first_row_tile)
            else:
                o_ref[...] = r.astype(out_dtype)

        if nk == 1:
            finish(_dot(a_ref[...], b_ref[...], ca, cb))
            return
        acc_ref = refs[-1]
        k = pl.program_id(2)

        @pl.when(k == 0)
        def _():
            acc_ref[...] = jnp.zeros_like(acc_ref)

        acc_ref[...] += _dot(a_ref[...], b_ref[...], ca, cb)

        @pl.when(k == nk - 1)
        def _():
            finish(acc_ref[...])

    in_specs = [a_spec, b_spec] + ([o_spec] if has_add else [])
    args = (a, b) + ((add,) if has_add else ())
    out_specs, out_shape = o_spec, jax.ShapeDtypeStruct((M, N), out_dtype)
    if epilogue is not None:
        for row_in in ep_rows:
            arr, col_block, width = (tuple(row_in) + (N,))[:3] if isinstance(row_in, tuple) else (row_in, 0, N)
            in_specs.append(pl.BlockSpec((tm, width), lambda i, j, k, _c=col_block: (i, _c)))
            args += (arr,)
        in_specs += [pl.BlockSpec(w.shape, lambda i, j, k, _n=w.ndim: (0,) * _n) for w in ep_whole]
        args += tuple(ep_whole)
        outs = [o if isinstance(o, tuple) else (o, N) for o in ep_outs]
        out_specs = ([pl.BlockSpec((tm, w), lambda i, j, k: (i, 0)) for _, w in outs]
                     + [pl.BlockSpec((r, N), lambda i, j, k: (0, 0)) for r in ep_accs])
        out_shape = [jax.ShapeDtypeStruct((M, w), dt) for dt, w in outs] + [jax.ShapeDtypeStruct((r, N), F32) for r in ep_accs]
    return _pcall(
        body, grid=grid, in_specs=in_specs, out_specs=out_specs, out_shape=out_shape,
        scratch_shapes=[pltpu.VMEM((tm, tn), F32)] if nk > 1 else [], name=name,
        compiler_params=pltpu.CompilerParams(dimension_semantics=("arbitrary" if epilogue is not None else "parallel", "parallel", "arbitrary")),
    )(*args)


def _ln_epilogue(res, c, g, b):
    def fn(r, ins, outs, first):
        res_ref, g_ref, b_ref = ins
        f_ref, h_ref, hb_ref = outs
        h = _ln(ALPHA * res_ref[...] + c * r, g_ref[...], b_ref[...])
        f_ref[...] = r
        h_ref[...] = h
        hb_ref[...] = h.astype(BF16)

    return (fn, [res], [g, b], (F32, F32, BF16), ())


def _ln_bwd_epilogue(res, f, c, g, b):
    def fn(r, ins, outs, first):
        res_ref, f_ref, g_ref, b_ref = ins
        dr_ref, df_ref, dg_ref, db_ref = outs
        pre = ALPHA * res_ref[...] + c * f_ref[...]
        xc = pre - jnp.mean(pre, axis=-1, keepdims=True)
        rstd = lax.rsqrt(jnp.mean(jnp.square(xc), axis=-1, keepdims=True) + EPS)
        xhat = xc * rstd
        dyg = r * g_ref[...]
        dpre = rstd * (dyg - jnp.mean(dyg, axis=-1, keepdims=True) - xhat * jnp.mean(dyg * xhat, axis=-1, keepdims=True))
        dr_ref[...] = ALPHA * dpre
        df_ref[...] = (c * dpre).astype(BF16)
        _accum(dg_ref, jnp.sum(r * xhat, axis=0, keepdims=True), first)
        _accum(db_ref, jnp.sum(r, axis=0, keepdims=True), first)

    return (fn, [res, f], [g, b], (F32, BF16), (1, 1))


def _rows(body, T, tm, ins, outs, name, accs=()):
    in_specs, args = [], []
    for arr, w, cb in ins:
        if w is None:
            in_specs.append(pl.BlockSpec(arr.shape, lambda i, _n=arr.ndim: (0,) * _n))
        else:
            in_specs.append(pl.BlockSpec((tm, w), lambda i, _cb=cb: (i, _cb)))
        args.append(arr)
    out_specs = [pl.BlockSpec((tm, w), lambda i: (i, 0)) for w, _ in outs]
    out_shape = [jax.ShapeDtypeStruct((T, w), dt) for w, dt in outs]
    for r, w in accs:
        out_specs.append(pl.BlockSpec((r, w), lambda i: (0, 0)))
        out_shape.append(jax.ShapeDtypeStruct((r, w), F32))
    return _pcall(
        body, grid=(T // tm,), in_specs=in_specs, out_specs=out_specs, out_shape=out_shape, name=name,
        compiler_params=pltpu.CompilerParams(dimension_semantics=("arbitrary",)),
    )(*args)


FFN_TN = 1408
FFN_TM = 512
LN_TM = 512


def _ffn_in(x, wt, name, order_after=None):
    T = x.shape[0]
    tm, tn = min(FFN_TM, T), FFN_TN
    nj = D_FF // tn
    emit_xb = x.dtype != BF16
    n_in = 3 + int(order_after is not None)

    def body(*refs):
        x_ref, wg_ref, wu_ref = refs[:3]
        hs_ref, gu_ref = refs[n_in], refs[n_in + 1]
        xv = x_ref[...].astype(BF16)
        g = _dot(xv, wg_ref[...], 1, 1)
        u = _dot(xv, wu_ref[...], 1, 1)
        hs_ref[...] = (_silu(g) * u).astype(BF16)
        gu_ref[0] = g.astype(BF16)
        gu_ref[1] = u.astype(BF16)
        if emit_xb:
            refs[n_in + 2][...] = xv

    in_specs = [pl.BlockSpec((tm, D), lambda i, j: (i, 0)), pl.BlockSpec((tn, D), lambda i, j: (j, 0)),
                pl.BlockSpec((tn, D), lambda i, j: (j + nj, 0))]
    args = [x, wt, wt]
    if order_after is not None:
        in_specs.append(pl.BlockSpec(order_after.shape, lambda i, j: (0, 0)))
        args.append(order_after)
    out_specs = [pl.BlockSpec((tm, tn), lambda i, j: (i, j)), pl.BlockSpec((2, tm, tn), lambda i, j: (0, i, j))]
    out_shape = [jax.ShapeDtypeStruct((T, D_FF), BF16), jax.ShapeDtypeStruct((2, T, D_FF), BF16)]
    if emit_xb:
        out_specs.append(pl.BlockSpec((tm, D), lambda i, j: (i, 0)))
        out_shape.append(jax.ShapeDtypeStruct((T, D), BF16))
    return _pcall(
        body, grid=(T // tm, nj), in_specs=in_specs, out_specs=out_specs, out_shape=out_shape, name=name,
        compiler_params=pltpu.CompilerParams(dimension_semantics=("parallel", "arbitrary")),
    )(*args)


def _ffn_act_bwd(df, wo, gu, name):
    T = df.shape[0]
    tm, tn = min(FFN_TM, T), FFN_TN

    def body(d_ref, w_ref, gu_ref, o_ref):
        dhs = _dot(d_ref[...], w_ref[...], 1, 1)
        g, u = gu_ref[0].astype(F32), gu_ref[1].astype(F32)
        sig = _sigmoid(g)
        act = g * sig
        o_ref[0] = (dhs * u * (sig + act * (1.0 - sig))).astype(BF16)
        o_ref[1] = (dhs * act).astype(BF16)

    return _pcall(
        body, grid=(T // tm, D_FF // tn),
        in_specs=[pl.BlockSpec((tm, D), lambda i, j: (i, 0)), pl.BlockSpec((tn, D), lambda i, j: (j, 0)),
                  pl.BlockSpec((2, tm, tn), lambda i, j: (0, i, j))],
        out_specs=pl.BlockSpec((2, tm, tn), lambda i, j: (0, i, j)),
        out_shape=jax.ShapeDtypeStruct((2, T, D_FF), BF16), name=name,
        compiler_params=pltpu.CompilerParams(dimension_semantics=("parallel", "parallel")),
    )(df, wo, gu)


def _rope128(t, cos, sin_s):
    return t * cos + pltpu.roll(t, 64, 1) * sin_s


def _rope128_t(g, cos, sin_s):
    return g * cos - pltpu.roll(g, 64, 1) * sin_s


def _partner32(t):
    lane = lax.broadcasted_iota(jnp.int32, t.shape, 1)
    return jnp.where((lane & 32) == 0, pltpu.roll(t, 96, 1), pltpu.roll(t, 32, 1))


def _rope64(t, cos, sin_s):
    return t * cos + _partner32(t) * sin_s


def _rope64_t(g, cos, sin_s):
    return g * cos - _partner32(g) * sin_s


def _mixer_prep_fwd(proj, tabs, qg, kvg, T):
    cos128, sin128, cos64, sin64 = tabs

    def body(rq_ref, rk_ref, cq_ref, ckv_ref, kpe_ref, c1_ref, s1_ref, c2_ref, s2_ref, qg_ref, kvg_ref,
             oq_ref, ok_ref, oqn_ref, okvn_ref, okpe_ref):
        c1, s1 = c1_ref[...], s1_ref[...]
        for h in range(HEADS):
            sl = slice(h * RET_DK, (h + 1) * RET_DK)
            oq_ref[:, sl] = _rope128(rq_ref[:, sl].astype(F32), c1, s1).astype(BF16)
            ok_ref[:, sl] = (_rope128(rk_ref[:, sl].astype(F32), c1, s1) * RET_SCALE).astype(BF16)
        oqn_ref[...] = _rms(cq_ref[...].astype(F32), qg_ref[...]).astype(BF16)
        okvn_ref[...] = _rms(ckv_ref[...].astype(F32), kvg_ref[...]).astype(BF16)
        okpe_ref[...] = _rope64(kpe_ref[...].astype(F32), c2_ref[...], s2_ref[...]).astype(BF16)

    ins = [(proj, 1024, 0), (proj, 1024, 1), (proj, 256, P_CQ // 256), (proj, 256, P_CKV // 256),
           (proj, 128, P_KPE // 128), (cos128, 128, 0), (sin128, 128, 0), (cos64, 128, 0), (sin64, 128, 0),
           (qg, None, None), (kvg, None, None)]
    outs = [(1024, BF16), (1024, BF16), (LORA, BF16), (LORA, BF16), (128, BF16)]
    return _rows(body, T, 256, ins, outs, "mixer_prep_fwd")


def _rms_bwd(proj, dqn, dkvn, dkpe_all, tabs, qg, kvg, T):
    _, _, cos64, sin64 = tabs

    def body(cq_ref, ckv_ref, dq_ref, dkv_ref, dk_ref, c2_ref, s2_ref, qg_ref, kvg_ref, ocq_ref, ockv_ref, okpe_ref,
             dqg_ref, dkvg_ref):
        _, vjp = jax.vjp(_rms, cq_ref[...].astype(F32), qg_ref[...])
        dx, dg = vjp(dq_ref[...])
        ocq_ref[...] = dx.astype(BF16)
        _accum(dqg_ref, dg)
        _, vjp = jax.vjp(_rms, ckv_ref[...].astype(F32), kvg_ref[...])
        dx, dg = vjp(dkv_ref[...])
        ockv_ref[...] = dx.astype(BF16)
        _accum(dkvg_ref, dg)
        g = dk_ref[:, 0:128]
        for h in range(1, HEADS):
            g = g + dk_ref[:, h * 128:(h + 1) * 128]
        lane = lax.broadcasted_iota(jnp.int32, g.shape, 1)
        g = jnp.where(lane < MLA_ROPE, g, 0.0)
        okpe_ref[...] = _rope64_t(g, c2_ref[...], s2_ref[...]).astype(BF16)

    ins = [(proj, 256, P_CQ // 256), (proj, 256, P_CKV // 256), (dqn, LORA, 0), (dkvn, LORA, 0), (dkpe_all, 1024, 0),
           (cos64, 128, 0), (sin64, 128, 0), (qg, None, None), (kvg, None, None)]
    return _rows(body, T, 256, ins, [(LORA, BF16), (LORA, BF16), (128, BF16)], "rms_bwd", accs=[(1, LORA), (1, LORA)])


def _gn_gate_bwd(y, rg, g, d):
    xc = y - jnp.mean(y, axis=-1, keepdims=True)
    rstd = lax.rsqrt(jnp.mean(jnp.square(xc), axis=-1, keepdims=True) + EPS)
    xhat = xc * rstd
    sig = _sigmoid(rg)
    act = rg * sig
    dn = d * act
    drg = d * (xhat * g) * (sig + act * (1.0 - sig))
    dxh = dn * g
    dy = rstd * (dxh - jnp.mean(dxh, axis=-1, keepdims=True) - xhat * jnp.mean(dxh * xhat, axis=-1, keepdims=True))
    return dy, drg, jnp.sum(dn * xhat, axis=0, keepdims=True)


def _gn_gate(y, rg, g):
    mu = jnp.mean(y, axis=-1, keepdims=True)
    var = jnp.mean(jnp.square(y - mu), axis=-1, keepdims=True)
    return _silu(rg) * ((y - mu) * lax.rsqrt(var + EPS) * g)


def _q_assemble_epilogue(tabs):
    _, _, cos64, sin64 = tabs

    def fn(q, ins, outs, first):
        c_ref, s_ref = ins
        on_ref, op_ref = outs
        on_ref[...] = (q[:, :1024] * Q_PRESCALE).astype(BF16)
        c, s = c_ref[...], s_ref[...]
        lane = lax.broadcasted_iota(jnp.int32, c.shape, 1)
        for j in range(HEADS // 2):
            r = _rope64(q[:, 1024 + 128 * j:1024 + 128 * (j + 1)], c, s) * Q_PRESCALE
            op_ref[:, 256 * j:256 * j + 128] = jnp.where(lane < 64, r, 0.0).astype(BF16)
            op_ref[:, 256 * j + 128:256 * j + 256] = jnp.where(lane < 64, pltpu.roll(r, 64, 1), 0.0).astype(BF16)

    return (fn, [(cos64, 0, 128), (sin64, 0, 128)], [], ((BF16, 1024), (BF16, 1024)), ())


def _q_assemble_bwd(dqn, dqpe, tabs, T):
    _, _, cos64, sin64 = tabs

    def body(dn_ref, dp_ref, c_ref, s_ref, on_ref, op_ref):
        on_ref[...] = dn_ref[...].astype(BF16)
        c, s = c_ref[...], s_ref[...]
        lane = lax.broadcasted_iota(jnp.int32, c.shape, 1)
        for j in range(HEADS // 2):
            g = jnp.where(lane < 64, dp_ref[:, 256 * j:256 * j + 128], pltpu.roll(dp_ref[:, 256 * j + 128:256 * j + 256], 64, 1))
            op_ref[:, 128 * j:128 * (j + 1)] = _rope64_t(g, c, s).astype(BF16)

    return _rows(body, T, 256, [(dqn, 1024, 0), (dqpe, 1024, 0), (cos64, 128, 0), (sin64, 128, 0)],
                 [(1024, BF16), (512, BF16)], "q_assemble_bwd")


def _mix_fn(gr, gm, yr, ym):
    return _sigmoid(gr) * yr + _sigmoid(gm) * ym


def _mix_epilogue(proj, y_ret):
    def fn(r, ins, outs, first):
        gr_ref, gm_ref, yr_ref = ins
        ym_ref, mix_ref = outs
        ym_ref[...] = r.astype(BF16)
        mix_ref[...] = _mix_fn(gr_ref[...].astype(F32), gm_ref[...].astype(F32), yr_ref[...].astype(F32), r).astype(BF16)

    return (fn, [(proj, P_GR // D), (proj, P_GM // D), y_ret], [], (BF16, BF16), ())


def _mix_bwd_epilogue(proj, y_ret, y_mla):
    def fn(r, ins, outs, first):
        gr_ref, gm_ref, yr_ref, ym_ref = ins
        dgr_ref, dgm_ref, dyr_ref, dym_ref = outs
        sr, sm = _sigmoid(gr_ref[...].astype(F32)), _sigmoid(gm_ref[...].astype(F32))
        dgr_ref[...] = (r * yr_ref[...].astype(F32) * (sr * (1.0 - sr))).astype(BF16)
        dgm_ref[...] = (r * ym_ref[...].astype(F32) * (sm * (1.0 - sm))).astype(BF16)
        dyr_ref[...] = (r * sr).astype(BF16)
        dym_ref[...] = (r * sm).astype(BF16)

    return (fn, [(proj, P_GR // D), (proj, P_GM // D), y_ret, y_mla], [], (BF16,) * 4, ())


def _head_epilogue(h3, pp, tgt, g, b):
    def fn(r, ins, outs, first):
        h_ref, pp_ref, t_ref, g_ref, b_ref = ins
        dh_ref, dgl_ref, dpp_ref, dg_ref, db_ref, loss_ref = outs

        sg, pp, gain = _sigmoid(r), pp_ref[...], g_ref[...]
        pre = ALPHA * h_ref[...] + sg * pp
        xc = pre - jnp.mean(pre, axis=-1, keepdims=True)
        rstd = lax.rsqrt(jnp.mean(jnp.square(xc), axis=-1, keepdims=True) + EPS)
        xhat = xc * rstd
        err = xhat * gain + b_ref[...] - t_ref[...]
        dy = err * (1.0 / D)
        dyg = dy * gain
        dpre = rstd * (dyg - jnp.mean(dyg, axis=-1, keepdims=True) - xhat * jnp.mean(dyg * xhat, axis=-1, keepdims=True))
        dh_ref[...] = ALPHA * dpre
        dgl_ref[...] = (dpre * pp * (sg * (1.0 - sg))).astype(BF16)
        dpp_ref[...] = (dpre * sg).astype(BF16)
        _accum(dg_ref, jnp.sum(dy * xhat, axis=0, keepdims=True), first)
        _accum(db_ref, jnp.sum(dy, axis=0, keepdims=True), first)
        part = 0.5 * jnp.sum(jnp.mean(jnp.square(err), axis=-1, keepdims=True), axis=0, keepdims=True)
        _accum(loss_ref, jnp.broadcast_to(part, loss_ref.shape), first)

    return (fn, [h3, pp, tgt], [g, b], (F32, BF16, BF16), (1, 1, 8))


def _delta_epilogue(o):
    def fn(r, ins, outs, first):
        (o_ref,) = ins
        db_ref, dl_ref = outs
        db_ref[...] = r.astype(BF16)
        for h in range(HEADS):
            sl = slice(h * MLA_DV, (h + 1) * MLA_DV)
            dl = jnp.sum(r[:, sl] * o_ref[:, sl].astype(F32), axis=-1, keepdims=True)
            dl_ref[:, sl] = jnp.broadcast_to(dl, (r.shape[0], MLA_DV))

    return (fn, [o], [], (BF16, F32), ())


def _ret_consts():
    L = RET_L
    lg = jnp.log(1.0 - 2.0 ** (-5.0 - jnp.arange(HEADS, dtype=F32)))[:, None, None]
    idx = jnp.arange(L, dtype=F32)
    ch = jnp.arange(L) // CHUNK
    dist = idx[:, None] - idx[None, :]
    same = (ch[:, None] == ch[None, :])[None]
    earlier = (ch[None, :] < ch[:, None])[None]
    dm = jnp.where(same, jnp.exp(lg * jnp.abs(dist)[None]), jnp.where(earlier, jnp.exp(lg * dist[None]), 0.0))
    xi = jnp.broadcast_to(jnp.exp(lg * (idx + 1.0)[None, :, None]), (HEADS, L, 128))
    zeta = jnp.broadcast_to(jnp.exp(lg * (L - 1.0 - idx)[None, :, None]), (HEADS, L, 128))
    gl = jnp.broadcast_to(jnp.exp(lg * float(L)), (HEADS, 8, 128))
    return dm.astype(F32), xi.astype(F32), zeta.astype(F32), gl.astype(F32)


def _whole(arr):
    return pl.BlockSpec(arr.shape, lambda n, _nd=arr.ndim: (0,) * _nd)


def _ret_fwd(q, k, v, proj, gn_g, consts, T):
    dm, xi, zeta, gl = consts
    L = RET_L
    n_sc = T // L

    def body(q_ref, k_ref, v_ref, rg_ref, g_ref, dm_ref, xi_ref, ze_ref, gl_ref, y_ref, yr_ref, s_ref, st_ref):
        @pl.when(pl.program_id(0) == 0)
        def _():
            st_ref[...] = jnp.zeros_like(st_ref)

        for h in range(HEADS):
            ks, vs = slice(h * RET_DK, (h + 1) * RET_DK), slice(h * RET_DV, (h + 1) * RET_DV)
            qq, kk, vv = q_ref[:, ks], k_ref[:, ks], v_ref[:, vs]
            st = st_ref[h]
            s_ref[h, 0] = st
            p = (_dot(qq, kk, 1, 1) * dm_ref[h]).astype(BF16)
            cross = _dot(qq, st.astype(BF16), 1, 0)
            xi_c = jnp.concatenate([xi_ref[h], xi_ref[h]], axis=1)
            y = _dot(p, vv, 1, 0) + cross * xi_c
            y_ref[:, vs] = y
            yr_ref[:, vs] = _gn_gate(y, rg_ref[:, vs].astype(F32), g_ref[:, vs]).astype(BF16)
            kz = (kk.astype(F32) * ze_ref[h]).astype(BF16)
            gl2 = jnp.concatenate([gl_ref[h, 0:1, :], gl_ref[h, 0:1, :]], axis=1)
            st_ref[h] = st * gl2 + _dot(kz, vv, 0, 0)

    return _pcall(
        body, grid=(n_sc,),
        in_specs=[pl.BlockSpec((L, 1024), lambda n: (n, 0)), pl.BlockSpec((L, 1024), lambda n: (n, 0)),
                  pl.BlockSpec((L, 2048), lambda n: (n, P_RV // 2048)), pl.BlockSpec((L, 2048), lambda n: (n, P_RG // 2048)),
                  _whole(gn_g), _whole(dm), _whole(xi), _whole(zeta), _whole(gl)],
        out_specs=[pl.BlockSpec((L, 2048), lambda n: (n, 0)), pl.BlockSpec((L, 2048), lambda n: (n, 0)),
                   pl.BlockSpec((HEADS, 1, 128, 256), lambda n: (0, n, 0, 0))],
        out_shape=[jax.ShapeDtypeStruct((T, HEADS * RET_DV), F32), jax.ShapeDtypeStruct((T, HEADS * RET_DV), BF16),
                   jax.ShapeDtypeStruct((HEADS, n_sc, 128, 256), F32)],
        scratch_shapes=[pltpu.VMEM((HEADS, 128, 256), F32)], name="ret_fwd",
        compiler_params=pltpu.CompilerParams(dimension_semantics=("arbitrary",)),
    )(q, k, v, proj, gn_g, dm, xi, zeta, gl)


def _ret_bwd(q, k, v, y, proj, gn_g, states, dyr, consts, tabs, T):
    dm, xi, zeta, gl = consts
    cos128, sin128, _, _ = tabs
    L = RET_L
    n_sc = T // L

    def body(q_ref, k_ref, v_ref, y_ref, rg_ref, g_ref, d_ref, s_ref, dm_ref, xi_ref, ze_ref, gl_ref, c_ref, sn_ref,
             dq_ref, dk_ref, dv_ref, drg_ref, dg_ref, gs_ref):
        @pl.when(pl.program_id(0) == 0)
        def _():
            gs_ref[...] = jnp.zeros_like(gs_ref)

        c, sn = c_ref[...], sn_ref[...]
        dgs = []
        for h in range(HEADS):
            ks, vs = slice(h * RET_DK, (h + 1) * RET_DK), slice(h * RET_DV, (h + 1) * RET_DV)
            dy, drg, dg = _gn_gate_bwd(y_ref[:, vs], rg_ref[:, vs].astype(F32), g_ref[:, vs], d_ref[:, vs].astype(F32))
            drg_ref[:, vs] = drg.astype(BF16)
            dgs.append(dg)
            qq, kk, vv, dyy = q_ref[:, ks], k_ref[:, ks], v_ref[:, vs], dy.astype(BF16)
            dmm = dm_ref[h]
            gb = gs_ref[h].astype(BF16)
            sb = s_ref[h, 0].astype(BF16)
            xi_c = jnp.concatenate([xi_ref[h], xi_ref[h]], axis=1)
            pb = (_dot(qq, kk, 1, 1) * dmm).astype(BF16)
            kz = (kk.astype(F32) * ze_ref[h]).astype(BF16)
            dv_ref[:, vs] = (_dot(pb, dyy, 0, 0) + _dot(kz, gb, 1, 0)).astype(BF16)
            da = (_dot(dyy, vv, 1, 1) * dmm).astype(BF16)
            dyx = (dyy.astype(F32) * xi_c).astype(BF16)
            dq = _dot(da, kk, 1, 0) + _dot(dyx, sb, 1, 1)
            dk = _dot(da, qq, 0, 0) + _dot(vv, gb, 1, 1) * ze_ref[h]
            dq_ref[:, ks] = _rope128_t(dq, c, sn).astype(BF16)
            dk_ref[:, ks] = (_rope128_t(dk, c, sn) * RET_SCALE).astype(BF16)
            gl2 = jnp.concatenate([gl_ref[h, 0:1, :], gl_ref[h, 0:1, :]], axis=1)
            gs_ref[h] = gs_ref[h] * gl2 + _dot(qq, dyx, 0, 0)
        _accum(dg_ref, jnp.concatenate(dgs, axis=1))

    rev = lambda n: n_sc - 1 - n
    return _pcall(
        body, grid=(n_sc,),
        in_specs=[pl.BlockSpec((L, 1024), lambda n: (rev(n), 0)), pl.BlockSpec((L, 1024), lambda n: (rev(n), 0)),
                  pl.BlockSpec((L, 2048), lambda n: (rev(n), P_RV // 2048)), pl.BlockSpec((L, 2048), lambda n: (rev(n), 0)),
                  pl.BlockSpec((L, 2048), lambda n: (rev(n), P_RG // 2048)), _whole(gn_g),
                  pl.BlockSpec((L, 2048), lambda n: (rev(n), 0)),
                  pl.BlockSpec((HEADS, 1, 128, 256), lambda n: (0, rev(n), 0, 0)),
                  _whole(dm), _whole(xi), _whole(zeta), _whole(gl),
                  pl.BlockSpec((L, 128), lambda n: (rev(n), 0)), pl.BlockSpec((L, 128), lambda n: (rev(n), 0))],
        out_specs=[pl.BlockSpec((L, 1024), lambda n: (rev(n), 0)), pl.BlockSpec((L, 1024), lambda n: (rev(n), 0)),
                   pl.BlockSpec((L, 2048), lambda n: (rev(n), 0)), pl.BlockSpec((L, 2048), lambda n: (rev(n), 0)),
                   pl.BlockSpec((1, 2048), lambda n: (0, 0))],
        out_shape=[jax.ShapeDtypeStruct((T, 1024), BF16), jax.ShapeDtypeStruct((T, 1024), BF16), jax.ShapeDtypeStruct((T, 2048), BF16),
                   jax.ShapeDtypeStruct((T, 2048), BF16), jax.ShapeDtypeStruct((1, 2048), F32)],
        scratch_shapes=[pltpu.VMEM((HEADS, 128, 256), F32)], name="ret_bwd",
        compiler_params=pltpu.CompilerParams(dimension_semantics=("arbitrary",)),
    )(q, k, v, y, proj, gn_g, dyr, states, dm, xi, zeta, gl, cos128, sin128)


def _diag_mask(nrows, ncols, row0):
    row = row0 + lax.broadcasted_iota(jnp.int32, (nrows, ncols), 0)
    col = lax.broadcasted_iota(jnp.int32, (nrows, ncols), 1)
    return lax.shift_right_logical(col, 6) <= lax.shift_right_logical(row, 6)


def _diag_spans(t):
    h = t // 2
    return ((0, h, h), (h, h, t)) if h % 128 == 0 else ((0, t, t),)


def _tri_steps(nb, by_key):
    if by_key:
        pairs = [(i, j) for j in range(nb) for i in range(j, nb)]
    else:
        pairs = [(i, j) for i in range(nb) for j in range(i + 1)]
    return jnp.array([a for a, _ in pairs], jnp.int32), jnp.array([b for _, b in pairs], jnp.int32)


def _attn_fwd(qn, qpe, kv, kpe, T):
    t = min(ATT_TF, T)
    nb = T // t
    ii, jj = _tri_steps(nb, by_key=False)

    hp = ATT_HP
    w = 128 * hp

    def body(ii_ref, jj_ref, qn_ref, qp_ref, kn_ref, kp_ref, v_ref, o_ref, lse_ref, m_sc, l_sc, acc_sc):
        st = pl.program_id(1)
        i, j = ii_ref[st], jj_ref[st]

        @pl.when(j == 0)
        def _():
            m_sc[...] = jnp.full_like(m_sc, NEG)
            l_sc[...] = jnp.zeros_like(l_sc)
            acc_sc[...] = jnp.zeros_like(acc_sc)

        def update(diag):
            kp = kp_ref[...]
            for hh in range(hp):
                sl = slice(128 * hh, 128 * (hh + 1))
                q = jnp.concatenate([qn_ref[:, sl], qp_ref[:, sl]], axis=1)
                k = jnp.concatenate([kn_ref[:, sl], kp], axis=1)
                s = _dot(q, k, 1, 1)
                if diag:
                    s = jnp.where(_diag_mask(t, t, 0), s, NEG)
                m_prev = m_sc[:, sl]
                m_new = jnp.maximum(m_prev, jnp.max(s, axis=1, keepdims=True))
                a = jnp.exp2(m_prev - m_new)
                p = jnp.exp2(s - m_new[:, 0:1])
                l_sc[:, sl] = a * l_sc[:, sl] + jnp.sum(p, axis=1, keepdims=True)
                acc_sc[:, sl] = a * acc_sc[:, sl] + _dot(p.astype(BF16), v_ref[:, sl], 1, 0)
                m_sc[:, sl] = m_new

        @pl.when(j < i)
        def _():
            update(False)

        @pl.when(j == i)
        def _():
            update(True)
            o_ref[...] = (acc_sc[...] / l_sc[...]).astype(BF16)
            lse_ref[...] = m_sc[...] + jnp.log2(l_sc[...])

    qs = pl.BlockSpec((t, w), lambda h, s, ii, jj: (ii[s], h))
    grid_spec = pltpu.PrefetchScalarGridSpec(
        num_scalar_prefetch=2, grid=(HEADS // hp, int(ii.shape[0])),
        in_specs=[qs, qs, pl.BlockSpec((t, w), lambda h, s, ii, jj: (jj[s], h)), pl.BlockSpec((t, 128), lambda h, s, ii, jj: (jj[s], 0)),
                  pl.BlockSpec((t, w), lambda h, s, ii, jj: (jj[s], HEADS // hp + h))],
        out_specs=[qs, qs],
        scratch_shapes=[pltpu.VMEM((t, w), F32), pltpu.VMEM((t, w), F32), pltpu.VMEM((t, w), F32)])
    return _pcall(
        body, grid_spec=grid_spec, out_shape=[jax.ShapeDtypeStruct((T, D), BF16), jax.ShapeDtypeStruct((T, D), F32)], name="attn_fwd",
        compiler_params=pltpu.CompilerParams(dimension_semantics=("arbitrary", "arbitrary")),
    )(ii, jj, qn, qpe, kv, kpe, kv)


def _attn_bwd(qn, qpe, kv, kpe, do, lse, delta, T):
    t = min(ATT_TB, T)
    nb = T // t
    ii, jj = _tri_steps(nb, by_key=True)

    def body(ii_ref, jj_ref, qn_ref, qp_ref, kn_ref, kp_ref, v_ref, do_ref, lse_ref, dl_ref,
             dqn_ref, dqp_ref, dkn_ref, dkp_ref, dv_ref, dk_sc, dv_sc):
        st = pl.program_id(1)
        i, j = ii_ref[st], jj_ref[st]

        @pl.when(st == 0)
        def _():
            dqn_ref[...] = jnp.zeros_like(dqn_ref)
            dqp_ref[...] = jnp.zeros_like(dqp_ref)

        @pl.when(i == j)
        def _():
            dk_sc[...] = jnp.zeros_like(dk_sc)
            dv_sc[...] = jnp.zeros_like(dv_sc)

        def update(diag):
            for row0, nr, nkeys in (_diag_spans(t) if diag else ((0, t, t),)):
                rows = slice(row0, row0 + nr)
                q = jnp.concatenate([qn_ref[rows, :], qp_ref[rows, :]], axis=1)
                k = jnp.concatenate([kn_ref[:nkeys, :], kp_ref[:nkeys, :]], axis=1)
                dob = do_ref[rows, :]
                s = _dot(q, k, 1, 1)
                if diag:
                    s = jnp.where(_diag_mask(nr, nkeys, row0), s, NEG)
                p = jnp.exp2(s - lse_ref[rows, 0:1])
                dv_sc[:nkeys, :] += _dot(p.astype(BF16), dob, 0, 0)
                dp = _dot(dob, v_ref[:nkeys, :], 1, 1)
                ds = (p * (dp - dl_ref[rows, 0:1])).astype(BF16)
                dk_sc[:nkeys, :] += _dot(ds, q, 0, 0)
                dq = _dot(ds, k, 1, 0) * MLA_SCALE
                grows = pl.ds(pl.multiple_of(i * t + row0, nr), nr)
                dqn_ref[grows, :] += dq[:, :128]
                dqp_ref[grows, :] += dq[:, 128:]

        @pl.when(i > j)
        def _():
            update(False)

        @pl.when(i == j)
        def _():
            update(True)

        @pl.when(i == nb - 1)
        def _():
            dkn_ref[...] = (dk_sc[:, :128] * (1.0 / LOG2E)).astype(BF16)
            dkp_ref[...] = dk_sc[:, 128:] * (1.0 / LOG2E)
            dv_ref[...] = dv_sc[...].astype(BF16)

    qs = pl.BlockSpec((t, 128), lambda h, s, ii, jj: (ii[s], h))
    ks = pl.BlockSpec((t, 128), lambda h, s, ii, jj: (jj[s], h))
    hs = pl.BlockSpec((T, 128), lambda h, s, ii, jj: (0, h))
    grid_spec = pltpu.PrefetchScalarGridSpec(
        num_scalar_prefetch=2, grid=(HEADS, int(ii.shape[0])),
        in_specs=[qs, qs, ks, pl.BlockSpec((t, 128), lambda h, s, ii, jj: (jj[s], 0)),
                  pl.BlockSpec((t, 128), lambda h, s, ii, jj: (jj[s], HEADS + h)), qs, qs, qs],
        out_specs=[hs, hs, ks, ks, ks],
        scratch_shapes=[pltpu.VMEM((t, 256), F32), pltpu.VMEM((t, 128), F32)])
    return _pcall(
        body, grid_spec=grid_spec,
        out_shape=[jax.ShapeDtypeStruct((T, D), F32), jax.ShapeDtypeStruct((T, D), F32), jax.ShapeDtypeStruct((T, D), BF16),
                   jax.ShapeDtypeStruct((T, D), F32), jax.ShapeDtypeStruct((T, D), BF16)], name="attn_bwd",
        compiler_params=pltpu.CompilerParams(dimension_semantics=("arbitrary", "arbitrary")),
    )(ii, jj, qn, qpe, kv, kpe, kv, do, lse, delta)


def _mesh_pos():
    x, y, c = lax.axis_index("x"), lax.axis_index("y"), lax.axis_index("c")
    return x, y, c, 4 * x + 2 * y + c


def _peer(x, y, c, k):
    px, py, pc = (x + ((k >> 2) & 1)) % 2, (y + ((k >> 1) & 1)) % 2, (c + (k & 1)) % 2
    return (px, py, pc), 4 * px + 2 * py + pc


_ANY = pl.BlockSpec(memory_space=pl.ANY)


def _rcopy(src, dst, send_sems, recv_sems, k, to):
    return pltpu.make_async_remote_copy(src_ref=src, dst_ref=dst, send_sem=send_sems.at[k], recv_sem=recv_sems.at[k],
                                        device_id=to, device_id_type=pl.DeviceIdType.MESH)


def _all_gather(wsh, ssh):
    def body(w_ref, s_ref, wall_ref, sall_ref, send_sems, recv_sems, loc_sems):
        x, y, c, me = _mesh_pos()
        sib = (x, y, 1 - c)
        chips = [(1 - x, y), (x, 1 - y), (1 - x, 1 - y)]
        slot = lambda px, py, pc: 4 * px + 2 * py + pc
        loc = [pltpu.make_async_copy(s_ref, sall_ref.at[me], loc_sems.at[0])]
        for cp in loc:
            cp.start()
        sends, fwd_waits = [], []
        for n, (src, dst) in enumerate(((w_ref, wall_ref), (s_ref, sall_ref))):
            o = 7 * n
            sends.append(_rcopy(src, dst.at[me], send_sems, recv_sems, o, sib))
            for j, chip in enumerate(chips):
                sends.append(_rcopy(src, dst.at[me], send_sems, recv_sems, o + 1 + j, (*chip, c)))
        for cp in sends:
            cp.start()
        for n, (src, dst) in enumerate(((w_ref, wall_ref), (s_ref, sall_ref))):
            o = 7 * n
            for j, chip in enumerate(chips):
                got = dst.at[slot(*chip, c)]
                _rcopy(src, got, send_sems, recv_sems, o + 1 + j, sib).wait_recv()
                fw = _rcopy(got, got, send_sems, recv_sems, o + 4 + j, sib)
                fw.start()
                sends.append(fw)
            fwd_waits.append(_rcopy(src, dst.at[slot(x, y, 1 - c)], send_sems, recv_sems, o, sib))
            for j, chip in enumerate(chips):
                fwd_waits.append(_rcopy(src, dst.at[slot(*chip, 1 - c)], send_sems, recv_sems, o + 4 + j, sib))
        for cp in fwd_waits:
            cp.wait_recv()
        for cp in sends:
            cp.wait_send()
        for cp in loc:
            cp.wait()

    wall, sall = _pcall(
        body, in_specs=[_ANY, _ANY], out_specs=[_ANY, _ANY],
        out_shape=[jax.ShapeDtypeStruct((N_DEV,) + wsh.shape, wsh.dtype), jax.ShapeDtypeStruct((N_DEV,) + ssh.shape, ssh.dtype)],
        scratch_shapes=[pltpu.SemaphoreType.DMA((14,)), pltpu.SemaphoreType.DMA((14,)), pltpu.SemaphoreType.DMA((1,))],
        name="all_gather_weights",
    )(wsh, ssh)
    me = 4 * lax.axis_index("x") + 2 * lax.axis_index("y") + lax.axis_index("c")
    return lax.dynamic_update_index_in_dim(wall, wsh, me, axis=0), sall


_HBM = pl.BlockSpec(memory_space=pltpu.HBM)
_SEM = pl.BlockSpec(memory_space=pltpu.SEMAPHORE)
_EFFECT = pltpu.SideEffectType.DATAFLOW_SIDE_EFFECTING


def _other_chips(x, y):
    return [(1 - x, y), (x, 1 - y), (1 - x, 1 - y)]


def _gather_start(wsh, order_after):
    def body(w_ref, land_ref, dep_ref, send_sems, recv_sems, w_thru, land_thru, token):
        x, y, c, me = _mesh_pos()
        _rcopy(w_ref, land_ref.at[me], send_sems, recv_sems, 0, (x, y, 1 - c)).start()
        for j, chip in enumerate(_other_chips(x, y)):
            _rcopy(w_ref, land_ref.at[me], send_sems, recv_sems, 1 + j, (*chip, c)).start()
        token[...] = jnp.zeros_like(token)

    shape = (N_DEV,) + wsh.shape
    land = pltpu.with_memory_space_constraint(lax.empty(shape, wsh.dtype), pltpu.HBM)
    return _pcall(
        body, name="gather_start",
        out_shape=(pltpu.SemaphoreType.DMA((4,)), pltpu.SemaphoreType.DMA((4,)), pltpu.HBM(wsh.shape, wsh.dtype),
                   pltpu.HBM(shape, wsh.dtype), jax.ShapeDtypeStruct((8, 128), F32)),
        in_specs=(_HBM, _HBM, _ANY), out_specs=(_SEM, _SEM, _HBM, _HBM, pl.BlockSpec(memory_space=pltpu.VMEM)),
        input_output_aliases={0: 2, 1: 3}, compiler_params=pltpu.CompilerParams(has_side_effects=_EFFECT),
    )(pltpu.with_memory_space_constraint(wsh, pltpu.HBM), land, order_after)


def _gather_wait(send_sems, recv_sems, w_thru, land_thru, after):
    def body(w_ref, land_ref, send_sems, recv_sems, after_ref, w_out, land_out):
        x, y, c, _ = _mesh_pos()
        senders = [(x, y, 1 - c)] + [(*chip, c) for chip in _other_chips(x, y)]
        for k, (px, py, pc) in enumerate(senders):
            cp = _rcopy(w_ref, land_ref.at[4 * px + 2 * py + pc], send_sems, recv_sems, k, (px, py, pc))
            cp.wait_send()
            cp.wait_recv()

    return _pcall(
        body, name="gather_wait", out_shape=(pltpu.HBM(w_thru.shape, w_thru.dtype), pltpu.HBM(land_thru.shape, land_thru.dtype)),
        in_specs=(_HBM, _HBM, _SEM, _SEM, _ANY), out_specs=(_HBM, _HBM), input_output_aliases={0: 0, 1: 1},
        compiler_params=pltpu.CompilerParams(has_side_effects=_EFFECT),
    )(w_thru, land_thru, send_sems, recv_sems, after)


def _gather_finish(wsh, land):
    def body(land_ref, out_ref, send_sems, recv_sems):
        x, y, c, _ = _mesh_pos()
        sib = (x, y, 1 - c)
        sends, recvs = [], []
        for j, (px, py) in enumerate(_other_chips(x, y)):
            mine, theirs = 4 * px + 2 * py + c, 4 * px + 2 * py + (1 - c)
            sends.append(_rcopy(land_ref.at[mine], out_ref.at[mine], send_sems, recv_sems, j, sib))
            recvs.append(_rcopy(land_ref.at[theirs], out_ref.at[theirs], send_sems, recv_sems, j, sib))
        for cp in sends:
            cp.start()
        for cp in recvs:
            cp.wait_recv()
        for cp in sends:
            cp.wait_send()

    done = _pcall(
        body, in_specs=[_ANY], out_specs=_ANY, out_shape=jax.ShapeDtypeStruct(land.shape, land.dtype),
        input_output_aliases={0: 0},
        scratch_shapes=[pltpu.SemaphoreType.DMA((3,)), pltpu.SemaphoreType.DMA((3,))], name="gather_finish",
    )(land)
    me = 4 * lax.axis_index("x") + 2 * lax.axis_index("y") + lax.axis_index("c")
    return lax.dynamic_update_index_in_dim(done, wsh, me, axis=0)


_ROW_TILES = (400, 368, 352, 256, 128, 16, 8)


def _exchange_sibling(g4, name):
    def body(g_ref, r_ref, send_sems, recv_sems):
        x, y, c, _ = _mesh_pos()
        sib = (x, y, 1 - c)
        cps = [_rcopy(g_ref.at[q, 1 - c], r_ref.at[q], send_sems, recv_sems, q, sib) for q in range(4)]
        for cp in cps:
            cp.start()
        for cp in cps:
            cp.wait()

    n, _, R, C = g4.shape
    return _pcall(
        body, in_specs=[_ANY], out_specs=_ANY, out_shape=jax.ShapeDtypeStruct((n, R, C), g4.dtype),
        scratch_shapes=[pltpu.SemaphoreType.DMA((4,)), pltpu.SemaphoreType.DMA((4,))], name=name,
    )(g4)


def _sum_sibling(g4, rsib, cvec, name):
    n, _, R, C = g4.shape
    tr = _pick(R, _ROW_TILES)

    def body(c_ref, g_ref, r_ref, o_ref):
        o_ref[...] = (g_ref[...].astype(F32) + r_ref[...].astype(F32)).astype(o_ref.dtype)

    grid_spec = pltpu.PrefetchScalarGridSpec(
        num_scalar_prefetch=1, grid=(n, R // tr),
        in_specs=[pl.BlockSpec((None, None, tr, C), lambda q, i, cr: (q, cr[0], i, 0)), pl.BlockSpec((None, tr, C), lambda q, i, cr: (q, i, 0))],
        out_specs=pl.BlockSpec((None, tr, C), lambda q, i, cr: (q, i, 0)))
    return _pcall(body, grid_spec=grid_spec, out_shape=jax.ShapeDtypeStruct((n, R, C), g4.dtype), name=name)(cvec, g4, rsib)


def _chips_start(part, name, order_after=None):
    n_in = 2 + int(order_after is not None)

    def body(*refs):
        p_ref, land_ref = refs[:2]
        send_sems, recv_sems, token = refs[n_in], refs[n_in + 1], refs[n_in + 4]
        x, y, c, _ = _mesh_pos()
        myq = 2 * x + y
        for j, (px, py) in enumerate(_other_chips(x, y)):
            _rcopy(p_ref.at[2 * px + py], land_ref.at[myq], send_sems, recv_sems, j, (px, py, c)).start()
        token[...] = jnp.zeros_like(token)

    land = pltpu.with_memory_space_constraint(lax.empty(part.shape, part.dtype), pltpu.HBM)
    extra = () if order_after is None else (order_after,)
    return _pcall(
        body, name=name,
        out_shape=(pltpu.SemaphoreType.DMA((3,)), pltpu.SemaphoreType.DMA((3,)), pltpu.HBM(part.shape, part.dtype),
                   pltpu.HBM(part.shape, part.dtype), jax.ShapeDtypeStruct((8, 128), F32)),
        in_specs=(_HBM, _HBM) + (_ANY,) * len(extra),
        out_specs=(_SEM, _SEM, _HBM, _HBM, pl.BlockSpec(memory_space=pltpu.VMEM)),
        input_output_aliases={0: 2, 1: 3}, compiler_params=pltpu.CompilerParams(has_side_effects=_EFFECT),
    )(pltpu.with_memory_space_constraint(part, pltpu.HBM), land, *extra)


def _chips_wait(send_sems, recv_sems, p_thru, land_thru, after, name):
    def body(p_ref, land_ref, send_sems, recv_sems, after_ref, p_out, land_out):
        x, y, c, _ = _mesh_pos()
        for j, (px, py) in enumerate(_other_chips(x, y)):
            q = 2 * px + py
            cp = _rcopy(p_ref.at[q], land_ref.at[q], send_sems, recv_sems, j, (px, py, c))
            cp.wait_send()
            cp.wait_recv()

    return _pcall(
        body, name=name, out_shape=(pltpu.HBM(p_thru.shape, p_thru.dtype), pltpu.HBM(p_thru.shape, p_thru.dtype)),
        in_specs=(_HBM, _HBM, _SEM, _SEM, _ANY), out_specs=(_HBM, _HBM), input_output_aliases={0: 0, 1: 1},
        compiler_params=pltpu.CompilerParams(has_side_effects=_EFFECT),
    )(p_thru, land_thru, send_sems, recv_sems, after)


def _sum_chips(part, land, qvec, name):
    n, R, C = part.shape
    tr = _pick(R, _ROW_TILES)

    def body(q_ref, p_ref, *refs):
        o_ref = refs[n]
        acc = None
        for q in range(n):
            term = jnp.where(q_ref[0] == q, p_ref[...], refs[q][...]).astype(F32)
            acc = term if acc is None else acc + term
        o_ref[...] = acc

    def land_spec(q):
        return pl.BlockSpec((None, tr, C), lambda i, m: (jnp.where(m[0] == q, (q + 1) % n, q), i, 0))

    grid_spec = pltpu.PrefetchScalarGridSpec(
        num_scalar_prefetch=1, grid=(R // tr,),
        in_specs=[pl.BlockSpec((None, tr, C), lambda i, m: (m[0], i, 0))] + [land_spec(q) for q in range(n)],
        out_specs=pl.BlockSpec((tr, C), lambda i, m: (i, 0)))
    return _pcall(body, grid_spec=grid_spec, out_shape=jax.ShapeDtypeStruct((R, C), F32), name=name)(qvec, part, *([land] * n))


def _exchange_small(gsmall):
    def body(s_ref, srecv_ref, send_sems, recv_sems, loc_sem):
        x, y, c, me = _mesh_pos()
        loc = pltpu.make_async_copy(s_ref, srecv_ref.at[me], loc_sem.at[0])
        loc.start()
        sends, recvs = [], []
        for k in range(1, N_DEV):
            to, pidx = _peer(x, y, c, k)
            sends.append(_rcopy(s_ref, srecv_ref.at[me], send_sems, recv_sems, k - 1, to))
            recvs.append(_rcopy(s_ref, srecv_ref.at[pidx], send_sems, recv_sems, k - 1, to))
        for cp in sends:
            cp.start()
        for cp in recvs:
            cp.wait_recv()
        for cp in sends:
            cp.wait_send()
        loc.wait()

    return _pcall(
        body, in_specs=[_ANY], out_specs=_ANY, out_shape=jax.ShapeDtypeStruct((N_DEV,) + gsmall.shape, gsmall.dtype),
        scratch_shapes=[pltpu.SemaphoreType.DMA((7,)), pltpu.SemaphoreType.DMA((7,)), pltpu.SemaphoreType.DMA((1,))],
        name="exchange_small",
    )(gsmall)


def _sum_slots(recv, name):
    n, R, C = recv.shape
    tr = _pick(R, _ROW_TILES)

    def body(r_ref, o_ref):
        acc = r_ref[0].astype(F32)
        for s in range(1, n):
            acc = acc + r_ref[s].astype(F32)
        o_ref[...] = acc

    return _pcall(body, grid=(R // tr,), in_specs=[pl.BlockSpec((n, tr, C), lambda i: (0, i, 0))],
                  out_specs=pl.BlockSpec((tr, C), lambda i: (i, 0)), out_shape=jax.ShapeDtypeStruct((R, C), F32), name=name)(recv)


def _adamw(w, g, m, v, name, order_after=None):
    shape = w.shape
    w2, g2, m2, v2 = (a.reshape(-1, shape[-1]) for a in (w, g, m, v))
    R, C = w2.shape
    tr = _pick(R, (256, 128, 64, 32, 16, 8)) if R > 256 else R
    n_in = 4 + int(order_after is not None)

    def body(*refs):
        w_ref, g_ref, m_ref, v_ref = refs[:4]
        d_ref, nm_ref, nv_ref = refs[n_in:]
        gg = g_ref[...]
        nm = ADAM_B1 * m_ref[...] + (1.0 - ADAM_B1) * gg
        nv = ADAM_B2 * v_ref[...] + (1.0 - ADAM_B2) * jnp.square(gg)
        m_hat = nm / (1.0 - ADAM_B1 ** ADAM_STEP)
        v_hat = nv / (1.0 - ADAM_B2 ** ADAM_STEP)
        d_ref[...] = -ADAM_LR * (m_hat / (jnp.sqrt(v_hat) + ADAM_EPS) + ADAM_WD * w_ref[...])
        nm_ref[...] = nm
        nv_ref[...] = nv

    spec = pl.BlockSpec((tr, C), lambda i: (i, 0))
    in_specs, args = [spec] * 4, [w2, g2, m2, v2]
    if order_after is not None:
        in_specs.append(pl.BlockSpec(memory_space=pl.ANY))
        args.append(order_after)
    d, nm, nv = _pcall(body, grid=(R // tr,), in_specs=in_specs, out_specs=[spec] * 3,
                       out_shape=[jax.ShapeDtypeStruct((R, C), F32)] * 3, name=name)(*args)
    return d.reshape(shape), nm.reshape(shape), nv.reshape(shape)


def _to_rows(name, w):
    w = w[0]
    if name in ("ffn1_w_in", "ffn2_w_in", "w_in"):
        return w.T
    if name in ("w_uq", "w_ukv", "ple_w_proj"):
        return w.T.reshape(-1, 1024)
    return w


def _from_rows(name, g, shape):
    if name in ("ffn1_w_in", "ffn2_w_in", "w_in"):
        return g.T.reshape(shape)
    if name in ("w_uq", "w_ukv", "ple_w_proj"):
        return g.reshape(-1, shape[1]).T.reshape(shape)
    return g.reshape(shape)


def _unpack(wall, group):
    out, off = {}, 0
    for name, r in group:
        out[name] = wall[:, off:off + r, :].reshape(N_DEV * r, 1024)
        off += _pad16(r)
    return out


def _w_in_internal(wt):
    return jnp.concatenate([wt[0:6144], wt[6720:8768], wt[6144:6720], jnp.zeros((P_W - W_IN_COLS, 1024), wt.dtype)], axis=0)


def _w_in_external(d):
    return jnp.concatenate([d[0:6144], d[8192:8768], d[6144:8192]], axis=0)


def _rope_tables(positions):
    pos = positions[0].astype(F32)

    def cs(half):
        inv = ROPE_BASE ** (-jnp.arange(half, dtype=F32) / half)
        ang = pos[:, None] * inv
        return jnp.cos(ang), jnp.sin(ang)

    c, s = cs(64)
    c2, s2 = cs(32)
    return (jnp.concatenate([c, c], axis=1), jnp.concatenate([-s, s], axis=1),
            jnp.concatenate([c2, c2, c2, c2], axis=1), jnp.concatenate([-s2, s2, -s2, s2], axis=1))


def _local_step(x, p, positions, target, W, ln_g, ln_b, gn_g, qg, kvg, early=None, rest_weights=None, start_token=None):
    T = x.shape[0]
    tabs = _rope_tables(positions)
    rc = _ret_consts()
    lg = [ln_g[i:i + 1] for i in range(4)]
    lb = [ln_b[i:i + 1] for i in range(4)]
    pb = p.astype(BF16)

    hs1, gu1, *xb = _ffn_in(x, W["ffn1_w_in"], "ffn1_in", order_after=start_token)
    xb = xb[0] if xb else x
    f1, h1, h1b = _mm(hs1, W["ffn1_w_out"], name="ffn1_out", tm=LN_TM, epilogue=_ln_epilogue(x, 0.5, lg[0], lb[0]))
    if rest_weights is not None:
        W = {**W, **rest_weights(h1b)}
    w_in_t = _w_in_internal(W["w_in"])
    wuq = W["w_uq"].reshape(1536, LORA).reshape(HEADS, 192, LORA)
    wuq = jnp.concatenate([wuq[:, :128].reshape(1024, LORA), wuq[:, 128:].reshape(512, LORA)], axis=0)
    wukv = W["w_ukv"].reshape(2048, LORA).reshape(HEADS, 2, 128, LORA).transpose(1, 0, 2, 3).reshape(2048, LORA)
    wp_t = W["ple_w_proj"].reshape(1024, D_PLE)
    proj = _mm(h1b, w_in_t, tb=True, out_dtype=BF16, name="mixer_in")
    rq, rk, qn, kvn, kpe = _mixer_prep_fwd(proj, tabs, qg, kvg, T)
    y, yr, states = _ret_fwd(rq, rk, proj, proj, gn_g, rc, T)
    y_ret = _mm(yr, W["w_ret_o"], out_dtype=BF16, name="ret_o")
    qnope, qpe = _mm(qn, wuq, tb=True, name="mla_uq", tm=LN_TM, tn=1536, epilogue=_q_assemble_epilogue(tabs))
    kv = _mm(kvn, wukv, tb=True, out_dtype=BF16, name="mla_ukv")
    o, lse = _attn_fwd(qnope, qpe, kv, kpe, T)
    y_mla, mix = _mm(o, W["w_mla_o"], name="mla_o", tm=LN_TM, epilogue=_mix_epilogue(proj, y_ret))
    mixed, h2, h2b = _mm(mix, W["w_out"], name="mixer_out", tm=LN_TM, epilogue=_ln_epilogue(h1, 1.0, lg[1], lb[1]))
    hs2, gu2 = _ffn_in(h2b, W["ffn2_w_in"], "ffn2_in")
    f2, h3, h3b = _mm(hs2, W["ffn2_w_out"], name="ffn2_out", tm=LN_TM, epilogue=_ln_epilogue(h2, 0.5, lg[2], lb[2]))
    pp = _mm(pb, wp_t, tb=True, name="ple_proj")

    G = {}
    dh3_a, dgl, dpp, dg3, db3, loss = _mm(h3b, W["ple_w_gate"], name="ple_gate", tm=LN_TM,
                                          epilogue=_head_epilogue(h3, pp, target, lg[3], lb[3]))
    G["ple_w_gate"] = _mm(h3b, dgl, ta=True, name="d_ple_gate")
    G["ple_w_proj"] = _mm(dpp, pb, ta=True, name="d_ple_proj")
    dh2_a, df2, dg2, db2 = _mm(dgl, W["ple_w_gate"], tb=True, add=dh3_a, name="dh3", tm=LN_TM,
                               epilogue=_ln_bwd_epilogue(h2, f2, 0.5, lg[2], lb[2]))
    G["ffn2_w_out"] = _mm(hs2, df2, ta=True, name="d_ffn2_out")
    da2 = _ffn_act_bwd(df2, W["ffn2_w_out"], gu2, "ffn2_act_bwd")
    G["ffn2_w_in"] = _mm(da2, h2b, ta=True, name="d_ffn2_in")
    dh1_a, dmixed, dg1, db1 = _mm(da2, W["ffn2_w_in"], add=dh2_a, name="dh2", tm=LN_TM,
                                  epilogue=_ln_bwd_epilogue(h1, mixed, 1.0, lg[1], lb[1]))
    G["w_out"] = _mm(mix, dmixed, ta=True, name="d_mixer_out")
    dgr, dgm, dy_ret, dy_mla = _mm(dmixed, W["w_out"], tb=True, name="dmix", tm=LN_TM,
                                   epilogue=_mix_bwd_epilogue(proj, y_ret, y_mla))
    G["w_mla_o"] = _mm(o, dy_mla, ta=True, name="d_mla_o")
    dob, delta = _mm(dy_mla, W["w_mla_o"], tb=True, name="do", tm=LN_TM, epilogue=_delta_epilogue(o))
    dqn_f, dqpe_f, dkn, dkpe_all, dv = _attn_bwd(qnope, qpe, kv, kpe, dob, lse, delta, T)
    dq_n, dq_r = _q_assemble_bwd(dqn_f, dqpe_f, tabs, T)
    g_uq = jnp.concatenate([_mm(dq_n, qn, ta=True, name="d_uq_nope"), _mm(dq_r, qn, ta=True, name="d_uq_rope")], axis=0)
    g_uq = jnp.concatenate([g_uq[:1024].reshape(HEADS, 128, LORA), g_uq[1024:].reshape(HEADS, 64, LORA)], axis=1)
    G["w_uq"] = g_uq.reshape(1536 * LORA // 1024, 1024)
    dqn = _mm(dq_r, wuq[1024:], add=_mm(dq_n, wuq[:1024], name="dqn_a"), name="dqn_b")
    g_ukv = jnp.stack([_mm(dkn, kvn, ta=True, name="d_ukv_k"), _mm(dv, kvn, ta=True, name="d_ukv_v")], axis=0)
    G["w_ukv"] = g_ukv.reshape(2, HEADS, 128, LORA).transpose(1, 0, 2, 3).reshape(2048 * LORA // 1024, 1024)
    dkvn = _mm(dv, wukv[1024:], add=_mm(dkn, wukv[:1024], name="dkvn_a"), name="dkvn_b")
    dcq, dckv, dkpe, dqg, dkvg = _rms_bwd(proj, dqn, dkvn, dkpe_all, tabs, qg, kvg, T)
    G["w_ret_o"] = _mm(yr, dy_ret, ta=True, name="d_ret_o")
    dyr = _mm(dy_ret, W["w_ret_o"], tb=True, out_dtype=BF16, name="dyr")
    drq, drk, drv, drg, dgn = _ret_bwd(rq, rk, proj, y, proj, gn_g, states, dyr, rc, tabs, T)
    dproj = jnp.concatenate([drq, drk, drv, drg, dgr, dgm, dcq, dckv, dkpe, jnp.zeros((T, P_W - P_KPE - 128), BF16)], axis=1)
    G["w_in"] = _w_in_external(_mm(dproj, h1b, ta=True, name="d_mixer_in"))
    lg0 = lg[0] if early is None else lg[0] + early(G)[0:1, 0:1]
    dx_a, df1, dg0, db0 = _mm(dproj, w_in_t, add=dh1_a, name="dh1", tm=LN_TM,
                              epilogue=_ln_bwd_epilogue(x, f1, 0.5, lg0, lb[0]))
    G["ffn1_w_out"] = _mm(hs1, df1, ta=True, name="d_ffn1_out")
    da1 = _ffn_act_bwd(df1, W["ffn1_w_out"], gu1, "ffn1_act_bwd")
    G["ffn1_w_in"] = _mm(da1, xb, ta=True, name="d_ffn1_in")
    grad_x = _mm(da1, W["ffn1_w_in"], add=dx_a, name="grad_x")

    small = dict(ln_g=jnp.concatenate([dg0, dg1, dg2, dg3], axis=0), ln_b=jnp.concatenate([db0, db1, db2, db3], axis=0),
                 ret_gn_g=dgn, q_norm_g=dqg, kv_norm_g=dkvg)
    return loss, grad_x, G, small


def kernel(x, p, positions, ln_g, ln_b, ffn1_w_in, ffn1_w_out, w_in, ret_gn_g, w_ret_o, q_norm_g, kv_norm_g, w_uq, w_ukv, w_mla_o, w_out, ffn2_w_in, ffn2_w_out, ple_w_gate, ple_w_proj, loss_target, m_ln_g, m_ln_b, m_ffn1_w_in, m_ffn1_w_out, m_w_in, m_ret_gn_g, m_w_ret_o, m_q_norm_g, m_kv_norm_g, m_w_uq, m_w_ukv, m_w_mla_o, m_w_out, m_ffn2_w_in, m_ffn2_w_out, m_ple_w_gate, m_ple_w_proj, v_ln_g, v_ln_b, v_ffn1_w_in, v_ffn1_w_out, v_w_in, v_ret_gn_g, v_w_ret_o, v_q_norm_g, v_kv_norm_g, v_w_uq, v_w_ukv, v_w_mla_o, v_w_out, v_ffn2_w_in, v_ffn2_w_out, v_ple_w_gate, v_ple_w_proj):
    names = ("ln_g", "ln_b", "ffn1_w_in", "ffn1_w_out", "w_in", "ret_gn_g", "w_ret_o", "q_norm_g", "kv_norm_g", "w_uq", "w_ukv",
             "w_mla_o", "w_out", "ffn2_w_in", "ffn2_w_out", "ple_w_gate", "ple_w_proj")
    ws = dict(zip(names, (ln_g, ln_b, ffn1_w_in, ffn1_w_out, w_in, ret_gn_g, w_ret_o, q_norm_g, kv_norm_g, w_uq, w_ukv, w_mla_o,
                          w_out, ffn2_w_in, ffn2_w_out, ple_w_gate, ple_w_proj)))
    ms = dict(zip(names, (m_ln_g, m_ln_b, m_ffn1_w_in, m_ffn1_w_out, m_w_in, m_ret_gn_g, m_w_ret_o, m_q_norm_g, m_kv_norm_g, m_w_uq,
                          m_w_ukv, m_w_mla_o, m_w_out, m_ffn2_w_in, m_ffn2_w_out, m_ple_w_gate, m_ple_w_proj)))
    vs = dict(zip(names, (v_ln_g, v_ln_b, v_ffn1_w_in, v_ffn1_w_out, v_w_in, v_ret_gn_g, v_w_ret_o, v_q_norm_g, v_kv_norm_g, v_w_uq,
                          v_w_ukv, v_w_mla_o, v_w_out, v_ffn2_w_in, v_ffn2_w_out, v_ple_w_gate, v_ple_w_proj)))

    parts = []
    for name, r in PACK:
        rows = _to_rows(name, ws[name])
        if _pad16(r) != r:
            rows = jnp.concatenate([rows, jnp.zeros((_pad16(r) - r, 1024), F32)], axis=0)
        parts.append(rows)
    wsh_first = jnp.concatenate(parts[:len(PACK_LATE)], axis=0).astype(BF16)
    wsh_rest = jnp.concatenate(parts[len(PACK_LATE):], axis=0).astype(BF16)
    ssh = jnp.concatenate([ln_g[0], ln_b[0]], axis=0)
    wall_first, sall = _all_gather(wsh_first, ssh)
    *gather_handles, start_token = _gather_start(wsh_rest, wall_first)
    W = _unpack(wall_first, PACK_LATE)
    ln_full = sall.reshape(N_DEV, 2, 4, 128).transpose(1, 2, 0, 3).reshape(2, 4, 1024)

    def rest_weights(after):
        w_thru, land = _gather_wait(*gather_handles, after)
        return _unpack(_gather_finish(w_thru, land), PACK_EARLY)

    cvec = lax.axis_index("c").astype(jnp.int32).reshape(1)
    qvec = (2 * lax.axis_index("x") + lax.axis_index("y")).astype(jnp.int32).reshape(1)

    def chip_partials(G, group, tag):
        gparts = []
        for name, r in group:
            g = G[name].reshape(N_DEV, r, 1024)
            if _pad16(r) != r:
                g = jnp.concatenate([g, jnp.zeros((N_DEV, _pad16(r) - r, 1024), g.dtype)], axis=1)
            gparts.append(g)
        gfull = jnp.concatenate(gparts, axis=1).astype(BF16)
        g4 = gfull.reshape(4, 2, gfull.shape[1], 1024)
        return _sum_sibling(g4, _exchange_sibling(g4, "exchange_sibling_" + tag), cvec, "sum_sibling_" + tag)

    in_flight = []

    def early(G):
        *handles, token = _chips_start(chip_partials(G, PACK_EARLY, "early"), "chips_start_early")
        in_flight.append(handles)
        return token

    loss_p, grad_x, G, small = _local_step(x[0], p[0, 0], positions, loss_target[0], W, ln_full[0], ln_full[1],
                                           ret_gn_g, q_norm_g, kv_norm_g, early=early, rest_weights=rest_weights,
                                           start_token=start_token)

    part_e, land_e = _chips_wait(*in_flight[0], grad_x, "chips_wait_early")
    gsh_early = _sum_chips(part_e, land_e, qvec, "sum_grads_early")
    pad256 = lambda a: jnp.concatenate([a, jnp.zeros((1, 1024 - a.shape[1]), F32)], axis=1)
    gsmall = jnp.concatenate([small["ln_g"], small["ln_b"], small["ret_gn_g"].reshape(2, 1024), pad256(small["q_norm_g"]),
                              pad256(small["kv_norm_g"]), jnp.zeros((SMALL_ROWS - 12, 1024), F32)], axis=0)
    srecv = _exchange_small(gsmall)
    part_l = chip_partials(G, PACK_LATE, "late")
    *late_handles, late_token = _chips_start(part_l, "chips_start_late", order_after=srecv)
    ssum = _sum_slots(srecv, "sum_small_grads")

    def unpack_grads(group, gsh):
        out, off = {}, 0
        for name, r in group:
            out[name] = _from_rows(name, gsh[off:off + r], ws[name].shape)
            off += _pad16(r)
        return out

    grads = unpack_grads(PACK_EARLY, gsh_early)
    me = 4 * lax.axis_index("x") + 2 * lax.axis_index("y") + lax.axis_index("c")
    grads["ln_g"] = lax.dynamic_slice(ssum[0:4], (0, me * 128), (4, 128)).reshape(1, 4, 128)
    grads["ln_b"] = lax.dynamic_slice(ssum[4:8], (0, me * 128), (4, 128)).reshape(1, 4, 128)
    grads["ret_gn_g"] = ssum[8:10].reshape(1, 2048)
    grads["q_norm_g"] = ssum[10:11, :256]
    grads["kv_norm_g"] = ssum[11:12, :256]

    delta, new_m, new_v = {}, {}, {}
    late_names = [n for n, _ in PACK_LATE]
    last = late_token
    for name in names:
        if name not in late_names:
            delta[name], new_m[name], new_v[name] = _adamw(ws[name], grads[name], ms[name], vs[name], "adamw_" + name,
                                                           order_after=last)
            last = new_v[name]
    part_l, land_l = _chips_wait(*late_handles, last, "chips_wait_late")
    grads.update(unpack_grads(PACK_LATE, _sum_chips(part_l, land_l, qvec, "sum_grads_late")))
    for name in late_names:
        delta[name], new_m[name], new_v[name] = _adamw(ws[name], grads[name], ms[name], vs[name], "adamw_" + name)

    loss = lax.psum(loss_p[0, 0], ("x", "y", "c"))
    return (loss, grad_x[None], *[grads[n] for n in names], *[delta[n] for n in names],
            *[new_m[n] for n in names], *[new_v[n] for n in names])
```

```python
import math

import jax
import jax.numpy as jnp
from jax import lax
from jax.experimental import pallas as pl
from jax.experimental.pallas import tpu as pltpu

F32 = jnp.float32
BF16 = jnp.bfloat16

N_DEV = 8
D = 1024
D_FF = 2816
D_PLE = 256
CHUNK = 64
HEADS = 8
RET_DK = 128
RET_DV = 256
MLA_NOPE = 128
MLA_ROPE = 64
MLA_DV = 128
LORA = 256
ROPE_BASE = 10000.0
EPS = 1e-5
ALPHA = 2.0 ** 0.25
RET_SCALE = RET_DK ** -0.5
MLA_SCALE = (MLA_NOPE + MLA_ROPE) ** -0.5
NEG = -1e30

ADAM_LR = 0.001
ADAM_B1 = 0.9
ADAM_B2 = 0.999
ADAM_EPS = 1e-08
ADAM_WD = 0.01
ADAM_STEP = 10

P_RQ, P_RK, P_RV, P_RG, P_GR, P_GM, P_CQ, P_CKV, P_KPE, P_W = 0, 1024, 2048, 4096, 6144, 7168, 8192, 8448, 8704, 8960
W_IN_COLS = 8768
RET_L = 256
ATT_TF = 2048
ATT_TB = 1024
ATT_HP = 1
LOG2E = math.log2(math.e)
Q_PRESCALE = MLA_SCALE * LOG2E

PACK = (("ffn1_w_in", 704), ("ffn1_w_out", 352), ("w_in", 1096), ("w_ret_o", 256), ("w_uq", 48), ("w_ukv", 64),
        ("w_mla_o", 128), ("w_out", 128), ("ffn2_w_in", 704), ("ffn2_w_out", 352), ("ple_w_gate", 128), ("ple_w_proj", 32))


def _pad16(r):
    return -(-r // 16) * 16


PACK_LATE = PACK[:2]
PACK_EARLY = PACK[2:]
SMALL_ROWS = 16


def _pcall(body, **kw):
    return pl.pallas_call(body, **kw)


def _pick(dim, prefs):
    for p in prefs:
        if dim % p == 0:
            return p
    return dim


def _sigmoid(x):
    return 1.0 / (1.0 + jnp.exp(-x))


def _silu(x):
    return x * _sigmoid(x)


def _ln(r, g, b):
    mu = jnp.mean(r, axis=-1, keepdims=True)
    var = jnp.mean(jnp.square(r - mu), axis=-1, keepdims=True)
    return (r - mu) * lax.rsqrt(var + EPS) * g + b


def _rms(x, g):
    return x * lax.rsqrt(jnp.mean(jnp.square(x), axis=-1, keepdims=True) + EPS) * g


def _dot(a, b, ca, cb):
    return lax.dot_general(a, b, (((ca,), (cb,)), ((), ())), preferred_element_type=F32)


def _accum(ref, val, first=None):
    @pl.when(pl.program_id(0) == 0 if first is None else first)
    def _():
        ref[...] = jnp.zeros_like(ref)

    ref[...] += val


def _mm(a, b, *, ta=False, tb=False, add=None, out_dtype=None, name, tm=None, tn=None, tk=None, epilogue=None):
    parts = a.shape[0] if a.ndim == 3 else 1
    ar, ac = a.shape[-2], a.shape[-1]
    out_dtype = out_dtype or (BF16 if ta else F32)
    if ta:
        K, M = ar, ac * parts
    else:
        M, K = ar, ac * parts
    if tb:
        N, K2 = b.shape
    else:
        K2, N = b.shape
    assert K == K2, (a.shape, b.shape, ta, tb)
    big = (1024, 1408, 1280, 768, 512, 256, 128)
    tm = tm or _pick(ac if (ta and parts > 1) else M, big)
    tn = tn or (N if N <= 1024 else _pick(N, big))
    kdim = ac if (not ta and parts > 1) else K
    tk = tk or (kdim if kdim <= 2816 and parts == 1 else
                _pick(kdim, (2048, 1408, 1280, 1024, 512) if tm <= 1024 else (1024, 1408, 1280, 512)))
    nk = K // tk
    grid = (M // tm, N // tn, nk)
    if parts > 1 and ta:
        per = ac // tm
        a_spec = pl.BlockSpec((None, tk, tm), lambda i, j, k: (i // per, k, i % per))
    elif parts > 1:
        per = ac // tk
        a_spec = pl.BlockSpec((None, tm, tk), lambda i, j, k: (k // per, i, k % per))
    else:
        a_spec = pl.BlockSpec((tk, tm), lambda i, j, k: (k, i)) if ta else pl.BlockSpec((tm, tk), lambda i, j, k: (i, k))
    b_spec = pl.BlockSpec((tn, tk), lambda i, j, k: (j, k)) if tb else pl.BlockSpec((tk, tn), lambda i, j, k: (k, j))
    o_spec = pl.BlockSpec((tm, tn), lambda i, j, k: (i, j))
    ca, cb = (0 if ta else 1), (1 if tb else 0)
    has_add = add is not None
    n_in = 2 + int(has_add)
    if epilogue is not None:
        assert tn == N and not ta
        ep_fn, ep_rows, ep_whole, ep_outs, ep_accs = epilogue
        n_ep_in = len(ep_rows) + len(ep_whole)
        n_out = len(ep_outs) + len(ep_accs)
    else:
        n_ep_in, n_out = 0, 1

    def body(*refs):
        a_ref, b_ref = refs[0], refs[1]
        add_ref = refs[2] if has_add else None
        o_ref = refs[n_in + n_ep_in]
        first_row_tile = pl.program_id(0) == 0

        def finish(r):
            if has_add:
                r = r + add_ref[...].astype(F32)
            if epilogue is not None:
                ep_fn(r, refs[n_in:n_in + n_ep_in], refs[n_in + n_ep_in:n_in + n_ep_in + n_out], first_row_tile)
            else:
                o_ref[...] = r.astype(out_dtype)

        if nk == 1:
            finish(_dot(a_ref[...], b_ref[...], ca, cb))
            return
        acc_ref = refs[-1]
        k = pl.program_id(2)

        @pl.when(k == 0)
        def _():
            acc_ref[...] = jnp.zeros_like(acc_ref)

        acc_ref[...] += _dot(a_ref[...], b_ref[...], ca, cb)

        @pl.when(k == nk - 1)
        def _():
            finish(acc_ref[...])

    in_specs = [a_spec, b_spec] + ([o_spec] if has_add else [])
    args = (a, b) + ((add,) if has_add else ())
    out_specs, out_shape = o_spec, jax.ShapeDtypeStruct((M, N), out_dtype)
    if epilogue is not None:
        for row_in in ep_rows:
            arr, col_block, width = (tuple(row_in) + (N,))[:3] if isinstance(row_in, tuple) else (row_in, 0, N)
            in_specs.append(pl.BlockSpec((tm, width), lambda i, j, k, _c=col_block: (i, _c)))
            args += (arr,)
        in_specs += [pl.BlockSpec(w.shape, lambda i, j, k, _n=w.ndim: (0,) * _n) for w in ep_whole]
        args += tuple(ep_whole)
        outs = [o if isinstance(o, tuple) else (o, N) for o in ep_outs]
        out_specs = ([pl.BlockSpec((tm, w), lambda i, j, k: (i, 0)) for _, w in outs]
                     + [pl.BlockSpec((r, N), lambda i, j, k: (0, 0)) for r in ep_accs])
        out_shape = [jax.ShapeDtypeStruct((M, w), dt) for dt, w in outs] + [jax.ShapeDtypeStruct((r, N), F32) for r in ep_accs]
    return _pcall(
        body, grid=grid, in_specs=in_specs, out_specs=out_specs, out_shape=out_shape,
        scratch_shapes=[pltpu.VMEM((tm, tn), F32)] if nk > 1 else [], name=name,
        compiler_params=pltpu.CompilerParams(dimension_semantics=("arbitrary" if epilogue is not None else "parallel", "parallel", "arbitrary")),
    )(*args)


def _ln_epilogue(res, c, g, b):
    def fn(r, ins, outs, first):
        res_ref, g_ref, b_ref = ins
        f_ref, h_ref, hb_ref = outs
        h = _ln(ALPHA * res_ref[...] + c * r, g_ref[...], b_ref[...])
        f_ref[...] = r
        h_ref[...] = h
        hb_ref[...] = h.astype(BF16)

    return (fn, [res], [g, b], (F32, F32, BF16), ())


def _ln_bwd_epilogue(res, f, c, g, b):
    def fn(r, ins, outs, first):
        res_ref, f_ref, g_ref, b_ref = ins
        dr_ref, df_ref, dg_ref, db_ref = outs
        pre = ALPHA * res_ref[...] + c * f_ref[...]
        xc = pre - jnp.mean(pre, axis=-1, keepdims=True)
        rstd = lax.rsqrt(jnp.mean(jnp.square(xc), axis=-1, keepdims=True) + EPS)
        xhat = xc * rstd
        dyg = r * g_ref[...]
        dpre = rstd * (dyg - jnp.mean(dyg, axis=-1, keepdims=True) - xhat * jnp.mean(dyg * xhat, axis=-1, keepdims=True))
        dr_ref[...] = ALPHA * dpre
        df_ref[...] = (c * dpre).astype(BF16)
        _accum(dg_ref, jnp.sum(r * xhat, axis=0, keepdims=True), first)
        _accum(db_ref, jnp.sum(r, axis=0, keepdims=True), first)

    return (fn, [res, f], [g, b], (F32, BF16), (1, 1))


def _rows(body, T, tm, ins, outs, name, accs=()):
    in_specs, args = [], []
    for arr, w, cb in ins:
        if w is None:
            in_specs.append(pl.BlockSpec(arr.shape, lambda i, _n=arr.ndim: (0,) * _n))
        else:
            in_specs.append(pl.BlockSpec((tm, w), lambda i, _cb=cb: (i, _cb)))
        args.append(arr)
    out_specs = [pl.BlockSpec((tm, w), lambda i: (i, 0)) for w, _ in outs]
    out_shape = [jax.ShapeDtypeStruct((T, w), dt) for w, dt in outs]
    for r, w in accs:
        out_specs.append(pl.BlockSpec((r, w), lambda i: (0, 0)))
        out_shape.append(jax.ShapeDtypeStruct((r, w), F32))
    return _pcall(
        body, grid=(T // tm,), in_specs=in_specs, out_specs=out_specs, out_shape=out_shape, name=name,
        compiler_params=pltpu.CompilerParams(dimension_semantics=("arbitrary",)),
    )(*args)


FFN_TN = 1408
FFN_TM = 512
LN_TM = 512


def _ffn_in(x, wt, name, order_after=None):
    T = x.shape[0]
    tm, tn = min(FFN_TM, T), FFN_TN
    nj = D_FF // tn
    emit_xb = x.dtype != BF16
    n_in = 3 + int(order_after is not None)

    def body(*refs):
        x_ref, wg_ref, wu_ref = refs[:3]
        hs_ref, gu_ref = refs[n_in], refs[n_in + 1]
        xv = x_ref[...].astype(BF16)
        g = _dot(xv, wg_ref[...], 1, 1)
        u = _dot(xv, wu_ref[...], 1, 1)
        hs_ref[...] = (_silu(g) * u).astype(BF16)
        gu_ref[0] = g.astype(BF16)
        gu_ref[1] = u.astype(BF16)
        if emit_xb:
            refs[n_in + 2][...] = xv

    in_specs = [pl.BlockSpec((tm, D), lambda i, j: (i, 0)), pl.BlockSpec((tn, D), lambda i, j: (j, 0)),
                pl.BlockSpec((tn, D), lambda i, j: (j + nj, 0))]
    args = [x, wt, wt]
    if order_after is not None:
        in_specs.append(pl.BlockSpec(order_after.shape, lambda i, j: (0, 0)))
        args.append(order_after)
    out_specs = [pl.BlockSpec((tm, tn), lambda i, j: (i, j)), pl.BlockSpec((2, tm, tn), lambda i, j: (0, i, j))]
    out_shape = [jax.ShapeDtypeStruct((T, D_FF), BF16), jax.ShapeDtypeStruct((2, T, D_FF), BF16)]
    if emit_xb:
        out_specs.append(pl.BlockSpec((tm, D), lambda i, j: (i, 0)))
        out_shape.append(jax.ShapeDtypeStruct((T, D), BF16))
    return _pcall(
        body, grid=(T // tm, nj), in_specs=in_specs, out_specs=out_specs, out_shape=out_shape, name=name,
        compiler_params=pltpu.CompilerParams(dimension_semantics=("parallel", "arbitrary")),
    )(*args)


def _ffn_act_bwd(df, wo, gu, name):
    T = df.shape[0]
    tm, tn = min(FFN_TM, T), FFN_TN

    def body(d_ref, w_ref, gu_ref, o_ref):
        dhs = _dot(d_ref[...], w_ref[...], 1, 1)
        g, u = gu_ref[0].astype(F32), gu_ref[1].astype(F32)
        sig = _sigmoid(g)
        act = g * sig
        o_ref[0] = (dhs * u * (sig + act * (1.0 - sig))).astype(BF16)
        o_ref[1] = (dhs * act).astype(BF16)

    return _pcall(
        body, grid=(T // tm, D_FF // tn),
        in_specs=[pl.BlockSpec((tm, D), lambda i, j: (i, 0)), pl.BlockSpec((tn, D), lambda i, j: (j, 0)),
                  pl.BlockSpec((2, tm, tn), lambda i, j: (0, i, j))],
        out_specs=pl.BlockSpec((2, tm, tn), lambda i, j: (0, i, j)),
        out_shape=jax.ShapeDtypeStruct((2, T, D_FF), BF16), name=name,
        compiler_params=pltpu.CompilerParams(dimension_semantics=("parallel", "parallel")),
    )(df, wo, gu)


def _rope128(t, cos, sin_s):
    return t * cos + pltpu.roll(t, 64, 1) * sin_s


def _rope128_t(g, cos, sin_s):
    return g * cos - pltpu.roll(g, 64, 1) * sin_s


def _partner32(t):
    lane = lax.broadcasted_iota(jnp.int32, t.shape, 1)
    return jnp.where((lane & 32) == 0, pltpu.roll(t, 96, 1), pltpu.roll(t, 32, 1))


def _rope64(t, cos, sin_s):
    return t * cos + _partner32(t) * sin_s


def _rope64_t(g, cos, sin_s):
    return g * cos - _partner32(g) * sin_s


def _mixer_prep_fwd(proj, tabs, qg, kvg, T):
    cos128, sin128, cos64, sin64 = tabs

    def body(rq_ref, rk_ref, cq_ref, ckv_ref, kpe_ref, c1_ref, s1_ref, c2_ref, s2_ref, qg_ref, kvg_ref,
             oq_ref, ok_ref, oqn_ref, okvn_ref, okpe_ref):
        c1, s1 = c1_ref[...], s1_ref[...]
        for h in range(HEADS):
            sl = slice(h * RET_DK, (h + 1) * RET_DK)
            oq_ref[:, sl] = _rope128(rq_ref[:, sl].astype(F32), c1, s1).astype(BF16)
            ok_ref[:, sl] = (_rope128(rk_ref[:, sl].astype(F32), c1, s1) * RET_SCALE).astype(BF16)
        oqn_ref[...] = _rms(cq_ref[...].astype(F32), qg_ref[...]).astype(BF16)
        okvn_ref[...] = _rms(ckv_ref[...].astype(F32), kvg_ref[...]).astype(BF16)
        okpe_ref[...] = _rope64(kpe_ref[...].astype(F32), c2_ref[...], s2_ref[...]).astype(BF16)

    ins = [(proj, 1024, 0), (proj, 1024, 1), (proj, 256, P_CQ // 256), (proj, 256, P_CKV // 256),
           (proj, 128, P_KPE // 128), (cos128, 128, 0), (sin128, 128, 0), (cos64, 128, 0), (sin64, 128, 0),
           (qg, None, None), (kvg, None, None)]
    outs = [(1024, BF16), (1024, BF16), (LORA, BF16), (LORA, BF16), (128, BF16)]
    return _rows(body, T, 256, ins, outs, "mixer_prep_fwd")


def _rms_bwd(proj, dqn, dkvn, dkpe_all, tabs, qg, kvg, T):
    _, _, cos64, sin64 = tabs

    def body(cq_ref, ckv_ref, dq_ref, dkv_ref, dk_ref, c2_ref, s2_ref, qg_ref, kvg_ref, ocq_ref, ockv_ref, okpe_ref,
             dqg_ref, dkvg_ref):
        _, vjp = jax.vjp(_rms, cq_ref[...].astype(F32), qg_ref[...])
        dx, dg = vjp(dq_ref[...])
        ocq_ref[...] = dx.astype(BF16)
        _accum(dqg_ref, dg)
        _, vjp = jax.vjp(_rms, ckv_ref[...].astype(F32), kvg_ref[...])
        dx, dg = vjp(dkv_ref[...])
        ockv_ref[...] = dx.astype(BF16)
        _accum(dkvg_ref, dg)
        g = dk_ref[:, 0:128]
        for h in range(1, HEADS):
            g = g + dk_ref[:, h * 128:(h + 1) * 128]
        lane = lax.broadcasted_iota(jnp.int32, g.shape, 1)
        g = jnp.where(lane < MLA_ROPE, g, 0.0)
        okpe_ref[...] = _rope64_t(g, c2_ref[...], s2_ref[...]).astype(BF16)

    ins = [(proj, 256, P_CQ // 256), (proj, 256, P_CKV // 256), (dqn, LORA, 0), (dkvn, LORA, 0), (dkpe_all, 1024, 0),
           (cos64, 128, 0), (sin64, 128, 0), (qg, None, None), (kvg, None, None)]
    return _rows(body, T, 256, ins, [(LORA, BF16), (LORA, BF16), (128, BF16)], "rms_bwd", accs=[(1, LORA), (1, LORA)])


def _gn_gate_bwd(y, rg, g, d):
    xc = y - jnp.mean(y, axis=-1, keepdims=True)
    rstd = lax.rsqrt(jnp.mean(jnp.square(xc), axis=-1, keepdims=True) + EPS)
    xhat = xc * rstd
    sig = _sigmoid(rg)
    act = rg * sig
    dn = d * act
    drg = d * (xhat * g) * (sig + act * (1.0 - sig))
    dxh = dn * g
    dy = rstd * (dxh - jnp.mean(dxh, axis=-1, keepdims=True) - xhat * jnp.mean(dxh * xhat, axis=-1, keepdims=True))
    return dy, drg, jnp.sum(dn * xhat, axis=0, keepdims=True)


def _gn_gate(y, rg, g):
    mu = jnp.mean(y, axis=-1, keepdims=True)
    var = jnp.mean(jnp.square(y - mu), axis=-1, keepdims=True)
    return _silu(rg) * ((y - mu) * lax.rsqrt(var + EPS) * g)


def _q_assemble_epilogue(tabs):
    _, _, cos64, sin64 = tabs

    def fn(q, ins, outs, first):
        c_ref, s_ref = ins
        on_ref, op_ref = outs
        on_ref[...] = (q[:, :1024] * Q_PRESCALE).astype(BF16)
        c, s = c_ref[...], s_ref[...]
        lane = lax.broadcasted_iota(jnp.int32, c.shape, 1)
        for j in range(HEADS // 2):
            r = _rope64(q[:, 1024 + 128 * j:1024 + 128 * (j + 1)], c, s) * Q_PRESCALE
            op_ref[:, 256 * j:256 * j + 128] = jnp.where(lane < 64, r, 0.0).astype(BF16)
            op_ref[:, 256 * j + 128:256 * j + 256] = jnp.where(lane < 64, pltpu.roll(r, 64, 1), 0.0).astype(BF16)

    return (fn, [(cos64, 0, 128), (sin64, 0, 128)], [], ((BF16, 1024), (BF16, 1024)), ())


def _q_assemble_bwd(dqn, dqpe, tabs, T):
    _, _, cos64, sin64 = tabs

    def body(dn_ref, dp_ref, c_ref, s_ref, on_ref, op_ref):
        on_ref[...] = dn_ref[...].astype(BF16)
        c, s = c_ref[...], s_ref[...]
        lane = lax.broadcasted_iota(jnp.int32, c.shape, 1)
        for j in range(HEADS // 2):
            g = jnp.where(lane < 64, dp_ref[:, 256 * j:256 * j + 128], pltpu.roll(dp_ref[:, 256 * j + 128:256 * j + 256], 64, 1))
            op_ref[:, 128 * j:128 * (j + 1)] = _rope64_t(g, c, s).astype(BF16)

    return _rows(body, T, 256, [(dqn, 1024, 0), (dqpe, 1024, 0), (cos64, 128, 0), (sin64, 128, 0)],
                 [(1024, BF16), (512, BF16)], "q_assemble_bwd")


def _mix_fn(gr, gm, yr, ym):
    return _sigmoid(gr) * yr + _sigmoid(gm) * ym


def _mix_epilogue(proj, y_ret):
    def fn(r, ins, outs, first):
        gr_ref, gm_ref, yr_ref = ins
        ym_ref, mix_ref = outs
        ym_ref[...] = r.astype(BF16)
        mix_ref[...] = _mix_fn(gr_ref[...].astype(F32), gm_ref[...].astype(F32), yr_ref[...].astype(F32), r).astype(BF16)

    return (fn, [(proj, P_GR // D), (proj, P_GM // D), y_ret], [], (BF16, BF16), ())


def _mix_bwd_epilogue(proj, y_ret, y_mla):
    def fn(r, ins, outs, first):
        gr_ref, gm_ref, yr_ref, ym_ref = ins
        dgr_ref, dgm_ref, dyr_ref, dym_ref = outs
        sr, sm = _sigmoid(gr_ref[...].astype(F32)), _sigmoid(gm_ref[...].astype(F32))
        dgr_ref[...] = (r * yr_ref[...].astype(F32) * (sr * (1.0 - sr))).astype(BF16)
        dgm_ref[...] = (r * ym_ref[...].astype(F32) * (sm * (1.0 - sm))).astype(BF16)
        dyr_ref[...] = (r * sr).astype(BF16)
        dym_ref[...] = (r * sm).astype(BF16)

    return (fn, [(proj, P_GR // D), (proj, P_GM // D), y_ret, y_mla], [], (BF16,) * 4, ())


def _head_epilogue(h3, pp, tgt, g, b):
    def fn(r, ins, outs, first):
        h_ref, pp_ref, t_ref, g_ref, b_ref = ins
        dh_ref, dgl_ref, dpp_ref, dg_ref, db_ref, loss_ref = outs

        sg, pp, gain = _sigmoid(r), pp_ref[...], g_ref[...]
        pre = ALPHA * h_ref[...] + sg * pp
        xc = pre - jnp.mean(pre, axis=-1, keepdims=True)
        rstd = lax.rsqrt(jnp.mean(jnp.square(xc), axis=-1, keepdims=True) + EPS)
        xhat = xc * rstd
        err = xhat * gain + b_ref[...] - t_ref[...]
        dy = err * (1.0 / D)
        dyg = dy * gain
        dpre = rstd * (dyg - jnp.mean(dyg, axis=-1, keepdims=True) - xhat * jnp.mean(dyg * xhat, axis=-1, keepdims=True))
        dh_ref[...] = ALPHA * dpre
        dgl_ref[...] = (dpre * pp * (sg * (1.0 - sg))).astype(BF16)
        dpp_ref[...] = (dpre * sg).astype(BF16)
        _accum(dg_ref, jnp.sum(dy * xhat, axis=0, keepdims=True), first)
        _accum(db_ref, jnp.sum(dy, axis=0, keepdims=True), first)
        part = 0.5 * jnp.sum(jnp.mean(jnp.square(err), axis=-1, keepdims=True), axis=0, keepdims=True)
        _accum(loss_ref, jnp.broadcast_to(part, loss_ref.shape), first)

    return (fn, [h3, pp, tgt], [g, b], (F32, BF16, BF16), (1, 1, 8))


def _delta_epilogue(o):
    def fn(r, ins, outs, first):
        (o_ref,) = ins
        db_ref, dl_ref = outs
        db_ref[...] = r.astype(BF16)
        for h in range(HEADS):
            sl = slice(h * MLA_DV, (h + 1) * MLA_DV)
            dl = jnp.sum(r[:, sl] * o_ref[:, sl].astype(F32), axis=-1, keepdims=True)
            dl_ref[:, sl] = jnp.broadcast_to(dl, (r.shape[0], MLA_DV))

    return (fn, [o], [], (BF16, F32), ())


def _ret_consts():
    L = RET_L
    lg = jnp.log(1.0 - 2.0 ** (-5.0 - jnp.arange(HEADS, dtype=F32)))[:, None, None]
    idx = jnp.arange(L, dtype=F32)
    ch = jnp.arange(L) // CHUNK
    dist = idx[:, None] - idx[None, :]
    same = (ch[:, None] == ch[None, :])[None]
    earlier = (ch[None, :] < ch[:, None])[None]
    dm = jnp.where(same, jnp.exp(lg * jnp.abs(dist)[None]), jnp.where(earlier, jnp.exp(lg * dist[None]), 0.0))
    xi = jnp.broadcast_to(jnp.exp(lg * (idx + 1.0)[None, :, None]), (HEADS, L, 128))
    zeta = jnp.broadcast_to(jnp.exp(lg * (L - 1.0 - idx)[None, :, None]), (HEADS, L, 128))
    gl = jnp.broadcast_to(jnp.exp(lg * float(L)), (HEADS, 8, 128))
    return dm.astype(F32), xi.astype(F32), zeta.astype(F32), gl.astype(F32)


def _whole(arr):
    return pl.BlockSpec(arr.shape, lambda n, _nd=arr.ndim: (0,) * _nd)


def _ret_fwd(q, k, v, proj, gn_g, consts, T):
    dm, xi, zeta, gl = consts
    L = RET_L
    n_sc = T // L

    def body(q_ref, k_ref, v_ref, rg_ref, g_ref, dm_ref, xi_ref, ze_ref, gl_ref, y_ref, yr_ref, s_ref, st_ref):
        @pl.when(pl.program_id(0) == 0)
        def _():
            st_ref[...] = jnp.zeros_like(st_ref)

        for h in range(HEADS):
            ks, vs = slice(h * RET_DK, (h + 1) * RET_DK), slice(h * RET_DV, (h + 1) * RET_DV)
            qq, kk, vv = q_ref[:, ks], k_ref[:, ks], v_ref[:, vs]
            st = st_ref[h]
            s_ref[h, 0] = st
            p = (_dot(qq, kk, 1, 1) * dm_ref[h]).astype(BF16)
            cross = _dot(qq, st.astype(BF16), 1, 0)
            xi_c = jnp.concatenate([xi_ref[h], xi_ref[h]], axis=1)
            y = _dot(p, vv, 1, 0) + cross * xi_c
            y_ref[:, vs] = y
            yr_ref[:, vs] = _gn_gate(y, rg_ref[:, vs].astype(F32), g_ref[:, vs]).astype(BF16)
            kz = (kk.astype(F32) * ze_ref[h]).astype(BF16)
            gl2 = jnp.concatenate([gl_ref[h, 0:1, :], gl_ref[h, 0:1, :]], axis=1)
            st_ref[h] = st * gl2 + _dot(kz, vv, 0, 0)

    return _pcall(
        body, grid=(n_sc,),
        in_specs=[pl.BlockSpec((L, 1024), lambda n: (n, 0)), pl.BlockSpec((L, 1024), lambda n: (n, 0)),
                  pl.BlockSpec((L, 2048), lambda n: (n, P_RV // 2048)), pl.BlockSpec((L, 2048), lambda n: (n, P_RG // 2048)),
                  _whole(gn_g), _whole(dm), _whole(xi), _whole(zeta), _whole(gl)],
        out_specs=[pl.BlockSpec((L, 2048), lambda n: (n, 0)), pl.BlockSpec((L, 2048), lambda n: (n, 0)),
                   pl.BlockSpec((HEADS, 1, 128, 256), lambda n: (0, n, 0, 0))],
        out_shape=[jax.ShapeDtypeStruct((T, HEADS * RET_DV), F32), jax.ShapeDtypeStruct((T, HEADS * RET_DV), BF16),
                   jax.ShapeDtypeStruct((HEADS, n_sc, 128, 256), F32)],
        scratch_shapes=[pltpu.VMEM((HEADS, 128, 256), F32)], name="ret_fwd",
        compiler_params=pltpu.CompilerParams(dimension_semantics=("arbitrary",)),
    )(q, k, v, proj, gn_g, dm, xi, zeta, gl)


def _ret_bwd(q, k, v, y, proj, gn_g, states, dyr, consts, tabs, T):
    dm, xi, zeta, gl = consts
    cos128, sin128, _, _ = tabs
    L = RET_L
    n_sc = T // L

    def body(q_ref, k_ref, v_ref, y_ref, rg_ref, g_ref, d_ref, s_ref, dm_ref, xi_ref, ze_ref, gl_ref, c_ref, sn_ref,
             dq_ref, dk_ref, dv_ref, drg_ref, dg_ref, gs_ref):
        @pl.when(pl.program_id(0) == 0)
        def _():
            gs_ref[...] = jnp.zeros_like(gs_ref)

        c, sn = c_ref[...], sn_ref[...]
        dgs = []
        for h in range(HEADS):
            ks, vs = slice(h * RET_DK, (h + 1) * RET_DK), slice(h * RET_DV, (h + 1) * RET_DV)
            dy, drg, dg = _gn_gate_bwd(y_ref[:, vs], rg_ref[:, vs].astype(F32), g_ref[:, vs], d_ref[:, vs].astype(F32))
            drg_ref[:, vs] = drg.astype(BF16)
            dgs.append(dg)
            qq, kk, vv, dyy = q_ref[:, ks], k_ref[:, ks], v_ref[:, vs], dy.astype(BF16)
            dmm = dm_ref[h]
            gb = gs_ref[h].astype(BF16)
            sb = s_ref[h, 0].astype(BF16)
            xi_c = jnp.concatenate([xi_ref[h], xi_ref[h]], axis=1)
            pb = (_dot(qq, kk, 1, 1) * dmm).astype(BF16)
            kz = (kk.astype(F32) * ze_ref[h]).astype(BF16)
            dv_ref[:, vs] = (_dot(pb, dyy, 0, 0) + _dot(kz, gb, 1, 0)).astype(BF16)
            da = (_dot(dyy, vv, 1, 1) * dmm).astype(BF16)
            dyx = (dyy.astype(F32) * xi_c).astype(BF16)
            dq = _dot(da, kk, 1, 0) + _dot(dyx, sb, 1, 1)
            dk = _dot(da, qq, 0, 0) + _dot(vv, gb, 1, 1) * ze_ref[h]
            dq_ref[:, ks] = _rope128_t(dq, c, sn).astype(BF16)
            dk_ref[:, ks] = (_rope128_t(dk, c, sn) * RET_SCALE).astype(BF16)
            gl2 = jnp.concatenate([gl_ref[h, 0:1, :], gl_ref[h, 0:1, :]], axis=1)
            gs_ref[h] = gs_ref[h] * gl2 + _dot(qq, dyx, 0, 0)
        _accum(dg_ref, jnp.concatenate(dgs, axis=1))

    rev = lambda n: n_sc - 1 - n
    return _pcall(
        body, grid=(n_sc,),
        in_specs=[pl.BlockSpec((L, 1024), lambda n: (rev(n), 0)), pl.BlockSpec((L, 1024), lambda n: (rev(n), 0)),
                  pl.BlockSpec((L, 2048), lambda n: (rev(n), P_RV // 2048)), pl.BlockSpec((L, 2048), lambda n: (rev(n), 0)),
                  pl.BlockSpec((L, 2048), lambda n: (rev(n), P_RG // 2048)), _whole(gn_g),
                  pl.BlockSpec((L, 2048), lambda n: (rev(n), 0)),
                  pl.BlockSpec((HEADS, 1, 128, 256), lambda n: (0, rev(n), 0, 0)),
                  _whole(dm), _whole(xi), _whole(zeta), _whole(gl),
                  pl.BlockSpec((L, 128), lambda n: (rev(n), 0)), pl.BlockSpec((L, 128), lambda n: (rev(n), 0))],
        out_specs=[pl.BlockSpec((L, 1024), lambda n: (rev(n), 0)), pl.BlockSpec((L, 1024), lambda n: (rev(n), 0)),
                   pl.BlockSpec((L, 2048), lambda n: (rev(n), 0)), pl.BlockSpec((L, 2048), lambda n: (rev(n), 0)),
                   pl.BlockSpec((1, 2048), lambda n: (0, 0))],
        out_shape=[jax.ShapeDtypeStruct((T, 1024), BF16), jax.ShapeDtypeStruct((T, 1024), BF16), jax.ShapeDtypeStruct((T, 2048), BF16),
                   jax.ShapeDtypeStruct((T, 2048), BF16), jax.ShapeDtypeStruct((1, 2048), F32)],
        scratch_shapes=[pltpu.VMEM((HEADS, 128, 256), F32)], name="ret_bwd",
        compiler_params=pltpu.CompilerParams(dimension_semantics=("arbitrary",)),
    )(q, k, v, y, proj, gn_g, dyr, states, dm, xi, zeta, gl, cos128, sin128)


def _diag_mask(nrows, ncols, row0):
    row = row0 + lax.broadcasted_iota(jnp.int32, (nrows, ncols), 0)
    col = lax.broadcasted_iota(jnp.int32, (nrows, ncols), 1)
    return lax.shift_right_logical(col, 6) <= lax.shift_right_logical(row, 6)


def _diag_spans(t):
    h = t // 2
    return ((0, h, h), (h, h, t)) if h % 128 == 0 else ((0, t, t),)


def _tri_steps(nb, by_key):
    if by_key:
        pairs = [(i, j) for j in range(nb) for i in range(j, nb)]
    else:
        pairs = [(i, j) for i in range(nb) for j in range(i + 1)]
    return jnp.array([a for a, _ in pairs], jnp.int32), jnp.array([b for _, b in pairs], jnp.int32)


def _attn_fwd(qn, qpe, kv, kpe, T):
    t = min(ATT_TF, T)
    nb = T // t
    ii, jj = _tri_steps(nb, by_key=False)

    hp = ATT_HP
    w = 128 * hp

    def body(ii_ref, jj_ref, qn_ref, qp_ref, kn_ref, kp_ref, v_ref, o_ref, lse_ref, m_sc, l_sc, acc_sc):
        st = pl.program_id(1)
        i, j = ii_ref[st], jj_ref[st]

        @pl.when(j == 0)
        def _():
            m_sc[...] = jnp.full_like(m_sc, NEG)
            l_sc[...] = jnp.zeros_like(l_sc)
            acc_sc[...] = jnp.zeros_like(acc_sc)

        def update(diag):
            kp = kp_ref[...]
            spans = _diag_spans(t) if diag else ((0, t, t),)
            for hh in range(hp):
                sl = slice(128 * hh, 128 * (hh + 1))
                for row0, nr, nkeys in spans:
                    rows = slice(row0, row0 + nr)
                    q = jnp.concatenate([qn_ref[rows, sl], qp_ref[rows, sl]], axis=1)
                    k = jnp.concatenate([kn_ref[:nkeys, sl], kp[:nkeys]], axis=1)
                    s = _dot(q, k, 1, 1)
                    if diag:
                        s = jnp.where(_diag_mask(nr, nkeys, row0), s, NEG)
                    m_prev = m_sc[rows, sl]
                    m_new = jnp.maximum(m_prev, jnp.max(s, axis=1, keepdims=True))
                    a = jnp.exp2(m_prev - m_new)
                    p = jnp.exp2(s - m_new[:, 0:1])
                    l_sc[rows, sl] = a * l_sc[rows, sl] + jnp.sum(p, axis=1, keepdims=True)
                    acc_sc[rows, sl] = a * acc_sc[rows, sl] + _dot(p.astype(BF16), v_ref[:nkeys, sl], 1, 0)
                    m_sc[rows, sl] = m_new

        @pl.when(j < i)
        def _():
            update(False)

        @pl.when(j == i)
        def _():
            update(True)
            o_ref[...] = (acc_sc[...] / l_sc[...]).astype(BF16)
            lse_ref[...] = m_sc[...] + jnp.log2(l_sc[...])

    qs = pl.BlockSpec((t, w), lambda h, s, ii, jj: (ii[s], h))
    grid_spec = pltpu.PrefetchScalarGridSpec(
        num_scalar_prefetch=2, grid=(HEADS // hp, int(ii.shape[0])),
        in_specs=[qs, qs, pl.BlockSpec((t, w), lambda h, s, ii, jj: (jj[s], h)), pl.BlockSpec((t, 128), lambda h, s, ii, jj: (jj[s], 0)),
                  pl.BlockSpec((t, w), lambda h, s, ii, jj: (jj[s], HEADS // hp + h))],
        out_specs=[qs, qs],
        scratch_shapes=[pltpu.VMEM((t, w), F32), pltpu.VMEM((t, w), F32), pltpu.VMEM((t, w), F32)])
    return _pcall(
        body, grid_spec=grid_spec, out_shape=[jax.ShapeDtypeStruct((T, D), BF16), jax.ShapeDtypeStruct((T, D), F32)], name="attn_fwd",
        compiler_params=pltpu.CompilerParams(dimension_semantics=("arbitrary", "arbitrary")),
    )(ii, jj, qn, qpe, kv, kpe, kv)


def _attn_bwd(qn, qpe, kv, kpe, do, lse, delta, T):
    t = min(ATT_TB, T)
    nb = T // t
    ii, jj = _tri_steps(nb, by_key=True)

    def body(ii_ref, jj_ref, qn_ref, qp_ref, kn_ref, kp_ref, v_ref, do_ref, lse_ref, dl_ref,
             dqn_ref, dqp_ref, dkn_ref, dkp_ref, dv_ref, dk_sc, dv_sc):
        st = pl.program_id(1)
        i, j = ii_ref[st], jj_ref[st]

        @pl.when(st == 0)
        def _():
            dqn_ref[...] = jnp.zeros_like(dqn_ref)
            dqp_ref[...] = jnp.zeros_like(dqp_ref)

        @pl.when(i == j)
        def _():
            dk_sc[...] = jnp.zeros_like(dk_sc)
            dv_sc[...] = jnp.zeros_like(dv_sc)

        def update(diag):
            for row0, nr, nkeys in (_diag_spans(t) if diag else ((0, t, t),)):
                rows = slice(row0, row0 + nr)
                q = jnp.concatenate([qn_ref[rows, :], qp_ref[rows, :]], axis=1)
                k = jnp.concatenate([kn_ref[:nkeys, :], kp_ref[:nkeys, :]], axis=1)
                dob = do_ref[rows, :]
                s = _dot(q, k, 1, 1)
                if diag:
                    s = jnp.where(_diag_mask(nr, nkeys, row0), s, NEG)
                p = jnp.exp2(s - lse_ref[rows, 0:1])
                dv_sc[:nkeys, :] += _dot(p.astype(BF16), dob, 0, 0)
                dp = _dot(dob, v_ref[:nkeys, :], 1, 1)
                ds = (p * (dp - dl_ref[rows, 0:1])).astype(BF16)
                dk_sc[:nkeys, :] += _dot(ds, q, 0, 0)
                dq = _dot(ds, k, 1, 0) * MLA_SCALE
                grows = pl.ds(pl.multiple_of(i * t + row0, nr), nr)
                dqn_ref[grows, :] += dq[:, :128]
                dqp_ref[grows, :] += dq[:, 128:]

        @pl.when(i > j)
        def _():
            update(False)

        @pl.when(i == j)
        def _():
            update(True)

        @pl.when(i == nb - 1)
        def _():
            dkn_ref[...] = (dk_sc[:, :128] * (1.0 / LOG2E)).astype(BF16)
            dkp_ref[...] = dk_sc[:, 128:] * (1.0 / LOG2E)
            dv_ref[...] = dv_sc[...].astype(BF16)

    qs = pl.BlockSpec((t, 128), lambda h, s, ii, jj: (ii[s], h))
    ks = pl.BlockSpec((t, 128), lambda h, s, ii, jj: (jj[s], h))
    hs = pl.BlockSpec((T, 128), lambda h, s, ii, jj: (0, h))
    grid_spec = pltpu.PrefetchScalarGridSpec(
        num_scalar_prefetch=2, grid=(HEADS, int(ii.shape[0])),
        in_specs=[qs, qs, ks, pl.BlockSpec((t, 128), lambda h, s, ii, jj: (jj[s], 0)),
                  pl.BlockSpec((t, 128), lambda h, s, ii, jj: (jj[s], HEADS + h)), qs, qs, qs],
        out_specs=[hs, hs, ks, ks, ks],
        scratch_shapes=[pltpu.VMEM((t, 256), F32), pltpu.VMEM((t, 128), F32)])
    return _pcall(
        body, grid_spec=grid_spec,
        out_shape=[jax.ShapeDtypeStruct((T, D), F32), jax.ShapeDtypeStruct((T, D), F32), jax.ShapeDtypeStruct((T, D), BF16),
                   jax.ShapeDtypeStruct((T, D), F32), jax.ShapeDtypeStruct((T, D), BF16)], name="attn_bwd",
        compiler_params=pltpu.CompilerParams(dimension_semantics=("arbitrary", "arbitrary")),
    )(ii, jj, qn, qpe, kv, kpe, kv, do, lse, delta)


def _mesh_pos():
    x, y, c = lax.axis_index("x"), lax.axis_index("y"), lax.axis_index("c")
    return x, y, c, 4 * x + 2 * y + c


def _peer(x, y, c, k):
    px, py, pc = (x + ((k >> 2) & 1)) % 2, (y + ((k >> 1) & 1)) % 2, (c + (k & 1)) % 2
    return (px, py, pc), 4 * px + 2 * py + pc


_ANY = pl.BlockSpec(memory_space=pl.ANY)


def _rcopy(src, dst, send_sems, recv_sems, k, to):
    return pltpu.make_async_remote_copy(src_ref=src, dst_ref=dst, send_sem=send_sems.at[k], recv_sem=recv_sems.at[k],
                                        device_id=to, device_id_type=pl.DeviceIdType.MESH)


def _all_gather(wsh, ssh):
    def body(w_ref, s_ref, wall_ref, sall_ref, send_sems, recv_sems, loc_sems):
        x, y, c, me = _mesh_pos()
        sib = (x, y, 1 - c)
        chips = [(1 - x, y), (x, 1 - y), (1 - x, 1 - y)]
        slot = lambda px, py, pc: 4 * px + 2 * py + pc
        loc = [pltpu.make_async_copy(s_ref, sall_ref.at[me], loc_sems.at[0])]
        for cp in loc:
            cp.start()
        sends, fwd_waits = [], []
        for n, (src, dst) in enumerate(((w_ref, wall_ref), (s_ref, sall_ref))):
            o = 7 * n
            sends.append(_rcopy(src, dst.at[me], send_sems, recv_sems, o, sib))
            for j, chip in enumerate(chips):
                sends.append(_rcopy(src, dst.at[me], send_sems, recv_sems, o + 1 + j, (*chip, c)))
        for cp in sends:
            cp.start()
        for n, (src, dst) in enumerate(((w_ref, wall_ref), (s_ref, sall_ref))):
            o = 7 * n
            for j, chip in enumerate(chips):
                got = dst.at[slot(*chip, c)]
                _rcopy(src, got, send_sems, recv_sems, o + 1 + j, sib).wait_recv()
                fw = _rcopy(got, got, send_sems, recv_sems, o + 4 + j, sib)
                fw.start()
                sends.append(fw)
            fwd_waits.append(_rcopy(src, dst.at[slot(x, y, 1 - c)], send_sems, recv_sems, o, sib))
            for j, chip in enumerate(chips):
                fwd_waits.append(_rcopy(src, dst.at[slot(*chip, 1 - c)], send_sems, recv_sems, o + 4 + j, sib))
        for cp in fwd_waits:
            cp.wait_recv()
        for cp in sends:
            cp.wait_send()
        for cp in loc:
            cp.wait()

    wall, sall = _pcall(
        body, in_specs=[_ANY, _ANY], out_specs=[_ANY, _ANY],
        out_shape=[jax.ShapeDtypeStruct((N_DEV,) + wsh.shape, wsh.dtype), jax.ShapeDtypeStruct((N_DEV,) + ssh.shape, ssh.dtype)],
        scratch_shapes=[pltpu.SemaphoreType.DMA((14,)), pltpu.SemaphoreType.DMA((14,)), pltpu.SemaphoreType.DMA((1,))],
        name="all_gather_weights",
    )(wsh, ssh)
    me = 4 * lax.axis_index("x") + 2 * lax.axis_index("y") + lax.axis_index("c")
    return lax.dynamic_update_index_in_dim(wall, wsh, me, axis=0), sall


_HBM = pl.BlockSpec(memory_space=pltpu.HBM)
_SEM = pl.BlockSpec(memory_space=pltpu.SEMAPHORE)
_EFFECT = pltpu.SideEffectType.DATAFLOW_SIDE_EFFECTING


def _other_chips(x, y):
    return [(1 - x, y), (x, 1 - y), (1 - x, 1 - y)]


def _gather_start(wsh, order_after):
    def body(w_ref, land_ref, dep_ref, send_sems, recv_sems, w_thru, land_thru, token):
        x, y, c, me = _mesh_pos()
        _rcopy(w_ref, land_ref.at[me], send_sems, recv_sems, 0, (x, y, 1 - c)).start()
        for j, chip in enumerate(_other_chips(x, y)):
            _rcopy(w_ref, land_ref.at[me], send_sems, recv_sems, 1 + j, (*chip, c)).start()
        token[...] = jnp.zeros_like(token)

    shape = (N_DEV,) + wsh.shape
    land = pltpu.with_memory_space_constraint(lax.empty(shape, wsh.dtype), pltpu.HBM)
    return _pcall(
        body, name="gather_start",
        out_shape=(pltpu.SemaphoreType.DMA((4,)), pltpu.SemaphoreType.DMA((4,)), pltpu.HBM(wsh.shape, wsh.dtype),
                   pltpu.HBM(shape, wsh.dtype), jax.ShapeDtypeStruct((8, 128), F32)),
        in_specs=(_HBM, _HBM, _ANY), out_specs=(_SEM, _SEM, _HBM, _HBM, pl.BlockSpec(memory_space=pltpu.VMEM)),
        input_output_aliases={0: 2, 1: 3}, compiler_params=pltpu.CompilerParams(has_side_effects=_EFFECT),
    )(pltpu.with_memory_space_constraint(wsh, pltpu.HBM), land, order_after)


def _gather_wait(send_sems, recv_sems, w_thru, land_thru, after):
    def body(w_ref, land_ref, send_sems, recv_sems, after_ref, w_out, land_out):
        x, y, c, _ = _mesh_pos()
        senders = [(x, y, 1 - c)] + [(*chip, c) for chip in _other_chips(x, y)]
        for k, (px, py, pc) in enumerate(senders):
            cp = _rcopy(w_ref, land_ref.at[4 * px + 2 * py + pc], send_sems, recv_sems, k, (px, py, pc))
            cp.wait_send()
            cp.wait_recv()

    return _pcall(
        body, name="gather_wait", out_shape=(pltpu.HBM(w_thru.shape, w_thru.dtype), pltpu.HBM(land_thru.shape, land_thru.dtype)),
        in_specs=(_HBM, _HBM, _SEM, _SEM, _ANY), out_specs=(_HBM, _HBM), input_output_aliases={0: 0, 1: 1},
        compiler_params=pltpu.CompilerParams(has_side_effects=_EFFECT),
    )(w_thru, land_thru, send_sems, recv_sems, after)


def _gather_finish(wsh, land):
    def body(land_ref, out_ref, send_sems, recv_sems):
        x, y, c, _ = _mesh_pos()
        sib = (x, y, 1 - c)
        sends, recvs = [], []
        for j, (px, py) in enumerate(_other_chips(x, y)):
            mine, theirs = 4 * px + 2 * py + c, 4 * px + 2 * py + (1 - c)
            sends.append(_rcopy(land_ref.at[mine], out_ref.at[mine], send_sems, recv_sems, j, sib))
            recvs.append(_rcopy(land_ref.at[theirs], out_ref.at[theirs], send_sems, recv_sems, j, sib))
        for cp in sends:
            cp.start()
        for cp in recvs:
            cp.wait_recv()
        for cp in sends:
            cp.wait_send()

    done = _pcall(
        body, in_specs=[_ANY], out_specs=_ANY, out_shape=jax.ShapeDtypeStruct(land.shape, land.dtype),
        input_output_aliases={0: 0},
        scratch_shapes=[pltpu.SemaphoreType.DMA((3,)), pltpu.SemaphoreType.DMA((3,))], name="gather_finish",
    )(land)
    me = 4 * lax.axis_index("x") + 2 * lax.axis_index("y") + lax.axis_index("c")
    return lax.dynamic_update_index_in_dim(done, wsh, me, axis=0)


_ROW_TILES = (400, 368, 352, 256, 128, 16, 8)


def _exchange_sibling(g4, name):
    def body(g_ref, r_ref, send_sems, recv_sems):
        x, y, c, _ = _mesh_pos()
        sib = (x, y, 1 - c)
        cps = [_rcopy(g_ref.at[q, 1 - c], r_ref.at[q], send_sems, recv_sems, q, sib) for q in range(4)]
        for cp in cps:
            cp.start()
        for cp in cps:
            cp.wait()

    n, _, R, C = g4.shape
    return _pcall(
        body, in_specs=[_ANY], out_specs=_ANY, out_shape=jax.ShapeDtypeStruct((n, R, C), g4.dtype),
        scratch_shapes=[pltpu.SemaphoreType.DMA((4,)), pltpu.SemaphoreType.DMA((4,))], name=name,
    )(g4)


def _sum_sibling(g4, rsib, cvec, name):
    n, _, R, C = g4.shape
    tr = _pick(R, _ROW_TILES)

    def body(c_ref, g_ref, r_ref, o_ref):
        o_ref[...] = (g_ref[...].astype(F32) + r_ref[...].astype(F32)).astype(o_ref.dtype)

    grid_spec = pltpu.PrefetchScalarGridSpec(
        num_scalar_prefetch=1, grid=(n, R // tr),
        in_specs=[pl.BlockSpec((None, None, tr, C), lambda q, i, cr: (q, cr[0], i, 0)), pl.BlockSpec((None, tr, C), lambda q, i, cr: (q, i, 0))],
        out_specs=pl.BlockSpec((None, tr, C), lambda q, i, cr: (q, i, 0)))
    return _pcall(body, grid_spec=grid_spec, out_shape=jax.ShapeDtypeStruct((n, R, C), g4.dtype), name=name)(cvec, g4, rsib)


def _chips_start(part, name, order_after=None):
    n_in = 2 + int(order_after is not None)

    def body(*refs):
        p_ref, land_ref = refs[:2]
        send_sems, recv_sems, token = refs[n_in], refs[n_in + 1], refs[n_in + 4]
        x, y, c, _ = _mesh_pos()
        myq = 2 * x + y
        for j, (px, py) in enumerate(_other_chips(x, y)):
            _rcopy(p_ref.at[2 * px + py], land_ref.at[myq], send_sems, recv_sems, j, (px, py, c)).start()
        token[...] = jnp.zeros_like(token)

    land = pltpu.with_memory_space_constraint(lax.empty(part.shape, part.dtype), pltpu.HBM)
    extra = () if order_after is None else (order_after,)
    return _pcall(
        body, name=name,
        out_shape=(pltpu.SemaphoreType.DMA((3,)), pltpu.SemaphoreType.DMA((3,)), pltpu.HBM(part.shape, part.dtype),
                   pltpu.HBM(part.shape, part.dtype), jax.ShapeDtypeStruct((8, 128), F32)),
        in_specs=(_HBM, _HBM) + (_ANY,) * len(extra),
        out_specs=(_SEM, _SEM, _HBM, _HBM, pl.BlockSpec(memory_space=pltpu.VMEM)),
        input_output_aliases={0: 2, 1: 3}, compiler_params=pltpu.CompilerParams(has_side_effects=_EFFECT),
    )(pltpu.with_memory_space_constraint(part, pltpu.HBM), land, *extra)


def _chips_wait(send_sems, recv_sems, p_thru, land_thru, after, name):
    def body(p_ref, land_ref, send_sems, recv_sems, after_ref, p_out, land_out):
        x, y, c, _ = _mesh_pos()
        for j, (px, py) in enumerate(_other_chips(x, y)):
            q = 2 * px + py
            cp = _rcopy(p_ref.at[q], land_ref.at[q], send_sems, recv_sems, j, (px, py, c))
            cp.wait_send()
            cp.wait_recv()

    return _pcall(
        body, name=name, out_shape=(pltpu.HBM(p_thru.shape, p_thru.dtype), pltpu.HBM(p_thru.shape, p_thru.dtype)),
        in_specs=(_HBM, _HBM, _SEM, _SEM, _ANY), out_specs=(_HBM, _HBM), input_output_aliases={0: 0, 1: 1},
        compiler_params=pltpu.CompilerParams(has_side_effects=_EFFECT),
    )(p_thru, land_thru, send_sems, recv_sems, after)


def _sum_chips(part, land, qvec, name):
    n, R, C = part.shape
    tr = _pick(R, _ROW_TILES)

    def body(q_ref, p_ref, *refs):
        o_ref = refs[n]
        acc = None
        for q in range(n):
            term = jnp.where(q_ref[0] == q, p_ref[...], refs[q][...]).astype(F32)
            acc = term if acc is None else acc + term
        o_ref[...] = acc

    def land_spec(q):
        return pl.BlockSpec((None, tr, C), lambda i, m: (jnp.where(m[0] == q, (q + 1) % n, q), i, 0))

    grid_spec = pltpu.PrefetchScalarGridSpec(
        num_scalar_prefetch=1, grid=(R // tr,),
        in_specs=[pl.BlockSpec((None, tr, C), lambda i, m: (m[0], i, 0))] + [land_spec(q) for q in range(n)],
        out_specs=pl.BlockSpec((tr, C), lambda i, m: (i, 0)))
    return _pcall(body, grid_spec=grid_spec, out_shape=jax.ShapeDtypeStruct((R, C), F32), name=name)(qvec, part, *([land] * n))


def _exchange_small(gsmall):
    def body(s_ref, srecv_ref, send_sems, recv_sems, loc_sem):
        x, y, c, me = _mesh_pos()
        loc = pltpu.make_async_copy(s_ref, srecv_ref.at[me], loc_sem.at[0])
        loc.start()
        sends, recvs = [], []
        for k in range(1, N_DEV):
            to, pidx = _peer(x, y, c, k)
            sends.append(_rcopy(s_ref, srecv_ref.at[me], send_sems, recv_sems, k - 1, to))
            recvs.append(_rcopy(s_ref, srecv_ref.at[pidx], send_sems, recv_sems, k - 1, to))
        for cp in sends:
            cp.start()
        for cp in recvs:
            cp.wait_recv()
        for cp in sends:
            cp.wait_send()
        loc.wait()

    return _pcall(
        body, in_specs=[_ANY], out_specs=_ANY, out_shape=jax.ShapeDtypeStruct((N_DEV,) + gsmall.shape, gsmall.dtype),
        scratch_shapes=[pltpu.SemaphoreType.DMA((7,)), pltpu.SemaphoreType.DMA((7,)), pltpu.SemaphoreType.DMA((1,))],
        name="exchange_small",
    )(gsmall)


def _sum_slots(recv, name):
    n, R, C = recv.shape
    tr = _pick(R, _ROW_TILES)

    def body(r_ref, o_ref):
        acc = r_ref[0].astype(F32)
        for s in range(1, n):
            acc = acc + r_ref[s].astype(F32)
        o_ref[...] = acc

    return _pcall(body, grid=(R // tr,), in_specs=[pl.BlockSpec((n, tr, C), lambda i: (0, i, 0))],
                  out_specs=pl.BlockSpec((tr, C), lambda i: (i, 0)), out_shape=jax.ShapeDtypeStruct((R, C), F32), name=name)(recv)


def _adamw(w, g, m, v, name, order_after=None):
    shape = w.shape
    w2, g2, m2, v2 = (a.reshape(-1, shape[-1]) for a in (w, g, m, v))
    R, C = w2.shape
    tr = _pick(R, (256, 128, 64, 32, 16, 8)) if R > 256 else R
    n_in = 4 + int(order_after is not None)

    def body(*refs):
        w_ref, g_ref, m_ref, v_ref = refs[:4]
        d_ref, nm_ref, nv_ref = refs[n_in:]
        gg = g_ref[...]
        nm = ADAM_B1 * m_ref[...] + (1.0 - ADAM_B1) * gg
        nv = ADAM_B2 * v_ref[...] + (1.0 - ADAM_B2) * jnp.square(gg)
        m_hat = nm / (1.0 - ADAM_B1 ** ADAM_STEP)
        v_hat = nv / (1.0 - ADAM_B2 ** ADAM_STEP)
        d_ref[...] = -ADAM_LR * (m_hat / (jnp.sqrt(v_hat) + ADAM_EPS) + ADAM_WD * w_ref[...])
        nm_ref[...] = nm
        nv_ref[...] = nv

    spec = pl.BlockSpec((tr, C), lambda i: (i, 0))
    in_specs, args = [spec] * 4, [w2, g2, m2, v2]
    if order_after is not None:
        in_specs.append(pl.BlockSpec(memory_space=pl.ANY))
        args.append(order_after)
    d, nm, nv = _pcall(body, grid=(R // tr,), in_specs=in_specs, out_specs=[spec] * 3,
                       out_shape=[jax.ShapeDtypeStruct((R, C), F32)] * 3, name=name)(*args)
    return d.reshape(shape), nm.reshape(shape), nv.reshape(shape)


def _to_rows(name, w):
    w = w[0]
    if name in ("ffn1_w_in", "ffn2_w_in", "w_in"):
        return w.T
    if name in ("w_uq", "w_ukv", "ple_w_proj"):
        return w.T.reshape(-1, 1024)
    return w


def _from_rows(name, g, shape):
    if name in ("ffn1_w_in", "ffn2_w_in", "w_in"):
        return g.T.reshape(shape)
    if name in ("w_uq", "w_ukv", "ple_w_proj"):
        return g.reshape(-1, shape[1]).T.reshape(shape)
    return g.reshape(shape)


def _unpack(wall, group):
    out, off = {}, 0
    for name, r in group:
        out[name] = wall[:, off:off + r, :].reshape(N_DEV * r, 1024)
        off += _pad16(r)
    return out


def _w_in_internal(wt):
    return jnp.concatenate([wt[0:6144], wt[6720:8768], wt[6144:6720], jnp.zeros((P_W - W_IN_COLS, 1024), wt.dtype)], axis=0)


def _w_in_external(d):
    return jnp.concatenate([d[0:6144], d[8192:8768], d[6144:8192]], axis=0)


def _rope_tables(positions):
    pos = positions[0].astype(F32)

    def cs(half):
        inv = ROPE_BASE ** (-jnp.arange(half, dtype=F32) / half)
        ang = pos[:, None] * inv
        return jnp.cos(ang), jnp.sin(ang)

    c, s = cs(64)
    c2, s2 = cs(32)
    return (jnp.concatenate([c, c], axis=1), jnp.concatenate([-s, s], axis=1),
            jnp.concatenate([c2, c2, c2, c2], axis=1), jnp.concatenate([-s2, s2, -s2, s2], axis=1))


def _local_step(x, p, positions, target, W, ln_g, ln_b, gn_g, qg, kvg, early=None, rest_weights=None, start_token=None):
    T = x.shape[0]
    tabs = _rope_tables(positions)
    rc = _ret_consts()
    lg = [ln_g[i:i + 1] for i in range(4)]
    lb = [ln_b[i:i + 1] for i in range(4)]
    pb = p.astype(BF16)

    hs1, gu1, *xb = _ffn_in(x, W["ffn1_w_in"], "ffn1_in", order_after=start_token)
    xb = xb[0] if xb else x
    f1, h1, h1b = _mm(hs1, W["ffn1_w_out"], name="ffn1_out", tm=LN_TM, epilogue=_ln_epilogue(x, 0.5, lg[0], lb[0]))
    if rest_weights is not None:
        W = {**W, **rest_weights(h1b)}
    w_in_t = _w_in_internal(W["w_in"])
    wuq = W["w_uq"].reshape(1536, LORA).reshape(HEADS, 192, LORA)
    wuq = jnp.concatenate([wuq[:, :128].reshape(1024, LORA), wuq[:, 128:].reshape(512, LORA)], axis=0)
    wukv = W["w_ukv"].reshape(2048, LORA).reshape(HEADS, 2, 128, LORA).transpose(1, 0, 2, 3).reshape(2048, LORA)
    wp_t = W["ple_w_proj"].reshape(1024, D_PLE)
    proj = _mm(h1b, w_in_t, tb=True, out_dtype=BF16, name="mixer_in")
    rq, rk, qn, kvn, kpe = _mixer_prep_fwd(proj, tabs, qg, kvg, T)
    y, yr, states = _ret_fwd(rq, rk, proj, proj, gn_g, rc, T)
    y_ret = _mm(yr, W["w_ret_o"], out_dtype=BF16, name="ret_o")
    qnope, qpe = _mm(qn, wuq, tb=True, name="mla_uq", tm=LN_TM, tn=1536, epilogue=_q_assemble_epilogue(tabs))
    kv = _mm(kvn, wukv, tb=True, out_dtype=BF16, name="mla_ukv")
    o, lse = _attn_fwd(qnope, qpe, kv, kpe, T)
    y_mla, mix = _mm(o, W["w_mla_o"], name="mla_o", tm=LN_TM, epilogue=_mix_epilogue(proj, y_ret))
    mixed, h2, h2b = _mm(mix, W["w_out"], name="mixer_out", tm=LN_TM, epilogue=_ln_epilogue(h1, 1.0, lg[1], lb[1]))
    hs2, gu2 = _ffn_in(h2b, W["ffn2_w_in"], "ffn2_in")
    f2, h3, h3b = _mm(hs2, W["ffn2_w_out"], name="ffn2_out", tm=LN_TM, epilogue=_ln_epilogue(h2, 0.5, lg[2], lb[2]))
    pp = _mm(pb, wp_t, tb=True, name="ple_proj")

    G = {}
    dh3_a, dgl, dpp, dg3, db3, loss = _mm(h3b, W["ple_w_gate"], name="ple_gate", tm=LN_TM,
                                          epilogue=_head_epilogue(h3, pp, target, lg[3], lb[3]))
    G["ple_w_gate"] = _mm(h3b, dgl, ta=True, name="d_ple_gate")
    G["ple_w_proj"] = _mm(dpp, pb, ta=True, name="d_ple_proj")
    dh2_a, df2, dg2, db2 = _mm(dgl, W["ple_w_gate"], tb=True, add=dh3_a, name="dh3", tm=LN_TM,
                               epilogue=_ln_bwd_epilogue(h2, f2, 0.5, lg[2], lb[2]))
    G["ffn2_w_out"] = _mm(hs2, df2, ta=True, name="d_ffn2_out")
    da2 = _ffn_act_bwd(df2, W["ffn2_w_out"], gu2, "ffn2_act_bwd")
    G["ffn2_w_in"] = _mm(da2, h2b, ta=True, name="d_ffn2_in")
    dh1_a, dmixed, dg1, db1 = _mm(da2, W["ffn2_w_in"], add=dh2_a, name="dh2", tm=LN_TM,
                                  epilogue=_ln_bwd_epilogue(h1, mixed, 1.0, lg[1], lb[1]))
    G["w_out"] = _mm(mix, dmixed, ta=True, name="d_mixer_out")
    dgr, dgm, dy_ret, dy_mla = _mm(dmixed, W["w_out"], tb=True, name="dmix", tm=LN_TM,
                                   epilogue=_mix_bwd_epilogue(proj, y_ret, y_mla))
    G["w_mla_o"] = _mm(o, dy_mla, ta=True, name="d_mla_o")
    dob, delta = _mm(dy_mla, W["w_mla_o"], tb=True, name="do", tm=LN_TM, epilogue=_delta_epilogue(o))
    dqn_f, dqpe_f, dkn, dkpe_all, dv = _attn_bwd(qnope, qpe, kv, kpe, dob, lse, delta, T)
    dq_n, dq_r = _q_assemble_bwd(dqn_f, dqpe_f, tabs, T)
    g_uq = jnp.concatenate([_mm(dq_n, qn, ta=True, name="d_uq_nope"), _mm(dq_r, qn, ta=True, name="d_uq_rope")], axis=0)
    g_uq = jnp.concatenate([g_uq[:1024].reshape(HEADS, 128, LORA), g_uq[1024:].reshape(HEADS, 64, LORA)], axis=1)
    G["w_uq"] = g_uq.reshape(1536 * LORA // 1024, 1024)
    dqn = _mm(dq_r, wuq[1024:], add=_mm(dq_n, wuq[:1024], name="dqn_a"), name="dqn_b")
    g_ukv = jnp.stack([_mm(dkn, kvn, ta=True, name="d_ukv_k"), _mm(dv, kvn, ta=True, name="d_ukv_v")], axis=0)
    G["w_ukv"] = g_ukv.reshape(2, HEADS, 128, LORA).transpose(1, 0, 2, 3).reshape(2048 * LORA // 1024, 1024)
    dkvn = _mm(dv, wukv[1024:], add=_mm(dkn, wukv[:1024], name="dkvn_a"), name="dkvn_b")
    dcq, dckv, dkpe, dqg, dkvg = _rms_bwd(proj, dqn, dkvn, dkpe_all, tabs, qg, kvg, T)
    G["w_ret_o"] = _mm(yr, dy_ret, ta=True, name="d_ret_o")
    dyr = _mm(dy_ret, W["w_ret_o"], tb=True, out_dtype=BF16, name="dyr")
    drq, drk, drv, drg, dgn = _ret_bwd(rq, rk, proj, y, proj, gn_g, states, dyr, rc, tabs, T)
    dproj = jnp.concatenate([drq, drk, drv, drg, dgr, dgm, dcq, dckv, dkpe, jnp.zeros((T, P_W - P_KPE - 128), BF16)], axis=1)
    G["w_in"] = _w_in_external(_mm(dproj, h1b, ta=True, name="d_mixer_in"))
    lg0 = lg[0] if early is None else lg[0] + early(G)[0:1, 0:1]
    dx_a, df1, dg0, db0 = _mm(dproj, w_in_t, add=dh1_a, name="dh1", tm=LN_TM,
                              epilogue=_ln_bwd_epilogue(x, f1, 0.5, lg0, lb[0]))
    G["ffn1_w_out"] = _mm(hs1, df1, ta=True, name="d_ffn1_out")
    da1 = _ffn_act_bwd(df1, W["ffn1_w_out"], gu1, "ffn1_act_bwd")
    G["ffn1_w_in"] = _mm(da1, xb, ta=True, name="d_ffn1_in")
    grad_x = _mm(da1, W["ffn1_w_in"], add=dx_a, name="grad_x")

    small = dict(ln_g=jnp.concatenate([dg0, dg1, dg2, dg3], axis=0), ln_b=jnp.concatenate([db0, db1, db2, db3], axis=0),
                 ret_gn_g=dgn, q_norm_g=dqg, kv_norm_g=dkvg)
    return loss, grad_x, G, small


def kernel(x, p, positions, ln_g, ln_b, ffn1_w_in, ffn1_w_out, w_in, ret_gn_g, w_ret_o, q_norm_g, kv_norm_g, w_uq, w_ukv, w_mla_o, w_out, ffn2_w_in, ffn2_w_out, ple_w_gate, ple_w_proj, loss_target, m_ln_g, m_ln_b, m_ffn1_w_in, m_ffn1_w_out, m_w_in, m_ret_gn_g, m_w_ret_o, m_q_norm_g, m_kv_norm_g, m_w_uq, m_w_ukv, m_w_mla_o, m_w_out, m_ffn2_w_in, m_ffn2_w_out, m_ple_w_gate, m_ple_w_proj, v_ln_g, v_ln_b, v_ffn1_w_in, v_ffn1_w_out, v_w_in, v_ret_gn_g, v_w_ret_o, v_q_norm_g, v_kv_norm_g, v_w_uq, v_w_ukv, v_w_mla_o, v_w_out, v_ffn2_w_in, v_ffn2_w_out, v_ple_w_gate, v_ple_w_proj):
    names = ("ln_g", "ln_b", "ffn1_w_in", "ffn1_w_out", "w_in", "ret_gn_g", "w_ret_o", "q_norm_g", "kv_norm_g", "w_uq", "w_ukv",
             "w_mla_o", "w_out", "ffn2_w_in", "ffn2_w_out", "ple_w_gate", "ple_w_proj")
    ws = dict(zip(names, (ln_g, ln_b, ffn1_w_in, ffn1_w_out, w_in, ret_gn_g, w_ret_o, q_norm_g, kv_norm_g, w_uq, w_ukv, w_mla_o,
                          w_out, ffn2_w_in, ffn2_w_out, ple_w_gate, ple_w_proj)))
    ms = dict(zip(names, (m_ln_g, m_ln_b, m_ffn1_w_in, m_ffn1_w_out, m_w_in, m_ret_gn_g, m_w_ret_o, m_q_norm_g, m_kv_norm_g, m_w_uq,
                          m_w_ukv, m_w_mla_o, m_w_out, m_ffn2_w_in, m_ffn2_w_out, m_ple_w_gate, m_ple_w_proj)))
    vs = dict(zip(names, (v_ln_g, v_ln_b, v_ffn1_w_in, v_ffn1_w_out, v_w_in, v_ret_gn_g, v_w_ret_o, v_q_norm_g, v_kv_norm_g, v_w_uq,
                          v_w_ukv, v_w_mla_o, v_w_out, v_ffn2_w_in, v_ffn2_w_out, v_ple_w_gate, v_ple_w_proj)))

    parts = []
    for name, r in PACK:
        rows = _to_rows(name, ws[name])
        if _pad16(r) != r:
            rows = jnp.concatenate([rows, jnp.zeros((_pad16(r) - r, 1024), F32)], axis=0)
        parts.append(rows)
    wsh_first = jnp.concatenate(parts[:len(PACK_LATE)], axis=0).astype(BF16)
    wsh_rest = jnp.concatenate(parts[len(PACK_LATE):], axis=0).astype(BF16)
    ssh = jnp.concatenate([ln_g[0], ln_b[0]], axis=0)
    wall_first, sall = _all_gather(wsh_first, ssh)
    *gather_handles, start_token = _gather_start(wsh_rest, wall_first)
    W = _unpack(wall_first, PACK_LATE)
    ln_full = sall.reshape(N_DEV, 2, 4, 128).transpose(1, 2, 0, 3).reshape(2, 4, 1024)

    def rest_weights(after):
        w_thru, land = _gather_wait(*gather_handles, after)
        return _unpack(_gather_finish(w_thru, land), PACK_EARLY)

    cvec = lax.axis_index("c").astype(jnp.int32).reshape(1)
    qvec = (2 * lax.axis_index("x") + lax.axis_index("y")).astype(jnp.int32).reshape(1)

    def chip_partials(G, group, tag):
        gparts = []
        for name, r in group:
            g = G[name].reshape(N_DEV, r, 1024)
            if _pad16(r) != r:
                g = jnp.concatenate([g, jnp.zeros((N_DEV, _pad16(r) - r, 1024), g.dtype)], axis=1)
            gparts.append(g)
        gfull = jnp.concatenate(gparts, axis=1).astype(BF16)
        g4 = gfull.reshape(4, 2, gfull.shape[1], 1024)
        return _sum_sibling(g4, _exchange_sibling(g4, "exchange_sibling_" + tag), cvec, "sum_sibling_" + tag)

    in_flight = []

    def early(G):
        *handles, token = _chips_start(chip_partials(G, PACK_EARLY, "early"), "chips_start_early")
        in_flight.append(handles)
        return token

    loss_p, grad_x, G, small = _local_step(x[0], p[0, 0], positions, loss_target[0], W, ln_full[0], ln_full[1],
                                           ret_gn_g, q_norm_g, kv_norm_g, early=early, rest_weights=rest_weights,
                                           start_token=start_token)

    part_e, land_e = _chips_wait(*in_flight[0], grad_x, "chips_wait_early")
    gsh_early = _sum_chips(part_e, land_e, qvec, "sum_grads_early")
    pad256 = lambda a: jnp.concatenate([a, jnp.zeros((1, 1024 - a.shape[1]), F32)], axis=1)
    gsmall = jnp.concatenate([small["ln_g"], small["ln_b"], small["ret_gn_g"].reshape(2, 1024), pad256(small["q_norm_g"]),
                              pad256(small["kv_norm_g"]), jnp.zeros((SMALL_ROWS - 12, 1024), F32)], axis=0)
    srecv = _exchange_small(gsmall)
    part_l = chip_partials(G, PACK_LATE, "late")
    *late_handles, late_token = _chips_start(part_l, "chips_start_late", order_after=srecv)
    ssum = _sum_slots(srecv, "sum_small_grads")

    def unpack_grads(group, gsh):
        out, off = {}, 0
        for name, r in group:
            out[name] = _from_rows(name, gsh[off:off + r], ws[name].shape)
            off += _pad16(r)
        return out

    grads = unpack_grads(PACK_EARLY, gsh_early)
    me = 4 * lax.axis_index("x") + 2 * lax.axis_index("y") + lax.axis_index("c")
    grads["ln_g"] = lax.dynamic_slice(ssum[0:4], (0, me * 128), (4, 128)).reshape(1, 4, 128)
    grads["ln_b"] = lax.dynamic_slice(ssum[4:8], (0, me * 128), (4, 128)).reshape(1, 4, 128)
    grads["ret_gn_g"] = ssum[8:10].reshape(1, 2048)
    grads["q_norm_g"] = ssum[10:11, :256]
    grads["kv_norm_g"] = ssum[11:12, :256]

    delta, new_m, new_v = {}, {}, {}
    late_names = [n for n, _ in PACK_LATE]
    last = late_token
    for name in names:
        if name not in late_names:
            delta[name], new_m[name], new_v[name] = _adamw(ws[name], grads[name], ms[name], vs[name], "adamw_" + name,
                                                           order_after=last)
            last = new_v[name]
    part_l, land_l = _chips_wait(*late_handles, last, "chips_wait_late")
    grads.update(unpack_grads(PACK_LATE, _sum_chips(part_l, land_l, qvec, "sum_grads_late")))
    for name in late_names:
        delta[name], new_m[name], new_v[name] = _adamw(ws[name], grads[name], ms[name], vs[name], "adamw_" + name)

    loss = lax.psum(loss_p[0, 0], ("x", "y", "c"))
    return (loss, grad_x[None], *[grads[n] for n in names], *[delta[n] for n in names],
            *[new_m[n] for n in names], *[new_v[n] for n in names])
```

```python
import math

import jax
import jax.numpy as jnp
from jax import lax
from jax.experimental import pallas as pl
from jax.experimental.pallas import tpu as pltpu

F32 = jnp.float32
BF16 = jnp.bfloat16

N_DEV = 8
D = 1024
D_FF = 2816
D_PLE = 256
CHUNK = 64
HEADS = 8
RET_DK = 128
RET_DV = 256
MLA_NOPE = 128
MLA_ROPE = 64
MLA_DV = 128
LORA = 256
ROPE_BASE = 10000.0
EPS = 1e-5
ALPHA = 2.0 ** 0.25
RET_SCALE = RET_DK ** -0.5
MLA_SCALE = (MLA_NOPE + MLA_ROPE) ** -0.5
NEG = -1e30

ADAM_LR = 0.001
ADAM_B1 = 0.9
ADAM_B2 = 0.999
ADAM_EPS = 1e-08
ADAM_WD = 0.01
ADAM_STEP = 10

P_RQ, P_RK, P_RV, P_RG, P_GR, P_GM, P_CQ, P_CKV, P_KPE, P_W = 0, 1024, 2048, 4096, 6144, 7168, 8192, 8448, 8704, 8960
W_IN_COLS = 8768
RET_L = 256
ATT_TF = 2048
ATT_TB = 1024
ATT_HP = 1
LOG2E = math.log2(math.e)
Q_PRESCALE = MLA_SCALE * LOG2E

PACK = (("ffn1_w_in", 704), ("ffn1_w_out", 352), ("w_in", 1096), ("w_ret_o", 256), ("w_uq", 48), ("w_ukv", 64),
        ("w_mla_o", 128), ("w_out", 128), ("ffn2_w_in", 704), ("ffn2_w_out", 352), ("ple_w_gate", 128), ("ple_w_proj", 32))


def _pad16(r):
    return -(-r // 16) * 16


PACK_LATE = PACK[:2]
PACK_EARLY = PACK[2:]
SMALL_ROWS = 16


def _pcall(body, **kw):
    return pl.pallas_call(body, **kw)


def _pick(dim, prefs):
    for p in prefs:
        if dim % p == 0:
            return p
    return dim


def _sigmoid(x):
    return 1.0 / (1.0 + jnp.exp(-x))


def _silu(x):
    return x * _sigmoid(x)


def _ln(r, g, b):
    mu = jnp.mean(r, axis=-1, keepdims=True)
    var = jnp.mean(jnp.square(r - mu), axis=-1, keepdims=True)
    return (r - mu) * lax.rsqrt(var + EPS) * g + b


def _rms(x, g):
    return x * lax.rsqrt(jnp.mean(jnp.square(x), axis=-1, keepdims=True) + EPS) * g


def _dot(a, b, ca, cb):
    return lax.dot_general(a, b, (((ca,), (cb,)), ((), ())), preferred_element_type=F32)


def _accum(ref, val, first=None):
    @pl.when(pl.program_id(0) == 0 if first is None else first)
    def _():
        ref[...] = jnp.zeros_like(ref)

    ref[...] += val


def _mm(a, b, *, ta=False, tb=False, add=None, out_dtype=None, name, tm=None, tn=None, tk=None, epilogue=None):
    parts = a.shape[0] if a.ndim == 3 else 1
    ar, ac = a.shape[-2], a.shape[-1]
    out_dtype = out_dtype or (BF16 if ta else F32)
    if ta:
        K, M = ar, ac * parts
    else:
        M, K = ar, ac * parts
    if tb:
        N, K2 = b.shape
    else:
        K2, N = b.shape
    assert K == K2, (a.shape, b.shape, ta, tb)
    big = (1024, 1408, 1280, 768, 512, 256, 128)
    tm = tm or _pick(ac if (ta and parts > 1) else M, big)
    tn = tn or (N if N <= 1024 else _pick(N, big))
    kdim = ac if (not ta and parts > 1) else K
    tk = tk or (kdim if kdim <= 2816 and parts == 1 else
                _pick(kdim, (2048, 1408, 1280, 1024, 512) if tm <= 1024 else (1024, 1408, 1280, 512)))
    nk = K // tk
    grid = (M // tm, N // tn, nk)
    if parts > 1 and ta:
        per = ac // tm
        a_spec = pl.BlockSpec((None, tk, tm), lambda i, j, k: (i // per, k, i % per))
    elif parts > 1:
        per = ac // tk
        a_spec = pl.BlockSpec((None, tm, tk), lambda i, j, k: (k // per, i, k % per))
    else:
        a_spec = pl.BlockSpec((tk, tm), lambda i, j, k: (k, i)) if ta else pl.BlockSpec((tm, tk), lambda i, j, k: (i, k))
    b_spec = pl.BlockSpec((tn, tk), lambda i, j, k: (j, k)) if tb else pl.BlockSpec((tk, tn), lambda i, j, k: (k, j))
    o_spec = pl.BlockSpec((tm, tn), lambda i, j, k: (i, j))
    ca, cb = (0 if ta else 1), (1 if tb else 0)
    has_add = add is not None
    n_in = 2 + int(has_add)
    if epilogue is not None:
        assert tn == N and not ta
        ep_fn, ep_rows, ep_whole, ep_outs, ep_accs = epilogue
        n_ep_in = len(ep_rows) + len(ep_whole)
        n_out = len(ep_outs) + len(ep_accs)
    else:
        n_ep_in, n_out = 0, 1

    def body(*refs):
        a_ref, b_ref = refs[0], refs[1]
        add_ref = refs[2] if has_add else None
        o_ref = refs[n_in + n_ep_in]
        first_row_tile = pl.program_id(0) == 0

        def finish(r):
            if has_add:
                r = r + add_ref[...].astype(F32)
            if epilogue is not None:
                ep_fn(r, refs[n_in:n_in + n_ep_in], refs[n_in + n_ep_in:n_in + n_ep_in + n_out], first_row_tile)
            else:
                o_ref[...] = r.astype(out_dtype)

        if nk == 1:
            finish(_dot(a_ref[...], b_ref[...], ca, cb))
            return
        acc_ref = refs[-1]
        k = pl.program_id(2)

        @pl.when(k == 0)
        def _():
            acc_ref[...] = jnp.zeros_like(acc_ref)

        acc_ref[...] += _dot(a_ref[...], b_ref[...], ca, cb)

        @pl.when(k == nk - 1)
        def _():
            finish(acc_ref[...])

    in_specs = [a_spec, b_spec] + ([o_spec] if has_add else [])
    args = (a, b) + ((add,) if has_add else ())
    out_specs, out_shape = o_spec, jax.ShapeDtypeStruct((M, N), out_dtype)
    if epilogue is not None:
        for row_in in ep_rows:
            arr, col_block, width = (tuple(row_in) + (N,))[:3] if isinstance(row_in, tuple) else (row_in, 0, N)
            in_specs.append(pl.BlockSpec((tm, width), lambda i, j, k, _c=col_block: (i, _c)))
            args += (arr,)
        in_specs += [pl.BlockSpec(w.shape, lambda i, j, k, _n=w.ndim: (0,) * _n) for w in ep_whole]
        args += tuple(ep_whole)
        outs = [o if isinstance(o, tuple) else (o, N) for o in ep_outs]
        out_specs = ([pl.BlockSpec((tm, w), lambda i, j, k: (i, 0)) for _, w in outs]
                     + [pl.BlockSpec((r, N), lambda i, j, k: (0, 0)) for r in ep_accs])
        out_shape = [jax.ShapeDtypeStruct((M, w), dt) for dt, w in outs] + [jax.ShapeDtypeStruct((r, N), F32) for r in ep_accs]
    return _pcall(
        body, grid=grid, in_specs=in_specs, out_specs=out_specs, out_shape=out_shape,
        scratch_shapes=[pltpu.VMEM((tm, tn), F32)] if nk > 1 else [], name=name,
        compiler_params=pltpu.CompilerParams(dimension_semantics=("arbitrary" if epilogue is not None else "parallel", "parallel", "arbitrary")),
    )(*args)


def _ln_epilogue(res, c, g, b):
    def fn(r, ins, outs, first):
        res_ref, g_ref, b_ref = ins
        f_ref, h_ref, hb_ref = outs
        h = _ln(ALPHA * res_ref[...] + c * r, g_ref[...], b_ref[...])
        f_ref[...] = r
        h_ref[...] = h
        hb_ref[...] = h.astype(BF16)

    return (fn, [res], [g, b], (F32, F32, BF16), ())


def _ln_bwd_epilogue(res, f, c, g, b):
    def fn(r, ins, outs, first):
        res_ref, f_ref, g_ref, b_ref = ins
        dr_ref, df_ref, dg_ref, db_ref = outs
        pre = ALPHA * res_ref[...] + c * f_ref[...]
        xc = pre - jnp.mean(pre, axis=-1, keepdims=True)
        rstd = lax.rsqrt(jnp.mean(jnp.square(xc), axis=-1, keepdims=True) + EPS)
        xhat = xc * rstd
        dyg = r * g_ref[...]
        dpre = rstd * (dyg - jnp.mean(dyg, axis=-1, keepdims=True) - xhat * jnp.mean(dyg * xhat, axis=-1, keepdims=True))
        dr_ref[...] = ALPHA * dpre
        df_ref[...] = (c * dpre).astype(BF16)
        _accum(dg_ref, jnp.sum(r * xhat, axis=0, keepdims=True), first)
        _accum(db_ref, jnp.sum(r, axis=0, keepdims=True), first)

    return (fn, [res, f], [g, b], (F32, BF16), (1, 1))


def _rows(body, T, tm, ins, outs, name, accs=()):
    in_specs, args = [], []
    for arr, w, cb in ins:
        if w is None:
            in_specs.append(pl.BlockSpec(arr.shape, lambda i, _n=arr.ndim: (0,) * _n))
        else:
            in_specs.append(pl.BlockSpec((tm, w), lambda i, _cb=cb: (i, _cb)))
        args.append(arr)
    out_specs = [pl.BlockSpec((tm, w), lambda i: (i, 0)) for w, _ in outs]
    out_shape = [jax.ShapeDtypeStruct((T, w), dt) for w, dt in outs]
    for r, w in accs:
        out_specs.append(pl.BlockSpec((r, w), lambda i: (0, 0)))
        out_shape.append(jax.ShapeDtypeStruct((r, w), F32))
    return _pcall(
        body, grid=(T // tm,), in_specs=in_specs, out_specs=out_specs, out_shape=out_shape, name=name,
        compiler_params=pltpu.CompilerParams(dimension_semantics=("arbitrary",)),
    )(*args)


def _ln_bwd_rows(res, f, c, g, b, dh, T, name):
    fn = _ln_bwd_epilogue(res, f, c, g, b)[0]

    def body(r_ref, f_ref, g_ref, b_ref, d_ref, dr_ref, df_ref, dg_ref, db_ref):
        fn(d_ref[...], (r_ref, f_ref, g_ref, b_ref), (dr_ref, df_ref, dg_ref, db_ref), pl.program_id(0) == 0)

    return _rows(body, T, 256, [(res, D, 0), (f, D, 0), (g, None, None), (b, None, None), (dh, D, 0)],
                 [(D, F32), (D, BF16)], name, accs=[(1, D), (1, D)])


FFN_TN = 1408
FFN_TM = 512
LN_TM = 512


def _ffn_in(x, wt, name, order_after=None):
    T = x.shape[0]
    tm, tn = min(FFN_TM, T), FFN_TN
    nj = D_FF // tn
    emit_xb = x.dtype != BF16
    n_in = 3 + int(order_after is not None)

    def body(*refs):
        x_ref, wg_ref, wu_ref = refs[:3]
        hs_ref, gu_ref = refs[n_in], refs[n_in + 1]
        xv = x_ref[...].astype(BF16)
        g = _dot(xv, wg_ref[...], 1, 1)
        u = _dot(xv, wu_ref[...], 1, 1)
        hs_ref[...] = (_silu(g) * u).astype(BF16)
        gu_ref[0] = g.astype(BF16)
        gu_ref[1] = u.astype(BF16)
        if emit_xb:
            refs[n_in + 2][...] = xv

    in_specs = [pl.BlockSpec((tm, D), lambda i, j: (i, 0)), pl.BlockSpec((tn, D), lambda i, j: (j, 0)),
                pl.BlockSpec((tn, D), lambda i, j: (j + nj, 0))]
    args = [x, wt, wt]
    if order_after is not None:
        in_specs.append(pl.BlockSpec(order_after.shape, lambda i, j: (0, 0)))
        args.append(order_after)
    out_specs = [pl.BlockSpec((tm, tn), lambda i, j: (i, j)), pl.BlockSpec((2, tm, tn), lambda i, j: (0, i, j))]
    out_shape = [jax.ShapeDtypeStruct((T, D_FF), BF16), jax.ShapeDtypeStruct((2, T, D_FF), BF16)]
    if emit_xb:
        out_specs.append(pl.BlockSpec((tm, D), lambda i, j: (i, 0)))
        out_shape.append(jax.ShapeDtypeStruct((T, D), BF16))
    return _pcall(
        body, grid=(T // tm, nj), in_specs=in_specs, out_specs=out_specs, out_shape=out_shape, name=name,
        compiler_params=pltpu.CompilerParams(dimension_semantics=("parallel", "arbitrary")),
    )(*args)


def _ffn_act_bwd(df, wo, gu, name):
    T = df.shape[0]
    tm, tn = min(FFN_TM, T), FFN_TN

    def body(d_ref, w_ref, gu_ref, o_ref):
        dhs = _dot(d_ref[...], w_ref[...], 1, 1)
        g, u = gu_ref[0].astype(F32), gu_ref[1].astype(F32)
        sig = _sigmoid(g)
        act = g * sig
        o_ref[0] = (dhs * u * (sig + act * (1.0 - sig))).astype(BF16)
        o_ref[1] = (dhs * act).astype(BF16)

    return _pcall(
        body, grid=(T // tm, D_FF // tn),
        in_specs=[pl.BlockSpec((tm, D), lambda i, j: (i, 0)), pl.BlockSpec((tn, D), lambda i, j: (j, 0)),
                  pl.BlockSpec((2, tm, tn), lambda i, j: (0, i, j))],
        out_specs=pl.BlockSpec((2, tm, tn), lambda i, j: (0, i, j)),
        out_shape=jax.ShapeDtypeStruct((2, T, D_FF), BF16), name=name,
        compiler_params=pltpu.CompilerParams(dimension_semantics=("parallel", "parallel")),
    )(df, wo, gu)


def _rope128(t, cos, sin_s):
    return t * cos + pltpu.roll(t, 64, 1) * sin_s


def _rope128_t(g, cos, sin_s):
    return g * cos - pltpu.roll(g, 64, 1) * sin_s


def _partner32(t):
    lane = lax.broadcasted_iota(jnp.int32, t.shape, 1)
    return jnp.where((lane & 32) == 0, pltpu.roll(t, 96, 1), pltpu.roll(t, 32, 1))


def _rope64(t, cos, sin_s):
    return t * cos + _partner32(t) * sin_s


def _rope64_t(g, cos, sin_s):
    return g * cos - _partner32(g) * sin_s


def _mixer_prep_fwd(proj, tabs, qg, kvg, T):
    cos128, sin128, cos64, sin64 = tabs

    def body(rq_ref, rk_ref, cq_ref, ckv_ref, kpe_ref, c1_ref, s1_ref, c2_ref, s2_ref, qg_ref, kvg_ref,
             oq_ref, ok_ref, oqn_ref, okvn_ref, okpe_ref):
        c1, s1 = c1_ref[...], s1_ref[...]
        for h in range(HEADS):
            sl = slice(h * RET_DK, (h + 1) * RET_DK)
            oq_ref[:, sl] = _rope128(rq_ref[:, sl].astype(F32), c1, s1).astype(BF16)
            ok_ref[:, sl] = (_rope128(rk_ref[:, sl].astype(F32), c1, s1) * RET_SCALE).astype(BF16)
        oqn_ref[...] = _rms(cq_ref[...].astype(F32), qg_ref[...]).astype(BF16)
        okvn_ref[...] = _rms(ckv_ref[...].astype(F32), kvg_ref[...]).astype(BF16)
        okpe_ref[...] = _rope64(kpe_ref[...].astype(F32), c2_ref[...], s2_ref[...]).astype(BF16)

    ins = [(proj, 1024, 0), (proj, 1024, 1), (proj, 256, P_CQ // 256), (proj, 256, P_CKV // 256),
           (proj, 128, P_KPE // 128), (cos128, 128, 0), (sin128, 128, 0), (cos64, 128, 0), (sin64, 128, 0),
           (qg, None, None), (kvg, None, None)]
    outs = [(1024, BF16), (1024, BF16), (LORA, BF16), (LORA, BF16), (128, BF16)]
    return _rows(body, T, 256, ins, outs, "mixer_prep_fwd")


def _rms_bwd(proj, dqn, dkvn, dkpe_all, tabs, qg, kvg, T):
    _, _, cos64, sin64 = tabs

    def body(cq_ref, ckv_ref, dq_ref, dkv_ref, dk_ref, c2_ref, s2_ref, qg_ref, kvg_ref, ocq_ref, ockv_ref, okpe_ref,
             dqg_ref, dkvg_ref):
        _, vjp = jax.vjp(_rms, cq_ref[...].astype(F32), qg_ref[...])
        dx, dg = vjp(dq_ref[...])
        ocq_ref[...] = dx.astype(BF16)
        _accum(dqg_ref, dg)
        _, vjp = jax.vjp(_rms, ckv_ref[...].astype(F32), kvg_ref[...])
        dx, dg = vjp(dkv_ref[...])
        ockv_ref[...] = dx.astype(BF16)
        _accum(dkvg_ref, dg)
        g = dk_ref[:, 0:128]
        for h in range(1, HEADS):
            g = g + dk_ref[:, h * 128:(h + 1) * 128]
        lane = lax.broadcasted_iota(jnp.int32, g.shape, 1)
        g = jnp.where(lane < MLA_ROPE, g, 0.0)
        okpe_ref[...] = _rope64_t(g, c2_ref[...], s2_ref[...]).astype(BF16)

    ins = [(proj, 256, P_CQ // 256), (proj, 256, P_CKV // 256), (dqn, LORA, 0), (dkvn, LORA, 0), (dkpe_all, 1024, 0),
           (cos64, 128, 0), (sin64, 128, 0), (qg, None, None), (kvg, None, None)]
    return _rows(body, T, 256, ins, [(LORA, BF16), (LORA, BF16), (128, BF16)], "rms_bwd", accs=[(1, LORA), (1, LORA)])


def _gn_gate_bwd(y, rg, g, d):
    xc = y - jnp.mean(y, axis=-1, keepdims=True)
    rstd = lax.rsqrt(jnp.mean(jnp.square(xc), axis=-1, keepdims=True) + EPS)
    xhat = xc * rstd
    sig = _sigmoid(rg)
    act = rg * sig
    dn = d * act
    drg = d * (xhat * g) * (sig + act * (1.0 - sig))
    dxh = dn * g
    dy = rstd * (dxh - jnp.mean(dxh, axis=-1, keepdims=True) - xhat * jnp.mean(dxh * xhat, axis=-1, keepdims=True))
    return dy, drg, jnp.sum(dn * xhat, axis=0, keepdims=True)


def _gn_gate(y, rg, g):
    mu = jnp.mean(y, axis=-1, keepdims=True)
    var = jnp.mean(jnp.square(y - mu), axis=-1, keepdims=True)
    return _silu(rg) * ((y - mu) * lax.rsqrt(var + EPS) * g)


def _q_assemble_epilogue(tabs):
    _, _, cos64, sin64 = tabs

    def fn(q, ins, outs, first):
        c_ref, s_ref = ins
        on_ref, op_ref = outs
        on_ref[...] = (q[:, :1024] * Q_PRESCALE).astype(BF16)
        c, s = c_ref[...], s_ref[...]
        lane = lax.broadcasted_iota(jnp.int32, c.shape, 1)
        for j in range(HEADS // 2):
            r = _rope64(q[:, 1024 + 128 * j:1024 + 128 * (j + 1)], c, s) * Q_PRESCALE
            op_ref[:, 256 * j:256 * j + 128] = jnp.where(lane < 64, r, 0.0).astype(BF16)
            op_ref[:, 256 * j + 128:256 * j + 256] = jnp.where(lane < 64, pltpu.roll(r, 64, 1), 0.0).astype(BF16)

    return (fn, [(cos64, 0, 128), (sin64, 0, 128)], [], ((BF16, 1024), (BF16, 1024)), ())


def _q_assemble_bwd(dqn, dqpe, tabs, T):
    _, _, cos64, sin64 = tabs

    def body(dn_ref, dp_ref, c_ref, s_ref, on_ref, op_ref):
        on_ref[...] = dn_ref[...].astype(BF16)
        c, s = c_ref[...], s_ref[...]
        lane = lax.broadcasted_iota(jnp.int32, c.shape, 1)
        for j in range(HEADS // 2):
            g = jnp.where(lane < 64, dp_ref[:, 256 * j:256 * j + 128], pltpu.roll(dp_ref[:, 256 * j + 128:256 * j + 256], 64, 1))
            op_ref[:, 128 * j:128 * (j + 1)] = _rope64_t(g, c, s).astype(BF16)

    return _rows(body, T, 256, [(dqn, 1024, 0), (dqpe, 1024, 0), (cos64, 128, 0), (sin64, 128, 0)],
                 [(1024, BF16), (512, BF16)], "q_assemble_bwd")


def _mix_fn(gr, gm, yr, ym):
    return _sigmoid(gr) * yr + _sigmoid(gm) * ym


def _mix_epilogue(proj, y_ret):
    def fn(r, ins, outs, first):
        gr_ref, gm_ref, yr_ref = ins
        ym_ref, mix_ref = outs
        ym_ref[...] = r.astype(BF16)
        mix_ref[...] = _mix_fn(gr_ref[...].astype(F32), gm_ref[...].astype(F32), yr_ref[...].astype(F32), r).astype(BF16)

    return (fn, [(proj, P_GR // D), (proj, P_GM // D), y_ret], [], (BF16, BF16), ())


def _mix_bwd_epilogue(proj, y_ret, y_mla):
    def fn(r, ins, outs, first):
        gr_ref, gm_ref, yr_ref, ym_ref = ins
        dgr_ref, dgm_ref, dyr_ref, dym_ref = outs
        sr, sm = _sigmoid(gr_ref[...].astype(F32)), _sigmoid(gm_ref[...].astype(F32))
        dgr_ref[...] = (r * yr_ref[...].astype(F32) * (sr * (1.0 - sr))).astype(BF16)
        dgm_ref[...] = (r * ym_ref[...].astype(F32) * (sm * (1.0 - sm))).astype(BF16)
        dyr_ref[...] = (r * sr).astype(BF16)
        dym_ref[...] = (r * sm).astype(BF16)

    return (fn, [(proj, P_GR // D), (proj, P_GM // D), y_ret, y_mla], [], (BF16,) * 4, ())


def _head_epilogue(h3, pp, tgt, g, b):
    def fn(r, ins, outs, first):
        h_ref, pp_ref, t_ref, g_ref, b_ref = ins
        dh_ref, dgl_ref, dpp_ref, dg_ref, db_ref, loss_ref = outs

        sg, pp, gain = _sigmoid(r), pp_ref[...], g_ref[...]
        pre = ALPHA * h_ref[...] + sg * pp
        xc = pre - jnp.mean(pre, axis=-1, keepdims=True)
        rstd = lax.rsqrt(jnp.mean(jnp.square(xc), axis=-1, keepdims=True) + EPS)
        xhat = xc * rstd
        err = xhat * gain + b_ref[...] - t_ref[...]
        dy = err * (1.0 / D)
        dyg = dy * gain
        dpre = rstd * (dyg - jnp.mean(dyg, axis=-1, keepdims=True) - xhat * jnp.mean(dyg * xhat, axis=-1, keepdims=True))
        dh_ref[...] = ALPHA * dpre
        dgl_ref[...] = (dpre * pp * (sg * (1.0 - sg))).astype(BF16)
        dpp_ref[...] = (dpre * sg).astype(BF16)
        _accum(dg_ref, jnp.sum(dy * xhat, axis=0, keepdims=True), first)
        _accum(db_ref, jnp.sum(dy, axis=0, keepdims=True), first)
        part = 0.5 * jnp.sum(jnp.mean(jnp.square(err), axis=-1, keepdims=True), axis=0, keepdims=True)
        _accum(loss_ref, jnp.broadcast_to(part, loss_ref.shape), first)

    return (fn, [h3, pp, tgt], [g, b], (F32, BF16, BF16), (1, 1, 8))


def _delta_epilogue(o):
    def fn(r, ins, outs, first):
        (o_ref,) = ins
        db_ref, dl_ref = outs
        db_ref[...] = r.astype(BF16)
        for h in range(HEADS):
            sl = slice(h * MLA_DV, (h + 1) * MLA_DV)
            dl = jnp.sum(r[:, sl] * o_ref[:, sl].astype(F32), axis=-1, keepdims=True)
            dl_ref[:, sl] = jnp.broadcast_to(dl, (r.shape[0], MLA_DV))

    return (fn, [o], [], (BF16, F32), ())


def _ret_consts():
    L = RET_L
    lg = jnp.log(1.0 - 2.0 ** (-5.0 - jnp.arange(HEADS, dtype=F32)))[:, None, None]
    idx = jnp.arange(L, dtype=F32)
    ch = jnp.arange(L) // CHUNK
    dist = idx[:, None] - idx[None, :]
    same = (ch[:, None] == ch[None, :])[None]
    earlier = (ch[None, :] < ch[:, None])[None]
    dm = jnp.where(same, jnp.exp(lg * jnp.abs(dist)[None]), jnp.where(earlier, jnp.exp(lg * dist[None]), 0.0))
    xi = jnp.broadcast_to(jnp.exp(lg * (idx + 1.0)[None, :, None]), (HEADS, L, 128))
    zeta = jnp.broadcast_to(jnp.exp(lg * (L - 1.0 - idx)[None, :, None]), (HEADS, L, 128))
    gl = jnp.broadcast_to(jnp.exp(lg * float(L)), (HEADS, 8, 128))
    return dm.astype(F32), xi.astype(F32), zeta.astype(F32), gl.astype(F32)


def _whole(arr):
    return pl.BlockSpec(arr.shape, lambda n, _nd=arr.ndim: (0,) * _nd)


def _ret_fwd(q, k, v, proj, gn_g, consts, T):
    dm, xi, zeta, gl = consts
    L = RET_L
    n_sc = T // L

    def body(q_ref, k_ref, v_ref, rg_ref, g_ref, dm_ref, xi_ref, ze_ref, gl_ref, y_ref, yr_ref, s_ref, st_ref):
        @pl.when(pl.program_id(0) == 0)
        def _():
            st_ref[...] = jnp.zeros_like(st_ref)

        for h in range(HEADS):
            ks, vs = slice(h * RET_DK, (h + 1) * RET_DK), slice(h * RET_DV, (h + 1) * RET_DV)
            qq, kk, vv = q_ref[:, ks], k_ref[:, ks], v_ref[:, vs]
            st = st_ref[h]
            s_ref[h, 0] = st
            p = (_dot(qq, kk, 1, 1) * dm_ref[h]).astype(BF16)
            cross = _dot(qq, st.astype(BF16), 1, 0)
            xi_c = jnp.concatenate([xi_ref[h], xi_ref[h]], axis=1)
            y = _dot(p, vv, 1, 0) + cross * xi_c
            y_ref[:, vs] = y
            yr_ref[:, vs] = _gn_gate(y, rg_ref[:, vs].astype(F32), g_ref[:, vs]).astype(BF16)
            kz = (kk.astype(F32) * ze_ref[h]).astype(BF16)
            gl2 = jnp.concatenate([gl_ref[h, 0:1, :], gl_ref[h, 0:1, :]], axis=1)
            st_ref[h] = st * gl2 + _dot(kz, vv, 0, 0)

    return _pcall(
        body, grid=(n_sc,),
        in_specs=[pl.BlockSpec((L, 1024), lambda n: (n, 0)), pl.BlockSpec((L, 1024), lambda n: (n, 0)),
                  pl.BlockSpec((L, 2048), lambda n: (n, P_RV // 2048)), pl.BlockSpec((L, 2048), lambda n: (n, P_RG // 2048)),
                  _whole(gn_g), _whole(dm), _whole(xi), _whole(zeta), _whole(gl)],
        out_specs=[pl.BlockSpec((L, 2048), lambda n: (n, 0)), pl.BlockSpec((L, 2048), lambda n: (n, 0)),
                   pl.BlockSpec((HEADS, 1, 128, 256), lambda n: (0, n, 0, 0))],
        out_shape=[jax.ShapeDtypeStruct((T, HEADS * RET_DV), F32), jax.ShapeDtypeStruct((T, HEADS * RET_DV), BF16),
                   jax.ShapeDtypeStruct((HEADS, n_sc, 128, 256), F32)],
        scratch_shapes=[pltpu.VMEM((HEADS, 128, 256), F32)], name="ret_fwd",
        compiler_params=pltpu.CompilerParams(dimension_semantics=("arbitrary",)),
    )(q, k, v, proj, gn_g, dm, xi, zeta, gl)


def _ret_bwd(q, k, v, y, proj, gn_g, states, dyr, consts, tabs, T):
    dm, xi, zeta, gl = consts
    cos128, sin128, _, _ = tabs
    L = RET_L
    n_sc = T // L

    def body(q_ref, k_ref, v_ref, y_ref, rg_ref, g_ref, d_ref, s_ref, dm_ref, xi_ref, ze_ref, gl_ref, c_ref, sn_ref,
             dq_ref, dk_ref, dv_ref, drg_ref, dg_ref, gs_ref):
        @pl.when(pl.program_id(0) == 0)
        def _():
            gs_ref[...] = jnp.zeros_like(gs_ref)

        c, sn = c_ref[...], sn_ref[...]
        dgs = []
        for h in range(HEADS):
            ks, vs = slice(h * RET_DK, (h + 1) * RET_DK), slice(h * RET_DV, (h + 1) * RET_DV)
            dy, drg, dg = _gn_gate_bwd(y_ref[:, vs], rg_ref[:, vs].astype(F32), g_ref[:, vs], d_ref[:, vs].astype(F32))
            drg_ref[:, vs] = drg.astype(BF16)
            dgs.append(dg)
            qq, kk, vv, dyy = q_ref[:, ks], k_ref[:, ks], v_ref[:, vs], dy.astype(BF16)
            dmm = dm_ref[h]
            gb = gs_ref[h].astype(BF16)
            sb = s_ref[h, 0].astype(BF16)
            xi_c = jnp.concatenate([xi_ref[h], xi_ref[h]], axis=1)
            pb = (_dot(qq, kk, 1, 1) * dmm).astype(BF16)
            kz = (kk.astype(F32) * ze_ref[h]).astype(BF16)
            dv_ref[:, vs] = (_dot(pb, dyy, 0, 0) + _dot(kz, gb, 1, 0)).astype(BF16)
            da = (_dot(dyy, vv, 1, 1) * dmm).astype(BF16)
            dyx = (dyy.astype(F32) * xi_c).astype(BF16)
            dq = _dot(da, kk, 1, 0) + _dot(dyx, sb, 1, 1)
            dk = _dot(da, qq, 0, 0) + _dot(vv, gb, 1, 1) * ze_ref[h]
            dq_ref[:, ks] = _rope128_t(dq, c, sn).astype(BF16)
            dk_ref[:, ks] = (_rope128_t(dk, c, sn) * RET_SCALE).astype(BF16)
            gl2 = jnp.concatenate([gl_ref[h, 0:1, :], gl_ref[h, 0:1, :]], axis=1)
            gs_ref[h] = gs_ref[h] * gl2 + _dot(qq, dyx, 0, 0)
        _accum(dg_ref, jnp.concatenate(dgs, axis=1))

    rev = lambda n: n_sc - 1 - n
    return _pcall(
        body, grid=(n_sc,),
        in_specs=[pl.BlockSpec((L, 1024), lambda n: (rev(n), 0)), pl.BlockSpec((L, 1024), lambda n: (rev(n), 0)),
                  pl.BlockSpec((L, 2048), lambda n: (rev(n), P_RV // 2048)), pl.BlockSpec((L, 2048), lambda n: (rev(n), 0)),
                  pl.BlockSpec((L, 2048), lambda n: (rev(n), P_RG // 2048)), _whole(gn_g),
                  pl.BlockSpec((L, 2048), lambda n: (rev(n), 0)),
                  pl.BlockSpec((HEADS, 1, 128, 256), lambda n: (0, rev(n), 0, 0)),
                  _whole(dm), _whole(xi), _whole(zeta), _whole(gl),
                  pl.BlockSpec((L, 128), lambda n: (rev(n), 0)), pl.BlockSpec((L, 128), lambda n: (rev(n), 0))],
        out_specs=[pl.BlockSpec((L, 1024), lambda n: (rev(n), 0)), pl.BlockSpec((L, 1024), lambda n: (rev(n), 0)),
                   pl.BlockSpec((L, 2048), lambda n: (rev(n), 0)), pl.BlockSpec((L, 2048), lambda n: (rev(n), 0)),
                   pl.BlockSpec((1, 2048), lambda n: (0, 0))],
        out_shape=[jax.ShapeDtypeStruct((T, 1024), BF16), jax.ShapeDtypeStruct((T, 1024), BF16), jax.ShapeDtypeStruct((T, 2048), BF16),
                   jax.ShapeDtypeStruct((T, 2048), BF16), jax.ShapeDtypeStruct((1, 2048), F32)],
        scratch_shapes=[pltpu.VMEM((HEADS, 128, 256), F32)], name="ret_bwd",
        compiler_params=pltpu.CompilerParams(dimension_semantics=("arbitrary",)),
    )(q, k, v, y, proj, gn_g, dyr, states, dm, xi, zeta, gl, cos128, sin128)


def _diag_mask(nrows, ncols, row0):
    row = row0 + lax.broadcasted_iota(jnp.int32, (nrows, ncols), 0)
    col = lax.broadcasted_iota(jnp.int32, (nrows, ncols), 1)
    return lax.shift_right_logical(col, 6) <= lax.shift_right_logical(row, 6)


def _diag_spans(t):
    h = t // 2
    return ((0, h, h), (h, h, t)) if h % 128 == 0 else ((0, t, t),)


def _tri_steps(nb, by_key):
    if by_key:
        pairs = [(i, j) for j in range(nb) for i in range(j, nb)]
    else:
        pairs = [(i, j) for i in range(nb) for j in range(i + 1)]
    return jnp.array([a for a, _ in pairs], jnp.int32), jnp.array([b for _, b in pairs], jnp.int32)


def _attn_fwd(qn, qpe, kv, kpe, T):
    t = min(ATT_TF, T)
    nb = T // t
    ii, jj = _tri_steps(nb, by_key=False)

    hp = ATT_HP
    w = 128 * hp

    def body(ii_ref, jj_ref, qn_ref, qp_ref, kn_ref, kp_ref, v_ref, o_ref, lse_ref, m_sc, l_sc, acc_sc):
        st = pl.program_id(1)
        i, j = ii_ref[st], jj_ref[st]

        @pl.when(j == 0)
        def _():
            m_sc[...] = jnp.full_like(m_sc, NEG)
            l_sc[...] = jnp.zeros_like(l_sc)
            acc_sc[...] = jnp.zeros_like(acc_sc)

        def update(diag):
            kp = kp_ref[...]
            spans = _diag_spans(t) if diag else ((0, t, t),)
            for hh in range(hp):
                sl = slice(128 * hh, 128 * (hh + 1))
                for row0, nr, nkeys in spans:
                    rows = slice(row0, row0 + nr)
                    q = jnp.concatenate([qn_ref[rows, sl], qp_ref[rows, sl]], axis=1)
                    k = jnp.concatenate([kn_ref[:nkeys, sl], kp[:nkeys]], axis=1)
                    s = _dot(q, k, 1, 1)
                    if diag:
                        s = jnp.where(_diag_mask(nr, nkeys, row0), s, NEG)
                    m_prev = m_sc[rows, sl]
                    m_new = jnp.maximum(m_prev, jnp.max(s, axis=1, keepdims=True))
                    a = jnp.exp2(m_prev - m_new)
                    p = jnp.exp2(s - m_new[:, 0:1])
                    l_sc[rows, sl] = a * l_sc[rows, sl] + jnp.sum(p, axis=1, keepdims=True)
                    acc_sc[rows, sl] = a * acc_sc[rows, sl] + _dot(p.astype(BF16), v_ref[:nkeys, sl], 1, 0)
                    m_sc[rows, sl] = m_new

        @pl.when(j < i)
        def _():
            update(False)

        @pl.when(j == i)
        def _():
            update(True)
            o_ref[...] = (acc_sc[...] / l_sc[...]).astype(BF16)
            lse_ref[...] = m_sc[...] + jnp.log2(l_sc[...])

    qs = pl.BlockSpec((t, w), lambda h, s, ii, jj: (ii[s], h))
    grid_spec = pltpu.PrefetchScalarGridSpec(
        num_scalar_prefetch=2, grid=(HEADS // hp, int(ii.shape[0])),
        in_specs=[qs, qs, pl.BlockSpec((t, w), lambda h, s, ii, jj: (jj[s], h)), pl.BlockSpec((t, 128), lambda h, s, ii, jj: (jj[s], 0)),
                  pl.BlockSpec((t, w), lambda h, s, ii, jj: (jj[s], HEADS // hp + h))],
        out_specs=[qs, qs],
        scratch_shapes=[pltpu.VMEM((t, w), F32), pltpu.VMEM((t, w), F32), pltpu.VMEM((t, w), F32)])
    return _pcall(
        body, grid_spec=grid_spec, out_shape=[jax.ShapeDtypeStruct((T, D), BF16), jax.ShapeDtypeStruct((T, D), F32)], name="attn_fwd",
        compiler_params=pltpu.CompilerParams(dimension_semantics=("arbitrary", "arbitrary")),
    )(ii, jj, qn, qpe, kv, kpe, kv)


def _attn_bwd(qn, qpe, kv, kpe, do, lse, delta, T):
    t = min(ATT_TB, T)
    nb = T // t
    ii, jj = _tri_steps(nb, by_key=True)

    def body(ii_ref, jj_ref, qn_ref, qp_ref, kn_ref, kp_ref, v_ref, do_ref, lse_ref, dl_ref,
             dqn_ref, dqp_ref, dkn_ref, dkp_ref, dv_ref, dk_sc, dv_sc):
        st = pl.program_id(1)
        i, j = ii_ref[st], jj_ref[st]

        @pl.when(st == 0)
        def _():
            dqn_ref[...] = jnp.zeros_like(dqn_ref)
            dqp_ref[...] = jnp.zeros_like(dqp_ref)

        @pl.when(i == j)
        def _():
            dk_sc[...] = jnp.zeros_like(dk_sc)
            dv_sc[...] = jnp.zeros_like(dv_sc)

        def update(diag):
            for row0, nr, nkeys in (_diag_spans(t) if diag else ((0, t, t),)):
                rows = slice(row0, row0 + nr)
                q = jnp.concatenate([qn_ref[rows, :], qp_ref[rows, :]], axis=1)
                k = jnp.concatenate([kn_ref[:nkeys, :], kp_ref[:nkeys, :]], axis=1)
                dob = do_ref[rows, :]
                s = _dot(q, k, 1, 1)
                if diag:
                    s = jnp.where(_diag_mask(nr, nkeys, row0), s, NEG)
                p = jnp.exp2(s - lse_ref[rows, 0:1])
                dv_sc[:nkeys, :] += _dot(p.astype(BF16), dob, 0, 0)
                dp = _dot(dob, v_ref[:nkeys, :], 1, 1)
                ds = (p * (dp - dl_ref[rows, 0:1])).astype(BF16)
                dk_sc[:nkeys, :] += _dot(ds, q, 0, 0)
                dq = _dot(ds, k, 1, 0) * MLA_SCALE
                grows = pl.ds(pl.multiple_of(i * t + row0, nr), nr)
                dqn_ref[grows, :] += dq[:, :128]
                dqp_ref[grows, :] += dq[:, 128:]

        @pl.when(i > j)
        def _():
            update(False)

        @pl.when(i == j)
        def _():
            update(True)

        @pl.when(i == nb - 1)
        def _():
            dkn_ref[...] = (dk_sc[:, :128] * (1.0 / LOG2E)).astype(BF16)
            dkp_ref[...] = dk_sc[:, 128:] * (1.0 / LOG2E)
            dv_ref[...] = dv_sc[...].astype(BF16)

    qs = pl.BlockSpec((t, 128), lambda h, s, ii, jj: (ii[s], h))
    ks = pl.BlockSpec((t, 128), lambda h, s, ii, jj: (jj[s], h))
    hs = pl.BlockSpec((T, 128), lambda h, s, ii, jj: (0, h))
    grid_spec = pltpu.PrefetchScalarGridSpec(
        num_scalar_prefetch=2, grid=(HEADS, int(ii.shape[0])),
        in_specs=[qs, qs, ks, pl.BlockSpec((t, 128), lambda h, s, ii, jj: (jj[s], 0)),
                  pl.BlockSpec((t, 128), lambda h, s, ii, jj: (jj[s], HEADS + h)), qs, qs, qs],
        out_specs=[hs, hs, ks, ks, ks],
        scratch_shapes=[pltpu.VMEM((t, 256), F32), pltpu.VMEM((t, 128), F32)])
    return _pcall(
        body, grid_spec=grid_spec,
        out_shape=[jax.ShapeDtypeStruct((T, D), F32), jax.ShapeDtypeStruct((T, D), F32), jax.ShapeDtypeStruct((T, D), BF16),
                   jax.ShapeDtypeStruct((T, D), F32), jax.ShapeDtypeStruct((T, D), BF16)], name="attn_bwd",
        compiler_params=pltpu.CompilerParams(dimension_semantics=("arbitrary", "arbitrary")),
    )(ii, jj, qn, qpe, kv, kpe, kv, do, lse, delta)


def _mesh_pos():
    x, y, c = lax.axis_index("x"), lax.axis_index("y"), lax.axis_index("c")
    return x, y, c, 4 * x + 2 * y + c


def _peer(x, y, c, k):
    px, py, pc = (x + ((k >> 2) & 1)) % 2, (y + ((k >> 1) & 1)) % 2, (c + (k & 1)) % 2
    return (px, py, pc), 4 * px + 2 * py + pc


_ANY = pl.BlockSpec(memory_space=pl.ANY)


def _rcopy(src, dst, send_sems, recv_sems, k, to):
    return pltpu.make_async_remote_copy(src_ref=src, dst_ref=dst, send_sem=send_sems.at[k], recv_sem=recv_sems.at[k],
                                        device_id=to, device_id_type=pl.DeviceIdType.MESH)


def _all_gather(wsh, ssh):
    def body(w_ref, s_ref, wall_ref, sall_ref, send_sems, recv_sems, loc_sems):
        x, y, c, me = _mesh_pos()
        sib = (x, y, 1 - c)
        chips = [(1 - x, y), (x, 1 - y), (1 - x, 1 - y)]
        slot = lambda px, py, pc: 4 * px + 2 * py + pc
        loc = [pltpu.make_async_copy(s_ref, sall_ref.at[me], loc_sems.at[0])]
        for cp in loc:
            cp.start()
        sends, fwd_waits = [], []
        for n, (src, dst) in enumerate(((w_ref, wall_ref), (s_ref, sall_ref))):
            o = 7 * n
            sends.append(_rcopy(src, dst.at[me], send_sems, recv_sems, o, sib))
            for j, chip in enumerate(chips):
                sends.append(_rcopy(src, dst.at[me], send_sems, recv_sems, o + 1 + j, (*chip, c)))
        for cp in sends:
            cp.start()
        for n, (src, dst) in enumerate(((w_ref, wall_ref), (s_ref, sall_ref))):
            o = 7 * n
            for j, chip in enumerate(chips):
                got = dst.at[slot(*chip, c)]
                _rcopy(src, got, send_sems, recv_sems, o + 1 + j, sib).wait_recv()
                fw = _rcopy(got, got, send_sems, recv_sems, o + 4 + j, sib)
                fw.start()
                sends.append(fw)
            fwd_waits.append(_rcopy(src, dst.at[slot(x, y, 1 - c)], send_sems, recv_sems, o, sib))
            for j, chip in enumerate(chips):
                fwd_waits.append(_rcopy(src, dst.at[slot(*chip, 1 - c)], send_sems, recv_sems, o + 4 + j, sib))
        for cp in fwd_waits:
            cp.wait_recv()
        for cp in sends:
            cp.wait_send()
        for cp in loc:
            cp.wait()

    wall, sall = _pcall(
        body, in_specs=[_ANY, _ANY], out_specs=[_ANY, _ANY],
        out_shape=[jax.ShapeDtypeStruct((N_DEV,) + wsh.shape, wsh.dtype), jax.ShapeDtypeStruct((N_DEV,) + ssh.shape, ssh.dtype)],
        scratch_shapes=[pltpu.SemaphoreType.DMA((14,)), pltpu.SemaphoreType.DMA((14,)), pltpu.SemaphoreType.DMA((1,))],
        name="all_gather_weights",
    )(wsh, ssh)
    me = 4 * lax.axis_index("x") + 2 * lax.axis_index("y") + lax.axis_index("c")
    return lax.dynamic_update_index_in_dim(wall, wsh, me, axis=0), sall


_HBM = pl.BlockSpec(memory_space=pltpu.HBM)
_SEM = pl.BlockSpec(memory_space=pltpu.SEMAPHORE)
_EFFECT = pltpu.SideEffectType.DATAFLOW_SIDE_EFFECTING


def _other_chips(x, y):
    return [(1 - x, y), (x, 1 - y), (1 - x, 1 - y)]


def _gather_start(wsh, order_after):
    def body(w_ref, land_ref, dep_ref, send_sems, recv_sems, w_thru, land_thru, token):
        x, y, c, me = _mesh_pos()
        _rcopy(w_ref, land_ref.at[me], send_sems, recv_sems, 0, (x, y, 1 - c)).start()
        for j, chip in enumerate(_other_chips(x, y)):
            _rcopy(w_ref, land_ref.at[me], send_sems, recv_sems, 1 + j, (*chip, c)).start()
        token[...] = jnp.zeros_like(token)

    shape = (N_DEV,) + wsh.shape
    land = pltpu.with_memory_space_constraint(lax.empty(shape, wsh.dtype), pltpu.HBM)
    return _pcall(
        body, name="gather_start",
        out_shape=(pltpu.SemaphoreType.DMA((4,)), pltpu.SemaphoreType.DMA((4,)), pltpu.HBM(wsh.shape, wsh.dtype),
                   pltpu.HBM(shape, wsh.dtype), jax.ShapeDtypeStruct((8, 128), F32)),
        in_specs=(_HBM, _HBM, _ANY), out_specs=(_SEM, _SEM, _HBM, _HBM, pl.BlockSpec(memory_space=pltpu.VMEM)),
        input_output_aliases={0: 2, 1: 3}, compiler_params=pltpu.CompilerParams(has_side_effects=_EFFECT),
    )(pltpu.with_memory_space_constraint(wsh, pltpu.HBM), land, order_after)


def _gather_wait(send_sems, recv_sems, w_thru, land_thru, after):
    def body(w_ref, land_ref, send_sems, recv_sems, after_ref, w_out, land_out):
        x, y, c, _ = _mesh_pos()
        senders = [(x, y, 1 - c)] + [(*chip, c) for chip in _other_chips(x, y)]
        for k, (px, py, pc) in enumerate(senders):
            cp = _rcopy(w_ref, land_ref.at[4 * px + 2 * py + pc], send_sems, recv_sems, k, (px, py, pc))
            cp.wait_send()
            cp.wait_recv()

    return _pcall(
        body, name="gather_wait", out_shape=(pltpu.HBM(w_thru.shape, w_thru.dtype), pltpu.HBM(land_thru.shape, land_thru.dtype)),
        in_specs=(_HBM, _HBM, _SEM, _SEM, _ANY), out_specs=(_HBM, _HBM), input_output_aliases={0: 0, 1: 1},
        compiler_params=pltpu.CompilerParams(has_side_effects=_EFFECT),
    )(w_thru, land_thru, send_sems, recv_sems, after)


def _gather_finish(wsh, land):
    def body(land_ref, out_ref, send_sems, recv_sems):
        x, y, c, _ = _mesh_pos()
        sib = (x, y, 1 - c)
        sends, recvs = [], []
        for j, (px, py) in enumerate(_other_chips(x, y)):
            mine, theirs = 4 * px + 2 * py + c, 4 * px + 2 * py + (1 - c)
            sends.append(_rcopy(land_ref.at[mine], out_ref.at[mine], send_sems, recv_sems, j, sib))
            recvs.append(_rcopy(land_ref.at[theirs], out_ref.at[theirs], send_sems, recv_sems, j, sib))
        for cp in sends:
            cp.start()
        for cp in recvs:
            cp.wait_recv()
        for cp in sends:
            cp.wait_send()

    done = _pcall(
        body, in_specs=[_ANY], out_specs=_ANY, out_shape=jax.ShapeDtypeStruct(land.shape, land.dtype),
        input_output_aliases={0: 0},
        scratch_shapes=[pltpu.SemaphoreType.DMA((3,)), pltpu.SemaphoreType.DMA((3,))], name="gather_finish",
    )(land)
    me = 4 * lax.axis_index("x") + 2 * lax.axis_index("y") + lax.axis_index("c")
    return lax.dynamic_update_index_in_dim(done, wsh, me, axis=0)


_ROW_TILES = (400, 368, 352, 256, 128, 16, 8)


def _exchange_sibling(g4, name):
    def body(g_ref, r_ref, send_sems, recv_sems):
        x, y, c, _ = _mesh_pos()
        sib = (x, y, 1 - c)
        cps = [_rcopy(g_ref.at[q, 1 - c], r_ref.at[q], send_sems, recv_sems, q, sib) for q in range(4)]
        for cp in cps:
            cp.start()
        for cp in cps:
            cp.wait()

    n, _, R, C = g4.shape
    return _pcall(
        body, in_specs=[_ANY], out_specs=_ANY, out_shape=jax.ShapeDtypeStruct((n, R, C), g4.dtype),
        scratch_shapes=[pltpu.SemaphoreType.DMA((4,)), pltpu.SemaphoreType.DMA((4,))], name=name,
    )(g4)


def _sum_sibling(g4, rsib, cvec, name):
    n, _, R, C = g4.shape
    tr = _pick(R, _ROW_TILES)

    def body(c_ref, g_ref, r_ref, o_ref):
        o_ref[...] = (g_ref[...].astype(F32) + r_ref[...].astype(F32)).astype(o_ref.dtype)

    grid_spec = pltpu.PrefetchScalarGridSpec(
        num_scalar_prefetch=1, grid=(n, R // tr),
        in_specs=[pl.BlockSpec((None, None, tr, C), lambda q, i, cr: (q, cr[0], i, 0)), pl.BlockSpec((None, tr, C), lambda q, i, cr: (q, i, 0))],
        out_specs=pl.BlockSpec((None, tr, C), lambda q, i, cr: (q, i, 0)))
    return _pcall(body, grid_spec=grid_spec, out_shape=jax.ShapeDtypeStruct((n, R, C), g4.dtype), name=name)(cvec, g4, rsib)


def _chips_start(part, name, order_after=None):
    n_in = 2 + int(order_after is not None)

    def body(*refs):
        p_ref, land_ref = refs[:2]
        send_sems, recv_sems, token = refs[n_in], refs[n_in + 1], refs[n_in + 4]
        x, y, c, _ = _mesh_pos()
        myq = 2 * x + y
        for j, (px, py) in enumerate(_other_chips(x, y)):
            _rcopy(p_ref.at[2 * px + py], land_ref.at[myq], send_sems, recv_sems, j, (px, py, c)).start()
        token[...] = jnp.zeros_like(token)

    land = pltpu.with_memory_space_constraint(lax.empty(part.shape, part.dtype), pltpu.HBM)
    extra = () if order_after is None else (order_after,)
    return _pcall(
        body, name=name,
        out_shape=(pltpu.SemaphoreType.DMA((3,)), pltpu.SemaphoreType.DMA((3,)), pltpu.HBM(part.shape, part.dtype),
                   pltpu.HBM(part.shape, part.dtype), jax.ShapeDtypeStruct((8, 128), F32)),
        in_specs=(_HBM, _HBM) + (_ANY,) * len(extra),
        out_specs=(_SEM, _SEM, _HBM, _HBM, pl.BlockSpec(memory_space=pltpu.VMEM)),
        input_output_aliases={0: 2, 1: 3}, compiler_params=pltpu.CompilerParams(has_side_effects=_EFFECT),
    )(pltpu.with_memory_space_constraint(part, pltpu.HBM), land, *extra)


def _chips_wait(send_sems, recv_sems, p_thru, land_thru, after, name):
    def body(p_ref, land_ref, send_sems, recv_sems, after_ref, p_out, land_out):
        x, y, c, _ = _mesh_pos()
        for j, (px, py) in enumerate(_other_chips(x, y)):
            q = 2 * px + py
            cp = _rcopy(p_ref.at[q], land_ref.at[q], send_sems, recv_sems, j, (px, py, c))
            cp.wait_send()
            cp.wait_recv()

    return _pcall(
        body, name=name, out_shape=(pltpu.HBM(p_thru.shape, p_thru.dtype), pltpu.HBM(p_thru.shape, p_thru.dtype)),
        in_specs=(_HBM, _HBM, _SEM, _SEM, _ANY), out_specs=(_HBM, _HBM), input_output_aliases={0: 0, 1: 1},
        compiler_params=pltpu.CompilerParams(has_side_effects=_EFFECT),
    )(p_thru, land_thru, send_sems, recv_sems, after)


def _sum_chips(part, land, qvec, name):
    n, R, C = part.shape
    tr = _pick(R, _ROW_TILES)

    def body(q_ref, p_ref, *refs):
        o_ref = refs[n]
        acc = None
        for q in range(n):
            term = jnp.where(q_ref[0] == q, p_ref[...], refs[q][...]).astype(F32)
            acc = term if acc is None else acc + term
        o_ref[...] = acc

    def land_spec(q):
        return pl.BlockSpec((None, tr, C), lambda i, m: (jnp.where(m[0] == q, (q + 1) % n, q), i, 0))

    grid_spec = pltpu.PrefetchScalarGridSpec(
        num_scalar_prefetch=1, grid=(R // tr,),
        in_specs=[pl.BlockSpec((None, tr, C), lambda i, m: (m[0], i, 0))] + [land_spec(q) for q in range(n)],
        out_specs=pl.BlockSpec((tr, C), lambda i, m: (i, 0)))
    return _pcall(body, grid_spec=grid_spec, out_shape=jax.ShapeDtypeStruct((R, C), F32), name=name)(qvec, part, *([land] * n))


def _exchange_small(gsmall):
    def body(s_ref, srecv_ref, send_sems, recv_sems, loc_sem):
        x, y, c, me = _mesh_pos()
        loc = pltpu.make_async_copy(s_ref, srecv_ref.at[me], loc_sem.at[0])
        loc.start()
        sends, recvs = [], []
        for k in range(1, N_DEV):
            to, pidx = _peer(x, y, c, k)
            sends.append(_rcopy(s_ref, srecv_ref.at[me], send_sems, recv_sems, k - 1, to))
            recvs.append(_rcopy(s_ref, srecv_ref.at[pidx], send_sems, recv_sems, k - 1, to))
        for cp in sends:
            cp.start()
        for cp in recvs:
            cp.wait_recv()
        for cp in sends:
            cp.wait_send()
        loc.wait()

    return _pcall(
        body, in_specs=[_ANY], out_specs=_ANY, out_shape=jax.ShapeDtypeStruct((N_DEV,) + gsmall.shape, gsmall.dtype),
        scratch_shapes=[pltpu.SemaphoreType.DMA((7,)), pltpu.SemaphoreType.DMA((7,)), pltpu.SemaphoreType.DMA((1,))],
        name="exchange_small",
    )(gsmall)


def _sum_slots(recv, name):
    n, R, C = recv.shape
    tr = _pick(R, _ROW_TILES)

    def body(r_ref, o_ref):
        acc = r_ref[0].astype(F32)
        for s in range(1, n):
            acc = acc + r_ref[s].astype(F32)
        o_ref[...] = acc

    return _pcall(body, grid=(R // tr,), in_specs=[pl.BlockSpec((n, tr, C), lambda i: (0, i, 0))],
                  out_specs=pl.BlockSpec((tr, C), lambda i: (i, 0)), out_shape=jax.ShapeDtypeStruct((R, C), F32), name=name)(recv)


def _adamw(w, g, m, v, name, order_after=None):
    shape = w.shape
    w2, g2, m2, v2 = (a.reshape(-1, shape[-1]) for a in (w, g, m, v))
    R, C = w2.shape
    tr = _pick(R, (256, 128, 64, 32, 16, 8)) if R > 256 else R
    n_in = 4 + int(order_after is not None)

    def body(*refs):
        w_ref, g_ref, m_ref, v_ref = refs[:4]
        d_ref, nm_ref, nv_ref = refs[n_in:]
        gg = g_ref[...]
        nm = ADAM_B1 * m_ref[...] + (1.0 - ADAM_B1) * gg
        nv = ADAM_B2 * v_ref[...] + (1.0 - ADAM_B2) * jnp.square(gg)
        m_hat = nm / (1.0 - ADAM_B1 ** ADAM_STEP)
        v_hat = nv / (1.0 - ADAM_B2 ** ADAM_STEP)
        d_ref[...] = -ADAM_LR * (m_hat / (jnp.sqrt(v_hat) + ADAM_EPS) + ADAM_WD * w_ref[...])
        nm_ref[...] = nm
        nv_ref[...] = nv

    spec = pl.BlockSpec((tr, C), lambda i: (i, 0))
    in_specs, args = [spec] * 4, [w2, g2, m2, v2]
    if order_after is not None:
        in_specs.append(pl.BlockSpec(memory_space=pl.ANY))
        args.append(order_after)
    d, nm, nv = _pcall(body, grid=(R // tr,), in_specs=in_specs, out_specs=[spec] * 3,
                       out_shape=[jax.ShapeDtypeStruct((R, C), F32)] * 3, name=name)(*args)
    return d.reshape(shape), nm.reshape(shape), nv.reshape(shape)


def _to_rows(name, w):
    w = w[0]
    if name in ("ffn1_w_in", "ffn2_w_in", "w_in"):
        return w.T
    if name in ("w_uq", "w_ukv", "ple_w_proj"):
        return w.T.reshape(-1, 1024)
    return w


def _from_rows(name, g, shape):
    if name in ("ffn1_w_in", "ffn2_w_in", "w_in"):
        return g.T.reshape(shape)
    if name in ("w_uq", "w_ukv", "ple_w_proj"):
        return g.reshape(-1, shape[1]).T.reshape(shape)
    return g.reshape(shape)


def _unpack(wall, group):
    out, off = {}, 0
    for name, r in group:
        out[name] = wall[:, off:off + r, :].reshape(N_DEV * r, 1024)
        off += _pad16(r)
    return out


def _w_in_internal(wt):
    return jnp.concatenate([wt[0:6144], wt[6720:8768], wt[6144:6720], jnp.zeros((P_W - W_IN_COLS, 1024), wt.dtype)], axis=0)


def _w_in_external(d):
    return jnp.concatenate([d[0:6144], d[8192:8768], d[6144:8192]], axis=0)


def _rope_tables(positions):
    pos = positions[0].astype(F32)

    def cs(half):
        inv = ROPE_BASE ** (-jnp.arange(half, dtype=F32) / half)
        ang = pos[:, None] * inv
        return jnp.cos(ang), jnp.sin(ang)

    c, s = cs(64)
    c2, s2 = cs(32)
    return (jnp.concatenate([c, c], axis=1), jnp.concatenate([-s, s], axis=1),
            jnp.concatenate([c2, c2, c2, c2], axis=1), jnp.concatenate([-s2, s2, -s2, s2], axis=1))


def _local_step(x, p, positions, target, W, ln_g, ln_b, gn_g, qg, kvg, early=None, rest_weights=None, start_token=None):
    T = x.shape[0]
    tabs = _rope_tables(positions)
    rc = _ret_consts()
    lg = [ln_g[i:i + 1] for i in range(4)]
    lb = [ln_b[i:i + 1] for i in range(4)]
    pb = p.astype(BF16)

    hs1, gu1, *xb = _ffn_in(x, W["ffn1_w_in"], "ffn1_in", order_after=start_token)
    xb = xb[0] if xb else x
    f1, h1, h1b = _mm(hs1, W["ffn1_w_out"], name="ffn1_out", tm=LN_TM, epilogue=_ln_epilogue(x, 0.5, lg[0], lb[0]))
    if rest_weights is not None:
        W = {**W, **rest_weights(h1b)}
    w_in_t = _w_in_internal(W["w_in"])
    wuq = W["w_uq"].reshape(1536, LORA).reshape(HEADS, 192, LORA)
    wuq = jnp.concatenate([wuq[:, :128].reshape(1024, LORA), wuq[:, 128:].reshape(512, LORA)], axis=0)
    wukv = W["w_ukv"].reshape(2048, LORA).reshape(HEADS, 2, 128, LORA).transpose(1, 0, 2, 3).reshape(2048, LORA)
    wp_t = W["ple_w_proj"].reshape(1024, D_PLE)
    proj = _mm(h1b, w_in_t, tb=True, out_dtype=BF16, name="mixer_in")
    rq, rk, qn, kvn, kpe = _mixer_prep_fwd(proj, tabs, qg, kvg, T)
    y, yr, states = _ret_fwd(rq, rk, proj, proj, gn_g, rc, T)
    y_ret = _mm(yr, W["w_ret_o"], out_dtype=BF16, name="ret_o")
    qnope, qpe = _mm(qn, wuq, tb=True, name="mla_uq", tm=LN_TM, tn=1536, epilogue=_q_assemble_epilogue(tabs))
    kv = _mm(kvn, wukv, tb=True, out_dtype=BF16, name="mla_ukv")
    o, lse = _attn_fwd(qnope, qpe, kv, kpe, T)
    y_mla, mix = _mm(o, W["w_mla_o"], name="mla_o", tm=LN_TM, epilogue=_mix_epilogue(proj, y_ret))
    mixed, h2, h2b = _mm(mix, W["w_out"], name="mixer_out", tm=LN_TM, epilogue=_ln_epilogue(h1, 1.0, lg[1], lb[1]))
    hs2, gu2 = _ffn_in(h2b, W["ffn2_w_in"], "ffn2_in")
    f2, h3, h3b = _mm(hs2, W["ffn2_w_out"], name="ffn2_out", tm=LN_TM, epilogue=_ln_epilogue(h2, 0.5, lg[2], lb[2]))
    pp = _mm(pb, wp_t, tb=True, name="ple_proj")

    G = {}
    dh3_a, dgl, dpp, dg3, db3, loss = _mm(h3b, W["ple_w_gate"], name="ple_gate", tm=LN_TM,
                                          epilogue=_head_epilogue(h3, pp, target, lg[3], lb[3]))
    G["ple_w_gate"] = _mm(h3b, dgl, ta=True, name="d_ple_gate")
    G["ple_w_proj"] = _mm(dpp, pb, ta=True, name="d_ple_proj")
    dh2_a, df2, dg2, db2 = _mm(dgl, W["ple_w_gate"], tb=True, add=dh3_a, name="dh3", tm=LN_TM,
                               epilogue=_ln_bwd_epilogue(h2, f2, 0.5, lg[2], lb[2]))
    G["ffn2_w_out"] = _mm(hs2, df2, ta=True, name="d_ffn2_out")
    da2 = _ffn_act_bwd(df2, W["ffn2_w_out"], gu2, "ffn2_act_bwd")
    G["ffn2_w_in"] = _mm(da2, h2b, ta=True, name="d_ffn2_in")
    dh1_a, dmixed, dg1, db1 = _mm(da2, W["ffn2_w_in"], add=dh2_a, name="dh2", tm=LN_TM,
                                  epilogue=_ln_bwd_epilogue(h1, mixed, 1.0, lg[1], lb[1]))
    G["w_out"] = _mm(mix, dmixed, ta=True, name="d_mixer_out")
    dgr, dgm, dy_ret, dy_mla = _mm(dmixed, W["w_out"], tb=True, name="dmix", tm=LN_TM,
                                   epilogue=_mix_bwd_epilogue(proj, y_ret, y_mla))
    G["w_mla_o"] = _mm(o, dy_mla, ta=True, name="d_mla_o")
    dob, delta = _mm(dy_mla, W["w_mla_o"], tb=True, name="do", tm=LN_TM, epilogue=_delta_epilogue(o))
    dqn_f, dqpe_f, dkn, dkpe_all, dv = _attn_bwd(qnope, qpe, kv, kpe, dob, lse, delta, T)
    dq_n, dq_r = _q_assemble_bwd(dqn_f, dqpe_f, tabs, T)
    g_uq = jnp.concatenate([_mm(dq_n, qn, ta=True, name="d_uq_nope"), _mm(dq_r, qn, ta=True, name="d_uq_rope")], axis=0)
    g_uq = jnp.concatenate([g_uq[:1024].reshape(HEADS, 128, LORA), g_uq[1024:].reshape(HEADS, 64, LORA)], axis=1)
    G["w_uq"] = g_uq.reshape(1536 * LORA // 1024, 1024)
    dqn = _mm(dq_r, wuq[1024:], add=_mm(dq_n, wuq[:1024], name="dqn_a"), name="dqn_b")
    g_ukv = jnp.stack([_mm(dkn, kvn, ta=True, name="d_ukv_k"), _mm(dv, kvn, ta=True, name="d_ukv_v")], axis=0)
    G["w_ukv"] = g_ukv.reshape(2, HEADS, 128, LORA).transpose(1, 0, 2, 3).reshape(2048 * LORA // 1024, 1024)
    dkvn = _mm(dv, wukv[1024:], add=_mm(dkn, wukv[:1024], name="dkvn_a"), name="dkvn_b")
    dcq, dckv, dkpe, dqg, dkvg = _rms_bwd(proj, dqn, dkvn, dkpe_all, tabs, qg, kvg, T)
    G["w_ret_o"] = _mm(yr, dy_ret, ta=True, name="d_ret_o")
    dyr = _mm(dy_ret, W["w_ret_o"], tb=True, out_dtype=BF16, name="dyr")
    drq, drk, drv, drg, dgn = _ret_bwd(rq, rk, proj, y, proj, gn_g, states, dyr, rc, tabs, T)
    dproj = jnp.concatenate([drq, drk, drv, drg, dgr, dgm, dcq, dckv, dkpe, jnp.zeros((T, P_W - P_KPE - 128), BF16)], axis=1)
    G["w_in"] = _w_in_external(_mm(dproj, h1b, ta=True, name="d_mixer_in"))
    lg0 = lg[0] if early is None else lg[0] + early(G)[0:1, 0:1]
    dh1 = _mm(dproj, w_in_t, add=dh1_a, name="dh1")
    dx_a, df1, dg0, db0 = _ln_bwd_rows(x, f1, 0.5, lg0, lb[0], dh1, T, "ln0_bwd")
    G["ffn1_w_out"] = _mm(hs1, df1, ta=True, name="d_ffn1_out")
    da1 = _ffn_act_bwd(df1, W["ffn1_w_out"], gu1, "ffn1_act_bwd")
    G["ffn1_w_in"] = _mm(da1, xb, ta=True, name="d_ffn1_in")
    grad_x = _mm(da1, W["ffn1_w_in"], add=dx_a, name="grad_x")

    small = dict(ln_g=jnp.concatenate([dg0, dg1, dg2, dg3], axis=0), ln_b=jnp.concatenate([db0, db1, db2, db3], axis=0),
                 ret_gn_g=dgn, q_norm_g=dqg, kv_norm_g=dkvg)
    return loss, grad_x, G, small


def kernel(x, p, positions, ln_g, ln_b, ffn1_w_in, ffn1_w_out, w_in, ret_gn_g, w_ret_o, q_norm_g, kv_norm_g, w_uq, w_ukv, w_mla_o, w_out, ffn2_w_in, ffn2_w_out, ple_w_gate, ple_w_proj, loss_target, m_ln_g, m_ln_b, m_ffn1_w_in, m_ffn1_w_out, m_w_in, m_ret_gn_g, m_w_ret_o, m_q_norm_g, m_kv_norm_g, m_w_uq, m_w_ukv, m_w_mla_o, m_w_out, m_ffn2_w_in, m_ffn2_w_out, m_ple_w_gate, m_ple_w_proj, v_ln_g, v_ln_b, v_ffn1_w_in, v_ffn1_w_out, v_w_in, v_ret_gn_g, v_w_ret_o, v_q_norm_g, v_kv_norm_g, v_w_uq, v_w_ukv, v_w_mla_o, v_w_out, v_ffn2_w_in, v_ffn2_w_out, v_ple_w_gate, v_ple_w_proj):
    names = ("ln_g", "ln_b", "ffn1_w_in", "ffn1_w_out", "w_in", "ret_gn_g", "w_ret_o", "q_norm_g", "kv_norm_g", "w_uq", "w_ukv",
             "w_mla_o", "w_out", "ffn2_w_in", "ffn2_w_out", "ple_w_gate", "ple_w_proj")
    ws = dict(zip(names, (ln_g, ln_b, ffn1_w_in, ffn1_w_out, w_in, ret_gn_g, w_ret_o, q_norm_g, kv_norm_g, w_uq, w_ukv, w_mla_o,
                          w_out, ffn2_w_in, ffn2_w_out, ple_w_gate, ple_w_proj)))
    ms = dict(zip(names, (m_ln_g, m_ln_b, m_ffn1_w_in, m_ffn1_w_out, m_w_in, m_ret_gn_g, m_w_ret_o, m_q_norm_g, m_kv_norm_g, m_w_uq,
                          m_w_ukv, m_w_mla_o, m_w_out, m_ffn2_w_in, m_ffn2_w_out, m_ple_w_gate, m_ple_w_proj)))
    vs = dict(zip(names, (v_ln_g, v_ln_b, v_ffn1_w_in, v_ffn1_w_out, v_w_in, v_ret_gn_g, v_w_ret_o, v_q_norm_g, v_kv_norm_g, v_w_uq,
                          v_w_ukv, v_w_mla_o, v_w_out, v_ffn2_w_in, v_ffn2_w_out, v_ple_w_gate, v_ple_w_proj)))

    parts = []
    for name, r in PACK:
        rows = _to_rows(name, ws[name])
        if _pad16(r) != r:
            rows = jnp.concatenate([rows, jnp.zeros((_pad16(r) - r, 1024), F32)], axis=0)
        parts.append(rows)
    wsh_first = jnp.concatenate(parts[:len(PACK_LATE)], axis=0).astype(BF16)
    wsh_rest = jnp.concatenate(parts[len(PACK_LATE):], axis=0).astype(BF16)
    ssh = jnp.concatenate([ln_g[0], ln_b[0]], axis=0)
    wall_first, sall = _all_gather(wsh_first, ssh)
    *gather_handles, start_token = _gather_start(wsh_rest, wall_first)
    W = _unpack(wall_first, PACK_LATE)
    ln_full = sall.reshape(N_DEV, 2, 4, 128).transpose(1, 2, 0, 3).reshape(2, 4, 1024)

    def rest_weights(after):
        w_thru, land = _gather_wait(*gather_handles, after)
        return _unpack(_gather_finish(w_thru, land), PACK_EARLY)

    cvec = lax.axis_index("c").astype(jnp.int32).reshape(1)
    qvec = (2 * lax.axis_index("x") + lax.axis_index("y")).astype(jnp.int32).reshape(1)

    def chip_partials(G, group, tag):
        gparts = []
        for name, r in group:
            g = G[name].reshape(N_DEV, r, 1024)
            if _pad16(r) != r:
                g = jnp.concatenate([g, jnp.zeros((N_DEV, _pad16(r) - r, 1024), g.dtype)], axis=1)
            gparts.append(g)
        gfull = jnp.concatenate(gparts, axis=1).astype(BF16)
        g4 = gfull.reshape(4, 2, gfull.shape[1], 1024)
        return _sum_sibling(g4, _exchange_sibling(g4, "exchange_sibling_" + tag), cvec, "sum_sibling_" + tag)

    in_flight = []

    def early(G):
        *handles, token = _chips_start(chip_partials(G, PACK_EARLY, "early"), "chips_start_early")
        in_flight.append(handles)
        return token

    loss_p, grad_x, G, small = _local_step(x[0], p[0, 0], positions, loss_target[0], W, ln_full[0], ln_full[1],
                                           ret_gn_g, q_norm_g, kv_norm_g, early=early, rest_weights=rest_weights,
                                           start_token=start_token)

    part_e, land_e = _chips_wait(*in_flight[0], grad_x, "chips_wait_early")
    gsh_early = _sum_chips(part_e, land_e, qvec, "sum_grads_early")
    pad256 = lambda a: jnp.concatenate([a, jnp.zeros((1, 1024 - a.shape[1]), F32)], axis=1)
    gsmall = jnp.concatenate([small["ln_g"], small["ln_b"], small["ret_gn_g"].reshape(2, 1024), pad256(small["q_norm_g"]),
                              pad256(small["kv_norm_g"]), jnp.zeros((SMALL_ROWS - 12, 1024), F32)], axis=0)
    srecv = _exchange_small(gsmall)
    part_l = chip_partials(G, PACK_LATE, "late")
    *late_handles, late_token = _chips_start(part_l, "chips_start_late", order_after=srecv)
    ssum = _sum_slots(srecv, "sum_small_grads")

    def unpack_grads(group, gsh):
        out, off = {}, 0
        for name, r in group:
            out[name] = _from_rows(name, gsh[off:off + r], ws[name].shape)
            off += _pad16(r)
        return out

    grads = unpack_grads(PACK_EARLY, gsh_early)
    me = 4 * lax.axis_index("x") + 2 * lax.axis_index("y") + lax.axis_index("c")
    grads["ln_g"] = lax.dynamic_slice(ssum[0:4], (0, me * 128), (4, 128)).reshape(1, 4, 128)
    grads["ln_b"] = lax.dynamic_slice(ssum[4:8], (0, me * 128), (4, 128)).reshape(1, 4, 128)
    grads["ret_gn_g"] = ssum[8:10].reshape(1, 2048)
    grads["q_norm_g"] = ssum[10:11, :256]
    grads["kv_norm_g"] = ssum[11:12, :256]

    delta, new_m, new_v = {}, {}, {}
    late_names = [n for n, _ in PACK_LATE]
    last = late_token
    for name in names:
        if name not in late_names:
            delta[name], new_m[name], new_v[name] = _adamw(ws[name], grads[name], ms[name], vs[name], "adamw_" + name,
                                                           order_after=last)
            last = new_v[name]
    part_l, land_l = _chips_wait(*late_handles, last, "chips_wait_late")
    grads.update(unpack_grads(PACK_LATE, _sum_chips(part_l, land_l, qvec, "sum_grads_late")))
    for name in late_names:
        delta[name], new_m[name], new_v[name] = _adamw(ws[name], grads[name], ms[name], vs[name], "adamw_" + name)

    loss = lax.psum(loss_p[0, 0], ("x", "y", "c"))
    return (loss, grad_x[None], *[grads[n] for n in names], *[delta[n] for n in names],
            *[new_m[n] for n in names], *[new_v[n] for n in names])
```

```python
import math

import jax
import jax.numpy as jnp
from jax import lax
from jax.experimental import pallas as pl
from jax.experimental.pallas import tpu as pltpu

F32 = jnp.float32
BF16 = jnp.bfloat16

N_DEV = 8
D = 1024
D_FF = 2816
D_PLE = 256
CHUNK = 64
HEADS = 8
RET_DK = 128
RET_DV = 256
MLA_NOPE = 128
MLA_ROPE = 64
MLA_DV = 128
LORA = 256
ROPE_BASE = 10000.0
EPS = 1e-5
ALPHA = 2.0 ** 0.25
RET_SCALE = RET_DK ** -0.5
MLA_SCALE = (MLA_NOPE + MLA_ROPE) ** -0.5
NEG = -1e30

ADAM_LR = 0.001
ADAM_B1 = 0.9
ADAM_B2 = 0.999
ADAM_EPS = 1e-08
ADAM_WD = 0.01
ADAM_STEP = 10

P_RQ, P_RK, P_RV, P_RG, P_GR, P_GM, P_CQ, P_CKV, P_KPE, P_W = 0, 1024, 2048, 4096, 6144, 7168, 8192, 8448, 8704, 8960
W_IN_COLS = 8768
RET_L = 256
ATT_TF = 2048
ATT_TB = 1024
ATT_HP = 1
LOG2E = math.log2(math.e)
Q_PRESCALE = MLA_SCALE * LOG2E

PACK = (("ffn1_w_in", 704), ("ffn1_w_out", 352), ("w_in", 1096), ("w_ret_o", 256), ("w_uq", 48), ("w_ukv", 64),
        ("w_mla_o", 128), ("w_out", 128), ("ffn2_w_in", 704), ("ffn2_w_out", 352), ("ple_w_gate", 128), ("ple_w_proj", 32))


def _pad16(r):
    return -(-r // 16) * 16


PACK_LATE = PACK[:2]
PACK_EARLY = PACK[2:]
SMALL_ROWS = 16


def _pcall(body, **kw):
    return pl.pallas_call(body, **kw)


def _pick(dim, prefs):
    for p in prefs:
        if dim % p == 0:
            return p
    return dim


def _sigmoid(x):
    return 1.0 / (1.0 + jnp.exp(-x))


def _silu(x):
    return x * _sigmoid(x)


def _ln(r, g, b):
    mu = jnp.mean(r, axis=-1, keepdims=True)
    var = jnp.mean(jnp.square(r - mu), axis=-1, keepdims=True)
    return (r - mu) * lax.rsqrt(var + EPS) * g + b


def _rms(x, g):
    return x * lax.rsqrt(jnp.mean(jnp.square(x), axis=-1, keepdims=True) + EPS) * g


def _dot(a, b, ca, cb):
    return lax.dot_general(a, b, (((ca,), (cb,)), ((), ())), preferred_element_type=F32)


def _accum(ref, val, first=None):
    @pl.when(pl.program_id(0) == 0 if first is None else first)
    def _():
        ref[...] = jnp.zeros_like(ref)

    ref[...] += val


def _mm(a, b, *, ta=False, tb=False, add=None, out_dtype=None, name, tm=None, tn=None, tk=None, epilogue=None):
    parts = a.shape[0] if a.ndim == 3 else 1
    ar, ac = a.shape[-2], a.shape[-1]
    out_dtype = out_dtype or (BF16 if ta else F32)
    if ta:
        K, M = ar, ac * parts
    else:
        M, K = ar, ac * parts
    if tb:
        N, K2 = b.shape
    else:
        K2, N = b.shape
    assert K == K2, (a.shape, b.shape, ta, tb)
    big = (1024, 1408, 1280, 768, 512, 256, 128)
    tm = tm or _pick(ac if (ta and parts > 1) else M, big)
    tn = tn or (N if N <= 1024 else _pick(N, big))
    kdim = ac if (not ta and parts > 1) else K
    tk = tk or (kdim if kdim <= 2816 and parts == 1 else
                _pick(kdim, (2048, 1408, 1280, 1024, 512) if tm <= 1024 else (1024, 1408, 1280, 512)))
    nk = K // tk
    grid = (M // tm, N // tn, nk)
    if parts > 1 and ta:
        per = ac // tm
        a_spec = pl.BlockSpec((None, tk, tm), lambda i, j, k: (i // per, k, i % per))
    elif parts > 1:
        per = ac // tk
        a_spec = pl.BlockSpec((None, tm, tk), lambda i, j, k: (k // per, i, k % per))
    else:
        a_spec = pl.BlockSpec((tk, tm), lambda i, j, k: (k, i)) if ta else pl.BlockSpec((tm, tk), lambda i, j, k: (i, k))
    b_spec = pl.BlockSpec((tn, tk), lambda i, j, k: (j, k)) if tb else pl.BlockSpec((tk, tn), lambda i, j, k: (k, j))
    o_spec = pl.BlockSpec((tm, tn), lambda i, j, k: (i, j))
    ca, cb = (0 if ta else 1), (1 if tb else 0)
    has_add = add is not None
    n_in = 2 + int(has_add)
    if epilogue is not None:
        assert tn == N and not ta
        ep_fn, ep_rows, ep_whole, ep_outs, ep_accs = epilogue
        n_ep_in = len(ep_rows) + len(ep_whole)
        n_out = len(ep_outs) + len(ep_accs)
    else:
        n_ep_in, n_out = 0, 1

    def body(*refs):
        a_ref, b_ref = refs[0], refs[1]
        add_ref = refs[2] if has_add else None
        o_ref = refs[n_in + n_ep_in]
        first_row_tile = pl.program_id(0) == 0

        def finish(r):
            if has_add:
                r = r + add_ref[...].astype(F32)
            if epilogue is not None:
                ep_fn(r, refs[n_in:n_in + n_ep_in], refs[n_in + n_ep_in:n_in + n_ep_in + n_out], first_row_tile)
            else:
                o_ref[...] = r.astype(out_dtype)

        if nk == 1:
            finish(_dot(a_ref[...], b_ref[...], ca, cb))
            return
        acc_ref = refs[-1]
        k = pl.program_id(2)

        @pl.when(k == 0)
        def _():
            acc_ref[...] = jnp.zeros_like(acc_ref)

        acc_ref[...] += _dot(a_ref[...], b_ref[...], ca, cb)

        @pl.when(k == nk - 1)
        def _():
            finish(acc_ref[...])

    in_specs = [a_spec, b_spec] + ([o_spec] if has_add else [])
    args = (a, b) + ((add,) if has_add else ())
    out_specs, out_shape = o_spec, jax.ShapeDtypeStruct((M, N), out_dtype)
    if epilogue is not None:
        for row_in in ep_rows:
            arr, col_block, width = (tuple(row_in) + (N,))[:3] if isinstance(row_in, tuple) else (row_in, 0, N)
            in_specs.append(pl.BlockSpec((tm, width), lambda i, j, k, _c=col_block: (i, _c)))
            args += (arr,)
        in_specs += [pl.BlockSpec(w.shape, lambda i, j, k, _n=w.ndim: (0,) * _n) for w in ep_whole]
        args += tuple(ep_whole)
        outs = [o if isinstance(o, tuple) else (o, N) for o in ep_outs]
        out_specs = ([pl.BlockSpec((tm, w), lambda i, j, k: (i, 0)) for _, w in outs]
                     + [pl.BlockSpec((r, N), lambda i, j, k: (0, 0)) for r in ep_accs])
        out_shape = [jax.ShapeDtypeStruct((M, w), dt) for dt, w in outs] + [jax.ShapeDtypeStruct((r, N), F32) for r in ep_accs]
    return _pcall(
        body, grid=grid, in_specs=in_specs, out_specs=out_specs, out_shape=out_shape,
        scratch_shapes=[pltpu.VMEM((tm, tn), F32)] if nk > 1 else [], name=name,
        compiler_params=pltpu.CompilerParams(dimension_semantics=("arbitrary" if epilogue is not None else "parallel", "parallel", "arbitrary")),
    )(*args)


def _mm_pieces(pieces, add, name):
    M, N = add.shape
    tm = _pick(M, (1024, 512, 256, 128))
    spans, in_specs, args, start = [], [], [], 0
    for a, b in pieces:
        K = a.shape[1]
        assert b.shape == (K, N) and a.shape[0] == M, (a.shape, b.shape)
        tk = _pick(K, (1024, 768, 512, 256, 128))
        n = K // tk
        spans.append((start, n))
        in_specs.append(pl.BlockSpec((tm, tk), lambda i, k, _s=start, _n=n: (i, jnp.clip(k - _s, 0, _n - 1))))
        in_specs.append(pl.BlockSpec((tk, N), lambda i, k, _s=start, _n=n: (jnp.clip(k - _s, 0, _n - 1), 0)))
        args += [a, b]
        start += n
    nk = start
    o_spec = pl.BlockSpec((tm, N), lambda i, k: (i, 0))

    def body(*refs):
        add_ref, o_ref, acc_ref = refs[-3], refs[-2], refs[-1]
        k = pl.program_id(1)

        @pl.when(k == 0)
        def _():
            acc_ref[...] = jnp.zeros_like(acc_ref)

        for p, (s, n) in enumerate(spans):
            @pl.when((k >= s) & (k < s + n))
            def _(p=p):
                acc_ref[...] += _dot(refs[2 * p][...], refs[2 * p + 1][...], 1, 0)

        @pl.when(k == nk - 1)
        def _():
            o_ref[...] = acc_ref[...] + add_ref[...]

    return _pcall(
        body, grid=(M // tm, nk), in_specs=in_specs + [o_spec], out_specs=o_spec, out_shape=jax.ShapeDtypeStruct((M, N), F32),
        scratch_shapes=[pltpu.VMEM((tm, N), F32)], name=name,
        compiler_params=pltpu.CompilerParams(dimension_semantics=("parallel", "arbitrary")),
    )(*args, add)


def _ln_epilogue(res, c, g, b):
    def fn(r, ins, outs, first):
        res_ref, g_ref, b_ref = ins
        f_ref, h_ref, hb_ref = outs
        h = _ln(ALPHA * res_ref[...] + c * r, g_ref[...], b_ref[...])
        f_ref[...] = r
        h_ref[...] = h
        hb_ref[...] = h.astype(BF16)

    return (fn, [res], [g, b], (F32, F32, BF16), ())


def _ln_bwd_epilogue(res, f, c, g, b):
    def fn(r, ins, outs, first):
        res_ref, f_ref, g_ref, b_ref = ins
        dr_ref, df_ref, dg_ref, db_ref = outs
        pre = ALPHA * res_ref[...] + c * f_ref[...]
        xc = pre - jnp.mean(pre, axis=-1, keepdims=True)
        rstd = lax.rsqrt(jnp.mean(jnp.square(xc), axis=-1, keepdims=True) + EPS)
        xhat = xc * rstd
        dyg = r * g_ref[...]
        dpre = rstd * (dyg - jnp.mean(dyg, axis=-1, keepdims=True) - xhat * jnp.mean(dyg * xhat, axis=-1, keepdims=True))
        dr_ref[...] = ALPHA * dpre
        df_ref[...] = (c * dpre).astype(BF16)
        _accum(dg_ref, jnp.sum(r * xhat, axis=0, keepdims=True), first)
        _accum(db_ref, jnp.sum(r, axis=0, keepdims=True), first)

    return (fn, [res, f], [g, b], (F32, BF16), (1, 1))


def _rows(body, T, tm, ins, outs, name, accs=()):
    in_specs, args = [], []
    for arr, w, cb in ins:
        if w is None:
            in_specs.append(pl.BlockSpec(arr.shape, lambda i, _n=arr.ndim: (0,) * _n))
        else:
            in_specs.append(pl.BlockSpec((tm, w), lambda i, _cb=cb: (i, _cb)))
        args.append(arr)
    out_specs = [pl.BlockSpec((tm, w), lambda i: (i, 0)) for w, _ in outs]
    out_shape = [jax.ShapeDtypeStruct((T, w), dt) for w, dt in outs]
    for r, w in accs:
        out_specs.append(pl.BlockSpec((r, w), lambda i: (0, 0)))
        out_shape.append(jax.ShapeDtypeStruct((r, w), F32))
    return _pcall(
        body, grid=(T // tm,), in_specs=in_specs, out_specs=out_specs, out_shape=out_shape, name=name,
        compiler_params=pltpu.CompilerParams(dimension_semantics=("arbitrary",)),
    )(*args)


def _ln_bwd_rows(res, f, c, g, b, dh, T, name):
    fn = _ln_bwd_epilogue(res, f, c, g, b)[0]

    def body(r_ref, f_ref, g_ref, b_ref, d_ref, dr_ref, df_ref, dg_ref, db_ref):
        fn(d_ref[...], (r_ref, f_ref, g_ref, b_ref), (dr_ref, df_ref, dg_ref, db_ref), pl.program_id(0) == 0)

    return _rows(body, T, 256, [(res, D, 0), (f, D, 0), (g, None, None), (b, None, None), (dh, D, 0)],
                 [(D, F32), (D, BF16)], name, accs=[(1, D), (1, D)])


FFN_TN = 1408
FFN_TM = 512
LN_TM = 512


def _ffn_in(x, wt, name, order_after=None):
    T = x.shape[0]
    tm, tn = min(FFN_TM, T), FFN_TN
    nj = D_FF // tn
    emit_xb = x.dtype != BF16
    n_in = 3 + int(order_after is not None)

    def body(*refs):
        x_ref, wg_ref, wu_ref = refs[:3]
        hs_ref, gu_ref = refs[n_in], refs[n_in + 1]
        xv = x_ref[...].astype(BF16)
        g = _dot(xv, wg_ref[...], 1, 1)
        u = _dot(xv, wu_ref[...], 1, 1)
        hs_ref[...] = (_silu(g) * u).astype(BF16)
        gu_ref[0] = g.astype(BF16)
        gu_ref[1] = u.astype(BF16)
        if emit_xb:
            refs[n_in + 2][...] = xv

    in_specs = [pl.BlockSpec((tm, D), lambda i, j: (i, 0)), pl.BlockSpec((tn, D), lambda i, j: (j, 0)),
                pl.BlockSpec((tn, D), lambda i, j: (j + nj, 0))]
    args = [x, wt, wt]
    if order_after is not None:
        in_specs.append(pl.BlockSpec(order_after.shape, lambda i, j: (0, 0)))
        args.append(order_after)
    out_specs = [pl.BlockSpec((tm, tn), lambda i, j: (i, j)), pl.BlockSpec((2, tm, tn), lambda i, j: (0, i, j))]
    out_shape = [jax.ShapeDtypeStruct((T, D_FF), BF16), jax.ShapeDtypeStruct((2, T, D_FF), BF16)]
    if emit_xb:
        out_specs.append(pl.BlockSpec((tm, D), lambda i, j: (i, 0)))
        out_shape.append(jax.ShapeDtypeStruct((T, D), BF16))
    return _pcall(
        body, grid=(T // tm, nj), in_specs=in_specs, out_specs=out_specs, out_shape=out_shape, name=name,
        compiler_params=pltpu.CompilerParams(dimension_semantics=("parallel", "arbitrary")),
    )(*args)


def _ffn_act_bwd(df, wo, gu, name):
    T = df.shape[0]
    tm, tn = min(FFN_TM, T), FFN_TN

    def body(d_ref, w_ref, gu_ref, o_ref):
        dhs = _dot(d_ref[...], w_ref[...], 1, 1)
        g, u = gu_ref[0].astype(F32), gu_ref[1].astype(F32)
        sig = _sigmoid(g)
        act = g * sig
        o_ref[0] = (dhs * u * (sig + act * (1.0 - sig))).astype(BF16)
        o_ref[1] = (dhs * act).astype(BF16)

    return _pcall(
        body, grid=(T // tm, D_FF // tn),
        in_specs=[pl.BlockSpec((tm, D), lambda i, j: (i, 0)), pl.BlockSpec((tn, D), lambda i, j: (j, 0)),
                  pl.BlockSpec((2, tm, tn), lambda i, j: (0, i, j))],
        out_specs=pl.BlockSpec((2, tm, tn), lambda i, j: (0, i, j)),
        out_shape=jax.ShapeDtypeStruct((2, T, D_FF), BF16), name=name,
        compiler_params=pltpu.CompilerParams(dimension_semantics=("parallel", "parallel")),
    )(df, wo, gu)


def _rope128(t, cos, sin_s):
    return t * cos + pltpu.roll(t, 64, 1) * sin_s


def _rope128_t(g, cos, sin_s):
    return g * cos - pltpu.roll(g, 64, 1) * sin_s


def _partner32(t):
    lane = lax.broadcasted_iota(jnp.int32, t.shape, 1)
    return jnp.where((lane & 32) == 0, pltpu.roll(t, 96, 1), pltpu.roll(t, 32, 1))


def _rope64(t, cos, sin_s):
    return t * cos + _partner32(t) * sin_s


def _rope64_t(g, cos, sin_s):
    return g * cos - _partner32(g) * sin_s


def _mixer_prep_fwd(proj, tabs, qg, kvg, T):
    cos128, sin128, cos64, sin64 = tabs

    def body(rq_ref, rk_ref, cq_ref, ckv_ref, kpe_ref, c1_ref, s1_ref, c2_ref, s2_ref, qg_ref, kvg_ref,
             oq_ref, ok_ref, oqn_ref, okvn_ref, okpe_ref):
        c1, s1 = c1_ref[...], s1_ref[...]
        for h in range(HEADS):
            sl = slice(h * RET_DK, (h + 1) * RET_DK)
            oq_ref[:, sl] = _rope128(rq_ref[:, sl].astype(F32), c1, s1).astype(BF16)
            ok_ref[:, sl] = (_rope128(rk_ref[:, sl].astype(F32), c1, s1) * RET_SCALE).astype(BF16)
        oqn_ref[...] = _rms(cq_ref[...].astype(F32), qg_ref[...]).astype(BF16)
        okvn_ref[...] = _rms(ckv_ref[...].astype(F32), kvg_ref[...]).astype(BF16)
        okpe_ref[...] = _rope64(kpe_ref[...].astype(F32), c2_ref[...], s2_ref[...]).astype(BF16)

    ins = [(proj, 1024, 0), (proj, 1024, 1), (proj, 256, P_CQ // 256), (proj, 256, P_CKV // 256),
           (proj, 128, P_KPE // 128), (cos128, 128, 0), (sin128, 128, 0), (cos64, 128, 0), (sin64, 128, 0),
           (qg, None, None), (kvg, None, None)]
    outs = [(1024, BF16), (1024, BF16), (LORA, BF16), (LORA, BF16), (128, BF16)]
    return _rows(body, T, 256, ins, outs, "mixer_prep_fwd")


def _rms_bwd(proj, dqn, dkvn, dkpe_all, tabs, qg, kvg, T):
    _, _, cos64, sin64 = tabs

    def body(cq_ref, ckv_ref, dq_ref, dkv_ref, dk_ref, c2_ref, s2_ref, qg_ref, kvg_ref, o_ref, dqg_ref, dkvg_ref):
        _, vjp = jax.vjp(_rms, cq_ref[...].astype(F32), qg_ref[...])
        dx, dg = vjp(dq_ref[...])
        o_ref[:, 0:LORA] = dx.astype(BF16)
        _accum(dqg_ref, dg)
        _, vjp = jax.vjp(_rms, ckv_ref[...].astype(F32), kvg_ref[...])
        dx, dg = vjp(dkv_ref[...])
        o_ref[:, P_CKV - P_CQ:P_CKV - P_CQ + LORA] = dx.astype(BF16)
        _accum(dkvg_ref, dg)
        g = dk_ref[:, 0:128]
        for h in range(1, HEADS):
            g = g + dk_ref[:, h * 128:(h + 1) * 128]
        lane = lax.broadcasted_iota(jnp.int32, g.shape, 1)
        g = jnp.where(lane < MLA_ROPE, g, 0.0)
        o_ref[:, P_KPE - P_CQ:P_KPE - P_CQ + 128] = _rope64_t(g, c2_ref[...], s2_ref[...]).astype(BF16)
        o_ref[:, P_KPE - P_CQ + 128:] = jnp.zeros((g.shape[0], P_W - P_KPE - 128), BF16)

    ins = [(proj, 256, P_CQ // 256), (proj, 256, P_CKV // 256), (dqn, LORA, 0), (dkvn, LORA, 0), (dkpe_all, 1024, 0),
           (cos64, 128, 0), (sin64, 128, 0), (qg, None, None), (kvg, None, None)]
    return _rows(body, T, 256, ins, [(P_W - P_CQ, BF16)], "rms_bwd", accs=[(1, LORA), (1, LORA)])


def _gn_gate_bwd(y, rg, g, d):
    xc = y - jnp.mean(y, axis=-1, keepdims=True)
    rstd = lax.rsqrt(jnp.mean(jnp.square(xc), axis=-1, keepdims=True) + EPS)
    xhat = xc * rstd
    sig = _sigmoid(rg)
    act = rg * sig
    dn = d * act
    drg = d * (xhat * g) * (sig + act * (1.0 - sig))
    dxh = dn * g
    dy = rstd * (dxh - jnp.mean(dxh, axis=-1, keepdims=True) - xhat * jnp.mean(dxh * xhat, axis=-1, keepdims=True))
    return dy, drg, jnp.sum(dn * xhat, axis=0, keepdims=True)


def _gn_gate(y, rg, g):
    mu = jnp.mean(y, axis=-1, keepdims=True)
    var = jnp.mean(jnp.square(y - mu), axis=-1, keepdims=True)
    return _silu(rg) * ((y - mu) * lax.rsqrt(var + EPS) * g)


def _q_assemble_epilogue(tabs):
    _, _, cos64, sin64 = tabs

    def fn(q, ins, outs, first):
        c_ref, s_ref = ins
        on_ref, op_ref = outs
        on_ref[...] = (q[:, :1024] * Q_PRESCALE).astype(BF16)
        c, s = c_ref[...], s_ref[...]
        lane = lax.broadcasted_iota(jnp.int32, c.shape, 1)
        for j in range(HEADS // 2):
            r = _rope64(q[:, 1024 + 128 * j:1024 + 128 * (j + 1)], c, s) * Q_PRESCALE
            op_ref[:, 256 * j:256 * j + 128] = jnp.where(lane < 64, r, 0.0).astype(BF16)
            op_ref[:, 256 * j + 128:256 * j + 256] = jnp.where(lane < 64, pltpu.roll(r, 64, 1), 0.0).astype(BF16)

    return (fn, [(cos64, 0, 128), (sin64, 0, 128)], [], ((BF16, 1024), (BF16, 1024)), ())


def _q_assemble_bwd(dqn, dqpe, tabs, T):
    _, _, cos64, sin64 = tabs

    def body(dn_ref, dp_ref, c_ref, s_ref, on_ref, op_ref):
        on_ref[...] = dn_ref[...].astype(BF16)
        c, s = c_ref[...], s_ref[...]
        lane = lax.broadcasted_iota(jnp.int32, c.shape, 1)
        for j in range(HEADS // 2):
            g = jnp.where(lane < 64, dp_ref[:, 256 * j:256 * j + 128], pltpu.roll(dp_ref[:, 256 * j + 128:256 * j + 256], 64, 1))
            op_ref[:, 128 * j:128 * (j + 1)] = _rope64_t(g, c, s).astype(BF16)

    return _rows(body, T, 256, [(dqn, 1024, 0), (dqpe, 1024, 0), (cos64, 128, 0), (sin64, 128, 0)],
                 [(1024, BF16), (512, BF16)], "q_assemble_bwd")


def _mix_fn(gr, gm, yr, ym):
    return _sigmoid(gr) * yr + _sigmoid(gm) * ym


def _mix_epilogue(proj, y_ret):
    def fn(r, ins, outs, first):
        gr_ref, gm_ref, yr_ref = ins
        ym_ref, mix_ref = outs
        ym_ref[...] = r.astype(BF16)
        mix_ref[...] = _mix_fn(gr_ref[...].astype(F32), gm_ref[...].astype(F32), yr_ref[...].astype(F32), r).astype(BF16)

    return (fn, [(proj, P_GR // D), (proj, P_GM // D), y_ret], [], (BF16, BF16), ())


def _mix_bwd_epilogue(proj, y_ret, y_mla):
    def fn(r, ins, outs, first):
        gr_ref, gm_ref, yr_ref, ym_ref = ins
        dgate_ref, dyr_ref, dym_ref = outs
        sr, sm = _sigmoid(gr_ref[...].astype(F32)), _sigmoid(gm_ref[...].astype(F32))
        dgate_ref[:, 0:D] = (r * yr_ref[...].astype(F32) * (sr * (1.0 - sr))).astype(BF16)
        dgate_ref[:, D:2 * D] = (r * ym_ref[...].astype(F32) * (sm * (1.0 - sm))).astype(BF16)
        dyr_ref[...] = (r * sr).astype(BF16)
        dym_ref[...] = (r * sm).astype(BF16)

    return (fn, [(proj, P_GR // D), (proj, P_GM // D), y_ret, y_mla], [], ((BF16, 2 * D), BF16, BF16), ())


def _head_epilogue(h3, pp, tgt, g, b):
    def fn(r, ins, outs, first):
        h_ref, pp_ref, t_ref, g_ref, b_ref = ins
        dh_ref, dgl_ref, dpp_ref, dg_ref, db_ref, loss_ref = outs

        sg, pp, gain = _sigmoid(r), pp_ref[...], g_ref[...]
        pre = ALPHA * h_ref[...] + sg * pp
        xc = pre - jnp.mean(pre, axis=-1, keepdims=True)
        rstd = lax.rsqrt(jnp.mean(jnp.square(xc), axis=-1, keepdims=True) + EPS)
        xhat = xc * rstd
        err = xhat * gain + b_ref[...] - t_ref[...]
        dy = err * (1.0 / D)
        dyg = dy * gain
        dpre = rstd * (dyg - jnp.mean(dyg, axis=-1, keepdims=True) - xhat * jnp.mean(dyg * xhat, axis=-1, keepdims=True))
        dh_ref[...] = ALPHA * dpre
        dgl_ref[...] = (dpre * pp * (sg * (1.0 - sg))).astype(BF16)
        dpp_ref[...] = (dpre * sg).astype(BF16)
        _accum(dg_ref, jnp.sum(dy * xhat, axis=0, keepdims=True), first)
        _accum(db_ref, jnp.sum(dy, axis=0, keepdims=True), first)
        part = 0.5 * jnp.sum(jnp.mean(jnp.square(err), axis=-1, keepdims=True), axis=0, keepdims=True)
        _accum(loss_ref, jnp.broadcast_to(part, loss_ref.shape), first)

    return (fn, [h3, pp, tgt], [g, b], (F32, BF16, BF16), (1, 1, 8))


def _delta_epilogue(o):
    def fn(r, ins, outs, first):
        (o_ref,) = ins
        db_ref, dl_ref = outs
        db_ref[...] = r.astype(BF16)
        for h in range(HEADS):
            sl = slice(h * MLA_DV, (h + 1) * MLA_DV)
            dl = jnp.sum(r[:, sl] * o_ref[:, sl].astype(F32), axis=-1, keepdims=True)
            dl_ref[:, sl] = jnp.broadcast_to(dl, (r.shape[0], MLA_DV))

    return (fn, [o], [], (BF16, F32), ())


def _ret_consts():
    L = RET_L
    lg = jnp.log(1.0 - 2.0 ** (-5.0 - jnp.arange(HEADS, dtype=F32)))[:, None, None]
    idx = jnp.arange(L, dtype=F32)
    ch = jnp.arange(L) // CHUNK
    dist = idx[:, None] - idx[None, :]
    same = (ch[:, None] == ch[None, :])[None]
    earlier = (ch[None, :] < ch[:, None])[None]
    dm = jnp.where(same, jnp.exp(lg * jnp.abs(dist)[None]), jnp.where(earlier, jnp.exp(lg * dist[None]), 0.0))
    xi = jnp.broadcast_to(jnp.exp(lg * (idx + 1.0)[None, :, None]), (HEADS, L, 128))
    zeta = jnp.broadcast_to(jnp.exp(lg * (L - 1.0 - idx)[None, :, None]), (HEADS, L, 128))
    gl = jnp.broadcast_to(jnp.exp(lg * float(L)), (HEADS, 8, 128))
    return dm.astype(F32), xi.astype(F32), zeta.astype(F32), gl.astype(F32)


def _whole(arr):
    return pl.BlockSpec(arr.shape, lambda n, _nd=arr.ndim: (0,) * _nd)


def _ret_fwd(q, k, v, proj, gn_g, consts, T):
    dm, xi, zeta, gl = consts
    L = RET_L
    n_sc = T // L

    def body(q_ref, k_ref, v_ref, rg_ref, g_ref, dm_ref, xi_ref, ze_ref, gl_ref, y_ref, yr_ref, s_ref, st_ref):
        @pl.when(pl.program_id(0) == 0)
        def _():
            st_ref[...] = jnp.zeros_like(st_ref)

        for h in range(HEADS):
            ks, vs = slice(h * RET_DK, (h + 1) * RET_DK), slice(h * RET_DV, (h + 1) * RET_DV)
            qq, kk, vv = q_ref[:, ks], k_ref[:, ks], v_ref[:, vs]
            st = st_ref[h]
            s_ref[h, 0] = st
            p = (_dot(qq, kk, 1, 1) * dm_ref[h]).astype(BF16)
            cross = _dot(qq, st.astype(BF16), 1, 0)
            xi_c = jnp.concatenate([xi_ref[h], xi_ref[h]], axis=1)
            y = _dot(p, vv, 1, 0) + cross * xi_c
            y_ref[:, vs] = y
            yr_ref[:, vs] = _gn_gate(y, rg_ref[:, vs].astype(F32), g_ref[:, vs]).astype(BF16)
            kz = (kk.astype(F32) * ze_ref[h]).astype(BF16)
            gl2 = jnp.concatenate([gl_ref[h, 0:1, :], gl_ref[h, 0:1, :]], axis=1)
            st_ref[h] = st * gl2 + _dot(kz, vv, 0, 0)

    return _pcall(
        body, grid=(n_sc,),
        in_specs=[pl.BlockSpec((L, 1024), lambda n: (n, 0)), pl.BlockSpec((L, 1024), lambda n: (n, 0)),
                  pl.BlockSpec((L, 2048), lambda n: (n, P_RV // 2048)), pl.BlockSpec((L, 2048), lambda n: (n, P_RG // 2048)),
                  _whole(gn_g), _whole(dm), _whole(xi), _whole(zeta), _whole(gl)],
        out_specs=[pl.BlockSpec((L, 2048), lambda n: (n, 0)), pl.BlockSpec((L, 2048), lambda n: (n, 0)),
                   pl.BlockSpec((HEADS, 1, 128, 256), lambda n: (0, n, 0, 0))],
        out_shape=[jax.ShapeDtypeStruct((T, HEADS * RET_DV), F32), jax.ShapeDtypeStruct((T, HEADS * RET_DV), BF16),
                   jax.ShapeDtypeStruct((HEADS, n_sc, 128, 256), F32)],
        scratch_shapes=[pltpu.VMEM((HEADS, 128, 256), F32)], name="ret_fwd",
        compiler_params=pltpu.CompilerParams(dimension_semantics=("arbitrary",)),
    )(q, k, v, proj, gn_g, dm, xi, zeta, gl)


def _ret_bwd(q, k, v, y, proj, gn_g, states, dyr, consts, tabs, T):
    dm, xi, zeta, gl = consts
    cos128, sin128, _, _ = tabs
    L = RET_L
    n_sc = T // L

    def body(q_ref, k_ref, v_ref, y_ref, rg_ref, g_ref, d_ref, s_ref, dm_ref, xi_ref, ze_ref, gl_ref, c_ref, sn_ref,
             dp_ref, dg_ref, gs_ref):
        @pl.when(pl.program_id(0) == 0)
        def _():
            gs_ref[...] = jnp.zeros_like(gs_ref)

        c, sn = c_ref[...], sn_ref[...]
        dgs = []
        for h in range(HEADS):
            ks, vs = slice(h * RET_DK, (h + 1) * RET_DK), slice(h * RET_DV, (h + 1) * RET_DV)
            dy, drg, dg = _gn_gate_bwd(y_ref[:, vs], rg_ref[:, vs].astype(F32), g_ref[:, vs], d_ref[:, vs].astype(F32))
            dp_ref[:, P_RG + h * RET_DV:P_RG + (h + 1) * RET_DV] = drg.astype(BF16)
            dgs.append(dg)
            qq, kk, vv, dyy = q_ref[:, ks], k_ref[:, ks], v_ref[:, vs], dy.astype(BF16)
            dmm = dm_ref[h]
            gb = gs_ref[h].astype(BF16)
            sb = s_ref[h, 0].astype(BF16)
            xi_c = jnp.concatenate([xi_ref[h], xi_ref[h]], axis=1)
            pb = (_dot(qq, kk, 1, 1) * dmm).astype(BF16)
            kz = (kk.astype(F32) * ze_ref[h]).astype(BF16)
            dp_ref[:, P_RV + h * RET_DV:P_RV + (h + 1) * RET_DV] = (_dot(pb, dyy, 0, 0) + _dot(kz, gb, 1, 0)).astype(BF16)
            da = (_dot(dyy, vv, 1, 1) * dmm).astype(BF16)
            dyx = (dyy.astype(F32) * xi_c).astype(BF16)
            dq = _dot(da, kk, 1, 0) + _dot(dyx, sb, 1, 1)
            dk = _dot(da, qq, 0, 0) + _dot(vv, gb, 1, 1) * ze_ref[h]
            dp_ref[:, P_RQ + h * RET_DK:P_RQ + (h + 1) * RET_DK] = _rope128_t(dq, c, sn).astype(BF16)
            dp_ref[:, P_RK + h * RET_DK:P_RK + (h + 1) * RET_DK] = (_rope128_t(dk, c, sn) * RET_SCALE).astype(BF16)
            gl2 = jnp.concatenate([gl_ref[h, 0:1, :], gl_ref[h, 0:1, :]], axis=1)
            gs_ref[h] = gs_ref[h] * gl2 + _dot(qq, dyx, 0, 0)
        _accum(dg_ref, jnp.concatenate(dgs, axis=1))

    rev = lambda n: n_sc - 1 - n
    return _pcall(
        body, grid=(n_sc,),
        in_specs=[pl.BlockSpec((L, 1024), lambda n: (rev(n), 0)), pl.BlockSpec((L, 1024), lambda n: (rev(n), 0)),
                  pl.BlockSpec((L, 2048), lambda n: (rev(n), P_RV // 2048)), pl.BlockSpec((L, 2048), lambda n: (rev(n), 0)),
                  pl.BlockSpec((L, 2048), lambda n: (rev(n), P_RG // 2048)), _whole(gn_g),
                  pl.BlockSpec((L, 2048), lambda n: (rev(n), 0)),
                  pl.BlockSpec((HEADS, 1, 128, 256), lambda n: (0, rev(n), 0, 0)),
                  _whole(dm), _whole(xi), _whole(zeta), _whole(gl),
                  pl.BlockSpec((L, 128), lambda n: (rev(n), 0)), pl.BlockSpec((L, 128), lambda n: (rev(n), 0))],
        out_specs=[pl.BlockSpec((L, P_GR), lambda n: (rev(n), 0)), pl.BlockSpec((1, 2048), lambda n: (0, 0))],
        out_shape=[jax.ShapeDtypeStruct((T, P_GR), BF16), jax.ShapeDtypeStruct((1, 2048), F32)],
        scratch_shapes=[pltpu.VMEM((HEADS, 128, 256), F32)], name="ret_bwd",
        compiler_params=pltpu.CompilerParams(dimension_semantics=("arbitrary",)),
    )(q, k, v, y, proj, gn_g, dyr, states, dm, xi, zeta, gl, cos128, sin128)


def _diag_mask(nrows, ncols, row0):
    row = row0 + lax.broadcasted_iota(jnp.int32, (nrows, ncols), 0)
    col = lax.broadcasted_iota(jnp.int32, (nrows, ncols), 1)
    return lax.shift_right_logical(col, 6) <= lax.shift_right_logical(row, 6)


def _diag_spans(t):
    h = t // 2
    return ((0, h, h), (h, h, t)) if h % 128 == 0 else ((0, t, t),)


def _tri_steps(nb, by_key):
    if by_key:
        pairs = [(i, j) for j in range(nb) for i in range(j, nb)]
    else:
        pairs = [(i, j) for i in range(nb) for j in range(i + 1)]
    return jnp.array([a for a, _ in pairs], jnp.int32), jnp.array([b for _, b in pairs], jnp.int32)


def _attn_fwd(qn, qpe, kv, kpe, T):
    t = min(ATT_TF, T)
    nb = T // t
    ii, jj = _tri_steps(nb, by_key=False)

    hp = ATT_HP
    w = 128 * hp

    def body(ii_ref, jj_ref, qn_ref, qp_ref, kn_ref, kp_ref, v_ref, o_ref, lse_ref, m_sc, l_sc, acc_sc):
        st = pl.program_id(1)
        i, j = ii_ref[st], jj_ref[st]

        @pl.when(j == 0)
        def _():
            m_sc[...] = jnp.full_like(m_sc, NEG)
            l_sc[...] = jnp.zeros_like(l_sc)
            acc_sc[...] = jnp.zeros_like(acc_sc)

        def update(diag):
            kp = kp_ref[...]
            spans = _diag_spans(t) if diag else ((0, t, t),)
            for hh in range(hp):
                sl = slice(128 * hh, 128 * (hh + 1))
                for row0, nr, nkeys in spans:
                    rows = slice(row0, row0 + nr)
                    q = jnp.concatenate([qn_ref[rows, sl], qp_ref[rows, sl]], axis=1)
                    k = jnp.concatenate([kn_ref[:nkeys, sl], kp[:nkeys]], axis=1)
                    s = _dot(q, k, 1, 1)
                    if diag:
                        s = jnp.where(_diag_mask(nr, nkeys, row0), s, NEG)
                    m_prev = m_sc[rows, sl]
                    m_new = jnp.maximum(m_prev, jnp.max(s, axis=1, keepdims=True))
                    a = jnp.exp2(m_prev - m_new)
                    p = jnp.exp2(s - m_new[:, 0:1])
                    l_sc[rows, sl] = a * l_sc[rows, sl] + jnp.sum(p, axis=1, keepdims=True)
                    acc_sc[rows, sl] = a * acc_sc[rows, sl] + _dot(p.astype(BF16), v_ref[:nkeys, sl], 1, 0)
                    m_sc[rows, sl] = m_new

        @pl.when(j < i)
        def _():
            update(False)

        @pl.when(j == i)
        def _():
            update(True)
            o_ref[...] = (acc_sc[...] / l_sc[...]).astype(BF16)
            lse_ref[...] = m_sc[...] + jnp.log2(l_sc[...])

    qs = pl.BlockSpec((t, w), lambda h, s, ii, jj: (ii[s], h))
    grid_spec = pltpu.PrefetchScalarGridSpec(
        num_scalar_prefetch=2, grid=(HEADS // hp, int(ii.shape[0])),
        in_specs=[qs, qs, pl.BlockSpec((t, w), lambda h, s, ii, jj: (jj[s], h)), pl.BlockSpec((t, 128), lambda h, s, ii, jj: (jj[s], 0)),
                  pl.BlockSpec((t, w), lambda h, s, ii, jj: (jj[s], HEADS // hp + h))],
        out_specs=[qs, qs],
        scratch_shapes=[pltpu.VMEM((t, w), F32), pltpu.VMEM((t, w), F32), pltpu.VMEM((t, w), F32)])
    return _pcall(
        body, grid_spec=grid_spec, out_shape=[jax.ShapeDtypeStruct((T, D), BF16), jax.ShapeDtypeStruct((T, D), F32)], name="attn_fwd",
        compiler_params=pltpu.CompilerParams(dimension_semantics=("arbitrary", "arbitrary")),
    )(ii, jj, qn, qpe, kv, kpe, kv)


def _attn_bwd(qn, qpe, kv, kpe, do, lse, delta, T):
    t = min(ATT_TB, T)
    nb = T // t
    ii, jj = _tri_steps(nb, by_key=True)

    def body(ii_ref, jj_ref, qn_ref, qp_ref, kn_ref, kp_ref, v_ref, do_ref, lse_ref, dl_ref,
             dqn_ref, dqp_ref, dkn_ref, dkp_ref, dv_ref, dk_sc, dv_sc):
        st = pl.program_id(1)
        i, j = ii_ref[st], jj_ref[st]

        @pl.when(st == 0)
        def _():
            dqn_ref[...] = jnp.zeros_like(dqn_ref)
            dqp_ref[...] = jnp.zeros_like(dqp_ref)

        @pl.when(i == j)
        def _():
            dk_sc[...] = jnp.zeros_like(dk_sc)
            dv_sc[...] = jnp.zeros_like(dv_sc)

        def update(diag):
            for row0, nr, nkeys in (_diag_spans(t) if diag else ((0, t, t),)):
                rows = slice(row0, row0 + nr)
                q = jnp.concatenate([qn_ref[rows, :], qp_ref[rows, :]], axis=1)
                k = jnp.concatenate([kn_ref[:nkeys, :], kp_ref[:nkeys, :]], axis=1)
                dob = do_ref[rows, :]
                s = _dot(q, k, 1, 1)
                if diag:
                    s = jnp.where(_diag_mask(nr, nkeys, row0), s, NEG)
                p = jnp.exp2(s - lse_ref[rows, 0:1])
                dv_sc[:nkeys, :] += _dot(p.astype(BF16), dob, 0, 0)
                dp = _dot(dob, v_ref[:nkeys, :], 1, 1)
                ds = (p * (dp - dl_ref[rows, 0:1])).astype(BF16)
                dk_sc[:nkeys, :] += _dot(ds, q, 0, 0)
                dq = _dot(ds, k, 1, 0) * MLA_SCALE
                grows = pl.ds(pl.multiple_of(i * t + row0, nr), nr)
                dqn_ref[grows, :] += dq[:, :128]
                dqp_ref[grows, :] += dq[:, 128:]

        @pl.when(i > j)
        def _():
            update(False)

        @pl.when(i == j)
        def _():
            update(True)

        @pl.when(i == nb - 1)
        def _():
            dkn_ref[...] = (dk_sc[:, :128] * (1.0 / LOG2E)).astype(BF16)
            dkp_ref[...] = dk_sc[:, 128:] * (1.0 / LOG2E)
            dv_ref[...] = dv_sc[...].astype(BF16)

    qs = pl.BlockSpec((t, 128), lambda h, s, ii, jj: (ii[s], h))
    ks = pl.BlockSpec((t, 128), lambda h, s, ii, jj: (jj[s], h))
    hs = pl.BlockSpec((T, 128), lambda h, s, ii, jj: (0, h))
    grid_spec = pltpu.PrefetchScalarGridSpec(
        num_scalar_prefetch=2, grid=(HEADS, int(ii.shape[0])),
        in_specs=[qs, qs, ks, pl.BlockSpec((t, 128), lambda h, s, ii, jj: (jj[s], 0)),
                  pl.BlockSpec((t, 128), lambda h, s, ii, jj: (jj[s], HEADS + h)), qs, qs, qs],
        out_specs=[hs, hs, ks, ks, ks],
        scratch_shapes=[pltpu.VMEM((t, 256), F32), pltpu.VMEM((t, 128), F32)])
    return _pcall(
        body, grid_spec=grid_spec,
        out_shape=[jax.ShapeDtypeStruct((T, D), F32), jax.ShapeDtypeStruct((T, D), F32), jax.ShapeDtypeStruct((T, D), BF16),
                   jax.ShapeDtypeStruct((T, D), F32), jax.ShapeDtypeStruct((T, D), BF16)], name="attn_bwd",
        compiler_params=pltpu.CompilerParams(dimension_semantics=("arbitrary", "arbitrary")),
    )(ii, jj, qn, qpe, kv, kpe, kv, do, lse, delta)


def _mesh_pos():
    x, y, c = lax.axis_index("x"), lax.axis_index("y"), lax.axis_index("c")
    return x, y, c, 4 * x + 2 * y + c


def _peer(x, y, c, k):
    px, py, pc = (x + ((k >> 2) & 1)) % 2, (y + ((k >> 1) & 1)) % 2, (c + (k & 1)) % 2
    return (px, py, pc), 4 * px + 2 * py + pc


_ANY = pl.BlockSpec(memory_space=pl.ANY)


def _rcopy(src, dst, send_sems, recv_sems, k, to):
    return pltpu.make_async_remote_copy(src_ref=src, dst_ref=dst, send_sem=send_sems.at[k], recv_sem=recv_sems.at[k],
                                        device_id=to, device_id_type=pl.DeviceIdType.MESH)


def _all_gather(wsh, ssh):
    def body(w_ref, s_ref, wall_ref, sall_ref, send_sems, recv_sems, loc_sems):
        x, y, c, me = _mesh_pos()
        sib = (x, y, 1 - c)
        chips = [(1 - x, y), (x, 1 - y), (1 - x, 1 - y)]
        slot = lambda px, py, pc: 4 * px + 2 * py + pc
        loc = [pltpu.make_async_copy(s_ref, sall_ref.at[me], loc_sems.at[0])]
        for cp in loc:
            cp.start()
        sends, fwd_waits = [], []
        for n, (src, dst) in enumerate(((w_ref, wall_ref), (s_ref, sall_ref))):
            o = 7 * n
            sends.append(_rcopy(src, dst.at[me], send_sems, recv_sems, o, sib))
            for j, chip in enumerate(chips):
                sends.append(_rcopy(src, dst.at[me], send_sems, recv_sems, o + 1 + j, (*chip, c)))
        for cp in sends:
            cp.start()
        for n, (src, dst) in enumerate(((w_ref, wall_ref), (s_ref, sall_ref))):
            o = 7 * n
            for j, chip in enumerate(chips):
                got = dst.at[slot(*chip, c)]
                _rcopy(src, got, send_sems, recv_sems, o + 1 + j, sib).wait_recv()
                fw = _rcopy(got, got, send_sems, recv_sems, o + 4 + j, sib)
                fw.start()
                sends.append(fw)
            fwd_waits.append(_rcopy(src, dst.at[slot(x, y, 1 - c)], send_sems, recv_sems, o, sib))
            for j, chip in enumerate(chips):
                fwd_waits.append(_rcopy(src, dst.at[slot(*chip, 1 - c)], send_sems, recv_sems, o + 4 + j, sib))
        for cp in fwd_waits:
            cp.wait_recv()
        for cp in sends:
            cp.wait_send()
        for cp in loc:
            cp.wait()

    wall, sall = _pcall(
        body, in_specs=[_ANY, _ANY], out_specs=[_ANY, _ANY],
        out_shape=[jax.ShapeDtypeStruct((N_DEV,) + wsh.shape, wsh.dtype), jax.ShapeDtypeStruct((N_DEV,) + ssh.shape, ssh.dtype)],
        scratch_shapes=[pltpu.SemaphoreType.DMA((14,)), pltpu.SemaphoreType.DMA((14,)), pltpu.SemaphoreType.DMA((1,))],
        name="all_gather_weights",
    )(wsh, ssh)
    me = 4 * lax.axis_index("x") + 2 * lax.axis_index("y") + lax.axis_index("c")
    return lax.dynamic_update_index_in_dim(wall, wsh, me, axis=0), sall


_HBM = pl.BlockSpec(memory_space=pltpu.HBM)
_SEM = pl.BlockSpec(memory_space=pltpu.SEMAPHORE)
_EFFECT = pltpu.SideEffectType.DATAFLOW_SIDE_EFFECTING


def _other_chips(x, y):
    return [(1 - x, y), (x, 1 - y), (1 - x, 1 - y)]


def _gather_start(wsh, order_after):
    def body(w_ref, land_ref, dep_ref, send_sems, recv_sems, w_thru, land_thru, token):
        x, y, c, me = _mesh_pos()
        _rcopy(w_ref, land_ref.at[me], send_sems, recv_sems, 0, (x, y, 1 - c)).start()
        for j, chip in enumerate(_other_chips(x, y)):
            _rcopy(w_ref, land_ref.at[me], send_sems, recv_sems, 1 + j, (*chip, c)).start()
        token[...] = jnp.zeros_like(token)

    shape = (N_DEV,) + wsh.shape
    land = pltpu.with_memory_space_constraint(lax.empty(shape, wsh.dtype), pltpu.HBM)
    return _pcall(
        body, name="gather_start",
        out_shape=(pltpu.SemaphoreType.DMA((4,)), pltpu.SemaphoreType.DMA((4,)), pltpu.HBM(wsh.shape, wsh.dtype),
                   pltpu.HBM(shape, wsh.dtype), jax.ShapeDtypeStruct((8, 128), F32)),
        in_specs=(_HBM, _HBM, _ANY), out_specs=(_SEM, _SEM, _HBM, _HBM, pl.BlockSpec(memory_space=pltpu.VMEM)),
        input_output_aliases={0: 2, 1: 3}, compiler_params=pltpu.CompilerParams(has_side_effects=_EFFECT),
    )(pltpu.with_memory_space_constraint(wsh, pltpu.HBM), land, order_after)


def _gather_wait(send_sems, recv_sems, w_thru, land_thru, after):
    def body(w_ref, land_ref, send_sems, recv_sems, after_ref, w_out, land_out):
        x, y, c, _ = _mesh_pos()
        senders = [(x, y, 1 - c)] + [(*chip, c) for chip in _other_chips(x, y)]
        for k, (px, py, pc) in enumerate(senders):
            cp = _rcopy(w_ref, land_ref.at[4 * px + 2 * py + pc], send_sems, recv_sems, k, (px, py, pc))
            cp.wait_send()
            cp.wait_recv()

    return _pcall(
        body, name="gather_wait", out_shape=(pltpu.HBM(w_thru.shape, w_thru.dtype), pltpu.HBM(land_thru.shape, land_thru.dtype)),
        in_specs=(_HBM, _HBM, _SEM, _SEM, _ANY), out_specs=(_HBM, _HBM), input_output_aliases={0: 0, 1: 1},
        compiler_params=pltpu.CompilerParams(has_side_effects=_EFFECT),
    )(w_thru, land_thru, send_sems, recv_sems, after)


def _gather_finish(wsh, land):
    def body(land_ref, out_ref, send_sems, recv_sems):
        x, y, c, _ = _mesh_pos()
        sib = (x, y, 1 - c)
        sends, recvs = [], []
        for j, (px, py) in enumerate(_other_chips(x, y)):
            mine, theirs = 4 * px + 2 * py + c, 4 * px + 2 * py + (1 - c)
            sends.append(_rcopy(land_ref.at[mine], out_ref.at[mine], send_sems, recv_sems, j, sib))
            recvs.append(_rcopy(land_ref.at[theirs], out_ref.at[theirs], send_sems, recv_sems, j, sib))
        for cp in sends:
            cp.start()
        for cp in recvs:
            cp.wait_recv()
        for cp in sends:
            cp.wait_send()

    done = _pcall(
        body, in_specs=[_ANY], out_specs=_ANY, out_shape=jax.ShapeDtypeStruct(land.shape, land.dtype),
        input_output_aliases={0: 0},
        scratch_shapes=[pltpu.SemaphoreType.DMA((3,)), pltpu.SemaphoreType.DMA((3,))], name="gather_finish",
    )(land)
    me = 4 * lax.axis_index("x") + 2 * lax.axis_index("y") + lax.axis_index("c")
    return lax.dynamic_update_index_in_dim(done, wsh, me, axis=0)


_ROW_TILES = (400, 368, 352, 256, 128, 16, 8)


def _exchange_sibling(g4, name):
    def body(g_ref, r_ref, send_sems, recv_sems):
        x, y, c, _ = _mesh_pos()
        sib = (x, y, 1 - c)
        cps = [_rcopy(g_ref.at[q, 1 - c], r_ref.at[q], send_sems, recv_sems, q, sib) for q in range(4)]
        for cp in cps:
            cp.start()
        for cp in cps:
            cp.wait()

    n, _, R, C = g4.shape
    return _pcall(
        body, in_specs=[_ANY], out_specs=_ANY, out_shape=jax.ShapeDtypeStruct((n, R, C), g4.dtype),
        scratch_shapes=[pltpu.SemaphoreType.DMA((4,)), pltpu.SemaphoreType.DMA((4,))], name=name,
    )(g4)


def _sum_sibling(g4, rsib, cvec, name):
    n, _, R, C = g4.shape
    tr = _pick(R, _ROW_TILES)

    def body(c_ref, g_ref, r_ref, o_ref):
        o_ref[...] = (g_ref[...].astype(F32) + r_ref[...].astype(F32)).astype(o_ref.dtype)

    grid_spec = pltpu.PrefetchScalarGridSpec(
        num_scalar_prefetch=1, grid=(n, R // tr),
        in_specs=[pl.BlockSpec((None, None, tr, C), lambda q, i, cr: (q, cr[0], i, 0)), pl.BlockSpec((None, tr, C), lambda q, i, cr: (q, i, 0))],
        out_specs=pl.BlockSpec((None, tr, C), lambda q, i, cr: (q, i, 0)))
    return _pcall(body, grid_spec=grid_spec, out_shape=jax.ShapeDtypeStruct((n, R, C), g4.dtype), name=name)(cvec, g4, rsib)


def _chips_start(part, name, order_after=None):
    n_in = 2 + int(order_after is not None)

    def body(*refs):
        p_ref, land_ref = refs[:2]
        send_sems, recv_sems, token = refs[n_in], refs[n_in + 1], refs[n_in + 4]
        x, y, c, _ = _mesh_pos()
        myq = 2 * x + y
        for j, (px, py) in enumerate(_other_chips(x, y)):
            _rcopy(p_ref.at[2 * px + py], land_ref.at[myq], send_sems, recv_sems, j, (px, py, c)).start()
        token[...] = jnp.zeros_like(token)

    land = pltpu.with_memory_space_constraint(lax.empty(part.shape, part.dtype), pltpu.HBM)
    extra = () if order_after is None else (order_after,)
    return _pcall(
        body, name=name,
        out_shape=(pltpu.SemaphoreType.DMA((3,)), pltpu.SemaphoreType.DMA((3,)), pltpu.HBM(part.shape, part.dtype),
                   pltpu.HBM(part.shape, part.dtype), jax.ShapeDtypeStruct((8, 128), F32)),
        in_specs=(_HBM, _HBM) + (_ANY,) * len(extra),
        out_specs=(_SEM, _SEM, _HBM, _HBM, pl.BlockSpec(memory_space=pltpu.VMEM)),
        input_output_aliases={0: 2, 1: 3}, compiler_params=pltpu.CompilerParams(has_side_effects=_EFFECT),
    )(pltpu.with_memory_space_constraint(part, pltpu.HBM), land, *extra)


def _chips_wait(send_sems, recv_sems, p_thru, land_thru, after, name):
    def body(p_ref, land_ref, send_sems, recv_sems, after_ref, p_out, land_out):
        x, y, c, _ = _mesh_pos()
        for j, (px, py) in enumerate(_other_chips(x, y)):
            q = 2 * px + py
            cp = _rcopy(p_ref.at[q], land_ref.at[q], send_sems, recv_sems, j, (px, py, c))
            cp.wait_send()
            cp.wait_recv()

    return _pcall(
        body, name=name, out_shape=(pltpu.HBM(p_thru.shape, p_thru.dtype), pltpu.HBM(p_thru.shape, p_thru.dtype)),
        in_specs=(_HBM, _HBM, _SEM, _SEM, _ANY), out_specs=(_HBM, _HBM), input_output_aliases={0: 0, 1: 1},
        compiler_params=pltpu.CompilerParams(has_side_effects=_EFFECT),
    )(p_thru, land_thru, send_sems, recv_sems, after)


def _sum_chips(part, land, qvec, name):
    n, R, C = part.shape
    tr = _pick(R, _ROW_TILES)

    def body(q_ref, p_ref, *refs):
        o_ref = refs[n]
        acc = None
        for q in range(n):
            term = jnp.where(q_ref[0] == q, p_ref[...], refs[q][...]).astype(F32)
            acc = term if acc is None else acc + term
        o_ref[...] = acc

    def land_spec(q):
        return pl.BlockSpec((None, tr, C), lambda i, m: (jnp.where(m[0] == q, (q + 1) % n, q), i, 0))

    grid_spec = pltpu.PrefetchScalarGridSpec(
        num_scalar_prefetch=1, grid=(R // tr,),
        in_specs=[pl.BlockSpec((None, tr, C), lambda i, m: (m[0], i, 0))] + [land_spec(q) for q in range(n)],
        out_specs=pl.BlockSpec((tr, C), lambda i, m: (i, 0)))
    return _pcall(body, grid_spec=grid_spec, out_shape=jax.ShapeDtypeStruct((R, C), F32), name=name)(qvec, part, *([land] * n))


def _exchange_small(gsmall):
    def body(s_ref, srecv_ref, send_sems, recv_sems, loc_sem):
        x, y, c, me = _mesh_pos()
        loc = pltpu.make_async_copy(s_ref, srecv_ref.at[me], loc_sem.at[0])
        loc.start()
        sends, recvs = [], []
        for k in range(1, N_DEV):
            to, pidx = _peer(x, y, c, k)
            sends.append(_rcopy(s_ref, srecv_ref.at[me], send_sems, recv_sems, k - 1, to))
            recvs.append(_rcopy(s_ref, srecv_ref.at[pidx], send_sems, recv_sems, k - 1, to))
        for cp in sends:
            cp.start()
        for cp in recvs:
            cp.wait_recv()
        for cp in sends:
            cp.wait_send()
        loc.wait()

    return _pcall(
        body, in_specs=[_ANY], out_specs=_ANY, out_shape=jax.ShapeDtypeStruct((N_DEV,) + gsmall.shape, gsmall.dtype),
        scratch_shapes=[pltpu.SemaphoreType.DMA((7,)), pltpu.SemaphoreType.DMA((7,)), pltpu.SemaphoreType.DMA((1,))],
        name="exchange_small",
    )(gsmall)


def _sum_slots(recv, name):
    n, R, C = recv.shape
    tr = _pick(R, _ROW_TILES)

    def body(r_ref, o_ref):
        acc = r_ref[0].astype(F32)
        for s in range(1, n):
            acc = acc + r_ref[s].astype(F32)
        o_ref[...] = acc

    return _pcall(body, grid=(R // tr,), in_specs=[pl.BlockSpec((n, tr, C), lambda i: (0, i, 0))],
                  out_specs=pl.BlockSpec((tr, C), lambda i: (i, 0)), out_shape=jax.ShapeDtypeStruct((R, C), F32), name=name)(recv)


def _adamw(w, g, m, v, name, order_after=None):
    shape = w.shape
    w2, g2, m2, v2 = (a.reshape(-1, shape[-1]) for a in (w, g, m, v))
    R, C = w2.shape
    tr = _pick(R, (256, 128, 64, 32, 16, 8)) if R > 256 else R
    n_in = 4 + int(order_after is not None)

    def body(*refs):
        w_ref, g_ref, m_ref, v_ref = refs[:4]
        d_ref, nm_ref, nv_ref = refs[n_in:]
        gg = g_ref[...]
        nm = ADAM_B1 * m_ref[...] + (1.0 - ADAM_B1) * gg
        nv = ADAM_B2 * v_ref[...] + (1.0 - ADAM_B2) * jnp.square(gg)
        m_hat = nm / (1.0 - ADAM_B1 ** ADAM_STEP)
        v_hat = nv / (1.0 - ADAM_B2 ** ADAM_STEP)
        d_ref[...] = -ADAM_LR * (m_hat / (jnp.sqrt(v_hat) + ADAM_EPS) + ADAM_WD * w_ref[...])
        nm_ref[...] = nm
        nv_ref[...] = nv

    spec = pl.BlockSpec((tr, C), lambda i: (i, 0))
    in_specs, args = [spec] * 4, [w2, g2, m2, v2]
    if order_after is not None:
        in_specs.append(pl.BlockSpec(memory_space=pl.ANY))
        args.append(order_after)
    d, nm, nv = _pcall(body, grid=(R // tr,), in_specs=in_specs, out_specs=[spec] * 3,
                       out_shape=[jax.ShapeDtypeStruct((R, C), F32)] * 3, name=name)(*args)
    return d.reshape(shape), nm.reshape(shape), nv.reshape(shape)


def _to_rows(name, w):
    w = w[0]
    if name in ("ffn1_w_in", "ffn2_w_in", "w_in"):
        return w.T
    if name in ("w_uq", "w_ukv", "ple_w_proj"):
        return w.T.reshape(-1, 1024)
    return w


def _from_rows(name, g, shape):
    if name in ("ffn1_w_in", "ffn2_w_in", "w_in"):
        return g.T.reshape(shape)
    if name in ("w_uq", "w_ukv", "ple_w_proj"):
        return g.reshape(-1, shape[1]).T.reshape(shape)
    return g.reshape(shape)


def _unpack(wall, group):
    out, off = {}, 0
    for name, r in group:
        out[name] = wall[:, off:off + r, :].reshape(N_DEV * r, 1024)
        off += _pad16(r)
    return out


def _w_in_internal(wt):
    return jnp.concatenate([wt[0:6144], wt[6720:8768], wt[6144:6720], jnp.zeros((P_W - W_IN_COLS, 1024), wt.dtype)], axis=0)


def _rope_tables(positions):
    pos = positions[0].astype(F32)

    def cs(half):
        inv = ROPE_BASE ** (-jnp.arange(half, dtype=F32) / half)
        ang = pos[:, None] * inv
        return jnp.cos(ang), jnp.sin(ang)

    c, s = cs(64)
    c2, s2 = cs(32)
    return (jnp.concatenate([c, c], axis=1), jnp.concatenate([-s, s], axis=1),
            jnp.concatenate([c2, c2, c2, c2], axis=1), jnp.concatenate([-s2, s2, -s2, s2], axis=1))


def _local_step(x, p, positions, target, W, ln_g, ln_b, gn_g, qg, kvg, early=None, rest_weights=None, start_token=None):
    T = x.shape[0]
    tabs = _rope_tables(positions)
    rc = _ret_consts()
    lg = [ln_g[i:i + 1] for i in range(4)]
    lb = [ln_b[i:i + 1] for i in range(4)]
    pb = p.astype(BF16)

    hs1, gu1, *xb = _ffn_in(x, W["ffn1_w_in"], "ffn1_in", order_after=start_token)
    xb = xb[0] if xb else x
    f1, h1, h1b = _mm(hs1, W["ffn1_w_out"], name="ffn1_out", tm=LN_TM, epilogue=_ln_epilogue(x, 0.5, lg[0], lb[0]))
    if rest_weights is not None:
        W = {**W, **rest_weights(h1b)}
    w_in_t = _w_in_internal(W["w_in"])
    wuq = W["w_uq"].reshape(1536, LORA).reshape(HEADS, 192, LORA)
    wuq = jnp.concatenate([wuq[:, :128].reshape(1024, LORA), wuq[:, 128:].reshape(512, LORA)], axis=0)
    wukv = W["w_ukv"].reshape(2048, LORA).reshape(HEADS, 2, 128, LORA).transpose(1, 0, 2, 3).reshape(2048, LORA)
    wp_t = W["ple_w_proj"].reshape(1024, D_PLE)
    proj = _mm(h1b, w_in_t, tb=True, out_dtype=BF16, name="mixer_in")
    rq, rk, qn, kvn, kpe = _mixer_prep_fwd(proj, tabs, qg, kvg, T)
    y, yr, states = _ret_fwd(rq, rk, proj, proj, gn_g, rc, T)
    y_ret = _mm(yr, W["w_ret_o"], out_dtype=BF16, name="ret_o")
    qnope, qpe = _mm(qn, wuq, tb=True, name="mla_uq", tm=LN_TM, tn=1536, epilogue=_q_assemble_epilogue(tabs))
    kv = _mm(kvn, wukv, tb=True, out_dtype=BF16, name="mla_ukv")
    o, lse = _attn_fwd(qnope, qpe, kv, kpe, T)
    y_mla, mix = _mm(o, W["w_mla_o"], name="mla_o", tm=LN_TM, epilogue=_mix_epilogue(proj, y_ret))
    mixed, h2, h2b = _mm(mix, W["w_out"], name="mixer_out", tm=LN_TM, epilogue=_ln_epilogue(h1, 1.0, lg[1], lb[1]))
    hs2, gu2 = _ffn_in(h2b, W["ffn2_w_in"], "ffn2_in")
    f2, h3, h3b = _mm(hs2, W["ffn2_w_out"], name="ffn2_out", tm=LN_TM, epilogue=_ln_epilogue(h2, 0.5, lg[2], lb[2]))
    pp = _mm(pb, wp_t, tb=True, name="ple_proj")

    G = {}
    dh3_a, dgl, dpp, dg3, db3, loss = _mm(h3b, W["ple_w_gate"], name="ple_gate", tm=LN_TM,
                                          epilogue=_head_epilogue(h3, pp, target, lg[3], lb[3]))
    G["ple_w_gate"] = _mm(h3b, dgl, ta=True, name="d_ple_gate")
    G["ple_w_proj"] = _mm(dpp, pb, ta=True, name="d_ple_proj")
    dh2_a, df2, dg2, db2 = _mm(dgl, W["ple_w_gate"], tb=True, add=dh3_a, name="dh3", tm=LN_TM,
                               epilogue=_ln_bwd_epilogue(h2, f2, 0.5, lg[2], lb[2]))
    G["ffn2_w_out"] = _mm(hs2, df2, ta=True, name="d_ffn2_out")
    da2 = _ffn_act_bwd(df2, W["ffn2_w_out"], gu2, "ffn2_act_bwd")
    G["ffn2_w_in"] = _mm(da2, h2b, ta=True, name="d_ffn2_in")
    dh1_a, dmixed, dg1, db1 = _mm(da2, W["ffn2_w_in"], add=dh2_a, name="dh2", tm=LN_TM,
                                  epilogue=_ln_bwd_epilogue(h1, mixed, 1.0, lg[1], lb[1]))
    G["w_out"] = _mm(mix, dmixed, ta=True, name="d_mixer_out")
    dp_gate, dy_ret, dy_mla = _mm(dmixed, W["w_out"], tb=True, name="dmix", tm=LN_TM,
                                  epilogue=_mix_bwd_epilogue(proj, y_ret, y_mla))
    G["w_mla_o"] = _mm(o, dy_mla, ta=True, name="d_mla_o")
    dob, delta = _mm(dy_mla, W["w_mla_o"], tb=True, name="do", tm=LN_TM, epilogue=_delta_epilogue(o))
    dqn_f, dqpe_f, dkn, dkpe_all, dv = _attn_bwd(qnope, qpe, kv, kpe, dob, lse, delta, T)
    dq_n, dq_r = _q_assemble_bwd(dqn_f, dqpe_f, tabs, T)
    g_uq = jnp.concatenate([_mm(dq_n, qn, ta=True, name="d_uq_nope"), _mm(dq_r, qn, ta=True, name="d_uq_rope")], axis=0)
    g_uq = jnp.concatenate([g_uq[:1024].reshape(HEADS, 128, LORA), g_uq[1024:].reshape(HEADS, 64, LORA)], axis=1)
    G["w_uq"] = g_uq.reshape(1536 * LORA // 1024, 1024)
    dqn = _mm(dq_r, wuq[1024:], add=_mm(dq_n, wuq[:1024], name="dqn_a"), name="dqn_b")
    g_ukv = jnp.stack([_mm(dkn, kvn, ta=True, name="d_ukv_k"), _mm(dv, kvn, ta=True, name="d_ukv_v")], axis=0)
    G["w_ukv"] = g_ukv.reshape(2, HEADS, 128, LORA).transpose(1, 0, 2, 3).reshape(2048 * LORA // 1024, 1024)
    dkvn = _mm(dv, wukv[1024:], add=_mm(dkn, wukv[:1024], name="dkvn_a"), name="dkvn_b")
    dp_mla, dqg, dkvg = _rms_bwd(proj, dqn, dkvn, dkpe_all, tabs, qg, kvg, T)
    G["w_ret_o"] = _mm(yr, dy_ret, ta=True, name="d_ret_o")
    dyr = _mm(dy_ret, W["w_ret_o"], tb=True, out_dtype=BF16, name="dyr")
    dp_ret, dgn = _ret_bwd(rq, rk, proj, y, proj, gn_g, states, dyr, rc, tabs, T)
    g_ret, g_gate, g_mla = (_mm(dp, h1b, ta=True, name="d_mixer_in_" + n)
                            for dp, n in ((dp_ret, "ret"), (dp_gate, "gate"), (dp_mla, "mla")))
    G["w_in"] = jnp.concatenate([g_ret, g_mla[:W_IN_COLS - P_CQ], g_gate], axis=0)
    lg0 = lg[0] if early is None else lg[0] + early(G)[0:1, 0:1]
    dh1 = _mm_pieces([(dp_ret, w_in_t[:P_GR]), (dp_gate, w_in_t[P_GR:P_CQ]), (dp_mla, w_in_t[P_CQ:])], dh1_a, "dh1")
    dx_a, df1, dg0, db0 = _ln_bwd_rows(x, f1, 0.5, lg0, lb[0], dh1, T, "ln0_bwd")
    G["ffn1_w_out"] = _mm(hs1, df1, ta=True, name="d_ffn1_out")
    da1 = _ffn_act_bwd(df1, W["ffn1_w_out"], gu1, "ffn1_act_bwd")
    G["ffn1_w_in"] = _mm(da1, xb, ta=True, name="d_ffn1_in")
    grad_x = _mm(da1, W["ffn1_w_in"], add=dx_a, name="grad_x")

    small = dict(ln_g=jnp.concatenate([dg0, dg1, dg2, dg3], axis=0), ln_b=jnp.concatenate([db0, db1, db2, db3], axis=0),
                 ret_gn_g=dgn, q_norm_g=dqg, kv_norm_g=dkvg)
    return loss, grad_x, G, small


def kernel(x, p, positions, ln_g, ln_b, ffn1_w_in, ffn1_w_out, w_in, ret_gn_g, w_ret_o, q_norm_g, kv_norm_g, w_uq, w_ukv, w_mla_o, w_out, ffn2_w_in, ffn2_w_out, ple_w_gate, ple_w_proj, loss_target, m_ln_g, m_ln_b, m_ffn1_w_in, m_ffn1_w_out, m_w_in, m_ret_gn_g, m_w_ret_o, m_q_norm_g, m_kv_norm_g, m_w_uq, m_w_ukv, m_w_mla_o, m_w_out, m_ffn2_w_in, m_ffn2_w_out, m_ple_w_gate, m_ple_w_proj, v_ln_g, v_ln_b, v_ffn1_w_in, v_ffn1_w_out, v_w_in, v_ret_gn_g, v_w_ret_o, v_q_norm_g, v_kv_norm_g, v_w_uq, v_w_ukv, v_w_mla_o, v_w_out, v_ffn2_w_in, v_ffn2_w_out, v_ple_w_gate, v_ple_w_proj):
    names = ("ln_g", "ln_b", "ffn1_w_in", "ffn1_w_out", "w_in", "ret_gn_g", "w_ret_o", "q_norm_g", "kv_norm_g", "w_uq", "w_ukv",
             "w_mla_o", "w_out", "ffn2_w_in", "ffn2_w_out", "ple_w_gate", "ple_w_proj")
    ws = dict(zip(names, (ln_g, ln_b, ffn1_w_in, ffn1_w_out, w_in, ret_gn_g, w_ret_o, q_norm_g, kv_norm_g, w_uq, w_ukv, w_mla_o,
                          w_out, ffn2_w_in, ffn2_w_out, ple_w_gate, ple_w_proj)))
    ms = dict(zip(names, (m_ln_g, m_ln_b, m_ffn1_w_in, m_ffn1_w_out, m_w_in, m_ret_gn_g, m_w_ret_o, m_q_norm_g, m_kv_norm_g, m_w_uq,
                          m_w_ukv, m_w_mla_o, m_w_out, m_ffn2_w_in, m_ffn2_w_out, m_ple_w_gate, m_ple_w_proj)))
    vs = dict(zip(names, (v_ln_g, v_ln_b, v_ffn1_w_in, v_ffn1_w_out, v_w_in, v_ret_gn_g, v_w_ret_o, v_q_norm_g, v_kv_norm_g, v_w_uq,
                          v_w_ukv, v_w_mla_o, v_w_out, v_ffn2_w_in, v_ffn2_w_out, v_ple_w_gate, v_ple_w_proj)))

    parts = []
    for name, r in PACK:
        rows = _to_rows(name, ws[name])
        if _pad16(r) != r:
            rows = jnp.concatenate([rows, jnp.zeros((_pad16(r) - r, 1024), F32)], axis=0)
        parts.append(rows)
    wsh_first = jnp.concatenate(parts[:len(PACK_LATE)], axis=0).astype(BF16)
    wsh_rest = jnp.concatenate(parts[len(PACK_LATE):], axis=0).astype(BF16)
    ssh = jnp.concatenate([ln_g[0], ln_b[0]], axis=0)
    wall_first, sall = _all_gather(wsh_first, ssh)
    *gather_handles, start_token = _gather_start(wsh_rest, wall_first)
    W = _unpack(wall_first, PACK_LATE)
    ln_full = sall.reshape(N_DEV, 2, 4, 128).transpose(1, 2, 0, 3).reshape(2, 4, 1024)

    def rest_weights(after):
        w_thru, land = _gather_wait(*gather_handles, after)
        return _unpack(_gather_finish(w_thru, land), PACK_EARLY)

    cvec = lax.axis_index("c").astype(jnp.int32).reshape(1)
    qvec = (2 * lax.axis_index("x") + lax.axis_index("y")).astype(jnp.int32).reshape(1)

    def chip_partials(G, group, tag):
        gparts = []
        for name, r in group:
            g = G[name].reshape(N_DEV, r, 1024)
            if _pad16(r) != r:
                g = jnp.concatenate([g, jnp.zeros((N_DEV, _pad16(r) - r, 1024), g.dtype)], axis=1)
            gparts.append(g)
        gfull = jnp.concatenate(gparts, axis=1).astype(BF16)
        g4 = gfull.reshape(4, 2, gfull.shape[1], 1024)
        return _sum_sibling(g4, _exchange_sibling(g4, "exchange_sibling_" + tag), cvec, "sum_sibling_" + tag)

    in_flight = []

    def early(G):
        *handles, token = _chips_start(chip_partials(G, PACK_EARLY, "early"), "chips_start_early")
        in_flight.append(handles)
        return token

    loss_p, grad_x, G, small = _local_step(x[0], p[0, 0], positions, loss_target[0], W, ln_full[0], ln_full[1],
                                           ret_gn_g, q_norm_g, kv_norm_g, early=early, rest_weights=rest_weights,
                                           start_token=start_token)

    part_e, land_e = _chips_wait(*in_flight[0], grad_x, "chips_wait_early")
    gsh_early = _sum_chips(part_e, land_e, qvec, "sum_grads_early")
    pad256 = lambda a: jnp.concatenate([a, jnp.zeros((1, 1024 - a.shape[1]), F32)], axis=1)
    gsmall = jnp.concatenate([small["ln_g"], small["ln_b"], small["ret_gn_g"].reshape(2, 1024), pad256(small["q_norm_g"]),
                              pad256(small["kv_norm_g"]), jnp.zeros((SMALL_ROWS - 12, 1024), F32)], axis=0)
    srecv = _exchange_small(gsmall)
    part_l = chip_partials(G, PACK_LATE, "late")
    *late_handles, late_token = _chips_start(part_l, "chips_start_late", order_after=srecv)
    ssum = _sum_slots(srecv, "sum_small_grads")

    def unpack_grads(group, gsh):
        out, off = {}, 0
        for name, r in group:
            out[name] = _from_rows(name, gsh[off:off + r], ws[name].shape)
            off += _pad16(r)
        return out

    grads = unpack_grads(PACK_EARLY, gsh_early)
    me = 4 * lax.axis_index("x") + 2 * lax.axis_index("y") + lax.axis_index("c")
    grads["ln_g"] = lax.dynamic_slice(ssum[0:4], (0, me * 128), (4, 128)).reshape(1, 4, 128)
    grads["ln_b"] = lax.dynamic_slice(ssum[4:8], (0, me * 128), (4, 128)).reshape(1, 4, 128)
    grads["ret_gn_g"] = ssum[8:10].reshape(1, 2048)
    grads["q_norm_g"] = ssum[10:11, :256]
    grads["kv_norm_g"] = ssum[11:12, :256]

    delta, new_m, new_v = {}, {}, {}
    late_names = [n for n, _ in PACK_LATE]
    last = late_token
    for name in names:
        if name not in late_names:
            delta[name], new_m[name], new_v[name] = _adamw(ws[name], grads[name], ms[name], vs[name], "adamw_" + name,
                                                           order_after=last)
            last = new_v[name]
    part_l, land_l = _chips_wait(*late_handles, last, "chips_wait_late")
    grads.update(unpack_grads(PACK_LATE, _sum_chips(part_l, land_l, qvec, "sum_grads_late")))
    for name in late_names:
        delta[name], new_m[name], new_v[name] = _adamw(ws[name], grads[name], ms[name], vs[name], "adamw_" + name)

    loss = lax.psum(loss_p[0, 0], ("x", "y", "c"))
    return (loss, grad_x[None], *[grads[n] for n in names], *[delta[n] for n in names],
            *[new_m[n] for n in names], *[new_v[n] for n in names])
```

```python
import math

import jax
import jax.numpy as jnp
from jax import lax
from jax.experimental import pallas as pl
from jax.experimental.pallas import tpu as pltpu

F32 = jnp.float32
BF16 = jnp.bfloat16

N_DEV = 8
D = 1024
D_FF = 2816
D_PLE = 256
CHUNK = 64
HEADS = 8
RET_DK = 128
RET_DV = 256
MLA_NOPE = 128
MLA_ROPE = 64
MLA_DV = 128
LORA = 256
ROPE_BASE = 10000.0
EPS = 1e-5
ALPHA = 2.0 ** 0.25
RET_SCALE = RET_DK ** -0.5
MLA_SCALE = (MLA_NOPE + MLA_ROPE) ** -0.5
NEG = -1e30

ADAM_LR = 0.001
ADAM_B1 = 0.9
ADAM_B2 = 0.999
ADAM_EPS = 1e-08
ADAM_WD = 0.01
ADAM_STEP = 10

P_RQ, P_RK, P_RV, P_RG, P_GR, P_GM, P_CQ, P_CKV, P_KPE, P_W = 0, 1024, 2048, 4096, 6144, 7168, 8192, 8448, 8704, 8960
W_IN_COLS = 8768
RET_L = 256
ATT_TF = 2048
ATT_TB = 1024
ATT_HP = 1
LOG2E = math.log2(math.e)
Q_PRESCALE = MLA_SCALE * LOG2E

PACK = (("ffn1_w_in", 704), ("ffn1_w_out", 352), ("w_in", 1096), ("w_ret_o", 256), ("w_uq", 48), ("w_ukv", 64),
        ("w_mla_o", 128), ("w_out", 128), ("ffn2_w_in", 704), ("ffn2_w_out", 352), ("ple_w_gate", 128), ("ple_w_proj", 32))


def _pad16(r):
    return -(-r // 16) * 16


PACK_LATE = PACK[:2]
PACK_EARLY = PACK[2:]
SMALL_ROWS = 16


def _pcall(body, **kw):
    return pl.pallas_call(body, **kw)


def _pick(dim, prefs):
    for p in prefs:
        if dim % p == 0:
            return p
    return dim


def _sigmoid(x):
    return 1.0 / (1.0 + jnp.exp(-x))


def _silu(x):
    return x * _sigmoid(x)


def _ln(r, g, b):
    mu = jnp.mean(r, axis=-1, keepdims=True)
    var = jnp.mean(jnp.square(r - mu), axis=-1, keepdims=True)
    return (r - mu) * lax.rsqrt(var + EPS) * g + b


def _rms(x, g):
    return x * lax.rsqrt(jnp.mean(jnp.square(x), axis=-1, keepdims=True) + EPS) * g


def _dot(a, b, ca, cb):
    return lax.dot_general(a, b, (((ca,), (cb,)), ((), ())), preferred_element_type=F32)


def _accum(ref, val, first=None):
    @pl.when(pl.program_id(0) == 0 if first is None else first)
    def _():
        ref[...] = jnp.zeros_like(ref)

    ref[...] += val


def _mm(a, b, *, ta=False, tb=False, add=None, out_dtype=None, name, tm=None, tn=None, tk=None, epilogue=None):
    parts = a.shape[0] if a.ndim == 3 else 1
    ar, ac = a.shape[-2], a.shape[-1]
    out_dtype = out_dtype or (BF16 if ta else F32)
    if ta:
        K, M = ar, ac * parts
    else:
        M, K = ar, ac * parts
    if tb:
        N, K2 = b.shape
    else:
        K2, N = b.shape
    assert K == K2, (a.shape, b.shape, ta, tb)
    big = (1024, 1408, 1280, 768, 512, 256, 128)
    tm = tm or _pick(ac if (ta and parts > 1) else M, big)
    tn = tn or (N if N <= 1024 else _pick(N, big))
    kdim = ac if (not ta and parts > 1) else K
    tk = tk or (kdim if kdim <= 2816 and parts == 1 else
                _pick(kdim, (2048, 1408, 1280, 1024, 512) if tm <= 1024 else (1024, 1408, 1280, 512)))
    nk = K // tk
    grid = (M // tm, N // tn, nk)
    if parts > 1 and ta:
        per = ac // tm
        a_spec = pl.BlockSpec((None, tk, tm), lambda i, j, k: (i // per, k, i % per))
    elif parts > 1:
        per = ac // tk
        a_spec = pl.BlockSpec((None, tm, tk), lambda i, j, k: (k // per, i, k % per))
    else:
        a_spec = pl.BlockSpec((tk, tm), lambda i, j, k: (k, i)) if ta else pl.BlockSpec((tm, tk), lambda i, j, k: (i, k))
    b_spec = pl.BlockSpec((tn, tk), lambda i, j, k: (j, k)) if tb else pl.BlockSpec((tk, tn), lambda i, j, k: (k, j))
    o_spec = pl.BlockSpec((tm, tn), lambda i, j, k: (i, j))
    ca, cb = (0 if ta else 1), (1 if tb else 0)
    has_add = add is not None
    n_in = 2 + int(has_add)
    if epilogue is not None:
        assert tn == N and not ta
        ep_fn, ep_rows, ep_whole, ep_outs, ep_accs = epilogue
        n_ep_in = len(ep_rows) + len(ep_whole)
        n_out = len(ep_outs) + len(ep_accs)
    else:
        n_ep_in, n_out = 0, 1

    def body(*refs):
        a_ref, b_ref = refs[0], refs[1]
        add_ref = refs[2] if has_add else None
        o_ref = refs[n_in + n_ep_in]
        first_row_tile = pl.program_id(0) == 0

        def finish(r):
            if has_add:
                r = r + add_ref[...].astype(F32)
            if epilogue is not None:
                ep_fn(r, refs[n_in:n_in + n_ep_in], refs[n_in + n_ep_in:n_in + n_ep_in + n_out], first_row_tile)
            else:
                o_ref[...] = r.astype(out_dtype)

        if nk == 1:
            finish(_dot(a_ref[...], b_ref[...], ca, cb))
            return
        acc_ref = refs[-1]
        k = pl.program_id(2)

        @pl.when(k == 0)
        def _():
            acc_ref[...] = jnp.zeros_like(acc_ref)

        acc_ref[...] += _dot(a_ref[...], b_ref[...], ca, cb)

        @pl.when(k == nk - 1)
        def _():
            finish(acc_ref[...])

    in_specs = [a_spec, b_spec] + ([o_spec] if has_add else [])
    args = (a, b) + ((add,) if has_add else ())
    out_specs, out_shape = o_spec, jax.ShapeDtypeStruct((M, N), out_dtype)
    if epilogue is not None:
        for row_in in ep_rows:
            arr, col_block, width = (tuple(row_in) + (N,))[:3] if isinstance(row_in, tuple) else (row_in, 0, N)
            in_specs.append(pl.BlockSpec((tm, width), lambda i, j, k, _c=col_block: (i, _c)))
            args += (arr,)
        in_specs += [pl.BlockSpec(w.shape, lambda i, j, k, _n=w.ndim: (0,) * _n) for w in ep_whole]
        args += tuple(ep_whole)
        outs = [o if isinstance(o, tuple) else (o, N) for o in ep_outs]
        out_specs = ([pl.BlockSpec((tm, w), lambda i, j, k: (i, 0)) for _, w in outs]
                     + [pl.BlockSpec((r, N), lambda i, j, k: (0, 0)) for r in ep_accs])
        out_shape = [jax.ShapeDtypeStruct((M, w), dt) for dt, w in outs] + [jax.ShapeDtypeStruct((r, N), F32) for r in ep_accs]
    return _pcall(
        body, grid=grid, in_specs=in_specs, out_specs=out_specs, out_shape=out_shape,
        scratch_shapes=[pltpu.VMEM((tm, tn), F32)] if nk > 1 else [], name=name,
        compiler_params=pltpu.CompilerParams(dimension_semantics=("arbitrary" if epilogue is not None else "parallel", "parallel", "arbitrary")),
    )(*args)


def _mm_pieces(pieces, add, name):
    M, N = add.shape
    tm = _pick(M, (1024, 512, 256, 128))
    spans, in_specs, args, start = [], [], [], 0
    for a, b in pieces:
        K = a.shape[1]
        assert b.shape == (K, N) and a.shape[0] == M, (a.shape, b.shape)
        tk = _pick(K, (1536, 1024, 768, 512, 256, 128))
        n = K // tk
        spans.append((start, n))
        block = lambda k, _s=start, _n=n: jnp.where(k < _s, _n - 1, jnp.minimum(k - _s, _n - 1))
        in_specs.append(pl.BlockSpec((tm, tk), lambda i, k, _s=start, _blk=block: (jnp.where(k < _s, jnp.maximum(i - 1, 0), i), _blk(k))))
        in_specs.append(pl.BlockSpec((tk, N), lambda i, k, _blk=block: (_blk(k), 0)))
        args += [a, b]
        start += n
    o_spec = pl.BlockSpec((tm, N), lambda i, k: (i, 0))

    def body(*refs):
        add_ref, o_ref = refs[-2], refs[-1]
        k = pl.program_id(1)

        @pl.when(k == 0)
        def _():
            o_ref[...] = add_ref[...]

        for p, (s, n) in enumerate(spans):
            @pl.when((k >= s) & (k < s + n))
            def _(p=p):
                o_ref[...] += _dot(refs[2 * p][...], refs[2 * p + 1][...], 1, 0)

    return _pcall(
        body, grid=(M // tm, start), in_specs=in_specs + [o_spec], out_specs=o_spec, out_shape=jax.ShapeDtypeStruct((M, N), F32),
        name=name, compiler_params=pltpu.CompilerParams(dimension_semantics=("parallel", "arbitrary")),
    )(*args, add)


def _ln_epilogue(res, c, g, b):
    def fn(r, ins, outs, first):
        res_ref, g_ref, b_ref = ins
        f_ref, h_ref, hb_ref = outs
        h = _ln(ALPHA * res_ref[...] + c * r, g_ref[...], b_ref[...])
        f_ref[...] = r
        h_ref[...] = h
        hb_ref[...] = h.astype(BF16)

    return (fn, [res], [g, b], (F32, F32, BF16), ())


def _ln_bwd_epilogue(res, f, c, g, b):
    def fn(r, ins, outs, first):
        res_ref, f_ref, g_ref, b_ref = ins
        dr_ref, df_ref, dg_ref, db_ref = outs
        pre = ALPHA * res_ref[...] + c * f_ref[...]
        xc = pre - jnp.mean(pre, axis=-1, keepdims=True)
        rstd = lax.rsqrt(jnp.mean(jnp.square(xc), axis=-1, keepdims=True) + EPS)
        xhat = xc * rstd
        dyg = r * g_ref[...]
        dpre = rstd * (dyg - jnp.mean(dyg, axis=-1, keepdims=True) - xhat * jnp.mean(dyg * xhat, axis=-1, keepdims=True))
        dr_ref[...] = ALPHA * dpre
        df_ref[...] = (c * dpre).astype(BF16)
        _accum(dg_ref, jnp.sum(r * xhat, axis=0, keepdims=True), first)
        _accum(db_ref, jnp.sum(r, axis=0, keepdims=True), first)

    return (fn, [res, f], [g, b], (F32, BF16), (1, 1))


def _rows(body, T, tm, ins, outs, name, accs=()):
    in_specs, args = [], []
    for arr, w, cb in ins:
        if w is None:
            in_specs.append(pl.BlockSpec(arr.shape, lambda i, _n=arr.ndim: (0,) * _n))
        else:
            in_specs.append(pl.BlockSpec((tm, w), lambda i, _cb=cb: (i, _cb)))
        args.append(arr)
    out_specs = [pl.BlockSpec((tm, w), lambda i: (i, 0)) for w, _ in outs]
    out_shape = [jax.ShapeDtypeStruct((T, w), dt) for w, dt in outs]
    for r, w in accs:
        out_specs.append(pl.BlockSpec((r, w), lambda i: (0, 0)))
        out_shape.append(jax.ShapeDtypeStruct((r, w), F32))
    return _pcall(
        body, grid=(T // tm,), in_specs=in_specs, out_specs=out_specs, out_shape=out_shape, name=name,
        compiler_params=pltpu.CompilerParams(dimension_semantics=("arbitrary",)),
    )(*args)


def _ln_bwd_rows(res, f, c, g, b, dh, T, name):
    fn = _ln_bwd_epilogue(res, f, c, g, b)[0]

    def body(r_ref, f_ref, g_ref, b_ref, d_ref, dr_ref, df_ref, dg_ref, db_ref):
        fn(d_ref[...], (r_ref, f_ref, g_ref, b_ref), (dr_ref, df_ref, dg_ref, db_ref), pl.program_id(0) == 0)

    return _rows(body, T, 256, [(res, D, 0), (f, D, 0), (g, None, None), (b, None, None), (dh, D, 0)],
                 [(D, F32), (D, BF16)], name, accs=[(1, D), (1, D)])


FFN_TN = 1408
FFN_TM = 512
LN_TM = 512


def _ffn_in(x, wt, name, order_after=None):
    T = x.shape[0]
    tm, tn = min(FFN_TM, T), FFN_TN
    nj = D_FF // tn
    emit_xb = x.dtype != BF16
    n_in = 3 + int(order_after is not None)

    def body(*refs):
        x_ref, wg_ref, wu_ref = refs[:3]
        hs_ref, gu_ref = refs[n_in], refs[n_in + 1]
        xv = x_ref[...].astype(BF16)
        g = _dot(xv, wg_ref[...], 1, 1)
        u = _dot(xv, wu_ref[...], 1, 1)
        hs_ref[...] = (_silu(g) * u).astype(BF16)
        gu_ref[0] = g.astype(BF16)
        gu_ref[1] = u.astype(BF16)
        if emit_xb:
            refs[n_in + 2][...] = xv

    in_specs = [pl.BlockSpec((tm, D), lambda i, j: (i, 0)), pl.BlockSpec((tn, D), lambda i, j: (j, 0)),
                pl.BlockSpec((tn, D), lambda i, j: (j + nj, 0))]
    args = [x, wt, wt]
    if order_after is not None:
        in_specs.append(pl.BlockSpec(order_after.shape, lambda i, j: (0, 0)))
        args.append(order_after)
    out_specs = [pl.BlockSpec((tm, tn), lambda i, j: (i, j)), pl.BlockSpec((2, tm, tn), lambda i, j: (0, i, j))]
    out_shape = [jax.ShapeDtypeStruct((T, D_FF), BF16), jax.ShapeDtypeStruct((2, T, D_FF), BF16)]
    if emit_xb:
        out_specs.append(pl.BlockSpec((tm, D), lambda i, j: (i, 0)))
        out_shape.append(jax.ShapeDtypeStruct((T, D), BF16))
    return _pcall(
        body, grid=(T // tm, nj), in_specs=in_specs, out_specs=out_specs, out_shape=out_shape, name=name,
        compiler_params=pltpu.CompilerParams(dimension_semantics=("parallel", "arbitrary")),
    )(*args)


def _ffn_act_bwd(df, wo, gu, name):
    T = df.shape[0]
    tm, tn = min(FFN_TM, T), FFN_TN

    def body(d_ref, w_ref, gu_ref, o_ref):
        dhs = _dot(d_ref[...], w_ref[...], 1, 1)
        g, u = gu_ref[0].astype(F32), gu_ref[1].astype(F32)
        sig = _sigmoid(g)
        act = g * sig
        o_ref[0] = (dhs * u * (sig + act * (1.0 - sig))).astype(BF16)
        o_ref[1] = (dhs * act).astype(BF16)

    return _pcall(
        body, grid=(T // tm, D_FF // tn),
        in_specs=[pl.BlockSpec((tm, D), lambda i, j: (i, 0)), pl.BlockSpec((tn, D), lambda i, j: (j, 0)),
                  pl.BlockSpec((2, tm, tn), lambda i, j: (0, i, j))],
        out_specs=pl.BlockSpec((2, tm, tn), lambda i, j: (0, i, j)),
        out_shape=jax.ShapeDtypeStruct((2, T, D_FF), BF16), name=name,
        compiler_params=pltpu.CompilerParams(dimension_semantics=("parallel", "parallel")),
    )(df, wo, gu)


def _rope128(t, cos, sin_s):
    return t * cos + pltpu.roll(t, 64, 1) * sin_s


def _rope128_t(g, cos, sin_s):
    return g * cos - pltpu.roll(g, 64, 1) * sin_s


def _partner32(t):
    lane = lax.broadcasted_iota(jnp.int32, t.shape, 1)
    return jnp.where((lane & 32) == 0, pltpu.roll(t, 96, 1), pltpu.roll(t, 32, 1))


def _rope64(t, cos, sin_s):
    return t * cos + _partner32(t) * sin_s


def _rope64_t(g, cos, sin_s):
    return g * cos - _partner32(g) * sin_s


def _mixer_prep_fwd(proj, tabs, qg, kvg, T):
    cos128, sin128, cos64, sin64 = tabs

    def body(rq_ref, rk_ref, cq_ref, ckv_ref, kpe_ref, c1_ref, s1_ref, c2_ref, s2_ref, qg_ref, kvg_ref,
             oq_ref, ok_ref, oqn_ref, okvn_ref, okpe_ref):
        c1, s1 = c1_ref[...], s1_ref[...]
        for h in range(HEADS):
            sl = slice(h * RET_DK, (h + 1) * RET_DK)
            oq_ref[:, sl] = _rope128(rq_ref[:, sl].astype(F32), c1, s1).astype(BF16)
            ok_ref[:, sl] = (_rope128(rk_ref[:, sl].astype(F32), c1, s1) * RET_SCALE).astype(BF16)
        oqn_ref[...] = _rms(cq_ref[...].astype(F32), qg_ref[...]).astype(BF16)
        okvn_ref[...] = _rms(ckv_ref[...].astype(F32), kvg_ref[...]).astype(BF16)
        okpe_ref[...] = _rope64(kpe_ref[...].astype(F32), c2_ref[...], s2_ref[...]).astype(BF16)

    ins = [(proj, 1024, 0), (proj, 1024, 1), (proj, 256, P_CQ // 256), (proj, 256, P_CKV // 256),
           (proj, 128, P_KPE // 128), (cos128, 128, 0), (sin128, 128, 0), (cos64, 128, 0), (sin64, 128, 0),
           (qg, None, None), (kvg, None, None)]
    outs = [(1024, BF16), (1024, BF16), (LORA, BF16), (LORA, BF16), (128, BF16)]
    return _rows(body, T, 256, ins, outs, "mixer_prep_fwd")


def _rms_bwd(proj, dqn, dkvn, dkpe_all, tabs, qg, kvg, T):
    _, _, cos64, sin64 = tabs

    def body(cq_ref, ckv_ref, dq_ref, dkv_ref, dk_ref, c2_ref, s2_ref, qg_ref, kvg_ref, o_ref, dqg_ref, dkvg_ref):
        _, vjp = jax.vjp(_rms, cq_ref[...].astype(F32), qg_ref[...])
        dx, dg = vjp(dq_ref[...])
        o_ref[:, 0:LORA] = dx.astype(BF16)
        _accum(dqg_ref, dg)
        _, vjp = jax.vjp(_rms, ckv_ref[...].astype(F32), kvg_ref[...])
        dx, dg = vjp(dkv_ref[...])
        o_ref[:, P_CKV - P_CQ:P_CKV - P_CQ + LORA] = dx.astype(BF16)
        _accum(dkvg_ref, dg)
        g = dk_ref[:, 0:128]
        for h in range(1, HEADS):
            g = g + dk_ref[:, h * 128:(h + 1) * 128]
        lane = lax.broadcasted_iota(jnp.int32, g.shape, 1)
        g = jnp.where(lane < MLA_ROPE, g, 0.0)
        o_ref[:, P_KPE - P_CQ:P_KPE - P_CQ + 128] = _rope64_t(g, c2_ref[...], s2_ref[...]).astype(BF16)
        o_ref[:, P_KPE - P_CQ + 128:] = jnp.zeros((g.shape[0], P_W - P_KPE - 128), BF16)

    ins = [(proj, 256, P_CQ // 256), (proj, 256, P_CKV // 256), (dqn, LORA, 0), (dkvn, LORA, 0), (dkpe_all, 1024, 0),
           (cos64, 128, 0), (sin64, 128, 0), (qg, None, None), (kvg, None, None)]
    return _rows(body, T, 256, ins, [(P_W - P_CQ, BF16)], "rms_bwd", accs=[(1, LORA), (1, LORA)])


def _gn_gate_bwd(y, rg, g, d):
    xc = y - jnp.mean(y, axis=-1, keepdims=True)
    rstd = lax.rsqrt(jnp.mean(jnp.square(xc), axis=-1, keepdims=True) + EPS)
    xhat = xc * rstd
    sig = _sigmoid(rg)
    act = rg * sig
    dn = d * act
    drg = d * (xhat * g) * (sig + act * (1.0 - sig))
    dxh = dn * g
    dy = rstd * (dxh - jnp.mean(dxh, axis=-1, keepdims=True) - xhat * jnp.mean(dxh * xhat, axis=-1, keepdims=True))
    return dy, drg, jnp.sum(dn * xhat, axis=0, keepdims=True)


def _gn_gate(y, rg, g):
    mu = jnp.mean(y, axis=-1, keepdims=True)
    var = jnp.mean(jnp.square(y - mu), axis=-1, keepdims=True)
    return _silu(rg) * ((y - mu) * lax.rsqrt(var + EPS) * g)


def _q_assemble_epilogue(tabs):
    _, _, cos64, sin64 = tabs

    def fn(q, ins, outs, first):
        c_ref, s_ref = ins
        on_ref, op_ref = outs
        on_ref[...] = (q[:, :1024] * Q_PRESCALE).astype(BF16)
        c, s = c_ref[...], s_ref[...]
        lane = lax.broadcasted_iota(jnp.int32, c.shape, 1)
        for j in range(HEADS // 2):
            r = _rope64(q[:, 1024 + 128 * j:1024 + 128 * (j + 1)], c, s) * Q_PRESCALE
            op_ref[:, 256 * j:256 * j + 128] = jnp.where(lane < 64, r, 0.0).astype(BF16)
            op_ref[:, 256 * j + 128:256 * j + 256] = jnp.where(lane < 64, pltpu.roll(r, 64, 1), 0.0).astype(BF16)

    return (fn, [(cos64, 0, 128), (sin64, 0, 128)], [], ((BF16, 1024), (BF16, 1024)), ())


def _q_assemble_bwd(dqn, dqpe, tabs, T):
    _, _, cos64, sin64 = tabs

    def body(dn_ref, dp_ref, c_ref, s_ref, on_ref, op_ref):
        on_ref[...] = dn_ref[...].astype(BF16)
        c, s = c_ref[...], s_ref[...]
        lane = lax.broadcasted_iota(jnp.int32, c.shape, 1)
        for j in range(HEADS // 2):
            g = jnp.where(lane < 64, dp_ref[:, 256 * j:256 * j + 128], pltpu.roll(dp_ref[:, 256 * j + 128:256 * j + 256], 64, 1))
            op_ref[:, 128 * j:128 * (j + 1)] = _rope64_t(g, c, s).astype(BF16)

    return _rows(body, T, 256, [(dqn, 1024, 0), (dqpe, 1024, 0), (cos64, 128, 0), (sin64, 128, 0)],
                 [(1024, BF16), (512, BF16)], "q_assemble_bwd")


def _mix_fn(gr, gm, yr, ym):
    return _sigmoid(gr) * yr + _sigmoid(gm) * ym


def _mix_epilogue(proj, y_ret):
    def fn(r, ins, outs, first):
        gr_ref, gm_ref, yr_ref = ins
        ym_ref, mix_ref = outs
        ym_ref[...] = r.astype(BF16)
        mix_ref[...] = _mix_fn(gr_ref[...].astype(F32), gm_ref[...].astype(F32), yr_ref[...].astype(F32), r).astype(BF16)

    return (fn, [(proj, P_GR // D), (proj, P_GM // D), y_ret], [], (BF16, BF16), ())


def _mix_bwd_epilogue(proj, y_ret, y_mla):
    def fn(r, ins, outs, first):
        gr_ref, gm_ref, yr_ref, ym_ref = ins
        dgate_ref, dyr_ref, dym_ref = outs
        sr, sm = _sigmoid(gr_ref[...].astype(F32)), _sigmoid(gm_ref[...].astype(F32))
        dgate_ref[:, 0:D] = (r * yr_ref[...].astype(F32) * (sr * (1.0 - sr))).astype(BF16)
        dgate_ref[:, D:2 * D] = (r * ym_ref[...].astype(F32) * (sm * (1.0 - sm))).astype(BF16)
        dyr_ref[...] = (r * sr).astype(BF16)
        dym_ref[...] = (r * sm).astype(BF16)

    return (fn, [(proj, P_GR // D), (proj, P_GM // D), y_ret, y_mla], [], ((BF16, 2 * D), BF16, BF16), ())


def _head_epilogue(h3, pp, tgt, g, b):
    def fn(r, ins, outs, first):
        h_ref, pp_ref, t_ref, g_ref, b_ref = ins
        dh_ref, dgl_ref, dpp_ref, dg_ref, db_ref, loss_ref = outs

        sg, pp, gain = _sigmoid(r), pp_ref[...], g_ref[...]
        pre = ALPHA * h_ref[...] + sg * pp
        xc = pre - jnp.mean(pre, axis=-1, keepdims=True)
        rstd = lax.rsqrt(jnp.mean(jnp.square(xc), axis=-1, keepdims=True) + EPS)
        xhat = xc * rstd
        err = xhat * gain + b_ref[...] - t_ref[...]
        dy = err * (1.0 / D)
        dyg = dy * gain
        dpre = rstd * (dyg - jnp.mean(dyg, axis=-1, keepdims=True) - xhat * jnp.mean(dyg * xhat, axis=-1, keepdims=True))
        dh_ref[...] = ALPHA * dpre
        dgl_ref[...] = (dpre * pp * (sg * (1.0 - sg))).astype(BF16)
        dpp_ref[...] = (dpre * sg).astype(BF16)
        _accum(dg_ref, jnp.sum(dy * xhat, axis=0, keepdims=True), first)
        _accum(db_ref, jnp.sum(dy, axis=0, keepdims=True), first)
        part = 0.5 * jnp.sum(jnp.mean(jnp.square(err), axis=-1, keepdims=True), axis=0, keepdims=True)
        _accum(loss_ref, jnp.broadcast_to(part, loss_ref.shape), first)

    return (fn, [h3, pp, tgt], [g, b], (F32, BF16, BF16), (1, 1, 8))


def _delta_epilogue(o):
    def fn(r, ins, outs, first):
        (o_ref,) = ins
        db_ref, dl_ref = outs
        db_ref[...] = r.astype(BF16)
        for h in range(HEADS):
            sl = slice(h * MLA_DV, (h + 1) * MLA_DV)
            dl = jnp.sum(r[:, sl] * o_ref[:, sl].astype(F32), axis=-1, keepdims=True)
            dl_ref[:, sl] = jnp.broadcast_to(dl, (r.shape[0], MLA_DV))

    return (fn, [o], [], (BF16, F32), ())


def _ret_consts():
    L = RET_L
    lg = jnp.log(1.0 - 2.0 ** (-5.0 - jnp.arange(HEADS, dtype=F32)))[:, None, None]
    idx = jnp.arange(L, dtype=F32)
    ch = jnp.arange(L) // CHUNK
    dist = idx[:, None] - idx[None, :]
    same = (ch[:, None] == ch[None, :])[None]
    earlier = (ch[None, :] < ch[:, None])[None]
    dm = jnp.where(same, jnp.exp(lg * jnp.abs(dist)[None]), jnp.where(earlier, jnp.exp(lg * dist[None]), 0.0))
    xi = jnp.broadcast_to(jnp.exp(lg * (idx + 1.0)[None, :, None]), (HEADS, L, 128))
    zeta = jnp.broadcast_to(jnp.exp(lg * (L - 1.0 - idx)[None, :, None]), (HEADS, L, 128))
    gl = jnp.broadcast_to(jnp.exp(lg * float(L)), (HEADS, 8, 128))
    return dm.astype(F32), xi.astype(F32), zeta.astype(F32), gl.astype(F32)


def _whole(arr):
    return pl.BlockSpec(arr.shape, lambda n, _nd=arr.ndim: (0,) * _nd)


def _ret_fwd(q, k, v, proj, gn_g, consts, T):
    dm, xi, zeta, gl = consts
    L = RET_L
    n_sc = T // L

    def body(q_ref, k_ref, v_ref, rg_ref, g_ref, dm_ref, xi_ref, ze_ref, gl_ref, y_ref, yr_ref, s_ref, st_ref):
        @pl.when(pl.program_id(0) == 0)
        def _():
            st_ref[...] = jnp.zeros_like(st_ref)

        for h in range(HEADS):
            ks, vs = slice(h * RET_DK, (h + 1) * RET_DK), slice(h * RET_DV, (h + 1) * RET_DV)
            qq, kk, vv = q_ref[:, ks], k_ref[:, ks], v_ref[:, vs]
            st = st_ref[h]
            s_ref[h, 0] = st
            p = (_dot(qq, kk, 1, 1) * dm_ref[h]).astype(BF16)
            cross = _dot(qq, st.astype(BF16), 1, 0)
            xi_c = jnp.concatenate([xi_ref[h], xi_ref[h]], axis=1)
            y = _dot(p, vv, 1, 0) + cross * xi_c
            y_ref[:, vs] = y
            yr_ref[:, vs] = _gn_gate(y, rg_ref[:, vs].astype(F32), g_ref[:, vs]).astype(BF16)
            kz = (kk.astype(F32) * ze_ref[h]).astype(BF16)
            gl2 = jnp.concatenate([gl_ref[h, 0:1, :], gl_ref[h, 0:1, :]], axis=1)
            st_ref[h] = st * gl2 + _dot(kz, vv, 0, 0)

    return _pcall(
        body, grid=(n_sc,),
        in_specs=[pl.BlockSpec((L, 1024), lambda n: (n, 0)), pl.BlockSpec((L, 1024), lambda n: (n, 0)),
                  pl.BlockSpec((L, 2048), lambda n: (n, P_RV // 2048)), pl.BlockSpec((L, 2048), lambda n: (n, P_RG // 2048)),
                  _whole(gn_g), _whole(dm), _whole(xi), _whole(zeta), _whole(gl)],
        out_specs=[pl.BlockSpec((L, 2048), lambda n: (n, 0)), pl.BlockSpec((L, 2048), lambda n: (n, 0)),
                   pl.BlockSpec((HEADS, 1, 128, 256), lambda n: (0, n, 0, 0))],
        out_shape=[jax.ShapeDtypeStruct((T, HEADS * RET_DV), F32), jax.ShapeDtypeStruct((T, HEADS * RET_DV), BF16),
                   jax.ShapeDtypeStruct((HEADS, n_sc, 128, 256), F32)],
        scratch_shapes=[pltpu.VMEM((HEADS, 128, 256), F32)], name="ret_fwd",
        compiler_params=pltpu.CompilerParams(dimension_semantics=("arbitrary",)),
    )(q, k, v, proj, gn_g, dm, xi, zeta, gl)


def _ret_bwd(q, k, v, y, proj, gn_g, states, dyr, consts, tabs, T):
    dm, xi, zeta, gl = consts
    cos128, sin128, _, _ = tabs
    L = RET_L
    n_sc = T // L

    def body(q_ref, k_ref, v_ref, y_ref, rg_ref, g_ref, d_ref, s_ref, dm_ref, xi_ref, ze_ref, gl_ref, c_ref, sn_ref,
             dp_ref, dg_ref, gs_ref):
        @pl.when(pl.program_id(0) == 0)
        def _():
            gs_ref[...] = jnp.zeros_like(gs_ref)

        c, sn = c_ref[...], sn_ref[...]
        dgs = []
        for h in range(HEADS):
            ks, vs = slice(h * RET_DK, (h + 1) * RET_DK), slice(h * RET_DV, (h + 1) * RET_DV)
            dy, drg, dg = _gn_gate_bwd(y_ref[:, vs], rg_ref[:, vs].astype(F32), g_ref[:, vs], d_ref[:, vs].astype(F32))
            dp_ref[:, P_RG + h * RET_DV:P_RG + (h + 1) * RET_DV] = drg.astype(BF16)
            dgs.append(dg)
            qq, kk, vv, dyy = q_ref[:, ks], k_ref[:, ks], v_ref[:, vs], dy.astype(BF16)
            dmm = dm_ref[h]
            gb = gs_ref[h].astype(BF16)
            sb = s_ref[h, 0].astype(BF16)
            xi_c = jnp.concatenate([xi_ref[h], xi_ref[h]], axis=1)
            pb = (_dot(qq, kk, 1, 1) * dmm).astype(BF16)
            kz = (kk.astype(F32) * ze_ref[h]).astype(BF16)
            dp_ref[:, P_RV + h * RET_DV:P_RV + (h + 1) * RET_DV] = (_dot(pb, dyy, 0, 0) + _dot(kz, gb, 1, 0)).astype(BF16)
            da = (_dot(dyy, vv, 1, 1) * dmm).astype(BF16)
            dyx = (dyy.astype(F32) * xi_c).astype(BF16)
            dq = _dot(da, kk, 1, 0) + _dot(dyx, sb, 1, 1)
            dk = _dot(da, qq, 0, 0) + _dot(vv, gb, 1, 1) * ze_ref[h]
            dp_ref[:, P_RQ + h * RET_DK:P_RQ + (h + 1) * RET_DK] = _rope128_t(dq, c, sn).astype(BF16)
            dp_ref[:, P_RK + h * RET_DK:P_RK + (h + 1) * RET_DK] = (_rope128_t(dk, c, sn) * RET_SCALE).astype(BF16)
            gl2 = jnp.concatenate([gl_ref[h, 0:1, :], gl_ref[h, 0:1, :]], axis=1)
            gs_ref[h] = gs_ref[h] * gl2 + _dot(qq, dyx, 0, 0)
        _accum(dg_ref, jnp.concatenate(dgs, axis=1))

    rev = lambda n: n_sc - 1 - n
    return _pcall(
        body, grid=(n_sc,),
        in_specs=[pl.BlockSpec((L, 1024), lambda n: (rev(n), 0)), pl.BlockSpec((L, 1024), lambda n: (rev(n), 0)),
                  pl.BlockSpec((L, 2048), lambda n: (rev(n), P_RV // 2048)), pl.BlockSpec((L, 2048), lambda n: (rev(n), 0)),
                  pl.BlockSpec((L, 2048), lambda n: (rev(n), P_RG // 2048)), _whole(gn_g),
                  pl.BlockSpec((L, 2048), lambda n: (rev(n), 0)),
                  pl.BlockSpec((HEADS, 1, 128, 256), lambda n: (0, rev(n), 0, 0)),
                  _whole(dm), _whole(xi), _whole(zeta), _whole(gl),
                  pl.BlockSpec((L, 128), lambda n: (rev(n), 0)), pl.BlockSpec((L, 128), lambda n: (rev(n), 0))],
        out_specs=[pl.BlockSpec((L, P_GR), lambda n: (rev(n), 0)), pl.BlockSpec((1, 2048), lambda n: (0, 0))],
        out_shape=[jax.ShapeDtypeStruct((T, P_GR), BF16), jax.ShapeDtypeStruct((1, 2048), F32)],
        scratch_shapes=[pltpu.VMEM((HEADS, 128, 256), F32)], name="ret_bwd",
        compiler_params=pltpu.CompilerParams(dimension_semantics=("arbitrary",)),
    )(q, k, v, y, proj, gn_g, dyr, states, dm, xi, zeta, gl, cos128, sin128)


def _diag_mask(nrows, ncols, row0):
    row = row0 + lax.broadcasted_iota(jnp.int32, (nrows, ncols), 0)
    col = lax.broadcasted_iota(jnp.int32, (nrows, ncols), 1)
    return lax.shift_right_logical(col, 6) <= lax.shift_right_logical(row, 6)


def _diag_spans(t):
    h = t // 2
    return ((0, h, h), (h, h, t)) if h % 128 == 0 else ((0, t, t),)


def _tri_steps(nb, by_key):
    if by_key:
        pairs = [(i, j) for j in range(nb) for i in range(j, nb)]
    else:
        pairs = [(i, j) for i in range(nb) for j in range(i + 1)]
    return jnp.array([a for a, _ in pairs], jnp.int32), jnp.array([b for _, b in pairs], jnp.int32)


def _attn_fwd(qn, qpe, kv, kpe, T):
    t = min(ATT_TF, T)
    nb = T // t
    ii, jj = _tri_steps(nb, by_key=False)

    hp = ATT_HP
    w = 128 * hp

    def body(ii_ref, jj_ref, qn_ref, qp_ref, kn_ref, kp_ref, v_ref, o_ref, lse_ref, m_sc, l_sc, acc_sc):
        st = pl.program_id(1)
        i, j = ii_ref[st], jj_ref[st]

        @pl.when(j == 0)
        def _():
            m_sc[...] = jnp.full_like(m_sc, NEG)
            l_sc[...] = jnp.zeros_like(l_sc)
            acc_sc[...] = jnp.zeros_like(acc_sc)

        def update(diag):
            kp = kp_ref[...]
            spans = _diag_spans(t) if diag else ((0, t, t),)
            for hh in range(hp):
                sl = slice(128 * hh, 128 * (hh + 1))
                for row0, nr, nkeys in spans:
                    rows = slice(row0, row0 + nr)
                    q = jnp.concatenate([qn_ref[rows, sl], qp_ref[rows, sl]], axis=1)
                    k = jnp.concatenate([kn_ref[:nkeys, sl], kp[:nkeys]], axis=1)
                    s = _dot(q, k, 1, 1)
                    if diag:
                        s = jnp.where(_diag_mask(nr, nkeys, row0), s, NEG)
                    m_prev = m_sc[rows, sl]
                    m_new = jnp.maximum(m_prev, jnp.max(s, axis=1, keepdims=True))
                    a = jnp.exp2(m_prev - m_new)
                    p = jnp.exp2(s - m_new[:, 0:1])
                    l_sc[rows, sl] = a * l_sc[rows, sl] + jnp.sum(p, axis=1, keepdims=True)
                    acc_sc[rows, sl] = a * acc_sc[rows, sl] + _dot(p.astype(BF16), v_ref[:nkeys, sl], 1, 0)
                    m_sc[rows, sl] = m_new

        @pl.when(j < i)
        def _():
            update(False)

        @pl.when(j == i)
        def _():
            update(True)
            o_ref[...] = (acc_sc[...] / l_sc[...]).astype(BF16)
            lse_ref[...] = m_sc[...] + jnp.log2(l_sc[...])

    qs = pl.BlockSpec((t, w), lambda h, s, ii, jj: (ii[s], h))
    grid_spec = pltpu.PrefetchScalarGridSpec(
        num_scalar_prefetch=2, grid=(HEADS // hp, int(ii.shape[0])),
        in_specs=[qs, qs, pl.BlockSpec((t, w), lambda h, s, ii, jj: (jj[s], h)), pl.BlockSpec((t, 128), lambda h, s, ii, jj: (jj[s], 0)),
                  pl.BlockSpec((t, w), lambda h, s, ii, jj: (jj[s], HEADS // hp + h))],
        out_specs=[qs, qs],
        scratch_shapes=[pltpu.VMEM((t, w), F32), pltpu.VMEM((t, w), F32), pltpu.VMEM((t, w), F32)])
    return _pcall(
        body, grid_spec=grid_spec, out_shape=[jax.ShapeDtypeStruct((T, D), BF16), jax.ShapeDtypeStruct((T, D), F32)], name="attn_fwd",
        compiler_params=pltpu.CompilerParams(dimension_semantics=("arbitrary", "arbitrary")),
    )(ii, jj, qn, qpe, kv, kpe, kv)


def _attn_bwd(qn, qpe, kv, kpe, do, lse, delta, T):
    t = min(ATT_TB, T)
    nb = T // t
    ii, jj = _tri_steps(nb, by_key=True)

    def body(ii_ref, jj_ref, qn_ref, qp_ref, kn_ref, kp_ref, v_ref, do_ref, lse_ref, dl_ref,
             dqn_ref, dqp_ref, dkn_ref, dkp_ref, dv_ref, dk_sc, dv_sc):
        st = pl.program_id(1)
        i, j = ii_ref[st], jj_ref[st]

        @pl.when(st == 0)
        def _():
            dqn_ref[...] = jnp.zeros_like(dqn_ref)
            dqp_ref[...] = jnp.zeros_like(dqp_ref)

        @pl.when(i == j)
        def _():
            dk_sc[...] = jnp.zeros_like(dk_sc)
            dv_sc[...] = jnp.zeros_like(dv_sc)

        def update(diag):
            for row0, nr, nkeys in (_diag_spans(t) if diag else ((0, t, t),)):
                rows = slice(row0, row0 + nr)
                q = jnp.concatenate([qn_ref[rows, :], qp_ref[rows, :]], axis=1)
                k = jnp.concatenate([kn_ref[:nkeys, :], kp_ref[:nkeys, :]], axis=1)
                dob = do_ref[rows, :]
                s = _dot(q, k, 1, 1)
                if diag:
                    s = jnp.where(_diag_mask(nr, nkeys, row0), s, NEG)
                p = jnp.exp2(s - lse_ref[rows, 0:1])
                dv_sc[:nkeys, :] += _dot(p.astype(BF16), dob, 0, 0)
                dp = _dot(dob, v_ref[:nkeys, :], 1, 1)
                ds = (p * (dp - dl_ref[rows, 0:1])).astype(BF16)
                dk_sc[:nkeys, :] += _dot(ds, q, 0, 0)
                dq = _dot(ds, k, 1, 0) * MLA_SCALE
                grows = pl.ds(pl.multiple_of(i * t + row0, nr), nr)
                dqn_ref[grows, :] += dq[:, :128]
                dqp_ref[grows, :] += dq[:, 128:]

        @pl.when(i > j)
        def _():
            update(False)

        @pl.when(i == j)
        def _():
            update(True)

        @pl.when(i == nb - 1)
        def _():
            dkn_ref[...] = (dk_sc[:, :128] * (1.0 / LOG2E)).astype(BF16)
            dkp_ref[...] = dk_sc[:, 128:] * (1.0 / LOG2E)
            dv_ref[...] = dv_sc[...].astype(BF16)

    qs = pl.BlockSpec((t, 128), lambda h, s, ii, jj: (ii[s], h))
    ks = pl.BlockSpec((t, 128), lambda h, s, ii, jj: (jj[s], h))
    hs = pl.BlockSpec((T, 128), lambda h, s, ii, jj: (0, h))
    grid_spec = pltpu.PrefetchScalarGridSpec(
        num_scalar_prefetch=2, grid=(HEADS, int(ii.shape[0])),
        in_specs=[qs, qs, ks, pl.BlockSpec((t, 128), lambda h, s, ii, jj: (jj[s], 0)),
                  pl.BlockSpec((t, 128), lambda h, s, ii, jj: (jj[s], HEADS + h)), qs, qs, qs],
        out_specs=[hs, hs, ks, ks, ks],
        scratch_shapes=[pltpu.VMEM((t, 256), F32), pltpu.VMEM((t, 128), F32)])
    return _pcall(
        body, grid_spec=grid_spec,
        out_shape=[jax.ShapeDtypeStruct((T, D), F32), jax.ShapeDtypeStruct((T, D), F32), jax.ShapeDtypeStruct((T, D), BF16),
                   jax.ShapeDtypeStruct((T, D), F32), jax.ShapeDtypeStruct((T, D), BF16)], name="attn_bwd",
        compiler_params=pltpu.CompilerParams(dimension_semantics=("arbitrary", "arbitrary")),
    )(ii, jj, qn, qpe, kv, kpe, kv, do, lse, delta)


def _mesh_pos():
    x, y, c = lax.axis_index("x"), lax.axis_index("y"), lax.axis_index("c")
    return x, y, c, 4 * x + 2 * y + c


def _peer(x, y, c, k):
    px, py, pc = (x + ((k >> 2) & 1)) % 2, (y + ((k >> 1) & 1)) % 2, (c + (k & 1)) % 2
    return (px, py, pc), 4 * px + 2 * py + pc


_ANY = pl.BlockSpec(memory_space=pl.ANY)


def _rcopy(src, dst, send_sems, recv_sems, k, to):
    return pltpu.make_async_remote_copy(src_ref=src, dst_ref=dst, send_sem=send_sems.at[k], recv_sem=recv_sems.at[k],
                                        device_id=to, device_id_type=pl.DeviceIdType.MESH)


def _all_gather(wsh, ssh):
    def body(w_ref, s_ref, wall_ref, sall_ref, send_sems, recv_sems, loc_sems):
        x, y, c, me = _mesh_pos()
        sib = (x, y, 1 - c)
        chips = [(1 - x, y), (x, 1 - y), (1 - x, 1 - y)]
        slot = lambda px, py, pc: 4 * px + 2 * py + pc
        loc = [pltpu.make_async_copy(s_ref, sall_ref.at[me], loc_sems.at[0])]
        for cp in loc:
            cp.start()
        sends, fwd_waits = [], []
        for n, (src, dst) in enumerate(((w_ref, wall_ref), (s_ref, sall_ref))):
            o = 7 * n
            sends.append(_rcopy(src, dst.at[me], send_sems, recv_sems, o, sib))
            for j, chip in enumerate(chips):
                sends.append(_rcopy(src, dst.at[me], send_sems, recv_sems, o + 1 + j, (*chip, c)))
        for cp in sends:
            cp.start()
        for n, (src, dst) in enumerate(((w_ref, wall_ref), (s_ref, sall_ref))):
            o = 7 * n
            for j, chip in enumerate(chips):
                got = dst.at[slot(*chip, c)]
                _rcopy(src, got, send_sems, recv_sems, o + 1 + j, sib).wait_recv()
                fw = _rcopy(got, got, send_sems, recv_sems, o + 4 + j, sib)
                fw.start()
                sends.append(fw)
            fwd_waits.append(_rcopy(src, dst.at[slot(x, y, 1 - c)], send_sems, recv_sems, o, sib))
            for j, chip in enumerate(chips):
                fwd_waits.append(_rcopy(src, dst.at[slot(*chip, 1 - c)], send_sems, recv_sems, o + 4 + j, sib))
        for cp in fwd_waits:
            cp.wait_recv()
        for cp in sends:
            cp.wait_send()
        for cp in loc:
            cp.wait()

    wall, sall = _pcall(
        body, in_specs=[_ANY, _ANY], out_specs=[_ANY, _ANY],
        out_shape=[jax.ShapeDtypeStruct((N_DEV,) + wsh.shape, wsh.dtype), jax.ShapeDtypeStruct((N_DEV,) + ssh.shape, ssh.dtype)],
        scratch_shapes=[pltpu.SemaphoreType.DMA((14,)), pltpu.SemaphoreType.DMA((14,)), pltpu.SemaphoreType.DMA((1,))],
        name="all_gather_weights",
    )(wsh, ssh)
    me = 4 * lax.axis_index("x") + 2 * lax.axis_index("y") + lax.axis_index("c")
    return lax.dynamic_update_index_in_dim(wall, wsh, me, axis=0), sall


_HBM = pl.BlockSpec(memory_space=pltpu.HBM)
_SEM = pl.BlockSpec(memory_space=pltpu.SEMAPHORE)
_EFFECT = pltpu.SideEffectType.DATAFLOW_SIDE_EFFECTING


def _other_chips(x, y):
    return [(1 - x, y), (x, 1 - y), (1 - x, 1 - y)]


def _gather_start(wsh, order_after):
    def body(w_ref, land_ref, dep_ref, send_sems, recv_sems, w_thru, land_thru, token):
        x, y, c, me = _mesh_pos()
        _rcopy(w_ref, land_ref.at[me], send_sems, recv_sems, 0, (x, y, 1 - c)).start()
        for j, chip in enumerate(_other_chips(x, y)):
            _rcopy(w_ref, land_ref.at[me], send_sems, recv_sems, 1 + j, (*chip, c)).start()
        token[...] = jnp.zeros_like(token)

    shape = (N_DEV,) + wsh.shape
    land = pltpu.with_memory_space_constraint(lax.empty(shape, wsh.dtype), pltpu.HBM)
    return _pcall(
        body, name="gather_start",
        out_shape=(pltpu.SemaphoreType.DMA((4,)), pltpu.SemaphoreType.DMA((4,)), pltpu.HBM(wsh.shape, wsh.dtype),
                   pltpu.HBM(shape, wsh.dtype), jax.ShapeDtypeStruct((8, 128), F32)),
        in_specs=(_HBM, _HBM, _ANY), out_specs=(_SEM, _SEM, _HBM, _HBM, pl.BlockSpec(memory_space=pltpu.VMEM)),
        input_output_aliases={0: 2, 1: 3}, compiler_params=pltpu.CompilerParams(has_side_effects=_EFFECT),
    )(pltpu.with_memory_space_constraint(wsh, pltpu.HBM), land, order_after)


def _gather_wait(send_sems, recv_sems, w_thru, land_thru, after):
    def body(w_ref, land_ref, send_sems, recv_sems, after_ref, w_out, land_out):
        x, y, c, _ = _mesh_pos()
        senders = [(x, y, 1 - c)] + [(*chip, c) for chip in _other_chips(x, y)]
        for k, (px, py, pc) in enumerate(senders):
            cp = _rcopy(w_ref, land_ref.at[4 * px + 2 * py + pc], send_sems, recv_sems, k, (px, py, pc))
            cp.wait_send()
            cp.wait_recv()

    return _pcall(
        body, name="gather_wait", out_shape=(pltpu.HBM(w_thru.shape, w_thru.dtype), pltpu.HBM(land_thru.shape, land_thru.dtype)),
        in_specs=(_HBM, _HBM, _SEM, _SEM, _ANY), out_specs=(_HBM, _HBM), input_output_aliases={0: 0, 1: 1},
        compiler_params=pltpu.CompilerParams(has_side_effects=_EFFECT),
    )(w_thru, land_thru, send_sems, recv_sems, after)


def _gather_finish(wsh, land):
    def body(land_ref, out_ref, send_sems, recv_sems):
        x, y, c, _ = _mesh_pos()
        sib = (x, y, 1 - c)
        sends, recvs = [], []
        for j, (px, py) in enumerate(_other_chips(x, y)):
            mine, theirs = 4 * px + 2 * py + c, 4 * px + 2 * py + (1 - c)
            sends.append(_rcopy(land_ref.at[mine], out_ref.at[mine], send_sems, recv_sems, j, sib))
            recvs.append(_rcopy(land_ref.at[theirs], out_ref.at[theirs], send_sems, recv_sems, j, sib))
        for cp in sends:
            cp.start()
        for cp in recvs:
            cp.wait_recv()
        for cp in sends:
            cp.wait_send()

    done = _pcall(
        body, in_specs=[_ANY], out_specs=_ANY, out_shape=jax.ShapeDtypeStruct(land.shape, land.dtype),
        input_output_aliases={0: 0},
        scratch_shapes=[pltpu.SemaphoreType.DMA((3,)), pltpu.SemaphoreType.DMA((3,))], name="gather_finish",
    )(land)
    me = 4 * lax.axis_index("x") + 2 * lax.axis_index("y") + lax.axis_index("c")
    return lax.dynamic_update_index_in_dim(done, wsh, me, axis=0)


_ROW_TILES = (400, 368, 352, 256, 128, 16, 8)


def _exchange_sibling(g4, name):
    def body(g_ref, r_ref, send_sems, recv_sems):
        x, y, c, _ = _mesh_pos()
        sib = (x, y, 1 - c)
        cps = [_rcopy(g_ref.at[q, 1 - c], r_ref.at[q], send_sems, recv_sems, q, sib) for q in range(4)]
        for cp in cps:
            cp.start()
        for cp in cps:
            cp.wait()

    n, _, R, C = g4.shape
    return _pcall(
        body, in_specs=[_ANY], out_specs=_ANY, out_shape=jax.ShapeDtypeStruct((n, R, C), g4.dtype),
        scratch_shapes=[pltpu.SemaphoreType.DMA((4,)), pltpu.SemaphoreType.DMA((4,))], name=name,
    )(g4)


def _sum_sibling(g4, rsib, cvec, name):
    n, _, R, C = g4.shape
    tr = _pick(R, _ROW_TILES)

    def body(c_ref, g_ref, r_ref, o_ref):
        o_ref[...] = (g_ref[...].astype(F32) + r_ref[...].astype(F32)).astype(o_ref.dtype)

    grid_spec = pltpu.PrefetchScalarGridSpec(
        num_scalar_prefetch=1, grid=(n, R // tr),
        in_specs=[pl.BlockSpec((None, None, tr, C), lambda q, i, cr: (q, cr[0], i, 0)), pl.BlockSpec((None, tr, C), lambda q, i, cr: (q, i, 0))],
        out_specs=pl.BlockSpec((None, tr, C), lambda q, i, cr: (q, i, 0)))
    return _pcall(body, grid_spec=grid_spec, out_shape=jax.ShapeDtypeStruct((n, R, C), g4.dtype), name=name)(cvec, g4, rsib)


def _chips_start(part, name, order_after=None):
    n_in = 2 + int(order_after is not None)

    def body(*refs):
        p_ref, land_ref = refs[:2]
        send_sems, recv_sems, token = refs[n_in], refs[n_in + 1], refs[n_in + 4]
        x, y, c, _ = _mesh_pos()
        myq = 2 * x + y
        for j, (px, py) in enumerate(_other_chips(x, y)):
            _rcopy(p_ref.at[2 * px + py], land_ref.at[myq], send_sems, recv_sems, j, (px, py, c)).start()
        token[...] = jnp.zeros_like(token)

    land = pltpu.with_memory_space_constraint(lax.empty(part.shape, part.dtype), pltpu.HBM)
    extra = () if order_after is None else (order_after,)
    return _pcall(
        body, name=name,
        out_shape=(pltpu.SemaphoreType.DMA((3,)), pltpu.SemaphoreType.DMA((3,)), pltpu.HBM(part.shape, part.dtype),
                   pltpu.HBM(part.shape, part.dtype), jax.ShapeDtypeStruct((8, 128), F32)),
        in_specs=(_HBM, _HBM) + (_ANY,) * len(extra),
        out_specs=(_SEM, _SEM, _HBM, _HBM, pl.BlockSpec(memory_space=pltpu.VMEM)),
        input_output_aliases={0: 2, 1: 3}, compiler_params=pltpu.CompilerParams(has_side_effects=_EFFECT),
    )(pltpu.with_memory_space_constraint(part, pltpu.HBM), land, *extra)


def _chips_wait(send_sems, recv_sems, p_thru, land_thru, after, name):
    def body(p_ref, land_ref, send_sems, recv_sems, after_ref, p_out, land_out):
        x, y, c, _ = _mesh_pos()
        for j, (px, py) in enumerate(_other_chips(x, y)):
            q = 2 * px + py
            cp = _rcopy(p_ref.at[q], land_ref.at[q], send_sems, recv_sems, j, (px, py, c))
            cp.wait_send()
            cp.wait_recv()

    return _pcall(
        body, name=name, out_shape=(pltpu.HBM(p_thru.shape, p_thru.dtype), pltpu.HBM(p_thru.shape, p_thru.dtype)),
        in_specs=(_HBM, _HBM, _SEM, _SEM, _ANY), out_specs=(_HBM, _HBM), input_output_aliases={0: 0, 1: 1},
        compiler_params=pltpu.CompilerParams(has_side_effects=_EFFECT),
    )(p_thru, land_thru, send_sems, recv_sems, after)


def _sum_chips(part, land, qvec, name):
    n, R, C = part.shape
    tr = _pick(R, _ROW_TILES)

    def body(q_ref, p_ref, *refs):
        o_ref = refs[n]
        acc = None
        for q in range(n):
            term = jnp.where(q_ref[0] == q, p_ref[...], refs[q][...]).astype(F32)
            acc = term if acc is None else acc + term
        o_ref[...] = acc

    def land_spec(q):
        return pl.BlockSpec((None, tr, C), lambda i, m: (jnp.where(m[0] == q, (q + 1) % n, q), i, 0))

    grid_spec = pltpu.PrefetchScalarGridSpec(
        num_scalar_prefetch=1, grid=(R // tr,),
        in_specs=[pl.BlockSpec((None, tr, C), lambda i, m: (m[0], i, 0))] + [land_spec(q) for q in range(n)],
        out_specs=pl.BlockSpec((tr, C), lambda i, m: (i, 0)))
    return _pcall(body, grid_spec=grid_spec, out_shape=jax.ShapeDtypeStruct((R, C), F32), name=name)(qvec, part, *([land] * n))


def _exchange_small(gsmall):
    def body(s_ref, srecv_ref, send_sems, recv_sems, loc_sem):
        x, y, c, me = _mesh_pos()
        loc = pltpu.make_async_copy(s_ref, srecv_ref.at[me], loc_sem.at[0])
        loc.start()
        sends, recvs = [], []
        for k in range(1, N_DEV):
            to, pidx = _peer(x, y, c, k)
            sends.append(_rcopy(s_ref, srecv_ref.at[me], send_sems, recv_sems, k - 1, to))
            recvs.append(_rcopy(s_ref, srecv_ref.at[pidx], send_sems, recv_sems, k - 1, to))
        for cp in sends:
            cp.start()
        for cp in recvs:
            cp.wait_recv()
        for cp in sends:
            cp.wait_send()
        loc.wait()

    return _pcall(
        body, in_specs=[_ANY], out_specs=_ANY, out_shape=jax.ShapeDtypeStruct((N_DEV,) + gsmall.shape, gsmall.dtype),
        scratch_shapes=[pltpu.SemaphoreType.DMA((7,)), pltpu.SemaphoreType.DMA((7,)), pltpu.SemaphoreType.DMA((1,))],
        name="exchange_small",
    )(gsmall)


def _sum_slots(recv, name):
    n, R, C = recv.shape
    tr = _pick(R, _ROW_TILES)

    def body(r_ref, o_ref):
        acc = r_ref[0].astype(F32)
        for s in range(1, n):
            acc = acc + r_ref[s].astype(F32)
        o_ref[...] = acc

    return _pcall(body, grid=(R // tr,), in_specs=[pl.BlockSpec((n, tr, C), lambda i: (0, i, 0))],
                  out_specs=pl.BlockSpec((tr, C), lambda i: (i, 0)), out_shape=jax.ShapeDtypeStruct((R, C), F32), name=name)(recv)


def _adamw(w, g, m, v, name, order_after=None):
    shape = w.shape
    w2, g2, m2, v2 = (a.reshape(-1, shape[-1]) for a in (w, g, m, v))
    R, C = w2.shape
    tr = _pick(R, (256, 128, 64, 32, 16, 8)) if R > 256 else R
    n_in = 4 + int(order_after is not None)

    def body(*refs):
        w_ref, g_ref, m_ref, v_ref = refs[:4]
        d_ref, nm_ref, nv_ref = refs[n_in:]
        gg = g_ref[...]
        nm = ADAM_B1 * m_ref[...] + (1.0 - ADAM_B1) * gg
        nv = ADAM_B2 * v_ref[...] + (1.0 - ADAM_B2) * jnp.square(gg)
        m_hat = nm / (1.0 - ADAM_B1 ** ADAM_STEP)
        v_hat = nv / (1.0 - ADAM_B2 ** ADAM_STEP)
        d_ref[...] = -ADAM_LR * (m_hat / (jnp.sqrt(v_hat) + ADAM_EPS) + ADAM_WD * w_ref[...])
        nm_ref[...] = nm
        nv_ref[...] = nv

    spec = pl.BlockSpec((tr, C), lambda i: (i, 0))
    in_specs, args = [spec] * 4, [w2, g2, m2, v2]
    if order_after is not None:
        in_specs.append(pl.BlockSpec(memory_space=pl.ANY))
        args.append(order_after)
    d, nm, nv = _pcall(body, grid=(R // tr,), in_specs=in_specs, out_specs=[spec] * 3,
                       out_shape=[jax.ShapeDtypeStruct((R, C), F32)] * 3, name=name)(*args)
    return d.reshape(shape), nm.reshape(shape), nv.reshape(shape)


def _to_rows(name, w):
    w = w[0]
    if name in ("ffn1_w_in", "ffn2_w_in", "w_in"):
        return w.T
    if name in ("w_uq", "w_ukv", "ple_w_proj"):
        return w.T.reshape(-1, 1024)
    return w


def _from_rows(name, g, shape):
    if name in ("ffn1_w_in", "ffn2_w_in", "w_in"):
        return g.T.reshape(shape)
    if name in ("w_uq", "w_ukv", "ple_w_proj"):
        return g.reshape(-1, shape[1]).T.reshape(shape)
    return g.reshape(shape)


def _unpack(wall, group):
    out, off = {}, 0
    for name, r in group:
        out[name] = wall[:, off:off + r, :].reshape(N_DEV * r, 1024)
        off += _pad16(r)
    return out


def _w_in_internal(wt):
    return jnp.concatenate([wt[0:6144], wt[6720:8768], wt[6144:6720], jnp.zeros((P_W - W_IN_COLS, 1024), wt.dtype)], axis=0)


def _rope_tables(positions):
    pos = positions[0].astype(F32)

    def cs(half):
        inv = ROPE_BASE ** (-jnp.arange(half, dtype=F32) / half)
        ang = pos[:, None] * inv
        return jnp.cos(ang), jnp.sin(ang)

    c, s = cs(64)
    c2, s2 = cs(32)
    return (jnp.concatenate([c, c], axis=1), jnp.concatenate([-s, s], axis=1),
            jnp.concatenate([c2, c2, c2, c2], axis=1), jnp.concatenate([-s2, s2, -s2, s2], axis=1))


def _local_step(x, p, positions, target, W, ln_g, ln_b, gn_g, qg, kvg, early=None, rest_weights=None, start_token=None):
    T = x.shape[0]
    tabs = _rope_tables(positions)
    rc = _ret_consts()
    lg = [ln_g[i:i + 1] for i in range(4)]
    lb = [ln_b[i:i + 1] for i in range(4)]
    pb = p.astype(BF16)

    hs1, gu1, *xb = _ffn_in(x, W["ffn1_w_in"], "ffn1_in", order_after=start_token)
    xb = xb[0] if xb else x
    f1, h1, h1b = _mm(hs1, W["ffn1_w_out"], name="ffn1_out", tm=LN_TM, epilogue=_ln_epilogue(x, 0.5, lg[0], lb[0]))
    if rest_weights is not None:
        W = {**W, **rest_weights(h1b)}
    w_in_t = _w_in_internal(W["w_in"])
    wuq = W["w_uq"].reshape(1536, LORA).reshape(HEADS, 192, LORA)
    wuq = jnp.concatenate([wuq[:, :128].reshape(1024, LORA), wuq[:, 128:].reshape(512, LORA)], axis=0)
    wukv = W["w_ukv"].reshape(2048, LORA).reshape(HEADS, 2, 128, LORA).transpose(1, 0, 2, 3).reshape(2048, LORA)
    wp_t = W["ple_w_proj"].reshape(1024, D_PLE)
    proj = _mm(h1b, w_in_t, tb=True, out_dtype=BF16, name="mixer_in")
    rq, rk, qn, kvn, kpe = _mixer_prep_fwd(proj, tabs, qg, kvg, T)
    y, yr, states = _ret_fwd(rq, rk, proj, proj, gn_g, rc, T)
    y_ret = _mm(yr, W["w_ret_o"], out_dtype=BF16, name="ret_o")
    qnope, qpe = _mm(qn, wuq, tb=True, name="mla_uq", tm=LN_TM, tn=1536, epilogue=_q_assemble_epilogue(tabs))
    kv = _mm(kvn, wukv, tb=True, out_dtype=BF16, name="mla_ukv")
    o, lse = _attn_fwd(qnope, qpe, kv, kpe, T)
    y_mla, mix = _mm(o, W["w_mla_o"], name="mla_o", tm=LN_TM, epilogue=_mix_epilogue(proj, y_ret))
    mixed, h2, h2b = _mm(mix, W["w_out"], name="mixer_out", tm=LN_TM, epilogue=_ln_epilogue(h1, 1.0, lg[1], lb[1]))
    hs2, gu2 = _ffn_in(h2b, W["ffn2_w_in"], "ffn2_in")
    f2, h3, h3b = _mm(hs2, W["ffn2_w_out"], name="ffn2_out", tm=LN_TM, epilogue=_ln_epilogue(h2, 0.5, lg[2], lb[2]))
    pp = _mm(pb, wp_t, tb=True, name="ple_proj")

    G = {}
    dh3_a, dgl, dpp, dg3, db3, loss = _mm(h3b, W["ple_w_gate"], name="ple_gate", tm=LN_TM,
                                          epilogue=_head_epilogue(h3, pp, target, lg[3], lb[3]))
    G["ple_w_gate"] = _mm(h3b, dgl, ta=True, name="d_ple_gate")
    G["ple_w_proj"] = _mm(dpp, pb, ta=True, name="d_ple_proj")
    dh2_a, df2, dg2, db2 = _mm(dgl, W["ple_w_gate"], tb=True, add=dh3_a, name="dh3", tm=LN_TM,
                               epilogue=_ln_bwd_epilogue(h2, f2, 0.5, lg[2], lb[2]))
    G["ffn2_w_out"] = _mm(hs2, df2, ta=True, name="d_ffn2_out")
    da2 = _ffn_act_bwd(df2, W["ffn2_w_out"], gu2, "ffn2_act_bwd")
    G["ffn2_w_in"] = _mm(da2, h2b, ta=True, name="d_ffn2_in")
    dh1_a, dmixed, dg1, db1 = _mm(da2, W["ffn2_w_in"], add=dh2_a, name="dh2", tm=LN_TM,
                                  epilogue=_ln_bwd_epilogue(h1, mixed, 1.0, lg[1], lb[1]))
    G["w_out"] = _mm(mix, dmixed, ta=True, name="d_mixer_out")
    dp_gate, dy_ret, dy_mla = _mm(dmixed, W["w_out"], tb=True, name="dmix", tm=LN_TM,
                                  epilogue=_mix_bwd_epilogue(proj, y_ret, y_mla))
    G["w_mla_o"] = _mm(o, dy_mla, ta=True, name="d_mla_o")
    dob, delta = _mm(dy_mla, W["w_mla_o"], tb=True, name="do", tm=LN_TM, epilogue=_delta_epilogue(o))
    dqn_f, dqpe_f, dkn, dkpe_all, dv = _attn_bwd(qnope, qpe, kv, kpe, dob, lse, delta, T)
    dq_n, dq_r = _q_assemble_bwd(dqn_f, dqpe_f, tabs, T)
    g_uq = jnp.concatenate([_mm(dq_n, qn, ta=True, name="d_uq_nope"), _mm(dq_r, qn, ta=True, name="d_uq_rope")], axis=0)
    g_uq = jnp.concatenate([g_uq[:1024].reshape(HEADS, 128, LORA), g_uq[1024:].reshape(HEADS, 64, LORA)], axis=1)
    G["w_uq"] = g_uq.reshape(1536 * LORA // 1024, 1024)
    dqn = _mm(dq_r, wuq[1024:], add=_mm(dq_n, wuq[:1024], name="dqn_a"), name="dqn_b")
    g_ukv = jnp.stack([_mm(dkn, kvn, ta=True, name="d_ukv_k"), _mm(dv, kvn, ta=True, name="d_ukv_v")], axis=0)
    G["w_ukv"] = g_ukv.reshape(2, HEADS, 128, LORA).transpose(1, 0, 2, 3).reshape(2048 * LORA // 1024, 1024)
    dkvn = _mm(dv, wukv[1024:], add=_mm(dkn, wukv[:1024], name="dkvn_a"), name="dkvn_b")
    dp_mla, dqg, dkvg = _rms_bwd(proj, dqn, dkvn, dkpe_all, tabs, qg, kvg, T)
    G["w_ret_o"] = _mm(yr, dy_ret, ta=True, name="d_ret_o")
    dyr = _mm(dy_ret, W["w_ret_o"], tb=True, out_dtype=BF16, name="dyr")
    dp_ret, dgn = _ret_bwd(rq, rk, proj, y, proj, gn_g, states, dyr, rc, tabs, T)
    g_ret, g_gate, g_mla = (_mm(dp, h1b, ta=True, name="d_mixer_in_" + n)
                            for dp, n in ((dp_ret, "ret"), (dp_gate, "gate"), (dp_mla, "mla")))
    G["w_in"] = jnp.concatenate([g_ret, g_mla[:W_IN_COLS - P_CQ], g_gate], axis=0)
    lg0 = lg[0] if early is None else lg[0] + early(G)[0:1, 0:1]
    dh1 = _mm_pieces([(dp_ret, w_in_t[:P_GR]), (dp_gate, w_in_t[P_GR:P_CQ]), (dp_mla, w_in_t[P_CQ:])], dh1_a, "dh1")
    dx_a, df1, dg0, db0 = _ln_bwd_rows(x, f1, 0.5, lg0, lb[0], dh1, T, "ln0_bwd")
    G["ffn1_w_out"] = _mm(hs1, df1, ta=True, name="d_ffn1_out")
    da1 = _ffn_act_bwd(df1, W["ffn1_w_out"], gu1, "ffn1_act_bwd")
    G["ffn1_w_in"] = _mm(da1, xb, ta=True, name="d_ffn1_in")
    grad_x = _mm(da1, W["ffn1_w_in"], add=dx_a, name="grad_x")

    small = dict(ln_g=jnp.concatenate([dg0, dg1, dg2, dg3], axis=0), ln_b=jnp.concatenate([db0, db1, db2, db3], axis=0),
                 ret_gn_g=dgn, q_norm_g=dqg, kv_norm_g=dkvg)
    return loss, grad_x, G, small


def kernel(x, p, positions, ln_g, ln_b, ffn1_w_in, ffn1_w_out, w_in, ret_gn_g, w_ret_o, q_norm_g, kv_norm_g, w_uq, w_ukv, w_mla_o, w_out, ffn2_w_in, ffn2_w_out, ple_w_gate, ple_w_proj, loss_target, m_ln_g, m_ln_b, m_ffn1_w_in, m_ffn1_w_out, m_w_in, m_ret_gn_g, m_w_ret_o, m_q_norm_g, m_kv_norm_g, m_w_uq, m_w_ukv, m_w_mla_o, m_w_out, m_ffn2_w_in, m_ffn2_w_out, m_ple_w_gate, m_ple_w_proj, v_ln_g, v_ln_b, v_ffn1_w_in, v_ffn1_w_out, v_w_in, v_ret_gn_g, v_w_ret_o, v_q_norm_g, v_kv_norm_g, v_w_uq, v_w_ukv, v_w_mla_o, v_w_out, v_ffn2_w_in, v_ffn2_w_out, v_ple_w_gate, v_ple_w_proj):
    names = ("ln_g", "ln_b", "ffn1_w_in", "ffn1_w_out", "w_in", "ret_gn_g", "w_ret_o", "q_norm_g", "kv_norm_g", "w_uq", "w_ukv",
             "w_mla_o", "w_out", "ffn2_w_in", "ffn2_w_out", "ple_w_gate", "ple_w_proj")
    ws = dict(zip(names, (ln_g, ln_b, ffn1_w_in, ffn1_w_out, w_in, ret_gn_g, w_ret_o, q_norm_g, kv_norm_g, w_uq, w_ukv, w_mla_o,
                          w_out, ffn2_w_in, ffn2_w_out, ple_w_gate, ple_w_proj)))
    ms = dict(zip(names, (m_ln_g, m_ln_b, m_ffn1_w_in, m_ffn1_w_out, m_w_in, m_ret_gn_g, m_w_ret_o, m_q_norm_g, m_kv_norm_g, m_w_uq,
                          m_w_ukv, m_w_mla_o, m_w_out, m_ffn2_w_in, m_ffn2_w_out, m_ple_w_gate, m_ple_w_proj)))
    vs = dict(zip(names, (v_ln_g, v_ln_b, v_ffn1_w_in, v_ffn1_w_out, v_w_in, v_ret_gn_g, v_w_ret_o, v_q_norm_g, v_kv_norm_g, v_w_uq,
                          v_w_ukv, v_w_mla_o, v_w_out, v_ffn2_w_in, v_ffn2_w_out, v_ple_w_gate, v_ple_w_proj)))

    parts = []
    for name, r in PACK:
        rows = _to_rows(name, ws[name])
        if _pad16(r) != r:
            rows = jnp.concatenate([rows, jnp.zeros((_pad16(r) - r, 1024), F32)], axis=0)
        parts.append(rows)
    wsh_first = jnp.concatenate(parts[:len(PACK_LATE)], axis=0).astype(BF16)
    wsh_rest = jnp.concatenate(parts[len(PACK_LATE):], axis=0).astype(BF16)
    ssh = jnp.concatenate([ln_g[0], ln_b[0]], axis=0)
    wall_first, sall = _all_gather(wsh_first, ssh)
    *gather_handles, start_token = _gather_start(wsh_rest, wall_first)
    W = _unpack(wall_first, PACK_LATE)
    ln_full = sall.reshape(N_DEV, 2, 4, 128).transpose(1, 2, 0, 3).reshape(2, 4, 1024)

    def rest_weights(after):
        w_thru, land = _gather_wait(*gather_handles, after)
        return _unpack(_gather_finish(w_thru, land), PACK_EARLY)

    cvec = lax.axis_index("c").astype(jnp.int32).reshape(1)
    qvec = (2 * lax.axis_index("x") + lax.axis_index("y")).astype(jnp.int32).reshape(1)

    def chip_partials(G, group, tag):
        gparts = []
        for name, r in group:
            g = G[name].reshape(N_DEV, r, 1024)
            if _pad16(r) != r:
                g = jnp.concatenate([g, jnp.zeros((N_DEV, _pad16(r) - r, 1024), g.dtype)], axis=1)
            gparts.append(g)
        gfull = jnp.concatenate(gparts, axis=1).astype(BF16)
        g4 = gfull.reshape(4, 2, gfull.shape[1], 1024)
        return _sum_sibling(g4, _exchange_sibling(g4, "exchange_sibling_" + tag), cvec, "sum_sibling_" + tag)

    in_flight = []

    def early(G):
        *handles, token = _chips_start(chip_partials(G, PACK_EARLY, "early"), "chips_start_early")
        in_flight.append(handles)
        return token

    loss_p, grad_x, G, small = _local_step(x[0], p[0, 0], positions, loss_target[0], W, ln_full[0], ln_full[1],
                                           ret_gn_g, q_norm_g, kv_norm_g, early=early, rest_weights=rest_weights,
                                           start_token=start_token)

    part_e, land_e = _chips_wait(*in_flight[0], grad_x, "chips_wait_early")
    gsh_early = _sum_chips(part_e, land_e, qvec, "sum_grads_early")
    pad256 = lambda a: jnp.concatenate([a, jnp.zeros((1, 1024 - a.shape[1]), F32)], axis=1)
    gsmall = jnp.concatenate([small["ln_g"], small["ln_b"], small["ret_gn_g"].reshape(2, 1024), pad256(small["q_norm_g"]),
                              pad256(small["kv_norm_g"]), jnp.zeros((SMALL_ROWS - 12, 1024), F32)], axis=0)
    srecv = _exchange_small(gsmall)
    part_l = chip_partials(G, PACK_LATE, "late")
    *late_handles, late_token = _chips_start(part_l, "chips_start_late", order_after=srecv)
    ssum = _sum_slots(srecv, "sum_small_grads")

    def unpack_grads(group, gsh):
        out, off = {}, 0
        for name, r in group:
            out[name] = _from_rows(name, gsh[off:off + r], ws[name].shape)
            off += _pad16(r)
        return out

    grads = unpack_grads(PACK_EARLY, gsh_early)
    me = 4 * lax.axis_index("x") + 2 * lax.axis_index("y") + lax.axis_index("c")
    grads["ln_g"] = lax.dynamic_slice(ssum[0:4], (0, me * 128), (4, 128)).reshape(1, 4, 128)
    grads["ln_b"] = lax.dynamic_slice(ssum[4:8], (0, me * 128), (4, 128)).reshape(1, 4, 128)
    grads["ret_gn_g"] = ssum[8:10].reshape(1, 2048)
    grads["q_norm_g"] = ssum[10:11, :256]
    grads["kv_norm_g"] = ssum[11:12, :256]

    delta, new_m, new_v = {}, {}, {}
    late_names = [n for n, _ in PACK_LATE]
    last = late_token
    for name in names:
        if name not in late_names:
            delta[name], new_m[name], new_v[name] = _adamw(ws[name], grads[name], ms[name], vs[name], "adamw_" + name,
                                                           order_after=last)
            last = new_v[name]
    part_l, land_l = _chips_wait(*late_handles, last, "chips_wait_late")
    grads.update(unpack_grads(PACK_LATE, _sum_chips(part_l, land_l, qvec, "sum_grads_late")))
    for name in late_names:
        delta[name], new_m[name], new_v[name] = _adamw(ws[name], grads[name], ms[name], vs[name], "adamw_" + name)

    loss = lax.psum(loss_p[0, 0], ("x", "y", "c"))
    return (loss, grad_x[None], *[grads[n] for n in names], *[delta[n] for n in names],
            *[new_m[n] for n in names], *[new_v[n] for n in names])
```

```python
import math

import jax
import jax.numpy as jnp
from jax import lax
from jax.experimental import pallas as pl
from jax.experimental.pallas import tpu as pltpu

F32 = jnp.float32
BF16 = jnp.bfloat16

N_DEV = 8
D = 1024
D_FF = 2816
D_PLE = 256
CHUNK = 64
HEADS = 8
RET_DK = 128
RET_DV = 256
MLA_NOPE = 128
MLA_ROPE = 64
MLA_DV = 128
LORA = 256
ROPE_BASE = 10000.0
EPS = 1e-5
ALPHA = 2.0 ** 0.25
RET_SCALE = RET_DK ** -0.5
MLA_SCALE = (MLA_NOPE + MLA_ROPE) ** -0.5
NEG = -1e30

ADAM_LR = 0.001
ADAM_B1 = 0.9
ADAM_B2 = 0.999
ADAM_EPS = 1e-08
ADAM_WD = 0.01
ADAM_STEP = 10

P_RQ, P_RK, P_RV, P_RG, P_GR, P_GM, P_CQ, P_CKV, P_KPE, P_W = 0, 1024, 2048, 4096, 6144, 7168, 8192, 8448, 8704, 8960
W_IN_COLS = 8768
RET_L = 256
ATT_TF = 2048
ATT_TB = 1024
ATT_HP = 1
LOG2E = math.log2(math.e)
Q_PRESCALE = MLA_SCALE * LOG2E

PACK = (("ffn1_w_in", 704), ("ffn1_w_out", 352), ("w_in", 1096), ("w_ret_o", 256), ("w_uq", 48), ("w_ukv", 64),
        ("w_mla_o", 128), ("w_out", 128), ("ffn2_w_in", 704), ("ffn2_w_out", 352), ("ple_w_gate", 128), ("ple_w_proj", 32))


def _pad16(r):
    return -(-r // 16) * 16


PACK_LATE = PACK[:2]
PACK_EARLY = PACK[2:]
SMALL_ROWS = 16


def _pcall(body, **kw):
    return pl.pallas_call(body, **kw)


def _pick(dim, prefs):
    for p in prefs:
        if dim % p == 0:
            return p
    return dim


def _sigmoid(x):
    return 1.0 / (1.0 + jnp.exp(-x))


def _silu(x):
    return x * _sigmoid(x)


def _ln(r, g, b):
    mu = jnp.mean(r, axis=-1, keepdims=True)
    var = jnp.mean(jnp.square(r - mu), axis=-1, keepdims=True)
    return (r - mu) * lax.rsqrt(var + EPS) * g + b


def _rms(x, g):
    return x * lax.rsqrt(jnp.mean(jnp.square(x), axis=-1, keepdims=True) + EPS) * g


def _dot(a, b, ca, cb):
    return lax.dot_general(a, b, (((ca,), (cb,)), ((), ())), preferred_element_type=F32)


def _accum(ref, val, first=None):
    @pl.when(pl.program_id(0) == 0 if first is None else first)
    def _():
        ref[...] = jnp.zeros_like(ref)

    ref[...] += val


def _mm(a, b, *, ta=False, tb=False, add=None, out_dtype=None, name, tm=None, tn=None, tk=None, epilogue=None):
    parts = a.shape[0] if a.ndim == 3 else 1
    ar, ac = a.shape[-2], a.shape[-1]
    out_dtype = out_dtype or (BF16 if ta else F32)
    if ta:
        K, M = ar, ac * parts
    else:
        M, K = ar, ac * parts
    if tb:
        N, K2 = b.shape
    else:
        K2, N = b.shape
    assert K == K2, (a.shape, b.shape, ta, tb)
    big = (1024, 1408, 1280, 768, 512, 256, 128)
    tm = tm or _pick(ac if (ta and parts > 1) else M, big)
    tn = tn or (N if N <= 1024 else _pick(N, big))
    kdim = ac if (not ta and parts > 1) else K
    tk = tk or (kdim if kdim <= 2816 and parts == 1 else
                _pick(kdim, (2048, 1408, 1280, 1024, 512) if tm <= 1024 else (1024, 1408, 1280, 512)))
    nk = K // tk
    grid = (M // tm, N // tn, nk)
    if parts > 1 and ta:
        per = ac // tm
        a_spec = pl.BlockSpec((None, tk, tm), lambda i, j, k: (i // per, k, i % per))
    elif parts > 1:
        per = ac // tk
        a_spec = pl.BlockSpec((None, tm, tk), lambda i, j, k: (k // per, i, k % per))
    else:
        a_spec = pl.BlockSpec((tk, tm), lambda i, j, k: (k, i)) if ta else pl.BlockSpec((tm, tk), lambda i, j, k: (i, k))
    b_spec = pl.BlockSpec((tn, tk), lambda i, j, k: (j, k)) if tb else pl.BlockSpec((tk, tn), lambda i, j, k: (k, j))
    o_spec = pl.BlockSpec((tm, tn), lambda i, j, k: (i, j))
    ca, cb = (0 if ta else 1), (1 if tb else 0)
    has_add = add is not None
    n_in = 2 + int(has_add)
    if epilogue is not None:
        assert tn == N and not ta
        ep_fn, ep_rows, ep_whole, ep_outs, ep_accs = epilogue
        n_ep_in = len(ep_rows) + len(ep_whole)
        n_out = len(ep_outs) + len(ep_accs)
    else:
        n_ep_in, n_out = 0, 1

    def body(*refs):
        a_ref, b_ref = refs[0], refs[1]
        add_ref = refs[2] if has_add else None
        o_ref = refs[n_in + n_ep_in]
        first_row_tile = pl.program_id(0) == 0

        def finish(r):
            if has_add:
                r = r + add_ref[...].astype(F32)
            if epilogue is not None:
                ep_fn(r, refs[n_in:n_in + n_ep_in], refs[n_in + n_ep_in:n_in + n_ep_in + n_out], first_row_tile)
            else:
                o_ref[...] = r.astype(out_dtype)

        if nk == 1:
            finish(_dot(a_ref[...], b_ref[...], ca, cb))
            return
        acc_ref = refs[-1]
        k = pl.program_id(2)

        @pl.when(k == 0)
        def _():
            acc_ref[...] = jnp.zeros_like(acc_ref)

        acc_ref[...] += _dot(a_ref[...], b_ref[...], ca, cb)

        @pl.when(k == nk - 1)
        def _():
            finish(acc_ref[...])

    in_specs = [a_spec, b_spec] + ([o_spec] if has_add else [])
    args = (a, b) + ((add,) if has_add else ())
    out_specs, out_shape = o_spec, jax.ShapeDtypeStruct((M, N), out_dtype)
    if epilogue is not None:
        for row_in in ep_rows:
            arr, col_block, width = (tuple(row_in) + (N,))[:3] if isinstance(row_in, tuple) else (row_in, 0, N)
            in_specs.append(pl.BlockSpec((tm, width), lambda i, j, k, _c=col_block: (i, _c)))
            args += (arr,)
        in_specs += [pl.BlockSpec(w.shape, lambda i, j, k, _n=w.ndim: (0,) * _n) for w in ep_whole]
        args += tuple(ep_whole)
        outs = [o if isinstance(o, tuple) else (o, N) for o in ep_outs]
        out_specs = ([pl.BlockSpec((tm, w), lambda i, j, k: (i, 0)) for _, w in outs]
                     + [pl.BlockSpec((r, N), lambda i, j, k: (0, 0)) for r in ep_accs])
        out_shape = [jax.ShapeDtypeStruct((M, w), dt) for dt, w in outs] + [jax.ShapeDtypeStruct((r, N), F32) for r in ep_accs]
    return _pcall(
        body, grid=grid, in_specs=in_specs, out_specs=out_specs, out_shape=out_shape,
        scratch_shapes=[pltpu.VMEM((tm, tn), F32)] if nk > 1 else [], name=name,
        compiler_params=pltpu.CompilerParams(dimension_semantics=("arbitrary" if epilogue is not None else "parallel", "parallel", "arbitrary")),
    )(*args)


def _mm_pieces(pieces, add, name):
    M, N = add.shape
    tm = _pick(M, (1024, 512, 256, 128))
    spans, in_specs, args, start = [], [], [], 0
    for a, b in pieces:
        K = a.shape[1]
        assert b.shape == (K, N) and a.shape[0] == M, (a.shape, b.shape)
        tk = _pick(K, (1536, 1024, 768, 512, 256, 128))
        n = K // tk
        spans.append((start, n))
        block = lambda k, _s=start, _n=n: jnp.where(k < _s, _n - 1, jnp.minimum(k - _s, _n - 1))
        in_specs.append(pl.BlockSpec((tm, tk), lambda i, k, _s=start, _blk=block: (jnp.where(k < _s, jnp.maximum(i - 1, 0), i), _blk(k))))
        in_specs.append(pl.BlockSpec((tk, N), lambda i, k, _blk=block: (_blk(k), 0)))
        args += [a, b]
        start += n
    o_spec = pl.BlockSpec((tm, N), lambda i, k: (i, 0))

    def body(*refs):
        add_ref, o_ref = refs[-2], refs[-1]
        k = pl.program_id(1)

        @pl.when(k == 0)
        def _():
            o_ref[...] = add_ref[...]

        for p, (s, n) in enumerate(spans):
            @pl.when((k >= s) & (k < s + n))
            def _(p=p):
                o_ref[...] += _dot(refs[2 * p][...], refs[2 * p + 1][...], 1, 0)

    return _pcall(
        body, grid=(M // tm, start), in_specs=in_specs + [o_spec], out_specs=o_spec, out_shape=jax.ShapeDtypeStruct((M, N), F32),
        name=name, compiler_params=pltpu.CompilerParams(dimension_semantics=("parallel", "arbitrary")),
    )(*args, add)


def _ln_epilogue(res, c, g, b):
    def fn(r, ins, outs, first):
        res_ref, g_ref, b_ref = ins
        f_ref, h_ref, hb_ref = outs
        h = _ln(ALPHA * res_ref[...] + c * r, g_ref[...], b_ref[...])
        f_ref[...] = r
        h_ref[...] = h
        hb_ref[...] = h.astype(BF16)

    return (fn, [res], [g, b], (F32, F32, BF16), ())


def _ln_bwd_epilogue(res, f, c, g, b):
    def fn(r, ins, outs, first):
        res_ref, f_ref, g_ref, b_ref = ins
        dr_ref, df_ref, dg_ref, db_ref = outs
        pre = ALPHA * res_ref[...] + c * f_ref[...]
        xc = pre - jnp.mean(pre, axis=-1, keepdims=True)
        rstd = lax.rsqrt(jnp.mean(jnp.square(xc), axis=-1, keepdims=True) + EPS)
        xhat = xc * rstd
        dyg = r * g_ref[...]
        dpre = rstd * (dyg - jnp.mean(dyg, axis=-1, keepdims=True) - xhat * jnp.mean(dyg * xhat, axis=-1, keepdims=True))
        dr_ref[...] = ALPHA * dpre
        df_ref[...] = (c * dpre).astype(BF16)
        _accum(dg_ref, jnp.sum(r * xhat, axis=0, keepdims=True), first)
        _accum(db_ref, jnp.sum(r, axis=0, keepdims=True), first)

    return (fn, [res, f], [g, b], (F32, BF16), (1, 1))


def _rows(body, T, tm, ins, outs, name, accs=()):
    in_specs, args = [], []
    for arr, w, cb in ins:
        if w is None:
            in_specs.append(pl.BlockSpec(arr.shape, lambda i, _n=arr.ndim: (0,) * _n))
        else:
            in_specs.append(pl.BlockSpec((tm, w), lambda i, _cb=cb: (i, _cb)))
        args.append(arr)
    out_specs = [pl.BlockSpec((tm, w), lambda i: (i, 0)) for w, _ in outs]
    out_shape = [jax.ShapeDtypeStruct((T, w), dt) for w, dt in outs]
    for r, w in accs:
        out_specs.append(pl.BlockSpec((r, w), lambda i: (0, 0)))
        out_shape.append(jax.ShapeDtypeStruct((r, w), F32))
    return _pcall(
        body, grid=(T // tm,), in_specs=in_specs, out_specs=out_specs, out_shape=out_shape, name=name,
        compiler_params=pltpu.CompilerParams(dimension_semantics=("arbitrary",)),
    )(*args)


def _ln_bwd_rows(res, f, c, g, b, dh, T, name):
    fn = _ln_bwd_epilogue(res, f, c, g, b)[0]

    def body(r_ref, f_ref, g_ref, b_ref, d_ref, dr_ref, df_ref, dg_ref, db_ref):
        fn(d_ref[...], (r_ref, f_ref, g_ref, b_ref), (dr_ref, df_ref, dg_ref, db_ref), pl.program_id(0) == 0)

    return _rows(body, T, 256, [(res, D, 0), (f, D, 0), (g, None, None), (b, None, None), (dh, D, 0)],
                 [(D, F32), (D, BF16)], name, accs=[(1, D), (1, D)])


FFN_TN = 1408
FFN_TM = 512
LN_TM = 512


def _ffn_in(x, wt, name, order_after=None):
    T = x.shape[0]
    tm, tn = min(FFN_TM, T), FFN_TN
    nj = D_FF // tn
    emit_xb = x.dtype != BF16
    n_in = 3 + int(order_after is not None)

    def body(*refs):
        x_ref, wg_ref, wu_ref = refs[:3]
        hs_ref, gu_ref = refs[n_in], refs[n_in + 1]
        xv = x_ref[...].astype(BF16)
        g = _dot(xv, wg_ref[...], 1, 1)
        u = _dot(xv, wu_ref[...], 1, 1)
        hs_ref[...] = (_silu(g) * u).astype(BF16)
        gu_ref[0] = g.astype(BF16)
        gu_ref[1] = u.astype(BF16)
        if emit_xb:
            refs[n_in + 2][...] = xv

    in_specs = [pl.BlockSpec((tm, D), lambda i, j: (i, 0)), pl.BlockSpec((tn, D), lambda i, j: (j, 0)),
                pl.BlockSpec((tn, D), lambda i, j: (j + nj, 0))]
    args = [x, wt, wt]
    if order_after is not None:
        in_specs.append(pl.BlockSpec(order_after.shape, lambda i, j: (0, 0)))
        args.append(order_after)
    out_specs = [pl.BlockSpec((tm, tn), lambda i, j: (i, j)), pl.BlockSpec((2, tm, tn), lambda i, j: (0, i, j))]
    out_shape = [jax.ShapeDtypeStruct((T, D_FF), BF16), jax.ShapeDtypeStruct((2, T, D_FF), BF16)]
    if emit_xb:
        out_specs.append(pl.BlockSpec((tm, D), lambda i, j: (i, 0)))
        out_shape.append(jax.ShapeDtypeStruct((T, D), BF16))
    return _pcall(
        body, grid=(T // tm, nj), in_specs=in_specs, out_specs=out_specs, out_shape=out_shape, name=name,
        compiler_params=pltpu.CompilerParams(dimension_semantics=("parallel", "arbitrary")),
    )(*args)


def _ffn_act_bwd(df, wo, gu, name):
    T = df.shape[0]
    tm, tn = min(FFN_TM, T), FFN_TN

    def body(d_ref, w_ref, gu_ref, o_ref):
        dhs = _dot(d_ref[...], w_ref[...], 1, 1)
        g, u = gu_ref[0].astype(F32), gu_ref[1].astype(F32)
        sig = _sigmoid(g)
        act = g * sig
        o_ref[0] = (dhs * u * (sig + act * (1.0 - sig))).astype(BF16)
        o_ref[1] = (dhs * act).astype(BF16)

    return _pcall(
        body, grid=(T // tm, D_FF // tn),
        in_specs=[pl.BlockSpec((tm, D), lambda i, j: (i, 0)), pl.BlockSpec((tn, D), lambda i, j: (j, 0)),
                  pl.BlockSpec((2, tm, tn), lambda i, j: (0, i, j))],
        out_specs=pl.BlockSpec((2, tm, tn), lambda i, j: (0, i, j)),
        out_shape=jax.ShapeDtypeStruct((2, T, D_FF), BF16), name=name,
        compiler_params=pltpu.CompilerParams(dimension_semantics=("parallel", "parallel")),
    )(df, wo, gu)


def _rope128(t, cos, sin_s):
    return t * cos + pltpu.roll(t, 64, 1) * sin_s


def _rope128_t(g, cos, sin_s):
    return g * cos - pltpu.roll(g, 64, 1) * sin_s


def _partner32(t):
    lane = lax.broadcasted_iota(jnp.int32, t.shape, 1)
    return jnp.where((lane & 32) == 0, pltpu.roll(t, 96, 1), pltpu.roll(t, 32, 1))


def _rope64(t, cos, sin_s):
    return t * cos + _partner32(t) * sin_s


def _rope64_t(g, cos, sin_s):
    return g * cos - _partner32(g) * sin_s


def _mixer_prep_fwd(proj, tabs, qg, kvg, T):
    cos128, sin128, cos64, sin64 = tabs

    def body(rq_ref, rk_ref, cq_ref, ckv_ref, kpe_ref, c1_ref, s1_ref, c2_ref, s2_ref, qg_ref, kvg_ref,
             oq_ref, ok_ref, oqn_ref, okvn_ref, okpe_ref):
        c1, s1 = c1_ref[...], s1_ref[...]
        for h in range(HEADS):
            sl = slice(h * RET_DK, (h + 1) * RET_DK)
            oq_ref[:, sl] = _rope128(rq_ref[:, sl].astype(F32), c1, s1).astype(BF16)
            ok_ref[:, sl] = (_rope128(rk_ref[:, sl].astype(F32), c1, s1) * RET_SCALE).astype(BF16)
        oqn_ref[...] = _rms(cq_ref[...].astype(F32), qg_ref[...]).astype(BF16)
        okvn_ref[...] = _rms(ckv_ref[...].astype(F32), kvg_ref[...]).astype(BF16)
        okpe_ref[...] = _rope64(kpe_ref[...].astype(F32), c2_ref[...], s2_ref[...]).astype(BF16)

    ins = [(proj, 1024, 0), (proj, 1024, 1), (proj, 256, P_CQ // 256), (proj, 256, P_CKV // 256),
           (proj, 128, P_KPE // 128), (cos128, 128, 0), (sin128, 128, 0), (cos64, 128, 0), (sin64, 128, 0),
           (qg, None, None), (kvg, None, None)]
    outs = [(1024, BF16), (1024, BF16), (LORA, BF16), (LORA, BF16), (128, BF16)]
    return _rows(body, T, 256, ins, outs, "mixer_prep_fwd")


def _rms_bwd(proj, dqn, dkvn, dkpe_all, tabs, qg, kvg, T):
    _, _, cos64, sin64 = tabs

    def body(cq_ref, ckv_ref, dq_ref, dkv_ref, dk_ref, c2_ref, s2_ref, qg_ref, kvg_ref, o_ref, dqg_ref, dkvg_ref):
        _, vjp = jax.vjp(_rms, cq_ref[...].astype(F32), qg_ref[...])
        dx, dg = vjp(dq_ref[...])
        o_ref[:, 0:LORA] = dx.astype(BF16)
        _accum(dqg_ref, dg)
        _, vjp = jax.vjp(_rms, ckv_ref[...].astype(F32), kvg_ref[...])
        dx, dg = vjp(dkv_ref[...])
        o_ref[:, P_CKV - P_CQ:P_CKV - P_CQ + LORA] = dx.astype(BF16)
        _accum(dkvg_ref, dg)
        g = dk_ref[:, 0:128]
        for h in range(1, HEADS):
            g = g + dk_ref[:, h * 128:(h + 1) * 128]
        lane = lax.broadcasted_iota(jnp.int32, g.shape, 1)
        g = jnp.where(lane < MLA_ROPE, g, 0.0)
        o_ref[:, P_KPE - P_CQ:P_KPE - P_CQ + 128] = _rope64_t(g, c2_ref[...], s2_ref[...]).astype(BF16)
        o_ref[:, P_KPE - P_CQ + 128:] = jnp.zeros((g.shape[0], P_W - P_KPE - 128), BF16)

    ins = [(proj, 256, P_CQ // 256), (proj, 256, P_CKV // 256), (dqn, LORA, 0), (dkvn, LORA, 0), (dkpe_all, 1024, 0),
           (cos64, 128, 0), (sin64, 128, 0), (qg, None, None), (kvg, None, None)]
    return _rows(body, T, 256, ins, [(P_W - P_CQ, BF16)], "rms_bwd", accs=[(1, LORA), (1, LORA)])


def _gn_gate_bwd(y, rg, g, d):
    xc = y - jnp.mean(y, axis=-1, keepdims=True)
    rstd = lax.rsqrt(jnp.mean(jnp.square(xc), axis=-1, keepdims=True) + EPS)
    xhat = xc * rstd
    sig = _sigmoid(rg)
    act = rg * sig
    dn = d * act
    drg = d * (xhat * g) * (sig + act * (1.0 - sig))
    dxh = dn * g
    dy = rstd * (dxh - jnp.mean(dxh, axis=-1, keepdims=True) - xhat * jnp.mean(dxh * xhat, axis=-1, keepdims=True))
    return dy, drg, jnp.sum(dn * xhat, axis=0, keepdims=True)


def _gn_gate(y, rg, g):
    mu = jnp.mean(y, axis=-1, keepdims=True)
    var = jnp.mean(jnp.square(y - mu), axis=-1, keepdims=True)
    return _silu(rg) * ((y - mu) * lax.rsqrt(var + EPS) * g)


def _q_assemble_epilogue(tabs):
    _, _, cos64, sin64 = tabs

    def fn(q, ins, outs, first):
        c_ref, s_ref = ins
        on_ref, op_ref = outs
        on_ref[...] = (q[:, :1024] * Q_PRESCALE).astype(BF16)
        c, s = c_ref[...], s_ref[...]
        lane = lax.broadcasted_iota(jnp.int32, c.shape, 1)
        for j in range(HEADS // 2):
            r = _rope64(q[:, 1024 + 128 * j:1024 + 128 * (j + 1)], c, s) * Q_PRESCALE
            op_ref[:, 256 * j:256 * j + 128] = jnp.where(lane < 64, r, 0.0).astype(BF16)
            op_ref[:, 256 * j + 128:256 * j + 256] = jnp.where(lane < 64, pltpu.roll(r, 64, 1), 0.0).astype(BF16)

    return (fn, [(cos64, 0, 128), (sin64, 0, 128)], [], ((BF16, 1024), (BF16, 1024)), ())


def _q_assemble_bwd(dqn, dqpe, tabs, T):
    _, _, cos64, sin64 = tabs

    def body(dn_ref, dp_ref, c_ref, s_ref, on_ref, op_ref):
        on_ref[...] = dn_ref[...].astype(BF16)
        c, s = c_ref[...], s_ref[...]
        lane = lax.broadcasted_iota(jnp.int32, c.shape, 1)
        for j in range(HEADS // 2):
            g = jnp.where(lane < 64, dp_ref[:, 256 * j:256 * j + 128], pltpu.roll(dp_ref[:, 256 * j + 128:256 * j + 256], 64, 1))
            op_ref[:, 128 * j:128 * (j + 1)] = _rope64_t(g, c, s).astype(BF16)

    return _rows(body, T, 256, [(dqn, 1024, 0), (dqpe, 1024, 0), (cos64, 128, 0), (sin64, 128, 0)],
                 [(1024, BF16), (512, BF16)], "q_assemble_bwd")


def _mix_fn(gr, gm, yr, ym):
    return _sigmoid(gr) * yr + _sigmoid(gm) * ym


def _mix_epilogue(proj, y_ret):
    def fn(r, ins, outs, first):
        gr_ref, gm_ref, yr_ref = ins
        ym_ref, mix_ref = outs
        ym_ref[...] = r.astype(BF16)
        mix_ref[...] = _mix_fn(gr_ref[...].astype(F32), gm_ref[...].astype(F32), yr_ref[...].astype(F32), r).astype(BF16)

    return (fn, [(proj, P_GR // D), (proj, P_GM // D), y_ret], [], (BF16, BF16), ())


def _mix_bwd_epilogue(proj, y_ret, y_mla):
    def fn(r, ins, outs, first):
        gr_ref, gm_ref, yr_ref, ym_ref = ins
        dgate_ref, dyr_ref, dym_ref = outs
        sr, sm = _sigmoid(gr_ref[...].astype(F32)), _sigmoid(gm_ref[...].astype(F32))
        dgate_ref[:, 0:D] = (r * yr_ref[...].astype(F32) * (sr * (1.0 - sr))).astype(BF16)
        dgate_ref[:, D:2 * D] = (r * ym_ref[...].astype(F32) * (sm * (1.0 - sm))).astype(BF16)
        dyr_ref[...] = (r * sr).astype(BF16)
        dym_ref[...] = (r * sm).astype(BF16)

    return (fn, [(proj, P_GR // D), (proj, P_GM // D), y_ret, y_mla], [], ((BF16, 2 * D), BF16, BF16), ())


def _head_epilogue(h3, pp, tgt, g, b):
    def fn(r, ins, outs, first):
        h_ref, pp_ref, t_ref, g_ref, b_ref = ins
        dh_ref, dgl_ref, dpp_ref, dg_ref, db_ref, loss_ref = outs

        sg, pp, gain = _sigmoid(r), pp_ref[...], g_ref[...]
        pre = ALPHA * h_ref[...] + sg * pp
        xc = pre - jnp.mean(pre, axis=-1, keepdims=True)
        rstd = lax.rsqrt(jnp.mean(jnp.square(xc), axis=-1, keepdims=True) + EPS)
        xhat = xc * rstd
        err = xhat * gain + b_ref[...] - t_ref[...]
        dy = err * (1.0 / D)
        dyg = dy * gain
        dpre = rstd * (dyg - jnp.mean(dyg, axis=-1, keepdims=True) - xhat * jnp.mean(dyg * xhat, axis=-1, keepdims=True))
        dh_ref[...] = ALPHA * dpre
        dgl_ref[...] = (dpre * pp * (sg * (1.0 - sg))).astype(BF16)
        dpp_ref[...] = (dpre * sg).astype(BF16)
        _accum(dg_ref, jnp.sum(dy * xhat, axis=0, keepdims=True), first)
        _accum(db_ref, jnp.sum(dy, axis=0, keepdims=True), first)
        part = 0.5 * jnp.sum(jnp.mean(jnp.square(err), axis=-1, keepdims=True), axis=0, keepdims=True)
        _accum(loss_ref, jnp.broadcast_to(part, loss_ref.shape), first)

    return (fn, [h3, pp, tgt], [g, b], (F32, BF16, BF16), (1, 1, 8))


def _delta_epilogue(o):
    def fn(r, ins, outs, first):
        (o_ref,) = ins
        db_ref, dl_ref = outs
        db_ref[...] = r.astype(BF16)
        for h in range(HEADS):
            sl = slice(h * MLA_DV, (h + 1) * MLA_DV)
            dl = jnp.sum(r[:, sl] * o_ref[:, sl].astype(F32), axis=-1, keepdims=True)
            dl_ref[:, sl] = jnp.broadcast_to(dl, (r.shape[0], MLA_DV))

    return (fn, [o], [], (BF16, F32), ())


def _ret_consts():
    L = RET_L
    lg = jnp.log(1.0 - 2.0 ** (-5.0 - jnp.arange(HEADS, dtype=F32)))[:, None, None]
    idx = jnp.arange(L, dtype=F32)
    ch = jnp.arange(L) // CHUNK
    dist = idx[:, None] - idx[None, :]
    same = (ch[:, None] == ch[None, :])[None]
    earlier = (ch[None, :] < ch[:, None])[None]
    dm = jnp.where(same, jnp.exp(lg * jnp.abs(dist)[None]), jnp.where(earlier, jnp.exp(lg * dist[None]), 0.0))
    xi = jnp.broadcast_to(jnp.exp(lg * (idx + 1.0)[None, :, None]), (HEADS, L, 128))
    zeta = jnp.broadcast_to(jnp.exp(lg * (L - 1.0 - idx)[None, :, None]), (HEADS, L, 128))
    gl = jnp.broadcast_to(jnp.exp(lg * float(L)), (HEADS, 8, 128))
    return dm.astype(F32), xi.astype(F32), zeta.astype(F32), gl.astype(F32)


def _whole(arr):
    return pl.BlockSpec(arr.shape, lambda n, _nd=arr.ndim: (0,) * _nd)


def _ret_fwd(q, k, v, proj, gn_g, consts, T):
    dm, xi, zeta, gl = consts
    L = RET_L
    n_sc = T // L

    def body(q_ref, k_ref, v_ref, rg_ref, g_ref, dm_ref, xi_ref, ze_ref, gl_ref, y_ref, yr_ref, s_ref, st_ref):
        @pl.when(pl.program_id(0) == 0)
        def _():
            st_ref[...] = jnp.zeros_like(st_ref)

        for h in range(HEADS):
            ks, vs = slice(h * RET_DK, (h + 1) * RET_DK), slice(h * RET_DV, (h + 1) * RET_DV)
            qq, kk, vv = q_ref[:, ks], k_ref[:, ks], v_ref[:, vs]
            st = st_ref[h]
            s_ref[h, 0] = st
            p = (_dot(qq, kk, 1, 1) * dm_ref[h]).astype(BF16)
            cross = _dot(qq, st.astype(BF16), 1, 0)
            xi_c = jnp.concatenate([xi_ref[h], xi_ref[h]], axis=1)
            y = _dot(p, vv, 1, 0) + cross * xi_c
            y_ref[:, vs] = y
            yr_ref[:, vs] = _gn_gate(y, rg_ref[:, vs].astype(F32), g_ref[:, vs]).astype(BF16)
            kz = (kk.astype(F32) * ze_ref[h]).astype(BF16)
            gl2 = jnp.concatenate([gl_ref[h, 0:1, :], gl_ref[h, 0:1, :]], axis=1)
            st_ref[h] = st * gl2 + _dot(kz, vv, 0, 0)

    return _pcall(
        body, grid=(n_sc,),
        in_specs=[pl.BlockSpec((L, 1024), lambda n: (n, 0)), pl.BlockSpec((L, 1024), lambda n: (n, 0)),
                  pl.BlockSpec((L, 2048), lambda n: (n, P_RV // 2048)), pl.BlockSpec((L, 2048), lambda n: (n, P_RG // 2048)),
                  _whole(gn_g), _whole(dm), _whole(xi), _whole(zeta), _whole(gl)],
        out_specs=[pl.BlockSpec((L, 2048), lambda n: (n, 0)), pl.BlockSpec((L, 2048), lambda n: (n, 0)),
                   pl.BlockSpec((HEADS, 1, 128, 256), lambda n: (0, n, 0, 0))],
        out_shape=[jax.ShapeDtypeStruct((T, HEADS * RET_DV), F32), jax.ShapeDtypeStruct((T, HEADS * RET_DV), BF16),
                   jax.ShapeDtypeStruct((HEADS, n_sc, 128, 256), F32)],
        scratch_shapes=[pltpu.VMEM((HEADS, 128, 256), F32)], name="ret_fwd",
        compiler_params=pltpu.CompilerParams(dimension_semantics=("arbitrary",)),
    )(q, k, v, proj, gn_g, dm, xi, zeta, gl)


def _ret_bwd(q, k, v, y, proj, gn_g, states, dyr, consts, tabs, T):
    dm, xi, zeta, gl = consts
    cos128, sin128, _, _ = tabs
    L = RET_L
    n_sc = T // L

    def body(q_ref, k_ref, v_ref, y_ref, rg_ref, g_ref, d_ref, s_ref, dm_ref, xi_ref, ze_ref, gl_ref, c_ref, sn_ref,
             dp_ref, dg_ref, gs_ref):
        @pl.when(pl.program_id(0) == 0)
        def _():
            gs_ref[...] = jnp.zeros_like(gs_ref)

        c, sn = c_ref[...], sn_ref[...]
        dgs = []
        for h in range(HEADS):
            ks, vs = slice(h * RET_DK, (h + 1) * RET_DK), slice(h * RET_DV, (h + 1) * RET_DV)
            dy, drg, dg = _gn_gate_bwd(y_ref[:, vs], rg_ref[:, vs].astype(F32), g_ref[:, vs], d_ref[:, vs].astype(F32))
            dp_ref[:, P_RG + h * RET_DV:P_RG + (h + 1) * RET_DV] = drg.astype(BF16)
            dgs.append(dg)
            qq, kk, vv, dyy = q_ref[:, ks], k_ref[:, ks], v_ref[:, vs], dy.astype(BF16)
            dmm = dm_ref[h]
            gb = gs_ref[h].astype(BF16)
            sb = s_ref[h, 0].astype(BF16)
            xi_c = jnp.concatenate([xi_ref[h], xi_ref[h]], axis=1)
            pb = (_dot(qq, kk, 1, 1) * dmm).astype(BF16)
            kz = (kk.astype(F32) * ze_ref[h]).astype(BF16)
            dp_ref[:, P_RV + h * RET_DV:P_RV + (h + 1) * RET_DV] = (_dot(pb, dyy, 0, 0) + _dot(kz, gb, 1, 0)).astype(BF16)
            da = (_dot(dyy, vv, 1, 1) * dmm).astype(BF16)
            dyx = (dyy.astype(F32) * xi_c).astype(BF16)
            dq = _dot(da, kk, 1, 0) + _dot(dyx, sb, 1, 1)
            dk = _dot(da, qq, 0, 0) + _dot(vv, gb, 1, 1) * ze_ref[h]
            dp_ref[:, P_RQ + h * RET_DK:P_RQ + (h + 1) * RET_DK] = _rope128_t(dq, c, sn).astype(BF16)
            dp_ref[:, P_RK + h * RET_DK:P_RK + (h + 1) * RET_DK] = (_rope128_t(dk, c, sn) * RET_SCALE).astype(BF16)
            gl2 = jnp.concatenate([gl_ref[h, 0:1, :], gl_ref[h, 0:1, :]], axis=1)
            gs_ref[h] = gs_ref[h] * gl2 + _dot(qq, dyx, 0, 0)
        _accum(dg_ref, jnp.concatenate(dgs, axis=1))

    rev = lambda n: n_sc - 1 - n
    return _pcall(
        body, grid=(n_sc,),
        in_specs=[pl.BlockSpec((L, 1024), lambda n: (rev(n), 0)), pl.BlockSpec((L, 1024), lambda n: (rev(n), 0)),
                  pl.BlockSpec((L, 2048), lambda n: (rev(n), P_RV // 2048)), pl.BlockSpec((L, 2048), lambda n: (rev(n), 0)),
                  pl.BlockSpec((L, 2048), lambda n: (rev(n), P_RG // 2048)), _whole(gn_g),
                  pl.BlockSpec((L, 2048), lambda n: (rev(n), 0)),
                  pl.BlockSpec((HEADS, 1, 128, 256), lambda n: (0, rev(n), 0, 0)),
                  _whole(dm), _whole(xi), _whole(zeta), _whole(gl),
                  pl.BlockSpec((L, 128), lambda n: (rev(n), 0)), pl.BlockSpec((L, 128), lambda n: (rev(n), 0))],
        out_specs=[pl.BlockSpec((L, P_GR), lambda n: (rev(n), 0)), pl.BlockSpec((1, 2048), lambda n: (0, 0))],
        out_shape=[jax.ShapeDtypeStruct((T, P_GR), BF16), jax.ShapeDtypeStruct((1, 2048), F32)],
        scratch_shapes=[pltpu.VMEM((HEADS, 128, 256), F32)], name="ret_bwd",
        compiler_params=pltpu.CompilerParams(dimension_semantics=("arbitrary",)),
    )(q, k, v, y, proj, gn_g, dyr, states, dm, xi, zeta, gl, cos128, sin128)


def _diag_mask(nrows, ncols, row0):
    row = row0 + lax.broadcasted_iota(jnp.int32, (nrows, ncols), 0)
    col = lax.broadcasted_iota(jnp.int32, (nrows, ncols), 1)
    return lax.shift_right_logical(col, 6) <= lax.shift_right_logical(row, 6)


def _diag_spans(t):
    h = t // 2
    return ((0, h, h), (h, h, t)) if h % 128 == 0 else ((0, t, t),)


def _tri_steps(nb, by_key):
    if by_key:
        pairs = [(i, j) for j in range(nb) for i in range(j, nb)]
    else:
        pairs = [(i, j) for i in range(nb) for j in range(i + 1)]
    return jnp.array([a for a, _ in pairs], jnp.int32), jnp.array([b for _, b in pairs], jnp.int32)


def _attn_fwd(qn, qpe, kv, kpe, T):
    t = min(ATT_TF, T)
    nb = T // t
    ii, jj = _tri_steps(nb, by_key=False)

    hp = ATT_HP
    w = 128 * hp

    def body(ii_ref, jj_ref, qn_ref, qp_ref, kn_ref, kp_ref, v_ref, o_ref, lse_ref, m_sc, l_sc, acc_sc):
        st = pl.program_id(1)
        i, j = ii_ref[st], jj_ref[st]

        @pl.when(j == 0)
        def _():
            m_sc[...] = jnp.full_like(m_sc, NEG)
            l_sc[...] = jnp.zeros_like(l_sc)
            acc_sc[...] = jnp.zeros_like(acc_sc)

        def update(diag):
            kp = kp_ref[...]
            spans = _diag_spans(t) if diag else ((0, t, t),)
            for hh in range(hp):
                sl = slice(128 * hh, 128 * (hh + 1))
                for row0, nr, nkeys in spans:
                    rows = slice(row0, row0 + nr)
                    q = jnp.concatenate([qn_ref[rows, sl], qp_ref[rows, sl]], axis=1)
                    k = jnp.concatenate([kn_ref[:nkeys, sl], kp[:nkeys]], axis=1)
                    s = _dot(q, k, 1, 1)
                    if diag:
                        s = jnp.where(_diag_mask(nr, nkeys, row0), s, NEG)
                    m_prev = m_sc[rows, sl]
                    m_new = jnp.maximum(m_prev, jnp.max(s, axis=1, keepdims=True))
                    a = jnp.exp2(m_prev - m_new)
                    p = jnp.exp2(s - m_new[:, 0:1])
                    l_sc[rows, sl] = a * l_sc[rows, sl] + jnp.sum(p, axis=1, keepdims=True)
                    acc_sc[rows, sl] = a * acc_sc[rows, sl] + _dot(p.astype(BF16), v_ref[:nkeys, sl], 1, 0)
                    m_sc[rows, sl] = m_new

        @pl.when(j < i)
        def _():
            update(False)

        @pl.when(j == i)
        def _():
            update(True)
            o_ref[...] = (acc_sc[...] / l_sc[...]).astype(BF16)
            lse_ref[...] = m_sc[...] + jnp.log2(l_sc[...])

    qs = pl.BlockSpec((t, w), lambda h, s, ii, jj: (ii[s], h))
    grid_spec = pltpu.PrefetchScalarGridSpec(
        num_scalar_prefetch=2, grid=(HEADS // hp, int(ii.shape[0])),
        in_specs=[qs, qs, pl.BlockSpec((t, w), lambda h, s, ii, jj: (jj[s], h)), pl.BlockSpec((t, 128), lambda h, s, ii, jj: (jj[s], 0)),
                  pl.BlockSpec((t, w), lambda h, s, ii, jj: (jj[s], HEADS // hp + h))],
        out_specs=[qs, qs],
        scratch_shapes=[pltpu.VMEM((t, w), F32), pltpu.VMEM((t, w), F32), pltpu.VMEM((t, w), F32)])
    return _pcall(
        body, grid_spec=grid_spec, out_shape=[jax.ShapeDtypeStruct((T, D), BF16), jax.ShapeDtypeStruct((T, D), F32)], name="attn_fwd",
        compiler_params=pltpu.CompilerParams(dimension_semantics=("arbitrary", "arbitrary")),
    )(ii, jj, qn, qpe, kv, kpe, kv)


def _attn_bwd(qn, qpe, kv, kpe, do, lse, delta, T):
    t = min(ATT_TB, T)
    nb = T // t
    ii, jj = _tri_steps(nb, by_key=True)

    def body(ii_ref, jj_ref, qn_ref, qp_ref, kn_ref, kp_ref, v_ref, do_ref, lse_ref, dl_ref,
             dqn_ref, dqp_ref, dkn_ref, dkp_ref, dv_ref, dk_sc, dv_sc):
        st = pl.program_id(1)
        i, j = ii_ref[st], jj_ref[st]

        @pl.when(st == 0)
        def _():
            dqn_ref[...] = jnp.zeros_like(dqn_ref)
            dqp_ref[...] = jnp.zeros_like(dqp_ref)

        @pl.when(i == j)
        def _():
            dk_sc[...] = jnp.zeros_like(dk_sc)
            dv_sc[...] = jnp.zeros_like(dv_sc)

        def update(diag):
            for row0, nr, nkeys in (_diag_spans(t) if diag else ((0, t, t),)):
                rows = slice(row0, row0 + nr)
                q = jnp.concatenate([qn_ref[rows, :], qp_ref[rows, :]], axis=1)
                k = jnp.concatenate([kn_ref[:nkeys, :], kp_ref[:nkeys, :]], axis=1)
                dob = do_ref[rows, :]
                s = _dot(q, k, 1, 1)
                if diag:
                    s = jnp.where(_diag_mask(nr, nkeys, row0), s, NEG)
                p = jnp.exp2(s - lse_ref[rows, 0:1])
                dv_sc[:nkeys, :] += _dot(p.astype(BF16), dob, 0, 0)
                dp = _dot(dob, v_ref[:nkeys, :], 1, 1)
                ds = (p * (dp - dl_ref[rows, 0:1])).astype(BF16)
                dk_sc[:nkeys, :] += _dot(ds, q, 0, 0)
                dq = _dot(ds, k, 1, 0) * MLA_SCALE
                grows = pl.ds(pl.multiple_of(i * t + row0, nr), nr)
                dqn_ref[grows, :] += dq[:, :128]
                dqp_ref[grows, :] += dq[:, 128:]

        @pl.when(i > j)
        def _():
            update(False)

        @pl.when(i == j)
        def _():
            update(True)

        @pl.when(i == nb - 1)
        def _():
            dkn_ref[...] = (dk_sc[:, :128] * (1.0 / LOG2E)).astype(BF16)
            dkp_ref[...] = dk_sc[:, 128:] * (1.0 / LOG2E)
            dv_ref[...] = dv_sc[...].astype(BF16)

    qs = pl.BlockSpec((t, 128), lambda h, s, ii, jj: (ii[s], h))
    ks = pl.BlockSpec((t, 128), lambda h, s, ii, jj: (jj[s], h))
    hs = pl.BlockSpec((T, 128), lambda h, s, ii, jj: (0, h))
    grid_spec = pltpu.PrefetchScalarGridSpec(
        num_scalar_prefetch=2, grid=(HEADS, int(ii.shape[0])),
        in_specs=[qs, qs, ks, pl.BlockSpec((t, 128), lambda h, s, ii, jj: (jj[s], 0)),
                  pl.BlockSpec((t, 128), lambda h, s, ii, jj: (jj[s], HEADS + h)), qs, qs, qs],
        out_specs=[hs, hs, ks, ks, ks],
        scratch_shapes=[pltpu.VMEM((t, 256), F32), pltpu.VMEM((t, 128), F32)])
    return _pcall(
        body, grid_spec=grid_spec,
        out_shape=[jax.ShapeDtypeStruct((T, D), F32), jax.ShapeDtypeStruct((T, D), F32), jax.ShapeDtypeStruct((T, D), BF16),
                   jax.ShapeDtypeStruct((T, D), F32), jax.ShapeDtypeStruct((T, D), BF16)], name="attn_bwd",
        compiler_params=pltpu.CompilerParams(dimension_semantics=("arbitrary", "arbitrary")),
    )(ii, jj, qn, qpe, kv, kpe, kv, do, lse, delta)


def _mesh_pos():
    x, y, c = lax.axis_index("x"), lax.axis_index("y"), lax.axis_index("c")
    return x, y, c, 4 * x + 2 * y + c


def _peer(x, y, c, k):
    px, py, pc = (x + ((k >> 2) & 1)) % 2, (y + ((k >> 1) & 1)) % 2, (c + (k & 1)) % 2
    return (px, py, pc), 4 * px + 2 * py + pc


_ANY = pl.BlockSpec(memory_space=pl.ANY)


def _rcopy(src, dst, send_sems, recv_sems, k, to):
    return pltpu.make_async_remote_copy(src_ref=src, dst_ref=dst, send_sem=send_sems.at[k], recv_sem=recv_sems.at[k],
                                        device_id=to, device_id_type=pl.DeviceIdType.MESH)


def _all_gather(wsh, ssh):
    def body(w_ref, s_ref, wall_ref, sall_ref, send_sems, recv_sems, loc_sems):
        x, y, c, me = _mesh_pos()
        sib = (x, y, 1 - c)
        chips = [(1 - x, y), (x, 1 - y), (1 - x, 1 - y)]
        slot = lambda px, py, pc: 4 * px + 2 * py + pc
        loc = [pltpu.make_async_copy(s_ref, sall_ref.at[me], loc_sems.at[0])]
        for cp in loc:
            cp.start()
        sends, fwd_waits = [], []
        for n, (src, dst) in enumerate(((w_ref, wall_ref), (s_ref, sall_ref))):
            o = 7 * n
            sends.append(_rcopy(src, dst.at[me], send_sems, recv_sems, o, sib))
            for j, chip in enumerate(chips):
                sends.append(_rcopy(src, dst.at[me], send_sems, recv_sems, o + 1 + j, (*chip, c)))
        for cp in sends:
            cp.start()
        for n, (src, dst) in enumerate(((w_ref, wall_ref), (s_ref, sall_ref))):
            o = 7 * n
            for j, chip in enumerate(chips):
                got = dst.at[slot(*chip, c)]
                _rcopy(src, got, send_sems, recv_sems, o + 1 + j, sib).wait_recv()
                fw = _rcopy(got, got, send_sems, recv_sems, o + 4 + j, sib)
                fw.start()
                sends.append(fw)
            fwd_waits.append(_rcopy(src, dst.at[slot(x, y, 1 - c)], send_sems, recv_sems, o, sib))
            for j, chip in enumerate(chips):
                fwd_waits.append(_rcopy(src, dst.at[slot(*chip, 1 - c)], send_sems, recv_sems, o + 4 + j, sib))
        for cp in fwd_waits:
            cp.wait_recv()
        for cp in sends:
            cp.wait_send()
        for cp in loc:
            cp.wait()

    wall, sall = _pcall(
        body, in_specs=[_ANY, _ANY], out_specs=[_ANY, _ANY],
        out_shape=[jax.ShapeDtypeStruct((N_DEV,) + wsh.shape, wsh.dtype), jax.ShapeDtypeStruct((N_DEV,) + ssh.shape, ssh.dtype)],
        scratch_shapes=[pltpu.SemaphoreType.DMA((14,)), pltpu.SemaphoreType.DMA((14,)), pltpu.SemaphoreType.DMA((1,))],
        name="all_gather_weights",
    )(wsh, ssh)
    me = 4 * lax.axis_index("x") + 2 * lax.axis_index("y") + lax.axis_index("c")
    return lax.dynamic_update_index_in_dim(wall, wsh, me, axis=0), sall


_HBM = pl.BlockSpec(memory_space=pltpu.HBM)
_SEM = pl.BlockSpec(memory_space=pltpu.SEMAPHORE)
_EFFECT = pltpu.SideEffectType.DATAFLOW_SIDE_EFFECTING


def _other_chips(x, y):
    return [(1 - x, y), (x, 1 - y), (1 - x, 1 - y)]


def _gather_start(wsh, order_after):
    def body(w_ref, land_ref, dep_ref, send_sems, recv_sems, w_thru, land_thru, token):
        x, y, c, me = _mesh_pos()
        _rcopy(w_ref, land_ref.at[me], send_sems, recv_sems, 0, (x, y, 1 - c)).start()
        for j, chip in enumerate(_other_chips(x, y)):
            _rcopy(w_ref, land_ref.at[me], send_sems, recv_sems, 1 + j, (*chip, c)).start()
        token[...] = jnp.zeros_like(token)

    shape = (N_DEV,) + wsh.shape
    land = pltpu.with_memory_space_constraint(lax.empty(shape, wsh.dtype), pltpu.HBM)
    return _pcall(
        body, name="gather_start",
        out_shape=(pltpu.SemaphoreType.DMA((4,)), pltpu.SemaphoreType.DMA((4,)), pltpu.HBM(wsh.shape, wsh.dtype),
                   pltpu.HBM(shape, wsh.dtype), jax.ShapeDtypeStruct((8, 128), F32)),
        in_specs=(_HBM, _HBM, _ANY), out_specs=(_SEM, _SEM, _HBM, _HBM, pl.BlockSpec(memory_space=pltpu.VMEM)),
        input_output_aliases={0: 2, 1: 3}, compiler_params=pltpu.CompilerParams(has_side_effects=_EFFECT),
    )(pltpu.with_memory_space_constraint(wsh, pltpu.HBM), land, order_after)


def _gather_wait(send_sems, recv_sems, w_thru, land_thru, after):
    def body(w_ref, land_ref, send_sems, recv_sems, after_ref, w_out, land_out):
        x, y, c, _ = _mesh_pos()
        senders = [(x, y, 1 - c)] + [(*chip, c) for chip in _other_chips(x, y)]
        for k, (px, py, pc) in enumerate(senders):
            cp = _rcopy(w_ref, land_ref.at[4 * px + 2 * py + pc], send_sems, recv_sems, k, (px, py, pc))
            cp.wait_send()
            cp.wait_recv()

    return _pcall(
        body, name="gather_wait", out_shape=(pltpu.HBM(w_thru.shape, w_thru.dtype), pltpu.HBM(land_thru.shape, land_thru.dtype)),
        in_specs=(_HBM, _HBM, _SEM, _SEM, _ANY), out_specs=(_HBM, _HBM), input_output_aliases={0: 0, 1: 1},
        compiler_params=pltpu.CompilerParams(has_side_effects=_EFFECT),
    )(w_thru, land_thru, send_sems, recv_sems, after)


def _gather_finish(wsh, land):
    def body(land_ref, out_ref, send_sems, recv_sems):
        x, y, c, _ = _mesh_pos()
        sib = (x, y, 1 - c)
        sends, recvs = [], []
        for j, (px, py) in enumerate(_other_chips(x, y)):
            mine, theirs = 4 * px + 2 * py + c, 4 * px + 2 * py + (1 - c)
            sends.append(_rcopy(land_ref.at[mine], out_ref.at[mine], send_sems, recv_sems, j, sib))
            recvs.append(_rcopy(land_ref.at[theirs], out_ref.at[theirs], send_sems, recv_sems, j, sib))
        for cp in sends:
            cp.start()
        for cp in recvs:
            cp.wait_recv()
        for cp in sends:
            cp.wait_send()

    done = _pcall(
        body, in_specs=[_ANY], out_specs=_ANY, out_shape=jax.ShapeDtypeStruct(land.shape, land.dtype),
        input_output_aliases={0: 0},
        scratch_shapes=[pltpu.SemaphoreType.DMA((3,)), pltpu.SemaphoreType.DMA((3,))], name="gather_finish",
    )(land)
    me = 4 * lax.axis_index("x") + 2 * lax.axis_index("y") + lax.axis_index("c")
    return lax.dynamic_update_index_in_dim(done, wsh, me, axis=0)


_ROW_TILES = (400, 368, 352, 256, 128, 16, 8)


def _exchange_sibling(g4, name):
    def body(g_ref, r_ref, send_sems, recv_sems):
        x, y, c, _ = _mesh_pos()
        sib = (x, y, 1 - c)
        cps = [_rcopy(g_ref.at[q, 1 - c], r_ref.at[q], send_sems, recv_sems, q, sib) for q in range(4)]
        for cp in cps:
            cp.start()
        for cp in cps:
            cp.wait()

    n, _, R, C = g4.shape
    return _pcall(
        body, in_specs=[_ANY], out_specs=_ANY, out_shape=jax.ShapeDtypeStruct((n, R, C), g4.dtype),
        scratch_shapes=[pltpu.SemaphoreType.DMA((4,)), pltpu.SemaphoreType.DMA((4,))], name=name,
    )(g4)


def _sum_sibling(g4, rsib, cvec, name):
    n, _, R, C = g4.shape
    tr = _pick(R, _ROW_TILES)

    def body(c_ref, g_ref, r_ref, o_ref):
        o_ref[...] = (g_ref[...].astype(F32) + r_ref[...].astype(F32)).astype(o_ref.dtype)

    grid_spec = pltpu.PrefetchScalarGridSpec(
        num_scalar_prefetch=1, grid=(n, R // tr),
        in_specs=[pl.BlockSpec((None, None, tr, C), lambda q, i, cr: (q, cr[0], i, 0)), pl.BlockSpec((None, tr, C), lambda q, i, cr: (q, i, 0))],
        out_specs=pl.BlockSpec((None, tr, C), lambda q, i, cr: (q, i, 0)))
    return _pcall(body, grid_spec=grid_spec, out_shape=jax.ShapeDtypeStruct((n, R, C), g4.dtype), name=name)(cvec, g4, rsib)


def _chips_start(part, name, order_after=None):
    n_in = 2 + int(order_after is not None)

    def body(*refs):
        p_ref, land_ref = refs[:2]
        send_sems, recv_sems, token = refs[n_in], refs[n_in + 1], refs[n_in + 4]
        x, y, c, _ = _mesh_pos()
        myq = 2 * x + y
        for j, (px, py) in enumerate(_other_chips(x, y)):
            _rcopy(p_ref.at[2 * px + py], land_ref.at[myq], send_sems, recv_sems, j, (px, py, c)).start()
        token[...] = jnp.zeros_like(token)

    land = pltpu.with_memory_space_constraint(lax.empty(part.shape, part.dtype), pltpu.HBM)
    extra = () if order_after is None else (order_after,)
    return _pcall(
        body, name=name,
        out_shape=(pltpu.SemaphoreType.DMA((3,)), pltpu.SemaphoreType.DMA((3,)), pltpu.HBM(part.shape, part.dtype),
                   pltpu.HBM(part.shape, part.dtype), jax.ShapeDtypeStruct((8, 128), F32)),
        in_specs=(_HBM, _HBM) + (_ANY,) * len(extra),
        out_specs=(_SEM, _SEM, _HBM, _HBM, pl.BlockSpec(memory_space=pltpu.VMEM)),
        input_output_aliases={0: 2, 1: 3}, compiler_params=pltpu.CompilerParams(has_side_effects=_EFFECT),
    )(pltpu.with_memory_space_constraint(part, pltpu.HBM), land, *extra)


def _chips_wait(send_sems, recv_sems, p_thru, land_thru, after, name):
    def body(p_ref, land_ref, send_sems, recv_sems, after_ref, p_out, land_out):
        x, y, c, _ = _mesh_pos()
        for j, (px, py) in enumerate(_other_chips(x, y)):
            q = 2 * px + py
            cp = _rcopy(p_ref.at[q], land_ref.at[q], send_sems, recv_sems, j, (px, py, c))
            cp.wait_send()
            cp.wait_recv()

    return _pcall(
        body, name=name, out_shape=(pltpu.HBM(p_thru.shape, p_thru.dtype), pltpu.HBM(p_thru.shape, p_thru.dtype)),
        in_specs=(_HBM, _HBM, _SEM, _SEM, _ANY), out_specs=(_HBM, _HBM), input_output_aliases={0: 0, 1: 1},
        compiler_params=pltpu.CompilerParams(has_side_effects=_EFFECT),
    )(p_thru, land_thru, send_sems, recv_sems, after)


def _sum_chips(part, land, qvec, name):
    n, R, C = part.shape
    tr = _pick(R, _ROW_TILES)

    def body(q_ref, p_ref, *refs):
        o_ref = refs[n]
        acc = None
        for q in range(n):
            term = jnp.where(q_ref[0] == q, p_ref[...], refs[q][...]).astype(F32)
            acc = term if acc is None else acc + term
        o_ref[...] = acc

    def land_spec(q):
        return pl.BlockSpec((None, tr, C), lambda i, m: (jnp.where(m[0] == q, (q + 1) % n, q), i, 0))

    grid_spec = pltpu.PrefetchScalarGridSpec(
        num_scalar_prefetch=1, grid=(R // tr,),
        in_specs=[pl.BlockSpec((None, tr, C), lambda i, m: (m[0], i, 0))] + [land_spec(q) for q in range(n)],
        out_specs=pl.BlockSpec((tr, C), lambda i, m: (i, 0)))
    return _pcall(body, grid_spec=grid_spec, out_shape=jax.ShapeDtypeStruct((R, C), F32), name=name)(qvec, part, *([land] * n))


def _exchange_small(gsmall):
    def body(s_ref, srecv_ref, send_sems, recv_sems, loc_sem):
        x, y, c, me = _mesh_pos()
        loc = pltpu.make_async_copy(s_ref, srecv_ref.at[me], loc_sem.at[0])
        loc.start()
        sends, recvs = [], []
        for k in range(1, N_DEV):
            to, pidx = _peer(x, y, c, k)
            sends.append(_rcopy(s_ref, srecv_ref.at[me], send_sems, recv_sems, k - 1, to))
            recvs.append(_rcopy(s_ref, srecv_ref.at[pidx], send_sems, recv_sems, k - 1, to))
        for cp in sends:
            cp.start()
        for cp in recvs:
            cp.wait_recv()
        for cp in sends:
            cp.wait_send()
        loc.wait()

    return _pcall(
        body, in_specs=[_ANY], out_specs=_ANY, out_shape=jax.ShapeDtypeStruct((N_DEV,) + gsmall.shape, gsmall.dtype),
        scratch_shapes=[pltpu.SemaphoreType.DMA((7,)), pltpu.SemaphoreType.DMA((7,)), pltpu.SemaphoreType.DMA((1,))],
        name="exchange_small",
    )(gsmall)


def _sum_slots(recv, name):
    n, R, C = recv.shape
    tr = _pick(R, _ROW_TILES)

    def body(r_ref, o_ref):
        acc = r_ref[0].astype(F32)
        for s in range(1, n):
            acc = acc + r_ref[s].astype(F32)
        o_ref[...] = acc

    return _pcall(body, grid=(R // tr,), in_specs=[pl.BlockSpec((n, tr, C), lambda i: (0, i, 0))],
                  out_specs=pl.BlockSpec((tr, C), lambda i: (i, 0)), out_shape=jax.ShapeDtypeStruct((R, C), F32), name=name)(recv)


def _adamw(w, g, m, v, name, order_after=None):
    shape = w.shape
    w2, g2, m2, v2 = (a.reshape(-1, shape[-1]) for a in (w, g, m, v))
    R, C = w2.shape
    tr = _pick(R, (256, 128, 64, 32, 16, 8)) if R > 256 else R
    n_in = 4 + int(order_after is not None)

    def body(*refs):
        w_ref, g_ref, m_ref, v_ref = refs[:4]
        d_ref, nm_ref, nv_ref = refs[n_in:]
        gg = g_ref[...]
        nm = ADAM_B1 * m_ref[...] + (1.0 - ADAM_B1) * gg
        nv = ADAM_B2 * v_ref[...] + (1.0 - ADAM_B2) * jnp.square(gg)
        m_hat = nm / (1.0 - ADAM_B1 ** ADAM_STEP)
        v_hat = nv / (1.0 - ADAM_B2 ** ADAM_STEP)
        d_ref[...] = -ADAM_LR * (m_hat / (jnp.sqrt(v_hat) + ADAM_EPS) + ADAM_WD * w_ref[...])
        nm_ref[...] = nm
        nv_ref[...] = nv

    spec = pl.BlockSpec((tr, C), lambda i: (i, 0))
    in_specs, args = [spec] * 4, [w2, g2, m2, v2]
    if order_after is not None:
        in_specs.append(pl.BlockSpec(memory_space=pl.ANY))
        args.append(order_after)
    d, nm, nv = _pcall(body, grid=(R // tr,), in_specs=in_specs, out_specs=[spec] * 3,
                       out_shape=[jax.ShapeDtypeStruct((R, C), F32)] * 3, name=name)(*args)
    return d.reshape(shape), nm.reshape(shape), nv.reshape(shape)


def _to_rows(name, w):
    w = w[0]
    if name in ("ffn1_w_in", "ffn2_w_in", "w_in"):
        return w.T
    if name in ("w_uq", "w_ukv", "ple_w_proj"):
        return w.T.reshape(-1, 1024)
    return w


def _from_rows(name, g, shape):
    if name in ("ffn1_w_in", "ffn2_w_in", "w_in"):
        return g.T.reshape(shape)
    if name in ("w_uq", "w_ukv", "ple_w_proj"):
        return g.reshape(-1, shape[1]).T.reshape(shape)
    return g.reshape(shape)


def _unpack(wall, group):
    out, off = {}, 0
    for name, r in group:
        out[name] = wall[:, off:off + r, :].reshape(N_DEV * r, 1024)
        off += _pad16(r)
    return out


def _w_in_internal(wt):
    return jnp.concatenate([wt[0:6144], wt[6720:8768], wt[6144:6720], jnp.zeros((P_W - W_IN_COLS, 1024), wt.dtype)], axis=0)


def _rope_tables(positions):
    pos = positions[0].astype(F32)

    def cs(half):
        inv = jnp.tile(ROPE_BASE ** (-jnp.arange(half, dtype=F32) / half), 128 // half)
        sign = jnp.tile(jnp.concatenate([-jnp.ones((half,), F32), jnp.ones((half,), F32)]), 64 // half)
        ang = pos[:, None] * inv
        return jnp.cos(ang), jnp.sin(ang) * sign

    return cs(64) + cs(32)


def _local_step(x, p, positions, target, W, ln_g, ln_b, gn_g, qg, kvg, early=None, rest_weights=None, start_token=None):
    T = x.shape[0]
    tabs = _rope_tables(positions)
    rc = _ret_consts()
    lg = [ln_g[i:i + 1] for i in range(4)]
    lb = [ln_b[i:i + 1] for i in range(4)]
    pb = p.astype(BF16)

    hs1, gu1, *xb = _ffn_in(x, W["ffn1_w_in"], "ffn1_in", order_after=start_token)
    xb = xb[0] if xb else x
    f1, h1, h1b = _mm(hs1, W["ffn1_w_out"], name="ffn1_out", tm=LN_TM, epilogue=_ln_epilogue(x, 0.5, lg[0], lb[0]))
    if rest_weights is not None:
        W = {**W, **rest_weights(h1b)}
    w_in_t = _w_in_internal(W["w_in"])
    wuq = W["w_uq"].reshape(1536, LORA).reshape(HEADS, 192, LORA)
    wuq = jnp.concatenate([wuq[:, :128].reshape(1024, LORA), wuq[:, 128:].reshape(512, LORA)], axis=0)
    wukv = W["w_ukv"].reshape(2048, LORA).reshape(HEADS, 2, 128, LORA).transpose(1, 0, 2, 3).reshape(2048, LORA)
    wp_t = W["ple_w_proj"].reshape(1024, D_PLE)
    proj = _mm(h1b, w_in_t, tb=True, out_dtype=BF16, name="mixer_in")
    rq, rk, qn, kvn, kpe = _mixer_prep_fwd(proj, tabs, qg, kvg, T)
    y, yr, states = _ret_fwd(rq, rk, proj, proj, gn_g, rc, T)
    y_ret = _mm(yr, W["w_ret_o"], out_dtype=BF16, name="ret_o")
    qnope, qpe = _mm(qn, wuq, tb=True, name="mla_uq", tm=LN_TM, tn=1536, epilogue=_q_assemble_epilogue(tabs))
    kv = _mm(kvn, wukv, tb=True, out_dtype=BF16, name="mla_ukv")
    o, lse = _attn_fwd(qnope, qpe, kv, kpe, T)
    y_mla, mix = _mm(o, W["w_mla_o"], name="mla_o", tm=LN_TM, epilogue=_mix_epilogue(proj, y_ret))
    mixed, h2, h2b = _mm(mix, W["w_out"], name="mixer_out", tm=LN_TM, epilogue=_ln_epilogue(h1, 1.0, lg[1], lb[1]))
    hs2, gu2 = _ffn_in(h2b, W["ffn2_w_in"], "ffn2_in")
    f2, h3, h3b = _mm(hs2, W["ffn2_w_out"], name="ffn2_out", tm=LN_TM, epilogue=_ln_epilogue(h2, 0.5, lg[2], lb[2]))
    pp = _mm(pb, wp_t, tb=True, name="ple_proj")

    G = {}
    dh3_a, dgl, dpp, dg3, db3, loss = _mm(h3b, W["ple_w_gate"], name="ple_gate", tm=LN_TM,
                                          epilogue=_head_epilogue(h3, pp, target, lg[3], lb[3]))
    G["ple_w_gate"] = _mm(h3b, dgl, ta=True, name="d_ple_gate")
    G["ple_w_proj"] = _mm(dpp, pb, ta=True, name="d_ple_proj")
    dh2_a, df2, dg2, db2 = _mm(dgl, W["ple_w_gate"], tb=True, add=dh3_a, name="dh3", tm=LN_TM,
                               epilogue=_ln_bwd_epilogue(h2, f2, 0.5, lg[2], lb[2]))
    G["ffn2_w_out"] = _mm(hs2, df2, ta=True, name="d_ffn2_out")
    da2 = _ffn_act_bwd(df2, W["ffn2_w_out"], gu2, "ffn2_act_bwd")
    G["ffn2_w_in"] = _mm(da2, h2b, ta=True, name="d_ffn2_in")
    dh1_a, dmixed, dg1, db1 = _mm(da2, W["ffn2_w_in"], add=dh2_a, name="dh2", tm=LN_TM,
                                  epilogue=_ln_bwd_epilogue(h1, mixed, 1.0, lg[1], lb[1]))
    G["w_out"] = _mm(mix, dmixed, ta=True, name="d_mixer_out")
    dp_gate, dy_ret, dy_mla = _mm(dmixed, W["w_out"], tb=True, name="dmix", tm=LN_TM,
                                  epilogue=_mix_bwd_epilogue(proj, y_ret, y_mla))
    G["w_mla_o"] = _mm(o, dy_mla, ta=True, name="d_mla_o")
    dob, delta = _mm(dy_mla, W["w_mla_o"], tb=True, name="do", tm=LN_TM, epilogue=_delta_epilogue(o))
    dqn_f, dqpe_f, dkn, dkpe_all, dv = _attn_bwd(qnope, qpe, kv, kpe, dob, lse, delta, T)
    dq_n, dq_r = _q_assemble_bwd(dqn_f, dqpe_f, tabs, T)
    g_uq = jnp.concatenate([_mm(dq_n, qn, ta=True, name="d_uq_nope"), _mm(dq_r, qn, ta=True, name="d_uq_rope")], axis=0)
    g_uq = jnp.concatenate([g_uq[:1024].reshape(HEADS, 128, LORA), g_uq[1024:].reshape(HEADS, 64, LORA)], axis=1)
    G["w_uq"] = g_uq.reshape(1536 * LORA // 1024, 1024)
    dqn = _mm(dq_r, wuq[1024:], add=_mm(dq_n, wuq[:1024], name="dqn_a"), name="dqn_b")
    g_ukv = jnp.stack([_mm(dkn, kvn, ta=True, name="d_ukv_k"), _mm(dv, kvn, ta=True, name="d_ukv_v")], axis=0)
    G["w_ukv"] = g_ukv.reshape(2, HEADS, 128, LORA).transpose(1, 0, 2, 3).reshape(2048 * LORA // 1024, 1024)
    dkvn = _mm(dv, wukv[1024:], add=_mm(dkn, wukv[:1024], name="dkvn_a"), name="dkvn_b")
    dp_mla, dqg, dkvg = _rms_bwd(proj, dqn, dkvn, dkpe_all, tabs, qg, kvg, T)
    G["w_ret_o"] = _mm(yr, dy_ret, ta=True, name="d_ret_o")
    dyr = _mm(dy_ret, W["w_ret_o"], tb=True, out_dtype=BF16, name="dyr")
    dp_ret, dgn = _ret_bwd(rq, rk, proj, y, proj, gn_g, states, dyr, rc, tabs, T)
    g_ret, g_gate, g_mla = (_mm(dp, h1b, ta=True, name="d_mixer_in_" + n)
                            for dp, n in ((dp_ret, "ret"), (dp_gate, "gate"), (dp_mla, "mla")))
    G["w_in"] = jnp.concatenate([g_ret, g_mla[:W_IN_COLS - P_CQ], g_gate], axis=0)
    lg0 = lg[0] if early is None else lg[0] + early(G)[0:1, 0:1]
    dh1 = _mm_pieces([(dp_ret, w_in_t[:P_GR]), (dp_gate, w_in_t[P_GR:P_CQ]), (dp_mla, w_in_t[P_CQ:])], dh1_a, "dh1")
    dx_a, df1, dg0, db0 = _ln_bwd_rows(x, f1, 0.5, lg0, lb[0], dh1, T, "ln0_bwd")
    G["ffn1_w_out"] = _mm(hs1, df1, ta=True, name="d_ffn1_out")
    da1 = _ffn_act_bwd(df1, W["ffn1_w_out"], gu1, "ffn1_act_bwd")
    G["ffn1_w_in"] = _mm(da1, xb, ta=True, name="d_ffn1_in")
    grad_x = _mm(da1, W["ffn1_w_in"], add=dx_a, name="grad_x")

    small = dict(ln_g=jnp.concatenate([dg0, dg1, dg2, dg3], axis=0), ln_b=jnp.concatenate([db0, db1, db2, db3], axis=0),
                 ret_gn_g=dgn, q_norm_g=dqg, kv_norm_g=dkvg)
    return loss, grad_x, G, small


def kernel(x, p, positions, ln_g, ln_b, ffn1_w_in, ffn1_w_out, w_in, ret_gn_g, w_ret_o, q_norm_g, kv_norm_g, w_uq, w_ukv, w_mla_o, w_out, ffn2_w_in, ffn2_w_out, ple_w_gate, ple_w_proj, loss_target, m_ln_g, m_ln_b, m_ffn1_w_in, m_ffn1_w_out, m_w_in, m_ret_gn_g, m_w_ret_o, m_q_norm_g, m_kv_norm_g, m_w_uq, m_w_ukv, m_w_mla_o, m_w_out, m_ffn2_w_in, m_ffn2_w_out, m_ple_w_gate, m_ple_w_proj, v_ln_g, v_ln_b, v_ffn1_w_in, v_ffn1_w_out, v_w_in, v_ret_gn_g, v_w_ret_o, v_q_norm_g, v_kv_norm_g, v_w_uq, v_w_ukv, v_w_mla_o, v_w_out, v_ffn2_w_in, v_ffn2_w_out, v_ple_w_gate, v_ple_w_proj):
    names = ("ln_g", "ln_b", "ffn1_w_in", "ffn1_w_out", "w_in", "ret_gn_g", "w_ret_o", "q_norm_g", "kv_norm_g", "w_uq", "w_ukv",
             "w_mla_o", "w_out", "ffn2_w_in", "ffn2_w_out", "ple_w_gate", "ple_w_proj")
    ws = dict(zip(names, (ln_g, ln_b, ffn1_w_in, ffn1_w_out, w_in, ret_gn_g, w_ret_o, q_norm_g, kv_norm_g, w_uq, w_ukv, w_mla_o,
                          w_out, ffn2_w_in, ffn2_w_out, ple_w_gate, ple_w_proj)))
    ms = dict(zip(names, (m_ln_g, m_ln_b, m_ffn1_w_in, m_ffn1_w_out, m_w_in, m_ret_gn_g, m_w_ret_o, m_q_norm_g, m_kv_norm_g, m_w_uq,
                          m_w_ukv, m_w_mla_o, m_w_out, m_ffn2_w_in, m_ffn2_w_out, m_ple_w_gate, m_ple_w_proj)))
    vs = dict(zip(names, (v_ln_g, v_ln_b, v_ffn1_w_in, v_ffn1_w_out, v_w_in, v_ret_gn_g, v_w_ret_o, v_q_norm_g, v_kv_norm_g, v_w_uq,
                          v_w_ukv, v_w_mla_o, v_w_out, v_ffn2_w_in, v_ffn2_w_out, v_ple_w_gate, v_ple_w_proj)))

    parts = []
    for name, r in PACK:
        rows = _to_rows(name, ws[name])
        if _pad16(r) != r:
            rows = jnp.concatenate([rows, jnp.zeros((_pad16(r) - r, 1024), F32)], axis=0)
        parts.append(rows)
    wsh_first = jnp.concatenate(parts[:len(PACK_LATE)], axis=0).astype(BF16)
    wsh_rest = jnp.concatenate(parts[len(PACK_LATE):], axis=0).astype(BF16)
    ssh = jnp.concatenate([ln_g[0], ln_b[0]], axis=0)
    wall_first, sall = _all_gather(wsh_first, ssh)
    *gather_handles, start_token = _gather_start(wsh_rest, wall_first)
    W = _unpack(wall_first, PACK_LATE)
    ln_full = sall.reshape(N_DEV, 2, 4, 128).transpose(1, 2, 0, 3).reshape(2, 4, 1024)

    def rest_weights(after):
        w_thru, land = _gather_wait(*gather_handles, after)
        return _unpack(_gather_finish(w_thru, land), PACK_EARLY)

    cvec = lax.axis_index("c").astype(jnp.int32).reshape(1)
    qvec = (2 * lax.axis_index("x") + lax.axis_index("y")).astype(jnp.int32).reshape(1)

    def chip_partials(G, group, tag):
        gparts = []
        for name, r in group:
            g = G[name].reshape(N_DEV, r, 1024)
            if _pad16(r) != r:
                g = jnp.concatenate([g, jnp.zeros((N_DEV, _pad16(r) - r, 1024), g.dtype)], axis=1)
            gparts.append(g)
        gfull = jnp.concatenate(gparts, axis=1).astype(BF16)
        g4 = gfull.reshape(4, 2, gfull.shape[1], 1024)
        return _sum_sibling(g4, _exchange_sibling(g4, "exchange_sibling_" + tag), cvec, "sum_sibling_" + tag)

    in_flight = []

    def early(G):
        *handles, token = _chips_start(chip_partials(G, PACK_EARLY, "early"), "chips_start_early")
        in_flight.append(handles)
        return token

    loss_p, grad_x, G, small = _local_step(x[0], p[0, 0], positions, loss_target[0], W, ln_full[0], ln_full[1],
                                           ret_gn_g, q_norm_g, kv_norm_g, early=early, rest_weights=rest_weights,
                                           start_token=start_token)

    part_e, land_e = _chips_wait(*in_flight[0], grad_x, "chips_wait_early")
    gsh_early = _sum_chips(part_e, land_e, qvec, "sum_grads_early")
    pad256 = lambda a: jnp.concatenate([a, jnp.zeros((1, 1024 - a.shape[1]), F32)], axis=1)
    gsmall = jnp.concatenate([small["ln_g"], small["ln_b"], small["ret_gn_g"].reshape(2, 1024), pad256(small["q_norm_g"]),
                              pad256(small["kv_norm_g"]), pad256(loss_p[0:1, 0:128]),
                              jnp.zeros((SMALL_ROWS - 13, 1024), F32)], axis=0)
    srecv = _exchange_small(gsmall)
    part_l = chip_partials(G, PACK_LATE, "late")
    *late_handles, late_token = _chips_start(part_l, "chips_start_late", order_after=srecv)
    ssum = _sum_slots(srecv, "sum_small_grads")

    def unpack_grads(group, gsh):
        out, off = {}, 0
        for name, r in group:
            out[name] = _from_rows(name, gsh[off:off + r], ws[name].shape)
            off += _pad16(r)
        return out

    grads = unpack_grads(PACK_EARLY, gsh_early)
    me = 4 * lax.axis_index("x") + 2 * lax.axis_index("y") + lax.axis_index("c")
    grads["ln_g"] = lax.dynamic_slice(ssum[0:4], (0, me * 128), (4, 128)).reshape(1, 4, 128)
    grads["ln_b"] = lax.dynamic_slice(ssum[4:8], (0, me * 128), (4, 128)).reshape(1, 4, 128)
    grads["ret_gn_g"] = ssum[8:10].reshape(1, 2048)
    grads["q_norm_g"] = ssum[10:11, :256]
    grads["kv_norm_g"] = ssum[11:12, :256]

    delta, new_m, new_v = {}, {}, {}
    late_names = [n for n, _ in PACK_LATE]
    last = late_token
    for name in names:
        if name not in late_names:
            delta[name], new_m[name], new_v[name] = _adamw(ws[name], grads[name], ms[name], vs[name], "adamw_" + name,
                                                           order_after=last)
            last = new_v[name]
    part_l, land_l = _chips_wait(*late_handles, last, "chips_wait_late")
    grads.update(unpack_grads(PACK_LATE, _sum_chips(part_l, land_l, qvec, "sum_grads_late")))
    for name in late_names:
        delta[name], new_m[name], new_v[name] = _adamw(ws[name], grads[name], ms[name], vs[name], "adamw_" + name)

    return (ssum[12, 0], grad_x[None], *[grads[n] for n in names], *[delta[n] for n in names],
            *[new_m[n] for n in names], *[new_v[n] for n in names])
```

```python
import math

import jax
import jax.numpy as jnp
from jax import lax
from jax.experimental import pallas as pl
from jax.experimental.pallas import tpu as pltpu

F32 = jnp.float32
BF16 = jnp.bfloat16

N_DEV = 8
D = 1024
D_FF = 2816
D_PLE = 256
CHUNK = 64
HEADS = 8
RET_DK = 128
RET_DV = 256
MLA_NOPE = 128
MLA_ROPE = 64
MLA_DV = 128
LORA = 256
ROPE_BASE = 10000.0
EPS = 1e-5
ALPHA = 2.0 ** 0.25
RET_SCALE = RET_DK ** -0.5
MLA_SCALE = (MLA_NOPE + MLA_ROPE) ** -0.5
NEG = -1e30

ADAM_LR = 0.001
ADAM_B1 = 0.9
ADAM_B2 = 0.999
ADAM_EPS = 1e-08
ADAM_WD = 0.01
ADAM_STEP = 10

P_RQ, P_RK, P_RV, P_RG, P_GR, P_GM, P_CQ, P_CKV, P_KPE, P_W = 0, 1024, 2048, 4096, 6144, 7168, 8192, 8448, 8704, 8960
W_IN_COLS = 8768
RET_L = 256
ATT_TF = 2048
ATT_TB = 1024
ATT_HP = 1
LOG2E = math.log2(math.e)
Q_PRESCALE = MLA_SCALE * LOG2E

PACK = (("ffn1_w_in", 704), ("ffn1_w_out", 352), ("w_in", 1096), ("w_ret_o", 256), ("w_uq", 48), ("w_ukv", 64),
        ("w_mla_o", 128), ("w_out", 128), ("ffn2_w_in", 704), ("ffn2_w_out", 352), ("ple_w_gate", 128), ("ple_w_proj", 32))


def _pad16(r):
    return -(-r // 16) * 16


PACK_LATE = PACK[:2]
PACK_EARLY = PACK[2:]
SMALL_ROWS = 16


def _pcall(body, **kw):
    return pl.pallas_call(body, **kw)


def _pick(dim, prefs):
    for p in prefs:
        if dim % p == 0:
            return p
    return dim


def _sigmoid(x):
    return 1.0 / (1.0 + jnp.exp(-x))


def _silu(x):
    return x * _sigmoid(x)


def _ln(r, g, b):
    mu = jnp.mean(r, axis=-1, keepdims=True)
    var = jnp.mean(jnp.square(r - mu), axis=-1, keepdims=True)
    return (r - mu) * lax.rsqrt(var + EPS) * g + b


def _rms(x, g):
    return x * lax.rsqrt(jnp.mean(jnp.square(x), axis=-1, keepdims=True) + EPS) * g


def _dot(a, b, ca, cb):
    return lax.dot_general(a, b, (((ca,), (cb,)), ((), ())), preferred_element_type=F32)


def _accum(ref, val, first=None):
    @pl.when(pl.program_id(0) == 0 if first is None else first)
    def _():
        ref[...] = jnp.zeros_like(ref)

    ref[...] += val


def _mm(a, b, *, ta=False, tb=False, add=None, out_dtype=None, name, tm=None, tn=None, tk=None, epilogue=None):
    parts = a.shape[0] if a.ndim == 3 else 1
    ar, ac = a.shape[-2], a.shape[-1]
    out_dtype = out_dtype or (BF16 if ta else F32)
    if ta:
        K, M = ar, ac * parts
    else:
        M, K = ar, ac * parts
    if tb:
        N, K2 = b.shape
    else:
        K2, N = b.shape
    assert K == K2, (a.shape, b.shape, ta, tb)
    big = (1024, 1408, 1280, 768, 512, 256, 128)
    tm = tm or _pick(ac if (ta and parts > 1) else M, big)
    tn = tn or (N if N <= 1024 else _pick(N, big))
    kdim = ac if (not ta and parts > 1) else K
    tk = tk or (kdim if kdim <= 2816 and parts == 1 else
                _pick(kdim, (2048, 1408, 1280, 1024, 512) if tm <= 1024 else (1024, 1408, 1280, 512)))
    nk = K // tk
    grid = (M // tm, N // tn, nk)
    if parts > 1 and ta:
        per = ac // tm
        a_spec = pl.BlockSpec((None, tk, tm), lambda i, j, k: (i // per, k, i % per))
    elif parts > 1:
        per = ac // tk
        a_spec = pl.BlockSpec((None, tm, tk), lambda i, j, k: (k // per, i, k % per))
    else:
        a_spec = pl.BlockSpec((tk, tm), lambda i, j, k: (k, i)) if ta else pl.BlockSpec((tm, tk), lambda i, j, k: (i, k))
    b_spec = pl.BlockSpec((tn, tk), lambda i, j, k: (j, k)) if tb else pl.BlockSpec((tk, tn), lambda i, j, k: (k, j))
    o_spec = pl.BlockSpec((tm, tn), lambda i, j, k: (i, j))
    ca, cb = (0 if ta else 1), (1 if tb else 0)
    has_add = add is not None
    n_in = 2 + int(has_add)
    if epilogue is not None:
        assert tn == N and not ta
        ep_fn, ep_rows, ep_whole, ep_outs, ep_accs = epilogue
        n_ep_in = len(ep_rows) + len(ep_whole)
        n_out = len(ep_outs) + len(ep_accs)
    else:
        n_ep_in, n_out = 0, 1

    def body(*refs):
        a_ref, b_ref = refs[0], refs[1]
        add_ref = refs[2] if has_add else None
        o_ref = refs[n_in + n_ep_in]
        first_row_tile = pl.program_id(0) == 0

        def finish(r):
            if has_add:
                r = r + add_ref[...].astype(F32)
            if epilogue is not None:
                ep_fn(r, refs[n_in:n_in + n_ep_in], refs[n_in + n_ep_in:n_in + n_ep_in + n_out], first_row_tile)
            else:
                o_ref[...] = r.astype(out_dtype)

        if nk == 1:
            finish(_dot(a_ref[...], b_ref[...], ca, cb))
            return
        acc_ref = refs[-1]
        k = pl.program_id(2)

        @pl.when(k == 0)
        def _():
            acc_ref[...] = jnp.zeros_like(acc_ref)

        acc_ref[...] += _dot(a_ref[...], b_ref[...], ca, cb)

        @pl.when(k == nk - 1)
        def _():
            finish(acc_ref[...])

    in_specs = [a_spec, b_spec] + ([o_spec] if has_add else [])
    args = (a, b) + ((add,) if has_add else ())
    out_specs, out_shape = o_spec, jax.ShapeDtypeStruct((M, N), out_dtype)
    if epilogue is not None:
        for row_in in ep_rows:
            arr, col_block, width = (tuple(row_in) + (N,))[:3] if isinstance(row_in, tuple) else (row_in, 0, N)
            in_specs.append(pl.BlockSpec((tm, width), lambda i, j, k, _c=col_block: (i, _c)))
            args += (arr,)
        in_specs += [pl.BlockSpec(w.shape, lambda i, j, k, _n=w.ndim: (0,) * _n) for w in ep_whole]
        args += tuple(ep_whole)
        outs = [o if isinstance(o, tuple) else (o, N) for o in ep_outs]
        out_specs = ([pl.BlockSpec((tm, w), lambda i, j, k: (i, 0)) for _, w in outs]
                     + [pl.BlockSpec((r, N), lambda i, j, k: (0, 0)) for r in ep_accs])
        out_shape = [jax.ShapeDtypeStruct((M, w), dt) for dt, w in outs] + [jax.ShapeDtypeStruct((r, N), F32) for r in ep_accs]
    return _pcall(
        body, grid=grid, in_specs=in_specs, out_specs=out_specs, out_shape=out_shape,
        scratch_shapes=[pltpu.VMEM((tm, tn), F32)] if nk > 1 else [], name=name,
        compiler_params=pltpu.CompilerParams(dimension_semantics=("arbitrary" if epilogue is not None else "parallel", "parallel", "arbitrary")),
    )(*args)


def _mm_pieces(pieces, add, name):
    M, N = add.shape
    tm = _pick(M, (1024, 512, 256, 128))
    spans, in_specs, args, start = [], [], [], 0
    for a, b in pieces:
        K = a.shape[1]
        assert b.shape == (K, N) and a.shape[0] == M, (a.shape, b.shape)
        tk = _pick(K, (1536, 1024, 768, 512, 256, 128))
        n = K // tk
        spans.append((start, n))
        block = lambda k, _s=start, _n=n: jnp.where(k < _s, _n - 1, jnp.minimum(k - _s, _n - 1))
        in_specs.append(pl.BlockSpec((tm, tk), lambda i, k, _s=start, _blk=block: (jnp.where(k < _s, jnp.maximum(i - 1, 0), i), _blk(k))))
        in_specs.append(pl.BlockSpec((tk, N), lambda i, k, _blk=block: (_blk(k), 0)))
        args += [a, b]
        start += n
    o_spec = pl.BlockSpec((tm, N), lambda i, k: (i, 0))

    def body(*refs):
        add_ref, o_ref = refs[-2], refs[-1]
        k = pl.program_id(1)

        @pl.when(k == 0)
        def _():
            o_ref[...] = add_ref[...]

        for p, (s, n) in enumerate(spans):
            @pl.when((k >= s) & (k < s + n))
            def _(p=p):
                o_ref[...] += _dot(refs[2 * p][...], refs[2 * p + 1][...], 1, 0)

    return _pcall(
        body, grid=(M // tm, start), in_specs=in_specs + [o_spec], out_specs=o_spec, out_shape=jax.ShapeDtypeStruct((M, N), F32),
        name=name, compiler_params=pltpu.CompilerParams(dimension_semantics=("parallel", "arbitrary")),
    )(*args, add)


def _ln_epilogue(res, c, g, b):
    def fn(r, ins, outs, first):
        res_ref, g_ref, b_ref = ins
        f_ref, h_ref, hb_ref = outs
        h = _ln(ALPHA * res_ref[...] + c * r, g_ref[...], b_ref[...])
        f_ref[...] = r
        h_ref[...] = h
        hb_ref[...] = h.astype(BF16)

    return (fn, [res], [g, b], (F32, F32, BF16), ())


def _ln_bwd_epilogue(res, f, c, g, b):
    def fn(r, ins, outs, first):
        res_ref, f_ref, g_ref, b_ref = ins
        dr_ref, df_ref, dg_ref, db_ref = outs
        pre = ALPHA * res_ref[...] + c * f_ref[...]
        xc = pre - jnp.mean(pre, axis=-1, keepdims=True)
        rstd = lax.rsqrt(jnp.mean(jnp.square(xc), axis=-1, keepdims=True) + EPS)
        xhat = xc * rstd
        dyg = r * g_ref[...]
        dpre = rstd * (dyg - jnp.mean(dyg, axis=-1, keepdims=True) - xhat * jnp.mean(dyg * xhat, axis=-1, keepdims=True))
        dr_ref[...] = ALPHA * dpre
        df_ref[...] = (c * dpre).astype(BF16)
        _accum(dg_ref, jnp.sum(r * xhat, axis=0, keepdims=True), first)
        _accum(db_ref, jnp.sum(r, axis=0, keepdims=True), first)

    return (fn, [res, f], [g, b], (F32, BF16), (1, 1))


ROW_TM = 512


def _rows(body, T, tm, ins, outs, name, accs=()):
    in_specs, args = [], []
    for arr, w, cb in ins:
        if w is None:
            in_specs.append(pl.BlockSpec(arr.shape, lambda i, _n=arr.ndim: (0,) * _n))
        else:
            in_specs.append(pl.BlockSpec((tm, w), lambda i, _cb=cb: (i, _cb)))
        args.append(arr)
    out_specs = [pl.BlockSpec((tm, w), lambda i: (i, 0)) for w, _ in outs]
    out_shape = [jax.ShapeDtypeStruct((T, w), dt) for w, dt in outs]
    for r, w in accs:
        out_specs.append(pl.BlockSpec((r, w), lambda i: (0, 0)))
        out_shape.append(jax.ShapeDtypeStruct((r, w), F32))
    return _pcall(
        body, grid=(T // tm,), in_specs=in_specs, out_specs=out_specs, out_shape=out_shape, name=name,
        compiler_params=pltpu.CompilerParams(dimension_semantics=("arbitrary",)),
    )(*args)


def _ln_bwd_rows(res, f, c, g, b, dh, T, name):
    fn = _ln_bwd_epilogue(res, f, c, g, b)[0]

    def body(r_ref, f_ref, g_ref, b_ref, d_ref, dr_ref, df_ref, dg_ref, db_ref):
        fn(d_ref[...], (r_ref, f_ref, g_ref, b_ref), (dr_ref, df_ref, dg_ref, db_ref), pl.program_id(0) == 0)

    return _rows(body, T, min(ROW_TM, T), [(res, D, 0), (f, D, 0), (g, None, None), (b, None, None), (dh, D, 0)],
                 [(D, F32), (D, BF16)], name, accs=[(1, D), (1, D)])


FFN_TN = 1408
FFN_TM = 512
LN_TM = 512


def _ffn_in(x, wt, name, order_after=None):
    T = x.shape[0]
    tm, tn = min(FFN_TM, T), FFN_TN
    nj = D_FF // tn
    emit_xb = x.dtype != BF16
    n_in = 3 + int(order_after is not None)

    def body(*refs):
        x_ref, wg_ref, wu_ref = refs[:3]
        hs_ref, gu_ref = refs[n_in], refs[n_in + 1]
        xv = x_ref[...].astype(BF16)
        g = _dot(xv, wg_ref[...], 1, 1)
        u = _dot(xv, wu_ref[...], 1, 1)
        hs_ref[...] = (_silu(g) * u).astype(BF16)
        gu_ref[0] = g.astype(BF16)
        gu_ref[1] = u.astype(BF16)
        if emit_xb:
            refs[n_in + 2][...] = xv

    in_specs = [pl.BlockSpec((tm, D), lambda i, j: (i, 0)), pl.BlockSpec((tn, D), lambda i, j: (j, 0)),
                pl.BlockSpec((tn, D), lambda i, j: (j + nj, 0))]
    args = [x, wt, wt]
    if order_after is not None:
        in_specs.append(pl.BlockSpec(order_after.shape, lambda i, j: (0, 0)))
        args.append(order_after)
    out_specs = [pl.BlockSpec((tm, tn), lambda i, j: (i, j)), pl.BlockSpec((2, tm, tn), lambda i, j: (0, i, j))]
    out_shape = [jax.ShapeDtypeStruct((T, D_FF), BF16), jax.ShapeDtypeStruct((2, T, D_FF), BF16)]
    if emit_xb:
        out_specs.append(pl.BlockSpec((tm, D), lambda i, j: (i, 0)))
        out_shape.append(jax.ShapeDtypeStruct((T, D), BF16))
    return _pcall(
        body, grid=(T // tm, nj), in_specs=in_specs, out_specs=out_specs, out_shape=out_shape, name=name,
        compiler_params=pltpu.CompilerParams(dimension_semantics=("parallel", "arbitrary")),
    )(*args)


def _ffn_act_bwd(df, wo, gu, name):
    T = df.shape[0]
    tm, tn = min(FFN_TM, T), FFN_TN

    def body(d_ref, w_ref, gu_ref, o_ref):
        dhs = _dot(d_ref[...], w_ref[...], 1, 1)
        g, u = gu_ref[0].astype(F32), gu_ref[1].astype(F32)
        sig = _sigmoid(g)
        act = g * sig
        o_ref[0] = (dhs * u * (sig + act * (1.0 - sig))).astype(BF16)
        o_ref[1] = (dhs * act).astype(BF16)

    return _pcall(
        body, grid=(T // tm, D_FF // tn),
        in_specs=[pl.BlockSpec((tm, D), lambda i, j: (i, 0)), pl.BlockSpec((tn, D), lambda i, j: (j, 0)),
                  pl.BlockSpec((2, tm, tn), lambda i, j: (0, i, j))],
        out_specs=pl.BlockSpec((2, tm, tn), lambda i, j: (0, i, j)),
        out_shape=jax.ShapeDtypeStruct((2, T, D_FF), BF16), name=name,
        compiler_params=pltpu.CompilerParams(dimension_semantics=("parallel", "parallel")),
    )(df, wo, gu)


def _rope128(t, cos, sin_s):
    return t * cos + pltpu.roll(t, 64, 1) * sin_s


def _rope128_t(g, cos, sin_s):
    return g * cos - pltpu.roll(g, 64, 1) * sin_s


def _partner32(t):
    lane = lax.broadcasted_iota(jnp.int32, t.shape, 1)
    return jnp.where((lane & 32) == 0, pltpu.roll(t, 96, 1), pltpu.roll(t, 32, 1))


def _rope64(t, cos, sin_s):
    return t * cos + _partner32(t) * sin_s


def _rope64_t(g, cos, sin_s):
    return g * cos - _partner32(g) * sin_s


def _mixer_prep_fwd(proj, tabs, qg, kvg, T):
    cos128, sin128, cos64, sin64 = tabs

    def body(rq_ref, rk_ref, cq_ref, ckv_ref, kpe_ref, c1_ref, s1_ref, c2_ref, s2_ref, qg_ref, kvg_ref,
             oq_ref, ok_ref, oqn_ref, okvn_ref, okpe_ref):
        c1, s1 = c1_ref[...], s1_ref[...]
        for h in range(HEADS):
            sl = slice(h * RET_DK, (h + 1) * RET_DK)
            oq_ref[:, sl] = _rope128(rq_ref[:, sl].astype(F32), c1, s1).astype(BF16)
            ok_ref[:, sl] = (_rope128(rk_ref[:, sl].astype(F32), c1, s1) * RET_SCALE).astype(BF16)
        oqn_ref[...] = _rms(cq_ref[...].astype(F32), qg_ref[...]).astype(BF16)
        okvn_ref[...] = _rms(ckv_ref[...].astype(F32), kvg_ref[...]).astype(BF16)
        okpe_ref[...] = _rope64(kpe_ref[...].astype(F32), c2_ref[...], s2_ref[...]).astype(BF16)

    ins = [(proj, 1024, 0), (proj, 1024, 1), (proj, 256, P_CQ // 256), (proj, 256, P_CKV // 256),
           (proj, 128, P_KPE // 128), (cos128, 128, 0), (sin128, 128, 0), (cos64, 128, 0), (sin64, 128, 0),
           (qg, None, None), (kvg, None, None)]
    outs = [(1024, BF16), (1024, BF16), (LORA, BF16), (LORA, BF16), (128, BF16)]
    return _rows(body, T, min(ROW_TM, T), ins, outs, "mixer_prep_fwd")


def _rms_bwd(proj, dqn, dkvn, dkpe_all, tabs, qg, kvg, T):
    _, _, cos64, sin64 = tabs

    def body(cq_ref, ckv_ref, dq_ref, dkv_ref, dk_ref, c2_ref, s2_ref, qg_ref, kvg_ref, o_ref, dqg_ref, dkvg_ref):
        _, vjp = jax.vjp(_rms, cq_ref[...].astype(F32), qg_ref[...])
        dx, dg = vjp(dq_ref[...])
        o_ref[:, 0:LORA] = dx.astype(BF16)
        _accum(dqg_ref, dg)
        _, vjp = jax.vjp(_rms, ckv_ref[...].astype(F32), kvg_ref[...])
        dx, dg = vjp(dkv_ref[...])
        o_ref[:, P_CKV - P_CQ:P_CKV - P_CQ + LORA] = dx.astype(BF16)
        _accum(dkvg_ref, dg)
        g = dk_ref[:, 0:128]
        for h in range(1, HEADS):
            g = g + dk_ref[:, h * 128:(h + 1) * 128]
        lane = lax.broadcasted_iota(jnp.int32, g.shape, 1)
        g = jnp.where(lane < MLA_ROPE, g, 0.0)
        o_ref[:, P_KPE - P_CQ:P_KPE - P_CQ + 128] = _rope64_t(g, c2_ref[...], s2_ref[...]).astype(BF16)
        o_ref[:, P_KPE - P_CQ + 128:] = jnp.zeros((g.shape[0], P_W - P_KPE - 128), BF16)

    ins = [(proj, 256, P_CQ // 256), (proj, 256, P_CKV // 256), (dqn, LORA, 0), (dkvn, LORA, 0), (dkpe_all, 1024, 0),
           (cos64, 128, 0), (sin64, 128, 0), (qg, None, None), (kvg, None, None)]
    return _rows(body, T, min(ROW_TM, T), ins, [(P_W - P_CQ, BF16)], "rms_bwd", accs=[(1, LORA), (1, LORA)])


def _gn_gate_bwd(y, rg, g, d):
    xc = y - jnp.mean(y, axis=-1, keepdims=True)
    rstd = lax.rsqrt(jnp.mean(jnp.square(xc), axis=-1, keepdims=True) + EPS)
    xhat = xc * rstd
    sig = _sigmoid(rg)
    act = rg * sig
    dn = d * act
    drg = d * (xhat * g) * (sig + act * (1.0 - sig))
    dxh = dn * g
    dy = rstd * (dxh - jnp.mean(dxh, axis=-1, keepdims=True) - xhat * jnp.mean(dxh * xhat, axis=-1, keepdims=True))
    return dy, drg, jnp.sum(dn * xhat, axis=0, keepdims=True)


def _gn_gate(y, rg, g):
    mu = jnp.mean(y, axis=-1, keepdims=True)
    var = jnp.mean(jnp.square(y - mu), axis=-1, keepdims=True)
    return _silu(rg) * ((y - mu) * lax.rsqrt(var + EPS) * g)


def _q_assemble_epilogue(tabs):
    _, _, cos64, sin64 = tabs

    def fn(q, ins, outs, first):
        c_ref, s_ref = ins
        on_ref, op_ref = outs
        on_ref[...] = (q[:, :1024] * Q_PRESCALE).astype(BF16)
        c, s = c_ref[...], s_ref[...]
        lane = lax.broadcasted_iota(jnp.int32, c.shape, 1)
        for j in range(HEADS // 2):
            r = _rope64(q[:, 1024 + 128 * j:1024 + 128 * (j + 1)], c, s) * Q_PRESCALE
            op_ref[:, 256 * j:256 * j + 128] = jnp.where(lane < 64, r, 0.0).astype(BF16)
            op_ref[:, 256 * j + 128:256 * j + 256] = jnp.where(lane < 64, pltpu.roll(r, 64, 1), 0.0).astype(BF16)

    return (fn, [(cos64, 0, 128), (sin64, 0, 128)], [], ((BF16, 1024), (BF16, 1024)), ())


def _q_assemble_bwd(dqn, dqpe, tabs, T):
    _, _, cos64, sin64 = tabs

    def body(dn_ref, dp_ref, c_ref, s_ref, on_ref, op_ref):
        on_ref[...] = dn_ref[...].astype(BF16)
        c, s = c_ref[...], s_ref[...]
        lane = lax.broadcasted_iota(jnp.int32, c.shape, 1)
        for j in range(HEADS // 2):
            g = jnp.where(lane < 64, dp_ref[:, 256 * j:256 * j + 128], pltpu.roll(dp_ref[:, 256 * j + 128:256 * j + 256], 64, 1))
            op_ref[:, 128 * j:128 * (j + 1)] = _rope64_t(g, c, s).astype(BF16)

    return _rows(body, T, min(ROW_TM, T), [(dqn, 1024, 0), (dqpe, 1024, 0), (cos64, 128, 0), (sin64, 128, 0)],
                 [(1024, BF16), (512, BF16)], "q_assemble_bwd")


def _mix_fn(gr, gm, yr, ym):
    return _sigmoid(gr) * yr + _sigmoid(gm) * ym


def _mix_epilogue(proj, y_ret):
    def fn(r, ins, outs, first):
        gr_ref, gm_ref, yr_ref = ins
        ym_ref, mix_ref = outs
        ym_ref[...] = r.astype(BF16)
        mix_ref[...] = _mix_fn(gr_ref[...].astype(F32), gm_ref[...].astype(F32), yr_ref[...].astype(F32), r).astype(BF16)

    return (fn, [(proj, P_GR // D), (proj, P_GM // D), y_ret], [], (BF16, BF16), ())


def _mix_bwd_epilogue(proj, y_ret, y_mla):
    def fn(r, ins, outs, first):
        gr_ref, gm_ref, yr_ref, ym_ref = ins
        dgate_ref, dyr_ref, dym_ref = outs
        sr, sm = _sigmoid(gr_ref[...].astype(F32)), _sigmoid(gm_ref[...].astype(F32))
        dgate_ref[:, 0:D] = (r * yr_ref[...].astype(F32) * (sr * (1.0 - sr))).astype(BF16)
        dgate_ref[:, D:2 * D] = (r * ym_ref[...].astype(F32) * (sm * (1.0 - sm))).astype(BF16)
        dyr_ref[...] = (r * sr).astype(BF16)
        dym_ref[...] = (r * sm).astype(BF16)

    return (fn, [(proj, P_GR // D), (proj, P_GM // D), y_ret, y_mla], [], ((BF16, 2 * D), BF16, BF16), ())


def _head_epilogue(h3, pp, tgt, g, b):
    def fn(r, ins, outs, first):
        h_ref, pp_ref, t_ref, g_ref, b_ref = ins
        dh_ref, dgl_ref, dpp_ref, dg_ref, db_ref, loss_ref = outs

        sg, pp, gain = _sigmoid(r), pp_ref[...], g_ref[...]
        pre = ALPHA * h_ref[...] + sg * pp
        xc = pre - jnp.mean(pre, axis=-1, keepdims=True)
        rstd = lax.rsqrt(jnp.mean(jnp.square(xc), axis=-1, keepdims=True) + EPS)
        xhat = xc * rstd
        err = xhat * gain + b_ref[...] - t_ref[...]
        dy = err * (1.0 / D)
        dyg = dy * gain
        dpre = rstd * (dyg - jnp.mean(dyg, axis=-1, keepdims=True) - xhat * jnp.mean(dyg * xhat, axis=-1, keepdims=True))
        dh_ref[...] = ALPHA * dpre
        dgl_ref[...] = (dpre * pp * (sg * (1.0 - sg))).astype(BF16)
        dpp_ref[...] = (dpre * sg).astype(BF16)
        _accum(dg_ref, jnp.sum(dy * xhat, axis=0, keepdims=True), first)
        _accum(db_ref, jnp.sum(dy, axis=0, keepdims=True), first)
        part = 0.5 * jnp.sum(jnp.mean(jnp.square(err), axis=-1, keepdims=True), axis=0, keepdims=True)
        _accum(loss_ref, jnp.broadcast_to(part, loss_ref.shape), first)

    return (fn, [h3, pp, tgt], [g, b], (F32, BF16, BF16), (1, 1, 8))


def _delta_epilogue(o):
    def fn(r, ins, outs, first):
        (o_ref,) = ins
        db_ref, dl_ref = outs
        db_ref[...] = r.astype(BF16)
        for h in range(HEADS):
            sl = slice(h * MLA_DV, (h + 1) * MLA_DV)
            dl = jnp.sum(r[:, sl] * o_ref[:, sl].astype(F32), axis=-1, keepdims=True)
            dl_ref[:, sl] = jnp.broadcast_to(dl, (r.shape[0], MLA_DV))

    return (fn, [o], [], (BF16, F32), ())


def _ret_consts():
    L = RET_L
    lg = jnp.log(1.0 - 2.0 ** (-5.0 - jnp.arange(HEADS, dtype=F32)))[:, None, None]
    idx = jnp.arange(L, dtype=F32)
    ch = jnp.arange(L) // CHUNK
    dist = idx[:, None] - idx[None, :]
    same = (ch[:, None] == ch[None, :])[None]
    earlier = (ch[None, :] < ch[:, None])[None]
    dm = jnp.where(same, jnp.exp(lg * jnp.abs(dist)[None]), jnp.where(earlier, jnp.exp(lg * dist[None]), 0.0))
    xi = jnp.broadcast_to(jnp.exp(lg * (idx + 1.0)[None, :, None]), (HEADS, L, 128))
    zeta = jnp.broadcast_to(jnp.exp(lg * (L - 1.0 - idx)[None, :, None]), (HEADS, L, 128))
    gl = jnp.broadcast_to(jnp.exp(lg * float(L)), (HEADS, 8, 128))
    return dm.astype(F32), xi.astype(F32), zeta.astype(F32), gl.astype(F32)


def _whole(arr):
    return pl.BlockSpec(arr.shape, lambda n, _nd=arr.ndim: (0,) * _nd)


def _ret_fwd(q, k, v, proj, gn_g, consts, T):
    dm, xi, zeta, gl = consts
    L = RET_L
    n_sc = T // L

    def body(q_ref, k_ref, v_ref, rg_ref, g_ref, dm_ref, xi_ref, ze_ref, gl_ref, y_ref, yr_ref, s_ref, st_ref):
        @pl.when(pl.program_id(0) == 0)
        def _():
            st_ref[...] = jnp.zeros_like(st_ref)

        for h in range(HEADS):
            ks, vs = slice(h * RET_DK, (h + 1) * RET_DK), slice(h * RET_DV, (h + 1) * RET_DV)
            qq, kk, vv = q_ref[:, ks], k_ref[:, ks], v_ref[:, vs]
            st = st_ref[h]
            s_ref[h, 0] = st
            p = (_dot(qq, kk, 1, 1) * dm_ref[h]).astype(BF16)
            cross = _dot(qq, st.astype(BF16), 1, 0)
            xi_c = jnp.concatenate([xi_ref[h], xi_ref[h]], axis=1)
            y = _dot(p, vv, 1, 0) + cross * xi_c
            y_ref[:, vs] = y
            yr_ref[:, vs] = _gn_gate(y, rg_ref[:, vs].astype(F32), g_ref[:, vs]).astype(BF16)
            kz = (kk.astype(F32) * ze_ref[h]).astype(BF16)
            gl2 = jnp.concatenate([gl_ref[h, 0:1, :], gl_ref[h, 0:1, :]], axis=1)
            st_ref[h] = st * gl2 + _dot(kz, vv, 0, 0)

    return _pcall(
        body, grid=(n_sc,),
        in_specs=[pl.BlockSpec((L, 1024), lambda n: (n, 0)), pl.BlockSpec((L, 1024), lambda n: (n, 0)),
                  pl.BlockSpec((L, 2048), lambda n: (n, P_RV // 2048)), pl.BlockSpec((L, 2048), lambda n: (n, P_RG // 2048)),
                  _whole(gn_g), _whole(dm), _whole(xi), _whole(zeta), _whole(gl)],
        out_specs=[pl.BlockSpec((L, 2048), lambda n: (n, 0)), pl.BlockSpec((L, 2048), lambda n: (n, 0)),
                   pl.BlockSpec((HEADS, 1, 128, 256), lambda n: (0, n, 0, 0))],
        out_shape=[jax.ShapeDtypeStruct((T, HEADS * RET_DV), F32), jax.ShapeDtypeStruct((T, HEADS * RET_DV), BF16),
                   jax.ShapeDtypeStruct((HEADS, n_sc, 128, 256), F32)],
        scratch_shapes=[pltpu.VMEM((HEADS, 128, 256), F32)], name="ret_fwd",
        compiler_params=pltpu.CompilerParams(dimension_semantics=("arbitrary",)),
    )(q, k, v, proj, gn_g, dm, xi, zeta, gl)


def _ret_bwd(q, k, v, y, proj, gn_g, states, dyr, consts, tabs, T):
    dm, xi, zeta, gl = consts
    cos128, sin128, _, _ = tabs
    L = RET_L
    n_sc = T // L

    def body(q_ref, k_ref, v_ref, y_ref, rg_ref, g_ref, d_ref, s_ref, dm_ref, xi_ref, ze_ref, gl_ref, c_ref, sn_ref,
             dp_ref, dg_ref, gs_ref):
        @pl.when(pl.program_id(0) == 0)
        def _():
            gs_ref[...] = jnp.zeros_like(gs_ref)

        c, sn = c_ref[...], sn_ref[...]
        dgs = []
        for h in range(HEADS):
            ks, vs = slice(h * RET_DK, (h + 1) * RET_DK), slice(h * RET_DV, (h + 1) * RET_DV)
            dy, drg, dg = _gn_gate_bwd(y_ref[:, vs], rg_ref[:, vs].astype(F32), g_ref[:, vs], d_ref[:, vs].astype(F32))
            dp_ref[:, P_RG + h * RET_DV:P_RG + (h + 1) * RET_DV] = drg.astype(BF16)
            dgs.append(dg)
            qq, kk, vv, dyy = q_ref[:, ks], k_ref[:, ks], v_ref[:, vs], dy.astype(BF16)
            dmm = dm_ref[h]
            gb = gs_ref[h].astype(BF16)
            sb = s_ref[h, 0].astype(BF16)
            xi_c = jnp.concatenate([xi_ref[h], xi_ref[h]], axis=1)
            pb = (_dot(qq, kk, 1, 1) * dmm).astype(BF16)
            kz = (kk.astype(F32) * ze_ref[h]).astype(BF16)
            dp_ref[:, P_RV + h * RET_DV:P_RV + (h + 1) * RET_DV] = (_dot(pb, dyy, 0, 0) + _dot(kz, gb, 1, 0)).astype(BF16)
            da = (_dot(dyy, vv, 1, 1) * dmm).astype(BF16)
            dyx = (dyy.astype(F32) * xi_c).astype(BF16)
            dq = _dot(da, kk, 1, 0) + _dot(dyx, sb, 1, 1)
            dk = _dot(da, qq, 0, 0) + _dot(vv, gb, 1, 1) * ze_ref[h]
            dp_ref[:, P_RQ + h * RET_DK:P_RQ + (h + 1) * RET_DK] = _rope128_t(dq, c, sn).astype(BF16)
            dp_ref[:, P_RK + h * RET_DK:P_RK + (h + 1) * RET_DK] = (_rope128_t(dk, c, sn) * RET_SCALE).astype(BF16)
            gl2 = jnp.concatenate([gl_ref[h, 0:1, :], gl_ref[h, 0:1, :]], axis=1)
            gs_ref[h] = gs_ref[h] * gl2 + _dot(qq, dyx, 0, 0)
        _accum(dg_ref, jnp.concatenate(dgs, axis=1))

    rev = lambda n: n_sc - 1 - n
    return _pcall(
        body, grid=(n_sc,),
        in_specs=[pl.BlockSpec((L, 1024), lambda n: (rev(n), 0)), pl.BlockSpec((L, 1024), lambda n: (rev(n), 0)),
                  pl.BlockSpec((L, 2048), lambda n: (rev(n), P_RV // 2048)), pl.BlockSpec((L, 2048), lambda n: (rev(n), 0)),
                  pl.BlockSpec((L, 2048), lambda n: (rev(n), P_RG // 2048)), _whole(gn_g),
                  pl.BlockSpec((L, 2048), lambda n: (rev(n), 0)),
                  pl.BlockSpec((HEADS, 1, 128, 256), lambda n: (0, rev(n), 0, 0)),
                  _whole(dm), _whole(xi), _whole(zeta), _whole(gl),
                  pl.BlockSpec((L, 128), lambda n: (rev(n), 0)), pl.BlockSpec((L, 128), lambda n: (rev(n), 0))],
        out_specs=[pl.BlockSpec((L, P_GR), lambda n: (rev(n), 0)), pl.BlockSpec((1, 2048), lambda n: (0, 0))],
        out_shape=[jax.ShapeDtypeStruct((T, P_GR), BF16), jax.ShapeDtypeStruct((1, 2048), F32)],
        scratch_shapes=[pltpu.VMEM((HEADS, 128, 256), F32)], name="ret_bwd",
        compiler_params=pltpu.CompilerParams(dimension_semantics=("arbitrary",)),
    )(q, k, v, y, proj, gn_g, dyr, states, dm, xi, zeta, gl, cos128, sin128)


def _diag_mask(nrows, ncols, row0):
    row = row0 + lax.broadcasted_iota(jnp.int32, (nrows, ncols), 0)
    col = lax.broadcasted_iota(jnp.int32, (nrows, ncols), 1)
    return lax.shift_right_logical(col, 6) <= lax.shift_right_logical(row, 6)


def _diag_spans(t):
    h = t // 2
    return ((0, h, h), (h, h, t)) if h % 128 == 0 else ((0, t, t),)


def _tri_steps(nb, by_key):
    if by_key:
        pairs = [(i, j) for j in range(nb) for i in range(j, nb)]
    else:
        pairs = [(i, j) for i in range(nb) for j in range(i + 1)]
    return jnp.array([a for a, _ in pairs], jnp.int32), jnp.array([b for _, b in pairs], jnp.int32)


def _attn_fwd(qn, qpe, kv, kpe, T):
    t = min(ATT_TF, T)
    nb = T // t
    ii, jj = _tri_steps(nb, by_key=False)

    hp = ATT_HP
    w = 128 * hp

    def body(ii_ref, jj_ref, qn_ref, qp_ref, kn_ref, kp_ref, v_ref, o_ref, lse_ref, m_sc, l_sc, acc_sc):
        st = pl.program_id(1)
        i, j = ii_ref[st], jj_ref[st]

        @pl.when(j == 0)
        def _():
            m_sc[...] = jnp.full_like(m_sc, NEG)
            l_sc[...] = jnp.zeros_like(l_sc)
            acc_sc[...] = jnp.zeros_like(acc_sc)

        def update(diag):
            kp = kp_ref[...]
            spans = _diag_spans(t) if diag else ((0, t, t),)
            for hh in range(hp):
                sl = slice(128 * hh, 128 * (hh + 1))
                for row0, nr, nkeys in spans:
                    rows = slice(row0, row0 + nr)
                    q = jnp.concatenate([qn_ref[rows, sl], qp_ref[rows, sl]], axis=1)
                    k = jnp.concatenate([kn_ref[:nkeys, sl], kp[:nkeys]], axis=1)
                    s = _dot(q, k, 1, 1)
                    if diag:
                        s = jnp.where(_diag_mask(nr, nkeys, row0), s, NEG)
                    m_prev = m_sc[rows, sl]
                    m_new = jnp.maximum(m_prev, jnp.max(s, axis=1, keepdims=True))
                    a = jnp.exp2(m_prev - m_new)
                    p = jnp.exp2(s - m_new[:, 0:1])
                    l_sc[rows, sl] = a * l_sc[rows, sl] + jnp.sum(p, axis=1, keepdims=True)
                    acc_sc[rows, sl] = a * acc_sc[rows, sl] + _dot(p.astype(BF16), v_ref[:nkeys, sl], 1, 0)
                    m_sc[rows, sl] = m_new

        @pl.when(j < i)
        def _():
            update(False)

        @pl.when(j == i)
        def _():
            update(True)
            o_ref[...] = (acc_sc[...] / l_sc[...]).astype(BF16)
            lse_ref[...] = m_sc[...] + jnp.log2(l_sc[...])

    qs = pl.BlockSpec((t, w), lambda h, s, ii, jj: (ii[s], h))
    grid_spec = pltpu.PrefetchScalarGridSpec(
        num_scalar_prefetch=2, grid=(HEADS // hp, int(ii.shape[0])),
        in_specs=[qs, qs, pl.BlockSpec((t, w), lambda h, s, ii, jj: (jj[s], h)), pl.BlockSpec((t, 128), lambda h, s, ii, jj: (jj[s], 0)),
                  pl.BlockSpec((t, w), lambda h, s, ii, jj: (jj[s], HEADS // hp + h))],
        out_specs=[qs, qs],
        scratch_shapes=[pltpu.VMEM((t, w), F32), pltpu.VMEM((t, w), F32), pltpu.VMEM((t, w), F32)])
    return _pcall(
        body, grid_spec=grid_spec, out_shape=[jax.ShapeDtypeStruct((T, D), BF16), jax.ShapeDtypeStruct((T, D), F32)], name="attn_fwd",
        compiler_params=pltpu.CompilerParams(dimension_semantics=("arbitrary", "arbitrary")),
    )(ii, jj, qn, qpe, kv, kpe, kv)


def _attn_bwd(qn, qpe, kv, kpe, do, lse, delta, T):
    t = min(ATT_TB, T)
    nb = T // t
    ii, jj = _tri_steps(nb, by_key=True)

    def body(ii_ref, jj_ref, qn_ref, qp_ref, kn_ref, kp_ref, v_ref, do_ref, lse_ref, dl_ref,
             dqn_ref, dqp_ref, dkn_ref, dkp_ref, dv_ref, dk_sc, dv_sc):
        st = pl.program_id(1)
        i, j = ii_ref[st], jj_ref[st]

        @pl.when(st == 0)
        def _():
            dqn_ref[...] = jnp.zeros_like(dqn_ref)
            dqp_ref[...] = jnp.zeros_like(dqp_ref)

        @pl.when(i == j)
        def _():
            dk_sc[...] = jnp.zeros_like(dk_sc)
            dv_sc[...] = jnp.zeros_like(dv_sc)

        def update(diag):
            for row0, nr, nkeys in (_diag_spans(t) if diag else ((0, t, t),)):
                rows = slice(row0, row0 + nr)
                q = jnp.concatenate([qn_ref[rows, :], qp_ref[rows, :]], axis=1)
                k = jnp.concatenate([kn_ref[:nkeys, :], kp_ref[:nkeys, :]], axis=1)
                dob = do_ref[rows, :]
                s = _dot(q, k, 1, 1)
                if diag:
                    s = jnp.where(_diag_mask(nr, nkeys, row0), s, NEG)
                p = jnp.exp2(s - lse_ref[rows, 0:1])
                dv_sc[:nkeys, :] += _dot(p.astype(BF16), dob, 0, 0)
                dp = _dot(dob, v_ref[:nkeys, :], 1, 1)
                ds = (p * (dp - dl_ref[rows, 0:1])).astype(BF16)
                dk_sc[:nkeys, :] += _dot(ds, q, 0, 0)
                dq = _dot(ds, k, 1, 0) * MLA_SCALE
                grows = pl.ds(pl.multiple_of(i * t + row0, nr), nr)
                dqn_ref[grows, :] += dq[:, :128]
                dqp_ref[grows, :] += dq[:, 128:]

        @pl.when(i > j)
        def _():
            update(False)

        @pl.when(i == j)
        def _():
            update(True)

        @pl.when(i == nb - 1)
        def _():
            dkn_ref[...] = (dk_sc[:, :128] * (1.0 / LOG2E)).astype(BF16)
            dkp_ref[...] = dk_sc[:, 128:] * (1.0 / LOG2E)
            dv_ref[...] = dv_sc[...].astype(BF16)

    qs = pl.BlockSpec((t, 128), lambda h, s, ii, jj: (ii[s], h))
    ks = pl.BlockSpec((t, 128), lambda h, s, ii, jj: (jj[s], h))
    hs = pl.BlockSpec((T, 128), lambda h, s, ii, jj: (0, h))
    grid_spec = pltpu.PrefetchScalarGridSpec(
        num_scalar_prefetch=2, grid=(HEADS, int(ii.shape[0])),
        in_specs=[qs, qs, ks, pl.BlockSpec((t, 128), lambda h, s, ii, jj: (jj[s], 0)),
                  pl.BlockSpec((t, 128), lambda h, s, ii, jj: (jj[s], HEADS + h)), qs, qs, qs],
        out_specs=[hs, hs, ks, ks, ks],
        scratch_shapes=[pltpu.VMEM((t, 256), F32), pltpu.VMEM((t, 128), F32)])
    return _pcall(
        body, grid_spec=grid_spec,
        out_shape=[jax.ShapeDtypeStruct((T, D), F32), jax.ShapeDtypeStruct((T, D), F32), jax.ShapeDtypeStruct((T, D), BF16),
                   jax.ShapeDtypeStruct((T, D), F32), jax.ShapeDtypeStruct((T, D), BF16)], name="attn_bwd",
        compiler_params=pltpu.CompilerParams(dimension_semantics=("arbitrary", "arbitrary")),
    )(ii, jj, qn, qpe, kv, kpe, kv, do, lse, delta)


def _mesh_pos():
    x, y, c = lax.axis_index("x"), lax.axis_index("y"), lax.axis_index("c")
    return x, y, c, 4 * x + 2 * y + c


def _peer(x, y, c, k):
    px, py, pc = (x + ((k >> 2) & 1)) % 2, (y + ((k >> 1) & 1)) % 2, (c + (k & 1)) % 2
    return (px, py, pc), 4 * px + 2 * py + pc


_ANY = pl.BlockSpec(memory_space=pl.ANY)


def _rcopy(src, dst, send_sems, recv_sems, k, to):
    return pltpu.make_async_remote_copy(src_ref=src, dst_ref=dst, send_sem=send_sems.at[k], recv_sem=recv_sems.at[k],
                                        device_id=to, device_id_type=pl.DeviceIdType.MESH)


def _all_gather(wsh, ssh):
    def body(w_ref, s_ref, wall_ref, sall_ref, send_sems, recv_sems, loc_sems):
        x, y, c, me = _mesh_pos()
        sib = (x, y, 1 - c)
        chips = [(1 - x, y), (x, 1 - y), (1 - x, 1 - y)]
        slot = lambda px, py, pc: 4 * px + 2 * py + pc
        loc = [pltpu.make_async_copy(s_ref, sall_ref.at[me], loc_sems.at[0])]
        for cp in loc:
            cp.start()
        sends, fwd_waits = [], []
        for n, (src, dst) in enumerate(((w_ref, wall_ref), (s_ref, sall_ref))):
            o = 7 * n
            sends.append(_rcopy(src, dst.at[me], send_sems, recv_sems, o, sib))
            for j, chip in enumerate(chips):
                sends.append(_rcopy(src, dst.at[me], send_sems, recv_sems, o + 1 + j, (*chip, c)))
        for cp in sends:
            cp.start()
        for n, (src, dst) in enumerate(((w_ref, wall_ref), (s_ref, sall_ref))):
            o = 7 * n
            for j, chip in enumerate(chips):
                got = dst.at[slot(*chip, c)]
                _rcopy(src, got, send_sems, recv_sems, o + 1 + j, sib).wait_recv()
                fw = _rcopy(got, got, send_sems, recv_sems, o + 4 + j, sib)
                fw.start()
                sends.append(fw)
            fwd_waits.append(_rcopy(src, dst.at[slot(x, y, 1 - c)], send_sems, recv_sems, o, sib))
            for j, chip in enumerate(chips):
                fwd_waits.append(_rcopy(src, dst.at[slot(*chip, 1 - c)], send_sems, recv_sems, o + 4 + j, sib))
        for cp in fwd_waits:
            cp.wait_recv()
        for cp in sends:
            cp.wait_send()
        for cp in loc:
            cp.wait()

    wall, sall = _pcall(
        body, in_specs=[_ANY, _ANY], out_specs=[_ANY, _ANY],
        out_shape=[jax.ShapeDtypeStruct((N_DEV,) + wsh.shape, wsh.dtype), jax.ShapeDtypeStruct((N_DEV,) + ssh.shape, ssh.dtype)],
        scratch_shapes=[pltpu.SemaphoreType.DMA((14,)), pltpu.SemaphoreType.DMA((14,)), pltpu.SemaphoreType.DMA((1,))],
        name="all_gather_weights",
    )(wsh, ssh)
    me = 4 * lax.axis_index("x") + 2 * lax.axis_index("y") + lax.axis_index("c")
    return lax.dynamic_update_index_in_dim(wall, wsh, me, axis=0), sall


_HBM = pl.BlockSpec(memory_space=pltpu.HBM)
_SEM = pl.BlockSpec(memory_space=pltpu.SEMAPHORE)
_EFFECT = pltpu.SideEffectType.DATAFLOW_SIDE_EFFECTING


def _other_chips(x, y):
    return [(1 - x, y), (x, 1 - y), (1 - x, 1 - y)]


def _gather_start(wsh, order_after):
    def body(w_ref, land_ref, dep_ref, send_sems, recv_sems, w_thru, land_thru, token):
        x, y, c, me = _mesh_pos()
        _rcopy(w_ref, land_ref.at[me], send_sems, recv_sems, 0, (x, y, 1 - c)).start()
        for j, chip in enumerate(_other_chips(x, y)):
            _rcopy(w_ref, land_ref.at[me], send_sems, recv_sems, 1 + j, (*chip, c)).start()
        token[...] = jnp.zeros_like(token)

    shape = (N_DEV,) + wsh.shape
    land = pltpu.with_memory_space_constraint(lax.empty(shape, wsh.dtype), pltpu.HBM)
    return _pcall(
        body, name="gather_start",
        out_shape=(pltpu.SemaphoreType.DMA((4,)), pltpu.SemaphoreType.DMA((4,)), pltpu.HBM(wsh.shape, wsh.dtype),
                   pltpu.HBM(shape, wsh.dtype), jax.ShapeDtypeStruct((8, 128), F32)),
        in_specs=(_HBM, _HBM, _ANY), out_specs=(_SEM, _SEM, _HBM, _HBM, pl.BlockSpec(memory_space=pltpu.VMEM)),
        input_output_aliases={0: 2, 1: 3}, compiler_params=pltpu.CompilerParams(has_side_effects=_EFFECT),
    )(pltpu.with_memory_space_constraint(wsh, pltpu.HBM), land, order_after)


def _gather_wait(send_sems, recv_sems, w_thru, land_thru, after):
    def body(w_ref, land_ref, send_sems, recv_sems, after_ref, w_out, land_out):
        x, y, c, _ = _mesh_pos()
        senders = [(x, y, 1 - c)] + [(*chip, c) for chip in _other_chips(x, y)]
        for k, (px, py, pc) in enumerate(senders):
            cp = _rcopy(w_ref, land_ref.at[4 * px + 2 * py + pc], send_sems, recv_sems, k, (px, py, pc))
            cp.wait_send()
            cp.wait_recv()

    return _pcall(
        body, name="gather_wait", out_shape=(pltpu.HBM(w_thru.shape, w_thru.dtype), pltpu.HBM(land_thru.shape, land_thru.dtype)),
        in_specs=(_HBM, _HBM, _SEM, _SEM, _ANY), out_specs=(_HBM, _HBM), input_output_aliases={0: 0, 1: 1},
        compiler_params=pltpu.CompilerParams(has_side_effects=_EFFECT),
    )(w_thru, land_thru, send_sems, recv_sems, after)


def _gather_finish(wsh, land):
    def body(land_ref, out_ref, send_sems, recv_sems):
        x, y, c, _ = _mesh_pos()
        sib = (x, y, 1 - c)
        sends, recvs = [], []
        for j, (px, py) in enumerate(_other_chips(x, y)):
            mine, theirs = 4 * px + 2 * py + c, 4 * px + 2 * py + (1 - c)
            sends.append(_rcopy(land_ref.at[mine], out_ref.at[mine], send_sems, recv_sems, j, sib))
            recvs.append(_rcopy(land_ref.at[theirs], out_ref.at[theirs], send_sems, recv_sems, j, sib))
        for cp in sends:
            cp.start()
        for cp in recvs:
            cp.wait_recv()
        for cp in sends:
            cp.wait_send()

    done = _pcall(
        body, in_specs=[_ANY], out_specs=_ANY, out_shape=jax.ShapeDtypeStruct(land.shape, land.dtype),
        input_output_aliases={0: 0},
        scratch_shapes=[pltpu.SemaphoreType.DMA((3,)), pltpu.SemaphoreType.DMA((3,))], name="gather_finish",
    )(land)
    me = 4 * lax.axis_index("x") + 2 * lax.axis_index("y") + lax.axis_index("c")
    return lax.dynamic_update_index_in_dim(done, wsh, me, axis=0)


_ROW_TILES = (400, 368, 352, 256, 128, 16, 8)


def _exchange_sibling(g4, name):
    def body(g_ref, r_ref, send_sems, recv_sems):
        x, y, c, _ = _mesh_pos()
        sib = (x, y, 1 - c)
        cps = [_rcopy(g_ref.at[q, 1 - c], r_ref.at[q], send_sems, recv_sems, q, sib) for q in range(4)]
        for cp in cps:
            cp.start()
        for cp in cps:
            cp.wait()

    n, _, R, C = g4.shape
    return _pcall(
        body, in_specs=[_ANY], out_specs=_ANY, out_shape=jax.ShapeDtypeStruct((n, R, C), g4.dtype),
        scratch_shapes=[pltpu.SemaphoreType.DMA((4,)), pltpu.SemaphoreType.DMA((4,))], name=name,
    )(g4)


def _sum_sibling(g4, rsib, cvec, name):
    n, _, R, C = g4.shape
    tr = _pick(R, _ROW_TILES)

    def body(c_ref, g_ref, r_ref, o_ref):
        o_ref[...] = (g_ref[...].astype(F32) + r_ref[...].astype(F32)).astype(o_ref.dtype)

    grid_spec = pltpu.PrefetchScalarGridSpec(
        num_scalar_prefetch=1, grid=(n, R // tr),
        in_specs=[pl.BlockSpec((None, None, tr, C), lambda q, i, cr: (q, cr[0], i, 0)), pl.BlockSpec((None, tr, C), lambda q, i, cr: (q, i, 0))],
        out_specs=pl.BlockSpec((None, tr, C), lambda q, i, cr: (q, i, 0)))
    return _pcall(body, grid_spec=grid_spec, out_shape=jax.ShapeDtypeStruct((n, R, C), g4.dtype), name=name)(cvec, g4, rsib)


def _chips_start(part, name, order_after=None):
    n_in = 2 + int(order_after is not None)

    def body(*refs):
        p_ref, land_ref = refs[:2]
        send_sems, recv_sems, token = refs[n_in], refs[n_in + 1], refs[n_in + 4]
        x, y, c, _ = _mesh_pos()
        myq = 2 * x + y
        for j, (px, py) in enumerate(_other_chips(x, y)):
            _rcopy(p_ref.at[2 * px + py], land_ref.at[myq], send_sems, recv_sems, j, (px, py, c)).start()
        token[...] = jnp.zeros_like(token)

    land = pltpu.with_memory_space_constraint(lax.empty(part.shape, part.dtype), pltpu.HBM)
    extra = () if order_after is None else (order_after,)
    return _pcall(
        body, name=name,
        out_shape=(pltpu.SemaphoreType.DMA((3,)), pltpu.SemaphoreType.DMA((3,)), pltpu.HBM(part.shape, part.dtype),
                   pltpu.HBM(part.shape, part.dtype), jax.ShapeDtypeStruct((8, 128), F32)),
        in_specs=(_HBM, _HBM) + (_ANY,) * len(extra),
        out_specs=(_SEM, _SEM, _HBM, _HBM, pl.BlockSpec(memory_space=pltpu.VMEM)),
        input_output_aliases={0: 2, 1: 3}, compiler_params=pltpu.CompilerParams(has_side_effects=_EFFECT),
    )(pltpu.with_memory_space_constraint(part, pltpu.HBM), land, *extra)


def _chips_wait(send_sems, recv_sems, p_thru, land_thru, after, name):
    def body(p_ref, land_ref, send_sems, recv_sems, after_ref, p_out, land_out):
        x, y, c, _ = _mesh_pos()
        for j, (px, py) in enumerate(_other_chips(x, y)):
            q = 2 * px + py
            cp = _rcopy(p_ref.at[q], land_ref.at[q], send_sems, recv_sems, j, (px, py, c))
            cp.wait_send()
            cp.wait_recv()

    return _pcall(
        body, name=name, out_shape=(pltpu.HBM(p_thru.shape, p_thru.dtype), pltpu.HBM(p_thru.shape, p_thru.dtype)),
        in_specs=(_HBM, _HBM, _SEM, _SEM, _ANY), out_specs=(_HBM, _HBM), input_output_aliases={0: 0, 1: 1},
        compiler_params=pltpu.CompilerParams(has_side_effects=_EFFECT),
    )(p_thru, land_thru, send_sems, recv_sems, after)


def _sum_chips(part, land, qvec, name):
    n, R, C = part.shape
    tr = _pick(R, _ROW_TILES)

    def body(q_ref, p_ref, *refs):
        o_ref = refs[n]
        acc = None
        for q in range(n):
            term = jnp.where(q_ref[0] == q, p_ref[...], refs[q][...]).astype(F32)
            acc = term if acc is None else acc + term
        o_ref[...] = acc

    def land_spec(q):
        return pl.BlockSpec((None, tr, C), lambda i, m: (jnp.where(m[0] == q, (q + 1) % n, q), i, 0))

    grid_spec = pltpu.PrefetchScalarGridSpec(
        num_scalar_prefetch=1, grid=(R // tr,),
        in_specs=[pl.BlockSpec((None, tr, C), lambda i, m: (m[0], i, 0))] + [land_spec(q) for q in range(n)],
        out_specs=pl.BlockSpec((tr, C), lambda i, m: (i, 0)))
    return _pcall(body, grid_spec=grid_spec, out_shape=jax.ShapeDtypeStruct((R, C), F32), name=name)(qvec, part, *([land] * n))


def _exchange_small(gsmall):
    def body(s_ref, srecv_ref, send_sems, recv_sems, loc_sem):
        x, y, c, me = _mesh_pos()
        loc = pltpu.make_async_copy(s_ref, srecv_ref.at[me], loc_sem.at[0])
        loc.start()
        sends, recvs = [], []
        for k in range(1, N_DEV):
            to, pidx = _peer(x, y, c, k)
            sends.append(_rcopy(s_ref, srecv_ref.at[me], send_sems, recv_sems, k - 1, to))
            recvs.append(_rcopy(s_ref, srecv_ref.at[pidx], send_sems, recv_sems, k - 1, to))
        for cp in sends:
            cp.start()
        for cp in recvs:
            cp.wait_recv()
        for cp in sends:
            cp.wait_send()
        loc.wait()

    return _pcall(
        body, in_specs=[_ANY], out_specs=_ANY, out_shape=jax.ShapeDtypeStruct((N_DEV,) + gsmall.shape, gsmall.dtype),
        scratch_shapes=[pltpu.SemaphoreType.DMA((7,)), pltpu.SemaphoreType.DMA((7,)), pltpu.SemaphoreType.DMA((1,))],
        name="exchange_small",
    )(gsmall)


def _sum_slots(recv, name):
    n, R, C = recv.shape
    tr = _pick(R, _ROW_TILES)

    def body(r_ref, o_ref):
        acc = r_ref[0].astype(F32)
        for s in range(1, n):
            acc = acc + r_ref[s].astype(F32)
        o_ref[...] = acc

    return _pcall(body, grid=(R // tr,), in_specs=[pl.BlockSpec((n, tr, C), lambda i: (0, i, 0))],
                  out_specs=pl.BlockSpec((tr, C), lambda i: (i, 0)), out_shape=jax.ShapeDtypeStruct((R, C), F32), name=name)(recv)


def _adamw(w, g, m, v, name, order_after=None):
    shape = w.shape
    w2, g2, m2, v2 = (a.reshape(-1, shape[-1]) for a in (w, g, m, v))
    R, C = w2.shape
    tr = _pick(R, (256, 128, 64, 32, 16, 8)) if R > 256 else R
    n_in = 4 + int(order_after is not None)

    def body(*refs):
        w_ref, g_ref, m_ref, v_ref = refs[:4]
        d_ref, nm_ref, nv_ref = refs[n_in:]
        gg = g_ref[...]
        nm = ADAM_B1 * m_ref[...] + (1.0 - ADAM_B1) * gg
        nv = ADAM_B2 * v_ref[...] + (1.0 - ADAM_B2) * jnp.square(gg)
        m_hat = nm / (1.0 - ADAM_B1 ** ADAM_STEP)
        v_hat = nv / (1.0 - ADAM_B2 ** ADAM_STEP)
        d_ref[...] = -ADAM_LR * (m_hat / (jnp.sqrt(v_hat) + ADAM_EPS) + ADAM_WD * w_ref[...])
        nm_ref[...] = nm
        nv_ref[...] = nv

    spec = pl.BlockSpec((tr, C), lambda i: (i, 0))
    in_specs, args = [spec] * 4, [w2, g2, m2, v2]
    if order_after is not None:
        in_specs.append(pl.BlockSpec(memory_space=pl.ANY))
        args.append(order_after)
    d, nm, nv = _pcall(body, grid=(R // tr,), in_specs=in_specs, out_specs=[spec] * 3,
                       out_shape=[jax.ShapeDtypeStruct((R, C), F32)] * 3, name=name)(*args)
    return d.reshape(shape), nm.reshape(shape), nv.reshape(shape)


def _to_rows(name, w):
    w = w[0]
    if name in ("ffn1_w_in", "ffn2_w_in", "w_in"):
        return w.T
    if name in ("w_uq", "w_ukv", "ple_w_proj"):
        return w.T.reshape(-1, 1024)
    return w


def _from_rows(name, g, shape):
    if name in ("ffn1_w_in", "ffn2_w_in", "w_in"):
        return g.T.reshape(shape)
    if name in ("w_uq", "w_ukv", "ple_w_proj"):
        return g.reshape(-1, shape[1]).T.reshape(shape)
    return g.reshape(shape)


def _unpack(wall, group):
    out, off = {}, 0
    for name, r in group:
        out[name] = wall[:, off:off + r, :].reshape(N_DEV * r, 1024)
        off += _pad16(r)
    return out


def _w_in_internal(wt):
    return jnp.concatenate([wt[0:6144], wt[6720:8768], wt[6144:6720], jnp.zeros((P_W - W_IN_COLS, 1024), wt.dtype)], axis=0)


def _rope_tables(positions):
    pos = positions[0].astype(F32)

    def cs(half):
        inv = jnp.tile(ROPE_BASE ** (-jnp.arange(half, dtype=F32) / half), 128 // half)
        sign = jnp.tile(jnp.concatenate([-jnp.ones((half,), F32), jnp.ones((half,), F32)]), 64 // half)
        ang = pos[:, None] * inv
        return jnp.cos(ang), jnp.sin(ang) * sign

    return cs(64) + cs(32)


def _local_step(x, p, positions, target, W, ln_g, ln_b, gn_g, qg, kvg, early=None, rest_weights=None, start_token=None):
    T = x.shape[0]
    tabs = _rope_tables(positions)
    rc = _ret_consts()
    lg = [ln_g[i:i + 1] for i in range(4)]
    lb = [ln_b[i:i + 1] for i in range(4)]
    pb = p.astype(BF16)

    hs1, gu1, *xb = _ffn_in(x, W["ffn1_w_in"], "ffn1_in", order_after=start_token)
    xb = xb[0] if xb else x
    f1, h1, h1b = _mm(hs1, W["ffn1_w_out"], name="ffn1_out", tm=LN_TM, epilogue=_ln_epilogue(x, 0.5, lg[0], lb[0]))
    if rest_weights is not None:
        W = {**W, **rest_weights(h1b)}
    w_in_t = _w_in_internal(W["w_in"])
    wuq = W["w_uq"].reshape(1536, LORA).reshape(HEADS, 192, LORA)
    wuq = jnp.concatenate([wuq[:, :128].reshape(1024, LORA), wuq[:, 128:].reshape(512, LORA)], axis=0)
    wukv = W["w_ukv"].reshape(2048, LORA).reshape(HEADS, 2, 128, LORA).transpose(1, 0, 2, 3).reshape(2048, LORA)
    wp_t = W["ple_w_proj"].reshape(1024, D_PLE)
    proj = _mm(h1b, w_in_t, tb=True, out_dtype=BF16, name="mixer_in")
    rq, rk, qn, kvn, kpe = _mixer_prep_fwd(proj, tabs, qg, kvg, T)
    y, yr, states = _ret_fwd(rq, rk, proj, proj, gn_g, rc, T)
    y_ret = _mm(yr, W["w_ret_o"], out_dtype=BF16, name="ret_o")
    qnope, qpe = _mm(qn, wuq, tb=True, name="mla_uq", tm=LN_TM, tn=1536, epilogue=_q_assemble_epilogue(tabs))
    kv = _mm(kvn, wukv, tb=True, out_dtype=BF16, name="mla_ukv")
    o, lse = _attn_fwd(qnope, qpe, kv, kpe, T)
    y_mla, mix = _mm(o, W["w_mla_o"], name="mla_o", tm=LN_TM, epilogue=_mix_epilogue(proj, y_ret))
    mixed, h2, h2b = _mm(mix, W["w_out"], name="mixer_out", tm=LN_TM, epilogue=_ln_epilogue(h1, 1.0, lg[1], lb[1]))
    hs2, gu2 = _ffn_in(h2b, W["ffn2_w_in"], "ffn2_in")
    f2, h3, h3b = _mm(hs2, W["ffn2_w_out"], name="ffn2_out", tm=LN_TM, epilogue=_ln_epilogue(h2, 0.5, lg[2], lb[2]))
    pp = _mm(pb, wp_t, tb=True, name="ple_proj")

    G = {}
    dh3_a, dgl, dpp, dg3, db3, loss = _mm(h3b, W["ple_w_gate"], name="ple_gate", tm=LN_TM,
                                          epilogue=_head_epilogue(h3, pp, target, lg[3], lb[3]))
    G["ple_w_gate"] = _mm(h3b, dgl, ta=True, name="d_ple_gate")
    G["ple_w_proj"] = _mm(dpp, pb, ta=True, name="d_ple_proj")
    dh2_a, df2, dg2, db2 = _mm(dgl, W["ple_w_gate"], tb=True, add=dh3_a, name="dh3", tm=LN_TM,
                               epilogue=_ln_bwd_epilogue(h2, f2, 0.5, lg[2], lb[2]))
    G["ffn2_w_out"] = _mm(hs2, df2, ta=True, name="d_ffn2_out")
    da2 = _ffn_act_bwd(df2, W["ffn2_w_out"], gu2, "ffn2_act_bwd")
    G["ffn2_w_in"] = _mm(da2, h2b, ta=True, name="d_ffn2_in")
    dh1_a, dmixed, dg1, db1 = _mm(da2, W["ffn2_w_in"], add=dh2_a, name="dh2", tm=LN_TM,
                                  epilogue=_ln_bwd_epilogue(h1, mixed, 1.0, lg[1], lb[1]))
    G["w_out"] = _mm(mix, dmixed, ta=True, name="d_mixer_out")
    dp_gate, dy_ret, dy_mla = _mm(dmixed, W["w_out"], tb=True, name="dmix", tm=LN_TM,
                                  epilogue=_mix_bwd_epilogue(proj, y_ret, y_mla))
    G["w_mla_o"] = _mm(o, dy_mla, ta=True, name="d_mla_o")
    dob, delta = _mm(dy_mla, W["w_mla_o"], tb=True, name="do", tm=LN_TM, epilogue=_delta_epilogue(o))
    dqn_f, dqpe_f, dkn, dkpe_all, dv = _attn_bwd(qnope, qpe, kv, kpe, dob, lse, delta, T)
    dq_n, dq_r = _q_assemble_bwd(dqn_f, dqpe_f, tabs, T)
    g_uq = jnp.concatenate([_mm(dq_n, qn, ta=True, name="d_uq_nope"), _mm(dq_r, qn, ta=True, name="d_uq_rope")], axis=0)
    g_uq = jnp.concatenate([g_uq[:1024].reshape(HEADS, 128, LORA), g_uq[1024:].reshape(HEADS, 64, LORA)], axis=1)
    G["w_uq"] = g_uq.reshape(1536 * LORA // 1024, 1024)
    dqn = _mm(dq_r, wuq[1024:], add=_mm(dq_n, wuq[:1024], name="dqn_a"), name="dqn_b")
    g_ukv = jnp.stack([_mm(dkn, kvn, ta=True, name="d_ukv_k"), _mm(dv, kvn, ta=True, name="d_ukv_v")], axis=0)
    G["w_ukv"] = g_ukv.reshape(2, HEADS, 128, LORA).transpose(1, 0, 2, 3).reshape(2048 * LORA // 1024, 1024)
    dkvn = _mm(dv, wukv[1024:], add=_mm(dkn, wukv[:1024], name="dkvn_a"), name="dkvn_b")
    dp_mla, dqg, dkvg = _rms_bwd(proj, dqn, dkvn, dkpe_all, tabs, qg, kvg, T)
    G["w_ret_o"] = _mm(yr, dy_ret, ta=True, name="d_ret_o")
    dyr = _mm(dy_ret, W["w_ret_o"], tb=True, out_dtype=BF16, name="dyr")
    dp_ret, dgn = _ret_bwd(rq, rk, proj, y, proj, gn_g, states, dyr, rc, tabs, T)
    g_ret, g_gate, g_mla = (_mm(dp, h1b, ta=True, name="d_mixer_in_" + n)
                            for dp, n in ((dp_ret, "ret"), (dp_gate, "gate"), (dp_mla, "mla")))
    G["w_in"] = jnp.concatenate([g_ret, g_mla[:W_IN_COLS - P_CQ], g_gate], axis=0)
    lg0 = lg[0] if early is None else lg[0] + early(G)[0:1, 0:1]
    dh1 = _mm_pieces([(dp_ret, w_in_t[:P_GR]), (dp_gate, w_in_t[P_GR:P_CQ]), (dp_mla, w_in_t[P_CQ:])], dh1_a, "dh1")
    dx_a, df1, dg0, db0 = _ln_bwd_rows(x, f1, 0.5, lg0, lb[0], dh1, T, "ln0_bwd")
    G["ffn1_w_out"] = _mm(hs1, df1, ta=True, name="d_ffn1_out")
    da1 = _ffn_act_bwd(df1, W["ffn1_w_out"], gu1, "ffn1_act_bwd")
    G["ffn1_w_in"] = _mm(da1, xb, ta=True, name="d_ffn1_in")
    grad_x = _mm(da1, W["ffn1_w_in"], add=dx_a, name="grad_x")

    small = dict(ln_g=jnp.concatenate([dg0, dg1, dg2, dg3], axis=0), ln_b=jnp.concatenate([db0, db1, db2, db3], axis=0),
                 ret_gn_g=dgn, q_norm_g=dqg, kv_norm_g=dkvg)
    return loss, grad_x, G, small


def kernel(x, p, positions, ln_g, ln_b, ffn1_w_in, ffn1_w_out, w_in, ret_gn_g, w_ret_o, q_norm_g, kv_norm_g, w_uq, w_ukv, w_mla_o, w_out, ffn2_w_in, ffn2_w_out, ple_w_gate, ple_w_proj, loss_target, m_ln_g, m_ln_b, m_ffn1_w_in, m_ffn1_w_out, m_w_in, m_ret_gn_g, m_w_ret_o, m_q_norm_g, m_kv_norm_g, m_w_uq, m_w_ukv, m_w_mla_o, m_w_out, m_ffn2_w_in, m_ffn2_w_out, m_ple_w_gate, m_ple_w_proj, v_ln_g, v_ln_b, v_ffn1_w_in, v_ffn1_w_out, v_w_in, v_ret_gn_g, v_w_ret_o, v_q_norm_g, v_kv_norm_g, v_w_uq, v_w_ukv, v_w_mla_o, v_w_out, v_ffn2_w_in, v_ffn2_w_out, v_ple_w_gate, v_ple_w_proj):
    names = ("ln_g", "ln_b", "ffn1_w_in", "ffn1_w_out", "w_in", "ret_gn_g", "w_ret_o", "q_norm_g", "kv_norm_g", "w_uq", "w_ukv",
             "w_mla_o", "w_out", "ffn2_w_in", "ffn2_w_out", "ple_w_gate", "ple_w_proj")
    ws = dict(zip(names, (ln_g, ln_b, ffn1_w_in, ffn1_w_out, w_in, ret_gn_g, w_ret_o, q_norm_g, kv_norm_g, w_uq, w_ukv, w_mla_o,
                          w_out, ffn2_w_in, ffn2_w_out, ple_w_gate, ple_w_proj)))
    ms = dict(zip(names, (m_ln_g, m_ln_b, m_ffn1_w_in, m_ffn1_w_out, m_w_in, m_ret_gn_g, m_w_ret_o, m_q_norm_g, m_kv_norm_g, m_w_uq,
                          m_w_ukv, m_w_mla_o, m_w_out, m_ffn2_w_in, m_ffn2_w_out, m_ple_w_gate, m_ple_w_proj)))
    vs = dict(zip(names, (v_ln_g, v_ln_b, v_ffn1_w_in, v_ffn1_w_out, v_w_in, v_ret_gn_g, v_w_ret_o, v_q_norm_g, v_kv_norm_g, v_w_uq,
                          v_w_ukv, v_w_mla_o, v_w_out, v_ffn2_w_in, v_ffn2_w_out, v_ple_w_gate, v_ple_w_proj)))

    parts = []
    for name, r in PACK:
        rows = _to_rows(name, ws[name])
        if _pad16(r) != r:
            rows = jnp.concatenate([rows, jnp.zeros((_pad16(r) - r, 1024), F32)], axis=0)
        parts.append(rows)
    wsh_first = jnp.concatenate(parts[:len(PACK_LATE)], axis=0).astype(BF16)
    wsh_rest = jnp.concatenate(parts[len(PACK_LATE):], axis=0).astype(BF16)
    ssh = jnp.concatenate([ln_g[0], ln_b[0]], axis=0)
    wall_first, sall = _all_gather(wsh_first, ssh)
    *gather_handles, start_token = _gather_start(wsh_rest, wall_first)
    W = _unpack(wall_first, PACK_LATE)
    ln_full = sall.reshape(N_DEV, 2, 4, 128).transpose(1, 2, 0, 3).reshape(2, 4, 1024)

    def rest_weights(after):
        w_thru, land = _gather_wait(*gather_handles, after)
        return _unpack(_gather_finish(w_thru, land), PACK_EARLY)

    cvec = lax.axis_index("c").astype(jnp.int32).reshape(1)
    qvec = (2 * lax.axis_index("x") + lax.axis_index("y")).astype(jnp.int32).reshape(1)

    def chip_partials(G, group, tag):
        gparts = []
        for name, r in group:
            g = G[name].reshape(N_DEV, r, 1024)
            if _pad16(r) != r:
                g = jnp.concatenate([g, jnp.zeros((N_DEV, _pad16(r) - r, 1024), g.dtype)], axis=1)
            gparts.append(g)
        gfull = jnp.concatenate(gparts, axis=1).astype(BF16)
        g4 = gfull.reshape(4, 2, gfull.shape[1], 1024)
        return _sum_sibling(g4, _exchange_sibling(g4, "exchange_sibling_" + tag), cvec, "sum_sibling_" + tag)

    in_flight = []

    def early(G):
        *handles, token = _chips_start(chip_partials(G, PACK_EARLY, "early"), "chips_start_early")
        in_flight.append(handles)
        return token

    loss_p, grad_x, G, small = _local_step(x[0], p[0, 0], positions, loss_target[0], W, ln_full[0], ln_full[1],
                                           ret_gn_g, q_norm_g, kv_norm_g, early=early, rest_weights=rest_weights,
                                           start_token=start_token)

    part_e, land_e = _chips_wait(*in_flight[0], grad_x, "chips_wait_early")
    gsh_early = _sum_chips(part_e, land_e, qvec, "sum_grads_early")
    pad256 = lambda a: jnp.concatenate([a, jnp.zeros((1, 1024 - a.shape[1]), F32)], axis=1)
    gsmall = jnp.concatenate([small["ln_g"], small["ln_b"], small["ret_gn_g"].reshape(2, 1024), pad256(small["q_norm_g"]),
                              pad256(small["kv_norm_g"]), pad256(loss_p[0:1, 0:128]),
                              jnp.zeros((SMALL_ROWS - 13, 1024), F32)], axis=0)
    srecv = _exchange_small(gsmall)
    part_l = chip_partials(G, PACK_LATE, "late")
    *late_handles, late_token = _chips_start(part_l, "chips_start_late", order_after=srecv)
    ssum = _sum_slots(srecv, "sum_small_grads")

    def unpack_grads(group, gsh):
        out, off = {}, 0
        for name, r in group:
            out[name] = _from_rows(name, gsh[off:off + r], ws[name].shape)
            off += _pad16(r)
        return out

    grads = unpack_grads(PACK_EARLY, gsh_early)
    me = 4 * lax.axis_index("x") + 2 * lax.axis_index("y") + lax.axis_index("c")
    grads["ln_g"] = lax.dynamic_slice(ssum[0:4], (0, me * 128), (4, 128)).reshape(1, 4, 128)
    grads["ln_b"] = lax.dynamic_slice(ssum[4:8], (0, me * 128), (4, 128)).reshape(1, 4, 128)
    grads["ret_gn_g"] = ssum[8:10].reshape(1, 2048)
    grads["q_norm_g"] = ssum[10:11, :256]
    grads["kv_norm_g"] = ssum[11:12, :256]

    delta, new_m, new_v = {}, {}, {}
    late_names = [n for n, _ in PACK_LATE]
    last = late_token
    for name in names:
        if name not in late_names:
            delta[name], new_m[name], new_v[name] = _adamw(ws[name], grads[name], ms[name], vs[name], "adamw_" + name,
                                                           order_after=last)
            last = new_v[name]
    part_l, land_l = _chips_wait(*late_handles, last, "chips_wait_late")
    grads.update(unpack_grads(PACK_LATE, _sum_chips(part_l, land_l, qvec, "sum_grads_late")))
    for name in late_names:
        delta[name], new_m[name], new_v[name] = _adamw(ws[name], grads[name], ms[name], vs[name], "adamw_" + name)

    return (ssum[12, 0], grad_x[None], *[grads[n] for n in names], *[delta[n] for n in names],
            *[new_m[n] for n in names], *[new_v[n] for n in names])
```

```python
import math

import jax
import jax.numpy as jnp
from jax import lax
from jax.experimental import pallas as pl
from jax.experimental.pallas import tpu as pltpu

F32 = jnp.float32
BF16 = jnp.bfloat16

N_DEV = 8
D = 1024
D_FF = 2816
D_PLE = 256
CHUNK = 64
HEADS = 8
RET_DK = 128
RET_DV = 256
MLA_NOPE = 128
MLA_ROPE = 64
MLA_DV = 128
LORA = 256
ROPE_BASE = 10000.0
EPS = 1e-5
ALPHA = 2.0 ** 0.25
RET_SCALE = RET_DK ** -0.5
MLA_SCALE = (MLA_NOPE + MLA_ROPE) ** -0.5
NEG = -1e30

ADAM_LR = 0.001
ADAM_B1 = 0.9
ADAM_B2 = 0.999
ADAM_EPS = 1e-08
ADAM_WD = 0.01
ADAM_STEP = 10

P_RQ, P_RK, P_RV, P_RG, P_GR, P_GM, P_CQ, P_CKV, P_KPE, P_W = 0, 1024, 2048, 4096, 6144, 7168, 8192, 8448, 8704, 8960
W_IN_COLS = 8768
RET_L = 256
ATT_TF = 2048
ATT_TB = 1024
ATT_HP = 1
LOG2E = math.log2(math.e)
Q_PRESCALE = MLA_SCALE * LOG2E

PACK = (("ffn1_w_in", 704), ("ffn1_w_out", 352), ("w_in", 1096), ("w_ret_o", 256), ("w_uq", 48), ("w_ukv", 64),
        ("w_mla_o", 128), ("w_out", 128), ("ffn2_w_in", 704), ("ffn2_w_out", 352), ("ple_w_gate", 128), ("ple_w_proj", 32))


def _pad16(r):
    return -(-r // 16) * 16


PACK_LATE = PACK[:2]
PACK_EARLY = PACK[2:]
SMALL_ROWS = 16


def _pcall(body, **kw):
    return pl.pallas_call(body, **kw)


def _pick(dim, prefs):
    for p in prefs:
        if dim % p == 0:
            return p
    return dim


def _sigmoid(x):
    return 1.0 / (1.0 + jnp.exp(-x))


def _silu(x):
    return x * _sigmoid(x)


def _ln(r, g, b):
    mu = jnp.mean(r, axis=-1, keepdims=True)
    var = jnp.mean(jnp.square(r - mu), axis=-1, keepdims=True)
    return (r - mu) * lax.rsqrt(var + EPS) * g + b


def _rms(x, g):
    return x * lax.rsqrt(jnp.mean(jnp.square(x), axis=-1, keepdims=True) + EPS) * g


def _dot(a, b, ca, cb):
    return lax.dot_general(a, b, (((ca,), (cb,)), ((), ())), preferred_element_type=F32)


def _accum(ref, val, first=None):
    @pl.when(pl.program_id(0) == 0 if first is None else first)
    def _():
        ref[...] = jnp.zeros_like(ref)

    ref[...] += val


def _mm(a, b, *, ta=False, tb=False, add=None, out_dtype=None, name, tm=None, tn=None, tk=None, epilogue=None):
    parts = a.shape[0] if a.ndim == 3 else 1
    ar, ac = a.shape[-2], a.shape[-1]
    out_dtype = out_dtype or (BF16 if ta else F32)
    if ta:
        K, M = ar, ac * parts
    else:
        M, K = ar, ac * parts
    if tb:
        N, K2 = b.shape
    else:
        K2, N = b.shape
    assert K == K2, (a.shape, b.shape, ta, tb)
    big = (1024, 1408, 1280, 768, 512, 256, 128)
    tm = tm or _pick(ac if (ta and parts > 1) else M, big)
    tn = tn or (N if N <= 1024 else _pick(N, big))
    kdim = ac if (not ta and parts > 1) else K
    tk = tk or (kdim if kdim <= 2816 and parts == 1 else
                _pick(kdim, (2048, 1408, 1280, 1024, 512) if tm <= 1024 else (1024, 1408, 1280, 512)))
    nk = K // tk
    grid = (M // tm, N // tn, nk)
    if parts > 1 and ta:
        per = ac // tm
        a_spec = pl.BlockSpec((None, tk, tm), lambda i, j, k: (i // per, k, i % per))
    elif parts > 1:
        per = ac // tk
        a_spec = pl.BlockSpec((None, tm, tk), lambda i, j, k: (k // per, i, k % per))
    else:
        a_spec = pl.BlockSpec((tk, tm), lambda i, j, k: (k, i)) if ta else pl.BlockSpec((tm, tk), lambda i, j, k: (i, k))
    b_spec = pl.BlockSpec((tn, tk), lambda i, j, k: (j, k)) if tb else pl.BlockSpec((tk, tn), lambda i, j, k: (k, j))
    o_spec = pl.BlockSpec((tm, tn), lambda i, j, k: (i, j))
    ca, cb = (0 if ta else 1), (1 if tb else 0)
    has_add = add is not None
    n_in = 2 + int(has_add)
    if epilogue is not None:
        assert tn == N and not ta
        ep_fn, ep_rows, ep_whole, ep_outs, ep_accs = epilogue
        n_ep_in = len(ep_rows) + len(ep_whole)
        n_out = len(ep_outs) + len(ep_accs)
    else:
        n_ep_in, n_out = 0, 1

    def body(*refs):
        a_ref, b_ref = refs[0], refs[1]
        add_ref = refs[2] if has_add else None
        o_ref = refs[n_in + n_ep_in]
        first_row_tile = pl.program_id(0) == 0

        def finish(r):
            if has_add:
                r = r + add_ref[...].astype(F32)
            if epilogue is not None:
                ep_fn(r, refs[n_in:n_in + n_ep_in], refs[n_in + n_ep_in:n_in + n_ep_in + n_out], first_row_tile)
            else:
                o_ref[...] = r.astype(out_dtype)

        if nk == 1:
            finish(_dot(a_ref[...], b_ref[...], ca, cb))
            return
        acc_ref = refs[-1]
        k = pl.program_id(2)

        @pl.when(k == 0)
        def _():
            acc_ref[...] = jnp.zeros_like(acc_ref)

        acc_ref[...] += _dot(a_ref[...], b_ref[...], ca, cb)

        @pl.when(k == nk - 1)
        def _():
            finish(acc_ref[...])

    in_specs = [a_spec, b_spec] + ([o_spec] if has_add else [])
    args = (a, b) + ((add,) if has_add else ())
    out_specs, out_shape = o_spec, jax.ShapeDtypeStruct((M, N), out_dtype)
    if epilogue is not None:
        for row_in in ep_rows:
            arr, col_block, width = (tuple(row_in) + (N,))[:3] if isinstance(row_in, tuple) else (row_in, 0, N)
            in_specs.append(pl.BlockSpec((tm, width), lambda i, j, k, _c=col_block: (i, _c)))
            args += (arr,)
        in_specs += [pl.BlockSpec(w.shape, lambda i, j, k, _n=w.ndim: (0,) * _n) for w in ep_whole]
        args += tuple(ep_whole)
        outs = [o if isinstance(o, tuple) else (o, N) for o in ep_outs]
        out_specs = ([pl.BlockSpec((tm, w), lambda i, j, k: (i, 0)) for _, w in outs]
                     + [pl.BlockSpec((r, N), lambda i, j, k: (0, 0)) for r in ep_accs])
        out_shape = [jax.ShapeDtypeStruct((M, w), dt) for dt, w in outs] + [jax.ShapeDtypeStruct((r, N), F32) for r in ep_accs]
    return _pcall(
        body, grid=grid, in_specs=in_specs, out_specs=out_specs, out_shape=out_shape,
        scratch_shapes=[pltpu.VMEM((tm, tn), F32)] if nk > 1 else [], name=name,
        compiler_params=pltpu.CompilerParams(dimension_semantics=("arbitrary" if epilogue is not None else "parallel", "parallel", "arbitrary")),
    )(*args)


def _mm_pieces(pieces, add, name):
    M, N = add.shape
    tm = _pick(M, (1024, 512, 256, 128))
    spans, in_specs, args, start = [], [], [], 0
    for a, b in pieces:
        K = a.shape[1]
        assert b.shape == (K, N) and a.shape[0] == M, (a.shape, b.shape)
        tk = _pick(K, (1536, 1024, 768, 512, 256, 128))
        n = K // tk
        spans.append((start, n))
        block = lambda k, _s=start, _n=n: jnp.where(k < _s, _n - 1, jnp.minimum(k - _s, _n - 1))
        in_specs.append(pl.BlockSpec((tm, tk), lambda i, k, _s=start, _blk=block: (jnp.where(k < _s, jnp.maximum(i - 1, 0), i), _blk(k))))
        in_specs.append(pl.BlockSpec((tk, N), lambda i, k, _blk=block: (_blk(k), 0)))
        args += [a, b]
        start += n
    o_spec = pl.BlockSpec((tm, N), lambda i, k: (i, 0))

    def body(*refs):
        add_ref, o_ref = refs[-2], refs[-1]
        k = pl.program_id(1)

        @pl.when(k == 0)
        def _():
            o_ref[...] = add_ref[...]

        for p, (s, n) in enumerate(spans):
            @pl.when((k >= s) & (k < s + n))
            def _(p=p):
                o_ref[...] += _dot(refs[2 * p][...], refs[2 * p + 1][...], 1, 0)

    return _pcall(
        body, grid=(M // tm, start), in_specs=in_specs + [o_spec], out_specs=o_spec, out_shape=jax.ShapeDtypeStruct((M, N), F32),
        name=name, compiler_params=pltpu.CompilerParams(dimension_semantics=("parallel", "arbitrary")),
    )(*args, add)


def _ln_epilogue(res, c, g, b):
    def fn(r, ins, outs, first):
        res_ref, g_ref, b_ref = ins
        f_ref, h_ref, hb_ref = outs
        h = _ln(ALPHA * res_ref[...] + c * r, g_ref[...], b_ref[...])
        f_ref[...] = r
        h_ref[...] = h
        hb_ref[...] = h.astype(BF16)

    return (fn, [res], [g, b], (F32, F32, BF16), ())


def _ln_bwd_epilogue(res, f, c, g, b):
    def fn(r, ins, outs, first):
        res_ref, f_ref, g_ref, b_ref = ins
        dr_ref, df_ref, dg_ref, db_ref = outs
        pre = ALPHA * res_ref[...] + c * f_ref[...]
        xc = pre - jnp.mean(pre, axis=-1, keepdims=True)
        rstd = lax.rsqrt(jnp.mean(jnp.square(xc), axis=-1, keepdims=True) + EPS)
        xhat = xc * rstd
        dyg = r * g_ref[...]
        dpre = rstd * (dyg - jnp.mean(dyg, axis=-1, keepdims=True) - xhat * jnp.mean(dyg * xhat, axis=-1, keepdims=True))
        dr_ref[...] = ALPHA * dpre
        df_ref[...] = (c * dpre).astype(BF16)
        _accum(dg_ref, jnp.sum(r * xhat, axis=0, keepdims=True), first)
        _accum(db_ref, jnp.sum(r, axis=0, keepdims=True), first)

    return (fn, [res, f], [g, b], (F32, BF16), (1, 1))


ROW_TM = 1024


def _rows(body, T, tm, ins, outs, name, accs=()):
    in_specs, args = [], []
    for arr, w, cb in ins:
        if w is None:
            in_specs.append(pl.BlockSpec(arr.shape, lambda i, _n=arr.ndim: (0,) * _n))
        else:
            in_specs.append(pl.BlockSpec((tm, w), lambda i, _cb=cb: (i, _cb)))
        args.append(arr)
    out_specs = [pl.BlockSpec((tm, w), lambda i: (i, 0)) for w, _ in outs]
    out_shape = [jax.ShapeDtypeStruct((T, w), dt) for w, dt in outs]
    for r, w in accs:
        out_specs.append(pl.BlockSpec((r, w), lambda i: (0, 0)))
        out_shape.append(jax.ShapeDtypeStruct((r, w), F32))
    return _pcall(
        body, grid=(T // tm,), in_specs=in_specs, out_specs=out_specs, out_shape=out_shape, name=name,
        compiler_params=pltpu.CompilerParams(dimension_semantics=("arbitrary",)),
    )(*args)


def _ln_bwd_rows(res, f, c, g, b, dh, T, name):
    fn = _ln_bwd_epilogue(res, f, c, g, b)[0]

    def body(r_ref, f_ref, g_ref, b_ref, d_ref, dr_ref, df_ref, dg_ref, db_ref):
        fn(d_ref[...], (r_ref, f_ref, g_ref, b_ref), (dr_ref, df_ref, dg_ref, db_ref), pl.program_id(0) == 0)

    return _rows(body, T, min(ROW_TM, T), [(res, D, 0), (f, D, 0), (g, None, None), (b, None, None), (dh, D, 0)],
                 [(D, F32), (D, BF16)], name, accs=[(1, D), (1, D)])


FFN_TN = 1408
FFN_TM = 512
LN_TM = 512


def _ffn_in(x, wt, name, order_after=None):
    T = x.shape[0]
    tm, tn = min(FFN_TM, T), FFN_TN
    nj = D_FF // tn
    emit_xb = x.dtype != BF16
    n_in = 3 + int(order_after is not None)

    def body(*refs):
        x_ref, wg_ref, wu_ref = refs[:3]
        hs_ref, gu_ref = refs[n_in], refs[n_in + 1]
        xv = x_ref[...].astype(BF16)
        g = _dot(xv, wg_ref[...], 1, 1)
        u = _dot(xv, wu_ref[...], 1, 1)
        hs_ref[...] = (_silu(g) * u).astype(BF16)
        gu_ref[0] = g.astype(BF16)
        gu_ref[1] = u.astype(BF16)
        if emit_xb:
            refs[n_in + 2][...] = xv

    in_specs = [pl.BlockSpec((tm, D), lambda i, j: (i, 0)), pl.BlockSpec((tn, D), lambda i, j: (j, 0)),
                pl.BlockSpec((tn, D), lambda i, j: (j + nj, 0))]
    args = [x, wt, wt]
    if order_after is not None:
        in_specs.append(pl.BlockSpec(order_after.shape, lambda i, j: (0, 0)))
        args.append(order_after)
    out_specs = [pl.BlockSpec((tm, tn), lambda i, j: (i, j)), pl.BlockSpec((2, tm, tn), lambda i, j: (0, i, j))]
    out_shape = [jax.ShapeDtypeStruct((T, D_FF), BF16), jax.ShapeDtypeStruct((2, T, D_FF), BF16)]
    if emit_xb:
        out_specs.append(pl.BlockSpec((tm, D), lambda i, j: (i, 0)))
        out_shape.append(jax.ShapeDtypeStruct((T, D), BF16))
    return _pcall(
        body, grid=(T // tm, nj), in_specs=in_specs, out_specs=out_specs, out_shape=out_shape, name=name,
        compiler_params=pltpu.CompilerParams(dimension_semantics=("parallel", "arbitrary")),
    )(*args)


def _ffn_act_bwd(df, wo, gu, name):
    T = df.shape[0]
    tm, tn = min(FFN_TM, T), FFN_TN

    def body(d_ref, w_ref, gu_ref, o_ref):
        dhs = _dot(d_ref[...], w_ref[...], 1, 1)
        g, u = gu_ref[0].astype(F32), gu_ref[1].astype(F32)
        sig = _sigmoid(g)
        act = g * sig
        o_ref[0] = (dhs * u * (sig + act * (1.0 - sig))).astype(BF16)
        o_ref[1] = (dhs * act).astype(BF16)

    return _pcall(
        body, grid=(T // tm, D_FF // tn),
        in_specs=[pl.BlockSpec((tm, D), lambda i, j: (i, 0)), pl.BlockSpec((tn, D), lambda i, j: (j, 0)),
                  pl.BlockSpec((2, tm, tn), lambda i, j: (0, i, j))],
        out_specs=pl.BlockSpec((2, tm, tn), lambda i, j: (0, i, j)),
        out_shape=jax.ShapeDtypeStruct((2, T, D_FF), BF16), name=name,
        compiler_params=pltpu.CompilerParams(dimension_semantics=("parallel", "parallel")),
    )(df, wo, gu)


def _rope128(t, cos, sin_s):
    return t * cos + pltpu.roll(t, 64, 1) * sin_s


def _rope128_t(g, cos, sin_s):
    return g * cos - pltpu.roll(g, 64, 1) * sin_s


def _partner32(t):
    lane = lax.broadcasted_iota(jnp.int32, t.shape, 1)
    return jnp.where((lane & 32) == 0, pltpu.roll(t, 96, 1), pltpu.roll(t, 32, 1))


def _rope64(t, cos, sin_s):
    return t * cos + _partner32(t) * sin_s


def _rope64_t(g, cos, sin_s):
    return g * cos - _partner32(g) * sin_s


def _mixer_prep_fwd(proj, tabs, qg, kvg, T):
    cos128, sin128, cos64, sin64 = tabs

    def body(rq_ref, rk_ref, cq_ref, ckv_ref, kpe_ref, c1_ref, s1_ref, c2_ref, s2_ref, qg_ref, kvg_ref,
             oq_ref, ok_ref, oqn_ref, okvn_ref, okpe_ref):
        c1, s1 = c1_ref[...], s1_ref[...]
        for h in range(HEADS):
            sl = slice(h * RET_DK, (h + 1) * RET_DK)
            oq_ref[:, sl] = _rope128(rq_ref[:, sl].astype(F32), c1, s1).astype(BF16)
            ok_ref[:, sl] = (_rope128(rk_ref[:, sl].astype(F32), c1, s1) * RET_SCALE).astype(BF16)
        oqn_ref[...] = _rms(cq_ref[...].astype(F32), qg_ref[...]).astype(BF16)
        okvn_ref[...] = _rms(ckv_ref[...].astype(F32), kvg_ref[...]).astype(BF16)
        okpe_ref[...] = _rope64(kpe_ref[...].astype(F32), c2_ref[...], s2_ref[...]).astype(BF16)

    ins = [(proj, 1024, 0), (proj, 1024, 1), (proj, 256, P_CQ // 256), (proj, 256, P_CKV // 256),
           (proj, 128, P_KPE // 128), (cos128, 128, 0), (sin128, 128, 0), (cos64, 128, 0), (sin64, 128, 0),
           (qg, None, None), (kvg, None, None)]
    outs = [(1024, BF16), (1024, BF16), (LORA, BF16), (LORA, BF16), (128, BF16)]
    return _rows(body, T, min(ROW_TM, T), ins, outs, "mixer_prep_fwd")


def _rms_bwd(proj, dqn, dkvn, dkpe_all, tabs, qg, kvg, T):
    _, _, cos64, sin64 = tabs

    def body(cq_ref, ckv_ref, dq_ref, dkv_ref, dk_ref, c2_ref, s2_ref, qg_ref, kvg_ref, o_ref, dqg_ref, dkvg_ref):
        _, vjp = jax.vjp(_rms, cq_ref[...].astype(F32), qg_ref[...])
        dx, dg = vjp(dq_ref[...])
        o_ref[:, 0:LORA] = dx.astype(BF16)
        _accum(dqg_ref, dg)
        _, vjp = jax.vjp(_rms, ckv_ref[...].astype(F32), kvg_ref[...])
        dx, dg = vjp(dkv_ref[...])
        o_ref[:, P_CKV - P_CQ:P_CKV - P_CQ + LORA] = dx.astype(BF16)
        _accum(dkvg_ref, dg)
        g = dk_ref[:, 0:128]
        for h in range(1, HEADS):
            g = g + dk_ref[:, h * 128:(h + 1) * 128]
        lane = lax.broadcasted_iota(jnp.int32, g.shape, 1)
        g = jnp.where(lane < MLA_ROPE, g, 0.0)
        o_ref[:, P_KPE - P_CQ:P_KPE - P_CQ + 128] = _rope64_t(g, c2_ref[...], s2_ref[...]).astype(BF16)
        o_ref[:, P_KPE - P_CQ + 128:] = jnp.zeros((g.shape[0], P_W - P_KPE - 128), BF16)

    ins = [(proj, 256, P_CQ // 256), (proj, 256, P_CKV // 256), (dqn, LORA, 0), (dkvn, LORA, 0), (dkpe_all, 1024, 0),
           (cos64, 128, 0), (sin64, 128, 0), (qg, None, None), (kvg, None, None)]
    return _rows(body, T, min(ROW_TM, T), ins, [(P_W - P_CQ, BF16)], "rms_bwd", accs=[(1, LORA), (1, LORA)])


def _gn_gate_bwd(y, rg, g, d):
    xc = y - jnp.mean(y, axis=-1, keepdims=True)
    rstd = lax.rsqrt(jnp.mean(jnp.square(xc), axis=-1, keepdims=True) + EPS)
    xhat = xc * rstd
    sig = _sigmoid(rg)
    act = rg * sig
    dn = d * act
    drg = d * (xhat * g) * (sig + act * (1.0 - sig))
    dxh = dn * g
    dy = rstd * (dxh - jnp.mean(dxh, axis=-1, keepdims=True) - xhat * jnp.mean(dxh * xhat, axis=-1, keepdims=True))
    return dy, drg, jnp.sum(dn * xhat, axis=0, keepdims=True)


def _gn_gate(y, rg, g):
    mu = jnp.mean(y, axis=-1, keepdims=True)
    var = jnp.mean(jnp.square(y - mu), axis=-1, keepdims=True)
    return _silu(rg) * ((y - mu) * lax.rsqrt(var + EPS) * g)


def _q_assemble_epilogue(tabs):
    _, _, cos64, sin64 = tabs

    def fn(q, ins, outs, first):
        c_ref, s_ref = ins
        on_ref, op_ref = outs
        on_ref[...] = (q[:, :1024] * Q_PRESCALE).astype(BF16)
        c, s = c_ref[...], s_ref[...]
        lane = lax.broadcasted_iota(jnp.int32, c.shape, 1)
        for j in range(HEADS // 2):
            r = _rope64(q[:, 1024 + 128 * j:1024 + 128 * (j + 1)], c, s) * Q_PRESCALE
            op_ref[:, 256 * j:256 * j + 128] = jnp.where(lane < 64, r, 0.0).astype(BF16)
            op_ref[:, 256 * j + 128:256 * j + 256] = jnp.where(lane < 64, pltpu.roll(r, 64, 1), 0.0).astype(BF16)

    return (fn, [(cos64, 0, 128), (sin64, 0, 128)], [], ((BF16, 1024), (BF16, 1024)), ())


def _q_assemble_bwd(dqn, dqpe, tabs, T):
    _, _, cos64, sin64 = tabs

    def body(dn_ref, dp_ref, c_ref, s_ref, on_ref, op_ref):
        on_ref[...] = dn_ref[...].astype(BF16)
        c, s = c_ref[...], s_ref[...]
        lane = lax.broadcasted_iota(jnp.int32, c.shape, 1)
        for j in range(HEADS // 2):
            g = jnp.where(lane < 64, dp_ref[:, 256 * j:256 * j + 128], pltpu.roll(dp_ref[:, 256 * j + 128:256 * j + 256], 64, 1))
            op_ref[:, 128 * j:128 * (j + 1)] = _rope64_t(g, c, s).astype(BF16)

    return _rows(body, T, min(ROW_TM, T), [(dqn, 1024, 0), (dqpe, 1024, 0), (cos64, 128, 0), (sin64, 128, 0)],
                 [(1024, BF16), (512, BF16)], "q_assemble_bwd")


def _mix_fn(gr, gm, yr, ym):
    return _sigmoid(gr) * yr + _sigmoid(gm) * ym


def _mix_epilogue(proj, y_ret):
    def fn(r, ins, outs, first):
        gr_ref, gm_ref, yr_ref = ins
        ym_ref, mix_ref = outs
        ym_ref[...] = r.astype(BF16)
        mix_ref[...] = _mix_fn(gr_ref[...].astype(F32), gm_ref[...].astype(F32), yr_ref[...].astype(F32), r).astype(BF16)

    return (fn, [(proj, P_GR // D), (proj, P_GM // D), y_ret], [], (BF16, BF16), ())


def _mix_bwd_epilogue(proj, y_ret, y_mla):
    def fn(r, ins, outs, first):
        gr_ref, gm_ref, yr_ref, ym_ref = ins
        dgate_ref, dyr_ref, dym_ref = outs
        sr, sm = _sigmoid(gr_ref[...].astype(F32)), _sigmoid(gm_ref[...].astype(F32))
        dgate_ref[:, 0:D] = (r * yr_ref[...].astype(F32) * (sr * (1.0 - sr))).astype(BF16)
        dgate_ref[:, D:2 * D] = (r * ym_ref[...].astype(F32) * (sm * (1.0 - sm))).astype(BF16)
        dyr_ref[...] = (r * sr).astype(BF16)
        dym_ref[...] = (r * sm).astype(BF16)

    return (fn, [(proj, P_GR // D), (proj, P_GM // D), y_ret, y_mla], [], ((BF16, 2 * D), BF16, BF16), ())


def _head_epilogue(h3, pp, tgt, g, b):
    def fn(r, ins, outs, first):
        h_ref, pp_ref, t_ref, g_ref, b_ref = ins
        dh_ref, dgl_ref, dpp_ref, dg_ref, db_ref, loss_ref = outs

        sg, pp, gain = _sigmoid(r), pp_ref[...], g_ref[...]
        pre = ALPHA * h_ref[...] + sg * pp
        xc = pre - jnp.mean(pre, axis=-1, keepdims=True)
        rstd = lax.rsqrt(jnp.mean(jnp.square(xc), axis=-1, keepdims=True) + EPS)
        xhat = xc * rstd
        err = xhat * gain + b_ref[...] - t_ref[...]
        dy = err * (1.0 / D)
        dyg = dy * gain
        dpre = rstd * (dyg - jnp.mean(dyg, axis=-1, keepdims=True) - xhat * jnp.mean(dyg * xhat, axis=-1, keepdims=True))
        dh_ref[...] = ALPHA * dpre
        dgl_ref[...] = (dpre * pp * (sg * (1.0 - sg))).astype(BF16)
        dpp_ref[...] = (dpre * sg).astype(BF16)
        _accum(dg_ref, jnp.sum(dy * xhat, axis=0, keepdims=True), first)
        _accum(db_ref, jnp.sum(dy, axis=0, keepdims=True), first)
        part = 0.5 * jnp.sum(jnp.mean(jnp.square(err), axis=-1, keepdims=True), axis=0, keepdims=True)
        _accum(loss_ref, jnp.broadcast_to(part, loss_ref.shape), first)

    return (fn, [h3, pp, tgt], [g, b], (F32, BF16, BF16), (1, 1, 8))


def _delta_epilogue(o):
    def fn(r, ins, outs, first):
        (o_ref,) = ins
        db_ref, dl_ref = outs
        db_ref[...] = r.astype(BF16)
        for h in range(HEADS):
            sl = slice(h * MLA_DV, (h + 1) * MLA_DV)
            dl = jnp.sum(r[:, sl] * o_ref[:, sl].astype(F32), axis=-1, keepdims=True)
            dl_ref[:, sl] = jnp.broadcast_to(dl, (r.shape[0], MLA_DV))

    return (fn, [o], [], (BF16, F32), ())


def _ret_consts():
    L = RET_L
    lg = jnp.log(1.0 - 2.0 ** (-5.0 - jnp.arange(HEADS, dtype=F32)))[:, None, None]
    idx = jnp.arange(L, dtype=F32)
    ch = jnp.arange(L) // CHUNK
    dist = idx[:, None] - idx[None, :]
    same = (ch[:, None] == ch[None, :])[None]
    earlier = (ch[None, :] < ch[:, None])[None]
    dm = jnp.where(same, jnp.exp(lg * jnp.abs(dist)[None]), jnp.where(earlier, jnp.exp(lg * dist[None]), 0.0))
    xi = jnp.broadcast_to(jnp.exp(lg * (idx + 1.0)[None, :, None]), (HEADS, L, 128))
    zeta = jnp.broadcast_to(jnp.exp(lg * (L - 1.0 - idx)[None, :, None]), (HEADS, L, 128))
    gl = jnp.broadcast_to(jnp.exp(lg * float(L)), (HEADS, 8, 128))
    return dm.astype(F32), xi.astype(F32), zeta.astype(F32), gl.astype(F32)


def _whole(arr):
    return pl.BlockSpec(arr.shape, lambda n, _nd=arr.ndim: (0,) * _nd)


def _ret_fwd(q, k, v, proj, gn_g, consts, T):
    dm, xi, zeta, gl = consts
    L = RET_L
    n_sc = T // L

    def body(q_ref, k_ref, v_ref, rg_ref, g_ref, dm_ref, xi_ref, ze_ref, gl_ref, y_ref, yr_ref, s_ref, st_ref):
        @pl.when(pl.program_id(0) == 0)
        def _():
            st_ref[...] = jnp.zeros_like(st_ref)

        for h in range(HEADS):
            ks, vs = slice(h * RET_DK, (h + 1) * RET_DK), slice(h * RET_DV, (h + 1) * RET_DV)
            qq, kk, vv = q_ref[:, ks], k_ref[:, ks], v_ref[:, vs]
            st = st_ref[h]
            s_ref[h, 0] = st
            p = (_dot(qq, kk, 1, 1) * dm_ref[h]).astype(BF16)
            cross = _dot(qq, st.astype(BF16), 1, 0)
            xi_c = jnp.concatenate([xi_ref[h], xi_ref[h]], axis=1)
            y = _dot(p, vv, 1, 0) + cross * xi_c
            y_ref[:, vs] = y
            yr_ref[:, vs] = _gn_gate(y, rg_ref[:, vs].astype(F32), g_ref[:, vs]).astype(BF16)
            kz = (kk.astype(F32) * ze_ref[h]).astype(BF16)
            gl2 = jnp.concatenate([gl_ref[h, 0:1, :], gl_ref[h, 0:1, :]], axis=1)
            st_ref[h] = st * gl2 + _dot(kz, vv, 0, 0)

    return _pcall(
        body, grid=(n_sc,),
        in_specs=[pl.BlockSpec((L, 1024), lambda n: (n, 0)), pl.BlockSpec((L, 1024), lambda n: (n, 0)),
                  pl.BlockSpec((L, 2048), lambda n: (n, P_RV // 2048)), pl.BlockSpec((L, 2048), lambda n: (n, P_RG // 2048)),
                  _whole(gn_g), _whole(dm), _whole(xi), _whole(zeta), _whole(gl)],
        out_specs=[pl.BlockSpec((L, 2048), lambda n: (n, 0)), pl.BlockSpec((L, 2048), lambda n: (n, 0)),
                   pl.BlockSpec((HEADS, 1, 128, 256), lambda n: (0, n, 0, 0))],
        out_shape=[jax.ShapeDtypeStruct((T, HEADS * RET_DV), F32), jax.ShapeDtypeStruct((T, HEADS * RET_DV), BF16),
                   jax.ShapeDtypeStruct((HEADS, n_sc, 128, 256), F32)],
        scratch_shapes=[pltpu.VMEM((HEADS, 128, 256), F32)], name="ret_fwd",
        compiler_params=pltpu.CompilerParams(dimension_semantics=("arbitrary",)),
    )(q, k, v, proj, gn_g, dm, xi, zeta, gl)


def _ret_bwd(q, k, v, y, proj, gn_g, states, dyr, consts, tabs, T):
    dm, xi, zeta, gl = consts
    cos128, sin128, _, _ = tabs
    L = RET_L
    n_sc = T // L

    def body(q_ref, k_ref, v_ref, y_ref, rg_ref, g_ref, d_ref, s_ref, dm_ref, xi_ref, ze_ref, gl_ref, c_ref, sn_ref,
             dp_ref, dg_ref, gs_ref):
        @pl.when(pl.program_id(0) == 0)
        def _():
            gs_ref[...] = jnp.zeros_like(gs_ref)

        c, sn = c_ref[...], sn_ref[...]
        dgs = []
        for h in range(HEADS):
            ks, vs = slice(h * RET_DK, (h + 1) * RET_DK), slice(h * RET_DV, (h + 1) * RET_DV)
            dy, drg, dg = _gn_gate_bwd(y_ref[:, vs], rg_ref[:, vs].astype(F32), g_ref[:, vs], d_ref[:, vs].astype(F32))
            dp_ref[:, P_RG + h * RET_DV:P_RG + (h + 1) * RET_DV] = drg.astype(BF16)
            dgs.append(dg)
            qq, kk, vv, dyy = q_ref[:, ks], k_ref[:, ks], v_ref[:, vs], dy.astype(BF16)
            dmm = dm_ref[h]
            gb = gs_ref[h].astype(BF16)
            sb = s_ref[h, 0].astype(BF16)
            xi_c = jnp.concatenate([xi_ref[h], xi_ref[h]], axis=1)
            pb = (_dot(qq, kk, 1, 1) * dmm).astype(BF16)
            kz = (kk.astype(F32) * ze_ref[h]).astype(BF16)
            dp_ref[:, P_RV + h * RET_DV:P_RV + (h + 1) * RET_DV] = (_dot(pb, dyy, 0, 0) + _dot(kz, gb, 1, 0)).astype(BF16)
            da = (_dot(dyy, vv, 1, 1) * dmm).astype(BF16)
            dyx = (dyy.astype(F32) * xi_c).astype(BF16)
            dq = _dot(da, kk, 1, 0) + _dot(dyx, sb, 1, 1)
            dk = _dot(da, qq, 0, 0) + _dot(vv, gb, 1, 1) * ze_ref[h]
            dp_ref[:, P_RQ + h * RET_DK:P_RQ + (h + 1) * RET_DK] = _rope128_t(dq, c, sn).astype(BF16)
            dp_ref[:, P_RK + h * RET_DK:P_RK + (h + 1) * RET_DK] = (_rope128_t(dk, c, sn) * RET_SCALE).astype(BF16)
            gl2 = jnp.concatenate([gl_ref[h, 0:1, :], gl_ref[h, 0:1, :]], axis=1)
            gs_ref[h] = gs_ref[h] * gl2 + _dot(qq, dyx, 0, 0)
        _accum(dg_ref, jnp.concatenate(dgs, axis=1))

    rev = lambda n: n_sc - 1 - n
    return _pcall(
        body, grid=(n_sc,),
        in_specs=[pl.BlockSpec((L, 1024), lambda n: (rev(n), 0)), pl.BlockSpec((L, 1024), lambda n: (rev(n), 0)),
                  pl.BlockSpec((L, 2048), lambda n: (rev(n), P_RV // 2048)), pl.BlockSpec((L, 2048), lambda n: (rev(n), 0)),
                  pl.BlockSpec((L, 2048), lambda n: (rev(n), P_RG // 2048)), _whole(gn_g),
                  pl.BlockSpec((L, 2048), lambda n: (rev(n), 0)),
                  pl.BlockSpec((HEADS, 1, 128, 256), lambda n: (0, rev(n), 0, 0)),
                  _whole(dm), _whole(xi), _whole(zeta), _whole(gl),
                  pl.BlockSpec((L, 128), lambda n: (rev(n), 0)), pl.BlockSpec((L, 128), lambda n: (rev(n), 0))],
        out_specs=[pl.BlockSpec((L, P_GR), lambda n: (rev(n), 0)), pl.BlockSpec((1, 2048), lambda n: (0, 0))],
        out_shape=[jax.ShapeDtypeStruct((T, P_GR), BF16), jax.ShapeDtypeStruct((1, 2048), F32)],
        scratch_shapes=[pltpu.VMEM((HEADS, 128, 256), F32)], name="ret_bwd",
        compiler_params=pltpu.CompilerParams(dimension_semantics=("arbitrary",)),
    )(q, k, v, y, proj, gn_g, dyr, states, dm, xi, zeta, gl, cos128, sin128)


def _diag_mask(nrows, ncols, row0):
    row = row0 + lax.broadcasted_iota(jnp.int32, (nrows, ncols), 0)
    col = lax.broadcasted_iota(jnp.int32, (nrows, ncols), 1)
    return lax.shift_right_logical(col, 6) <= lax.shift_right_logical(row, 6)


def _diag_spans(t):
    h = t // 2
    return ((0, h, h), (h, h, t)) if h % 128 == 0 else ((0, t, t),)


def _tri_steps(nb, by_key):
    if by_key:
        pairs = [(i, j) for j in range(nb) for i in range(j, nb)]
    else:
        pairs = [(i, j) for i in range(nb) for j in range(i + 1)]
    return jnp.array([a for a, _ in pairs], jnp.int32), jnp.array([b for _, b in pairs], jnp.int32)


def _attn_fwd(qn, qpe, kv, kpe, T):
    t = min(ATT_TF, T)
    nb = T // t
    ii, jj = _tri_steps(nb, by_key=False)

    hp = ATT_HP
    w = 128 * hp

    def body(ii_ref, jj_ref, qn_ref, qp_ref, kn_ref, kp_ref, v_ref, o_ref, lse_ref, m_sc, l_sc, acc_sc):
        st = pl.program_id(1)
        i, j = ii_ref[st], jj_ref[st]

        @pl.when(j == 0)
        def _():
            m_sc[...] = jnp.full_like(m_sc, NEG)
            l_sc[...] = jnp.zeros_like(l_sc)
            acc_sc[...] = jnp.zeros_like(acc_sc)

        def update(diag):
            kp = kp_ref[...]
            spans = _diag_spans(t) if diag else ((0, t, t),)
            for hh in range(hp):
                sl = slice(128 * hh, 128 * (hh + 1))
                for row0, nr, nkeys in spans:
                    rows = slice(row0, row0 + nr)
                    q = jnp.concatenate([qn_ref[rows, sl], qp_ref[rows, sl]], axis=1)
                    k = jnp.concatenate([kn_ref[:nkeys, sl], kp[:nkeys]], axis=1)
                    s = _dot(q, k, 1, 1)
                    if diag:
                        s = jnp.where(_diag_mask(nr, nkeys, row0), s, NEG)
                    m_prev = m_sc[rows, sl]
                    m_new = jnp.maximum(m_prev, jnp.max(s, axis=1, keepdims=True))
                    a = jnp.exp2(m_prev - m_new)
                    p = jnp.exp2(s - m_new[:, 0:1])
                    l_sc[rows, sl] = a * l_sc[rows, sl] + jnp.sum(p, axis=1, keepdims=True)
                    acc_sc[rows, sl] = a * acc_sc[rows, sl] + _dot(p.astype(BF16), v_ref[:nkeys, sl], 1, 0)
                    m_sc[rows, sl] = m_new

        @pl.when(j < i)
        def _():
            update(False)

        @pl.when(j == i)
        def _():
            update(True)
            o_ref[...] = (acc_sc[...] / l_sc[...]).astype(BF16)
            lse_ref[...] = m_sc[...] + jnp.log2(l_sc[...])

    qs = pl.BlockSpec((t, w), lambda h, s, ii, jj: (ii[s], h))
    grid_spec = pltpu.PrefetchScalarGridSpec(
        num_scalar_prefetch=2, grid=(HEADS // hp, int(ii.shape[0])),
        in_specs=[qs, qs, pl.BlockSpec((t, w), lambda h, s, ii, jj: (jj[s], h)), pl.BlockSpec((t, 128), lambda h, s, ii, jj: (jj[s], 0)),
                  pl.BlockSpec((t, w), lambda h, s, ii, jj: (jj[s], HEADS // hp + h))],
        out_specs=[qs, qs],
        scratch_shapes=[pltpu.VMEM((t, w), F32), pltpu.VMEM((t, w), F32), pltpu.VMEM((t, w), F32)])
    return _pcall(
        body, grid_spec=grid_spec, out_shape=[jax.ShapeDtypeStruct((T, D), BF16), jax.ShapeDtypeStruct((T, D), F32)], name="attn_fwd",
        compiler_params=pltpu.CompilerParams(dimension_semantics=("arbitrary", "arbitrary")),
    )(ii, jj, qn, qpe, kv, kpe, kv)


def _attn_bwd(qn, qpe, kv, kpe, do, lse, delta, T):
    t = min(ATT_TB, T)
    nb = T // t
    ii, jj = _tri_steps(nb, by_key=True)

    def body(ii_ref, jj_ref, qn_ref, qp_ref, kn_ref, kp_ref, v_ref, do_ref, lse_ref, dl_ref,
             dqn_ref, dqp_ref, dkn_ref, dkp_ref, dv_ref, dk_sc, dv_sc):
        st = pl.program_id(1)
        i, j = ii_ref[st], jj_ref[st]

        @pl.when(st == 0)
        def _():
            dqn_ref[...] = jnp.zeros_like(dqn_ref)
            dqp_ref[...] = jnp.zeros_like(dqp_ref)

        @pl.when(i == j)
        def _():
            dk_sc[...] = jnp.zeros_like(dk_sc)
            dv_sc[...] = jnp.zeros_like(dv_sc)

        def update(diag):
            for row0, nr, nkeys in (_diag_spans(t) if diag else ((0, t, t),)):
                rows = slice(row0, row0 + nr)
                q = jnp.concatenate([qn_ref[rows, :], qp_ref[rows, :]], axis=1)
                k = jnp.concatenate([kn_ref[:nkeys, :], kp_ref[:nkeys, :]], axis=1)
                dob = do_ref[rows, :]
                s = _dot(q, k, 1, 1)
                if diag:
                    s = jnp.where(_diag_mask(nr, nkeys, row0), s, NEG)
                p = jnp.exp2(s - lse_ref[rows, 0:1])
                dv_sc[:nkeys, :] += _dot(p.astype(BF16), dob, 0, 0)
                dp = _dot(dob, v_ref[:nkeys, :], 1, 1)
                ds = (p * (dp - dl_ref[rows, 0:1])).astype(BF16)
                dk_sc[:nkeys, :] += _dot(ds, q, 0, 0)
                dq = _dot(ds, k, 1, 0) * MLA_SCALE
                grows = pl.ds(pl.multiple_of(i * t + row0, nr), nr)
                dqn_ref[grows, :] += dq[:, :128]
                dqp_ref[grows, :] += dq[:, 128:]

        @pl.when(i > j)
        def _():
            update(False)

        @pl.when(i == j)
        def _():
            update(True)

        @pl.when(i == nb - 1)
        def _():
            dkn_ref[...] = (dk_sc[:, :128] * (1.0 / LOG2E)).astype(BF16)
            dkp_ref[...] = dk_sc[:, 128:] * (1.0 / LOG2E)
            dv_ref[...] = dv_sc[...].astype(BF16)

    qs = pl.BlockSpec((t, 128), lambda h, s, ii, jj: (ii[s], h))
    ks = pl.BlockSpec((t, 128), lambda h, s, ii, jj: (jj[s], h))
    hs = pl.BlockSpec((T, 128), lambda h, s, ii, jj: (0, h))
    grid_spec = pltpu.PrefetchScalarGridSpec(
        num_scalar_prefetch=2, grid=(HEADS, int(ii.shape[0])),
        in_specs=[qs, qs, ks, pl.BlockSpec((t, 128), lambda h, s, ii, jj: (jj[s], 0)),
                  pl.BlockSpec((t, 128), lambda h, s, ii, jj: (jj[s], HEADS + h)), qs, qs, qs],
        out_specs=[hs, hs, ks, ks, ks],
        scratch_shapes=[pltpu.VMEM((t, 256), F32), pltpu.VMEM((t, 128), F32)])
    return _pcall(
        body, grid_spec=grid_spec,
        out_shape=[jax.ShapeDtypeStruct((T, D), F32), jax.ShapeDtypeStruct((T, D), F32), jax.ShapeDtypeStruct((T, D), BF16),
                   jax.ShapeDtypeStruct((T, D), F32), jax.ShapeDtypeStruct((T, D), BF16)], name="attn_bwd",
        compiler_params=pltpu.CompilerParams(dimension_semantics=("arbitrary", "arbitrary")),
    )(ii, jj, qn, qpe, kv, kpe, kv, do, lse, delta)


def _mesh_pos():
    x, y, c = lax.axis_index("x"), lax.axis_index("y"), lax.axis_index("c")
    return x, y, c, 4 * x + 2 * y + c


def _peer(x, y, c, k):
    px, py, pc = (x + ((k >> 2) & 1)) % 2, (y + ((k >> 1) & 1)) % 2, (c + (k & 1)) % 2
    return (px, py, pc), 4 * px + 2 * py + pc


_ANY = pl.BlockSpec(memory_space=pl.ANY)


def _rcopy(src, dst, send_sems, recv_sems, k, to):
    return pltpu.make_async_remote_copy(src_ref=src, dst_ref=dst, send_sem=send_sems.at[k], recv_sem=recv_sems.at[k],
                                        device_id=to, device_id_type=pl.DeviceIdType.MESH)


def _all_gather(wsh, ssh):
    def body(w_ref, s_ref, wall_ref, sall_ref, send_sems, recv_sems, loc_sems):
        x, y, c, me = _mesh_pos()
        sib = (x, y, 1 - c)
        chips = [(1 - x, y), (x, 1 - y), (1 - x, 1 - y)]
        slot = lambda px, py, pc: 4 * px + 2 * py + pc
        loc = [pltpu.make_async_copy(s_ref, sall_ref.at[me], loc_sems.at[0])]
        for cp in loc:
            cp.start()
        sends, fwd_waits = [], []
        for n, (src, dst) in enumerate(((w_ref, wall_ref), (s_ref, sall_ref))):
            o = 7 * n
            sends.append(_rcopy(src, dst.at[me], send_sems, recv_sems, o, sib))
            for j, chip in enumerate(chips):
                sends.append(_rcopy(src, dst.at[me], send_sems, recv_sems, o + 1 + j, (*chip, c)))
        for cp in sends:
            cp.start()
        for n, (src, dst) in enumerate(((w_ref, wall_ref), (s_ref, sall_ref))):
            o = 7 * n
            for j, chip in enumerate(chips):
                got = dst.at[slot(*chip, c)]
                _rcopy(src, got, send_sems, recv_sems, o + 1 + j, sib).wait_recv()
                fw = _rcopy(got, got, send_sems, recv_sems, o + 4 + j, sib)
                fw.start()
                sends.append(fw)
            fwd_waits.append(_rcopy(src, dst.at[slot(x, y, 1 - c)], send_sems, recv_sems, o, sib))
            for j, chip in enumerate(chips):
                fwd_waits.append(_rcopy(src, dst.at[slot(*chip, 1 - c)], send_sems, recv_sems, o + 4 + j, sib))
        for cp in fwd_waits:
            cp.wait_recv()
        for cp in sends:
            cp.wait_send()
        for cp in loc:
            cp.wait()

    wall, sall = _pcall(
        body, in_specs=[_ANY, _ANY], out_specs=[_ANY, _ANY],
        out_shape=[jax.ShapeDtypeStruct((N_DEV,) + wsh.shape, wsh.dtype), jax.ShapeDtypeStruct((N_DEV,) + ssh.shape, ssh.dtype)],
        scratch_shapes=[pltpu.SemaphoreType.DMA((14,)), pltpu.SemaphoreType.DMA((14,)), pltpu.SemaphoreType.DMA((1,))],
        name="all_gather_weights",
    )(wsh, ssh)
    me = 4 * lax.axis_index("x") + 2 * lax.axis_index("y") + lax.axis_index("c")
    return lax.dynamic_update_index_in_dim(wall, wsh, me, axis=0), sall


_HBM = pl.BlockSpec(memory_space=pltpu.HBM)
_SEM = pl.BlockSpec(memory_space=pltpu.SEMAPHORE)
_EFFECT = pltpu.SideEffectType.DATAFLOW_SIDE_EFFECTING


def _other_chips(x, y):
    return [(1 - x, y), (x, 1 - y), (1 - x, 1 - y)]


def _gather_start(wsh, order_after):
    def body(w_ref, land_ref, dep_ref, send_sems, recv_sems, w_thru, land_thru, token):
        x, y, c, me = _mesh_pos()
        _rcopy(w_ref, land_ref.at[me], send_sems, recv_sems, 0, (x, y, 1 - c)).start()
        for j, chip in enumerate(_other_chips(x, y)):
            _rcopy(w_ref, land_ref.at[me], send_sems, recv_sems, 1 + j, (*chip, c)).start()
        token[...] = jnp.zeros_like(token)

    shape = (N_DEV,) + wsh.shape
    land = pltpu.with_memory_space_constraint(lax.empty(shape, wsh.dtype), pltpu.HBM)
    return _pcall(
        body, name="gather_start",
        out_shape=(pltpu.SemaphoreType.DMA((4,)), pltpu.SemaphoreType.DMA((4,)), pltpu.HBM(wsh.shape, wsh.dtype),
                   pltpu.HBM(shape, wsh.dtype), jax.ShapeDtypeStruct((8, 128), F32)),
        in_specs=(_HBM, _HBM, _ANY), out_specs=(_SEM, _SEM, _HBM, _HBM, pl.BlockSpec(memory_space=pltpu.VMEM)),
        input_output_aliases={0: 2, 1: 3}, compiler_params=pltpu.CompilerParams(has_side_effects=_EFFECT),
    )(pltpu.with_memory_space_constraint(wsh, pltpu.HBM), land, order_after)


def _gather_wait(send_sems, recv_sems, w_thru, land_thru, after):
    def body(w_ref, land_ref, send_sems, recv_sems, after_ref, w_out, land_out):
        x, y, c, _ = _mesh_pos()
        senders = [(x, y, 1 - c)] + [(*chip, c) for chip in _other_chips(x, y)]
        for k, (px, py, pc) in enumerate(senders):
            cp = _rcopy(w_ref, land_ref.at[4 * px + 2 * py + pc], send_sems, recv_sems, k, (px, py, pc))
            cp.wait_send()
            cp.wait_recv()

    return _pcall(
        body, name="gather_wait", out_shape=(pltpu.HBM(w_thru.shape, w_thru.dtype), pltpu.HBM(land_thru.shape, land_thru.dtype)),
        in_specs=(_HBM, _HBM, _SEM, _SEM, _ANY), out_specs=(_HBM, _HBM), input_output_aliases={0: 0, 1: 1},
        compiler_params=pltpu.CompilerParams(has_side_effects=_EFFECT),
    )(w_thru, land_thru, send_sems, recv_sems, after)


def _gather_finish(wsh, land):
    def body(land_ref, out_ref, send_sems, recv_sems):
        x, y, c, _ = _mesh_pos()
        sib = (x, y, 1 - c)
        sends, recvs = [], []
        for j, (px, py) in enumerate(_other_chips(x, y)):
            mine, theirs = 4 * px + 2 * py + c, 4 * px + 2 * py + (1 - c)
            sends.append(_rcopy(land_ref.at[mine], out_ref.at[mine], send_sems, recv_sems, j, sib))
            recvs.append(_rcopy(land_ref.at[theirs], out_ref.at[theirs], send_sems, recv_sems, j, sib))
        for cp in sends:
            cp.start()
        for cp in recvs:
            cp.wait_recv()
        for cp in sends:
            cp.wait_send()

    done = _pcall(
        body, in_specs=[_ANY], out_specs=_ANY, out_shape=jax.ShapeDtypeStruct(land.shape, land.dtype),
        input_output_aliases={0: 0},
        scratch_shapes=[pltpu.SemaphoreType.DMA((3,)), pltpu.SemaphoreType.DMA((3,))], name="gather_finish",
    )(land)
    me = 4 * lax.axis_index("x") + 2 * lax.axis_index("y") + lax.axis_index("c")
    return lax.dynamic_update_index_in_dim(done, wsh, me, axis=0)


_ROW_TILES = (400, 368, 352, 256, 128, 16, 8)


def _exchange_sibling(g4, name):
    def body(g_ref, r_ref, send_sems, recv_sems):
        x, y, c, _ = _mesh_pos()
        sib = (x, y, 1 - c)
        cps = [_rcopy(g_ref.at[q, 1 - c], r_ref.at[q], send_sems, recv_sems, q, sib) for q in range(4)]
        for cp in cps:
            cp.start()
        for cp in cps:
            cp.wait()

    n, _, R, C = g4.shape
    return _pcall(
        body, in_specs=[_ANY], out_specs=_ANY, out_shape=jax.ShapeDtypeStruct((n, R, C), g4.dtype),
        scratch_shapes=[pltpu.SemaphoreType.DMA((4,)), pltpu.SemaphoreType.DMA((4,))], name=name,
    )(g4)


def _sum_sibling(g4, rsib, cvec, name):
    n, _, R, C = g4.shape
    tr = _pick(R, _ROW_TILES)

    def body(c_ref, g_ref, r_ref, o_ref):
        o_ref[...] = (g_ref[...].astype(F32) + r_ref[...].astype(F32)).astype(o_ref.dtype)

    grid_spec = pltpu.PrefetchScalarGridSpec(
        num_scalar_prefetch=1, grid=(n, R // tr),
        in_specs=[pl.BlockSpec((None, None, tr, C), lambda q, i, cr: (q, cr[0], i, 0)), pl.BlockSpec((None, tr, C), lambda q, i, cr: (q, i, 0))],
        out_specs=pl.BlockSpec((None, tr, C), lambda q, i, cr: (q, i, 0)))
    return _pcall(body, grid_spec=grid_spec, out_shape=jax.ShapeDtypeStruct((n, R, C), g4.dtype), name=name)(cvec, g4, rsib)


def _chips_start(part, name, order_after=None):
    n_in = 2 + int(order_after is not None)

    def body(*refs):
        p_ref, land_ref = refs[:2]
        send_sems, recv_sems, token = refs[n_in], refs[n_in + 1], refs[n_in + 4]
        x, y, c, _ = _mesh_pos()
        myq = 2 * x + y
        for j, (px, py) in enumerate(_other_chips(x, y)):
            _rcopy(p_ref.at[2 * px + py], land_ref.at[myq], send_sems, recv_sems, j, (px, py, c)).start()
        token[...] = jnp.zeros_like(token)

    land = pltpu.with_memory_space_constraint(lax.empty(part.shape, part.dtype), pltpu.HBM)
    extra = () if order_after is None else (order_after,)
    return _pcall(
        body, name=name,
        out_shape=(pltpu.SemaphoreType.DMA((3,)), pltpu.SemaphoreType.DMA((3,)), pltpu.HBM(part.shape, part.dtype),
                   pltpu.HBM(part.shape, part.dtype), jax.ShapeDtypeStruct((8, 128), F32)),
        in_specs=(_HBM, _HBM) + (_ANY,) * len(extra),
        out_specs=(_SEM, _SEM, _HBM, _HBM, pl.BlockSpec(memory_space=pltpu.VMEM)),
        input_output_aliases={0: 2, 1: 3}, compiler_params=pltpu.CompilerParams(has_side_effects=_EFFECT),
    )(pltpu.with_memory_space_constraint(part, pltpu.HBM), land, *extra)


def _chips_wait(send_sems, recv_sems, p_thru, land_thru, after, name):
    def body(p_ref, land_ref, send_sems, recv_sems, after_ref, p_out, land_out):
        x, y, c, _ = _mesh_pos()
        for j, (px, py) in enumerate(_other_chips(x, y)):
            q = 2 * px + py
            cp = _rcopy(p_ref.at[q], land_ref.at[q], send_sems, recv_sems, j, (px, py, c))
            cp.wait_send()
            cp.wait_recv()

    return _pcall(
        body, name=name, out_shape=(pltpu.HBM(p_thru.shape, p_thru.dtype), pltpu.HBM(p_thru.shape, p_thru.dtype)),
        in_specs=(_HBM, _HBM, _SEM, _SEM, _ANY), out_specs=(_HBM, _HBM), input_output_aliases={0: 0, 1: 1},
        compiler_params=pltpu.CompilerParams(has_side_effects=_EFFECT),
    )(p_thru, land_thru, send_sems, recv_sems, after)


def _sum_chips(part, land, qvec, name):
    n, R, C = part.shape
    tr = _pick(R, _ROW_TILES)

    def body(q_ref, p_ref, *refs):
        o_ref = refs[n]
        acc = None
        for q in range(n):
            term = jnp.where(q_ref[0] == q, p_ref[...], refs[q][...]).astype(F32)
            acc = term if acc is None else acc + term
        o_ref[...] = acc

    def land_spec(q):
        return pl.BlockSpec((None, tr, C), lambda i, m: (jnp.where(m[0] == q, (q + 1) % n, q), i, 0))

    grid_spec = pltpu.PrefetchScalarGridSpec(
        num_scalar_prefetch=1, grid=(R // tr,),
        in_specs=[pl.BlockSpec((None, tr, C), lambda i, m: (m[0], i, 0))] + [land_spec(q) for q in range(n)],
        out_specs=pl.BlockSpec((tr, C), lambda i, m: (i, 0)))
    return _pcall(body, grid_spec=grid_spec, out_shape=jax.ShapeDtypeStruct((R, C), F32), name=name)(qvec, part, *([land] * n))


def _exchange_small(gsmall):
    def body(s_ref, srecv_ref, send_sems, recv_sems, loc_sem):
        x, y, c, me = _mesh_pos()
        loc = pltpu.make_async_copy(s_ref, srecv_ref.at[me], loc_sem.at[0])
        loc.start()
        sends, recvs = [], []
        for k in range(1, N_DEV):
            to, pidx = _peer(x, y, c, k)
            sends.append(_rcopy(s_ref, srecv_ref.at[me], send_sems, recv_sems, k - 1, to))
            recvs.append(_rcopy(s_ref, srecv_ref.at[pidx], send_sems, recv_sems, k - 1, to))
        for cp in sends:
            cp.start()
        for cp in recvs:
            cp.wait_recv()
        for cp in sends:
            cp.wait_send()
        loc.wait()

    return _pcall(
        body, in_specs=[_ANY], out_specs=_ANY, out_shape=jax.ShapeDtypeStruct((N_DEV,) + gsmall.shape, gsmall.dtype),
        scratch_shapes=[pltpu.SemaphoreType.DMA((7,)), pltpu.SemaphoreType.DMA((7,)), pltpu.SemaphoreType.DMA((1,))],
        name="exchange_small",
    )(gsmall)


def _sum_slots(recv, name):
    n, R, C = recv.shape
    tr = _pick(R, _ROW_TILES)

    def body(r_ref, o_ref):
        acc = r_ref[0].astype(F32)
        for s in range(1, n):
            acc = acc + r_ref[s].astype(F32)
        o_ref[...] = acc

    return _pcall(body, grid=(R // tr,), in_specs=[pl.BlockSpec((n, tr, C), lambda i: (0, i, 0))],
                  out_specs=pl.BlockSpec((tr, C), lambda i: (i, 0)), out_shape=jax.ShapeDtypeStruct((R, C), F32), name=name)(recv)


def _adamw(w, g, m, v, name, order_after=None):
    shape = w.shape
    w2, g2, m2, v2 = (a.reshape(-1, shape[-1]) for a in (w, g, m, v))
    R, C = w2.shape
    tr = _pick(R, (256, 128, 64, 32, 16, 8)) if R > 256 else R
    n_in = 4 + int(order_after is not None)

    def body(*refs):
        w_ref, g_ref, m_ref, v_ref = refs[:4]
        d_ref, nm_ref, nv_ref = refs[n_in:]
        gg = g_ref[...]
        nm = ADAM_B1 * m_ref[...] + (1.0 - ADAM_B1) * gg
        nv = ADAM_B2 * v_ref[...] + (1.0 - ADAM_B2) * jnp.square(gg)
        m_hat = nm / (1.0 - ADAM_B1 ** ADAM_STEP)
        v_hat = nv / (1.0 - ADAM_B2 ** ADAM_STEP)
        d_ref[...] = -ADAM_LR * (m_hat / (jnp.sqrt(v_hat) + ADAM_EPS) + ADAM_WD * w_ref[...])
        nm_ref[...] = nm
        nv_ref[...] = nv

    spec = pl.BlockSpec((tr, C), lambda i: (i, 0))
    in_specs, args = [spec] * 4, [w2, g2, m2, v2]
    if order_after is not None:
        in_specs.append(pl.BlockSpec(memory_space=pl.ANY))
        args.append(order_after)
    d, nm, nv = _pcall(body, grid=(R // tr,), in_specs=in_specs, out_specs=[spec] * 3,
                       out_shape=[jax.ShapeDtypeStruct((R, C), F32)] * 3, name=name)(*args)
    return d.reshape(shape), nm.reshape(shape), nv.reshape(shape)


def _to_rows(name, w):
    w = w[0]
    if name in ("ffn1_w_in", "ffn2_w_in", "w_in"):
        return w.T
    if name in ("w_uq", "w_ukv", "ple_w_proj"):
        return w.T.reshape(-1, 1024)
    return w


def _from_rows(name, g, shape):
    if name in ("ffn1_w_in", "ffn2_w_in", "w_in"):
        return g.T.reshape(shape)
    if name in ("w_uq", "w_ukv", "ple_w_proj"):
        return g.reshape(-1, shape[1]).T.reshape(shape)
    return g.reshape(shape)


def _unpack(wall, group):
    out, off = {}, 0
    for name, r in group:
        out[name] = wall[:, off:off + r, :].reshape(N_DEV * r, 1024)
        off += _pad16(r)
    return out


def _w_in_internal(wt):
    return jnp.concatenate([wt[0:6144], wt[6720:8768], wt[6144:6720], jnp.zeros((P_W - W_IN_COLS, 1024), wt.dtype)], axis=0)


def _rope_tables(positions):
    pos = positions[0].astype(F32)

    def cs(half):
        inv = jnp.tile(ROPE_BASE ** (-jnp.arange(half, dtype=F32) / half), 128 // half)
        sign = jnp.tile(jnp.concatenate([-jnp.ones((half,), F32), jnp.ones((half,), F32)]), 64 // half)
        ang = pos[:, None] * inv
        return jnp.cos(ang), jnp.sin(ang) * sign

    return cs(64) + cs(32)


def _local_step(x, p, positions, target, W, ln_g, ln_b, gn_g, qg, kvg, early=None, rest_weights=None, start_token=None):
    T = x.shape[0]
    tabs = _rope_tables(positions)
    rc = _ret_consts()
    lg = [ln_g[i:i + 1] for i in range(4)]
    lb = [ln_b[i:i + 1] for i in range(4)]
    pb = p.astype(BF16)

    hs1, gu1, *xb = _ffn_in(x, W["ffn1_w_in"], "ffn1_in", order_after=start_token)
    xb = xb[0] if xb else x
    f1, h1, h1b = _mm(hs1, W["ffn1_w_out"], name="ffn1_out", tm=LN_TM, epilogue=_ln_epilogue(x, 0.5, lg[0], lb[0]))
    if rest_weights is not None:
        W = {**W, **rest_weights(h1b)}
    w_in_t = _w_in_internal(W["w_in"])
    wuq = W["w_uq"].reshape(1536, LORA).reshape(HEADS, 192, LORA)
    wuq = jnp.concatenate([wuq[:, :128].reshape(1024, LORA), wuq[:, 128:].reshape(512, LORA)], axis=0)
    wukv = W["w_ukv"].reshape(2048, LORA).reshape(HEADS, 2, 128, LORA).transpose(1, 0, 2, 3).reshape(2048, LORA)
    wp_t = W["ple_w_proj"].reshape(1024, D_PLE)
    proj = _mm(h1b, w_in_t, tb=True, out_dtype=BF16, name="mixer_in")
    rq, rk, qn, kvn, kpe = _mixer_prep_fwd(proj, tabs, qg, kvg, T)
    y, yr, states = _ret_fwd(rq, rk, proj, proj, gn_g, rc, T)
    y_ret = _mm(yr, W["w_ret_o"], out_dtype=BF16, name="ret_o")
    qnope, qpe = _mm(qn, wuq, tb=True, name="mla_uq", tm=LN_TM, tn=1536, epilogue=_q_assemble_epilogue(tabs))
    kv = _mm(kvn, wukv, tb=True, out_dtype=BF16, name="mla_ukv")
    o, lse = _attn_fwd(qnope, qpe, kv, kpe, T)
    y_mla, mix = _mm(o, W["w_mla_o"], name="mla_o", tm=LN_TM, epilogue=_mix_epilogue(proj, y_ret))
    mixed, h2, h2b = _mm(mix, W["w_out"], name="mixer_out", tm=LN_TM, epilogue=_ln_epilogue(h1, 1.0, lg[1], lb[1]))
    hs2, gu2 = _ffn_in(h2b, W["ffn2_w_in"], "ffn2_in")
    f2, h3, h3b = _mm(hs2, W["ffn2_w_out"], name="ffn2_out", tm=LN_TM, epilogue=_ln_epilogue(h2, 0.5, lg[2], lb[2]))
    pp = _mm(pb, wp_t, tb=True, name="ple_proj")

    G = {}
    dh3_a, dgl, dpp, dg3, db3, loss = _mm(h3b, W["ple_w_gate"], name="ple_gate", tm=LN_TM,
                                          epilogue=_head_epilogue(h3, pp, target, lg[3], lb[3]))
    G["ple_w_gate"] = _mm(h3b, dgl, ta=True, name="d_ple_gate")
    G["ple_w_proj"] = _mm(dpp, pb, ta=True, name="d_ple_proj")
    dh2_a, df2, dg2, db2 = _mm(dgl, W["ple_w_gate"], tb=True, add=dh3_a, name="dh3", tm=LN_TM,
                               epilogue=_ln_bwd_epilogue(h2, f2, 0.5, lg[2], lb[2]))
    G["ffn2_w_out"] = _mm(hs2, df2, ta=True, name="d_ffn2_out")
    da2 = _ffn_act_bwd(df2, W["ffn2_w_out"], gu2, "ffn2_act_bwd")
    G["ffn2_w_in"] = _mm(da2, h2b, ta=True, name="d_ffn2_in")
    dh1_a, dmixed, dg1, db1 = _mm(da2, W["ffn2_w_in"], add=dh2_a, name="dh2", tm=LN_TM,
                                  epilogue=_ln_bwd_epilogue(h1, mixed, 1.0, lg[1], lb[1]))
    G["w_out"] = _mm(mix, dmixed, ta=True, name="d_mixer_out")
    dp_gate, dy_ret, dy_mla = _mm(dmixed, W["w_out"], tb=True, name="dmix", tm=LN_TM,
                                  epilogue=_mix_bwd_epilogue(proj, y_ret, y_mla))
    G["w_mla_o"] = _mm(o, dy_mla, ta=True, name="d_mla_o")
    dob, delta = _mm(dy_mla, W["w_mla_o"], tb=True, name="do", tm=LN_TM, epilogue=_delta_epilogue(o))
    dqn_f, dqpe_f, dkn, dkpe_all, dv = _attn_bwd(qnope, qpe, kv, kpe, dob, lse, delta, T)
    dq_n, dq_r = _q_assemble_bwd(dqn_f, dqpe_f, tabs, T)
    g_uq = jnp.concatenate([_mm(dq_n, qn, ta=True, name="d_uq_nope"), _mm(dq_r, qn, ta=True, name="d_uq_rope")], axis=0)
    g_uq = jnp.concatenate([g_uq[:1024].reshape(HEADS, 128, LORA), g_uq[1024:].reshape(HEADS, 64, LORA)], axis=1)
    G["w_uq"] = g_uq.reshape(1536 * LORA // 1024, 1024)
    dqn = _mm(dq_r, wuq[1024:], add=_mm(dq_n, wuq[:1024], name="dqn_a"), name="dqn_b")
    g_ukv = jnp.stack([_mm(dkn, kvn, ta=True, name="d_ukv_k"), _mm(dv, kvn, ta=True, name="d_ukv_v")], axis=0)
    G["w_ukv"] = g_ukv.reshape(2, HEADS, 128, LORA).transpose(1, 0, 2, 3).reshape(2048 * LORA // 1024, 1024)
    dkvn = _mm(dv, wukv[1024:], add=_mm(dkn, wukv[:1024], name="dkvn_a"), name="dkvn_b")
    dp_mla, dqg, dkvg = _rms_bwd(proj, dqn, dkvn, dkpe_all, tabs, qg, kvg, T)
    G["w_ret_o"] = _mm(yr, dy_ret, ta=True, name="d_ret_o")
    dyr = _mm(dy_ret, W["w_ret_o"], tb=True, out_dtype=BF16, name="dyr")
    dp_ret, dgn = _ret_bwd(rq, rk, proj, y, proj, gn_g, states, dyr, rc, tabs, T)
    g_ret, g_gate, g_mla = (_mm(dp, h1b, ta=True, name="d_mixer_in_" + n)
                            for dp, n in ((dp_ret, "ret"), (dp_gate, "gate"), (dp_mla, "mla")))
    G["w_in"] = jnp.concatenate([g_ret, g_mla[:W_IN_COLS - P_CQ], g_gate], axis=0)
    lg0 = lg[0] if early is None else lg[0] + early(G)[0:1, 0:1]
    dh1 = _mm_pieces([(dp_ret, w_in_t[:P_GR]), (dp_gate, w_in_t[P_GR:P_CQ]), (dp_mla, w_in_t[P_CQ:])], dh1_a, "dh1")
    dx_a, df1, dg0, db0 = _ln_bwd_rows(x, f1, 0.5, lg0, lb[0], dh1, T, "ln0_bwd")
    G["ffn1_w_out"] = _mm(hs1, df1, ta=True, name="d_ffn1_out")
    da1 = _ffn_act_bwd(df1, W["ffn1_w_out"], gu1, "ffn1_act_bwd")
    G["ffn1_w_in"] = _mm(da1, xb, ta=True, name="d_ffn1_in")
    grad_x = _mm(da1, W["ffn1_w_in"], add=dx_a, name="grad_x")

    small = dict(ln_g=jnp.concatenate([dg0, dg1, dg2, dg3], axis=0), ln_b=jnp.concatenate([db0, db1, db2, db3], axis=0),
                 ret_gn_g=dgn, q_norm_g=dqg, kv_norm_g=dkvg)
    return loss, grad_x, G, small


def kernel(x, p, positions, ln_g, ln_b, ffn1_w_in, ffn1_w_out, w_in, ret_gn_g, w_ret_o, q_norm_g, kv_norm_g, w_uq, w_ukv, w_mla_o, w_out, ffn2_w_in, ffn2_w_out, ple_w_gate, ple_w_proj, loss_target, m_ln_g, m_ln_b, m_ffn1_w_in, m_ffn1_w_out, m_w_in, m_ret_gn_g, m_w_ret_o, m_q_norm_g, m_kv_norm_g, m_w_uq, m_w_ukv, m_w_mla_o, m_w_out, m_ffn2_w_in, m_ffn2_w_out, m_ple_w_gate, m_ple_w_proj, v_ln_g, v_ln_b, v_ffn1_w_in, v_ffn1_w_out, v_w_in, v_ret_gn_g, v_w_ret_o, v_q_norm_g, v_kv_norm_g, v_w_uq, v_w_ukv, v_w_mla_o, v_w_out, v_ffn2_w_in, v_ffn2_w_out, v_ple_w_gate, v_ple_w_proj):
    names = ("ln_g", "ln_b", "ffn1_w_in", "ffn1_w_out", "w_in", "ret_gn_g", "w_ret_o", "q_norm_g", "kv_norm_g", "w_uq", "w_ukv",
             "w_mla_o", "w_out", "ffn2_w_in", "ffn2_w_out", "ple_w_gate", "ple_w_proj")
    ws = dict(zip(names, (ln_g, ln_b, ffn1_w_in, ffn1_w_out, w_in, ret_gn_g, w_ret_o, q_norm_g, kv_norm_g, w_uq, w_ukv, w_mla_o,
                          w_out, ffn2_w_in, ffn2_w_out, ple_w_gate, ple_w_proj)))
    ms = dict(zip(names, (m_ln_g, m_ln_b, m_ffn1_w_in, m_ffn1_w_out, m_w_in, m_ret_gn_g, m_w_ret_o, m_q_norm_g, m_kv_norm_g, m_w_uq,
                          m_w_ukv, m_w_mla_o, m_w_out, m_ffn2_w_in, m_ffn2_w_out, m_ple_w_gate, m_ple_w_proj)))
    vs = dict(zip(names, (v_ln_g, v_ln_b, v_ffn1_w_in, v_ffn1_w_out, v_w_in, v_ret_gn_g, v_w_ret_o, v_q_norm_g, v_kv_norm_g, v_w_uq,
                          v_w_ukv, v_w_mla_o, v_w_out, v_ffn2_w_in, v_ffn2_w_out, v_ple_w_gate, v_ple_w_proj)))

    parts = []
    for name, r in PACK:
        rows = _to_rows(name, ws[name])
        if _pad16(r) != r:
            rows = jnp.concatenate([rows, jnp.zeros((_pad16(r) - r, 1024), F32)], axis=0)
        parts.append(rows)
    wsh_first = jnp.concatenate(parts[:len(PACK_LATE)], axis=0).astype(BF16)
    wsh_rest = jnp.concatenate(parts[len(PACK_LATE):], axis=0).astype(BF16)
    ssh = jnp.concatenate([ln_g[0], ln_b[0]], axis=0)
    wall_first, sall = _all_gather(wsh_first, ssh)
    *gather_handles, start_token = _gather_start(wsh_rest, wall_first)
    W = _unpack(wall_first, PACK_LATE)
    ln_full = sall.reshape(N_DEV, 2, 4, 128).transpose(1, 2, 0, 3).reshape(2, 4, 1024)

    def rest_weights(after):
        w_thru, land = _gather_wait(*gather_handles, after)
        return _unpack(_gather_finish(w_thru, land), PACK_EARLY)

    cvec = lax.axis_index("c").astype(jnp.int32).reshape(1)
    qvec = (2 * lax.axis_index("x") + lax.axis_index("y")).astype(jnp.int32).reshape(1)

    def chip_partials(G, group, tag):
        gparts = []
        for name, r in group:
            g = G[name].reshape(N_DEV, r, 1024)
            if _pad16(r) != r:
                g = jnp.concatenate([g, jnp.zeros((N_DEV, _pad16(r) - r, 1024), g.dtype)], axis=1)
            gparts.append(g)
        gfull = jnp.concatenate(gparts, axis=1).astype(BF16)
        g4 = gfull.reshape(4, 2, gfull.shape[1], 1024)
        return _sum_sibling(g4, _exchange_sibling(g4, "exchange_sibling_" + tag), cvec, "sum_sibling_" + tag)

    in_flight = []

    def early(G):
        *handles, token = _chips_start(chip_partials(G, PACK_EARLY, "early"), "chips_start_early")
        in_flight.append(handles)
        return token

    loss_p, grad_x, G, small = _local_step(x[0], p[0, 0], positions, loss_target[0], W, ln_full[0], ln_full[1],
                                           ret_gn_g, q_norm_g, kv_norm_g, early=early, rest_weights=rest_weights,
                                           start_token=start_token)

    part_e, land_e = _chips_wait(*in_flight[0], grad_x, "chips_wait_early")
    gsh_early = _sum_chips(part_e, land_e, qvec, "sum_grads_early")
    pad256 = lambda a: jnp.concatenate([a, jnp.zeros((1, 1024 - a.shape[1]), F32)], axis=1)
    gsmall = jnp.concatenate([small["ln_g"], small["ln_b"], small["ret_gn_g"].reshape(2, 1024), pad256(small["q_norm_g"]),
                              pad256(small["kv_norm_g"]), pad256(loss_p[0:1, 0:128]),
                              jnp.zeros((SMALL_ROWS - 13, 1024), F32)], axis=0)
    srecv = _exchange_small(gsmall)
    part_l = chip_partials(G, PACK_LATE, "late")
    *late_handles, late_token = _chips_start(part_l, "chips_start_late", order_after=srecv)
    ssum = _sum_slots(srecv, "sum_small_grads")

    def unpack_grads(group, gsh):
        out, off = {}, 0
        for name, r in group:
            out[name] = _from_rows(name, gsh[off:off + r], ws[name].shape)
            off += _pad16(r)
        return out

    grads = unpack_grads(PACK_EARLY, gsh_early)
    me = 4 * lax.axis_index("x") + 2 * lax.axis_index("y") + lax.axis_index("c")
    grads["ln_g"] = lax.dynamic_slice(ssum[0:4], (0, me * 128), (4, 128)).reshape(1, 4, 128)
    grads["ln_b"] = lax.dynamic_slice(ssum[4:8], (0, me * 128), (4, 128)).reshape(1, 4, 128)
    grads["ret_gn_g"] = ssum[8:10].reshape(1, 2048)
    grads["q_norm_g"] = ssum[10:11, :256]
    grads["kv_norm_g"] = ssum[11:12, :256]

    delta, new_m, new_v = {}, {}, {}
    late_names = [n for n, _ in PACK_LATE]
    last = late_token
    for name in names:
        if name not in late_names:
            delta[name], new_m[name], new_v[name] = _adamw(ws[name], grads[name], ms[name], vs[name], "adamw_" + name,
                                                           order_after=last)
            last = new_v[name]
    part_l, land_l = _chips_wait(*late_handles, last, "chips_wait_late")
    grads.update(unpack_grads(PACK_LATE, _sum_chips(part_l, land_l, qvec, "sum_grads_late")))
    for name in late_names:
        delta[name], new_m[name], new_v[name] = _adamw(ws[name], grads[name], ms[name], vs[name], "adamw_" + name)

    return (ssum[12, 0], grad_x[None], *[grads[n] for n in names], *[delta[n] for n in names],
            *[new_m[n] for n in names], *[new_v[n] for n in names])
```

```python
import math

import jax
import jax.numpy as jnp
from jax import lax
from jax.experimental import pallas as pl
from jax.experimental.pallas import tpu as pltpu

F32 = jnp.float32
BF16 = jnp.bfloat16

N_DEV = 8
D = 1024
D_FF = 2816
D_PLE = 256
CHUNK = 64
HEADS = 8
RET_DK = 128
RET_DV = 256
MLA_NOPE = 128
MLA_ROPE = 64
MLA_DV = 128
LORA = 256
ROPE_BASE = 10000.0
EPS = 1e-5
ALPHA = 2.0 ** 0.25
RET_SCALE = RET_DK ** -0.5
MLA_SCALE = (MLA_NOPE + MLA_ROPE) ** -0.5
NEG = -1e30

ADAM_LR = 0.001
ADAM_B1 = 0.9
ADAM_B2 = 0.999
ADAM_EPS = 1e-08
ADAM_WD = 0.01
ADAM_STEP = 10

P_RQ, P_RK, P_RV, P_RG, P_GR, P_GM, P_CQ, P_CKV, P_KPE, P_W = 0, 1024, 2048, 4096, 6144, 7168, 8192, 8448, 8704, 8960
W_IN_COLS = 8768
RET_L = 256
ATT_TF = 2048
ATT_TB = 1024
ATT_HP = 1
LOG2E = math.log2(math.e)
Q_PRESCALE = MLA_SCALE * LOG2E

PACK = (("ffn1_w_in", 704), ("ffn1_w_out", 352), ("w_in", 1096), ("w_ret_o", 256), ("w_uq", 48), ("w_ukv", 64),
        ("w_mla_o", 128), ("w_out", 128), ("ffn2_w_in", 704), ("ffn2_w_out", 352), ("ple_w_gate", 128), ("ple_w_proj", 32))


def _pad16(r):
    return -(-r // 16) * 16


PACK_LATE = PACK[:2]
PACK_EARLY = PACK[2:]
SMALL_ROWS = 16


def _pcall(body, **kw):
    return pl.pallas_call(body, **kw)


def _pick(dim, prefs):
    for p in prefs:
        if dim % p == 0:
            return p
    return dim


def _sigmoid(x):
    return 1.0 / (1.0 + jnp.exp(-x))


def _silu(x):
    return x * _sigmoid(x)


def _ln(r, g, b):
    mu = jnp.mean(r, axis=-1, keepdims=True)
    var = jnp.mean(jnp.square(r - mu), axis=-1, keepdims=True)
    return (r - mu) * lax.rsqrt(var + EPS) * g + b


def _rms(x, g):
    return x * lax.rsqrt(jnp.mean(jnp.square(x), axis=-1, keepdims=True) + EPS) * g


def _dot(a, b, ca, cb):
    return lax.dot_general(a, b, (((ca,), (cb,)), ((), ())), preferred_element_type=F32)


def _accum(ref, val, first=None):
    @pl.when(pl.program_id(0) == 0 if first is None else first)
    def _():
        ref[...] = jnp.zeros_like(ref)

    ref[...] += val


def _mm(a, b, *, ta=False, tb=False, add=None, out_dtype=None, name, tm=None, tn=None, tk=None, epilogue=None):
    parts = a.shape[0] if a.ndim == 3 else 1
    ar, ac = a.shape[-2], a.shape[-1]
    out_dtype = out_dtype or (BF16 if ta else F32)
    if ta:
        K, M = ar, ac * parts
    else:
        M, K = ar, ac * parts
    if tb:
        N, K2 = b.shape
    else:
        K2, N = b.shape
    assert K == K2, (a.shape, b.shape, ta, tb)
    big = (1024, 1408, 1280, 768, 512, 256, 128)
    tm = tm or _pick(ac if (ta and parts > 1) else M, big)
    tn = tn or (N if N <= 1024 else _pick(N, big))
    kdim = ac if (not ta and parts > 1) else K
    tk = tk or (kdim if kdim <= 2816 and parts == 1 else
                _pick(kdim, (2048, 1408, 1280, 1024, 512) if tm <= 1024 else (1024, 1408, 1280, 512)))
    nk = K // tk
    grid = (M // tm, N // tn, nk)
    if parts > 1 and ta:
        per = ac // tm
        a_spec = pl.BlockSpec((None, tk, tm), lambda i, j, k: (i // per, k, i % per))
    elif parts > 1:
        per = ac // tk
        a_spec = pl.BlockSpec((None, tm, tk), lambda i, j, k: (k // per, i, k % per))
    else:
        a_spec = pl.BlockSpec((tk, tm), lambda i, j, k: (k, i)) if ta else pl.BlockSpec((tm, tk), lambda i, j, k: (i, k))
    b_spec = pl.BlockSpec((tn, tk), lambda i, j, k: (j, k)) if tb else pl.BlockSpec((tk, tn), lambda i, j, k: (k, j))
    o_spec = pl.BlockSpec((tm, tn), lambda i, j, k: (i, j))
    ca, cb = (0 if ta else 1), (1 if tb else 0)
    has_add = add is not None
    n_in = 2 + int(has_add)
    if epilogue is not None:
        assert tn == N and not ta
        ep_fn, ep_rows, ep_whole, ep_outs, ep_accs = epilogue
        n_ep_in = len(ep_rows) + len(ep_whole)
        n_out = len(ep_outs) + len(ep_accs)
    else:
        n_ep_in, n_out = 0, 1

    def body(*refs):
        a_ref, b_ref = refs[0], refs[1]
        add_ref = refs[2] if has_add else None
        o_ref = refs[n_in + n_ep_in]
        first_row_tile = pl.program_id(0) == 0

        def finish(r):
            if has_add:
                r = r + add_ref[...].astype(F32)
            if epilogue is not None:
                ep_fn(r, refs[n_in:n_in + n_ep_in], refs[n_in + n_ep_in:n_in + n_ep_in + n_out], first_row_tile)
            else:
                o_ref[...] = r.astype(out_dtype)

        if nk == 1:
            finish(_dot(a_ref[...], b_ref[...], ca, cb))
            return
        acc_ref = refs[-1]
        k = pl.program_id(2)

        @pl.when(k == 0)
        def _():
            acc_ref[...] = jnp.zeros_like(acc_ref)

        acc_ref[...] += _dot(a_ref[...], b_ref[...], ca, cb)

        @pl.when(k == nk - 1)
        def _():
            finish(acc_ref[...])

    in_specs = [a_spec, b_spec] + ([o_spec] if has_add else [])
    args = (a, b) + ((add,) if has_add else ())
    out_specs, out_shape = o_spec, jax.ShapeDtypeStruct((M, N), out_dtype)
    if epilogue is not None:
        for row_in in ep_rows:
            arr, col_block, width = (tuple(row_in) + (N,))[:3] if isinstance(row_in, tuple) else (row_in, 0, N)
            in_specs.append(pl.BlockSpec((tm, width), lambda i, j, k, _c=col_block: (i, _c)))
            args += (arr,)
        in_specs += [pl.BlockSpec(w.shape, lambda i, j, k, _n=w.ndim: (0,) * _n) for w in ep_whole]
        args += tuple(ep_whole)
        outs = [o if isinstance(o, tuple) else (o, N) for o in ep_outs]
        out_specs = ([pl.BlockSpec((tm, w), lambda i, j, k: (i, 0)) for _, w in outs]
                     + [pl.BlockSpec((r, N), lambda i, j, k: (0, 0)) for r in ep_accs])
        out_shape = [jax.ShapeDtypeStruct((M, w), dt) for dt, w in outs] + [jax.ShapeDtypeStruct((r, N), F32) for r in ep_accs]
    return _pcall(
        body, grid=grid, in_specs=in_specs, out_specs=out_specs, out_shape=out_shape,
        scratch_shapes=[pltpu.VMEM((tm, tn), F32)] if nk > 1 else [], name=name,
        compiler_params=pltpu.CompilerParams(dimension_semantics=("arbitrary" if epilogue is not None else "parallel", "parallel", "arbitrary")),
    )(*args)


def _mm_pieces(pieces, add, name):
    M, N = add.shape
    tm = _pick(M, (1024, 512, 256, 128))
    spans, in_specs, args, start = [], [], [], 0
    for a, b in pieces:
        K = a.shape[1]
        assert b.shape == (K, N) and a.shape[0] == M, (a.shape, b.shape)
        tk = _pick(K, (1536, 1024, 768, 512, 256, 128))
        n = K // tk
        spans.append((start, n))
        block = lambda k, _s=start, _n=n: jnp.where(k < _s, _n - 1, jnp.minimum(k - _s, _n - 1))
        in_specs.append(pl.BlockSpec((tm, tk), lambda i, k, _s=start, _blk=block: (jnp.where(k < _s, jnp.maximum(i - 1, 0), i), _blk(k))))
        in_specs.append(pl.BlockSpec((tk, N), lambda i, k, _blk=block: (_blk(k), 0)))
        args += [a, b]
        start += n
    o_spec = pl.BlockSpec((tm, N), lambda i, k: (i, 0))

    def body(*refs):
        add_ref, o_ref = refs[-2], refs[-1]
        k = pl.program_id(1)

        @pl.when(k == 0)
        def _():
            o_ref[...] = add_ref[...]

        for p, (s, n) in enumerate(spans):
            @pl.when((k >= s) & (k < s + n))
            def _(p=p):
                o_ref[...] += _dot(refs[2 * p][...], refs[2 * p + 1][...], 1, 0)

    return _pcall(
        body, grid=(M // tm, start), in_specs=in_specs + [o_spec], out_specs=o_spec, out_shape=jax.ShapeDtypeStruct((M, N), F32),
        name=name, compiler_params=pltpu.CompilerParams(dimension_semantics=("parallel", "arbitrary")),
    )(*args, add)


def _ln_epilogue(res, c, g, b):
    def fn(r, ins, outs, first):
        res_ref, g_ref, b_ref = ins
        f_ref, h_ref, hb_ref = outs
        h = _ln(ALPHA * res_ref[...] + c * r, g_ref[...], b_ref[...])
        f_ref[...] = r
        h_ref[...] = h
        hb_ref[...] = h.astype(BF16)

    return (fn, [res], [g, b], (F32, F32, BF16), ())


def _ln_bwd_epilogue(res, f, c, g, b):
    def fn(r, ins, outs, first):
        res_ref, f_ref, g_ref, b_ref = ins
        dr_ref, df_ref, dg_ref, db_ref = outs
        pre = ALPHA * res_ref[...] + c * f_ref[...]
        xc = pre - jnp.mean(pre, axis=-1, keepdims=True)
        rstd = lax.rsqrt(jnp.mean(jnp.square(xc), axis=-1, keepdims=True) + EPS)
        xhat = xc * rstd
        dyg = r * g_ref[...]
        dpre = rstd * (dyg - jnp.mean(dyg, axis=-1, keepdims=True) - xhat * jnp.mean(dyg * xhat, axis=-1, keepdims=True))
        dr_ref[...] = ALPHA * dpre
        df_ref[...] = (c * dpre).astype(BF16)
        _accum(dg_ref, jnp.sum(r * xhat, axis=0, keepdims=True), first)
        _accum(db_ref, jnp.sum(r, axis=0, keepdims=True), first)

    return (fn, [res, f], [g, b], (F32, BF16), (1, 1))


ROW_TM = 1024


def _rows(body, T, tm, ins, outs, name, accs=()):
    in_specs, args = [], []
    for arr, w, cb in ins:
        if w is None:
            in_specs.append(pl.BlockSpec(arr.shape, lambda i, _n=arr.ndim: (0,) * _n))
        else:
            in_specs.append(pl.BlockSpec((tm, w), lambda i, _cb=cb: (i, _cb)))
        args.append(arr)
    out_specs = [pl.BlockSpec((tm, w), lambda i: (i, 0)) for w, _ in outs]
    out_shape = [jax.ShapeDtypeStruct((T, w), dt) for w, dt in outs]
    for r, w in accs:
        out_specs.append(pl.BlockSpec((r, w), lambda i: (0, 0)))
        out_shape.append(jax.ShapeDtypeStruct((r, w), F32))
    return _pcall(
        body, grid=(T // tm,), in_specs=in_specs, out_specs=out_specs, out_shape=out_shape, name=name,
        compiler_params=pltpu.CompilerParams(dimension_semantics=("arbitrary",)),
    )(*args)


def _ln_bwd_rows(res, f, c, g, b, dh, T, name):
    fn = _ln_bwd_epilogue(res, f, c, g, b)[0]

    def body(r_ref, f_ref, g_ref, b_ref, d_ref, dr_ref, df_ref, dg_ref, db_ref):
        fn(d_ref[...], (r_ref, f_ref, g_ref, b_ref), (dr_ref, df_ref, dg_ref, db_ref), pl.program_id(0) == 0)

    return _rows(body, T, min(ROW_TM, T), [(res, D, 0), (f, D, 0), (g, None, None), (b, None, None), (dh, D, 0)],
                 [(D, F32), (D, BF16)], name, accs=[(1, D), (1, D)])


FFN_TN = 1408
FFN_TM = 512
LN_TM = 512


def _ffn_in(x, wt, name, order_after=None):
    T = x.shape[0]
    tm, tn = min(FFN_TM, T), FFN_TN
    nj = D_FF // tn
    emit_xb = x.dtype != BF16
    n_in = 3 + int(order_after is not None)

    def body(*refs):
        x_ref, wg_ref, wu_ref = refs[:3]
        hs_ref, gu_ref = refs[n_in], refs[n_in + 1]
        xv = x_ref[...].astype(BF16)
        g = _dot(xv, wg_ref[...], 1, 1)
        u = _dot(xv, wu_ref[...], 1, 1)
        hs_ref[...] = (_silu(g) * u).astype(BF16)
        gu_ref[0] = g.astype(BF16)
        gu_ref[1] = u.astype(BF16)
        if emit_xb:
            refs[n_in + 2][...] = xv

    in_specs = [pl.BlockSpec((tm, D), lambda i, j: (i, 0)), pl.BlockSpec((tn, D), lambda i, j: (j, 0)),
                pl.BlockSpec((tn, D), lambda i, j: (j + nj, 0))]
    args = [x, wt, wt]
    if order_after is not None:
        in_specs.append(pl.BlockSpec(order_after.shape, lambda i, j: (0, 0)))
        args.append(order_after)
    out_specs = [pl.BlockSpec((tm, tn), lambda i, j: (i, j)), pl.BlockSpec((2, tm, tn), lambda i, j: (0, i, j))]
    out_shape = [jax.ShapeDtypeStruct((T, D_FF), BF16), jax.ShapeDtypeStruct((2, T, D_FF), BF16)]
    if emit_xb:
        out_specs.append(pl.BlockSpec((tm, D), lambda i, j: (i, 0)))
        out_shape.append(jax.ShapeDtypeStruct((T, D), BF16))
    return _pcall(
        body, grid=(T // tm, nj), in_specs=in_specs, out_specs=out_specs, out_shape=out_shape, name=name,
        compiler_params=pltpu.CompilerParams(dimension_semantics=("parallel", "arbitrary")),
    )(*args)


def _ffn_act_bwd(df, wo, gu, name):
    T = df.shape[0]
    tm, tn = min(FFN_TM, T), FFN_TN

    def body(d_ref, w_ref, gu_ref, o_ref):
        dhs = _dot(d_ref[...], w_ref[...], 1, 1)
        g, u = gu_ref[0].astype(F32), gu_ref[1].astype(F32)
        sig = _sigmoid(g)
        act = g * sig
        o_ref[0] = (dhs * u * (sig + act * (1.0 - sig))).astype(BF16)
        o_ref[1] = (dhs * act).astype(BF16)

    return _pcall(
        body, grid=(T // tm, D_FF // tn),
        in_specs=[pl.BlockSpec((tm, D), lambda i, j: (i, 0)), pl.BlockSpec((tn, D), lambda i, j: (j, 0)),
                  pl.BlockSpec((2, tm, tn), lambda i, j: (0, i, j))],
        out_specs=pl.BlockSpec((2, tm, tn), lambda i, j: (0, i, j)),
        out_shape=jax.ShapeDtypeStruct((2, T, D_FF), BF16), name=name,
        compiler_params=pltpu.CompilerParams(dimension_semantics=("parallel", "parallel")),
    )(df, wo, gu)


def _rope128(t, cos, sin_s):
    return t * cos + pltpu.roll(t, 64, 1) * sin_s


def _rope128_t(g, cos, sin_s):
    return g * cos - pltpu.roll(g, 64, 1) * sin_s


def _partner32(t):
    lane = lax.broadcasted_iota(jnp.int32, t.shape, 1)
    return jnp.where((lane & 32) == 0, pltpu.roll(t, 96, 1), pltpu.roll(t, 32, 1))


def _rope64(t, cos, sin_s):
    return t * cos + _partner32(t) * sin_s


def _rope64_t(g, cos, sin_s):
    return g * cos - _partner32(g) * sin_s


def _mixer_prep_fwd(proj, tabs, qg, kvg, T):
    cos128, sin128, cos64, sin64 = tabs

    def body(rq_ref, rk_ref, cq_ref, ckv_ref, kpe_ref, c1_ref, s1_ref, c2_ref, s2_ref, qg_ref, kvg_ref,
             oq_ref, ok_ref, oqn_ref, okvn_ref, okpe_ref):
        c1, s1 = c1_ref[...], s1_ref[...]
        for h in range(HEADS):
            sl = slice(h * RET_DK, (h + 1) * RET_DK)
            oq_ref[:, sl] = _rope128(rq_ref[:, sl].astype(F32), c1, s1).astype(BF16)
            ok_ref[:, sl] = (_rope128(rk_ref[:, sl].astype(F32), c1, s1) * RET_SCALE).astype(BF16)
        oqn_ref[...] = _rms(cq_ref[...].astype(F32), qg_ref[...]).astype(BF16)
        okvn_ref[...] = _rms(ckv_ref[...].astype(F32), kvg_ref[...]).astype(BF16)
        okpe_ref[...] = _rope64(kpe_ref[...].astype(F32), c2_ref[...], s2_ref[...]).astype(BF16)

    ins = [(proj, 1024, 0), (proj, 1024, 1), (proj, 256, P_CQ // 256), (proj, 256, P_CKV // 256),
           (proj, 128, P_KPE // 128), (cos128, 128, 0), (sin128, 128, 0), (cos64, 128, 0), (sin64, 128, 0),
           (qg, None, None), (kvg, None, None)]
    outs = [(1024, BF16), (1024, BF16), (LORA, BF16), (LORA, BF16), (128, BF16)]
    return _rows(body, T, min(ROW_TM, T), ins, outs, "mixer_prep_fwd")


def _rms_bwd(proj, dqn, dkvn, dkpe_all, tabs, qg, kvg, T):
    _, _, cos64, sin64 = tabs

    def body(cq_ref, ckv_ref, dq_ref, dkv_ref, dk_ref, c2_ref, s2_ref, qg_ref, kvg_ref, o_ref, dqg_ref, dkvg_ref):
        _, vjp = jax.vjp(_rms, cq_ref[...].astype(F32), qg_ref[...])
        dx, dg = vjp(dq_ref[...])
        o_ref[:, 0:LORA] = dx.astype(BF16)
        _accum(dqg_ref, dg)
        _, vjp = jax.vjp(_rms, ckv_ref[...].astype(F32), kvg_ref[...])
        dx, dg = vjp(dkv_ref[...])
        o_ref[:, P_CKV - P_CQ:P_CKV - P_CQ + LORA] = dx.astype(BF16)
        _accum(dkvg_ref, dg)
        g = dk_ref[:, 0:128]
        for h in range(1, HEADS):
            g = g + dk_ref[:, h * 128:(h + 1) * 128]
        lane = lax.broadcasted_iota(jnp.int32, g.shape, 1)
        g = jnp.where(lane < MLA_ROPE, g, 0.0)
        o_ref[:, P_KPE - P_CQ:P_KPE - P_CQ + 128] = _rope64_t(g, c2_ref[...], s2_ref[...]).astype(BF16)
        o_ref[:, P_KPE - P_CQ + 128:] = jnp.zeros((g.shape[0], P_W - P_KPE - 128), BF16)

    ins = [(proj, 256, P_CQ // 256), (proj, 256, P_CKV // 256), (dqn, LORA, 0), (dkvn, LORA, 0), (dkpe_all, 1024, 0),
           (cos64, 128, 0), (sin64, 128, 0), (qg, None, None), (kvg, None, None)]
    return _rows(body, T, min(ROW_TM, T), ins, [(P_W - P_CQ, BF16)], "rms_bwd", accs=[(1, LORA), (1, LORA)])


def _gn_gate_bwd(y, rg, g, d):
    xc = y - jnp.mean(y, axis=-1, keepdims=True)
    rstd = lax.rsqrt(jnp.mean(jnp.square(xc), axis=-1, keepdims=True) + EPS)
    xhat = xc * rstd
    sig = _sigmoid(rg)
    act = rg * sig
    dn = d * act
    drg = d * (xhat * g) * (sig + act * (1.0 - sig))
    dxh = dn * g
    dy = rstd * (dxh - jnp.mean(dxh, axis=-1, keepdims=True) - xhat * jnp.mean(dxh * xhat, axis=-1, keepdims=True))
    return dy, drg, jnp.sum(dn * xhat, axis=0, keepdims=True)


def _gn_gate(y, rg, g):
    mu = jnp.mean(y, axis=-1, keepdims=True)
    var = jnp.mean(jnp.square(y - mu), axis=-1, keepdims=True)
    return _silu(rg) * ((y - mu) * lax.rsqrt(var + EPS) * g)


def _q_assemble_epilogue(tabs):
    _, _, cos64, sin64 = tabs

    def fn(q, ins, outs, first):
        c_ref, s_ref = ins
        on_ref, op_ref = outs
        on_ref[...] = (q[:, :1024] * Q_PRESCALE).astype(BF16)
        c, s = c_ref[...], s_ref[...]
        lane = lax.broadcasted_iota(jnp.int32, c.shape, 1)
        for j in range(HEADS // 2):
            r = _rope64(q[:, 1024 + 128 * j:1024 + 128 * (j + 1)], c, s) * Q_PRESCALE
            op_ref[:, 256 * j:256 * j + 128] = jnp.where(lane < 64, r, 0.0).astype(BF16)
            op_ref[:, 256 * j + 128:256 * j + 256] = jnp.where(lane < 64, pltpu.roll(r, 64, 1), 0.0).astype(BF16)

    return (fn, [(cos64, 0, 128), (sin64, 0, 128)], [], ((BF16, 1024), (BF16, 1024)), ())


def _q_assemble_bwd(dqn, dqpe, tabs, T):
    _, _, cos64, sin64 = tabs

    def body(dn_ref, dp_ref, c_ref, s_ref, on_ref, op_ref):
        on_ref[...] = dn_ref[...].astype(BF16)
        c, s = c_ref[...], s_ref[...]
        lane = lax.broadcasted_iota(jnp.int32, c.shape, 1)
        for j in range(HEADS // 2):
            g = jnp.where(lane < 64, dp_ref[:, 256 * j:256 * j + 128], pltpu.roll(dp_ref[:, 256 * j + 128:256 * j + 256], 64, 1))
            op_ref[:, 128 * j:128 * (j + 1)] = _rope64_t(g, c, s).astype(BF16)

    return _rows(body, T, min(ROW_TM, T), [(dqn, 1024, 0), (dqpe, 1024, 0), (cos64, 128, 0), (sin64, 128, 0)],
                 [(1024, BF16), (512, BF16)], "q_assemble_bwd")


def _mix_fn(gr, gm, yr, ym):
    return _sigmoid(gr) * yr + _sigmoid(gm) * ym


def _mix_epilogue(proj, y_ret):
    def fn(r, ins, outs, first):
        gr_ref, gm_ref, yr_ref = ins
        ym_ref, mix_ref = outs
        ym_ref[...] = r.astype(BF16)
        mix_ref[...] = _mix_fn(gr_ref[...].astype(F32), gm_ref[...].astype(F32), yr_ref[...].astype(F32), r).astype(BF16)

    return (fn, [(proj, P_GR // D), (proj, P_GM // D), y_ret], [], (BF16, BF16), ())


def _mix_bwd_epilogue(proj, y_ret, y_mla):
    def fn(r, ins, outs, first):
        gr_ref, gm_ref, yr_ref, ym_ref = ins
        dgate_ref, dyr_ref, dym_ref = outs
        sr, sm = _sigmoid(gr_ref[...].astype(F32)), _sigmoid(gm_ref[...].astype(F32))
        dgate_ref[:, 0:D] = (r * yr_ref[...].astype(F32) * (sr * (1.0 - sr))).astype(BF16)
        dgate_ref[:, D:2 * D] = (r * ym_ref[...].astype(F32) * (sm * (1.0 - sm))).astype(BF16)
        dyr_ref[...] = (r * sr).astype(BF16)
        dym_ref[...] = (r * sm).astype(BF16)

    return (fn, [(proj, P_GR // D), (proj, P_GM // D), y_ret, y_mla], [], ((BF16, 2 * D), BF16, BF16), ())


def _head_epilogue(h3, pp, tgt, g, b):
    def fn(r, ins, outs, first):
        h_ref, pp_ref, t_ref, g_ref, b_ref = ins
        dh_ref, dgl_ref, dpp_ref, dg_ref, db_ref, loss_ref = outs

        sg, pp, gain = _sigmoid(r), pp_ref[...], g_ref[...]
        pre = ALPHA * h_ref[...] + sg * pp
        xc = pre - jnp.mean(pre, axis=-1, keepdims=True)
        rstd = lax.rsqrt(jnp.mean(jnp.square(xc), axis=-1, keepdims=True) + EPS)
        xhat = xc * rstd
        err = xhat * gain + b_ref[...] - t_ref[...]
        dy = err * (1.0 / D)
        dyg = dy * gain
        dpre = rstd * (dyg - jnp.mean(dyg, axis=-1, keepdims=True) - xhat * jnp.mean(dyg * xhat, axis=-1, keepdims=True))
        dh_ref[...] = ALPHA * dpre
        dgl_ref[...] = (dpre * pp * (sg * (1.0 - sg))).astype(BF16)
        dpp_ref[...] = (dpre * sg).astype(BF16)
        _accum(dg_ref, jnp.sum(dy * xhat, axis=0, keepdims=True), first)
        _accum(db_ref, jnp.sum(dy, axis=0, keepdims=True), first)
        part = 0.5 * jnp.sum(jnp.mean(jnp.square(err), axis=-1, keepdims=True), axis=0, keepdims=True)
        _accum(loss_ref, jnp.broadcast_to(part, loss_ref.shape), first)

    return (fn, [h3, pp, tgt], [g, b], (F32, BF16, BF16), (1, 1, 8))


def _delta_epilogue(o):
    def fn(r, ins, outs, first):
        (o_ref,) = ins
        db_ref, dl_ref = outs
        db_ref[...] = r.astype(BF16)
        for h in range(HEADS):
            sl = slice(h * MLA_DV, (h + 1) * MLA_DV)
            dl = jnp.sum(r[:, sl] * o_ref[:, sl].astype(F32), axis=-1, keepdims=True)
            dl_ref[:, sl] = jnp.broadcast_to(dl, (r.shape[0], MLA_DV))

    return (fn, [o], [], (BF16, F32), ())


def _ret_consts():
    L = RET_L
    lg = jnp.log(1.0 - 2.0 ** (-5.0 - jnp.arange(HEADS, dtype=F32)))[:, None, None]
    idx = jnp.arange(L, dtype=F32)
    ch = jnp.arange(L) // CHUNK
    dist = idx[:, None] - idx[None, :]
    same = (ch[:, None] == ch[None, :])[None]
    earlier = (ch[None, :] < ch[:, None])[None]
    dm = jnp.where(same, jnp.exp(lg * jnp.abs(dist)[None]), jnp.where(earlier, jnp.exp(lg * dist[None]), 0.0))
    xi = jnp.broadcast_to(jnp.exp(lg * (idx + 1.0)[None, :, None]), (HEADS, L, 128))
    zeta = jnp.broadcast_to(jnp.exp(lg * (L - 1.0 - idx)[None, :, None]), (HEADS, L, 128))
    gl = jnp.broadcast_to(jnp.exp(lg * float(L)), (HEADS, 8, 128))
    return dm.astype(F32), xi.astype(F32), zeta.astype(F32), gl.astype(F32)


def _whole(arr):
    return pl.BlockSpec(arr.shape, lambda n, _nd=arr.ndim: (0,) * _nd)


def _ret_fwd(q, k, v, proj, gn_g, consts, T):
    dm, xi, zeta, gl = consts
    L = RET_L
    n_sc = T // L

    def body(q_ref, k_ref, v_ref, rg_ref, g_ref, dm_ref, xi_ref, ze_ref, gl_ref, y_ref, yr_ref, s_ref, st_ref):
        @pl.when(pl.program_id(0) == 0)
        def _():
            st_ref[...] = jnp.zeros_like(st_ref)

        for h in range(HEADS):
            ks, vs = slice(h * RET_DK, (h + 1) * RET_DK), slice(h * RET_DV, (h + 1) * RET_DV)
            qq, kk, vv = q_ref[:, ks], k_ref[:, ks], v_ref[:, vs]
            st = st_ref[h]
            s_ref[h, 0] = st
            p = (_dot(qq, kk, 1, 1) * dm_ref[h]).astype(BF16)
            cross = _dot(qq, st.astype(BF16), 1, 0)
            xi_c = jnp.concatenate([xi_ref[h], xi_ref[h]], axis=1)
            y = _dot(p, vv, 1, 0) + cross * xi_c
            y_ref[:, vs] = y
            yr_ref[:, vs] = _gn_gate(y, rg_ref[:, vs].astype(F32), g_ref[:, vs]).astype(BF16)
            kz = (kk.astype(F32) * ze_ref[h]).astype(BF16)
            gl2 = jnp.concatenate([gl_ref[h, 0:1, :], gl_ref[h, 0:1, :]], axis=1)
            st_ref[h] = st * gl2 + _dot(kz, vv, 0, 0)

    return _pcall(
        body, grid=(n_sc,),
        in_specs=[pl.BlockSpec((L, 1024), lambda n: (n, 0)), pl.BlockSpec((L, 1024), lambda n: (n, 0)),
                  pl.BlockSpec((L, 2048), lambda n: (n, P_RV // 2048)), pl.BlockSpec((L, 2048), lambda n: (n, P_RG // 2048)),
                  _whole(gn_g), _whole(dm), _whole(xi), _whole(zeta), _whole(gl)],
        out_specs=[pl.BlockSpec((L, 2048), lambda n: (n, 0)), pl.BlockSpec((L, 2048), lambda n: (n, 0)),
                   pl.BlockSpec((HEADS, 1, 128, 256), lambda n: (0, n, 0, 0))],
        out_shape=[jax.ShapeDtypeStruct((T, HEADS * RET_DV), F32), jax.ShapeDtypeStruct((T, HEADS * RET_DV), BF16),
                   jax.ShapeDtypeStruct((HEADS, n_sc, 128, 256), F32)],
        scratch_shapes=[pltpu.VMEM((HEADS, 128, 256), F32)], name="ret_fwd",
        compiler_params=pltpu.CompilerParams(dimension_semantics=("arbitrary",)),
    )(q, k, v, proj, gn_g, dm, xi, zeta, gl)


def _ret_bwd(q, k, v, y, proj, gn_g, states, dyr, consts, tabs, T):
    dm, xi, zeta, gl = consts
    cos128, sin128, _, _ = tabs
    L = RET_L
    n_sc = T // L

    def body(q_ref, k_ref, v_ref, y_ref, rg_ref, g_ref, d_ref, s_ref, dm_ref, xi_ref, ze_ref, gl_ref, c_ref, sn_ref,
             dp_ref, dg_ref, gs_ref):
        @pl.when(pl.program_id(0) == 0)
        def _():
            gs_ref[...] = jnp.zeros_like(gs_ref)

        c, sn = c_ref[...], sn_ref[...]
        dgs = []
        for h in range(HEADS):
            ks, vs = slice(h * RET_DK, (h + 1) * RET_DK), slice(h * RET_DV, (h + 1) * RET_DV)
            dy, drg, dg = _gn_gate_bwd(y_ref[:, vs], rg_ref[:, vs].astype(F32), g_ref[:, vs], d_ref[:, vs].astype(F32))
            dp_ref[:, P_RG + h * RET_DV:P_RG + (h + 1) * RET_DV] = drg.astype(BF16)
            dgs.append(dg)
            qq, kk, vv, dyy = q_ref[:, ks], k_ref[:, ks], v_ref[:, vs], dy.astype(BF16)
            dmm = dm_ref[h]
            gb = gs_ref[h].astype(BF16)
            sb = s_ref[h, 0].astype(BF16)
            xi_c = jnp.concatenate([xi_ref[h], xi_ref[h]], axis=1)
            pb = (_dot(qq, kk, 1, 1) * dmm).astype(BF16)
            kz = (kk.astype(F32) * ze_ref[h]).astype(BF16)
            dp_ref[:, P_RV + h * RET_DV:P_RV + (h + 1) * RET_DV] = (_dot(pb, dyy, 0, 0) + _dot(kz, gb, 1, 0)).astype(BF16)
            da = (_dot(dyy, vv, 1, 1) * dmm).astype(BF16)
            dyx = (dyy.astype(F32) * xi_c).astype(BF16)
            dq = _dot(da, kk, 1, 0) + _dot(dyx, sb, 1, 1)
            dk = _dot(da, qq, 0, 0) + _dot(vv, gb, 1, 1) * ze_ref[h]
            dp_ref[:, P_RQ + h * RET_DK:P_RQ + (h + 1) * RET_DK] = _rope128_t(dq, c, sn).astype(BF16)
            dp_ref[:, P_RK + h * RET_DK:P_RK + (h + 1) * RET_DK] = (_rope128_t(dk, c, sn) * RET_SCALE).astype(BF16)
            gl2 = jnp.concatenate([gl_ref[h, 0:1, :], gl_ref[h, 0:1, :]], axis=1)
            gs_ref[h] = gs_ref[h] * gl2 + _dot(qq, dyx, 0, 0)
        _accum(dg_ref, jnp.concatenate(dgs, axis=1))

    rev = lambda n: n_sc - 1 - n
    return _pcall(
        body, grid=(n_sc,),
        in_specs=[pl.BlockSpec((L, 1024), lambda n: (rev(n), 0)), pl.BlockSpec((L, 1024), lambda n: (rev(n), 0)),
                  pl.BlockSpec((L, 2048), lambda n: (rev(n), P_RV // 2048)), pl.BlockSpec((L, 2048), lambda n: (rev(n), 0)),
                  pl.BlockSpec((L, 2048), lambda n: (rev(n), P_RG // 2048)), _whole(gn_g),
                  pl.BlockSpec((L, 2048), lambda n: (rev(n), 0)),
                  pl.BlockSpec((HEADS, 1, 128, 256), lambda n: (0, rev(n), 0, 0)),
                  _whole(dm), _whole(xi), _whole(zeta), _whole(gl),
                  pl.BlockSpec((L, 128), lambda n: (rev(n), 0)), pl.BlockSpec((L, 128), lambda n: (rev(n), 0))],
        out_specs=[pl.BlockSpec((L, P_GR), lambda n: (rev(n), 0)), pl.BlockSpec((1, 2048), lambda n: (0, 0))],
        out_shape=[jax.ShapeDtypeStruct((T, P_GR), BF16), jax.ShapeDtypeStruct((1, 2048), F32)],
        scratch_shapes=[pltpu.VMEM((HEADS, 128, 256), F32)], name="ret_bwd",
        compiler_params=pltpu.CompilerParams(dimension_semantics=("arbitrary",)),
    )(q, k, v, y, proj, gn_g, dyr, states, dm, xi, zeta, gl, cos128, sin128)


def _diag_mask(nrows, ncols, row0):
    row = row0 + lax.broadcasted_iota(jnp.int32, (nrows, ncols), 0)
    col = lax.broadcasted_iota(jnp.int32, (nrows, ncols), 1)
    return lax.shift_right_logical(col, 6) <= lax.shift_right_logical(row, 6)


def _diag_spans(t):
    h = t // 2
    return ((0, h, h), (h, h, t)) if h % 128 == 0 else ((0, t, t),)


def _tri_steps(nb, by_key):
    if by_key:
        pairs = [(i, j) for j in range(nb) for i in range(j, nb)]
    else:
        pairs = [(i, j) for i in range(nb) for j in range(i + 1)]
    return jnp.array([a for a, _ in pairs], jnp.int32), jnp.array([b for _, b in pairs], jnp.int32)


def _attn_fwd(qn, qpe, kv, kpe, T):
    t = min(ATT_TF, T)
    nb = T // t
    ii, jj = _tri_steps(nb, by_key=False)

    hp = ATT_HP
    w = 128 * hp

    def body(ii_ref, jj_ref, qn_ref, qp_ref, kn_ref, kp_ref, v_ref, o_ref, lse_ref, m_sc, l_sc, acc_sc):
        st = pl.program_id(1)
        i, j = ii_ref[st], jj_ref[st]

        @pl.when(j == 0)
        def _():
            m_sc[...] = jnp.full_like(m_sc, NEG)
            l_sc[...] = jnp.zeros_like(l_sc)
            acc_sc[...] = jnp.zeros_like(acc_sc)

        def update(diag):
            kp = kp_ref[...]
            spans = _diag_spans(t) if diag else ((0, t, t),)
            for hh in range(hp):
                sl = slice(128 * hh, 128 * (hh + 1))
                for row0, nr, nkeys in spans:
                    rows = slice(row0, row0 + nr)
                    q = jnp.concatenate([qn_ref[rows, sl], qp_ref[rows, sl]], axis=1)
                    k = jnp.concatenate([kn_ref[:nkeys, sl], kp[:nkeys]], axis=1)
                    s = _dot(q, k, 1, 1)
                    if diag:
                        s = jnp.where(_diag_mask(nr, nkeys, row0), s, NEG)
                    m_prev = m_sc[rows, sl]
                    m_new = jnp.maximum(m_prev, jnp.max(s, axis=1, keepdims=True))
                    a = jnp.exp2(m_prev - m_new)
                    p = jnp.exp2(s - m_new[:, 0:1])
                    l_sc[rows, sl] = a * l_sc[rows, sl] + jnp.sum(p, axis=1, keepdims=True)
                    acc_sc[rows, sl] = a * acc_sc[rows, sl] + _dot(p.astype(BF16), v_ref[:nkeys, sl], 1, 0)
                    m_sc[rows, sl] = m_new

        @pl.when(j < i)
        def _():
            update(False)

        @pl.when(j == i)
        def _():
            update(True)
            o_ref[...] = (acc_sc[...] / l_sc[...]).astype(BF16)
            lse_ref[...] = m_sc[...] + jnp.log2(l_sc[...])

    qs = pl.BlockSpec((t, w), lambda h, s, ii, jj: (ii[s], h))
    grid_spec = pltpu.PrefetchScalarGridSpec(
        num_scalar_prefetch=2, grid=(HEADS // hp, int(ii.shape[0])),
        in_specs=[qs, qs, pl.BlockSpec((t, w), lambda h, s, ii, jj: (jj[s], h)), pl.BlockSpec((t, 128), lambda h, s, ii, jj: (jj[s], 0)),
                  pl.BlockSpec((t, w), lambda h, s, ii, jj: (jj[s], HEADS // hp + h))],
        out_specs=[qs, qs],
        scratch_shapes=[pltpu.VMEM((t, w), F32), pltpu.VMEM((t, w), F32), pltpu.VMEM((t, w), F32)])
    return _pcall(
        body, grid_spec=grid_spec, out_shape=[jax.ShapeDtypeStruct((T, D), BF16), jax.ShapeDtypeStruct((T, D), F32)], name="attn_fwd",
        compiler_params=pltpu.CompilerParams(dimension_semantics=("arbitrary", "arbitrary")),
    )(ii, jj, qn, qpe, kv, kpe, kv)


def _attn_bwd(qn, qpe, kv, kpe, do, lse, delta, T):
    t = min(ATT_TB, T)
    nb = T // t
    ii, jj = _tri_steps(nb, by_key=True)

    def body(ii_ref, jj_ref, qn_ref, qp_ref, kn_ref, kp_ref, v_ref, do_ref, lse_ref, dl_ref,
             dqn_ref, dqp_ref, dkn_ref, dkp_ref, dv_ref, dk_sc, dv_sc):
        st = pl.program_id(1)
        i, j = ii_ref[st], jj_ref[st]

        @pl.when(st == 0)
        def _():
            dqn_ref[...] = jnp.zeros_like(dqn_ref)
            dqp_ref[...] = jnp.zeros_like(dqp_ref)

        @pl.when(i == j)
        def _():
            dk_sc[...] = jnp.zeros_like(dk_sc)
            dv_sc[...] = jnp.zeros_like(dv_sc)

        def update(diag):
            for row0, nr, nkeys in (_diag_spans(t) if diag else ((0, t, t),)):
                rows = slice(row0, row0 + nr)
                q = jnp.concatenate([qn_ref[rows, :], qp_ref[rows, :]], axis=1)
                k = jnp.concatenate([kn_ref[:nkeys, :], kp_ref[:nkeys, :]], axis=1)
                dob = do_ref[rows, :]
                s = _dot(q, k, 1, 1)
                if diag:
                    s = jnp.where(_diag_mask(nr, nkeys, row0), s, NEG)
                p = jnp.exp2(s - lse_ref[rows, 0:1])
                dv_sc[:nkeys, :] += _dot(p.astype(BF16), dob, 0, 0)
                dp = _dot(dob, v_ref[:nkeys, :], 1, 1)
                ds = (p * (dp - dl_ref[rows, 0:1])).astype(BF16)
                dk_sc[:nkeys, :] += _dot(ds, q, 0, 0)
                dq = _dot(ds, k, 1, 0) * MLA_SCALE
                grows = pl.ds(pl.multiple_of(i * t + row0, nr), nr)
                dqn_ref[grows, :] += dq[:, :128]
                dqp_ref[grows, :] += dq[:, 128:]

        @pl.when(i > j)
        def _():
            update(False)

        @pl.when(i == j)
        def _():
            update(True)

        @pl.when(i == nb - 1)
        def _():
            dkn_ref[...] = (dk_sc[:, :128] * (1.0 / LOG2E)).astype(BF16)
            dkp_ref[...] = dk_sc[:, 128:] * (1.0 / LOG2E)
            dv_ref[...] = dv_sc[...].astype(BF16)

    qs = pl.BlockSpec((t, 128), lambda h, s, ii, jj: (ii[s], h))
    ks = pl.BlockSpec((t, 128), lambda h, s, ii, jj: (jj[s], h))
    hs = pl.BlockSpec((T, 128), lambda h, s, ii, jj: (0, h))
    grid_spec = pltpu.PrefetchScalarGridSpec(
        num_scalar_prefetch=2, grid=(HEADS, int(ii.shape[0])),
        in_specs=[qs, qs, ks, pl.BlockSpec((t, 128), lambda h, s, ii, jj: (jj[s], 0)),
                  pl.BlockSpec((t, 128), lambda h, s, ii, jj: (jj[s], HEADS + h)), qs, qs, qs],
        out_specs=[hs, hs, ks, ks, ks],
        scratch_shapes=[pltpu.VMEM((t, 256), F32), pltpu.VMEM((t, 128), F32)])
    return _pcall(
        body, grid_spec=grid_spec,
        out_shape=[jax.ShapeDtypeStruct((T, D), F32), jax.ShapeDtypeStruct((T, D), F32), jax.ShapeDtypeStruct((T, D), BF16),
                   jax.ShapeDtypeStruct((T, D), F32), jax.ShapeDtypeStruct((T, D), BF16)], name="attn_bwd",
        compiler_params=pltpu.CompilerParams(dimension_semantics=("arbitrary", "arbitrary")),
    )(ii, jj, qn, qpe, kv, kpe, kv, do, lse, delta)


def _mesh_pos():
    x, y, c = lax.axis_index("x"), lax.axis_index("y"), lax.axis_index("c")
    return x, y, c, 4 * x + 2 * y + c


def _peer(x, y, c, k):
    px, py, pc = (x + ((k >> 2) & 1)) % 2, (y + ((k >> 1) & 1)) % 2, (c + (k & 1)) % 2
    return (px, py, pc), 4 * px + 2 * py + pc


_ANY = pl.BlockSpec(memory_space=pl.ANY)


def _rcopy(src, dst, send_sems, recv_sems, k, to):
    return pltpu.make_async_remote_copy(src_ref=src, dst_ref=dst, send_sem=send_sems.at[k], recv_sem=recv_sems.at[k],
                                        device_id=to, device_id_type=pl.DeviceIdType.MESH)


def _all_gather(wsh, ssh):
    def body(w_ref, s_ref, wall_ref, sall_ref, send_sems, recv_sems, loc_sems):
        x, y, c, me = _mesh_pos()
        sib = (x, y, 1 - c)
        chips = [(1 - x, y), (x, 1 - y), (1 - x, 1 - y)]
        slot = lambda px, py, pc: 4 * px + 2 * py + pc
        loc = [pltpu.make_async_copy(s_ref, sall_ref.at[me], loc_sems.at[0])]
        for cp in loc:
            cp.start()
        sends, fwd_waits = [], []
        for n, (src, dst) in enumerate(((w_ref, wall_ref), (s_ref, sall_ref))):
            o = 7 * n
            sends.append(_rcopy(src, dst.at[me], send_sems, recv_sems, o, sib))
            for j, chip in enumerate(chips):
                sends.append(_rcopy(src, dst.at[me], send_sems, recv_sems, o + 1 + j, (*chip, c)))
        for cp in sends:
            cp.start()
        for n, (src, dst) in enumerate(((w_ref, wall_ref), (s_ref, sall_ref))):
            o = 7 * n
            for j, chip in enumerate(chips):
                got = dst.at[slot(*chip, c)]
                _rcopy(src, got, send_sems, recv_sems, o + 1 + j, sib).wait_recv()
                fw = _rcopy(got, got, send_sems, recv_sems, o + 4 + j, sib)
                fw.start()
                sends.append(fw)
            fwd_waits.append(_rcopy(src, dst.at[slot(x, y, 1 - c)], send_sems, recv_sems, o, sib))
            for j, chip in enumerate(chips):
                fwd_waits.append(_rcopy(src, dst.at[slot(*chip, 1 - c)], send_sems, recv_sems, o + 4 + j, sib))
        for cp in fwd_waits:
            cp.wait_recv()
        for cp in sends:
            cp.wait_send()
        for cp in loc:
            cp.wait()

    wall, sall = _pcall(
        body, in_specs=[_ANY, _ANY], out_specs=[_ANY, _ANY],
        out_shape=[jax.ShapeDtypeStruct((N_DEV,) + wsh.shape, wsh.dtype), jax.ShapeDtypeStruct((N_DEV,) + ssh.shape, ssh.dtype)],
        scratch_shapes=[pltpu.SemaphoreType.DMA((14,)), pltpu.SemaphoreType.DMA((14,)), pltpu.SemaphoreType.DMA((1,))],
        name="all_gather_weights",
    )(wsh, ssh)
    me = 4 * lax.axis_index("x") + 2 * lax.axis_index("y") + lax.axis_index("c")
    return lax.dynamic_update_index_in_dim(wall, wsh, me, axis=0), sall


_HBM = pl.BlockSpec(memory_space=pltpu.HBM)
_SEM = pl.BlockSpec(memory_space=pltpu.SEMAPHORE)
_EFFECT = pltpu.SideEffectType.DATAFLOW_SIDE_EFFECTING


def _other_chips(x, y):
    return [(1 - x, y), (x, 1 - y), (1 - x, 1 - y)]


def _gather_start(wsh, order_after):
    def body(w_ref, land_ref, dep_ref, send_sems, recv_sems, w_thru, land_thru, token):
        x, y, c, me = _mesh_pos()
        _rcopy(w_ref, land_ref.at[me], send_sems, recv_sems, 0, (x, y, 1 - c)).start()
        for j, chip in enumerate(_other_chips(x, y)):
            _rcopy(w_ref, land_ref.at[me], send_sems, recv_sems, 1 + j, (*chip, c)).start()
        token[...] = jnp.zeros_like(token)

    shape = (N_DEV,) + wsh.shape
    land = pltpu.with_memory_space_constraint(lax.empty(shape, wsh.dtype), pltpu.HBM)
    return _pcall(
        body, name="gather_start",
        out_shape=(pltpu.SemaphoreType.DMA((4,)), pltpu.SemaphoreType.DMA((4,)), pltpu.HBM(wsh.shape, wsh.dtype),
                   pltpu.HBM(shape, wsh.dtype), jax.ShapeDtypeStruct((8, 128), F32)),
        in_specs=(_HBM, _HBM, _ANY), out_specs=(_SEM, _SEM, _HBM, _HBM, pl.BlockSpec(memory_space=pltpu.VMEM)),
        input_output_aliases={0: 2, 1: 3}, compiler_params=pltpu.CompilerParams(has_side_effects=_EFFECT),
    )(pltpu.with_memory_space_constraint(wsh, pltpu.HBM), land, order_after)


def _gather_wait(send_sems, recv_sems, w_thru, land_thru, after):
    def body(w_ref, land_ref, send_sems, recv_sems, after_ref, w_out, land_out):
        x, y, c, _ = _mesh_pos()
        senders = [(x, y, 1 - c)] + [(*chip, c) for chip in _other_chips(x, y)]
        for k, (px, py, pc) in enumerate(senders):
            cp = _rcopy(w_ref, land_ref.at[4 * px + 2 * py + pc], send_sems, recv_sems, k, (px, py, pc))
            cp.wait_send()
            cp.wait_recv()

    return _pcall(
        body, name="gather_wait", out_shape=(pltpu.HBM(w_thru.shape, w_thru.dtype), pltpu.HBM(land_thru.shape, land_thru.dtype)),
        in_specs=(_HBM, _HBM, _SEM, _SEM, _ANY), out_specs=(_HBM, _HBM), input_output_aliases={0: 0, 1: 1},
        compiler_params=pltpu.CompilerParams(has_side_effects=_EFFECT),
    )(w_thru, land_thru, send_sems, recv_sems, after)


def _gather_finish(wsh, land):
    def body(land_ref, out_ref, send_sems, recv_sems):
        x, y, c, _ = _mesh_pos()
        sib = (x, y, 1 - c)
        sends, recvs = [], []
        for j, (px, py) in enumerate(_other_chips(x, y)):
            mine, theirs = 4 * px + 2 * py + c, 4 * px + 2 * py + (1 - c)
            sends.append(_rcopy(land_ref.at[mine], out_ref.at[mine], send_sems, recv_sems, j, sib))
            recvs.append(_rcopy(land_ref.at[theirs], out_ref.at[theirs], send_sems, recv_sems, j, sib))
        for cp in sends:
            cp.start()
        for cp in recvs:
            cp.wait_recv()
        for cp in sends:
            cp.wait_send()

    done = _pcall(
        body, in_specs=[_ANY], out_specs=_ANY, out_shape=jax.ShapeDtypeStruct(land.shape, land.dtype),
        input_output_aliases={0: 0},
        scratch_shapes=[pltpu.SemaphoreType.DMA((3,)), pltpu.SemaphoreType.DMA((3,))], name="gather_finish",
    )(land)
    me = 4 * lax.axis_index("x") + 2 * lax.axis_index("y") + lax.axis_index("c")
    return lax.dynamic_update_index_in_dim(done, wsh, me, axis=0)


_ROW_TILES = (736, 528, 400, 368, 352, 256, 128, 16, 8)


def _exchange_sibling(g4, name):
    def body(g_ref, r_ref, send_sems, recv_sems):
        x, y, c, _ = _mesh_pos()
        sib = (x, y, 1 - c)
        cps = [_rcopy(g_ref.at[q, 1 - c], r_ref.at[q], send_sems, recv_sems, q, sib) for q in range(4)]
        for cp in cps:
            cp.start()
        for cp in cps:
            cp.wait()

    n, _, R, C = g4.shape
    return _pcall(
        body, in_specs=[_ANY], out_specs=_ANY, out_shape=jax.ShapeDtypeStruct((n, R, C), g4.dtype),
        scratch_shapes=[pltpu.SemaphoreType.DMA((4,)), pltpu.SemaphoreType.DMA((4,))], name=name,
    )(g4)


def _sum_sibling(g4, rsib, cvec, name):
    n, _, R, C = g4.shape
    tr = _pick(R, _ROW_TILES)

    def body(c_ref, g_ref, r_ref, o_ref):
        o_ref[...] = (g_ref[...].astype(F32) + r_ref[...].astype(F32)).astype(o_ref.dtype)

    grid_spec = pltpu.PrefetchScalarGridSpec(
        num_scalar_prefetch=1, grid=(n, R // tr),
        in_specs=[pl.BlockSpec((None, None, tr, C), lambda q, i, cr: (q, cr[0], i, 0)), pl.BlockSpec((None, tr, C), lambda q, i, cr: (q, i, 0))],
        out_specs=pl.BlockSpec((None, tr, C), lambda q, i, cr: (q, i, 0)))
    return _pcall(body, grid_spec=grid_spec, out_shape=jax.ShapeDtypeStruct((n, R, C), g4.dtype), name=name)(cvec, g4, rsib)


def _chips_start(part, name, order_after=None):
    n_in = 2 + int(order_after is not None)

    def body(*refs):
        p_ref, land_ref = refs[:2]
        send_sems, recv_sems, token = refs[n_in], refs[n_in + 1], refs[n_in + 4]
        x, y, c, _ = _mesh_pos()
        myq = 2 * x + y
        for j, (px, py) in enumerate(_other_chips(x, y)):
            _rcopy(p_ref.at[2 * px + py], land_ref.at[myq], send_sems, recv_sems, j, (px, py, c)).start()
        token[...] = jnp.zeros_like(token)

    land = pltpu.with_memory_space_constraint(lax.empty(part.shape, part.dtype), pltpu.HBM)
    extra = () if order_after is None else (order_after,)
    return _pcall(
        body, name=name,
        out_shape=(pltpu.SemaphoreType.DMA((3,)), pltpu.SemaphoreType.DMA((3,)), pltpu.HBM(part.shape, part.dtype),
                   pltpu.HBM(part.shape, part.dtype), jax.ShapeDtypeStruct((8, 128), F32)),
        in_specs=(_HBM, _HBM) + (_ANY,) * len(extra),
        out_specs=(_SEM, _SEM, _HBM, _HBM, pl.BlockSpec(memory_space=pltpu.VMEM)),
        input_output_aliases={0: 2, 1: 3}, compiler_params=pltpu.CompilerParams(has_side_effects=_EFFECT),
    )(pltpu.with_memory_space_constraint(part, pltpu.HBM), land, *extra)


def _chips_wait(send_sems, recv_sems, p_thru, land_thru, after, name):
    def body(p_ref, land_ref, send_sems, recv_sems, after_ref, p_out, land_out):
        x, y, c, _ = _mesh_pos()
        for j, (px, py) in enumerate(_other_chips(x, y)):
            q = 2 * px + py
            cp = _rcopy(p_ref.at[q], land_ref.at[q], send_sems, recv_sems, j, (px, py, c))
            cp.wait_send()
            cp.wait_recv()

    return _pcall(
        body, name=name, out_shape=(pltpu.HBM(p_thru.shape, p_thru.dtype), pltpu.HBM(p_thru.shape, p_thru.dtype)),
        in_specs=(_HBM, _HBM, _SEM, _SEM, _ANY), out_specs=(_HBM, _HBM), input_output_aliases={0: 0, 1: 1},
        compiler_params=pltpu.CompilerParams(has_side_effects=_EFFECT),
    )(p_thru, land_thru, send_sems, recv_sems, after)


def _sum_chips(part, land, qvec, name):
    n, R, C = part.shape
    tr = _pick(R, _ROW_TILES)

    def body(q_ref, p_ref, *refs):
        o_ref = refs[n]
        acc = None
        for q in range(n):
            term = jnp.where(q_ref[0] == q, p_ref[...], refs[q][...]).astype(F32)
            acc = term if acc is None else acc + term
        o_ref[...] = acc

    def land_spec(q):
        return pl.BlockSpec((None, tr, C), lambda i, m: (jnp.where(m[0] == q, (q + 1) % n, q), i, 0))

    grid_spec = pltpu.PrefetchScalarGridSpec(
        num_scalar_prefetch=1, grid=(R // tr,),
        in_specs=[pl.BlockSpec((None, tr, C), lambda i, m: (m[0], i, 0))] + [land_spec(q) for q in range(n)],
        out_specs=pl.BlockSpec((tr, C), lambda i, m: (i, 0)))
    return _pcall(body, grid_spec=grid_spec, out_shape=jax.ShapeDtypeStruct((R, C), F32), name=name)(qvec, part, *([land] * n))


def _exchange_small(gsmall):
    def body(s_ref, srecv_ref, send_sems, recv_sems, loc_sem):
        x, y, c, me = _mesh_pos()
        loc = pltpu.make_async_copy(s_ref, srecv_ref.at[me], loc_sem.at[0])
        loc.start()
        sends, recvs = [], []
        for k in range(1, N_DEV):
            to, pidx = _peer(x, y, c, k)
            sends.append(_rcopy(s_ref, srecv_ref.at[me], send_sems, recv_sems, k - 1, to))
            recvs.append(_rcopy(s_ref, srecv_ref.at[pidx], send_sems, recv_sems, k - 1, to))
        for cp in sends:
            cp.start()
        for cp in recvs:
            cp.wait_recv()
        for cp in sends:
            cp.wait_send()
        loc.wait()

    return _pcall(
        body, in_specs=[_ANY], out_specs=_ANY, out_shape=jax.ShapeDtypeStruct((N_DEV,) + gsmall.shape, gsmall.dtype),
        scratch_shapes=[pltpu.SemaphoreType.DMA((7,)), pltpu.SemaphoreType.DMA((7,)), pltpu.SemaphoreType.DMA((1,))],
        name="exchange_small",
    )(gsmall)


def _sum_slots(recv, name):
    n, R, C = recv.shape
    tr = _pick(R, _ROW_TILES)

    def body(r_ref, o_ref):
        acc = r_ref[0].astype(F32)
        for s in range(1, n):
            acc = acc + r_ref[s].astype(F32)
        o_ref[...] = acc

    return _pcall(body, grid=(R // tr,), in_specs=[pl.BlockSpec((n, tr, C), lambda i: (0, i, 0))],
                  out_specs=pl.BlockSpec((tr, C), lambda i: (i, 0)), out_shape=jax.ShapeDtypeStruct((R, C), F32), name=name)(recv)


def _adamw(w, g, m, v, name, order_after=None):
    shape = w.shape
    w2, g2, m2, v2 = (a.reshape(-1, shape[-1]) for a in (w, g, m, v))
    R, C = w2.shape
    tr = _pick(R, (256, 128, 64, 32, 16, 8)) if R > 256 else R
    n_in = 4 + int(order_after is not None)

    def body(*refs):
        w_ref, g_ref, m_ref, v_ref = refs[:4]
        d_ref, nm_ref, nv_ref = refs[n_in:]
        gg = g_ref[...]
        nm = ADAM_B1 * m_ref[...] + (1.0 - ADAM_B1) * gg
        nv = ADAM_B2 * v_ref[...] + (1.0 - ADAM_B2) * jnp.square(gg)
        m_hat = nm / (1.0 - ADAM_B1 ** ADAM_STEP)
        v_hat = nv / (1.0 - ADAM_B2 ** ADAM_STEP)
        d_ref[...] = -ADAM_LR * (m_hat / (jnp.sqrt(v_hat) + ADAM_EPS) + ADAM_WD * w_ref[...])
        nm_ref[...] = nm
        nv_ref[...] = nv

    spec = pl.BlockSpec((tr, C), lambda i: (i, 0))
    in_specs, args = [spec] * 4, [w2, g2, m2, v2]
    if order_after is not None:
        in_specs.append(pl.BlockSpec(memory_space=pl.ANY))
        args.append(order_after)
    d, nm, nv = _pcall(body, grid=(R // tr,), in_specs=in_specs, out_specs=[spec] * 3,
                       out_shape=[jax.ShapeDtypeStruct((R, C), F32)] * 3, name=name)(*args)
    return d.reshape(shape), nm.reshape(shape), nv.reshape(shape)


def _to_rows(name, w):
    w = w[0]
    if name in ("ffn1_w_in", "ffn2_w_in", "w_in"):
        return w.T
    if name in ("w_uq", "w_ukv", "ple_w_proj"):
        return w.T.reshape(-1, 1024)
    return w


def _from_rows(name, g, shape):
    if name in ("ffn1_w_in", "ffn2_w_in", "w_in"):
        return g.T.reshape(shape)
    if name in ("w_uq", "w_ukv", "ple_w_proj"):
        return g.reshape(-1, shape[1]).T.reshape(shape)
    return g.reshape(shape)


def _unpack(wall, group):
    out, off = {}, 0
    for name, r in group:
        out[name] = wall[:, off:off + r, :].reshape(N_DEV * r, 1024)
        off += _pad16(r)
    return out


def _w_in_internal(wt):
    return jnp.concatenate([wt[0:6144], wt[6720:8768], wt[6144:6720], jnp.zeros((P_W - W_IN_COLS, 1024), wt.dtype)], axis=0)


def _rope_tables(positions):
    pos = positions[0].astype(F32)

    def cs(half):
        inv = jnp.tile(ROPE_BASE ** (-jnp.arange(half, dtype=F32) / half), 128 // half)
        sign = jnp.tile(jnp.concatenate([-jnp.ones((half,), F32), jnp.ones((half,), F32)]), 64 // half)
        ang = pos[:, None] * inv
        return jnp.cos(ang), jnp.sin(ang) * sign

    return cs(64) + cs(32)


def _local_step(x, p, positions, target, W, ln_g, ln_b, gn_g, qg, kvg, early=None, rest_weights=None, start_token=None):
    T = x.shape[0]
    tabs = _rope_tables(positions)
    rc = _ret_consts()
    lg = [ln_g[i:i + 1] for i in range(4)]
    lb = [ln_b[i:i + 1] for i in range(4)]
    pb = p.astype(BF16)

    hs1, gu1, *xb = _ffn_in(x, W["ffn1_w_in"], "ffn1_in", order_after=start_token)
    xb = xb[0] if xb else x
    f1, h1, h1b = _mm(hs1, W["ffn1_w_out"], name="ffn1_out", tm=LN_TM, epilogue=_ln_epilogue(x, 0.5, lg[0], lb[0]))
    if rest_weights is not None:
        W = {**W, **rest_weights(h1b)}
    w_in_t = _w_in_internal(W["w_in"])
    wuq = W["w_uq"].reshape(1536, LORA).reshape(HEADS, 192, LORA)
    wuq = jnp.concatenate([wuq[:, :128].reshape(1024, LORA), wuq[:, 128:].reshape(512, LORA)], axis=0)
    wukv = W["w_ukv"].reshape(2048, LORA).reshape(HEADS, 2, 128, LORA).transpose(1, 0, 2, 3).reshape(2048, LORA)
    wp_t = W["ple_w_proj"].reshape(1024, D_PLE)
    proj = _mm(h1b, w_in_t, tb=True, out_dtype=BF16, name="mixer_in")
    rq, rk, qn, kvn, kpe = _mixer_prep_fwd(proj, tabs, qg, kvg, T)
    y, yr, states = _ret_fwd(rq, rk, proj, proj, gn_g, rc, T)
    y_ret = _mm(yr, W["w_ret_o"], out_dtype=BF16, name="ret_o")
    qnope, qpe = _mm(qn, wuq, tb=True, name="mla_uq", tm=LN_TM, tn=1536, epilogue=_q_assemble_epilogue(tabs))
    kv = _mm(kvn, wukv, tb=True, out_dtype=BF16, name="mla_ukv")
    o, lse = _attn_fwd(qnope, qpe, kv, kpe, T)
    y_mla, mix = _mm(o, W["w_mla_o"], name="mla_o", tm=LN_TM, epilogue=_mix_epilogue(proj, y_ret))
    mixed, h2, h2b = _mm(mix, W["w_out"], name="mixer_out", tm=LN_TM, epilogue=_ln_epilogue(h1, 1.0, lg[1], lb[1]))
    hs2, gu2 = _ffn_in(h2b, W["ffn2_w_in"], "ffn2_in")
    f2, h3, h3b = _mm(hs2, W["ffn2_w_out"], name="ffn2_out", tm=LN_TM, epilogue=_ln_epilogue(h2, 0.5, lg[2], lb[2]))
    pp = _mm(pb, wp_t, tb=True, name="ple_proj")

    G = {}
    dh3_a, dgl, dpp, dg3, db3, loss = _mm(h3b, W["ple_w_gate"], name="ple_gate", tm=LN_TM,
                                          epilogue=_head_epilogue(h3, pp, target, lg[3], lb[3]))
    G["ple_w_gate"] = _mm(h3b, dgl, ta=True, name="d_ple_gate")
    G["ple_w_proj"] = _mm(dpp, pb, ta=True, name="d_ple_proj")
    dh2_a, df2, dg2, db2 = _mm(dgl, W["ple_w_gate"], tb=True, add=dh3_a, name="dh3", tm=LN_TM,
                               epilogue=_ln_bwd_epilogue(h2, f2, 0.5, lg[2], lb[2]))
    G["ffn2_w_out"] = _mm(hs2, df2, ta=True, name="d_ffn2_out")
    da2 = _ffn_act_bwd(df2, W["ffn2_w_out"], gu2, "ffn2_act_bwd")
    G["ffn2_w_in"] = _mm(da2, h2b, ta=True, name="d_ffn2_in")
    dh1_a, dmixed, dg1, db1 = _mm(da2, W["ffn2_w_in"], add=dh2_a, name="dh2", tm=LN_TM,
                                  epilogue=_ln_bwd_epilogue(h1, mixed, 1.0, lg[1], lb[1]))
    G["w_out"] = _mm(mix, dmixed, ta=True, name="d_mixer_out")
    dp_gate, dy_ret, dy_mla = _mm(dmixed, W["w_out"], tb=True, name="dmix", tm=LN_TM,
                                  epilogue=_mix_bwd_epilogue(proj, y_ret, y_mla))
    G["w_mla_o"] = _mm(o, dy_mla, ta=True, name="d_mla_o")
    dob, delta = _mm(dy_mla, W["w_mla_o"], tb=True, name="do", tm=LN_TM, epilogue=_delta_epilogue(o))
    dqn_f, dqpe_f, dkn, dkpe_all, dv = _attn_bwd(qnope, qpe, kv, kpe, dob, lse, delta, T)
    dq_n, dq_r = _q_assemble_bwd(dqn_f, dqpe_f, tabs, T)
    g_uq = jnp.concatenate([_mm(dq_n, qn, ta=True, name="d_uq_nope"), _mm(dq_r, qn, ta=True, name="d_uq_rope")], axis=0)
    g_uq = jnp.concatenate([g_uq[:1024].reshape(HEADS, 128, LORA), g_uq[1024:].reshape(HEADS, 64, LORA)], axis=1)
    G["w_uq"] = g_uq.reshape(1536 * LORA // 1024, 1024)
    dqn = _mm(dq_r, wuq[1024:], add=_mm(dq_n, wuq[:1024], name="dqn_a"), name="dqn_b")
    g_ukv = jnp.stack([_mm(dkn, kvn, ta=True, name="d_ukv_k"), _mm(dv, kvn, ta=True, name="d_ukv_v")], axis=0)
    G["w_ukv"] = g_ukv.reshape(2, HEADS, 128, LORA).transpose(1, 0, 2, 3).reshape(2048 * LORA // 1024, 1024)
    dkvn = _mm(dv, wukv[1024:], add=_mm(dkn, wukv[:1024], name="dkvn_a"), name="dkvn_b")
    dp_mla, dqg, dkvg = _rms_bwd(proj, dqn, dkvn, dkpe_all, tabs, qg, kvg, T)
    G["w_ret_o"] = _mm(yr, dy_ret, ta=True, name="d_ret_o")
    dyr = _mm(dy_ret, W["w_ret_o"], tb=True, out_dtype=BF16, name="dyr")
    dp_ret, dgn = _ret_bwd(rq, rk, proj, y, proj, gn_g, states, dyr, rc, tabs, T)
    g_ret, g_gate, g_mla = (_mm(dp, h1b, ta=True, name="d_mixer_in_" + n)
                            for dp, n in ((dp_ret, "ret"), (dp_gate, "gate"), (dp_mla, "mla")))
    G["w_in"] = jnp.concatenate([g_ret, g_mla[:W_IN_COLS - P_CQ], g_gate], axis=0)
    lg0 = lg[0] if early is None else lg[0] + early(G)[0:1, 0:1]
    dh1 = _mm_pieces([(dp_ret, w_in_t[:P_GR]), (dp_gate, w_in_t[P_GR:P_CQ]), (dp_mla, w_in_t[P_CQ:])], dh1_a, "dh1")
    dx_a, df1, dg0, db0 = _ln_bwd_rows(x, f1, 0.5, lg0, lb[0], dh1, T, "ln0_bwd")
    G["ffn1_w_out"] = _mm(hs1, df1, ta=True, name="d_ffn1_out")
    da1 = _ffn_act_bwd(df1, W["ffn1_w_out"], gu1, "ffn1_act_bwd")
    G["ffn1_w_in"] = _mm(da1, xb, ta=True, name="d_ffn1_in")
    grad_x = _mm(da1, W["ffn1_w_in"], add=dx_a, name="grad_x")

    small = dict(ln_g=jnp.concatenate([dg0, dg1, dg2, dg3], axis=0), ln_b=jnp.concatenate([db0, db1, db2, db3], axis=0),
                 ret_gn_g=dgn, q_norm_g=dqg, kv_norm_g=dkvg)
    return loss, grad_x, G, small


def kernel(x, p, positions, ln_g, ln_b, ffn1_w_in, ffn1_w_out, w_in, ret_gn_g, w_ret_o, q_norm_g, kv_norm_g, w_uq, w_ukv, w_mla_o, w_out, ffn2_w_in, ffn2_w_out, ple_w_gate, ple_w_proj, loss_target, m_ln_g, m_ln_b, m_ffn1_w_in, m_ffn1_w_out, m_w_in, m_ret_gn_g, m_w_ret_o, m_q_norm_g, m_kv_norm_g, m_w_uq, m_w_ukv, m_w_mla_o, m_w_out, m_ffn2_w_in, m_ffn2_w_out, m_ple_w_gate, m_ple_w_proj, v_ln_g, v_ln_b, v_ffn1_w_in, v_ffn1_w_out, v_w_in, v_ret_gn_g, v_w_ret_o, v_q_norm_g, v_kv_norm_g, v_w_uq, v_w_ukv, v_w_mla_o, v_w_out, v_ffn2_w_in, v_ffn2_w_out, v_ple_w_gate, v_ple_w_proj):
    names = ("ln_g", "ln_b", "ffn1_w_in", "ffn1_w_out", "w_in", "ret_gn_g", "w_ret_o", "q_norm_g", "kv_norm_g", "w_uq", "w_ukv",
             "w_mla_o", "w_out", "ffn2_w_in", "ffn2_w_out", "ple_w_gate", "ple_w_proj")
    ws = dict(zip(names, (ln_g, ln_b, ffn1_w_in, ffn1_w_out, w_in, ret_gn_g, w_ret_o, q_norm_g, kv_norm_g, w_uq, w_ukv, w_mla_o,
                          w_out, ffn2_w_in, ffn2_w_out, ple_w_gate, ple_w_proj)))
    ms = dict(zip(names, (m_ln_g, m_ln_b, m_ffn1_w_in, m_ffn1_w_out, m_w_in, m_ret_gn_g, m_w_ret_o, m_q_norm_g, m_kv_norm_g, m_w_uq,
                          m_w_ukv, m_w_mla_o, m_w_out, m_ffn2_w_in, m_ffn2_w_out, m_ple_w_gate, m_ple_w_proj)))
    vs = dict(zip(names, (v_ln_g, v_ln_b, v_ffn1_w_in, v_ffn1_w_out, v_w_in, v_ret_gn_g, v_w_ret_o, v_q_norm_g, v_kv_norm_g, v_w_uq,
                          v_w_ukv, v_w_mla_o, v_w_out, v_ffn2_w_in, v_ffn2_w_out, v_ple_w_gate, v_ple_w_proj)))

    parts = []
    for name, r in PACK:
        rows = _to_rows(name, ws[name])
        if _pad16(r) != r:
            rows = jnp.concatenate([rows, jnp.zeros((_pad16(r) - r, 1024), F32)], axis=0)
        parts.append(rows)
    wsh_first = jnp.concatenate(parts[:len(PACK_LATE)], axis=0).astype(BF16)
    wsh_rest = jnp.concatenate(parts[len(PACK_LATE):], axis=0).astype(BF16)
    ssh = jnp.concatenate([ln_g[0], ln_b[0]], axis=0)
    wall_first, sall = _all_gather(wsh_first, ssh)
    *gather_handles, start_token = _gather_start(wsh_rest, wall_first)
    W = _unpack(wall_first, PACK_LATE)
    ln_full = sall.reshape(N_DEV, 2, 4, 128).transpose(1, 2, 0, 3).reshape(2, 4, 1024)

    def rest_weights(after):
        w_thru, land = _gather_wait(*gather_handles, after)
        return _unpack(_gather_finish(w_thru, land), PACK_EARLY)

    cvec = lax.axis_index("c").astype(jnp.int32).reshape(1)
    qvec = (2 * lax.axis_index("x") + lax.axis_index("y")).astype(jnp.int32).reshape(1)

    def chip_partials(G, group, tag):
        gparts = []
        for name, r in group:
            g = G[name].reshape(N_DEV, r, 1024)
            if _pad16(r) != r:
                g = jnp.concatenate([g, jnp.zeros((N_DEV, _pad16(r) - r, 1024), g.dtype)], axis=1)
            gparts.append(g)
        gfull = jnp.concatenate(gparts, axis=1).astype(BF16)
        g4 = gfull.reshape(4, 2, gfull.shape[1], 1024)
        return _sum_sibling(g4, _exchange_sibling(g4, "exchange_sibling_" + tag), cvec, "sum_sibling_" + tag)

    in_flight = []

    def early(G):
        *handles, token = _chips_start(chip_partials(G, PACK_EARLY, "early"), "chips_start_early")
        in_flight.append(handles)
        return token

    loss_p, grad_x, G, small = _local_step(x[0], p[0, 0], positions, loss_target[0], W, ln_full[0], ln_full[1],
                                           ret_gn_g, q_norm_g, kv_norm_g, early=early, rest_weights=rest_weights,
                                           start_token=start_token)

    part_e, land_e = _chips_wait(*in_flight[0], grad_x, "chips_wait_early")
    gsh_early = _sum_chips(part_e, land_e, qvec, "sum_grads_early")
    pad256 = lambda a: jnp.concatenate([a, jnp.zeros((1, 1024 - a.shape[1]), F32)], axis=1)
    gsmall = jnp.concatenate([small["ln_g"], small["ln_b"], small["ret_gn_g"].reshape(2, 1024), pad256(small["q_norm_g"]),
                              pad256(small["kv_norm_g"]), pad256(loss_p[0:1, 0:128]),
                              jnp.zeros((SMALL_ROWS - 13, 1024), F32)], axis=0)
    srecv = _exchange_small(gsmall)
    part_l = chip_partials(G, PACK_LATE, "late")
    *late_handles, late_token = _chips_start(part_l, "chips_start_late", order_after=srecv)
    ssum = _sum_slots(srecv, "sum_small_grads")

    def unpack_grads(group, gsh):
        out, off = {}, 0
        for name, r in group:
            out[name] = _from_rows(name, gsh[off:off + r], ws[name].shape)
            off += _pad16(r)
        return out

    grads = unpack_grads(PACK_EARLY, gsh_early)
    me = 4 * lax.axis_index("x") + 2 * lax.axis_index("y") + lax.axis_index("c")
    grads["ln_g"] = lax.dynamic_slice(ssum[0:4], (0, me * 128), (4, 128)).reshape(1, 4, 128)
    grads["ln_b"] = lax.dynamic_slice(ssum[4:8], (0, me * 128), (4, 128)).reshape(1, 4, 128)
    grads["ret_gn_g"] = ssum[8:10].reshape(1, 2048)
    grads["q_norm_g"] = ssum[10:11, :256]
    grads["kv_norm_g"] = ssum[11:12, :256]

    delta, new_m, new_v = {}, {}, {}
    late_names = [n for n, _ in PACK_LATE]
    last = late_token
    for name in names:
        if name not in late_names:
            delta[name], new_m[name], new_v[name] = _adamw(ws[name], grads[name], ms[name], vs[name], "adamw_" + name,
                                                           order_after=last)
            last = new_v[name]
    part_l, land_l = _chips_wait(*late_handles, last, "chips_wait_late")
    grads.update(unpack_grads(PACK_LATE, _sum_chips(part_l, land_l, qvec, "sum_grads_late")))
    for name in late_names:
        delta[name], new_m[name], new_v[name] = _adamw(ws[name], grads[name], ms[name], vs[name], "adamw_" + name)

    return (ssum[12, 0], grad_x[None], *[grads[n] for n in names], *[delta[n] for n in names],
            *[new_m[n] for n in names], *[new_v[n] for n in names])
```

```python
import math

import jax
import jax.numpy as jnp
from jax import lax
from jax.experimental import pallas as pl
from jax.experimental.pallas import tpu as pltpu

F32 = jnp.float32
BF16 = jnp.bfloat16

N_DEV = 8
D = 1024
D_FF = 2816
D_PLE = 256
CHUNK = 64
HEADS = 8
RET_DK = 128
RET_DV = 256
MLA_NOPE = 128
MLA_ROPE = 64
MLA_DV = 128
LORA = 256
ROPE_BASE = 10000.0
EPS = 1e-5
ALPHA = 2.0 ** 0.25
RET_SCALE = RET_DK ** -0.5
MLA_SCALE = (MLA_NOPE + MLA_ROPE) ** -0.5
NEG = -1e30

ADAM_LR = 0.001
ADAM_B1 = 0.9
ADAM_B2 = 0.999
ADAM_EPS = 1e-08
ADAM_WD = 0.01
ADAM_STEP = 10

P_RQ, P_RK, P_RV, P_RG, P_GR, P_GM, P_CQ, P_CKV, P_KPE, P_W = 0, 1024, 2048, 4096, 6144, 7168, 8192, 8448, 8704, 8960
W_IN_COLS = 8768
RET_L = 256
ATT_TF = 2048
ATT_TB = 1024
ATT_HP = 1
LOG2E = math.log2(math.e)
Q_PRESCALE = MLA_SCALE * LOG2E

PACK = (("ffn1_w_in", 704), ("ffn1_w_out", 352), ("w_in", 1096), ("w_ret_o", 256), ("w_uq", 48), ("w_ukv", 64),
        ("w_mla_o", 128), ("w_out", 128), ("ffn2_w_in", 704), ("ffn2_w_out", 352), ("ple_w_gate", 128), ("ple_w_proj", 32))


def _pad16(r):
    return -(-r // 16) * 16


PACK_LATE = PACK[:2]
PACK_EARLY = PACK[2:]
SMALL_ROWS = 16


def _pcall(body, **kw):
    return pl.pallas_call(body, **kw)


def _pick(dim, prefs):
    for p in prefs:
        if dim % p == 0:
            return p
    return dim


def _sigmoid(x):
    return 1.0 / (1.0 + jnp.exp(-x))


def _silu(x):
    return x * _sigmoid(x)


def _ln(r, g, b):
    mu = jnp.mean(r, axis=-1, keepdims=True)
    var = jnp.mean(jnp.square(r - mu), axis=-1, keepdims=True)
    return (r - mu) * lax.rsqrt(var + EPS) * g + b


def _rms(x, g):
    return x * lax.rsqrt(jnp.mean(jnp.square(x), axis=-1, keepdims=True) + EPS) * g


def _dot(a, b, ca, cb):
    return lax.dot_general(a, b, (((ca,), (cb,)), ((), ())), preferred_element_type=F32)


def _accum(ref, val, first=None):
    @pl.when(pl.program_id(0) == 0 if first is None else first)
    def _():
        ref[...] = jnp.zeros_like(ref)

    ref[...] += val


def _mm(a, b, *, ta=False, tb=False, add=None, out_dtype=None, name, tm=None, tn=None, tk=None, epilogue=None):
    parts = a.shape[0] if a.ndim == 3 else 1
    ar, ac = a.shape[-2], a.shape[-1]
    out_dtype = out_dtype or (BF16 if ta else F32)
    if ta:
        K, M = ar, ac * parts
    else:
        M, K = ar, ac * parts
    if tb:
        N, K2 = b.shape
    else:
        K2, N = b.shape
    assert K == K2, (a.shape, b.shape, ta, tb)
    big = (1024, 1408, 1280, 768, 512, 256, 128)
    tm = tm or _pick(ac if (ta and parts > 1) else M, big)
    tn = tn or (N if N <= 1024 else _pick(N, big))
    kdim = ac if (not ta and parts > 1) else K
    tk = tk or (kdim if kdim <= 2816 else
                _pick(kdim, (2048, 1408, 1280, 1024, 512) if tm <= 1024 else (1024, 1408, 1280, 512)))
    nk = K // tk
    grid = (M // tm, N // tn, nk)
    if parts > 1 and ta:
        per = ac // tm
        a_spec = pl.BlockSpec((None, tk, tm), lambda i, j, k: (i // per, k, i % per))
    elif parts > 1:
        per = ac // tk
        a_spec = pl.BlockSpec((None, tm, tk), lambda i, j, k: (k // per, i, k % per))
    else:
        a_spec = pl.BlockSpec((tk, tm), lambda i, j, k: (k, i)) if ta else pl.BlockSpec((tm, tk), lambda i, j, k: (i, k))
    b_spec = pl.BlockSpec((tn, tk), lambda i, j, k: (j, k)) if tb else pl.BlockSpec((tk, tn), lambda i, j, k: (k, j))
    o_spec = pl.BlockSpec((tm, tn), lambda i, j, k: (i, j))
    ca, cb = (0 if ta else 1), (1 if tb else 0)
    has_add = add is not None
    n_in = 2 + int(has_add)
    if epilogue is not None:
        assert tn == N and not ta
        ep_fn, ep_rows, ep_whole, ep_outs, ep_accs = epilogue
        n_ep_in = len(ep_rows) + len(ep_whole)
        n_out = len(ep_outs) + len(ep_accs)
    else:
        n_ep_in, n_out = 0, 1

    def body(*refs):
        a_ref, b_ref = refs[0], refs[1]
        add_ref = refs[2] if has_add else None
        o_ref = refs[n_in + n_ep_in]
        first_row_tile = pl.program_id(0) == 0

        def finish(r):
            if has_add:
                r = r + add_ref[...].astype(F32)
            if epilogue is not None:
                ep_fn(r, refs[n_in:n_in + n_ep_in], refs[n_in + n_ep_in:n_in + n_ep_in + n_out], first_row_tile)
            else:
                o_ref[...] = r.astype(out_dtype)

        if nk == 1:
            finish(_dot(a_ref[...], b_ref[...], ca, cb))
            return
        acc_ref = refs[-1]
        k = pl.program_id(2)

        @pl.when(k == 0)
        def _():
            acc_ref[...] = jnp.zeros_like(acc_ref)

        acc_ref[...] += _dot(a_ref[...], b_ref[...], ca, cb)

        @pl.when(k == nk - 1)
        def _():
            finish(acc_ref[...])

    in_specs = [a_spec, b_spec] + ([o_spec] if has_add else [])
    args = (a, b) + ((add,) if has_add else ())
    out_specs, out_shape = o_spec, jax.ShapeDtypeStruct((M, N), out_dtype)
    if epilogue is not None:
        for row_in in ep_rows:
            arr, col_block, width = (tuple(row_in) + (N,))[:3] if isinstance(row_in, tuple) else (row_in, 0, N)
            in_specs.append(pl.BlockSpec((tm, width), lambda i, j, k, _c=col_block: (i, _c)))
            args += (arr,)
        in_specs += [pl.BlockSpec(w.shape, lambda i, j, k, _n=w.ndim: (0,) * _n) for w in ep_whole]
        args += tuple(ep_whole)
        outs = [o if isinstance(o, tuple) else (o, N) for o in ep_outs]
        out_specs = ([pl.BlockSpec((tm, w), lambda i, j, k: (i, 0)) for _, w in outs]
                     + [pl.BlockSpec((r, N), lambda i, j, k: (0, 0)) for r in ep_accs])
        out_shape = [jax.ShapeDtypeStruct((M, w), dt) for dt, w in outs] + [jax.ShapeDtypeStruct((r, N), F32) for r in ep_accs]
    return _pcall(
        body, grid=grid, in_specs=in_specs, out_specs=out_specs, out_shape=out_shape,
        scratch_shapes=[pltpu.VMEM((tm, tn), F32)] if nk > 1 else [], name=name,
        compiler_params=pltpu.CompilerParams(dimension_semantics=("arbitrary" if epilogue is not None else "parallel", "parallel", "arbitrary")),
    )(*args)


def _mm_pieces(pieces, add, name):
    M, N = add.shape
    tm = _pick(M, (1024, 512, 256, 128))
    spans, in_specs, args, start = [], [], [], 0
    for a, b in pieces:
        K = a.shape[1]
        assert b.shape == (K, N) and a.shape[0] == M, (a.shape, b.shape)
        tk = _pick(K, (1536, 1024, 768, 512, 256, 128))
        n = K // tk
        spans.append((start, n))
        block = lambda k, _s=start, _n=n: jnp.where(k < _s, _n - 1, jnp.minimum(k - _s, _n - 1))
        in_specs.append(pl.BlockSpec((tm, tk), lambda i, k, _s=start, _blk=block: (jnp.where(k < _s, jnp.maximum(i - 1, 0), i), _blk(k))))
        in_specs.append(pl.BlockSpec((tk, N), lambda i, k, _blk=block: (_blk(k), 0)))
        args += [a, b]
        start += n
    o_spec = pl.BlockSpec((tm, N), lambda i, k: (i, 0))

    def body(*refs):
        add_ref, o_ref = refs[-2], refs[-1]
        k = pl.program_id(1)

        @pl.when(k == 0)
        def _():
            o_ref[...] = add_ref[...]

        for p, (s, n) in enumerate(spans):
            @pl.when((k >= s) & (k < s + n))
            def _(p=p):
                o_ref[...] += _dot(refs[2 * p][...], refs[2 * p + 1][...], 1, 0)

    return _pcall(
        body, grid=(M // tm, start), in_specs=in_specs + [o_spec], out_specs=o_spec, out_shape=jax.ShapeDtypeStruct((M, N), F32),
        name=name, compiler_params=pltpu.CompilerParams(dimension_semantics=("parallel", "arbitrary")),
    )(*args, add)


def _ln_epilogue(res, c, g, b):
    def fn(r, ins, outs, first):
        res_ref, g_ref, b_ref = ins
        f_ref, h_ref, hb_ref = outs
        h = _ln(ALPHA * res_ref[...] + c * r, g_ref[...], b_ref[...])
        f_ref[...] = r
        h_ref[...] = h
        hb_ref[...] = h.astype(BF16)

    return (fn, [res], [g, b], (F32, F32, BF16), ())


def _ln_bwd_epilogue(res, f, c, g, b):
    def fn(r, ins, outs, first):
        res_ref, f_ref, g_ref, b_ref = ins
        dr_ref, df_ref, dg_ref, db_ref = outs
        pre = ALPHA * res_ref[...] + c * f_ref[...]
        xc = pre - jnp.mean(pre, axis=-1, keepdims=True)
        rstd = lax.rsqrt(jnp.mean(jnp.square(xc), axis=-1, keepdims=True) + EPS)
        xhat = xc * rstd
        dyg = r * g_ref[...]
        dpre = rstd * (dyg - jnp.mean(dyg, axis=-1, keepdims=True) - xhat * jnp.mean(dyg * xhat, axis=-1, keepdims=True))
        dr_ref[...] = ALPHA * dpre
        df_ref[...] = (c * dpre).astype(BF16)
        _accum(dg_ref, jnp.sum(r * xhat, axis=0, keepdims=True), first)
        _accum(db_ref, jnp.sum(r, axis=0, keepdims=True), first)

    return (fn, [res, f], [g, b], (F32, BF16), (1, 1))


ROW_TM = 1024


def _rows(body, T, tm, ins, outs, name, accs=()):
    in_specs, args = [], []
    for arr, w, cb in ins:
        if w is None:
            in_specs.append(pl.BlockSpec(arr.shape, lambda i, _n=arr.ndim: (0,) * _n))
        else:
            in_specs.append(pl.BlockSpec((tm, w), lambda i, _cb=cb: (i, _cb)))
        args.append(arr)
    out_specs = [pl.BlockSpec((tm, w), lambda i: (i, 0)) for w, _ in outs]
    out_shape = [jax.ShapeDtypeStruct((T, w), dt) for w, dt in outs]
    for r, w in accs:
        out_specs.append(pl.BlockSpec((r, w), lambda i: (0, 0)))
        out_shape.append(jax.ShapeDtypeStruct((r, w), F32))
    return _pcall(
        body, grid=(T // tm,), in_specs=in_specs, out_specs=out_specs, out_shape=out_shape, name=name,
        compiler_params=pltpu.CompilerParams(dimension_semantics=("arbitrary",)),
    )(*args)


def _ln_bwd_rows(res, f, c, g, b, dh, T, name):
    fn = _ln_bwd_epilogue(res, f, c, g, b)[0]

    def body(r_ref, f_ref, g_ref, b_ref, d_ref, dr_ref, df_ref, dg_ref, db_ref):
        fn(d_ref[...], (r_ref, f_ref, g_ref, b_ref), (dr_ref, df_ref, dg_ref, db_ref), pl.program_id(0) == 0)

    return _rows(body, T, min(ROW_TM, T), [(res, D, 0), (f, D, 0), (g, None, None), (b, None, None), (dh, D, 0)],
                 [(D, F32), (D, BF16)], name, accs=[(1, D), (1, D)])


FFN_TN = 1408
FFN_TM = 512
LN_TM = 512


def _ffn_in(x, wt, name, order_after=None):
    T = x.shape[0]
    tm, tn = min(FFN_TM, T), FFN_TN
    nj = D_FF // tn
    emit_xb = x.dtype != BF16
    n_in = 3 + int(order_after is not None)

    def body(*refs):
        x_ref, wg_ref, wu_ref = refs[:3]
        hs_ref, gu_ref = refs[n_in], refs[n_in + 1]
        xv = x_ref[...].astype(BF16)
        g = _dot(xv, wg_ref[...], 1, 1)
        u = _dot(xv, wu_ref[...], 1, 1)
        hs_ref[...] = (_silu(g) * u).astype(BF16)
        gu_ref[0] = g.astype(BF16)
        gu_ref[1] = u.astype(BF16)
        if emit_xb:
            refs[n_in + 2][...] = xv

    in_specs = [pl.BlockSpec((tm, D), lambda i, j: (i, 0)), pl.BlockSpec((tn, D), lambda i, j: (j, 0)),
                pl.BlockSpec((tn, D), lambda i, j: (j + nj, 0))]
    args = [x, wt, wt]
    if order_after is not None:
        in_specs.append(pl.BlockSpec(order_after.shape, lambda i, j: (0, 0)))
        args.append(order_after)
    out_specs = [pl.BlockSpec((tm, tn), lambda i, j: (i, j)), pl.BlockSpec((2, tm, tn), lambda i, j: (0, i, j))]
    out_shape = [jax.ShapeDtypeStruct((T, D_FF), BF16), jax.ShapeDtypeStruct((2, T, D_FF), BF16)]
    if emit_xb:
        out_specs.append(pl.BlockSpec((tm, D), lambda i, j: (i, 0)))
        out_shape.append(jax.ShapeDtypeStruct((T, D), BF16))
    return _pcall(
        body, grid=(T // tm, nj), in_specs=in_specs, out_specs=out_specs, out_shape=out_shape, name=name,
        compiler_params=pltpu.CompilerParams(dimension_semantics=("parallel", "arbitrary")),
    )(*args)


def _ffn_act_bwd(df, wo, gu, name):
    T = df.shape[0]
    tm, tn = min(FFN_TM, T), FFN_TN

    def body(d_ref, w_ref, gu_ref, o_ref):
        dhs = _dot(d_ref[...], w_ref[...], 1, 1)
        g, u = gu_ref[0].astype(F32), gu_ref[1].astype(F32)
        sig = _sigmoid(g)
        act = g * sig
        o_ref[0] = (dhs * u * (sig + act * (1.0 - sig))).astype(BF16)
        o_ref[1] = (dhs * act).astype(BF16)

    return _pcall(
        body, grid=(T // tm, D_FF // tn),
        in_specs=[pl.BlockSpec((tm, D), lambda i, j: (i, 0)), pl.BlockSpec((tn, D), lambda i, j: (j, 0)),
                  pl.BlockSpec((2, tm, tn), lambda i, j: (0, i, j))],
        out_specs=pl.BlockSpec((2, tm, tn), lambda i, j: (0, i, j)),
        out_shape=jax.ShapeDtypeStruct((2, T, D_FF), BF16), name=name,
        compiler_params=pltpu.CompilerParams(dimension_semantics=("parallel", "parallel")),
    )(df, wo, gu)


def _rope128(t, cos, sin_s):
    return t * cos + pltpu.roll(t, 64, 1) * sin_s


def _rope128_t(g, cos, sin_s):
    return g * cos - pltpu.roll(g, 64, 1) * sin_s


def _partner32(t):
    lane = lax.broadcasted_iota(jnp.int32, t.shape, 1)
    return jnp.where((lane & 32) == 0, pltpu.roll(t, 96, 1), pltpu.roll(t, 32, 1))


def _rope64(t, cos, sin_s):
    return t * cos + _partner32(t) * sin_s


def _rope64_t(g, cos, sin_s):
    return g * cos - _partner32(g) * sin_s


def _mixer_prep_fwd(proj, tabs, qg, kvg, T):
    cos128, sin128, cos64, sin64 = tabs

    def body(rq_ref, rk_ref, cq_ref, ckv_ref, kpe_ref, c1_ref, s1_ref, c2_ref, s2_ref, qg_ref, kvg_ref,
             oq_ref, ok_ref, oqn_ref, okvn_ref, okpe_ref):
        c1, s1 = c1_ref[...], s1_ref[...]
        for h in range(HEADS):
            sl = slice(h * RET_DK, (h + 1) * RET_DK)
            oq_ref[:, sl] = _rope128(rq_ref[:, sl].astype(F32), c1, s1).astype(BF16)
            ok_ref[:, sl] = (_rope128(rk_ref[:, sl].astype(F32), c1, s1) * RET_SCALE).astype(BF16)
        oqn_ref[...] = _rms(cq_ref[...].astype(F32), qg_ref[...]).astype(BF16)
        okvn_ref[...] = _rms(ckv_ref[...].astype(F32), kvg_ref[...]).astype(BF16)
        okpe_ref[...] = _rope64(kpe_ref[...].astype(F32), c2_ref[...], s2_ref[...]).astype(BF16)

    ins = [(proj, 1024, 0), (proj, 1024, 1), (proj, 256, P_CQ // 256), (proj, 256, P_CKV // 256),
           (proj, 128, P_KPE // 128), (cos128, 128, 0), (sin128, 128, 0), (cos64, 128, 0), (sin64, 128, 0),
           (qg, None, None), (kvg, None, None)]
    outs = [(1024, BF16), (1024, BF16), (LORA, BF16), (LORA, BF16), (128, BF16)]
    return _rows(body, T, min(ROW_TM, T), ins, outs, "mixer_prep_fwd")


def _rms_bwd(proj, dqn, dkvn, dkpe_all, tabs, qg, kvg, T):
    _, _, cos64, sin64 = tabs

    def body(cq_ref, ckv_ref, dq_ref, dkv_ref, dk_ref, c2_ref, s2_ref, qg_ref, kvg_ref, o_ref, dqg_ref, dkvg_ref):
        _, vjp = jax.vjp(_rms, cq_ref[...].astype(F32), qg_ref[...])
        dx, dg = vjp(dq_ref[...])
        o_ref[:, 0:LORA] = dx.astype(BF16)
        _accum(dqg_ref, dg)
        _, vjp = jax.vjp(_rms, ckv_ref[...].astype(F32), kvg_ref[...])
        dx, dg = vjp(dkv_ref[...])
        o_ref[:, P_CKV - P_CQ:P_CKV - P_CQ + LORA] = dx.astype(BF16)
        _accum(dkvg_ref, dg)
        g = dk_ref[:, 0:128]
        for h in range(1, HEADS):
            g = g + dk_ref[:, h * 128:(h + 1) * 128]
        lane = lax.broadcasted_iota(jnp.int32, g.shape, 1)
        g = jnp.where(lane < MLA_ROPE, g, 0.0)
        o_ref[:, P_KPE - P_CQ:P_KPE - P_CQ + 128] = _rope64_t(g, c2_ref[...], s2_ref[...]).astype(BF16)
        o_ref[:, P_KPE - P_CQ + 128:] = jnp.zeros((g.shape[0], P_W - P_KPE - 128), BF16)

    ins = [(proj, 256, P_CQ // 256), (proj, 256, P_CKV // 256), (dqn, LORA, 0), (dkvn, LORA, 0), (dkpe_all, 1024, 0),
           (cos64, 128, 0), (sin64, 128, 0), (qg, None, None), (kvg, None, None)]
    return _rows(body, T, min(ROW_TM, T), ins, [(P_W - P_CQ, BF16)], "rms_bwd", accs=[(1, LORA), (1, LORA)])


def _gn_gate_bwd(y, rg, g, d):
    xc = y - jnp.mean(y, axis=-1, keepdims=True)
    rstd = lax.rsqrt(jnp.mean(jnp.square(xc), axis=-1, keepdims=True) + EPS)
    xhat = xc * rstd
    sig = _sigmoid(rg)
    act = rg * sig
    dn = d * act
    drg = d * (xhat * g) * (sig + act * (1.0 - sig))
    dxh = dn * g
    dy = rstd * (dxh - jnp.mean(dxh, axis=-1, keepdims=True) - xhat * jnp.mean(dxh * xhat, axis=-1, keepdims=True))
    return dy, drg, jnp.sum(dn * xhat, axis=0, keepdims=True)


def _gn_gate(y, rg, g):
    mu = jnp.mean(y, axis=-1, keepdims=True)
    var = jnp.mean(jnp.square(y - mu), axis=-1, keepdims=True)
    return _silu(rg) * ((y - mu) * lax.rsqrt(var + EPS) * g)


def _q_assemble_epilogue(tabs):
    _, _, cos64, sin64 = tabs

    def fn(q, ins, outs, first):
        c_ref, s_ref = ins
        on_ref, op_ref = outs
        on_ref[...] = (q[:, :1024] * Q_PRESCALE).astype(BF16)
        c, s = c_ref[...], s_ref[...]
        lane = lax.broadcasted_iota(jnp.int32, c.shape, 1)
        for j in range(HEADS // 2):
            r = _rope64(q[:, 1024 + 128 * j:1024 + 128 * (j + 1)], c, s) * Q_PRESCALE
            op_ref[:, 256 * j:256 * j + 128] = jnp.where(lane < 64, r, 0.0).astype(BF16)
            op_ref[:, 256 * j + 128:256 * j + 256] = jnp.where(lane < 64, pltpu.roll(r, 64, 1), 0.0).astype(BF16)

    return (fn, [(cos64, 0, 128), (sin64, 0, 128)], [], ((BF16, 1024), (BF16, 1024)), ())


def _q_assemble_bwd(dqn, dqpe, tabs, T):
    _, _, cos64, sin64 = tabs

    def body(dn_ref, dp_ref, c_ref, s_ref, on_ref, op_ref):
        on_ref[...] = dn_ref[...].astype(BF16)
        c, s = c_ref[...], s_ref[...]
        lane = lax.broadcasted_iota(jnp.int32, c.shape, 1)
        for j in range(HEADS // 2):
            g = jnp.where(lane < 64, dp_ref[:, 256 * j:256 * j + 128], pltpu.roll(dp_ref[:, 256 * j + 128:256 * j + 256], 64, 1))
            op_ref[:, 128 * j:128 * (j + 1)] = _rope64_t(g, c, s).astype(BF16)

    return _rows(body, T, min(ROW_TM, T), [(dqn, 1024, 0), (dqpe, 1024, 0), (cos64, 128, 0), (sin64, 128, 0)],
                 [(1024, BF16), (512, BF16)], "q_assemble_bwd")


def _mix_fn(gr, gm, yr, ym):
    return _sigmoid(gr) * yr + _sigmoid(gm) * ym


def _mix_epilogue(proj, y_ret):
    def fn(r, ins, outs, first):
        gr_ref, gm_ref, yr_ref = ins
        ym_ref, mix_ref = outs
        ym_ref[...] = r.astype(BF16)
        mix_ref[...] = _mix_fn(gr_ref[...].astype(F32), gm_ref[...].astype(F32), yr_ref[...].astype(F32), r).astype(BF16)

    return (fn, [(proj, P_GR // D), (proj, P_GM // D), y_ret], [], (BF16, BF16), ())


def _mix_bwd_epilogue(proj, y_ret, y_mla):
    def fn(r, ins, outs, first):
        gr_ref, gm_ref, yr_ref, ym_ref = ins
        dgate_ref, dyr_ref, dym_ref = outs
        sr, sm = _sigmoid(gr_ref[...].astype(F32)), _sigmoid(gm_ref[...].astype(F32))
        dgate_ref[:, 0:D] = (r * yr_ref[...].astype(F32) * (sr * (1.0 - sr))).astype(BF16)
        dgate_ref[:, D:2 * D] = (r * ym_ref[...].astype(F32) * (sm * (1.0 - sm))).astype(BF16)
        dyr_ref[...] = (r * sr).astype(BF16)
        dym_ref[...] = (r * sm).astype(BF16)

    return (fn, [(proj, P_GR // D), (proj, P_GM // D), y_ret, y_mla], [], ((BF16, 2 * D), BF16, BF16), ())


def _head_epilogue(h3, pp, tgt, g, b):
    def fn(r, ins, outs, first):
        h_ref, pp_ref, t_ref, g_ref, b_ref = ins
        dh_ref, dgl_ref, dpp_ref, dg_ref, db_ref, loss_ref = outs

        sg, pp, gain = _sigmoid(r), pp_ref[...], g_ref[...]
        pre = ALPHA * h_ref[...] + sg * pp
        xc = pre - jnp.mean(pre, axis=-1, keepdims=True)
        rstd = lax.rsqrt(jnp.mean(jnp.square(xc), axis=-1, keepdims=True) + EPS)
        xhat = xc * rstd
        err = xhat * gain + b_ref[...] - t_ref[...]
        dy = err * (1.0 / D)
        dyg = dy * gain
        dpre = rstd * (dyg - jnp.mean(dyg, axis=-1, keepdims=True) - xhat * jnp.mean(dyg * xhat, axis=-1, keepdims=True))
        dh_ref[...] = ALPHA * dpre
        dgl_ref[...] = (dpre * pp * (sg * (1.0 - sg))).astype(BF16)
        dpp_ref[...] = (dpre * sg).astype(BF16)
        _accum(dg_ref, jnp.sum(dy * xhat, axis=0, keepdims=True), first)
        _accum(db_ref, jnp.sum(dy, axis=0, keepdims=True), first)
        part = 0.5 * jnp.sum(jnp.mean(jnp.square(err), axis=-1, keepdims=True), axis=0, keepdims=True)
        _accum(loss_ref, jnp.broadcast_to(part, loss_ref.shape), first)

    return (fn, [h3, pp, tgt], [g, b], (F32, BF16, BF16), (1, 1, 8))


def _delta_epilogue(o):
    def fn(r, ins, outs, first):
        (o_ref,) = ins
        db_ref, dl_ref = outs
        db_ref[...] = r.astype(BF16)
        for h in range(HEADS):
            sl = slice(h * MLA_DV, (h + 1) * MLA_DV)
            dl = jnp.sum(r[:, sl] * o_ref[:, sl].astype(F32), axis=-1, keepdims=True)
            dl_ref[:, sl] = jnp.broadcast_to(dl, (r.shape[0], MLA_DV))

    return (fn, [o], [], (BF16, F32), ())


def _ret_consts():
    L = RET_L
    lg = jnp.log(1.0 - 2.0 ** (-5.0 - jnp.arange(HEADS, dtype=F32)))[:, None, None]
    idx = jnp.arange(L, dtype=F32)
    ch = jnp.arange(L) // CHUNK
    dist = idx[:, None] - idx[None, :]
    same = (ch[:, None] == ch[None, :])[None]
    earlier = (ch[None, :] < ch[:, None])[None]
    dm = jnp.where(same, jnp.exp(lg * jnp.abs(dist)[None]), jnp.where(earlier, jnp.exp(lg * dist[None]), 0.0))
    xi = jnp.broadcast_to(jnp.exp(lg * (idx + 1.0)[None, :, None]), (HEADS, L, 128))
    zeta = jnp.broadcast_to(jnp.exp(lg * (L - 1.0 - idx)[None, :, None]), (HEADS, L, 128))
    gl = jnp.broadcast_to(jnp.exp(lg * float(L)), (HEADS, 8, 128))
    return dm.astype(F32), xi.astype(F32), zeta.astype(F32), gl.astype(F32)


def _whole(arr):
    return pl.BlockSpec(arr.shape, lambda n, _nd=arr.ndim: (0,) * _nd)


def _ret_fwd(q, k, v, proj, gn_g, consts, T):
    dm, xi, zeta, gl = consts
    L = RET_L
    n_sc = T // L

    def body(q_ref, k_ref, v_ref, rg_ref, g_ref, dm_ref, xi_ref, ze_ref, gl_ref, y_ref, yr_ref, s_ref, st_ref):
        @pl.when(pl.program_id(0) == 0)
        def _():
            st_ref[...] = jnp.zeros_like(st_ref)

        for h in range(HEADS):
            ks, vs = slice(h * RET_DK, (h + 1) * RET_DK), slice(h * RET_DV, (h + 1) * RET_DV)
            qq, kk, vv = q_ref[:, ks], k_ref[:, ks], v_ref[:, vs]
            st = st_ref[h]
            s_ref[h, 0] = st
            p = (_dot(qq, kk, 1, 1) * dm_ref[h]).astype(BF16)
            cross = _dot(qq, st.astype(BF16), 1, 0)
            xi_c = jnp.concatenate([xi_ref[h], xi_ref[h]], axis=1)
            y = _dot(p, vv, 1, 0) + cross * xi_c
            y_ref[:, vs] = y
            yr_ref[:, vs] = _gn_gate(y, rg_ref[:, vs].astype(F32), g_ref[:, vs]).astype(BF16)
            kz = (kk.astype(F32) * ze_ref[h]).astype(BF16)
            gl2 = jnp.concatenate([gl_ref[h, 0:1, :], gl_ref[h, 0:1, :]], axis=1)
            st_ref[h] = st * gl2 + _dot(kz, vv, 0, 0)

    return _pcall(
        body, grid=(n_sc,),
        in_specs=[pl.BlockSpec((L, 1024), lambda n: (n, 0)), pl.BlockSpec((L, 1024), lambda n: (n, 0)),
                  pl.BlockSpec((L, 2048), lambda n: (n, P_RV // 2048)), pl.BlockSpec((L, 2048), lambda n: (n, P_RG // 2048)),
                  _whole(gn_g), _whole(dm), _whole(xi), _whole(zeta), _whole(gl)],
        out_specs=[pl.BlockSpec((L, 2048), lambda n: (n, 0)), pl.BlockSpec((L, 2048), lambda n: (n, 0)),
                   pl.BlockSpec((HEADS, 1, 128, 256), lambda n: (0, n, 0, 0))],
        out_shape=[jax.ShapeDtypeStruct((T, HEADS * RET_DV), F32), jax.ShapeDtypeStruct((T, HEADS * RET_DV), BF16),
                   jax.ShapeDtypeStruct((HEADS, n_sc, 128, 256), F32)],
        scratch_shapes=[pltpu.VMEM((HEADS, 128, 256), F32)], name="ret_fwd",
        compiler_params=pltpu.CompilerParams(dimension_semantics=("arbitrary",)),
    )(q, k, v, proj, gn_g, dm, xi, zeta, gl)


def _ret_bwd(q, k, v, y, proj, gn_g, states, dyr, consts, tabs, T):
    dm, xi, zeta, gl = consts
    cos128, sin128, _, _ = tabs
    L = RET_L
    n_sc = T // L

    def body(q_ref, k_ref, v_ref, y_ref, rg_ref, g_ref, d_ref, s_ref, dm_ref, xi_ref, ze_ref, gl_ref, c_ref, sn_ref,
             dp_ref, dg_ref, gs_ref):
        @pl.when(pl.program_id(0) == 0)
        def _():
            gs_ref[...] = jnp.zeros_like(gs_ref)

        c, sn = c_ref[...], sn_ref[...]
        dgs = []
        for h in range(HEADS):
            ks, vs = slice(h * RET_DK, (h + 1) * RET_DK), slice(h * RET_DV, (h + 1) * RET_DV)
            dy, drg, dg = _gn_gate_bwd(y_ref[:, vs], rg_ref[:, vs].astype(F32), g_ref[:, vs], d_ref[:, vs].astype(F32))
            dp_ref[:, P_RG + h * RET_DV:P_RG + (h + 1) * RET_DV] = drg.astype(BF16)
            dgs.append(dg)
            qq, kk, vv, dyy = q_ref[:, ks], k_ref[:, ks], v_ref[:, vs], dy.astype(BF16)
            dmm = dm_ref[h]
            gb = gs_ref[h].astype(BF16)
            sb = s_ref[h, 0].astype(BF16)
            xi_c = jnp.concatenate([xi_ref[h], xi_ref[h]], axis=1)
            pb = (_dot(qq, kk, 1, 1) * dmm).astype(BF16)
            kz = (kk.astype(F32) * ze_ref[h]).astype(BF16)
            dp_ref[:, P_RV + h * RET_DV:P_RV + (h + 1) * RET_DV] = (_dot(pb, dyy, 0, 0) + _dot(kz, gb, 1, 0)).astype(BF16)
            da = (_dot(dyy, vv, 1, 1) * dmm).astype(BF16)
            dyx = (dyy.astype(F32) * xi_c).astype(BF16)
            dq = _dot(da, kk, 1, 0) + _dot(dyx, sb, 1, 1)
            dk = _dot(da, qq, 0, 0) + _dot(vv, gb, 1, 1) * ze_ref[h]
            dp_ref[:, P_RQ + h * RET_DK:P_RQ + (h + 1) * RET_DK] = _rope128_t(dq, c, sn).astype(BF16)
            dp_ref[:, P_RK + h * RET_DK:P_RK + (h + 1) * RET_DK] = (_rope128_t(dk, c, sn) * RET_SCALE).astype(BF16)
            gl2 = jnp.concatenate([gl_ref[h, 0:1, :], gl_ref[h, 0:1, :]], axis=1)
            gs_ref[h] = gs_ref[h] * gl2 + _dot(qq, dyx, 0, 0)
        _accum(dg_ref, jnp.concatenate(dgs, axis=1))

    rev = lambda n: n_sc - 1 - n
    return _pcall(
        body, grid=(n_sc,),
        in_specs=[pl.BlockSpec((L, 1024), lambda n: (rev(n), 0)), pl.BlockSpec((L, 1024), lambda n: (rev(n), 0)),
                  pl.BlockSpec((L, 2048), lambda n: (rev(n), P_RV // 2048)), pl.BlockSpec((L, 2048), lambda n: (rev(n), 0)),
                  pl.BlockSpec((L, 2048), lambda n: (rev(n), P_RG // 2048)), _whole(gn_g),
                  pl.BlockSpec((L, 2048), lambda n: (rev(n), 0)),
                  pl.BlockSpec((HEADS, 1, 128, 256), lambda n: (0, rev(n), 0, 0)),
                  _whole(dm), _whole(xi), _whole(zeta), _whole(gl),
                  pl.BlockSpec((L, 128), lambda n: (rev(n), 0)), pl.BlockSpec((L, 128), lambda n: (rev(n), 0))],
        out_specs=[pl.BlockSpec((L, P_GR), lambda n: (rev(n), 0)), pl.BlockSpec((1, 2048), lambda n: (0, 0))],
        out_shape=[jax.ShapeDtypeStruct((T, P_GR), BF16), jax.ShapeDtypeStruct((1, 2048), F32)],
        scratch_shapes=[pltpu.VMEM((HEADS, 128, 256), F32)], name="ret_bwd",
        compiler_params=pltpu.CompilerParams(dimension_semantics=("arbitrary",)),
    )(q, k, v, y, proj, gn_g, dyr, states, dm, xi, zeta, gl, cos128, sin128)


def _diag_mask(nrows, ncols, row0):
    row = row0 + lax.broadcasted_iota(jnp.int32, (nrows, ncols), 0)
    col = lax.broadcasted_iota(jnp.int32, (nrows, ncols), 1)
    return lax.shift_right_logical(col, 6) <= lax.shift_right_logical(row, 6)


def _diag_spans(t):
    h = t // 2
    return ((0, h, h), (h, h, t)) if h % 128 == 0 else ((0, t, t),)


def _tri_steps(nb, by_key):
    if by_key:
        pairs = [(i, j) for j in range(nb) for i in range(j, nb)]
    else:
        pairs = [(i, j) for i in range(nb) for j in range(i + 1)]
    return jnp.array([a for a, _ in pairs], jnp.int32), jnp.array([b for _, b in pairs], jnp.int32)


def _attn_fwd(qn, qpe, kv, kpe, T):
    t = min(ATT_TF, T)
    nb = T // t
    ii, jj = _tri_steps(nb, by_key=False)

    hp = ATT_HP
    w = 128 * hp

    def body(ii_ref, jj_ref, qn_ref, qp_ref, kn_ref, kp_ref, v_ref, o_ref, lse_ref, m_sc, l_sc, acc_sc):
        st = pl.program_id(1)
        i, j = ii_ref[st], jj_ref[st]

        @pl.when(j == 0)
        def _():
            m_sc[...] = jnp.full_like(m_sc, NEG)
            l_sc[...] = jnp.zeros_like(l_sc)
            acc_sc[...] = jnp.zeros_like(acc_sc)

        def update(diag):
            kp = kp_ref[...]
            spans = _diag_spans(t) if diag else ((0, t, t),)
            for hh in range(hp):
                sl = slice(128 * hh, 128 * (hh + 1))
                for row0, nr, nkeys in spans:
                    rows = slice(row0, row0 + nr)
                    q = jnp.concatenate([qn_ref[rows, sl], qp_ref[rows, sl]], axis=1)
                    k = jnp.concatenate([kn_ref[:nkeys, sl], kp[:nkeys]], axis=1)
                    s = _dot(q, k, 1, 1)
                    if diag:
                        s = jnp.where(_diag_mask(nr, nkeys, row0), s, NEG)
                    m_prev = m_sc[rows, sl]
                    m_new = jnp.maximum(m_prev, jnp.max(s, axis=1, keepdims=True))
                    a = jnp.exp2(m_prev - m_new)
                    p = jnp.exp2(s - m_new[:, 0:1])
                    l_sc[rows, sl] = a * l_sc[rows, sl] + jnp.sum(p, axis=1, keepdims=True)
                    acc_sc[rows, sl] = a * acc_sc[rows, sl] + _dot(p.astype(BF16), v_ref[:nkeys, sl], 1, 0)
                    m_sc[rows, sl] = m_new

        @pl.when(j < i)
        def _():
            update(False)

        @pl.when(j == i)
        def _():
            update(True)
            o_ref[...] = (acc_sc[...] / l_sc[...]).astype(BF16)
            lse_ref[...] = m_sc[...] + jnp.log2(l_sc[...])

    qs = pl.BlockSpec((t, w), lambda h, s, ii, jj: (ii[s], h))
    grid_spec = pltpu.PrefetchScalarGridSpec(
        num_scalar_prefetch=2, grid=(HEADS // hp, int(ii.shape[0])),
        in_specs=[qs, qs, pl.BlockSpec((t, w), lambda h, s, ii, jj: (jj[s], h)), pl.BlockSpec((t, 128), lambda h, s, ii, jj: (jj[s], 0)),
                  pl.BlockSpec((t, w), lambda h, s, ii, jj: (jj[s], HEADS // hp + h))],
        out_specs=[qs, qs],
        scratch_shapes=[pltpu.VMEM((t, w), F32), pltpu.VMEM((t, w), F32), pltpu.VMEM((t, w), F32)])
    return _pcall(
        body, grid_spec=grid_spec, out_shape=[jax.ShapeDtypeStruct((T, D), BF16), jax.ShapeDtypeStruct((T, D), F32)], name="attn_fwd",
        compiler_params=pltpu.CompilerParams(dimension_semantics=("arbitrary", "arbitrary")),
    )(ii, jj, qn, qpe, kv, kpe, kv)


def _attn_bwd(qn, qpe, kv, kpe, do, lse, delta, T):
    t = min(ATT_TB, T)
    nb = T // t
    ii, jj = _tri_steps(nb, by_key=True)

    def body(ii_ref, jj_ref, qn_ref, qp_ref, kn_ref, kp_ref, v_ref, do_ref, lse_ref, dl_ref,
             dqn_ref, dqp_ref, dkn_ref, dkp_ref, dv_ref, dk_sc, dv_sc):
        st = pl.program_id(1)
        i, j = ii_ref[st], jj_ref[st]

        @pl.when(st == 0)
        def _():
            dqn_ref[...] = jnp.zeros_like(dqn_ref)
            dqp_ref[...] = jnp.zeros_like(dqp_ref)

        @pl.when(i == j)
        def _():
            dk_sc[...] = jnp.zeros_like(dk_sc)
            dv_sc[...] = jnp.zeros_like(dv_sc)

        def update(diag):
            for row0, nr, nkeys in (_diag_spans(t) if diag else ((0, t, t),)):
                rows = slice(row0, row0 + nr)
                q = jnp.concatenate([qn_ref[rows, :], qp_ref[rows, :]], axis=1)
                k = jnp.concatenate([kn_ref[:nkeys, :], kp_ref[:nkeys, :]], axis=1)
                dob = do_ref[rows, :]
                s = _dot(q, k, 1, 1)
                if diag:
                    s = jnp.where(_diag_mask(nr, nkeys, row0), s, NEG)
                p = jnp.exp2(s - lse_ref[rows, 0:1])
                dv_sc[:nkeys, :] += _dot(p.astype(BF16), dob, 0, 0)
                dp = _dot(dob, v_ref[:nkeys, :], 1, 1)
                ds = (p * (dp - dl_ref[rows, 0:1])).astype(BF16)
                dk_sc[:nkeys, :] += _dot(ds, q, 0, 0)
                dq = _dot(ds, k, 1, 0) * MLA_SCALE
                grows = pl.ds(pl.multiple_of(i * t + row0, nr), nr)
                dqn_ref[grows, :] += dq[:, :128]
                dqp_ref[grows, :] += dq[:, 128:]

        @pl.when(i > j)
        def _():
            update(False)

        @pl.when(i == j)
        def _():
            update(True)

        @pl.when(i == nb - 1)
        def _():
            dkn_ref[...] = (dk_sc[:, :128] * (1.0 / LOG2E)).astype(BF16)
            dkp_ref[...] = dk_sc[:, 128:] * (1.0 / LOG2E)
            dv_ref[...] = dv_sc[...].astype(BF16)

    qs = pl.BlockSpec((t, 128), lambda h, s, ii, jj: (ii[s], h))
    ks = pl.BlockSpec((t, 128), lambda h, s, ii, jj: (jj[s], h))
    hs = pl.BlockSpec((T, 128), lambda h, s, ii, jj: (0, h))
    grid_spec = pltpu.PrefetchScalarGridSpec(
        num_scalar_prefetch=2, grid=(HEADS, int(ii.shape[0])),
        in_specs=[qs, qs, ks, pl.BlockSpec((t, 128), lambda h, s, ii, jj: (jj[s], 0)),
                  pl.BlockSpec((t, 128), lambda h, s, ii, jj: (jj[s], HEADS + h)), qs, qs, qs],
        out_specs=[hs, hs, ks, ks, ks],
        scratch_shapes=[pltpu.VMEM((t, 256), F32), pltpu.VMEM((t, 128), F32)])
    return _pcall(
        body, grid_spec=grid_spec,
        out_shape=[jax.ShapeDtypeStruct((T, D), F32), jax.ShapeDtypeStruct((T, D), F32), jax.ShapeDtypeStruct((T, D), BF16),
                   jax.ShapeDtypeStruct((T, D), F32), jax.ShapeDtypeStruct((T, D), BF16)], name="attn_bwd",
        compiler_params=pltpu.CompilerParams(dimension_semantics=("arbitrary", "arbitrary")),
    )(ii, jj, qn, qpe, kv, kpe, kv, do, lse, delta)


def _mesh_pos():
    x, y, c = lax.axis_index("x"), lax.axis_index("y"), lax.axis_index("c")
    return x, y, c, 4 * x + 2 * y + c


def _peer(x, y, c, k):
    px, py, pc = (x + ((k >> 2) & 1)) % 2, (y + ((k >> 1) & 1)) % 2, (c + (k & 1)) % 2
    return (px, py, pc), 4 * px + 2 * py + pc


_ANY = pl.BlockSpec(memory_space=pl.ANY)


def _rcopy(src, dst, send_sems, recv_sems, k, to):
    return pltpu.make_async_remote_copy(src_ref=src, dst_ref=dst, send_sem=send_sems.at[k], recv_sem=recv_sems.at[k],
                                        device_id=to, device_id_type=pl.DeviceIdType.MESH)


def _all_gather(wsh, ssh):
    def body(w_ref, s_ref, wall_ref, sall_ref, send_sems, recv_sems, loc_sems):
        x, y, c, me = _mesh_pos()
        sib = (x, y, 1 - c)
        chips = [(1 - x, y), (x, 1 - y), (1 - x, 1 - y)]
        slot = lambda px, py, pc: 4 * px + 2 * py + pc
        loc = [pltpu.make_async_copy(s_ref, sall_ref.at[me], loc_sems.at[0])]
        for cp in loc:
            cp.start()
        sends, fwd_waits = [], []
        for n, (src, dst) in enumerate(((w_ref, wall_ref), (s_ref, sall_ref))):
            o = 7 * n
            sends.append(_rcopy(src, dst.at[me], send_sems, recv_sems, o, sib))
            for j, chip in enumerate(chips):
                sends.append(_rcopy(src, dst.at[me], send_sems, recv_sems, o + 1 + j, (*chip, c)))
        for cp in sends:
            cp.start()
        for n, (src, dst) in enumerate(((w_ref, wall_ref), (s_ref, sall_ref))):
            o = 7 * n
            for j, chip in enumerate(chips):
                got = dst.at[slot(*chip, c)]
                _rcopy(src, got, send_sems, recv_sems, o + 1 + j, sib).wait_recv()
                fw = _rcopy(got, got, send_sems, recv_sems, o + 4 + j, sib)
                fw.start()
                sends.append(fw)
            fwd_waits.append(_rcopy(src, dst.at[slot(x, y, 1 - c)], send_sems, recv_sems, o, sib))
            for j, chip in enumerate(chips):
                fwd_waits.append(_rcopy(src, dst.at[slot(*chip, 1 - c)], send_sems, recv_sems, o + 4 + j, sib))
        for cp in fwd_waits:
            cp.wait_recv()
        for cp in sends:
            cp.wait_send()
        for cp in loc:
            cp.wait()

    wall, sall = _pcall(
        body, in_specs=[_ANY, _ANY], out_specs=[_ANY, _ANY],
        out_shape=[jax.ShapeDtypeStruct((N_DEV,) + wsh.shape, wsh.dtype), jax.ShapeDtypeStruct((N_DEV,) + ssh.shape, ssh.dtype)],
        scratch_shapes=[pltpu.SemaphoreType.DMA((14,)), pltpu.SemaphoreType.DMA((14,)), pltpu.SemaphoreType.DMA((1,))],
        name="all_gather_weights",
    )(wsh, ssh)
    me = 4 * lax.axis_index("x") + 2 * lax.axis_index("y") + lax.axis_index("c")
    return lax.dynamic_update_index_in_dim(wall, wsh, me, axis=0), sall


_HBM = pl.BlockSpec(memory_space=pltpu.HBM)
_SEM = pl.BlockSpec(memory_space=pltpu.SEMAPHORE)
_EFFECT = pltpu.SideEffectType.DATAFLOW_SIDE_EFFECTING


def _other_chips(x, y):
    return [(1 - x, y), (x, 1 - y), (1 - x, 1 - y)]


def _gather_start(wsh, order_after):
    def body(w_ref, land_ref, dep_ref, send_sems, recv_sems, w_thru, land_thru, token):
        x, y, c, me = _mesh_pos()
        _rcopy(w_ref, land_ref.at[me], send_sems, recv_sems, 0, (x, y, 1 - c)).start()
        for j, chip in enumerate(_other_chips(x, y)):
            _rcopy(w_ref, land_ref.at[me], send_sems, recv_sems, 1 + j, (*chip, c)).start()
        token[...] = jnp.zeros_like(token)

    shape = (N_DEV,) + wsh.shape
    land = pltpu.with_memory_space_constraint(lax.empty(shape, wsh.dtype), pltpu.HBM)
    return _pcall(
        body, name="gather_start",
        out_shape=(pltpu.SemaphoreType.DMA((4,)), pltpu.SemaphoreType.DMA((4,)), pltpu.HBM(wsh.shape, wsh.dtype),
                   pltpu.HBM(shape, wsh.dtype), jax.ShapeDtypeStruct((8, 128), F32)),
        in_specs=(_HBM, _HBM, _ANY), out_specs=(_SEM, _SEM, _HBM, _HBM, pl.BlockSpec(memory_space=pltpu.VMEM)),
        input_output_aliases={0: 2, 1: 3}, compiler_params=pltpu.CompilerParams(has_side_effects=_EFFECT),
    )(pltpu.with_memory_space_constraint(wsh, pltpu.HBM), land, order_after)


def _gather_wait(send_sems, recv_sems, w_thru, land_thru, after):
    def body(w_ref, land_ref, send_sems, recv_sems, after_ref, w_out, land_out):
        x, y, c, _ = _mesh_pos()
        senders = [(x, y, 1 - c)] + [(*chip, c) for chip in _other_chips(x, y)]
        for k, (px, py, pc) in enumerate(senders):
            cp = _rcopy(w_ref, land_ref.at[4 * px + 2 * py + pc], send_sems, recv_sems, k, (px, py, pc))
            cp.wait_send()
            cp.wait_recv()

    return _pcall(
        body, name="gather_wait", out_shape=(pltpu.HBM(w_thru.shape, w_thru.dtype), pltpu.HBM(land_thru.shape, land_thru.dtype)),
        in_specs=(_HBM, _HBM, _SEM, _SEM, _ANY), out_specs=(_HBM, _HBM), input_output_aliases={0: 0, 1: 1},
        compiler_params=pltpu.CompilerParams(has_side_effects=_EFFECT),
    )(w_thru, land_thru, send_sems, recv_sems, after)


def _gather_finish(wsh, land):
    def body(land_ref, out_ref, send_sems, recv_sems):
        x, y, c, _ = _mesh_pos()
        sib = (x, y, 1 - c)
        sends, recvs = [], []
        for j, (px, py) in enumerate(_other_chips(x, y)):
            mine, theirs = 4 * px + 2 * py + c, 4 * px + 2 * py + (1 - c)
            sends.append(_rcopy(land_ref.at[mine], out_ref.at[mine], send_sems, recv_sems, j, sib))
            recvs.append(_rcopy(land_ref.at[theirs], out_ref.at[theirs], send_sems, recv_sems, j, sib))
        for cp in sends:
            cp.start()
        for cp in recvs:
            cp.wait_recv()
        for cp in sends:
            cp.wait_send()

    done = _pcall(
        body, in_specs=[_ANY], out_specs=_ANY, out_shape=jax.ShapeDtypeStruct(land.shape, land.dtype),
        input_output_aliases={0: 0},
        scratch_shapes=[pltpu.SemaphoreType.DMA((3,)), pltpu.SemaphoreType.DMA((3,))], name="gather_finish",
    )(land)
    me = 4 * lax.axis_index("x") + 2 * lax.axis_index("y") + lax.axis_index("c")
    return lax.dynamic_update_index_in_dim(done, wsh, me, axis=0)


_ROW_TILES = (736, 528, 400, 368, 352, 256, 128, 16, 8)


def _exchange_sibling(g4, name):
    def body(g_ref, r_ref, send_sems, recv_sems):
        x, y, c, _ = _mesh_pos()
        sib = (x, y, 1 - c)
        cps = [_rcopy(g_ref.at[q, 1 - c], r_ref.at[q], send_sems, recv_sems, q, sib) for q in range(4)]
        for cp in cps:
            cp.start()
        for cp in cps:
            cp.wait()

    n, _, R, C = g4.shape
    return _pcall(
        body, in_specs=[_ANY], out_specs=_ANY, out_shape=jax.ShapeDtypeStruct((n, R, C), g4.dtype),
        scratch_shapes=[pltpu.SemaphoreType.DMA((4,)), pltpu.SemaphoreType.DMA((4,))], name=name,
    )(g4)


def _sum_sibling(g4, rsib, cvec, name):
    n, _, R, C = g4.shape
    tr = _pick(R, _ROW_TILES)

    def body(c_ref, g_ref, r_ref, o_ref):
        o_ref[...] = (g_ref[...].astype(F32) + r_ref[...].astype(F32)).astype(o_ref.dtype)

    grid_spec = pltpu.PrefetchScalarGridSpec(
        num_scalar_prefetch=1, grid=(n, R // tr),
        in_specs=[pl.BlockSpec((None, None, tr, C), lambda q, i, cr: (q, cr[0], i, 0)), pl.BlockSpec((None, tr, C), lambda q, i, cr: (q, i, 0))],
        out_specs=pl.BlockSpec((None, tr, C), lambda q, i, cr: (q, i, 0)))
    return _pcall(body, grid_spec=grid_spec, out_shape=jax.ShapeDtypeStruct((n, R, C), g4.dtype), name=name)(cvec, g4, rsib)


def _chips_start(part, name, order_after=None):
    n_in = 2 + int(order_after is not None)

    def body(*refs):
        p_ref, land_ref = refs[:2]
        send_sems, recv_sems, token = refs[n_in], refs[n_in + 1], refs[n_in + 4]
        x, y, c, _ = _mesh_pos()
        myq = 2 * x + y
        for j, (px, py) in enumerate(_other_chips(x, y)):
            _rcopy(p_ref.at[2 * px + py], land_ref.at[myq], send_sems, recv_sems, j, (px, py, c)).start()
        token[...] = jnp.zeros_like(token)

    land = pltpu.with_memory_space_constraint(lax.empty(part.shape, part.dtype), pltpu.HBM)
    extra = () if order_after is None else (order_after,)
    return _pcall(
        body, name=name,
        out_shape=(pltpu.SemaphoreType.DMA((3,)), pltpu.SemaphoreType.DMA((3,)), pltpu.HBM(part.shape, part.dtype),
                   pltpu.HBM(part.shape, part.dtype), jax.ShapeDtypeStruct((8, 128), F32)),
        in_specs=(_HBM, _HBM) + (_ANY,) * len(extra),
        out_specs=(_SEM, _SEM, _HBM, _HBM, pl.BlockSpec(memory_space=pltpu.VMEM)),
        input_output_aliases={0: 2, 1: 3}, compiler_params=pltpu.CompilerParams(has_side_effects=_EFFECT),
    )(pltpu.with_memory_space_constraint(part, pltpu.HBM), land, *extra)


def _chips_wait(send_sems, recv_sems, p_thru, land_thru, after, name):
    def body(p_ref, land_ref, send_sems, recv_sems, after_ref, p_out, land_out):
        x, y, c, _ = _mesh_pos()
        for j, (px, py) in enumerate(_other_chips(x, y)):
            q = 2 * px + py
            cp = _rcopy(p_ref.at[q], land_ref.at[q], send_sems, recv_sems, j, (px, py, c))
            cp.wait_send()
            cp.wait_recv()

    return _pcall(
        body, name=name, out_shape=(pltpu.HBM(p_thru.shape, p_thru.dtype), pltpu.HBM(p_thru.shape, p_thru.dtype)),
        in_specs=(_HBM, _HBM, _SEM, _SEM, _ANY), out_specs=(_HBM, _HBM), input_output_aliases={0: 0, 1: 1},
        compiler_params=pltpu.CompilerParams(has_side_effects=_EFFECT),
    )(p_thru, land_thru, send_sems, recv_sems, after)


def _sum_chips(part, land, qvec, name):
    n, R, C = part.shape
    tr = _pick(R, _ROW_TILES)

    def body(q_ref, p_ref, *refs):
        o_ref = refs[n]
        acc = None
        for q in range(n):
            term = jnp.where(q_ref[0] == q, p_ref[...], refs[q][...]).astype(F32)
            acc = term if acc is None else acc + term
        o_ref[...] = acc

    def land_spec(q):
        return pl.BlockSpec((None, tr, C), lambda i, m: (jnp.where(m[0] == q, (q + 1) % n, q), i, 0))

    grid_spec = pltpu.PrefetchScalarGridSpec(
        num_scalar_prefetch=1, grid=(R // tr,),
        in_specs=[pl.BlockSpec((None, tr, C), lambda i, m: (m[0], i, 0))] + [land_spec(q) for q in range(n)],
        out_specs=pl.BlockSpec((tr, C), lambda i, m: (i, 0)))
    return _pcall(body, grid_spec=grid_spec, out_shape=jax.ShapeDtypeStruct((R, C), F32), name=name)(qvec, part, *([land] * n))


def _exchange_small(gsmall):
    def body(s_ref, srecv_ref, send_sems, recv_sems, loc_sem):
        x, y, c, me = _mesh_pos()
        loc = pltpu.make_async_copy(s_ref, srecv_ref.at[me], loc_sem.at[0])
        loc.start()
        sends, recvs = [], []
        for k in range(1, N_DEV):
            to, pidx = _peer(x, y, c, k)
            sends.append(_rcopy(s_ref, srecv_ref.at[me], send_sems, recv_sems, k - 1, to))
            recvs.append(_rcopy(s_ref, srecv_ref.at[pidx], send_sems, recv_sems, k - 1, to))
        for cp in sends:
            cp.start()
        for cp in recvs:
            cp.wait_recv()
        for cp in sends:
            cp.wait_send()
        loc.wait()

    return _pcall(
        body, in_specs=[_ANY], out_specs=_ANY, out_shape=jax.ShapeDtypeStruct((N_DEV,) + gsmall.shape, gsmall.dtype),
        scratch_shapes=[pltpu.SemaphoreType.DMA((7,)), pltpu.SemaphoreType.DMA((7,)), pltpu.SemaphoreType.DMA((1,))],
        name="exchange_small",
    )(gsmall)


def _sum_slots(recv, name):
    n, R, C = recv.shape
    tr = _pick(R, _ROW_TILES)

    def body(r_ref, o_ref):
        acc = r_ref[0].astype(F32)
        for s in range(1, n):
            acc = acc + r_ref[s].astype(F32)
        o_ref[...] = acc

    return _pcall(body, grid=(R // tr,), in_specs=[pl.BlockSpec((n, tr, C), lambda i: (0, i, 0))],
                  out_specs=pl.BlockSpec((tr, C), lambda i: (i, 0)), out_shape=jax.ShapeDtypeStruct((R, C), F32), name=name)(recv)


def _adamw(w, g, m, v, name, order_after=None):
    shape = w.shape
    w2, g2, m2, v2 = (a.reshape(-1, shape[-1]) for a in (w, g, m, v))
    R, C = w2.shape
    tr = _pick(R, (256, 128, 64, 32, 16, 8)) if R > 256 else R
    n_in = 4 + int(order_after is not None)

    def body(*refs):
        w_ref, g_ref, m_ref, v_ref = refs[:4]
        d_ref, nm_ref, nv_ref = refs[n_in:]
        gg = g_ref[...]
        nm = ADAM_B1 * m_ref[...] + (1.0 - ADAM_B1) * gg
        nv = ADAM_B2 * v_ref[...] + (1.0 - ADAM_B2) * jnp.square(gg)
        m_hat = nm / (1.0 - ADAM_B1 ** ADAM_STEP)
        v_hat = nv / (1.0 - ADAM_B2 ** ADAM_STEP)
        d_ref[...] = -ADAM_LR * (m_hat / (jnp.sqrt(v_hat) + ADAM_EPS) + ADAM_WD * w_ref[...])
        nm_ref[...] = nm
        nv_ref[...] = nv

    spec = pl.BlockSpec((tr, C), lambda i: (i, 0))
    in_specs, args = [spec] * 4, [w2, g2, m2, v2]
    if order_after is not None:
        in_specs.append(pl.BlockSpec(memory_space=pl.ANY))
        args.append(order_after)
    d, nm, nv = _pcall(body, grid=(R // tr,), in_specs=in_specs, out_specs=[spec] * 3,
                       out_shape=[jax.ShapeDtypeStruct((R, C), F32)] * 3, name=name)(*args)
    return d.reshape(shape), nm.reshape(shape), nv.reshape(shape)


def _to_rows(name, w):
    w = w[0]
    if name in ("ffn1_w_in", "ffn2_w_in", "w_in"):
        return w.T
    if name in ("w_uq", "w_ukv", "ple_w_proj"):
        return w.T.reshape(-1, 1024)
    return w


def _from_rows(name, g, shape):
    if name in ("ffn1_w_in", "ffn2_w_in", "w_in"):
        return g.T.reshape(shape)
    if name in ("w_uq", "w_ukv", "ple_w_proj"):
        return g.reshape(-1, shape[1]).T.reshape(shape)
    return g.reshape(shape)


def _unpack(wall, group):
    out, off = {}, 0
    for name, r in group:
        out[name] = wall[:, off:off + r, :].reshape(N_DEV * r, 1024)
        off += _pad16(r)
    return out


def _w_in_internal(wt):
    return jnp.concatenate([wt[0:6144], wt[6720:8768], wt[6144:6720], jnp.zeros((P_W - W_IN_COLS, 1024), wt.dtype)], axis=0)


def _rope_tables(positions):
    pos = positions[0].astype(F32)

    def cs(half):
        inv = jnp.tile(ROPE_BASE ** (-jnp.arange(half, dtype=F32) / half), 128 // half)
        sign = jnp.tile(jnp.concatenate([-jnp.ones((half,), F32), jnp.ones((half,), F32)]), 64 // half)
        ang = pos[:, None] * inv
        return jnp.cos(ang), jnp.sin(ang) * sign

    return cs(64) + cs(32)


def _local_step(x, p, positions, target, W, ln_g, ln_b, gn_g, qg, kvg, early=None, rest_weights=None, start_token=None):
    T = x.shape[0]
    tabs = _rope_tables(positions)
    rc = _ret_consts()
    lg = [ln_g[i:i + 1] for i in range(4)]
    lb = [ln_b[i:i + 1] for i in range(4)]
    pb = p.astype(BF16)

    hs1, gu1, *xb = _ffn_in(x, W["ffn1_w_in"], "ffn1_in", order_after=start_token)
    xb = xb[0] if xb else x
    f1, h1, h1b = _mm(hs1, W["ffn1_w_out"], name="ffn1_out", tm=LN_TM, epilogue=_ln_epilogue(x, 0.5, lg[0], lb[0]))
    if rest_weights is not None:
        W = {**W, **rest_weights(h1b)}
    w_in_t = _w_in_internal(W["w_in"])
    wuq = W["w_uq"].reshape(1536, LORA).reshape(HEADS, 192, LORA)
    wuq = jnp.concatenate([wuq[:, :128].reshape(1024, LORA), wuq[:, 128:].reshape(512, LORA)], axis=0)
    wukv = W["w_ukv"].reshape(2048, LORA).reshape(HEADS, 2, 128, LORA).transpose(1, 0, 2, 3).reshape(2048, LORA)
    wp_t = W["ple_w_proj"].reshape(1024, D_PLE)
    proj = _mm(h1b, w_in_t, tb=True, out_dtype=BF16, name="mixer_in")
    rq, rk, qn, kvn, kpe = _mixer_prep_fwd(proj, tabs, qg, kvg, T)
    y, yr, states = _ret_fwd(rq, rk, proj, proj, gn_g, rc, T)
    y_ret = _mm(yr, W["w_ret_o"], out_dtype=BF16, name="ret_o")
    qnope, qpe = _mm(qn, wuq, tb=True, name="mla_uq", tm=LN_TM, tn=1536, epilogue=_q_assemble_epilogue(tabs))
    kv = _mm(kvn, wukv, tb=True, out_dtype=BF16, name="mla_ukv")
    o, lse = _attn_fwd(qnope, qpe, kv, kpe, T)
    y_mla, mix = _mm(o, W["w_mla_o"], name="mla_o", tm=LN_TM, epilogue=_mix_epilogue(proj, y_ret))
    mixed, h2, h2b = _mm(mix, W["w_out"], name="mixer_out", tm=LN_TM, epilogue=_ln_epilogue(h1, 1.0, lg[1], lb[1]))
    hs2, gu2 = _ffn_in(h2b, W["ffn2_w_in"], "ffn2_in")
    f2, h3, h3b = _mm(hs2, W["ffn2_w_out"], name="ffn2_out", tm=LN_TM, epilogue=_ln_epilogue(h2, 0.5, lg[2], lb[2]))
    pp = _mm(pb, wp_t, tb=True, name="ple_proj")

    G = {}
    dh3_a, dgl, dpp, dg3, db3, loss = _mm(h3b, W["ple_w_gate"], name="ple_gate", tm=LN_TM,
                                          epilogue=_head_epilogue(h3, pp, target, lg[3], lb[3]))
    G["ple_w_gate"] = _mm(h3b, dgl, ta=True, name="d_ple_gate")
    G["ple_w_proj"] = _mm(dpp, pb, ta=True, name="d_ple_proj")
    dh2_a, df2, dg2, db2 = _mm(dgl, W["ple_w_gate"], tb=True, add=dh3_a, name="dh3", tm=LN_TM,
                               epilogue=_ln_bwd_epilogue(h2, f2, 0.5, lg[2], lb[2]))
    G["ffn2_w_out"] = _mm(hs2, df2, ta=True, name="d_ffn2_out")
    da2 = _ffn_act_bwd(df2, W["ffn2_w_out"], gu2, "ffn2_act_bwd")
    G["ffn2_w_in"] = _mm(da2, h2b, ta=True, name="d_ffn2_in")
    dh1_a, dmixed, dg1, db1 = _mm(da2, W["ffn2_w_in"], add=dh2_a, name="dh2", tm=LN_TM,
                                  epilogue=_ln_bwd_epilogue(h1, mixed, 1.0, lg[1], lb[1]))
    G["w_out"] = _mm(mix, dmixed, ta=True, name="d_mixer_out")
    dp_gate, dy_ret, dy_mla = _mm(dmixed, W["w_out"], tb=True, name="dmix", tm=LN_TM,
                                  epilogue=_mix_bwd_epilogue(proj, y_ret, y_mla))
    G["w_mla_o"] = _mm(o, dy_mla, ta=True, name="d_mla_o")
    dob, delta = _mm(dy_mla, W["w_mla_o"], tb=True, name="do", tm=LN_TM, epilogue=_delta_epilogue(o))
    dqn_f, dqpe_f, dkn, dkpe_all, dv = _attn_bwd(qnope, qpe, kv, kpe, dob, lse, delta, T)
    dq_n, dq_r = _q_assemble_bwd(dqn_f, dqpe_f, tabs, T)
    g_uq = jnp.concatenate([_mm(dq_n, qn, ta=True, name="d_uq_nope"), _mm(dq_r, qn, ta=True, name="d_uq_rope")], axis=0)
    g_uq = jnp.concatenate([g_uq[:1024].reshape(HEADS, 128, LORA), g_uq[1024:].reshape(HEADS, 64, LORA)], axis=1)
    G["w_uq"] = g_uq.reshape(1536 * LORA // 1024, 1024)
    dqn = _mm(dq_r, wuq[1024:], add=_mm(dq_n, wuq[:1024], name="dqn_a"), name="dqn_b")
    g_ukv = jnp.stack([_mm(dkn, kvn, ta=True, name="d_ukv_k"), _mm(dv, kvn, ta=True, name="d_ukv_v")], axis=0)
    G["w_ukv"] = g_ukv.reshape(2, HEADS, 128, LORA).transpose(1, 0, 2, 3).reshape(2048 * LORA // 1024, 1024)
    dkvn = _mm(dv, wukv[1024:], add=_mm(dkn, wukv[:1024], name="dkvn_a"), name="dkvn_b")
    dp_mla, dqg, dkvg = _rms_bwd(proj, dqn, dkvn, dkpe_all, tabs, qg, kvg, T)
    G["w_ret_o"] = _mm(yr, dy_ret, ta=True, name="d_ret_o")
    dyr = _mm(dy_ret, W["w_ret_o"], tb=True, out_dtype=BF16, name="dyr")
    dp_ret, dgn = _ret_bwd(rq, rk, proj, y, proj, gn_g, states, dyr, rc, tabs, T)
    g_ret, g_gate, g_mla = (_mm(dp, h1b, ta=True, name="d_mixer_in_" + n)
                            for dp, n in ((dp_ret, "ret"), (dp_gate, "gate"), (dp_mla, "mla")))
    G["w_in"] = jnp.concatenate([g_ret, g_mla[:W_IN_COLS - P_CQ], g_gate], axis=0)
    lg0 = lg[0] if early is None else lg[0] + early(G)[0:1, 0:1]
    dh1 = _mm_pieces([(dp_ret, w_in_t[:P_GR]), (dp_gate, w_in_t[P_GR:P_CQ]), (dp_mla, w_in_t[P_CQ:])], dh1_a, "dh1")
    dx_a, df1, dg0, db0 = _ln_bwd_rows(x, f1, 0.5, lg0, lb[0], dh1, T, "ln0_bwd")
    G["ffn1_w_out"] = _mm(hs1, df1, ta=True, name="d_ffn1_out")
    da1 = _ffn_act_bwd(df1, W["ffn1_w_out"], gu1, "ffn1_act_bwd")
    G["ffn1_w_in"] = _mm(da1, xb, ta=True, name="d_ffn1_in")
    grad_x = _mm(da1, W["ffn1_w_in"], add=dx_a, name="grad_x")

    small = dict(ln_g=jnp.concatenate([dg0, dg1, dg2, dg3], axis=0), ln_b=jnp.concatenate([db0, db1, db2, db3], axis=0),
                 ret_gn_g=dgn, q_norm_g=dqg, kv_norm_g=dkvg)
    return loss, grad_x, G, small


def kernel(x, p, positions, ln_g, ln_b, ffn1_w_in, ffn1_w_out, w_in, ret_gn_g, w_ret_o, q_norm_g, kv_norm_g, w_uq, w_ukv, w_mla_o, w_out, ffn2_w_in, ffn2_w_out, ple_w_gate, ple_w_proj, loss_target, m_ln_g, m_ln_b, m_ffn1_w_in, m_ffn1_w_out, m_w_in, m_ret_gn_g, m_w_ret_o, m_q_norm_g, m_kv_norm_g, m_w_uq, m_w_ukv, m_w_mla_o, m_w_out, m_ffn2_w_in, m_ffn2_w_out, m_ple_w_gate, m_ple_w_proj, v_ln_g, v_ln_b, v_ffn1_w_in, v_ffn1_w_out, v_w_in, v_ret_gn_g, v_w_ret_o, v_q_norm_g, v_kv_norm_g, v_w_uq, v_w_ukv, v_w_mla_o, v_w_out, v_ffn2_w_in, v_ffn2_w_out, v_ple_w_gate, v_ple_w_proj):
    names = ("ln_g", "ln_b", "ffn1_w_in", "ffn1_w_out", "w_in", "ret_gn_g", "w_ret_o", "q_norm_g", "kv_norm_g", "w_uq", "w_ukv",
             "w_mla_o", "w_out", "ffn2_w_in", "ffn2_w_out", "ple_w_gate", "ple_w_proj")
    ws = dict(zip(names, (ln_g, ln_b, ffn1_w_in, ffn1_w_out, w_in, ret_gn_g, w_ret_o, q_norm_g, kv_norm_g, w_uq, w_ukv, w_mla_o,
                          w_out, ffn2_w_in, ffn2_w_out, ple_w_gate, ple_w_proj)))
    ms = dict(zip(names, (m_ln_g, m_ln_b, m_ffn1_w_in, m_ffn1_w_out, m_w_in, m_ret_gn_g, m_w_ret_o, m_q_norm_g, m_kv_norm_g, m_w_uq,
                          m_w_ukv, m_w_mla_o, m_w_out, m_ffn2_w_in, m_ffn2_w_out, m_ple_w_gate, m_ple_w_proj)))
    vs = dict(zip(names, (v_ln_g, v_ln_b, v_ffn1_w_in, v_ffn1_w_out, v_w_in, v_ret_gn_g, v_w_ret_o, v_q_norm_g, v_kv_norm_g, v_w_uq,
                          v_w_ukv, v_w_mla_o, v_w_out, v_ffn2_w_in, v_ffn2_w_out, v_ple_w_gate, v_ple_w_proj)))

    parts = []
    for name, r in PACK:
        rows = _to_rows(name, ws[name])
        if _pad16(r) != r:
            rows = jnp.concatenate([rows, jnp.zeros((_pad16(r) - r, 1024), F32)], axis=0)
        parts.append(rows)
    wsh_first = jnp.concatenate(parts[:len(PACK_LATE)], axis=0).astype(BF16)
    wsh_rest = jnp.concatenate(parts[len(PACK_LATE):], axis=0).astype(BF16)
    ssh = jnp.concatenate([ln_g[0], ln_b[0]], axis=0)
    wall_first, sall = _all_gather(wsh_first, ssh)
    *gather_handles, start_token = _gather_start(wsh_rest, wall_first)
    W = _unpack(wall_first, PACK_LATE)
    ln_full = sall.reshape(N_DEV, 2, 4, 128).transpose(1, 2, 0, 3).reshape(2, 4, 1024)

    def rest_weights(after):
        w_thru, land = _gather_wait(*gather_handles, after)
        return _unpack(_gather_finish(w_thru, land), PACK_EARLY)

    cvec = lax.axis_index("c").astype(jnp.int32).reshape(1)
    qvec = (2 * lax.axis_index("x") + lax.axis_index("y")).astype(jnp.int32).reshape(1)

    def chip_partials(G, group, tag):
        gparts = []
        for name, r in group:
            g = G[name].reshape(N_DEV, r, 1024)
            if _pad16(r) != r:
                g = jnp.concatenate([g, jnp.zeros((N_DEV, _pad16(r) - r, 1024), g.dtype)], axis=1)
            gparts.append(g)
        gfull = jnp.concatenate(gparts, axis=1).astype(BF16)
        g4 = gfull.reshape(4, 2, gfull.shape[1], 1024)
        return _sum_sibling(g4, _exchange_sibling(g4, "exchange_sibling_" + tag), cvec, "sum_sibling_" + tag)

    in_flight = []

    def early(G):
        *handles, token = _chips_start(chip_partials(G, PACK_EARLY, "early"), "chips_start_early")
        in_flight.append(handles)
        return token

    loss_p, grad_x, G, small = _local_step(x[0], p[0, 0], positions, loss_target[0], W, ln_full[0], ln_full[1],
                                           ret_gn_g, q_norm_g, kv_norm_g, early=early, rest_weights=rest_weights,
                                           start_token=start_token)

    part_e, land_e = _chips_wait(*in_flight[0], grad_x, "chips_wait_early")
    gsh_early = _sum_chips(part_e, land_e, qvec, "sum_grads_early")
    pad256 = lambda a: jnp.concatenate([a, jnp.zeros((1, 1024 - a.shape[1]), F32)], axis=1)
    gsmall = jnp.concatenate([small["ln_g"], small["ln_b"], small["ret_gn_g"].reshape(2, 1024), pad256(small["q_norm_g"]),
                              pad256(small["kv_norm_g"]), pad256(loss_p[0:1, 0:128]),
                              jnp.zeros((SMALL_ROWS - 13, 1024), F32)], axis=0)
    srecv = _exchange_small(gsmall)
    part_l = chip_partials(G, PACK_LATE, "late")
    *late_handles, late_token = _chips_start(part_l, "chips_start_late", order_after=srecv)
    ssum = _sum_slots(srecv, "sum_small_grads")

    def unpack_grads(group, gsh):
        out, off = {}, 0
        for name, r in group:
            out[name] = _from_rows(name, gsh[off:off + r], ws[name].shape)
            off += _pad16(r)
        return out

    grads = unpack_grads(PACK_EARLY, gsh_early)
    me = 4 * lax.axis_index("x") + 2 * lax.axis_index("y") + lax.axis_index("c")
    grads["ln_g"] = lax.dynamic_slice(ssum[0:4], (0, me * 128), (4, 128)).reshape(1, 4, 128)
    grads["ln_b"] = lax.dynamic_slice(ssum[4:8], (0, me * 128), (4, 128)).reshape(1, 4, 128)
    grads["ret_gn_g"] = ssum[8:10].reshape(1, 2048)
    grads["q_norm_g"] = ssum[10:11, :256]
    grads["kv_norm_g"] = ssum[11:12, :256]

    delta, new_m, new_v = {}, {}, {}
    late_names = [n for n, _ in PACK_LATE]
    last = late_token
    for name in names:
        if name not in late_names:
            delta[name], new_m[name], new_v[name] = _adamw(ws[name], grads[name], ms[name], vs[name], "adamw_" + name,
                                                           order_after=last)
            last = new_v[name]
    part_l, land_l = _chips_wait(*late_handles, last, "chips_wait_late")
    grads.update(unpack_grads(PACK_LATE, _sum_chips(part_l, land_l, qvec, "sum_grads_late")))
    for name in late_names:
        delta[name], new_m[name], new_v[name] = _adamw(ws[name], grads[name], ms[name], vs[name], "adamw_" + name)

    return (ssum[12, 0], grad_x[None], *[grads[n] for n in names], *[delta[n] for n in names],
            *[new_m[n] for n in names], *[new_v[n] for n in names])
```

```python
import math

import jax
import jax.numpy as jnp
from jax import lax
from jax.experimental import pallas as pl
from jax.experimental.pallas import tpu as pltpu

F32 = jnp.float32
BF16 = jnp.bfloat16

N_DEV = 8
D = 1024
D_FF = 2816
D_PLE = 256
CHUNK = 64
HEADS = 8
RET_DK = 128
RET_DV = 256
MLA_NOPE = 128
MLA_ROPE = 64
MLA_DV = 128
LORA = 256
ROPE_BASE = 10000.0
EPS = 1e-5
ALPHA = 2.0 ** 0.25
RET_SCALE = RET_DK ** -0.5
MLA_SCALE = (MLA_NOPE + MLA_ROPE) ** -0.5
NEG = -1e30

ADAM_LR = 0.001
ADAM_B1 = 0.9
ADAM_B2 = 0.999
ADAM_EPS = 1e-08
ADAM_WD = 0.01
ADAM_STEP = 10

P_RQ, P_RK, P_RV, P_RG, P_GR, P_GM, P_CQ, P_CKV, P_KPE, P_W = 0, 1024, 2048, 4096, 6144, 7168, 8192, 8448, 8704, 8960
W_IN_COLS = 8768
RET_L = 256
ATT_TF = 2048
ATT_TB = 1024
ATT_HP = 1
LOG2E = math.log2(math.e)
Q_PRESCALE = MLA_SCALE * LOG2E

PACK = (("ffn1_w_in", 704), ("ffn1_w_out", 352), ("w_in", 1096), ("w_ret_o", 256), ("w_uq", 48), ("w_ukv", 64),
        ("w_mla_o", 128), ("w_out", 128), ("ffn2_w_in", 704), ("ffn2_w_out", 352), ("ple_w_gate", 128), ("ple_w_proj", 32))


def _pad16(r):
    return -(-r // 16) * 16


PACK_LATE = PACK[:2]
PACK_EARLY = PACK[2:]
SMALL_ROWS = 16


def _pcall(body, **kw):
    return pl.pallas_call(body, **kw)


def _pick(dim, prefs):
    for p in prefs:
        if dim % p == 0:
            return p
    return dim


def _sigmoid(x):
    return 1.0 / (1.0 + jnp.exp(-x))


def _silu(x):
    return x * _sigmoid(x)


def _ln(r, g, b):
    mu = jnp.mean(r, axis=-1, keepdims=True)
    var = jnp.mean(jnp.square(r - mu), axis=-1, keepdims=True)
    return (r - mu) * lax.rsqrt(var + EPS) * g + b


def _rms(x, g):
    return x * lax.rsqrt(jnp.mean(jnp.square(x), axis=-1, keepdims=True) + EPS) * g


def _dot(a, b, ca, cb):
    return lax.dot_general(a, b, (((ca,), (cb,)), ((), ())), preferred_element_type=F32)


def _accum(ref, val, first=None):
    @pl.when(pl.program_id(0) == 0 if first is None else first)
    def _():
        ref[...] = jnp.zeros_like(ref)

    ref[...] += val


def _mm(a, b, *, ta=False, tb=False, add=None, out_dtype=None, name, tm=None, tn=None, tk=None, epilogue=None):
    parts = a.shape[0] if a.ndim == 3 else 1
    ar, ac = a.shape[-2], a.shape[-1]
    out_dtype = out_dtype or (BF16 if ta else F32)
    if ta:
        K, M = ar, ac * parts
    else:
        M, K = ar, ac * parts
    if tb:
        N, K2 = b.shape
    else:
        K2, N = b.shape
    assert K == K2, (a.shape, b.shape, ta, tb)
    big = (1024, 1408, 1280, 768, 512, 256, 128)
    tm = tm or _pick(ac if (ta and parts > 1) else M, big)
    tn = tn or (N if N <= 1024 else _pick(N, big))
    kdim = ac if (not ta and parts > 1) else K
    tk = tk or (kdim if kdim <= 2816 else
                _pick(kdim, (2048, 1408, 1280, 1024, 512) if tm <= 1024 else (1024, 1408, 1280, 512)))
    nk = K // tk
    grid = (M // tm, N // tn, nk)
    if parts > 1 and ta:
        per = ac // tm
        a_spec = pl.BlockSpec((None, tk, tm), lambda i, j, k: (i // per, k, i % per))
    elif parts > 1:
        per = ac // tk
        a_spec = pl.BlockSpec((None, tm, tk), lambda i, j, k: (k // per, i, k % per))
    else:
        a_spec = pl.BlockSpec((tk, tm), lambda i, j, k: (k, i)) if ta else pl.BlockSpec((tm, tk), lambda i, j, k: (i, k))
    b_spec = pl.BlockSpec((tn, tk), lambda i, j, k: (j, k)) if tb else pl.BlockSpec((tk, tn), lambda i, j, k: (k, j))
    o_spec = pl.BlockSpec((tm, tn), lambda i, j, k: (i, j))
    ca, cb = (0 if ta else 1), (1 if tb else 0)
    has_add = add is not None
    n_in = 2 + int(has_add)
    if epilogue is not None:
        assert tn == N and not ta
        ep_fn, ep_rows, ep_whole, ep_outs, ep_accs = epilogue
        n_ep_in = len(ep_rows) + len(ep_whole)
        n_out = len(ep_outs) + len(ep_accs)
    else:
        n_ep_in, n_out = 0, 1

    def body(*refs):
        a_ref, b_ref = refs[0], refs[1]
        add_ref = refs[2] if has_add else None
        o_ref = refs[n_in + n_ep_in]
        first_row_tile = pl.program_id(0) == 0

        def finish(r):
            if has_add:
                r = r + add_ref[...].astype(F32)
            if epilogue is not None:
                ep_fn(r, refs[n_in:n_in + n_ep_in], refs[n_in + n_ep_in:n_in + n_ep_in + n_out], first_row_tile)
            else:
                o_ref[...] = r.astype(out_dtype)

        if nk == 1:
            finish(_dot(a_ref[...], b_ref[...], ca, cb))
            return
        acc_ref = refs[-1]
        k = pl.program_id(2)

        @pl.when(k == 0)
        def _():
            acc_ref[...] = jnp.zeros_like(acc_ref)

        acc_ref[...] += _dot(a_ref[...], b_ref[...], ca, cb)

        @pl.when(k == nk - 1)
        def _():
            finish(acc_ref[...])

    in_specs = [a_spec, b_spec] + ([o_spec] if has_add else [])
    args = (a, b) + ((add,) if has_add else ())
    out_specs, out_shape = o_spec, jax.ShapeDtypeStruct((M, N), out_dtype)
    if epilogue is not None:
        for row_in in ep_rows:
            arr, col_block, width = (tuple(row_in) + (N,))[:3] if isinstance(row_in, tuple) else (row_in, 0, N)
            in_specs.append(pl.BlockSpec((tm, width), lambda i, j, k, _c=col_block: (i, _c)))
            args += (arr,)
        in_specs += [pl.BlockSpec(w.shape, lambda i, j, k, _n=w.ndim: (0,) * _n) for w in ep_whole]
        args += tuple(ep_whole)
        outs = [o if isinstance(o, tuple) else (o, N) for o in ep_outs]
        out_specs = ([pl.BlockSpec((tm, w), lambda i, j, k: (i, 0)) for _, w in outs]
                     + [pl.BlockSpec((r, N), lambda i, j, k: (0, 0)) for r in ep_accs])
        out_shape = [jax.ShapeDtypeStruct((M, w), dt) for dt, w in outs] + [jax.ShapeDtypeStruct((r, N), F32) for r in ep_accs]
    return _pcall(
        body, grid=grid, in_specs=in_specs, out_specs=out_specs, out_shape=out_shape,
        scratch_shapes=[pltpu.VMEM((tm, tn), F32)] if nk > 1 else [], name=name,
        compiler_params=pltpu.CompilerParams(dimension_semantics=("arbitrary" if epilogue is not None else "parallel", "parallel", "arbitrary")),
    )(*args)


def _mm_pieces(pieces, add, name):
    M, N = add.shape
    tm = _pick(M, (1024, 512, 256, 128))
    spans, in_specs, args, start = [], [], [], 0
    for a, b in pieces:
        K = a.shape[1]
        assert b.shape == (K, N) and a.shape[0] == M, (a.shape, b.shape)
        tk = _pick(K, (1536, 1024, 768, 512, 256, 128))
        n = K // tk
        spans.append((start, n))
        block = lambda k, _s=start, _n=n: jnp.where(k < _s, _n - 1, jnp.minimum(k - _s, _n - 1))
        in_specs.append(pl.BlockSpec((tm, tk), lambda i, k, _s=start, _blk=block: (jnp.where(k < _s, jnp.maximum(i - 1, 0), i), _blk(k))))
        in_specs.append(pl.BlockSpec((tk, N), lambda i, k, _blk=block: (_blk(k), 0)))
        args += [a, b]
        start += n
    o_spec = pl.BlockSpec((tm, N), lambda i, k: (i, 0))

    def body(*refs):
        add_ref, o_ref = refs[-2], refs[-1]
        k = pl.program_id(1)

        @pl.when(k == 0)
        def _():
            o_ref[...] = add_ref[...]

        for p, (s, n) in enumerate(spans):
            @pl.when((k >= s) & (k < s + n))
            def _(p=p):
                o_ref[...] += _dot(refs[2 * p][...], refs[2 * p + 1][...], 1, 0)

    return _pcall(
        body, grid=(M // tm, start), in_specs=in_specs + [o_spec], out_specs=o_spec, out_shape=jax.ShapeDtypeStruct((M, N), F32),
        name=name, compiler_params=pltpu.CompilerParams(dimension_semantics=("parallel", "arbitrary")),
    )(*args, add)


def _ln_epilogue(res, c, g, b):
    def fn(r, ins, outs, first):
        res_ref, g_ref, b_ref = ins
        f_ref, h_ref, hb_ref = outs
        h = _ln(ALPHA * res_ref[...] + c * r, g_ref[...], b_ref[...])
        f_ref[...] = r
        h_ref[...] = h
        hb_ref[...] = h.astype(BF16)

    return (fn, [res], [g, b], (F32, F32, BF16), ())


def _ln_bwd_epilogue(res, f, c, g, b):
    def fn(r, ins, outs, first):
        res_ref, f_ref, g_ref, b_ref = ins
        dr_ref, df_ref, dg_ref, db_ref = outs
        pre = ALPHA * res_ref[...] + c * f_ref[...]
        xc = pre - jnp.mean(pre, axis=-1, keepdims=True)
        rstd = lax.rsqrt(jnp.mean(jnp.square(xc), axis=-1, keepdims=True) + EPS)
        xhat = xc * rstd
        dyg = r * g_ref[...]
        dpre = rstd * (dyg - jnp.mean(dyg, axis=-1, keepdims=True) - xhat * jnp.mean(dyg * xhat, axis=-1, keepdims=True))
        dr_ref[...] = ALPHA * dpre
        df_ref[...] = (c * dpre).astype(BF16)
        _accum(dg_ref, jnp.sum(r * xhat, axis=0, keepdims=True), first)
        _accum(db_ref, jnp.sum(r, axis=0, keepdims=True), first)

    return (fn, [res, f], [g, b], (F32, BF16), (1, 1))


ROW_TM = 1024


def _rows(body, T, tm, ins, outs, name, accs=()):
    in_specs, args = [], []
    for arr, w, cb in ins:
        if w is None:
            in_specs.append(pl.BlockSpec(arr.shape, lambda i, _n=arr.ndim: (0,) * _n))
        else:
            in_specs.append(pl.BlockSpec((tm, w), lambda i, _cb=cb: (i, _cb)))
        args.append(arr)
    out_specs = [pl.BlockSpec((tm, w), lambda i: (i, 0)) for w, _ in outs]
    out_shape = [jax.ShapeDtypeStruct((T, w), dt) for w, dt in outs]
    for r, w in accs:
        out_specs.append(pl.BlockSpec((r, w), lambda i: (0, 0)))
        out_shape.append(jax.ShapeDtypeStruct((r, w), F32))
    return _pcall(
        body, grid=(T // tm,), in_specs=in_specs, out_specs=out_specs, out_shape=out_shape, name=name,
        compiler_params=pltpu.CompilerParams(dimension_semantics=("arbitrary",)),
    )(*args)


def _ln_bwd_rows(res, f, c, g, b, dh, T, name):
    fn = _ln_bwd_epilogue(res, f, c, g, b)[0]

    def body(r_ref, f_ref, g_ref, b_ref, d_ref, dr_ref, df_ref, dg_ref, db_ref):
        fn(d_ref[...], (r_ref, f_ref, g_ref, b_ref), (dr_ref, df_ref, dg_ref, db_ref), pl.program_id(0) == 0)

    return _rows(body, T, min(ROW_TM, T), [(res, D, 0), (f, D, 0), (g, None, None), (b, None, None), (dh, D, 0)],
                 [(D, F32), (D, BF16)], name, accs=[(1, D), (1, D)])


FFN_TN = 1408
FFN_TM = 512
LN_TM = 512


def _ffn_in(x, wt, name, order_after=None):
    T = x.shape[0]
    tm, tn = min(FFN_TM, T), FFN_TN
    nj = D_FF // tn
    emit_xb = x.dtype != BF16
    n_in = 3 + int(order_after is not None)

    def body(*refs):
        x_ref, wg_ref, wu_ref = refs[:3]
        hs_ref, gu_ref = refs[n_in], refs[n_in + 1]
        xv = x_ref[...].astype(BF16)
        gu = _dot(xv, jnp.concatenate([wg_ref[...], wu_ref[...]], axis=0), 1, 1)
        g, u = gu[:, :tn], gu[:, tn:]
        hs_ref[...] = (_silu(g) * u).astype(BF16)
        gu_ref[0] = g.astype(BF16)
        gu_ref[1] = u.astype(BF16)
        if emit_xb:
            refs[n_in + 2][...] = xv

    in_specs = [pl.BlockSpec((tm, D), lambda i, j: (i, 0)), pl.BlockSpec((tn, D), lambda i, j: (j, 0)),
                pl.BlockSpec((tn, D), lambda i, j: (j + nj, 0))]
    args = [x, wt, wt]
    if order_after is not None:
        in_specs.append(pl.BlockSpec(order_after.shape, lambda i, j: (0, 0)))
        args.append(order_after)
    out_specs = [pl.BlockSpec((tm, tn), lambda i, j: (i, j)), pl.BlockSpec((2, tm, tn), lambda i, j: (0, i, j))]
    out_shape = [jax.ShapeDtypeStruct((T, D_FF), BF16), jax.ShapeDtypeStruct((2, T, D_FF), BF16)]
    if emit_xb:
        out_specs.append(pl.BlockSpec((tm, D), lambda i, j: (i, 0)))
        out_shape.append(jax.ShapeDtypeStruct((T, D), BF16))
    return _pcall(
        body, grid=(T // tm, nj), in_specs=in_specs, out_specs=out_specs, out_shape=out_shape, name=name,
        compiler_params=pltpu.CompilerParams(dimension_semantics=("parallel", "arbitrary")),
    )(*args)


def _ffn_act_bwd(df, wo, gu, name):
    T = df.shape[0]
    tm, tn = min(FFN_TM, T), FFN_TN

    def body(d_ref, w_ref, gu_ref, o_ref):
        dhs = _dot(d_ref[...], w_ref[...], 1, 1)
        g, u = gu_ref[0].astype(F32), gu_ref[1].astype(F32)
        sig = _sigmoid(g)
        act = g * sig
        o_ref[0] = (dhs * u * (sig + act * (1.0 - sig))).astype(BF16)
        o_ref[1] = (dhs * act).astype(BF16)

    return _pcall(
        body, grid=(T // tm, D_FF // tn),
        in_specs=[pl.BlockSpec((tm, D), lambda i, j: (i, 0)), pl.BlockSpec((tn, D), lambda i, j: (j, 0)),
                  pl.BlockSpec((2, tm, tn), lambda i, j: (0, i, j))],
        out_specs=pl.BlockSpec((2, tm, tn), lambda i, j: (0, i, j)),
        out_shape=jax.ShapeDtypeStruct((2, T, D_FF), BF16), name=name,
        compiler_params=pltpu.CompilerParams(dimension_semantics=("parallel", "parallel")),
    )(df, wo, gu)


def _rope128(t, cos, sin_s):
    return t * cos + pltpu.roll(t, 64, 1) * sin_s


def _rope128_t(g, cos, sin_s):
    return g * cos - pltpu.roll(g, 64, 1) * sin_s


def _partner32(t):
    lane = lax.broadcasted_iota(jnp.int32, t.shape, 1)
    return jnp.where((lane & 32) == 0, pltpu.roll(t, 96, 1), pltpu.roll(t, 32, 1))


def _rope64(t, cos, sin_s):
    return t * cos + _partner32(t) * sin_s


def _rope64_t(g, cos, sin_s):
    return g * cos - _partner32(g) * sin_s


def _mixer_prep_fwd(proj, tabs, qg, kvg, T):
    cos128, sin128, cos64, sin64 = tabs

    def body(rq_ref, rk_ref, cq_ref, ckv_ref, kpe_ref, c1_ref, s1_ref, c2_ref, s2_ref, qg_ref, kvg_ref,
             oq_ref, ok_ref, oqn_ref, okvn_ref, okpe_ref):
        c1, s1 = c1_ref[...], s1_ref[...]
        for h in range(HEADS):
            sl = slice(h * RET_DK, (h + 1) * RET_DK)
            oq_ref[:, sl] = _rope128(rq_ref[:, sl].astype(F32), c1, s1).astype(BF16)
            ok_ref[:, sl] = (_rope128(rk_ref[:, sl].astype(F32), c1, s1) * RET_SCALE).astype(BF16)
        oqn_ref[...] = _rms(cq_ref[...].astype(F32), qg_ref[...]).astype(BF16)
        okvn_ref[...] = _rms(ckv_ref[...].astype(F32), kvg_ref[...]).astype(BF16)
        okpe_ref[...] = _rope64(kpe_ref[...].astype(F32), c2_ref[...], s2_ref[...]).astype(BF16)

    ins = [(proj, 1024, 0), (proj, 1024, 1), (proj, 256, P_CQ // 256), (proj, 256, P_CKV // 256),
           (proj, 128, P_KPE // 128), (cos128, 128, 0), (sin128, 128, 0), (cos64, 128, 0), (sin64, 128, 0),
           (qg, None, None), (kvg, None, None)]
    outs = [(1024, BF16), (1024, BF16), (LORA, BF16), (LORA, BF16), (128, BF16)]
    return _rows(body, T, min(ROW_TM, T), ins, outs, "mixer_prep_fwd")


def _rms_bwd(proj, dqn, dkvn, dkpe_all, tabs, qg, kvg, T):
    _, _, cos64, sin64 = tabs

    def body(cq_ref, ckv_ref, dq_ref, dkv_ref, dk_ref, c2_ref, s2_ref, qg_ref, kvg_ref, o_ref, dqg_ref, dkvg_ref):
        _, vjp = jax.vjp(_rms, cq_ref[...].astype(F32), qg_ref[...])
        dx, dg = vjp(dq_ref[...])
        o_ref[:, 0:LORA] = dx.astype(BF16)
        _accum(dqg_ref, dg)
        _, vjp = jax.vjp(_rms, ckv_ref[...].astype(F32), kvg_ref[...])
        dx, dg = vjp(dkv_ref[...])
        o_ref[:, P_CKV - P_CQ:P_CKV - P_CQ + LORA] = dx.astype(BF16)
        _accum(dkvg_ref, dg)
        g = dk_ref[:, 0:128]
        for h in range(1, HEADS):
            g = g + dk_ref[:, h * 128:(h + 1) * 128]
        lane = lax.broadcasted_iota(jnp.int32, g.shape, 1)
        g = jnp.where(lane < MLA_ROPE, g, 0.0)
        o_ref[:, P_KPE - P_CQ:P_KPE - P_CQ + 128] = _rope64_t(g, c2_ref[...], s2_ref[...]).astype(BF16)
        o_ref[:, P_KPE - P_CQ + 128:] = jnp.zeros((g.shape[0], P_W - P_KPE - 128), BF16)

    ins = [(proj, 256, P_CQ // 256), (proj, 256, P_CKV // 256), (dqn, LORA, 0), (dkvn, LORA, 0), (dkpe_all, 1024, 0),
           (cos64, 128, 0), (sin64, 128, 0), (qg, None, None), (kvg, None, None)]
    return _rows(body, T, min(ROW_TM, T), ins, [(P_W - P_CQ, BF16)], "rms_bwd", accs=[(1, LORA), (1, LORA)])


def _gn_gate_bwd(y, rg, g, d):
    xc = y - jnp.mean(y, axis=-1, keepdims=True)
    rstd = lax.rsqrt(jnp.mean(jnp.square(xc), axis=-1, keepdims=True) + EPS)
    xhat = xc * rstd
    sig = _sigmoid(rg)
    act = rg * sig
    dn = d * act
    drg = d * (xhat * g) * (sig + act * (1.0 - sig))
    dxh = dn * g
    dy = rstd * (dxh - jnp.mean(dxh, axis=-1, keepdims=True) - xhat * jnp.mean(dxh * xhat, axis=-1, keepdims=True))
    return dy, drg, jnp.sum(dn * xhat, axis=0, keepdims=True)


def _gn_gate(y, rg, g):
    mu = jnp.mean(y, axis=-1, keepdims=True)
    var = jnp.mean(jnp.square(y - mu), axis=-1, keepdims=True)
    return _silu(rg) * ((y - mu) * lax.rsqrt(var + EPS) * g)


def _q_assemble_epilogue(tabs):
    _, _, cos64, sin64 = tabs

    def fn(q, ins, outs, first):
        c_ref, s_ref = ins
        on_ref, op_ref = outs
        on_ref[...] = (q[:, :1024] * Q_PRESCALE).astype(BF16)
        c, s = c_ref[...], s_ref[...]
        lane = lax.broadcasted_iota(jnp.int32, c.shape, 1)
        for j in range(HEADS // 2):
            r = _rope64(q[:, 1024 + 128 * j:1024 + 128 * (j + 1)], c, s) * Q_PRESCALE
            op_ref[:, 256 * j:256 * j + 128] = jnp.where(lane < 64, r, 0.0).astype(BF16)
            op_ref[:, 256 * j + 128:256 * j + 256] = jnp.where(lane < 64, pltpu.roll(r, 64, 1), 0.0).astype(BF16)

    return (fn, [(cos64, 0, 128), (sin64, 0, 128)], [], ((BF16, 1024), (BF16, 1024)), ())


def _q_assemble_bwd(dqn, dqpe, tabs, T):
    _, _, cos64, sin64 = tabs

    def body(dn_ref, dp_ref, c_ref, s_ref, on_ref, op_ref):
        on_ref[...] = dn_ref[...].astype(BF16)
        c, s = c_ref[...], s_ref[...]
        lane = lax.broadcasted_iota(jnp.int32, c.shape, 1)
        for j in range(HEADS // 2):
            g = jnp.where(lane < 64, dp_ref[:, 256 * j:256 * j + 128], pltpu.roll(dp_ref[:, 256 * j + 128:256 * j + 256], 64, 1))
            op_ref[:, 128 * j:128 * (j + 1)] = _rope64_t(g, c, s).astype(BF16)

    return _rows(body, T, min(ROW_TM, T), [(dqn, 1024, 0), (dqpe, 1024, 0), (cos64, 128, 0), (sin64, 128, 0)],
                 [(1024, BF16), (512, BF16)], "q_assemble_bwd")


def _mix_fn(gr, gm, yr, ym):
    return _sigmoid(gr) * yr + _sigmoid(gm) * ym


def _mix_epilogue(proj, y_ret):
    def fn(r, ins, outs, first):
        gr_ref, gm_ref, yr_ref = ins
        ym_ref, mix_ref = outs
        ym_ref[...] = r.astype(BF16)
        mix_ref[...] = _mix_fn(gr_ref[...].astype(F32), gm_ref[...].astype(F32), yr_ref[...].astype(F32), r).astype(BF16)

    return (fn, [(proj, P_GR // D), (proj, P_GM // D), y_ret], [], (BF16, BF16), ())


def _mix_bwd_epilogue(proj, y_ret, y_mla):
    def fn(r, ins, outs, first):
        gr_ref, gm_ref, yr_ref, ym_ref = ins
        dgate_ref, dyr_ref, dym_ref = outs
        sr, sm = _sigmoid(gr_ref[...].astype(F32)), _sigmoid(gm_ref[...].astype(F32))
        dgate_ref[:, 0:D] = (r * yr_ref[...].astype(F32) * (sr * (1.0 - sr))).astype(BF16)
        dgate_ref[:, D:2 * D] = (r * ym_ref[...].astype(F32) * (sm * (1.0 - sm))).astype(BF16)
        dyr_ref[...] = (r * sr).astype(BF16)
        dym_ref[...] = (r * sm).astype(BF16)

    return (fn, [(proj, P_GR // D), (proj, P_GM // D), y_ret, y_mla], [], ((BF16, 2 * D), BF16, BF16), ())


def _head_epilogue(h3, pp, tgt, g, b):
    def fn(r, ins, outs, first):
        h_ref, pp_ref, t_ref, g_ref, b_ref = ins
        dh_ref, dgl_ref, dpp_ref, dg_ref, db_ref, loss_ref = outs

        sg, pp, gain = _sigmoid(r), pp_ref[...], g_ref[...]
        pre = ALPHA * h_ref[...] + sg * pp
        xc = pre - jnp.mean(pre, axis=-1, keepdims=True)
        rstd = lax.rsqrt(jnp.mean(jnp.square(xc), axis=-1, keepdims=True) + EPS)
        xhat = xc * rstd
        err = xhat * gain + b_ref[...] - t_ref[...]
        dy = err * (1.0 / D)
        dyg = dy * gain
        dpre = rstd * (dyg - jnp.mean(dyg, axis=-1, keepdims=True) - xhat * jnp.mean(dyg * xhat, axis=-1, keepdims=True))
        dh_ref[...] = ALPHA * dpre
        dgl_ref[...] = (dpre * pp * (sg * (1.0 - sg))).astype(BF16)
        dpp_ref[...] = (dpre * sg).astype(BF16)
        _accum(dg_ref, jnp.sum(dy * xhat, axis=0, keepdims=True), first)
        _accum(db_ref, jnp.sum(dy, axis=0, keepdims=True), first)
        part = 0.5 * jnp.sum(jnp.mean(jnp.square(err), axis=-1, keepdims=True), axis=0, keepdims=True)
        _accum(loss_ref, jnp.broadcast_to(part, loss_ref.shape), first)

    return (fn, [h3, pp, tgt], [g, b], (F32, BF16, BF16), (1, 1, 8))


def _delta_epilogue(o):
    def fn(r, ins, outs, first):
        (o_ref,) = ins
        db_ref, dl_ref = outs
        db_ref[...] = r.astype(BF16)
        for h in range(HEADS):
            sl = slice(h * MLA_DV, (h + 1) * MLA_DV)
            dl = jnp.sum(r[:, sl] * o_ref[:, sl].astype(F32), axis=-1, keepdims=True)
            dl_ref[:, sl] = jnp.broadcast_to(dl, (r.shape[0], MLA_DV))

    return (fn, [o], [], (BF16, F32), ())


def _ret_consts():
    L = RET_L
    lg = jnp.log(1.0 - 2.0 ** (-5.0 - jnp.arange(HEADS, dtype=F32)))[:, None, None]
    idx = jnp.arange(L, dtype=F32)
    ch = jnp.arange(L) // CHUNK
    dist = idx[:, None] - idx[None, :]
    same = (ch[:, None] == ch[None, :])[None]
    earlier = (ch[None, :] < ch[:, None])[None]
    dm = jnp.where(same, jnp.exp(lg * jnp.abs(dist)[None]), jnp.where(earlier, jnp.exp(lg * dist[None]), 0.0))
    xi = jnp.broadcast_to(jnp.exp(lg * (idx + 1.0)[None, :, None]), (HEADS, L, 128))
    zeta = jnp.broadcast_to(jnp.exp(lg * (L - 1.0 - idx)[None, :, None]), (HEADS, L, 128))
    gl = jnp.broadcast_to(jnp.exp(lg * float(L)), (HEADS, 8, 128))
    return dm.astype(F32), xi.astype(F32), zeta.astype(F32), gl.astype(F32)


def _whole(arr):
    return pl.BlockSpec(arr.shape, lambda n, _nd=arr.ndim: (0,) * _nd)


def _ret_fwd(q, k, v, proj, gn_g, consts, T):
    dm, xi, zeta, gl = consts
    L = RET_L
    n_sc = T // L

    def body(q_ref, k_ref, v_ref, rg_ref, g_ref, dm_ref, xi_ref, ze_ref, gl_ref, y_ref, yr_ref, s_ref, st_ref):
        @pl.when(pl.program_id(0) == 0)
        def _():
            st_ref[...] = jnp.zeros_like(st_ref)

        for h in range(HEADS):
            ks, vs = slice(h * RET_DK, (h + 1) * RET_DK), slice(h * RET_DV, (h + 1) * RET_DV)
            qq, kk, vv = q_ref[:, ks], k_ref[:, ks], v_ref[:, vs]
            st = st_ref[h]
            s_ref[h, 0] = st
            p = (_dot(qq, kk, 1, 1) * dm_ref[h]).astype(BF16)
            cross = _dot(qq, st.astype(BF16), 1, 0)
            xi_c = jnp.concatenate([xi_ref[h], xi_ref[h]], axis=1)
            y = _dot(p, vv, 1, 0) + cross * xi_c
            y_ref[:, vs] = y
            yr_ref[:, vs] = _gn_gate(y, rg_ref[:, vs].astype(F32), g_ref[:, vs]).astype(BF16)
            kz = (kk.astype(F32) * ze_ref[h]).astype(BF16)
            gl2 = jnp.concatenate([gl_ref[h, 0:1, :], gl_ref[h, 0:1, :]], axis=1)
            st_ref[h] = st * gl2 + _dot(kz, vv, 0, 0)

    return _pcall(
        body, grid=(n_sc,),
        in_specs=[pl.BlockSpec((L, 1024), lambda n: (n, 0)), pl.BlockSpec((L, 1024), lambda n: (n, 0)),
                  pl.BlockSpec((L, 2048), lambda n: (n, P_RV // 2048)), pl.BlockSpec((L, 2048), lambda n: (n, P_RG // 2048)),
                  _whole(gn_g), _whole(dm), _whole(xi), _whole(zeta), _whole(gl)],
        out_specs=[pl.BlockSpec((L, 2048), lambda n: (n, 0)), pl.BlockSpec((L, 2048), lambda n: (n, 0)),
                   pl.BlockSpec((HEADS, 1, 128, 256), lambda n: (0, n, 0, 0))],
        out_shape=[jax.ShapeDtypeStruct((T, HEADS * RET_DV), F32), jax.ShapeDtypeStruct((T, HEADS * RET_DV), BF16),
                   jax.ShapeDtypeStruct((HEADS, n_sc, 128, 256), F32)],
        scratch_shapes=[pltpu.VMEM((HEADS, 128, 256), F32)], name="ret_fwd",
        compiler_params=pltpu.CompilerParams(dimension_semantics=("arbitrary",)),
    )(q, k, v, proj, gn_g, dm, xi, zeta, gl)


def _ret_bwd(q, k, v, y, proj, gn_g, states, dyr, consts, tabs, T):
    dm, xi, zeta, gl = consts
    cos128, sin128, _, _ = tabs
    L = RET_L
    n_sc = T // L

    def body(q_ref, k_ref, v_ref, y_ref, rg_ref, g_ref, d_ref, s_ref, dm_ref, xi_ref, ze_ref, gl_ref, c_ref, sn_ref,
             dp_ref, dg_ref, gs_ref):
        @pl.when(pl.program_id(0) == 0)
        def _():
            gs_ref[...] = jnp.zeros_like(gs_ref)

        c, sn = c_ref[...], sn_ref[...]
        dgs = []
        for h in range(HEADS):
            ks, vs = slice(h * RET_DK, (h + 1) * RET_DK), slice(h * RET_DV, (h + 1) * RET_DV)
            dy, drg, dg = _gn_gate_bwd(y_ref[:, vs], rg_ref[:, vs].astype(F32), g_ref[:, vs], d_ref[:, vs].astype(F32))
            dp_ref[:, P_RG + h * RET_DV:P_RG + (h + 1) * RET_DV] = drg.astype(BF16)
            dgs.append(dg)
            qq, kk, vv, dyy = q_ref[:, ks], k_ref[:, ks], v_ref[:, vs], dy.astype(BF16)
            dmm = dm_ref[h]
            gb = gs_ref[h].astype(BF16)
            sb = s_ref[h, 0].astype(BF16)
            xi_c = jnp.concatenate([xi_ref[h], xi_ref[h]], axis=1)
            pb = (_dot(qq, kk, 1, 1) * dmm).astype(BF16)
            kz = (kk.astype(F32) * ze_ref[h]).astype(BF16)
            dp_ref[:, P_RV + h * RET_DV:P_RV + (h + 1) * RET_DV] = (_dot(pb, dyy, 0, 0) + _dot(kz, gb, 1, 0)).astype(BF16)
            da = (_dot(dyy, vv, 1, 1) * dmm).astype(BF16)
            dyx = (dyy.astype(F32) * xi_c).astype(BF16)
            dq = _dot(da, kk, 1, 0) + _dot(dyx, sb, 1, 1)
            dk = _dot(da, qq, 0, 0) + _dot(vv, gb, 1, 1) * ze_ref[h]
            dp_ref[:, P_RQ + h * RET_DK:P_RQ + (h + 1) * RET_DK] = _rope128_t(dq, c, sn).astype(BF16)
            dp_ref[:, P_RK + h * RET_DK:P_RK + (h + 1) * RET_DK] = (_rope128_t(dk, c, sn) * RET_SCALE).astype(BF16)
            gl2 = jnp.concatenate([gl_ref[h, 0:1, :], gl_ref[h, 0:1, :]], axis=1)
            gs_ref[h] = gs_ref[h] * gl2 + _dot(qq, dyx, 0, 0)
        _accum(dg_ref, jnp.concatenate(dgs, axis=1))

    rev = lambda n: n_sc - 1 - n
    return _pcall(
        body, grid=(n_sc,),
        in_specs=[pl.BlockSpec((L, 1024), lambda n: (rev(n), 0)), pl.BlockSpec((L, 1024), lambda n: (rev(n), 0)),
                  pl.BlockSpec((L, 2048), lambda n: (rev(n), P_RV // 2048)), pl.BlockSpec((L, 2048), lambda n: (rev(n), 0)),
                  pl.BlockSpec((L, 2048), lambda n: (rev(n), P_RG // 2048)), _whole(gn_g),
                  pl.BlockSpec((L, 2048), lambda n: (rev(n), 0)),
                  pl.BlockSpec((HEADS, 1, 128, 256), lambda n: (0, rev(n), 0, 0)),
                  _whole(dm), _whole(xi), _whole(zeta), _whole(gl),
                  pl.BlockSpec((L, 128), lambda n: (rev(n), 0)), pl.BlockSpec((L, 128), lambda n: (rev(n), 0))],
        out_specs=[pl.BlockSpec((L, P_GR), lambda n: (rev(n), 0)), pl.BlockSpec((1, 2048), lambda n: (0, 0))],
        out_shape=[jax.ShapeDtypeStruct((T, P_GR), BF16), jax.ShapeDtypeStruct((1, 2048), F32)],
        scratch_shapes=[pltpu.VMEM((HEADS, 128, 256), F32)], name="ret_bwd",
        compiler_params=pltpu.CompilerParams(dimension_semantics=("arbitrary",)),
    )(q, k, v, y, proj, gn_g, dyr, states, dm, xi, zeta, gl, cos128, sin128)


def _diag_mask(nrows, ncols, row0):
    row = row0 + lax.broadcasted_iota(jnp.int32, (nrows, ncols), 0)
    col = lax.broadcasted_iota(jnp.int32, (nrows, ncols), 1)
    return lax.shift_right_logical(col, 6) <= lax.shift_right_logical(row, 6)


def _diag_spans(t):
    h = t // 2
    return ((0, h, h), (h, h, t)) if h % 128 == 0 else ((0, t, t),)


def _tri_steps(nb, by_key):
    if by_key:
        pairs = [(i, j) for j in range(nb) for i in range(j, nb)]
    else:
        pairs = [(i, j) for i in range(nb) for j in range(i + 1)]
    return jnp.array([a for a, _ in pairs], jnp.int32), jnp.array([b for _, b in pairs], jnp.int32)


def _attn_fwd(qn, qpe, kv, kpe, T):
    t = min(ATT_TF, T)
    nb = T // t
    ii, jj = _tri_steps(nb, by_key=False)

    hp = ATT_HP
    w = 128 * hp

    def body(ii_ref, jj_ref, qn_ref, qp_ref, kn_ref, kp_ref, v_ref, o_ref, lse_ref, m_sc, l_sc, acc_sc):
        st = pl.program_id(1)
        i, j = ii_ref[st], jj_ref[st]

        @pl.when(j == 0)
        def _():
            m_sc[...] = jnp.full_like(m_sc, NEG)
            l_sc[...] = jnp.zeros_like(l_sc)
            acc_sc[...] = jnp.zeros_like(acc_sc)

        def update(diag):
            kp = kp_ref[...]
            spans = _diag_spans(t) if diag else ((0, t, t),)
            for hh in range(hp):
                sl = slice(128 * hh, 128 * (hh + 1))
                for row0, nr, nkeys in spans:
                    rows = slice(row0, row0 + nr)
                    q = jnp.concatenate([qn_ref[rows, sl], qp_ref[rows, sl]], axis=1)
                    k = jnp.concatenate([kn_ref[:nkeys, sl], kp[:nkeys]], axis=1)
                    s = _dot(q, k, 1, 1)
                    if diag:
                        s = jnp.where(_diag_mask(nr, nkeys, row0), s, NEG)
                    m_prev = m_sc[rows, sl]
                    m_new = jnp.maximum(m_prev, jnp.max(s, axis=1, keepdims=True))
                    a = jnp.exp2(m_prev - m_new)
                    p = jnp.exp2(s - m_new[:, 0:1])
                    l_sc[rows, sl] = a * l_sc[rows, sl] + jnp.sum(p, axis=1, keepdims=True)
                    acc_sc[rows, sl] = a * acc_sc[rows, sl] + _dot(p.astype(BF16), v_ref[:nkeys, sl], 1, 0)
                    m_sc[rows, sl] = m_new

        @pl.when(j < i)
        def _():
            update(False)

        @pl.when(j == i)
        def _():
            update(True)
            o_ref[...] = (acc_sc[...] / l_sc[...]).astype(BF16)
            lse_ref[...] = m_sc[...] + jnp.log2(l_sc[...])

    qs = pl.BlockSpec((t, w), lambda h, s, ii, jj: (ii[s], h))
    grid_spec = pltpu.PrefetchScalarGridSpec(
        num_scalar_prefetch=2, grid=(HEADS // hp, int(ii.shape[0])),
        in_specs=[qs, qs, pl.BlockSpec((t, w), lambda h, s, ii, jj: (jj[s], h)), pl.BlockSpec((t, 128), lambda h, s, ii, jj: (jj[s], 0)),
                  pl.BlockSpec((t, w), lambda h, s, ii, jj: (jj[s], HEADS // hp + h))],
        out_specs=[qs, qs],
        scratch_shapes=[pltpu.VMEM((t, w), F32), pltpu.VMEM((t, w), F32), pltpu.VMEM((t, w), F32)])
    return _pcall(
        body, grid_spec=grid_spec, out_shape=[jax.ShapeDtypeStruct((T, D), BF16), jax.ShapeDtypeStruct((T, D), F32)], name="attn_fwd",
        compiler_params=pltpu.CompilerParams(dimension_semantics=("arbitrary", "arbitrary")),
    )(ii, jj, qn, qpe, kv, kpe, kv)


def _attn_bwd(qn, qpe, kv, kpe, do, lse, delta, T):
    t = min(ATT_TB, T)
    nb = T // t
    ii, jj = _tri_steps(nb, by_key=True)

    def body(ii_ref, jj_ref, qn_ref, qp_ref, kn_ref, kp_ref, v_ref, do_ref, lse_ref, dl_ref,
             dqn_ref, dqp_ref, dkn_ref, dkp_ref, dv_ref, dk_sc, dv_sc):
        st = pl.program_id(1)
        i, j = ii_ref[st], jj_ref[st]

        @pl.when(st == 0)
        def _():
            dqn_ref[...] = jnp.zeros_like(dqn_ref)
            dqp_ref[...] = jnp.zeros_like(dqp_ref)

        @pl.when(i == j)
        def _():
            dk_sc[...] = jnp.zeros_like(dk_sc)
            dv_sc[...] = jnp.zeros_like(dv_sc)

        def update(diag):
            for row0, nr, nkeys in (_diag_spans(t) if diag else ((0, t, t),)):
                rows = slice(row0, row0 + nr)
                q = jnp.concatenate([qn_ref[rows, :], qp_ref[rows, :]], axis=1)
                k = jnp.concatenate([kn_ref[:nkeys, :], kp_ref[:nkeys, :]], axis=1)
                dob = do_ref[rows, :]
                s = _dot(q, k, 1, 1)
                if diag:
                    s = jnp.where(_diag_mask(nr, nkeys, row0), s, NEG)
                p = jnp.exp2(s - lse_ref[rows, 0:1])
                dv_sc[:nkeys, :] += _dot(p.astype(BF16), dob, 0, 0)
                dp = _dot(dob, v_ref[:nkeys, :], 1, 1)
                ds = (p * (dp - dl_ref[rows, 0:1])).astype(BF16)
                dk_sc[:nkeys, :] += _dot(ds, q, 0, 0)
                dq = _dot(ds, k, 1, 0) * MLA_SCALE
                grows = pl.ds(pl.multiple_of(i * t + row0, nr), nr)
                dqn_ref[grows, :] += dq[:, :128]
                dqp_ref[grows, :] += dq[:, 128:]

        @pl.when(i > j)
        def _():
            update(False)

        @pl.when(i == j)
        def _():
            update(True)

        @pl.when(i == nb - 1)
        def _():
            dkn_ref[...] = (dk_sc[:, :128] * (1.0 / LOG2E)).astype(BF16)
            dkp_ref[...] = dk_sc[:, 128:] * (1.0 / LOG2E)
            dv_ref[...] = dv_sc[...].astype(BF16)

    qs = pl.BlockSpec((t, 128), lambda h, s, ii, jj: (ii[s], h))
    ks = pl.BlockSpec((t, 128), lambda h, s, ii, jj: (jj[s], h))
    hs = pl.BlockSpec((T, 128), lambda h, s, ii, jj: (0, h))
    grid_spec = pltpu.PrefetchScalarGridSpec(
        num_scalar_prefetch=2, grid=(HEADS, int(ii.shape[0])),
        in_specs=[qs, qs, ks, pl.BlockSpec((t, 128), lambda h, s, ii, jj: (jj[s], 0)),
                  pl.BlockSpec((t, 128), lambda h, s, ii, jj: (jj[s], HEADS + h)), qs, qs, qs],
        out_specs=[hs, hs, ks, ks, ks],
        scratch_shapes=[pltpu.VMEM((t, 256), F32), pltpu.VMEM((t, 128), F32)])
    return _pcall(
        body, grid_spec=grid_spec,
        out_shape=[jax.ShapeDtypeStruct((T, D), F32), jax.ShapeDtypeStruct((T, D), F32), jax.ShapeDtypeStruct((T, D), BF16),
                   jax.ShapeDtypeStruct((T, D), F32), jax.ShapeDtypeStruct((T, D), BF16)], name="attn_bwd",
        compiler_params=pltpu.CompilerParams(dimension_semantics=("arbitrary", "arbitrary")),
    )(ii, jj, qn, qpe, kv, kpe, kv, do, lse, delta)


def _mesh_pos():
    x, y, c = lax.axis_index("x"), lax.axis_index("y"), lax.axis_index("c")
    return x, y, c, 4 * x + 2 * y + c


def _peer(x, y, c, k):
    px, py, pc = (x + ((k >> 2) & 1)) % 2, (y + ((k >> 1) & 1)) % 2, (c + (k & 1)) % 2
    return (px, py, pc), 4 * px + 2 * py + pc


_ANY = pl.BlockSpec(memory_space=pl.ANY)


def _rcopy(src, dst, send_sems, recv_sems, k, to):
    return pltpu.make_async_remote_copy(src_ref=src, dst_ref=dst, send_sem=send_sems.at[k], recv_sem=recv_sems.at[k],
                                        device_id=to, device_id_type=pl.DeviceIdType.MESH)


def _all_gather(wsh, ssh):
    def body(w_ref, s_ref, wall_ref, sall_ref, send_sems, recv_sems, loc_sems):
        x, y, c, me = _mesh_pos()
        sib = (x, y, 1 - c)
        chips = [(1 - x, y), (x, 1 - y), (1 - x, 1 - y)]
        slot = lambda px, py, pc: 4 * px + 2 * py + pc
        loc = [pltpu.make_async_copy(s_ref, sall_ref.at[me], loc_sems.at[0])]
        for cp in loc:
            cp.start()
        sends, fwd_waits = [], []
        for n, (src, dst) in enumerate(((w_ref, wall_ref), (s_ref, sall_ref))):
            o = 7 * n
            sends.append(_rcopy(src, dst.at[me], send_sems, recv_sems, o, sib))
            for j, chip in enumerate(chips):
                sends.append(_rcopy(src, dst.at[me], send_sems, recv_sems, o + 1 + j, (*chip, c)))
        for cp in sends:
            cp.start()
        for n, (src, dst) in enumerate(((w_ref, wall_ref), (s_ref, sall_ref))):
            o = 7 * n
            for j, chip in enumerate(chips):
                got = dst.at[slot(*chip, c)]
                _rcopy(src, got, send_sems, recv_sems, o + 1 + j, sib).wait_recv()
                fw = _rcopy(got, got, send_sems, recv_sems, o + 4 + j, sib)
                fw.start()
                sends.append(fw)
            fwd_waits.append(_rcopy(src, dst.at[slot(x, y, 1 - c)], send_sems, recv_sems, o, sib))
            for j, chip in enumerate(chips):
                fwd_waits.append(_rcopy(src, dst.at[slot(*chip, 1 - c)], send_sems, recv_sems, o + 4 + j, sib))
        for cp in fwd_waits:
            cp.wait_recv()
        for cp in sends:
            cp.wait_send()
        for cp in loc:
            cp.wait()

    wall, sall = _pcall(
        body, in_specs=[_ANY, _ANY], out_specs=[_ANY, _ANY],
        out_shape=[jax.ShapeDtypeStruct((N_DEV,) + wsh.shape, wsh.dtype), jax.ShapeDtypeStruct((N_DEV,) + ssh.shape, ssh.dtype)],
        scratch_shapes=[pltpu.SemaphoreType.DMA((14,)), pltpu.SemaphoreType.DMA((14,)), pltpu.SemaphoreType.DMA((1,))],
        name="all_gather_weights",
    )(wsh, ssh)
    me = 4 * lax.axis_index("x") + 2 * lax.axis_index("y") + lax.axis_index("c")
    return lax.dynamic_update_index_in_dim(wall, wsh, me, axis=0), sall


_HBM = pl.BlockSpec(memory_space=pltpu.HBM)
_SEM = pl.BlockSpec(memory_space=pltpu.SEMAPHORE)
_EFFECT = pltpu.SideEffectType.DATAFLOW_SIDE_EFFECTING


def _other_chips(x, y):
    return [(1 - x, y), (x, 1 - y), (1 - x, 1 - y)]


def _gather_start(wsh, order_after):
    def body(w_ref, land_ref, dep_ref, send_sems, recv_sems, w_thru, land_thru, token):
        x, y, c, me = _mesh_pos()
        _rcopy(w_ref, land_ref.at[me], send_sems, recv_sems, 0, (x, y, 1 - c)).start()
        for j, chip in enumerate(_other_chips(x, y)):
            _rcopy(w_ref, land_ref.at[me], send_sems, recv_sems, 1 + j, (*chip, c)).start()
        token[...] = jnp.zeros_like(token)

    shape = (N_DEV,) + wsh.shape
    land = pltpu.with_memory_space_constraint(lax.empty(shape, wsh.dtype), pltpu.HBM)
    return _pcall(
        body, name="gather_start",
        out_shape=(pltpu.SemaphoreType.DMA((4,)), pltpu.SemaphoreType.DMA((4,)), pltpu.HBM(wsh.shape, wsh.dtype),
                   pltpu.HBM(shape, wsh.dtype), jax.ShapeDtypeStruct((8, 128), F32)),
        in_specs=(_HBM, _HBM, _ANY), out_specs=(_SEM, _SEM, _HBM, _HBM, pl.BlockSpec(memory_space=pltpu.VMEM)),
        input_output_aliases={0: 2, 1: 3}, compiler_params=pltpu.CompilerParams(has_side_effects=_EFFECT),
    )(pltpu.with_memory_space_constraint(wsh, pltpu.HBM), land, order_after)


def _gather_wait(send_sems, recv_sems, w_thru, land_thru, after):
    def body(w_ref, land_ref, send_sems, recv_sems, after_ref, w_out, land_out):
        x, y, c, _ = _mesh_pos()
        senders = [(x, y, 1 - c)] + [(*chip, c) for chip in _other_chips(x, y)]
        for k, (px, py, pc) in enumerate(senders):
            cp = _rcopy(w_ref, land_ref.at[4 * px + 2 * py + pc], send_sems, recv_sems, k, (px, py, pc))
            cp.wait_send()
            cp.wait_recv()

    return _pcall(
        body, name="gather_wait", out_shape=(pltpu.HBM(w_thru.shape, w_thru.dtype), pltpu.HBM(land_thru.shape, land_thru.dtype)),
        in_specs=(_HBM, _HBM, _SEM, _SEM, _ANY), out_specs=(_HBM, _HBM), input_output_aliases={0: 0, 1: 1},
        compiler_params=pltpu.CompilerParams(has_side_effects=_EFFECT),
    )(w_thru, land_thru, send_sems, recv_sems, after)


def _gather_finish(wsh, land):
    def body(land_ref, out_ref, send_sems, recv_sems):
        x, y, c, _ = _mesh_pos()
        sib = (x, y, 1 - c)
        sends, recvs = [], []
        for j, (px, py) in enumerate(_other_chips(x, y)):
            mine, theirs = 4 * px + 2 * py + c, 4 * px + 2 * py + (1 - c)
            sends.append(_rcopy(land_ref.at[mine], out_ref.at[mine], send_sems, recv_sems, j, sib))
            recvs.append(_rcopy(land_ref.at[theirs], out_ref.at[theirs], send_sems, recv_sems, j, sib))
        for cp in sends:
            cp.start()
        for cp in recvs:
            cp.wait_recv()
        for cp in sends:
            cp.wait_send()

    done = _pcall(
        body, in_specs=[_ANY], out_specs=_ANY, out_shape=jax.ShapeDtypeStruct(land.shape, land.dtype),
        input_output_aliases={0: 0},
        scratch_shapes=[pltpu.SemaphoreType.DMA((3,)), pltpu.SemaphoreType.DMA((3,))], name="gather_finish",
    )(land)
    me = 4 * lax.axis_index("x") + 2 * lax.axis_index("y") + lax.axis_index("c")
    return lax.dynamic_update_index_in_dim(done, wsh, me, axis=0)


_ROW_TILES = (736, 528, 400, 368, 352, 256, 128, 16, 8)


def _exchange_sibling(g4, name):
    def body(g_ref, r_ref, send_sems, recv_sems):
        x, y, c, _ = _mesh_pos()
        sib = (x, y, 1 - c)
        cps = [_rcopy(g_ref.at[q, 1 - c], r_ref.at[q], send_sems, recv_sems, q, sib) for q in range(4)]
        for cp in cps:
            cp.start()
        for cp in cps:
            cp.wait()

    n, _, R, C = g4.shape
    return _pcall(
        body, in_specs=[_ANY], out_specs=_ANY, out_shape=jax.ShapeDtypeStruct((n, R, C), g4.dtype),
        scratch_shapes=[pltpu.SemaphoreType.DMA((4,)), pltpu.SemaphoreType.DMA((4,))], name=name,
    )(g4)


def _sum_sibling(g4, rsib, cvec, name):
    n, _, R, C = g4.shape
    tr = _pick(R, _ROW_TILES)

    def body(c_ref, g_ref, r_ref, o_ref):
        o_ref[...] = (g_ref[...].astype(F32) + r_ref[...].astype(F32)).astype(o_ref.dtype)

    grid_spec = pltpu.PrefetchScalarGridSpec(
        num_scalar_prefetch=1, grid=(n, R // tr),
        in_specs=[pl.BlockSpec((None, None, tr, C), lambda q, i, cr: (q, cr[0], i, 0)), pl.BlockSpec((None, tr, C), lambda q, i, cr: (q, i, 0))],
        out_specs=pl.BlockSpec((None, tr, C), lambda q, i, cr: (q, i, 0)))
    return _pcall(body, grid_spec=grid_spec, out_shape=jax.ShapeDtypeStruct((n, R, C), g4.dtype), name=name)(cvec, g4, rsib)


def _chips_start(part, name, order_after=None):
    n_in = 2 + int(order_after is not None)

    def body(*refs):
        p_ref, land_ref = refs[:2]
        send_sems, recv_sems, token = refs[n_in], refs[n_in + 1], refs[n_in + 4]
        x, y, c, _ = _mesh_pos()
        myq = 2 * x + y
        for j, (px, py) in enumerate(_other_chips(x, y)):
            _rcopy(p_ref.at[2 * px + py], land_ref.at[myq], send_sems, recv_sems, j, (px, py, c)).start()
        token[...] = jnp.zeros_like(token)

    land = pltpu.with_memory_space_constraint(lax.empty(part.shape, part.dtype), pltpu.HBM)
    extra = () if order_after is None else (order_after,)
    return _pcall(
        body, name=name,
        out_shape=(pltpu.SemaphoreType.DMA((3,)), pltpu.SemaphoreType.DMA((3,)), pltpu.HBM(part.shape, part.dtype),
                   pltpu.HBM(part.shape, part.dtype), jax.ShapeDtypeStruct((8, 128), F32)),
        in_specs=(_HBM, _HBM) + (_ANY,) * len(extra),
        out_specs=(_SEM, _SEM, _HBM, _HBM, pl.BlockSpec(memory_space=pltpu.VMEM)),
        input_output_aliases={0: 2, 1: 3}, compiler_params=pltpu.CompilerParams(has_side_effects=_EFFECT),
    )(pltpu.with_memory_space_constraint(part, pltpu.HBM), land, *extra)


def _chips_wait(send_sems, recv_sems, p_thru, land_thru, after, name):
    def body(p_ref, land_ref, send_sems, recv_sems, after_ref, p_out, land_out):
        x, y, c, _ = _mesh_pos()
        for j, (px, py) in enumerate(_other_chips(x, y)):
            q = 2 * px + py
            cp = _rcopy(p_ref.at[q], land_ref.at[q], send_sems, recv_sems, j, (px, py, c))
            cp.wait_send()
            cp.wait_recv()

    return _pcall(
        body, name=name, out_shape=(pltpu.HBM(p_thru.shape, p_thru.dtype), pltpu.HBM(p_thru.shape, p_thru.dtype)),
        in_specs=(_HBM, _HBM, _SEM, _SEM, _ANY), out_specs=(_HBM, _HBM), input_output_aliases={0: 0, 1: 1},
        compiler_params=pltpu.CompilerParams(has_side_effects=_EFFECT),
    )(p_thru, land_thru, send_sems, recv_sems, after)


def _sum_chips(part, land, qvec, name):
    n, R, C = part.shape
    tr = _pick(R, _ROW_TILES)

    def body(q_ref, p_ref, *refs):
        o_ref = refs[n]
        acc = None
        for q in range(n):
            term = jnp.where(q_ref[0] == q, p_ref[...], refs[q][...]).astype(F32)
            acc = term if acc is None else acc + term
        o_ref[...] = acc

    def land_spec(q):
        return pl.BlockSpec((None, tr, C), lambda i, m: (jnp.where(m[0] == q, (q + 1) % n, q), i, 0))

    grid_spec = pltpu.PrefetchScalarGridSpec(
        num_scalar_prefetch=1, grid=(R // tr,),
        in_specs=[pl.BlockSpec((None, tr, C), lambda i, m: (m[0], i, 0))] + [land_spec(q) for q in range(n)],
        out_specs=pl.BlockSpec((tr, C), lambda i, m: (i, 0)))
    return _pcall(body, grid_spec=grid_spec, out_shape=jax.ShapeDtypeStruct((R, C), F32), name=name)(qvec, part, *([land] * n))


def _exchange_small(gsmall):
    def body(s_ref, srecv_ref, send_sems, recv_sems, loc_sem):
        x, y, c, me = _mesh_pos()
        loc = pltpu.make_async_copy(s_ref, srecv_ref.at[me], loc_sem.at[0])
        loc.start()
        sends, recvs = [], []
        for k in range(1, N_DEV):
            to, pidx = _peer(x, y, c, k)
            sends.append(_rcopy(s_ref, srecv_ref.at[me], send_sems, recv_sems, k - 1, to))
            recvs.append(_rcopy(s_ref, srecv_ref.at[pidx], send_sems, recv_sems, k - 1, to))
        for cp in sends:
            cp.start()
        for cp in recvs:
            cp.wait_recv()
        for cp in sends:
            cp.wait_send()
        loc.wait()

    return _pcall(
        body, in_specs=[_ANY], out_specs=_ANY, out_shape=jax.ShapeDtypeStruct((N_DEV,) + gsmall.shape, gsmall.dtype),
        scratch_shapes=[pltpu.SemaphoreType.DMA((7,)), pltpu.SemaphoreType.DMA((7,)), pltpu.SemaphoreType.DMA((1,))],
        name="exchange_small",
    )(gsmall)


def _sum_slots(recv, name):
    n, R, C = recv.shape
    tr = _pick(R, _ROW_TILES)

    def body(r_ref, o_ref):
        acc = r_ref[0].astype(F32)
        for s in range(1, n):
            acc = acc + r_ref[s].astype(F32)
        o_ref[...] = acc

    return _pcall(body, grid=(R // tr,), in_specs=[pl.BlockSpec((n, tr, C), lambda i: (0, i, 0))],
                  out_specs=pl.BlockSpec((tr, C), lambda i: (i, 0)), out_shape=jax.ShapeDtypeStruct((R, C), F32), name=name)(recv)


def _adamw(w, g, m, v, name, order_after=None):
    shape = w.shape
    w2, g2, m2, v2 = (a.reshape(-1, shape[-1]) for a in (w, g, m, v))
    R, C = w2.shape
    tr = _pick(R, (256, 128, 64, 32, 16, 8)) if R > 256 else R
    n_in = 4 + int(order_after is not None)

    def body(*refs):
        w_ref, g_ref, m_ref, v_ref = refs[:4]
        d_ref, nm_ref, nv_ref = refs[n_in:]
        gg = g_ref[...]
        nm = ADAM_B1 * m_ref[...] + (1.0 - ADAM_B1) * gg
        nv = ADAM_B2 * v_ref[...] + (1.0 - ADAM_B2) * jnp.square(gg)
        m_hat = nm / (1.0 - ADAM_B1 ** ADAM_STEP)
        v_hat = nv / (1.0 - ADAM_B2 ** ADAM_STEP)
        d_ref[...] = -ADAM_LR * (m_hat / (jnp.sqrt(v_hat) + ADAM_EPS) + ADAM_WD * w_ref[...])
        nm_ref[...] = nm
        nv_ref[...] = nv

    spec = pl.BlockSpec((tr, C), lambda i: (i, 0))
    in_specs, args = [spec] * 4, [w2, g2, m2, v2]
    if order_after is not None:
        in_specs.append(pl.BlockSpec(memory_space=pl.ANY))
        args.append(order_after)
    d, nm, nv = _pcall(body, grid=(R // tr,), in_specs=in_specs, out_specs=[spec] * 3,
                       out_shape=[jax.ShapeDtypeStruct((R, C), F32)] * 3, name=name)(*args)
    return d.reshape(shape), nm.reshape(shape), nv.reshape(shape)


def _to_rows(name, w):
    w = w[0]
    if name in ("ffn1_w_in", "ffn2_w_in", "w_in"):
        return w.T
    if name in ("w_uq", "w_ukv", "ple_w_proj"):
        return w.T.reshape(-1, 1024)
    return w


def _from_rows(name, g, shape):
    if name in ("ffn1_w_in", "ffn2_w_in", "w_in"):
        return g.T.reshape(shape)
    if name in ("w_uq", "w_ukv", "ple_w_proj"):
        return g.reshape(-1, shape[1]).T.reshape(shape)
    return g.reshape(shape)


def _unpack(wall, group):
    out, off = {}, 0
    for name, r in group:
        out[name] = wall[:, off:off + r, :].reshape(N_DEV * r, 1024)
        off += _pad16(r)
    return out


def _w_in_internal(wt):
    return jnp.concatenate([wt[0:6144], wt[6720:8768], wt[6144:6720], jnp.zeros((P_W - W_IN_COLS, 1024), wt.dtype)], axis=0)


def _rope_tables(positions):
    pos = positions[0].astype(F32)

    def cs(half):
        inv = jnp.tile(ROPE_BASE ** (-jnp.arange(half, dtype=F32) / half), 128 // half)
        sign = jnp.tile(jnp.concatenate([-jnp.ones((half,), F32), jnp.ones((half,), F32)]), 64 // half)
        ang = pos[:, None] * inv
        return jnp.cos(ang), jnp.sin(ang) * sign

    return cs(64) + cs(32)


def _local_step(x, p, positions, target, W, ln_g, ln_b, gn_g, qg, kvg, early=None, rest_weights=None, start_token=None):
    T = x.shape[0]
    tabs = _rope_tables(positions)
    rc = _ret_consts()
    lg = [ln_g[i:i + 1] for i in range(4)]
    lb = [ln_b[i:i + 1] for i in range(4)]
    pb = p.astype(BF16)

    hs1, gu1, *xb = _ffn_in(x, W["ffn1_w_in"], "ffn1_in", order_after=start_token)
    xb = xb[0] if xb else x
    f1, h1, h1b = _mm(hs1, W["ffn1_w_out"], name="ffn1_out", tm=LN_TM, epilogue=_ln_epilogue(x, 0.5, lg[0], lb[0]))
    if rest_weights is not None:
        W = {**W, **rest_weights(h1b)}
    w_in_t = _w_in_internal(W["w_in"])
    wuq = W["w_uq"].reshape(1536, LORA).reshape(HEADS, 192, LORA)
    wuq = jnp.concatenate([wuq[:, :128].reshape(1024, LORA), wuq[:, 128:].reshape(512, LORA)], axis=0)
    wukv = W["w_ukv"].reshape(2048, LORA).reshape(HEADS, 2, 128, LORA).transpose(1, 0, 2, 3).reshape(2048, LORA)
    wp_t = W["ple_w_proj"].reshape(1024, D_PLE)
    proj = _mm(h1b, w_in_t, tb=True, out_dtype=BF16, name="mixer_in")
    rq, rk, qn, kvn, kpe = _mixer_prep_fwd(proj, tabs, qg, kvg, T)
    y, yr, states = _ret_fwd(rq, rk, proj, proj, gn_g, rc, T)
    y_ret = _mm(yr, W["w_ret_o"], out_dtype=BF16, name="ret_o")
    qnope, qpe = _mm(qn, wuq, tb=True, name="mla_uq", tm=LN_TM, tn=1536, epilogue=_q_assemble_epilogue(tabs))
    kv = _mm(kvn, wukv, tb=True, out_dtype=BF16, name="mla_ukv")
    o, lse = _attn_fwd(qnope, qpe, kv, kpe, T)
    y_mla, mix = _mm(o, W["w_mla_o"], name="mla_o", tm=LN_TM, epilogue=_mix_epilogue(proj, y_ret))
    mixed, h2, h2b = _mm(mix, W["w_out"], name="mixer_out", tm=LN_TM, epilogue=_ln_epilogue(h1, 1.0, lg[1], lb[1]))
    hs2, gu2 = _ffn_in(h2b, W["ffn2_w_in"], "ffn2_in")
    f2, h3, h3b = _mm(hs2, W["ffn2_w_out"], name="ffn2_out", tm=LN_TM, epilogue=_ln_epilogue(h2, 0.5, lg[2], lb[2]))
    pp = _mm(pb, wp_t, tb=True, name="ple_proj")

    G = {}
    dh3_a, dgl, dpp, dg3, db3, loss = _mm(h3b, W["ple_w_gate"], name="ple_gate", tm=LN_TM,
                                          epilogue=_head_epilogue(h3, pp, target, lg[3], lb[3]))
    G["ple_w_gate"] = _mm(h3b, dgl, ta=True, name="d_ple_gate")
    G["ple_w_proj"] = _mm(dpp, pb, ta=True, name="d_ple_proj")
    dh2_a, df2, dg2, db2 = _mm(dgl, W["ple_w_gate"], tb=True, add=dh3_a, name="dh3", tm=LN_TM,
                               epilogue=_ln_bwd_epilogue(h2, f2, 0.5, lg[2], lb[2]))
    G["ffn2_w_out"] = _mm(hs2, df2, ta=True, name="d_ffn2_out")
    da2 = _ffn_act_bwd(df2, W["ffn2_w_out"], gu2, "ffn2_act_bwd")
    G["ffn2_w_in"] = _mm(da2, h2b, ta=True, name="d_ffn2_in")
    dh1_a, dmixed, dg1, db1 = _mm(da2, W["ffn2_w_in"], add=dh2_a, name="dh2", tm=LN_TM,
                                  epilogue=_ln_bwd_epilogue(h1, mixed, 1.0, lg[1], lb[1]))
    G["w_out"] = _mm(mix, dmixed, ta=True, name="d_mixer_out")
    dp_gate, dy_ret, dy_mla = _mm(dmixed, W["w_out"], tb=True, name="dmix", tm=LN_TM,
                                  epilogue=_mix_bwd_epilogue(proj, y_ret, y_mla))
    G["w_mla_o"] = _mm(o, dy_mla, ta=True, name="d_mla_o")
    dob, delta = _mm(dy_mla, W["w_mla_o"], tb=True, name="do", tm=LN_TM, epilogue=_delta_epilogue(o))
    dqn_f, dqpe_f, dkn, dkpe_all, dv = _attn_bwd(qnope, qpe, kv, kpe, dob, lse, delta, T)
    dq_n, dq_r = _q_assemble_bwd(dqn_f, dqpe_f, tabs, T)
    g_uq = jnp.concatenate([_mm(dq_n, qn, ta=True, name="d_uq_nope"), _mm(dq_r, qn, ta=True, name="d_uq_rope")], axis=0)
    g_uq = jnp.concatenate([g_uq[:1024].reshape(HEADS, 128, LORA), g_uq[1024:].reshape(HEADS, 64, LORA)], axis=1)
    G["w_uq"] = g_uq.reshape(1536 * LORA // 1024, 1024)
    dqn = _mm(dq_r, wuq[1024:], add=_mm(dq_n, wuq[:1024], name="dqn_a"), name="dqn_b")
    g_ukv = jnp.stack([_mm(dkn, kvn, ta=True, name="d_ukv_k"), _mm(dv, kvn, ta=True, name="d_ukv_v")], axis=0)
    G["w_ukv"] = g_ukv.reshape(2, HEADS, 128, LORA).transpose(1, 0, 2, 3).reshape(2048 * LORA // 1024, 1024)
    dkvn = _mm(dv, wukv[1024:], add=_mm(dkn, wukv[:1024], name="dkvn_a"), name="dkvn_b")
    dp_mla, dqg, dkvg = _rms_bwd(proj, dqn, dkvn, dkpe_all, tabs, qg, kvg, T)
    G["w_ret_o"] = _mm(yr, dy_ret, ta=True, name="d_ret_o")
    dyr = _mm(dy_ret, W["w_ret_o"], tb=True, out_dtype=BF16, name="dyr")
    dp_ret, dgn = _ret_bwd(rq, rk, proj, y, proj, gn_g, states, dyr, rc, tabs, T)
    g_ret, g_gate, g_mla = (_mm(dp, h1b, ta=True, name="d_mixer_in_" + n)
                            for dp, n in ((dp_ret, "ret"), (dp_gate, "gate"), (dp_mla, "mla")))
    G["w_in"] = jnp.concatenate([g_ret, g_mla[:W_IN_COLS - P_CQ], g_gate], axis=0)
    lg0 = lg[0] if early is None else lg[0] + early(G)[0:1, 0:1]
    dh1 = _mm_pieces([(dp_ret, w_in_t[:P_GR]), (dp_gate, w_in_t[P_GR:P_CQ]), (dp_mla, w_in_t[P_CQ:])], dh1_a, "dh1")
    dx_a, df1, dg0, db0 = _ln_bwd_rows(x, f1, 0.5, lg0, lb[0], dh1, T, "ln0_bwd")
    G["ffn1_w_out"] = _mm(hs1, df1, ta=True, name="d_ffn1_out")
    da1 = _ffn_act_bwd(df1, W["ffn1_w_out"], gu1, "ffn1_act_bwd")
    G["ffn1_w_in"] = _mm(da1, xb, ta=True, name="d_ffn1_in")
    grad_x = _mm(da1, W["ffn1_w_in"], add=dx_a, name="grad_x")

    small = dict(ln_g=jnp.concatenate([dg0, dg1, dg2, dg3], axis=0), ln_b=jnp.concatenate([db0, db1, db2, db3], axis=0),
                 ret_gn_g=dgn, q_norm_g=dqg, kv_norm_g=dkvg)
    return loss, grad_x, G, small


def kernel(x, p, positions, ln_g, ln_b, ffn1_w_in, ffn1_w_out, w_in, ret_gn_g, w_ret_o, q_norm_g, kv_norm_g, w_uq, w_ukv, w_mla_o, w_out, ffn2_w_in, ffn2_w_out, ple_w_gate, ple_w_proj, loss_target, m_ln_g, m_ln_b, m_ffn1_w_in, m_ffn1_w_out, m_w_in, m_ret_gn_g, m_w_ret_o, m_q_norm_g, m_kv_norm_g, m_w_uq, m_w_ukv, m_w_mla_o, m_w_out, m_ffn2_w_in, m_ffn2_w_out, m_ple_w_gate, m_ple_w_proj, v_ln_g, v_ln_b, v_ffn1_w_in, v_ffn1_w_out, v_w_in, v_ret_gn_g, v_w_ret_o, v_q_norm_g, v_kv_norm_g, v_w_uq, v_w_ukv, v_w_mla_o, v_w_out, v_ffn2_w_in, v_ffn2_w_out, v_ple_w_gate, v_ple_w_proj):
    names = ("ln_g", "ln_b", "ffn1_w_in", "ffn1_w_out", "w_in", "ret_gn_g", "w_ret_o", "q_norm_g", "kv_norm_g", "w_uq", "w_ukv",
             "w_mla_o", "w_out", "ffn2_w_in", "ffn2_w_out", "ple_w_gate", "ple_w_proj")
    ws = dict(zip(names, (ln_g, ln_b, ffn1_w_in, ffn1_w_out, w_in, ret_gn_g, w_ret_o, q_norm_g, kv_norm_g, w_uq, w_ukv, w_mla_o,
                          w_out, ffn2_w_in, ffn2_w_out, ple_w_gate, ple_w_proj)))
    ms = dict(zip(names, (m_ln_g, m_ln_b, m_ffn1_w_in, m_ffn1_w_out, m_w_in, m_ret_gn_g, m_w_ret_o, m_q_norm_g, m_kv_norm_g, m_w_uq,
                          m_w_ukv, m_w_mla_o, m_w_out, m_ffn2_w_in, m_ffn2_w_out, m_ple_w_gate, m_ple_w_proj)))
    vs = dict(zip(names, (v_ln_g, v_ln_b, v_ffn1_w_in, v_ffn1_w_out, v_w_in, v_ret_gn_g, v_w_ret_o, v_q_norm_g, v_kv_norm_g, v_w_uq,
                          v_w_ukv, v_w_mla_o, v_w_out, v_ffn2_w_in, v_ffn2_w_out, v_ple_w_gate, v_ple_w_proj)))

    parts = []
    for name, r in PACK:
        rows = _to_rows(name, ws[name])
        if _pad16(r) != r:
            rows = jnp.concatenate([rows, jnp.zeros((_pad16(r) - r, 1024), F32)], axis=0)
        parts.append(rows)
    wsh_first = jnp.concatenate(parts[:len(PACK_LATE)], axis=0).astype(BF16)
    wsh_rest = jnp.concatenate(parts[len(PACK_LATE):], axis=0).astype(BF16)
    ssh = jnp.concatenate([ln_g[0], ln_b[0]], axis=0)
    wall_first, sall = _all_gather(wsh_first, ssh)
    *gather_handles, start_token = _gather_start(wsh_rest, wall_first)
    W = _unpack(wall_first, PACK_LATE)
    ln_full = sall.reshape(N_DEV, 2, 4, 128).transpose(1, 2, 0, 3).reshape(2, 4, 1024)

    def rest_weights(after):
        w_thru, land = _gather_wait(*gather_handles, after)
        return _unpack(_gather_finish(w_thru, land), PACK_EARLY)

    cvec = lax.axis_index("c").astype(jnp.int32).reshape(1)
    qvec = (2 * lax.axis_index("x") + lax.axis_index("y")).astype(jnp.int32).reshape(1)

    def chip_partials(G, group, tag):
        gparts = []
        for name, r in group:
            g = G[name].reshape(N_DEV, r, 1024)
            if _pad16(r) != r:
                g = jnp.concatenate([g, jnp.zeros((N_DEV, _pad16(r) - r, 1024), g.dtype)], axis=1)
            gparts.append(g)
        gfull = jnp.concatenate(gparts, axis=1).astype(BF16)
        g4 = gfull.reshape(4, 2, gfull.shape[1], 1024)
        return _sum_sibling(g4, _exchange_sibling(g4, "exchange_sibling_" + tag), cvec, "sum_sibling_" + tag)

    in_flight = []

    def early(G):
        *handles, token = _chips_start(chip_partials(G, PACK_EARLY, "early"), "chips_start_early")
        in_flight.append(handles)
        return token

    loss_p, grad_x, G, small = _local_step(x[0], p[0, 0], positions, loss_target[0], W, ln_full[0], ln_full[1],
                                           ret_gn_g, q_norm_g, kv_norm_g, early=early, rest_weights=rest_weights,
                                           start_token=start_token)

    part_e, land_e = _chips_wait(*in_flight[0], grad_x, "chips_wait_early")
    gsh_early = _sum_chips(part_e, land_e, qvec, "sum_grads_early")
    pad256 = lambda a: jnp.concatenate([a, jnp.zeros((1, 1024 - a.shape[1]), F32)], axis=1)
    gsmall = jnp.concatenate([small["ln_g"], small["ln_b"], small["ret_gn_g"].reshape(2, 1024), pad256(small["q_norm_g"]),
                              pad256(small["kv_norm_g"]), pad256(loss_p[0:1, 0:128]),
                              jnp.zeros((SMALL_ROWS - 13, 1024), F32)], axis=0)
    srecv = _exchange_small(gsmall)
    part_l = chip_partials(G, PACK_LATE, "late")
    *late_handles, late_token = _chips_start(part_l, "chips_start_late", order_after=srecv)
    ssum = _sum_slots(srecv, "sum_small_grads")

    def unpack_grads(group, gsh):
        out, off = {}, 0
        for name, r in group:
            out[name] = _from_rows(name, gsh[off:off + r], ws[name].shape)
            off += _pad16(r)
        return out

    grads = unpack_grads(PACK_EARLY, gsh_early)
    me = 4 * lax.axis_index("x") + 2 * lax.axis_index("y") + lax.axis_index("c")
    grads["ln_g"] = lax.dynamic_slice(ssum[0:4], (0, me * 128), (4, 128)).reshape(1, 4, 128)
    grads["ln_b"] = lax.dynamic_slice(ssum[4:8], (0, me * 128), (4, 128)).reshape(1, 4, 128)
    grads["ret_gn_g"] = ssum[8:10].reshape(1, 2048)
    grads["q_norm_g"] = ssum[10:11, :256]
    grads["kv_norm_g"] = ssum[11:12, :256]

    delta, new_m, new_v = {}, {}, {}
    late_names = [n for n, _ in PACK_LATE]
    last = late_token
    for name in names:
        if name not in late_names:
            delta[name], new_m[name], new_v[name] = _adamw(ws[name], grads[name], ms[name], vs[name], "adamw_" + name,
                                                           order_after=last)
            last = new_v[name]
    part_l, land_l = _chips_wait(*late_handles, last, "chips_wait_late")
    grads.update(unpack_grads(PACK_LATE, _sum_chips(part_l, land_l, qvec, "sum_grads_late")))
    for name in late_names:
        delta[name], new_m[name], new_v[name] = _adamw(ws[name], grads[name], ms[name], vs[name], "adamw_" + name)

    return (ssum[12, 0], grad_x[None], *[grads[n] for n in names], *[delta[n] for n in names],
            *[new_m[n] for n in names], *[new_v[n] for n in names])
```

```python
import math

import jax
import jax.numpy as jnp
from jax import lax
from jax.experimental import pallas as pl
from jax.experimental.pallas import tpu as pltpu

F32 = jnp.float32
BF16 = jnp.bfloat16

N_DEV = 8
D = 1024
D_FF = 2816
D_PLE = 256
CHUNK = 64
HEADS = 8
RET_DK = 128
RET_DV = 256
MLA_NOPE = 128
MLA_ROPE = 64
MLA_DV = 128
LORA = 256
ROPE_BASE = 10000.0
EPS = 1e-5
ALPHA = 2.0 ** 0.25
RET_SCALE = RET_DK ** -0.5
MLA_SCALE = (MLA_NOPE + MLA_ROPE) ** -0.5
NEG = -1e30

ADAM_LR = 0.001
ADAM_B1 = 0.9
ADAM_B2 = 0.999
ADAM_EPS = 1e-08
ADAM_WD = 0.01
ADAM_STEP = 10

P_RQ, P_RK, P_RV, P_RG, P_GR, P_GM, P_CQ, P_CKV, P_KPE, P_W = 0, 1024, 2048, 4096, 6144, 7168, 8192, 8448, 8704, 8960
W_IN_COLS = 8768
RET_L = 256
ATT_TF = 2048
ATT_TB = 1024
ATT_HP = 1
LOG2E = math.log2(math.e)
Q_PRESCALE = MLA_SCALE * LOG2E

PACK = (("ffn1_w_in", 704), ("ffn1_w_out", 352), ("w_in", 1096), ("w_ret_o", 256), ("w_uq", 48), ("w_ukv", 64),
        ("w_mla_o", 128), ("w_out", 128), ("ffn2_w_in", 704), ("ffn2_w_out", 352), ("ple_w_gate", 128), ("ple_w_proj", 32))


def _pad16(r):
    return -(-r // 16) * 16


PACK_LATE = PACK[:2]
PACK_EARLY = PACK[2:]
SMALL_ROWS = 16


def _pcall(body, **kw):
    return pl.pallas_call(body, **kw)


def _pick(dim, prefs):
    for p in prefs:
        if dim % p == 0:
            return p
    return dim


def _sigmoid(x):
    return 1.0 / (1.0 + jnp.exp(-x))


def _silu(x):
    return x * _sigmoid(x)


def _ln(r, g, b):
    mu = jnp.mean(r, axis=-1, keepdims=True)
    var = jnp.mean(jnp.square(r - mu), axis=-1, keepdims=True)
    return (r - mu) * lax.rsqrt(var + EPS) * g + b


def _rms(x, g):
    return x * lax.rsqrt(jnp.mean(jnp.square(x), axis=-1, keepdims=True) + EPS) * g


def _dot(a, b, ca, cb):
    return lax.dot_general(a, b, (((ca,), (cb,)), ((), ())), preferred_element_type=F32)


def _accum(ref, val, first=None):
    @pl.when(pl.program_id(0) == 0 if first is None else first)
    def _():
        ref[...] = jnp.zeros_like(ref)

    ref[...] += val


def _mm(a, b, *, ta=False, tb=False, add=None, out_dtype=None, name, tm=None, tn=None, tk=None, epilogue=None):
    parts = a.shape[0] if a.ndim == 3 else 1
    ar, ac = a.shape[-2], a.shape[-1]
    out_dtype = out_dtype or (BF16 if ta else F32)
    if ta:
        K, M = ar, ac * parts
    else:
        M, K = ar, ac * parts
    if tb:
        N, K2 = b.shape
    else:
        K2, N = b.shape
    assert K == K2, (a.shape, b.shape, ta, tb)
    big = (1024, 1408, 1280, 768, 512, 256, 128)
    tm = tm or _pick(ac if (ta and parts > 1) else M, big)
    tn = tn or (N if N <= 1024 else _pick(N, big))
    kdim = ac if (not ta and parts > 1) else K
    tk = tk or (kdim if kdim <= 2816 else
                _pick(kdim, (2048, 1408, 1280, 1024, 512) if tm <= 1024 else (1024, 1408, 1280, 512)))
    nk = K // tk
    grid = (M // tm, N // tn, nk)
    if parts > 1 and ta:
        per = ac // tm
        a_spec = pl.BlockSpec((None, tk, tm), lambda i, j, k: (i // per, k, i % per))
    elif parts > 1:
        per = ac // tk
        a_spec = pl.BlockSpec((None, tm, tk), lambda i, j, k: (k // per, i, k % per))
    else:
        a_spec = pl.BlockSpec((tk, tm), lambda i, j, k: (k, i)) if ta else pl.BlockSpec((tm, tk), lambda i, j, k: (i, k))
    b_spec = pl.BlockSpec((tn, tk), lambda i, j, k: (j, k)) if tb else pl.BlockSpec((tk, tn), lambda i, j, k: (k, j))
    o_spec = pl.BlockSpec((tm, tn), lambda i, j, k: (i, j))
    ca, cb = (0 if ta else 1), (1 if tb else 0)
    has_add = add is not None
    n_in = 2 + int(has_add)
    if epilogue is not None:
        assert tn == N and not ta
        ep_fn, ep_rows, ep_whole, ep_outs, ep_accs = epilogue
        n_ep_in = len(ep_rows) + len(ep_whole)
        n_out = len(ep_outs) + len(ep_accs)
    else:
        n_ep_in, n_out = 0, 1

    def body(*refs):
        a_ref, b_ref = refs[0], refs[1]
        add_ref = refs[2] if has_add else None
        o_ref = refs[n_in + n_ep_in]
        first_row_tile = pl.program_id(0) == 0

        def finish(r):
            if has_add:
                r = r + add_ref[...].astype(F32)
            if epilogue is not None:
                ep_fn(r, refs[n_in:n_in + n_ep_in], refs[n_in + n_ep_in:n_in + n_ep_in + n_out], first_row_tile)
            else:
                o_ref[...] = r.astype(out_dtype)

        if nk == 1:
            finish(_dot(a_ref[...], b_ref[...], ca, cb))
            return
        acc_ref = refs[-1]
        k = pl.program_id(2)

        @pl.when(k == 0)
        def _():
            acc_ref[...] = jnp.zeros_like(acc_ref)

        acc_ref[...] += _dot(a_ref[...], b_ref[...], ca, cb)

        @pl.when(k == nk - 1)
        def _():
            finish(acc_ref[...])

    in_specs = [a_spec, b_spec] + ([o_spec] if has_add else [])
    args = (a, b) + ((add,) if has_add else ())
    out_specs, out_shape = o_spec, jax.ShapeDtypeStruct((M, N), out_dtype)
    if epilogue is not None:
        for row_in in ep_rows:
            arr, col_block, width = (tuple(row_in) + (N,))[:3] if isinstance(row_in, tuple) else (row_in, 0, N)
            in_specs.append(pl.BlockSpec((tm, width), lambda i, j, k, _c=col_block: (i, _c)))
            args += (arr,)
        in_specs += [pl.BlockSpec(w.shape, lambda i, j, k, _n=w.ndim: (0,) * _n) for w in ep_whole]
        args += tuple(ep_whole)
        outs = [o if isinstance(o, tuple) else (o, N) for o in ep_outs]
        out_specs = ([pl.BlockSpec((tm, w), lambda i, j, k: (i, 0)) for _, w in outs]
                     + [pl.BlockSpec((r, N), lambda i, j, k: (0, 0)) for r in ep_accs])
        out_shape = [jax.ShapeDtypeStruct((M, w), dt) for dt, w in outs] + [jax.ShapeDtypeStruct((r, N), F32) for r in ep_accs]
    return _pcall(
        body, grid=grid, in_specs=in_specs, out_specs=out_specs, out_shape=out_shape,
        scratch_shapes=[pltpu.VMEM((tm, tn), F32)] if nk > 1 else [], name=name,
        compiler_params=pltpu.CompilerParams(dimension_semantics=("arbitrary" if epilogue is not None else "parallel", "parallel", "arbitrary")),
    )(*args)


def _mm_pieces(pieces, add, name):
    M, N = add.shape
    tm = _pick(M, (1024, 512, 256, 128))
    spans, in_specs, args, start = [], [], [], 0
    for a, b in pieces:
        K = a.shape[1]
        assert b.shape == (K, N) and a.shape[0] == M, (a.shape, b.shape)
        tk = _pick(K, (1536, 1024, 768, 512, 256, 128))
        n = K // tk
        spans.append((start, n))
        block = lambda k, _s=start, _n=n: jnp.where(k < _s, _n - 1, jnp.minimum(k - _s, _n - 1))
        in_specs.append(pl.BlockSpec((tm, tk), lambda i, k, _s=start, _blk=block: (jnp.where(k < _s, jnp.maximum(i - 1, 0), i), _blk(k))))
        in_specs.append(pl.BlockSpec((tk, N), lambda i, k, _blk=block: (_blk(k), 0)))
        args += [a, b]
        start += n
    o_spec = pl.BlockSpec((tm, N), lambda i, k: (i, 0))

    def body(*refs):
        add_ref, o_ref = refs[-2], refs[-1]
        k = pl.program_id(1)

        @pl.when(k == 0)
        def _():
            o_ref[...] = add_ref[...]

        for p, (s, n) in enumerate(spans):
            @pl.when((k >= s) & (k < s + n))
            def _(p=p):
                o_ref[...] += _dot(refs[2 * p][...], refs[2 * p + 1][...], 1, 0)

    return _pcall(
        body, grid=(M // tm, start), in_specs=in_specs + [o_spec], out_specs=o_spec, out_shape=jax.ShapeDtypeStruct((M, N), F32),
        name=name, compiler_params=pltpu.CompilerParams(dimension_semantics=("parallel", "arbitrary")),
    )(*args, add)


def _ln_epilogue(res, c, g, b):
    def fn(r, ins, outs, first):
        res_ref, g_ref, b_ref = ins
        f_ref, h_ref, hb_ref = outs
        h = _ln(ALPHA * res_ref[...] + c * r, g_ref[...], b_ref[...])
        f_ref[...] = r
        h_ref[...] = h
        hb_ref[...] = h.astype(BF16)

    return (fn, [res], [g, b], (F32, F32, BF16), ())


def _ln_bwd_epilogue(res, f, c, g, b):
    def fn(r, ins, outs, first):
        res_ref, f_ref, g_ref, b_ref = ins
        dr_ref, df_ref, dg_ref, db_ref = outs
        pre = ALPHA * res_ref[...] + c * f_ref[...]
        xc = pre - jnp.mean(pre, axis=-1, keepdims=True)
        rstd = lax.rsqrt(jnp.mean(jnp.square(xc), axis=-1, keepdims=True) + EPS)
        xhat = xc * rstd
        dyg = r * g_ref[...]
        dpre = rstd * (dyg - jnp.mean(dyg, axis=-1, keepdims=True) - xhat * jnp.mean(dyg * xhat, axis=-1, keepdims=True))
        dr_ref[...] = ALPHA * dpre
        df_ref[...] = (c * dpre).astype(BF16)
        _accum(dg_ref, jnp.sum(r * xhat, axis=0, keepdims=True), first)
        _accum(db_ref, jnp.sum(r, axis=0, keepdims=True), first)

    return (fn, [res, f], [g, b], (F32, BF16), (1, 1))


ROW_TM = 1024


def _rows(body, T, tm, ins, outs, name, accs=()):
    in_specs, args = [], []
    for arr, w, cb in ins:
        if w is None:
            in_specs.append(pl.BlockSpec(arr.shape, lambda i, _n=arr.ndim: (0,) * _n))
        else:
            in_specs.append(pl.BlockSpec((tm, w), lambda i, _cb=cb: (i, _cb)))
        args.append(arr)
    out_specs = [pl.BlockSpec((tm, w), lambda i: (i, 0)) for w, _ in outs]
    out_shape = [jax.ShapeDtypeStruct((T, w), dt) for w, dt in outs]
    for r, w in accs:
        out_specs.append(pl.BlockSpec((r, w), lambda i: (0, 0)))
        out_shape.append(jax.ShapeDtypeStruct((r, w), F32))
    return _pcall(
        body, grid=(T // tm,), in_specs=in_specs, out_specs=out_specs, out_shape=out_shape, name=name,
        compiler_params=pltpu.CompilerParams(dimension_semantics=("arbitrary",)),
    )(*args)


def _ln_bwd_rows(res, f, c, g, b, dh, T, name):
    fn = _ln_bwd_epilogue(res, f, c, g, b)[0]

    def body(r_ref, f_ref, g_ref, b_ref, d_ref, dr_ref, df_ref, dg_ref, db_ref):
        fn(d_ref[...], (r_ref, f_ref, g_ref, b_ref), (dr_ref, df_ref, dg_ref, db_ref), pl.program_id(0) == 0)

    return _rows(body, T, min(ROW_TM, T), [(res, D, 0), (f, D, 0), (g, None, None), (b, None, None), (dh, D, 0)],
                 [(D, F32), (D, BF16)], name, accs=[(1, D), (1, D)])


FFN_TN = 1408
FFN_TM = 512
LN_TM = 512


def _ffn_in(x, wt, name, order_after=None):
    T = x.shape[0]
    tm, tn = min(FFN_TM, T), FFN_TN
    nj = D_FF // tn
    emit_xb = x.dtype != BF16
    n_in = 3 + int(order_after is not None)

    def body(*refs):
        x_ref, wg_ref, wu_ref = refs[:3]
        hs_ref, gu_ref = refs[n_in], refs[n_in + 1]
        xv = x_ref[...].astype(BF16)
        g = _dot(xv, wg_ref[...], 1, 1)
        u = _dot(xv, wu_ref[...], 1, 1)
        hs_ref[...] = (_silu(g) * u).astype(BF16)
        gu_ref[0] = g.astype(BF16)
        gu_ref[1] = u.astype(BF16)
        if emit_xb:
            refs[n_in + 2][...] = xv

    in_specs = [pl.BlockSpec((tm, D), lambda i, j: (i, 0)), pl.BlockSpec((tn, D), lambda i, j: (j, 0)),
                pl.BlockSpec((tn, D), lambda i, j: (j + nj, 0))]
    args = [x, wt, wt]
    if order_after is not None:
        in_specs.append(pl.BlockSpec(order_after.shape, lambda i, j: (0, 0)))
        args.append(order_after)
    out_specs = [pl.BlockSpec((tm, tn), lambda i, j: (i, j)), pl.BlockSpec((2, tm, tn), lambda i, j: (0, i, j))]
    out_shape = [jax.ShapeDtypeStruct((T, D_FF), BF16), jax.ShapeDtypeStruct((2, T, D_FF), BF16)]
    if emit_xb:
        out_specs.append(pl.BlockSpec((tm, D), lambda i, j: (i, 0)))
        out_shape.append(jax.ShapeDtypeStruct((T, D), BF16))
    return _pcall(
        body, grid=(T // tm, nj), in_specs=in_specs, out_specs=out_specs, out_shape=out_shape, name=name,
        compiler_params=pltpu.CompilerParams(dimension_semantics=("parallel", "arbitrary")),
    )(*args)


def _ffn_act_bwd(df, wo, gu, name):
    T = df.shape[0]
    tm, tn = min(FFN_TM, T), FFN_TN

    def body(d_ref, w_ref, gu_ref, o_ref):
        dhs = _dot(d_ref[...], w_ref[...], 1, 1)
        g, u = gu_ref[0].astype(F32), gu_ref[1].astype(F32)
        sig = _sigmoid(g)
        act = g * sig
        o_ref[0] = (dhs * u * (sig + act * (1.0 - sig))).astype(BF16)
        o_ref[1] = (dhs * act).astype(BF16)

    return _pcall(
        body, grid=(T // tm, D_FF // tn),
        in_specs=[pl.BlockSpec((tm, D), lambda i, j: (i, 0)), pl.BlockSpec((tn, D), lambda i, j: (j, 0)),
                  pl.BlockSpec((2, tm, tn), lambda i, j: (0, i, j))],
        out_specs=pl.BlockSpec((2, tm, tn), lambda i, j: (0, i, j)),
        out_shape=jax.ShapeDtypeStruct((2, T, D_FF), BF16), name=name,
        compiler_params=pltpu.CompilerParams(dimension_semantics=("parallel", "parallel")),
    )(df, wo, gu)


def _rope128(t, cos, sin_s):
    return t * cos + pltpu.roll(t, 64, 1) * sin_s


def _rope128_t(g, cos, sin_s):
    return g * cos - pltpu.roll(g, 64, 1) * sin_s


def _partner32(t):
    lane = lax.broadcasted_iota(jnp.int32, t.shape, 1)
    return jnp.where((lane & 32) == 0, pltpu.roll(t, 96, 1), pltpu.roll(t, 32, 1))


def _rope64(t, cos, sin_s):
    return t * cos + _partner32(t) * sin_s


def _rope64_t(g, cos, sin_s):
    return g * cos - _partner32(g) * sin_s


def _mixer_prep_fwd(proj, tabs, qg, kvg, T):
    cos128, sin128, cos64, sin64 = tabs

    def body(rq_ref, rk_ref, cq_ref, ckv_ref, kpe_ref, c1_ref, s1_ref, c2_ref, s2_ref, qg_ref, kvg_ref,
             oq_ref, ok_ref, oqn_ref, okvn_ref, okpe_ref):
        c1, s1 = c1_ref[...], s1_ref[...]
        for h in range(HEADS):
            sl = slice(h * RET_DK, (h + 1) * RET_DK)
            oq_ref[:, sl] = _rope128(rq_ref[:, sl].astype(F32), c1, s1).astype(BF16)
            ok_ref[:, sl] = (_rope128(rk_ref[:, sl].astype(F32), c1, s1) * RET_SCALE).astype(BF16)
        oqn_ref[...] = _rms(cq_ref[...].astype(F32), qg_ref[...]).astype(BF16)
        okvn_ref[...] = _rms(ckv_ref[...].astype(F32), kvg_ref[...]).astype(BF16)
        okpe_ref[...] = _rope64(kpe_ref[...].astype(F32), c2_ref[...], s2_ref[...]).astype(BF16)

    ins = [(proj, 1024, 0), (proj, 1024, 1), (proj, 256, P_CQ // 256), (proj, 256, P_CKV // 256),
           (proj, 128, P_KPE // 128), (cos128, 128, 0), (sin128, 128, 0), (cos64, 128, 0), (sin64, 128, 0),
           (qg, None, None), (kvg, None, None)]
    outs = [(1024, BF16), (1024, BF16), (LORA, BF16), (LORA, BF16), (128, BF16)]
    return _rows(body, T, min(ROW_TM, T), ins, outs, "mixer_prep_fwd")


def _rms_bwd(proj, dqn, dkvn, dkpe_all, tabs, qg, kvg, T):
    _, _, cos64, sin64 = tabs

    def body(cq_ref, ckv_ref, dq_ref, dkv_ref, dk_ref, c2_ref, s2_ref, qg_ref, kvg_ref, o_ref, dqg_ref, dkvg_ref):
        _, vjp = jax.vjp(_rms, cq_ref[...].astype(F32), qg_ref[...])
        dx, dg = vjp(dq_ref[...])
        o_ref[:, 0:LORA] = dx.astype(BF16)
        _accum(dqg_ref, dg)
        _, vjp = jax.vjp(_rms, ckv_ref[...].astype(F32), kvg_ref[...])
        dx, dg = vjp(dkv_ref[...])
        o_ref[:, P_CKV - P_CQ:P_CKV - P_CQ + LORA] = dx.astype(BF16)
        _accum(dkvg_ref, dg)
        g = dk_ref[:, 0:128]
        for h in range(1, HEADS):
            g = g + dk_ref[:, h * 128:(h + 1) * 128]
        lane = lax.broadcasted_iota(jnp.int32, g.shape, 1)
        g = jnp.where(lane < MLA_ROPE, g, 0.0)
        o_ref[:, P_KPE - P_CQ:P_KPE - P_CQ + 128] = _rope64_t(g, c2_ref[...], s2_ref[...]).astype(BF16)
        o_ref[:, P_KPE - P_CQ + 128:] = jnp.zeros((g.shape[0], P_W - P_KPE - 128), BF16)

    ins = [(proj, 256, P_CQ // 256), (proj, 256, P_CKV // 256), (dqn, LORA, 0), (dkvn, LORA, 0), (dkpe_all, 1024, 0),
           (cos64, 128, 0), (sin64, 128, 0), (qg, None, None), (kvg, None, None)]
    return _rows(body, T, min(ROW_TM, T), ins, [(P_W - P_CQ, BF16)], "rms_bwd", accs=[(1, LORA), (1, LORA)])


def _gn_gate_bwd(y, rg, g, d):
    xc = y - jnp.mean(y, axis=-1, keepdims=True)
    rstd = lax.rsqrt(jnp.mean(jnp.square(xc), axis=-1, keepdims=True) + EPS)
    xhat = xc * rstd
    sig = _sigmoid(rg)
    act = rg * sig
    dn = d * act
    drg = d * (xhat * g) * (sig + act * (1.0 - sig))
    dxh = dn * g
    dy = rstd * (dxh - jnp.mean(dxh, axis=-1, keepdims=True) - xhat * jnp.mean(dxh * xhat, axis=-1, keepdims=True))
    return dy, drg, jnp.sum(dn * xhat, axis=0, keepdims=True)


def _gn_gate(y, rg, g):
    mu = jnp.mean(y, axis=-1, keepdims=True)
    var = jnp.mean(jnp.square(y - mu), axis=-1, keepdims=True)
    return _silu(rg) * ((y - mu) * lax.rsqrt(var + EPS) * g)


def _q_assemble_epilogue(tabs):
    _, _, cos64, sin64 = tabs

    def fn(q, ins, outs, first):
        c_ref, s_ref = ins
        on_ref, op_ref = outs
        on_ref[...] = (q[:, :1024] * Q_PRESCALE).astype(BF16)
        c, s = c_ref[...], s_ref[...]
        lane = lax.broadcasted_iota(jnp.int32, c.shape, 1)
        for j in range(HEADS // 2):
            r = _rope64(q[:, 1024 + 128 * j:1024 + 128 * (j + 1)], c, s) * Q_PRESCALE
            op_ref[:, 256 * j:256 * j + 128] = jnp.where(lane < 64, r, 0.0).astype(BF16)
            op_ref[:, 256 * j + 128:256 * j + 256] = jnp.where(lane < 64, pltpu.roll(r, 64, 1), 0.0).astype(BF16)

    return (fn, [(cos64, 0, 128), (sin64, 0, 128)], [], ((BF16, 1024), (BF16, 1024)), ())


def _q_assemble_bwd(dqn, dqpe, tabs, T):
    _, _, cos64, sin64 = tabs

    def body(dn_ref, dp_ref, c_ref, s_ref, on_ref, op_ref):
        on_ref[...] = dn_ref[...].astype(BF16)
        c, s = c_ref[...], s_ref[...]
        lane = lax.broadcasted_iota(jnp.int32, c.shape, 1)
        for j in range(HEADS // 2):
            g = jnp.where(lane < 64, dp_ref[:, 256 * j:256 * j + 128], pltpu.roll(dp_ref[:, 256 * j + 128:256 * j + 256], 64, 1))
            op_ref[:, 128 * j:128 * (j + 1)] = _rope64_t(g, c, s).astype(BF16)

    return _rows(body, T, min(ROW_TM, T), [(dqn, 1024, 0), (dqpe, 1024, 0), (cos64, 128, 0), (sin64, 128, 0)],
                 [(1024, BF16), (512, BF16)], "q_assemble_bwd")


def _mix_fn(gr, gm, yr, ym):
    return _sigmoid(gr) * yr + _sigmoid(gm) * ym


def _mix_epilogue(proj, y_ret):
    def fn(r, ins, outs, first):
        gr_ref, gm_ref, yr_ref = ins
        ym_ref, mix_ref = outs
        ym_ref[...] = r.astype(BF16)
        mix_ref[...] = _mix_fn(gr_ref[...].astype(F32), gm_ref[...].astype(F32), yr_ref[...].astype(F32), r).astype(BF16)

    return (fn, [(proj, P_GR // D), (proj, P_GM // D), y_ret], [], (BF16, BF16), ())


def _mix_bwd_epilogue(proj, y_ret, y_mla):
    def fn(r, ins, outs, first):
        gr_ref, gm_ref, yr_ref, ym_ref = ins
        dgate_ref, dyr_ref, dym_ref = outs
        sr, sm = _sigmoid(gr_ref[...].astype(F32)), _sigmoid(gm_ref[...].astype(F32))
        dgate_ref[:, 0:D] = (r * yr_ref[...].astype(F32) * (sr * (1.0 - sr))).astype(BF16)
        dgate_ref[:, D:2 * D] = (r * ym_ref[...].astype(F32) * (sm * (1.0 - sm))).astype(BF16)
        dyr_ref[...] = (r * sr).astype(BF16)
        dym_ref[...] = (r * sm).astype(BF16)

    return (fn, [(proj, P_GR // D), (proj, P_GM // D), y_ret, y_mla], [], ((BF16, 2 * D), BF16, BF16), ())


def _head_epilogue(h3, pp, tgt, g, b):
    def fn(r, ins, outs, first):
        h_ref, pp_ref, t_ref, g_ref, b_ref = ins
        dh_ref, dgl_ref, dpp_ref, dg_ref, db_ref, loss_ref = outs

        sg, pp, gain = _sigmoid(r), pp_ref[...], g_ref[...]
        pre = ALPHA * h_ref[...] + sg * pp
        xc = pre - jnp.mean(pre, axis=-1, keepdims=True)
        rstd = lax.rsqrt(jnp.mean(jnp.square(xc), axis=-1, keepdims=True) + EPS)
        xhat = xc * rstd
        err = xhat * gain + b_ref[...] - t_ref[...]
        dy = err * (1.0 / D)
        dyg = dy * gain
        dpre = rstd * (dyg - jnp.mean(dyg, axis=-1, keepdims=True) - xhat * jnp.mean(dyg * xhat, axis=-1, keepdims=True))
        dh_ref[...] = ALPHA * dpre
        dgl_ref[...] = (dpre * pp * (sg * (1.0 - sg))).astype(BF16)
        dpp_ref[...] = (dpre * sg).astype(BF16)
        _accum(dg_ref, jnp.sum(dy * xhat, axis=0, keepdims=True), first)
        _accum(db_ref, jnp.sum(dy, axis=0, keepdims=True), first)
        part = 0.5 * jnp.sum(jnp.mean(jnp.square(err), axis=-1, keepdims=True), axis=0, keepdims=True)
        _accum(loss_ref, jnp.broadcast_to(part, loss_ref.shape), first)

    return (fn, [h3, pp, tgt], [g, b], (F32, BF16, BF16), (1, 1, 8))


def _delta_epilogue(o):
    def fn(r, ins, outs, first):
        (o_ref,) = ins
        db_ref, dl_ref = outs
        db_ref[...] = r.astype(BF16)
        for h in range(HEADS):
            sl = slice(h * MLA_DV, (h + 1) * MLA_DV)
            dl = jnp.sum(r[:, sl] * o_ref[:, sl].astype(F32), axis=-1, keepdims=True)
            dl_ref[:, sl] = jnp.broadcast_to(dl, (r.shape[0], MLA_DV))

    return (fn, [o], [], (BF16, F32), ())


def _ret_consts():
    L = RET_L
    lg = jnp.log(1.0 - 2.0 ** (-5.0 - jnp.arange(HEADS, dtype=F32)))[:, None, None]
    idx = jnp.arange(L, dtype=F32)
    ch = jnp.arange(L) // CHUNK
    dist = idx[:, None] - idx[None, :]
    same = (ch[:, None] == ch[None, :])[None]
    earlier = (ch[None, :] < ch[:, None])[None]
    dm = jnp.where(same, jnp.exp(lg * jnp.abs(dist)[None]), jnp.where(earlier, jnp.exp(lg * dist[None]), 0.0))
    xi = jnp.broadcast_to(jnp.exp(lg * (idx + 1.0)[None, :, None]), (HEADS, L, 128))
    zeta = jnp.broadcast_to(jnp.exp(lg * (L - 1.0 - idx)[None, :, None]), (HEADS, L, 128))
    gl = jnp.broadcast_to(jnp.exp(lg * float(L)), (HEADS, 8, 128))
    return dm.astype(F32), xi.astype(F32), zeta.astype(F32), gl.astype(F32)


def _whole(arr):
    return pl.BlockSpec(arr.shape, lambda n, _nd=arr.ndim: (0,) * _nd)


def _ret_fwd(q, k, v, proj, gn_g, consts, T):
    dm, xi, zeta, gl = consts
    L = RET_L
    n_sc = T // L

    def body(q_ref, k_ref, v_ref, rg_ref, g_ref, dm_ref, xi_ref, ze_ref, gl_ref, y_ref, yr_ref, s_ref, st_ref):
        @pl.when(pl.program_id(0) == 0)
        def _():
            st_ref[...] = jnp.zeros_like(st_ref)

        for h in range(HEADS):
            ks, vs = slice(h * RET_DK, (h + 1) * RET_DK), slice(h * RET_DV, (h + 1) * RET_DV)
            qq, kk, vv = q_ref[:, ks], k_ref[:, ks], v_ref[:, vs]
            st = st_ref[h]
            s_ref[h, 0] = st
            p = (_dot(qq, kk, 1, 1) * dm_ref[h]).astype(BF16)
            cross = _dot(qq, st.astype(BF16), 1, 0)
            xi_c = jnp.concatenate([xi_ref[h], xi_ref[h]], axis=1)
            y = _dot(p, vv, 1, 0) + cross * xi_c
            y_ref[:, vs] = y
            yr_ref[:, vs] = _gn_gate(y, rg_ref[:, vs].astype(F32), g_ref[:, vs]).astype(BF16)
            kz = (kk.astype(F32) * ze_ref[h]).astype(BF16)
            gl2 = jnp.concatenate([gl_ref[h, 0:1, :], gl_ref[h, 0:1, :]], axis=1)
            st_ref[h] = st * gl2 + _dot(kz, vv, 0, 0)

    return _pcall(
        body, grid=(n_sc,),
        in_specs=[pl.BlockSpec((L, 1024), lambda n: (n, 0)), pl.BlockSpec((L, 1024), lambda n: (n, 0)),
                  pl.BlockSpec((L, 2048), lambda n: (n, P_RV // 2048)), pl.BlockSpec((L, 2048), lambda n: (n, P_RG // 2048)),
                  _whole(gn_g), _whole(dm), _whole(xi), _whole(zeta), _whole(gl)],
        out_specs=[pl.BlockSpec((L, 2048), lambda n: (n, 0)), pl.BlockSpec((L, 2048), lambda n: (n, 0)),
                   pl.BlockSpec((HEADS, 1, 128, 256), lambda n: (0, n, 0, 0))],
        out_shape=[jax.ShapeDtypeStruct((T, HEADS * RET_DV), F32), jax.ShapeDtypeStruct((T, HEADS * RET_DV), BF16),
                   jax.ShapeDtypeStruct((HEADS, n_sc, 128, 256), F32)],
        scratch_shapes=[pltpu.VMEM((HEADS, 128, 256), F32)], name="ret_fwd",
        compiler_params=pltpu.CompilerParams(dimension_semantics=("arbitrary",)),
    )(q, k, v, proj, gn_g, dm, xi, zeta, gl)


def _ret_bwd(q, k, v, y, proj, gn_g, states, dyr, consts, tabs, T):
    dm, xi, zeta, gl = consts
    cos128, sin128, _, _ = tabs
    L = RET_L
    n_sc = T // L

    def body(q_ref, k_ref, v_ref, y_ref, rg_ref, g_ref, d_ref, s_ref, dm_ref, xi_ref, ze_ref, gl_ref, c_ref, sn_ref,
             dp_ref, dg_ref, gs_ref):
        @pl.when(pl.program_id(0) == 0)
        def _():
            gs_ref[...] = jnp.zeros_like(gs_ref)

        c, sn = c_ref[...], sn_ref[...]
        dgs = []
        for h in range(HEADS):
            ks, vs = slice(h * RET_DK, (h + 1) * RET_DK), slice(h * RET_DV, (h + 1) * RET_DV)
            dy, drg, dg = _gn_gate_bwd(y_ref[:, vs], rg_ref[:, vs].astype(F32), g_ref[:, vs], d_ref[:, vs].astype(F32))
            dp_ref[:, P_RG + h * RET_DV:P_RG + (h + 1) * RET_DV] = drg.astype(BF16)
            dgs.append(dg)
            qq, kk, vv, dyy = q_ref[:, ks], k_ref[:, ks], v_ref[:, vs], dy.astype(BF16)
            dmm = dm_ref[h]
            gb = gs_ref[h].astype(BF16)
            sb = s_ref[h, 0].astype(BF16)
            xi_c = jnp.concatenate([xi_ref[h], xi_ref[h]], axis=1)
            pb = (_dot(qq, kk, 1, 1) * dmm).astype(BF16)
            kz = (kk.astype(F32) * ze_ref[h]).astype(BF16)
            dp_ref[:, P_RV + h * RET_DV:P_RV + (h + 1) * RET_DV] = (_dot(pb, dyy, 0, 0) + _dot(kz, gb, 1, 0)).astype(BF16)
            da = (_dot(dyy, vv, 1, 1) * dmm).astype(BF16)
            dyx = (dyy.astype(F32) * xi_c).astype(BF16)
            dq = _dot(da, kk, 1, 0) + _dot(dyx, sb, 1, 1)
            dk = _dot(da, qq, 0, 0) + _dot(vv, gb, 1, 1) * ze_ref[h]
            dp_ref[:, P_RQ + h * RET_DK:P_RQ + (h + 1) * RET_DK] = _rope128_t(dq, c, sn).astype(BF16)
            dp_ref[:, P_RK + h * RET_DK:P_RK + (h + 1) * RET_DK] = (_rope128_t(dk, c, sn) * RET_SCALE).astype(BF16)
            gl2 = jnp.concatenate([gl_ref[h, 0:1, :], gl_ref[h, 0:1, :]], axis=1)
            gs_ref[h] = gs_ref[h] * gl2 + _dot(qq, dyx, 0, 0)
        _accum(dg_ref, jnp.concatenate(dgs, axis=1))

    rev = lambda n: n_sc - 1 - n
    return _pcall(
        body, grid=(n_sc,),
        in_specs=[pl.BlockSpec((L, 1024), lambda n: (rev(n), 0)), pl.BlockSpec((L, 1024), lambda n: (rev(n), 0)),
                  pl.BlockSpec((L, 2048), lambda n: (rev(n), P_RV // 2048)), pl.BlockSpec((L, 2048), lambda n: (rev(n), 0)),
                  pl.BlockSpec((L, 2048), lambda n: (rev(n), P_RG // 2048)), _whole(gn_g),
                  pl.BlockSpec((L, 2048), lambda n: (rev(n), 0)),
                  pl.BlockSpec((HEADS, 1, 128, 256), lambda n: (0, rev(n), 0, 0)),
                  _whole(dm), _whole(xi), _whole(zeta), _whole(gl),
                  pl.BlockSpec((L, 128), lambda n: (rev(n), 0)), pl.BlockSpec((L, 128), lambda n: (rev(n), 0))],
        out_specs=[pl.BlockSpec((L, P_GR), lambda n: (rev(n), 0)), pl.BlockSpec((1, 2048), lambda n: (0, 0))],
        out_shape=[jax.ShapeDtypeStruct((T, P_GR), BF16), jax.ShapeDtypeStruct((1, 2048), F32)],
        scratch_shapes=[pltpu.VMEM((HEADS, 128, 256), F32)], name="ret_bwd",
        compiler_params=pltpu.CompilerParams(dimension_semantics=("arbitrary",)),
    )(q, k, v, y, proj, gn_g, dyr, states, dm, xi, zeta, gl, cos128, sin128)


def _diag_mask(nrows, ncols, row0):
    row = row0 + lax.broadcasted_iota(jnp.int32, (nrows, ncols), 0)
    col = lax.broadcasted_iota(jnp.int32, (nrows, ncols), 1)
    return lax.shift_right_logical(col, 6) <= lax.shift_right_logical(row, 6)


def _diag_spans(t):
    h = t // 2
    return ((0, h, h), (h, h, t)) if h % 128 == 0 else ((0, t, t),)


def _tri_steps(nb, by_key):
    if by_key:
        pairs = [(i, j) for j in range(nb) for i in range(j, nb)]
    else:
        pairs = [(i, j) for i in range(nb) for j in range(i + 1)]
    return jnp.array([a for a, _ in pairs], jnp.int32), jnp.array([b for _, b in pairs], jnp.int32)


def _attn_fwd(qn, qpe, kv, kpe, T):
    t = min(ATT_TF, T)
    nb = T // t
    ii, jj = _tri_steps(nb, by_key=False)

    hp = ATT_HP
    w = 128 * hp

    def body(ii_ref, jj_ref, qn_ref, qp_ref, kn_ref, kp_ref, v_ref, o_ref, lse_ref, m_sc, l_sc, acc_sc):
        st = pl.program_id(1)
        i, j = ii_ref[st], jj_ref[st]

        @pl.when(j == 0)
        def _():
            m_sc[...] = jnp.full_like(m_sc, NEG)
            l_sc[...] = jnp.zeros_like(l_sc)
            acc_sc[...] = jnp.zeros_like(acc_sc)

        def update(diag):
            kp = kp_ref[...]
            spans = _diag_spans(t) if diag else ((0, t, t),)
            for hh in range(hp):
                sl = slice(128 * hh, 128 * (hh + 1))
                for row0, nr, nkeys in spans:
                    rows = slice(row0, row0 + nr)
                    q = jnp.concatenate([qn_ref[rows, sl], qp_ref[rows, sl]], axis=1)
                    k = jnp.concatenate([kn_ref[:nkeys, sl], kp[:nkeys]], axis=1)
                    s = _dot(q, k, 1, 1)
                    if diag:
                        s = jnp.where(_diag_mask(nr, nkeys, row0), s, NEG)
                    m_prev = m_sc[rows, sl]
                    m_new = jnp.maximum(m_prev, jnp.max(s, axis=1, keepdims=True))
                    a = jnp.exp2(m_prev - m_new)
                    p = jnp.exp2(s - m_new[:, 0:1])
                    l_sc[rows, sl] = a * l_sc[rows, sl] + jnp.sum(p, axis=1, keepdims=True)
                    acc_sc[rows, sl] = a * acc_sc[rows, sl] + _dot(p.astype(BF16), v_ref[:nkeys, sl], 1, 0)
                    m_sc[rows, sl] = m_new

        @pl.when(j < i)
        def _():
            update(False)

        @pl.when(j == i)
        def _():
            update(True)
            o_ref[...] = (acc_sc[...] / l_sc[...]).astype(BF16)
            lse_ref[...] = m_sc[...] + jnp.log2(l_sc[...])

    qs = pl.BlockSpec((t, w), lambda h, s, ii, jj: (ii[s], h))
    grid_spec = pltpu.PrefetchScalarGridSpec(
        num_scalar_prefetch=2, grid=(HEADS // hp, int(ii.shape[0])),
        in_specs=[qs, qs, pl.BlockSpec((t, w), lambda h, s, ii, jj: (jj[s], h)), pl.BlockSpec((t, 128), lambda h, s, ii, jj: (jj[s], 0)),
                  pl.BlockSpec((t, w), lambda h, s, ii, jj: (jj[s], HEADS // hp + h))],
        out_specs=[qs, qs],
        scratch_shapes=[pltpu.VMEM((t, w), F32), pltpu.VMEM((t, w), F32), pltpu.VMEM((t, w), F32)])
    return _pcall(
        body, grid_spec=grid_spec, out_shape=[jax.ShapeDtypeStruct((T, D), BF16), jax.ShapeDtypeStruct((T, D), F32)], name="attn_fwd",
        compiler_params=pltpu.CompilerParams(dimension_semantics=("arbitrary", "arbitrary")),
    )(ii, jj, qn, qpe, kv, kpe, kv)


def _attn_bwd(qn, qpe, kv, kpe, do, lse, delta, T):
    t = min(ATT_TB, T)
    nb = T // t
    ii, jj = _tri_steps(nb, by_key=True)

    def body(ii_ref, jj_ref, qn_ref, qp_ref, kn_ref, kp_ref, v_ref, do_ref, lse_ref, dl_ref,
             dqn_ref, dqp_ref, dkn_ref, dkp_ref, dv_ref, dk_sc, dv_sc):
        st = pl.program_id(1)
        i, j = ii_ref[st], jj_ref[st]

        @pl.when(st == 0)
        def _():
            dqn_ref[...] = jnp.zeros_like(dqn_ref)
            dqp_ref[...] = jnp.zeros_like(dqp_ref)

        @pl.when(i == j)
        def _():
            dk_sc[...] = jnp.zeros_like(dk_sc)
            dv_sc[...] = jnp.zeros_like(dv_sc)

        def update(diag):
            for row0, nr, nkeys in (_diag_spans(t) if diag else ((0, t, t),)):
                rows = slice(row0, row0 + nr)
                q = jnp.concatenate([qn_ref[rows, :], qp_ref[rows, :]], axis=1)
                k = jnp.concatenate([kn_ref[:nkeys, :], kp_ref[:nkeys, :]], axis=1)
                dob = do_ref[rows, :]
                s = _dot(q, k, 1, 1)
                if diag:
                    s = jnp.where(_diag_mask(nr, nkeys, row0), s, NEG)
                p = jnp.exp2(s - lse_ref[rows, 0:1])
                dv_sc[:nkeys, :] += _dot(p.astype(BF16), dob, 0, 0)
                dp = _dot(dob, v_ref[:nkeys, :], 1, 1)
                ds = (p * (dp - dl_ref[rows, 0:1])).astype(BF16)
                dk_sc[:nkeys, :] += _dot(ds, q, 0, 0)
                dq = _dot(ds, k, 1, 0) * MLA_SCALE
                grows = pl.ds(pl.multiple_of(i * t + row0, nr), nr)
                dqn_ref[grows, :] += dq[:, :128]
                dqp_ref[grows, :] += dq[:, 128:]

        @pl.when(i > j)
        def _():
            update(False)

        @pl.when(i == j)
        def _():
            update(True)

        @pl.when(i == nb - 1)
        def _():
            dkn_ref[...] = (dk_sc[:, :128] * (1.0 / LOG2E)).astype(BF16)
            dkp_ref[...] = dk_sc[:, 128:] * (1.0 / LOG2E)
            dv_ref[...] = dv_sc[...].astype(BF16)

    qs = pl.BlockSpec((t, 128), lambda h, s, ii, jj: (ii[s], h))
    ks = pl.BlockSpec((t, 128), lambda h, s, ii, jj: (jj[s], h))
    hs = pl.BlockSpec((T, 128), lambda h, s, ii, jj: (0, h))
    grid_spec = pltpu.PrefetchScalarGridSpec(
        num_scalar_prefetch=2, grid=(HEADS, int(ii.shape[0])),
        in_specs=[qs, qs, ks, pl.BlockSpec((t, 128), lambda h, s, ii, jj: (jj[s], 0)),
                  pl.BlockSpec((t, 128), lambda h, s, ii, jj: (jj[s], HEADS + h)), qs, qs, qs],
        out_specs=[hs, hs, ks, ks, ks],
        scratch_shapes=[pltpu.VMEM((t, 256), F32), pltpu.VMEM((t, 128), F32)])
    return _pcall(
        body, grid_spec=grid_spec,
        out_shape=[jax.ShapeDtypeStruct((T, D), F32), jax.ShapeDtypeStruct((T, D), F32), jax.ShapeDtypeStruct((T, D), BF16),
                   jax.ShapeDtypeStruct((T, D), F32), jax.ShapeDtypeStruct((T, D), BF16)], name="attn_bwd",
        compiler_params=pltpu.CompilerParams(dimension_semantics=("arbitrary", "arbitrary")),
    )(ii, jj, qn, qpe, kv, kpe, kv, do, lse, delta)


def _mesh_pos():
    x, y, c = lax.axis_index("x"), lax.axis_index("y"), lax.axis_index("c")
    return x, y, c, 4 * x + 2 * y + c


def _peer(x, y, c, k):
    px, py, pc = (x + ((k >> 2) & 1)) % 2, (y + ((k >> 1) & 1)) % 2, (c + (k & 1)) % 2
    return (px, py, pc), 4 * px + 2 * py + pc


_ANY = pl.BlockSpec(memory_space=pl.ANY)


def _rcopy(src, dst, send_sems, recv_sems, k, to):
    return pltpu.make_async_remote_copy(src_ref=src, dst_ref=dst, send_sem=send_sems.at[k], recv_sem=recv_sems.at[k],
                                        device_id=to, device_id_type=pl.DeviceIdType.MESH)


def _all_gather(wsh, ssh):
    def body(w_ref, s_ref, wall_ref, sall_ref, send_sems, recv_sems, loc_sems):
        x, y, c, me = _mesh_pos()
        sib = (x, y, 1 - c)
        chips = [(1 - x, y), (x, 1 - y), (1 - x, 1 - y)]
        slot = lambda px, py, pc: 4 * px + 2 * py + pc
        loc = [pltpu.make_async_copy(s_ref, sall_ref.at[me], loc_sems.at[0])]
        for cp in loc:
            cp.start()
        sends, fwd_waits = [], []
        for n, (src, dst) in enumerate(((w_ref, wall_ref), (s_ref, sall_ref))):
            o = 7 * n
            sends.append(_rcopy(src, dst.at[me], send_sems, recv_sems, o, sib))
            for j, chip in enumerate(chips):
                sends.append(_rcopy(src, dst.at[me], send_sems, recv_sems, o + 1 + j, (*chip, c)))
        for cp in sends:
            cp.start()
        for n, (src, dst) in enumerate(((w_ref, wall_ref), (s_ref, sall_ref))):
            o = 7 * n
            for j, chip in enumerate(chips):
                got = dst.at[slot(*chip, c)]
                _rcopy(src, got, send_sems, recv_sems, o + 1 + j, sib).wait_recv()
                fw = _rcopy(got, got, send_sems, recv_sems, o + 4 + j, sib)
                fw.start()
                sends.append(fw)
            fwd_waits.append(_rcopy(src, dst.at[slot(x, y, 1 - c)], send_sems, recv_sems, o, sib))
            for j, chip in enumerate(chips):
                fwd_waits.append(_rcopy(src, dst.at[slot(*chip, 1 - c)], send_sems, recv_sems, o + 4 + j, sib))
        for cp in fwd_waits:
            cp.wait_recv()
        for cp in sends:
            cp.wait_send()
        for cp in loc:
            cp.wait()

    wall, sall = _pcall(
        body, in_specs=[_ANY, _ANY], out_specs=[_ANY, _ANY],
        out_shape=[jax.ShapeDtypeStruct((N_DEV,) + wsh.shape, wsh.dtype), jax.ShapeDtypeStruct((N_DEV,) + ssh.shape, ssh.dtype)],
        scratch_shapes=[pltpu.SemaphoreType.DMA((14,)), pltpu.SemaphoreType.DMA((14,)), pltpu.SemaphoreType.DMA((1,))],
        name="all_gather_weights",
    )(wsh, ssh)
    me = 4 * lax.axis_index("x") + 2 * lax.axis_index("y") + lax.axis_index("c")
    return lax.dynamic_update_index_in_dim(wall, wsh, me, axis=0), sall


_HBM = pl.BlockSpec(memory_space=pltpu.HBM)
_SEM = pl.BlockSpec(memory_space=pltpu.SEMAPHORE)
_EFFECT = pltpu.SideEffectType.DATAFLOW_SIDE_EFFECTING


def _other_chips(x, y):
    return [(1 - x, y), (x, 1 - y), (1 - x, 1 - y)]


def _gather_start(wsh, order_after):
    def body(w_ref, land_ref, dep_ref, send_sems, recv_sems, w_thru, land_thru, token):
        x, y, c, me = _mesh_pos()
        _rcopy(w_ref, land_ref.at[me], send_sems, recv_sems, 0, (x, y, 1 - c)).start()
        for j, chip in enumerate(_other_chips(x, y)):
            _rcopy(w_ref, land_ref.at[me], send_sems, recv_sems, 1 + j, (*chip, c)).start()
        token[...] = jnp.zeros_like(token)

    shape = (N_DEV,) + wsh.shape
    land = pltpu.with_memory_space_constraint(lax.empty(shape, wsh.dtype), pltpu.HBM)
    return _pcall(
        body, name="gather_start",
        out_shape=(pltpu.SemaphoreType.DMA((4,)), pltpu.SemaphoreType.DMA((4,)), pltpu.HBM(wsh.shape, wsh.dtype),
                   pltpu.HBM(shape, wsh.dtype), jax.ShapeDtypeStruct((8, 128), F32)),
        in_specs=(_HBM, _HBM, _ANY), out_specs=(_SEM, _SEM, _HBM, _HBM, pl.BlockSpec(memory_space=pltpu.VMEM)),
        input_output_aliases={0: 2, 1: 3}, compiler_params=pltpu.CompilerParams(has_side_effects=_EFFECT),
    )(pltpu.with_memory_space_constraint(wsh, pltpu.HBM), land, order_after)


def _gather_wait(send_sems, recv_sems, w_thru, land_thru, after):
    def body(w_ref, land_ref, send_sems, recv_sems, after_ref, w_out, land_out):
        x, y, c, _ = _mesh_pos()
        senders = [(x, y, 1 - c)] + [(*chip, c) for chip in _other_chips(x, y)]
        for k, (px, py, pc) in enumerate(senders):
            cp = _rcopy(w_ref, land_ref.at[4 * px + 2 * py + pc], send_sems, recv_sems, k, (px, py, pc))
            cp.wait_send()
            cp.wait_recv()

    return _pcall(
        body, name="gather_wait", out_shape=(pltpu.HBM(w_thru.shape, w_thru.dtype), pltpu.HBM(land_thru.shape, land_thru.dtype)),
        in_specs=(_HBM, _HBM, _SEM, _SEM, _ANY), out_specs=(_HBM, _HBM), input_output_aliases={0: 0, 1: 1},
        compiler_params=pltpu.CompilerParams(has_side_effects=_EFFECT),
    )(w_thru, land_thru, send_sems, recv_sems, after)


def _gather_finish(wsh, land):
    def body(land_ref, out_ref, send_sems, recv_sems):
        x, y, c, _ = _mesh_pos()
        sib = (x, y, 1 - c)
        sends, recvs = [], []
        for j, (px, py) in enumerate(_other_chips(x, y)):
            mine, theirs = 4 * px + 2 * py + c, 4 * px + 2 * py + (1 - c)
            sends.append(_rcopy(land_ref.at[mine], out_ref.at[mine], send_sems, recv_sems, j, sib))
            recvs.append(_rcopy(land_ref.at[theirs], out_ref.at[theirs], send_sems, recv_sems, j, sib))
        for cp in sends:
            cp.start()
        for cp in recvs:
            cp.wait_recv()
        for cp in sends:
            cp.wait_send()

    done = _pcall(
        body, in_specs=[_ANY], out_specs=_ANY, out_shape=jax.ShapeDtypeStruct(land.shape, land.dtype),
        input_output_aliases={0: 0},
        scratch_shapes=[pltpu.SemaphoreType.DMA((3,)), pltpu.SemaphoreType.DMA((3,))], name="gather_finish",
    )(land)
    me = 4 * lax.axis_index("x") + 2 * lax.axis_index("y") + lax.axis_index("c")
    return lax.dynamic_update_index_in_dim(done, wsh, me, axis=0)


_ROW_TILES = (736, 528, 400, 368, 352, 256, 128, 16, 8)


def _exchange_sibling(g4, name):
    def body(g_ref, r_ref, send_sems, recv_sems):
        x, y, c, _ = _mesh_pos()
        sib = (x, y, 1 - c)
        cps = [_rcopy(g_ref.at[q, 1 - c], r_ref.at[q], send_sems, recv_sems, q, sib) for q in range(4)]
        for cp in cps:
            cp.start()
        for cp in cps:
            cp.wait()

    n, _, R, C = g4.shape
    return _pcall(
        body, in_specs=[_ANY], out_specs=_ANY, out_shape=jax.ShapeDtypeStruct((n, R, C), g4.dtype),
        scratch_shapes=[pltpu.SemaphoreType.DMA((4,)), pltpu.SemaphoreType.DMA((4,))], name=name,
    )(g4)


def _sum_sibling(g4, rsib, cvec, name):
    n, _, R, C = g4.shape
    tr = _pick(R, _ROW_TILES)

    def body(c_ref, g_ref, r_ref, o_ref):
        o_ref[...] = (g_ref[...].astype(F32) + r_ref[...].astype(F32)).astype(o_ref.dtype)

    grid_spec = pltpu.PrefetchScalarGridSpec(
        num_scalar_prefetch=1, grid=(n, R // tr),
        in_specs=[pl.BlockSpec((None, None, tr, C), lambda q, i, cr: (q, cr[0], i, 0)), pl.BlockSpec((None, tr, C), lambda q, i, cr: (q, i, 0))],
        out_specs=pl.BlockSpec((None, tr, C), lambda q, i, cr: (q, i, 0)))
    return _pcall(body, grid_spec=grid_spec, out_shape=jax.ShapeDtypeStruct((n, R, C), g4.dtype), name=name)(cvec, g4, rsib)


def _chips_start(part, name, order_after=None):
    n_in = 2 + int(order_after is not None)

    def body(*refs):
        p_ref, land_ref = refs[:2]
        send_sems, recv_sems, token = refs[n_in], refs[n_in + 1], refs[n_in + 4]
        x, y, c, _ = _mesh_pos()
        myq = 2 * x + y
        for j, (px, py) in enumerate(_other_chips(x, y)):
            _rcopy(p_ref.at[2 * px + py], land_ref.at[myq], send_sems, recv_sems, j, (px, py, c)).start()
        token[...] = jnp.zeros_like(token)

    land = pltpu.with_memory_space_constraint(lax.empty(part.shape, part.dtype), pltpu.HBM)
    extra = () if order_after is None else (order_after,)
    return _pcall(
        body, name=name,
        out_shape=(pltpu.SemaphoreType.DMA((3,)), pltpu.SemaphoreType.DMA((3,)), pltpu.HBM(part.shape, part.dtype),
                   pltpu.HBM(part.shape, part.dtype), jax.ShapeDtypeStruct((8, 128), F32)),
        in_specs=(_HBM, _HBM) + (_ANY,) * len(extra),
        out_specs=(_SEM, _SEM, _HBM, _HBM, pl.BlockSpec(memory_space=pltpu.VMEM)),
        input_output_aliases={0: 2, 1: 3}, compiler_params=pltpu.CompilerParams(has_side_effects=_EFFECT),
    )(pltpu.with_memory_space_constraint(part, pltpu.HBM), land, *extra)


def _chips_wait(send_sems, recv_sems, p_thru, land_thru, after, name):
    def body(p_ref, land_ref, send_sems, recv_sems, after_ref, p_out, land_out):
        x, y, c, _ = _mesh_pos()
        for j, (px, py) in enumerate(_other_chips(x, y)):
            q = 2 * px + py
            cp = _rcopy(p_ref.at[q], land_ref.at[q], send_sems, recv_sems, j, (px, py, c))
            cp.wait_send()
            cp.wait_recv()

    return _pcall(
        body, name=name, out_shape=(pltpu.HBM(p_thru.shape, p_thru.dtype), pltpu.HBM(p_thru.shape, p_thru.dtype)),
        in_specs=(_HBM, _HBM, _SEM, _SEM, _ANY), out_specs=(_HBM, _HBM), input_output_aliases={0: 0, 1: 1},
        compiler_params=pltpu.CompilerParams(has_side_effects=_EFFECT),
    )(p_thru, land_thru, send_sems, recv_sems, after)


def _sum_chips(part, land, qvec, name):
    n, R, C = part.shape
    tr = _pick(R, _ROW_TILES)

    def body(q_ref, p_ref, *refs):
        o_ref = refs[n]
        acc = None
        for q in range(n):
            term = jnp.where(q_ref[0] == q, p_ref[...], refs[q][...]).astype(F32)
            acc = term if acc is None else acc + term
        o_ref[...] = acc

    def land_spec(q):
        return pl.BlockSpec((None, tr, C), lambda i, m: (jnp.where(m[0] == q, (q + 1) % n, q), i, 0))

    grid_spec = pltpu.PrefetchScalarGridSpec(
        num_scalar_prefetch=1, grid=(R // tr,),
        in_specs=[pl.BlockSpec((None, tr, C), lambda i, m: (m[0], i, 0))] + [land_spec(q) for q in range(n)],
        out_specs=pl.BlockSpec((tr, C), lambda i, m: (i, 0)))
    return _pcall(body, grid_spec=grid_spec, out_shape=jax.ShapeDtypeStruct((R, C), F32), name=name)(qvec, part, *([land] * n))


def _exchange_small(gsmall):
    def body(s_ref, srecv_ref, send_sems, recv_sems, loc_sem):
        x, y, c, me = _mesh_pos()
        loc = pltpu.make_async_copy(s_ref, srecv_ref.at[me], loc_sem.at[0])
        loc.start()
        sends, recvs = [], []
        for k in range(1, N_DEV):
            to, pidx = _peer(x, y, c, k)
            sends.append(_rcopy(s_ref, srecv_ref.at[me], send_sems, recv_sems, k - 1, to))
            recvs.append(_rcopy(s_ref, srecv_ref.at[pidx], send_sems, recv_sems, k - 1, to))
        for cp in sends:
            cp.start()
        for cp in recvs:
            cp.wait_recv()
        for cp in sends:
            cp.wait_send()
        loc.wait()

    return _pcall(
        body, in_specs=[_ANY], out_specs=_ANY, out_shape=jax.ShapeDtypeStruct((N_DEV,) + gsmall.shape, gsmall.dtype),
        scratch_shapes=[pltpu.SemaphoreType.DMA((7,)), pltpu.SemaphoreType.DMA((7,)), pltpu.SemaphoreType.DMA((1,))],
        name="exchange_small",
    )(gsmall)


def _sum_slots(recv, name):
    n, R, C = recv.shape
    tr = _pick(R, _ROW_TILES)

    def body(r_ref, o_ref):
        acc = r_ref[0].astype(F32)
        for s in range(1, n):
            acc = acc + r_ref[s].astype(F32)
        o_ref[...] = acc

    return _pcall(body, grid=(R // tr,), in_specs=[pl.BlockSpec((n, tr, C), lambda i: (0, i, 0))],
                  out_specs=pl.BlockSpec((tr, C), lambda i: (i, 0)), out_shape=jax.ShapeDtypeStruct((R, C), F32), name=name)(recv)


def _adamw(w, g, m, v, name, order_after=None):
    shape = w.shape
    w2, g2, m2, v2 = (a.reshape(-1, shape[-1]) for a in (w, g, m, v))
    R, C = w2.shape
    tr = _pick(R, (256, 128, 64, 32, 16, 8)) if R > 256 else R
    n_in = 4 + int(order_after is not None)

    def body(*refs):
        w_ref, g_ref, m_ref, v_ref = refs[:4]
        d_ref, nm_ref, nv_ref = refs[n_in:]
        gg = g_ref[...]
        nm = ADAM_B1 * m_ref[...] + (1.0 - ADAM_B1) * gg
        nv = ADAM_B2 * v_ref[...] + (1.0 - ADAM_B2) * jnp.square(gg)
        m_hat = nm / (1.0 - ADAM_B1 ** ADAM_STEP)
        v_hat = nv / (1.0 - ADAM_B2 ** ADAM_STEP)
        d_ref[...] = -ADAM_LR * (m_hat / (jnp.sqrt(v_hat) + ADAM_EPS) + ADAM_WD * w_ref[...])
        nm_ref[...] = nm
        nv_ref[...] = nv

    spec = pl.BlockSpec((tr, C), lambda i: (i, 0))
    in_specs, args = [spec] * 4, [w2, g2, m2, v2]
    if order_after is not None:
        in_specs.append(pl.BlockSpec(memory_space=pl.ANY))
        args.append(order_after)
    d, nm, nv = _pcall(body, grid=(R // tr,), in_specs=in_specs, out_specs=[spec] * 3,
                       out_shape=[jax.ShapeDtypeStruct((R, C), F32)] * 3, name=name)(*args)
    return d.reshape(shape), nm.reshape(shape), nv.reshape(shape)


def _to_rows(name, w):
    w = w[0]
    if name in ("ffn1_w_in", "ffn2_w_in", "w_in"):
        return w.T
    if name in ("w_uq", "w_ukv", "ple_w_proj"):
        return w.T.reshape(-1, 1024)
    return w


def _from_rows(name, g, shape):
    if name in ("ffn1_w_in", "ffn2_w_in", "w_in"):
        return g.T.reshape(shape)
    if name in ("w_uq", "w_ukv", "ple_w_proj"):
        return g.reshape(-1, shape[1]).T.reshape(shape)
    return g.reshape(shape)


def _unpack(wall, group):
    out, off = {}, 0
    for name, r in group:
        out[name] = wall[:, off:off + r, :].reshape(N_DEV * r, 1024)
        off += _pad16(r)
    return out


def _w_in_internal(wt):
    return jnp.concatenate([wt[0:6144], wt[6720:8768], wt[6144:6720], jnp.zeros((P_W - W_IN_COLS, 1024), wt.dtype)], axis=0)


def _rope_tables(positions):
    pos = positions[0].astype(F32)

    def cs(half):
        inv = jnp.tile(ROPE_BASE ** (-jnp.arange(half, dtype=F32) / half), 128 // half)
        sign = jnp.tile(jnp.concatenate([-jnp.ones((half,), F32), jnp.ones((half,), F32)]), 64 // half)
        ang = pos[:, None] * inv
        return jnp.cos(ang), jnp.sin(ang) * sign

    return cs(64) + cs(32)


def _local_step(x, p, positions, target, W, ln_g, ln_b, gn_g, qg, kvg, early=None, rest_weights=None, start_token=None):
    T = x.shape[0]
    tabs = _rope_tables(positions)
    rc = _ret_consts()
    lg = [ln_g[i:i + 1] for i in range(4)]
    lb = [ln_b[i:i + 1] for i in range(4)]
    pb = p.astype(BF16)

    hs1, gu1, *xb = _ffn_in(x, W["ffn1_w_in"], "ffn1_in", order_after=start_token)
    xb = xb[0] if xb else x
    f1, h1, h1b = _mm(hs1, W["ffn1_w_out"], name="ffn1_out", tm=LN_TM, epilogue=_ln_epilogue(x, 0.5, lg[0], lb[0]))
    if rest_weights is not None:
        W = {**W, **rest_weights(h1b)}
    w_in_t = _w_in_internal(W["w_in"])
    wuq = W["w_uq"].reshape(1536, LORA).reshape(HEADS, 192, LORA)
    wuq = jnp.concatenate([wuq[:, :128].reshape(1024, LORA), wuq[:, 128:].reshape(512, LORA)], axis=0)
    wukv = W["w_ukv"].reshape(2048, LORA).reshape(HEADS, 2, 128, LORA).transpose(1, 0, 2, 3).reshape(2048, LORA)
    wp_t = W["ple_w_proj"].reshape(1024, D_PLE)
    proj = _mm(h1b, w_in_t, tb=True, out_dtype=BF16, name="mixer_in")
    rq, rk, qn, kvn, kpe = _mixer_prep_fwd(proj, tabs, qg, kvg, T)
    y, yr, states = _ret_fwd(rq, rk, proj, proj, gn_g, rc, T)
    y_ret = _mm(yr, W["w_ret_o"], out_dtype=BF16, name="ret_o")
    qnope, qpe = _mm(qn, wuq, tb=True, name="mla_uq", tm=LN_TM, tn=1536, epilogue=_q_assemble_epilogue(tabs))
    kv = _mm(kvn, wukv, tb=True, out_dtype=BF16, name="mla_ukv")
    o, lse = _attn_fwd(qnope, qpe, kv, kpe, T)
    y_mla, mix = _mm(o, W["w_mla_o"], name="mla_o", tm=LN_TM, epilogue=_mix_epilogue(proj, y_ret))
    mixed, h2, h2b = _mm(mix, W["w_out"], name="mixer_out", tm=LN_TM, epilogue=_ln_epilogue(h1, 1.0, lg[1], lb[1]))
    hs2, gu2 = _ffn_in(h2b, W["ffn2_w_in"], "ffn2_in")
    f2, h3, h3b = _mm(hs2, W["ffn2_w_out"], name="ffn2_out", tm=LN_TM, epilogue=_ln_epilogue(h2, 0.5, lg[2], lb[2]))
    pp = _mm(pb, wp_t, tb=True, name="ple_proj")

    G = {}
    dh3_a, dgl, dpp, dg3, db3, loss = _mm(h3b, W["ple_w_gate"], name="ple_gate", tm=LN_TM,
                                          epilogue=_head_epilogue(h3, pp, target, lg[3], lb[3]))
    G["ple_w_gate"] = _mm(h3b, dgl, ta=True, name="d_ple_gate")
    G["ple_w_proj"] = _mm(dpp, pb, ta=True, name="d_ple_proj")
    dh2_a, df2, dg2, db2 = _mm(dgl, W["ple_w_gate"], tb=True, add=dh3_a, name="dh3", tm=LN_TM,
                               epilogue=_ln_bwd_epilogue(h2, f2, 0.5, lg[2], lb[2]))
    wide = dict(tm=D_FF, tk=min(1024, T))
    G["ffn2_w_out"] = _mm(hs2, df2, ta=True, name="d_ffn2_out", **wide)
    da2 = _ffn_act_bwd(df2, W["ffn2_w_out"], gu2, "ffn2_act_bwd")
    G["ffn2_w_in"] = _mm(da2, h2b, ta=True, name="d_ffn2_in", **wide)
    dh1_a, dmixed, dg1, db1 = _mm(da2, W["ffn2_w_in"], add=dh2_a, name="dh2", tm=LN_TM,
                                  epilogue=_ln_bwd_epilogue(h1, mixed, 1.0, lg[1], lb[1]))
    G["w_out"] = _mm(mix, dmixed, ta=True, name="d_mixer_out")
    dp_gate, dy_ret, dy_mla = _mm(dmixed, W["w_out"], tb=True, name="dmix", tm=LN_TM,
                                  epilogue=_mix_bwd_epilogue(proj, y_ret, y_mla))
    G["w_mla_o"] = _mm(o, dy_mla, ta=True, name="d_mla_o")
    dob, delta = _mm(dy_mla, W["w_mla_o"], tb=True, name="do", tm=LN_TM, epilogue=_delta_epilogue(o))
    dqn_f, dqpe_f, dkn, dkpe_all, dv = _attn_bwd(qnope, qpe, kv, kpe, dob, lse, delta, T)
    dq_n, dq_r = _q_assemble_bwd(dqn_f, dqpe_f, tabs, T)
    g_uq = jnp.concatenate([_mm(dq_n, qn, ta=True, name="d_uq_nope"), _mm(dq_r, qn, ta=True, name="d_uq_rope")], axis=0)
    g_uq = jnp.concatenate([g_uq[:1024].reshape(HEADS, 128, LORA), g_uq[1024:].reshape(HEADS, 64, LORA)], axis=1)
    G["w_uq"] = g_uq.reshape(1536 * LORA // 1024, 1024)
    dqn = _mm(dq_r, wuq[1024:], add=_mm(dq_n, wuq[:1024], name="dqn_a"), name="dqn_b")
    g_ukv = jnp.stack([_mm(dkn, kvn, ta=True, name="d_ukv_k"), _mm(dv, kvn, ta=True, name="d_ukv_v")], axis=0)
    G["w_ukv"] = g_ukv.reshape(2, HEADS, 128, LORA).transpose(1, 0, 2, 3).reshape(2048 * LORA // 1024, 1024)
    dkvn = _mm(dv, wukv[1024:], add=_mm(dkn, wukv[:1024], name="dkvn_a"), name="dkvn_b")
    dp_mla, dqg, dkvg = _rms_bwd(proj, dqn, dkvn, dkpe_all, tabs, qg, kvg, T)
    G["w_ret_o"] = _mm(yr, dy_ret, ta=True, name="d_ret_o")
    dyr = _mm(dy_ret, W["w_ret_o"], tb=True, out_dtype=BF16, name="dyr")
    dp_ret, dgn = _ret_bwd(rq, rk, proj, y, proj, gn_g, states, dyr, rc, tabs, T)
    g_ret, g_gate, g_mla = (_mm(dp, h1b, ta=True, name="d_mixer_in_" + n)
                            for dp, n in ((dp_ret, "ret"), (dp_gate, "gate"), (dp_mla, "mla")))
    G["w_in"] = jnp.concatenate([g_ret, g_mla[:W_IN_COLS - P_CQ], g_gate], axis=0)
    lg0 = lg[0] if early is None else lg[0] + early(G)[0:1, 0:1]
    dh1 = _mm_pieces([(dp_ret, w_in_t[:P_GR]), (dp_gate, w_in_t[P_GR:P_CQ]), (dp_mla, w_in_t[P_CQ:])], dh1_a, "dh1")
    dx_a, df1, dg0, db0 = _ln_bwd_rows(x, f1, 0.5, lg0, lb[0], dh1, T, "ln0_bwd")
    G["ffn1_w_out"] = _mm(hs1, df1, ta=True, name="d_ffn1_out", **wide)
    da1 = _ffn_act_bwd(df1, W["ffn1_w_out"], gu1, "ffn1_act_bwd")
    G["ffn1_w_in"] = _mm(da1, xb, ta=True, name="d_ffn1_in", **wide)
    grad_x = _mm(da1, W["ffn1_w_in"], add=dx_a, name="grad_x")

    small = dict(ln_g=jnp.concatenate([dg0, dg1, dg2, dg3], axis=0), ln_b=jnp.concatenate([db0, db1, db2, db3], axis=0),
                 ret_gn_g=dgn, q_norm_g=dqg, kv_norm_g=dkvg)
    return loss, grad_x, G, small


def kernel(x, p, positions, ln_g, ln_b, ffn1_w_in, ffn1_w_out, w_in, ret_gn_g, w_ret_o, q_norm_g, kv_norm_g, w_uq, w_ukv, w_mla_o, w_out, ffn2_w_in, ffn2_w_out, ple_w_gate, ple_w_proj, loss_target, m_ln_g, m_ln_b, m_ffn1_w_in, m_ffn1_w_out, m_w_in, m_ret_gn_g, m_w_ret_o, m_q_norm_g, m_kv_norm_g, m_w_uq, m_w_ukv, m_w_mla_o, m_w_out, m_ffn2_w_in, m_ffn2_w_out, m_ple_w_gate, m_ple_w_proj, v_ln_g, v_ln_b, v_ffn1_w_in, v_ffn1_w_out, v_w_in, v_ret_gn_g, v_w_ret_o, v_q_norm_g, v_kv_norm_g, v_w_uq, v_w_ukv, v_w_mla_o, v_w_out, v_ffn2_w_in, v_ffn2_w_out, v_ple_w_gate, v_ple_w_proj):
    names = ("ln_g", "ln_b", "ffn1_w_in", "ffn1_w_out", "w_in", "ret_gn_g", "w_ret_o", "q_norm_g", "kv_norm_g", "w_uq", "w_ukv",
             "w_mla_o", "w_out", "ffn2_w_in", "ffn2_w_out", "ple_w_gate", "ple_w_proj")
    ws = dict(zip(names, (ln_g, ln_b, ffn1_w_in, ffn1_w_out, w_in, ret_gn_g, w_ret_o, q_norm_g, kv_norm_g, w_uq, w_ukv, w_mla_o,
                          w_out, ffn2_w_in, ffn2_w_out, ple_w_gate, ple_w_proj)))
    ms = dict(zip(names, (m_ln_g, m_ln_b, m_ffn1_w_in, m_ffn1_w_out, m_w_in, m_ret_gn_g, m_w_ret_o, m_q_norm_g, m_kv_norm_g, m_w_uq,
                          m_w_ukv, m_w_mla_o, m_w_out, m_ffn2_w_in, m_ffn2_w_out, m_ple_w_gate, m_ple_w_proj)))
    vs = dict(zip(names, (v_ln_g, v_ln_b, v_ffn1_w_in, v_ffn1_w_out, v_w_in, v_ret_gn_g, v_w_ret_o, v_q_norm_g, v_kv_norm_g, v_w_uq,
                          v_w_ukv, v_w_mla_o, v_w_out, v_ffn2_w_in, v_ffn2_w_out, v_ple_w_gate, v_ple_w_proj)))

    parts = []
    for name, r in PACK:
        rows = _to_rows(name, ws[name])
        if _pad16(r) != r:
            rows = jnp.concatenate([rows, jnp.zeros((_pad16(r) - r, 1024), F32)], axis=0)
        parts.append(rows)
    wsh_first = jnp.concatenate(parts[:len(PACK_LATE)], axis=0).astype(BF16)
    wsh_rest = jnp.concatenate(parts[len(PACK_LATE):], axis=0).astype(BF16)
    ssh = jnp.concatenate([ln_g[0], ln_b[0]], axis=0)
    wall_first, sall = _all_gather(wsh_first, ssh)
    *gather_handles, start_token = _gather_start(wsh_rest, wall_first)
    W = _unpack(wall_first, PACK_LATE)
    ln_full = sall.reshape(N_DEV, 2, 4, 128).transpose(1, 2, 0, 3).reshape(2, 4, 1024)

    def rest_weights(after):
        w_thru, land = _gather_wait(*gather_handles, after)
        return _unpack(_gather_finish(w_thru, land), PACK_EARLY)

    cvec = lax.axis_index("c").astype(jnp.int32).reshape(1)
    qvec = (2 * lax.axis_index("x") + lax.axis_index("y")).astype(jnp.int32).reshape(1)

    def chip_partials(G, group, tag):
        gparts = []
        for name, r in group:
            g = G[name].reshape(N_DEV, r, 1024)
            if _pad16(r) != r:
                g = jnp.concatenate([g, jnp.zeros((N_DEV, _pad16(r) - r, 1024), g.dtype)], axis=1)
            gparts.append(g)
        gfull = jnp.concatenate(gparts, axis=1).astype(BF16)
        g4 = gfull.reshape(4, 2, gfull.shape[1], 1024)
        return _sum_sibling(g4, _exchange_sibling(g4, "exchange_sibling_" + tag), cvec, "sum_sibling_" + tag)

    in_flight = []

    def early(G):
        *handles, token = _chips_start(chip_partials(G, PACK_EARLY, "early"), "chips_start_early")
        in_flight.append(handles)
        return token

    loss_p, grad_x, G, small = _local_step(x[0], p[0, 0], positions, loss_target[0], W, ln_full[0], ln_full[1],
                                           ret_gn_g, q_norm_g, kv_norm_g, early=early, rest_weights=rest_weights,
                                           start_token=start_token)

    part_e, land_e = _chips_wait(*in_flight[0], grad_x, "chips_wait_early")
    gsh_early = _sum_chips(part_e, land_e, qvec, "sum_grads_early")
    pad256 = lambda a: jnp.concatenate([a, jnp.zeros((1, 1024 - a.shape[1]), F32)], axis=1)
    gsmall = jnp.concatenate([small["ln_g"], small["ln_b"], small["ret_gn_g"].reshape(2, 1024), pad256(small["q_norm_g"]),
                              pad256(small["kv_norm_g"]), pad256(loss_p[0:1, 0:128]),
                              jnp.zeros((SMALL_ROWS - 13, 1024), F32)], axis=0)
    srecv = _exchange_small(gsmall)
    part_l = chip_partials(G, PACK_LATE, "late")
    *late_handles, late_token = _chips_start(part_l, "chips_start_late", order_after=srecv)
    ssum = _sum_slots(srecv, "sum_small_grads")

    def unpack_grads(group, gsh):
        out, off = {}, 0
        for name, r in group:
            out[name] = _from_rows(name, gsh[off:off + r], ws[name].shape)
            off += _pad16(r)
        return out

    grads = unpack_grads(PACK_EARLY, gsh_early)
    me = 4 * lax.axis_index("x") + 2 * lax.axis_index("y") + lax.axis_index("c")
    grads["ln_g"] = lax.dynamic_slice(ssum[0:4], (0, me * 128), (4, 128)).reshape(1, 4, 128)
    grads["ln_b"] = lax.dynamic_slice(ssum[4:8], (0, me * 128), (4, 128)).reshape(1, 4, 128)
    grads["ret_gn_g"] = ssum[8:10].reshape(1, 2048)
    grads["q_norm_g"] = ssum[10:11, :256]
    grads["kv_norm_g"] = ssum[11:12, :256]

    delta, new_m, new_v = {}, {}, {}
    late_names = [n for n, _ in PACK_LATE]
    last = late_token
    for name in names:
        if name not in late_names:
            delta[name], new_m[name], new_v[name] = _adamw(ws[name], grads[name], ms[name], vs[name], "adamw_" + name,
                                                           order_after=last)
            last = new_v[name]
    part_l, land_l = _chips_wait(*late_handles, last, "chips_wait_late")
    grads.update(unpack_grads(PACK_LATE, _sum_chips(part_l, land_l, qvec, "sum_grads_late")))
    for name in late_names:
        delta[name], new_m[name], new_v[name] = _adamw(ws[name], grads[name], ms[name], vs[name], "adamw_" + name)

    return (ssum[12, 0], grad_x[None], *[grads[n] for n in names], *[delta[n] for n in names],
            *[new_m[n] for n in names], *[new_v[n] for n in names])
```

```python
import math

import jax
import jax.numpy as jnp
from jax import lax
from jax.experimental import pallas as pl
from jax.experimental.pallas import tpu as pltpu

F32 = jnp.float32
BF16 = jnp.bfloat16

N_DEV = 8
D = 1024
D_FF = 2816
D_PLE = 256
CHUNK = 64
HEADS = 8
RET_DK = 128
RET_DV = 256
MLA_NOPE = 128
MLA_ROPE = 64
MLA_DV = 128
LORA = 256
ROPE_BASE = 10000.0
EPS = 1e-5
ALPHA = 2.0 ** 0.25
RET_SCALE = RET_DK ** -0.5
MLA_SCALE = (MLA_NOPE + MLA_ROPE) ** -0.5
NEG = -1e30

ADAM_LR = 0.001
ADAM_B1 = 0.9
ADAM_B2 = 0.999
ADAM_EPS = 1e-08
ADAM_WD = 0.01
ADAM_STEP = 10

P_RQ, P_RK, P_RV, P_RG, P_GR, P_GM, P_CQ, P_CKV, P_KPE, P_W = 0, 1024, 2048, 4096, 6144, 7168, 8192, 8448, 8704, 8960
W_IN_COLS = 8768
RET_L = 256
ATT_TF = 2048
ATT_TB = 1024
ATT_HP = 1
LOG2E = math.log2(math.e)
Q_PRESCALE = MLA_SCALE * LOG2E

PACK = (("ffn1_w_in", 704), ("ffn1_w_out", 352), ("w_in", 1096), ("w_ret_o", 256), ("w_uq", 48), ("w_ukv", 64),
        ("w_mla_o", 128), ("w_out", 128), ("ffn2_w_in", 704), ("ffn2_w_out", 352), ("ple_w_gate", 128), ("ple_w_proj", 32))


def _pad16(r):
    return -(-r // 16) * 16


PACK_LATE = PACK[:2]
PACK_EARLY = PACK[2:]
SMALL_ROWS = 16


def _pcall(body, **kw):
    return pl.pallas_call(body, **kw)


def _pick(dim, prefs):
    for p in prefs:
        if dim % p == 0:
            return p
    return dim


def _sigmoid(x):
    return 1.0 / (1.0 + jnp.exp(-x))


def _silu(x):
    return x * _sigmoid(x)


def _ln(r, g, b):
    mu = jnp.mean(r, axis=-1, keepdims=True)
    var = jnp.mean(jnp.square(r - mu), axis=-1, keepdims=True)
    return (r - mu) * lax.rsqrt(var + EPS) * g + b


def _rms(x, g):
    return x * lax.rsqrt(jnp.mean(jnp.square(x), axis=-1, keepdims=True) + EPS) * g


def _dot(a, b, ca, cb):
    return lax.dot_general(a, b, (((ca,), (cb,)), ((), ())), preferred_element_type=F32)


def _accum(ref, val, first=None):
    @pl.when(pl.program_id(0) == 0 if first is None else first)
    def _():
        ref[...] = jnp.zeros_like(ref)

    ref[...] += val


def _mm(a, b, *, ta=False, tb=False, add=None, out_dtype=None, name, tm=None, tn=None, tk=None, epilogue=None):
    parts = a.shape[0] if a.ndim == 3 else 1
    ar, ac = a.shape[-2], a.shape[-1]
    out_dtype = out_dtype or (BF16 if ta else F32)
    if ta:
        K, M = ar, ac * parts
    else:
        M, K = ar, ac * parts
    if tb:
        N, K2 = b.shape
    else:
        K2, N = b.shape
    assert K == K2, (a.shape, b.shape, ta, tb)
    big = (1024, 1408, 1280, 768, 512, 256, 128)
    tm = tm or _pick(ac if (ta and parts > 1) else M, big)
    tn = tn or (N if N <= 1024 else _pick(N, big))
    kdim = ac if (not ta and parts > 1) else K
    tk = tk or (kdim if kdim <= 2816 else
                _pick(kdim, (2048, 1408, 1280, 1024, 512) if tm <= 1024 else (1024, 1408, 1280, 512)))
    nk = K // tk
    grid = (M // tm, N // tn, nk)
    if parts > 1 and ta:
        per = ac // tm
        a_spec = pl.BlockSpec((None, tk, tm), lambda i, j, k: (i // per, k, i % per))
    elif parts > 1:
        per = ac // tk
        a_spec = pl.BlockSpec((None, tm, tk), lambda i, j, k: (k // per, i, k % per))
    else:
        a_spec = pl.BlockSpec((tk, tm), lambda i, j, k: (k, i)) if ta else pl.BlockSpec((tm, tk), lambda i, j, k: (i, k))
    b_spec = pl.BlockSpec((tn, tk), lambda i, j, k: (j, k)) if tb else pl.BlockSpec((tk, tn), lambda i, j, k: (k, j))
    o_spec = pl.BlockSpec((tm, tn), lambda i, j, k: (i, j))
    ca, cb = (0 if ta else 1), (1 if tb else 0)
    has_add = add is not None
    n_in = 2 + int(has_add)
    if epilogue is not None:
        assert tn == N and not ta
        ep_fn, ep_rows, ep_whole, ep_outs, ep_accs = epilogue
        n_ep_in = len(ep_rows) + len(ep_whole)
        n_out = len(ep_outs) + len(ep_accs)
    else:
        n_ep_in, n_out = 0, 1

    def body(*refs):
        a_ref, b_ref = refs[0], refs[1]
        add_ref = refs[2] if has_add else None
        o_ref = refs[n_in + n_ep_in]
        first_row_tile = pl.program_id(0) == 0

        def finish(r):
            if has_add:
                r = r + add_ref[...].astype(F32)
            if epilogue is not None:
                ep_fn(r, refs[n_in:n_in + n_ep_in], refs[n_in + n_ep_in:n_in + n_ep_in + n_out], first_row_tile)
            else:
                o_ref[...] = r.astype(out_dtype)

        if nk == 1:
            finish(_dot(a_ref[...], b_ref[...], ca, cb))
            return
        acc_ref = refs[-1]
        k = pl.program_id(2)

        @pl.when(k == 0)
        def _():
            acc_ref[...] = jnp.zeros_like(acc_ref)

        acc_ref[...] += _dot(a_ref[...], b_ref[...], ca, cb)

        @pl.when(k == nk - 1)
        def _():
            finish(acc_ref[...])

    in_specs = [a_spec, b_spec] + ([o_spec] if has_add else [])
    args = (a, b) + ((add,) if has_add else ())
    out_specs, out_shape = o_spec, jax.ShapeDtypeStruct((M, N), out_dtype)
    if epilogue is not None:
        for row_in in ep_rows:
            arr, col_block, width = (tuple(row_in) + (N,))[:3] if isinstance(row_in, tuple) else (row_in, 0, N)
            in_specs.append(pl.BlockSpec((tm, width), lambda i, j, k, _c=col_block: (i, _c)))
            args += (arr,)
        in_specs += [pl.BlockSpec(w.shape, lambda i, j, k, _n=w.ndim: (0,) * _n) for w in ep_whole]
        args += tuple(ep_whole)
        outs = [o if isinstance(o, tuple) else (o, N) for o in ep_outs]
        out_specs = ([pl.BlockSpec((tm, w), lambda i, j, k: (i, 0)) for _, w in outs]
                     + [pl.BlockSpec((r, N), lambda i, j, k: (0, 0)) for r in ep_accs])
        out_shape = [jax.ShapeDtypeStruct((M, w), dt) for dt, w in outs] + [jax.ShapeDtypeStruct((r, N), F32) for r in ep_accs]
    return _pcall(
        body, grid=grid, in_specs=in_specs, out_specs=out_specs, out_shape=out_shape,
        scratch_shapes=[pltpu.VMEM((tm, tn), F32)] if nk > 1 else [], name=name,
        compiler_params=pltpu.CompilerParams(dimension_semantics=("arbitrary" if epilogue is not None else "parallel", "parallel", "arbitrary")),
    )(*args)


def _mm_pieces(pieces, add, name):
    M, N = add.shape
    tm = _pick(M, (1024, 512, 256, 128))
    spans, in_specs, args, start = [], [], [], 0
    for a, b in pieces:
        K = a.shape[1]
        assert b.shape == (K, N) and a.shape[0] == M, (a.shape, b.shape)
        tk = _pick(K, (1536, 1024, 768, 512, 256, 128))
        n = K // tk
        spans.append((start, n))
        block = lambda k, _s=start, _n=n: jnp.where(k < _s, _n - 1, jnp.minimum(k - _s, _n - 1))
        in_specs.append(pl.BlockSpec((tm, tk), lambda i, k, _s=start, _blk=block: (jnp.where(k < _s, jnp.maximum(i - 1, 0), i), _blk(k))))
        in_specs.append(pl.BlockSpec((tk, N), lambda i, k, _blk=block: (_blk(k), 0)))
        args += [a, b]
        start += n
    o_spec = pl.BlockSpec((tm, N), lambda i, k: (i, 0))

    def body(*refs):
        add_ref, o_ref = refs[-2], refs[-1]
        k = pl.program_id(1)

        @pl.when(k == 0)
        def _():
            o_ref[...] = add_ref[...]

        for p, (s, n) in enumerate(spans):
            @pl.when((k >= s) & (k < s + n))
            def _(p=p):
                o_ref[...] += _dot(refs[2 * p][...], refs[2 * p + 1][...], 1, 0)

    return _pcall(
        body, grid=(M // tm, start), in_specs=in_specs + [o_spec], out_specs=o_spec, out_shape=jax.ShapeDtypeStruct((M, N), F32),
        name=name, compiler_params=pltpu.CompilerParams(dimension_semantics=("parallel", "arbitrary")),
    )(*args, add)


def _ln_epilogue(res, c, g, b):
    def fn(r, ins, outs, first):
        res_ref, g_ref, b_ref = ins
        f_ref, h_ref, hb_ref = outs
        h = _ln(ALPHA * res_ref[...] + c * r, g_ref[...], b_ref[...])
        f_ref[...] = r
        h_ref[...] = h
        hb_ref[...] = h.astype(BF16)

    return (fn, [res], [g, b], (F32, F32, BF16), ())


def _ln_bwd_epilogue(res, f, c, g, b):
    def fn(r, ins, outs, first):
        res_ref, f_ref, g_ref, b_ref = ins
        dr_ref, df_ref, dg_ref, db_ref = outs
        pre = ALPHA * res_ref[...] + c * f_ref[...]
        xc = pre - jnp.mean(pre, axis=-1, keepdims=True)
        rstd = lax.rsqrt(jnp.mean(jnp.square(xc), axis=-1, keepdims=True) + EPS)
        xhat = xc * rstd
        dyg = r * g_ref[...]
        dpre = rstd * (dyg - jnp.mean(dyg, axis=-1, keepdims=True) - xhat * jnp.mean(dyg * xhat, axis=-1, keepdims=True))
        dr_ref[...] = ALPHA * dpre
        df_ref[...] = (c * dpre).astype(BF16)
        _accum(dg_ref, jnp.sum(r * xhat, axis=0, keepdims=True), first)
        _accum(db_ref, jnp.sum(r, axis=0, keepdims=True), first)

    return (fn, [res, f], [g, b], (F32, BF16), (1, 1))


ROW_TM = 1024


def _rows(body, T, tm, ins, outs, name, accs=()):
    in_specs, args = [], []
    for arr, w, cb in ins:
        if w is None:
            in_specs.append(pl.BlockSpec(arr.shape, lambda i, _n=arr.ndim: (0,) * _n))
        else:
            in_specs.append(pl.BlockSpec((tm, w), lambda i, _cb=cb: (i, _cb)))
        args.append(arr)
    out_specs = [pl.BlockSpec((tm, w), lambda i: (i, 0)) for w, _ in outs]
    out_shape = [jax.ShapeDtypeStruct((T, w), dt) for w, dt in outs]
    for r, w in accs:
        out_specs.append(pl.BlockSpec((r, w), lambda i: (0, 0)))
        out_shape.append(jax.ShapeDtypeStruct((r, w), F32))
    return _pcall(
        body, grid=(T // tm,), in_specs=in_specs, out_specs=out_specs, out_shape=out_shape, name=name,
        compiler_params=pltpu.CompilerParams(dimension_semantics=("arbitrary",)),
    )(*args)


def _ln_bwd_rows(res, f, c, g, b, dh, T, name):
    fn = _ln_bwd_epilogue(res, f, c, g, b)[0]

    def body(r_ref, f_ref, g_ref, b_ref, d_ref, dr_ref, df_ref, dg_ref, db_ref):
        fn(d_ref[...], (r_ref, f_ref, g_ref, b_ref), (dr_ref, df_ref, dg_ref, db_ref), pl.program_id(0) == 0)

    return _rows(body, T, min(ROW_TM, T), [(res, D, 0), (f, D, 0), (g, None, None), (b, None, None), (dh, D, 0)],
                 [(D, F32), (D, BF16)], name, accs=[(1, D), (1, D)])


FFN_TN = 1408
FFN_TM = 512
LN_TM = 512


def _ffn_in(x, wt, name, order_after=None):
    T = x.shape[0]
    tm, tn = min(FFN_TM, T), FFN_TN
    nj = D_FF // tn
    emit_xb = x.dtype != BF16
    n_in = 3 + int(order_after is not None)

    def body(*refs):
        x_ref, wg_ref, wu_ref = refs[:3]
        hs_ref, gu_ref = refs[n_in], refs[n_in + 1]
        xv = x_ref[...].astype(BF16)
        g = _dot(xv, wg_ref[...], 1, 1)
        u = _dot(xv, wu_ref[...], 1, 1)
        hs_ref[...] = (_silu(g) * u).astype(BF16)
        gu_ref[0] = g.astype(BF16)
        gu_ref[1] = u.astype(BF16)
        if emit_xb:
            refs[n_in + 2][...] = xv

    in_specs = [pl.BlockSpec((tm, D), lambda i, j: (i, 0)), pl.BlockSpec((tn, D), lambda i, j: (j, 0)),
                pl.BlockSpec((tn, D), lambda i, j: (j + nj, 0))]
    args = [x, wt, wt]
    if order_after is not None:
        in_specs.append(pl.BlockSpec(order_after.shape, lambda i, j: (0, 0)))
        args.append(order_after)
    out_specs = [pl.BlockSpec((tm, tn), lambda i, j: (i, j)), pl.BlockSpec((2, tm, tn), lambda i, j: (0, i, j))]
    out_shape = [jax.ShapeDtypeStruct((T, D_FF), BF16), jax.ShapeDtypeStruct((2, T, D_FF), BF16)]
    if emit_xb:
        out_specs.append(pl.BlockSpec((tm, D), lambda i, j: (i, 0)))
        out_shape.append(jax.ShapeDtypeStruct((T, D), BF16))
    return _pcall(
        body, grid=(T // tm, nj), in_specs=in_specs, out_specs=out_specs, out_shape=out_shape, name=name,
        compiler_params=pltpu.CompilerParams(dimension_semantics=("parallel", "arbitrary")),
    )(*args)


def _ffn_act_bwd(df, wo, gu, name):
    T = df.shape[0]
    tm, tn = min(FFN_TM, T), FFN_TN

    def body(d_ref, w_ref, gu_ref, o_ref):
        dhs = _dot(d_ref[...], w_ref[...], 1, 1)
        g, u = gu_ref[0].astype(F32), gu_ref[1].astype(F32)
        sig = _sigmoid(g)
        act = g * sig
        o_ref[0] = (dhs * u * (sig + act * (1.0 - sig))).astype(BF16)
        o_ref[1] = (dhs * act).astype(BF16)

    return _pcall(
        body, grid=(T // tm, D_FF // tn),
        in_specs=[pl.BlockSpec((tm, D), lambda i, j: (i, 0)), pl.BlockSpec((tn, D), lambda i, j: (j, 0)),
                  pl.BlockSpec((2, tm, tn), lambda i, j: (0, i, j))],
        out_specs=pl.BlockSpec((2, tm, tn), lambda i, j: (0, i, j)),
        out_shape=jax.ShapeDtypeStruct((2, T, D_FF), BF16), name=name,
        compiler_params=pltpu.CompilerParams(dimension_semantics=("parallel", "parallel")),
    )(df, wo, gu)


def _rope128(t, cos, sin_s):
    return t * cos + pltpu.roll(t, 64, 1) * sin_s


def _rope128_t(g, cos, sin_s):
    return g * cos - pltpu.roll(g, 64, 1) * sin_s


def _partner32(t):
    lane = lax.broadcasted_iota(jnp.int32, t.shape, 1)
    return jnp.where((lane & 32) == 0, pltpu.roll(t, 96, 1), pltpu.roll(t, 32, 1))


def _rope64(t, cos, sin_s):
    return t * cos + _partner32(t) * sin_s


def _rope64_t(g, cos, sin_s):
    return g * cos - _partner32(g) * sin_s


def _mixer_prep_fwd(proj, tabs, qg, kvg, T):
    cos128, sin128, cos64, sin64 = tabs

    def body(rq_ref, rk_ref, cq_ref, ckv_ref, kpe_ref, c1_ref, s1_ref, c2_ref, s2_ref, qg_ref, kvg_ref,
             oq_ref, ok_ref, oqn_ref, okvn_ref, okpe_ref):
        c1, s1 = c1_ref[...], s1_ref[...]
        for h in range(HEADS):
            sl = slice(h * RET_DK, (h + 1) * RET_DK)
            oq_ref[:, sl] = _rope128(rq_ref[:, sl].astype(F32), c1, s1).astype(BF16)
            ok_ref[:, sl] = (_rope128(rk_ref[:, sl].astype(F32), c1, s1) * RET_SCALE).astype(BF16)
        oqn_ref[...] = _rms(cq_ref[...].astype(F32), qg_ref[...]).astype(BF16)
        okvn_ref[...] = _rms(ckv_ref[...].astype(F32), kvg_ref[...]).astype(BF16)
        okpe_ref[...] = _rope64(kpe_ref[...].astype(F32), c2_ref[...], s2_ref[...]).astype(BF16)

    ins = [(proj, 1024, 0), (proj, 1024, 1), (proj, 256, P_CQ // 256), (proj, 256, P_CKV // 256),
           (proj, 128, P_KPE // 128), (cos128, 128, 0), (sin128, 128, 0), (cos64, 128, 0), (sin64, 128, 0),
           (qg, None, None), (kvg, None, None)]
    outs = [(1024, BF16), (1024, BF16), (LORA, BF16), (LORA, BF16), (128, BF16)]
    return _rows(body, T, min(ROW_TM, T), ins, outs, "mixer_prep_fwd")


def _rms_bwd(proj, dqn, dkvn, dkpe_all, tabs, qg, kvg, T):
    _, _, cos64, sin64 = tabs

    def body(cq_ref, ckv_ref, dq_ref, dkv_ref, dk_ref, c2_ref, s2_ref, qg_ref, kvg_ref, o_ref, dqg_ref, dkvg_ref):
        _, vjp = jax.vjp(_rms, cq_ref[...].astype(F32), qg_ref[...])
        dx, dg = vjp(dq_ref[...])
        o_ref[:, 0:LORA] = dx.astype(BF16)
        _accum(dqg_ref, dg)
        _, vjp = jax.vjp(_rms, ckv_ref[...].astype(F32), kvg_ref[...])
        dx, dg = vjp(dkv_ref[...])
        o_ref[:, P_CKV - P_CQ:P_CKV - P_CQ + LORA] = dx.astype(BF16)
        _accum(dkvg_ref, dg)
        g = dk_ref[:, 0:128]
        for h in range(1, HEADS):
            g = g + dk_ref[:, h * 128:(h + 1) * 128]
        lane = lax.broadcasted_iota(jnp.int32, g.shape, 1)
        g = jnp.where(lane < MLA_ROPE, g, 0.0)
        o_ref[:, P_KPE - P_CQ:P_KPE - P_CQ + 128] = _rope64_t(g, c2_ref[...], s2_ref[...]).astype(BF16)
        o_ref[:, P_KPE - P_CQ + 128:] = jnp.zeros((g.shape[0], P_W - P_KPE - 128), BF16)

    ins = [(proj, 256, P_CQ // 256), (proj, 256, P_CKV // 256), (dqn, LORA, 0), (dkvn, LORA, 0), (dkpe_all, 1024, 0),
           (cos64, 128, 0), (sin64, 128, 0), (qg, None, None), (kvg, None, None)]
    return _rows(body, T, min(ROW_TM, T), ins, [(P_W - P_CQ, BF16)], "rms_bwd", accs=[(1, LORA), (1, LORA)])


def _gn_gate_bwd(y, rg, g, d):
    xc = y - jnp.mean(y, axis=-1, keepdims=True)
    rstd = lax.rsqrt(jnp.mean(jnp.square(xc), axis=-1, keepdims=True) + EPS)
    xhat = xc * rstd
    sig = _sigmoid(rg)
    act = rg * sig
    dn = d * act
    drg = d * (xhat * g) * (sig + act * (1.0 - sig))
    dxh = dn * g
    dy = rstd * (dxh - jnp.mean(dxh, axis=-1, keepdims=True) - xhat * jnp.mean(dxh * xhat, axis=-1, keepdims=True))
    return dy, drg, jnp.sum(dn * xhat, axis=0, keepdims=True)


def _gn_gate(y, rg, g):
    mu = jnp.mean(y, axis=-1, keepdims=True)
    var = jnp.mean(jnp.square(y - mu), axis=-1, keepdims=True)
    return _silu(rg) * ((y - mu) * lax.rsqrt(var + EPS) * g)


def _q_assemble_epilogue(tabs):
    _, _, cos64, sin64 = tabs

    def fn(q, ins, outs, first):
        c_ref, s_ref = ins
        on_ref, op_ref = outs
        on_ref[...] = (q[:, :1024] * Q_PRESCALE).astype(BF16)
        c, s = c_ref[...], s_ref[...]
        lane = lax.broadcasted_iota(jnp.int32, c.shape, 1)
        for j in range(HEADS // 2):
            r = _rope64(q[:, 1024 + 128 * j:1024 + 128 * (j + 1)], c, s) * Q_PRESCALE
            op_ref[:, 256 * j:256 * j + 128] = jnp.where(lane < 64, r, 0.0).astype(BF16)
            op_ref[:, 256 * j + 128:256 * j + 256] = jnp.where(lane < 64, pltpu.roll(r, 64, 1), 0.0).astype(BF16)

    return (fn, [(cos64, 0, 128), (sin64, 0, 128)], [], ((BF16, 1024), (BF16, 1024)), ())


def _q_assemble_bwd(dqn, dqpe, tabs, T):
    _, _, cos64, sin64 = tabs

    def body(dn_ref, dp_ref, c_ref, s_ref, on_ref, op_ref):
        on_ref[...] = dn_ref[...].astype(BF16)
        c, s = c_ref[...], s_ref[...]
        lane = lax.broadcasted_iota(jnp.int32, c.shape, 1)
        for j in range(HEADS // 2):
            g = jnp.where(lane < 64, dp_ref[:, 256 * j:256 * j + 128], pltpu.roll(dp_ref[:, 256 * j + 128:256 * j + 256], 64, 1))
            op_ref[:, 128 * j:128 * (j + 1)] = _rope64_t(g, c, s).astype(BF16)

    return _rows(body, T, min(ROW_TM, T), [(dqn, 1024, 0), (dqpe, 1024, 0), (cos64, 128, 0), (sin64, 128, 0)],
                 [(1024, BF16), (512, BF16)], "q_assemble_bwd")


def _mix_fn(gr, gm, yr, ym):
    return _sigmoid(gr) * yr + _sigmoid(gm) * ym


def _mix_epilogue(proj, y_ret):
    def fn(r, ins, outs, first):
        gr_ref, gm_ref, yr_ref = ins
        ym_ref, mix_ref = outs
        ym_ref[...] = r.astype(BF16)
        mix_ref[...] = _mix_fn(gr_ref[...].astype(F32), gm_ref[...].astype(F32), yr_ref[...].astype(F32), r).astype(BF16)

    return (fn, [(proj, P_GR // D), (proj, P_GM // D), y_ret], [], (BF16, BF16), ())


def _mix_bwd_epilogue(proj, y_ret, y_mla):
    def fn(r, ins, outs, first):
        gr_ref, gm_ref, yr_ref, ym_ref = ins
        dgate_ref, dyr_ref, dym_ref = outs
        sr, sm = _sigmoid(gr_ref[...].astype(F32)), _sigmoid(gm_ref[...].astype(F32))
        dgate_ref[:, 0:D] = (r * yr_ref[...].astype(F32) * (sr * (1.0 - sr))).astype(BF16)
        dgate_ref[:, D:2 * D] = (r * ym_ref[...].astype(F32) * (sm * (1.0 - sm))).astype(BF16)
        dyr_ref[...] = (r * sr).astype(BF16)
        dym_ref[...] = (r * sm).astype(BF16)

    return (fn, [(proj, P_GR // D), (proj, P_GM // D), y_ret, y_mla], [], ((BF16, 2 * D), BF16, BF16), ())


def _head_epilogue(h3, pp, tgt, g, b):
    def fn(r, ins, outs, first):
        h_ref, pp_ref, t_ref, g_ref, b_ref = ins
        dh_ref, dgl_ref, dpp_ref, dg_ref, db_ref, loss_ref = outs

        sg, pp, gain = _sigmoid(r), pp_ref[...], g_ref[...]
        pre = ALPHA * h_ref[...] + sg * pp
        xc = pre - jnp.mean(pre, axis=-1, keepdims=True)
        rstd = lax.rsqrt(jnp.mean(jnp.square(xc), axis=-1, keepdims=True) + EPS)
        xhat = xc * rstd
        err = xhat * gain + b_ref[...] - t_ref[...]
        dy = err * (1.0 / D)
        dyg = dy * gain
        dpre = rstd * (dyg - jnp.mean(dyg, axis=-1, keepdims=True) - xhat * jnp.mean(dyg * xhat, axis=-1, keepdims=True))
        dh_ref[...] = ALPHA * dpre
        dgl_ref[...] = (dpre * pp * (sg * (1.0 - sg))).astype(BF16)
        dpp_ref[...] = (dpre * sg).astype(BF16)
        _accum(dg_ref, jnp.sum(dy * xhat, axis=0, keepdims=True), first)
        _accum(db_ref, jnp.sum(dy, axis=0, keepdims=True), first)
        part = 0.5 * jnp.sum(jnp.mean(jnp.square(err), axis=-1, keepdims=True), axis=0, keepdims=True)
        _accum(loss_ref, jnp.broadcast_to(part, loss_ref.shape), first)

    return (fn, [h3, pp, tgt], [g, b], (F32, BF16, BF16), (1, 1, 8))


def _delta_epilogue(o):
    def fn(r, ins, outs, first):
        (o_ref,) = ins
        db_ref, dl_ref = outs
        db_ref[...] = r.astype(BF16)
        for h in range(HEADS):
            sl = slice(h * MLA_DV, (h + 1) * MLA_DV)
            dl = jnp.sum(r[:, sl] * o_ref[:, sl].astype(F32), axis=-1, keepdims=True)
            dl_ref[:, sl] = jnp.broadcast_to(dl, (r.shape[0], MLA_DV))

    return (fn, [o], [], (BF16, F32), ())


def _ret_consts():
    L = RET_L
    lg = jnp.log(1.0 - 2.0 ** (-5.0 - jnp.arange(HEADS, dtype=F32)))[:, None, None]
    idx = jnp.arange(L, dtype=F32)
    ch = jnp.arange(L) // CHUNK
    dist = idx[:, None] - idx[None, :]
    same = (ch[:, None] == ch[None, :])[None]
    earlier = (ch[None, :] < ch[:, None])[None]
    dm = jnp.where(same, jnp.exp(lg * jnp.abs(dist)[None]), jnp.where(earlier, jnp.exp(lg * dist[None]), 0.0))
    xi = jnp.broadcast_to(jnp.exp(lg * (idx + 1.0)[None, :, None]), (HEADS, L, 128))
    zeta = jnp.broadcast_to(jnp.exp(lg * (L - 1.0 - idx)[None, :, None]), (HEADS, L, 128))
    gl = jnp.broadcast_to(jnp.exp(lg * float(L)), (HEADS, 8, 128))
    return dm.astype(F32), xi.astype(F32), zeta.astype(F32), gl.astype(F32)


def _whole(arr):
    return pl.BlockSpec(arr.shape, lambda n, _nd=arr.ndim: (0,) * _nd)


def _ret_fwd(q, k, v, proj, gn_g, consts, T):
    dm, xi, zeta, gl = consts
    L = RET_L
    n_sc = T // L

    def body(q_ref, k_ref, v_ref, rg_ref, g_ref, dm_ref, xi_ref, ze_ref, gl_ref, y_ref, yr_ref, s_ref, st_ref):
        @pl.when(pl.program_id(0) == 0)
        def _():
            st_ref[...] = jnp.zeros_like(st_ref)

        for h in range(HEADS):
            ks, vs = slice(h * RET_DK, (h + 1) * RET_DK), slice(h * RET_DV, (h + 1) * RET_DV)
            qq, kk, vv = q_ref[:, ks], k_ref[:, ks], v_ref[:, vs]
            st = st_ref[h]
            s_ref[h, 0] = st
            p = (_dot(qq, kk, 1, 1) * dm_ref[h]).astype(BF16)
            cross = _dot(qq, st.astype(BF16), 1, 0)
            xi_c = jnp.concatenate([xi_ref[h], xi_ref[h]], axis=1)
            y = _dot(p, vv, 1, 0) + cross * xi_c
            y_ref[:, vs] = y
            yr_ref[:, vs] = _gn_gate(y, rg_ref[:, vs].astype(F32), g_ref[:, vs]).astype(BF16)
            kz = (kk.astype(F32) * ze_ref[h]).astype(BF16)
            gl2 = jnp.concatenate([gl_ref[h, 0:1, :], gl_ref[h, 0:1, :]], axis=1)
            st_ref[h] = st * gl2 + _dot(kz, vv, 0, 0)

    return _pcall(
        body, grid=(n_sc,),
        in_specs=[pl.BlockSpec((L, 1024), lambda n: (n, 0)), pl.BlockSpec((L, 1024), lambda n: (n, 0)),
                  pl.BlockSpec((L, 2048), lambda n: (n, P_RV // 2048)), pl.BlockSpec((L, 2048), lambda n: (n, P_RG // 2048)),
                  _whole(gn_g), _whole(dm), _whole(xi), _whole(zeta), _whole(gl)],
        out_specs=[pl.BlockSpec((L, 2048), lambda n: (n, 0)), pl.BlockSpec((L, 2048), lambda n: (n, 0)),
                   pl.BlockSpec((HEADS, 1, 128, 256), lambda n: (0, n, 0, 0))],
        out_shape=[jax.ShapeDtypeStruct((T, HEADS * RET_DV), F32), jax.ShapeDtypeStruct((T, HEADS * RET_DV), BF16),
                   jax.ShapeDtypeStruct((HEADS, n_sc, 128, 256), F32)],
        scratch_shapes=[pltpu.VMEM((HEADS, 128, 256), F32)], name="ret_fwd",
        compiler_params=pltpu.CompilerParams(dimension_semantics=("arbitrary",)),
    )(q, k, v, proj, gn_g, dm, xi, zeta, gl)


def _ret_bwd(q, k, v, y, proj, gn_g, states, dyr, consts, tabs, T):
    dm, xi, zeta, gl = consts
    cos128, sin128, _, _ = tabs
    L = RET_L
    n_sc = T // L

    def body(q_ref, k_ref, v_ref, y_ref, rg_ref, g_ref, d_ref, s_ref, dm_ref, xi_ref, ze_ref, gl_ref, c_ref, sn_ref,
             dp_ref, dg_ref, gs_ref):
        @pl.when(pl.program_id(0) == 0)
        def _():
            gs_ref[...] = jnp.zeros_like(gs_ref)

        c, sn = c_ref[...], sn_ref[...]
        dgs = []
        for h in range(HEADS):
            ks, vs = slice(h * RET_DK, (h + 1) * RET_DK), slice(h * RET_DV, (h + 1) * RET_DV)
            dy, drg, dg = _gn_gate_bwd(y_ref[:, vs], rg_ref[:, vs].astype(F32), g_ref[:, vs], d_ref[:, vs].astype(F32))
            dp_ref[:, P_RG + h * RET_DV:P_RG + (h + 1) * RET_DV] = drg.astype(BF16)
            dgs.append(dg)
            qq, kk, vv, dyy = q_ref[:, ks], k_ref[:, ks], v_ref[:, vs], dy.astype(BF16)
            dmm = dm_ref[h]
            gb = gs_ref[h].astype(BF16)
            sb = s_ref[h, 0].astype(BF16)
            xi_c = jnp.concatenate([xi_ref[h], xi_ref[h]], axis=1)
            pb = (_dot(qq, kk, 1, 1) * dmm).astype(BF16)
            kz = (kk.astype(F32) * ze_ref[h]).astype(BF16)
            dp_ref[:, P_RV + h * RET_DV:P_RV + (h + 1) * RET_DV] = (_dot(pb, dyy, 0, 0) + _dot(kz, gb, 1, 0)).astype(BF16)
            da = (_dot(dyy, vv, 1, 1) * dmm).astype(BF16)
            dyx = (dyy.astype(F32) * xi_c).astype(BF16)
            dq = _dot(da, kk, 1, 0) + _dot(dyx, sb, 1, 1)
            dk = _dot(da, qq, 0, 0) + _dot(vv, gb, 1, 1) * ze_ref[h]
            dp_ref[:, P_RQ + h * RET_DK:P_RQ + (h + 1) * RET_DK] = _rope128_t(dq, c, sn).astype(BF16)
            dp_ref[:, P_RK + h * RET_DK:P_RK + (h + 1) * RET_DK] = (_rope128_t(dk, c, sn) * RET_SCALE).astype(BF16)
            gl2 = jnp.concatenate([gl_ref[h, 0:1, :], gl_ref[h, 0:1, :]], axis=1)
            gs_ref[h] = gs_ref[h] * gl2 + _dot(qq, dyx, 0, 0)
        _accum(dg_ref, jnp.concatenate(dgs, axis=1))

    rev = lambda n: n_sc - 1 - n
    return _pcall(
        body, grid=(n_sc,),
        in_specs=[pl.BlockSpec((L, 1024), lambda n: (rev(n), 0)), pl.BlockSpec((L, 1024), lambda n: (rev(n), 0)),
                  pl.BlockSpec((L, 2048), lambda n: (rev(n), P_RV // 2048)), pl.BlockSpec((L, 2048), lambda n: (rev(n), 0)),
                  pl.BlockSpec((L, 2048), lambda n: (rev(n), P_RG // 2048)), _whole(gn_g),
                  pl.BlockSpec((L, 2048), lambda n: (rev(n), 0)),
                  pl.BlockSpec((HEADS, 1, 128, 256), lambda n: (0, rev(n), 0, 0)),
                  _whole(dm), _whole(xi), _whole(zeta), _whole(gl),
                  pl.BlockSpec((L, 128), lambda n: (rev(n), 0)), pl.BlockSpec((L, 128), lambda n: (rev(n), 0))],
        out_specs=[pl.BlockSpec((L, P_GR), lambda n: (rev(n), 0)), pl.BlockSpec((1, 2048), lambda n: (0, 0))],
        out_shape=[jax.ShapeDtypeStruct((T, P_GR), BF16), jax.ShapeDtypeStruct((1, 2048), F32)],
        scratch_shapes=[pltpu.VMEM((HEADS, 128, 256), F32)], name="ret_bwd",
        compiler_params=pltpu.CompilerParams(dimension_semantics=("arbitrary",)),
    )(q, k, v, y, proj, gn_g, dyr, states, dm, xi, zeta, gl, cos128, sin128)


def _diag_mask(nrows, ncols, row0):
    row = row0 + lax.broadcasted_iota(jnp.int32, (nrows, ncols), 0)
    col = lax.broadcasted_iota(jnp.int32, (nrows, ncols), 1)
    return lax.shift_right_logical(col, 6) <= lax.shift_right_logical(row, 6)


def _diag_spans(t):
    h = t // 2
    return ((0, h, h), (h, h, t)) if h % 128 == 0 else ((0, t, t),)


def _tri_steps(nb, by_key):
    if by_key:
        pairs = [(i, j) for j in range(nb) for i in range(j, nb)]
    else:
        pairs = [(i, j) for i in range(nb) for j in range(i + 1)]
    return jnp.array([a for a, _ in pairs], jnp.int32), jnp.array([b for _, b in pairs], jnp.int32)


def _attn_fwd(qn, qpe, kv, kpe, T):
    t = min(ATT_TF, T)
    nb = T // t
    ii, jj = _tri_steps(nb, by_key=False)

    hp = ATT_HP
    w = 128 * hp

    def body(ii_ref, jj_ref, qn_ref, qp_ref, kn_ref, kp_ref, v_ref, o_ref, lse_ref, m_sc, l_sc, acc_sc):
        st = pl.program_id(1)
        i, j = ii_ref[st], jj_ref[st]

        @pl.when(j == 0)
        def _():
            m_sc[...] = jnp.full_like(m_sc, NEG)
            l_sc[...] = jnp.zeros_like(l_sc)
            acc_sc[...] = jnp.zeros_like(acc_sc)

        def update(diag):
            kp = kp_ref[...]
            spans = _diag_spans(t) if diag else ((0, t, t),)
            for hh in range(hp):
                sl = slice(128 * hh, 128 * (hh + 1))
                for row0, nr, nkeys in spans:
                    rows = slice(row0, row0 + nr)
                    q = jnp.concatenate([qn_ref[rows, sl], qp_ref[rows, sl]], axis=1)
                    k = jnp.concatenate([kn_ref[:nkeys, sl], kp[:nkeys]], axis=1)
                    s = _dot(q, k, 1, 1)
                    if diag:
                        s = jnp.where(_diag_mask(nr, nkeys, row0), s, NEG)
                    m_prev = m_sc[rows, sl]
                    m_new = jnp.maximum(m_prev, jnp.max(s, axis=1, keepdims=True))
                    a = jnp.exp2(m_prev - m_new)
                    p = jnp.exp2(s - m_new[:, 0:1])
                    l_sc[rows, sl] = a * l_sc[rows, sl] + jnp.sum(p, axis=1, keepdims=True)
                    acc_sc[rows, sl] = a * acc_sc[rows, sl] + _dot(p.astype(BF16), v_ref[:nkeys, sl], 1, 0)
                    m_sc[rows, sl] = m_new

        @pl.when(j < i)
        def _():
            update(False)

        @pl.when(j == i)
        def _():
            update(True)
            o_ref[...] = (acc_sc[...] / l_sc[...]).astype(BF16)
            lse_ref[...] = m_sc[...] + jnp.log2(l_sc[...])

    qs = pl.BlockSpec((t, w), lambda h, s, ii, jj: (ii[s], h))
    grid_spec = pltpu.PrefetchScalarGridSpec(
        num_scalar_prefetch=2, grid=(HEADS // hp, int(ii.shape[0])),
        in_specs=[qs, qs, pl.BlockSpec((t, w), lambda h, s, ii, jj: (jj[s], h)), pl.BlockSpec((t, 128), lambda h, s, ii, jj: (jj[s], 0)),
                  pl.BlockSpec((t, w), lambda h, s, ii, jj: (jj[s], HEADS // hp + h))],
        out_specs=[qs, qs],
        scratch_shapes=[pltpu.VMEM((t, w), F32), pltpu.VMEM((t, w), F32), pltpu.VMEM((t, w), F32)])
    return _pcall(
        body, grid_spec=grid_spec, out_shape=[jax.ShapeDtypeStruct((T, D), BF16), jax.ShapeDtypeStruct((T, D), F32)], name="attn_fwd",
        compiler_params=pltpu.CompilerParams(dimension_semantics=("arbitrary", "arbitrary")),
    )(ii, jj, qn, qpe, kv, kpe, kv)


def _attn_bwd(qn, qpe, kv, kpe, do, lse, delta, T):
    t = min(ATT_TB, T)
    nb = T // t
    ii, jj = _tri_steps(nb, by_key=True)

    def body(ii_ref, jj_ref, qn_ref, qp_ref, kn_ref, kp_ref, v_ref, do_ref, lse_ref, dl_ref,
             dqn_ref, dqp_ref, dkn_ref, dkp_ref, dv_ref, dk_sc, dv_sc):
        st = pl.program_id(1)
        i, j = ii_ref[st], jj_ref[st]

        @pl.when(st == 0)
        def _():
            dqn_ref[...] = jnp.zeros_like(dqn_ref)
            dqp_ref[...] = jnp.zeros_like(dqp_ref)

        @pl.when(i == j)
        def _():
            dk_sc[...] = jnp.zeros_like(dk_sc)
            dv_sc[...] = jnp.zeros_like(dv_sc)

        def update(diag):
            for row0, nr, nkeys in (_diag_spans(t) if diag else ((0, t, t),)):
                rows = slice(row0, row0 + nr)
                q = jnp.concatenate([qn_ref[rows, :], qp_ref[rows, :]], axis=1)
                k = jnp.concatenate([kn_ref[:nkeys, :], kp_ref[:nkeys, :]], axis=1)
                dob = do_ref[rows, :]
                s = _dot(q, k, 1, 1)
                if diag:
                    s = jnp.where(_diag_mask(nr, nkeys, row0), s, NEG)
                p = jnp.exp2(s - lse_ref[rows, 0:1])
                dv_sc[:nkeys, :] += _dot(p.astype(BF16), dob, 0, 0)
                dp = _dot(dob, v_ref[:nkeys, :], 1, 1)
                ds = (p * (dp - dl_ref[rows, 0:1])).astype(BF16)
                dk_sc[:nkeys, :] += _dot(ds, q, 0, 0)
                dq = _dot(ds, k, 1, 0) * MLA_SCALE
                grows = pl.ds(pl.multiple_of(i * t + row0, nr), nr)
                dqn_ref[grows, :] += dq[:, :128]
                dqp_ref[grows, :] += dq[:, 128:]

        @pl.when(i > j)
        def _():
            update(False)

        @pl.when(i == j)
        def _():
            update(True)

        @pl.when(i == nb - 1)
        def _():
            dkn_ref[...] = (dk_sc[:, :128] * (1.0 / LOG2E)).astype(BF16)
            dkp_ref[...] = dk_sc[:, 128:] * (1.0 / LOG2E)
            dv_ref[...] = dv_sc[...].astype(BF16)

    qs = pl.BlockSpec((t, 128), lambda h, s, ii, jj: (ii[s], h))
    ks = pl.BlockSpec((t, 128), lambda h, s, ii, jj: (jj[s], h))
    hs = pl.BlockSpec((T, 128), lambda h, s, ii, jj: (0, h))
    grid_spec = pltpu.PrefetchScalarGridSpec(
        num_scalar_prefetch=2, grid=(HEADS, int(ii.shape[0])),
        in_specs=[qs, qs, ks, pl.BlockSpec((t, 128), lambda h, s, ii, jj: (jj[s], 0)),
                  pl.BlockSpec((t, 128), lambda h, s, ii, jj: (jj[s], HEADS + h)), qs, qs, qs],
        out_specs=[hs, hs, ks, ks, ks],
        scratch_shapes=[pltpu.VMEM((t, 256), F32), pltpu.VMEM((t, 128), F32)])
    return _pcall(
        body, grid_spec=grid_spec,
        out_shape=[jax.ShapeDtypeStruct((T, D), F32), jax.ShapeDtypeStruct((T, D), F32), jax.ShapeDtypeStruct((T, D), BF16),
                   jax.ShapeDtypeStruct((T, D), F32), jax.ShapeDtypeStruct((T, D), BF16)], name="attn_bwd",
        compiler_params=pltpu.CompilerParams(dimension_semantics=("arbitrary", "arbitrary")),
    )(ii, jj, qn, qpe, kv, kpe, kv, do, lse, delta)


def _mesh_pos():
    x, y, c = lax.axis_index("x"), lax.axis_index("y"), lax.axis_index("c")
    return x, y, c, 4 * x + 2 * y + c


def _peer(x, y, c, k):
    px, py, pc = (x + ((k >> 2) & 1)) % 2, (y + ((k >> 1) & 1)) % 2, (c + (k & 1)) % 2
    return (px, py, pc), 4 * px + 2 * py + pc


_ANY = pl.BlockSpec(memory_space=pl.ANY)


def _rcopy(src, dst, send_sems, recv_sems, k, to):
    return pltpu.make_async_remote_copy(src_ref=src, dst_ref=dst, send_sem=send_sems.at[k], recv_sem=recv_sems.at[k],
                                        device_id=to, device_id_type=pl.DeviceIdType.MESH)


def _all_gather(wsh, ssh):
    def body(w_ref, s_ref, wall_ref, sall_ref, send_sems, recv_sems, loc_sems):
        x, y, c, me = _mesh_pos()
        sib = (x, y, 1 - c)
        chips = [(1 - x, y), (x, 1 - y), (1 - x, 1 - y)]
        slot = lambda px, py, pc: 4 * px + 2 * py + pc
        loc = [pltpu.make_async_copy(s_ref, sall_ref.at[me], loc_sems.at[0])]
        for cp in loc:
            cp.start()
        sends, fwd_waits = [], []
        for n, (src, dst) in enumerate(((w_ref, wall_ref), (s_ref, sall_ref))):
            o = 7 * n
            sends.append(_rcopy(src, dst.at[me], send_sems, recv_sems, o, sib))
            for j, chip in enumerate(chips):
                sends.append(_rcopy(src, dst.at[me], send_sems, recv_sems, o + 1 + j, (*chip, c)))
        for cp in sends:
            cp.start()
        for n, (src, dst) in enumerate(((w_ref, wall_ref), (s_ref, sall_ref))):
            o = 7 * n
            for j, chip in enumerate(chips):
                got = dst.at[slot(*chip, c)]
                _rcopy(src, got, send_sems, recv_sems, o + 1 + j, sib).wait_recv()
                fw = _rcopy(got, got, send_sems, recv_sems, o + 4 + j, sib)
                fw.start()
                sends.append(fw)
            fwd_waits.append(_rcopy(src, dst.at[slot(x, y, 1 - c)], send_sems, recv_sems, o, sib))
            for j, chip in enumerate(chips):
                fwd_waits.append(_rcopy(src, dst.at[slot(*chip, 1 - c)], send_sems, recv_sems, o + 4 + j, sib))
        for cp in fwd_waits:
            cp.wait_recv()
        for cp in sends:
            cp.wait_send()
        for cp in loc:
            cp.wait()

    wall, sall = _pcall(
        body, in_specs=[_ANY, _ANY], out_specs=[_ANY, _ANY],
        out_shape=[jax.ShapeDtypeStruct((N_DEV,) + wsh.shape, wsh.dtype), jax.ShapeDtypeStruct((N_DEV,) + ssh.shape, ssh.dtype)],
        scratch_shapes=[pltpu.SemaphoreType.DMA((14,)), pltpu.SemaphoreType.DMA((14,)), pltpu.SemaphoreType.DMA((1,))],
        name="all_gather_weights",
    )(wsh, ssh)
    me = 4 * lax.axis_index("x") + 2 * lax.axis_index("y") + lax.axis_index("c")
    return lax.dynamic_update_index_in_dim(wall, wsh, me, axis=0), sall


_HBM = pl.BlockSpec(memory_space=pltpu.HBM)
_SEM = pl.BlockSpec(memory_space=pltpu.SEMAPHORE)
_EFFECT = pltpu.SideEffectType.DATAFLOW_SIDE_EFFECTING


def _other_chips(x, y):
    return [(1 - x, y), (x, 1 - y), (1 - x, 1 - y)]


def _gather_start(wsh, order_after):
    def body(w_ref, land_ref, dep_ref, send_sems, recv_sems, w_thru, land_thru, token):
        x, y, c, me = _mesh_pos()
        _rcopy(w_ref, land_ref.at[me], send_sems, recv_sems, 0, (x, y, 1 - c)).start()
        for j, chip in enumerate(_other_chips(x, y)):
            _rcopy(w_ref, land_ref.at[me], send_sems, recv_sems, 1 + j, (*chip, c)).start()
        token[...] = jnp.zeros_like(token)

    shape = (N_DEV,) + wsh.shape
    land = pltpu.with_memory_space_constraint(lax.empty(shape, wsh.dtype), pltpu.HBM)
    return _pcall(
        body, name="gather_start",
        out_shape=(pltpu.SemaphoreType.DMA((4,)), pltpu.SemaphoreType.DMA((4,)), pltpu.HBM(wsh.shape, wsh.dtype),
                   pltpu.HBM(shape, wsh.dtype), jax.ShapeDtypeStruct((8, 128), F32)),
        in_specs=(_HBM, _HBM, _ANY), out_specs=(_SEM, _SEM, _HBM, _HBM, pl.BlockSpec(memory_space=pltpu.VMEM)),
        input_output_aliases={0: 2, 1: 3}, compiler_params=pltpu.CompilerParams(has_side_effects=_EFFECT),
    )(pltpu.with_memory_space_constraint(wsh, pltpu.HBM), land, order_after)


def _gather_wait(send_sems, recv_sems, w_thru, land_thru, after):
    def body(w_ref, land_ref, send_sems, recv_sems, after_ref, w_out, land_out):
        x, y, c, _ = _mesh_pos()
        senders = [(x, y, 1 - c)] + [(*chip, c) for chip in _other_chips(x, y)]
        for k, (px, py, pc) in enumerate(senders):
            cp = _rcopy(w_ref, land_ref.at[4 * px + 2 * py + pc], send_sems, recv_sems, k, (px, py, pc))
            cp.wait_send()
            cp.wait_recv()

    return _pcall(
        body, name="gather_wait", out_shape=(pltpu.HBM(w_thru.shape, w_thru.dtype), pltpu.HBM(land_thru.shape, land_thru.dtype)),
        in_specs=(_HBM, _HBM, _SEM, _SEM, _ANY), out_specs=(_HBM, _HBM), input_output_aliases={0: 0, 1: 1},
        compiler_params=pltpu.CompilerParams(has_side_effects=_EFFECT),
    )(w_thru, land_thru, send_sems, recv_sems, after)


def _gather_finish(wsh, land):
    def body(land_ref, out_ref, send_sems, recv_sems):
        x, y, c, _ = _mesh_pos()
        sib = (x, y, 1 - c)
        sends, recvs = [], []
        for j, (px, py) in enumerate(_other_chips(x, y)):
            mine, theirs = 4 * px + 2 * py + c, 4 * px + 2 * py + (1 - c)
            sends.append(_rcopy(land_ref.at[mine], out_ref.at[mine], send_sems, recv_sems, j, sib))
            recvs.append(_rcopy(land_ref.at[theirs], out_ref.at[theirs], send_sems, recv_sems, j, sib))
        for cp in sends:
            cp.start()
        for cp in recvs:
            cp.wait_recv()
        for cp in sends:
            cp.wait_send()

    done = _pcall(
        body, in_specs=[_ANY], out_specs=_ANY, out_shape=jax.ShapeDtypeStruct(land.shape, land.dtype),
        input_output_aliases={0: 0},
        scratch_shapes=[pltpu.SemaphoreType.DMA((3,)), pltpu.SemaphoreType.DMA((3,))], name="gather_finish",
    )(land)
    me = 4 * lax.axis_index("x") + 2 * lax.axis_index("y") + lax.axis_index("c")
    return lax.dynamic_update_index_in_dim(done, wsh, me, axis=0)


_ROW_TILES = (736, 528, 400, 368, 352, 256, 128, 16, 8)


def _exchange_sibling(g4, name):
    def body(g_ref, r_ref, send_sems, recv_sems):
        x, y, c, _ = _mesh_pos()
        sib = (x, y, 1 - c)
        cps = [_rcopy(g_ref.at[q, 1 - c], r_ref.at[q], send_sems, recv_sems, q, sib) for q in range(4)]
        for cp in cps:
            cp.start()
        for cp in cps:
            cp.wait()

    n, _, R, C = g4.shape
    return _pcall(
        body, in_specs=[_ANY], out_specs=_ANY, out_shape=jax.ShapeDtypeStruct((n, R, C), g4.dtype),
        scratch_shapes=[pltpu.SemaphoreType.DMA((4,)), pltpu.SemaphoreType.DMA((4,))], name=name,
    )(g4)


def _sum_sibling(g4, rsib, cvec, name):
    n, _, R, C = g4.shape
    tr = _pick(R, _ROW_TILES)

    def body(c_ref, g_ref, r_ref, o_ref):
        o_ref[...] = (g_ref[...].astype(F32) + r_ref[...].astype(F32)).astype(o_ref.dtype)

    grid_spec = pltpu.PrefetchScalarGridSpec(
        num_scalar_prefetch=1, grid=(n, R // tr),
        in_specs=[pl.BlockSpec((None, None, tr, C), lambda q, i, cr: (q, cr[0], i, 0)), pl.BlockSpec((None, tr, C), lambda q, i, cr: (q, i, 0))],
        out_specs=pl.BlockSpec((None, tr, C), lambda q, i, cr: (q, i, 0)))
    return _pcall(body, grid_spec=grid_spec, out_shape=jax.ShapeDtypeStruct((n, R, C), g4.dtype), name=name)(cvec, g4, rsib)


def _chips_start(part, name, order_after=None):
    n_in = 2 + int(order_after is not None)

    def body(*refs):
        p_ref, land_ref = refs[:2]
        send_sems, recv_sems, token = refs[n_in], refs[n_in + 1], refs[n_in + 4]
        x, y, c, _ = _mesh_pos()
        myq = 2 * x + y
        for j, (px, py) in enumerate(_other_chips(x, y)):
            _rcopy(p_ref.at[2 * px + py], land_ref.at[myq], send_sems, recv_sems, j, (px, py, c)).start()
        token[...] = jnp.zeros_like(token)

    land = pltpu.with_memory_space_constraint(lax.empty(part.shape, part.dtype), pltpu.HBM)
    extra = () if order_after is None else (order_after,)
    return _pcall(
        body, name=name,
        out_shape=(pltpu.SemaphoreType.DMA((3,)), pltpu.SemaphoreType.DMA((3,)), pltpu.HBM(part.shape, part.dtype),
                   pltpu.HBM(part.shape, part.dtype), jax.ShapeDtypeStruct((8, 128), F32)),
        in_specs=(_HBM, _HBM) + (_ANY,) * len(extra),
        out_specs=(_SEM, _SEM, _HBM, _HBM, pl.BlockSpec(memory_space=pltpu.VMEM)),
        input_output_aliases={0: 2, 1: 3}, compiler_params=pltpu.CompilerParams(has_side_effects=_EFFECT),
    )(pltpu.with_memory_space_constraint(part, pltpu.HBM), land, *extra)


def _chips_wait(send_sems, recv_sems, p_thru, land_thru, after, name):
    def body(p_ref, land_ref, send_sems, recv_sems, after_ref, p_out, land_out):
        x, y, c, _ = _mesh_pos()
        for j, (px, py) in enumerate(_other_chips(x, y)):
            q = 2 * px + py
            cp = _rcopy(p_ref.at[q], land_ref.at[q], send_sems, recv_sems, j, (px, py, c))
            cp.wait_send()
            cp.wait_recv()

    return _pcall(
        body, name=name, out_shape=(pltpu.HBM(p_thru.shape, p_thru.dtype), pltpu.HBM(p_thru.shape, p_thru.dtype)),
        in_specs=(_HBM, _HBM, _SEM, _SEM, _ANY), out_specs=(_HBM, _HBM), input_output_aliases={0: 0, 1: 1},
        compiler_params=pltpu.CompilerParams(has_side_effects=_EFFECT),
    )(p_thru, land_thru, send_sems, recv_sems, after)


def _sum_chips(part, land, qvec, name):
    n, R, C = part.shape
    tr = _pick(R, _ROW_TILES)

    def body(q_ref, p_ref, *refs):
        o_ref = refs[n]
        acc = None
        for q in range(n):
            term = jnp.where(q_ref[0] == q, p_ref[...], refs[q][...]).astype(F32)
            acc = term if acc is None else acc + term
        o_ref[...] = acc

    def land_spec(q):
        return pl.BlockSpec((None, tr, C), lambda i, m: (jnp.where(m[0] == q, (q + 1) % n, q), i, 0))

    grid_spec = pltpu.PrefetchScalarGridSpec(
        num_scalar_prefetch=1, grid=(R // tr,),
        in_specs=[pl.BlockSpec((None, tr, C), lambda i, m: (m[0], i, 0))] + [land_spec(q) for q in range(n)],
        out_specs=pl.BlockSpec((tr, C), lambda i, m: (i, 0)))
    return _pcall(body, grid_spec=grid_spec, out_shape=jax.ShapeDtypeStruct((R, C), F32), name=name)(qvec, part, *([land] * n))


def _exchange_small(gsmall):
    def body(s_ref, srecv_ref, send_sems, recv_sems, loc_sem):
        x, y, c, me = _mesh_pos()
        loc = pltpu.make_async_copy(s_ref, srecv_ref.at[me], loc_sem.at[0])
        loc.start()
        sends, recvs = [], []
        for k in range(1, N_DEV):
            to, pidx = _peer(x, y, c, k)
            sends.append(_rcopy(s_ref, srecv_ref.at[me], send_sems, recv_sems, k - 1, to))
            recvs.append(_rcopy(s_ref, srecv_ref.at[pidx], send_sems, recv_sems, k - 1, to))
        for cp in sends:
            cp.start()
        for cp in recvs:
            cp.wait_recv()
        for cp in sends:
            cp.wait_send()
        loc.wait()

    return _pcall(
        body, in_specs=[_ANY], out_specs=_ANY, out_shape=jax.ShapeDtypeStruct((N_DEV,) + gsmall.shape, gsmall.dtype),
        scratch_shapes=[pltpu.SemaphoreType.DMA((7,)), pltpu.SemaphoreType.DMA((7,)), pltpu.SemaphoreType.DMA((1,))],
        name="exchange_small",
    )(gsmall)


def _sum_slots(recv, name):
    n, R, C = recv.shape
    tr = _pick(R, _ROW_TILES)

    def body(r_ref, o_ref):
        acc = r_ref[0].astype(F32)
        for s in range(1, n):
            acc = acc + r_ref[s].astype(F32)
        o_ref[...] = acc

    return _pcall(body, grid=(R // tr,), in_specs=[pl.BlockSpec((n, tr, C), lambda i: (0, i, 0))],
                  out_specs=pl.BlockSpec((tr, C), lambda i: (i, 0)), out_shape=jax.ShapeDtypeStruct((R, C), F32), name=name)(recv)


def _adamw(w, g, m, v, name, order_after=None):
    shape = w.shape
    w2, g2, m2, v2 = (a.reshape(-1, shape[-1]) for a in (w, g, m, v))
    R, C = w2.shape
    tr = _pick(R, (256, 128, 64, 32, 16, 8)) if R > 256 else R
    n_in = 4 + int(order_after is not None)

    def body(*refs):
        w_ref, g_ref, m_ref, v_ref = refs[:4]
        d_ref, nm_ref, nv_ref = refs[n_in:]
        gg = g_ref[...]
        nm = ADAM_B1 * m_ref[...] + (1.0 - ADAM_B1) * gg
        nv = ADAM_B2 * v_ref[...] + (1.0 - ADAM_B2) * jnp.square(gg)
        m_hat = nm / (1.0 - ADAM_B1 ** ADAM_STEP)
        v_hat = nv / (1.0 - ADAM_B2 ** ADAM_STEP)
        d_ref[...] = -ADAM_LR * (m_hat / (jnp.sqrt(v_hat) + ADAM_EPS) + ADAM_WD * w_ref[...])
        nm_ref[...] = nm
        nv_ref[...] = nv

    spec = pl.BlockSpec((tr, C), lambda i: (i, 0))
    in_specs, args = [spec] * 4, [w2, g2, m2, v2]
    if order_after is not None:
        in_specs.append(pl.BlockSpec(memory_space=pl.ANY))
        args.append(order_after)
    d, nm, nv = _pcall(body, grid=(R // tr,), in_specs=in_specs, out_specs=[spec] * 3,
                       out_shape=[jax.ShapeDtypeStruct((R, C), F32)] * 3, name=name)(*args)
    return d.reshape(shape), nm.reshape(shape), nv.reshape(shape)


def _to_rows(name, w):
    w = w[0]
    if name in ("ffn1_w_in", "ffn2_w_in", "w_in"):
        return w.T
    if name in ("w_uq", "w_ukv", "ple_w_proj"):
        return w.T.reshape(-1, 1024)
    return w


def _from_rows(name, g, shape):
    if name in ("ffn1_w_in", "ffn2_w_in", "w_in"):
        return g.T.reshape(shape)
    if name in ("w_uq", "w_ukv", "ple_w_proj"):
        return g.reshape(-1, shape[1]).T.reshape(shape)
    return g.reshape(shape)


def _unpack(wall, group):
    out, off = {}, 0
    for name, r in group:
        out[name] = wall[:, off:off + r, :].reshape(N_DEV * r, 1024)
        off += _pad16(r)
    return out


def _w_in_internal(wt):
    return jnp.concatenate([wt[0:6144], wt[6720:8768], wt[6144:6720], jnp.zeros((P_W - W_IN_COLS, 1024), wt.dtype)], axis=0)


def _rope_tables(positions):
    pos = positions[0].astype(F32)

    def cs(half):
        inv = jnp.tile(ROPE_BASE ** (-jnp.arange(half, dtype=F32) / half), 128 // half)
        sign = jnp.tile(jnp.concatenate([-jnp.ones((half,), F32), jnp.ones((half,), F32)]), 64 // half)
        ang = pos[:, None] * inv
        return jnp.cos(ang), jnp.sin(ang) * sign

    return cs(64) + cs(32)


def _local_step(x, p, positions, target, W, ln_g, ln_b, gn_g, qg, kvg, early=None, rest_weights=None, start_token=None):
    T = x.shape[0]
    tabs = _rope_tables(positions)
    rc = _ret_consts()
    lg = [ln_g[i:i + 1] for i in range(4)]
    lb = [ln_b[i:i + 1] for i in range(4)]
    pb = p.astype(BF16)

    hs1, gu1, *xb = _ffn_in(x, W["ffn1_w_in"], "ffn1_in", order_after=start_token)
    xb = xb[0] if xb else x
    f1, h1, h1b = _mm(hs1, W["ffn1_w_out"], name="ffn1_out", tm=LN_TM, epilogue=_ln_epilogue(x, 0.5, lg[0], lb[0]))
    if rest_weights is not None:
        W = {**W, **rest_weights(h1b)}
    w_in_t = _w_in_internal(W["w_in"])
    wuq = W["w_uq"].reshape(1536, LORA).reshape(HEADS, 192, LORA)
    wuq = jnp.concatenate([wuq[:, :128].reshape(1024, LORA), wuq[:, 128:].reshape(512, LORA)], axis=0)
    wukv = W["w_ukv"].reshape(2048, LORA).reshape(HEADS, 2, 128, LORA).transpose(1, 0, 2, 3).reshape(2048, LORA)
    wp_t = W["ple_w_proj"].reshape(1024, D_PLE)
    proj = _mm(h1b, w_in_t, tb=True, out_dtype=BF16, name="mixer_in")
    rq, rk, qn, kvn, kpe = _mixer_prep_fwd(proj, tabs, qg, kvg, T)
    y, yr, states = _ret_fwd(rq, rk, proj, proj, gn_g, rc, T)
    y_ret = _mm(yr, W["w_ret_o"], out_dtype=BF16, name="ret_o")
    qnope, qpe = _mm(qn, wuq, tb=True, name="mla_uq", tm=LN_TM, tn=1536, epilogue=_q_assemble_epilogue(tabs))
    kv = _mm(kvn, wukv, tb=True, out_dtype=BF16, name="mla_ukv")
    o, lse = _attn_fwd(qnope, qpe, kv, kpe, T)
    y_mla, mix = _mm(o, W["w_mla_o"], name="mla_o", tm=LN_TM, epilogue=_mix_epilogue(proj, y_ret))
    mixed, h2, h2b = _mm(mix, W["w_out"], name="mixer_out", tm=LN_TM, epilogue=_ln_epilogue(h1, 1.0, lg[1], lb[1]))
    hs2, gu2 = _ffn_in(h2b, W["ffn2_w_in"], "ffn2_in")
    f2, h3, h3b = _mm(hs2, W["ffn2_w_out"], name="ffn2_out", tm=LN_TM, epilogue=_ln_epilogue(h2, 0.5, lg[2], lb[2]))
    pp = _mm(pb, wp_t, tb=True, name="ple_proj")

    G = {}
    dh3_a, dgl, dpp, dg3, db3, loss = _mm(h3b, W["ple_w_gate"], name="ple_gate", tm=LN_TM,
                                          epilogue=_head_epilogue(h3, pp, target, lg[3], lb[3]))
    G["ple_w_gate"] = _mm(h3b, dgl, ta=True, name="d_ple_gate")
    G["ple_w_proj"] = _mm(dpp, pb, ta=True, name="d_ple_proj")
    dh2_a, df2, dg2, db2 = _mm(dgl, W["ple_w_gate"], tb=True, add=dh3_a, name="dh3", tm=LN_TM,
                               epilogue=_ln_bwd_epilogue(h2, f2, 0.5, lg[2], lb[2]))
    wide = dict(tm=D_FF, tk=min(1024, T))
    G["ffn2_w_out"] = _mm(hs2, df2, ta=True, name="d_ffn2_out", **wide)
    da2 = _ffn_act_bwd(df2, W["ffn2_w_out"], gu2, "ffn2_act_bwd")
    G["ffn2_w_in"] = _mm(da2, h2b, ta=True, name="d_ffn2_in", **wide)
    dh1_a, dmixed, dg1, db1 = _mm(da2, W["ffn2_w_in"], add=dh2_a, name="dh2", tm=LN_TM,
                                  epilogue=_ln_bwd_epilogue(h1, mixed, 1.0, lg[1], lb[1]))
    G["w_out"] = _mm(mix, dmixed, ta=True, name="d_mixer_out")
    dp_gate, dy_ret, dy_mla = _mm(dmixed, W["w_out"], tb=True, name="dmix", tm=LN_TM,
                                  epilogue=_mix_bwd_epilogue(proj, y_ret, y_mla))
    G["w_mla_o"] = _mm(o, dy_mla, ta=True, name="d_mla_o")
    dob, delta = _mm(dy_mla, W["w_mla_o"], tb=True, name="do", tm=LN_TM, epilogue=_delta_epilogue(o))
    dqn_f, dqpe_f, dkn, dkpe_all, dv = _attn_bwd(qnope, qpe, kv, kpe, dob, lse, delta, T)
    dq_n, dq_r = _q_assemble_bwd(dqn_f, dqpe_f, tabs, T)
    g_uq = jnp.concatenate([_mm(dq_n, qn, ta=True, name="d_uq_nope"), _mm(dq_r, qn, ta=True, name="d_uq_rope")], axis=0)
    g_uq = jnp.concatenate([g_uq[:1024].reshape(HEADS, 128, LORA), g_uq[1024:].reshape(HEADS, 64, LORA)], axis=1)
    G["w_uq"] = g_uq.reshape(1536 * LORA // 1024, 1024)
    dqn = _mm(dq_r, wuq[1024:], add=_mm(dq_n, wuq[:1024], name="dqn_a"), name="dqn_b")
    g_ukv = jnp.stack([_mm(dkn, kvn, ta=True, name="d_ukv_k"), _mm(dv, kvn, ta=True, name="d_ukv_v")], axis=0)
    G["w_ukv"] = g_ukv.reshape(2, HEADS, 128, LORA).transpose(1, 0, 2, 3).reshape(2048 * LORA // 1024, 1024)
    dkvn = _mm(dv, wukv[1024:], add=_mm(dkn, wukv[:1024], name="dkvn_a"), name="dkvn_b")
    dp_mla, dqg, dkvg = _rms_bwd(proj, dqn, dkvn, dkpe_all, tabs, qg, kvg, T)
    G["w_ret_o"] = _mm(yr, dy_ret, ta=True, name="d_ret_o")
    dyr = _mm(dy_ret, W["w_ret_o"], tb=True, out_dtype=BF16, name="dyr")
    dp_ret, dgn = _ret_bwd(rq, rk, proj, y, proj, gn_g, states, dyr, rc, tabs, T)
    g_ret, g_gate, g_mla = (_mm(dp, h1b, ta=True, name="d_mixer_in_" + n, tm=_pick(dp.shape[1], (2048, 768)), tk=min(1024, T))
                            for dp, n in ((dp_ret, "ret"), (dp_gate, "gate"), (dp_mla, "mla")))
    G["w_in"] = jnp.concatenate([g_ret, g_mla[:W_IN_COLS - P_CQ], g_gate], axis=0)
    lg0 = lg[0] if early is None else lg[0] + early(G)[0:1, 0:1]
    dh1 = _mm_pieces([(dp_ret, w_in_t[:P_GR]), (dp_gate, w_in_t[P_GR:P_CQ]), (dp_mla, w_in_t[P_CQ:])], dh1_a, "dh1")
    dx_a, df1, dg0, db0 = _ln_bwd_rows(x, f1, 0.5, lg0, lb[0], dh1, T, "ln0_bwd")
    G["ffn1_w_out"] = _mm(hs1, df1, ta=True, name="d_ffn1_out", **wide)
    da1 = _ffn_act_bwd(df1, W["ffn1_w_out"], gu1, "ffn1_act_bwd")
    G["ffn1_w_in"] = _mm(da1, xb, ta=True, name="d_ffn1_in", **wide)
    grad_x = _mm(da1, W["ffn1_w_in"], add=dx_a, name="grad_x")

    small = dict(ln_g=jnp.concatenate([dg0, dg1, dg2, dg3], axis=0), ln_b=jnp.concatenate([db0, db1, db2, db3], axis=0),
                 ret_gn_g=dgn, q_norm_g=dqg, kv_norm_g=dkvg)
    return loss, grad_x, G, small


def kernel(x, p, positions, ln_g, ln_b, ffn1_w_in, ffn1_w_out, w_in, ret_gn_g, w_ret_o, q_norm_g, kv_norm_g, w_uq, w_ukv, w_mla_o, w_out, ffn2_w_in, ffn2_w_out, ple_w_gate, ple_w_proj, loss_target, m_ln_g, m_ln_b, m_ffn1_w_in, m_ffn1_w_out, m_w_in, m_ret_gn_g, m_w_ret_o, m_q_norm_g, m_kv_norm_g, m_w_uq, m_w_ukv, m_w_mla_o, m_w_out, m_ffn2_w_in, m_ffn2_w_out, m_ple_w_gate, m_ple_w_proj, v_ln_g, v_ln_b, v_ffn1_w_in, v_ffn1_w_out, v_w_in, v_ret_gn_g, v_w_ret_o, v_q_norm_g, v_kv_norm_g, v_w_uq, v_w_ukv, v_w_mla_o, v_w_out, v_ffn2_w_in, v_ffn2_w_out, v_ple_w_gate, v_ple_w_proj):
    names = ("ln_g", "ln_b", "ffn1_w_in", "ffn1_w_out", "w_in", "ret_gn_g", "w_ret_o", "q_norm_g", "kv_norm_g", "w_uq", "w_ukv",
             "w_mla_o", "w_out", "ffn2_w_in", "ffn2_w_out", "ple_w_gate", "ple_w_proj")
    ws = dict(zip(names, (ln_g, ln_b, ffn1_w_in, ffn1_w_out, w_in, ret_gn_g, w_ret_o, q_norm_g, kv_norm_g, w_uq, w_ukv, w_mla_o,
                          w_out, ffn2_w_in, ffn2_w_out, ple_w_gate, ple_w_proj)))
    ms = dict(zip(names, (m_ln_g, m_ln_b, m_ffn1_w_in, m_ffn1_w_out, m_w_in, m_ret_gn_g, m_w_ret_o, m_q_norm_g, m_kv_norm_g, m_w_uq,
                          m_w_ukv, m_w_mla_o, m_w_out, m_ffn2_w_in, m_ffn2_w_out, m_ple_w_gate, m_ple_w_proj)))
    vs = dict(zip(names, (v_ln_g, v_ln_b, v_ffn1_w_in, v_ffn1_w_out, v_w_in, v_ret_gn_g, v_w_ret_o, v_q_norm_g, v_kv_norm_g, v_w_uq,
                          v_w_ukv, v_w_mla_o, v_w_out, v_ffn2_w_in, v_ffn2_w_out, v_ple_w_gate, v_ple_w_proj)))

    parts = []
    for name, r in PACK:
        rows = _to_rows(name, ws[name])
        if _pad16(r) != r:
            rows = jnp.concatenate([rows, jnp.zeros((_pad16(r) - r, 1024), F32)], axis=0)
        parts.append(rows)
    wsh_first = jnp.concatenate(parts[:len(PACK_LATE)], axis=0).astype(BF16)
    wsh_rest = jnp.concatenate(parts[len(PACK_LATE):], axis=0).astype(BF16)
    ssh = jnp.concatenate([ln_g[0], ln_b[0]], axis=0)
    wall_first, sall = _all_gather(wsh_first, ssh)
    *gather_handles, start_token = _gather_start(wsh_rest, wall_first)
    W = _unpack(wall_first, PACK_LATE)
    ln_full = sall.reshape(N_DEV, 2, 4, 128).transpose(1, 2, 0, 3).reshape(2, 4, 1024)

    def rest_weights(after):
        w_thru, land = _gather_wait(*gather_handles, after)
        return _unpack(_gather_finish(w_thru, land), PACK_EARLY)

    cvec = lax.axis_index("c").astype(jnp.int32).reshape(1)
    qvec = (2 * lax.axis_index("x") + lax.axis_index("y")).astype(jnp.int32).reshape(1)

    def chip_partials(G, group, tag):
        gparts = []
        for name, r in group:
            g = G[name].reshape(N_DEV, r, 1024)
            if _pad16(r) != r:
                g = jnp.concatenate([g, jnp.zeros((N_DEV, _pad16(r) - r, 1024), g.dtype)], axis=1)
            gparts.append(g)
        gfull = jnp.concatenate(gparts, axis=1).astype(BF16)
        g4 = gfull.reshape(4, 2, gfull.shape[1], 1024)
        return _sum_sibling(g4, _exchange_sibling(g4, "exchange_sibling_" + tag), cvec, "sum_sibling_" + tag)

    in_flight = []

    def early(G):
        *handles, token = _chips_start(chip_partials(G, PACK_EARLY, "early"), "chips_start_early")
        in_flight.append(handles)
        return token

    loss_p, grad_x, G, small = _local_step(x[0], p[0, 0], positions, loss_target[0], W, ln_full[0], ln_full[1],
                                           ret_gn_g, q_norm_g, kv_norm_g, early=early, rest_weights=rest_weights,
                                           start_token=start_token)

    part_e, land_e = _chips_wait(*in_flight[0], grad_x, "chips_wait_early")
    gsh_early = _sum_chips(part_e, land_e, qvec, "sum_grads_early")
    pad256 = lambda a: jnp.concatenate([a, jnp.zeros((1, 1024 - a.shape[1]), F32)], axis=1)
    gsmall = jnp.concatenate([small["ln_g"], small["ln_b"], small["ret_gn_g"].reshape(2, 1024), pad256(small["q_norm_g"]),
                              pad256(small["kv_norm_g"]), pad256(loss_p[0:1, 0:128]),
                              jnp.zeros((SMALL_ROWS - 13, 1024), F32)], axis=0)
    srecv = _exchange_small(gsmall)
    part_l = chip_partials(G, PACK_LATE, "late")
    *late_handles, late_token = _chips_start(part_l, "chips_start_late", order_after=srecv)
    ssum = _sum_slots(srecv, "sum_small_grads")

    def unpack_grads(group, gsh):
        out, off = {}, 0
        for name, r in group:
            out[name] = _from_rows(name, gsh[off:off + r], ws[name].shape)
            off += _pad16(r)
        return out

    grads = unpack_grads(PACK_EARLY, gsh_early)
    me = 4 * lax.axis_index("x") + 2 * lax.axis_index("y") + lax.axis_index("c")
    grads["ln_g"] = lax.dynamic_slice(ssum[0:4], (0, me * 128), (4, 128)).reshape(1, 4, 128)
    grads["ln_b"] = lax.dynamic_slice(ssum[4:8], (0, me * 128), (4, 128)).reshape(1, 4, 128)
    grads["ret_gn_g"] = ssum[8:10].reshape(1, 2048)
    grads["q_norm_g"] = ssum[10:11, :256]
    grads["kv_norm_g"] = ssum[11:12, :256]

    delta, new_m, new_v = {}, {}, {}
    late_names = [n for n, _ in PACK_LATE]
    last = late_token
    for name in names:
        if name not in late_names:
            delta[name], new_m[name], new_v[name] = _adamw(ws[name], grads[name], ms[name], vs[name], "adamw_" + name,
                                                           order_after=last)
            last = new_v[name]
    part_l, land_l = _chips_wait(*late_handles, last, "chips_wait_late")
    grads.update(unpack_grads(PACK_LATE, _sum_chips(part_l, land_l, qvec, "sum_grads_late")))
    for name in late_names:
        delta[name], new_m[name], new_v[name] = _adamw(ws[name], grads[name], ms[name], vs[name], "adamw_" + name)

    return (ssum[12, 0], grad_x[None], *[grads[n] for n in names], *[delta[n] for n in names],
            *[new_m[n] for n in names], *[new_v[n] for n in names])
```

```python
import math

import jax
import jax.numpy as jnp
from jax import lax
from jax.experimental import pallas as pl
from jax.experimental.pallas import tpu as pltpu

F32 = jnp.float32
BF16 = jnp.bfloat16

N_DEV = 8
D = 1024
D_FF = 2816
D_PLE = 256
CHUNK = 64
HEADS = 8
RET_DK = 128
RET_DV = 256
MLA_NOPE = 128
MLA_ROPE = 64
MLA_DV = 128
LORA = 256
ROPE_BASE = 10000.0
EPS = 1e-5
ALPHA = 2.0 ** 0.25
RET_SCALE = RET_DK ** -0.5
MLA_SCALE = (MLA_NOPE + MLA_ROPE) ** -0.5
NEG = -1e30

ADAM_LR = 0.001
ADAM_B1 = 0.9
ADAM_B2 = 0.999
ADAM_EPS = 1e-08
ADAM_WD = 0.01
ADAM_STEP = 10

P_RQ, P_RK, P_RV, P_RG, P_GR, P_GM, P_CQ, P_CKV, P_KPE, P_W = 0, 1024, 2048, 4096, 6144, 7168, 8192, 8448, 8704, 8960
W_IN_COLS = 8768
RET_L = 256
ATT_TF = 2048
ATT_TB = 1024
ATT_HP = 1
LOG2E = math.log2(math.e)
Q_PRESCALE = MLA_SCALE * LOG2E

PACK = (("ffn1_w_in", 704), ("ffn1_w_out", 352), ("w_in", 1096), ("w_ret_o", 256), ("w_uq", 48), ("w_ukv", 64),
        ("w_mla_o", 128), ("w_out", 128), ("ffn2_w_in", 704), ("ffn2_w_out", 352), ("ple_w_gate", 128), ("ple_w_proj", 32))


def _pad16(r):
    return -(-r // 16) * 16


PACK_LATE = PACK[:2]
PACK_EARLY = PACK[2:]
SMALL_ROWS = 16


def _pcall(body, **kw):
    return pl.pallas_call(body, **kw)


def _pick(dim, prefs):
    for p in prefs:
        if dim % p == 0:
            return p
    return dim


def _sigmoid(x):
    return 1.0 / (1.0 + jnp.exp(-x))


def _silu(x):
    return x * _sigmoid(x)


def _ln(r, g, b):
    mu = jnp.mean(r, axis=-1, keepdims=True)
    var = jnp.mean(jnp.square(r - mu), axis=-1, keepdims=True)
    return (r - mu) * lax.rsqrt(var + EPS) * g + b


def _rms(x, g):
    return x * lax.rsqrt(jnp.mean(jnp.square(x), axis=-1, keepdims=True) + EPS) * g


def _dot(a, b, ca, cb):
    return lax.dot_general(a, b, (((ca,), (cb,)), ((), ())), preferred_element_type=F32)


def _accum(ref, val, first=None):
    @pl.when(pl.program_id(0) == 0 if first is None else first)
    def _():
        ref[...] = jnp.zeros_like(ref)

    ref[...] += val


def _mm(a, b, *, ta=False, tb=False, add=None, out_dtype=None, name, tm=None, tn=None, tk=None, epilogue=None):
    parts = a.shape[0] if a.ndim == 3 else 1
    ar, ac = a.shape[-2], a.shape[-1]
    out_dtype = out_dtype or (BF16 if ta else F32)
    if ta:
        K, M = ar, ac * parts
    else:
        M, K = ar, ac * parts
    if tb:
        N, K2 = b.shape
    else:
        K2, N = b.shape
    assert K == K2, (a.shape, b.shape, ta, tb)
    big = (1024, 1408, 1280, 768, 512, 256, 128)
    tm = tm or _pick(ac if (ta and parts > 1) else M, big)
    tn = tn or (N if N <= 1024 else _pick(N, big))
    kdim = ac if (not ta and parts > 1) else K
    tk = tk or (kdim if kdim <= 2816 else
                _pick(kdim, (2048, 1408, 1280, 1024, 512) if tm <= 1024 else (1024, 1408, 1280, 512)))
    nk = K // tk
    grid = (M // tm, N // tn, nk)
    if parts > 1 and ta:
        per = ac // tm
        a_spec = pl.BlockSpec((None, tk, tm), lambda i, j, k: (i // per, k, i % per))
    elif parts > 1:
        per = ac // tk
        a_spec = pl.BlockSpec((None, tm, tk), lambda i, j, k: (k // per, i, k % per))
    else:
        a_spec = pl.BlockSpec((tk, tm), lambda i, j, k: (k, i)) if ta else pl.BlockSpec((tm, tk), lambda i, j, k: (i, k))
    b_spec = pl.BlockSpec((tn, tk), lambda i, j, k: (j, k)) if tb else pl.BlockSpec((tk, tn), lambda i, j, k: (k, j))
    o_spec = pl.BlockSpec((tm, tn), lambda i, j, k: (i, j))
    ca, cb = (0 if ta else 1), (1 if tb else 0)
    has_add = add is not None
    n_in = 2 + int(has_add)
    if epilogue is not None:
        assert tn == N and not ta
        ep_fn, ep_rows, ep_whole, ep_outs, ep_accs = epilogue
        n_ep_in = len(ep_rows) + len(ep_whole)
        n_out = len(ep_outs) + len(ep_accs)
    else:
        n_ep_in, n_out = 0, 1

    def body(*refs):
        a_ref, b_ref = refs[0], refs[1]
        add_ref = refs[2] if has_add else None
        o_ref = refs[n_in + n_ep_in]
        first_row_tile = pl.program_id(0) == 0

        def finish(r):
            if has_add:
                r = r + add_ref[...].astype(F32)
            if epilogue is not None:
                ep_fn(r, refs[n_in:n_in + n_ep_in], refs[n_in + n_ep_in:n_in + n_ep_in + n_out], first_row_tile)
            else:
                o_ref[...] = r.astype(out_dtype)

        if nk == 1:
            finish(_dot(a_ref[...], b_ref[...], ca, cb))
            return
        acc_ref = refs[-1]
        k = pl.program_id(2)

        @pl.when(k == 0)
        def _():
            acc_ref[...] = jnp.zeros_like(acc_ref)

        acc_ref[...] += _dot(a_ref[...], b_ref[...], ca, cb)

        @pl.when(k == nk - 1)
        def _():
            finish(acc_ref[...])

    in_specs = [a_spec, b_spec] + ([o_spec] if has_add else [])
    args = (a, b) + ((add,) if has_add else ())
    out_specs, out_shape = o_spec, jax.ShapeDtypeStruct((M, N), out_dtype)
    if epilogue is not None:
        for row_in in ep_rows:
            arr, col_block, width = (tuple(row_in) + (N,))[:3] if isinstance(row_in, tuple) else (row_in, 0, N)
            in_specs.append(pl.BlockSpec((tm, width), lambda i, j, k, _c=col_block: (i, _c)))
            args += (arr,)
        in_specs += [pl.BlockSpec(w.shape, lambda i, j, k, _n=w.ndim: (0,) * _n) for w in ep_whole]
        args += tuple(ep_whole)
        outs = [o if isinstance(o, tuple) else (o, N) for o in ep_outs]
        out_specs = ([pl.BlockSpec((tm, w), lambda i, j, k: (i, 0)) for _, w in outs]
                     + [pl.BlockSpec((r, N), lambda i, j, k: (0, 0)) for r in ep_accs])
        out_shape = [jax.ShapeDtypeStruct((M, w), dt) for dt, w in outs] + [jax.ShapeDtypeStruct((r, N), F32) for r in ep_accs]
    return _pcall(
        body, grid=grid, in_specs=in_specs, out_specs=out_specs, out_shape=out_shape,
        scratch_shapes=[pltpu.VMEM((tm, tn), F32)] if nk > 1 else [], name=name,
        compiler_params=pltpu.CompilerParams(dimension_semantics=("arbitrary" if epilogue is not None else "parallel", "parallel", "arbitrary")),
    )(*args)


def _mm_pieces(pieces, add, name):
    M, N = add.shape
    tm = _pick(M, (1024, 512, 256, 128))
    spans, in_specs, args, start = [], [], [], 0
    for a, b in pieces:
        K = a.shape[1]
        assert b.shape == (K, N) and a.shape[0] == M, (a.shape, b.shape)
        tk = _pick(K, (1536, 1024, 768, 512, 256, 128))
        n = K // tk
        spans.append((start, n))
        block = lambda k, _s=start, _n=n: jnp.where(k < _s, _n - 1, jnp.minimum(k - _s, _n - 1))
        in_specs.append(pl.BlockSpec((tm, tk), lambda i, k, _s=start, _blk=block: (jnp.where(k < _s, jnp.maximum(i - 1, 0), i), _blk(k))))
        in_specs.append(pl.BlockSpec((tk, N), lambda i, k, _blk=block: (_blk(k), 0)))
        args += [a, b]
        start += n
    o_spec = pl.BlockSpec((tm, N), lambda i, k: (i, 0))

    def body(*refs):
        add_ref, o_ref = refs[-2], refs[-1]
        k = pl.program_id(1)

        @pl.when(k == 0)
        def _():
            o_ref[...] = add_ref[...]

        for p, (s, n) in enumerate(spans):
            @pl.when((k >= s) & (k < s + n))
            def _(p=p):
                o_ref[...] += _dot(refs[2 * p][...], refs[2 * p + 1][...], 1, 0)

    return _pcall(
        body, grid=(M // tm, start), in_specs=in_specs + [o_spec], out_specs=o_spec, out_shape=jax.ShapeDtypeStruct((M, N), F32),
        name=name, compiler_params=pltpu.CompilerParams(dimension_semantics=("parallel", "arbitrary")),
    )(*args, add)


def _ln_epilogue(res, c, g, b):
    def fn(r, ins, outs, first):
        res_ref, g_ref, b_ref = ins
        f_ref, h_ref, hb_ref = outs
        h = _ln(ALPHA * res_ref[...] + c * r, g_ref[...], b_ref[...])
        f_ref[...] = r
        h_ref[...] = h
        hb_ref[...] = h.astype(BF16)

    return (fn, [res], [g, b], (F32, F32, BF16), ())


def _ln_bwd_epilogue(res, f, c, g, b):
    def fn(r, ins, outs, first):
        res_ref, f_ref, g_ref, b_ref = ins
        dr_ref, df_ref, dg_ref, db_ref = outs
        pre = ALPHA * res_ref[...] + c * f_ref[...]
        xc = pre - jnp.mean(pre, axis=-1, keepdims=True)
        rstd = lax.rsqrt(jnp.mean(jnp.square(xc), axis=-1, keepdims=True) + EPS)
        xhat = xc * rstd
        dyg = r * g_ref[...]
        dpre = rstd * (dyg - jnp.mean(dyg, axis=-1, keepdims=True) - xhat * jnp.mean(dyg * xhat, axis=-1, keepdims=True))
        dr_ref[...] = ALPHA * dpre
        df_ref[...] = (c * dpre).astype(BF16)
        _accum(dg_ref, jnp.sum(r * xhat, axis=0, keepdims=True), first)
        _accum(db_ref, jnp.sum(r, axis=0, keepdims=True), first)

    return (fn, [res, f], [g, b], (F32, BF16), (1, 1))


ROW_TM = 1024


def _rows(body, T, tm, ins, outs, name, accs=()):
    in_specs, args = [], []
    for arr, w, cb in ins:
        if w is None:
            in_specs.append(pl.BlockSpec(arr.shape, lambda i, _n=arr.ndim: (0,) * _n))
        else:
            in_specs.append(pl.BlockSpec((tm, w), lambda i, _cb=cb: (i, _cb)))
        args.append(arr)
    out_specs = [pl.BlockSpec((tm, w), lambda i: (i, 0)) for w, _ in outs]
    out_shape = [jax.ShapeDtypeStruct((T, w), dt) for w, dt in outs]
    for r, w in accs:
        out_specs.append(pl.BlockSpec((r, w), lambda i: (0, 0)))
        out_shape.append(jax.ShapeDtypeStruct((r, w), F32))
    return _pcall(
        body, grid=(T // tm,), in_specs=in_specs, out_specs=out_specs, out_shape=out_shape, name=name,
        compiler_params=pltpu.CompilerParams(dimension_semantics=("arbitrary",)),
    )(*args)


def _ln_bwd_rows(res, f, c, g, b, dh, T, name):
    fn = _ln_bwd_epilogue(res, f, c, g, b)[0]

    def body(r_ref, f_ref, g_ref, b_ref, d_ref, dr_ref, df_ref, dg_ref, db_ref):
        fn(d_ref[...], (r_ref, f_ref, g_ref, b_ref), (dr_ref, df_ref, dg_ref, db_ref), pl.program_id(0) == 0)

    return _rows(body, T, min(ROW_TM, T), [(res, D, 0), (f, D, 0), (g, None, None), (b, None, None), (dh, D, 0)],
                 [(D, F32), (D, BF16)], name, accs=[(1, D), (1, D)])


FFN_TN = 1408
FFN_TM = 512
LN_TM = 512


def _ffn_in(x, wt, name, order_after=None):
    T = x.shape[0]
    tm, tn = min(FFN_TM, T), FFN_TN
    nj = D_FF // tn
    emit_xb = x.dtype != BF16
    n_in = 3 + int(order_after is not None)

    def body(*refs):
        x_ref, wg_ref, wu_ref = refs[:3]
        hs_ref, gu_ref = refs[n_in], refs[n_in + 1]
        xv = x_ref[...].astype(BF16)
        g = _dot(xv, wg_ref[...], 1, 1)
        u = _dot(xv, wu_ref[...], 1, 1)
        hs_ref[...] = (_silu(g) * u).astype(BF16)
        gu_ref[0] = g.astype(BF16)
        gu_ref[1] = u.astype(BF16)
        if emit_xb:
            refs[n_in + 2][...] = xv

    in_specs = [pl.BlockSpec((tm, D), lambda i, j: (i, 0)), pl.BlockSpec((tn, D), lambda i, j: (j, 0)),
                pl.BlockSpec((tn, D), lambda i, j: (j + nj, 0))]
    args = [x, wt, wt]
    if order_after is not None:
        in_specs.append(pl.BlockSpec(order_after.shape, lambda i, j: (0, 0)))
        args.append(order_after)
    out_specs = [pl.BlockSpec((tm, tn), lambda i, j: (i, j)), pl.BlockSpec((2, tm, tn), lambda i, j: (0, i, j))]
    out_shape = [jax.ShapeDtypeStruct((T, D_FF), BF16), jax.ShapeDtypeStruct((2, T, D_FF), BF16)]
    if emit_xb:
        out_specs.append(pl.BlockSpec((tm, D), lambda i, j: (i, 0)))
        out_shape.append(jax.ShapeDtypeStruct((T, D), BF16))
    return _pcall(
        body, grid=(T // tm, nj), in_specs=in_specs, out_specs=out_specs, out_shape=out_shape, name=name,
        compiler_params=pltpu.CompilerParams(dimension_semantics=("parallel", "arbitrary")),
    )(*args)


def _ffn_act_bwd(df, wo, gu, name):
    T = df.shape[0]
    tm, tn = min(FFN_TM, T), FFN_TN

    def body(d_ref, w_ref, gu_ref, o_ref):
        dhs = _dot(d_ref[...], w_ref[...], 1, 1)
        g, u = gu_ref[0].astype(F32), gu_ref[1].astype(F32)
        sig = _sigmoid(g)
        act = g * sig
        o_ref[0] = (dhs * u * (sig + act * (1.0 - sig))).astype(BF16)
        o_ref[1] = (dhs * act).astype(BF16)

    return _pcall(
        body, grid=(T // tm, D_FF // tn),
        in_specs=[pl.BlockSpec((tm, D), lambda i, j: (i, 0)), pl.BlockSpec((tn, D), lambda i, j: (j, 0)),
                  pl.BlockSpec((2, tm, tn), lambda i, j: (0, i, j))],
        out_specs=pl.BlockSpec((2, tm, tn), lambda i, j: (0, i, j)),
        out_shape=jax.ShapeDtypeStruct((2, T, D_FF), BF16), name=name,
        compiler_params=pltpu.CompilerParams(dimension_semantics=("parallel", "parallel")),
    )(df, wo, gu)


def _rope128(t, cos, sin_s):
    return t * cos + pltpu.roll(t, 64, 1) * sin_s


def _rope128_t(g, cos, sin_s):
    return g * cos - pltpu.roll(g, 64, 1) * sin_s


def _partner32(t):
    lane = lax.broadcasted_iota(jnp.int32, t.shape, 1)
    return jnp.where((lane & 32) == 0, pltpu.roll(t, 96, 1), pltpu.roll(t, 32, 1))


def _rope64(t, cos, sin_s):
    return t * cos + _partner32(t) * sin_s


def _rope64_t(g, cos, sin_s):
    return g * cos - _partner32(g) * sin_s


def _mixer_prep_fwd(proj, tabs, qg, kvg, T):
    cos128, sin128, cos64, sin64 = tabs

    def body(rq_ref, rk_ref, cq_ref, ckv_ref, kpe_ref, c1_ref, s1_ref, c2_ref, s2_ref, qg_ref, kvg_ref,
             oq_ref, ok_ref, oqn_ref, okvn_ref, okpe_ref):
        c1, s1 = c1_ref[...], s1_ref[...]
        for h in range(HEADS):
            sl = slice(h * RET_DK, (h + 1) * RET_DK)
            oq_ref[:, sl] = _rope128(rq_ref[:, sl].astype(F32), c1, s1).astype(BF16)
            ok_ref[:, sl] = (_rope128(rk_ref[:, sl].astype(F32), c1, s1) * RET_SCALE).astype(BF16)
        oqn_ref[...] = _rms(cq_ref[...].astype(F32), qg_ref[...]).astype(BF16)
        okvn_ref[...] = _rms(ckv_ref[...].astype(F32), kvg_ref[...]).astype(BF16)
        okpe_ref[...] = _rope64(kpe_ref[...].astype(F32), c2_ref[...], s2_ref[...]).astype(BF16)

    ins = [(proj, 1024, 0), (proj, 1024, 1), (proj, 256, P_CQ // 256), (proj, 256, P_CKV // 256),
           (proj, 128, P_KPE // 128), (cos128, 128, 0), (sin128, 128, 0), (cos64, 128, 0), (sin64, 128, 0),
           (qg, None, None), (kvg, None, None)]
    outs = [(1024, BF16), (1024, BF16), (LORA, BF16), (LORA, BF16), (128, BF16)]
    return _rows(body, T, min(ROW_TM, T), ins, outs, "mixer_prep_fwd")


def _rms_bwd(proj, dqn, dkvn, dkpe_all, tabs, qg, kvg, T):
    _, _, cos64, sin64 = tabs

    def body(cq_ref, ckv_ref, dq_ref, dkv_ref, dk_ref, c2_ref, s2_ref, qg_ref, kvg_ref, o_ref, dqg_ref, dkvg_ref):
        _, vjp = jax.vjp(_rms, cq_ref[...].astype(F32), qg_ref[...])
        dx, dg = vjp(dq_ref[...])
        o_ref[:, 0:LORA] = dx.astype(BF16)
        _accum(dqg_ref, dg)
        _, vjp = jax.vjp(_rms, ckv_ref[...].astype(F32), kvg_ref[...])
        dx, dg = vjp(dkv_ref[...])
        o_ref[:, P_CKV - P_CQ:P_CKV - P_CQ + LORA] = dx.astype(BF16)
        _accum(dkvg_ref, dg)
        g = dk_ref[:, 0:128]
        for h in range(1, HEADS):
            g = g + dk_ref[:, h * 128:(h + 1) * 128]
        lane = lax.broadcasted_iota(jnp.int32, g.shape, 1)
        g = jnp.where(lane < MLA_ROPE, g, 0.0)
        o_ref[:, P_KPE - P_CQ:P_KPE - P_CQ + 128] = _rope64_t(g, c2_ref[...], s2_ref[...]).astype(BF16)
        o_ref[:, P_KPE - P_CQ + 128:] = jnp.zeros((g.shape[0], P_W - P_KPE - 128), BF16)

    ins = [(proj, 256, P_CQ // 256), (proj, 256, P_CKV // 256), (dqn, LORA, 0), (dkvn, LORA, 0), (dkpe_all, 1024, 0),
           (cos64, 128, 0), (sin64, 128, 0), (qg, None, None), (kvg, None, None)]
    return _rows(body, T, min(ROW_TM, T), ins, [(P_W - P_CQ, BF16)], "rms_bwd", accs=[(1, LORA), (1, LORA)])


def _gn_gate_bwd(y, rg, g, d):
    xc = y - jnp.mean(y, axis=-1, keepdims=True)
    rstd = lax.rsqrt(jnp.mean(jnp.square(xc), axis=-1, keepdims=True) + EPS)
    xhat = xc * rstd
    sig = _sigmoid(rg)
    act = rg * sig
    dn = d * act
    drg = d * (xhat * g) * (sig + act * (1.0 - sig))
    dxh = dn * g
    dy = rstd * (dxh - jnp.mean(dxh, axis=-1, keepdims=True) - xhat * jnp.mean(dxh * xhat, axis=-1, keepdims=True))
    return dy, drg, jnp.sum(dn * xhat, axis=0, keepdims=True)


def _gn_gate(y, rg, g):
    mu = jnp.mean(y, axis=-1, keepdims=True)
    var = jnp.mean(jnp.square(y - mu), axis=-1, keepdims=True)
    return _silu(rg) * ((y - mu) * lax.rsqrt(var + EPS) * g)


def _q_assemble_epilogue(tabs):
    _, _, cos64, sin64 = tabs

    def fn(q, ins, outs, first):
        c_ref, s_ref = ins
        on_ref, op_ref = outs
        on_ref[...] = (q[:, :1024] * Q_PRESCALE).astype(BF16)
        c, s = c_ref[...], s_ref[...]
        lane = lax.broadcasted_iota(jnp.int32, c.shape, 1)
        for j in range(HEADS // 2):
            r = _rope64(q[:, 1024 + 128 * j:1024 + 128 * (j + 1)], c, s) * Q_PRESCALE
            op_ref[:, 256 * j:256 * j + 128] = jnp.where(lane < 64, r, 0.0).astype(BF16)
            op_ref[:, 256 * j + 128:256 * j + 256] = jnp.where(lane < 64, pltpu.roll(r, 64, 1), 0.0).astype(BF16)

    return (fn, [(cos64, 0, 128), (sin64, 0, 128)], [], ((BF16, 1024), (BF16, 1024)), ())


def _q_assemble_bwd(dqn, dqpe, tabs, T):
    _, _, cos64, sin64 = tabs

    def body(dn_ref, dp_ref, c_ref, s_ref, on_ref, op_ref):
        on_ref[...] = dn_ref[...].astype(BF16)
        c, s = c_ref[...], s_ref[...]
        lane = lax.broadcasted_iota(jnp.int32, c.shape, 1)
        for j in range(HEADS // 2):
            g = jnp.where(lane < 64, dp_ref[:, 256 * j:256 * j + 128], pltpu.roll(dp_ref[:, 256 * j + 128:256 * j + 256], 64, 1))
            op_ref[:, 128 * j:128 * (j + 1)] = _rope64_t(g, c, s).astype(BF16)

    return _rows(body, T, min(ROW_TM, T), [(dqn, 1024, 0), (dqpe, 1024, 0), (cos64, 128, 0), (sin64, 128, 0)],
                 [(1024, BF16), (512, BF16)], "q_assemble_bwd")


def _mix_fn(gr, gm, yr, ym):
    return _sigmoid(gr) * yr + _sigmoid(gm) * ym


def _mix_epilogue(proj, y_ret):
    def fn(r, ins, outs, first):
        gr_ref, gm_ref, yr_ref = ins
        ym_ref, mix_ref = outs
        ym_ref[...] = r.astype(BF16)
        mix_ref[...] = _mix_fn(gr_ref[...].astype(F32), gm_ref[...].astype(F32), yr_ref[...].astype(F32), r).astype(BF16)

    return (fn, [(proj, P_GR // D), (proj, P_GM // D), y_ret], [], (BF16, BF16), ())


def _mix_bwd_epilogue(proj, y_ret, y_mla):
    def fn(r, ins, outs, first):
        gr_ref, gm_ref, yr_ref, ym_ref = ins
        dgate_ref, dyr_ref, dym_ref = outs
        sr, sm = _sigmoid(gr_ref[...].astype(F32)), _sigmoid(gm_ref[...].astype(F32))
        dgate_ref[:, 0:D] = (r * yr_ref[...].astype(F32) * (sr * (1.0 - sr))).astype(BF16)
        dgate_ref[:, D:2 * D] = (r * ym_ref[...].astype(F32) * (sm * (1.0 - sm))).astype(BF16)
        dyr_ref[...] = (r * sr).astype(BF16)
        dym_ref[...] = (r * sm).astype(BF16)

    return (fn, [(proj, P_GR // D), (proj, P_GM // D), y_ret, y_mla], [], ((BF16, 2 * D), BF16, BF16), ())


def _head_epilogue(h3, pp, tgt, g, b):
    def fn(r, ins, outs, first):
        h_ref, pp_ref, t_ref, g_ref, b_ref = ins
        dh_ref, dgl_ref, dpp_ref, dg_ref, db_ref, loss_ref = outs

        sg, pp, gain = _sigmoid(r), pp_ref[...], g_ref[...]
        pre = ALPHA * h_ref[...] + sg * pp
        xc = pre - jnp.mean(pre, axis=-1, keepdims=True)
        rstd = lax.rsqrt(jnp.mean(jnp.square(xc), axis=-1, keepdims=True) + EPS)
        xhat = xc * rstd
        err = xhat * gain + b_ref[...] - t_ref[...]
        dy = err * (1.0 / D)
        dyg = dy * gain
        dpre = rstd * (dyg - jnp.mean(dyg, axis=-1, keepdims=True) - xhat * jnp.mean(dyg * xhat, axis=-1, keepdims=True))
        dh_ref[...] = ALPHA * dpre
        dgl_ref[...] = (dpre * pp * (sg * (1.0 - sg))).astype(BF16)
        dpp_ref[...] = (dpre * sg).astype(BF16)
        _accum(dg_ref, jnp.sum(dy * xhat, axis=0, keepdims=True), first)
        _accum(db_ref, jnp.sum(dy, axis=0, keepdims=True), first)
        part = 0.5 * jnp.sum(jnp.mean(jnp.square(err), axis=-1, keepdims=True), axis=0, keepdims=True)
        _accum(loss_ref, jnp.broadcast_to(part, loss_ref.shape), first)

    return (fn, [h3, pp, tgt], [g, b], (F32, BF16, BF16), (1, 1, 8))


def _delta_epilogue(o):
    def fn(r, ins, outs, first):
        (o_ref,) = ins
        db_ref, dl_ref = outs
        db_ref[...] = r.astype(BF16)
        for h in range(HEADS):
            sl = slice(h * MLA_DV, (h + 1) * MLA_DV)
            dl = jnp.sum(r[:, sl] * o_ref[:, sl].astype(F32), axis=-1, keepdims=True)
            dl_ref[:, sl] = jnp.broadcast_to(dl, (r.shape[0], MLA_DV))

    return (fn, [o], [], (BF16, F32), ())


def _ret_consts():
    L = RET_L
    lg = jnp.log(1.0 - 2.0 ** (-5.0 - jnp.arange(HEADS, dtype=F32)))[:, None, None]
    idx = jnp.arange(L, dtype=F32)
    ch = jnp.arange(L) // CHUNK
    dist = idx[:, None] - idx[None, :]
    same = (ch[:, None] == ch[None, :])[None]
    earlier = (ch[None, :] < ch[:, None])[None]
    dm = jnp.where(same, jnp.exp(lg * jnp.abs(dist)[None]), jnp.where(earlier, jnp.exp(lg * dist[None]), 0.0))
    xi = jnp.broadcast_to(jnp.exp(lg * (idx + 1.0)[None, :, None]), (HEADS, L, 128))
    zeta = jnp.broadcast_to(jnp.exp(lg * (L - 1.0 - idx)[None, :, None]), (HEADS, L, 128))
    gl = jnp.broadcast_to(jnp.exp(lg * float(L)), (HEADS, 8, 128))
    return dm.astype(F32), xi.astype(F32), zeta.astype(F32), gl.astype(F32)


def _whole(arr):
    return pl.BlockSpec(arr.shape, lambda n, _nd=arr.ndim: (0,) * _nd)


def _ret_fwd(q, k, v, proj, gn_g, consts, T):
    dm, xi, zeta, gl = consts
    L = RET_L
    n_sc = T // L

    def body(q_ref, k_ref, v_ref, rg_ref, g_ref, dm_ref, xi_ref, ze_ref, gl_ref, y_ref, yr_ref, s_ref, st_ref):
        @pl.when(pl.program_id(0) == 0)
        def _():
            st_ref[...] = jnp.zeros_like(st_ref)

        for h in range(HEADS):
            ks, vs = slice(h * RET_DK, (h + 1) * RET_DK), slice(h * RET_DV, (h + 1) * RET_DV)
            qq, kk, vv = q_ref[:, ks], k_ref[:, ks], v_ref[:, vs]
            st = st_ref[h]
            s_ref[h, 0] = st
            p = (_dot(qq, kk, 1, 1) * dm_ref[h]).astype(BF16)
            cross = _dot(qq, st.astype(BF16), 1, 0)
            xi_c = jnp.concatenate([xi_ref[h], xi_ref[h]], axis=1)
            y = _dot(p, vv, 1, 0) + cross * xi_c
            y_ref[:, vs] = y
            yr_ref[:, vs] = _gn_gate(y, rg_ref[:, vs].astype(F32), g_ref[:, vs]).astype(BF16)
            kz = (kk.astype(F32) * ze_ref[h]).astype(BF16)
            gl2 = jnp.concatenate([gl_ref[h, 0:1, :], gl_ref[h, 0:1, :]], axis=1)
            st_ref[h] = st * gl2 + _dot(kz, vv, 0, 0)

    return _pcall(
        body, grid=(n_sc,),
        in_specs=[pl.BlockSpec((L, 1024), lambda n: (n, 0)), pl.BlockSpec((L, 1024), lambda n: (n, 0)),
                  pl.BlockSpec((L, 2048), lambda n: (n, P_RV // 2048)), pl.BlockSpec((L, 2048), lambda n: (n, P_RG // 2048)),
                  _whole(gn_g), _whole(dm), _whole(xi), _whole(zeta), _whole(gl)],
        out_specs=[pl.BlockSpec((L, 2048), lambda n: (n, 0)), pl.BlockSpec((L, 2048), lambda n: (n, 0)),
                   pl.BlockSpec((HEADS, 1, 128, 256), lambda n: (0, n, 0, 0))],
        out_shape=[jax.ShapeDtypeStruct((T, HEADS * RET_DV), F32), jax.ShapeDtypeStruct((T, HEADS * RET_DV), BF16),
                   jax.ShapeDtypeStruct((HEADS, n_sc, 128, 256), F32)],
        scratch_shapes=[pltpu.VMEM((HEADS, 128, 256), F32)], name="ret_fwd",
        compiler_params=pltpu.CompilerParams(dimension_semantics=("arbitrary",)),
    )(q, k, v, proj, gn_g, dm, xi, zeta, gl)


def _ret_bwd(q, k, v, y, proj, gn_g, states, dyr, consts, tabs, T):
    dm, xi, zeta, gl = consts
    cos128, sin128, _, _ = tabs
    L = RET_L
    n_sc = T // L

    def body(q_ref, k_ref, v_ref, y_ref, rg_ref, g_ref, d_ref, s_ref, dm_ref, xi_ref, ze_ref, gl_ref, c_ref, sn_ref,
             dp_ref, dg_ref, gs_ref):
        @pl.when(pl.program_id(0) == 0)
        def _():
            gs_ref[...] = jnp.zeros_like(gs_ref)

        c, sn = c_ref[...], sn_ref[...]
        dgs = []
        for h in range(HEADS):
            ks, vs = slice(h * RET_DK, (h + 1) * RET_DK), slice(h * RET_DV, (h + 1) * RET_DV)
            dy, drg, dg = _gn_gate_bwd(y_ref[:, vs], rg_ref[:, vs].astype(F32), g_ref[:, vs], d_ref[:, vs].astype(F32))
            dp_ref[:, P_RG + h * RET_DV:P_RG + (h + 1) * RET_DV] = drg.astype(BF16)
            dgs.append(dg)
            qq, kk, vv, dyy = q_ref[:, ks], k_ref[:, ks], v_ref[:, vs], dy.astype(BF16)
            dmm = dm_ref[h]
            gb = gs_ref[h].astype(BF16)
            sb = s_ref[h, 0].astype(BF16)
            xi_c = jnp.concatenate([xi_ref[h], xi_ref[h]], axis=1)
            pb = (_dot(qq, kk, 1, 1) * dmm).astype(BF16)
            kz = (kk.astype(F32) * ze_ref[h]).astype(BF16)
            dp_ref[:, P_RV + h * RET_DV:P_RV + (h + 1) * RET_DV] = (_dot(pb, dyy, 0, 0) + _dot(kz, gb, 1, 0)).astype(BF16)
            da = (_dot(dyy, vv, 1, 1) * dmm).astype(BF16)
            dyx = (dyy.astype(F32) * xi_c).astype(BF16)
            dq = _dot(da, kk, 1, 0) + _dot(dyx, sb, 1, 1)
            dk = _dot(da, qq, 0, 0) + _dot(vv, gb, 1, 1) * ze_ref[h]
            dp_ref[:, P_RQ + h * RET_DK:P_RQ + (h + 1) * RET_DK] = _rope128_t(dq, c, sn).astype(BF16)
            dp_ref[:, P_RK + h * RET_DK:P_RK + (h + 1) * RET_DK] = (_rope128_t(dk, c, sn) * RET_SCALE).astype(BF16)
            gl2 = jnp.concatenate([gl_ref[h, 0:1, :], gl_ref[h, 0:1, :]], axis=1)
            gs_ref[h] = gs_ref[h] * gl2 + _dot(qq, dyx, 0, 0)
        _accum(dg_ref, jnp.concatenate(dgs, axis=1))

    rev = lambda n: n_sc - 1 - n
    return _pcall(
        body, grid=(n_sc,),
        in_specs=[pl.BlockSpec((L, 1024), lambda n: (rev(n), 0)), pl.BlockSpec((L, 1024), lambda n: (rev(n), 0)),
                  pl.BlockSpec((L, 2048), lambda n: (rev(n), P_RV // 2048)), pl.BlockSpec((L, 2048), lambda n: (rev(n), 0)),
                  pl.BlockSpec((L, 2048), lambda n: (rev(n), P_RG // 2048)), _whole(gn_g),
                  pl.BlockSpec((L, 2048), lambda n: (rev(n), 0)),
                  pl.BlockSpec((HEADS, 1, 128, 256), lambda n: (0, rev(n), 0, 0)),
                  _whole(dm), _whole(xi), _whole(zeta), _whole(gl),
                  pl.BlockSpec((L, 128), lambda n: (rev(n), 0)), pl.BlockSpec((L, 128), lambda n: (rev(n), 0))],
        out_specs=[pl.BlockSpec((L, P_GR), lambda n: (rev(n), 0)), pl.BlockSpec((1, 2048), lambda n: (0, 0))],
        out_shape=[jax.ShapeDtypeStruct((T, P_GR), BF16), jax.ShapeDtypeStruct((1, 2048), F32)],
        scratch_shapes=[pltpu.VMEM((HEADS, 128, 256), F32)], name="ret_bwd",
        compiler_params=pltpu.CompilerParams(dimension_semantics=("arbitrary",)),
    )(q, k, v, y, proj, gn_g, dyr, states, dm, xi, zeta, gl, cos128, sin128)


def _diag_mask(nrows, ncols, row0):
    row = row0 + lax.broadcasted_iota(jnp.int32, (nrows, ncols), 0)
    col = lax.broadcasted_iota(jnp.int32, (nrows, ncols), 1)
    return lax.shift_right_logical(col, 6) <= lax.shift_right_logical(row, 6)


def _diag_spans(t):
    h = t // 2
    return ((0, h, h), (h, h, t)) if h % 128 == 0 else ((0, t, t),)


def _tri_steps(nb, by_key):
    if by_key:
        pairs = [(i, j) for j in range(nb) for i in range(j, nb)]
    else:
        pairs = [(i, j) for i in range(nb) for j in range(i + 1)]
    return jnp.array([a for a, _ in pairs], jnp.int32), jnp.array([b for _, b in pairs], jnp.int32)


def _attn_fwd(qn, qpe, kv, kpe, T):
    t = min(ATT_TF, T)
    nb = T // t
    ii, jj = _tri_steps(nb, by_key=False)

    hp = ATT_HP
    w = 128 * hp

    def body(ii_ref, jj_ref, qn_ref, qp_ref, kn_ref, kp_ref, v_ref, o_ref, lse_ref, m_sc, l_sc, acc_sc):
        st = pl.program_id(1)
        i, j = ii_ref[st], jj_ref[st]

        @pl.when(j == 0)
        def _():
            m_sc[...] = jnp.full_like(m_sc, NEG)
            l_sc[...] = jnp.zeros_like(l_sc)
            acc_sc[...] = jnp.zeros_like(acc_sc)

        def update(diag):
            kp = kp_ref[...]
            spans = _diag_spans(t) if diag else ((0, t, t),)
            for hh in range(hp):
                sl = slice(128 * hh, 128 * (hh + 1))
                for row0, nr, nkeys in spans:
                    rows = slice(row0, row0 + nr)
                    q = jnp.concatenate([qn_ref[rows, sl], qp_ref[rows, sl]], axis=1)
                    k = jnp.concatenate([kn_ref[:nkeys, sl], kp[:nkeys]], axis=1)
                    s = _dot(q, k, 1, 1)
                    if diag:
                        s = jnp.where(_diag_mask(nr, nkeys, row0), s, NEG)
                    m_prev = m_sc[rows, sl]
                    m_new = jnp.maximum(m_prev, jnp.max(s, axis=1, keepdims=True))
                    a = jnp.exp2(m_prev - m_new)
                    p = jnp.exp2(s - m_new[:, 0:1])
                    l_sc[rows, sl] = a * l_sc[rows, sl] + jnp.sum(p, axis=1, keepdims=True)
                    acc_sc[rows, sl] = a * acc_sc[rows, sl] + _dot(p.astype(BF16), v_ref[:nkeys, sl], 1, 0)
                    m_sc[rows, sl] = m_new

        @pl.when(j < i)
        def _():
            update(False)

        @pl.when(j == i)
        def _():
            update(True)
            o_ref[...] = (acc_sc[...] / l_sc[...]).astype(BF16)
            lse_ref[...] = m_sc[...] + jnp.log2(l_sc[...])

    qs = pl.BlockSpec((t, w), lambda h, s, ii, jj: (ii[s], h))
    grid_spec = pltpu.PrefetchScalarGridSpec(
        num_scalar_prefetch=2, grid=(HEADS // hp, int(ii.shape[0])),
        in_specs=[qs, qs, pl.BlockSpec((t, w), lambda h, s, ii, jj: (jj[s], h)), pl.BlockSpec((t, 128), lambda h, s, ii, jj: (jj[s], 0)),
                  pl.BlockSpec((t, w), lambda h, s, ii, jj: (jj[s], HEADS // hp + h))],
        out_specs=[qs, qs],
        scratch_shapes=[pltpu.VMEM((t, w), F32), pltpu.VMEM((t, w), F32), pltpu.VMEM((t, w), F32)])
    return _pcall(
        body, grid_spec=grid_spec, out_shape=[jax.ShapeDtypeStruct((T, D), BF16), jax.ShapeDtypeStruct((T, D), F32)], name="attn_fwd",
        compiler_params=pltpu.CompilerParams(dimension_semantics=("arbitrary", "arbitrary")),
    )(ii, jj, qn, qpe, kv, kpe, kv)


def _attn_bwd(qn, qpe, kv, kpe, do, lse, delta, T):
    t = min(ATT_TB, T)
    nb = T // t
    ii, jj = _tri_steps(nb, by_key=True)

    def body(ii_ref, jj_ref, qn_ref, qp_ref, kn_ref, kp_ref, v_ref, do_ref, lse_ref, dl_ref,
             dqn_ref, dqp_ref, dkn_ref, dkp_ref, dv_ref, dk_sc, dv_sc):
        st = pl.program_id(1)
        i, j = ii_ref[st], jj_ref[st]

        @pl.when(st == 0)
        def _():
            dqn_ref[...] = jnp.zeros_like(dqn_ref)
            dqp_ref[...] = jnp.zeros_like(dqp_ref)

        @pl.when(i == j)
        def _():
            dk_sc[...] = jnp.zeros_like(dk_sc)
            dv_sc[...] = jnp.zeros_like(dv_sc)

        def update(diag):
            for row0, nr, nkeys in (_diag_spans(t) if diag else ((0, t, t),)):
                rows = slice(row0, row0 + nr)
                q = jnp.concatenate([qn_ref[rows, :], qp_ref[rows, :]], axis=1)
                k = jnp.concatenate([kn_ref[:nkeys, :], kp_ref[:nkeys, :]], axis=1)
                dob = do_ref[rows, :]
                s = _dot(q, k, 1, 1)
                if diag:
                    s = jnp.where(_diag_mask(nr, nkeys, row0), s, NEG)
                p = jnp.exp2(s - lse_ref[rows, 0:1])
                dv_sc[:nkeys, :] += _dot(p.astype(BF16), dob, 0, 0)
                dp = _dot(dob, v_ref[:nkeys, :], 1, 1)
                ds = (p * (dp - dl_ref[rows, 0:1])).astype(BF16)
                dk_sc[:nkeys, :] += _dot(ds, q, 0, 0)
                dq = _dot(ds, k, 1, 0) * MLA_SCALE
                grows = pl.ds(pl.multiple_of(i * t + row0, nr), nr)
                dqn_ref[grows, :] += dq[:, :128]
                dqp_ref[grows, :] += dq[:, 128:]

        @pl.when(i > j)
        def _():
            update(False)

        @pl.when(i == j)
        def _():
            update(True)

        @pl.when(i == nb - 1)
        def _():
            dkn_ref[...] = (dk_sc[:, :128] * (1.0 / LOG2E)).astype(BF16)
            dkp_ref[...] = dk_sc[:, 128:] * (1.0 / LOG2E)
            dv_ref[...] = dv_sc[...].astype(BF16)

    qs = pl.BlockSpec((t, 128), lambda h, s, ii, jj: (ii[s], h))
    ks = pl.BlockSpec((t, 128), lambda h, s, ii, jj: (jj[s], h))
    hs = pl.BlockSpec((T, 128), lambda h, s, ii, jj: (0, h))
    grid_spec = pltpu.PrefetchScalarGridSpec(
        num_scalar_prefetch=2, grid=(HEADS, int(ii.shape[0])),
        in_specs=[qs, qs, ks, pl.BlockSpec((t, 128), lambda h, s, ii, jj: (jj[s], 0)),
                  pl.BlockSpec((t, 128), lambda h, s, ii, jj: (jj[s], HEADS + h)), qs, qs, qs],
        out_specs=[hs, hs, ks, ks, ks],
        scratch_shapes=[pltpu.VMEM((t, 256), F32), pltpu.VMEM((t, 128), F32)])
    return _pcall(
        body, grid_spec=grid_spec,
        out_shape=[jax.ShapeDtypeStruct((T, D), F32), jax.ShapeDtypeStruct((T, D), F32), jax.ShapeDtypeStruct((T, D), BF16),
                   jax.ShapeDtypeStruct((T, D), F32), jax.ShapeDtypeStruct((T, D), BF16)], name="attn_bwd",
        compiler_params=pltpu.CompilerParams(dimension_semantics=("arbitrary", "arbitrary")),
    )(ii, jj, qn, qpe, kv, kpe, kv, do, lse, delta)


def _mesh_pos():
    x, y, c = lax.axis_index("x"), lax.axis_index("y"), lax.axis_index("c")
    return x, y, c, 4 * x + 2 * y + c


def _peer(x, y, c, k):
    px, py, pc = (x + ((k >> 2) & 1)) % 2, (y + ((k >> 1) & 1)) % 2, (c + (k & 1)) % 2
    return (px, py, pc), 4 * px + 2 * py + pc


_ANY = pl.BlockSpec(memory_space=pl.ANY)


def _rcopy(src, dst, send_sems, recv_sems, k, to):
    return pltpu.make_async_remote_copy(src_ref=src, dst_ref=dst, send_sem=send_sems.at[k], recv_sem=recv_sems.at[k],
                                        device_id=to, device_id_type=pl.DeviceIdType.MESH)


def _all_gather(wsh, ssh):
    def body(w_ref, s_ref, wall_ref, sall_ref, send_sems, recv_sems, loc_sems):
        x, y, c, me = _mesh_pos()
        sib = (x, y, 1 - c)
        chips = [(1 - x, y), (x, 1 - y), (1 - x, 1 - y)]
        slot = lambda px, py, pc: 4 * px + 2 * py + pc
        loc = [pltpu.make_async_copy(s_ref, sall_ref.at[me], loc_sems.at[0])]
        for cp in loc:
            cp.start()
        sends, fwd_waits = [], []
        for n, (src, dst) in enumerate(((w_ref, wall_ref), (s_ref, sall_ref))):
            o = 7 * n
            sends.append(_rcopy(src, dst.at[me], send_sems, recv_sems, o, sib))
            for j, chip in enumerate(chips):
                sends.append(_rcopy(src, dst.at[me], send_sems, recv_sems, o + 1 + j, (*chip, c)))
        for cp in sends:
            cp.start()
        for n, (src, dst) in enumerate(((w_ref, wall_ref), (s_ref, sall_ref))):
            o = 7 * n
            for j, chip in enumerate(chips):
                got = dst.at[slot(*chip, c)]
                _rcopy(src, got, send_sems, recv_sems, o + 1 + j, sib).wait_recv()
                fw = _rcopy(got, got, send_sems, recv_sems, o + 4 + j, sib)
                fw.start()
                sends.append(fw)
            fwd_waits.append(_rcopy(src, dst.at[slot(x, y, 1 - c)], send_sems, recv_sems, o, sib))
            for j, chip in enumerate(chips):
                fwd_waits.append(_rcopy(src, dst.at[slot(*chip, 1 - c)], send_sems, recv_sems, o + 4 + j, sib))
        for cp in fwd_waits:
            cp.wait_recv()
        for cp in sends:
            cp.wait_send()
        for cp in loc:
            cp.wait()

    wall, sall = _pcall(
        body, in_specs=[_ANY, _ANY], out_specs=[_ANY, _ANY],
        out_shape=[jax.ShapeDtypeStruct((N_DEV,) + wsh.shape, wsh.dtype), jax.ShapeDtypeStruct((N_DEV,) + ssh.shape, ssh.dtype)],
        scratch_shapes=[pltpu.SemaphoreType.DMA((14,)), pltpu.SemaphoreType.DMA((14,)), pltpu.SemaphoreType.DMA((1,))],
        name="all_gather_weights",
    )(wsh, ssh)
    me = 4 * lax.axis_index("x") + 2 * lax.axis_index("y") + lax.axis_index("c")
    return lax.dynamic_update_index_in_dim(wall, wsh, me, axis=0), sall


_HBM = pl.BlockSpec(memory_space=pltpu.HBM)
_SEM = pl.BlockSpec(memory_space=pltpu.SEMAPHORE)
_EFFECT = pltpu.SideEffectType.DATAFLOW_SIDE_EFFECTING


def _other_chips(x, y):
    return [(1 - x, y), (x, 1 - y), (1 - x, 1 - y)]


def _gather_start(wsh, order_after):
    def body(w_ref, land_ref, dep_ref, send_sems, recv_sems, w_thru, land_thru, token):
        x, y, c, me = _mesh_pos()
        _rcopy(w_ref, land_ref.at[me], send_sems, recv_sems, 0, (x, y, 1 - c)).start()
        for j, chip in enumerate(_other_chips(x, y)):
            _rcopy(w_ref, land_ref.at[me], send_sems, recv_sems, 1 + j, (*chip, c)).start()
        token[...] = jnp.zeros_like(token)

    shape = (N_DEV,) + wsh.shape
    land = pltpu.with_memory_space_constraint(lax.empty(shape, wsh.dtype), pltpu.HBM)
    return _pcall(
        body, name="gather_start",
        out_shape=(pltpu.SemaphoreType.DMA((4,)), pltpu.SemaphoreType.DMA((4,)), pltpu.HBM(wsh.shape, wsh.dtype),
                   pltpu.HBM(shape, wsh.dtype), jax.ShapeDtypeStruct((8, 128), F32)),
        in_specs=(_HBM, _HBM, _ANY), out_specs=(_SEM, _SEM, _HBM, _HBM, pl.BlockSpec(memory_space=pltpu.VMEM)),
        input_output_aliases={0: 2, 1: 3}, compiler_params=pltpu.CompilerParams(has_side_effects=_EFFECT),
    )(pltpu.with_memory_space_constraint(wsh, pltpu.HBM), land, order_after)


def _gather_wait(send_sems, recv_sems, w_thru, land_thru, after):
    def body(w_ref, land_ref, send_sems, recv_sems, after_ref, w_out, land_out):
        x, y, c, _ = _mesh_pos()
        senders = [(x, y, 1 - c)] + [(*chip, c) for chip in _other_chips(x, y)]
        for k, (px, py, pc) in enumerate(senders):
            cp = _rcopy(w_ref, land_ref.at[4 * px + 2 * py + pc], send_sems, recv_sems, k, (px, py, pc))
            cp.wait_send()
            cp.wait_recv()

    return _pcall(
        body, name="gather_wait", out_shape=(pltpu.HBM(w_thru.shape, w_thru.dtype), pltpu.HBM(land_thru.shape, land_thru.dtype)),
        in_specs=(_HBM, _HBM, _SEM, _SEM, _ANY), out_specs=(_HBM, _HBM), input_output_aliases={0: 0, 1: 1},
        compiler_params=pltpu.CompilerParams(has_side_effects=_EFFECT),
    )(w_thru, land_thru, send_sems, recv_sems, after)


def _gather_finish(wsh, land):
    def body(land_ref, out_ref, send_sems, recv_sems):
        x, y, c, _ = _mesh_pos()
        sib = (x, y, 1 - c)
        sends, recvs = [], []
        for j, (px, py) in enumerate(_other_chips(x, y)):
            mine, theirs = 4 * px + 2 * py + c, 4 * px + 2 * py + (1 - c)
            sends.append(_rcopy(land_ref.at[mine], out_ref.at[mine], send_sems, recv_sems, j, sib))
            recvs.append(_rcopy(land_ref.at[theirs], out_ref.at[theirs], send_sems, recv_sems, j, sib))
        for cp in sends:
            cp.start()
        for cp in recvs:
            cp.wait_recv()
        for cp in sends:
            cp.wait_send()

    done = _pcall(
        body, in_specs=[_ANY], out_specs=_ANY, out_shape=jax.ShapeDtypeStruct(land.shape, land.dtype),
        input_output_aliases={0: 0},
        scratch_shapes=[pltpu.SemaphoreType.DMA((3,)), pltpu.SemaphoreType.DMA((3,))], name="gather_finish",
    )(land)
    me = 4 * lax.axis_index("x") + 2 * lax.axis_index("y") + lax.axis_index("c")
    return lax.dynamic_update_index_in_dim(done, wsh, me, axis=0)


_ROW_TILES = (736, 528, 400, 368, 352, 256, 128, 16, 8)


def _exchange_sibling(g4, name):
    def body(g_ref, r_ref, send_sems, recv_sems):
        x, y, c, _ = _mesh_pos()
        sib = (x, y, 1 - c)
        cps = [_rcopy(g_ref.at[q, 1 - c], r_ref.at[q], send_sems, recv_sems, q, sib) for q in range(4)]
        for cp in cps:
            cp.start()
        for cp in cps:
            cp.wait()

    n, _, R, C = g4.shape
    return _pcall(
        body, in_specs=[_ANY], out_specs=_ANY, out_shape=jax.ShapeDtypeStruct((n, R, C), g4.dtype),
        scratch_shapes=[pltpu.SemaphoreType.DMA((4,)), pltpu.SemaphoreType.DMA((4,))], name=name,
    )(g4)


def _sum_sibling(g4, rsib, cvec, name):
    n, _, R, C = g4.shape
    tr = _pick(R, _ROW_TILES)

    def body(c_ref, g_ref, r_ref, o_ref):
        o_ref[...] = (g_ref[...].astype(F32) + r_ref[...].astype(F32)).astype(o_ref.dtype)

    grid_spec = pltpu.PrefetchScalarGridSpec(
        num_scalar_prefetch=1, grid=(n, R // tr),
        in_specs=[pl.BlockSpec((None, None, tr, C), lambda q, i, cr: (q, cr[0], i, 0)), pl.BlockSpec((None, tr, C), lambda q, i, cr: (q, i, 0))],
        out_specs=pl.BlockSpec((None, tr, C), lambda q, i, cr: (q, i, 0)))
    return _pcall(body, grid_spec=grid_spec, out_shape=jax.ShapeDtypeStruct((n, R, C), g4.dtype), name=name)(cvec, g4, rsib)


def _chips_start(part, name, order_after=None):
    n_in = 2 + int(order_after is not None)

    def body(*refs):
        p_ref, land_ref = refs[:2]
        send_sems, recv_sems, token = refs[n_in], refs[n_in + 1], refs[n_in + 4]
        x, y, c, _ = _mesh_pos()
        myq = 2 * x + y
        for j, (px, py) in enumerate(_other_chips(x, y)):
            _rcopy(p_ref.at[2 * px + py], land_ref.at[myq], send_sems, recv_sems, j, (px, py, c)).start()
        token[...] = jnp.zeros_like(token)

    land = pltpu.with_memory_space_constraint(lax.empty(part.shape, part.dtype), pltpu.HBM)
    extra = () if order_after is None else (order_after,)
    return _pcall(
        body, name=name,
        out_shape=(pltpu.SemaphoreType.DMA((3,)), pltpu.SemaphoreType.DMA((3,)), pltpu.HBM(part.shape, part.dtype),
                   pltpu.HBM(part.shape, part.dtype), jax.ShapeDtypeStruct((8, 128), F32)),
        in_specs=(_HBM, _HBM) + (_ANY,) * len(extra),
        out_specs=(_SEM, _SEM, _HBM, _HBM, pl.BlockSpec(memory_space=pltpu.VMEM)),
        input_output_aliases={0: 2, 1: 3}, compiler_params=pltpu.CompilerParams(has_side_effects=_EFFECT),
    )(pltpu.with_memory_space_constraint(part, pltpu.HBM), land, *extra)


def _chips_wait(send_sems, recv_sems, p_thru, land_thru, after, name):
    def body(p_ref, land_ref, send_sems, recv_sems, after_ref, p_out, land_out):
        x, y, c, _ = _mesh_pos()
        for j, (px, py) in enumerate(_other_chips(x, y)):
            q = 2 * px + py
            cp = _rcopy(p_ref.at[q], land_ref.at[q], send_sems, recv_sems, j, (px, py, c))
            cp.wait_send()
            cp.wait_recv()

    return _pcall(
        body, name=name, out_shape=(pltpu.HBM(p_thru.shape, p_thru.dtype), pltpu.HBM(p_thru.shape, p_thru.dtype)),
        in_specs=(_HBM, _HBM, _SEM, _SEM, _ANY), out_specs=(_HBM, _HBM), input_output_aliases={0: 0, 1: 1},
        compiler_params=pltpu.CompilerParams(has_side_effects=_EFFECT),
    )(p_thru, land_thru, send_sems, recv_sems, after)


def _sum_chips(part, land, qvec, name):
    n, R, C = part.shape
    tr = _pick(R, _ROW_TILES)

    def body(q_ref, p_ref, *refs):
        o_ref = refs[n]
        acc = None
        for q in range(n):
            term = jnp.where(q_ref[0] == q, p_ref[...], refs[q][...]).astype(F32)
            acc = term if acc is None else acc + term
        o_ref[...] = acc

    def land_spec(q):
        return pl.BlockSpec((None, tr, C), lambda i, m: (jnp.where(m[0] == q, (q + 1) % n, q), i, 0))

    grid_spec = pltpu.PrefetchScalarGridSpec(
        num_scalar_prefetch=1, grid=(R // tr,),
        in_specs=[pl.BlockSpec((None, tr, C), lambda i, m: (m[0], i, 0))] + [land_spec(q) for q in range(n)],
        out_specs=pl.BlockSpec((tr, C), lambda i, m: (i, 0)))
    return _pcall(body, grid_spec=grid_spec, out_shape=jax.ShapeDtypeStruct((R, C), F32), name=name)(qvec, part, *([land] * n))


def _exchange_small(gsmall):
    def body(s_ref, srecv_ref, send_sems, recv_sems, loc_sem):
        x, y, c, me = _mesh_pos()
        loc = pltpu.make_async_copy(s_ref, srecv_ref.at[me], loc_sem.at[0])
        loc.start()
        sends, recvs = [], []
        for k in range(1, N_DEV):
            to, pidx = _peer(x, y, c, k)
            sends.append(_rcopy(s_ref, srecv_ref.at[me], send_sems, recv_sems, k - 1, to))
            recvs.append(_rcopy(s_ref, srecv_ref.at[pidx], send_sems, recv_sems, k - 1, to))
        for cp in sends:
            cp.start()
        for cp in recvs:
            cp.wait_recv()
        for cp in sends:
            cp.wait_send()
        loc.wait()

    return _pcall(
        body, in_specs=[_ANY], out_specs=_ANY, out_shape=jax.ShapeDtypeStruct((N_DEV,) + gsmall.shape, gsmall.dtype),
        scratch_shapes=[pltpu.SemaphoreType.DMA((7,)), pltpu.SemaphoreType.DMA((7,)), pltpu.SemaphoreType.DMA((1,))],
        name="exchange_small",
    )(gsmall)


def _sum_slots(recv, name):
    n, R, C = recv.shape
    tr = _pick(R, _ROW_TILES)

    def body(r_ref, o_ref):
        acc = r_ref[0].astype(F32)
        for s in range(1, n):
            acc = acc + r_ref[s].astype(F32)
        o_ref[...] = acc

    return _pcall(body, grid=(R // tr,), in_specs=[pl.BlockSpec((n, tr, C), lambda i: (0, i, 0))],
                  out_specs=pl.BlockSpec((tr, C), lambda i: (i, 0)), out_shape=jax.ShapeDtypeStruct((R, C), F32), name=name)(recv)


def _adamw(w, g, m, v, name, order_after=None):
    shape = w.shape
    w2, g2, m2, v2 = (a.reshape(-1, shape[-1]) for a in (w, g, m, v))
    R, C = w2.shape
    tr = _pick(R, (256, 128, 64, 32, 16, 8)) if R > 256 else R
    n_in = 4 + int(order_after is not None)

    def body(*refs):
        w_ref, g_ref, m_ref, v_ref = refs[:4]
        d_ref, nm_ref, nv_ref = refs[n_in:]
        gg = g_ref[...]
        nm = ADAM_B1 * m_ref[...] + (1.0 - ADAM_B1) * gg
        nv = ADAM_B2 * v_ref[...] + (1.0 - ADAM_B2) * jnp.square(gg)
        m_hat = nm / (1.0 - ADAM_B1 ** ADAM_STEP)
        v_hat = nv / (1.0 - ADAM_B2 ** ADAM_STEP)
        d_ref[...] = -ADAM_LR * (m_hat / (jnp.sqrt(v_hat) + ADAM_EPS) + ADAM_WD * w_ref[...])
        nm_ref[...] = nm
        nv_ref[...] = nv

    spec = pl.BlockSpec((tr, C), lambda i: (i, 0))
    in_specs, args = [spec] * 4, [w2, g2, m2, v2]
    if order_after is not None:
        in_specs.append(pl.BlockSpec(memory_space=pl.ANY))
        args.append(order_after)
    d, nm, nv = _pcall(body, grid=(R // tr,), in_specs=in_specs, out_specs=[spec] * 3,
                       out_shape=[jax.ShapeDtypeStruct((R, C), F32)] * 3, name=name)(*args)
    return d.reshape(shape), nm.reshape(shape), nv.reshape(shape)


def _to_rows(name, w):
    w = w[0]
    if name in ("ffn1_w_in", "ffn2_w_in", "w_in"):
        return w.T
    if name in ("w_uq", "w_ukv", "ple_w_proj"):
        return w.T.reshape(-1, 1024)
    return w


def _from_rows(name, g, shape):
    if name in ("ffn1_w_in", "ffn2_w_in", "w_in"):
        return g.T.reshape(shape)
    if name in ("w_uq", "w_ukv", "ple_w_proj"):
        return g.reshape(-1, shape[1]).T.reshape(shape)
    return g.reshape(shape)


def _unpack(wall, group):
    out, off = {}, 0
    for name, r in group:
        out[name] = wall[:, off:off + r, :].reshape(N_DEV * r, 1024)
        off += _pad16(r)
    return out


def _w_in_internal(wt):
    return jnp.concatenate([wt[0:6144], wt[6720:8768], wt[6144:6720], jnp.zeros((P_W - W_IN_COLS, 1024), wt.dtype)], axis=0)


def _rope_tables(positions):
    pos = positions[0].astype(F32)

    def cs(half):
        inv = jnp.tile(ROPE_BASE ** (-jnp.arange(half, dtype=F32) / half), 128 // half)
        sign = jnp.tile(jnp.concatenate([-jnp.ones((half,), F32), jnp.ones((half,), F32)]), 64 // half)
        ang = pos[:, None] * inv
        return jnp.cos(ang), jnp.sin(ang) * sign

    return cs(64) + cs(32)


def _local_step(x, p, positions, target, W, ln_g, ln_b, gn_g, qg, kvg, early=None, rest_weights=None, start_token=None):
    T = x.shape[0]
    tabs = _rope_tables(positions)
    rc = _ret_consts()
    lg = [ln_g[i:i + 1] for i in range(4)]
    lb = [ln_b[i:i + 1] for i in range(4)]
    pb = p.astype(BF16)

    hs1, gu1, *xb = _ffn_in(x, W["ffn1_w_in"], "ffn1_in", order_after=start_token)
    xb = xb[0] if xb else x
    f1, h1, h1b = _mm(hs1, W["ffn1_w_out"], name="ffn1_out", tm=LN_TM, epilogue=_ln_epilogue(x, 0.5, lg[0], lb[0]))
    if rest_weights is not None:
        W = {**W, **rest_weights(h1b)}
    w_in_t = _w_in_internal(W["w_in"])
    wuq = W["w_uq"].reshape(1536, LORA).reshape(HEADS, 192, LORA)
    wuq = jnp.concatenate([wuq[:, :128].reshape(1024, LORA), wuq[:, 128:].reshape(512, LORA)], axis=0)
    wukv = W["w_ukv"].reshape(2048, LORA).reshape(HEADS, 2, 128, LORA).transpose(1, 0, 2, 3).reshape(2048, LORA)
    wp_t = W["ple_w_proj"].reshape(1024, D_PLE)
    proj = _mm(h1b, w_in_t, tb=True, out_dtype=BF16, name="mixer_in")
    rq, rk, qn, kvn, kpe = _mixer_prep_fwd(proj, tabs, qg, kvg, T)
    y, yr, states = _ret_fwd(rq, rk, proj, proj, gn_g, rc, T)
    y_ret = _mm(yr, W["w_ret_o"], out_dtype=BF16, name="ret_o")
    qnope, qpe = _mm(qn, wuq, tb=True, name="mla_uq", tm=LN_TM, tn=1536, epilogue=_q_assemble_epilogue(tabs))
    kv = _mm(kvn, wukv, tb=True, out_dtype=BF16, name="mla_ukv")
    o, lse = _attn_fwd(qnope, qpe, kv, kpe, T)
    y_mla, mix = _mm(o, W["w_mla_o"], name="mla_o", tm=LN_TM, epilogue=_mix_epilogue(proj, y_ret))
    mixed, h2, h2b = _mm(mix, W["w_out"], name="mixer_out", tm=LN_TM, epilogue=_ln_epilogue(h1, 1.0, lg[1], lb[1]))
    hs2, gu2 = _ffn_in(h2b, W["ffn2_w_in"], "ffn2_in")
    f2, h3, h3b = _mm(hs2, W["ffn2_w_out"], name="ffn2_out", tm=LN_TM, epilogue=_ln_epilogue(h2, 0.5, lg[2], lb[2]))
    pp = _mm(pb, wp_t, tb=True, name="ple_proj")

    G = {}
    dh3_a, dgl, dpp, dg3, db3, loss = _mm(h3b, W["ple_w_gate"], name="ple_gate", tm=LN_TM,
                                          epilogue=_head_epilogue(h3, pp, target, lg[3], lb[3]))
    G["ple_w_gate"] = _mm(h3b, dgl, ta=True, name="d_ple_gate")
    G["ple_w_proj"] = _mm(dpp, pb, ta=True, name="d_ple_proj")
    dh2_a, df2, dg2, db2 = _mm(dgl, W["ple_w_gate"], tb=True, add=dh3_a, name="dh3", tm=LN_TM,
                               epilogue=_ln_bwd_epilogue(h2, f2, 0.5, lg[2], lb[2]))
    wide = dict(tm=D_FF, tk=min(1024, T))
    G["ffn2_w_out"] = _mm(hs2, df2, ta=True, name="d_ffn2_out", **wide)
    da2 = _ffn_act_bwd(df2, W["ffn2_w_out"], gu2, "ffn2_act_bwd")
    G["ffn2_w_in"] = _mm(da2, h2b, ta=True, name="d_ffn2_in", **wide)
    dh1_a, dmixed, dg1, db1 = _mm(da2, W["ffn2_w_in"], add=dh2_a, name="dh2", tm=LN_TM,
                                  epilogue=_ln_bwd_epilogue(h1, mixed, 1.0, lg[1], lb[1]))
    G["w_out"] = _mm(mix, dmixed, ta=True, name="d_mixer_out")
    dp_gate, dy_ret, dy_mla = _mm(dmixed, W["w_out"], tb=True, name="dmix", tm=LN_TM,
                                  epilogue=_mix_bwd_epilogue(proj, y_ret, y_mla))
    G["w_mla_o"] = _mm(o, dy_mla, ta=True, name="d_mla_o")
    dob, delta = _mm(dy_mla, W["w_mla_o"], tb=True, name="do", tm=min(2 * LN_TM, T), epilogue=_delta_epilogue(o))
    dqn_f, dqpe_f, dkn, dkpe_all, dv = _attn_bwd(qnope, qpe, kv, kpe, dob, lse, delta, T)
    dq_n, dq_r = _q_assemble_bwd(dqn_f, dqpe_f, tabs, T)
    g_uq = jnp.concatenate([_mm(dq_n, qn, ta=True, name="d_uq_nope"), _mm(dq_r, qn, ta=True, name="d_uq_rope")], axis=0)
    g_uq = jnp.concatenate([g_uq[:1024].reshape(HEADS, 128, LORA), g_uq[1024:].reshape(HEADS, 64, LORA)], axis=1)
    G["w_uq"] = g_uq.reshape(1536 * LORA // 1024, 1024)
    dqn = _mm(dq_r, wuq[1024:], add=_mm(dq_n, wuq[:1024], name="dqn_a"), name="dqn_b")
    g_ukv = jnp.stack([_mm(dkn, kvn, ta=True, name="d_ukv_k"), _mm(dv, kvn, ta=True, name="d_ukv_v")], axis=0)
    G["w_ukv"] = g_ukv.reshape(2, HEADS, 128, LORA).transpose(1, 0, 2, 3).reshape(2048 * LORA // 1024, 1024)
    dkvn = _mm(dv, wukv[1024:], add=_mm(dkn, wukv[:1024], name="dkvn_a"), name="dkvn_b")
    dp_mla, dqg, dkvg = _rms_bwd(proj, dqn, dkvn, dkpe_all, tabs, qg, kvg, T)
    G["w_ret_o"] = _mm(yr, dy_ret, ta=True, name="d_ret_o")
    dyr = _mm(dy_ret, W["w_ret_o"], tb=True, out_dtype=BF16, name="dyr")
    dp_ret, dgn = _ret_bwd(rq, rk, proj, y, proj, gn_g, states, dyr, rc, tabs, T)
    g_ret, g_gate, g_mla = (_mm(dp, h1b, ta=True, name="d_mixer_in_" + n)
                            for dp, n in ((dp_ret, "ret"), (dp_gate, "gate"), (dp_mla, "mla")))
    G["w_in"] = jnp.concatenate([g_ret, g_mla[:W_IN_COLS - P_CQ], g_gate], axis=0)
    lg0 = lg[0] if early is None else lg[0] + early(G)[0:1, 0:1]
    dh1 = _mm_pieces([(dp_ret, w_in_t[:P_GR]), (dp_gate, w_in_t[P_GR:P_CQ]), (dp_mla, w_in_t[P_CQ:])], dh1_a, "dh1")
    dx_a, df1, dg0, db0 = _ln_bwd_rows(x, f1, 0.5, lg0, lb[0], dh1, T, "ln0_bwd")
    G["ffn1_w_out"] = _mm(hs1, df1, ta=True, name="d_ffn1_out", **wide)
    da1 = _ffn_act_bwd(df1, W["ffn1_w_out"], gu1, "ffn1_act_bwd")
    G["ffn1_w_in"] = _mm(da1, xb, ta=True, name="d_ffn1_in", **wide)
    grad_x = _mm(da1, W["ffn1_w_in"], add=dx_a, name="grad_x")

    small = dict(ln_g=jnp.concatenate([dg0, dg1, dg2, dg3], axis=0), ln_b=jnp.concatenate([db0, db1, db2, db3], axis=0),
                 ret_gn_g=dgn, q_norm_g=dqg, kv_norm_g=dkvg)
    return loss, grad_x, G, small


def kernel(x, p, positions, ln_g, ln_b, ffn1_w_in, ffn1_w_out, w_in, ret_gn_g, w_ret_o, q_norm_g, kv_norm_g, w_uq, w_ukv, w_mla_o, w_out, ffn2_w_in, ffn2_w_out, ple_w_gate, ple_w_proj, loss_target, m_ln_g, m_ln_b, m_ffn1_w_in, m_ffn1_w_out, m_w_in, m_ret_gn_g, m_w_ret_o, m_q_norm_g, m_kv_norm_g, m_w_uq, m_w_ukv, m_w_mla_o, m_w_out, m_ffn2_w_in, m_ffn2_w_out, m_ple_w_gate, m_ple_w_proj, v_ln_g, v_ln_b, v_ffn1_w_in, v_ffn1_w_out, v_w_in, v_ret_gn_g, v_w_ret_o, v_q_norm_g, v_kv_norm_g, v_w_uq, v_w_ukv, v_w_mla_o, v_w_out, v_ffn2_w_in, v_ffn2_w_out, v_ple_w_gate, v_ple_w_proj):
    names = ("ln_g", "ln_b", "ffn1_w_in", "ffn1_w_out", "w_in", "ret_gn_g", "w_ret_o", "q_norm_g", "kv_norm_g", "w_uq", "w_ukv",
             "w_mla_o", "w_out", "ffn2_w_in", "ffn2_w_out", "ple_w_gate", "ple_w_proj")
    ws = dict(zip(names, (ln_g, ln_b, ffn1_w_in, ffn1_w_out, w_in, ret_gn_g, w_ret_o, q_norm_g, kv_norm_g, w_uq, w_ukv, w_mla_o,
                          w_out, ffn2_w_in, ffn2_w_out, ple_w_gate, ple_w_proj)))
    ms = dict(zip(names, (m_ln_g, m_ln_b, m_ffn1_w_in, m_ffn1_w_out, m_w_in, m_ret_gn_g, m_w_ret_o, m_q_norm_g, m_kv_norm_g, m_w_uq,
                          m_w_ukv, m_w_mla_o, m_w_out, m_ffn2_w_in, m_ffn2_w_out, m_ple_w_gate, m_ple_w_proj)))
    vs = dict(zip(names, (v_ln_g, v_ln_b, v_ffn1_w_in, v_ffn1_w_out, v_w_in, v_ret_gn_g, v_w_ret_o, v_q_norm_g, v_kv_norm_g, v_w_uq,
                          v_w_ukv, v_w_mla_o, v_w_out, v_ffn2_w_in, v_ffn2_w_out, v_ple_w_gate, v_ple_w_proj)))

    parts = []
    for name, r in PACK:
        rows = _to_rows(name, ws[name])
        if _pad16(r) != r:
            rows = jnp.concatenate([rows, jnp.zeros((_pad16(r) - r, 1024), F32)], axis=0)
        parts.append(rows)
    wsh_first = jnp.concatenate(parts[:len(PACK_LATE)], axis=0).astype(BF16)
    wsh_rest = jnp.concatenate(parts[len(PACK_LATE):], axis=0).astype(BF16)
    ssh = jnp.concatenate([ln_g[0], ln_b[0]], axis=0)
    wall_first, sall = _all_gather(wsh_first, ssh)
    *gather_handles, start_token = _gather_start(wsh_rest, wall_first)
    W = _unpack(wall_first, PACK_LATE)
    ln_full = sall.reshape(N_DEV, 2, 4, 128).transpose(1, 2, 0, 3).reshape(2, 4, 1024)

    def rest_weights(after):
        w_thru, land = _gather_wait(*gather_handles, after)
        return _unpack(_gather_finish(w_thru, land), PACK_EARLY)

    cvec = lax.axis_index("c").astype(jnp.int32).reshape(1)
    qvec = (2 * lax.axis_index("x") + lax.axis_index("y")).astype(jnp.int32).reshape(1)

    def chip_partials(G, group, tag):
        gparts = []
        for name, r in group:
            g = G[name].reshape(N_DEV, r, 1024)
            if _pad16(r) != r:
                g = jnp.concatenate([g, jnp.zeros((N_DEV, _pad16(r) - r, 1024), g.dtype)], axis=1)
            gparts.append(g)
        gfull = jnp.concatenate(gparts, axis=1).astype(BF16)
        g4 = gfull.reshape(4, 2, gfull.shape[1], 1024)
        return _sum_sibling(g4, _exchange_sibling(g4, "exchange_sibling_" + tag), cvec, "sum_sibling_" + tag)

    in_flight = []

    def early(G):
        *handles, token = _chips_start(chip_partials(G, PACK_EARLY, "early"), "chips_start_early")
        in_flight.append(handles)
        return token

    loss_p, grad_x, G, small = _local_step(x[0], p[0, 0], positions, loss_target[0], W, ln_full[0], ln_full[1],
                                           ret_gn_g, q_norm_g, kv_norm_g, early=early, rest_weights=rest_weights,
                                           start_token=start_token)

    part_e, land_e = _chips_wait(*in_flight[0], grad_x, "chips_wait_early")
    gsh_early = _sum_chips(part_e, land_e, qvec, "sum_grads_early")
    pad256 = lambda a: jnp.concatenate([a, jnp.zeros((1, 1024 - a.shape[1]), F32)], axis=1)
    gsmall = jnp.concatenate([small["ln_g"], small["ln_b"], small["ret_gn_g"].reshape(2, 1024), pad256(small["q_norm_g"]),
                              pad256(small["kv_norm_g"]), pad256(loss_p[0:1, 0:128]),
                              jnp.zeros((SMALL_ROWS - 13, 1024), F32)], axis=0)
    srecv = _exchange_small(gsmall)
    part_l = chip_partials(G, PACK_LATE, "late")
    *late_handles, late_token = _chips_start(part_l, "chips_start_late", order_after=srecv)
    ssum = _sum_slots(srecv, "sum_small_grads")

    def unpack_grads(group, gsh):
        out, off = {}, 0
        for name, r in group:
            out[name] = _from_rows(name, gsh[off:off + r], ws[name].shape)
            off += _pad16(r)
        return out

    grads = unpack_grads(PACK_EARLY, gsh_early)
    me = 4 * lax.axis_index("x") + 2 * lax.axis_index("y") + lax.axis_index("c")
    grads["ln_g"] = lax.dynamic_slice(ssum[0:4], (0, me * 128), (4, 128)).reshape(1, 4, 128)
    grads["ln_b"] = lax.dynamic_slice(ssum[4:8], (0, me * 128), (4, 128)).reshape(1, 4, 128)
    grads["ret_gn_g"] = ssum[8:10].reshape(1, 2048)
    grads["q_norm_g"] = ssum[10:11, :256]
    grads["kv_norm_g"] = ssum[11:12, :256]

    delta, new_m, new_v = {}, {}, {}
    late_names = [n for n, _ in PACK_LATE]
    last = late_token
    for name in names:
        if name not in late_names:
            delta[name], new_m[name], new_v[name] = _adamw(ws[name], grads[name], ms[name], vs[name], "adamw_" + name,
                                                           order_after=last)
            last = new_v[name]
    part_l, land_l = _chips_wait(*late_handles, last, "chips_wait_late")
    grads.update(unpack_grads(PACK_LATE, _sum_chips(part_l, land_l, qvec, "sum_grads_late")))
    for name in late_names:
        delta[name], new_m[name], new_v[name] = _adamw(ws[name], grads[name], ms[name], vs[name], "adamw_" + name)

    return (ssum[12, 0], grad_x[None], *[grads[n] for n in names], *[delta[n] for n in names],
            *[new_m[n] for n in names], *[new_v[n] for n in names])
```
